```python
import jax, jax.numpy as jnp
from jax import lax
import numpy as np

D_MODEL = 1024
BATCH = 8
SEQ = 2048
DEPTH = 1

PLE_DIM = 256
HEAD_DIM = 64
RWKV_HEADS = 8
RWKV_DIM = RWKV_HEADS * HEAD_DIM
DECAY_LORA = 64
ICLR_LORA = 64
GATE_LORA = 128
GN_EPS = 64e-5
ATTN_GROUPS = ((128, 1), (512, 4), (2048, 16))
HEADS_PER_GROUP = 4
ATTN_HEADS = HEADS_PER_GROUP * len(ATTN_GROUPS)
ATTN_DIM = ATTN_HEADS * HEAD_DIM
ATTN_OUT_DIM = HEADS_PER_GROUP * HEAD_DIM
BAND_BLOCK = 128
ROPE_THETA = 10000.0
NEG_INF = -1e30
D_FF = 2816
RMS_EPS = 1e-6
N_BRANCHES = 2
RWKV_COLS = 3 * RWKV_DIM + DECAY_LORA + ICLR_LORA + GATE_LORA
ATTN_COLS = 3 * ATTN_DIM
GATE_COLS = N_BRANCHES * D_MODEL
IN_COLS = RWKV_COLS + ATTN_COLS + GATE_COLS

kernel_name = 'hybrid_rwkv7_dilated_attn_macaron_block'


def rms_norm(x, gain):
    xf = x.astype(jnp.float32)
    y = xf * lax.rsqrt(jnp.mean(xf * xf, axis=-1, keepdims=True) + RMS_EPS)
    return (y * gain.astype(jnp.float32)).astype(x.dtype)


def swiglu(h, w_gate, w_up, w_down):
    return (jax.nn.silu(h @ w_gate) * (h @ w_up)) @ w_down


def token_shift(z):
    return jnp.pad(z, ((0, 0), (1, 0), (0, 0)))[:, :-1]


def apply_rope(x, cos, sin):
    x1, x2 = jnp.split(x.astype(jnp.float32), 2, axis=-1)
    return jnp.concatenate([x1 * cos - x2 * sin, x2 * cos + x1 * sin], axis=-1).astype(x.dtype)


def wkv7_scan(r, decay, k, v, kk, a):
    B, S, H, N = r.shape

    def step(state, inp):
        r_t, w_t, k_t, v_t, kk_t, a_t = inp
        sa = jnp.einsum('bhvk,bhk->bhv', state, -kk_t)
        state = (state * w_t[:, :, None, :]
                 + sa[..., None] * (kk_t * a_t)[:, :, None, :]
                 + v_t[..., None] * k_t[:, :, None, :])
        y_t = jnp.einsum('bhvk,bhk->bhv', state, r_t)
        return state, y_t

    xs = tuple(jnp.moveaxis(t, 1, 0) for t in (r, decay, k, v, kk, a))
    s0 = jnp.zeros((B, H, N, N), jnp.float32)
    _, y = lax.scan(step, s0, xs)
    return jnp.moveaxis(y, 0, 1)


def rwkv7_time_mix(z, mu, w0, w2, a0, a2, g2, k_k, k_a, r_k, gn_w, gn_b):
    B, S, _ = z.shape
    z = z + (token_shift(z) - z) * mu
    r, k, v, wd, ad, gd = jnp.split(
        z, [RWKV_DIM, 2 * RWKV_DIM, 3 * RWKV_DIM, 3 * RWKV_DIM + DECAY_LORA,
            3 * RWKV_DIM + DECAY_LORA + ICLR_LORA], axis=-1)
    w = -jax.nn.softplus(-(w0 + jnp.tanh(wd) @ w2)) - 0.5
    a = jax.nn.sigmoid(a0 + ad @ a2)
    g = jax.nn.sigmoid(gd) @ g2
    kk = k * k_k
    k = k * (1.0 + (a - 1.0) * k_a)

    def heads(t):
        return t.reshape(B, S, RWKV_HEADS, HEAD_DIM).astype(jnp.float32)

    kk = heads(kk)
    kk = kk * lax.rsqrt(jnp.maximum(jnp.sum(kk * kk, axis=-1, keepdims=True), 1e-24))
    decay = jnp.exp(-jnp.exp(heads(w)))
    r_h, k_h, v_h, a_h = heads(r), heads(k), heads(v), heads(a)
    y = wkv7_scan(r_h, decay, k_h, v_h, kk, a_h)
    mean = jnp.mean(y, axis=-1, keepdims=True)
    var = jnp.mean(jnp.square(y - mean), axis=-1, keepdims=True)
    y = ((y - mean) * lax.rsqrt(var + GN_EPS)).reshape(B, S, RWKV_DIM)
    y = y * gn_w.astype(jnp.float32) + gn_b.astype(jnp.float32)
    bonus = jnp.sum(r_h * k_h * r_k.astype(jnp.float32), axis=-1, keepdims=True) * v_h
    y = y + bonus.reshape(B, S, RWKV_DIM)
    return (y * g.astype(jnp.float32)).astype(z.dtype)


def banded_causal_attention(q, k, v, band):
    N, L, H, Dh = q.shape
    nb = -(-L // BAND_BLOCK)
    Lp = nb * BAND_BLOCK
    pad = Lp - L
    qb = jnp.pad(q, ((0, 0), (0, pad), (0, 0), (0, 0))).reshape(N, nb, BAND_BLOCK, H, Dh)

    def key_blocks(t):
        tp = jnp.pad(t, ((0, 0), (BAND_BLOCK, pad), (0, 0), (0, 0))).reshape(N, nb + 1, BAND_BLOCK, H, Dh)
        return jnp.concatenate([tp[:, :-1], tp[:, 1:]], axis=2)

    kb, vb = key_blocks(k), key_blocks(v)
    s = jnp.einsum('nbqhd,nbkhd->nbhqk', qb.astype(jnp.float32), kb.astype(jnp.float32)) * (Dh ** -0.5)
    blk = jnp.arange(nb)[:, None]
    qpos = blk * BAND_BLOCK + jnp.arange(BAND_BLOCK)[None, :]
    kpos = blk * BAND_BLOCK - BAND_BLOCK + jnp.arange(2 * BAND_BLOCK)[None, :]
    dist = qpos[:, :, None] - kpos[:, None, :]
    valid = (dist >= 0) & (dist <= band) & (kpos[:, None, :] >= 0)
    s = jnp.where(valid[None, :, None], s, NEG_INF)
    m = jnp.max(s, axis=-1, keepdims=True)
    e = jnp.exp(s - m)
    l = jnp.sum(e, axis=-1, keepdims=True)
    o = jnp.einsum('nbhqk,nbkhd->nbqhd', e, vb.astype(jnp.float32)) / jnp.swapaxes(l, 2, 3)
    lse = jnp.swapaxes((m + jnp.log(l))[..., 0], 2, 3)
    return o.reshape(N, Lp, H, Dh)[:, :L], lse.reshape(N, Lp, H)[:, :L]


def dilated_causal_attention(q, k, v, window, dilation):
    B, S, H, Dh = q.shape
    L = S // dilation

    def fold(t):
        return t.reshape(B, L, dilation, H, Dh).transpose(0, 2, 1, 3, 4).reshape(B * dilation, L, H, Dh)

    o, lse = banded_causal_attention(fold(q), fold(k), fold(v), window // dilation)
    o = o.reshape(B, dilation, L, H, Dh).transpose(0, 2, 1, 3, 4).reshape(B, S, H, Dh)
    lse = lse.reshape(B, dilation, L, H).transpose(0, 2, 1, 3).reshape(B, S, H)
    return o, lse


def dilated_attention_mix(z, cos, sin, q_gain, k_gain):
    B, S, _ = z.shape
    q, k, v = jnp.split(z, 3, axis=-1)
    q = q.reshape(B, S, ATTN_HEADS, HEAD_DIM)
    k = k.reshape(B, S, ATTN_HEADS, HEAD_DIM)
    v = v.reshape(B, S, ATTN_HEADS, HEAD_DIM)
    q = apply_rope(rms_norm(q, q_gain), cos, sin)
    k = apply_rope(rms_norm(k, k_gain), cos, sin)
    outs, lses = [], []
    for gi, (window, dilation) in enumerate(ATTN_GROUPS):
        hs = slice(gi * HEADS_PER_GROUP, (gi + 1) * HEADS_PER_GROUP)
        o, lse = dilated_causal_attention(q[:, :, hs], k[:, :, hs], v[:, :, hs], window, dilation)
        outs.append(o)
        lses.append(lse)
    wts = jax.nn.softmax(jnp.stack(lses, axis=0), axis=0)
    o = jnp.sum(wts[..., None] * jnp.stack(outs, axis=0), axis=0)
    return o.reshape(B, S, ATTN_OUT_DIM).astype(z.dtype)


def hybrid_layer(x, p_i, cos, sin, ffn1_norm, ffn1_w_gate, ffn1_w_up, ffn1_w_down, mix_norm, w_in,
                 rwkv_mu, rwkv_w0, rwkv_w2, rwkv_a0, rwkv_a2, rwkv_g2, rwkv_k_k, rwkv_k_a, rwkv_r_k,
                 rwkv_gn_w, rwkv_gn_b, q_norm, k_norm, w_br_rwkv, w_br_attn, w_out,
                 ffn2_norm, ffn2_w_gate, ffn2_w_up, ffn2_w_down, ple_norm, ple_w_gate, ple_w_proj):
    x = x + 0.5 * swiglu(rms_norm(x, ffn1_norm), ffn1_w_gate, ffn1_w_up, ffn1_w_down)
    h = rms_norm(x, mix_norm)
    z = h @ w_in
    z_rwkv, z_attn, z_gate = jnp.split(z, [RWKV_COLS, RWKV_COLS + ATTN_COLS], axis=-1)
    y_rwkv = rwkv7_time_mix(z_rwkv, rwkv_mu, rwkv_w0, rwkv_w2, rwkv_a0, rwkv_a2, rwkv_g2,
                            rwkv_k_k, rwkv_k_a, rwkv_r_k, rwkv_gn_w, rwkv_gn_b)
    y_attn = dilated_attention_mix(z_attn, cos, sin, q_norm, k_norm)
    g_rwkv, g_attn = jnp.split(jax.nn.sigmoid(z_gate), N_BRANCHES, axis=-1)
    merged = g_rwkv * (y_rwkv @ w_br_rwkv) + g_attn * (y_attn @ w_br_attn)
    x = x + merged @ w_out
    x = x + 0.5 * swiglu(rms_norm(x, ffn2_norm), ffn2_w_gate, ffn2_w_up, ffn2_w_down)
    x = x + jax.nn.sigmoid(rms_norm(x, ple_norm) @ ple_w_gate) * (p_i @ ple_w_proj)
    return x


def _fwd_setup_inputs(seed: int = 0) -> dict:
    key = jax.random.key(seed)
    ks = jax.random.split(key, 40)
    f32 = jnp.float32

    def nrm(i, shape, scale):
        return jax.random.normal(ks[i], shape, f32) * scale

    def gain(i, shape):
        return 1.0 + 0.02 * jax.random.normal(ks[i], shape, f32)

    L = DEPTH
    return {
        'x': nrm(0, (BATCH, SEQ, D_MODEL), 1.0),
        'p': nrm(1, (DEPTH, BATCH, SEQ, PLE_DIM), 1.0),
        'positions': (jnp.arange(SEQ, dtype=jnp.int32)[None, :]
                      + jax.random.randint(ks[2], (BATCH, 1), 0, 4096, dtype=jnp.int32)),
        'ffn1_norm': gain(3, (L, D_MODEL)),
        'ffn1_w_gate': nrm(4, (L, D_MODEL, D_FF), D_MODEL ** -0.5),
        'ffn1_w_up': nrm(5, (L, D_MODEL, D_FF), D_MODEL ** -0.5),
        'ffn1_w_down': nrm(6, (L, D_FF, D_MODEL), D_FF ** -0.5),
        'mix_norm': gain(7, (L, D_MODEL)),
        'w_in': nrm(8, (L, D_MODEL, IN_COLS), D_MODEL ** -0.5),
        'rwkv_mu': jax.random.uniform(ks[9], (L, RWKV_COLS), f32),
        'rwkv_w0': -0.5 - 4.5 * jax.random.uniform(ks[10], (L, RWKV_DIM), f32),
        'rwkv_w2': nrm(11, (L, DECAY_LORA, RWKV_DIM), 0.1),
        'rwkv_a0': nrm(12, (L, RWKV_DIM), 0.1),
        'rwkv_a2': nrm(13, (L, ICLR_LORA, RWKV_DIM), 0.1),
        'rwkv_g2': nrm(14, (L, GATE_LORA, RWKV_DIM), GATE_LORA ** -0.5),
        'rwkv_k_k': 0.85 + nrm(15, (L, RWKV_DIM), 0.02),
        'rwkv_k_a': 1.0 + nrm(16, (L, RWKV_DIM), 0.02),
        'rwkv_r_k': nrm(17, (L, RWKV_HEADS, HEAD_DIM), 0.1),
        'rwkv_gn_w': gain(18, (L, RWKV_DIM)),
        'rwkv_gn_b': nrm(19, (L, RWKV_DIM), 0.01),
        'q_norm': gain(20, (L, HEAD_DIM)),
        'k_norm': gain(21, (L, HEAD_DIM)),
        'w_br_rwkv': nrm(22, (L, RWKV_DIM, D_MODEL), RWKV_DIM ** -0.5),
        'w_br_attn': nrm(23, (L, ATTN_OUT_DIM, D_MODEL), ATTN_OUT_DIM ** -0.5),
        'w_out': nrm(24, (L, D_MODEL, D_MODEL), D_MODEL ** -0.5),
        'ffn2_norm': gain(25, (L, D_MODEL)),
        'ffn2_w_gate': nrm(26, (L, D_MODEL, D_FF), D_MODEL ** -0.5),
        'ffn2_w_up': nrm(27, (L, D_MODEL, D_FF), D_MODEL ** -0.5),
        'ffn2_w_down': nrm(28, (L, D_FF, D_MODEL), D_FF ** -0.5),
        'ple_norm': gain(29, (L, D_MODEL)),
        'ple_w_gate': nrm(30, (L, D_MODEL, D_MODEL), D_MODEL ** -0.5),
        'ple_w_proj': nrm(31, (L, PLE_DIM, D_MODEL), PLE_DIM ** -0.5),
    }


def _fwd_reference(x, p, positions, ffn1_norm, ffn1_w_gate, ffn1_w_up, ffn1_w_down, mix_norm, w_in,
              rwkv_mu, rwkv_w0, rwkv_w2, rwkv_a0, rwkv_a2, rwkv_g2, rwkv_k_k, rwkv_k_a, rwkv_r_k,
              rwkv_gn_w, rwkv_gn_b, q_norm, k_norm, w_br_rwkv, w_br_attn, w_out,
              ffn2_norm, ffn2_w_gate, ffn2_w_up, ffn2_w_down, ple_norm, ple_w_gate, ple_w_proj):
    inv_freq = 1.0 / (ROPE_THETA ** (jnp.arange(0, HEAD_DIM, 2, dtype=jnp.float32) / HEAD_DIM))
    ang = positions.astype(jnp.float32)[..., None] * inv_freq
    cos = jnp.cos(ang)[:, :, None, :]
    sin = jnp.sin(ang)[:, :, None, :]
    for i in range(DEPTH):
        x = hybrid_layer(x, p[i], cos, sin, ffn1_norm[i], ffn1_w_gate[i], ffn1_w_up[i], ffn1_w_down[i],
                         mix_norm[i], w_in[i], rwkv_mu[i], rwkv_w0[i], rwkv_w2[i], rwkv_a0[i], rwkv_a2[i],
                         rwkv_g2[i], rwkv_k_k[i], rwkv_k_a[i], rwkv_r_k[i], rwkv_gn_w[i], rwkv_gn_b[i],
                         q_norm[i], k_norm[i], w_br_rwkv[i], w_br_attn[i], w_out[i],
                         ffn2_norm[i], ffn2_w_gate[i], ffn2_w_up[i], ffn2_w_down[i],
                         ple_norm[i], ple_w_gate[i], ple_w_proj[i])
    return x


import jax as _jax
import jax.numpy as _jnp

TWIN_FORMAT = 'train_step'
FWD_PARAMS = ['x', 'p', 'positions', 'ffn1_norm', 'ffn1_w_gate', 'ffn1_w_up', 'ffn1_w_down', 'mix_norm', 'w_in', 'rwkv_mu', 'rwkv_w0', 'rwkv_w2', 'rwkv_a0', 'rwkv_a2', 'rwkv_g2', 'rwkv_k_k', 'rwkv_k_a', 'rwkv_r_k', 'rwkv_gn_w', 'rwkv_gn_b', 'q_norm', 'k_norm', 'w_br_rwkv', 'w_br_attn', 'w_out', 'ffn2_norm', 'ffn2_w_gate', 'ffn2_w_up', 'ffn2_w_down', 'ple_norm', 'ple_w_gate', 'ple_w_proj']
TWIN_WEIGHTS = ['ffn1_norm', 'ffn1_w_gate', 'ffn1_w_up', 'ffn1_w_down', 'mix_norm', 'w_in', 'rwkv_mu', 'rwkv_w0', 'rwkv_w2', 'rwkv_a0', 'rwkv_a2', 'rwkv_g2', 'rwkv_k_k', 'rwkv_k_a', 'rwkv_r_k', 'rwkv_gn_w', 'rwkv_gn_b', 'q_norm', 'k_norm', 'w_br_rwkv', 'w_br_attn', 'w_out', 'ffn2_norm', 'ffn2_w_gate', 'ffn2_w_up', 'ffn2_w_down', 'ple_norm', 'ple_w_gate', 'ple_w_proj']
TWIN_DIFF_INPUT = 'x'
TWIN_INPUTS = ['x', 'p', 'positions', 'ffn1_norm', 'ffn1_w_gate', 'ffn1_w_up', 'ffn1_w_down', 'mix_norm', 'w_in', 'rwkv_mu', 'rwkv_w0', 'rwkv_w2', 'rwkv_a0', 'rwkv_a2', 'rwkv_g2', 'rwkv_k_k', 'rwkv_k_a', 'rwkv_r_k', 'rwkv_gn_w', 'rwkv_gn_b', 'q_norm', 'k_norm', 'w_br_rwkv', 'w_br_attn', 'w_out', 'ffn2_norm', 'ffn2_w_gate', 'ffn2_w_up', 'ffn2_w_down', 'ple_norm', 'ple_w_gate', 'ple_w_proj', 'loss_target', 'm_ffn1_norm', 'm_ffn1_w_gate', 'm_ffn1_w_up', 'm_ffn1_w_down', 'm_mix_norm', 'm_w_in', 'm_rwkv_mu', 'm_rwkv_w0', 'm_rwkv_w2', 'm_rwkv_a0', 'm_rwkv_a2', 'm_rwkv_g2', 'm_rwkv_k_k', 'm_rwkv_k_a', 'm_rwkv_r_k', 'm_rwkv_gn_w', 'm_rwkv_gn_b', 'm_q_norm', 'm_k_norm', 'm_w_br_rwkv', 'm_w_br_attn', 'm_w_out', 'm_ffn2_norm', 'm_ffn2_w_gate', 'm_ffn2_w_up', 'm_ffn2_w_down', 'm_ple_norm', 'm_ple_w_gate', 'm_ple_w_proj', 'v_ffn1_norm', 'v_ffn1_w_gate', 'v_ffn1_w_up', 'v_ffn1_w_down', 'v_mix_norm', 'v_w_in', 'v_rwkv_mu', 'v_rwkv_w0', 'v_rwkv_w2', 'v_rwkv_a0', 'v_rwkv_a2', 'v_rwkv_g2', 'v_rwkv_k_k', 'v_rwkv_k_a', 'v_rwkv_r_k', 'v_rwkv_gn_w', 'v_rwkv_gn_b', 'v_q_norm', 'v_k_norm', 'v_w_br_rwkv', 'v_w_br_attn', 'v_w_out', 'v_ffn2_norm', 'v_ffn2_w_gate', 'v_ffn2_w_up', 'v_ffn2_w_down', 'v_ple_norm', 'v_ple_w_gate', 'v_ple_w_proj']
TWIN_OUTPUTS = ['loss', 'grad_x', 'grad_ffn1_norm', 'grad_ffn1_w_gate', 'grad_ffn1_w_up', 'grad_ffn1_w_down', 'grad_mix_norm', 'grad_w_in', 'grad_rwkv_mu', 'grad_rwkv_w0', 'grad_rwkv_w2', 'grad_rwkv_a0', 'grad_rwkv_a2', 'grad_rwkv_g2', 'grad_rwkv_k_k', 'grad_rwkv_k_a', 'grad_rwkv_r_k', 'grad_rwkv_gn_w', 'grad_rwkv_gn_b', 'grad_q_norm', 'grad_k_norm', 'grad_w_br_rwkv', 'grad_w_br_attn', 'grad_w_out', 'grad_ffn2_norm', 'grad_ffn2_w_gate', 'grad_ffn2_w_up', 'grad_ffn2_w_down', 'grad_ple_norm', 'grad_ple_w_gate', 'grad_ple_w_proj', 'delta_ffn1_norm', 'delta_ffn1_w_gate', 'delta_ffn1_w_up', 'delta_ffn1_w_down', 'delta_mix_norm', 'delta_w_in', 'delta_rwkv_mu', 'delta_rwkv_w0', 'delta_rwkv_w2', 'delta_rwkv_a0', 'delta_rwkv_a2', 'delta_rwkv_g2', 'delta_rwkv_k_k', 'delta_rwkv_k_a', 'delta_rwkv_r_k', 'delta_rwkv_gn_w', 'delta_rwkv_gn_b', 'delta_q_norm', 'delta_k_norm', 'delta_w_br_rwkv', 'delta_w_br_attn', 'delta_w_out', 'delta_ffn2_norm', 'delta_ffn2_w_gate', 'delta_ffn2_w_up', 'delta_ffn2_w_down', 'delta_ple_norm', 'delta_ple_w_gate', 'delta_ple_w_proj', 'new_m_ffn1_norm', 'new_m_ffn1_w_gate', 'new_m_ffn1_w_up', 'new_m_ffn1_w_down', 'new_m_mix_norm', 'new_m_w_in', 'new_m_rwkv_mu', 'new_m_rwkv_w0', 'new_m_rwkv_w2', 'new_m_rwkv_a0', 'new_m_rwkv_a2', 'new_m_rwkv_g2', 'new_m_rwkv_k_k', 'new_m_rwkv_k_a', 'new_m_rwkv_r_k', 'new_m_rwkv_gn_w', 'new_m_rwkv_gn_b', 'new_m_q_norm', 'new_m_k_norm', 'new_m_w_br_rwkv', 'new_m_w_br_attn', 'new_m_w_out', 'new_m_ffn2_norm', 'new_m_ffn2_w_gate', 'new_m_ffn2_w_up', 'new_m_ffn2_w_down', 'new_m_ple_norm', 'new_m_ple_w_gate', 'new_m_ple_w_proj', 'new_v_ffn1_norm', 'new_v_ffn1_w_gate', 'new_v_ffn1_w_up', 'new_v_ffn1_w_down', 'new_v_mix_norm', 'new_v_w_in', 'new_v_rwkv_mu', 'new_v_rwkv_w0', 'new_v_rwkv_w2', 'new_v_rwkv_a0', 'new_v_rwkv_a2', 'new_v_rwkv_g2', 'new_v_rwkv_k_k', 'new_v_rwkv_k_a', 'new_v_rwkv_r_k', 'new_v_rwkv_gn_w', 'new_v_rwkv_gn_b', 'new_v_q_norm', 'new_v_k_norm', 'new_v_w_br_rwkv', 'new_v_w_br_attn', 'new_v_w_out', 'new_v_ffn2_norm', 'new_v_ffn2_w_gate', 'new_v_ffn2_w_up', 'new_v_ffn2_w_down', 'new_v_ple_norm', 'new_v_ple_w_gate', 'new_v_ple_w_proj']
TWIN_LEAF_KINDS = {'loss': 'loss', 'grad_x': 'grad_x', 'grad_ffn1_norm': 'grad_w', 'grad_ffn1_w_gate': 'grad_w', 'grad_ffn1_w_up': 'grad_w', 'grad_ffn1_w_down': 'grad_w', 'grad_mix_norm': 'grad_w', 'grad_w_in': 'grad_w', 'grad_rwkv_mu': 'grad_w', 'grad_rwkv_w0': 'grad_w', 'grad_rwkv_w2': 'grad_w', 'grad_rwkv_a0': 'grad_w', 'grad_rwkv_a2': 'grad_w', 'grad_rwkv_g2': 'grad_w', 'grad_rwkv_k_k': 'grad_w', 'grad_rwkv_k_a': 'grad_w', 'grad_rwkv_r_k': 'grad_w', 'grad_rwkv_gn_w': 'grad_w', 'grad_rwkv_gn_b': 'grad_w', 'grad_q_norm': 'grad_w', 'grad_k_norm': 'grad_w', 'grad_w_br_rwkv': 'grad_w', 'grad_w_br_attn': 'grad_w', 'grad_w_out': 'grad_w', 'grad_ffn2_norm': 'grad_w', 'grad_ffn2_w_gate': 'grad_w', 'grad_ffn2_w_up': 'grad_w', 'grad_ffn2_w_down': 'grad_w', 'grad_ple_norm': 'grad_w', 'grad_ple_w_gate': 'grad_w', 'grad_ple_w_proj': 'grad_w', 'delta_ffn1_norm': 'delta_w', 'delta_ffn1_w_gate': 'delta_w', 'delta_ffn1_w_up': 'delta_w', 'delta_ffn1_w_down': 'delta_w', 'delta_mix_norm': 'delta_w', 'delta_w_in': 'delta_w', 'delta_rwkv_mu': 'delta_w', 'delta_rwkv_w0': 'delta_w', 'delta_rwkv_w2': 'delta_w', 'delta_rwkv_a0': 'delta_w', 'delta_rwkv_a2': 'delta_w', 'delta_rwkv_g2': 'delta_w', 'delta_rwkv_k_k': 'delta_w', 'delta_rwkv_k_a': 'delta_w', 'delta_rwkv_r_k': 'delta_w', 'delta_rwkv_gn_w': 'delta_w', 'delta_rwkv_gn_b': 'delta_w', 'delta_q_norm': 'delta_w', 'delta_k_norm': 'delta_w', 'delta_w_br_rwkv': 'delta_w', 'delta_w_br_attn': 'delta_w', 'delta_w_out': 'delta_w', 'delta_ffn2_norm': 'delta_w', 'delta_ffn2_w_gate': 'delta_w', 'delta_ffn2_w_up': 'delta_w', 'delta_ffn2_w_down': 'delta_w', 'delta_ple_norm': 'delta_w', 'delta_ple_w_gate': 'delta_w', 'delta_ple_w_proj': 'delta_w', 'new_m_ffn1_norm': 'new_m', 'new_m_ffn1_w_gate': 'new_m', 'new_m_ffn1_w_up': 'new_m', 'new_m_ffn1_w_down': 'new_m', 'new_m_mix_norm': 'new_m', 'new_m_w_in': 'new_m', 'new_m_rwkv_mu': 'new_m', 'new_m_rwkv_w0': 'new_m', 'new_m_rwkv_w2': 'new_m', 'new_m_rwkv_a0': 'new_m', 'new_m_rwkv_a2': 'new_m', 'new_m_rwkv_g2': 'new_m', 'new_m_rwkv_k_k': 'new_m', 'new_m_rwkv_k_a': 'new_m', 'new_m_rwkv_r_k': 'new_m', 'new_m_rwkv_gn_w': 'new_m', 'new_m_rwkv_gn_b': 'new_m', 'new_m_q_norm': 'new_m', 'new_m_k_norm': 'new_m', 'new_m_w_br_rwkv': 'new_m', 'new_m_w_br_attn': 'new_m', 'new_m_w_out': 'new_m', 'new_m_ffn2_norm': 'new_m', 'new_m_ffn2_w_gate': 'new_m', 'new_m_ffn2_w_up': 'new_m', 'new_m_ffn2_w_down': 'new_m', 'new_m_ple_norm': 'new_m', 'new_m_ple_w_gate': 'new_m', 'new_m_ple_w_proj': 'new_m', 'new_v_ffn1_norm': 'new_v', 'new_v_ffn1_w_gate': 'new_v', 'new_v_ffn1_w_up': 'new_v', 'new_v_ffn1_w_down': 'new_v', 'new_v_mix_norm': 'new_v', 'new_v_w_in': 'new_v', 'new_v_rwkv_mu': 'new_v', 'new_v_rwkv_w0': 'new_v', 'new_v_rwkv_w2': 'new_v', 'new_v_rwkv_a0': 'new_v', 'new_v_rwkv_a2': 'new_v', 'new_v_rwkv_g2': 'new_v', 'new_v_rwkv_k_k': 'new_v', 'new_v_rwkv_k_a': 'new_v', 'new_v_rwkv_r_k': 'new_v', 'new_v_rwkv_gn_w': 'new_v', 'new_v_rwkv_gn_b': 'new_v', 'new_v_q_norm': 'new_v', 'new_v_k_norm': 'new_v', 'new_v_w_br_rwkv': 'new_v', 'new_v_w_br_attn': 'new_v', 'new_v_w_out': 'new_v', 'new_v_ffn2_norm': 'new_v', 'new_v_ffn2_w_gate': 'new_v', 'new_v_ffn2_w_up': 'new_v', 'new_v_ffn2_w_down': 'new_v', 'new_v_ple_norm': 'new_v', 'new_v_ple_w_gate': 'new_v', 'new_v_ple_w_proj': 'new_v'}


def _forward(args):
    return _fwd_reference(*[args[k] for k in FWD_PARAMS])


def _output_shape():
    out = _jax.eval_shape(lambda: _forward(_fwd_setup_inputs(0)))
    return out.shape, out.dtype

N_MICROBATCH = 1
ADAM_LR = 0.001
ADAM_B1 = 0.9
ADAM_B2 = 0.999
ADAM_EPS = 1e-08
ADAM_WD = 0.01
ADAM_STEP = 10
PER_EXAMPLE_BATCH_AXIS = {'x': 0, 'p': 1, 'positions': 0, 'loss_target': 0}
SHARED_INPUTS = []
_WEIGHT_DTYPES = {'ffn1_norm': _jnp.float32, 'ffn1_w_gate': _jnp.float32, 'ffn1_w_up': _jnp.float32, 'ffn1_w_down': _jnp.float32, 'mix_norm': _jnp.float32, 'w_in': _jnp.float32, 'rwkv_mu': _jnp.float32, 'rwkv_w0': _jnp.float32, 'rwkv_w2': _jnp.float32, 'rwkv_a0': _jnp.float32, 'rwkv_a2': _jnp.float32, 'rwkv_g2': _jnp.float32, 'rwkv_k_k': _jnp.float32, 'rwkv_k_a': _jnp.float32, 'rwkv_r_k': _jnp.float32, 'rwkv_gn_w': _jnp.float32, 'rwkv_gn_b': _jnp.float32, 'q_norm': _jnp.float32, 'k_norm': _jnp.float32, 'w_br_rwkv': _jnp.float32, 'w_br_attn': _jnp.float32, 'w_out': _jnp.float32, 'ffn2_norm': _jnp.float32, 'ffn2_w_gate': _jnp.float32, 'ffn2_w_up': _jnp.float32, 'ffn2_w_down': _jnp.float32, 'ple_norm': _jnp.float32, 'ple_w_gate': _jnp.float32, 'ple_w_proj': _jnp.float32}
MOMENT_SCALE = {'ffn1_norm': 3.056958e+00, 'ffn1_w_gate': 5.103074e-02, 'ffn1_w_up': 5.355578e-02, 'ffn1_w_down': 8.703433e-02, 'mix_norm': 4.164138e-01, 'w_in': 6.117058e-02, 'rwkv_mu': 1.199124e+00, 'rwkv_w0': 5.166685e-02, 'rwkv_w2': 6.651812e-03, 'rwkv_a0': 1.109689e-01, 'rwkv_a2': 3.905693e-02, 'rwkv_g2': 2.392734e+00, 'rwkv_k_k': 9.372988e-02, 'rwkv_k_a': 2.307538e-01, 'rwkv_r_k': 1.755495e+00, 'rwkv_gn_w': 4.147841e+00, 'rwkv_gn_b': 5.909700e-01, 'q_norm': 4.182704e-01, 'k_norm': 4.241186e-01, 'w_br_rwkv': 1.181527e-01, 'w_br_attn': 2.479528e-02, 'w_out': 9.318068e-02, 'ffn2_norm': 3.096397e+00, 'ffn2_w_gate': 4.223866e-02, 'ffn2_w_up': 4.693917e-02, 'ffn2_w_down': 7.579576e-02, 'ple_norm': 4.803548e-01, 'ple_w_gate': 4.941844e-02, 'ple_w_proj': 2.529332e-01}


def _to_microbatches(a, axis):
    t = _jnp.moveaxis(a, axis, 0)
    t = t.reshape((N_MICROBATCH, t.shape[0] // N_MICROBATCH) + t.shape[1:])
    return _jnp.moveaxis(t, 1, axis + 1)


def setup_inputs(seed: int = 0) -> dict:
    inp = _fwd_setup_inputs(seed)
    key = _jax.random.fold_in(_jax.random.key(seed), 7919)
    shape, _ = _output_shape()
    out = dict(inp)
    out["loss_target"] = _jax.random.normal(_jax.random.fold_in(key, 0), shape, _jnp.float32)
    for i, name in enumerate(TWIN_WEIGHTS):
        w = inp[name].astype(_jnp.float32)
        if MOMENT_SCALE is None:
            s = _jnp.sqrt(_jnp.mean(_jnp.square(w)) + 1e-30)
        else:
            s = MOMENT_SCALE[name]
        km, kv = _jax.random.split(_jax.random.fold_in(key, i + 1))
        out[name] = w
        out["m_" + name] = s * _jax.random.normal(km, w.shape, _jnp.float32)
        out["v_" + name] = (s * s) * _jax.random.uniform(kv, w.shape, _jnp.float32, 0.5, 1.5)
    if N_MICROBATCH > 1:
        for name, axis in PER_EXAMPLE_BATCH_AXIS.items():
            out[name] = _to_microbatches(out[name], axis)
    return {'x': out['x'], 'p': out['p'], 'positions': out['positions'], 'ffn1_norm': out['ffn1_norm'], 'ffn1_w_gate': out['ffn1_w_gate'], 'ffn1_w_up': out['ffn1_w_up'], 'ffn1_w_down': out['ffn1_w_down'], 'mix_norm': out['mix_norm'], 'w_in': out['w_in'], 'rwkv_mu': out['rwkv_mu'], 'rwkv_w0': out['rwkv_w0'], 'rwkv_w2': out['rwkv_w2'], 'rwkv_a0': out['rwkv_a0'], 'rwkv_a2': out['rwkv_a2'], 'rwkv_g2': out['rwkv_g2'], 'rwkv_k_k': out['rwkv_k_k'], 'rwkv_k_a': out['rwkv_k_a'], 'rwkv_r_k': out['rwkv_r_k'], 'rwkv_gn_w': out['rwkv_gn_w'], 'rwkv_gn_b': out['rwkv_gn_b'], 'q_norm': out['q_norm'], 'k_norm': out['k_norm'], 'w_br_rwkv': out['w_br_rwkv'], 'w_br_attn': out['w_br_attn'], 'w_out': out['w_out'], 'ffn2_norm': out['ffn2_norm'], 'ffn2_w_gate': out['ffn2_w_gate'], 'ffn2_w_up': out['ffn2_w_up'], 'ffn2_w_down': out['ffn2_w_down'], 'ple_norm': out['ple_norm'], 'ple_w_gate': out['ple_w_gate'], 'ple_w_proj': out['ple_w_proj'], 'loss_target': out['loss_target'], 'm_ffn1_norm': out['m_ffn1_norm'], 'm_ffn1_w_gate': out['m_ffn1_w_gate'], 'm_ffn1_w_up': out['m_ffn1_w_up'], 'm_ffn1_w_down': out['m_ffn1_w_down'], 'm_mix_norm': out['m_mix_norm'], 'm_w_in': out['m_w_in'], 'm_rwkv_mu': out['m_rwkv_mu'], 'm_rwkv_w0': out['m_rwkv_w0'], 'm_rwkv_w2': out['m_rwkv_w2'], 'm_rwkv_a0': out['m_rwkv_a0'], 'm_rwkv_a2': out['m_rwkv_a2'], 'm_rwkv_g2': out['m_rwkv_g2'], 'm_rwkv_k_k': out['m_rwkv_k_k'], 'm_rwkv_k_a': out['m_rwkv_k_a'], 'm_rwkv_r_k': out['m_rwkv_r_k'], 'm_rwkv_gn_w': out['m_rwkv_gn_w'], 'm_rwkv_gn_b': out['m_rwkv_gn_b'], 'm_q_norm': out['m_q_norm'], 'm_k_norm': out['m_k_norm'], 'm_w_br_rwkv': out['m_w_br_rwkv'], 'm_w_br_attn': out['m_w_br_attn'], 'm_w_out': out['m_w_out'], 'm_ffn2_norm': out['m_ffn2_norm'], 'm_ffn2_w_gate': out['m_ffn2_w_gate'], 'm_ffn2_w_up': out['m_ffn2_w_up'], 'm_ffn2_w_down': out['m_ffn2_w_down'], 'm_ple_norm': out['m_ple_norm'], 'm_ple_w_gate': out['m_ple_w_gate'], 'm_ple_w_proj': out['m_ple_w_proj'], 'v_ffn1_norm': out['v_ffn1_norm'], 'v_ffn1_w_gate': out['v_ffn1_w_gate'], 'v_ffn1_w_up': out['v_ffn1_w_up'], 'v_ffn1_w_down': out['v_ffn1_w_down'], 'v_mix_norm': out['v_mix_norm'], 'v_w_in': out['v_w_in'], 'v_rwkv_mu': out['v_rwkv_mu'], 'v_rwkv_w0': out['v_rwkv_w0'], 'v_rwkv_w2': out['v_rwkv_w2'], 'v_rwkv_a0': out['v_rwkv_a0'], 'v_rwkv_a2': out['v_rwkv_a2'], 'v_rwkv_g2': out['v_rwkv_g2'], 'v_rwkv_k_k': out['v_rwkv_k_k'], 'v_rwkv_k_a': out['v_rwkv_k_a'], 'v_rwkv_r_k': out['v_rwkv_r_k'], 'v_rwkv_gn_w': out['v_rwkv_gn_w'], 'v_rwkv_gn_b': out['v_rwkv_gn_b'], 'v_q_norm': out['v_q_norm'], 'v_k_norm': out['v_k_norm'], 'v_w_br_rwkv': out['v_w_br_rwkv'], 'v_w_br_attn': out['v_w_br_attn'], 'v_w_out': out['v_w_out'], 'v_ffn2_norm': out['v_ffn2_norm'], 'v_ffn2_w_gate': out['v_ffn2_w_gate'], 'v_ffn2_w_up': out['v_ffn2_w_up'], 'v_ffn2_w_down': out['v_ffn2_w_down'], 'v_ple_norm': out['v_ple_norm'], 'v_ple_w_gate': out['v_ple_w_gate'], 'v_ple_w_proj': out['v_ple_w_proj']}


def _loss(weights, diff, rest, loss_target):
    with _jax.named_scope("forward"):
        args = {**rest, TWIN_DIFF_INPUT: diff, **{k: w.astype(_WEIGHT_DTYPES[k]) for k, w in weights.items()}}
        y = _forward(args)
    with _jax.named_scope("loss_head"):
        err = _jnp.square(y.astype(_jnp.float32) - loss_target)
        return 0.5 * _jnp.sum(_jnp.mean(err, axis=-1)) if err.ndim else 0.5 * err


def _adamw(w, g, m, v):
    m = ADAM_B1 * m + (1.0 - ADAM_B1) * g
    v = ADAM_B2 * v + (1.0 - ADAM_B2) * _jnp.square(g)
    m_hat = m / (1.0 - ADAM_B1 ** ADAM_STEP)
    v_hat = v / (1.0 - ADAM_B2 ** ADAM_STEP)
    delta = -ADAM_LR * (m_hat / (_jnp.sqrt(v_hat) + ADAM_EPS) + ADAM_WD * w)
    return delta, m, v


def reference(x, p, positions, ffn1_norm, ffn1_w_gate, ffn1_w_up, ffn1_w_down, mix_norm, w_in, rwkv_mu, rwkv_w0, rwkv_w2, rwkv_a0, rwkv_a2, rwkv_g2, rwkv_k_k, rwkv_k_a, rwkv_r_k, rwkv_gn_w, rwkv_gn_b, q_norm, k_norm, w_br_rwkv, w_br_attn, w_out, ffn2_norm, ffn2_w_gate, ffn2_w_up, ffn2_w_down, ple_norm, ple_w_gate, ple_w_proj, loss_target, m_ffn1_norm, m_ffn1_w_gate, m_ffn1_w_up, m_ffn1_w_down, m_mix_norm, m_w_in, m_rwkv_mu, m_rwkv_w0, m_rwkv_w2, m_rwkv_a0, m_rwkv_a2, m_rwkv_g2, m_rwkv_k_k, m_rwkv_k_a, m_rwkv_r_k, m_rwkv_gn_w, m_rwkv_gn_b, m_q_norm, m_k_norm, m_w_br_rwkv, m_w_br_attn, m_w_out, m_ffn2_norm, m_ffn2_w_gate, m_ffn2_w_up, m_ffn2_w_down, m_ple_norm, m_ple_w_gate, m_ple_w_proj, v_ffn1_norm, v_ffn1_w_gate, v_ffn1_w_up, v_ffn1_w_down, v_mix_norm, v_w_in, v_rwkv_mu, v_rwkv_w0, v_rwkv_w2, v_rwkv_a0, v_rwkv_a2, v_rwkv_g2, v_rwkv_k_k, v_rwkv_k_a, v_rwkv_r_k, v_rwkv_gn_w, v_rwkv_gn_b, v_q_norm, v_k_norm, v_w_br_rwkv, v_w_br_attn, v_w_out, v_ffn2_norm, v_ffn2_w_gate, v_ffn2_w_up, v_ffn2_w_down, v_ple_norm, v_ple_w_gate, v_ple_w_proj):
    given = dict(x=x, p=p, positions=positions, ffn1_norm=ffn1_norm, ffn1_w_gate=ffn1_w_gate, ffn1_w_up=ffn1_w_up, ffn1_w_down=ffn1_w_down, mix_norm=mix_norm, w_in=w_in, rwkv_mu=rwkv_mu, rwkv_w0=rwkv_w0, rwkv_w2=rwkv_w2, rwkv_a0=rwkv_a0, rwkv_a2=rwkv_a2, rwkv_g2=rwkv_g2, rwkv_k_k=rwkv_k_k, rwkv_k_a=rwkv_k_a, rwkv_r_k=rwkv_r_k, rwkv_gn_w=rwkv_gn_w, rwkv_gn_b=rwkv_gn_b, q_norm=q_norm, k_norm=k_norm, w_br_rwkv=w_br_rwkv, w_br_attn=w_br_attn, w_out=w_out, ffn2_norm=ffn2_norm, ffn2_w_gate=ffn2_w_gate, ffn2_w_up=ffn2_w_up, ffn2_w_down=ffn2_w_down, ple_norm=ple_norm, ple_w_gate=ple_w_gate, ple_w_proj=ple_w_proj, loss_target=loss_target, m_ffn1_norm=m_ffn1_norm, m_ffn1_w_gate=m_ffn1_w_gate, m_ffn1_w_up=m_ffn1_w_up, m_ffn1_w_down=m_ffn1_w_down, m_mix_norm=m_mix_norm, m_w_in=m_w_in, m_rwkv_mu=m_rwkv_mu, m_rwkv_w0=m_rwkv_w0, m_rwkv_w2=m_rwkv_w2, m_rwkv_a0=m_rwkv_a0, m_rwkv_a2=m_rwkv_a2, m_rwkv_g2=m_rwkv_g2, m_rwkv_k_k=m_rwkv_k_k, m_rwkv_k_a=m_rwkv_k_a, m_rwkv_r_k=m_rwkv_r_k, m_rwkv_gn_w=m_rwkv_gn_w, m_rwkv_gn_b=m_rwkv_gn_b, m_q_norm=m_q_norm, m_k_norm=m_k_norm, m_w_br_rwkv=m_w_br_rwkv, m_w_br_attn=m_w_br_attn, m_w_out=m_w_out, m_ffn2_norm=m_ffn2_norm, m_ffn2_w_gate=m_ffn2_w_gate, m_ffn2_w_up=m_ffn2_w_up, m_ffn2_w_down=m_ffn2_w_down, m_ple_norm=m_ple_norm, m_ple_w_gate=m_ple_w_gate, m_ple_w_proj=m_ple_w_proj, v_ffn1_norm=v_ffn1_norm, v_ffn1_w_gate=v_ffn1_w_gate, v_ffn1_w_up=v_ffn1_w_up, v_ffn1_w_down=v_ffn1_w_down, v_mix_norm=v_mix_norm, v_w_in=v_w_in, v_rwkv_mu=v_rwkv_mu, v_rwkv_w0=v_rwkv_w0, v_rwkv_w2=v_rwkv_w2, v_rwkv_a0=v_rwkv_a0, v_rwkv_a2=v_rwkv_a2, v_rwkv_g2=v_rwkv_g2, v_rwkv_k_k=v_rwkv_k_k, v_rwkv_k_a=v_rwkv_k_a, v_rwkv_r_k=v_rwkv_r_k, v_rwkv_gn_w=v_rwkv_gn_w, v_rwkv_gn_b=v_rwkv_gn_b, v_q_norm=v_q_norm, v_k_norm=v_k_norm, v_w_br_rwkv=v_w_br_rwkv, v_w_br_attn=v_w_br_attn, v_w_out=v_w_out, v_ffn2_norm=v_ffn2_norm, v_ffn2_w_gate=v_ffn2_w_gate, v_ffn2_w_up=v_ffn2_w_up, v_ffn2_w_down=v_ffn2_w_down, v_ple_norm=v_ple_norm, v_ple_w_gate=v_ple_w_gate, v_ple_w_proj=v_ple_w_proj)
    weights = {n: given[n] for n in TWIN_WEIGHTS}
    shared = {n: given[n] for n in SHARED_INPUTS}
    per_example = {n: given[n] for n in ['x', 'p', 'positions']}
    grad_fn = _jax.value_and_grad(_loss, argnums=(0, 1))

    def one_microbatch(ex, loss_target):
        ex = dict(ex)
        diff = ex.pop(TWIN_DIFF_INPUT)
        return grad_fn(weights, diff, {**shared, **ex}, loss_target)

    if N_MICROBATCH == 1:
        loss, (grad_w, grad_x) = one_microbatch(per_example, given["loss_target"])
    else:
        def body(carry, xs):
            loss_sum, grad_sum = carry
            l_k, (gw_k, gx_k) = one_microbatch(xs[0], xs[1])
            with _jax.named_scope("update"):
                return (loss_sum + l_k, _jax.tree.map(_jnp.add, grad_sum, gw_k)), gx_k

        init = (_jnp.zeros((), _jnp.float32), _jax.tree.map(_jnp.zeros_like, weights))
        (loss, grad_w), grad_x = _jax.lax.scan(body, init, (per_example, given["loss_target"]))
    with _jax.named_scope("update"):
        delta_w, new_m, new_v = {}, {}, {}
        for n in TWIN_WEIGHTS:
            delta_w[n], new_m[n], new_v[n] = _adamw(weights[n], grad_w[n], given["m_" + n], given["v_" + n])
    return (loss, grad_x, *[grad_w[n] for n in TWIN_WEIGHTS], *[delta_w[n] for n in TWIN_WEIGHTS],
            *[new_m[n] for n in TWIN_WEIGHTS], *[new_v[n] for n in TWIN_WEIGHTS])
```

```python
import functools

import jax
import jax.numpy as jnp
from jax import lax
from jax.experimental import pallas as pl
from jax.experimental.pallas import tpu as pltpu

F32, BF16 = jnp.float32, jnp.bfloat16
MESH = pl.DeviceIdType.MESH
N_DEV = 8
LANES = 128
VMEM_LIMIT = 56 * 1024 * 1024

D_MODEL = 1024
PLE_DIM = 256
HEAD_DIM = 64
RWKV_HEADS = 8
RWKV_DIM = RWKV_HEADS * HEAD_DIM
DECAY_LORA = 64
ICLR_LORA = 64
GATE_LORA = 128
GN_EPS = 64e-5
ATTN_GROUPS = ((128, 1), (512, 4), (2048, 16))
HEADS_PER_GROUP = 4
ATTN_HEADS = HEADS_PER_GROUP * len(ATTN_GROUPS)
ATTN_DIM = ATTN_HEADS * HEAD_DIM
BAND_BLOCK = 128
ROPE_THETA = 10000.0
NEG_INF = -1e30
D_FF = 2816
RMS_EPS = 1e-6
RWKV_COLS = 3 * RWKV_DIM + DECAY_LORA + ICLR_LORA + GATE_LORA
ADAM_LR = 0.001
ADAM_B1 = 0.9
ADAM_B2 = 0.999
ADAM_EPS = 1e-08
ADAM_WD = 0.01
ADAM_STEP = 10

K_LO = 16
K_HI = HEAD_DIM // K_LO
SCAN_CHUNK = 32

SHARDED = (
    ("ffn1_w_gate", D_MODEL, D_FF, 1), ("ffn1_w_up", D_MODEL, D_FF, 1), ("ffn1_w_down", D_FF, D_MODEL, 0),
    ("w_in", D_MODEL, 6144, 1),
    ("rwkv_w2", DECAY_LORA, RWKV_DIM, 1), ("rwkv_a2", ICLR_LORA, RWKV_DIM, 1), ("rwkv_g2", GATE_LORA, RWKV_DIM, 1),
    ("w_br_rwkv", RWKV_DIM, D_MODEL, 1), ("w_br_attn", 256, D_MODEL, 1), ("w_out", D_MODEL, D_MODEL, 0),
    ("ffn2_w_gate", D_MODEL, D_FF, 1), ("ffn2_w_up", D_MODEL, D_FF, 1), ("ffn2_w_down", D_FF, D_MODEL, 0),
    ("ple_w_gate", D_MODEL, D_MODEL, 0), ("ple_w_proj", PLE_DIM, D_MODEL, 1),
)
SMALL = ("ffn1_norm", "mix_norm", "rwkv_mu", "rwkv_w0", "rwkv_a0", "rwkv_k_k", "rwkv_k_a", "rwkv_r_k",
         "rwkv_gn_w", "rwkv_gn_b", "q_norm", "k_norm", "ffn2_norm", "ple_norm")
WEIGHTS = ("ffn1_norm", "ffn1_w_gate", "ffn1_w_up", "ffn1_w_down", "mix_norm", "w_in", "rwkv_mu", "rwkv_w0",
           "rwkv_w2", "rwkv_a0", "rwkv_a2", "rwkv_g2", "rwkv_k_k", "rwkv_k_a", "rwkv_r_k", "rwkv_gn_w",
           "rwkv_gn_b", "q_norm", "k_norm", "w_br_rwkv", "w_br_attn", "w_out", "ffn2_norm", "ffn2_w_gate",
           "ffn2_w_up", "ffn2_w_down", "ple_norm", "ple_w_gate", "ple_w_proj")


def _cparams(**kw):
    return pltpu.CompilerParams(vmem_limit_bytes=VMEM_LIMIT, **kw)


def _tile(n, cap):
    best = None
    for t in range(LANES, min(n, cap) + 1, LANES):
        if n % t == 0:
            best = t
    return best if best is not None else n


@jax.custom_vjp
def _bdot(a, w):
    return jnp.dot(a.astype(BF16), w.astype(BF16), preferred_element_type=F32)


def _bdot_fwd(a, w):
    return _bdot(a, w), (a, w)


def _bdot_bwd(res, g):
    a, w = res
    gb = g.astype(BF16)
    da = lax.dot_general(gb, w.astype(BF16), (((1,), (1,)), ((), ())), preferred_element_type=F32)
    dw = lax.dot_general(a.astype(BF16), gb, (((0,), (0,)), ((), ())), preferred_element_type=F32)
    return da.astype(a.dtype), dw.astype(w.dtype)


_bdot.defvjp(_bdot_fwd, _bdot_bwd)


@jax.custom_vjp
def _bdot_nt(a, b):
    return lax.dot_general(a.astype(BF16), b.astype(BF16), (((1,), (1,)), ((), ())), preferred_element_type=F32)


def _bdot_nt_fwd(a, b):
    return _bdot_nt(a, b), (a, b)


def _bdot_nt_bwd(res, g):
    a, b = res
    gb = g.astype(BF16)
    da = jnp.dot(gb, b.astype(BF16), preferred_element_type=F32)
    db = lax.dot_general(gb, a.astype(BF16), (((0,), (0,)), ((), ())), preferred_element_type=F32)
    return da.astype(a.dtype), db.astype(b.dtype)


_bdot_nt.defvjp(_bdot_nt_fwd, _bdot_nt_bwd)


def _mm(name, a, b, mode, out_dtype=F32, res=None, scale=None):
    if mode == "nn":
        (m, k), n = a.shape, b.shape[1]
    elif mode == "nt":
        (m, k), n = a.shape, b.shape[0]
    else:
        (k, m), n = a.shape, b.shape[1]
    tm, tn = _tile(m, 512), _tile(n, 512)
    a_spec = pl.BlockSpec((k, tm), lambda i, j: (0, i)) if mode == "tn" else pl.BlockSpec((tm, k), lambda i, j: (i, 0))
    b_spec = pl.BlockSpec((tn, k), lambda i, j: (j, 0)) if mode == "nt" else pl.BlockSpec((k, tn), lambda i, j: (0, j))
    dims = {"nn": ((1,), (0,)), "nt": ((1,), (1,)), "tn": ((0,), (0,))}[mode]
    o_spec = pl.BlockSpec((tm, tn), lambda i, j: (i, j))
    ins, in_specs = [a, b], [a_spec, b_spec]
    if res is not None:
        ins.append(res)
        in_specs.append(o_spec)

    def body(*refs):
        acc = lax.dot_general(refs[0][...].astype(BF16), refs[1][...].astype(BF16), (dims, ((), ())),
                              preferred_element_type=F32)
        if scale is not None:
            acc = acc * scale
        if res is not None:
            acc = acc + refs[2][...].astype(F32)
        refs[-1][...] = acc.astype(refs[-1].dtype)

    return pl.pallas_call(
        body, name=name, grid=(m // tm, n // tn), in_specs=in_specs, out_specs=o_spec,
        out_shape=jax.ShapeDtypeStruct((m, n), out_dtype),
        compiler_params=_cparams(dimension_semantics=("parallel", "parallel")),
    )(*ins)


def _rowwise(name, fn, rows, params, out_rows, out_accs=(), tr=256):
    r = rows[0].shape[0]
    in_specs = [pl.BlockSpec((tr, a.shape[1]), lambda i: (i, 0)) for a in rows]
    in_specs += [pl.BlockSpec(p.shape, lambda i, nd=p.ndim: (0,) * nd) for p in params]
    out_shape = [jax.ShapeDtypeStruct((r, c), dt) for c, dt in out_rows]
    out_shape += [jax.ShapeDtypeStruct(s, F32) for s in out_accs]
    out_specs = [pl.BlockSpec((tr, c), lambda i: (i, 0)) for c, _ in out_rows]
    out_specs += [pl.BlockSpec(s, lambda i, nd=len(s): (0,) * nd) for s in out_accs]
    n_in, n_ro = len(rows) + len(params), len(out_rows)

    def body(*refs):
        res = fn(*[ref[...] for ref in refs[:n_in]])
        outs = refs[n_in:]
        for o, v in zip(outs[:n_ro], res[:n_ro]):
            o[...] = v.astype(o.dtype)
        for o, v in zip(outs[n_ro:], res[n_ro:]):
            _accumulate(o, v)

    return pl.pallas_call(
        body, name=name, grid=(r // tr,), in_specs=in_specs, out_specs=out_specs, out_shape=out_shape,
        compiler_params=_cparams(dimension_semantics=("arbitrary",)),
    )(*rows, *params)


def _accumulate(o_ref, v):
    @pl.when(pl.program_id(0) == 0)
    def _():
        o_ref[...] = v

    @pl.when(pl.program_id(0) != 0)
    def _():
        o_ref[...] += v


def _rms(x, g):
    return x * lax.rsqrt(jnp.mean(x * x, axis=-1, keepdims=True) + RMS_EPS) * g


def _sigmoid(x):
    return jax.nn.sigmoid(x)


def _softplus(x):
    return jnp.maximum(x, 0.0) + jnp.log1p(jnp.exp(-jnp.abs(x)))


def _norm_fwd(name, x, g):
    return _rowwise(name, lambda xv, gv: (_rms(xv, gv),), [x], [g], [(x.shape[1], BF16)])[0]


def _norm_bwd(name, x, g, dh, dres):
    def fn(xv, dhv, drv, gv):
        _, vjp = jax.vjp(_rms, xv, gv)
        dx, dg = vjp(dhv)
        return dx + drv, dg
    return _rowwise(name, fn, [x, dh, dres], [g], [(x.shape[1], F32)], [g.shape])


def _f_act(gate, up):
    return gate * _sigmoid(gate) * up


def _ffn_fwd(tag, x, norm, w_gu, w_down):
    h = _norm_fwd(tag + "_norm", x, norm)
    gu = _mm(tag + "_gu", h, w_gu, "nn")

    def act(guv):
        return (_f_act(guv[:, :D_FF], guv[:, D_FF:]),)
    a = _rowwise(tag + "_act", act, [gu], [], [(D_FF, BF16)])[0]
    out = _mm(tag + "_down", a, w_down, "nn", res=x, scale=0.5)
    return out, (h, gu, a)


def _ffn_bwd(tag, x, norm, w_gu, w_down, saved, dout):
    h, gu, a = saved
    da = _mm(tag + "_dact", dout, w_down, "nt", scale=0.5)
    d_wdown = _mm(tag + "_dwdown", a, dout, "tn", scale=0.5)

    def act_bwd(guv, dav):
        _, vjp = jax.vjp(_f_act, guv[:, :D_FF], guv[:, D_FF:])
        dg, du = vjp(dav)
        return (jnp.concatenate([dg, du], axis=1),)
    dgu = _rowwise(tag + "_dgu", act_bwd, [gu, da], [], [(2 * D_FF, BF16)])[0]
    dh = _mm(tag + "_dh", dgu, w_gu, "nt")
    d_wgu = _mm(tag + "_dwgu", h, dgu, "tn")
    dx, dnorm = _norm_bwd(tag + "_dnorm", x, norm, dh, dout)
    return dx, dnorm, d_wgu[:, :D_FF], d_wgu[:, D_FF:], d_wdown


def _shift_fwd(z, mu):
    t, c = z.shape
    tr = 256

    def body(z_ref, zp_ref, mu_ref, r_ref, k_ref, v_ref, lo_ref, gd_ref):
        zv = z_ref[...]
        prev = zp_ref[7:8, :] * jnp.where(pl.program_id(0) == 0, 0.0, 1.0)
        row = lax.broadcasted_iota(jnp.int32, zv.shape, 0)
        zsh = jnp.where(row == 0, prev, pltpu.roll(zv, 1, 0))
        zs = zv + (zsh - zv) * mu_ref[...]
        r_ref[...] = zs[:, 0:512]
        k_ref[...] = zs[:, 512:1024]
        v_ref[...] = zs[:, 1024:1536]
        lo_ref[...] = zs[:, 1536:1664]
        gd_ref[...] = zs[:, 1664:1792]

    widths = (512, 512, 512, 128, 128)
    return pl.pallas_call(
        body, name="rwkv_shift", grid=(t // tr,),
        in_specs=[pl.BlockSpec((tr, c), lambda i: (i, 0)),
                  pl.BlockSpec((8, c), lambda i: (jnp.maximum(i * (tr // 8) - 1, 0), 0)),
                  pl.BlockSpec((1, c), lambda i: (0, 0))],
        out_specs=[pl.BlockSpec((tr, w), lambda i: (i, 0)) for w in widths],
        out_shape=[jax.ShapeDtypeStruct((t, w), F32) for w in widths],
        compiler_params=_cparams(dimension_semantics=("parallel",)),
    )(z, z, mu)


def _shift_bwd(z, mu, dr, dk, dv, dlo, dgd):
    t, c = z.shape
    tr = 256
    nt = t // tr

    def body(z_ref, zp_ref, mu_ref, dr_ref, dk_ref, dv_ref, dlo_ref, dgd_ref,
             drn_ref, dkn_ref, dvn_ref, dlon_ref, dgdn_ref, dz_ref, dmu_ref):
        i = pl.program_id(0)
        zv, muv = z_ref[...], mu_ref[...]
        prev = zp_ref[7:8, :] * jnp.where(i == 0, 0.0, 1.0)
        row = lax.broadcasted_iota(jnp.int32, zv.shape, 0)
        zsh = jnp.where(row == 0, prev, pltpu.roll(zv, 1, 0))
        dzs = jnp.concatenate([dr_ref[...], dk_ref[...], dv_ref[...], dlo_ref[...], dgd_ref[...]], axis=1)
        nxt = jnp.concatenate([drn_ref[0:1, :], dkn_ref[0:1, :], dvn_ref[0:1, :], dlon_ref[0:1, :],
                               dgdn_ref[0:1, :]], axis=1) * jnp.where(i == nt - 1, 0.0, 1.0)
        u = dzs * muv
        un = jnp.where(row == tr - 1, nxt * muv, pltpu.roll(u, tr - 1, 0))
        dz_ref[...] = (dzs - u + un).astype(dz_ref.dtype)
        _accumulate(dmu_ref, jnp.sum(dzs * (zsh - zv), axis=0, keepdims=True))

    widths = (512, 512, 512, 128, 128)
    nxt_map = lambda i: (jnp.minimum((i + 1) * (tr // 8), t // 8 - 1), 0)
    return pl.pallas_call(
        body, name="rwkv_shift_bwd", grid=(nt,),
        in_specs=[pl.BlockSpec((tr, c), lambda i: (i, 0)),
                  pl.BlockSpec((8, c), lambda i: (jnp.maximum(i * (tr // 8) - 1, 0), 0)),
                  pl.BlockSpec((1, c), lambda i: (0, 0))]
        + [pl.BlockSpec((tr, w), lambda i: (i, 0)) for w in widths]
        + [pl.BlockSpec((8, w), nxt_map) for w in widths],
        out_specs=[pl.BlockSpec((tr, c), lambda i: (i, 0)), pl.BlockSpec((1, c), lambda i: (0, 0))],
        out_shape=[jax.ShapeDtypeStruct((t, c), BF16), jax.ShapeDtypeStruct((1, c), F32)],
        compiler_params=_cparams(dimension_semantics=("arbitrary",)),
    )(z, z, mu, dr, dk, dv, dlo, dgd, dr, dk, dv, dlo, dgd)


def _f_pre(k, lo, gd, w0, w2p, a0, a2p, g2, k_k, k_a):
    lane = lax.broadcasted_iota(jnp.int32, lo.shape, 1)
    lo_act = jnp.where(lane < DECAY_LORA, jnp.tanh(lo), lo)
    w = -_softplus(-(w0 + _bdot(lo_act, w2p))) - 0.5
    a = _sigmoid(a0 + _bdot(lo_act, a2p))
    g = _bdot(_sigmoid(gd), g2)
    kk = k * k_k
    k2 = k * (1.0 + (a - 1.0) * k_a)
    decay = jnp.exp(-jnp.exp(w))
    return k2, kk, a, decay, g


def _f_kk(kk, a):
    kkn = kk * lax.rsqrt(jnp.maximum(jnp.sum(kk * kk, axis=-1, keepdims=True), 1e-24))
    return kkn, kkn * a


def _f_post(y, r, k2, v, g, gn_w, gn_b, r_k):
    reps = y.shape[0] // RWKV_HEADS
    mean = jnp.mean(y, axis=-1, keepdims=True)
    var = jnp.mean(jnp.square(y - mean), axis=-1, keepdims=True)
    yn = (y - mean) * lax.rsqrt(var + GN_EPS)
    yn = yn * jnp.tile(gn_w, (reps, 1)) + jnp.tile(gn_b, (reps, 1))
    bonus = jnp.sum(r * k2 * jnp.tile(r_k, (reps, 1)), axis=-1, keepdims=True) * v
    return (yn + bonus) * g


def _allreduce16(x, lane):
    for s in (1, 2, 4, 8):
        up = pltpu.roll(x, s, 1)
        dn = pltpu.roll(x, LANES - s, 1)
        x = x + jnp.where((lane & s) != 0, up, dn)
    return x


def _to_scan_rows(x):
    t = x.shape[0]
    return x.reshape(t, RWKV_HEADS, K_HI, K_LO).transpose(0, 2, 1, 3).reshape(t, K_HI, LANES)


def _from_scan_rows(x):
    t = x.shape[0]
    return x.reshape(t, K_HI, RWKV_HEADS, K_LO).transpose(0, 2, 1, 3).reshape(t, RWKV_DIM)


def _to_scan_cols(x):
    t = x.shape[0]
    return jnp.repeat(x.reshape(t, RWKV_HEADS, HEAD_DIM).transpose(0, 2, 1), K_LO, axis=2)


def _from_scan_cols(x):
    t = x.shape[0]
    return x[:, :, ::K_LO].transpose(0, 2, 1).reshape(t, RWKV_DIM)


def _wkv_fwd(r, w, k, kk, b, vx):
    t = r.shape[0]
    tc = SCAN_CHUNK
    row_spec = pl.BlockSpec((tc, K_HI, LANES), lambda i: (i, 0, 0))
    col_spec = pl.BlockSpec((tc, HEAD_DIM, LANES), lambda i: (i, 0, 0))

    def body(r_ref, w_ref, k_ref, kk_ref, b_ref, vx_ref, yx_ref, st_ref, s_scr):
        @pl.when(pl.program_id(0) == 0)
        def _():
            s_scr[...] = jnp.zeros_like(s_scr)

        lane = lax.broadcasted_iota(jnp.int32, (HEAD_DIM, LANES), 1)

        def step(ti, s):
            rt, wt, kt, kkt, bt = r_ref[ti], w_ref[ti], k_ref[ti], kk_ref[ti], b_ref[ti]
            vxt = vx_ref[ti]
            acc = s[0] * kkt[0:1]
            for j in range(1, K_HI):
                acc = acc + s[j] * kkt[j:j + 1]
            sa = -_allreduce16(acc, lane)
            new = tuple(s[j] * wt[j:j + 1] + sa * bt[j:j + 1] + vxt * kt[j:j + 1] for j in range(K_HI))
            y = new[0] * rt[0:1]
            for j in range(1, K_HI):
                y = y + new[j] * rt[j:j + 1]
            yx_ref[ti] = _allreduce16(y, lane)
            for j in range(K_HI):
                st_ref[ti, j] = new[j]
            return new

        s = lax.fori_loop(0, tc, step, tuple(s_scr[j] for j in range(K_HI)))
        for j in range(K_HI):
            s_scr[j] = s[j]

    return pl.pallas_call(
        body, name="wkv_fwd", grid=(t // tc,),
        in_specs=[row_spec] * 5 + [col_spec],
        out_specs=[col_spec, pl.BlockSpec((tc, K_HI, HEAD_DIM, LANES), lambda i: (i, 0, 0, 0))],
        out_shape=[jax.ShapeDtypeStruct((t, HEAD_DIM, LANES), F32),
                   jax.ShapeDtypeStruct((t, K_HI, HEAD_DIM, LANES), F32)],
        scratch_shapes=[pltpu.VMEM((K_HI, HEAD_DIM, LANES), F32)],
        compiler_params=_cparams(dimension_semantics=("arbitrary",)),
    )(r, w, k, kk, b, vx)


def _wkv_bwd(r, w, k, kk, b, vx, states, dyx):
    t = r.shape[0]
    tc = SCAN_CHUNK
    nb = t // tc
    row_spec = pl.BlockSpec((tc, K_HI, LANES), lambda i: (nb - 1 - i, 0, 0))
    col_spec = pl.BlockSpec((tc, HEAD_DIM, LANES), lambda i: (nb - 1 - i, 0, 0))
    st_spec = pl.BlockSpec((tc, K_HI, HEAD_DIM, LANES), lambda i: (nb - 1 - i, 0, 0, 0))
    stp_spec = pl.BlockSpec((1, K_HI, HEAD_DIM, LANES), lambda i: (jnp.maximum((nb - 1 - i) * tc - 1, 0), 0, 0, 0))

    def body(r_ref, w_ref, k_ref, kk_ref, b_ref, vx_ref, st_ref, stp_ref, dyx_ref,
             dr_ref, dw_ref, dk_ref, dkk_ref, db_ref, dvx_ref, ds_scr):
        @pl.when(pl.program_id(0) == 0)
        def _():
            ds_scr[...] = jnp.zeros_like(ds_scr)

        lane = lax.broadcasted_iota(jnp.int32, (HEAD_DIM, LANES), 1)

        def colsum(x):
            return jnp.sum(x, axis=0, keepdims=True)

        def rows(xs):
            return jnp.concatenate(xs, axis=0)

        def step(ti, ds, sp):
            rt, wt, kt, kkt, bt = r_ref[ti], w_ref[ti], k_ref[ti], kk_ref[ti], b_ref[ti]
            vxt, dy = vx_ref[ti], dyx_ref[ti]
            st = tuple(st_ref[ti, j] for j in range(K_HI))
            ds = tuple(ds[j] + dy * rt[j:j + 1] for j in range(K_HI))
            dr_ref[ti] = rows([colsum(st[j] * dy) for j in range(K_HI)])
            dk_ref[ti] = rows([colsum(ds[j] * vxt) for j in range(K_HI)])
            acc_v = ds[0] * kt[0:1]
            acc_sa = sp[0] * kkt[0:1]
            acc_dsa = ds[0] * bt[0:1]
            for j in range(1, K_HI):
                acc_v = acc_v + ds[j] * kt[j:j + 1]
                acc_sa = acc_sa + sp[j] * kkt[j:j + 1]
                acc_dsa = acc_dsa + ds[j] * bt[j:j + 1]
            dvx_ref[ti] = _allreduce16(acc_v, lane)
            sa = -_allreduce16(acc_sa, lane)
            dsa = _allreduce16(acc_dsa, lane)
            db_ref[ti] = rows([colsum(ds[j] * sa) for j in range(K_HI)])
            dw_ref[ti] = rows([colsum(ds[j] * sp[j]) for j in range(K_HI)])
            dkk_ref[ti] = rows([-colsum(sp[j] * dsa) for j in range(K_HI)])
            return tuple(ds[j] * wt[j:j + 1] - dsa * kkt[j:j + 1] for j in range(K_HI))

        def loop_body(n, ds):
            ti = tc - 1 - n
            return step(ti, ds, tuple(st_ref[ti - 1, j] for j in range(K_HI)))

        ds = lax.fori_loop(0, tc - 1, loop_body, tuple(ds_scr[j] for j in range(K_HI)))
        keep = jnp.where(pl.program_id(0) == nb - 1, 0.0, 1.0)
        ds = step(0, ds, tuple(stp_ref[0, j] * keep for j in range(K_HI)))
        for j in range(K_HI):
            ds_scr[j] = ds[j]

    row_out = jax.ShapeDtypeStruct((t, K_HI, LANES), F32)
    return pl.pallas_call(
        body, name="wkv_bwd", grid=(nb,),
        in_specs=[row_spec] * 5 + [col_spec, st_spec, stp_spec, col_spec],
        out_specs=[row_spec] * 5 + [col_spec],
        out_shape=[row_out] * 5 + [jax.ShapeDtypeStruct((t, HEAD_DIM, LANES), F32)],
        scratch_shapes=[pltpu.VMEM((K_HI, HEAD_DIM, LANES), F32)],
        compiler_params=_cparams(dimension_semantics=("arbitrary",)),
    )(r, w, k, kk, b, vx, states, states, dyx)


@jax.custom_vjp
def _rope(x, cs):
    c, s = cs[:, :HEAD_DIM // 2], cs[:, HEAD_DIM // 2:]
    x1, x2 = x[:, :HEAD_DIM // 2], x[:, HEAD_DIM // 2:]
    return jnp.concatenate([x1 * c - x2 * s, x2 * c + x1 * s], axis=1)


def _rope_fwd(x, cs):
    return _rope(x, cs), cs


def _rope_bwd(cs, d):
    c, s = cs[:, :HEAD_DIM // 2], cs[:, HEAD_DIM // 2:]
    d1, d2 = d[:, :HEAD_DIM // 2], d[:, HEAD_DIM // 2:]
    return jnp.concatenate([d1 * c + d2 * s, d2 * c - d1 * s], axis=1), jnp.zeros_like(cs)


_rope.defvjp(_rope_fwd, _rope_bwd)


def _f_qk(x, cs, gain):
    return _rope(_rms(x, gain), cs)


def _attn_block(q, kp, kc, vp, vc, kmin):
    k2 = jnp.concatenate([kp, kc], axis=0)
    v2 = jnp.concatenate([vp, vc], axis=0)
    s = _bdot_nt(q, k2) * (HEAD_DIM ** -0.5)
    qi = lax.broadcasted_iota(jnp.int32, s.shape, 0)
    kj = lax.broadcasted_iota(jnp.int32, s.shape, 1)
    dist = qi + BAND_BLOCK - kj
    valid = (dist >= 0) & (dist <= BAND_BLOCK) & (kj >= kmin)
    s = jnp.where(valid, s, NEG_INF)
    m = lax.stop_gradient(jnp.max(s, axis=-1, keepdims=True))
    e = jnp.exp(s - m)
    l = jnp.sum(e, axis=-1, keepdims=True)
    o = _bdot(e, v2) / l
    return o, m + jnp.log(l)


def _attn_specs(n, nb):
    cur = pl.BlockSpec((None, BAND_BLOCK, HEAD_DIM), lambda i, j: (i, j, 0))
    prev = pl.BlockSpec((None, BAND_BLOCK, HEAD_DIM), lambda i, j: (i, jnp.maximum(j - 1, 0), 0))
    lse = pl.BlockSpec((None, BAND_BLOCK, 1), lambda i, j: (i, j, 0))
    return cur, prev, lse


def _attn_fwd(name, q, k, v):
    n, l, _ = q.shape
    nb = l // BAND_BLOCK
    cur, prev, lse = _attn_specs(n, nb)

    def body(q_ref, kp_ref, kc_ref, vp_ref, vc_ref, o_ref, lse_ref):
        kmin = jnp.where(pl.program_id(1) == 0, BAND_BLOCK, 0)
        o, ls = _attn_block(q_ref[...], kp_ref[...], kc_ref[...], vp_ref[...], vc_ref[...], kmin)
        o_ref[...] = o
        lse_ref[...] = ls

    return pl.pallas_call(
        body, name=name, grid=(n, nb), in_specs=[cur, prev, cur, prev, cur], out_specs=[cur, lse],
        out_shape=[jax.ShapeDtypeStruct((n, l, HEAD_DIM), F32), jax.ShapeDtypeStruct((n, l, 1), F32)],
        compiler_params=_cparams(dimension_semantics=("parallel", "parallel")),
    )(q, k, k, v, v)


def _attn_bwd(name, q, k, v, do, dlse):
    n, l, _ = q.shape
    nb = l // BAND_BLOCK
    cur, prev, lse = _attn_specs(n, nb)

    def body(q_ref, kp_ref, kc_ref, vp_ref, vc_ref, do_ref, dl_ref, dq_ref, dkp_ref, dkc_ref, dvp_ref, dvc_ref):
        kmin = jnp.where(pl.program_id(1) == 0, BAND_BLOCK, 0)
        _, vjp = jax.vjp(functools.partial(_attn_block, kmin=kmin),
                         q_ref[...], kp_ref[...], kc_ref[...], vp_ref[...], vc_ref[...])
        dq, dkp, dkc, dvp, dvc = vjp((do_ref[...], dl_ref[...]))
        dq_ref[...], dkp_ref[...], dkc_ref[...], dvp_ref[...], dvc_ref[...] = dq, dkp, dkc, dvp, dvc

    out = jax.ShapeDtypeStruct((n, l, HEAD_DIM), F32)
    return pl.pallas_call(
        body, name=name, grid=(n, nb), in_specs=[cur, prev, cur, prev, cur, cur, lse], out_specs=[cur] * 5,
        out_shape=[out] * 5,
        compiler_params=_cparams(dimension_semantics=("parallel", "parallel")),
    )(q, k, k, v, v, do, dlse)


def _attn_join(name, dkp, dkc, dvp, dvc):
    n, l, _ = dkc.shape
    nb = l // BAND_BLOCK
    cur = pl.BlockSpec((None, BAND_BLOCK, HEAD_DIM), lambda i, j: (i, j, 0))
    nxt = pl.BlockSpec((None, BAND_BLOCK, HEAD_DIM), lambda i, j: (i, jnp.minimum(j + 1, nb - 1), 0))

    def body(dkp_ref, dkc_ref, dvp_ref, dvc_ref, dk_ref, dv_ref):
        keep = jnp.where(pl.program_id(1) == nb - 1, 0.0, 1.0)
        dk_ref[...] = dkc_ref[...] + dkp_ref[...] * keep
        dv_ref[...] = dvc_ref[...] + dvp_ref[...] * keep

    out = jax.ShapeDtypeStruct((n, l, HEAD_DIM), F32)
    return pl.pallas_call(
        body, name=name, grid=(n, nb), in_specs=[nxt, cur, nxt, cur], out_specs=[cur, cur], out_shape=[out, out],
        compiler_params=_cparams(dimension_semantics=("parallel", "parallel")),
    )(dkp, dkc, dvp, dvc)


def _fold(x, dil):
    t, _, c = x.shape
    return x.reshape(t // dil, dil, HEADS_PER_GROUP, c).transpose(1, 2, 0, 3).reshape(
        dil * HEADS_PER_GROUP, t // dil, c)


def _unfold(x, dil):
    n, l, c = x.shape
    return x.reshape(dil, HEADS_PER_GROUP, l, c).transpose(2, 0, 1, 3).reshape(l * dil, HEADS_PER_GROUP, c)


def _f_comb(o1, o2, o3, l1, l2, l3):
    m = jnp.maximum(jnp.maximum(l1, l2), l3)
    e1, e2, e3 = jnp.exp(l1 - m), jnp.exp(l2 - m), jnp.exp(l3 - m)
    den = e1 + e2 + e3
    return (e1 / den) * o1 + (e2 / den) * o2 + (e3 / den) * o3


def _all_gather_hbm(x):
    rws, cols = x.shape

    def body(x_ref, out_ref, send_sems, recv_sems, local_sem):
        mx, my, mc = lax.axis_index("x"), lax.axis_index("y"), lax.axis_index("c")
        me, sibling = (mx, my, mc), (mx, my, 1 - mc)
        chips = [(1 - mx, my), (mx, 1 - my), (1 - mx, 1 - my)]

        def slot(px, py, pc):
            return out_ref.at[4 * px + 2 * py + pc]

        def copy(k, block, to, src=None):
            return pltpu.make_async_remote_copy(
                src_ref=slot(*block) if src is None else src, dst_ref=slot(*block),
                send_sem=send_sems.at[k], recv_sem=recv_sems.at[k], device_id=to, device_id_type=MESH)

        mine = pltpu.make_async_copy(x_ref, slot(*me), local_sem)
        mine.start()
        first = [copy(0, me, sibling, src=x_ref)]
        first += [copy(1 + j, me, (*chip, mc), src=x_ref) for j, chip in enumerate(chips)]
        for cp in first:
            cp.start()
        passed = [copy(4 + j, (*chip, mc), sibling) for j, chip in enumerate(chips)]
        for j, chip in enumerate(chips):
            copy(1 + j, (*chip, mc), me).wait_recv()
            passed[j].start()
        copy(0, sibling, me).wait_recv()
        for j, chip in enumerate(chips):
            copy(4 + j, (*chip, 1 - mc), me).wait_recv()
        for cp in first + passed:
            cp.wait_send()
        mine.wait()

    return pl.pallas_call(
        body, name="all_gather_weights",
        out_shape=jax.ShapeDtypeStruct((N_DEV, rws, cols), x.dtype),
        in_specs=[pl.BlockSpec(memory_space=pl.ANY)], out_specs=pl.BlockSpec(memory_space=pl.ANY),
        scratch_shapes=[pltpu.SemaphoreType.DMA((7,)), pltpu.SemaphoreType.DMA((7,)), pltpu.SemaphoreType.DMA],
    )(x)


def _all_gather_vmem(x):
    rws, cols = x.shape

    def body(x_ref, out_ref, send_sems, recv_sems):
        mx, my, mc = lax.axis_index("x"), lax.axis_index("y"), lax.axis_index("c")
        me, sibling = (mx, my, mc), (mx, my, 1 - mc)
        chips = [(1 - mx, my), (mx, 1 - my), (1 - mx, 1 - my)]

        def slot(px, py, pc):
            return out_ref.at[4 * px + 2 * py + pc]

        def copy(k, block, to, src=None):
            return pltpu.make_async_remote_copy(
                src_ref=slot(*block) if src is None else src, dst_ref=slot(*block),
                send_sem=send_sems.at[k], recv_sem=recv_sems.at[k], device_id=to, device_id_type=MESH)

        first = [copy(0, me, sibling, src=x_ref)]
        first += [copy(1 + j, me, (*chip, mc), src=x_ref) for j, chip in enumerate(chips)]
        for cp in first:
            cp.start()
        out_ref[4 * mx + 2 * my + mc] = x_ref[...]
        passed = [copy(4 + j, (*chip, mc), sibling) for j, chip in enumerate(chips)]
        for j, chip in enumerate(chips):
            copy(1 + j, (*chip, mc), me).wait_recv()
            passed[j].start()
        copy(0, sibling, me).wait_recv()
        for j, chip in enumerate(chips):
            copy(4 + j, (*chip, 1 - mc), me).wait_recv()
        for cp in first + passed:
            cp.wait_send()

    return pl.pallas_call(
        body, name="all_gather_small",
        out_shape=jax.ShapeDtypeStruct((N_DEV, rws, cols), x.dtype),
        in_specs=[pl.BlockSpec(memory_space=pltpu.VMEM)], out_specs=pl.BlockSpec(memory_space=pltpu.VMEM),
        scratch_shapes=[pltpu.SemaphoreType.DMA((7,)), pltpu.SemaphoreType.DMA((7,))],
    )(x)


def _all_to_all_hbm(g):
    _, rws, cols = g.shape

    def body(g_ref, out_ref, send_sems, recv_sems, local_sem):
        mx, my, mc = lax.axis_index("x"), lax.axis_index("y"), lax.axis_index("c")
        me = 4 * mx + 2 * my + mc
        mine = pltpu.make_async_copy(g_ref.at[me], out_ref.at[me], local_sem)
        mine.start()
        copies = []
        for k in range(1, N_DEV):
            px, py, pc = mx ^ (k >> 2), my ^ ((k >> 1) & 1), mc ^ (k & 1)
            peer = 4 * px + 2 * py + pc
            copies.append(pltpu.make_async_remote_copy(
                src_ref=g_ref.at[peer], dst_ref=out_ref.at[me], send_sem=send_sems.at[k - 1],
                recv_sem=recv_sems.at[k - 1], device_id=(px, py, pc), device_id_type=MESH))
        for cp in copies:
            cp.start()
        for cp in copies:
            cp.wait_recv()
        for cp in copies:
            cp.wait_send()
        mine.wait()

    return pl.pallas_call(
        body, name="all_to_all_grads",
        out_shape=jax.ShapeDtypeStruct(g.shape, g.dtype),
        in_specs=[pl.BlockSpec(memory_space=pl.ANY)], out_specs=pl.BlockSpec(memory_space=pl.ANY),
        scratch_shapes=[pltpu.SemaphoreType.DMA((7,)), pltpu.SemaphoreType.DMA((7,)), pltpu.SemaphoreType.DMA],
    )(g)


def _sum_slots(name, g, tr):
    _, rws, cols = g.shape

    def body(g_ref, o_ref):
        acc = g_ref[0].astype(F32)
        for j in range(1, N_DEV):
            acc = acc + g_ref[j].astype(F32)
        o_ref[...] = acc

    return pl.pallas_call(
        body, name=name, grid=(rws // tr,),
        in_specs=[pl.BlockSpec((N_DEV, tr, cols), lambda i: (0, i, 0))],
        out_specs=pl.BlockSpec((tr, cols), lambda i: (i, 0)),
        out_shape=jax.ShapeDtypeStruct((rws, cols), F32),
        compiler_params=_cparams(dimension_semantics=("parallel",)),
    )(g)


def _adamw(name, w, g, m, v, tr):
    def fn(wv, gv, mv, vv):
        mn = ADAM_B1 * mv + (1.0 - ADAM_B1) * gv
        vn = ADAM_B2 * vv + (1.0 - ADAM_B2) * jnp.square(gv)
        m_hat = mn / (1.0 - ADAM_B1 ** ADAM_STEP)
        v_hat = vn / (1.0 - ADAM_B2 ** ADAM_STEP)
        delta = -ADAM_LR * (m_hat / (jnp.sqrt(v_hat) + ADAM_EPS) + ADAM_WD * wv)
        return delta, mn, vn
    return _rowwise(name, fn, [w, g, m, v], [], [(LANES, F32)] * 3, tr=tr)


def _pack_local(arrs):
    return jnp.concatenate([a.reshape(-1, LANES) for a in arrs], axis=0)


def _shard_rows(rows_full, cols_full):
    return rows_full * cols_full // N_DEV // LANES


def _unpack_gathered(wall):
    out, off = {}, 0
    for name, rf, cf, axis in SHARDED:
        n = _shard_rows(rf, cf)
        piece = wall[:, off:off + n, :]
        if axis == 1:
            out[name] = piece.reshape(N_DEV, rf, cf // N_DEV).transpose(1, 0, 2).reshape(rf, cf)
        else:
            out[name] = piece.reshape(rf, cf)
        off += n
    return out


def _pack_grads(grads):
    pieces = []
    for name, rf, cf, axis in SHARDED:
        g = grads[name]
        if axis == 1:
            g = g.reshape(rf, N_DEV, cf // N_DEV).transpose(1, 0, 2)
        pieces.append(g.reshape(N_DEV, -1, LANES))
    return jnp.concatenate(pieces, axis=1)


def _unpack_local(flat, like):
    out, off = [], 0
    for a in like:
        n = a.size // LANES
        out.append(flat[off:off + n].reshape(a.shape))
        off += n
    return out


def _pack_small(arrs, rows):
    flat = jnp.concatenate([a.reshape(-1) for a in arrs])
    return jnp.pad(flat, (0, rows * LANES - flat.shape[0])).reshape(rows, LANES)


def _unpack_small(flat, like):
    flat = flat.reshape(-1)
    out, off = [], 0
    for a in like:
        out.append(flat[off:off + a.size].reshape(a.shape))
        off += a.size
    return out


def _local_step(x, p, pos, target, sm, wf):
    t = x.shape[0]
    hv = t * RWKV_HEADS

    w_gu1 = jnp.concatenate([wf["ffn1_w_gate"], wf["ffn1_w_up"]], axis=1)
    w_gu2 = jnp.concatenate([wf["ffn2_w_gate"], wf["ffn2_w_up"]], axis=1)
    x1, ffn1_saved = _ffn_fwd("ffn1", x, sm["ffn1_norm"], w_gu1, wf["ffn1_w_down"])
    h2 = _norm_fwd("mix_norm", x1, sm["mix_norm"])
    z = _mm("w_in", h2, wf["w_in"], "nn")
    z_r = z[:, :RWKV_COLS]
    q_raw = z[:, RWKV_COLS:RWKV_COLS + ATTN_DIM].reshape(t * ATTN_HEADS, HEAD_DIM)
    k_raw = z[:, RWKV_COLS + ATTN_DIM:RWKV_COLS + 2 * ATTN_DIM].reshape(t * ATTN_HEADS, HEAD_DIM)
    v_att = z[:, RWKV_COLS + 2 * ATTN_DIM:RWKV_COLS + 3 * ATTN_DIM].reshape(t, ATTN_HEADS, HEAD_DIM)
    z_g = z[:, RWKV_COLS + 3 * ATTN_DIM:]

    r, k, v, lo, gd = _shift_fwd(z_r, sm["rwkv_mu"])
    zero_lo = jnp.zeros((DECAY_LORA, RWKV_DIM), BF16)
    w2p = jnp.concatenate([wf["rwkv_w2"], zero_lo], axis=0).astype(F32)
    a2p = jnp.concatenate([zero_lo, wf["rwkv_a2"]], axis=0).astype(F32)
    pre_params = [sm["rwkv_w0"], w2p, sm["rwkv_a0"], a2p, wf["rwkv_g2"].astype(F32), sm["rwkv_k_k"],
                  sm["rwkv_k_a"]]
    wide = [(RWKV_DIM, F32)]
    k2, kk, a, decay, g = _rowwise("rwkv_pre", _f_pre, [k, lo, gd], pre_params, wide * 5)
    as_heads = lambda u: u.reshape(hv, HEAD_DIM)
    kkn, b = _rowwise("rwkv_kk", _f_kk, [as_heads(kk), as_heads(a)], [], [(HEAD_DIM, F32)] * 2, tr=512)
    scan_in = [_to_scan_rows(u) for u in (r, decay, k2, kkn.reshape(t, RWKV_DIM), b.reshape(t, RWKV_DIM))]
    vx = _to_scan_cols(v)
    yx, states = _wkv_fwd(*scan_in, vx)
    y = _from_scan_cols(yx)
    post_params = [sm["rwkv_gn_w"].reshape(RWKV_HEADS, HEAD_DIM), sm["rwkv_gn_b"].reshape(RWKV_HEADS, HEAD_DIM),
                   sm["rwkv_r_k"].reshape(RWKV_HEADS, HEAD_DIM)]
    post_rows = [as_heads(y), as_heads(r), as_heads(k2), as_heads(v), as_heads(g)]
    y_rwkv = _rowwise("rwkv_post", lambda *av: (_f_post(*av),), post_rows, post_params, [(HEAD_DIM, F32)], tr=512)[0]
    y_rwkv = y_rwkv.reshape(t, RWKV_DIM)

    inv_freq = 1.0 / (ROPE_THETA ** (jnp.arange(0, HEAD_DIM, 2, dtype=F32) / HEAD_DIM))

    def rope_table(posv, fr):
        ang = posv * fr
        return (jnp.concatenate([jnp.cos(ang), jnp.sin(ang)], axis=1),)
    cs = _rowwise("rope_table", rope_table, [pos.astype(F32).reshape(t, 1)], [inv_freq.reshape(1, HEAD_DIM // 2)],
                  [(HEAD_DIM, F32)])[0]
    cs_h = jnp.repeat(cs, ATTN_HEADS, axis=0)
    qn = _rowwise("q_rope", lambda xv, cv, gv: (_f_qk(xv, cv, gv),), [q_raw, cs_h], [sm["q_norm"]],
                  [(HEAD_DIM, F32)], tr=384)[0].reshape(t, ATTN_HEADS, HEAD_DIM)
    kn = _rowwise("k_rope", lambda xv, cv, gv: (_f_qk(xv, cv, gv),), [k_raw, cs_h], [sm["k_norm"]],
                  [(HEAD_DIM, F32)], tr=384)[0].reshape(t, ATTN_HEADS, HEAD_DIM)
    folded, outs, lses = [], [], []
    for gi, (_, dil) in enumerate(ATTN_GROUPS):
        hs = slice(gi * HEADS_PER_GROUP, (gi + 1) * HEADS_PER_GROUP)
        qf, kf, vf = _fold(qn[:, hs], dil), _fold(kn[:, hs], dil), _fold(v_att[:, hs], dil)
        o, lse = _attn_fwd("attn_fwd_%d" % gi, qf, kf, vf)
        folded.append((qf, kf, vf))
        outs.append(_unfold(o, dil).reshape(t * HEADS_PER_GROUP, HEAD_DIM))
        lses.append(_unfold(lse, dil).reshape(t * HEADS_PER_GROUP, 1))
    y_attn = _rowwise("attn_comb", lambda *av: (_f_comb(*av),), outs + lses, [], [(HEAD_DIM, F32)], tr=512)[0]
    y_attn = y_attn.reshape(t, HEADS_PER_GROUP * HEAD_DIM)

    u_r = _mm("br_rwkv", y_rwkv, wf["w_br_rwkv"], "nn")
    u_a = _mm("br_attn", y_attn, wf["w_br_attn"], "nn")

    def f_merge(zgr, zga, ur, ua):
        return _sigmoid(zgr) * ur + _sigmoid(zga) * ua
    merged = _rowwise("merge", lambda zg, ur, ua: (f_merge(zg[:, :D_MODEL], zg[:, D_MODEL:], ur, ua),),
                      [z_g, u_r, u_a], [], [(D_MODEL, BF16)])[0]
    x2 = _mm("w_out", merged, wf["w_out"], "nn", res=x1)
    x3, ffn2_saved = _ffn_fwd("ffn2", x2, sm["ffn2_norm"], w_gu2, wf["ffn2_w_down"])

    hn = _norm_fwd("ple_norm", x3, sm["ple_norm"])
    gz = _mm("ple_gate", hn, wf["ple_w_gate"], "nn")
    pp = _mm("ple_proj", p, wf["ple_w_proj"], "nn")

    def f_head(x3v, gzv, ppv, tg):
        sg = _sigmoid(gzv)
        err = x3v + sg * ppv - tg
        part = 0.5 * jnp.sum(jnp.mean(err * err, axis=-1, keepdims=True))
        dx4 = err * (1.0 / D_MODEL)
        return dx4, dx4 * ppv * sg * (1.0 - sg), dx4 * sg, jnp.full((1, LANES), part, F32)
    dx4, dgz, dpp, loss_row = _rowwise("ple_loss", f_head, [x3, gz, pp, target], [],
                                       [(D_MODEL, F32), (D_MODEL, BF16), (D_MODEL, BF16)], [(1, LANES)])
    loss = loss_row[0, 0]

    gs, gm = {}, {}
    dhn = _mm("ple_dhn", dgz, wf["ple_w_gate"], "nt")
    gm["ple_w_gate"] = _mm("ple_dwgate", hn, dgz, "tn")
    gm["ple_w_proj"] = _mm("ple_dwproj", p, dpp, "tn")
    dx3, gs["ple_norm"] = _norm_bwd("ple_dnorm", x3, sm["ple_norm"], dhn, dx4)

    dx2, gs["ffn2_norm"], gm["ffn2_w_gate"], gm["ffn2_w_up"], gm["ffn2_w_down"] = _ffn_bwd(
        "ffn2", x2, sm["ffn2_norm"], w_gu2, wf["ffn2_w_down"], ffn2_saved, dx3)

    dmerged = _mm("w_out_dmerged", dx2, wf["w_out"], "nt")
    gm["w_out"] = _mm("w_out_dw", merged, dx2, "tn")

    def merge_bwd(zg, ur, ua, dm):
        _, vjp = jax.vjp(f_merge, zg[:, :D_MODEL], zg[:, D_MODEL:], ur, ua)
        dzr, dza, dur, dua = vjp(dm)
        return jnp.concatenate([dzr, dza], axis=1), dur, dua
    dz_g, du_r, du_a = _rowwise("merge_bwd", merge_bwd, [z_g, u_r, u_a, dmerged], [],
                                [(2 * D_MODEL, BF16), (D_MODEL, BF16), (D_MODEL, BF16)])
    dy_rwkv = _mm("br_rwkv_dy", du_r, wf["w_br_rwkv"], "nt")
    gm["w_br_rwkv"] = _mm("br_rwkv_dw", y_rwkv, du_r, "tn")
    dy_attn = _mm("br_attn_dy", du_a, wf["w_br_attn"], "nt")
    gm["w_br_attn"] = _mm("br_attn_dw", y_attn, du_a, "tn")

    def comb_bwd(*av):
        _, vjp = jax.vjp(_f_comb, *av[:6])
        return vjp(av[6])
    comb_out = _rowwise("attn_comb_bwd", comb_bwd, outs + lses + [dy_attn.reshape(t * HEADS_PER_GROUP, HEAD_DIM)],
                        [], [(HEAD_DIM, F32)] * 3 + [(1, F32)] * 3, tr=512)
    dq_parts, dk_parts, dv_parts = [], [], []
    for gi, (_, dil) in enumerate(ATTN_GROUPS):
        qf, kf, vf = folded[gi]
        dof = _fold(comb_out[gi].reshape(t, HEADS_PER_GROUP, HEAD_DIM), dil)
        dlf = _fold(comb_out[3 + gi].reshape(t, HEADS_PER_GROUP, 1), dil)
        dqf, dkp, dkc, dvp, dvc = _attn_bwd("attn_bwd_%d" % gi, qf, kf, vf, dof, dlf)
        dkf, dvf = _attn_join("attn_join_%d" % gi, dkp, dkc, dvp, dvc)
        dq_parts.append(_unfold(dqf, dil))
        dk_parts.append(_unfold(dkf, dil))
        dv_parts.append(_unfold(dvf, dil))
    dqn = jnp.concatenate(dq_parts, axis=1).reshape(t * ATTN_HEADS, HEAD_DIM)
    dkn = jnp.concatenate(dk_parts, axis=1).reshape(t * ATTN_HEADS, HEAD_DIM)
    dv_att = jnp.concatenate(dv_parts, axis=1).reshape(t, ATTN_DIM)

    def qk_bwd(xv, cv, dv_, gv):
        _, vjp = jax.vjp(lambda xx, gg: _f_qk(xx, cv, gg), xv, gv)
        return vjp(dv_)
    dq_raw, gs["q_norm"] = _rowwise("q_rope_bwd", qk_bwd, [q_raw, cs_h, dqn], [sm["q_norm"]],
                                    [(HEAD_DIM, BF16)], [(1, HEAD_DIM)], tr=384)
    dk_raw, gs["k_norm"] = _rowwise("k_rope_bwd", qk_bwd, [k_raw, cs_h, dkn], [sm["k_norm"]],
                                    [(HEAD_DIM, BF16)], [(1, HEAD_DIM)], tr=384)

    def post_bwd(yv, rv, k2v, vv, gv, dv_, gnw, gnb, rk):
        _, vjp = jax.vjp(_f_post, yv, rv, k2v, vv, gv, gnw, gnb, rk)
        return vjp(dv_)
    head_acc = (RWKV_HEADS, HEAD_DIM)
    dy, dr1, dk2a, dv1, dg, d_gnw, d_gnb, d_rk = _rowwise(
        "rwkv_post_bwd", post_bwd, post_rows + [as_heads(dy_rwkv)], post_params, [(HEAD_DIM, F32)] * 5,
        [head_acc] * 3, tr=512)
    gs["rwkv_gn_w"], gs["rwkv_gn_b"], gs["rwkv_r_k"] = d_gnw, d_gnb, d_rk
    dyx = _to_scan_cols(dy.reshape(t, RWKV_DIM))
    dr2, ddecay, dk2b, dkkn, db, dvx = _wkv_bwd(*scan_in, vx, states, dyx)
    dr2, ddecay, dk2b, dkkn, db = [_from_scan_rows(u) for u in (dr2, ddecay, dk2b, dkkn, db)]
    dv2 = _from_scan_cols(dvx)

    def kk_bwd(kkv, av, dkknv, dbv):
        _, vjp = jax.vjp(_f_kk, kkv, av)
        return vjp((dkknv, dbv))
    dkk, da = _rowwise("rwkv_kk_bwd", kk_bwd, [as_heads(kk), as_heads(a), as_heads(dkkn), as_heads(db)], [],
                       [(HEAD_DIM, F32)] * 2, tr=512)

    def pre_bwd(kv, lov, gdv, dk2x, dk2y, dkkv, dav, ddec, dgv, w0, w2p_, a0, a2p_, g2, k_k, k_a):
        _, vjp = jax.vjp(_f_pre, kv, lov, gdv, w0, w2p_, a0, a2p_, g2, k_k, k_a)
        return vjp((dk2x + dk2y, dkkv, dav, ddec, dgv))
    lora_acc = (DECAY_LORA + ICLR_LORA, RWKV_DIM)
    dk, dlo, dgd, d_w0, d_w2p, d_a0, d_a2p, d_g2, d_kk, d_ka = _rowwise(
        "rwkv_pre_bwd", pre_bwd,
        [k, lo, gd, dk2a.reshape(t, RWKV_DIM), dk2b, dkk.reshape(t, RWKV_DIM), da.reshape(t, RWKV_DIM), ddecay,
         dg.reshape(t, RWKV_DIM)],
        pre_params, [(RWKV_DIM, F32), (LANES, F32), (LANES, F32)],
        [(1, RWKV_DIM), lora_acc, (1, RWKV_DIM), lora_acc, (GATE_LORA, RWKV_DIM), (1, RWKV_DIM), (1, RWKV_DIM)])
    gs["rwkv_w0"], gs["rwkv_a0"], gs["rwkv_k_k"], gs["rwkv_k_a"] = d_w0, d_a0, d_kk, d_ka
    gm["rwkv_w2"], gm["rwkv_a2"], gm["rwkv_g2"] = d_w2p[:DECAY_LORA], d_a2p[DECAY_LORA:], d_g2
    add2 = lambda u, w_: _rowwise("rwkv_add", lambda p_, q_: (p_ + q_,), [u, w_], [], [(RWKV_DIM, F32)])[0]
    dz_r, gs["rwkv_mu"] = _shift_bwd(z_r, sm["rwkv_mu"], add2(dr1.reshape(t, RWKV_DIM), dr2), dk,
                                     add2(dv1.reshape(t, RWKV_DIM), dv2), dlo, dgd)

    dz = jnp.concatenate([dz_r, dq_raw.reshape(t, ATTN_DIM), dk_raw.reshape(t, ATTN_DIM), dv_att.astype(BF16), dz_g],
                         axis=1)
    dh2 = _mm("w_in_dh", dz, wf["w_in"], "nt")
    gm["w_in"] = _mm("w_in_dw", h2, dz, "tn")
    dx1, gs["mix_norm"] = _norm_bwd("mix_dnorm", x1, sm["mix_norm"], dh2, dx2)

    dx0, gs["ffn1_norm"], gm["ffn1_w_gate"], gm["ffn1_w_up"], gm["ffn1_w_down"] = _ffn_bwd(
        "ffn1", x, sm["ffn1_norm"], w_gu1, wf["ffn1_w_down"], ffn1_saved, dx1)
    return loss, dx0, gm, gs


def kernel(x, p, positions, ffn1_norm, ffn1_w_gate, ffn1_w_up, ffn1_w_down, mix_norm, w_in, rwkv_mu, rwkv_w0, rwkv_w2, rwkv_a0, rwkv_a2, rwkv_g2, rwkv_k_k, rwkv_k_a, rwkv_r_k, rwkv_gn_w, rwkv_gn_b, q_norm, k_norm, w_br_rwkv, w_br_attn, w_out, ffn2_norm, ffn2_w_gate, ffn2_w_up, ffn2_w_down, ple_norm, ple_w_gate, ple_w_proj, loss_target, m_ffn1_norm, m_ffn1_w_gate, m_ffn1_w_up, m_ffn1_w_down, m_mix_norm, m_w_in, m_rwkv_mu, m_rwkv_w0, m_rwkv_w2, m_rwkv_a0, m_rwkv_a2, m_rwkv_g2, m_rwkv_k_k, m_rwkv_k_a, m_rwkv_r_k, m_rwkv_gn_w, m_rwkv_gn_b, m_q_norm, m_k_norm, m_w_br_rwkv, m_w_br_attn, m_w_out, m_ffn2_norm, m_ffn2_w_gate, m_ffn2_w_up, m_ffn2_w_down, m_ple_norm, m_ple_w_gate, m_ple_w_proj, v_ffn1_norm, v_ffn1_w_gate, v_ffn1_w_up, v_ffn1_w_down, v_mix_norm, v_w_in, v_rwkv_mu, v_rwkv_w0, v_rwkv_w2, v_rwkv_a0, v_rwkv_a2, v_rwkv_g2, v_rwkv_k_k, v_rwkv_k_a, v_rwkv_r_k, v_rwkv_gn_w, v_rwkv_gn_b, v_q_norm, v_k_norm, v_w_br_rwkv, v_w_br_attn, v_w_out, v_ffn2_norm, v_ffn2_w_gate, v_ffn2_w_up, v_ffn2_w_down, v_ple_norm, v_ple_w_gate, v_ple_w_proj):
    args = locals()
    w = {n: args[n] for n in WEIGHTS}
    m = {n: args["m_" + n] for n in WEIGHTS}
    v = {n: args["v_" + n] for n in WEIGHTS}
    sharded_names = [s[0] for s in SHARDED]

    w_local = _pack_local([w[n][0] for n in sharded_names])
    wf = _unpack_gathered(_all_gather_hbm(w_local.astype(BF16)))

    sm = {n: w[n][0].reshape(1, -1) for n in SMALL}
    loss_part, dx, gm, gs = _local_step(x[0], p[0, 0], positions[0], loss_target[0], sm, wf)
    loss = lax.psum(loss_part, ("x", "y", "c"))

    rows_local = w_local.shape[0]
    recv = _all_to_all_hbm(_pack_grads(gm).astype(BF16))
    g_local = _sum_slots("sum_grads", recv, 640)
    small_like = [w[n][0] for n in SMALL]
    small_rows = 80
    gs_all = _all_gather_vmem(_pack_small([gs[n] for n in SMALL], small_rows))
    gs_sum = _sum_slots("sum_small_grads", gs_all, small_rows)

    m_local = _pack_local([m[n][0] for n in sharded_names])
    v_local = _pack_local([v[n][0] for n in sharded_names])
    d_l, m_l, v_l = _adamw("adamw", w_local, g_local, m_local, v_local, 640)
    d_s, m_s, v_s = _adamw("adamw_small", _pack_small(small_like, small_rows), gs_sum,
                           _pack_small([m[n][0] for n in SMALL], small_rows),
                           _pack_small([v[n][0] for n in SMALL], small_rows), small_rows)

    shard_like = [w[n][0] for n in sharded_names]
    res = {}
    for tag, big, small in (("grad", g_local, gs_sum), ("delta", d_l, d_s), ("new_m", m_l, m_s), ("new_v", v_l, v_s)):
        for n, a in zip(sharded_names, _unpack_local(big, shard_like)):
            res[tag, n] = a[None]
        for n, a in zip(SMALL, _unpack_small(small, small_like)):
            res[tag, n] = a[None]
    outs = [loss, dx[None]]
    for tag in ("grad", "delta", "new_m", "new_v"):
        outs += [res[tag, n] for n in WEIGHTS]
    return tuple(outs)
```

```python
import functools

import jax
import jax.numpy as jnp
from jax import lax
from jax.experimental import pallas as pl
from jax.experimental.pallas import tpu as pltpu

F32, BF16 = jnp.float32, jnp.bfloat16
MESH = pl.DeviceIdType.MESH
N_DEV = 8
LANES = 128
VMEM_LIMIT = 56 * 1024 * 1024

D_MODEL = 1024
PLE_DIM = 256
HEAD_DIM = 64
RWKV_HEADS = 8
RWKV_DIM = RWKV_HEADS * HEAD_DIM
DECAY_LORA = 64
ICLR_LORA = 64
GATE_LORA = 128
GN_EPS = 64e-5
ATTN_GROUPS = ((128, 1), (512, 4), (2048, 16))
HEADS_PER_GROUP = 4
ATTN_HEADS = HEADS_PER_GROUP * len(ATTN_GROUPS)
ATTN_DIM = ATTN_HEADS * HEAD_DIM
BAND_BLOCK = 128
ROPE_THETA = 10000.0
NEG_INF = -1e30
D_FF = 2816
RMS_EPS = 1e-6
RWKV_COLS = 3 * RWKV_DIM + DECAY_LORA + ICLR_LORA + GATE_LORA
ADAM_LR = 0.001
ADAM_B1 = 0.9
ADAM_B2 = 0.999
ADAM_EPS = 1e-08
ADAM_WD = 0.01
ADAM_STEP = 10

V_LO = LANES // RWKV_HEADS
V_HI = HEAD_DIM // V_LO
SCAN_CHUNK = 32

SHARDED = (
    ("ffn1_w_gate", D_MODEL, D_FF, 1), ("ffn1_w_up", D_MODEL, D_FF, 1), ("ffn1_w_down", D_FF, D_MODEL, 0),
    ("w_in", D_MODEL, 6144, 1),
    ("rwkv_w2", DECAY_LORA, RWKV_DIM, 1), ("rwkv_a2", ICLR_LORA, RWKV_DIM, 1), ("rwkv_g2", GATE_LORA, RWKV_DIM, 1),
    ("w_br_rwkv", RWKV_DIM, D_MODEL, 1), ("w_br_attn", 256, D_MODEL, 1), ("w_out", D_MODEL, D_MODEL, 0),
    ("ffn2_w_gate", D_MODEL, D_FF, 1), ("ffn2_w_up", D_MODEL, D_FF, 1), ("ffn2_w_down", D_FF, D_MODEL, 0),
    ("ple_w_gate", D_MODEL, D_MODEL, 0), ("ple_w_proj", PLE_DIM, D_MODEL, 1),
)
SMALL = ("ffn1_norm", "mix_norm", "rwkv_mu", "rwkv_w0", "rwkv_a0", "rwkv_k_k", "rwkv_k_a", "rwkv_r_k",
         "rwkv_gn_w", "rwkv_gn_b", "q_norm", "k_norm", "ffn2_norm", "ple_norm")
WEIGHTS = ("ffn1_norm", "ffn1_w_gate", "ffn1_w_up", "ffn1_w_down", "mix_norm", "w_in", "rwkv_mu", "rwkv_w0",
           "rwkv_w2", "rwkv_a0", "rwkv_a2", "rwkv_g2", "rwkv_k_k", "rwkv_k_a", "rwkv_r_k", "rwkv_gn_w",
           "rwkv_gn_b", "q_norm", "k_norm", "w_br_rwkv", "w_br_attn", "w_out", "ffn2_norm", "ffn2_w_gate",
           "ffn2_w_up", "ffn2_w_down", "ple_norm", "ple_w_gate", "ple_w_proj")


def _cparams(**kw):
    return pltpu.CompilerParams(vmem_limit_bytes=VMEM_LIMIT, **kw)


def _tile(n, cap):
    best = None
    for t in range(LANES, min(n, cap) + 1, LANES):
        if n % t == 0:
            best = t
    return best if best is not None else n


@jax.custom_vjp
def _bdot(a, w):
    return jnp.dot(a.astype(BF16), w.astype(BF16), preferred_element_type=F32)


def _bdot_fwd(a, w):
    return _bdot(a, w), (a, w)


def _bdot_bwd(res, g):
    a, w = res
    gb = g.astype(BF16)
    da = lax.dot_general(gb, w.astype(BF16), (((1,), (1,)), ((), ())), preferred_element_type=F32)
    dw = lax.dot_general(a.astype(BF16), gb, (((0,), (0,)), ((), ())), preferred_element_type=F32)
    return da.astype(a.dtype), dw.astype(w.dtype)


_bdot.defvjp(_bdot_fwd, _bdot_bwd)


@jax.custom_vjp
def _bdot_nt(a, b):
    return lax.dot_general(a.astype(BF16), b.astype(BF16), (((1,), (1,)), ((), ())), preferred_element_type=F32)


def _bdot_nt_fwd(a, b):
    return _bdot_nt(a, b), (a, b)


def _bdot_nt_bwd(res, g):
    a, b = res
    gb = g.astype(BF16)
    da = jnp.dot(gb, b.astype(BF16), preferred_element_type=F32)
    db = lax.dot_general(gb, a.astype(BF16), (((0,), (0,)), ((), ())), preferred_element_type=F32)
    return da.astype(a.dtype), db.astype(b.dtype)


_bdot_nt.defvjp(_bdot_nt_fwd, _bdot_nt_bwd)


def _mm(name, a, b, mode, out_dtype=F32, res=None, scale=None):
    if mode == "nn":
        (m, k), n = a.shape, b.shape[1]
    elif mode == "nt":
        (m, k), n = a.shape, b.shape[0]
    else:
        (k, m), n = a.shape, b.shape[1]
    tm, tn = _tile(m, 512), _tile(n, 512)
    a_spec = pl.BlockSpec((k, tm), lambda i, j: (0, i)) if mode == "tn" else pl.BlockSpec((tm, k), lambda i, j: (i, 0))
    b_spec = pl.BlockSpec((tn, k), lambda i, j: (j, 0)) if mode == "nt" else pl.BlockSpec((k, tn), lambda i, j: (0, j))
    dims = {"nn": ((1,), (0,)), "nt": ((1,), (1,)), "tn": ((0,), (0,))}[mode]
    o_spec = pl.BlockSpec((tm, tn), lambda i, j: (i, j))
    ins, in_specs = [a, b], [a_spec, b_spec]
    if res is not None:
        ins.append(res)
        in_specs.append(o_spec)

    def body(*refs):
        acc = lax.dot_general(refs[0][...].astype(BF16), refs[1][...].astype(BF16), (dims, ((), ())),
                              preferred_element_type=F32)
        if scale is not None:
            acc = acc * scale
        if res is not None:
            acc = acc + refs[2][...].astype(F32)
        refs[-1][...] = acc.astype(refs[-1].dtype)

    return pl.pallas_call(
        body, name=name, grid=(m // tm, n // tn), in_specs=in_specs, out_specs=o_spec,
        out_shape=jax.ShapeDtypeStruct((m, n), out_dtype),
        compiler_params=_cparams(dimension_semantics=("parallel", "parallel")),
    )(*ins)


def _rowwise(name, fn, rows, params, out_rows, out_accs=(), tr=256):
    r = rows[0].shape[0]
    in_specs = [pl.BlockSpec((tr, a.shape[1]), lambda i: (i, 0)) for a in rows]
    in_specs += [pl.BlockSpec(p.shape, lambda i, nd=p.ndim: (0,) * nd) for p in params]
    out_shape = [jax.ShapeDtypeStruct((r, c), dt) for c, dt in out_rows]
    out_shape += [jax.ShapeDtypeStruct(s, F32) for s in out_accs]
    out_specs = [pl.BlockSpec((tr, c), lambda i: (i, 0)) for c, _ in out_rows]
    out_specs += [pl.BlockSpec(s, lambda i, nd=len(s): (0,) * nd) for s in out_accs]
    n_in, n_ro = len(rows) + len(params), len(out_rows)

    def body(*refs):
        res = fn(*[ref[...] for ref in refs[:n_in]])
        outs = refs[n_in:]
        for o, v in zip(outs[:n_ro], res[:n_ro]):
            o[...] = v.astype(o.dtype)
        for o, v in zip(outs[n_ro:], res[n_ro:]):
            _accumulate(o, v)

    return pl.pallas_call(
        body, name=name, grid=(r // tr,), in_specs=in_specs, out_specs=out_specs, out_shape=out_shape,
        compiler_params=_cparams(dimension_semantics=("arbitrary",)),
    )(*rows, *params)


def _accumulate(o_ref, v):
    @pl.when(pl.program_id(0) == 0)
    def _():
        o_ref[...] = v

    @pl.when(pl.program_id(0) != 0)
    def _():
        o_ref[...] += v


def _rms(x, g):
    return x * lax.rsqrt(jnp.mean(x * x, axis=-1, keepdims=True) + RMS_EPS) * g


def _sigmoid(x):
    return jax.nn.sigmoid(x)


def _softplus(x):
    return jnp.maximum(x, 0.0) + jnp.log1p(jnp.exp(-jnp.abs(x)))


def _norm_fwd(name, x, g):
    return _rowwise(name, lambda xv, gv: (_rms(xv, gv),), [x], [g], [(x.shape[1], BF16)])[0]


def _norm_bwd(name, x, g, dh, dres):
    def fn(xv, dhv, drv, gv):
        _, vjp = jax.vjp(_rms, xv, gv)
        dx, dg = vjp(dhv)
        return dx + drv, dg
    return _rowwise(name, fn, [x, dh, dres], [g], [(x.shape[1], F32)], [g.shape])


def _f_act(gate, up):
    return gate * _sigmoid(gate) * up


def _ffn_fwd(tag, x, norm, w_gu, w_down):
    h = _norm_fwd(tag + "_norm", x, norm)
    gu = _mm(tag + "_gu", h, w_gu, "nn")

    def act(guv):
        return (_f_act(guv[:, :D_FF], guv[:, D_FF:]),)
    a = _rowwise(tag + "_act", act, [gu], [], [(D_FF, BF16)])[0]
    out = _mm(tag + "_down", a, w_down, "nn", res=x, scale=0.5)
    return out, (h, gu, a)


def _ffn_bwd(tag, x, norm, w_gu, w_down, saved, dout):
    h, gu, a = saved
    da = _mm(tag + "_dact", dout, w_down, "nt", scale=0.5)
    d_wdown = _mm(tag + "_dwdown", a, dout, "tn", scale=0.5)

    def act_bwd(guv, dav):
        _, vjp = jax.vjp(_f_act, guv[:, :D_FF], guv[:, D_FF:])
        dg, du = vjp(dav)
        return (jnp.concatenate([dg, du], axis=1),)
    dgu = _rowwise(tag + "_dgu", act_bwd, [gu, da], [], [(2 * D_FF, BF16)])[0]
    dh = _mm(tag + "_dh", dgu, w_gu, "nt")
    d_wgu = _mm(tag + "_dwgu", h, dgu, "tn")
    dx, dnorm = _norm_bwd(tag + "_dnorm", x, norm, dh, dout)
    return dx, dnorm, d_wgu[:, :D_FF], d_wgu[:, D_FF:], d_wdown


def _shift_fwd(z, mu):
    t, c = z.shape
    tr = 256

    def body(z_ref, zp_ref, mu_ref, r_ref, k_ref, v_ref, lo_ref, gd_ref):
        zv = z_ref[...]
        prev = zp_ref[7:8, :] * jnp.where(pl.program_id(0) == 0, 0.0, 1.0)
        row = lax.broadcasted_iota(jnp.int32, zv.shape, 0)
        zsh = jnp.where(row == 0, prev, pltpu.roll(zv, 1, 0))
        zs = zv + (zsh - zv) * mu_ref[...]
        r_ref[...] = zs[:, 0:512]
        k_ref[...] = zs[:, 512:1024]
        v_ref[...] = zs[:, 1024:1536]
        lo_ref[...] = zs[:, 1536:1664]
        gd_ref[...] = zs[:, 1664:1792]

    widths = (512, 512, 512, 128, 128)
    return pl.pallas_call(
        body, name="rwkv_shift", grid=(t // tr,),
        in_specs=[pl.BlockSpec((tr, c), lambda i: (i, 0)),
                  pl.BlockSpec((8, c), lambda i: (jnp.maximum(i * (tr // 8) - 1, 0), 0)),
                  pl.BlockSpec((1, c), lambda i: (0, 0))],
        out_specs=[pl.BlockSpec((tr, w), lambda i: (i, 0)) for w in widths],
        out_shape=[jax.ShapeDtypeStruct((t, w), F32) for w in widths],
        compiler_params=_cparams(dimension_semantics=("parallel",)),
    )(z, z, mu)


def _shift_bwd(z, mu, dr, dk, dv, dlo, dgd):
    t, c = z.shape
    tr = 256
    nt = t // tr

    def body(z_ref, zp_ref, mu_ref, dr_ref, dk_ref, dv_ref, dlo_ref, dgd_ref,
             drn_ref, dkn_ref, dvn_ref, dlon_ref, dgdn_ref, dz_ref, dmu_ref):
        i = pl.program_id(0)
        zv, muv = z_ref[...], mu_ref[...]
        prev = zp_ref[7:8, :] * jnp.where(i == 0, 0.0, 1.0)
        row = lax.broadcasted_iota(jnp.int32, zv.shape, 0)
        zsh = jnp.where(row == 0, prev, pltpu.roll(zv, 1, 0))
        dzs = jnp.concatenate([dr_ref[...], dk_ref[...], dv_ref[...], dlo_ref[...], dgd_ref[...]], axis=1)
        nxt = jnp.concatenate([drn_ref[0:1, :], dkn_ref[0:1, :], dvn_ref[0:1, :], dlon_ref[0:1, :],
                               dgdn_ref[0:1, :]], axis=1) * jnp.where(i == nt - 1, 0.0, 1.0)
        u = dzs * muv
        un = jnp.where(row == tr - 1, nxt * muv, pltpu.roll(u, tr - 1, 0))
        dz_ref[...] = (dzs - u + un).astype(dz_ref.dtype)
        _accumulate(dmu_ref, jnp.sum(dzs * (zsh - zv), axis=0, keepdims=True))

    widths = (512, 512, 512, 128, 128)
    nxt_map = lambda i: (jnp.minimum((i + 1) * (tr // 8), t // 8 - 1), 0)
    return pl.pallas_call(
        body, name="rwkv_shift_bwd", grid=(nt,),
        in_specs=[pl.BlockSpec((tr, c), lambda i: (i, 0)),
                  pl.BlockSpec((8, c), lambda i: (jnp.maximum(i * (tr // 8) - 1, 0), 0)),
                  pl.BlockSpec((1, c), lambda i: (0, 0))]
        + [pl.BlockSpec((tr, w), lambda i: (i, 0)) for w in widths]
        + [pl.BlockSpec((8, w), nxt_map) for w in widths],
        out_specs=[pl.BlockSpec((tr, c), lambda i: (i, 0)), pl.BlockSpec((1, c), lambda i: (0, 0))],
        out_shape=[jax.ShapeDtypeStruct((t, c), BF16), jax.ShapeDtypeStruct((1, c), F32)],
        compiler_params=_cparams(dimension_semantics=("arbitrary",)),
    )(z, z, mu, dr, dk, dv, dlo, dgd, dr, dk, dv, dlo, dgd)


def _f_pre(k, lo, gd, w0, w2p, a0, a2p, g2, k_k, k_a):
    lane = lax.broadcasted_iota(jnp.int32, lo.shape, 1)
    lo_act = jnp.where(lane < DECAY_LORA, jnp.tanh(lo), lo)
    w = -_softplus(-(w0 + _bdot(lo_act, w2p))) - 0.5
    a = _sigmoid(a0 + _bdot(lo_act, a2p))
    g = _bdot(_sigmoid(gd), g2)
    kk = k * k_k
    k2 = k * (1.0 + (a - 1.0) * k_a)
    decay = jnp.exp(-jnp.exp(w))
    return k2, kk, a, decay, g


def _f_kk(kk, a):
    kkn = kk * lax.rsqrt(jnp.maximum(jnp.sum(kk * kk, axis=-1, keepdims=True), 1e-24))
    return kkn, kkn * a


def _f_post(y, r, k2, v, g, gn_w, gn_b, r_k):
    reps = y.shape[0] // RWKV_HEADS
    mean = jnp.mean(y, axis=-1, keepdims=True)
    var = jnp.mean(jnp.square(y - mean), axis=-1, keepdims=True)
    yn = (y - mean) * lax.rsqrt(var + GN_EPS)
    yn = yn * jnp.tile(gn_w, (reps, 1)) + jnp.tile(gn_b, (reps, 1))
    bonus = jnp.sum(r * k2 * jnp.tile(r_k, (reps, 1)), axis=-1, keepdims=True) * v
    return (yn + bonus) * g


def _to_v_rows(x):
    t = x.shape[0]
    return x.reshape(t, RWKV_HEADS, V_HI, V_LO).transpose(0, 2, 3, 1).reshape(t, V_HI, LANES)


def _from_v_rows(x):
    t = x.shape[0]
    return x.reshape(t, V_HI, V_LO, RWKV_HEADS).transpose(0, 3, 1, 2).reshape(t, RWKV_DIM)


def _k_cols(x):
    return jnp.tile(x, (V_LO, 1)).T


def _k_rows(x):
    xt = x.T
    out = xt[0:RWKV_HEADS]
    for l in range(1, V_LO):
        out = out + xt[l * RWKV_HEADS:(l + 1) * RWKV_HEADS]
    return out


def _wkv_fwd(r, w, k, kk, b, v):
    t = r.shape[0]
    tc = SCAN_CHUNK
    key_spec = pl.BlockSpec((tc, RWKV_HEADS, HEAD_DIM), lambda i: (i, 0, 0))
    row_spec = pl.BlockSpec((tc, V_HI, LANES), lambda i: (i, 0, 0))

    def body(r_ref, w_ref, k_ref, kk_ref, b_ref, v_ref, y_ref, st_ref, s_scr):
        @pl.when(pl.program_id(0) == 0)
        def _():
            s_scr[...] = jnp.zeros_like(s_scr)

        def step(ti, s):
            rc, wc, kc, kkc, bc = (_k_cols(ref[ti]) for ref in (r_ref, w_ref, k_ref, kk_ref, b_ref))
            vt = v_ref[ti]
            new, ys = [], []
            for j in range(V_HI):
                sa = -jnp.sum(s[j] * kkc, axis=0, keepdims=True)
                nj = s[j] * wc + bc * sa + kc * vt[j:j + 1]
                st_ref[ti, j] = nj
                ys.append(jnp.sum(nj * rc, axis=0, keepdims=True))
                new.append(nj)
            y_ref[ti] = jnp.concatenate(ys, axis=0)
            return tuple(new)

        s = lax.fori_loop(0, tc, step, tuple(s_scr[j] for j in range(V_HI)))
        for j in range(V_HI):
            s_scr[j] = s[j]

    return pl.pallas_call(
        body, name="wkv_fwd", grid=(t // tc,),
        in_specs=[key_spec] * 5 + [row_spec],
        out_specs=[row_spec, pl.BlockSpec((tc, V_HI, HEAD_DIM, LANES), lambda i: (i, 0, 0, 0))],
        out_shape=[jax.ShapeDtypeStruct((t, V_HI, LANES), F32),
                   jax.ShapeDtypeStruct((t, V_HI, HEAD_DIM, LANES), F32)],
        scratch_shapes=[pltpu.VMEM((V_HI, HEAD_DIM, LANES), F32)],
        compiler_params=_cparams(dimension_semantics=("arbitrary",)),
    )(r, w, k, kk, b, v)


def _wkv_bwd(r, w, k, kk, b, v, states, dy):
    t = r.shape[0]
    tc = SCAN_CHUNK
    nb = t // tc
    key_spec = pl.BlockSpec((tc, RWKV_HEADS, HEAD_DIM), lambda i: (nb - 1 - i, 0, 0))
    row_spec = pl.BlockSpec((tc, V_HI, LANES), lambda i: (nb - 1 - i, 0, 0))
    st_spec = pl.BlockSpec((tc, V_HI, HEAD_DIM, LANES), lambda i: (nb - 1 - i, 0, 0, 0))
    stp_spec = pl.BlockSpec((1, V_HI, HEAD_DIM, LANES), lambda i: (jnp.maximum((nb - 1 - i) * tc - 1, 0), 0, 0, 0))

    def body(r_ref, w_ref, k_ref, kk_ref, b_ref, v_ref, st_ref, stp_ref, dy_ref,
             dr_ref, dw_ref, dk_ref, dkk_ref, db_ref, dv_ref, ds_scr):
        @pl.when(pl.program_id(0) == 0)
        def _():
            ds_scr[...] = jnp.zeros_like(ds_scr)

        def colsum(x):
            return jnp.sum(x, axis=0, keepdims=True)

        def step(ti, ds, sp):
            rc, wc, kc, kkc, bc = (_k_cols(ref[ti]) for ref in (r_ref, w_ref, k_ref, kk_ref, b_ref))
            vt, dyt = v_ref[ti], dy_ref[ti]
            acc_r = acc_k = acc_b = acc_w = acc_kk = None
            new, dvs = [], []
            for j in range(V_HI):
                st = st_ref[ti, j]
                dsj = ds[j] + rc * dyt[j:j + 1]
                sa = -colsum(sp[j] * kkc)
                dsa = colsum(dsj * bc)
                dvs.append(colsum(dsj * kc))
                parts = (st * dyt[j:j + 1], dsj * vt[j:j + 1], dsj * sa, dsj * sp[j], sp[j] * dsa)
                if j == 0:
                    acc_r, acc_k, acc_b, acc_w, acc_kk = parts
                else:
                    acc_r, acc_k, acc_b, acc_w, acc_kk = (a + p for a, p in
                                                          zip((acc_r, acc_k, acc_b, acc_w, acc_kk), parts))
                new.append(dsj * wc - kkc * dsa)
            dv_ref[ti] = jnp.concatenate(dvs, axis=0)
            dr_ref[ti] = _k_rows(acc_r)
            dk_ref[ti] = _k_rows(acc_k)
            db_ref[ti] = _k_rows(acc_b)
            dw_ref[ti] = _k_rows(acc_w)
            dkk_ref[ti] = -_k_rows(acc_kk)
            return tuple(new)

        def loop_body(n, ds):
            ti = tc - 1 - n
            return step(ti, ds, tuple(st_ref[ti - 1, j] for j in range(V_HI)))

        ds = lax.fori_loop(0, tc - 1, loop_body, tuple(ds_scr[j] for j in range(V_HI)))
        keep = jnp.where(pl.program_id(0) == nb - 1, 0.0, 1.0)
        ds = step(0, ds, tuple(stp_ref[0, j] * keep for j in range(V_HI)))
        for j in range(V_HI):
            ds_scr[j] = ds[j]

    key_out = jax.ShapeDtypeStruct((t, RWKV_HEADS, HEAD_DIM), F32)
    return pl.pallas_call(
        body, name="wkv_bwd", grid=(nb,),
        in_specs=[key_spec] * 5 + [row_spec, st_spec, stp_spec, row_spec],
        out_specs=[key_spec] * 5 + [row_spec],
        out_shape=[key_out] * 5 + [jax.ShapeDtypeStruct((t, V_HI, LANES), F32)],
        scratch_shapes=[pltpu.VMEM((V_HI, HEAD_DIM, LANES), F32)],
        compiler_params=_cparams(dimension_semantics=("arbitrary",)),
    )(r, w, k, kk, b, v, states, states, dy)


@jax.custom_vjp
def _rope(x, cs):
    c, s = cs[:, :HEAD_DIM // 2], cs[:, HEAD_DIM // 2:]
    x1, x2 = x[:, :HEAD_DIM // 2], x[:, HEAD_DIM // 2:]
    return jnp.concatenate([x1 * c - x2 * s, x2 * c + x1 * s], axis=1)


def _rope_fwd(x, cs):
    return _rope(x, cs), cs


def _rope_bwd(cs, d):
    c, s = cs[:, :HEAD_DIM // 2], cs[:, HEAD_DIM // 2:]
    d1, d2 = d[:, :HEAD_DIM // 2], d[:, HEAD_DIM // 2:]
    return jnp.concatenate([d1 * c + d2 * s, d2 * c - d1 * s], axis=1), jnp.zeros_like(cs)


_rope.defvjp(_rope_fwd, _rope_bwd)


def _f_qk(x, cs, gain):
    return _rope(_rms(x, gain), cs)


def _attn_block(q, kp, kc, vp, vc, kmin):
    k2 = jnp.concatenate([kp, kc], axis=0)
    v2 = jnp.concatenate([vp, vc], axis=0)
    s = _bdot_nt(q, k2) * (HEAD_DIM ** -0.5)
    qi = lax.broadcasted_iota(jnp.int32, s.shape, 0)
    kj = lax.broadcasted_iota(jnp.int32, s.shape, 1)
    dist = qi + BAND_BLOCK - kj
    valid = (dist >= 0) & (dist <= BAND_BLOCK) & (kj >= kmin)
    s = jnp.where(valid, s, NEG_INF)
    m = lax.stop_gradient(jnp.max(s, axis=-1, keepdims=True))
    e = jnp.exp(s - m)
    l = jnp.sum(e, axis=-1, keepdims=True)
    o = _bdot(e, v2) / l
    return o, m + jnp.log(l)


def _attn_specs(n, nb):
    cur = pl.BlockSpec((None, BAND_BLOCK, HEAD_DIM), lambda i, j: (i, j, 0))
    prev = pl.BlockSpec((None, BAND_BLOCK, HEAD_DIM), lambda i, j: (i, jnp.maximum(j - 1, 0), 0))
    lse = pl.BlockSpec((None, BAND_BLOCK, 1), lambda i, j: (i, j, 0))
    return cur, prev, lse


def _attn_fwd(name, q, k, v):
    n, l, _ = q.shape
    nb = l // BAND_BLOCK
    cur, prev, lse = _attn_specs(n, nb)

    def body(q_ref, kp_ref, kc_ref, vp_ref, vc_ref, o_ref, lse_ref):
        kmin = jnp.where(pl.program_id(1) == 0, BAND_BLOCK, 0)
        o, ls = _attn_block(q_ref[...], kp_ref[...], kc_ref[...], vp_ref[...], vc_ref[...], kmin)
        o_ref[...] = o
        lse_ref[...] = ls

    return pl.pallas_call(
        body, name=name, grid=(n, nb), in_specs=[cur, prev, cur, prev, cur], out_specs=[cur, lse],
        out_shape=[jax.ShapeDtypeStruct((n, l, HEAD_DIM), F32), jax.ShapeDtypeStruct((n, l, 1), F32)],
        compiler_params=_cparams(dimension_semantics=("parallel", "parallel")),
    )(q, k, k, v, v)


def _attn_bwd(name, q, k, v, do, dlse):
    n, l, _ = q.shape
    nb = l // BAND_BLOCK
    cur, prev, lse = _attn_specs(n, nb)

    def body(q_ref, kp_ref, kc_ref, vp_ref, vc_ref, do_ref, dl_ref, dq_ref, dkp_ref, dkc_ref, dvp_ref, dvc_ref):
        kmin = jnp.where(pl.program_id(1) == 0, BAND_BLOCK, 0)
        _, vjp = jax.vjp(functools.partial(_attn_block, kmin=kmin),
                         q_ref[...], kp_ref[...], kc_ref[...], vp_ref[...], vc_ref[...])
        dq, dkp, dkc, dvp, dvc = vjp((do_ref[...], dl_ref[...]))
        dq_ref[...], dkp_ref[...], dkc_ref[...], dvp_ref[...], dvc_ref[...] = dq, dkp, dkc, dvp, dvc

    out = jax.ShapeDtypeStruct((n, l, HEAD_DIM), F32)
    return pl.pallas_call(
        body, name=name, grid=(n, nb), in_specs=[cur, prev, cur, prev, cur, cur, lse], out_specs=[cur] * 5,
        out_shape=[out] * 5,
        compiler_params=_cparams(dimension_semantics=("parallel", "parallel")),
    )(q, k, k, v, v, do, dlse)


def _attn_join(name, dkp, dkc, dvp, dvc):
    n, l, _ = dkc.shape
    nb = l // BAND_BLOCK
    cur = pl.BlockSpec((None, BAND_BLOCK, HEAD_DIM), lambda i, j: (i, j, 0))
    nxt = pl.BlockSpec((None, BAND_BLOCK, HEAD_DIM), lambda i, j: (i, jnp.minimum(j + 1, nb - 1), 0))

    def body(dkp_ref, dkc_ref, dvp_ref, dvc_ref, dk_ref, dv_ref):
        keep = jnp.where(pl.program_id(1) == nb - 1, 0.0, 1.0)
        dk_ref[...] = dkc_ref[...] + dkp_ref[...] * keep
        dv_ref[...] = dvc_ref[...] + dvp_ref[...] * keep

    out = jax.ShapeDtypeStruct((n, l, HEAD_DIM), F32)
    return pl.pallas_call(
        body, name=name, grid=(n, nb), in_specs=[nxt, cur, nxt, cur], out_specs=[cur, cur], out_shape=[out, out],
        compiler_params=_cparams(dimension_semantics=("parallel", "parallel")),
    )(dkp, dkc, dvp, dvc)


def _fold(x, dil):
    t, _, c = x.shape
    return x.reshape(t // dil, dil, HEADS_PER_GROUP, c).transpose(1, 2, 0, 3).reshape(
        dil * HEADS_PER_GROUP, t // dil, c)


def _unfold(x, dil):
    n, l, c = x.shape
    return x.reshape(dil, HEADS_PER_GROUP, l, c).transpose(2, 0, 1, 3).reshape(l * dil, HEADS_PER_GROUP, c)


def _f_comb(o1, o2, o3, l1, l2, l3):
    m = jnp.maximum(jnp.maximum(l1, l2), l3)
    e1, e2, e3 = jnp.exp(l1 - m), jnp.exp(l2 - m), jnp.exp(l3 - m)
    den = e1 + e2 + e3
    return (e1 / den) * o1 + (e2 / den) * o2 + (e3 / den) * o3


def _all_gather_hbm(x):
    rws, cols = x.shape

    def body(x_ref, out_ref, send_sems, recv_sems, local_sem):
        mx, my, mc = lax.axis_index("x"), lax.axis_index("y"), lax.axis_index("c")
        me, sibling = (mx, my, mc), (mx, my, 1 - mc)
        chips = [(1 - mx, my), (mx, 1 - my), (1 - mx, 1 - my)]

        def slot(px, py, pc):
            return out_ref.at[4 * px + 2 * py + pc]

        def copy(k, block, to, src=None):
            return pltpu.make_async_remote_copy(
                src_ref=slot(*block) if src is None else src, dst_ref=slot(*block),
                send_sem=send_sems.at[k], recv_sem=recv_sems.at[k], device_id=to, device_id_type=MESH)

        mine = pltpu.make_async_copy(x_ref, slot(*me), local_sem)
        mine.start()
        first = [copy(0, me, sibling, src=x_ref)]
        first += [copy(1 + j, me, (*chip, mc), src=x_ref) for j, chip in enumerate(chips)]
        for cp in first:
            cp.start()
        passed = [copy(4 + j, (*chip, mc), sibling) for j, chip in enumerate(chips)]
        for j, chip in enumerate(chips):
            copy(1 + j, (*chip, mc), me).wait_recv()
            passed[j].start()
        copy(0, sibling, me).wait_recv()
        for j, chip in enumerate(chips):
            copy(4 + j, (*chip, 1 - mc), me).wait_recv()
        for cp in first + passed:
            cp.wait_send()
        mine.wait()

    return pl.pallas_call(
        body, name="all_gather_weights",
        out_shape=jax.ShapeDtypeStruct((N_DEV, rws, cols), x.dtype),
        in_specs=[pl.BlockSpec(memory_space=pl.ANY)], out_specs=pl.BlockSpec(memory_space=pl.ANY),
        scratch_shapes=[pltpu.SemaphoreType.DMA((7,)), pltpu.SemaphoreType.DMA((7,)), pltpu.SemaphoreType.DMA],
    )(x)


def _all_gather_vmem(x):
    rws, cols = x.shape

    def body(x_ref, out_ref, send_sems, recv_sems):
        mx, my, mc = lax.axis_index("x"), lax.axis_index("y"), lax.axis_index("c")
        me, sibling = (mx, my, mc), (mx, my, 1 - mc)
        chips = [(1 - mx, my), (mx, 1 - my), (1 - mx, 1 - my)]

        def slot(px, py, pc):
            return out_ref.at[4 * px + 2 * py + pc]

        def copy(k, block, to, src=None):
            return pltpu.make_async_remote_copy(
                src_ref=slot(*block) if src is None else src, dst_ref=slot(*block),
                send_sem=send_sems.at[k], recv_sem=recv_sems.at[k], device_id=to, device_id_type=MESH)

        first = [copy(0, me, sibling, src=x_ref)]
        first += [copy(1 + j, me, (*chip, mc), src=x_ref) for j, chip in enumerate(chips)]
        for cp in first:
            cp.start()
        out_ref[4 * mx + 2 * my + mc] = x_ref[...]
        passed = [copy(4 + j, (*chip, mc), sibling) for j, chip in enumerate(chips)]
        for j, chip in enumerate(chips):
            copy(1 + j, (*chip, mc), me).wait_recv()
            passed[j].start()
        copy(0, sibling, me).wait_recv()
        for j, chip in enumerate(chips):
            copy(4 + j, (*chip, 1 - mc), me).wait_recv()
        for cp in first + passed:
            cp.wait_send()

    return pl.pallas_call(
        body, name="all_gather_small",
        out_shape=jax.ShapeDtypeStruct((N_DEV, rws, cols), x.dtype),
        in_specs=[pl.BlockSpec(memory_space=pltpu.VMEM)], out_specs=pl.BlockSpec(memory_space=pltpu.VMEM),
        scratch_shapes=[pltpu.SemaphoreType.DMA((7,)), pltpu.SemaphoreType.DMA((7,))],
    )(x)


def _all_to_all_hbm(g):
    _, rws, cols = g.shape

    def body(g_ref, out_ref, send_sems, recv_sems, local_sem):
        mx, my, mc = lax.axis_index("x"), lax.axis_index("y"), lax.axis_index("c")
        me = 4 * mx + 2 * my + mc
        mine = pltpu.make_async_copy(g_ref.at[me], out_ref.at[me], local_sem)
        mine.start()
        copies = []
        for k in range(1, N_DEV):
            px, py, pc = mx ^ (k >> 2), my ^ ((k >> 1) & 1), mc ^ (k & 1)
            peer = 4 * px + 2 * py + pc
            copies.append(pltpu.make_async_remote_copy(
                src_ref=g_ref.at[peer], dst_ref=out_ref.at[me], send_sem=send_sems.at[k - 1],
                recv_sem=recv_sems.at[k - 1], device_id=(px, py, pc), device_id_type=MESH))
        for cp in copies:
            cp.start()
        for cp in copies:
            cp.wait_recv()
        for cp in copies:
            cp.wait_send()
        mine.wait()

    return pl.pallas_call(
        body, name="all_to_all_grads",
        out_shape=jax.ShapeDtypeStruct(g.shape, g.dtype),
        in_specs=[pl.BlockSpec(memory_space=pl.ANY)], out_specs=pl.BlockSpec(memory_space=pl.ANY),
        scratch_shapes=[pltpu.SemaphoreType.DMA((7,)), pltpu.SemaphoreType.DMA((7,)), pltpu.SemaphoreType.DMA],
    )(g)


def _sum_slots(name, g, tr):
    _, rws, cols = g.shape

    def body(g_ref, o_ref):
        acc = g_ref[0].astype(F32)
        for j in range(1, N_DEV):
            acc = acc + g_ref[j].astype(F32)
        o_ref[...] = acc

    return pl.pallas_call(
        body, name=name, grid=(rws // tr,),
        in_specs=[pl.BlockSpec((N_DEV, tr, cols), lambda i: (0, i, 0))],
        out_specs=pl.BlockSpec((tr, cols), lambda i: (i, 0)),
        out_shape=jax.ShapeDtypeStruct((rws, cols), F32),
        compiler_params=_cparams(dimension_semantics=("parallel",)),
    )(g)


def _adamw(name, w, g, m, v, tr):
    def fn(wv, gv, mv, vv):
        mn = ADAM_B1 * mv + (1.0 - ADAM_B1) * gv
        vn = ADAM_B2 * vv + (1.0 - ADAM_B2) * jnp.square(gv)
        m_hat = mn / (1.0 - ADAM_B1 ** ADAM_STEP)
        v_hat = vn / (1.0 - ADAM_B2 ** ADAM_STEP)
        delta = -ADAM_LR * (m_hat / (jnp.sqrt(v_hat) + ADAM_EPS) + ADAM_WD * wv)
        return delta, mn, vn
    return _rowwise(name, fn, [w, g, m, v], [], [(LANES, F32)] * 3, tr=tr)


def _pack_local(arrs):
    return jnp.concatenate([a.reshape(-1, LANES) for a in arrs], axis=0)


def _shard_rows(rows_full, cols_full):
    return rows_full * cols_full // N_DEV // LANES


def _unpack_gathered(wall):
    out, off = {}, 0
    for name, rf, cf, axis in SHARDED:
        n = _shard_rows(rf, cf)
        piece = wall[:, off:off + n, :]
        if axis == 1:
            out[name] = piece.reshape(N_DEV, rf, cf // N_DEV).transpose(1, 0, 2).reshape(rf, cf)
        else:
            out[name] = piece.reshape(rf, cf)
        off += n
    return out


def _pack_grads(grads):
    pieces = []
    for name, rf, cf, axis in SHARDED:
        g = grads[name]
        if axis == 1:
            g = g.reshape(rf, N_DEV, cf // N_DEV).transpose(1, 0, 2)
        pieces.append(g.reshape(N_DEV, -1, LANES))
    return jnp.concatenate(pieces, axis=1)


def _unpack_local(flat, like):
    out, off = [], 0
    for a in like:
        n = a.size // LANES
        out.append(flat[off:off + n].reshape(a.shape))
        off += n
    return out


def _pack_small(arrs, rows):
    flat = jnp.concatenate([a.reshape(-1) for a in arrs])
    return jnp.pad(flat, (0, rows * LANES - flat.shape[0])).reshape(rows, LANES)


def _unpack_small(flat, like):
    flat = flat.reshape(-1)
    out, off = [], 0
    for a in like:
        out.append(flat[off:off + a.size].reshape(a.shape))
        off += a.size
    return out


def _local_step(x, p, pos, target, sm, wf):
    t = x.shape[0]
    hv = t * RWKV_HEADS

    w_gu1 = jnp.concatenate([wf["ffn1_w_gate"], wf["ffn1_w_up"]], axis=1)
    w_gu2 = jnp.concatenate([wf["ffn2_w_gate"], wf["ffn2_w_up"]], axis=1)
    x1, ffn1_saved = _ffn_fwd("ffn1", x, sm["ffn1_norm"], w_gu1, wf["ffn1_w_down"])
    h2 = _norm_fwd("mix_norm", x1, sm["mix_norm"])
    z = _mm("w_in", h2, wf["w_in"], "nn")
    z_r = z[:, :RWKV_COLS]
    q_raw = z[:, RWKV_COLS:RWKV_COLS + ATTN_DIM].reshape(t * ATTN_HEADS, HEAD_DIM)
    k_raw = z[:, RWKV_COLS + ATTN_DIM:RWKV_COLS + 2 * ATTN_DIM].reshape(t * ATTN_HEADS, HEAD_DIM)
    v_att = z[:, RWKV_COLS + 2 * ATTN_DIM:RWKV_COLS + 3 * ATTN_DIM].reshape(t, ATTN_HEADS, HEAD_DIM)
    z_g = z[:, RWKV_COLS + 3 * ATTN_DIM:]

    r, k, v, lo, gd = _shift_fwd(z_r, sm["rwkv_mu"])
    zero_lo = jnp.zeros((DECAY_LORA, RWKV_DIM), BF16)
    w2p = jnp.concatenate([wf["rwkv_w2"], zero_lo], axis=0).astype(F32)
    a2p = jnp.concatenate([zero_lo, wf["rwkv_a2"]], axis=0).astype(F32)
    pre_params = [sm["rwkv_w0"], w2p, sm["rwkv_a0"], a2p, wf["rwkv_g2"].astype(F32), sm["rwkv_k_k"],
                  sm["rwkv_k_a"]]
    wide = [(RWKV_DIM, F32)]
    k2, kk, a, decay, g = _rowwise("rwkv_pre", _f_pre, [k, lo, gd], pre_params, wide * 5)
    as_heads = lambda u: u.reshape(hv, HEAD_DIM)
    kkn, b = _rowwise("rwkv_kk", _f_kk, [as_heads(kk), as_heads(a)], [], [(HEAD_DIM, F32)] * 2, tr=512)
    scan_in = [u.reshape(t, RWKV_HEADS, HEAD_DIM) for u in (r, decay, k2, kkn, b)]
    v_rows = _to_v_rows(v)
    y_rows, states = _wkv_fwd(*scan_in, v_rows)
    y = _from_v_rows(y_rows)
    post_params = [sm["rwkv_gn_w"].reshape(RWKV_HEADS, HEAD_DIM), sm["rwkv_gn_b"].reshape(RWKV_HEADS, HEAD_DIM),
                   sm["rwkv_r_k"].reshape(RWKV_HEADS, HEAD_DIM)]
    post_rows = [as_heads(y), as_heads(r), as_heads(k2), as_heads(v), as_heads(g)]
    y_rwkv = _rowwise("rwkv_post", lambda *av: (_f_post(*av),), post_rows, post_params, [(HEAD_DIM, F32)], tr=512)[0]
    y_rwkv = y_rwkv.reshape(t, RWKV_DIM)

    inv_freq = 1.0 / (ROPE_THETA ** (jnp.arange(0, HEAD_DIM, 2, dtype=F32) / HEAD_DIM))

    def rope_table(posv, fr):
        ang = posv * fr
        return (jnp.concatenate([jnp.cos(ang), jnp.sin(ang)], axis=1),)
    cs = _rowwise("rope_table", rope_table, [pos.astype(F32).reshape(t, 1)], [inv_freq.reshape(1, HEAD_DIM // 2)],
                  [(HEAD_DIM, F32)])[0]
    cs_h = jnp.repeat(cs, ATTN_HEADS, axis=0)
    qn = _rowwise("q_rope", lambda xv, cv, gv: (_f_qk(xv, cv, gv),), [q_raw, cs_h], [sm["q_norm"]],
                  [(HEAD_DIM, F32)], tr=384)[0].reshape(t, ATTN_HEADS, HEAD_DIM)
    kn = _rowwise("k_rope", lambda xv, cv, gv: (_f_qk(xv, cv, gv),), [k_raw, cs_h], [sm["k_norm"]],
                  [(HEAD_DIM, F32)], tr=384)[0].reshape(t, ATTN_HEADS, HEAD_DIM)
    folded, outs, lses = [], [], []
    for gi, (_, dil) in enumerate(ATTN_GROUPS):
        hs = slice(gi * HEADS_PER_GROUP, (gi + 1) * HEADS_PER_GROUP)
        qf, kf, vf = _fold(qn[:, hs], dil), _fold(kn[:, hs], dil), _fold(v_att[:, hs], dil)
        o, lse = _attn_fwd("attn_fwd_%d" % gi, qf, kf, vf)
        folded.append((qf, kf, vf))
        outs.append(_unfold(o, dil).reshape(t * HEADS_PER_GROUP, HEAD_DIM))
        lses.append(_unfold(lse, dil).reshape(t * HEADS_PER_GROUP, 1))
    y_attn = _rowwise("attn_comb", lambda *av: (_f_comb(*av),), outs + lses, [], [(HEAD_DIM, F32)], tr=512)[0]
    y_attn = y_attn.reshape(t, HEADS_PER_GROUP * HEAD_DIM)

    u_r = _mm("br_rwkv", y_rwkv, wf["w_br_rwkv"], "nn")
    u_a = _mm("br_attn", y_attn, wf["w_br_attn"], "nn")

    def f_merge(zgr, zga, ur, ua):
        return _sigmoid(zgr) * ur + _sigmoid(zga) * ua
    merged = _rowwise("merge", lambda zg, ur, ua: (f_merge(zg[:, :D_MODEL], zg[:, D_MODEL:], ur, ua),),
                      [z_g, u_r, u_a], [], [(D_MODEL, BF16)])[0]
    x2 = _mm("w_out", merged, wf["w_out"], "nn", res=x1)
    x3, ffn2_saved = _ffn_fwd("ffn2", x2, sm["ffn2_norm"], w_gu2, wf["ffn2_w_down"])

    hn = _norm_fwd("ple_norm", x3, sm["ple_norm"])
    gz = _mm("ple_gate", hn, wf["ple_w_gate"], "nn")
    pp = _mm("ple_proj", p, wf["ple_w_proj"], "nn")

    def f_head(x3v, gzv, ppv, tg):
        sg = _sigmoid(gzv)
        err = x3v + sg * ppv - tg
        part = 0.5 * jnp.sum(jnp.mean(err * err, axis=-1, keepdims=True))
        dx4 = err * (1.0 / D_MODEL)
        return dx4, dx4 * ppv * sg * (1.0 - sg), dx4 * sg, jnp.full((1, LANES), part, F32)
    dx4, dgz, dpp, loss_row = _rowwise("ple_loss", f_head, [x3, gz, pp, target], [],
                                       [(D_MODEL, F32), (D_MODEL, BF16), (D_MODEL, BF16)], [(1, LANES)])
    loss = loss_row[0, 0]

    gs, gm = {}, {}
    dhn = _mm("ple_dhn", dgz, wf["ple_w_gate"], "nt")
    gm["ple_w_gate"] = _mm("ple_dwgate", hn, dgz, "tn")
    gm["ple_w_proj"] = _mm("ple_dwproj", p, dpp, "tn")
    dx3, gs["ple_norm"] = _norm_bwd("ple_dnorm", x3, sm["ple_norm"], dhn, dx4)

    dx2, gs["ffn2_norm"], gm["ffn2_w_gate"], gm["ffn2_w_up"], gm["ffn2_w_down"] = _ffn_bwd(
        "ffn2", x2, sm["ffn2_norm"], w_gu2, wf["ffn2_w_down"], ffn2_saved, dx3)

    dmerged = _mm("w_out_dmerged", dx2, wf["w_out"], "nt")
    gm["w_out"] = _mm("w_out_dw", merged, dx2, "tn")

    def merge_bwd(zg, ur, ua, dm):
        _, vjp = jax.vjp(f_merge, zg[:, :D_MODEL], zg[:, D_MODEL:], ur, ua)
        dzr, dza, dur, dua = vjp(dm)
        return jnp.concatenate([dzr, dza], axis=1), dur, dua
    dz_g, du_r, du_a = _rowwise("merge_bwd", merge_bwd, [z_g, u_r, u_a, dmerged], [],
                                [(2 * D_MODEL, BF16), (D_MODEL, BF16), (D_MODEL, BF16)])
    dy_rwkv = _mm("br_rwkv_dy", du_r, wf["w_br_rwkv"], "nt")
    gm["w_br_rwkv"] = _mm("br_rwkv_dw", y_rwkv, du_r, "tn")
    dy_attn = _mm("br_attn_dy", du_a, wf["w_br_attn"], "nt")
    gm["w_br_attn"] = _mm("br_attn_dw", y_attn, du_a, "tn")

    def comb_bwd(*av):
        _, vjp = jax.vjp(_f_comb, *av[:6])
        return vjp(av[6])
    comb_out = _rowwise("attn_comb_bwd", comb_bwd, outs + lses + [dy_attn.reshape(t * HEADS_PER_GROUP, HEAD_DIM)],
                        [], [(HEAD_DIM, F32)] * 3 + [(1, F32)] * 3, tr=512)
    dq_parts, dk_parts, dv_parts = [], [], []
    for gi, (_, dil) in enumerate(ATTN_GROUPS):
        qf, kf, vf = folded[gi]
        dof = _fold(comb_out[gi].reshape(t, HEADS_PER_GROUP, HEAD_DIM), dil)
        dlf = _fold(comb_out[3 + gi].reshape(t, HEADS_PER_GROUP, 1), dil)
        dqf, dkp, dkc, dvp, dvc = _attn_bwd("attn_bwd_%d" % gi, qf, kf, vf, dof, dlf)
        dkf, dvf = _attn_join("attn_join_%d" % gi, dkp, dkc, dvp, dvc)
        dq_parts.append(_unfold(dqf, dil))
        dk_parts.append(_unfold(dkf, dil))
        dv_parts.append(_unfold(dvf, dil))
    dqn = jnp.concatenate(dq_parts, axis=1).reshape(t * ATTN_HEADS, HEAD_DIM)
    dkn = jnp.concatenate(dk_parts, axis=1).reshape(t * ATTN_HEADS, HEAD_DIM)
    dv_att = jnp.concatenate(dv_parts, axis=1).reshape(t, ATTN_DIM)

    def qk_bwd(xv, cv, dv_, gv):
        _, vjp = jax.vjp(lambda xx, gg: _f_qk(xx, cv, gg), xv, gv)
        return vjp(dv_)
    dq_raw, gs["q_norm"] = _rowwise("q_rope_bwd", qk_bwd, [q_raw, cs_h, dqn], [sm["q_norm"]],
                                    [(HEAD_DIM, BF16)], [(1, HEAD_DIM)], tr=384)
    dk_raw, gs["k_norm"] = _rowwise("k_rope_bwd", qk_bwd, [k_raw, cs_h, dkn], [sm["k_norm"]],
                                    [(HEAD_DIM, BF16)], [(1, HEAD_DIM)], tr=384)

    def post_bwd(yv, rv, k2v, vv, gv, dv_, gnw, gnb, rk):
        _, vjp = jax.vjp(_f_post, yv, rv, k2v, vv, gv, gnw, gnb, rk)
        return vjp(dv_)
    head_acc = (RWKV_HEADS, HEAD_DIM)
    dy, dr1, dk2a, dv1, dg, d_gnw, d_gnb, d_rk = _rowwise(
        "rwkv_post_bwd", post_bwd, post_rows + [as_heads(dy_rwkv)], post_params, [(HEAD_DIM, F32)] * 5,
        [head_acc] * 3, tr=512)
    gs["rwkv_gn_w"], gs["rwkv_gn_b"], gs["rwkv_r_k"] = d_gnw, d_gnb, d_rk
    dr2, ddecay, dk2b, dkkn, db, dv_rows = _wkv_bwd(*scan_in, v_rows, states, _to_v_rows(dy.reshape(t, RWKV_DIM)))
    dr2, ddecay, dk2b, dkkn, db = [u.reshape(t, RWKV_DIM) for u in (dr2, ddecay, dk2b, dkkn, db)]
    dv2 = _from_v_rows(dv_rows)

    def kk_bwd(kkv, av, dkknv, dbv):
        _, vjp = jax.vjp(_f_kk, kkv, av)
        return vjp((dkknv, dbv))
    dkk, da = _rowwise("rwkv_kk_bwd", kk_bwd, [as_heads(kk), as_heads(a), as_heads(dkkn), as_heads(db)], [],
                       [(HEAD_DIM, F32)] * 2, tr=512)

    def pre_bwd(kv, lov, gdv, dk2x, dk2y, dkkv, dav, ddec, dgv, w0, w2p_, a0, a2p_, g2, k_k, k_a):
        _, vjp = jax.vjp(_f_pre, kv, lov, gdv, w0, w2p_, a0, a2p_, g2, k_k, k_a)
        return vjp((dk2x + dk2y, dkkv, dav, ddec, dgv))
    lora_acc = (DECAY_LORA + ICLR_LORA, RWKV_DIM)
    dk, dlo, dgd, d_w0, d_w2p, d_a0, d_a2p, d_g2, d_kk, d_ka = _rowwise(
        "rwkv_pre_bwd", pre_bwd,
        [k, lo, gd, dk2a.reshape(t, RWKV_DIM), dk2b, dkk.reshape(t, RWKV_DIM), da.reshape(t, RWKV_DIM), ddecay,
         dg.reshape(t, RWKV_DIM)],
        pre_params, [(RWKV_DIM, F32), (LANES, F32), (LANES, F32)],
        [(1, RWKV_DIM), lora_acc, (1, RWKV_DIM), lora_acc, (GATE_LORA, RWKV_DIM), (1, RWKV_DIM), (1, RWKV_DIM)])
    gs["rwkv_w0"], gs["rwkv_a0"], gs["rwkv_k_k"], gs["rwkv_k_a"] = d_w0, d_a0, d_kk, d_ka
    gm["rwkv_w2"], gm["rwkv_a2"], gm["rwkv_g2"] = d_w2p[:DECAY_LORA], d_a2p[DECAY_LORA:], d_g2
    add2 = lambda u, w_: _rowwise("rwkv_add", lambda p_, q_: (p_ + q_,), [u, w_], [], [(RWKV_DIM, F32)])[0]
    dz_r, gs["rwkv_mu"] = _shift_bwd(z_r, sm["rwkv_mu"], add2(dr1.reshape(t, RWKV_DIM), dr2), dk,
                                     add2(dv1.reshape(t, RWKV_DIM), dv2), dlo, dgd)

    dz = jnp.concatenate([dz_r, dq_raw.reshape(t, ATTN_DIM), dk_raw.reshape(t, ATTN_DIM), dv_att.astype(BF16), dz_g],
                         axis=1)
    dh2 = _mm("w_in_dh", dz, wf["w_in"], "nt")
    gm["w_in"] = _mm("w_in_dw", h2, dz, "tn")
    dx1, gs["mix_norm"] = _norm_bwd("mix_dnorm", x1, sm["mix_norm"], dh2, dx2)

    dx0, gs["ffn1_norm"], gm["ffn1_w_gate"], gm["ffn1_w_up"], gm["ffn1_w_down"] = _ffn_bwd(
        "ffn1", x, sm["ffn1_norm"], w_gu1, wf["ffn1_w_down"], ffn1_saved, dx1)
    return loss, dx0, gm, gs


def kernel(x, p, positions, ffn1_norm, ffn1_w_gate, ffn1_w_up, ffn1_w_down, mix_norm, w_in, rwkv_mu, rwkv_w0, rwkv_w2, rwkv_a0, rwkv_a2, rwkv_g2, rwkv_k_k, rwkv_k_a, rwkv_r_k, rwkv_gn_w, rwkv_gn_b, q_norm, k_norm, w_br_rwkv, w_br_attn, w_out, ffn2_norm, ffn2_w_gate, ffn2_w_up, ffn2_w_down, ple_norm, ple_w_gate, ple_w_proj, loss_target, m_ffn1_norm, m_ffn1_w_gate, m_ffn1_w_up, m_ffn1_w_down, m_mix_norm, m_w_in, m_rwkv_mu, m_rwkv_w0, m_rwkv_w2, m_rwkv_a0, m_rwkv_a2, m_rwkv_g2, m_rwkv_k_k, m_rwkv_k_a, m_rwkv_r_k, m_rwkv_gn_w, m_rwkv_gn_b, m_q_norm, m_k_norm, m_w_br_rwkv, m_w_br_attn, m_w_out, m_ffn2_norm, m_ffn2_w_gate, m_ffn2_w_up, m_ffn2_w_down, m_ple_norm, m_ple_w_gate, m_ple_w_proj, v_ffn1_norm, v_ffn1_w_gate, v_ffn1_w_up, v_ffn1_w_down, v_mix_norm, v_w_in, v_rwkv_mu, v_rwkv_w0, v_rwkv_w2, v_rwkv_a0, v_rwkv_a2, v_rwkv_g2, v_rwkv_k_k, v_rwkv_k_a, v_rwkv_r_k, v_rwkv_gn_w, v_rwkv_gn_b, v_q_norm, v_k_norm, v_w_br_rwkv, v_w_br_attn, v_w_out, v_ffn2_norm, v_ffn2_w_gate, v_ffn2_w_up, v_ffn2_w_down, v_ple_norm, v_ple_w_gate, v_ple_w_proj):
    args = locals()
    w = {n: args[n] for n in WEIGHTS}
    m = {n: args["m_" + n] for n in WEIGHTS}
    v = {n: args["v_" + n] for n in WEIGHTS}
    sharded_names = [s[0] for s in SHARDED]

    w_local = _pack_local([w[n][0] for n in sharded_names])
    wf = _unpack_gathered(_all_gather_hbm(w_local.astype(BF16)))

    sm = {n: w[n][0].reshape(1, -1) for n in SMALL}
    loss_part, dx, gm, gs = _local_step(x[0], p[0, 0], positions[0], loss_target[0], sm, wf)
    loss = lax.psum(loss_part, ("x", "y", "c"))

    rows_local = w_local.shape[0]
    recv = _all_to_all_hbm(_pack_grads(gm).astype(BF16))
    g_local = _sum_slots("sum_grads", recv, 640)
    small_like = [w[n][0] for n in SMALL]
    small_rows = 80
    gs_all = _all_gather_vmem(_pack_small([gs[n] for n in SMALL], small_rows))
    gs_sum = _sum_slots("sum_small_grads", gs_all, small_rows)

    m_local = _pack_local([m[n][0] for n in sharded_names])
    v_local = _pack_local([v[n][0] for n in sharded_names])
    d_l, m_l, v_l = _adamw("adamw", w_local, g_local, m_local, v_local, 640)
    d_s, m_s, v_s = _adamw("adamw_small", _pack_small(small_like, small_rows), gs_sum,
                           _pack_small([m[n][0] for n in SMALL], small_rows),
                           _pack_small([v[n][0] for n in SMALL], small_rows), small_rows)

    shard_like = [w[n][0] for n in sharded_names]
    res = {}
    for tag, big, small in (("grad", g_local, gs_sum), ("delta", d_l, d_s), ("new_m", m_l, m_s), ("new_v", v_l, v_s)):
        for n, a in zip(sharded_names, _unpack_local(big, shard_like)):
            res[tag, n] = a[None]
        for n, a in zip(SMALL, _unpack_small(small, small_like)):
            res[tag, n] = a[None]
    outs = [loss, dx[None]]
    for tag in ("grad", "delta", "new_m", "new_v"):
        outs += [res[tag, n] for n in WEIGHTS]
    return tuple(outs)
```

```python
import functools

import jax
import jax.numpy as jnp
from jax import lax
from jax.experimental import pallas as pl
from jax.experimental.pallas import tpu as pltpu

F32, BF16 = jnp.float32, jnp.bfloat16
MESH = pl.DeviceIdType.MESH
N_DEV = 8
LANES = 128
VMEM_LIMIT = 56 * 1024 * 1024

D_MODEL = 1024
PLE_DIM = 256
HEAD_DIM = 64
RWKV_HEADS = 8
RWKV_DIM = RWKV_HEADS * HEAD_DIM
DECAY_LORA = 64
ICLR_LORA = 64
GATE_LORA = 128
GN_EPS = 64e-5
ATTN_GROUPS = ((128, 1), (512, 4), (2048, 16))
HEADS_PER_GROUP = 4
ATTN_HEADS = HEADS_PER_GROUP * len(ATTN_GROUPS)
ATTN_DIM = ATTN_HEADS * HEAD_DIM
BAND_BLOCK = 128
ROPE_THETA = 10000.0
NEG_INF = -1e30
D_FF = 2816
RMS_EPS = 1e-6
RWKV_COLS = 3 * RWKV_DIM + DECAY_LORA + ICLR_LORA + GATE_LORA
ADAM_LR = 0.001
ADAM_B1 = 0.9
ADAM_B2 = 0.999
ADAM_EPS = 1e-08
ADAM_WD = 0.01
ADAM_STEP = 10

V_LO = LANES // RWKV_HEADS
V_HI = HEAD_DIM // V_LO
SCAN_CHUNK = 32

FF_SHARD = D_FF // N_DEV
FF_PAD = -(-FF_SHARD // LANES) * LANES
FF_HID = N_DEV * FF_PAD
IN_SHARD = 6144 // N_DEV
OUT_SHARD = D_MODEL // N_DEV

SMALL = ("ffn1_norm", "mix_norm", "rwkv_mu", "rwkv_w0", "rwkv_a0", "rwkv_k_k", "rwkv_k_a", "rwkv_r_k",
         "rwkv_gn_w", "rwkv_gn_b", "q_norm", "k_norm", "ffn2_norm", "ple_norm")
WEIGHTS = ("ffn1_norm", "ffn1_w_gate", "ffn1_w_up", "ffn1_w_down", "mix_norm", "w_in", "rwkv_mu", "rwkv_w0",
           "rwkv_w2", "rwkv_a0", "rwkv_a2", "rwkv_g2", "rwkv_k_k", "rwkv_k_a", "rwkv_r_k", "rwkv_gn_w",
           "rwkv_gn_b", "q_norm", "k_norm", "w_br_rwkv", "w_br_attn", "w_out", "ffn2_norm", "ffn2_w_gate",
           "ffn2_w_up", "ffn2_w_down", "ple_norm", "ple_w_gate", "ple_w_proj")


def _cparams(**kw):
    return pltpu.CompilerParams(vmem_limit_bytes=VMEM_LIMIT, **kw)


def _tile(n, cap):
    best = None
    for t in range(LANES, min(n, cap) + 1, LANES):
        if n % t == 0:
            best = t
    return best if best is not None else n


@jax.custom_vjp
def _bdot(a, w):
    return jnp.dot(a.astype(BF16), w.astype(BF16), preferred_element_type=F32)


def _bdot_fwd(a, w):
    return _bdot(a, w), (a, w)


def _bdot_bwd(res, g):
    a, w = res
    gb = g.astype(BF16)
    da = lax.dot_general(gb, w.astype(BF16), (((1,), (1,)), ((), ())), preferred_element_type=F32)
    dw = lax.dot_general(a.astype(BF16), gb, (((0,), (0,)), ((), ())), preferred_element_type=F32)
    return da.astype(a.dtype), dw.astype(w.dtype)


_bdot.defvjp(_bdot_fwd, _bdot_bwd)


@jax.custom_vjp
def _bdot_nt(a, b):
    return lax.dot_general(a.astype(BF16), b.astype(BF16), (((1,), (1,)), ((), ())), preferred_element_type=F32)


def _bdot_nt_fwd(a, b):
    return _bdot_nt(a, b), (a, b)


def _bdot_nt_bwd(res, g):
    a, b = res
    gb = g.astype(BF16)
    da = jnp.dot(gb, b.astype(BF16), preferred_element_type=F32)
    db = lax.dot_general(gb, a.astype(BF16), (((0,), (0,)), ((), ())), preferred_element_type=F32)
    return da.astype(a.dtype), db.astype(b.dtype)


_bdot_nt.defvjp(_bdot_nt_fwd, _bdot_nt_bwd)


def _mm(name, a, b, mode, out_dtype=F32, res=None, scale=None):
    if mode == "nn":
        (m, k), n = a.shape, b.shape[1]
    elif mode == "nt":
        (m, k), n = a.shape, b.shape[0]
    else:
        (k, m), n = a.shape, b.shape[1]
    tm, tn = _tile(m, 512), _tile(n, 512)
    a_spec = pl.BlockSpec((k, tm), lambda i, j: (0, i)) if mode == "tn" else pl.BlockSpec((tm, k), lambda i, j: (i, 0))
    b_spec = pl.BlockSpec((tn, k), lambda i, j: (j, 0)) if mode == "nt" else pl.BlockSpec((k, tn), lambda i, j: (0, j))
    dims = {"nn": ((1,), (0,)), "nt": ((1,), (1,)), "tn": ((0,), (0,))}[mode]
    o_spec = pl.BlockSpec((tm, tn), lambda i, j: (i, j))
    ins, in_specs = [a, b], [a_spec, b_spec]
    if res is not None:
        ins.append(res)
        in_specs.append(o_spec)

    def body(*refs):
        acc = lax.dot_general(refs[0][...].astype(BF16), refs[1][...].astype(BF16), (dims, ((), ())),
                              preferred_element_type=F32)
        if scale is not None:
            acc = acc * scale
        if res is not None:
            acc = acc + refs[2][...].astype(F32)
        refs[-1][...] = acc.astype(refs[-1].dtype)

    return pl.pallas_call(
        body, name=name, grid=(m // tm, n // tn), in_specs=in_specs, out_specs=o_spec,
        out_shape=jax.ShapeDtypeStruct((m, n), out_dtype),
        compiler_params=_cparams(dimension_semantics=("parallel", "parallel")),
    )(*ins)


def _mmc_nn(name, a, wb, ki, ci, n, out_dtype=F32):
    m, k = a.shape
    tm = _tile(m, 512)

    def body(a_ref, w_ref, o_ref):
        o_ref[...] = jnp.dot(a_ref[...].astype(BF16), w_ref[...], preferred_element_type=F32).astype(o_ref.dtype)

    return pl.pallas_call(
        body, name=name, grid=(m // tm, N_DEV),
        in_specs=[pl.BlockSpec((tm, k), lambda i, j: (i, 0)), pl.BlockSpec((None, k, n), lambda i, j: (j, ki, ci))],
        out_specs=pl.BlockSpec((tm, n), lambda i, j: (i, j)),
        out_shape=jax.ShapeDtypeStruct((m, N_DEV * n), out_dtype),
        compiler_params=_cparams(dimension_semantics=("parallel", "parallel")),
    )(a, wb)


def _mmc_nt(name, a, wb, ki, ci, n, k, res=None):
    m = a.shape[0]
    tm = _tile(m, 512)
    o_spec = pl.BlockSpec((tm, k), lambda i, j: (i, 0))
    ins = [a, wb] + ([res] if res is not None else [])
    in_specs = [pl.BlockSpec((tm, n), lambda i, j: (i, j)), pl.BlockSpec((None, k, n), lambda i, j: (j, ki, ci))]
    in_specs += [o_spec] if res is not None else []

    def body(*refs):
        a_ref, w_ref, o_ref = refs[0], refs[1], refs[-1]
        acc = lax.dot_general(a_ref[...].astype(BF16), w_ref[...], (((1,), (1,)), ((), ())),
                              preferred_element_type=F32)

        @pl.when(pl.program_id(1) == 0)
        def _():
            o_ref[...] = acc + refs[2][...] if res is not None else acc

        @pl.when(pl.program_id(1) != 0)
        def _():
            o_ref[...] += acc

    return pl.pallas_call(
        body, name=name, grid=(m // tm, N_DEV), in_specs=in_specs, out_specs=o_spec,
        out_shape=jax.ShapeDtypeStruct((m, k), F32),
        compiler_params=_cparams(dimension_semantics=("parallel", "arbitrary")),
    )(*ins)


def _mmc_tn(name, x, dy, n):
    m, k = x.shape
    tk = _tile(k, 512)

    def body(x_ref, dy_ref, o_ref):
        o_ref[...] = lax.dot_general(x_ref[...].astype(BF16), dy_ref[...].astype(BF16), (((0,), (0,)), ((), ())),
                                     preferred_element_type=F32).astype(o_ref.dtype)

    return pl.pallas_call(
        body, name=name, grid=(N_DEV, k // tk),
        in_specs=[pl.BlockSpec((m, tk), lambda j, i: (0, i)), pl.BlockSpec((m, n), lambda j, i: (0, j))],
        out_specs=pl.BlockSpec((None, tk, n), lambda j, i: (j, i, 0)),
        out_shape=jax.ShapeDtypeStruct((N_DEV, k, n), BF16),
        compiler_params=_cparams(dimension_semantics=("parallel", "parallel")),
    )(x, dy)


def _rowwise(name, fn, rows, params, out_rows, out_accs=(), tr=256):
    r = rows[0].shape[0]
    in_specs = [pl.BlockSpec((tr, a.shape[1]), lambda i: (i, 0)) for a in rows]
    in_specs += [pl.BlockSpec(p.shape, lambda i, nd=p.ndim: (0,) * nd) for p in params]
    out_shape = [jax.ShapeDtypeStruct((r, c), dt) for c, dt in out_rows]
    out_shape += [jax.ShapeDtypeStruct(s, F32) for s in out_accs]
    out_specs = [pl.BlockSpec((tr, c), lambda i: (i, 0)) for c, _ in out_rows]
    out_specs += [pl.BlockSpec(s, lambda i, nd=len(s): (0,) * nd) for s in out_accs]
    n_in, n_ro = len(rows) + len(params), len(out_rows)

    def body(*refs):
        res = fn(*[ref[...] for ref in refs[:n_in]])
        outs = refs[n_in:]
        for o, v in zip(outs[:n_ro], res[:n_ro]):
            o[...] = v.astype(o.dtype)
        for o, v in zip(outs[n_ro:], res[n_ro:]):
            _accumulate(o, v)

    return pl.pallas_call(
        body, name=name, grid=(r // tr,), in_specs=in_specs, out_specs=out_specs, out_shape=out_shape,
        compiler_params=_cparams(dimension_semantics=("arbitrary",)),
    )(*rows, *params)


def _accumulate(o_ref, v):
    @pl.when(pl.program_id(0) == 0)
    def _():
        o_ref[...] = v

    @pl.when(pl.program_id(0) != 0)
    def _():
        o_ref[...] += v


def _rms(x, g):
    return x * lax.rsqrt(jnp.mean(x * x, axis=-1, keepdims=True) + RMS_EPS) * g


def _sigmoid(x):
    return jax.nn.sigmoid(x)


def _softplus(x):
    return jnp.maximum(x, 0.0) + jnp.log1p(jnp.exp(-jnp.abs(x)))


def _norm_fwd(name, x, g):
    return _rowwise(name, lambda xv, gv: (_rms(xv, gv),), [x], [g], [(x.shape[1], BF16)])[0]


def _norm_bwd(name, x, g, dh, dres):
    def fn(xv, dhv, drv, gv):
        _, vjp = jax.vjp(_rms, xv, gv)
        dx, dg = vjp(dhv)
        return dx + drv, dg
    return _rowwise(name, fn, [x, dh, dres], [g], [(x.shape[1], F32)], [g.shape])


def _f_act(gate, up):
    return gate * _sigmoid(gate) * up


def _gate_up(guv, j):
    base = j * 2 * FF_PAD
    return guv[:, base:base + FF_PAD], guv[:, base + FF_PAD:base + 2 * FF_PAD]


def _ffn_fwd(tag, x, norm, w_gu, w_down):
    h = _norm_fwd(tag + "_norm", x, norm)
    gu = _mmc_nn(tag + "_gu", h, w_gu, 0, 0, 2 * FF_PAD)

    def act(guv):
        return (jnp.concatenate([_f_act(*_gate_up(guv, j)) for j in range(N_DEV)], axis=1),)
    a = _rowwise(tag + "_act", act, [gu], [], [(FF_HID, BF16)])[0]
    out = _mm(tag + "_down", a, w_down, "nn", res=x, scale=0.5)
    return out, (h, gu, a)


def _ffn_bwd(tag, x, norm, w_gu, w_down, saved, dout):
    h, gu, a = saved
    da = _mm(tag + "_dact", dout, w_down, "nt", scale=0.5)
    d_wdown = _mm(tag + "_dwdown", a, dout, "tn", out_dtype=BF16, scale=0.5)

    def act_bwd(guv, dav):
        outs = []
        for j in range(N_DEV):
            _, vjp = jax.vjp(_f_act, *_gate_up(guv, j))
            outs += list(vjp(dav[:, j * FF_PAD:(j + 1) * FF_PAD]))
        return (jnp.concatenate(outs, axis=1),)
    dgu = _rowwise(tag + "_dgu", act_bwd, [gu, da], [], [(2 * FF_HID, BF16)])[0]
    dh = _mmc_nt(tag + "_dh", dgu, w_gu, 0, 0, 2 * FF_PAD, D_MODEL)
    d_wgu = _mmc_tn(tag + "_dwgu", h, dgu, 2 * FF_PAD)
    dx, dnorm = _norm_bwd(tag + "_dnorm", x, norm, dh, dout)
    return dx, dnorm, d_wgu, d_wdown.reshape(N_DEV, FF_PAD, D_MODEL)


def _shift_fwd(z, mu):
    t, c = z.shape
    tr = 256

    def body(z_ref, zp_ref, mu_ref, r_ref, k_ref, v_ref, lo_ref, gd_ref):
        zv = z_ref[...]
        prev = zp_ref[7:8, :] * jnp.where(pl.program_id(0) == 0, 0.0, 1.0)
        row = lax.broadcasted_iota(jnp.int32, zv.shape, 0)
        zsh = jnp.where(row == 0, prev, pltpu.roll(zv, 1, 0))
        zs = zv + (zsh - zv) * mu_ref[...]
        r_ref[...] = zs[:, 0:512]
        k_ref[...] = zs[:, 512:1024]
        v_ref[...] = zs[:, 1024:1536]
        lo_ref[...] = zs[:, 1536:1664]
        gd_ref[...] = zs[:, 1664:1792]

    widths = (512, 512, 512, 128, 128)
    return pl.pallas_call(
        body, name="rwkv_shift", grid=(t // tr,),
        in_specs=[pl.BlockSpec((tr, c), lambda i: (i, 0)),
                  pl.BlockSpec((8, c), lambda i: (jnp.maximum(i * (tr // 8) - 1, 0), 0)),
                  pl.BlockSpec((1, c), lambda i: (0, 0))],
        out_specs=[pl.BlockSpec((tr, w), lambda i: (i, 0)) for w in widths],
        out_shape=[jax.ShapeDtypeStruct((t, w), F32) for w in widths],
        compiler_params=_cparams(dimension_semantics=("parallel",)),
    )(z, z, mu)


def _shift_bwd(z, mu, dr, dk, dv, dlo, dgd):
    t, c = z.shape
    tr = 256
    nt = t // tr

    def body(z_ref, zp_ref, mu_ref, dr_ref, dk_ref, dv_ref, dlo_ref, dgd_ref,
             drn_ref, dkn_ref, dvn_ref, dlon_ref, dgdn_ref, dz_ref, dmu_ref):
        i = pl.program_id(0)
        zv, muv = z_ref[...], mu_ref[...]
        prev = zp_ref[7:8, :] * jnp.where(i == 0, 0.0, 1.0)
        row = lax.broadcasted_iota(jnp.int32, zv.shape, 0)
        zsh = jnp.where(row == 0, prev, pltpu.roll(zv, 1, 0))
        dzs = jnp.concatenate([dr_ref[...], dk_ref[...], dv_ref[...], dlo_ref[...], dgd_ref[...]], axis=1)
        nxt = jnp.concatenate([drn_ref[0:1, :], dkn_ref[0:1, :], dvn_ref[0:1, :], dlon_ref[0:1, :],
                               dgdn_ref[0:1, :]], axis=1) * jnp.where(i == nt - 1, 0.0, 1.0)
        u = dzs * muv
        un = jnp.where(row == tr - 1, nxt * muv, pltpu.roll(u, tr - 1, 0))
        dz_ref[...] = (dzs - u + un).astype(dz_ref.dtype)
        _accumulate(dmu_ref, jnp.sum(dzs * (zsh - zv), axis=0, keepdims=True))

    widths = (512, 512, 512, 128, 128)
    nxt_map = lambda i: (jnp.minimum((i + 1) * (tr // 8), t // 8 - 1), 0)
    return pl.pallas_call(
        body, name="rwkv_shift_bwd", grid=(nt,),
        in_specs=[pl.BlockSpec((tr, c), lambda i: (i, 0)),
                  pl.BlockSpec((8, c), lambda i: (jnp.maximum(i * (tr // 8) - 1, 0), 0)),
                  pl.BlockSpec((1, c), lambda i: (0, 0))]
        + [pl.BlockSpec((tr, w), lambda i: (i, 0)) for w in widths]
        + [pl.BlockSpec((8, w), nxt_map) for w in widths],
        out_specs=[pl.BlockSpec((tr, c), lambda i: (i, 0)), pl.BlockSpec((1, c), lambda i: (0, 0))],
        out_shape=[jax.ShapeDtypeStruct((t, c), BF16), jax.ShapeDtypeStruct((1, c), F32)],
        compiler_params=_cparams(dimension_semantics=("arbitrary",)),
    )(z, z, mu, dr, dk, dv, dlo, dgd, dr, dk, dv, dlo, dgd)


def _f_pre(k, lo, gd, w0, w2p, a0, a2p, g2, k_k, k_a):
    lane = lax.broadcasted_iota(jnp.int32, lo.shape, 1)
    lo_act = jnp.where(lane < DECAY_LORA, jnp.tanh(lo), lo)
    w = -_softplus(-(w0 + _bdot(lo_act, w2p))) - 0.5
    a = _sigmoid(a0 + _bdot(lo_act, a2p))
    g = _bdot(_sigmoid(gd), g2)
    kk = k * k_k
    k2 = k * (1.0 + (a - 1.0) * k_a)
    decay = jnp.exp(-jnp.exp(w))
    return k2, kk, a, decay, g


def _f_kk(kk, a):
    kkn = kk * lax.rsqrt(jnp.maximum(jnp.sum(kk * kk, axis=-1, keepdims=True), 1e-24))
    return kkn, kkn * a


def _f_post(y, r, k2, v, g, gn_w, gn_b, r_k):
    reps = y.shape[0] // RWKV_HEADS
    mean = jnp.mean(y, axis=-1, keepdims=True)
    var = jnp.mean(jnp.square(y - mean), axis=-1, keepdims=True)
    yn = (y - mean) * lax.rsqrt(var + GN_EPS)
    yn = yn * jnp.tile(gn_w, (reps, 1)) + jnp.tile(gn_b, (reps, 1))
    bonus = jnp.sum(r * k2 * jnp.tile(r_k, (reps, 1)), axis=-1, keepdims=True) * v
    return (yn + bonus) * g


def _to_v_rows(x):
    t = x.shape[0]
    return x.reshape(t, RWKV_HEADS, V_HI, V_LO).transpose(0, 2, 3, 1).reshape(t, V_HI, LANES)


def _from_v_rows(x):
    t = x.shape[0]
    return x.reshape(t, V_HI, V_LO, RWKV_HEADS).transpose(0, 3, 1, 2).reshape(t, RWKV_DIM)


def _k_cols(x):
    return jnp.tile(x, (V_LO, 1)).T


def _k_rows(x):
    xt = x.T
    out = xt[0:RWKV_HEADS]
    for l in range(1, V_LO):
        out = out + xt[l * RWKV_HEADS:(l + 1) * RWKV_HEADS]
    return out


def _wkv_fwd(r, w, k, kk, b, v):
    t = r.shape[0]
    tc = SCAN_CHUNK
    key_spec = pl.BlockSpec((tc, RWKV_HEADS, HEAD_DIM), lambda i: (i, 0, 0))
    row_spec = pl.BlockSpec((tc, V_HI, LANES), lambda i: (i, 0, 0))

    def body(r_ref, w_ref, k_ref, kk_ref, b_ref, v_ref, y_ref, st_ref, s_scr):
        @pl.when(pl.program_id(0) == 0)
        def _():
            s_scr[...] = jnp.zeros_like(s_scr)

        def step(ti, s):
            rc, wc, kc, kkc, bc = (_k_cols(ref[ti]) for ref in (r_ref, w_ref, k_ref, kk_ref, b_ref))
            vt = v_ref[ti]
            new, ys = [], []
            for j in range(V_HI):
                sa = -jnp.sum(s[j] * kkc, axis=0, keepdims=True)
                nj = s[j] * wc + bc * sa + kc * vt[j:j + 1]
                st_ref[ti, j] = nj
                ys.append(jnp.sum(nj * rc, axis=0, keepdims=True))
                new.append(nj)
            y_ref[ti] = jnp.concatenate(ys, axis=0)
            return tuple(new)

        s = lax.fori_loop(0, tc, step, tuple(s_scr[j] for j in range(V_HI)))
        for j in range(V_HI):
            s_scr[j] = s[j]

    return pl.pallas_call(
        body, name="wkv_fwd", grid=(t // tc,),
        in_specs=[key_spec] * 5 + [row_spec],
        out_specs=[row_spec, pl.BlockSpec((tc, V_HI, HEAD_DIM, LANES), lambda i: (i, 0, 0, 0))],
        out_shape=[jax.ShapeDtypeStruct((t, V_HI, LANES), F32),
                   jax.ShapeDtypeStruct((t, V_HI, HEAD_DIM, LANES), F32)],
        scratch_shapes=[pltpu.VMEM((V_HI, HEAD_DIM, LANES), F32)],
        compiler_params=_cparams(dimension_semantics=("arbitrary",)),
    )(r, w, k, kk, b, v)


def _wkv_bwd(r, w, k, kk, b, v, states, dy):
    t = r.shape[0]
    tc = SCAN_CHUNK
    nb = t // tc
    key_spec = pl.BlockSpec((tc, RWKV_HEADS, HEAD_DIM), lambda i: (nb - 1 - i, 0, 0))
    row_spec = pl.BlockSpec((tc, V_HI, LANES), lambda i: (nb - 1 - i, 0, 0))
    st_spec = pl.BlockSpec((tc, V_HI, HEAD_DIM, LANES), lambda i: (nb - 1 - i, 0, 0, 0))
    stp_spec = pl.BlockSpec((1, V_HI, HEAD_DIM, LANES), lambda i: (jnp.maximum((nb - 1 - i) * tc - 1, 0), 0, 0, 0))

    def body(r_ref, w_ref, k_ref, kk_ref, b_ref, v_ref, st_ref, stp_ref, dy_ref,
             dr_ref, dw_ref, dk_ref, dkk_ref, db_ref, dv_ref, ds_scr):
        @pl.when(pl.program_id(0) == 0)
        def _():
            ds_scr[...] = jnp.zeros_like(ds_scr)

        def colsum(x):
            return jnp.sum(x, axis=0, keepdims=True)

        def step(ti, ds, sp):
            rc, wc, kc, kkc, bc = (_k_cols(ref[ti]) for ref in (r_ref, w_ref, k_ref, kk_ref, b_ref))
            vt, dyt = v_ref[ti], dy_ref[ti]
            acc_r = acc_k = acc_b = acc_w = acc_kk = None
            new, dvs = [], []
            for j in range(V_HI):
                st = st_ref[ti, j]
                dsj = ds[j] + rc * dyt[j:j + 1]
                sa = -colsum(sp[j] * kkc)
                dsa = colsum(dsj * bc)
                dvs.append(colsum(dsj * kc))
                parts = (st * dyt[j:j + 1], dsj * vt[j:j + 1], dsj * sa, dsj * sp[j], sp[j] * dsa)
                if j == 0:
                    acc_r, acc_k, acc_b, acc_w, acc_kk = parts
                else:
                    acc_r, acc_k, acc_b, acc_w, acc_kk = (a + p for a, p in
                                                          zip((acc_r, acc_k, acc_b, acc_w, acc_kk), parts))
                new.append(dsj * wc - kkc * dsa)
            dv_ref[ti] = jnp.concatenate(dvs, axis=0)
            dr_ref[ti] = _k_rows(acc_r)
            dk_ref[ti] = _k_rows(acc_k)
            db_ref[ti] = _k_rows(acc_b)
            dw_ref[ti] = _k_rows(acc_w)
            dkk_ref[ti] = -_k_rows(acc_kk)
            return tuple(new)

        def loop_body(n, ds):
            ti = tc - 1 - n
            return step(ti, ds, tuple(st_ref[ti - 1, j] for j in range(V_HI)))

        ds = lax.fori_loop(0, tc - 1, loop_body, tuple(ds_scr[j] for j in range(V_HI)))
        keep = jnp.where(pl.program_id(0) == nb - 1, 0.0, 1.0)
        ds = step(0, ds, tuple(stp_ref[0, j] * keep for j in range(V_HI)))
        for j in range(V_HI):
            ds_scr[j] = ds[j]

    key_out = jax.ShapeDtypeStruct((t, RWKV_HEADS, HEAD_DIM), F32)
    return pl.pallas_call(
        body, name="wkv_bwd", grid=(nb,),
        in_specs=[key_spec] * 5 + [row_spec, st_spec, stp_spec, row_spec],
        out_specs=[key_spec] * 5 + [row_spec],
        out_shape=[key_out] * 5 + [jax.ShapeDtypeStruct((t, V_HI, LANES), F32)],
        scratch_shapes=[pltpu.VMEM((V_HI, HEAD_DIM, LANES), F32)],
        compiler_params=_cparams(dimension_semantics=("arbitrary",)),
    )(r, w, k, kk, b, v, states, states, dy)


@jax.custom_vjp
def _rope(x, cs):
    c, s = cs[:, :HEAD_DIM // 2], cs[:, HEAD_DIM // 2:]
    x1, x2 = x[:, :HEAD_DIM // 2], x[:, HEAD_DIM // 2:]
    return jnp.concatenate([x1 * c - x2 * s, x2 * c + x1 * s], axis=1)


def _rope_fwd(x, cs):
    return _rope(x, cs), cs


def _rope_bwd(cs, d):
    c, s = cs[:, :HEAD_DIM // 2], cs[:, HEAD_DIM // 2:]
    d1, d2 = d[:, :HEAD_DIM // 2], d[:, HEAD_DIM // 2:]
    return jnp.concatenate([d1 * c + d2 * s, d2 * c - d1 * s], axis=1), jnp.zeros_like(cs)


_rope.defvjp(_rope_fwd, _rope_bwd)


def _f_qk(x, cs, gain):
    return _rope(_rms(x, gain), cs)


def _attn_block(q, kp, kc, vp, vc, kmin):
    k2 = jnp.concatenate([kp, kc], axis=0)
    v2 = jnp.concatenate([vp, vc], axis=0)
    s = _bdot_nt(q, k2) * (HEAD_DIM ** -0.5)
    qi = lax.broadcasted_iota(jnp.int32, s.shape, 0)
    kj = lax.broadcasted_iota(jnp.int32, s.shape, 1)
    dist = qi + BAND_BLOCK - kj
    valid = (dist >= 0) & (dist <= BAND_BLOCK) & (kj >= kmin)
    s = jnp.where(valid, s, NEG_INF)
    m = lax.stop_gradient(jnp.max(s, axis=-1, keepdims=True))
    e = jnp.exp(s - m)
    l = jnp.sum(e, axis=-1, keepdims=True)
    o = _bdot(e, v2) / l
    return o, m + jnp.log(l)


def _attn_specs(n, nb):
    cur = pl.BlockSpec((None, BAND_BLOCK, HEAD_DIM), lambda i, j: (i, j, 0))
    prev = pl.BlockSpec((None, BAND_BLOCK, HEAD_DIM), lambda i, j: (i, jnp.maximum(j - 1, 0), 0))
    lse = pl.BlockSpec((None, BAND_BLOCK, 1), lambda i, j: (i, j, 0))
    return cur, prev, lse


def _attn_fwd(name, q, k, v):
    n, l, _ = q.shape
    nb = l // BAND_BLOCK
    cur, prev, lse = _attn_specs(n, nb)

    def body(q_ref, kp_ref, kc_ref, vp_ref, vc_ref, o_ref, lse_ref):
        kmin = jnp.where(pl.program_id(1) == 0, BAND_BLOCK, 0)
        o, ls = _attn_block(q_ref[...], kp_ref[...], kc_ref[...], vp_ref[...], vc_ref[...], kmin)
        o_ref[...] = o
        lse_ref[...] = ls

    return pl.pallas_call(
        body, name=name, grid=(n, nb), in_specs=[cur, prev, cur, prev, cur], out_specs=[cur, lse],
        out_shape=[jax.ShapeDtypeStruct((n, l, HEAD_DIM), F32), jax.ShapeDtypeStruct((n, l, 1), F32)],
        compiler_params=_cparams(dimension_semantics=("parallel", "parallel")),
    )(q, k, k, v, v)


def _attn_bwd(name, q, k, v, do, dlse):
    n, l, _ = q.shape
    nb = l // BAND_BLOCK
    cur, prev, lse = _attn_specs(n, nb)

    def body(q_ref, kp_ref, kc_ref, vp_ref, vc_ref, do_ref, dl_ref, dq_ref, dkp_ref, dkc_ref, dvp_ref, dvc_ref):
        kmin = jnp.where(pl.program_id(1) == 0, BAND_BLOCK, 0)
        _, vjp = jax.vjp(functools.partial(_attn_block, kmin=kmin),
                         q_ref[...], kp_ref[...], kc_ref[...], vp_ref[...], vc_ref[...])
        dq, dkp, dkc, dvp, dvc = vjp((do_ref[...], dl_ref[...]))
        dq_ref[...], dkp_ref[...], dkc_ref[...], dvp_ref[...], dvc_ref[...] = dq, dkp, dkc, dvp, dvc

    out = jax.ShapeDtypeStruct((n, l, HEAD_DIM), F32)
    return pl.pallas_call(
        body, name=name, grid=(n, nb), in_specs=[cur, prev, cur, prev, cur, cur, lse], out_specs=[cur] * 5,
        out_shape=[out] * 5,
        compiler_params=_cparams(dimension_semantics=("parallel", "parallel")),
    )(q, k, k, v, v, do, dlse)


def _attn_join(name, dkp, dkc, dvp, dvc):
    n, l, _ = dkc.shape
    nb = l // BAND_BLOCK
    cur = pl.BlockSpec((None, BAND_BLOCK, HEAD_DIM), lambda i, j: (i, j, 0))
    nxt = pl.BlockSpec((None, BAND_BLOCK, HEAD_DIM), lambda i, j: (i, jnp.minimum(j + 1, nb - 1), 0))

    def body(dkp_ref, dkc_ref, dvp_ref, dvc_ref, dk_ref, dv_ref):
        keep = jnp.where(pl.program_id(1) == nb - 1, 0.0, 1.0)
        dk_ref[...] = dkc_ref[...] + dkp_ref[...] * keep
        dv_ref[...] = dvc_ref[...] + dvp_ref[...] * keep

    out = jax.ShapeDtypeStruct((n, l, HEAD_DIM), F32)
    return pl.pallas_call(
        body, name=name, grid=(n, nb), in_specs=[nxt, cur, nxt, cur], out_specs=[cur, cur], out_shape=[out, out],
        compiler_params=_cparams(dimension_semantics=("parallel", "parallel")),
    )(dkp, dkc, dvp, dvc)


def _fold(x, dil):
    t, _, c = x.shape
    return x.reshape(t // dil, dil, HEADS_PER_GROUP, c).transpose(1, 2, 0, 3).reshape(
        dil * HEADS_PER_GROUP, t // dil, c)


def _unfold(x, dil):
    n, l, c = x.shape
    return x.reshape(dil, HEADS_PER_GROUP, l, c).transpose(2, 0, 1, 3).reshape(l * dil, HEADS_PER_GROUP, c)


def _f_comb(o1, o2, o3, l1, l2, l3):
    m = jnp.maximum(jnp.maximum(l1, l2), l3)
    e1, e2, e3 = jnp.exp(l1 - m), jnp.exp(l2 - m), jnp.exp(l3 - m)
    den = e1 + e2 + e3
    return (e1 / den) * o1 + (e2 / den) * o2 + (e3 / den) * o3


def _all_gather_hbm(name, arrs):
    na = len(arrs)

    def body(*refs):
        x_refs, out_refs = refs[:na], refs[na:2 * na]
        send_sems, recv_sems, local_sems = refs[2 * na:]
        mx, my, mc = lax.axis_index("x"), lax.axis_index("y"), lax.axis_index("c")
        me, sibling = (mx, my, mc), (mx, my, 1 - mc)
        chips = [(1 - mx, my), (mx, 1 - my), (1 - mx, 1 - my)]

        def slot(a, px, py, pc):
            return out_refs[a].at[4 * px + 2 * py + pc]

        def copy(a, k, block, to, src=None):
            return pltpu.make_async_remote_copy(
                src_ref=slot(a, *block) if src is None else src, dst_ref=slot(a, *block),
                send_sem=send_sems.at[a, k], recv_sem=recv_sems.at[a, k], device_id=to, device_id_type=MESH)

        mine = [pltpu.make_async_copy(x_refs[a], slot(a, *me), local_sems.at[a]) for a in range(na)]
        for cp in mine:
            cp.start()
        first = []
        for a in range(na):
            first.append(copy(a, 0, me, sibling, src=x_refs[a]))
            first += [copy(a, 1 + j, me, (*chip, mc), src=x_refs[a]) for j, chip in enumerate(chips)]
        for cp in first:
            cp.start()
        passed = []
        for j, chip in enumerate(chips):
            for a in range(na):
                copy(a, 1 + j, (*chip, mc), me).wait_recv()
                passed.append(copy(a, 4 + j, (*chip, mc), sibling))
                passed[-1].start()
        for a in range(na):
            copy(a, 0, sibling, me).wait_recv()
            for j, chip in enumerate(chips):
                copy(a, 4 + j, (*chip, 1 - mc), me).wait_recv()
        for cp in first + passed:
            cp.wait_send()
        for cp in mine:
            cp.wait()

    hbm = pl.BlockSpec(memory_space=pl.ANY)
    return pl.pallas_call(
        body, name=name,
        out_shape=[jax.ShapeDtypeStruct((N_DEV,) + a.shape, a.dtype) for a in arrs],
        in_specs=[hbm] * na, out_specs=[hbm] * na,
        scratch_shapes=[pltpu.SemaphoreType.DMA((na, 7)), pltpu.SemaphoreType.DMA((na, 7)),
                        pltpu.SemaphoreType.DMA((na,))],
    )(*arrs)


def _all_gather_vmem(x):
    rws, cols = x.shape

    def body(x_ref, out_ref, send_sems, recv_sems):
        mx, my, mc = lax.axis_index("x"), lax.axis_index("y"), lax.axis_index("c")
        me, sibling = (mx, my, mc), (mx, my, 1 - mc)
        chips = [(1 - mx, my), (mx, 1 - my), (1 - mx, 1 - my)]

        def slot(px, py, pc):
            return out_ref.at[4 * px + 2 * py + pc]

        def copy(k, block, to, src=None):
            return pltpu.make_async_remote_copy(
                src_ref=slot(*block) if src is None else src, dst_ref=slot(*block),
                send_sem=send_sems.at[k], recv_sem=recv_sems.at[k], device_id=to, device_id_type=MESH)

        first = [copy(0, me, sibling, src=x_ref)]
        first += [copy(1 + j, me, (*chip, mc), src=x_ref) for j, chip in enumerate(chips)]
        for cp in first:
            cp.start()
        out_ref[4 * mx + 2 * my + mc] = x_ref[...]
        passed = [copy(4 + j, (*chip, mc), sibling) for j, chip in enumerate(chips)]
        for j, chip in enumerate(chips):
            copy(1 + j, (*chip, mc), me).wait_recv()
            passed[j].start()
        copy(0, sibling, me).wait_recv()
        for j, chip in enumerate(chips):
            copy(4 + j, (*chip, 1 - mc), me).wait_recv()
        for cp in first + passed:
            cp.wait_send()

    return pl.pallas_call(
        body, name="all_gather_small",
        out_shape=jax.ShapeDtypeStruct((N_DEV, rws, cols), x.dtype),
        in_specs=[pl.BlockSpec(memory_space=pltpu.VMEM)], out_specs=pl.BlockSpec(memory_space=pltpu.VMEM),
        scratch_shapes=[pltpu.SemaphoreType.DMA((7,)), pltpu.SemaphoreType.DMA((7,))],
    )(x)


def _all_to_all_hbm(name, arrs):
    na = len(arrs)

    def body(*refs):
        g_refs, out_refs = refs[:na], refs[na:2 * na]
        send_sems, recv_sems, local_sems = refs[2 * na:]
        mx, my, mc = lax.axis_index("x"), lax.axis_index("y"), lax.axis_index("c")
        me = 4 * mx + 2 * my + mc
        mine = [pltpu.make_async_copy(g_refs[a].at[me], out_refs[a].at[me], local_sems.at[a]) for a in range(na)]
        for cp in mine:
            cp.start()
        copies = []
        for k in range(1, N_DEV):
            px, py, pc = mx ^ (k >> 2), my ^ ((k >> 1) & 1), mc ^ (k & 1)
            peer = 4 * px + 2 * py + pc
            for a in range(na):
                copies.append(pltpu.make_async_remote_copy(
                    src_ref=g_refs[a].at[peer], dst_ref=out_refs[a].at[me], send_sem=send_sems.at[a, k - 1],
                    recv_sem=recv_sems.at[a, k - 1], device_id=(px, py, pc), device_id_type=MESH))
        for cp in copies:
            cp.start()
        for cp in copies:
            cp.wait_recv()
        for cp in copies:
            cp.wait_send()
        for cp in mine:
            cp.wait()

    hbm = pl.BlockSpec(memory_space=pl.ANY)
    return pl.pallas_call(
        body, name=name,
        out_shape=[jax.ShapeDtypeStruct(a.shape, a.dtype) for a in arrs],
        in_specs=[hbm] * na, out_specs=[hbm] * na,
        scratch_shapes=[pltpu.SemaphoreType.DMA((na, 7)), pltpu.SemaphoreType.DMA((na, 7)),
                        pltpu.SemaphoreType.DMA((na,))],
    )(*arrs)


def _sum_slots(name, g, tr):
    _, rws, cols = g.shape

    def body(g_ref, o_ref):
        acc = g_ref[0].astype(F32)
        for j in range(1, N_DEV):
            acc = acc + g_ref[j].astype(F32)
        o_ref[...] = acc

    return pl.pallas_call(
        body, name=name, grid=(rws // tr,),
        in_specs=[pl.BlockSpec((N_DEV, tr, cols), lambda i: (0, i, 0))],
        out_specs=pl.BlockSpec((tr, cols), lambda i: (i, 0)),
        out_shape=jax.ShapeDtypeStruct((rws, cols), F32),
        compiler_params=_cparams(dimension_semantics=("parallel",)),
    )(g)


def _adam_math(wv, gv, mv, vv):
    mn = ADAM_B1 * mv + (1.0 - ADAM_B1) * gv
    vn = ADAM_B2 * vv + (1.0 - ADAM_B2) * jnp.square(gv)
    m_hat = mn / (1.0 - ADAM_B1 ** ADAM_STEP)
    v_hat = vn / (1.0 - ADAM_B2 ** ADAM_STEP)
    delta = -ADAM_LR * (m_hat / (jnp.sqrt(v_hat) + ADAM_EPS) + ADAM_WD * wv)
    return delta, mn, vn


def _adamw(name, w, g, m, v, tr):
    return _rowwise(name, _adam_math, [w, g, m, v], [], [(LANES, F32)] * 3, tr=tr)


def _adamw_slots(name, recv, rb, cb, cw, w, m, v, tr):
    nr, nc = w.shape

    def body(g_ref, w_ref, m_ref, v_ref, go_ref, d_ref, mo_ref, vo_ref):
        acc = g_ref[0].astype(F32)
        for s in range(1, N_DEV):
            acc = acc + g_ref[s].astype(F32)
        g = acc[:, :nc]
        go_ref[...] = g
        d_ref[...], mo_ref[...], vo_ref[...] = _adam_math(w_ref[...], g, m_ref[...], v_ref[...])

    nat = pl.BlockSpec((tr, nc), lambda i: (i, 0))
    return pl.pallas_call(
        body, name=name, grid=(nr // tr,),
        in_specs=[pl.BlockSpec((N_DEV, tr, cw), lambda i: (0, rb + i, cb)), nat, nat, nat],
        out_specs=[nat] * 4, out_shape=[jax.ShapeDtypeStruct((nr, nc), F32)] * 4,
        compiler_params=_cparams(dimension_semantics=("parallel",)),
    )(recv, w, m, v)


def _local_blocks(w):
    pad_cols = lambda a: jnp.pad(a, ((0, 0), (0, FF_PAD - FF_SHARD)))
    pad_rows = lambda a: jnp.pad(a, ((0, FF_PAD - FF_SHARD), (0, 0)))
    gate_up = lambda tag: jnp.concatenate([pad_cols(w[tag + "_w_gate"]), pad_cols(w[tag + "_w_up"])], axis=1)
    blocks = {
        "ffn1_gu": gate_up("ffn1"), "ffn1_d": pad_rows(w["ffn1_w_down"]), "w_in": w["w_in"],
        "lora": jnp.concatenate([w["rwkv_w2"], w["rwkv_a2"], w["rwkv_g2"]], axis=0),
        "br": jnp.concatenate([w["w_br_rwkv"], w["w_br_attn"], w["ple_w_proj"]], axis=0),
        "w_out": w["w_out"], "ffn2_gu": gate_up("ffn2"), "ffn2_d": pad_rows(w["ffn2_w_down"]),
        "ple_gate": w["ple_w_gate"],
    }
    return {n: a.astype(BF16) for n, a in blocks.items()}


ADAM_PLAN = (
    ("ffn1_w_gate", "ffn1_gu", 0, 0, FF_PAD, 256), ("ffn1_w_up", "ffn1_gu", 0, 1, FF_PAD, 256),
    ("ffn1_w_down", "ffn1_d", 0, 0, D_MODEL, FF_SHARD // 2), ("w_in", "w_in", 0, 0, IN_SHARD, 256),
    ("rwkv_w2", "lora", 0, 0, HEAD_DIM, 64), ("rwkv_a2", "lora", 1, 0, HEAD_DIM, 64),
    ("rwkv_g2", "lora", 2, 0, HEAD_DIM, 64),
    ("w_br_rwkv", "br", 0, 0, OUT_SHARD, 256), ("w_br_attn", "br", 2, 0, OUT_SHARD, 256),
    ("ple_w_proj", "br", 3, 0, OUT_SHARD, 256), ("w_out", "w_out", 0, 0, D_MODEL, OUT_SHARD),
    ("ffn2_w_gate", "ffn2_gu", 0, 0, FF_PAD, 256), ("ffn2_w_up", "ffn2_gu", 0, 1, FF_PAD, 256),
    ("ffn2_w_down", "ffn2_d", 0, 0, D_MODEL, FF_SHARD // 2), ("ple_w_gate", "ple_gate", 0, 0, D_MODEL, OUT_SHARD),
)


def _pack_small(arrs, rows):
    flat = jnp.concatenate([a.reshape(-1) for a in arrs])
    return jnp.pad(flat, (0, rows * LANES - flat.shape[0])).reshape(rows, LANES)


def _unpack_small(flat, like):
    flat = flat.reshape(-1)
    out, off = [], 0
    for a in like:
        out.append(flat[off:off + a.size].reshape(a.shape))
        off += a.size
    return out


def _local_step(x, p, pos, target, sm, wg):
    t = x.shape[0]
    hv = t * RWKV_HEADS

    w_d1 = wg["ffn1_d"].reshape(FF_HID, D_MODEL)
    w_d2 = wg["ffn2_d"].reshape(FF_HID, D_MODEL)
    w_out = wg["w_out"].reshape(D_MODEL, D_MODEL)
    w_pg = wg["ple_gate"].reshape(D_MODEL, D_MODEL)
    full_cols = lambda blk: blk.transpose(1, 0, 2).reshape(blk.shape[1], N_DEV * blk.shape[2])
    lora_w2 = full_cols(wg["lora"][:, :DECAY_LORA])
    lora_a2 = full_cols(wg["lora"][:, DECAY_LORA:DECAY_LORA + ICLR_LORA])
    lora_g2 = full_cols(wg["lora"][:, DECAY_LORA + ICLR_LORA:])
    x1, ffn1_saved = _ffn_fwd("ffn1", x, sm["ffn1_norm"], wg["ffn1_gu"], w_d1)
    h2 = _norm_fwd("mix_norm", x1, sm["mix_norm"])
    z = _mmc_nn("w_in", h2, wg["w_in"], 0, 0, IN_SHARD)
    z_r = z[:, :RWKV_COLS]
    q_raw = z[:, RWKV_COLS:RWKV_COLS + ATTN_DIM].reshape(t * ATTN_HEADS, HEAD_DIM)
    k_raw = z[:, RWKV_COLS + ATTN_DIM:RWKV_COLS + 2 * ATTN_DIM].reshape(t * ATTN_HEADS, HEAD_DIM)
    v_att = z[:, RWKV_COLS + 2 * ATTN_DIM:RWKV_COLS + 3 * ATTN_DIM].reshape(t, ATTN_HEADS, HEAD_DIM)
    z_g = z[:, RWKV_COLS + 3 * ATTN_DIM:]

    r, k, v, lo, gd = _shift_fwd(z_r, sm["rwkv_mu"])
    zero_lo = jnp.zeros((DECAY_LORA, RWKV_DIM), BF16)
    w2p = jnp.concatenate([lora_w2, zero_lo], axis=0).astype(F32)
    a2p = jnp.concatenate([zero_lo, lora_a2], axis=0).astype(F32)
    pre_params = [sm["rwkv_w0"], w2p, sm["rwkv_a0"], a2p, lora_g2.astype(F32), sm["rwkv_k_k"], sm["rwkv_k_a"]]
    wide = [(RWKV_DIM, F32)]
    k2, kk, a, decay, g = _rowwise("rwkv_pre", _f_pre, [k, lo, gd], pre_params, wide * 5)
    as_heads = lambda u: u.reshape(hv, HEAD_DIM)
    kkn, b = _rowwise("rwkv_kk", _f_kk, [as_heads(kk), as_heads(a)], [], [(HEAD_DIM, F32)] * 2, tr=512)
    scan_in = [u.reshape(t, RWKV_HEADS, HEAD_DIM) for u in (r, decay, k2, kkn, b)]
    v_rows = _to_v_rows(v)
    y_rows, states = _wkv_fwd(*scan_in, v_rows)
    y = _from_v_rows(y_rows)
    post_params = [sm["rwkv_gn_w"].reshape(RWKV_HEADS, HEAD_DIM), sm["rwkv_gn_b"].reshape(RWKV_HEADS, HEAD_DIM),
                   sm["rwkv_r_k"].reshape(RWKV_HEADS, HEAD_DIM)]
    post_rows = [as_heads(y), as_heads(r), as_heads(k2), as_heads(v), as_heads(g)]
    y_rwkv = _rowwise("rwkv_post", lambda *av: (_f_post(*av),), post_rows, post_params, [(HEAD_DIM, F32)], tr=512)[0]
    y_rwkv = y_rwkv.reshape(t, RWKV_DIM)

    inv_freq = 1.0 / (ROPE_THETA ** (jnp.arange(0, HEAD_DIM, 2, dtype=F32) / HEAD_DIM))

    def rope_table(posv, fr):
        ang = posv * fr
        return (jnp.concatenate([jnp.cos(ang), jnp.sin(ang)], axis=1),)
    cs = _rowwise("rope_table", rope_table, [pos.astype(F32).reshape(t, 1)], [inv_freq.reshape(1, HEAD_DIM // 2)],
                  [(HEAD_DIM, F32)])[0]
    cs_h = jnp.repeat(cs, ATTN_HEADS, axis=0)
    qn = _rowwise("q_rope", lambda xv, cv, gv: (_f_qk(xv, cv, gv),), [q_raw, cs_h], [sm["q_norm"]],
                  [(HEAD_DIM, F32)], tr=384)[0].reshape(t, ATTN_HEADS, HEAD_DIM)
    kn = _rowwise("k_rope", lambda xv, cv, gv: (_f_qk(xv, cv, gv),), [k_raw, cs_h], [sm["k_norm"]],
                  [(HEAD_DIM, F32)], tr=384)[0].reshape(t, ATTN_HEADS, HEAD_DIM)
    folded, outs, lses = [], [], []
    for gi, (_, dil) in enumerate(ATTN_GROUPS):
        hs = slice(gi * HEADS_PER_GROUP, (gi + 1) * HEADS_PER_GROUP)
        qf, kf, vf = _fold(qn[:, hs], dil), _fold(kn[:, hs], dil), _fold(v_att[:, hs], dil)
        o, lse = _attn_fwd("attn_fwd_%d" % gi, qf, kf, vf)
        folded.append((qf, kf, vf))
        outs.append(_unfold(o, dil).reshape(t * HEADS_PER_GROUP, HEAD_DIM))
        lses.append(_unfold(lse, dil).reshape(t * HEADS_PER_GROUP, 1))
    y_attn = _rowwise("attn_comb", lambda *av: (_f_comb(*av),), outs + lses, [], [(HEAD_DIM, F32)], tr=512)[0]
    y_attn = y_attn.reshape(t, HEADS_PER_GROUP * HEAD_DIM)

    u_r = _mmc_nn("br_rwkv", y_rwkv, wg["br"], 0, 0, OUT_SHARD)
    u_a = _mmc_nn("br_attn", y_attn, wg["br"], 2, 0, OUT_SHARD)

    def f_merge(zgr, zga, ur, ua):
        return _sigmoid(zgr) * ur + _sigmoid(zga) * ua
    merged = _rowwise("merge", lambda zg, ur, ua: (f_merge(zg[:, :D_MODEL], zg[:, D_MODEL:], ur, ua),),
                      [z_g, u_r, u_a], [], [(D_MODEL, BF16)])[0]
    x2 = _mm("w_out", merged, w_out, "nn", res=x1)
    x3, ffn2_saved = _ffn_fwd("ffn2", x2, sm["ffn2_norm"], wg["ffn2_gu"], w_d2)

    hn = _norm_fwd("ple_norm", x3, sm["ple_norm"])
    gz = _mm("ple_gate", hn, w_pg, "nn")
    pp = _mmc_nn("ple_proj", p, wg["br"], 3, 0, OUT_SHARD)

    def f_head(x3v, gzv, ppv, tg):
        sg = _sigmoid(gzv)
        err = x3v + sg * ppv - tg
        part = 0.5 * jnp.sum(jnp.mean(err * err, axis=-1, keepdims=True))
        dx4 = err * (1.0 / D_MODEL)
        return dx4, dx4 * ppv * sg * (1.0 - sg), dx4 * sg, jnp.full((1, LANES), part, F32)
    dx4, dgz, dpp, loss_row = _rowwise("ple_loss", f_head, [x3, gz, pp, target], [],
                                       [(D_MODEL, F32), (D_MODEL, BF16), (D_MODEL, BF16)], [(1, LANES)])
    loss = loss_row[0, 0]

    gs, gm = {}, {}
    row_blocks = lambda g: g.reshape(N_DEV, g.shape[0] // N_DEV, g.shape[1])
    dhn = _mm("ple_dhn", dgz, w_pg, "nt")
    gm["ple_gate"] = row_blocks(_mm("ple_dwgate", hn, dgz, "tn", out_dtype=BF16))
    d_pproj = _mmc_tn("ple_dwproj", p, dpp, OUT_SHARD)
    dx3, gs["ple_norm"] = _norm_bwd("ple_dnorm", x3, sm["ple_norm"], dhn, dx4)

    dx2, gs["ffn2_norm"], gm["ffn2_gu"], gm["ffn2_d"] = _ffn_bwd(
        "ffn2", x2, sm["ffn2_norm"], wg["ffn2_gu"], w_d2, ffn2_saved, dx3)

    dmerged = _mm("w_out_dmerged", dx2, w_out, "nt")
    gm["w_out"] = row_blocks(_mm("w_out_dw", merged, dx2, "tn", out_dtype=BF16))

    def merge_bwd(zg, ur, ua, dm):
        _, vjp = jax.vjp(f_merge, zg[:, :D_MODEL], zg[:, D_MODEL:], ur, ua)
        dzr, dza, dur, dua = vjp(dm)
        return jnp.concatenate([dzr, dza], axis=1), dur, dua
    dz_g, du_r, du_a = _rowwise("merge_bwd", merge_bwd, [z_g, u_r, u_a, dmerged], [],
                                [(2 * D_MODEL, BF16), (D_MODEL, BF16), (D_MODEL, BF16)])
    dy_rwkv = _mmc_nt("br_rwkv_dy", du_r, wg["br"], 0, 0, OUT_SHARD, RWKV_DIM)
    dy_attn = _mmc_nt("br_attn_dy", du_a, wg["br"], 2, 0, OUT_SHARD, HEADS_PER_GROUP * HEAD_DIM)
    gm["br"] = jnp.concatenate([_mmc_tn("br_rwkv_dw", y_rwkv, du_r, OUT_SHARD),
                                _mmc_tn("br_attn_dw", y_attn, du_a, OUT_SHARD), d_pproj], axis=1)

    def comb_bwd(*av):
        _, vjp = jax.vjp(_f_comb, *av[:6])
        return vjp(av[6])
    comb_out = _rowwise("attn_comb_bwd", comb_bwd, outs + lses + [dy_attn.reshape(t * HEADS_PER_GROUP, HEAD_DIM)],
                        [], [(HEAD_DIM, F32)] * 3 + [(1, F32)] * 3, tr=512)
    dq_parts, dk_parts, dv_parts = [], [], []
    for gi, (_, dil) in enumerate(ATTN_GROUPS):
        qf, kf, vf = folded[gi]
        dof = _fold(comb_out[gi].reshape(t, HEADS_PER_GROUP, HEAD_DIM), dil)
        dlf = _fold(comb_out[3 + gi].reshape(t, HEADS_PER_GROUP, 1), dil)
        dqf, dkp, dkc, dvp, dvc = _attn_bwd("attn_bwd_%d" % gi, qf, kf, vf, dof, dlf)
        dkf, dvf = _attn_join("attn_join_%d" % gi, dkp, dkc, dvp, dvc)
        dq_parts.append(_unfold(dqf, dil))
        dk_parts.append(_unfold(dkf, dil))
        dv_parts.append(_unfold(dvf, dil))
    dqn = jnp.concatenate(dq_parts, axis=1).reshape(t * ATTN_HEADS, HEAD_DIM)
    dkn = jnp.concatenate(dk_parts, axis=1).reshape(t * ATTN_HEADS, HEAD_DIM)
    dv_att = jnp.concatenate(dv_parts, axis=1).reshape(t, ATTN_DIM)

    def qk_bwd(xv, cv, dv_, gv):
        _, vjp = jax.vjp(lambda xx, gg: _f_qk(xx, cv, gg), xv, gv)
        return vjp(dv_)
    dq_raw, gs["q_norm"] = _rowwise("q_rope_bwd", qk_bwd, [q_raw, cs_h, dqn], [sm["q_norm"]],
                                    [(HEAD_DIM, BF16)], [(1, HEAD_DIM)], tr=384)
    dk_raw, gs["k_norm"] = _rowwise("k_rope_bwd", qk_bwd, [k_raw, cs_h, dkn], [sm["k_norm"]],
                                    [(HEAD_DIM, BF16)], [(1, HEAD_DIM)], tr=384)

    def post_bwd(yv, rv, k2v, vv, gv, dv_, gnw, gnb, rk):
        _, vjp = jax.vjp(_f_post, yv, rv, k2v, vv, gv, gnw, gnb, rk)
        return vjp(dv_)
    head_acc = (RWKV_HEADS, HEAD_DIM)
    dy, dr1, dk2a, dv1, dg, d_gnw, d_gnb, d_rk = _rowwise(
        "rwkv_post_bwd", post_bwd, post_rows + [as_heads(dy_rwkv)], post_params, [(HEAD_DIM, F32)] * 5,
        [head_acc] * 3, tr=512)
    gs["rwkv_gn_w"], gs["rwkv_gn_b"], gs["rwkv_r_k"] = d_gnw, d_gnb, d_rk
    dr2, ddecay, dk2b, dkkn, db, dv_rows = _wkv_bwd(*scan_in, v_rows, states, _to_v_rows(dy.reshape(t, RWKV_DIM)))
    dr2, ddecay, dk2b, dkkn, db = [u.reshape(t, RWKV_DIM) for u in (dr2, ddecay, dk2b, dkkn, db)]
    dv2 = _from_v_rows(dv_rows)

    def kk_bwd(kkv, av, dkknv, dbv):
        _, vjp = jax.vjp(_f_kk, kkv, av)
        return vjp((dkknv, dbv))
    dkk, da = _rowwise("rwkv_kk_bwd", kk_bwd, [as_heads(kk), as_heads(a), as_heads(dkkn), as_heads(db)], [],
                       [(HEAD_DIM, F32)] * 2, tr=512)

    def pre_bwd(kv, lov, gdv, dk2x, dk2y, dkkv, dav, ddec, dgv, w0, w2p_, a0, a2p_, g2, k_k, k_a):
        _, vjp = jax.vjp(_f_pre, kv, lov, gdv, w0, w2p_, a0, a2p_, g2, k_k, k_a)
        return vjp((dk2x + dk2y, dkkv, dav, ddec, dgv))
    lora_acc = (DECAY_LORA + ICLR_LORA, RWKV_DIM)
    dk, dlo, dgd, d_w0, d_w2p, d_a0, d_a2p, d_g2, d_kk, d_ka = _rowwise(
        "rwkv_pre_bwd", pre_bwd,
        [k, lo, gd, dk2a.reshape(t, RWKV_DIM), dk2b, dkk.reshape(t, RWKV_DIM), da.reshape(t, RWKV_DIM), ddecay,
         dg.reshape(t, RWKV_DIM)],
        pre_params, [(RWKV_DIM, F32), (LANES, F32), (LANES, F32)],
        [(1, RWKV_DIM), lora_acc, (1, RWKV_DIM), lora_acc, (GATE_LORA, RWKV_DIM), (1, RWKV_DIM), (1, RWKV_DIM)])
    gs["rwkv_w0"], gs["rwkv_a0"], gs["rwkv_k_k"], gs["rwkv_k_a"] = d_w0, d_a0, d_kk, d_ka
    col_blocks = lambda g: g.reshape(g.shape[0], N_DEV, g.shape[1] // N_DEV).transpose(1, 0, 2)
    gm["lora"] = jnp.concatenate([col_blocks(d_w2p[:DECAY_LORA]), col_blocks(d_a2p[DECAY_LORA:]), col_blocks(d_g2)],
                                 axis=1).astype(BF16)
    add2 = lambda u, w_: _rowwise("rwkv_add", lambda p_, q_: (p_ + q_,), [u, w_], [], [(RWKV_DIM, F32)])[0]
    dz_r, gs["rwkv_mu"] = _shift_bwd(z_r, sm["rwkv_mu"], add2(dr1.reshape(t, RWKV_DIM), dr2), dk,
                                     add2(dv1.reshape(t, RWKV_DIM), dv2), dlo, dgd)

    dz = jnp.concatenate([dz_r, dq_raw.reshape(t, ATTN_DIM), dk_raw.reshape(t, ATTN_DIM), dv_att.astype(BF16), dz_g],
                         axis=1)
    dh2 = _mmc_nt("w_in_dh", dz, wg["w_in"], 0, 0, IN_SHARD, D_MODEL)
    gm["w_in"] = _mmc_tn("w_in_dw", h2, dz, IN_SHARD)
    dx1, gs["mix_norm"] = _norm_bwd("mix_dnorm", x1, sm["mix_norm"], dh2, dx2)

    dx0, gs["ffn1_norm"], gm["ffn1_gu"], gm["ffn1_d"] = _ffn_bwd(
        "ffn1", x, sm["ffn1_norm"], wg["ffn1_gu"], w_d1, ffn1_saved, dx1)
    return loss, dx0, gm, gs


def kernel(x, p, positions, ffn1_norm, ffn1_w_gate, ffn1_w_up, ffn1_w_down, mix_norm, w_in, rwkv_mu, rwkv_w0, rwkv_w2, rwkv_a0, rwkv_a2, rwkv_g2, rwkv_k_k, rwkv_k_a, rwkv_r_k, rwkv_gn_w, rwkv_gn_b, q_norm, k_norm, w_br_rwkv, w_br_attn, w_out, ffn2_norm, ffn2_w_gate, ffn2_w_up, ffn2_w_down, ple_norm, ple_w_gate, ple_w_proj, loss_target, m_ffn1_norm, m_ffn1_w_gate, m_ffn1_w_up, m_ffn1_w_down, m_mix_norm, m_w_in, m_rwkv_mu, m_rwkv_w0, m_rwkv_w2, m_rwkv_a0, m_rwkv_a2, m_rwkv_g2, m_rwkv_k_k, m_rwkv_k_a, m_rwkv_r_k, m_rwkv_gn_w, m_rwkv_gn_b, m_q_norm, m_k_norm, m_w_br_rwkv, m_w_br_attn, m_w_out, m_ffn2_norm, m_ffn2_w_gate, m_ffn2_w_up, m_ffn2_w_down, m_ple_norm, m_ple_w_gate, m_ple_w_proj, v_ffn1_norm, v_ffn1_w_gate, v_ffn1_w_up, v_ffn1_w_down, v_mix_norm, v_w_in, v_rwkv_mu, v_rwkv_w0, v_rwkv_w2, v_rwkv_a0, v_rwkv_a2, v_rwkv_g2, v_rwkv_k_k, v_rwkv_k_a, v_rwkv_r_k, v_rwkv_gn_w, v_rwkv_gn_b, v_q_norm, v_k_norm, v_w_br_rwkv, v_w_br_attn, v_w_out, v_ffn2_norm, v_ffn2_w_gate, v_ffn2_w_up, v_ffn2_w_down, v_ple_norm, v_ple_w_gate, v_ple_w_proj):
    args = locals()
    w = {n: args[n][0] for n in WEIGHTS}
    m = {n: args["m_" + n][0] for n in WEIGHTS}
    v = {n: args["v_" + n][0] for n in WEIGHTS}

    w_loc = _local_blocks(w)
    names = list(w_loc)
    wg = dict(zip(names, _all_gather_hbm("all_gather_weights", [w_loc[n] for n in names])))

    sm = {n: w[n].reshape(1, -1) for n in SMALL}
    loss_part, dx, gm, gs = _local_step(x[0], p[0, 0], positions[0], loss_target[0], sm, wg)
    loss = lax.psum(loss_part, ("x", "y", "c"))

    recv = dict(zip(names, _all_to_all_hbm("all_to_all_grads", [gm[n] for n in names])))
    small_like = [w[n] for n in SMALL]
    small_rows = 80
    gs_all = _all_gather_vmem(_pack_small([gs[n] for n in SMALL], small_rows))
    gs_sum = _sum_slots("sum_small_grads", gs_all, small_rows)

    res = {}
    for n, src, rb, cb, cw, tr in ADAM_PLAN:
        outs4 = _adamw_slots("adamw_" + n, recv[src], rb, cb, cw, w[n], m[n], v[n], tr)
        for tag, a in zip(("grad", "delta", "new_m", "new_v"), outs4):
            res[tag, n] = a[None]
    d_s, m_s, v_s = _adamw("adamw_small", _pack_small(small_like, small_rows), gs_sum,
                           _pack_small([m[n] for n in SMALL], small_rows),
                           _pack_small([v[n] for n in SMALL], small_rows), small_rows)
    for tag, small in (("grad", gs_sum), ("delta", d_s), ("new_m", m_s), ("new_v", v_s)):
        for n, a in zip(SMALL, _unpack_small(small, small_like)):
            res[tag, n] = a[None]
    outs = [loss, dx[None]]
    for tag in ("grad", "delta", "new_m", "new_v"):
        outs += [res[tag, n] for n in WEIGHTS]
    return tuple(outs)
```

```python
import functools

import jax
import jax.numpy as jnp
from jax import lax
from jax.experimental import pallas as pl
from jax.experimental.pallas import tpu as pltpu

F32, BF16 = jnp.float32, jnp.bfloat16
MESH = pl.DeviceIdType.MESH
N_DEV = 8
LANES = 128
VMEM_LIMIT = 56 * 1024 * 1024

D_MODEL = 1024
PLE_DIM = 256
HEAD_DIM = 64
RWKV_HEADS = 8
RWKV_DIM = RWKV_HEADS * HEAD_DIM
DECAY_LORA = 64
ICLR_LORA = 64
GATE_LORA = 128
GN_EPS = 64e-5
ATTN_GROUPS = ((128, 1), (512, 4), (2048, 16))
HEADS_PER_GROUP = 4
ATTN_HEADS = HEADS_PER_GROUP * len(ATTN_GROUPS)
ATTN_DIM = ATTN_HEADS * HEAD_DIM
BAND_BLOCK = 128
ROPE_THETA = 10000.0
NEG_INF = -1e30
D_FF = 2816
RMS_EPS = 1e-6
RWKV_COLS = 3 * RWKV_DIM + DECAY_LORA + ICLR_LORA + GATE_LORA
ADAM_LR = 0.001
ADAM_B1 = 0.9
ADAM_B2 = 0.999
ADAM_EPS = 1e-08
ADAM_WD = 0.01
ADAM_STEP = 10

V_LO = LANES // RWKV_HEADS
V_HI = HEAD_DIM // V_LO
SCAN_CHUNK = 32

FF_SHARD = D_FF // N_DEV
FF_PAD = -(-FF_SHARD // LANES) * LANES
FF_HID = N_DEV * FF_PAD
IN_SHARD = 6144 // N_DEV
OUT_SHARD = D_MODEL // N_DEV

SMALL = ("ffn1_norm", "mix_norm", "rwkv_mu", "rwkv_w0", "rwkv_a0", "rwkv_k_k", "rwkv_k_a", "rwkv_r_k",
         "rwkv_gn_w", "rwkv_gn_b", "q_norm", "k_norm", "ffn2_norm", "ple_norm")
WEIGHTS = ("ffn1_norm", "ffn1_w_gate", "ffn1_w_up", "ffn1_w_down", "mix_norm", "w_in", "rwkv_mu", "rwkv_w0",
           "rwkv_w2", "rwkv_a0", "rwkv_a2", "rwkv_g2", "rwkv_k_k", "rwkv_k_a", "rwkv_r_k", "rwkv_gn_w",
           "rwkv_gn_b", "q_norm", "k_norm", "w_br_rwkv", "w_br_attn", "w_out", "ffn2_norm", "ffn2_w_gate",
           "ffn2_w_up", "ffn2_w_down", "ple_norm", "ple_w_gate", "ple_w_proj")


def _cparams(**kw):
    return pltpu.CompilerParams(vmem_limit_bytes=VMEM_LIMIT, **kw)


def _tile(n, cap):
    best = None
    for t in range(LANES, min(n, cap) + 1, LANES):
        if n % t == 0:
            best = t
    return best if best is not None else n


@jax.custom_vjp
def _bdot(a, w):
    return jnp.dot(a.astype(BF16), w.astype(BF16), preferred_element_type=F32)


def _bdot_fwd(a, w):
    return _bdot(a, w), (a, w)


def _bdot_bwd(res, g):
    a, w = res
    gb = g.astype(BF16)
    da = lax.dot_general(gb, w.astype(BF16), (((1,), (1,)), ((), ())), preferred_element_type=F32)
    dw = lax.dot_general(a.astype(BF16), gb, (((0,), (0,)), ((), ())), preferred_element_type=F32)
    return da.astype(a.dtype), dw.astype(w.dtype)


_bdot.defvjp(_bdot_fwd, _bdot_bwd)


@jax.custom_vjp
def _bdot_nt(a, b):
    return lax.dot_general(a.astype(BF16), b.astype(BF16), (((1,), (1,)), ((), ())), preferred_element_type=F32)


def _bdot_nt_fwd(a, b):
    return _bdot_nt(a, b), (a, b)


def _bdot_nt_bwd(res, g):
    a, b = res
    gb = g.astype(BF16)
    da = jnp.dot(gb, b.astype(BF16), preferred_element_type=F32)
    db = lax.dot_general(gb, a.astype(BF16), (((0,), (0,)), ((), ())), preferred_element_type=F32)
    return da.astype(a.dtype), db.astype(b.dtype)


_bdot_nt.defvjp(_bdot_nt_fwd, _bdot_nt_bwd)


def _mm(name, a, b, mode, out_dtype=F32, res=None, scale=None):
    if mode == "nn":
        (m, k), n = a.shape, b.shape[1]
    elif mode == "nt":
        (m, k), n = a.shape, b.shape[0]
    else:
        (k, m), n = a.shape, b.shape[1]
    tm, tn = _tile(m, 512), _tile(n, 512)
    a_spec = pl.BlockSpec((k, tm), lambda i, j: (0, i)) if mode == "tn" else pl.BlockSpec((tm, k), lambda i, j: (i, 0))
    b_spec = pl.BlockSpec((tn, k), lambda i, j: (j, 0)) if mode == "nt" else pl.BlockSpec((k, tn), lambda i, j: (0, j))
    dims = {"nn": ((1,), (0,)), "nt": ((1,), (1,)), "tn": ((0,), (0,))}[mode]
    o_spec = pl.BlockSpec((tm, tn), lambda i, j: (i, j))
    ins, in_specs = [a, b], [a_spec, b_spec]
    if res is not None:
        ins.append(res)
        in_specs.append(o_spec)

    def body(*refs):
        acc = lax.dot_general(refs[0][...].astype(BF16), refs[1][...].astype(BF16), (dims, ((), ())),
                              preferred_element_type=F32)
        if scale is not None:
            acc = acc * scale
        if res is not None:
            acc = acc + refs[2][...].astype(F32)
        refs[-1][...] = acc.astype(refs[-1].dtype)

    return pl.pallas_call(
        body, name=name, grid=(m // tm, n // tn), in_specs=in_specs, out_specs=o_spec,
        out_shape=jax.ShapeDtypeStruct((m, n), out_dtype),
        compiler_params=_cparams(dimension_semantics=("parallel", "parallel")),
    )(*ins)


def _mmc_nn(name, a, wb, ki, ci, n, out_dtype=F32):
    m, k = a.shape
    tm = _tile(m, 512)

    def body(a_ref, w_ref, o_ref):
        o_ref[...] = jnp.dot(a_ref[...].astype(BF16), w_ref[...], preferred_element_type=F32).astype(o_ref.dtype)

    return pl.pallas_call(
        body, name=name, grid=(m // tm, N_DEV),
        in_specs=[pl.BlockSpec((tm, k), lambda i, j: (i, 0)), pl.BlockSpec((None, k, n), lambda i, j: (j, ki, ci))],
        out_specs=pl.BlockSpec((tm, n), lambda i, j: (i, j)),
        out_shape=jax.ShapeDtypeStruct((m, N_DEV * n), out_dtype),
        compiler_params=_cparams(dimension_semantics=("parallel", "parallel")),
    )(a, wb)


def _mmc_nt(name, a, wb, ki, ci, n, k, res=None):
    m = a.shape[0]
    tm = _tile(m, 512)
    o_spec = pl.BlockSpec((tm, k), lambda i, j: (i, 0))
    ins = [a, wb] + ([res] if res is not None else [])
    in_specs = [pl.BlockSpec((tm, n), lambda i, j: (i, j)), pl.BlockSpec((None, k, n), lambda i, j: (j, ki, ci))]
    in_specs += [o_spec] if res is not None else []

    def body(*refs):
        a_ref, w_ref, o_ref = refs[0], refs[1], refs[-1]
        acc = lax.dot_general(a_ref[...].astype(BF16), w_ref[...], (((1,), (1,)), ((), ())),
                              preferred_element_type=F32)

        @pl.when(pl.program_id(1) == 0)
        def _():
            o_ref[...] = acc + refs[2][...] if res is not None else acc

        @pl.when(pl.program_id(1) != 0)
        def _():
            o_ref[...] += acc

    return pl.pallas_call(
        body, name=name, grid=(m // tm, N_DEV), in_specs=in_specs, out_specs=o_spec,
        out_shape=jax.ShapeDtypeStruct((m, k), F32),
        compiler_params=_cparams(dimension_semantics=("parallel", "arbitrary")),
    )(*ins)


def _mmc_tn(name, x, dy, n):
    m, k = x.shape
    tk = _tile(k, 512)

    def body(x_ref, dy_ref, o_ref):
        o_ref[...] = lax.dot_general(x_ref[...].astype(BF16), dy_ref[...].astype(BF16), (((0,), (0,)), ((), ())),
                                     preferred_element_type=F32).astype(o_ref.dtype)

    return pl.pallas_call(
        body, name=name, grid=(N_DEV, k // tk),
        in_specs=[pl.BlockSpec((m, tk), lambda j, i: (0, i)), pl.BlockSpec((m, n), lambda j, i: (0, j))],
        out_specs=pl.BlockSpec((None, tk, n), lambda j, i: (j, i, 0)),
        out_shape=jax.ShapeDtypeStruct((N_DEV, k, n), BF16),
        compiler_params=_cparams(dimension_semantics=("parallel", "parallel")),
    )(x, dy)


def _rowwise(name, fn, rows, params, out_rows, out_accs=(), tr=256):
    r = rows[0].shape[0]
    in_specs = [pl.BlockSpec((tr, a.shape[1]), lambda i: (i, 0)) for a in rows]
    in_specs += [pl.BlockSpec(p.shape, lambda i, nd=p.ndim: (0,) * nd) for p in params]
    out_shape = [jax.ShapeDtypeStruct((r, c), dt) for c, dt in out_rows]
    out_shape += [jax.ShapeDtypeStruct(s, F32) for s in out_accs]
    out_specs = [pl.BlockSpec((tr, c), lambda i: (i, 0)) for c, _ in out_rows]
    out_specs += [pl.BlockSpec(s, lambda i, nd=len(s): (0,) * nd) for s in out_accs]
    n_in, n_ro = len(rows) + len(params), len(out_rows)

    def body(*refs):
        res = fn(*[ref[...] for ref in refs[:n_in]])
        outs = refs[n_in:]
        for o, v in zip(outs[:n_ro], res[:n_ro]):
            o[...] = v.astype(o.dtype)
        for o, v in zip(outs[n_ro:], res[n_ro:]):
            _accumulate(o, v)

    return pl.pallas_call(
        body, name=name, grid=(r // tr,), in_specs=in_specs, out_specs=out_specs, out_shape=out_shape,
        compiler_params=_cparams(dimension_semantics=("arbitrary",)),
    )(*rows, *params)


def _accumulate(o_ref, v):
    @pl.when(pl.program_id(0) == 0)
    def _():
        o_ref[...] = v

    @pl.when(pl.program_id(0) != 0)
    def _():
        o_ref[...] += v


def _rms(x, g):
    return x * lax.rsqrt(jnp.mean(x * x, axis=-1, keepdims=True) + RMS_EPS) * g


def _sigmoid(x):
    return jax.nn.sigmoid(x)


def _softplus(x):
    return jnp.maximum(x, 0.0) + jnp.log1p(jnp.exp(-jnp.abs(x)))


def _norm_fwd(name, x, g):
    return _rowwise(name, lambda xv, gv: (_rms(xv, gv),), [x], [g], [(x.shape[1], BF16)])[0]


def _norm_bwd(name, x, g, dh, dres):
    def fn(xv, dhv, drv, gv):
        _, vjp = jax.vjp(_rms, xv, gv)
        dx, dg = vjp(dhv)
        return dx + drv, dg
    return _rowwise(name, fn, [x, dh, dres], [g], [(x.shape[1], F32)], [g.shape])


def _f_act(gate, up):
    return gate * _sigmoid(gate) * up


def _gate_up(guv, j):
    base = j * 2 * FF_PAD
    return guv[:, base:base + FF_PAD], guv[:, base + FF_PAD:base + 2 * FF_PAD]


def _ffn_fwd(tag, x, norm, w_gu, w_down):
    h = _norm_fwd(tag + "_norm", x, norm)
    gu = _mmc_nn(tag + "_gu", h, w_gu, 0, 0, 2 * FF_PAD)

    def act(guv):
        return (jnp.concatenate([_f_act(*_gate_up(guv, j)) for j in range(N_DEV)], axis=1),)
    a = _rowwise(tag + "_act", act, [gu], [], [(FF_HID, BF16)])[0]
    out = _mm(tag + "_down", a, w_down, "nn", res=x, scale=0.5)
    return out, (h, gu, a)


def _ffn_bwd(tag, x, norm, w_gu, w_down, saved, dout):
    h, gu, a = saved
    da = _mm(tag + "_dact", dout, w_down, "nt", scale=0.5)
    d_wdown = _mm(tag + "_dwdown", a, dout, "tn", out_dtype=BF16, scale=0.5)

    def act_bwd(guv, dav):
        outs = []
        for j in range(N_DEV):
            _, vjp = jax.vjp(_f_act, *_gate_up(guv, j))
            outs += list(vjp(dav[:, j * FF_PAD:(j + 1) * FF_PAD]))
        return (jnp.concatenate(outs, axis=1),)
    dgu = _rowwise(tag + "_dgu", act_bwd, [gu, da], [], [(2 * FF_HID, BF16)])[0]
    dh = _mmc_nt(tag + "_dh", dgu, w_gu, 0, 0, 2 * FF_PAD, D_MODEL)
    d_wgu = _mmc_tn(tag + "_dwgu", h, dgu, 2 * FF_PAD)
    dx, dnorm = _norm_bwd(tag + "_dnorm", x, norm, dh, dout)
    return dx, dnorm, d_wgu, d_wdown.reshape(N_DEV, FF_PAD, D_MODEL)


def _shift_fwd(z, mu):
    t, c = z.shape
    tr = 256

    def body(z_ref, zp_ref, mu_ref, r_ref, k_ref, v_ref, lo_ref, gd_ref):
        zv = z_ref[...]
        prev = zp_ref[7:8, :] * jnp.where(pl.program_id(0) == 0, 0.0, 1.0)
        row = lax.broadcasted_iota(jnp.int32, zv.shape, 0)
        zsh = jnp.where(row == 0, prev, pltpu.roll(zv, 1, 0))
        zs = zv + (zsh - zv) * mu_ref[...]
        r_ref[...] = zs[:, 0:512]
        k_ref[...] = zs[:, 512:1024]
        v_ref[...] = zs[:, 1024:1536]
        lo_ref[...] = zs[:, 1536:1664]
        gd_ref[...] = zs[:, 1664:1792]

    widths = (512, 512, 512, 128, 128)
    return pl.pallas_call(
        body, name="rwkv_shift", grid=(t // tr,),
        in_specs=[pl.BlockSpec((tr, c), lambda i: (i, 0)),
                  pl.BlockSpec((8, c), lambda i: (jnp.maximum(i * (tr // 8) - 1, 0), 0)),
                  pl.BlockSpec((1, c), lambda i: (0, 0))],
        out_specs=[pl.BlockSpec((tr, w), lambda i: (i, 0)) for w in widths],
        out_shape=[jax.ShapeDtypeStruct((t, w), F32) for w in widths],
        compiler_params=_cparams(dimension_semantics=("parallel",)),
    )(z, z, mu)


def _shift_bwd(z, mu, dr, dk, dv, dlo, dgd):
    t, c = z.shape
    tr = 256
    nt = t // tr

    def body(z_ref, zp_ref, mu_ref, dr_ref, dk_ref, dv_ref, dlo_ref, dgd_ref,
             drn_ref, dkn_ref, dvn_ref, dlon_ref, dgdn_ref, dz_ref, dmu_ref):
        i = pl.program_id(0)
        zv, muv = z_ref[...], mu_ref[...]
        prev = zp_ref[7:8, :] * jnp.where(i == 0, 0.0, 1.0)
        row = lax.broadcasted_iota(jnp.int32, zv.shape, 0)
        zsh = jnp.where(row == 0, prev, pltpu.roll(zv, 1, 0))
        dzs = jnp.concatenate([dr_ref[...], dk_ref[...], dv_ref[...], dlo_ref[...], dgd_ref[...]], axis=1)
        nxt = jnp.concatenate([drn_ref[0:1, :], dkn_ref[0:1, :], dvn_ref[0:1, :], dlon_ref[0:1, :],
                               dgdn_ref[0:1, :]], axis=1) * jnp.where(i == nt - 1, 0.0, 1.0)
        u = dzs * muv
        un = jnp.where(row == tr - 1, nxt * muv, pltpu.roll(u, tr - 1, 0))
        dz_ref[...] = (dzs - u + un).astype(dz_ref.dtype)
        _accumulate(dmu_ref, jnp.sum(dzs * (zsh - zv), axis=0, keepdims=True))

    widths = (512, 512, 512, 128, 128)
    nxt_map = lambda i: (jnp.minimum((i + 1) * (tr // 8), t // 8 - 1), 0)
    return pl.pallas_call(
        body, name="rwkv_shift_bwd", grid=(nt,),
        in_specs=[pl.BlockSpec((tr, c), lambda i: (i, 0)),
                  pl.BlockSpec((8, c), lambda i: (jnp.maximum(i * (tr // 8) - 1, 0), 0)),
                  pl.BlockSpec((1, c), lambda i: (0, 0))]
        + [pl.BlockSpec((tr, w), lambda i: (i, 0)) for w in widths]
        + [pl.BlockSpec((8, w), nxt_map) for w in widths],
        out_specs=[pl.BlockSpec((tr, c), lambda i: (i, 0)), pl.BlockSpec((1, c), lambda i: (0, 0))],
        out_shape=[jax.ShapeDtypeStruct((t, c), BF16), jax.ShapeDtypeStruct((1, c), F32)],
        compiler_params=_cparams(dimension_semantics=("arbitrary",)),
    )(z, z, mu, dr, dk, dv, dlo, dgd, dr, dk, dv, dlo, dgd)


def _f_pre(k, lo, gd, w0, w2p, a0, a2p, g2, k_k, k_a):
    lane = lax.broadcasted_iota(jnp.int32, lo.shape, 1)
    lo_act = jnp.where(lane < DECAY_LORA, jnp.tanh(lo), lo)
    w = -_softplus(-(w0 + _bdot(lo_act, w2p))) - 0.5
    a = _sigmoid(a0 + _bdot(lo_act, a2p))
    g = _bdot(_sigmoid(gd), g2)
    kk = k * k_k
    k2 = k * (1.0 + (a - 1.0) * k_a)
    decay = jnp.exp(-jnp.exp(w))
    return k2, kk, a, decay, g


def _f_kk(kk, a):
    kkn = kk * lax.rsqrt(jnp.maximum(jnp.sum(kk * kk, axis=-1, keepdims=True), 1e-24))
    return kkn, kkn * a


def _f_post(y, r, k2, v, g, gn_w, gn_b, r_k):
    reps = y.shape[0] // RWKV_HEADS
    mean = jnp.mean(y, axis=-1, keepdims=True)
    var = jnp.mean(jnp.square(y - mean), axis=-1, keepdims=True)
    yn = (y - mean) * lax.rsqrt(var + GN_EPS)
    yn = yn * jnp.tile(gn_w, (reps, 1)) + jnp.tile(gn_b, (reps, 1))
    bonus = jnp.sum(r * k2 * jnp.tile(r_k, (reps, 1)), axis=-1, keepdims=True) * v
    return (yn + bonus) * g


def _to_v_rows(x):
    t = x.shape[0]
    return x.reshape(t, RWKV_HEADS, V_HI, V_LO).transpose(0, 2, 3, 1).reshape(t, V_HI, LANES)


def _from_v_rows(x):
    t = x.shape[0]
    return x.reshape(t, V_HI, V_LO, RWKV_HEADS).transpose(0, 3, 1, 2).reshape(t, RWKV_DIM)


def _k_cols(x):
    return jnp.tile(x, (V_LO, 1)).T


def _k_rows(x):
    xt = x.T
    out = xt[0:RWKV_HEADS]
    for l in range(1, V_LO):
        out = out + xt[l * RWKV_HEADS:(l + 1) * RWKV_HEADS]
    return out


def _wkv_fwd(r, w, k, kk, b, v):
    t = r.shape[0]
    tc = SCAN_CHUNK
    key_spec = pl.BlockSpec((tc, RWKV_HEADS, HEAD_DIM), lambda i: (i, 0, 0))
    row_spec = pl.BlockSpec((tc, V_HI, LANES), lambda i: (i, 0, 0))

    def body(r_ref, w_ref, k_ref, kk_ref, b_ref, v_ref, y_ref, st_ref, s_scr):
        @pl.when(pl.program_id(0) == 0)
        def _():
            s_scr[...] = jnp.zeros_like(s_scr)

        def step(ti, s):
            rc, wc, kc, kkc, bc = (_k_cols(ref[ti]) for ref in (r_ref, w_ref, k_ref, kk_ref, b_ref))
            vt = v_ref[ti]
            new, ys = [], []
            for j in range(V_HI):
                sa = -jnp.sum(s[j] * kkc, axis=0, keepdims=True)
                nj = s[j] * wc + bc * sa + kc * vt[j:j + 1]
                st_ref[ti, j] = nj
                ys.append(jnp.sum(nj * rc, axis=0, keepdims=True))
                new.append(nj)
            y_ref[ti] = jnp.concatenate(ys, axis=0)
            return tuple(new)

        s = lax.fori_loop(0, tc, step, tuple(s_scr[j] for j in range(V_HI)))
        for j in range(V_HI):
            s_scr[j] = s[j]

    return pl.pallas_call(
        body, name="wkv_fwd", grid=(t // tc,),
        in_specs=[key_spec] * 5 + [row_spec],
        out_specs=[row_spec, pl.BlockSpec((tc, V_HI, HEAD_DIM, LANES), lambda i: (i, 0, 0, 0))],
        out_shape=[jax.ShapeDtypeStruct((t, V_HI, LANES), F32),
                   jax.ShapeDtypeStruct((t, V_HI, HEAD_DIM, LANES), F32)],
        scratch_shapes=[pltpu.VMEM((V_HI, HEAD_DIM, LANES), F32)],
        compiler_params=_cparams(dimension_semantics=("arbitrary",)),
    )(r, w, k, kk, b, v)


def _wkv_bwd(r, w, k, kk, b, v, states, dy):
    t = r.shape[0]
    tc = SCAN_CHUNK
    nb = t // tc
    key_spec = pl.BlockSpec((tc, RWKV_HEADS, HEAD_DIM), lambda i: (nb - 1 - i, 0, 0))
    row_spec = pl.BlockSpec((tc, V_HI, LANES), lambda i: (nb - 1 - i, 0, 0))
    st_spec = pl.BlockSpec((tc, V_HI, HEAD_DIM, LANES), lambda i: (nb - 1 - i, 0, 0, 0))
    stp_spec = pl.BlockSpec((1, V_HI, HEAD_DIM, LANES), lambda i: (jnp.maximum((nb - 1 - i) * tc - 1, 0), 0, 0, 0))

    def body(r_ref, w_ref, k_ref, kk_ref, b_ref, v_ref, st_ref, stp_ref, dy_ref,
             dr_ref, dw_ref, dk_ref, dkk_ref, db_ref, dv_ref, ds_scr):
        @pl.when(pl.program_id(0) == 0)
        def _():
            ds_scr[...] = jnp.zeros_like(ds_scr)

        def colsum(x):
            return jnp.sum(x, axis=0, keepdims=True)

        def step(ti, ds, sp):
            rc, wc, kc, kkc, bc = (_k_cols(ref[ti]) for ref in (r_ref, w_ref, k_ref, kk_ref, b_ref))
            vt, dyt = v_ref[ti], dy_ref[ti]
            acc_r = acc_k = acc_b = acc_w = acc_kk = None
            new, dvs = [], []
            for j in range(V_HI):
                st = st_ref[ti, j]
                dsj = ds[j] + rc * dyt[j:j + 1]
                sa = -colsum(sp[j] * kkc)
                dsa = colsum(dsj * bc)
                dvs.append(colsum(dsj * kc))
                parts = (st * dyt[j:j + 1], dsj * vt[j:j + 1], dsj * sa, dsj * sp[j], sp[j] * dsa)
                if j == 0:
                    acc_r, acc_k, acc_b, acc_w, acc_kk = parts
                else:
                    acc_r, acc_k, acc_b, acc_w, acc_kk = (a + p for a, p in
                                                          zip((acc_r, acc_k, acc_b, acc_w, acc_kk), parts))
                new.append(dsj * wc - kkc * dsa)
            dv_ref[ti] = jnp.concatenate(dvs, axis=0)
            dr_ref[ti] = _k_rows(acc_r)
            dk_ref[ti] = _k_rows(acc_k)
            db_ref[ti] = _k_rows(acc_b)
            dw_ref[ti] = _k_rows(acc_w)
            dkk_ref[ti] = -_k_rows(acc_kk)
            return tuple(new)

        def loop_body(n, ds):
            ti = tc - 1 - n
            return step(ti, ds, tuple(st_ref[ti - 1, j] for j in range(V_HI)))

        ds = lax.fori_loop(0, tc - 1, loop_body, tuple(ds_scr[j] for j in range(V_HI)))
        keep = jnp.where(pl.program_id(0) == nb - 1, 0.0, 1.0)
        ds = step(0, ds, tuple(stp_ref[0, j] * keep for j in range(V_HI)))
        for j in range(V_HI):
            ds_scr[j] = ds[j]

    key_out = jax.ShapeDtypeStruct((t, RWKV_HEADS, HEAD_DIM), F32)
    return pl.pallas_call(
        body, name="wkv_bwd", grid=(nb,),
        in_specs=[key_spec] * 5 + [row_spec, st_spec, stp_spec, row_spec],
        out_specs=[key_spec] * 5 + [row_spec],
        out_shape=[key_out] * 5 + [jax.ShapeDtypeStruct((t, V_HI, LANES), F32)],
        scratch_shapes=[pltpu.VMEM((V_HI, HEAD_DIM, LANES), F32)],
        compiler_params=_cparams(dimension_semantics=("arbitrary",)),
    )(r, w, k, kk, b, v, states, states, dy)


PAIR = 2 * HEAD_DIM
N_PAIRS = ATTN_HEADS // 2
Q_COL0 = RWKV_COLS // PAIR
K_COL0 = Q_COL0 + N_PAIRS
V_COL0 = K_COL0 + N_PAIRS


def _swap_halves(x):
    lane = lax.broadcasted_iota(jnp.int32, x.shape, 1)
    return jnp.where((lane & (HEAD_DIM - 1)) < HEAD_DIM // 2, pltpu.roll(x, PAIR - HEAD_DIM // 2, 1),
                     pltpu.roll(x, HEAD_DIM // 2, 1))


@jax.custom_vjp
def _rope(x, cosf, sinf):
    return x * cosf + _swap_halves(x) * sinf


def _rope_fwd(x, cosf, sinf):
    return _rope(x, cosf, sinf), (cosf, sinf)


def _rope_bwd(res, d):
    cosf, sinf = res
    return d * cosf + _swap_halves(d * sinf), jnp.zeros_like(cosf), jnp.zeros_like(sinf)


_rope.defvjp(_rope_fwd, _rope_bwd)


def _head_sums(x):
    lane = lax.broadcasted_iota(jnp.int32, x.shape, 1)
    lo = jnp.where(lane < HEAD_DIM, 1.0, 0.0)
    hi = 1.0 - lo
    return lo * jnp.sum(x * lo, axis=1, keepdims=True) + hi * jnp.sum(x * hi, axis=1, keepdims=True)


def _f_qk(x, cosf, sinf, gain2):
    xn = x * lax.rsqrt(_head_sums(x * x) * (1.0 / HEAD_DIM) + RMS_EPS) * gain2
    return _rope(xn, cosf, sinf)


def _qk_prep(z, tab, q_gain, k_gain):
    t = z.shape[0]
    tr = 256

    def body(z_ref, c_ref, s_ref, qg_ref, kg_ref, o_ref):
        g = jnp.where(pl.program_id(0) < N_PAIRS, qg_ref[...], kg_ref[...])
        o_ref[...] = _f_qk(z_ref[...], c_ref[...], s_ref[...], jnp.concatenate([g, g], axis=1))

    gain = pl.BlockSpec((1, HEAD_DIM), lambda c, i: (0, 0))
    return pl.pallas_call(
        body, name="qk_prep", grid=(2 * N_PAIRS, t // tr),
        in_specs=[pl.BlockSpec((tr, PAIR), lambda c, i: (i, Q_COL0 + c)), pl.BlockSpec((tr, PAIR), lambda c, i: (i, 0)),
                  pl.BlockSpec((tr, PAIR), lambda c, i: (i, 1)), gain, gain],
        out_specs=pl.BlockSpec((tr, PAIR), lambda c, i: (i, c)),
        out_shape=jax.ShapeDtypeStruct((t, 2 * N_PAIRS * PAIR), F32),
        compiler_params=_cparams(dimension_semantics=("parallel", "parallel")),
    )(z, tab, tab, q_gain, k_gain)


def _qk_prep_bwd(z, tab, q_gain, k_gain, dq, dk):
    t = z.shape[0]
    tr = 256

    def body(z_ref, c_ref, s_ref, qg_ref, kg_ref, dq_ref, dk_ref, dz_ref, dqg_ref, dkg_ref):
        c, i = pl.program_id(0), pl.program_id(1)
        is_q = c < N_PAIRS
        g = jnp.where(is_q, qg_ref[...], kg_ref[...])
        d = jnp.where(is_q, dq_ref[...], dk_ref[...])
        _, vjp = jax.vjp(lambda xx, gg: _f_qk(xx, c_ref[...], s_ref[...], gg), z_ref[...],
                         jnp.concatenate([g, g], axis=1))
        dx, dg2 = vjp(d)
        dz_ref[...] = dx.astype(dz_ref.dtype)
        dg = dg2[:, :HEAD_DIM] + dg2[:, HEAD_DIM:]
        first_q = jnp.logical_and(c == 0, i == 0)
        first_k = jnp.logical_and(c == N_PAIRS, i == 0)

        @pl.when(first_q)
        def _():
            dqg_ref[...] = dg

        @pl.when(jnp.logical_and(is_q, jnp.logical_not(first_q)))
        def _():
            dqg_ref[...] += dg

        @pl.when(first_k)
        def _():
            dkg_ref[...] = dg

        @pl.when(jnp.logical_and(jnp.logical_not(is_q), jnp.logical_not(first_k)))
        def _():
            dkg_ref[...] += dg

    gain = pl.BlockSpec((1, HEAD_DIM), lambda c, i: (0, 0))
    return pl.pallas_call(
        body, name="qk_prep_bwd", grid=(2 * N_PAIRS, t // tr),
        in_specs=[pl.BlockSpec((tr, PAIR), lambda c, i: (i, Q_COL0 + c)), pl.BlockSpec((tr, PAIR), lambda c, i: (i, 0)),
                  pl.BlockSpec((tr, PAIR), lambda c, i: (i, 1)), gain, gain,
                  pl.BlockSpec((tr, PAIR), lambda c, i: (i, jnp.minimum(c, N_PAIRS - 1))),
                  pl.BlockSpec((tr, PAIR), lambda c, i: (i, jnp.maximum(c - N_PAIRS, 0)))],
        out_specs=[pl.BlockSpec((tr, PAIR), lambda c, i: (i, c)), gain, gain],
        out_shape=[jax.ShapeDtypeStruct((t, 2 * N_PAIRS * PAIR), BF16), jax.ShapeDtypeStruct((1, HEAD_DIM), F32),
                   jax.ShapeDtypeStruct((1, HEAD_DIM), F32)],
        compiler_params=_cparams(dimension_semantics=("arbitrary", "arbitrary")),
    )(z, tab, tab, q_gain, k_gain, dq, dk)


def _attn_block(q, kp, kc, vp, vc, kmin):
    k2 = jnp.concatenate([kp, kc], axis=0)
    v2 = jnp.concatenate([vp, vc], axis=0)
    s = _bdot_nt(q, k2) * (HEAD_DIM ** -0.5)
    qi = lax.broadcasted_iota(jnp.int32, s.shape, 0)
    kj = lax.broadcasted_iota(jnp.int32, s.shape, 1)
    dist = qi + BAND_BLOCK - kj
    valid = (dist >= 0) & (dist <= BAND_BLOCK) & (kj >= kmin)
    s = jnp.where(valid, s, NEG_INF)
    m = lax.stop_gradient(jnp.max(s, axis=-1, keepdims=True))
    e = jnp.exp(s - m)
    l = jnp.sum(e, axis=-1, keepdims=True)
    o = _bdot(e, v2) / l
    return o, m + jnp.log(l)


def _fold(src_ref, dst_ref, dil):
    t = src_ref.shape[0]
    ln = t // dil
    for j in range(dil):
        dst_ref[j * ln:(j + 1) * ln, :] = src_ref[pl.ds(j, ln, stride=dil), :]


def _unfold(src_ref, dst_ref, dil):
    t = src_ref.shape[0]
    ln = t // dil
    for j in range(dil):
        dst_ref[pl.ds(j, ln, stride=dil), :] = src_ref[j * ln:(j + 1) * ln, :]


def _per_group(fn):
    pair = pl.program_id(0)
    for gi, (_, dil) in enumerate(ATTN_GROUPS):
        @pl.when(jnp.logical_or(pair == 2 * gi, pair == 2 * gi + 1))
        def _(dil=dil):
            fn(dil)


def _block_rows(idx, blocks_per_seq):
    first = (idx & (blocks_per_seq - 1)) == 0
    cur = pl.ds(pl.multiple_of(idx * BAND_BLOCK, BAND_BLOCK), BAND_BLOCK)
    prev = pl.ds(pl.multiple_of(jnp.maximum(idx - 1, 0) * BAND_BLOCK, BAND_BLOCK), BAND_BLOCK)
    return first, cur, prev


def _heads(x):
    return x[:, :HEAD_DIM], x[:, HEAD_DIM:]


def _attn_fwd(qk, z):
    t = z.shape[0]
    n_blocks = t // BAND_BLOCK

    def body(q_ref, k_ref, v_ref, o_ref, lse_ref, qf, kf, vf, of, lf):
        def run(dil):
            _fold(q_ref, qf, dil)
            _fold(k_ref, kf, dil)
            _fold(v_ref, vf, dil)
            blocks_per_seq = n_blocks // dil

            def block(idx, carry):
                first, cur, prev = _block_rows(idx, blocks_per_seq)
                kmin = jnp.where(first, BAND_BLOCK, 0)
                outs, lses = [], []
                for q, kp, kc, vp, vc in zip(_heads(qf[cur, :]), _heads(kf[prev, :]), _heads(kf[cur, :]),
                                             _heads(vf[prev, :]), _heads(vf[cur, :])):
                    o, ls = _attn_block(q, kp, kc, vp, vc, kmin)
                    outs.append(o)
                    lses.append(jnp.broadcast_to(ls, o.shape))
                of[cur, :] = jnp.concatenate(outs, axis=1)
                lf[cur, :] = jnp.concatenate(lses, axis=1)
                return carry

            lax.fori_loop(0, n_blocks, block, 0)
            _unfold(of, o_ref, dil)
            _unfold(lf, lse_ref, dil)

        _per_group(run)

    slab = jax.ShapeDtypeStruct((t, N_PAIRS * PAIR), F32)
    out_spec = pl.BlockSpec((t, PAIR), lambda p: (0, p))
    return pl.pallas_call(
        body, name="attn_fwd", grid=(N_PAIRS,),
        in_specs=[pl.BlockSpec((t, PAIR), lambda p: (0, p)), pl.BlockSpec((t, PAIR), lambda p: (0, N_PAIRS + p)),
                  pl.BlockSpec((t, PAIR), lambda p: (0, V_COL0 + p))],
        out_specs=[out_spec, out_spec], out_shape=[slab, slab],
        scratch_shapes=[pltpu.VMEM((t, PAIR), F32)] * 5,
        compiler_params=_cparams(dimension_semantics=("parallel",)),
    )(qk, qk, z)


def _attn_bwd(qk, z, do, dlse):
    t = z.shape[0]
    n_blocks = t // BAND_BLOCK

    def body(q_ref, k_ref, v_ref, do_ref, dl_ref, dq_ref, dk_ref, dv_ref, qf, kf, vf, dof, dlf, dqf, dkf, dvf):
        def run(dil):
            for src, dst in ((q_ref, qf), (k_ref, kf), (v_ref, vf), (do_ref, dof), (dl_ref, dlf)):
                _fold(src, dst, dil)
            blocks_per_seq = n_blocks // dil

            def block(idx, carry):
                first, cur, prev = _block_rows(idx, blocks_per_seq)
                kmin = jnp.where(first, BAND_BLOCK, 0)
                grads = []
                for q, kp, kc, vp, vc, do_h, dl_h in zip(
                        _heads(qf[cur, :]), _heads(kf[prev, :]), _heads(kf[cur, :]), _heads(vf[prev, :]),
                        _heads(vf[cur, :]), _heads(dof[cur, :]), _heads(dlf[cur, :])):
                    _, vjp = jax.vjp(functools.partial(_attn_block, kmin=kmin), q, kp, kc, vp, vc)
                    grads.append(vjp((do_h, jnp.sum(dl_h, axis=1, keepdims=True))))
                dq, dkp, dkc, dvp, dvc = (jnp.concatenate([a, b], axis=1) for a, b in zip(*grads))
                dqf[cur, :] = dq
                dkf[cur, :] = dkc
                dvf[cur, :] = dvc

                @pl.when(jnp.logical_not(first))
                def _():
                    dkf[prev, :] += dkp
                    dvf[prev, :] += dvp

                return carry

            lax.fori_loop(0, n_blocks, block, 0)
            _unfold(dqf, dq_ref, dil)
            _unfold(dkf, dk_ref, dil)
            _unfold(dvf, dv_ref, dil)

        _per_group(run)

    slab = jax.ShapeDtypeStruct((t, N_PAIRS * PAIR), F32)
    own = pl.BlockSpec((t, PAIR), lambda p: (0, p))
    return pl.pallas_call(
        body, name="attn_bwd", grid=(N_PAIRS,),
        in_specs=[own, pl.BlockSpec((t, PAIR), lambda p: (0, N_PAIRS + p)),
                  pl.BlockSpec((t, PAIR), lambda p: (0, V_COL0 + p)), own, own],
        out_specs=[own] * 3, out_shape=[slab] * 3,
        scratch_shapes=[pltpu.VMEM((t, PAIR), F32)] * 8,
        compiler_params=_cparams(dimension_semantics=("parallel",)),
    )(qk, qk, z, do, dlse)


def _f_comb(o1, o2, o3, l1, l2, l3):
    m = jnp.maximum(jnp.maximum(l1, l2), l3)
    e1, e2, e3 = jnp.exp(l1 - m), jnp.exp(l2 - m), jnp.exp(l3 - m)
    den = e1 + e2 + e3
    return (e1 / den) * o1 + (e2 / den) * o2 + (e3 / den) * o3


def _all_gather_hbm(name, arrs):
    na = len(arrs)

    def body(*refs):
        x_refs, out_refs = refs[:na], refs[na:2 * na]
        send_sems, recv_sems, local_sems = refs[2 * na:]
        mx, my, mc = lax.axis_index("x"), lax.axis_index("y"), lax.axis_index("c")
        me, sibling = (mx, my, mc), (mx, my, 1 - mc)
        chips = [(1 - mx, my), (mx, 1 - my), (1 - mx, 1 - my)]

        def slot(a, px, py, pc):
            return out_refs[a].at[4 * px + 2 * py + pc]

        def copy(a, k, block, to, src=None):
            return pltpu.make_async_remote_copy(
                src_ref=slot(a, *block) if src is None else src, dst_ref=slot(a, *block),
                send_sem=send_sems.at[a, k], recv_sem=recv_sems.at[a, k], device_id=to, device_id_type=MESH)

        mine = [pltpu.make_async_copy(x_refs[a], slot(a, *me), local_sems.at[a]) for a in range(na)]
        for cp in mine:
            cp.start()
        first = []
        for a in range(na):
            first.append(copy(a, 0, me, sibling, src=x_refs[a]))
            first += [copy(a, 1 + j, me, (*chip, mc), src=x_refs[a]) for j, chip in enumerate(chips)]
        for cp in first:
            cp.start()
        passed = []
        for j, chip in enumerate(chips):
            for a in range(na):
                copy(a, 1 + j, (*chip, mc), me).wait_recv()
                passed.append(copy(a, 4 + j, (*chip, mc), sibling))
                passed[-1].start()
        for a in range(na):
            copy(a, 0, sibling, me).wait_recv()
            for j, chip in enumerate(chips):
                copy(a, 4 + j, (*chip, 1 - mc), me).wait_recv()
        for cp in first + passed:
            cp.wait_send()
        for cp in mine:
            cp.wait()

    hbm = pl.BlockSpec(memory_space=pl.ANY)
    return pl.pallas_call(
        body, name=name,
        out_shape=[jax.ShapeDtypeStruct((N_DEV,) + a.shape, a.dtype) for a in arrs],
        in_specs=[hbm] * na, out_specs=[hbm] * na,
        scratch_shapes=[pltpu.SemaphoreType.DMA((na, 7)), pltpu.SemaphoreType.DMA((na, 7)),
                        pltpu.SemaphoreType.DMA((na,))],
    )(*arrs)


def _all_gather_vmem(x):
    rws, cols = x.shape

    def body(x_ref, out_ref, send_sems, recv_sems):
        mx, my, mc = lax.axis_index("x"), lax.axis_index("y"), lax.axis_index("c")
        me, sibling = (mx, my, mc), (mx, my, 1 - mc)
        chips = [(1 - mx, my), (mx, 1 - my), (1 - mx, 1 - my)]

        def slot(px, py, pc):
            return out_ref.at[4 * px + 2 * py + pc]

        def copy(k, block, to, src=None):
            return pltpu.make_async_remote_copy(
                src_ref=slot(*block) if src is None else src, dst_ref=slot(*block),
                send_sem=send_sems.at[k], recv_sem=recv_sems.at[k], device_id=to, device_id_type=MESH)

        first = [copy(0, me, sibling, src=x_ref)]
        first += [copy(1 + j, me, (*chip, mc), src=x_ref) for j, chip in enumerate(chips)]
        for cp in first:
            cp.start()
        out_ref[4 * mx + 2 * my + mc] = x_ref[...]
        passed = [copy(4 + j, (*chip, mc), sibling) for j, chip in enumerate(chips)]
        for j, chip in enumerate(chips):
            copy(1 + j, (*chip, mc), me).wait_recv()
            passed[j].start()
        copy(0, sibling, me).wait_recv()
        for j, chip in enumerate(chips):
            copy(4 + j, (*chip, 1 - mc), me).wait_recv()
        for cp in first + passed:
            cp.wait_send()

    return pl.pallas_call(
        body, name="all_gather_small",
        out_shape=jax.ShapeDtypeStruct((N_DEV, rws, cols), x.dtype),
        in_specs=[pl.BlockSpec(memory_space=pltpu.VMEM)], out_specs=pl.BlockSpec(memory_space=pltpu.VMEM),
        scratch_shapes=[pltpu.SemaphoreType.DMA((7,)), pltpu.SemaphoreType.DMA((7,))],
    )(x)


def _all_to_all_hbm(name, arrs):
    na = len(arrs)

    def body(*refs):
        g_refs, out_refs = refs[:na], refs[na:2 * na]
        send_sems, recv_sems, local_sems = refs[2 * na:]
        mx, my, mc = lax.axis_index("x"), lax.axis_index("y"), lax.axis_index("c")
        me = 4 * mx + 2 * my + mc
        mine = [pltpu.make_async_copy(g_refs[a].at[me], out_refs[a].at[me], local_sems.at[a]) for a in range(na)]
        for cp in mine:
            cp.start()
        copies = []
        for k in range(1, N_DEV):
            px, py, pc = mx ^ (k >> 2), my ^ ((k >> 1) & 1), mc ^ (k & 1)
            peer = 4 * px + 2 * py + pc
            for a in range(na):
                copies.append(pltpu.make_async_remote_copy(
                    src_ref=g_refs[a].at[peer], dst_ref=out_refs[a].at[me], send_sem=send_sems.at[a, k - 1],
                    recv_sem=recv_sems.at[a, k - 1], device_id=(px, py, pc), device_id_type=MESH))
        for cp in copies:
            cp.start()
        for cp in copies:
            cp.wait_recv()
        for cp in copies:
            cp.wait_send()
        for cp in mine:
            cp.wait()

    hbm = pl.BlockSpec(memory_space=pl.ANY)
    return pl.pallas_call(
        body, name=name,
        out_shape=[jax.ShapeDtypeStruct(a.shape, a.dtype) for a in arrs],
        in_specs=[hbm] * na, out_specs=[hbm] * na,
        scratch_shapes=[pltpu.SemaphoreType.DMA((na, 7)), pltpu.SemaphoreType.DMA((na, 7)),
                        pltpu.SemaphoreType.DMA((na,))],
    )(*arrs)


def _sum_slots(name, g, tr):
    _, rws, cols = g.shape

    def body(g_ref, o_ref):
        acc = g_ref[0].astype(F32)
        for j in range(1, N_DEV):
            acc = acc + g_ref[j].astype(F32)
        o_ref[...] = acc

    return pl.pallas_call(
        body, name=name, grid=(rws // tr,),
        in_specs=[pl.BlockSpec((N_DEV, tr, cols), lambda i: (0, i, 0))],
        out_specs=pl.BlockSpec((tr, cols), lambda i: (i, 0)),
        out_shape=jax.ShapeDtypeStruct((rws, cols), F32),
        compiler_params=_cparams(dimension_semantics=("parallel",)),
    )(g)


def _adam_math(wv, gv, mv, vv):
    mn = ADAM_B1 * mv + (1.0 - ADAM_B1) * gv
    vn = ADAM_B2 * vv + (1.0 - ADAM_B2) * jnp.square(gv)
    m_hat = mn / (1.0 - ADAM_B1 ** ADAM_STEP)
    v_hat = vn / (1.0 - ADAM_B2 ** ADAM_STEP)
    delta = -ADAM_LR * (m_hat / (jnp.sqrt(v_hat) + ADAM_EPS) + ADAM_WD * wv)
    return delta, mn, vn


def _adamw(name, w, g, m, v, tr):
    return _rowwise(name, _adam_math, [w, g, m, v], [], [(LANES, F32)] * 3, tr=tr)


def _adamw_slots(name, recv, rb, cb, cw, w, m, v, tr):
    nr, nc = w.shape

    def body(g_ref, w_ref, m_ref, v_ref, go_ref, d_ref, mo_ref, vo_ref):
        acc = g_ref[0].astype(F32)
        for s in range(1, N_DEV):
            acc = acc + g_ref[s].astype(F32)
        g = acc[:, :nc]
        go_ref[...] = g
        d_ref[...], mo_ref[...], vo_ref[...] = _adam_math(w_ref[...], g, m_ref[...], v_ref[...])

    nat = pl.BlockSpec((tr, nc), lambda i: (i, 0))
    return pl.pallas_call(
        body, name=name, grid=(nr // tr,),
        in_specs=[pl.BlockSpec((N_DEV, tr, cw), lambda i: (0, rb + i, cb)), nat, nat, nat],
        out_specs=[nat] * 4, out_shape=[jax.ShapeDtypeStruct((nr, nc), F32)] * 4,
        compiler_params=_cparams(dimension_semantics=("parallel",)),
    )(recv, w, m, v)


def _local_blocks(w):
    pad_cols = lambda a: jnp.pad(a, ((0, 0), (0, FF_PAD - FF_SHARD)))
    pad_rows = lambda a: jnp.pad(a, ((0, FF_PAD - FF_SHARD), (0, 0)))
    gate_up = lambda tag: jnp.concatenate([pad_cols(w[tag + "_w_gate"]), pad_cols(w[tag + "_w_up"])], axis=1)
    blocks = {
        "ffn1_gu": gate_up("ffn1"), "ffn1_d": pad_rows(w["ffn1_w_down"]), "w_in": w["w_in"],
        "lora": jnp.concatenate([w["rwkv_w2"], w["rwkv_a2"], w["rwkv_g2"]], axis=0),
        "br": jnp.concatenate([w["w_br_rwkv"], w["w_br_attn"], w["ple_w_proj"]], axis=0),
        "w_out": w["w_out"], "ffn2_gu": gate_up("ffn2"), "ffn2_d": pad_rows(w["ffn2_w_down"]),
        "ple_gate": w["ple_w_gate"],
    }
    return {n: a.astype(BF16) for n, a in blocks.items()}


ADAM_PLAN = (
    ("ffn1_w_gate", "ffn1_gu", 0, 0, FF_PAD, 256), ("ffn1_w_up", "ffn1_gu", 0, 1, FF_PAD, 256),
    ("ffn1_w_down", "ffn1_d", 0, 0, D_MODEL, FF_SHARD // 2), ("w_in", "w_in", 0, 0, IN_SHARD, 256),
    ("rwkv_w2", "lora", 0, 0, HEAD_DIM, 64), ("rwkv_a2", "lora", 1, 0, HEAD_DIM, 64),
    ("rwkv_g2", "lora", 2, 0, HEAD_DIM, 64),
    ("w_br_rwkv", "br", 0, 0, OUT_SHARD, 256), ("w_br_attn", "br", 2, 0, OUT_SHARD, 256),
    ("ple_w_proj", "br", 3, 0, OUT_SHARD, 256), ("w_out", "w_out", 0, 0, D_MODEL, OUT_SHARD),
    ("ffn2_w_gate", "ffn2_gu", 0, 0, FF_PAD, 256), ("ffn2_w_up", "ffn2_gu", 0, 1, FF_PAD, 256),
    ("ffn2_w_down", "ffn2_d", 0, 0, D_MODEL, FF_SHARD // 2), ("ple_w_gate", "ple_gate", 0, 0, D_MODEL, OUT_SHARD),
)


def _pack_small(arrs, rows):
    flat = jnp.concatenate([a.reshape(-1) for a in arrs])
    return jnp.pad(flat, (0, rows * LANES - flat.shape[0])).reshape(rows, LANES)


def _unpack_small(flat, like):
    flat = flat.reshape(-1)
    out, off = [], 0
    for a in like:
        out.append(flat[off:off + a.size].reshape(a.shape))
        off += a.size
    return out


def _local_step(x, p, pos, target, sm, wg):
    t = x.shape[0]
    hv = t * RWKV_HEADS

    w_d1 = wg["ffn1_d"].reshape(FF_HID, D_MODEL)
    w_d2 = wg["ffn2_d"].reshape(FF_HID, D_MODEL)
    w_out = wg["w_out"].reshape(D_MODEL, D_MODEL)
    w_pg = wg["ple_gate"].reshape(D_MODEL, D_MODEL)
    full_cols = lambda blk: blk.transpose(1, 0, 2).reshape(blk.shape[1], N_DEV * blk.shape[2])
    lora_w2 = full_cols(wg["lora"][:, :DECAY_LORA])
    lora_a2 = full_cols(wg["lora"][:, DECAY_LORA:DECAY_LORA + ICLR_LORA])
    lora_g2 = full_cols(wg["lora"][:, DECAY_LORA + ICLR_LORA:])
    x1, ffn1_saved = _ffn_fwd("ffn1", x, sm["ffn1_norm"], wg["ffn1_gu"], w_d1)
    h2 = _norm_fwd("mix_norm", x1, sm["mix_norm"])
    z = _mmc_nn("w_in", h2, wg["w_in"], 0, 0, IN_SHARD)
    z_r = z[:, :RWKV_COLS]
    z_g = z[:, RWKV_COLS + 3 * ATTN_DIM:]

    r, k, v, lo, gd = _shift_fwd(z_r, sm["rwkv_mu"])
    zero_lo = jnp.zeros((DECAY_LORA, RWKV_DIM), BF16)
    w2p = jnp.concatenate([lora_w2, zero_lo], axis=0).astype(F32)
    a2p = jnp.concatenate([zero_lo, lora_a2], axis=0).astype(F32)
    pre_params = [sm["rwkv_w0"], w2p, sm["rwkv_a0"], a2p, lora_g2.astype(F32), sm["rwkv_k_k"], sm["rwkv_k_a"]]
    wide = [(RWKV_DIM, F32)]
    k2, kk, a, decay, g = _rowwise("rwkv_pre", _f_pre, [k, lo, gd], pre_params, wide * 5)
    as_heads = lambda u: u.reshape(hv, HEAD_DIM)
    kkn, b = _rowwise("rwkv_kk", _f_kk, [as_heads(kk), as_heads(a)], [], [(HEAD_DIM, F32)] * 2, tr=512)
    scan_in = [u.reshape(t, RWKV_HEADS, HEAD_DIM) for u in (r, decay, k2, kkn, b)]
    v_rows = _to_v_rows(v)
    y_rows, states = _wkv_fwd(*scan_in, v_rows)
    y = _from_v_rows(y_rows)
    post_params = [sm["rwkv_gn_w"].reshape(RWKV_HEADS, HEAD_DIM), sm["rwkv_gn_b"].reshape(RWKV_HEADS, HEAD_DIM),
                   sm["rwkv_r_k"].reshape(RWKV_HEADS, HEAD_DIM)]
    post_rows = [as_heads(y), as_heads(r), as_heads(k2), as_heads(v), as_heads(g)]
    y_rwkv = _rowwise("rwkv_post", lambda *av: (_f_post(*av),), post_rows, post_params, [(HEAD_DIM, F32)], tr=512)[0]
    y_rwkv = y_rwkv.reshape(t, RWKV_DIM)

    inv_freq = 1.0 / (ROPE_THETA ** (jnp.arange(0, HEAD_DIM, 2, dtype=F32) / HEAD_DIM))
    freq2 = jnp.tile(inv_freq, 2 * PAIR // HEAD_DIM).reshape(1, PAIR)
    half = jnp.ones((HEAD_DIM // 2,), F32)
    sign2 = jnp.tile(jnp.concatenate([-half, half]), PAIR // HEAD_DIM).reshape(1, PAIR)

    def rope_table(posv, fr, sg):
        ang = posv * fr
        return (jnp.concatenate([jnp.cos(ang), jnp.sin(ang) * sg], axis=1),)
    tab = _rowwise("rope_table", rope_table, [pos.astype(F32).reshape(t, 1)], [freq2, sign2], [(2 * PAIR, F32)])[0]
    qk = _qk_prep(z, tab, sm["q_norm"], sm["k_norm"])
    o_all, lse_all = _attn_fwd(qk, z)
    gw = HEADS_PER_GROUP * HEAD_DIM

    def by_group(ov, lv):
        return [ov[:, i * gw:(i + 1) * gw] for i in range(3)] + [lv[:, i * gw:(i + 1) * gw] for i in range(3)]
    y_attn = _rowwise("attn_comb", lambda ov, lv: (_f_comb(*by_group(ov, lv)),), [o_all, lse_all], [], [(gw, F32)])[0]

    u_r = _mmc_nn("br_rwkv", y_rwkv, wg["br"], 0, 0, OUT_SHARD)
    u_a = _mmc_nn("br_attn", y_attn, wg["br"], 2, 0, OUT_SHARD)

    def f_merge(zgr, zga, ur, ua):
        return _sigmoid(zgr) * ur + _sigmoid(zga) * ua
    merged = _rowwise("merge", lambda zg, ur, ua: (f_merge(zg[:, :D_MODEL], zg[:, D_MODEL:], ur, ua),),
                      [z_g, u_r, u_a], [], [(D_MODEL, BF16)])[0]
    x2 = _mm("w_out", merged, w_out, "nn", res=x1)
    x3, ffn2_saved = _ffn_fwd("ffn2", x2, sm["ffn2_norm"], wg["ffn2_gu"], w_d2)

    hn = _norm_fwd("ple_norm", x3, sm["ple_norm"])
    gz = _mm("ple_gate", hn, w_pg, "nn")
    pp = _mmc_nn("ple_proj", p, wg["br"], 3, 0, OUT_SHARD)

    def f_head(x3v, gzv, ppv, tg):
        sg = _sigmoid(gzv)
        err = x3v + sg * ppv - tg
        part = 0.5 * jnp.sum(jnp.mean(err * err, axis=-1, keepdims=True))
        dx4 = err * (1.0 / D_MODEL)
        return dx4, dx4 * ppv * sg * (1.0 - sg), dx4 * sg, jnp.full((1, LANES), part, F32)
    dx4, dgz, dpp, loss_row = _rowwise("ple_loss", f_head, [x3, gz, pp, target], [],
                                       [(D_MODEL, F32), (D_MODEL, BF16), (D_MODEL, BF16)], [(1, LANES)])
    loss = loss_row[0, 0]

    gs, gm = {}, {}
    row_blocks = lambda g: g.reshape(N_DEV, g.shape[0] // N_DEV, g.shape[1])
    dhn = _mm("ple_dhn", dgz, w_pg, "nt")
    gm["ple_gate"] = row_blocks(_mm("ple_dwgate", hn, dgz, "tn", out_dtype=BF16))
    d_pproj = _mmc_tn("ple_dwproj", p, dpp, OUT_SHARD)
    dx3, gs["ple_norm"] = _norm_bwd("ple_dnorm", x3, sm["ple_norm"], dhn, dx4)

    dx2, gs["ffn2_norm"], gm["ffn2_gu"], gm["ffn2_d"] = _ffn_bwd(
        "ffn2", x2, sm["ffn2_norm"], wg["ffn2_gu"], w_d2, ffn2_saved, dx3)

    dmerged = _mm("w_out_dmerged", dx2, w_out, "nt")
    gm["w_out"] = row_blocks(_mm("w_out_dw", merged, dx2, "tn", out_dtype=BF16))

    def merge_bwd(zg, ur, ua, dm):
        _, vjp = jax.vjp(f_merge, zg[:, :D_MODEL], zg[:, D_MODEL:], ur, ua)
        dzr, dza, dur, dua = vjp(dm)
        return jnp.concatenate([dzr, dza], axis=1), dur, dua
    dz_g, du_r, du_a = _rowwise("merge_bwd", merge_bwd, [z_g, u_r, u_a, dmerged], [],
                                [(2 * D_MODEL, BF16), (D_MODEL, BF16), (D_MODEL, BF16)])
    dy_rwkv = _mmc_nt("br_rwkv_dy", du_r, wg["br"], 0, 0, OUT_SHARD, RWKV_DIM)
    dy_attn = _mmc_nt("br_attn_dy", du_a, wg["br"], 2, 0, OUT_SHARD, HEADS_PER_GROUP * HEAD_DIM)
    gm["br"] = jnp.concatenate([_mmc_tn("br_rwkv_dw", y_rwkv, du_r, OUT_SHARD),
                                _mmc_tn("br_attn_dw", y_attn, du_a, OUT_SHARD), d_pproj], axis=1)

    def comb_bwd(ov, lv, dyv):
        _, vjp = jax.vjp(_f_comb, *by_group(ov, lv))
        d = vjp(dyv)
        return jnp.concatenate(d[:3], axis=1), jnp.concatenate(d[3:], axis=1)
    do_all, dl_all = _rowwise("attn_comb_bwd", comb_bwd, [o_all, lse_all, dy_attn], [],
                              [(ATTN_DIM, F32), (ATTN_DIM, F32)])
    dq_all, dk_all, dv_all = _attn_bwd(qk, z, do_all, dl_all)
    dqk_raw, gs["q_norm"], gs["k_norm"] = _qk_prep_bwd(z, tab, sm["q_norm"], sm["k_norm"], dq_all, dk_all)

    def post_bwd(yv, rv, k2v, vv, gv, dv_, gnw, gnb, rk):
        _, vjp = jax.vjp(_f_post, yv, rv, k2v, vv, gv, gnw, gnb, rk)
        return vjp(dv_)
    head_acc = (RWKV_HEADS, HEAD_DIM)
    dy, dr1, dk2a, dv1, dg, d_gnw, d_gnb, d_rk = _rowwise(
        "rwkv_post_bwd", post_bwd, post_rows + [as_heads(dy_rwkv)], post_params, [(HEAD_DIM, F32)] * 5,
        [head_acc] * 3, tr=512)
    gs["rwkv_gn_w"], gs["rwkv_gn_b"], gs["rwkv_r_k"] = d_gnw, d_gnb, d_rk
    dr2, ddecay, dk2b, dkkn, db, dv_rows = _wkv_bwd(*scan_in, v_rows, states, _to_v_rows(dy.reshape(t, RWKV_DIM)))
    dr2, ddecay, dk2b, dkkn, db = [u.reshape(t, RWKV_DIM) for u in (dr2, ddecay, dk2b, dkkn, db)]
    dv2 = _from_v_rows(dv_rows)

    def kk_bwd(kkv, av, dkknv, dbv):
        _, vjp = jax.vjp(_f_kk, kkv, av)
        return vjp((dkknv, dbv))
    dkk, da = _rowwise("rwkv_kk_bwd", kk_bwd, [as_heads(kk), as_heads(a), as_heads(dkkn), as_heads(db)], [],
                       [(HEAD_DIM, F32)] * 2, tr=512)

    def pre_bwd(kv, lov, gdv, dk2x, dk2y, dkkv, dav, ddec, dgv, w0, w2p_, a0, a2p_, g2, k_k, k_a):
        _, vjp = jax.vjp(_f_pre, kv, lov, gdv, w0, w2p_, a0, a2p_, g2, k_k, k_a)
        return vjp((dk2x + dk2y, dkkv, dav, ddec, dgv))
    lora_acc = (DECAY_LORA + ICLR_LORA, RWKV_DIM)
    dk, dlo, dgd, d_w0, d_w2p, d_a0, d_a2p, d_g2, d_kk, d_ka = _rowwise(
        "rwkv_pre_bwd", pre_bwd,
        [k, lo, gd, dk2a.reshape(t, RWKV_DIM), dk2b, dkk.reshape(t, RWKV_DIM), da.reshape(t, RWKV_DIM), ddecay,
         dg.reshape(t, RWKV_DIM)],
        pre_params, [(RWKV_DIM, F32), (LANES, F32), (LANES, F32)],
        [(1, RWKV_DIM), lora_acc, (1, RWKV_DIM), lora_acc, (GATE_LORA, RWKV_DIM), (1, RWKV_DIM), (1, RWKV_DIM)])
    gs["rwkv_w0"], gs["rwkv_a0"], gs["rwkv_k_k"], gs["rwkv_k_a"] = d_w0, d_a0, d_kk, d_ka
    col_blocks = lambda g: g.reshape(g.shape[0], N_DEV, g.shape[1] // N_DEV).transpose(1, 0, 2)
    gm["lora"] = jnp.concatenate([col_blocks(d_w2p[:DECAY_LORA]), col_blocks(d_a2p[DECAY_LORA:]), col_blocks(d_g2)],
                                 axis=1).astype(BF16)
    add2 = lambda u, w_: _rowwise("rwkv_add", lambda p_, q_: (p_ + q_,), [u, w_], [], [(RWKV_DIM, F32)])[0]
    dz_r, gs["rwkv_mu"] = _shift_bwd(z_r, sm["rwkv_mu"], add2(dr1.reshape(t, RWKV_DIM), dr2), dk,
                                     add2(dv1.reshape(t, RWKV_DIM), dv2), dlo, dgd)

    dz = jnp.concatenate([dz_r, dqk_raw, dv_all.astype(BF16), dz_g], axis=1)
    dh2 = _mmc_nt("w_in_dh", dz, wg["w_in"], 0, 0, IN_SHARD, D_MODEL)
    gm["w_in"] = _mmc_tn("w_in_dw", h2, dz, IN_SHARD)
    dx1, gs["mix_norm"] = _norm_bwd("mix_dnorm", x1, sm["mix_norm"], dh2, dx2)

    dx0, gs["ffn1_norm"], gm["ffn1_gu"], gm["ffn1_d"] = _ffn_bwd(
        "ffn1", x, sm["ffn1_norm"], wg["ffn1_gu"], w_d1, ffn1_saved, dx1)
    return loss, dx0, gm, gs


def kernel(x, p, positions, ffn1_norm, ffn1_w_gate, ffn1_w_up, ffn1_w_down, mix_norm, w_in, rwkv_mu, rwkv_w0, rwkv_w2, rwkv_a0, rwkv_a2, rwkv_g2, rwkv_k_k, rwkv_k_a, rwkv_r_k, rwkv_gn_w, rwkv_gn_b, q_norm, k_norm, w_br_rwkv, w_br_attn, w_out, ffn2_norm, ffn2_w_gate, ffn2_w_up, ffn2_w_down, ple_norm, ple_w_gate, ple_w_proj, loss_target, m_ffn1_norm, m_ffn1_w_gate, m_ffn1_w_up, m_ffn1_w_down, m_mix_norm, m_w_in, m_rwkv_mu, m_rwkv_w0, m_rwkv_w2, m_rwkv_a0, m_rwkv_a2, m_rwkv_g2, m_rwkv_k_k, m_rwkv_k_a, m_rwkv_r_k, m_rwkv_gn_w, m_rwkv_gn_b, m_q_norm, m_k_norm, m_w_br_rwkv, m_w_br_attn, m_w_out, m_ffn2_norm, m_ffn2_w_gate, m_ffn2_w_up, m_ffn2_w_down, m_ple_norm, m_ple_w_gate, m_ple_w_proj, v_ffn1_norm, v_ffn1_w_gate, v_ffn1_w_up, v_ffn1_w_down, v_mix_norm, v_w_in, v_rwkv_mu, v_rwkv_w0, v_rwkv_w2, v_rwkv_a0, v_rwkv_a2, v_rwkv_g2, v_rwkv_k_k, v_rwkv_k_a, v_rwkv_r_k, v_rwkv_gn_w, v_rwkv_gn_b, v_q_norm, v_k_norm, v_w_br_rwkv, v_w_br_attn, v_w_out, v_ffn2_norm, v_ffn2_w_gate, v_ffn2_w_up, v_ffn2_w_down, v_ple_norm, v_ple_w_gate, v_ple_w_proj):
    args = locals()
    w = {n: args[n][0] for n in WEIGHTS}
    m = {n: args["m_" + n][0] for n in WEIGHTS}
    v = {n: args["v_" + n][0] for n in WEIGHTS}

    w_loc = _local_blocks(w)
    names = list(w_loc)
    wg = dict(zip(names, _all_gather_hbm("all_gather_weights", [w_loc[n] for n in names])))

    sm = {n: w[n].reshape(1, -1) for n in SMALL}
    loss_part, dx, gm, gs = _local_step(x[0], p[0, 0], positions[0], loss_target[0], sm, wg)
    loss = lax.psum(loss_part, ("x", "y", "c"))

    recv = dict(zip(names, _all_to_all_hbm("all_to_all_grads", [gm[n] for n in names])))
    small_like = [w[n] for n in SMALL]
    small_rows = 80
    gs_all = _all_gather_vmem(_pack_small([gs[n] for n in SMALL], small_rows))
    gs_sum = _sum_slots("sum_small_grads", gs_all, small_rows)

    res = {}
    for n, src, rb, cb, cw, tr in ADAM_PLAN:
        outs4 = _adamw_slots("adamw_" + n, recv[src], rb, cb, cw, w[n], m[n], v[n], tr)
        for tag, a in zip(("grad", "delta", "new_m", "new_v"), outs4):
            res[tag, n] = a[None]
    d_s, m_s, v_s = _adamw("adamw_small", _pack_small(small_like, small_rows), gs_sum,
                           _pack_small([m[n] for n in SMALL], small_rows),
                           _pack_small([v[n] for n in SMALL], small_rows), small_rows)
    for tag, small in (("grad", gs_sum), ("delta", d_s), ("new_m", m_s), ("new_v", v_s)):
        for n, a in zip(SMALL, _unpack_small(small, small_like)):
            res[tag, n] = a[None]
    outs = [loss, dx[None]]
    for tag in ("grad", "delta", "new_m", "new_v"):
        outs += [res[tag, n] for n in WEIGHTS]
    return tuple(outs)
```

```python
import functools

import jax
import jax.numpy as jnp
from jax import lax
from jax.experimental import pallas as pl
from jax.experimental.pallas import tpu as pltpu

F32, BF16 = jnp.float32, jnp.bfloat16
MESH = pl.DeviceIdType.MESH
N_DEV = 8
LANES = 128
VMEM_LIMIT = 56 * 1024 * 1024

D_MODEL = 1024
PLE_DIM = 256
HEAD_DIM = 64
RWKV_HEADS = 8
RWKV_DIM = RWKV_HEADS * HEAD_DIM
DECAY_LORA = 64
ICLR_LORA = 64
GATE_LORA = 128
GN_EPS = 64e-5
ATTN_GROUPS = ((128, 1), (512, 4), (2048, 16))
HEADS_PER_GROUP = 4
ATTN_HEADS = HEADS_PER_GROUP * len(ATTN_GROUPS)
ATTN_DIM = ATTN_HEADS * HEAD_DIM
BAND_BLOCK = 128
ROPE_THETA = 10000.0
NEG_INF = -1e30
D_FF = 2816
RMS_EPS = 1e-6
RWKV_COLS = 3 * RWKV_DIM + DECAY_LORA + ICLR_LORA + GATE_LORA
ADAM_LR = 0.001
ADAM_B1 = 0.9
ADAM_B2 = 0.999
ADAM_EPS = 1e-08
ADAM_WD = 0.01
ADAM_STEP = 10

V_LO = LANES // RWKV_HEADS
V_HI = HEAD_DIM // V_LO
SCAN_CHUNK = 32

FF_SHARD = D_FF // N_DEV
FF_PAD = -(-FF_SHARD // LANES) * LANES
FF_HID = N_DEV * FF_PAD
IN_SHARD = 6144 // N_DEV
OUT_SHARD = D_MODEL // N_DEV

SMALL = ("ffn1_norm", "mix_norm", "rwkv_mu", "rwkv_w0", "rwkv_a0", "rwkv_k_k", "rwkv_k_a", "rwkv_r_k",
         "rwkv_gn_w", "rwkv_gn_b", "q_norm", "k_norm", "ffn2_norm", "ple_norm")
WEIGHTS = ("ffn1_norm", "ffn1_w_gate", "ffn1_w_up", "ffn1_w_down", "mix_norm", "w_in", "rwkv_mu", "rwkv_w0",
           "rwkv_w2", "rwkv_a0", "rwkv_a2", "rwkv_g2", "rwkv_k_k", "rwkv_k_a", "rwkv_r_k", "rwkv_gn_w",
           "rwkv_gn_b", "q_norm", "k_norm", "w_br_rwkv", "w_br_attn", "w_out", "ffn2_norm", "ffn2_w_gate",
           "ffn2_w_up", "ffn2_w_down", "ple_norm", "ple_w_gate", "ple_w_proj")


def _cparams(**kw):
    return pltpu.CompilerParams(vmem_limit_bytes=VMEM_LIMIT, **kw)


def _tile(n, cap):
    best = None
    for t in range(LANES, min(n, cap) + 1, LANES):
        if n % t == 0:
            best = t
    return best if best is not None else n


@jax.custom_vjp
def _bdot(a, w):
    return jnp.dot(a.astype(BF16), w.astype(BF16), preferred_element_type=F32)


def _bdot_fwd(a, w):
    return _bdot(a, w), (a, w)


def _bdot_bwd(res, g):
    a, w = res
    gb = g.astype(BF16)
    da = lax.dot_general(gb, w.astype(BF16), (((1,), (1,)), ((), ())), preferred_element_type=F32)
    dw = lax.dot_general(a.astype(BF16), gb, (((0,), (0,)), ((), ())), preferred_element_type=F32)
    return da.astype(a.dtype), dw.astype(w.dtype)


_bdot.defvjp(_bdot_fwd, _bdot_bwd)


@jax.custom_vjp
def _bdot_nt(a, b):
    return lax.dot_general(a.astype(BF16), b.astype(BF16), (((1,), (1,)), ((), ())), preferred_element_type=F32)


def _bdot_nt_fwd(a, b):
    return _bdot_nt(a, b), (a, b)


def _bdot_nt_bwd(res, g):
    a, b = res
    gb = g.astype(BF16)
    da = jnp.dot(gb, b.astype(BF16), preferred_element_type=F32)
    db = lax.dot_general(gb, a.astype(BF16), (((0,), (0,)), ((), ())), preferred_element_type=F32)
    return da.astype(a.dtype), db.astype(b.dtype)


_bdot_nt.defvjp(_bdot_nt_fwd, _bdot_nt_bwd)


def _mm(name, a, b, mode, out_dtype=F32, res=None, scale=None):
    if mode == "nn":
        (m, k), n = a.shape, b.shape[1]
    elif mode == "nt":
        (m, k), n = a.shape, b.shape[0]
    else:
        (k, m), n = a.shape, b.shape[1]
    tm, tn = _tile(m, 512), _tile(n, 512)
    a_spec = pl.BlockSpec((k, tm), lambda i, j: (0, i)) if mode == "tn" else pl.BlockSpec((tm, k), lambda i, j: (i, 0))
    b_spec = pl.BlockSpec((tn, k), lambda i, j: (j, 0)) if mode == "nt" else pl.BlockSpec((k, tn), lambda i, j: (0, j))
    dims = {"nn": ((1,), (0,)), "nt": ((1,), (1,)), "tn": ((0,), (0,))}[mode]
    o_spec = pl.BlockSpec((tm, tn), lambda i, j: (i, j))
    ins, in_specs = [a, b], [a_spec, b_spec]
    if res is not None:
        ins.append(res)
        in_specs.append(o_spec)

    def body(*refs):
        acc = lax.dot_general(refs[0][...].astype(BF16), refs[1][...].astype(BF16), (dims, ((), ())),
                              preferred_element_type=F32)
        if scale is not None:
            acc = acc * scale
        if res is not None:
            acc = acc + refs[2][...].astype(F32)
        refs[-1][...] = acc.astype(refs[-1].dtype)

    return pl.pallas_call(
        body, name=name, grid=(m // tm, n // tn), in_specs=in_specs, out_specs=o_spec,
        out_shape=jax.ShapeDtypeStruct((m, n), out_dtype),
        compiler_params=_cparams(dimension_semantics=("parallel", "parallel")),
    )(*ins)


def _mmc_nn(name, a, wb, ki, ci, n, out_dtype=F32):
    m, k = a.shape
    tm = _tile(m, 512)

    def body(a_ref, w_ref, o_ref):
        o_ref[...] = jnp.dot(a_ref[...].astype(BF16), w_ref[...], preferred_element_type=F32).astype(o_ref.dtype)

    return pl.pallas_call(
        body, name=name, grid=(m // tm, N_DEV),
        in_specs=[pl.BlockSpec((tm, k), lambda i, j: (i, 0)), pl.BlockSpec((None, k, n), lambda i, j: (j, ki, ci))],
        out_specs=pl.BlockSpec((tm, n), lambda i, j: (i, j)),
        out_shape=jax.ShapeDtypeStruct((m, N_DEV * n), out_dtype),
        compiler_params=_cparams(dimension_semantics=("parallel", "parallel")),
    )(a, wb)


def _mmc_nt(name, a, wb, ki, ci, n, k, res=None):
    m = a.shape[0]
    tm = _tile(m, 512)
    o_spec = pl.BlockSpec((tm, k), lambda i, j: (i, 0))
    ins = [a, wb] + ([res] if res is not None else [])
    in_specs = [pl.BlockSpec((tm, n), lambda i, j: (i, j)), pl.BlockSpec((None, k, n), lambda i, j: (j, ki, ci))]
    in_specs += [o_spec] if res is not None else []

    def body(*refs):
        a_ref, w_ref, o_ref = refs[0], refs[1], refs[-1]
        acc = lax.dot_general(a_ref[...].astype(BF16), w_ref[...], (((1,), (1,)), ((), ())),
                              preferred_element_type=F32)

        @pl.when(pl.program_id(1) == 0)
        def _():
            o_ref[...] = acc + refs[2][...] if res is not None else acc

        @pl.when(pl.program_id(1) != 0)
        def _():
            o_ref[...] += acc

    return pl.pallas_call(
        body, name=name, grid=(m // tm, N_DEV), in_specs=in_specs, out_specs=o_spec,
        out_shape=jax.ShapeDtypeStruct((m, k), F32),
        compiler_params=_cparams(dimension_semantics=("parallel", "arbitrary")),
    )(*ins)


def _mmc_tn(name, x, dy, n):
    m, k = x.shape
    tk = _tile(k, 512)

    def body(x_ref, dy_ref, o_ref):
        o_ref[...] = lax.dot_general(x_ref[...].astype(BF16), dy_ref[...].astype(BF16), (((0,), (0,)), ((), ())),
                                     preferred_element_type=F32).astype(o_ref.dtype)

    return pl.pallas_call(
        body, name=name, grid=(N_DEV, k // tk),
        in_specs=[pl.BlockSpec((m, tk), lambda j, i: (0, i)), pl.BlockSpec((m, n), lambda j, i: (0, j))],
        out_specs=pl.BlockSpec((None, tk, n), lambda j, i: (j, i, 0)),
        out_shape=jax.ShapeDtypeStruct((N_DEV, k, n), BF16),
        compiler_params=_cparams(dimension_semantics=("parallel", "parallel")),
    )(x, dy)


def _rowwise(name, fn, rows, params, out_rows, out_accs=(), tr=256):
    r = rows[0].shape[0]
    in_specs = [pl.BlockSpec((tr, a.shape[1]), lambda i: (i, 0)) for a in rows]
    in_specs += [pl.BlockSpec(p.shape, lambda i, nd=p.ndim: (0,) * nd) for p in params]
    out_shape = [jax.ShapeDtypeStruct((r, c), dt) for c, dt in out_rows]
    out_shape += [jax.ShapeDtypeStruct(s, F32) for s in out_accs]
    out_specs = [pl.BlockSpec((tr, c), lambda i: (i, 0)) for c, _ in out_rows]
    out_specs += [pl.BlockSpec(s, lambda i, nd=len(s): (0,) * nd) for s in out_accs]
    n_in, n_ro = len(rows) + len(params), len(out_rows)

    def body(*refs):
        res = fn(*[ref[...] for ref in refs[:n_in]])
        outs = refs[n_in:]
        for o, v in zip(outs[:n_ro], res[:n_ro]):
            o[...] = v.astype(o.dtype)
        for o, v in zip(outs[n_ro:], res[n_ro:]):
            _accumulate(o, v)

    return pl.pallas_call(
        body, name=name, grid=(r // tr,), in_specs=in_specs, out_specs=out_specs, out_shape=out_shape,
        compiler_params=_cparams(dimension_semantics=("arbitrary",)),
    )(*rows, *params)


def _accumulate(o_ref, v):
    @pl.when(pl.program_id(0) == 0)
    def _():
        o_ref[...] = v

    @pl.when(pl.program_id(0) != 0)
    def _():
        o_ref[...] += v


def _rms(x, g):
    return x * lax.rsqrt(jnp.mean(x * x, axis=-1, keepdims=True) + RMS_EPS) * g


def _sigmoid(x):
    return jax.nn.sigmoid(x)


def _softplus(x):
    return jnp.maximum(x, 0.0) + jnp.log1p(jnp.exp(-jnp.abs(x)))


def _norm_fwd(name, x, g):
    return _rowwise(name, lambda xv, gv: (_rms(xv, gv),), [x], [g], [(x.shape[1], BF16)])[0]


def _norm_bwd(name, x, g, dh, dres):
    def fn(xv, dhv, drv, gv):
        _, vjp = jax.vjp(_rms, xv, gv)
        dx, dg = vjp(dhv)
        return dx + drv, dg
    return _rowwise(name, fn, [x, dh, dres], [g], [(x.shape[1], F32)], [g.shape])


def _f_act(gate, up):
    return gate * _sigmoid(gate) * up


def _gate_up(guv, j):
    base = j * 2 * FF_PAD
    return guv[:, base:base + FF_PAD], guv[:, base + FF_PAD:base + 2 * FF_PAD]


def _ffn_fwd(tag, x, norm, w_gu, w_down):
    h = _norm_fwd(tag + "_norm", x, norm)
    gu = _mmc_nn(tag + "_gu", h, w_gu, 0, 0, 2 * FF_PAD)

    def act(guv):
        return (jnp.concatenate([_f_act(*_gate_up(guv, j)) for j in range(N_DEV)], axis=1),)
    a = _rowwise(tag + "_act", act, [gu], [], [(FF_HID, BF16)])[0]
    out = _mm(tag + "_down", a, w_down, "nn", res=x, scale=0.5)
    return out, (h, gu, a)


def _ffn_bwd(tag, x, norm, w_gu, w_down, saved, dout):
    h, gu, a = saved
    da = _mm(tag + "_dact", dout, w_down, "nt", scale=0.5)
    d_wdown = _mm(tag + "_dwdown", a, dout, "tn", out_dtype=BF16, scale=0.5)

    def act_bwd(guv, dav):
        outs = []
        for j in range(N_DEV):
            _, vjp = jax.vjp(_f_act, *_gate_up(guv, j))
            outs += list(vjp(dav[:, j * FF_PAD:(j + 1) * FF_PAD]))
        return (jnp.concatenate(outs, axis=1),)
    dgu = _rowwise(tag + "_dgu", act_bwd, [gu, da], [], [(2 * FF_HID, BF16)])[0]
    dh = _mmc_nt(tag + "_dh", dgu, w_gu, 0, 0, 2 * FF_PAD, D_MODEL)
    d_wgu = _mmc_tn(tag + "_dwgu", h, dgu, 2 * FF_PAD)
    dx, dnorm = _norm_bwd(tag + "_dnorm", x, norm, dh, dout)
    return dx, dnorm, d_wgu, d_wdown.reshape(N_DEV, FF_PAD, D_MODEL)


def _shift_fwd(z, mu):
    t, c = z.shape
    tr = 256

    def body(z_ref, zp_ref, mu_ref, r_ref, k_ref, v_ref, lo_ref, gd_ref):
        zv = z_ref[...]
        prev = zp_ref[7:8, :] * jnp.where(pl.program_id(0) == 0, 0.0, 1.0)
        row = lax.broadcasted_iota(jnp.int32, zv.shape, 0)
        zsh = jnp.where(row == 0, prev, pltpu.roll(zv, 1, 0))
        zs = zv + (zsh - zv) * mu_ref[...]
        r_ref[...] = zs[:, 0:512]
        k_ref[...] = zs[:, 512:1024]
        v_ref[...] = zs[:, 1024:1536]
        lo_ref[...] = zs[:, 1536:1664]
        gd_ref[...] = zs[:, 1664:1792]

    widths = (512, 512, 512, 128, 128)
    return pl.pallas_call(
        body, name="rwkv_shift", grid=(t // tr,),
        in_specs=[pl.BlockSpec((tr, c), lambda i: (i, 0)),
                  pl.BlockSpec((8, c), lambda i: (jnp.maximum(i * (tr // 8) - 1, 0), 0)),
                  pl.BlockSpec((1, c), lambda i: (0, 0))],
        out_specs=[pl.BlockSpec((tr, w), lambda i: (i, 0)) for w in widths],
        out_shape=[jax.ShapeDtypeStruct((t, w), F32) for w in widths],
        compiler_params=_cparams(dimension_semantics=("parallel",)),
    )(z, z, mu)


def _shift_bwd(z, mu, dr, dk, dv, dlo, dgd):
    t, c = z.shape
    tr = 256
    nt = t // tr

    def body(z_ref, zp_ref, mu_ref, dr_ref, dk_ref, dv_ref, dlo_ref, dgd_ref,
             drn_ref, dkn_ref, dvn_ref, dlon_ref, dgdn_ref, dz_ref, dmu_ref):
        i = pl.program_id(0)
        zv, muv = z_ref[...], mu_ref[...]
        prev = zp_ref[7:8, :] * jnp.where(i == 0, 0.0, 1.0)
        row = lax.broadcasted_iota(jnp.int32, zv.shape, 0)
        zsh = jnp.where(row == 0, prev, pltpu.roll(zv, 1, 0))
        dzs = jnp.concatenate([dr_ref[...], dk_ref[...], dv_ref[...], dlo_ref[...], dgd_ref[...]], axis=1)
        nxt = jnp.concatenate([drn_ref[0:1, :], dkn_ref[0:1, :], dvn_ref[0:1, :], dlon_ref[0:1, :],
                               dgdn_ref[0:1, :]], axis=1) * jnp.where(i == nt - 1, 0.0, 1.0)
        u = dzs * muv
        un = jnp.where(row == tr - 1, nxt * muv, pltpu.roll(u, tr - 1, 0))
        dz_ref[...] = (dzs - u + un).astype(dz_ref.dtype)
        _accumulate(dmu_ref, jnp.sum(dzs * (zsh - zv), axis=0, keepdims=True))

    widths = (512, 512, 512, 128, 128)
    nxt_map = lambda i: (jnp.minimum((i + 1) * (tr // 8), t // 8 - 1), 0)
    return pl.pallas_call(
        body, name="rwkv_shift_bwd", grid=(nt,),
        in_specs=[pl.BlockSpec((tr, c), lambda i: (i, 0)),
                  pl.BlockSpec((8, c), lambda i: (jnp.maximum(i * (tr // 8) - 1, 0), 0)),
                  pl.BlockSpec((1, c), lambda i: (0, 0))]
        + [pl.BlockSpec((tr, w), lambda i: (i, 0)) for w in widths]
        + [pl.BlockSpec((8, w), nxt_map) for w in widths],
        out_specs=[pl.BlockSpec((tr, c), lambda i: (i, 0)), pl.BlockSpec((1, c), lambda i: (0, 0))],
        out_shape=[jax.ShapeDtypeStruct((t, c), BF16), jax.ShapeDtypeStruct((1, c), F32)],
        compiler_params=_cparams(dimension_semantics=("arbitrary",)),
    )(z, z, mu, dr, dk, dv, dlo, dgd, dr, dk, dv, dlo, dgd)


def _f_pre(k, lo, gd, w0, w2p, a0, a2p, g2, k_k, k_a):
    lane = lax.broadcasted_iota(jnp.int32, lo.shape, 1)
    lo_act = jnp.where(lane < DECAY_LORA, jnp.tanh(lo), lo)
    w = -_softplus(-(w0 + _bdot(lo_act, w2p))) - 0.5
    a = _sigmoid(a0 + _bdot(lo_act, a2p))
    g = _bdot(_sigmoid(gd), g2)
    kk = k * k_k
    k2 = k * (1.0 + (a - 1.0) * k_a)
    decay = jnp.exp(-jnp.exp(w))
    return k2, kk, a, decay, g


def _f_kk(kk, a):
    kkn = kk * lax.rsqrt(jnp.maximum(jnp.sum(kk * kk, axis=-1, keepdims=True), 1e-24))
    return kkn, kkn * a


def _f_post(y, r, k2, v, g, gn_w, gn_b, r_k):
    reps = y.shape[0] // RWKV_HEADS
    mean = jnp.mean(y, axis=-1, keepdims=True)
    var = jnp.mean(jnp.square(y - mean), axis=-1, keepdims=True)
    yn = (y - mean) * lax.rsqrt(var + GN_EPS)
    yn = yn * jnp.tile(gn_w, (reps, 1)) + jnp.tile(gn_b, (reps, 1))
    bonus = jnp.sum(r * k2 * jnp.tile(r_k, (reps, 1)), axis=-1, keepdims=True) * v
    return (yn + bonus) * g


def _to_v_rows(x):
    t = x.shape[0]
    return x.reshape(t, RWKV_HEADS, V_HI, V_LO).transpose(0, 2, 3, 1).reshape(t, V_HI, LANES)


def _from_v_rows(x):
    t = x.shape[0]
    return x.reshape(t, V_HI, V_LO, RWKV_HEADS).transpose(0, 3, 1, 2).reshape(t, RWKV_DIM)


def _k_cols(x):
    return jnp.tile(x, (V_LO, 1)).T


def _k_rows(x):
    xt = x.T
    out = xt[0:RWKV_HEADS]
    for l in range(1, V_LO):
        out = out + xt[l * RWKV_HEADS:(l + 1) * RWKV_HEADS]
    return out


def _wkv_fwd(r, w, k, kk, b, v):
    t = r.shape[0]
    tc = SCAN_CHUNK
    key_spec = pl.BlockSpec((tc, RWKV_HEADS, HEAD_DIM), lambda i: (i, 0, 0))
    row_spec = pl.BlockSpec((tc, V_HI, LANES), lambda i: (i, 0, 0))

    def body(r_ref, w_ref, k_ref, kk_ref, b_ref, v_ref, y_ref, st_ref, s_scr):
        @pl.when(pl.program_id(0) == 0)
        def _():
            s_scr[...] = jnp.zeros_like(s_scr)

        def step(ti, s):
            rc, wc, kc, kkc, bc = (_k_cols(ref[ti]) for ref in (r_ref, w_ref, k_ref, kk_ref, b_ref))
            vt = v_ref[ti]
            new, ys = [], []
            for j in range(V_HI):
                sa = -jnp.sum(s[j] * kkc, axis=0, keepdims=True)
                nj = s[j] * wc + bc * sa + kc * vt[j:j + 1]
                st_ref[ti, j] = nj
                ys.append(jnp.sum(nj * rc, axis=0, keepdims=True))
                new.append(nj)
            y_ref[ti] = jnp.concatenate(ys, axis=0)
            return tuple(new)

        s = lax.fori_loop(0, tc, step, tuple(s_scr[j] for j in range(V_HI)))
        for j in range(V_HI):
            s_scr[j] = s[j]

    return pl.pallas_call(
        body, name="wkv_fwd", grid=(t // tc,),
        in_specs=[key_spec] * 5 + [row_spec],
        out_specs=[row_spec, pl.BlockSpec((tc, V_HI, HEAD_DIM, LANES), lambda i: (i, 0, 0, 0))],
        out_shape=[jax.ShapeDtypeStruct((t, V_HI, LANES), F32),
                   jax.ShapeDtypeStruct((t, V_HI, HEAD_DIM, LANES), F32)],
        scratch_shapes=[pltpu.VMEM((V_HI, HEAD_DIM, LANES), F32)],
        compiler_params=_cparams(dimension_semantics=("arbitrary",)),
    )(r, w, k, kk, b, v)


def _wkv_bwd(r, w, k, kk, b, v, states, dy):
    t = r.shape[0]
    tc = SCAN_CHUNK
    nb = t // tc
    key_spec = pl.BlockSpec((tc, RWKV_HEADS, HEAD_DIM), lambda i: (nb - 1 - i, 0, 0))
    row_spec = pl.BlockSpec((tc, V_HI, LANES), lambda i: (nb - 1 - i, 0, 0))
    st_spec = pl.BlockSpec((tc, V_HI, HEAD_DIM, LANES), lambda i: (nb - 1 - i, 0, 0, 0))
    stp_spec = pl.BlockSpec((1, V_HI, HEAD_DIM, LANES), lambda i: (jnp.maximum((nb - 1 - i) * tc - 1, 0), 0, 0, 0))

    def body(r_ref, w_ref, k_ref, kk_ref, b_ref, v_ref, st_ref, stp_ref, dy_ref,
             dr_ref, dw_ref, dk_ref, dkk_ref, db_ref, dv_ref, ds_scr):
        @pl.when(pl.program_id(0) == 0)
        def _():
            ds_scr[...] = jnp.zeros_like(ds_scr)

        def colsum(x):
            return jnp.sum(x, axis=0, keepdims=True)

        def step(ti, ds, sp):
            rc, wc, kc, kkc, bc = (_k_cols(ref[ti]) for ref in (r_ref, w_ref, k_ref, kk_ref, b_ref))
            vt, dyt = v_ref[ti], dy_ref[ti]
            acc_r = acc_k = acc_b = acc_w = acc_kk = None
            new, dvs = [], []
            for j in range(V_HI):
                st = st_ref[ti, j]
                dsj = ds[j] + rc * dyt[j:j + 1]
                sa = -colsum(sp[j] * kkc)
                dsa = colsum(dsj * bc)
                dvs.append(colsum(dsj * kc))
                parts = (st * dyt[j:j + 1], dsj * vt[j:j + 1], dsj * sa, dsj * sp[j], sp[j] * dsa)
                if j == 0:
                    acc_r, acc_k, acc_b, acc_w, acc_kk = parts
                else:
                    acc_r, acc_k, acc_b, acc_w, acc_kk = (a + p for a, p in
                                                          zip((acc_r, acc_k, acc_b, acc_w, acc_kk), parts))
                new.append(dsj * wc - kkc * dsa)
            dv_ref[ti] = jnp.concatenate(dvs, axis=0)
            dr_ref[ti] = _k_rows(acc_r)
            dk_ref[ti] = _k_rows(acc_k)
            db_ref[ti] = _k_rows(acc_b)
            dw_ref[ti] = _k_rows(acc_w)
            dkk_ref[ti] = -_k_rows(acc_kk)
            return tuple(new)

        def loop_body(n, ds):
            ti = tc - 1 - n
            return step(ti, ds, tuple(st_ref[ti - 1, j] for j in range(V_HI)))

        ds = lax.fori_loop(0, tc - 1, loop_body, tuple(ds_scr[j] for j in range(V_HI)))
        keep = jnp.where(pl.program_id(0) == nb - 1, 0.0, 1.0)
        ds = step(0, ds, tuple(stp_ref[0, j] * keep for j in range(V_HI)))
        for j in range(V_HI):
            ds_scr[j] = ds[j]

    key_out = jax.ShapeDtypeStruct((t, RWKV_HEADS, HEAD_DIM), F32)
    return pl.pallas_call(
        body, name="wkv_bwd", grid=(nb,),
        in_specs=[key_spec] * 5 + [row_spec, st_spec, stp_spec, row_spec],
        out_specs=[key_spec] * 5 + [row_spec],
        out_shape=[key_out] * 5 + [jax.ShapeDtypeStruct((t, V_HI, LANES), F32)],
        scratch_shapes=[pltpu.VMEM((V_HI, HEAD_DIM, LANES), F32)],
        compiler_params=_cparams(dimension_semantics=("arbitrary",)),
    )(r, w, k, kk, b, v, states, states, dy)


PAIR = 2 * HEAD_DIM
N_PAIRS = ATTN_HEADS // 2
Q_COL0 = RWKV_COLS // PAIR
K_COL0 = Q_COL0 + N_PAIRS
V_COL0 = K_COL0 + N_PAIRS


def _swap_halves(x):
    lane = lax.broadcasted_iota(jnp.int32, x.shape, 1)
    return jnp.where((lane & (HEAD_DIM - 1)) < HEAD_DIM // 2, pltpu.roll(x, PAIR - HEAD_DIM // 2, 1),
                     pltpu.roll(x, HEAD_DIM // 2, 1))


@jax.custom_vjp
def _rope(x, cosf, sinf):
    return x * cosf + _swap_halves(x) * sinf


def _rope_fwd(x, cosf, sinf):
    return _rope(x, cosf, sinf), (cosf, sinf)


def _rope_bwd(res, d):
    cosf, sinf = res
    return d * cosf + _swap_halves(d * sinf), jnp.zeros_like(cosf), jnp.zeros_like(sinf)


_rope.defvjp(_rope_fwd, _rope_bwd)


def _head_sums(x):
    lane = lax.broadcasted_iota(jnp.int32, x.shape, 1)
    lo = jnp.where(lane < HEAD_DIM, 1.0, 0.0)
    hi = 1.0 - lo
    return lo * jnp.sum(x * lo, axis=1, keepdims=True) + hi * jnp.sum(x * hi, axis=1, keepdims=True)


def _f_qk(x, cosf, sinf, gain2):
    xn = x * lax.rsqrt(_head_sums(x * x) * (1.0 / HEAD_DIM) + RMS_EPS) * gain2
    return _rope(xn, cosf, sinf)


def _qk_prep(z, tab, q_gain, k_gain):
    t = z.shape[0]
    tr = 256

    def body(z_ref, c_ref, s_ref, qg_ref, kg_ref, o_ref):
        g = jnp.where(pl.program_id(0) < N_PAIRS, qg_ref[...], kg_ref[...])
        o_ref[...] = _f_qk(z_ref[...], c_ref[...], s_ref[...], jnp.concatenate([g, g], axis=1))

    gain = pl.BlockSpec((1, HEAD_DIM), lambda c, i: (0, 0))
    return pl.pallas_call(
        body, name="qk_prep", grid=(2 * N_PAIRS, t // tr),
        in_specs=[pl.BlockSpec((tr, PAIR), lambda c, i: (i, Q_COL0 + c)), pl.BlockSpec((tr, PAIR), lambda c, i: (i, 0)),
                  pl.BlockSpec((tr, PAIR), lambda c, i: (i, 1)), gain, gain],
        out_specs=pl.BlockSpec((tr, PAIR), lambda c, i: (i, c)),
        out_shape=jax.ShapeDtypeStruct((t, 2 * N_PAIRS * PAIR), F32),
        compiler_params=_cparams(dimension_semantics=("parallel", "parallel")),
    )(z, tab, tab, q_gain, k_gain)


def _qk_prep_bwd(z, tab, q_gain, k_gain, dq, dk):
    t = z.shape[0]
    tr = 256

    def body(z_ref, c_ref, s_ref, qg_ref, kg_ref, dq_ref, dk_ref, dz_ref, dqg_ref, dkg_ref):
        c, i = pl.program_id(0), pl.program_id(1)
        is_q = c < N_PAIRS
        g = jnp.where(is_q, qg_ref[...], kg_ref[...])
        d = jnp.where(is_q, dq_ref[...], dk_ref[...])
        _, vjp = jax.vjp(lambda xx, gg: _f_qk(xx, c_ref[...], s_ref[...], gg), z_ref[...],
                         jnp.concatenate([g, g], axis=1))
        dx, dg2 = vjp(d)
        dz_ref[...] = dx.astype(dz_ref.dtype)
        dg = dg2[:, :HEAD_DIM] + dg2[:, HEAD_DIM:]
        first_q = jnp.logical_and(c == 0, i == 0)
        first_k = jnp.logical_and(c == N_PAIRS, i == 0)

        @pl.when(first_q)
        def _():
            dqg_ref[...] = dg

        @pl.when(jnp.logical_and(is_q, jnp.logical_not(first_q)))
        def _():
            dqg_ref[...] += dg

        @pl.when(first_k)
        def _():
            dkg_ref[...] = dg

        @pl.when(jnp.logical_and(jnp.logical_not(is_q), jnp.logical_not(first_k)))
        def _():
            dkg_ref[...] += dg

    gain = pl.BlockSpec((1, HEAD_DIM), lambda c, i: (0, 0))
    return pl.pallas_call(
        body, name="qk_prep_bwd", grid=(2 * N_PAIRS, t // tr),
        in_specs=[pl.BlockSpec((tr, PAIR), lambda c, i: (i, Q_COL0 + c)), pl.BlockSpec((tr, PAIR), lambda c, i: (i, 0)),
                  pl.BlockSpec((tr, PAIR), lambda c, i: (i, 1)), gain, gain,
                  pl.BlockSpec((tr, PAIR), lambda c, i: (i, jnp.minimum(c, N_PAIRS - 1))),
                  pl.BlockSpec((tr, PAIR), lambda c, i: (i, jnp.maximum(c - N_PAIRS, 0)))],
        out_specs=[pl.BlockSpec((tr, PAIR), lambda c, i: (i, c)), gain, gain],
        out_shape=[jax.ShapeDtypeStruct((t, 2 * N_PAIRS * PAIR), BF16), jax.ShapeDtypeStruct((1, HEAD_DIM), F32),
                   jax.ShapeDtypeStruct((1, HEAD_DIM), F32)],
        compiler_params=_cparams(dimension_semantics=("arbitrary", "arbitrary")),
    )(z, tab, tab, q_gain, k_gain, dq, dk)


def _attn_block(q, kp, kc, vp, vc, kmin):
    k2 = jnp.concatenate([kp, kc], axis=0)
    v2 = jnp.concatenate([vp, vc], axis=0)
    s = _bdot_nt(q, k2) * (HEAD_DIM ** -0.5)
    qi = lax.broadcasted_iota(jnp.int32, s.shape, 0)
    kj = lax.broadcasted_iota(jnp.int32, s.shape, 1)
    dist = qi + BAND_BLOCK - kj
    valid = (dist >= 0) & (dist <= BAND_BLOCK) & (kj >= kmin)
    s = jnp.where(valid, s, NEG_INF)
    m = lax.stop_gradient(jnp.max(s, axis=-1, keepdims=True))
    e = jnp.exp(s - m)
    l = jnp.sum(e, axis=-1, keepdims=True)
    o = _bdot(e, v2) / l
    return o, m + jnp.log(l)


def _fold(src_ref, dst_ref, dil):
    t = src_ref.shape[0]
    ln = t // dil
    for j in range(dil):
        dst_ref[j * ln:(j + 1) * ln, :] = src_ref[pl.ds(j, ln, stride=dil), :]


def _unfold(src_ref, dst_ref, dil):
    t = src_ref.shape[0]
    ln = t // dil
    for j in range(dil):
        dst_ref[pl.ds(j, ln, stride=dil), :] = src_ref[j * ln:(j + 1) * ln, :]


def _per_group(fn):
    pair = pl.program_id(0)
    for gi, (_, dil) in enumerate(ATTN_GROUPS):
        @pl.when(jnp.logical_or(pair == 2 * gi, pair == 2 * gi + 1))
        def _(dil=dil):
            fn(dil)


def _block_rows(idx, blocks_per_seq):
    first = (idx & (blocks_per_seq - 1)) == 0
    cur = pl.ds(pl.multiple_of(idx * BAND_BLOCK, BAND_BLOCK), BAND_BLOCK)
    prev = pl.ds(pl.multiple_of(jnp.maximum(idx - 1, 0) * BAND_BLOCK, BAND_BLOCK), BAND_BLOCK)
    return first, cur, prev


def _heads(x):
    return x[:, :HEAD_DIM], x[:, HEAD_DIM:]


def _attn_fwd(qk, z):
    t = z.shape[0]
    n_blocks = t // BAND_BLOCK

    def body(q_ref, k_ref, v_ref, o_ref, lse_ref, qf, kf, vf, of, lf):
        def run(dil):
            _fold(q_ref, qf, dil)
            _fold(k_ref, kf, dil)
            _fold(v_ref, vf, dil)
            blocks_per_seq = n_blocks // dil

            def block(idx, carry):
                first, cur, prev = _block_rows(idx, blocks_per_seq)
                kmin = jnp.where(first, BAND_BLOCK, 0)
                outs, lses = [], []
                for q, kp, kc, vp, vc in zip(_heads(qf[cur, :]), _heads(kf[prev, :]), _heads(kf[cur, :]),
                                             _heads(vf[prev, :]), _heads(vf[cur, :])):
                    o, ls = _attn_block(q, kp, kc, vp, vc, kmin)
                    outs.append(o)
                    lses.append(jnp.broadcast_to(ls, o.shape))
                of[cur, :] = jnp.concatenate(outs, axis=1)
                lf[cur, :] = jnp.concatenate(lses, axis=1)
                return carry

            lax.fori_loop(0, n_blocks, block, 0)
            _unfold(of, o_ref, dil)
            _unfold(lf, lse_ref, dil)

        _per_group(run)

    slab = jax.ShapeDtypeStruct((t, N_PAIRS * PAIR), F32)
    out_spec = pl.BlockSpec((t, PAIR), lambda p: (0, p))
    return pl.pallas_call(
        body, name="attn_fwd", grid=(N_PAIRS,),
        in_specs=[pl.BlockSpec((t, PAIR), lambda p: (0, p)), pl.BlockSpec((t, PAIR), lambda p: (0, N_PAIRS + p)),
                  pl.BlockSpec((t, PAIR), lambda p: (0, V_COL0 + p))],
        out_specs=[out_spec, out_spec], out_shape=[slab, slab],
        scratch_shapes=[pltpu.VMEM((t, PAIR), F32)] * 5,
        compiler_params=_cparams(dimension_semantics=("parallel",)),
    )(qk, qk, z)


def _attn_bwd(qk, z, do, dlse):
    t = z.shape[0]
    n_blocks = t // BAND_BLOCK

    def body(q_ref, k_ref, v_ref, do_ref, dl_ref, dq_ref, dk_ref, dv_ref, qf, kf, vf, dof, dlf, dqf, dkf, dvf):
        def run(dil):
            for src, dst in ((q_ref, qf), (k_ref, kf), (v_ref, vf), (do_ref, dof), (dl_ref, dlf)):
                _fold(src, dst, dil)
            blocks_per_seq = n_blocks // dil

            def block(idx, carry):
                first, cur, prev = _block_rows(idx, blocks_per_seq)
                kmin = jnp.where(first, BAND_BLOCK, 0)
                grads = []
                for q, kp, kc, vp, vc, do_h, dl_h in zip(
                        _heads(qf[cur, :]), _heads(kf[prev, :]), _heads(kf[cur, :]), _heads(vf[prev, :]),
                        _heads(vf[cur, :]), _heads(dof[cur, :]), _heads(dlf[cur, :])):
                    _, vjp = jax.vjp(functools.partial(_attn_block, kmin=kmin), q, kp, kc, vp, vc)
                    grads.append(vjp((do_h, jnp.sum(dl_h, axis=1, keepdims=True))))
                dq, dkp, dkc, dvp, dvc = (jnp.concatenate([a, b], axis=1) for a, b in zip(*grads))
                dqf[cur, :] = dq
                dkf[cur, :] = dkc
                dvf[cur, :] = dvc

                @pl.when(jnp.logical_not(first))
                def _():
                    dkf[prev, :] += dkp
                    dvf[prev, :] += dvp

                return carry

            lax.fori_loop(0, n_blocks, block, 0)
            _unfold(dqf, dq_ref, dil)
            _unfold(dkf, dk_ref, dil)
            _unfold(dvf, dv_ref, dil)

        _per_group(run)

    slab = jax.ShapeDtypeStruct((t, N_PAIRS * PAIR), F32)
    own = pl.BlockSpec((t, PAIR), lambda p: (0, p))
    return pl.pallas_call(
        body, name="attn_bwd", grid=(N_PAIRS,),
        in_specs=[own, pl.BlockSpec((t, PAIR), lambda p: (0, N_PAIRS + p)),
                  pl.BlockSpec((t, PAIR), lambda p: (0, V_COL0 + p)), own, own],
        out_specs=[own] * 3, out_shape=[slab] * 3,
        scratch_shapes=[pltpu.VMEM((t, PAIR), F32)] * 8,
        compiler_params=_cparams(dimension_semantics=("parallel",)),
    )(qk, qk, z, do, dlse)


def _f_comb(o1, o2, o3, l1, l2, l3):
    m = jnp.maximum(jnp.maximum(l1, l2), l3)
    e1, e2, e3 = jnp.exp(l1 - m), jnp.exp(l2 - m), jnp.exp(l3 - m)
    den = e1 + e2 + e3
    return (e1 / den) * o1 + (e2 / den) * o2 + (e3 / den) * o3


def _all_gather_hbm(name, arrs):
    na = len(arrs)

    def body(*refs):
        x_refs, out_refs = refs[:na], refs[na:2 * na]
        send_sems, recv_sems, local_sems = refs[2 * na:]
        mx, my, mc = lax.axis_index("x"), lax.axis_index("y"), lax.axis_index("c")
        me, sibling = (mx, my, mc), (mx, my, 1 - mc)
        chips = [(1 - mx, my), (mx, 1 - my), (1 - mx, 1 - my)]

        def slot(a, px, py, pc):
            return out_refs[a].at[4 * px + 2 * py + pc]

        def copy(a, k, block, to, src=None):
            return pltpu.make_async_remote_copy(
                src_ref=slot(a, *block) if src is None else src, dst_ref=slot(a, *block),
                send_sem=send_sems.at[a, k], recv_sem=recv_sems.at[a, k], device_id=to, device_id_type=MESH)

        mine = [pltpu.make_async_copy(x_refs[a], slot(a, *me), local_sems.at[a]) for a in range(na)]
        for cp in mine:
            cp.start()
        first = []
        for a in range(na):
            first.append(copy(a, 0, me, sibling, src=x_refs[a]))
            first += [copy(a, 1 + j, me, (*chip, mc), src=x_refs[a]) for j, chip in enumerate(chips)]
        for cp in first:
            cp.start()
        passed = []
        for j, chip in enumerate(chips):
            for a in range(na):
                copy(a, 1 + j, (*chip, mc), me).wait_recv()
                passed.append(copy(a, 4 + j, (*chip, mc), sibling))
                passed[-1].start()
        for a in range(na):
            copy(a, 0, sibling, me).wait_recv()
            for j, chip in enumerate(chips):
                copy(a, 4 + j, (*chip, 1 - mc), me).wait_recv()
        for cp in first + passed:
            cp.wait_send()
        for cp in mine:
            cp.wait()

    hbm = pl.BlockSpec(memory_space=pl.ANY)
    return pl.pallas_call(
        body, name=name,
        out_shape=[jax.ShapeDtypeStruct((N_DEV,) + a.shape, a.dtype) for a in arrs],
        in_specs=[hbm] * na, out_specs=[hbm] * na,
        scratch_shapes=[pltpu.SemaphoreType.DMA((na, 7)), pltpu.SemaphoreType.DMA((na, 7)),
                        pltpu.SemaphoreType.DMA((na,))],
    )(*arrs)


def _all_gather_vmem(x):
    rws, cols = x.shape

    def body(x_ref, out_ref, send_sems, recv_sems):
        mx, my, mc = lax.axis_index("x"), lax.axis_index("y"), lax.axis_index("c")
        me, sibling = (mx, my, mc), (mx, my, 1 - mc)
        chips = [(1 - mx, my), (mx, 1 - my), (1 - mx, 1 - my)]

        def slot(px, py, pc):
            return out_ref.at[4 * px + 2 * py + pc]

        def copy(k, block, to, src=None):
            return pltpu.make_async_remote_copy(
                src_ref=slot(*block) if src is None else src, dst_ref=slot(*block),
                send_sem=send_sems.at[k], recv_sem=recv_sems.at[k], device_id=to, device_id_type=MESH)

        first = [copy(0, me, sibling, src=x_ref)]
        first += [copy(1 + j, me, (*chip, mc), src=x_ref) for j, chip in enumerate(chips)]
        for cp in first:
            cp.start()
        out_ref[4 * mx + 2 * my + mc] = x_ref[...]
        passed = [copy(4 + j, (*chip, mc), sibling) for j, chip in enumerate(chips)]
        for j, chip in enumerate(chips):
            copy(1 + j, (*chip, mc), me).wait_recv()
            passed[j].start()
        copy(0, sibling, me).wait_recv()
        for j, chip in enumerate(chips):
            copy(4 + j, (*chip, 1 - mc), me).wait_recv()
        for cp in first + passed:
            cp.wait_send()

    return pl.pallas_call(
        body, name="all_gather_small",
        out_shape=jax.ShapeDtypeStruct((N_DEV, rws, cols), x.dtype),
        in_specs=[pl.BlockSpec(memory_space=pltpu.VMEM)], out_specs=pl.BlockSpec(memory_space=pltpu.VMEM),
        scratch_shapes=[pltpu.SemaphoreType.DMA((7,)), pltpu.SemaphoreType.DMA((7,))],
    )(x)


def _all_to_all_hbm(name, arrs):
    na = len(arrs)

    def body(*refs):
        g_refs, out_refs = refs[:na], refs[na:2 * na]
        send_sems, recv_sems, local_sems = refs[2 * na:]
        mx, my, mc = lax.axis_index("x"), lax.axis_index("y"), lax.axis_index("c")
        me = 4 * mx + 2 * my + mc
        mine = [pltpu.make_async_copy(g_refs[a].at[me], out_refs[a].at[me], local_sems.at[a]) for a in range(na)]
        for cp in mine:
            cp.start()
        copies = []
        for k in range(1, N_DEV):
            px, py, pc = mx ^ (k >> 2), my ^ ((k >> 1) & 1), mc ^ (k & 1)
            peer = 4 * px + 2 * py + pc
            for a in range(na):
                copies.append(pltpu.make_async_remote_copy(
                    src_ref=g_refs[a].at[peer], dst_ref=out_refs[a].at[me], send_sem=send_sems.at[a, k - 1],
                    recv_sem=recv_sems.at[a, k - 1], device_id=(px, py, pc), device_id_type=MESH))
        for cp in copies:
            cp.start()
        for cp in copies:
            cp.wait_recv()
        for cp in copies:
            cp.wait_send()
        for cp in mine:
            cp.wait()

    hbm = pl.BlockSpec(memory_space=pl.ANY)
    return pl.pallas_call(
        body, name=name,
        out_shape=[jax.ShapeDtypeStruct(a.shape, a.dtype) for a in arrs],
        in_specs=[hbm] * na, out_specs=[hbm] * na,
        scratch_shapes=[pltpu.SemaphoreType.DMA((na, 7)), pltpu.SemaphoreType.DMA((na, 7)),
                        pltpu.SemaphoreType.DMA((na,))],
    )(*arrs)


def _scatter_copies(g_refs, land_refs, send_sems, recv_sems):
    mx, my, mc = lax.axis_index("x"), lax.axis_index("y"), lax.axis_index("c")
    me = 4 * mx + 2 * my + mc
    copies = []
    for k in range(1, N_DEV):
        px, py, pc = mx ^ (k >> 2), my ^ ((k >> 1) & 1), mc ^ (k & 1)
        peer = 4 * px + 2 * py + pc
        for a, (g_ref, land_ref) in enumerate(zip(g_refs, land_refs)):
            copies.append(pltpu.make_async_remote_copy(
                src_ref=g_ref.at[peer], dst_ref=land_ref.at[me], send_sem=send_sems.at[a * (N_DEV - 1) + k - 1],
                recv_sem=recv_sems.at[a * (N_DEV - 1) + k - 1], device_id=(px, py, pc), device_id_type=MESH))
    return copies


_HBM = pl.BlockSpec(memory_space=pltpu.HBM)
_SEM = pl.BlockSpec(memory_space=pltpu.SEMAPHORE)
_DATAFLOW = pltpu.SideEffectType.DATAFLOW_SIDE_EFFECTING


def _all_to_all_start(name, arrs):
    na = len(arrs)

    def body(*refs):
        for cp in _scatter_copies(refs[:na], refs[na:2 * na], refs[2 * na], refs[2 * na + 1]):
            cp.start()

    in_hbm = lambda a: pltpu.with_memory_space_constraint(a, pltpu.HBM)
    buffers = [pltpu.HBM(a.shape, a.dtype) for a in arrs]
    outs = pl.pallas_call(
        body, name=name,
        out_shape=(pltpu.SemaphoreType.DMA((na * (N_DEV - 1),)), pltpu.SemaphoreType.DMA((na * (N_DEV - 1),)),
                   *buffers, *buffers),
        in_specs=[_HBM] * (2 * na), out_specs=(_SEM, _SEM, *[_HBM] * (2 * na)),
        input_output_aliases={i: 2 + i for i in range(2 * na)},
        compiler_params=pltpu.CompilerParams(has_side_effects=_DATAFLOW),
    )(*[in_hbm(a) for a in arrs], *[in_hbm(lax.empty(a.shape, a.dtype)) for a in arrs])
    return outs[0], outs[1], outs[2:2 + na], outs[2 + na:]


def _all_to_all_wait(name, send_sems, recv_sems, sent, lands, after):
    na = len(sent)

    def body(*refs):
        for cp in _scatter_copies(refs[:na], refs[na:2 * na], refs[2 * na], refs[2 * na + 1]):
            cp.wait_send()
            cp.wait_recv()

    outs = pl.pallas_call(
        body, name=name,
        out_shape=[pltpu.HBM(a.shape, a.dtype) for a in list(sent) + list(lands)],
        in_specs=[_HBM] * (2 * na) + [_SEM, _SEM, pl.BlockSpec(memory_space=pl.ANY)], out_specs=[_HBM] * (2 * na),
        input_output_aliases={i: i for i in range(2 * na)},
        compiler_params=pltpu.CompilerParams(has_side_effects=_DATAFLOW),
    )(*sent, *lands, send_sems, recv_sems, after)
    return outs[:na], outs[na:]


def _sum_slots(name, g, tr):
    _, rws, cols = g.shape

    def body(g_ref, o_ref):
        acc = g_ref[0].astype(F32)
        for j in range(1, N_DEV):
            acc = acc + g_ref[j].astype(F32)
        o_ref[...] = acc

    return pl.pallas_call(
        body, name=name, grid=(rws // tr,),
        in_specs=[pl.BlockSpec((N_DEV, tr, cols), lambda i: (0, i, 0))],
        out_specs=pl.BlockSpec((tr, cols), lambda i: (i, 0)),
        out_shape=jax.ShapeDtypeStruct((rws, cols), F32),
        compiler_params=_cparams(dimension_semantics=("parallel",)),
    )(g)


def _adam_math(wv, gv, mv, vv):
    mn = ADAM_B1 * mv + (1.0 - ADAM_B1) * gv
    vn = ADAM_B2 * vv + (1.0 - ADAM_B2) * jnp.square(gv)
    m_hat = mn / (1.0 - ADAM_B1 ** ADAM_STEP)
    v_hat = vn / (1.0 - ADAM_B2 ** ADAM_STEP)
    delta = -ADAM_LR * (m_hat / (jnp.sqrt(v_hat) + ADAM_EPS) + ADAM_WD * wv)
    return delta, mn, vn


def _adamw(name, w, g, m, v, tr):
    return _rowwise(name, _adam_math, [w, g, m, v], [], [(LANES, F32)] * 3, tr=tr)


def _adamw_slots(name, recv, own, rb, cb, cw, w, m, v, tr):
    nr, nc = w.shape

    def body(g_ref, own_ref, w_ref, m_ref, v_ref, go_ref, d_ref, mo_ref, vo_ref):
        me = 4 * lax.axis_index("x") + 2 * lax.axis_index("y") + lax.axis_index("c")
        acc = None
        for s in range(N_DEV):
            part = jnp.where(me == s, own_ref[s], g_ref[s]).astype(F32)
            acc = part if acc is None else acc + part
        g = acc[:, :nc]
        go_ref[...] = g
        d_ref[...], mo_ref[...], vo_ref[...] = _adam_math(w_ref[...], g, m_ref[...], v_ref[...])

    nat = pl.BlockSpec((tr, nc), lambda i: (i, 0))
    slots = pl.BlockSpec((N_DEV, tr, cw), lambda i: (0, rb + i, cb))
    return pl.pallas_call(
        body, name=name, grid=(nr // tr,),
        in_specs=[slots, slots, nat, nat, nat],
        out_specs=[nat] * 4, out_shape=[jax.ShapeDtypeStruct((nr, nc), F32)] * 4,
        compiler_params=_cparams(dimension_semantics=("parallel",)),
    )(recv, own, w, m, v)


def _local_blocks(w):
    pad_cols = lambda a: jnp.pad(a, ((0, 0), (0, FF_PAD - FF_SHARD)))
    pad_rows = lambda a: jnp.pad(a, ((0, FF_PAD - FF_SHARD), (0, 0)))
    gate_up = lambda tag: jnp.concatenate([pad_cols(w[tag + "_w_gate"]), pad_cols(w[tag + "_w_up"])], axis=1)
    blocks = {
        "ffn1_gu": gate_up("ffn1"), "ffn1_d": pad_rows(w["ffn1_w_down"]), "w_in": w["w_in"],
        "lora": jnp.concatenate([w["rwkv_w2"], w["rwkv_a2"], w["rwkv_g2"]], axis=0),
        "br": jnp.concatenate([w["w_br_rwkv"], w["w_br_attn"], w["ple_w_proj"]], axis=0),
        "w_out": w["w_out"], "ffn2_gu": gate_up("ffn2"), "ffn2_d": pad_rows(w["ffn2_w_down"]),
        "ple_gate": w["ple_w_gate"],
    }
    return {n: a.astype(BF16) for n, a in blocks.items()}


SCATTER_GROUPS = {"tail": ("ple_gate", "ple_proj", "ffn2_gu", "ffn2_d"), "mixer": ("w_out", "br", "lora", "w_in"),
                  "head": ("ffn1_gu", "ffn1_d")}

ADAM_PLAN = (
    ("ffn1_w_gate", "ffn1_gu", 0, 0, FF_PAD, 256), ("ffn1_w_up", "ffn1_gu", 0, 1, FF_PAD, 256),
    ("ffn1_w_down", "ffn1_d", 0, 0, D_MODEL, FF_SHARD // 2), ("w_in", "w_in", 0, 0, IN_SHARD, 256),
    ("rwkv_w2", "lora", 0, 0, HEAD_DIM, 64), ("rwkv_a2", "lora", 1, 0, HEAD_DIM, 64),
    ("rwkv_g2", "lora", 2, 0, HEAD_DIM, 64),
    ("w_br_rwkv", "br", 0, 0, OUT_SHARD, 256), ("w_br_attn", "br", 2, 0, OUT_SHARD, 256),
    ("ple_w_proj", "ple_proj", 0, 0, OUT_SHARD, 256), ("w_out", "w_out", 0, 0, D_MODEL, OUT_SHARD),
    ("ffn2_w_gate", "ffn2_gu", 0, 0, FF_PAD, 256), ("ffn2_w_up", "ffn2_gu", 0, 1, FF_PAD, 256),
    ("ffn2_w_down", "ffn2_d", 0, 0, D_MODEL, FF_SHARD // 2), ("ple_w_gate", "ple_gate", 0, 0, D_MODEL, OUT_SHARD),
)


def _pack_small(arrs, rows):
    flat = jnp.concatenate([a.reshape(-1) for a in arrs])
    return jnp.pad(flat, (0, rows * LANES - flat.shape[0])).reshape(rows, LANES)


def _unpack_small(flat, like):
    flat = flat.reshape(-1)
    out, off = [], 0
    for a in like:
        out.append(flat[off:off + a.size].reshape(a.shape))
        off += a.size
    return out


def _local_step(x, p, pos, target, sm, wg, on_grads):
    t = x.shape[0]
    hv = t * RWKV_HEADS

    w_d1 = wg["ffn1_d"].reshape(FF_HID, D_MODEL)
    w_d2 = wg["ffn2_d"].reshape(FF_HID, D_MODEL)
    w_out = wg["w_out"].reshape(D_MODEL, D_MODEL)
    w_pg = wg["ple_gate"].reshape(D_MODEL, D_MODEL)
    full_cols = lambda blk: blk.transpose(1, 0, 2).reshape(blk.shape[1], N_DEV * blk.shape[2])
    lora_w2 = full_cols(wg["lora"][:, :DECAY_LORA])
    lora_a2 = full_cols(wg["lora"][:, DECAY_LORA:DECAY_LORA + ICLR_LORA])
    lora_g2 = full_cols(wg["lora"][:, DECAY_LORA + ICLR_LORA:])
    x1, ffn1_saved = _ffn_fwd("ffn1", x, sm["ffn1_norm"], wg["ffn1_gu"], w_d1)
    h2 = _norm_fwd("mix_norm", x1, sm["mix_norm"])
    z = _mmc_nn("w_in", h2, wg["w_in"], 0, 0, IN_SHARD)
    z_r = z[:, :RWKV_COLS]
    z_g = z[:, RWKV_COLS + 3 * ATTN_DIM:]

    r, k, v, lo, gd = _shift_fwd(z_r, sm["rwkv_mu"])
    zero_lo = jnp.zeros((DECAY_LORA, RWKV_DIM), BF16)
    w2p = jnp.concatenate([lora_w2, zero_lo], axis=0).astype(F32)
    a2p = jnp.concatenate([zero_lo, lora_a2], axis=0).astype(F32)
    pre_params = [sm["rwkv_w0"], w2p, sm["rwkv_a0"], a2p, lora_g2.astype(F32), sm["rwkv_k_k"], sm["rwkv_k_a"]]
    wide = [(RWKV_DIM, F32)]
    k2, kk, a, decay, g = _rowwise("rwkv_pre", _f_pre, [k, lo, gd], pre_params, wide * 5)
    as_heads = lambda u: u.reshape(hv, HEAD_DIM)
    kkn, b = _rowwise("rwkv_kk", _f_kk, [as_heads(kk), as_heads(a)], [], [(HEAD_DIM, F32)] * 2, tr=512)
    scan_in = [u.reshape(t, RWKV_HEADS, HEAD_DIM) for u in (r, decay, k2, kkn, b)]
    v_rows = _to_v_rows(v)
    y_rows, states = _wkv_fwd(*scan_in, v_rows)
    y = _from_v_rows(y_rows)
    post_params = [sm["rwkv_gn_w"].reshape(RWKV_HEADS, HEAD_DIM), sm["rwkv_gn_b"].reshape(RWKV_HEADS, HEAD_DIM),
                   sm["rwkv_r_k"].reshape(RWKV_HEADS, HEAD_DIM)]
    post_rows = [as_heads(y), as_heads(r), as_heads(k2), as_heads(v), as_heads(g)]
    y_rwkv = _rowwise("rwkv_post", lambda *av: (_f_post(*av),), post_rows, post_params, [(HEAD_DIM, F32)], tr=512)[0]
    y_rwkv = y_rwkv.reshape(t, RWKV_DIM)

    inv_freq = 1.0 / (ROPE_THETA ** (jnp.arange(0, HEAD_DIM, 2, dtype=F32) / HEAD_DIM))
    freq2 = jnp.tile(inv_freq, 2 * PAIR // HEAD_DIM).reshape(1, PAIR)
    half = jnp.ones((HEAD_DIM // 2,), F32)
    sign2 = jnp.tile(jnp.concatenate([-half, half]), PAIR // HEAD_DIM).reshape(1, PAIR)

    def rope_table(posv, fr, sg):
        ang = posv * fr
        return (jnp.concatenate([jnp.cos(ang), jnp.sin(ang) * sg], axis=1),)
    tab = _rowwise("rope_table", rope_table, [pos.astype(F32).reshape(t, 1)], [freq2, sign2], [(2 * PAIR, F32)])[0]
    qk = _qk_prep(z, tab, sm["q_norm"], sm["k_norm"])
    o_all, lse_all = _attn_fwd(qk, z)
    gw = HEADS_PER_GROUP * HEAD_DIM

    def by_group(ov, lv):
        return [ov[:, i * gw:(i + 1) * gw] for i in range(3)] + [lv[:, i * gw:(i + 1) * gw] for i in range(3)]
    y_attn = _rowwise("attn_comb", lambda ov, lv: (_f_comb(*by_group(ov, lv)),), [o_all, lse_all], [], [(gw, F32)])[0]

    u_r = _mmc_nn("br_rwkv", y_rwkv, wg["br"], 0, 0, OUT_SHARD)
    u_a = _mmc_nn("br_attn", y_attn, wg["br"], 2, 0, OUT_SHARD)

    def f_merge(zgr, zga, ur, ua):
        return _sigmoid(zgr) * ur + _sigmoid(zga) * ua
    merged = _rowwise("merge", lambda zg, ur, ua: (f_merge(zg[:, :D_MODEL], zg[:, D_MODEL:], ur, ua),),
                      [z_g, u_r, u_a], [], [(D_MODEL, BF16)])[0]
    x2 = _mm("w_out", merged, w_out, "nn", res=x1)
    x3, ffn2_saved = _ffn_fwd("ffn2", x2, sm["ffn2_norm"], wg["ffn2_gu"], w_d2)

    hn = _norm_fwd("ple_norm", x3, sm["ple_norm"])
    gz = _mm("ple_gate", hn, w_pg, "nn")
    pp = _mmc_nn("ple_proj", p, wg["br"], 3, 0, OUT_SHARD)

    def f_head(x3v, gzv, ppv, tg):
        sg = _sigmoid(gzv)
        err = x3v + sg * ppv - tg
        part = 0.5 * jnp.sum(jnp.mean(err * err, axis=-1, keepdims=True))
        dx4 = err * (1.0 / D_MODEL)
        return dx4, dx4 * ppv * sg * (1.0 - sg), dx4 * sg, jnp.full((1, LANES), part, F32)
    dx4, dgz, dpp, loss_row = _rowwise("ple_loss", f_head, [x3, gz, pp, target], [],
                                       [(D_MODEL, F32), (D_MODEL, BF16), (D_MODEL, BF16)], [(1, LANES)])
    loss = loss_row[0, 0]

    gs, gm = {}, {}
    row_blocks = lambda g: g.reshape(N_DEV, g.shape[0] // N_DEV, g.shape[1])
    dhn = _mm("ple_dhn", dgz, w_pg, "nt")
    gm["ple_gate"] = row_blocks(_mm("ple_dwgate", hn, dgz, "tn", out_dtype=BF16))
    gm["ple_proj"] = _mmc_tn("ple_dwproj", p, dpp, OUT_SHARD)
    dx3, gs["ple_norm"] = _norm_bwd("ple_dnorm", x3, sm["ple_norm"], dhn, dx4)

    dx2, gs["ffn2_norm"], gm["ffn2_gu"], gm["ffn2_d"] = _ffn_bwd(
        "ffn2", x2, sm["ffn2_norm"], wg["ffn2_gu"], w_d2, ffn2_saved, dx3)
    on_grads("tail", {n: gm.pop(n) for n in SCATTER_GROUPS["tail"]})

    dmerged = _mm("w_out_dmerged", dx2, w_out, "nt")
    gm["w_out"] = row_blocks(_mm("w_out_dw", merged, dx2, "tn", out_dtype=BF16))

    def merge_bwd(zg, ur, ua, dm):
        _, vjp = jax.vjp(f_merge, zg[:, :D_MODEL], zg[:, D_MODEL:], ur, ua)
        dzr, dza, dur, dua = vjp(dm)
        return jnp.concatenate([dzr, dza], axis=1), dur, dua
    dz_g, du_r, du_a = _rowwise("merge_bwd", merge_bwd, [z_g, u_r, u_a, dmerged], [],
                                [(2 * D_MODEL, BF16), (D_MODEL, BF16), (D_MODEL, BF16)])
    dy_rwkv = _mmc_nt("br_rwkv_dy", du_r, wg["br"], 0, 0, OUT_SHARD, RWKV_DIM)
    dy_attn = _mmc_nt("br_attn_dy", du_a, wg["br"], 2, 0, OUT_SHARD, HEADS_PER_GROUP * HEAD_DIM)
    gm["br"] = jnp.concatenate([_mmc_tn("br_rwkv_dw", y_rwkv, du_r, OUT_SHARD),
                                _mmc_tn("br_attn_dw", y_attn, du_a, OUT_SHARD)], axis=1)

    def comb_bwd(ov, lv, dyv):
        _, vjp = jax.vjp(_f_comb, *by_group(ov, lv))
        d = vjp(dyv)
        return jnp.concatenate(d[:3], axis=1), jnp.concatenate(d[3:], axis=1)
    do_all, dl_all = _rowwise("attn_comb_bwd", comb_bwd, [o_all, lse_all, dy_attn], [],
                              [(ATTN_DIM, F32), (ATTN_DIM, F32)])
    dq_all, dk_all, dv_all = _attn_bwd(qk, z, do_all, dl_all)
    dqk_raw, gs["q_norm"], gs["k_norm"] = _qk_prep_bwd(z, tab, sm["q_norm"], sm["k_norm"], dq_all, dk_all)

    def post_bwd(yv, rv, k2v, vv, gv, dv_, gnw, gnb, rk):
        _, vjp = jax.vjp(_f_post, yv, rv, k2v, vv, gv, gnw, gnb, rk)
        return vjp(dv_)
    head_acc = (RWKV_HEADS, HEAD_DIM)
    dy, dr1, dk2a, dv1, dg, d_gnw, d_gnb, d_rk = _rowwise(
        "rwkv_post_bwd", post_bwd, post_rows + [as_heads(dy_rwkv)], post_params, [(HEAD_DIM, F32)] * 5,
        [head_acc] * 3, tr=512)
    gs["rwkv_gn_w"], gs["rwkv_gn_b"], gs["rwkv_r_k"] = d_gnw, d_gnb, d_rk
    dr2, ddecay, dk2b, dkkn, db, dv_rows = _wkv_bwd(*scan_in, v_rows, states, _to_v_rows(dy.reshape(t, RWKV_DIM)))
    dr2, ddecay, dk2b, dkkn, db = [u.reshape(t, RWKV_DIM) for u in (dr2, ddecay, dk2b, dkkn, db)]
    dv2 = _from_v_rows(dv_rows)

    def kk_bwd(kkv, av, dkknv, dbv):
        _, vjp = jax.vjp(_f_kk, kkv, av)
        return vjp((dkknv, dbv))
    dkk, da = _rowwise("rwkv_kk_bwd", kk_bwd, [as_heads(kk), as_heads(a), as_heads(dkkn), as_heads(db)], [],
                       [(HEAD_DIM, F32)] * 2, tr=512)

    def pre_bwd(kv, lov, gdv, dk2x, dk2y, dkkv, dav, ddec, dgv, w0, w2p_, a0, a2p_, g2, k_k, k_a):
        _, vjp = jax.vjp(_f_pre, kv, lov, gdv, w0, w2p_, a0, a2p_, g2, k_k, k_a)
        return vjp((dk2x + dk2y, dkkv, dav, ddec, dgv))
    lora_acc = (DECAY_LORA + ICLR_LORA, RWKV_DIM)
    dk, dlo, dgd, d_w0, d_w2p, d_a0, d_a2p, d_g2, d_kk, d_ka = _rowwise(
        "rwkv_pre_bwd", pre_bwd,
        [k, lo, gd, dk2a.reshape(t, RWKV_DIM), dk2b, dkk.reshape(t, RWKV_DIM), da.reshape(t, RWKV_DIM), ddecay,
         dg.reshape(t, RWKV_DIM)],
        pre_params, [(RWKV_DIM, F32), (LANES, F32), (LANES, F32)],
        [(1, RWKV_DIM), lora_acc, (1, RWKV_DIM), lora_acc, (GATE_LORA, RWKV_DIM), (1, RWKV_DIM), (1, RWKV_DIM)])
    gs["rwkv_w0"], gs["rwkv_a0"], gs["rwkv_k_k"], gs["rwkv_k_a"] = d_w0, d_a0, d_kk, d_ka
    col_blocks = lambda g: g.reshape(g.shape[0], N_DEV, g.shape[1] // N_DEV).transpose(1, 0, 2)
    gm["lora"] = jnp.concatenate([col_blocks(d_w2p[:DECAY_LORA]), col_blocks(d_a2p[DECAY_LORA:]), col_blocks(d_g2)],
                                 axis=1).astype(BF16)
    add2 = lambda u, w_: _rowwise("rwkv_add", lambda p_, q_: (p_ + q_,), [u, w_], [], [(RWKV_DIM, F32)])[0]
    dz_r, gs["rwkv_mu"] = _shift_bwd(z_r, sm["rwkv_mu"], add2(dr1.reshape(t, RWKV_DIM), dr2), dk,
                                     add2(dv1.reshape(t, RWKV_DIM), dv2), dlo, dgd)

    dz = jnp.concatenate([dz_r, dqk_raw, dv_all.astype(BF16), dz_g], axis=1)
    dh2 = _mmc_nt("w_in_dh", dz, wg["w_in"], 0, 0, IN_SHARD, D_MODEL)
    gm["w_in"] = _mmc_tn("w_in_dw", h2, dz, IN_SHARD)
    on_grads("mixer", {n: gm.pop(n) for n in SCATTER_GROUPS["mixer"]})
    dx1, gs["mix_norm"] = _norm_bwd("mix_dnorm", x1, sm["mix_norm"], dh2, dx2)

    dx0, gs["ffn1_norm"], gm["ffn1_gu"], gm["ffn1_d"] = _ffn_bwd(
        "ffn1", x, sm["ffn1_norm"], wg["ffn1_gu"], w_d1, ffn1_saved, dx1)
    return loss, dx0, gm, gs


def kernel(x, p, positions, ffn1_norm, ffn1_w_gate, ffn1_w_up, ffn1_w_down, mix_norm, w_in, rwkv_mu, rwkv_w0, rwkv_w2, rwkv_a0, rwkv_a2, rwkv_g2, rwkv_k_k, rwkv_k_a, rwkv_r_k, rwkv_gn_w, rwkv_gn_b, q_norm, k_norm, w_br_rwkv, w_br_attn, w_out, ffn2_norm, ffn2_w_gate, ffn2_w_up, ffn2_w_down, ple_norm, ple_w_gate, ple_w_proj, loss_target, m_ffn1_norm, m_ffn1_w_gate, m_ffn1_w_up, m_ffn1_w_down, m_mix_norm, m_w_in, m_rwkv_mu, m_rwkv_w0, m_rwkv_w2, m_rwkv_a0, m_rwkv_a2, m_rwkv_g2, m_rwkv_k_k, m_rwkv_k_a, m_rwkv_r_k, m_rwkv_gn_w, m_rwkv_gn_b, m_q_norm, m_k_norm, m_w_br_rwkv, m_w_br_attn, m_w_out, m_ffn2_norm, m_ffn2_w_gate, m_ffn2_w_up, m_ffn2_w_down, m_ple_norm, m_ple_w_gate, m_ple_w_proj, v_ffn1_norm, v_ffn1_w_gate, v_ffn1_w_up, v_ffn1_w_down, v_mix_norm, v_w_in, v_rwkv_mu, v_rwkv_w0, v_rwkv_w2, v_rwkv_a0, v_rwkv_a2, v_rwkv_g2, v_rwkv_k_k, v_rwkv_k_a, v_rwkv_r_k, v_rwkv_gn_w, v_rwkv_gn_b, v_q_norm, v_k_norm, v_w_br_rwkv, v_w_br_attn, v_w_out, v_ffn2_norm, v_ffn2_w_gate, v_ffn2_w_up, v_ffn2_w_down, v_ple_norm, v_ple_w_gate, v_ple_w_proj):
    args = locals()
    w = {n: args[n][0] for n in WEIGHTS}
    m = {n: args["m_" + n][0] for n in WEIGHTS}
    v = {n: args["v_" + n][0] for n in WEIGHTS}

    w_loc = _local_blocks(w)
    names = list(w_loc)
    wg = dict(zip(names, _all_gather_hbm("all_gather_weights", [w_loc[n] for n in names])))

    sm = {n: w[n].reshape(1, -1) for n in SMALL}
    in_flight = {}

    def scatter_early(group, arrays):
        in_flight[group] = _all_to_all_start("scatter_start_" + group, [arrays[n] for n in SCATTER_GROUPS[group]])
    loss_part, dx, gm, gs = _local_step(x[0], p[0, 0], positions[0], loss_target[0], sm, wg, scatter_early)
    loss = lax.psum(loss_part, ("x", "y", "c"))
    recv, own = {}, {}
    for group, handles in in_flight.items():
        sent, lands = _all_to_all_wait("scatter_wait_" + group, *handles, dx)
        own.update(zip(SCATTER_GROUPS[group], sent))
        recv.update(zip(SCATTER_GROUPS[group], lands))
    head = _all_to_all_hbm("scatter_head", [gm[n] for n in SCATTER_GROUPS["head"]])
    recv.update(zip(SCATTER_GROUPS["head"], head))
    own.update(zip(SCATTER_GROUPS["head"], head))
    small_like = [w[n] for n in SMALL]
    small_rows = 80
    gs_all = _all_gather_vmem(_pack_small([gs[n] for n in SMALL], small_rows))
    gs_sum = _sum_slots("sum_small_grads", gs_all, small_rows)

    res = {}
    for n, src, rb, cb, cw, tr in ADAM_PLAN:
        outs4 = _adamw_slots("adamw_" + n, recv[src], own[src], rb, cb, cw, w[n], m[n], v[n], tr)
        for tag, a in zip(("grad", "delta", "new_m", "new_v"), outs4):
            res[tag, n] = a[None]
    d_s, m_s, v_s = _adamw("adamw_small", _pack_small(small_like, small_rows), gs_sum,
                           _pack_small([m[n] for n in SMALL], small_rows),
                           _pack_small([v[n] for n in SMALL], small_rows), small_rows)
    for tag, small in (("grad", gs_sum), ("delta", d_s), ("new_m", m_s), ("new_v", v_s)):
        for n, a in zip(SMALL, _unpack_small(small, small_like)):
            res[tag, n] = a[None]
    outs = [loss, dx[None]]
    for tag in ("grad", "delta", "new_m", "new_v"):
        outs += [res[tag, n] for n in WEIGHTS]
    return tuple(outs)
```

```python
import functools

import jax
import jax.numpy as jnp
from jax import lax
from jax.experimental import pallas as pl
from jax.experimental.pallas import tpu as pltpu

F32, BF16 = jnp.float32, jnp.bfloat16
MESH = pl.DeviceIdType.MESH
N_DEV = 8
LANES = 128
VMEM_LIMIT = 56 * 1024 * 1024

D_MODEL = 1024
PLE_DIM = 256
HEAD_DIM = 64
RWKV_HEADS = 8
RWKV_DIM = RWKV_HEADS * HEAD_DIM
DECAY_LORA = 64
ICLR_LORA = 64
GATE_LORA = 128
GN_EPS = 64e-5
ATTN_GROUPS = ((128, 1), (512, 4), (2048, 16))
HEADS_PER_GROUP = 4
ATTN_HEADS = HEADS_PER_GROUP * len(ATTN_GROUPS)
ATTN_DIM = ATTN_HEADS * HEAD_DIM
BAND_BLOCK = 128
ROPE_THETA = 10000.0
NEG_INF = -1e30
D_FF = 2816
RMS_EPS = 1e-6
RWKV_COLS = 3 * RWKV_DIM + DECAY_LORA + ICLR_LORA + GATE_LORA
ADAM_LR = 0.001
ADAM_B1 = 0.9
ADAM_B2 = 0.999
ADAM_EPS = 1e-08
ADAM_WD = 0.01
ADAM_STEP = 10

V_LO = LANES // RWKV_HEADS
V_HI = HEAD_DIM // V_LO
SCAN_CHUNK = 32

FF_SHARD = D_FF // N_DEV
FF_PAD = -(-FF_SHARD // LANES) * LANES
FF_HID = N_DEV * FF_PAD
IN_SHARD = 6144 // N_DEV
OUT_SHARD = D_MODEL // N_DEV

SMALL = ("ffn1_norm", "mix_norm", "rwkv_mu", "rwkv_w0", "rwkv_a0", "rwkv_k_k", "rwkv_k_a", "rwkv_r_k",
         "rwkv_gn_w", "rwkv_gn_b", "q_norm", "k_norm", "ffn2_norm", "ple_norm")
WEIGHTS = ("ffn1_norm", "ffn1_w_gate", "ffn1_w_up", "ffn1_w_down", "mix_norm", "w_in", "rwkv_mu", "rwkv_w0",
           "rwkv_w2", "rwkv_a0", "rwkv_a2", "rwkv_g2", "rwkv_k_k", "rwkv_k_a", "rwkv_r_k", "rwkv_gn_w",
           "rwkv_gn_b", "q_norm", "k_norm", "w_br_rwkv", "w_br_attn", "w_out", "ffn2_norm", "ffn2_w_gate",
           "ffn2_w_up", "ffn2_w_down", "ple_norm", "ple_w_gate", "ple_w_proj")


def _cparams(**kw):
    return pltpu.CompilerParams(vmem_limit_bytes=VMEM_LIMIT, **kw)


def _tile(n, cap):
    best = None
    for t in range(LANES, min(n, cap) + 1, LANES):
        if n % t == 0:
            best = t
    return best if best is not None else n


@jax.custom_vjp
def _bdot(a, w):
    return jnp.dot(a.astype(BF16), w.astype(BF16), preferred_element_type=F32)


def _bdot_fwd(a, w):
    return _bdot(a, w), (a, w)


def _bdot_bwd(res, g):
    a, w = res
    gb = g.astype(BF16)
    da = lax.dot_general(gb, w.astype(BF16), (((1,), (1,)), ((), ())), preferred_element_type=F32)
    dw = lax.dot_general(a.astype(BF16), gb, (((0,), (0,)), ((), ())), preferred_element_type=F32)
    return da.astype(a.dtype), dw.astype(w.dtype)


_bdot.defvjp(_bdot_fwd, _bdot_bwd)


@jax.custom_vjp
def _bdot_nt(a, b):
    return lax.dot_general(a.astype(BF16), b.astype(BF16), (((1,), (1,)), ((), ())), preferred_element_type=F32)


def _bdot_nt_fwd(a, b):
    return _bdot_nt(a, b), (a, b)


def _bdot_nt_bwd(res, g):
    a, b = res
    gb = g.astype(BF16)
    da = jnp.dot(gb, b.astype(BF16), preferred_element_type=F32)
    db = lax.dot_general(gb, a.astype(BF16), (((0,), (0,)), ((), ())), preferred_element_type=F32)
    return da.astype(a.dtype), db.astype(b.dtype)


_bdot_nt.defvjp(_bdot_nt_fwd, _bdot_nt_bwd)


def _mm(name, a, b, mode, out_dtype=F32, res=None, scale=None):
    if mode == "nn":
        (m, k), n = a.shape, b.shape[1]
    elif mode == "nt":
        (m, k), n = a.shape, b.shape[0]
    else:
        (k, m), n = a.shape, b.shape[1]
    tm, tn = _tile(m, 512), _tile(n, 512)
    a_spec = pl.BlockSpec((k, tm), lambda i, j: (0, i)) if mode == "tn" else pl.BlockSpec((tm, k), lambda i, j: (i, 0))
    b_spec = pl.BlockSpec((tn, k), lambda i, j: (j, 0)) if mode == "nt" else pl.BlockSpec((k, tn), lambda i, j: (0, j))
    dims = {"nn": ((1,), (0,)), "nt": ((1,), (1,)), "tn": ((0,), (0,))}[mode]
    o_spec = pl.BlockSpec((tm, tn), lambda i, j: (i, j))
    ins, in_specs = [a, b], [a_spec, b_spec]
    if res is not None:
        ins.append(res)
        in_specs.append(o_spec)

    def body(*refs):
        acc = lax.dot_general(refs[0][...].astype(BF16), refs[1][...].astype(BF16), (dims, ((), ())),
                              preferred_element_type=F32)
        if scale is not None:
            acc = acc * scale
        if res is not None:
            acc = acc + refs[2][...].astype(F32)
        refs[-1][...] = acc.astype(refs[-1].dtype)

    return pl.pallas_call(
        body, name=name, grid=(m // tm, n // tn), in_specs=in_specs, out_specs=o_spec,
        out_shape=jax.ShapeDtypeStruct((m, n), out_dtype),
        compiler_params=_cparams(dimension_semantics=("parallel", "parallel")),
    )(*ins)


def _mmc_nn(name, a, wb, ki, ci, n, out_dtype=F32):
    m, k = a.shape
    tm = _tile(m, 512)

    def body(a_ref, w_ref, o_ref):
        o_ref[...] = jnp.dot(a_ref[...].astype(BF16), w_ref[...], preferred_element_type=F32).astype(o_ref.dtype)

    return pl.pallas_call(
        body, name=name, grid=(m // tm, N_DEV),
        in_specs=[pl.BlockSpec((tm, k), lambda i, j: (i, 0)), pl.BlockSpec((None, k, n), lambda i, j: (j, ki, ci))],
        out_specs=pl.BlockSpec((tm, n), lambda i, j: (i, j)),
        out_shape=jax.ShapeDtypeStruct((m, N_DEV * n), out_dtype),
        compiler_params=_cparams(dimension_semantics=("parallel", "parallel")),
    )(a, wb)


def _mmc_nt(name, a, wb, ki, ci, n, k, res=None):
    m = a.shape[0]
    tm = _tile(m, 512)
    o_spec = pl.BlockSpec((tm, k), lambda i, j: (i, 0))
    ins = [a, wb] + ([res] if res is not None else [])
    in_specs = [pl.BlockSpec((tm, n), lambda i, j: (i, j)), pl.BlockSpec((None, k, n), lambda i, j: (j, ki, ci))]
    in_specs += [o_spec] if res is not None else []

    def body(*refs):
        a_ref, w_ref, o_ref = refs[0], refs[1], refs[-1]
        acc = lax.dot_general(a_ref[...].astype(BF16), w_ref[...], (((1,), (1,)), ((), ())),
                              preferred_element_type=F32)

        @pl.when(pl.program_id(1) == 0)
        def _():
            o_ref[...] = acc + refs[2][...] if res is not None else acc

        @pl.when(pl.program_id(1) != 0)
        def _():
            o_ref[...] += acc

    return pl.pallas_call(
        body, name=name, grid=(m // tm, N_DEV), in_specs=in_specs, out_specs=o_spec,
        out_shape=jax.ShapeDtypeStruct((m, k), F32),
        compiler_params=_cparams(dimension_semantics=("parallel", "arbitrary")),
    )(*ins)


def _mmc_tn(name, x, dy, n):
    m, k = x.shape
    tk = _tile(k, 512)

    def body(x_ref, dy_ref, o_ref):
        o_ref[...] = lax.dot_general(x_ref[...].astype(BF16), dy_ref[...].astype(BF16), (((0,), (0,)), ((), ())),
                                     preferred_element_type=F32).astype(o_ref.dtype)

    return pl.pallas_call(
        body, name=name, grid=(N_DEV, k // tk),
        in_specs=[pl.BlockSpec((m, tk), lambda j, i: (0, i)), pl.BlockSpec((m, n), lambda j, i: (0, j))],
        out_specs=pl.BlockSpec((None, tk, n), lambda j, i: (j, i, 0)),
        out_shape=jax.ShapeDtypeStruct((N_DEV, k, n), BF16),
        compiler_params=_cparams(dimension_semantics=("parallel", "parallel")),
    )(x, dy)


def _rowwise(name, fn, rows, params, out_rows, out_accs=(), tr=256):
    r = rows[0].shape[0]
    in_specs = [pl.BlockSpec((tr, a.shape[1]), lambda i: (i, 0)) for a in rows]
    in_specs += [pl.BlockSpec(p.shape, lambda i, nd=p.ndim: (0,) * nd) for p in params]
    out_shape = [jax.ShapeDtypeStruct((r, c), dt) for c, dt in out_rows]
    out_shape += [jax.ShapeDtypeStruct(s, F32) for s in out_accs]
    out_specs = [pl.BlockSpec((tr, c), lambda i: (i, 0)) for c, _ in out_rows]
    out_specs += [pl.BlockSpec(s, lambda i, nd=len(s): (0,) * nd) for s in out_accs]
    n_in, n_ro = len(rows) + len(params), len(out_rows)

    def body(*refs):
        res = fn(*[ref[...] for ref in refs[:n_in]])
        outs = refs[n_in:]
        for o, v in zip(outs[:n_ro], res[:n_ro]):
            o[...] = v.astype(o.dtype)
        for o, v in zip(outs[n_ro:], res[n_ro:]):
            _accumulate(o, v)

    return pl.pallas_call(
        body, name=name, grid=(r // tr,), in_specs=in_specs, out_specs=out_specs, out_shape=out_shape,
        compiler_params=_cparams(dimension_semantics=("arbitrary",)),
    )(*rows, *params)


def _accumulate(o_ref, v):
    @pl.when(pl.program_id(0) == 0)
    def _():
        o_ref[...] = v

    @pl.when(pl.program_id(0) != 0)
    def _():
        o_ref[...] += v


def _rms(x, g):
    return x * lax.rsqrt(jnp.mean(x * x, axis=-1, keepdims=True) + RMS_EPS) * g


def _sigmoid(x):
    return jax.nn.sigmoid(x)


def _softplus(x):
    return jnp.maximum(x, 0.0) + jnp.log1p(jnp.exp(-jnp.abs(x)))


def _norm_fwd(name, x, g):
    return _rowwise(name, lambda xv, gv: (_rms(xv, gv),), [x], [g], [(x.shape[1], BF16)])[0]


def _norm_bwd(name, x, g, dh, dres):
    def fn(xv, dhv, drv, gv):
        _, vjp = jax.vjp(_rms, xv, gv)
        dx, dg = vjp(dhv)
        return dx + drv, dg
    return _rowwise(name, fn, [x, dh, dres], [g], [(x.shape[1], F32)], [g.shape])


def _f_act(gate, up):
    return gate * _sigmoid(gate) * up


def _gate_up(guv, j):
    base = j * 2 * FF_PAD
    return guv[:, base:base + FF_PAD], guv[:, base + FF_PAD:base + 2 * FF_PAD]


def _ffn_fwd(tag, x, norm, w_gu, w_down):
    h = _norm_fwd(tag + "_norm", x, norm)
    gu = _mmc_nn(tag + "_gu", h, w_gu, 0, 0, 2 * FF_PAD)

    def act(guv):
        return (jnp.concatenate([_f_act(*_gate_up(guv, j)) for j in range(N_DEV)], axis=1),)
    a = _rowwise(tag + "_act", act, [gu], [], [(FF_HID, BF16)])[0]
    out = _mm(tag + "_down", a, w_down, "nn", res=x, scale=0.5)
    return out, (h, gu, a)


def _ffn_bwd(tag, x, norm, w_gu, w_down, saved, dout):
    h, gu, a = saved
    da = _mm(tag + "_dact", dout, w_down, "nt", scale=0.5)
    d_wdown = _mm(tag + "_dwdown", a, dout, "tn", out_dtype=BF16, scale=0.5)

    def act_bwd(guv, dav):
        outs = []
        for j in range(N_DEV):
            _, vjp = jax.vjp(_f_act, *_gate_up(guv, j))
            outs += list(vjp(dav[:, j * FF_PAD:(j + 1) * FF_PAD]))
        return (jnp.concatenate(outs, axis=1),)
    dgu = _rowwise(tag + "_dgu", act_bwd, [gu, da], [], [(2 * FF_HID, BF16)])[0]
    dh = _mmc_nt(tag + "_dh", dgu, w_gu, 0, 0, 2 * FF_PAD, D_MODEL)
    d_wgu = _mmc_tn(tag + "_dwgu", h, dgu, 2 * FF_PAD)
    dx, dnorm = _norm_bwd(tag + "_dnorm", x, norm, dh, dout)
    return dx, dnorm, d_wgu, d_wdown.reshape(N_DEV, FF_PAD, D_MODEL)


def _shift_fwd(z, mu):
    t, c = z.shape
    tr = 256

    def body(z_ref, zp_ref, mu_ref, r_ref, k_ref, v_ref, lo_ref, gd_ref):
        zv = z_ref[...]
        prev = zp_ref[7:8, :] * jnp.where(pl.program_id(0) == 0, 0.0, 1.0)
        row = lax.broadcasted_iota(jnp.int32, zv.shape, 0)
        zsh = jnp.where(row == 0, prev, pltpu.roll(zv, 1, 0))
        zs = zv + (zsh - zv) * mu_ref[...]
        r_ref[...] = zs[:, 0:512]
        k_ref[...] = zs[:, 512:1024]
        v_ref[...] = zs[:, 1024:1536]
        lo_ref[...] = zs[:, 1536:1664]
        gd_ref[...] = zs[:, 1664:1792]

    widths = (512, 512, 512, 128, 128)
    return pl.pallas_call(
        body, name="rwkv_shift", grid=(t // tr,),
        in_specs=[pl.BlockSpec((tr, c), lambda i: (i, 0)),
                  pl.BlockSpec((8, c), lambda i: (jnp.maximum(i * (tr // 8) - 1, 0), 0)),
                  pl.BlockSpec((1, c), lambda i: (0, 0))],
        out_specs=[pl.BlockSpec((tr, w), lambda i: (i, 0)) for w in widths],
        out_shape=[jax.ShapeDtypeStruct((t, w), F32) for w in widths],
        compiler_params=_cparams(dimension_semantics=("parallel",)),
    )(z, z, mu)


def _shift_bwd(z, mu, dr, dk, dv, dlo, dgd):
    t, c = z.shape
    tr = 256
    nt = t // tr

    def body(z_ref, zp_ref, mu_ref, dr_ref, dk_ref, dv_ref, dlo_ref, dgd_ref,
             drn_ref, dkn_ref, dvn_ref, dlon_ref, dgdn_ref, dz_ref, dmu_ref):
        i = pl.program_id(0)
        zv, muv = z_ref[...], mu_ref[...]
        prev = zp_ref[7:8, :] * jnp.where(i == 0, 0.0, 1.0)
        row = lax.broadcasted_iota(jnp.int32, zv.shape, 0)
        zsh = jnp.where(row == 0, prev, pltpu.roll(zv, 1, 0))
        dzs = jnp.concatenate([dr_ref[...], dk_ref[...], dv_ref[...], dlo_ref[...], dgd_ref[...]], axis=1)
        nxt = jnp.concatenate([drn_ref[0:1, :], dkn_ref[0:1, :], dvn_ref[0:1, :], dlon_ref[0:1, :],
                               dgdn_ref[0:1, :]], axis=1) * jnp.where(i == nt - 1, 0.0, 1.0)
        u = dzs * muv
        un = jnp.where(row == tr - 1, nxt * muv, pltpu.roll(u, tr - 1, 0))
        dz_ref[...] = (dzs - u + un).astype(dz_ref.dtype)
        _accumulate(dmu_ref, jnp.sum(dzs * (zsh - zv), axis=0, keepdims=True))

    widths = (512, 512, 512, 128, 128)
    nxt_map = lambda i: (jnp.minimum((i + 1) * (tr // 8), t // 8 - 1), 0)
    return pl.pallas_call(
        body, name="rwkv_shift_bwd", grid=(nt,),
        in_specs=[pl.BlockSpec((tr, c), lambda i: (i, 0)),
                  pl.BlockSpec((8, c), lambda i: (jnp.maximum(i * (tr // 8) - 1, 0), 0)),
                  pl.BlockSpec((1, c), lambda i: (0, 0))]
        + [pl.BlockSpec((tr, w), lambda i: (i, 0)) for w in widths]
        + [pl.BlockSpec((8, w), nxt_map) for w in widths],
        out_specs=[pl.BlockSpec((tr, c), lambda i: (i, 0)), pl.BlockSpec((1, c), lambda i: (0, 0))],
        out_shape=[jax.ShapeDtypeStruct((t, c), BF16), jax.ShapeDtypeStruct((1, c), F32)],
        compiler_params=_cparams(dimension_semantics=("arbitrary",)),
    )(z, z, mu, dr, dk, dv, dlo, dgd, dr, dk, dv, dlo, dgd)


def _f_pre(k, lo, gd, w0, w2p, a0, a2p, g2, k_k, k_a):
    lane = lax.broadcasted_iota(jnp.int32, lo.shape, 1)
    lo_act = jnp.where(lane < DECAY_LORA, jnp.tanh(lo), lo)
    w = -_softplus(-(w0 + _bdot(lo_act, w2p))) - 0.5
    a = _sigmoid(a0 + _bdot(lo_act, a2p))
    g = _bdot(_sigmoid(gd), g2)
    kk = k * k_k
    k2 = k * (1.0 + (a - 1.0) * k_a)
    decay = jnp.exp(-jnp.exp(w))
    return k2, kk, a, decay, g


def _f_kk(kk, a):
    kkn = kk * lax.rsqrt(jnp.maximum(jnp.sum(kk * kk, axis=-1, keepdims=True), 1e-24))
    return kkn, kkn * a


def _f_post(y, r, k2, v, g, gn_w, gn_b, r_k):
    reps = y.shape[0] // RWKV_HEADS
    mean = jnp.mean(y, axis=-1, keepdims=True)
    var = jnp.mean(jnp.square(y - mean), axis=-1, keepdims=True)
    yn = (y - mean) * lax.rsqrt(var + GN_EPS)
    yn = yn * jnp.tile(gn_w, (reps, 1)) + jnp.tile(gn_b, (reps, 1))
    bonus = jnp.sum(r * k2 * jnp.tile(r_k, (reps, 1)), axis=-1, keepdims=True) * v
    return (yn + bonus) * g


def _to_v_rows(x):
    t = x.shape[0]
    return x.reshape(t, RWKV_HEADS, V_HI, V_LO).transpose(0, 2, 3, 1).reshape(t, V_HI, LANES)


def _from_v_rows(x):
    t = x.shape[0]
    return x.reshape(t, V_HI, V_LO, RWKV_HEADS).transpose(0, 3, 1, 2).reshape(t, RWKV_DIM)


def _k_cols(x):
    return jnp.tile(x, (V_LO, 1)).T


def _k_rows(x):
    xt = x.T
    out = xt[0:RWKV_HEADS]
    for l in range(1, V_LO):
        out = out + xt[l * RWKV_HEADS:(l + 1) * RWKV_HEADS]
    return out


def _wkv_fwd(r, w, k, kk, b, v):
    t = r.shape[0]
    tc = SCAN_CHUNK
    key_spec = pl.BlockSpec((tc, RWKV_HEADS, HEAD_DIM), lambda i: (i, 0, 0))
    row_spec = pl.BlockSpec((tc, V_HI, LANES), lambda i: (i, 0, 0))

    def body(r_ref, w_ref, k_ref, kk_ref, b_ref, v_ref, y_ref, st_ref, s_scr):
        @pl.when(pl.program_id(0) == 0)
        def _():
            s_scr[...] = jnp.zeros_like(s_scr)

        def step(ti, s):
            rc, wc, kc, kkc, bc = (_k_cols(ref[ti]) for ref in (r_ref, w_ref, k_ref, kk_ref, b_ref))
            vt = v_ref[ti]
            new, ys = [], []
            for j in range(V_HI):
                sa = -jnp.sum(s[j] * kkc, axis=0, keepdims=True)
                nj = s[j] * wc + bc * sa + kc * vt[j:j + 1]
                st_ref[ti, j] = nj
                ys.append(jnp.sum(nj * rc, axis=0, keepdims=True))
                new.append(nj)
            y_ref[ti] = jnp.concatenate(ys, axis=0)
            return tuple(new)

        s = lax.fori_loop(0, tc, step, tuple(s_scr[j] for j in range(V_HI)))
        for j in range(V_HI):
            s_scr[j] = s[j]

    return pl.pallas_call(
        body, name="wkv_fwd", grid=(t // tc,),
        in_specs=[key_spec] * 5 + [row_spec],
        out_specs=[row_spec, pl.BlockSpec((tc, V_HI, HEAD_DIM, LANES), lambda i: (i, 0, 0, 0))],
        out_shape=[jax.ShapeDtypeStruct((t, V_HI, LANES), F32),
                   jax.ShapeDtypeStruct((t, V_HI, HEAD_DIM, LANES), F32)],
        scratch_shapes=[pltpu.VMEM((V_HI, HEAD_DIM, LANES), F32)],
        compiler_params=_cparams(dimension_semantics=("arbitrary",)),
    )(r, w, k, kk, b, v)


def _wkv_bwd(r, w, k, kk, b, v, states, dy):
    t = r.shape[0]
    tc = SCAN_CHUNK
    nb = t // tc
    key_spec = pl.BlockSpec((tc, RWKV_HEADS, HEAD_DIM), lambda i: (nb - 1 - i, 0, 0))
    row_spec = pl.BlockSpec((tc, V_HI, LANES), lambda i: (nb - 1 - i, 0, 0))
    st_spec = pl.BlockSpec((tc, V_HI, HEAD_DIM, LANES), lambda i: (nb - 1 - i, 0, 0, 0))
    stp_spec = pl.BlockSpec((1, V_HI, HEAD_DIM, LANES), lambda i: (jnp.maximum((nb - 1 - i) * tc - 1, 0), 0, 0, 0))

    def body(r_ref, w_ref, k_ref, kk_ref, b_ref, v_ref, st_ref, stp_ref, dy_ref,
             dr_ref, dw_ref, dk_ref, dkk_ref, db_ref, dv_ref, ds_scr):
        @pl.when(pl.program_id(0) == 0)
        def _():
            ds_scr[...] = jnp.zeros_like(ds_scr)

        def colsum(x):
            return jnp.sum(x, axis=0, keepdims=True)

        def step(ti, ds, sp):
            rc, wc, kc, kkc, bc = (_k_cols(ref[ti]) for ref in (r_ref, w_ref, k_ref, kk_ref, b_ref))
            vt, dyt = v_ref[ti], dy_ref[ti]
            acc_r = acc_k = acc_b = acc_w = acc_kk = None
            new, dvs = [], []
            for j in range(V_HI):
                st = st_ref[ti, j]
                dsj = ds[j] + rc * dyt[j:j + 1]
                sa = -colsum(sp[j] * kkc)
                dsa = colsum(dsj * bc)
                dvs.append(colsum(dsj * kc))
                parts = (st * dyt[j:j + 1], dsj * vt[j:j + 1], dsj * sa, dsj * sp[j], sp[j] * dsa)
                if j == 0:
                    acc_r, acc_k, acc_b, acc_w, acc_kk = parts
                else:
                    acc_r, acc_k, acc_b, acc_w, acc_kk = (a + p for a, p in
                                                          zip((acc_r, acc_k, acc_b, acc_w, acc_kk), parts))
                new.append(dsj * wc - kkc * dsa)
            dv_ref[ti] = jnp.concatenate(dvs, axis=0)
            dr_ref[ti] = _k_rows(acc_r)
            dk_ref[ti] = _k_rows(acc_k)
            db_ref[ti] = _k_rows(acc_b)
            dw_ref[ti] = _k_rows(acc_w)
            dkk_ref[ti] = -_k_rows(acc_kk)
            return tuple(new)

        def loop_body(n, ds):
            ti = tc - 1 - n
            return step(ti, ds, tuple(st_ref[ti - 1, j] for j in range(V_HI)))

        ds = lax.fori_loop(0, tc - 1, loop_body, tuple(ds_scr[j] for j in range(V_HI)))
        keep = jnp.where(pl.program_id(0) == nb - 1, 0.0, 1.0)
        ds = step(0, ds, tuple(stp_ref[0, j] * keep for j in range(V_HI)))
        for j in range(V_HI):
            ds_scr[j] = ds[j]

    key_out = jax.ShapeDtypeStruct((t, RWKV_HEADS, HEAD_DIM), F32)
    return pl.pallas_call(
        body, name="wkv_bwd", grid=(nb,),
        in_specs=[key_spec] * 5 + [row_spec, st_spec, stp_spec, row_spec],
        out_specs=[key_spec] * 5 + [row_spec],
        out_shape=[key_out] * 5 + [jax.ShapeDtypeStruct((t, V_HI, LANES), F32)],
        scratch_shapes=[pltpu.VMEM((V_HI, HEAD_DIM, LANES), F32)],
        compiler_params=_cparams(dimension_semantics=("arbitrary",)),
    )(r, w, k, kk, b, v, states, states, dy)


PAIR = 2 * HEAD_DIM
N_PAIRS = ATTN_HEADS // 2
Q_COL0 = RWKV_COLS // PAIR
K_COL0 = Q_COL0 + N_PAIRS
V_COL0 = K_COL0 + N_PAIRS


def _swap_halves(x):
    lane = lax.broadcasted_iota(jnp.int32, x.shape, 1)
    return jnp.where((lane & (HEAD_DIM - 1)) < HEAD_DIM // 2, pltpu.roll(x, PAIR - HEAD_DIM // 2, 1),
                     pltpu.roll(x, HEAD_DIM // 2, 1))


@jax.custom_vjp
def _rope(x, cosf, sinf):
    return x * cosf + _swap_halves(x) * sinf


def _rope_fwd(x, cosf, sinf):
    return _rope(x, cosf, sinf), (cosf, sinf)


def _rope_bwd(res, d):
    cosf, sinf = res
    return d * cosf + _swap_halves(d * sinf), jnp.zeros_like(cosf), jnp.zeros_like(sinf)


_rope.defvjp(_rope_fwd, _rope_bwd)


def _head_sums(x):
    lane = lax.broadcasted_iota(jnp.int32, x.shape, 1)
    lo = jnp.where(lane < HEAD_DIM, 1.0, 0.0)
    hi = 1.0 - lo
    return lo * jnp.sum(x * lo, axis=1, keepdims=True) + hi * jnp.sum(x * hi, axis=1, keepdims=True)


def _f_qk(x, cosf, sinf, gain2):
    xn = x * lax.rsqrt(_head_sums(x * x) * (1.0 / HEAD_DIM) + RMS_EPS) * gain2
    return _rope(xn, cosf, sinf)


def _qk_prep(z, tab, q_gain, k_gain):
    t = z.shape[0]
    tr = 256

    def body(z_ref, c_ref, s_ref, qg_ref, kg_ref, o_ref):
        g = jnp.where(pl.program_id(0) < N_PAIRS, qg_ref[...], kg_ref[...])
        o_ref[...] = _f_qk(z_ref[...], c_ref[...], s_ref[...], jnp.concatenate([g, g], axis=1))

    gain = pl.BlockSpec((1, HEAD_DIM), lambda c, i: (0, 0))
    return pl.pallas_call(
        body, name="qk_prep", grid=(2 * N_PAIRS, t // tr),
        in_specs=[pl.BlockSpec((tr, PAIR), lambda c, i: (i, Q_COL0 + c)), pl.BlockSpec((tr, PAIR), lambda c, i: (i, 0)),
                  pl.BlockSpec((tr, PAIR), lambda c, i: (i, 1)), gain, gain],
        out_specs=pl.BlockSpec((tr, PAIR), lambda c, i: (i, c)),
        out_shape=jax.ShapeDtypeStruct((t, 2 * N_PAIRS * PAIR), F32),
        compiler_params=_cparams(dimension_semantics=("parallel", "parallel")),
    )(z, tab, tab, q_gain, k_gain)


def _qk_prep_bwd(z, tab, q_gain, k_gain, dq, dk):
    t = z.shape[0]
    tr = 256

    def body(z_ref, c_ref, s_ref, qg_ref, kg_ref, dq_ref, dk_ref, dz_ref, dqg_ref, dkg_ref):
        c, i = pl.program_id(0), pl.program_id(1)
        is_q = c < N_PAIRS
        g = jnp.where(is_q, qg_ref[...], kg_ref[...])
        d = jnp.where(is_q, dq_ref[...], dk_ref[...])
        _, vjp = jax.vjp(lambda xx, gg: _f_qk(xx, c_ref[...], s_ref[...], gg), z_ref[...],
                         jnp.concatenate([g, g], axis=1))
        dx, dg2 = vjp(d)
        dz_ref[...] = dx.astype(dz_ref.dtype)
        dg = dg2[:, :HEAD_DIM] + dg2[:, HEAD_DIM:]
        first_q = jnp.logical_and(c == 0, i == 0)
        first_k = jnp.logical_and(c == N_PAIRS, i == 0)

        @pl.when(first_q)
        def _():
            dqg_ref[...] = dg

        @pl.when(jnp.logical_and(is_q, jnp.logical_not(first_q)))
        def _():
            dqg_ref[...] += dg

        @pl.when(first_k)
        def _():
            dkg_ref[...] = dg

        @pl.when(jnp.logical_and(jnp.logical_not(is_q), jnp.logical_not(first_k)))
        def _():
            dkg_ref[...] += dg

    gain = pl.BlockSpec((1, HEAD_DIM), lambda c, i: (0, 0))
    return pl.pallas_call(
        body, name="qk_prep_bwd", grid=(2 * N_PAIRS, t // tr),
        in_specs=[pl.BlockSpec((tr, PAIR), lambda c, i: (i, Q_COL0 + c)), pl.BlockSpec((tr, PAIR), lambda c, i: (i, 0)),
                  pl.BlockSpec((tr, PAIR), lambda c, i: (i, 1)), gain, gain,
                  pl.BlockSpec((tr, PAIR), lambda c, i: (i, jnp.minimum(c, N_PAIRS - 1))),
                  pl.BlockSpec((tr, PAIR), lambda c, i: (i, jnp.maximum(c - N_PAIRS, 0)))],
        out_specs=[pl.BlockSpec((tr, PAIR), lambda c, i: (i, c)), gain, gain],
        out_shape=[jax.ShapeDtypeStruct((t, 2 * N_PAIRS * PAIR), BF16), jax.ShapeDtypeStruct((1, HEAD_DIM), F32),
                   jax.ShapeDtypeStruct((1, HEAD_DIM), F32)],
        compiler_params=_cparams(dimension_semantics=("arbitrary", "arbitrary")),
    )(z, tab, tab, q_gain, k_gain, dq, dk)


def _attn_block(q, kp, kc, vp, vc, kmin):
    k2 = jnp.concatenate([kp, kc], axis=0)
    v2 = jnp.concatenate([vp, vc], axis=0)
    s = _bdot_nt(q, k2) * (HEAD_DIM ** -0.5)
    qi = lax.broadcasted_iota(jnp.int32, s.shape, 0)
    kj = lax.broadcasted_iota(jnp.int32, s.shape, 1)
    dist = qi + BAND_BLOCK - kj
    valid = (dist >= 0) & (dist <= BAND_BLOCK) & (kj >= kmin)
    s = jnp.where(valid, s, NEG_INF)
    m = lax.stop_gradient(jnp.max(s, axis=-1, keepdims=True))
    e = jnp.exp(s - m)
    l = jnp.sum(e, axis=-1, keepdims=True)
    o = _bdot(e, v2) / l
    return o, m + jnp.log(l)


def _fold(src_ref, dst_ref, dil):
    t = src_ref.shape[0]
    ln = t // dil
    for j in range(dil):
        dst_ref[j * ln:(j + 1) * ln, :] = src_ref[pl.ds(j, ln, stride=dil), :]


def _unfold(src_ref, dst_ref, dil):
    t = src_ref.shape[0]
    ln = t // dil
    for j in range(dil):
        dst_ref[pl.ds(j, ln, stride=dil), :] = src_ref[j * ln:(j + 1) * ln, :]


def _per_group(fn):
    pair = pl.program_id(0)
    for gi, (_, dil) in enumerate(ATTN_GROUPS):
        @pl.when(jnp.logical_or(pair == 2 * gi, pair == 2 * gi + 1))
        def _(dil=dil):
            fn(dil)


def _block_rows(idx, blocks_per_seq):
    first = (idx & (blocks_per_seq - 1)) == 0
    cur = pl.ds(pl.multiple_of(idx * BAND_BLOCK, BAND_BLOCK), BAND_BLOCK)
    prev = pl.ds(pl.multiple_of(jnp.maximum(idx - 1, 0) * BAND_BLOCK, BAND_BLOCK), BAND_BLOCK)
    return first, cur, prev


def _heads(x):
    return x[:, :HEAD_DIM], x[:, HEAD_DIM:]


def _attn_fwd(qk, z):
    t = z.shape[0]
    n_blocks = t // BAND_BLOCK

    def body(q_ref, k_ref, v_ref, o_ref, lse_ref, qf, kf, vf, of, lf):
        def run(dil):
            _fold(q_ref, qf, dil)
            _fold(k_ref, kf, dil)
            _fold(v_ref, vf, dil)
            blocks_per_seq = n_blocks // dil

            def block(idx, carry):
                first, cur, prev = _block_rows(idx, blocks_per_seq)
                kmin = jnp.where(first, BAND_BLOCK, 0)
                outs, lses = [], []
                for q, kp, kc, vp, vc in zip(_heads(qf[cur, :]), _heads(kf[prev, :]), _heads(kf[cur, :]),
                                             _heads(vf[prev, :]), _heads(vf[cur, :])):
                    o, ls = _attn_block(q, kp, kc, vp, vc, kmin)
                    outs.append(o)
                    lses.append(jnp.broadcast_to(ls, o.shape))
                of[cur, :] = jnp.concatenate(outs, axis=1)
                lf[cur, :] = jnp.concatenate(lses, axis=1)
                return carry

            lax.fori_loop(0, n_blocks, block, 0)
            _unfold(of, o_ref, dil)
            _unfold(lf, lse_ref, dil)

        _per_group(run)

    slab = jax.ShapeDtypeStruct((t, N_PAIRS * PAIR), F32)
    out_spec = pl.BlockSpec((t, PAIR), lambda p: (0, p))
    return pl.pallas_call(
        body, name="attn_fwd", grid=(N_PAIRS,),
        in_specs=[pl.BlockSpec((t, PAIR), lambda p: (0, p)), pl.BlockSpec((t, PAIR), lambda p: (0, N_PAIRS + p)),
                  pl.BlockSpec((t, PAIR), lambda p: (0, V_COL0 + p))],
        out_specs=[out_spec, out_spec], out_shape=[slab, slab],
        scratch_shapes=[pltpu.VMEM((t, PAIR), F32)] * 5,
        compiler_params=_cparams(dimension_semantics=("parallel",)),
    )(qk, qk, z)


def _attn_bwd(qk, z, do, dlse):
    t = z.shape[0]
    n_blocks = t // BAND_BLOCK

    def body(q_ref, k_ref, v_ref, do_ref, dl_ref, dq_ref, dk_ref, dv_ref, qf, kf, vf, dof, dlf, dqf, dkf, dvf):
        def run(dil):
            for src, dst in ((q_ref, qf), (k_ref, kf), (v_ref, vf), (do_ref, dof), (dl_ref, dlf)):
                _fold(src, dst, dil)
            blocks_per_seq = n_blocks // dil

            def block(idx, carry):
                first, cur, prev = _block_rows(idx, blocks_per_seq)
                kmin = jnp.where(first, BAND_BLOCK, 0)
                grads = []
                for q, kp, kc, vp, vc, do_h, dl_h in zip(
                        _heads(qf[cur, :]), _heads(kf[prev, :]), _heads(kf[cur, :]), _heads(vf[prev, :]),
                        _heads(vf[cur, :]), _heads(dof[cur, :]), _heads(dlf[cur, :])):
                    _, vjp = jax.vjp(functools.partial(_attn_block, kmin=kmin), q, kp, kc, vp, vc)
                    grads.append(vjp((do_h, jnp.sum(dl_h, axis=1, keepdims=True))))
                dq, dkp, dkc, dvp, dvc = (jnp.concatenate([a, b], axis=1) for a, b in zip(*grads))
                dqf[cur, :] = dq
                dkf[cur, :] = dkc
                dvf[cur, :] = dvc

                @pl.when(jnp.logical_not(first))
                def _():
                    dkf[prev, :] += dkp
                    dvf[prev, :] += dvp

                return carry

            lax.fori_loop(0, n_blocks, block, 0)
            _unfold(dqf, dq_ref, dil)
            _unfold(dkf, dk_ref, dil)
            _unfold(dvf, dv_ref, dil)

        _per_group(run)

    slab = jax.ShapeDtypeStruct((t, N_PAIRS * PAIR), F32)
    own = pl.BlockSpec((t, PAIR), lambda p: (0, p))
    return pl.pallas_call(
        body, name="attn_bwd", grid=(N_PAIRS,),
        in_specs=[own, pl.BlockSpec((t, PAIR), lambda p: (0, N_PAIRS + p)),
                  pl.BlockSpec((t, PAIR), lambda p: (0, V_COL0 + p)), own, own],
        out_specs=[own] * 3, out_shape=[slab] * 3,
        scratch_shapes=[pltpu.VMEM((t, PAIR), F32)] * 8,
        compiler_params=_cparams(dimension_semantics=("parallel",)),
    )(qk, qk, z, do, dlse)


def _f_comb(o1, o2, o3, l1, l2, l3):
    m = jnp.maximum(jnp.maximum(l1, l2), l3)
    e1, e2, e3 = jnp.exp(l1 - m), jnp.exp(l2 - m), jnp.exp(l3 - m)
    den = e1 + e2 + e3
    return (e1 / den) * o1 + (e2 / den) * o2 + (e3 / den) * o3


def _all_gather_hbm(name, arrs):
    na = len(arrs)

    def body(*refs):
        x_refs, out_refs = refs[:na], refs[na:2 * na]
        send_sems, recv_sems, local_sems = refs[2 * na:]
        mx, my, mc = lax.axis_index("x"), lax.axis_index("y"), lax.axis_index("c")
        me, sibling = (mx, my, mc), (mx, my, 1 - mc)
        chips = [(1 - mx, my), (mx, 1 - my), (1 - mx, 1 - my)]

        def slot(a, px, py, pc):
            return out_refs[a].at[4 * px + 2 * py + pc]

        def copy(a, k, block, to, src=None):
            return pltpu.make_async_remote_copy(
                src_ref=slot(a, *block) if src is None else src, dst_ref=slot(a, *block),
                send_sem=send_sems.at[a, k], recv_sem=recv_sems.at[a, k], device_id=to, device_id_type=MESH)

        mine = [pltpu.make_async_copy(x_refs[a], slot(a, *me), local_sems.at[a]) for a in range(na)]
        for cp in mine:
            cp.start()
        first = []
        for a in range(na):
            first.append(copy(a, 0, me, sibling, src=x_refs[a]))
            first += [copy(a, 1 + j, me, (*chip, mc), src=x_refs[a]) for j, chip in enumerate(chips)]
        for cp in first:
            cp.start()
        passed = []
        for j, chip in enumerate(chips):
            for a in range(na):
                copy(a, 1 + j, (*chip, mc), me).wait_recv()
                passed.append(copy(a, 4 + j, (*chip, mc), sibling))
                passed[-1].start()
        for a in range(na):
            copy(a, 0, sibling, me).wait_recv()
            for j, chip in enumerate(chips):
                copy(a, 4 + j, (*chip, 1 - mc), me).wait_recv()
        for cp in first + passed:
            cp.wait_send()
        for cp in mine:
            cp.wait()

    hbm = pl.BlockSpec(memory_space=pl.ANY)
    return pl.pallas_call(
        body, name=name,
        out_shape=[jax.ShapeDtypeStruct((N_DEV,) + a.shape, a.dtype) for a in arrs],
        in_specs=[hbm] * na, out_specs=[hbm] * na,
        scratch_shapes=[pltpu.SemaphoreType.DMA((na, 7)), pltpu.SemaphoreType.DMA((na, 7)),
                        pltpu.SemaphoreType.DMA((na,))],
    )(*arrs)


def _all_gather_vmem(x):
    rws, cols = x.shape

    def body(x_ref, out_ref, send_sems, recv_sems):
        mx, my, mc = lax.axis_index("x"), lax.axis_index("y"), lax.axis_index("c")
        me, sibling = (mx, my, mc), (mx, my, 1 - mc)
        chips = [(1 - mx, my), (mx, 1 - my), (1 - mx, 1 - my)]

        def slot(px, py, pc):
            return out_ref.at[4 * px + 2 * py + pc]

        def copy(k, block, to, src=None):
            return pltpu.make_async_remote_copy(
                src_ref=slot(*block) if src is None else src, dst_ref=slot(*block),
                send_sem=send_sems.at[k], recv_sem=recv_sems.at[k], device_id=to, device_id_type=MESH)

        first = [copy(0, me, sibling, src=x_ref)]
        first += [copy(1 + j, me, (*chip, mc), src=x_ref) for j, chip in enumerate(chips)]
        for cp in first:
            cp.start()
        out_ref[4 * mx + 2 * my + mc] = x_ref[...]
        passed = [copy(4 + j, (*chip, mc), sibling) for j, chip in enumerate(chips)]
        for j, chip in enumerate(chips):
            copy(1 + j, (*chip, mc), me).wait_recv()
            passed[j].start()
        copy(0, sibling, me).wait_recv()
        for j, chip in enumerate(chips):
            copy(4 + j, (*chip, 1 - mc), me).wait_recv()
        for cp in first + passed:
            cp.wait_send()

    return pl.pallas_call(
        body, name="all_gather_small",
        out_shape=jax.ShapeDtypeStruct((N_DEV, rws, cols), x.dtype),
        in_specs=[pl.BlockSpec(memory_space=pltpu.VMEM)], out_specs=pl.BlockSpec(memory_space=pltpu.VMEM),
        scratch_shapes=[pltpu.SemaphoreType.DMA((7,)), pltpu.SemaphoreType.DMA((7,))],
    )(x)


def _all_to_all_hbm(name, arrs):
    na = len(arrs)

    def body(*refs):
        g_refs, out_refs = refs[:na], refs[na:2 * na]
        send_sems, recv_sems, local_sems = refs[2 * na:]
        mx, my, mc = lax.axis_index("x"), lax.axis_index("y"), lax.axis_index("c")
        me = 4 * mx + 2 * my + mc
        mine = [pltpu.make_async_copy(g_refs[a].at[me], out_refs[a].at[me], local_sems.at[a]) for a in range(na)]
        for cp in mine:
            cp.start()
        copies = []
        for k in range(1, N_DEV):
            px, py, pc = mx ^ (k >> 2), my ^ ((k >> 1) & 1), mc ^ (k & 1)
            peer = 4 * px + 2 * py + pc
            for a in range(na):
                copies.append(pltpu.make_async_remote_copy(
                    src_ref=g_refs[a].at[peer], dst_ref=out_refs[a].at[me], send_sem=send_sems.at[a, k - 1],
                    recv_sem=recv_sems.at[a, k - 1], device_id=(px, py, pc), device_id_type=MESH))
        for cp in copies:
            cp.start()
        for cp in copies:
            cp.wait_recv()
        for cp in copies:
            cp.wait_send()
        for cp in mine:
            cp.wait()

    hbm = pl.BlockSpec(memory_space=pl.ANY)
    return pl.pallas_call(
        body, name=name,
        out_shape=[jax.ShapeDtypeStruct(a.shape, a.dtype) for a in arrs],
        in_specs=[hbm] * na, out_specs=[hbm] * na,
        scratch_shapes=[pltpu.SemaphoreType.DMA((na, 7)), pltpu.SemaphoreType.DMA((na, 7)),
                        pltpu.SemaphoreType.DMA((na,))],
    )(*arrs)


def _scatter_copies(g_refs, land_refs, send_sems, recv_sems):
    mx, my, mc = lax.axis_index("x"), lax.axis_index("y"), lax.axis_index("c")
    me = 4 * mx + 2 * my + mc
    copies = []
    for k in range(1, N_DEV):
        px, py, pc = mx ^ (k >> 2), my ^ ((k >> 1) & 1), mc ^ (k & 1)
        peer = 4 * px + 2 * py + pc
        for a, (g_ref, land_ref) in enumerate(zip(g_refs, land_refs)):
            copies.append(pltpu.make_async_remote_copy(
                src_ref=g_ref.at[peer], dst_ref=land_ref.at[me], send_sem=send_sems.at[a * (N_DEV - 1) + k - 1],
                recv_sem=recv_sems.at[a * (N_DEV - 1) + k - 1], device_id=(px, py, pc), device_id_type=MESH))
    return copies


_HBM = pl.BlockSpec(memory_space=pltpu.HBM)
_SEM = pl.BlockSpec(memory_space=pltpu.SEMAPHORE)
_DATAFLOW = pltpu.SideEffectType.DATAFLOW_SIDE_EFFECTING


def _gather_copies(x_refs, land_refs, send_sems, recv_sems):
    mx, my, mc = lax.axis_index("x"), lax.axis_index("y"), lax.axis_index("c")
    me = 4 * mx + 2 * my + mc
    copies = []
    for k in range(1, N_DEV):
        px, py, pc = mx ^ (k >> 2), my ^ ((k >> 1) & 1), mc ^ (k & 1)
        for a, (x_ref, land_ref) in enumerate(zip(x_refs, land_refs)):
            copies.append(pltpu.make_async_remote_copy(
                src_ref=x_ref, dst_ref=land_ref.at[me], send_sem=send_sems.at[a * (N_DEV - 1) + k - 1],
                recv_sem=recv_sems.at[a * (N_DEV - 1) + k - 1], device_id=(px, py, pc), device_id_type=MESH))
    return copies


def _exchange_start(name, copies, srcs, lands, after):
    na = len(srcs)

    def body(*refs):
        for cp in copies(refs[:na], refs[na:2 * na], refs[2 * na + 1], refs[2 * na + 2]):
            cp.start()
        refs[-1][...] = jnp.zeros_like(refs[-1])

    in_hbm = lambda a: pltpu.with_memory_space_constraint(a, pltpu.HBM)
    outs = pl.pallas_call(
        body, name=name,
        out_shape=(pltpu.SemaphoreType.DMA((na * (N_DEV - 1),)), pltpu.SemaphoreType.DMA((na * (N_DEV - 1),)),
                   *[pltpu.HBM(a.shape, a.dtype) for a in list(srcs) + list(lands)],
                   jax.ShapeDtypeStruct((8, LANES), F32)),
        in_specs=[_HBM] * (2 * na) + [pl.BlockSpec(memory_space=pl.ANY)],
        out_specs=(_SEM, _SEM, *[_HBM] * (2 * na), pl.BlockSpec(memory_space=pltpu.VMEM)),
        input_output_aliases={i: 2 + i for i in range(2 * na)},
        compiler_params=pltpu.CompilerParams(has_side_effects=_DATAFLOW),
    )(*[in_hbm(a) for a in srcs], *[in_hbm(a) for a in lands], after)
    return outs[0], outs[1], outs[2:2 + na], outs[2 + na:2 + 2 * na], outs[-1][0, 0]


def _exchange_wait(name, copies, send_sems, recv_sems, srcs, lands, after):
    na = len(srcs)

    def body(*refs):
        for cp in copies(refs[:na], refs[na:2 * na], refs[2 * na], refs[2 * na + 1]):
            cp.wait_send()
            cp.wait_recv()

    outs = pl.pallas_call(
        body, name=name,
        out_shape=[pltpu.HBM(a.shape, a.dtype) for a in list(srcs) + list(lands)],
        in_specs=[_HBM] * (2 * na) + [_SEM, _SEM, pl.BlockSpec(memory_space=pl.ANY)], out_specs=[_HBM] * (2 * na),
        input_output_aliases={i: i for i in range(2 * na)},
        compiler_params=pltpu.CompilerParams(has_side_effects=_DATAFLOW),
    )(*srcs, *lands, send_sems, recv_sems, after)
    return outs[:na], outs[na:]


def _sum_slots(name, g, tr):
    _, rws, cols = g.shape

    def body(g_ref, o_ref):
        acc = g_ref[0].astype(F32)
        for j in range(1, N_DEV):
            acc = acc + g_ref[j].astype(F32)
        o_ref[...] = acc

    return pl.pallas_call(
        body, name=name, grid=(rws // tr,),
        in_specs=[pl.BlockSpec((N_DEV, tr, cols), lambda i: (0, i, 0))],
        out_specs=pl.BlockSpec((tr, cols), lambda i: (i, 0)),
        out_shape=jax.ShapeDtypeStruct((rws, cols), F32),
        compiler_params=_cparams(dimension_semantics=("parallel",)),
    )(g)


def _adam_math(wv, gv, mv, vv):
    mn = ADAM_B1 * mv + (1.0 - ADAM_B1) * gv
    vn = ADAM_B2 * vv + (1.0 - ADAM_B2) * jnp.square(gv)
    m_hat = mn / (1.0 - ADAM_B1 ** ADAM_STEP)
    v_hat = vn / (1.0 - ADAM_B2 ** ADAM_STEP)
    delta = -ADAM_LR * (m_hat / (jnp.sqrt(v_hat) + ADAM_EPS) + ADAM_WD * wv)
    return delta, mn, vn


def _adamw(name, w, g, m, v, tr):
    return _rowwise(name, _adam_math, [w, g, m, v], [], [(LANES, F32)] * 3, tr=tr)


def _adamw_slots(name, recv, own, rb, cb, cw, w, m, v, tr):
    nr, nc = w.shape

    def body(g_ref, own_ref, w_ref, m_ref, v_ref, go_ref, d_ref, mo_ref, vo_ref):
        me = 4 * lax.axis_index("x") + 2 * lax.axis_index("y") + lax.axis_index("c")
        acc = None
        for s in range(N_DEV):
            part = jnp.where(me == s, own_ref[s], g_ref[s]).astype(F32)
            acc = part if acc is None else acc + part
        g = acc[:, :nc]
        go_ref[...] = g
        d_ref[...], mo_ref[...], vo_ref[...] = _adam_math(w_ref[...], g, m_ref[...], v_ref[...])

    nat = pl.BlockSpec((tr, nc), lambda i: (i, 0))
    slots = pl.BlockSpec((N_DEV, tr, cw), lambda i: (0, rb + i, cb))
    return pl.pallas_call(
        body, name=name, grid=(nr // tr,),
        in_specs=[slots, slots, nat, nat, nat],
        out_specs=[nat] * 4, out_shape=[jax.ShapeDtypeStruct((nr, nc), F32)] * 4,
        compiler_params=_cparams(dimension_semantics=("parallel",)),
    )(recv, own, w, m, v)


def _local_blocks(w):
    pad_cols = lambda a: jnp.pad(a, ((0, 0), (0, FF_PAD - FF_SHARD)))
    pad_rows = lambda a: jnp.pad(a, ((0, FF_PAD - FF_SHARD), (0, 0)))
    gate_up = lambda tag: jnp.concatenate([pad_cols(w[tag + "_w_gate"]), pad_cols(w[tag + "_w_up"])], axis=1)
    blocks = {
        "ffn1_gu": gate_up("ffn1"), "ffn1_d": pad_rows(w["ffn1_w_down"]), "w_in": w["w_in"],
        "lora": jnp.concatenate([w["rwkv_w2"], w["rwkv_a2"], w["rwkv_g2"]], axis=0),
        "br": jnp.concatenate([w["w_br_rwkv"], w["w_br_attn"], w["ple_w_proj"]], axis=0),
        "w_out": w["w_out"], "ffn2_gu": gate_up("ffn2"), "ffn2_d": pad_rows(w["ffn2_w_down"]),
        "ple_gate": w["ple_w_gate"],
    }
    return {n: a.astype(BF16) for n, a in blocks.items()}


GATHER_HEAD = ("ffn1_gu", "ffn1_d", "w_in", "lora")
GATHER_REST = ("br", "w_out", "ffn2_gu", "ffn2_d", "ple_gate")

SCATTER_GROUPS = {"tail": ("ple_gate", "ple_proj", "ffn2_gu", "ffn2_d"), "mixer": ("w_out", "br", "lora", "w_in"),
                  "head": ("ffn1_gu", "ffn1_d")}

ADAM_PLAN = (
    ("ffn1_w_gate", "ffn1_gu", 0, 0, FF_PAD, 256), ("ffn1_w_up", "ffn1_gu", 0, 1, FF_PAD, 256),
    ("ffn1_w_down", "ffn1_d", 0, 0, D_MODEL, FF_SHARD // 2), ("w_in", "w_in", 0, 0, IN_SHARD, 256),
    ("rwkv_w2", "lora", 0, 0, HEAD_DIM, 64), ("rwkv_a2", "lora", 1, 0, HEAD_DIM, 64),
    ("rwkv_g2", "lora", 2, 0, HEAD_DIM, 64),
    ("w_br_rwkv", "br", 0, 0, OUT_SHARD, 256), ("w_br_attn", "br", 2, 0, OUT_SHARD, 256),
    ("ple_w_proj", "ple_proj", 0, 0, OUT_SHARD, 256), ("w_out", "w_out", 0, 0, D_MODEL, OUT_SHARD),
    ("ffn2_w_gate", "ffn2_gu", 0, 0, FF_PAD, 256), ("ffn2_w_up", "ffn2_gu", 0, 1, FF_PAD, 256),
    ("ffn2_w_down", "ffn2_d", 0, 0, D_MODEL, FF_SHARD // 2), ("ple_w_gate", "ple_gate", 0, 0, D_MODEL, OUT_SHARD),
)


def _pack_small(arrs, rows):
    flat = jnp.concatenate([a.reshape(-1) for a in arrs])
    return jnp.pad(flat, (0, rows * LANES - flat.shape[0])).reshape(rows, LANES)


def _unpack_small(flat, like):
    flat = flat.reshape(-1)
    out, off = [], 0
    for a in like:
        out.append(flat[off:off + a.size].reshape(a.shape))
        off += a.size
    return out


def _local_step(x, p, pos, target, sm, wg, fetch_rest, on_grads):
    t = x.shape[0]
    hv = t * RWKV_HEADS

    w_d1 = wg["ffn1_d"].reshape(FF_HID, D_MODEL)
    full_cols = lambda blk: blk.transpose(1, 0, 2).reshape(blk.shape[1], N_DEV * blk.shape[2])
    lora_w2 = full_cols(wg["lora"][:, :DECAY_LORA])
    lora_a2 = full_cols(wg["lora"][:, DECAY_LORA:DECAY_LORA + ICLR_LORA])
    lora_g2 = full_cols(wg["lora"][:, DECAY_LORA + ICLR_LORA:])
    x1, ffn1_saved = _ffn_fwd("ffn1", x, sm["ffn1_norm"], wg["ffn1_gu"], w_d1)
    h2 = _norm_fwd("mix_norm", x1, sm["mix_norm"])
    z = _mmc_nn("w_in", h2, wg["w_in"], 0, 0, IN_SHARD)
    z_r = z[:, :RWKV_COLS]
    z_g = z[:, RWKV_COLS + 3 * ATTN_DIM:]

    r, k, v, lo, gd = _shift_fwd(z_r, sm["rwkv_mu"])
    zero_lo = jnp.zeros((DECAY_LORA, RWKV_DIM), BF16)
    w2p = jnp.concatenate([lora_w2, zero_lo], axis=0).astype(F32)
    a2p = jnp.concatenate([zero_lo, lora_a2], axis=0).astype(F32)
    pre_params = [sm["rwkv_w0"], w2p, sm["rwkv_a0"], a2p, lora_g2.astype(F32), sm["rwkv_k_k"], sm["rwkv_k_a"]]
    wide = [(RWKV_DIM, F32)]
    k2, kk, a, decay, g = _rowwise("rwkv_pre", _f_pre, [k, lo, gd], pre_params, wide * 5)
    as_heads = lambda u: u.reshape(hv, HEAD_DIM)
    kkn, b = _rowwise("rwkv_kk", _f_kk, [as_heads(kk), as_heads(a)], [], [(HEAD_DIM, F32)] * 2, tr=512)
    scan_in = [u.reshape(t, RWKV_HEADS, HEAD_DIM) for u in (r, decay, k2, kkn, b)]
    v_rows = _to_v_rows(v)
    y_rows, states = _wkv_fwd(*scan_in, v_rows)
    y = _from_v_rows(y_rows)
    post_params = [sm["rwkv_gn_w"].reshape(RWKV_HEADS, HEAD_DIM), sm["rwkv_gn_b"].reshape(RWKV_HEADS, HEAD_DIM),
                   sm["rwkv_r_k"].reshape(RWKV_HEADS, HEAD_DIM)]
    post_rows = [as_heads(y), as_heads(r), as_heads(k2), as_heads(v), as_heads(g)]
    y_rwkv = _rowwise("rwkv_post", lambda *av: (_f_post(*av),), post_rows, post_params, [(HEAD_DIM, F32)], tr=512)[0]
    y_rwkv = y_rwkv.reshape(t, RWKV_DIM)

    inv_freq = 1.0 / (ROPE_THETA ** (jnp.arange(0, HEAD_DIM, 2, dtype=F32) / HEAD_DIM))
    freq2 = jnp.tile(inv_freq, 2 * PAIR // HEAD_DIM).reshape(1, PAIR)
    half = jnp.ones((HEAD_DIM // 2,), F32)
    sign2 = jnp.tile(jnp.concatenate([-half, half]), PAIR // HEAD_DIM).reshape(1, PAIR)

    def rope_table(posv, fr, sg):
        ang = posv * fr
        return (jnp.concatenate([jnp.cos(ang), jnp.sin(ang) * sg], axis=1),)
    tab = _rowwise("rope_table", rope_table, [pos.astype(F32).reshape(t, 1)], [freq2, sign2], [(2 * PAIR, F32)])[0]
    qk = _qk_prep(z, tab, sm["q_norm"], sm["k_norm"])
    o_all, lse_all = _attn_fwd(qk, z)
    gw = HEADS_PER_GROUP * HEAD_DIM

    def by_group(ov, lv):
        return [ov[:, i * gw:(i + 1) * gw] for i in range(3)] + [lv[:, i * gw:(i + 1) * gw] for i in range(3)]
    y_attn = _rowwise("attn_comb", lambda ov, lv: (_f_comb(*by_group(ov, lv)),), [o_all, lse_all], [], [(gw, F32)])[0]

    wg = {**wg, **fetch_rest(y_rwkv)}
    w_d2 = wg["ffn2_d"].reshape(FF_HID, D_MODEL)
    w_out = wg["w_out"].reshape(D_MODEL, D_MODEL)
    w_pg = wg["ple_gate"].reshape(D_MODEL, D_MODEL)
    u_r =_mmc_nn("br_rwkv", y_rwkv, wg["br"], 0, 0, OUT_SHARD)
    u_a = _mmc_nn("br_attn", y_attn, wg["br"], 2, 0, OUT_SHARD)

    def f_merge(zgr, zga, ur, ua):
        return _sigmoid(zgr) * ur + _sigmoid(zga) * ua
    merged = _rowwise("merge", lambda zg, ur, ua: (f_merge(zg[:, :D_MODEL], zg[:, D_MODEL:], ur, ua),),
                      [z_g, u_r, u_a], [], [(D_MODEL, BF16)])[0]
    x2 = _mm("w_out", merged, w_out, "nn", res=x1)
    x3, ffn2_saved = _ffn_fwd("ffn2", x2, sm["ffn2_norm"], wg["ffn2_gu"], w_d2)

    hn = _norm_fwd("ple_norm", x3, sm["ple_norm"])
    gz = _mm("ple_gate", hn, w_pg, "nn")
    pp = _mmc_nn("ple_proj", p, wg["br"], 3, 0, OUT_SHARD)

    def f_head(x3v, gzv, ppv, tg):
        sg = _sigmoid(gzv)
        err = x3v + sg * ppv - tg
        part = 0.5 * jnp.sum(jnp.mean(err * err, axis=-1, keepdims=True))
        dx4 = err * (1.0 / D_MODEL)
        return dx4, dx4 * ppv * sg * (1.0 - sg), dx4 * sg, jnp.full((1, LANES), part, F32)
    dx4, dgz, dpp, loss_row = _rowwise("ple_loss", f_head, [x3, gz, pp, target], [],
                                       [(D_MODEL, F32), (D_MODEL, BF16), (D_MODEL, BF16)], [(1, LANES)])
    loss = loss_row[0, 0]

    gs, gm = {}, {}
    row_blocks = lambda g: g.reshape(N_DEV, g.shape[0] // N_DEV, g.shape[1])
    dhn = _mm("ple_dhn", dgz, w_pg, "nt")
    gm["ple_gate"] = row_blocks(_mm("ple_dwgate", hn, dgz, "tn", out_dtype=BF16))
    gm["ple_proj"] = _mmc_tn("ple_dwproj", p, dpp, OUT_SHARD)
    dx3, gs["ple_norm"] = _norm_bwd("ple_dnorm", x3, sm["ple_norm"], dhn, dx4)

    dx2, gs["ffn2_norm"], gm["ffn2_gu"], gm["ffn2_d"] = _ffn_bwd(
        "ffn2", x2, sm["ffn2_norm"], wg["ffn2_gu"], w_d2, ffn2_saved, dx3)
    tail_token = on_grads("tail", {n: gm.pop(n) for n in SCATTER_GROUPS["tail"]})

    dmerged = _mm("w_out_dmerged", dx2, w_out, "nt")
    gm["w_out"] = row_blocks(_mm("w_out_dw", merged, dx2, "tn", out_dtype=BF16))

    def merge_bwd(zg, ur, ua, dm):
        _, vjp = jax.vjp(f_merge, zg[:, :D_MODEL], zg[:, D_MODEL:], ur, ua)
        dzr, dza, dur, dua = vjp(dm)
        return jnp.concatenate([dzr, dza], axis=1), dur, dua
    dz_g, du_r, du_a = _rowwise("merge_bwd", merge_bwd, [z_g, u_r, u_a, dmerged], [],
                                [(2 * D_MODEL, BF16), (D_MODEL, BF16), (D_MODEL, BF16)])
    dy_rwkv = _mmc_nt("br_rwkv_dy", du_r, wg["br"], 0, 0, OUT_SHARD, RWKV_DIM)
    dy_attn = _mmc_nt("br_attn_dy", du_a, wg["br"], 2, 0, OUT_SHARD, HEADS_PER_GROUP * HEAD_DIM)
    gm["br"] = jnp.concatenate([_mmc_tn("br_rwkv_dw", y_rwkv, du_r, OUT_SHARD),
                                _mmc_tn("br_attn_dw", y_attn, du_a, OUT_SHARD)], axis=1)

    def comb_bwd(ov, lv, dyv):
        _, vjp = jax.vjp(_f_comb, *by_group(ov, lv))
        d = vjp(dyv)
        return jnp.concatenate(d[:3], axis=1), jnp.concatenate(d[3:], axis=1)
    do_all, dl_all = _rowwise("attn_comb_bwd", comb_bwd, [o_all, lse_all, dy_attn], [],
                              [(ATTN_DIM, F32), (ATTN_DIM, F32)])
    dq_all, dk_all, dv_all = _attn_bwd(qk, z, do_all, dl_all)
    dqk_raw, gs["q_norm"], gs["k_norm"] = _qk_prep_bwd(z, tab, sm["q_norm"], sm["k_norm"], dq_all, dk_all)

    def post_bwd(yv, rv, k2v, vv, gv, dv_, gnw, gnb, rk):
        _, vjp = jax.vjp(_f_post, yv, rv, k2v, vv, gv, gnw, gnb, rk)
        return vjp(dv_)
    head_acc = (RWKV_HEADS, HEAD_DIM)
    dy, dr1, dk2a, dv1, dg, d_gnw, d_gnb, d_rk = _rowwise(
        "rwkv_post_bwd", post_bwd, post_rows + [as_heads(dy_rwkv)], [post_params[0] + tail_token] + post_params[1:],
        [(HEAD_DIM, F32)] * 5, [head_acc] * 3, tr=512)
    gs["rwkv_gn_w"], gs["rwkv_gn_b"], gs["rwkv_r_k"] = d_gnw, d_gnb, d_rk
    dr2, ddecay, dk2b, dkkn, db, dv_rows = _wkv_bwd(*scan_in, v_rows, states, _to_v_rows(dy.reshape(t, RWKV_DIM)))
    dr2, ddecay, dk2b, dkkn, db = [u.reshape(t, RWKV_DIM) for u in (dr2, ddecay, dk2b, dkkn, db)]
    dv2 = _from_v_rows(dv_rows)

    def kk_bwd(kkv, av, dkknv, dbv):
        _, vjp = jax.vjp(_f_kk, kkv, av)
        return vjp((dkknv, dbv))
    dkk, da = _rowwise("rwkv_kk_bwd", kk_bwd, [as_heads(kk), as_heads(a), as_heads(dkkn), as_heads(db)], [],
                       [(HEAD_DIM, F32)] * 2, tr=512)

    def pre_bwd(kv, lov, gdv, dk2x, dk2y, dkkv, dav, ddec, dgv, w0, w2p_, a0, a2p_, g2, k_k, k_a):
        _, vjp = jax.vjp(_f_pre, kv, lov, gdv, w0, w2p_, a0, a2p_, g2, k_k, k_a)
        return vjp((dk2x + dk2y, dkkv, dav, ddec, dgv))
    lora_acc = (DECAY_LORA + ICLR_LORA, RWKV_DIM)
    dk, dlo, dgd, d_w0, d_w2p, d_a0, d_a2p, d_g2, d_kk, d_ka = _rowwise(
        "rwkv_pre_bwd", pre_bwd,
        [k, lo, gd, dk2a.reshape(t, RWKV_DIM), dk2b, dkk.reshape(t, RWKV_DIM), da.reshape(t, RWKV_DIM), ddecay,
         dg.reshape(t, RWKV_DIM)],
        pre_params, [(RWKV_DIM, F32), (LANES, F32), (LANES, F32)],
        [(1, RWKV_DIM), lora_acc, (1, RWKV_DIM), lora_acc, (GATE_LORA, RWKV_DIM), (1, RWKV_DIM), (1, RWKV_DIM)])
    gs["rwkv_w0"], gs["rwkv_a0"], gs["rwkv_k_k"], gs["rwkv_k_a"] = d_w0, d_a0, d_kk, d_ka
    col_blocks = lambda g: g.reshape(g.shape[0], N_DEV, g.shape[1] // N_DEV).transpose(1, 0, 2)
    gm["lora"] = jnp.concatenate([col_blocks(d_w2p[:DECAY_LORA]), col_blocks(d_a2p[DECAY_LORA:]), col_blocks(d_g2)],
                                 axis=1).astype(BF16)
    add2 = lambda u, w_: _rowwise("rwkv_add", lambda p_, q_: (p_ + q_,), [u, w_], [], [(RWKV_DIM, F32)])[0]
    dz_r, gs["rwkv_mu"] = _shift_bwd(z_r, sm["rwkv_mu"], add2(dr1.reshape(t, RWKV_DIM), dr2), dk,
                                     add2(dv1.reshape(t, RWKV_DIM), dv2), dlo, dgd)

    dz = jnp.concatenate([dz_r, dqk_raw, dv_all.astype(BF16), dz_g], axis=1)
    dh2 = _mmc_nt("w_in_dh", dz, wg["w_in"], 0, 0, IN_SHARD, D_MODEL)
    gm["w_in"] = _mmc_tn("w_in_dw", h2, dz, IN_SHARD)
    mixer_token = on_grads("mixer", {n: gm.pop(n) for n in SCATTER_GROUPS["mixer"]})
    dx1, gs["mix_norm"] = _norm_bwd("mix_dnorm", x1, sm["mix_norm"] + mixer_token, dh2, dx2)

    dx0, gs["ffn1_norm"], gm["ffn1_gu"], gm["ffn1_d"] = _ffn_bwd(
        "ffn1", x, sm["ffn1_norm"], wg["ffn1_gu"], w_d1, ffn1_saved, dx1)
    return loss, dx0, gm, gs


def kernel(x, p, positions, ffn1_norm, ffn1_w_gate, ffn1_w_up, ffn1_w_down, mix_norm, w_in, rwkv_mu, rwkv_w0, rwkv_w2, rwkv_a0, rwkv_a2, rwkv_g2, rwkv_k_k, rwkv_k_a, rwkv_r_k, rwkv_gn_w, rwkv_gn_b, q_norm, k_norm, w_br_rwkv, w_br_attn, w_out, ffn2_norm, ffn2_w_gate, ffn2_w_up, ffn2_w_down, ple_norm, ple_w_gate, ple_w_proj, loss_target, m_ffn1_norm, m_ffn1_w_gate, m_ffn1_w_up, m_ffn1_w_down, m_mix_norm, m_w_in, m_rwkv_mu, m_rwkv_w0, m_rwkv_w2, m_rwkv_a0, m_rwkv_a2, m_rwkv_g2, m_rwkv_k_k, m_rwkv_k_a, m_rwkv_r_k, m_rwkv_gn_w, m_rwkv_gn_b, m_q_norm, m_k_norm, m_w_br_rwkv, m_w_br_attn, m_w_out, m_ffn2_norm, m_ffn2_w_gate, m_ffn2_w_up, m_ffn2_w_down, m_ple_norm, m_ple_w_gate, m_ple_w_proj, v_ffn1_norm, v_ffn1_w_gate, v_ffn1_w_up, v_ffn1_w_down, v_mix_norm, v_w_in, v_rwkv_mu, v_rwkv_w0, v_rwkv_w2, v_rwkv_a0, v_rwkv_a2, v_rwkv_g2, v_rwkv_k_k, v_rwkv_k_a, v_rwkv_r_k, v_rwkv_gn_w, v_rwkv_gn_b, v_q_norm, v_k_norm, v_w_br_rwkv, v_w_br_attn, v_w_out, v_ffn2_norm, v_ffn2_w_gate, v_ffn2_w_up, v_ffn2_w_down, v_ple_norm, v_ple_w_gate, v_ple_w_proj):
    args = locals()
    w = {n: args[n][0] for n in WEIGHTS}
    m = {n: args["m_" + n][0] for n in WEIGHTS}
    v = {n: args["v_" + n][0] for n in WEIGHTS}

    w_loc = _local_blocks(w)
    wg = dict(zip(GATHER_HEAD, _all_gather_hbm("gather_head", [w_loc[n] for n in GATHER_HEAD])))
    me = 4 * lax.axis_index("x") + 2 * lax.axis_index("y") + lax.axis_index("c")
    rest = [w_loc[n] for n in GATHER_REST]
    zones = [lax.dynamic_update_slice(lax.empty((N_DEV,) + a.shape, a.dtype), a[None], (me, 0, 0)) for a in rest]
    rest_flight = _exchange_start("gather_start_rest", _gather_copies, rest, zones, wg[GATHER_HEAD[0]])

    def fetch_rest(after):
        _, got = _exchange_wait("gather_wait_rest", _gather_copies, *rest_flight[:4], after)
        return dict(zip(GATHER_REST, got))

    sm = {n: w[n].reshape(1, -1) for n in SMALL}
    in_flight = {}

    def scatter_early(group, arrays):
        arrs = [arrays[n] for n in SCATTER_GROUPS[group]]
        *in_flight[group], token = _exchange_start("scatter_start_" + group, _scatter_copies, arrs,
                                                   [lax.empty(a.shape, a.dtype) for a in arrs], arrs[0])
        return token
    loss_part, dx, gm, gs = _local_step(x[0], p[0, 0], positions[0], loss_target[0], sm, wg, fetch_rest,
                                        scatter_early)
    loss = lax.psum(loss_part, ("x", "y", "c"))
    recv, own = {}, {}
    for group, handles in in_flight.items():
        sent, lands = _exchange_wait("scatter_wait_" + group, _scatter_copies, *handles, gm[SCATTER_GROUPS["head"][0]])
        own.update(zip(SCATTER_GROUPS[group], sent))
        recv.update(zip(SCATTER_GROUPS[group], lands))
    head = _all_to_all_hbm("scatter_head", [gm[n] for n in SCATTER_GROUPS["head"]])
    recv.update(zip(SCATTER_GROUPS["head"], head))
    own.update(zip(SCATTER_GROUPS["head"], head))
    small_like = [w[n] for n in SMALL]
    small_rows = 80
    gs_all = _all_gather_vmem(_pack_small([gs[n] for n in SMALL], small_rows))
    gs_sum = _sum_slots("sum_small_grads", gs_all, small_rows)

    res = {}
    for n, src, rb, cb, cw, tr in ADAM_PLAN:
        outs4 = _adamw_slots("adamw_" + n, recv[src], own[src], rb, cb, cw, w[n], m[n], v[n], tr)
        for tag, a in zip(("grad", "delta", "new_m", "new_v"), outs4):
            res[tag, n] = a[None]
    d_s, m_s, v_s = _adamw("adamw_small", _pack_small(small_like, small_rows), gs_sum,
                           _pack_small([m[n] for n in SMALL], small_rows),
                           _pack_small([v[n] for n in SMALL], small_rows), small_rows)
    for tag, small in (("grad", gs_sum), ("delta", d_s), ("new_m", m_s), ("new_v", v_s)):
        for n, a in zip(SMALL, _unpack_small(small, small_like)):
            res[tag, n] = a[None]
    outs = [loss, dx[None]]
    for tag in ("grad", "delta", "new_m", "new_v"):
        outs += [res[tag, n] for n in WEIGHTS]
    return tuple(outs)
```

```python
import functools

import jax
import jax.numpy as jnp
from jax import lax
from jax.experimental import pallas as pl
from jax.experimental.pallas import tpu as pltpu

F32, BF16 = jnp.float32, jnp.bfloat16
MESH = pl.DeviceIdType.MESH
N_DEV = 8
LANES = 128
VMEM_LIMIT = 56 * 1024 * 1024

D_MODEL = 1024
PLE_DIM = 256
HEAD_DIM = 64
RWKV_HEADS = 8
RWKV_DIM = RWKV_HEADS * HEAD_DIM
DECAY_LORA = 64
ICLR_LORA = 64
GATE_LORA = 128
GN_EPS = 64e-5
ATTN_GROUPS = ((128, 1), (512, 4), (2048, 16))
HEADS_PER_GROUP = 4
ATTN_HEADS = HEADS_PER_GROUP * len(ATTN_GROUPS)
ATTN_DIM = ATTN_HEADS * HEAD_DIM
BAND_BLOCK = 128
ROPE_THETA = 10000.0
NEG_INF = -1e30
D_FF = 2816
RMS_EPS = 1e-6
RWKV_COLS = 3 * RWKV_DIM + DECAY_LORA + ICLR_LORA + GATE_LORA
ADAM_LR = 0.001
ADAM_B1 = 0.9
ADAM_B2 = 0.999
ADAM_EPS = 1e-08
ADAM_WD = 0.01
ADAM_STEP = 10

V_LO = LANES // RWKV_HEADS
V_HI = HEAD_DIM // V_LO
SCAN_CHUNK = 32

FF_SHARD = D_FF // N_DEV
FF_PAD = -(-FF_SHARD // LANES) * LANES
FF_HID = N_DEV * FF_PAD
IN_SHARD = 6144 // N_DEV
OUT_SHARD = D_MODEL // N_DEV

SMALL = ("ffn1_norm", "mix_norm", "rwkv_mu", "rwkv_w0", "rwkv_a0", "rwkv_k_k", "rwkv_k_a", "rwkv_r_k",
         "rwkv_gn_w", "rwkv_gn_b", "q_norm", "k_norm", "ffn2_norm", "ple_norm")
WEIGHTS = ("ffn1_norm", "ffn1_w_gate", "ffn1_w_up", "ffn1_w_down", "mix_norm", "w_in", "rwkv_mu", "rwkv_w0",
           "rwkv_w2", "rwkv_a0", "rwkv_a2", "rwkv_g2", "rwkv_k_k", "rwkv_k_a", "rwkv_r_k", "rwkv_gn_w",
           "rwkv_gn_b", "q_norm", "k_norm", "w_br_rwkv", "w_br_attn", "w_out", "ffn2_norm", "ffn2_w_gate",
           "ffn2_w_up", "ffn2_w_down", "ple_norm", "ple_w_gate", "ple_w_proj")


def _cparams(**kw):
    return pltpu.CompilerParams(vmem_limit_bytes=VMEM_LIMIT, **kw)


def _tile(n, cap):
    best = None
    for t in range(LANES, min(n, cap) + 1, LANES):
        if n % t == 0:
            best = t
    return best if best is not None else n


@jax.custom_vjp
def _bdot(a, w):
    return jnp.dot(a.astype(BF16), w.astype(BF16), preferred_element_type=F32)


def _bdot_fwd(a, w):
    return _bdot(a, w), (a, w)


def _bdot_bwd(res, g):
    a, w = res
    gb = g.astype(BF16)
    da = lax.dot_general(gb, w.astype(BF16), (((1,), (1,)), ((), ())), preferred_element_type=F32)
    dw = lax.dot_general(a.astype(BF16), gb, (((0,), (0,)), ((), ())), preferred_element_type=F32)
    return da.astype(a.dtype), dw.astype(w.dtype)


_bdot.defvjp(_bdot_fwd, _bdot_bwd)


@jax.custom_vjp
def _bdot_nt(a, b):
    return lax.dot_general(a.astype(BF16), b.astype(BF16), (((1,), (1,)), ((), ())), preferred_element_type=F32)


def _bdot_nt_fwd(a, b):
    return _bdot_nt(a, b), (a, b)


def _bdot_nt_bwd(res, g):
    a, b = res
    gb = g.astype(BF16)
    da = jnp.dot(gb, b.astype(BF16), preferred_element_type=F32)
    db = lax.dot_general(gb, a.astype(BF16), (((0,), (0,)), ((), ())), preferred_element_type=F32)
    return da.astype(a.dtype), db.astype(b.dtype)


_bdot_nt.defvjp(_bdot_nt_fwd, _bdot_nt_bwd)


def _mm(name, a, b, mode, out_dtype=F32, res=None, scale=None):
    if mode == "nn":
        (m, k), n = a.shape, b.shape[1]
    elif mode == "nt":
        (m, k), n = a.shape, b.shape[0]
    else:
        (k, m), n = a.shape, b.shape[1]
    tm, tn = _tile(m, 512), _tile(n, 512)
    a_spec = pl.BlockSpec((k, tm), lambda i, j: (0, i)) if mode == "tn" else pl.BlockSpec((tm, k), lambda i, j: (i, 0))
    b_spec = pl.BlockSpec((tn, k), lambda i, j: (j, 0)) if mode == "nt" else pl.BlockSpec((k, tn), lambda i, j: (0, j))
    dims = {"nn": ((1,), (0,)), "nt": ((1,), (1,)), "tn": ((0,), (0,))}[mode]
    o_spec = pl.BlockSpec((tm, tn), lambda i, j: (i, j))
    ins, in_specs = [a, b], [a_spec, b_spec]
    if res is not None:
        ins.append(res)
        in_specs.append(o_spec)

    def body(*refs):
        acc = lax.dot_general(refs[0][...].astype(BF16), refs[1][...].astype(BF16), (dims, ((), ())),
                              preferred_element_type=F32)
        if scale is not None:
            acc = acc * scale
        if res is not None:
            acc = acc + refs[2][...].astype(F32)
        refs[-1][...] = acc.astype(refs[-1].dtype)

    return pl.pallas_call(
        body, name=name, grid=(m // tm, n // tn), in_specs=in_specs, out_specs=o_spec,
        out_shape=jax.ShapeDtypeStruct((m, n), out_dtype),
        compiler_params=_cparams(dimension_semantics=("parallel", "parallel")),
    )(*ins)


def _mmc_nn(name, a, wb, ki, ci, n, out_dtype=F32):
    m, k = a.shape
    tm = _tile(m, 512)

    def body(a_ref, w_ref, o_ref):
        o_ref[...] = jnp.dot(a_ref[...].astype(BF16), w_ref[...], preferred_element_type=F32).astype(o_ref.dtype)

    return pl.pallas_call(
        body, name=name, grid=(m // tm, N_DEV),
        in_specs=[pl.BlockSpec((tm, k), lambda i, j: (i, 0)), pl.BlockSpec((None, k, n), lambda i, j: (j, ki, ci))],
        out_specs=pl.BlockSpec((tm, n), lambda i, j: (i, j)),
        out_shape=jax.ShapeDtypeStruct((m, N_DEV * n), out_dtype),
        compiler_params=_cparams(dimension_semantics=("parallel", "parallel")),
    )(a, wb)


def _mmc_nt(name, a, wb, ki, ci, n, k, res=None):
    m = a.shape[0]
    tm = _tile(m, 512)
    o_spec = pl.BlockSpec((tm, k), lambda i, j: (i, 0))
    ins = [a, wb] + ([res] if res is not None else [])
    in_specs = [pl.BlockSpec((tm, n), lambda i, j: (i, j)), pl.BlockSpec((None, k, n), lambda i, j: (j, ki, ci))]
    in_specs += [o_spec] if res is not None else []

    def body(*refs):
        a_ref, w_ref, o_ref = refs[0], refs[1], refs[-1]
        acc = lax.dot_general(a_ref[...].astype(BF16), w_ref[...], (((1,), (1,)), ((), ())),
                              preferred_element_type=F32)

        @pl.when(pl.program_id(1) == 0)
        def _():
            o_ref[...] = acc + refs[2][...] if res is not None else acc

        @pl.when(pl.program_id(1) != 0)
        def _():
            o_ref[...] += acc

    return pl.pallas_call(
        body, name=name, grid=(m // tm, N_DEV), in_specs=in_specs, out_specs=o_spec,
        out_shape=jax.ShapeDtypeStruct((m, k), F32),
        compiler_params=_cparams(dimension_semantics=("parallel", "arbitrary")),
    )(*ins)


def _mmc_tn(name, x, dy, n):
    m, k = x.shape
    tk = _tile(k, 512)

    def body(x_ref, dy_ref, o_ref):
        o_ref[...] = lax.dot_general(x_ref[...].astype(BF16), dy_ref[...].astype(BF16), (((0,), (0,)), ((), ())),
                                     preferred_element_type=F32).astype(o_ref.dtype)

    return pl.pallas_call(
        body, name=name, grid=(N_DEV, k // tk),
        in_specs=[pl.BlockSpec((m, tk), lambda j, i: (0, i)), pl.BlockSpec((m, n), lambda j, i: (0, j))],
        out_specs=pl.BlockSpec((None, tk, n), lambda j, i: (j, i, 0)),
        out_shape=jax.ShapeDtypeStruct((N_DEV, k, n), BF16),
        compiler_params=_cparams(dimension_semantics=("parallel", "parallel")),
    )(x, dy)


def _rowwise(name, fn, rows, params, out_rows, out_accs=(), tr=256):
    r = rows[0].shape[0]
    in_specs = [pl.BlockSpec((tr, a.shape[1]), lambda i: (i, 0)) for a in rows]
    in_specs += [pl.BlockSpec(p.shape, lambda i, nd=p.ndim: (0,) * nd) for p in params]
    out_shape = [jax.ShapeDtypeStruct((r, c), dt) for c, dt in out_rows]
    out_shape += [jax.ShapeDtypeStruct(s, F32) for s in out_accs]
    out_specs = [pl.BlockSpec((tr, c), lambda i: (i, 0)) for c, _ in out_rows]
    out_specs += [pl.BlockSpec(s, lambda i, nd=len(s): (0,) * nd) for s in out_accs]
    n_in, n_ro = len(rows) + len(params), len(out_rows)

    def body(*refs):
        res = fn(*[ref[...] for ref in refs[:n_in]])
        outs = refs[n_in:]
        for o, v in zip(outs[:n_ro], res[:n_ro]):
            o[...] = v.astype(o.dtype)
        for o, v in zip(outs[n_ro:], res[n_ro:]):
            _accumulate(o, v)

    return pl.pallas_call(
        body, name=name, grid=(r // tr,), in_specs=in_specs, out_specs=out_specs, out_shape=out_shape,
        compiler_params=_cparams(dimension_semantics=("arbitrary",)),
    )(*rows, *params)


def _accumulate(o_ref, v):
    @pl.when(pl.program_id(0) == 0)
    def _():
        o_ref[...] = v

    @pl.when(pl.program_id(0) != 0)
    def _():
        o_ref[...] += v


def _rms(x, g):
    return x * lax.rsqrt(jnp.mean(x * x, axis=-1, keepdims=True) + RMS_EPS) * g


def _sigmoid(x):
    return jax.nn.sigmoid(x)


def _softplus(x):
    return jnp.maximum(x, 0.0) + jnp.log1p(jnp.exp(-jnp.abs(x)))


def _norm_fwd(name, x, g):
    return _rowwise(name, lambda xv, gv: (_rms(xv, gv),), [x], [g], [(x.shape[1], BF16)])[0]


def _norm_bwd(name, x, g, dh, dres):
    def fn(xv, dhv, drv, gv):
        _, vjp = jax.vjp(_rms, xv, gv)
        dx, dg = vjp(dhv)
        return dx + drv, dg
    return _rowwise(name, fn, [x, dh, dres], [g], [(x.shape[1], F32)], [g.shape])


def _f_act(gate, up):
    return gate * _sigmoid(gate) * up


def _gate_up(guv, j):
    base = j * 2 * FF_PAD
    return guv[:, base:base + FF_PAD], guv[:, base + FF_PAD:base + 2 * FF_PAD]


def _ffn_fwd(tag, x, norm, w_gu, w_down):
    h = _norm_fwd(tag + "_norm", x, norm)
    gu = _mmc_nn(tag + "_gu", h, w_gu, 0, 0, 2 * FF_PAD)

    def act(guv):
        return (jnp.concatenate([_f_act(*_gate_up(guv, j)) for j in range(N_DEV)], axis=1),)
    a = _rowwise(tag + "_act", act, [gu], [], [(FF_HID, BF16)])[0]
    out = _mm(tag + "_down", a, w_down, "nn", res=x, scale=0.5)
    return out, (h, gu, a)


def _ffn_bwd(tag, x, norm, w_gu, w_down, saved, dout, on_down=None):
    h, gu, a = saved
    d_wdown = _mm(tag + "_dwdown", a, dout, "tn", out_dtype=BF16, scale=0.5).reshape(N_DEV, FF_PAD, D_MODEL)
    token = jnp.zeros((1, LANES), F32) + (on_down(d_wdown) if on_down is not None else 0.0)
    da = _mm(tag + "_dact", dout, w_down, "nt", scale=0.5)

    def act_bwd(guv, dav, _):
        outs = []
        for j in range(N_DEV):
            _, vjp = jax.vjp(_f_act, *_gate_up(guv, j))
            outs += list(vjp(dav[:, j * FF_PAD:(j + 1) * FF_PAD]))
        return (jnp.concatenate(outs, axis=1),)
    dgu = _rowwise(tag + "_dgu", act_bwd, [gu, da], [token], [(2 * FF_HID, BF16)])[0]
    dh = _mmc_nt(tag + "_dh", dgu, w_gu, 0, 0, 2 * FF_PAD, D_MODEL)
    d_wgu = _mmc_tn(tag + "_dwgu", h, dgu, 2 * FF_PAD)
    dx, dnorm = _norm_bwd(tag + "_dnorm", x, norm, dh, dout)
    return dx, dnorm, d_wgu, d_wdown


def _shift_fwd(z, mu):
    t, c = z.shape
    tr = 256

    def body(z_ref, zp_ref, mu_ref, r_ref, k_ref, v_ref, lo_ref, gd_ref):
        zv = z_ref[...]
        prev = zp_ref[7:8, :] * jnp.where(pl.program_id(0) == 0, 0.0, 1.0)
        row = lax.broadcasted_iota(jnp.int32, zv.shape, 0)
        zsh = jnp.where(row == 0, prev, pltpu.roll(zv, 1, 0))
        zs = zv + (zsh - zv) * mu_ref[...]
        r_ref[...] = zs[:, 0:512]
        k_ref[...] = zs[:, 512:1024]
        v_ref[...] = zs[:, 1024:1536]
        lo_ref[...] = zs[:, 1536:1664]
        gd_ref[...] = zs[:, 1664:1792]

    widths = (512, 512, 512, 128, 128)
    return pl.pallas_call(
        body, name="rwkv_shift", grid=(t // tr,),
        in_specs=[pl.BlockSpec((tr, c), lambda i: (i, 0)),
                  pl.BlockSpec((8, c), lambda i: (jnp.maximum(i * (tr // 8) - 1, 0), 0)),
                  pl.BlockSpec((1, c), lambda i: (0, 0))],
        out_specs=[pl.BlockSpec((tr, w), lambda i: (i, 0)) for w in widths],
        out_shape=[jax.ShapeDtypeStruct((t, w), F32) for w in widths],
        compiler_params=_cparams(dimension_semantics=("parallel",)),
    )(z, z, mu)


def _shift_bwd(z, mu, dr, dk, dv, dlo, dgd):
    t, c = z.shape
    tr = 256
    nt = t // tr

    def body(z_ref, zp_ref, mu_ref, dr_ref, dk_ref, dv_ref, dlo_ref, dgd_ref,
             drn_ref, dkn_ref, dvn_ref, dlon_ref, dgdn_ref, dz_ref, dmu_ref):
        i = pl.program_id(0)
        zv, muv = z_ref[...], mu_ref[...]
        prev = zp_ref[7:8, :] * jnp.where(i == 0, 0.0, 1.0)
        row = lax.broadcasted_iota(jnp.int32, zv.shape, 0)
        zsh = jnp.where(row == 0, prev, pltpu.roll(zv, 1, 0))
        dzs = jnp.concatenate([dr_ref[...], dk_ref[...], dv_ref[...], dlo_ref[...], dgd_ref[...]], axis=1)
        nxt = jnp.concatenate([drn_ref[0:1, :], dkn_ref[0:1, :], dvn_ref[0:1, :], dlon_ref[0:1, :],
                               dgdn_ref[0:1, :]], axis=1) * jnp.where(i == nt - 1, 0.0, 1.0)
        u = dzs * muv
        un = jnp.where(row == tr - 1, nxt * muv, pltpu.roll(u, tr - 1, 0))
        dz_ref[...] = (dzs - u + un).astype(dz_ref.dtype)
        _accumulate(dmu_ref, jnp.sum(dzs * (zsh - zv), axis=0, keepdims=True))

    widths = (512, 512, 512, 128, 128)
    nxt_map = lambda i: (jnp.minimum((i + 1) * (tr // 8), t // 8 - 1), 0)
    return pl.pallas_call(
        body, name="rwkv_shift_bwd", grid=(nt,),
        in_specs=[pl.BlockSpec((tr, c), lambda i: (i, 0)),
                  pl.BlockSpec((8, c), lambda i: (jnp.maximum(i * (tr // 8) - 1, 0), 0)),
                  pl.BlockSpec((1, c), lambda i: (0, 0))]
        + [pl.BlockSpec((tr, w), lambda i: (i, 0)) for w in widths]
        + [pl.BlockSpec((8, w), nxt_map) for w in widths],
        out_specs=[pl.BlockSpec((tr, c), lambda i: (i, 0)), pl.BlockSpec((1, c), lambda i: (0, 0))],
        out_shape=[jax.ShapeDtypeStruct((t, c), BF16), jax.ShapeDtypeStruct((1, c), F32)],
        compiler_params=_cparams(dimension_semantics=("arbitrary",)),
    )(z, z, mu, dr, dk, dv, dlo, dgd, dr, dk, dv, dlo, dgd)


def _f_pre(k, lo, gd, w0, w2p, a0, a2p, g2, k_k, k_a):
    lane = lax.broadcasted_iota(jnp.int32, lo.shape, 1)
    lo_act = jnp.where(lane < DECAY_LORA, jnp.tanh(lo), lo)
    w = -_softplus(-(w0 + _bdot(lo_act, w2p))) - 0.5
    a = _sigmoid(a0 + _bdot(lo_act, a2p))
    g = _bdot(_sigmoid(gd), g2)
    kk = k * k_k
    k2 = k * (1.0 + (a - 1.0) * k_a)
    decay = jnp.exp(-jnp.exp(w))
    return k2, kk, a, decay, g


def _f_kk(kk, a):
    kkn = kk * lax.rsqrt(jnp.maximum(jnp.sum(kk * kk, axis=-1, keepdims=True), 1e-24))
    return kkn, kkn * a


def _f_post(y, r, k2, v, g, gn_w, gn_b, r_k):
    reps = y.shape[0] // RWKV_HEADS
    mean = jnp.mean(y, axis=-1, keepdims=True)
    var = jnp.mean(jnp.square(y - mean), axis=-1, keepdims=True)
    yn = (y - mean) * lax.rsqrt(var + GN_EPS)
    yn = yn * jnp.tile(gn_w, (reps, 1)) + jnp.tile(gn_b, (reps, 1))
    bonus = jnp.sum(r * k2 * jnp.tile(r_k, (reps, 1)), axis=-1, keepdims=True) * v
    return (yn + bonus) * g


def _to_v_rows(x):
    t = x.shape[0]
    return x.reshape(t, RWKV_HEADS, V_HI, V_LO).transpose(0, 2, 3, 1).reshape(t, V_HI, LANES)


def _from_v_rows(x):
    t = x.shape[0]
    return x.reshape(t, V_HI, V_LO, RWKV_HEADS).transpose(0, 3, 1, 2).reshape(t, RWKV_DIM)


def _k_cols(x):
    return jnp.tile(x, (V_LO, 1)).T


def _k_rows(x):
    xt = x.T
    out = xt[0:RWKV_HEADS]
    for l in range(1, V_LO):
        out = out + xt[l * RWKV_HEADS:(l + 1) * RWKV_HEADS]
    return out


def _wkv_fwd(r, w, k, kk, b, v):
    t = r.shape[0]
    tc = SCAN_CHUNK
    key_spec = pl.BlockSpec((tc, RWKV_HEADS, HEAD_DIM), lambda i: (i, 0, 0))
    row_spec = pl.BlockSpec((tc, V_HI, LANES), lambda i: (i, 0, 0))

    def body(r_ref, w_ref, k_ref, kk_ref, b_ref, v_ref, y_ref, st_ref, s_scr):
        @pl.when(pl.program_id(0) == 0)
        def _():
            s_scr[...] = jnp.zeros_like(s_scr)

        def step(ti, s):
            rc, wc, kc, kkc, bc = (_k_cols(ref[ti]) for ref in (r_ref, w_ref, k_ref, kk_ref, b_ref))
            vt = v_ref[ti]
            new, ys = [], []
            for j in range(V_HI):
                sa = -jnp.sum(s[j] * kkc, axis=0, keepdims=True)
                nj = s[j] * wc + bc * sa + kc * vt[j:j + 1]
                st_ref[ti, j] = nj
                ys.append(jnp.sum(nj * rc, axis=0, keepdims=True))
                new.append(nj)
            y_ref[ti] = jnp.concatenate(ys, axis=0)
            return tuple(new)

        s = lax.fori_loop(0, tc, step, tuple(s_scr[j] for j in range(V_HI)))
        for j in range(V_HI):
            s_scr[j] = s[j]

    return pl.pallas_call(
        body, name="wkv_fwd", grid=(t // tc,),
        in_specs=[key_spec] * 5 + [row_spec],
        out_specs=[row_spec, pl.BlockSpec((tc, V_HI, HEAD_DIM, LANES), lambda i: (i, 0, 0, 0))],
        out_shape=[jax.ShapeDtypeStruct((t, V_HI, LANES), F32),
                   jax.ShapeDtypeStruct((t, V_HI, HEAD_DIM, LANES), F32)],
        scratch_shapes=[pltpu.VMEM((V_HI, HEAD_DIM, LANES), F32)],
        compiler_params=_cparams(dimension_semantics=("arbitrary",)),
    )(r, w, k, kk, b, v)


def _wkv_bwd(r, w, k, kk, b, v, states, dy):
    t = r.shape[0]
    tc = SCAN_CHUNK
    nb = t // tc
    key_spec = pl.BlockSpec((tc, RWKV_HEADS, HEAD_DIM), lambda i: (nb - 1 - i, 0, 0))
    row_spec = pl.BlockSpec((tc, V_HI, LANES), lambda i: (nb - 1 - i, 0, 0))
    st_spec = pl.BlockSpec((tc, V_HI, HEAD_DIM, LANES), lambda i: (nb - 1 - i, 0, 0, 0))
    stp_spec = pl.BlockSpec((1, V_HI, HEAD_DIM, LANES), lambda i: (jnp.maximum((nb - 1 - i) * tc - 1, 0), 0, 0, 0))

    def body(r_ref, w_ref, k_ref, kk_ref, b_ref, v_ref, st_ref, stp_ref, dy_ref,
             dr_ref, dw_ref, dk_ref, dkk_ref, db_ref, dv_ref, ds_scr):
        @pl.when(pl.program_id(0) == 0)
        def _():
            ds_scr[...] = jnp.zeros_like(ds_scr)

        def colsum(x):
            return jnp.sum(x, axis=0, keepdims=True)

        def step(ti, ds, sp):
            rc, wc, kc, kkc, bc = (_k_cols(ref[ti]) for ref in (r_ref, w_ref, k_ref, kk_ref, b_ref))
            vt, dyt = v_ref[ti], dy_ref[ti]
            acc_r = acc_k = acc_b = acc_w = acc_kk = None
            new, dvs = [], []
            for j in range(V_HI):
                st = st_ref[ti, j]
                dsj = ds[j] + rc * dyt[j:j + 1]
                sa = -colsum(sp[j] * kkc)
                dsa = colsum(dsj * bc)
                dvs.append(colsum(dsj * kc))
                parts = (st * dyt[j:j + 1], dsj * vt[j:j + 1], dsj * sa, dsj * sp[j], sp[j] * dsa)
                if j == 0:
                    acc_r, acc_k, acc_b, acc_w, acc_kk = parts
                else:
                    acc_r, acc_k, acc_b, acc_w, acc_kk = (a + p for a, p in
                                                          zip((acc_r, acc_k, acc_b, acc_w, acc_kk), parts))
                new.append(dsj * wc - kkc * dsa)
            dv_ref[ti] = jnp.concatenate(dvs, axis=0)
            dr_ref[ti] = _k_rows(acc_r)
            dk_ref[ti] = _k_rows(acc_k)
            db_ref[ti] = _k_rows(acc_b)
            dw_ref[ti] = _k_rows(acc_w)
            dkk_ref[ti] = -_k_rows(acc_kk)
            return tuple(new)

        def loop_body(n, ds):
            ti = tc - 1 - n
            return step(ti, ds, tuple(st_ref[ti - 1, j] for j in range(V_HI)))

        ds = lax.fori_loop(0, tc - 1, loop_body, tuple(ds_scr[j] for j in range(V_HI)))
        keep = jnp.where(pl.program_id(0) == nb - 1, 0.0, 1.0)
        ds = step(0, ds, tuple(stp_ref[0, j] * keep for j in range(V_HI)))
        for j in range(V_HI):
            ds_scr[j] = ds[j]

    key_out = jax.ShapeDtypeStruct((t, RWKV_HEADS, HEAD_DIM), F32)
    return pl.pallas_call(
        body, name="wkv_bwd", grid=(nb,),
        in_specs=[key_spec] * 5 + [row_spec, st_spec, stp_spec, row_spec],
        out_specs=[key_spec] * 5 + [row_spec],
        out_shape=[key_out] * 5 + [jax.ShapeDtypeStruct((t, V_HI, LANES), F32)],
        scratch_shapes=[pltpu.VMEM((V_HI, HEAD_DIM, LANES), F32)],
        compiler_params=_cparams(dimension_semantics=("arbitrary",)),
    )(r, w, k, kk, b, v, states, states, dy)


PAIR = 2 * HEAD_DIM
N_PAIRS = ATTN_HEADS // 2
Q_COL0 = RWKV_COLS // PAIR
K_COL0 = Q_COL0 + N_PAIRS
V_COL0 = K_COL0 + N_PAIRS


def _swap_halves(x):
    lane = lax.broadcasted_iota(jnp.int32, x.shape, 1)
    return jnp.where((lane & (HEAD_DIM - 1)) < HEAD_DIM // 2, pltpu.roll(x, PAIR - HEAD_DIM // 2, 1),
                     pltpu.roll(x, HEAD_DIM // 2, 1))


@jax.custom_vjp
def _rope(x, cosf, sinf):
    return x * cosf + _swap_halves(x) * sinf


def _rope_fwd(x, cosf, sinf):
    return _rope(x, cosf, sinf), (cosf, sinf)


def _rope_bwd(res, d):
    cosf, sinf = res
    return d * cosf + _swap_halves(d * sinf), jnp.zeros_like(cosf), jnp.zeros_like(sinf)


_rope.defvjp(_rope_fwd, _rope_bwd)


def _head_sums(x):
    lane = lax.broadcasted_iota(jnp.int32, x.shape, 1)
    lo = jnp.where(lane < HEAD_DIM, 1.0, 0.0)
    hi = 1.0 - lo
    return lo * jnp.sum(x * lo, axis=1, keepdims=True) + hi * jnp.sum(x * hi, axis=1, keepdims=True)


def _f_qk(x, cosf, sinf, gain2):
    xn = x * lax.rsqrt(_head_sums(x * x) * (1.0 / HEAD_DIM) + RMS_EPS) * gain2
    return _rope(xn, cosf, sinf)


def _qk_prep(z, tab, q_gain, k_gain):
    t = z.shape[0]
    tr = 256

    def body(z_ref, c_ref, s_ref, qg_ref, kg_ref, o_ref):
        g = jnp.where(pl.program_id(0) < N_PAIRS, qg_ref[...], kg_ref[...])
        o_ref[...] = _f_qk(z_ref[...], c_ref[...], s_ref[...], jnp.concatenate([g, g], axis=1))

    gain = pl.BlockSpec((1, HEAD_DIM), lambda c, i: (0, 0))
    return pl.pallas_call(
        body, name="qk_prep", grid=(2 * N_PAIRS, t // tr),
        in_specs=[pl.BlockSpec((tr, PAIR), lambda c, i: (i, Q_COL0 + c)), pl.BlockSpec((tr, PAIR), lambda c, i: (i, 0)),
                  pl.BlockSpec((tr, PAIR), lambda c, i: (i, 1)), gain, gain],
        out_specs=pl.BlockSpec((tr, PAIR), lambda c, i: (i, c)),
        out_shape=jax.ShapeDtypeStruct((t, 2 * N_PAIRS * PAIR), F32),
        compiler_params=_cparams(dimension_semantics=("parallel", "parallel")),
    )(z, tab, tab, q_gain, k_gain)


def _qk_prep_bwd(z, tab, q_gain, k_gain, dq, dk):
    t = z.shape[0]
    tr = 256

    def body(z_ref, c_ref, s_ref, qg_ref, kg_ref, dq_ref, dk_ref, dz_ref, dqg_ref, dkg_ref):
        c, i = pl.program_id(0), pl.program_id(1)
        is_q = c < N_PAIRS
        g = jnp.where(is_q, qg_ref[...], kg_ref[...])
        d = jnp.where(is_q, dq_ref[...], dk_ref[...])
        _, vjp = jax.vjp(lambda xx, gg: _f_qk(xx, c_ref[...], s_ref[...], gg), z_ref[...],
                         jnp.concatenate([g, g], axis=1))
        dx, dg2 = vjp(d)
        dz_ref[...] = dx.astype(dz_ref.dtype)
        dg = dg2[:, :HEAD_DIM] + dg2[:, HEAD_DIM:]
        first_q = jnp.logical_and(c == 0, i == 0)
        first_k = jnp.logical_and(c == N_PAIRS, i == 0)

        @pl.when(first_q)
        def _():
            dqg_ref[...] = dg

        @pl.when(jnp.logical_and(is_q, jnp.logical_not(first_q)))
        def _():
            dqg_ref[...] += dg

        @pl.when(first_k)
        def _():
            dkg_ref[...] = dg

        @pl.when(jnp.logical_and(jnp.logical_not(is_q), jnp.logical_not(first_k)))
        def _():
            dkg_ref[...] += dg

    gain = pl.BlockSpec((1, HEAD_DIM), lambda c, i: (0, 0))
    return pl.pallas_call(
        body, name="qk_prep_bwd", grid=(2 * N_PAIRS, t // tr),
        in_specs=[pl.BlockSpec((tr, PAIR), lambda c, i: (i, Q_COL0 + c)), pl.BlockSpec((tr, PAIR), lambda c, i: (i, 0)),
                  pl.BlockSpec((tr, PAIR), lambda c, i: (i, 1)), gain, gain,
                  pl.BlockSpec((tr, PAIR), lambda c, i: (i, jnp.minimum(c, N_PAIRS - 1))),
                  pl.BlockSpec((tr, PAIR), lambda c, i: (i, jnp.maximum(c - N_PAIRS, 0)))],
        out_specs=[pl.BlockSpec((tr, PAIR), lambda c, i: (i, c)), gain, gain],
        out_shape=[jax.ShapeDtypeStruct((t, 2 * N_PAIRS * PAIR), BF16), jax.ShapeDtypeStruct((1, HEAD_DIM), F32),
                   jax.ShapeDtypeStruct((1, HEAD_DIM), F32)],
        compiler_params=_cparams(dimension_semantics=("arbitrary", "arbitrary")),
    )(z, tab, tab, q_gain, k_gain, dq, dk)


def _attn_block(q, kp, kc, vp, vc, kmin):
    k2 = jnp.concatenate([kp, kc], axis=0)
    v2 = jnp.concatenate([vp, vc], axis=0)
    s = _bdot_nt(q, k2) * (HEAD_DIM ** -0.5)
    qi = lax.broadcasted_iota(jnp.int32, s.shape, 0)
    kj = lax.broadcasted_iota(jnp.int32, s.shape, 1)
    dist = qi + BAND_BLOCK - kj
    valid = (dist >= 0) & (dist <= BAND_BLOCK) & (kj >= kmin)
    s = jnp.where(valid, s, NEG_INF)
    m = lax.stop_gradient(jnp.max(s, axis=-1, keepdims=True))
    e = jnp.exp(s - m)
    l = jnp.sum(e, axis=-1, keepdims=True)
    o = _bdot(e, v2) / l
    return o, m + jnp.log(l)


def _fold(src_ref, dst_ref, dil):
    t = src_ref.shape[0]
    ln = t // dil
    for j in range(dil):
        dst_ref[j * ln:(j + 1) * ln, :] = src_ref[pl.ds(j, ln, stride=dil), :]


def _unfold(src_ref, dst_ref, dil):
    t = src_ref.shape[0]
    ln = t // dil
    for j in range(dil):
        dst_ref[pl.ds(j, ln, stride=dil), :] = src_ref[j * ln:(j + 1) * ln, :]


def _per_group(fn):
    pair = pl.program_id(0)
    for gi, (_, dil) in enumerate(ATTN_GROUPS):
        @pl.when(jnp.logical_or(pair == 2 * gi, pair == 2 * gi + 1))
        def _(dil=dil):
            fn(dil)


def _block_rows(idx, blocks_per_seq):
    first = (idx & (blocks_per_seq - 1)) == 0
    cur = pl.ds(pl.multiple_of(idx * BAND_BLOCK, BAND_BLOCK), BAND_BLOCK)
    prev = pl.ds(pl.multiple_of(jnp.maximum(idx - 1, 0) * BAND_BLOCK, BAND_BLOCK), BAND_BLOCK)
    return first, cur, prev


def _heads(x):
    return x[:, :HEAD_DIM], x[:, HEAD_DIM:]


def _attn_fwd(qk, z):
    t = z.shape[0]
    n_blocks = t // BAND_BLOCK

    def body(q_ref, k_ref, v_ref, o_ref, lse_ref, qf, kf, vf, of, lf):
        def run(dil):
            _fold(q_ref, qf, dil)
            _fold(k_ref, kf, dil)
            _fold(v_ref, vf, dil)
            blocks_per_seq = n_blocks // dil

            def block(idx, carry):
                first, cur, prev = _block_rows(idx, blocks_per_seq)
                kmin = jnp.where(first, BAND_BLOCK, 0)
                outs, lses = [], []
                for q, kp, kc, vp, vc in zip(_heads(qf[cur, :]), _heads(kf[prev, :]), _heads(kf[cur, :]),
                                             _heads(vf[prev, :]), _heads(vf[cur, :])):
                    o, ls = _attn_block(q, kp, kc, vp, vc, kmin)
                    outs.append(o)
                    lses.append(jnp.broadcast_to(ls, o.shape))
                of[cur, :] = jnp.concatenate(outs, axis=1)
                lf[cur, :] = jnp.concatenate(lses, axis=1)
                return carry

            lax.fori_loop(0, n_blocks, block, 0)
            _unfold(of, o_ref, dil)
            _unfold(lf, lse_ref, dil)

        _per_group(run)

    slab = jax.ShapeDtypeStruct((t, N_PAIRS * PAIR), F32)
    out_spec = pl.BlockSpec((t, PAIR), lambda p: (0, p))
    return pl.pallas_call(
        body, name="attn_fwd", grid=(N_PAIRS,),
        in_specs=[pl.BlockSpec((t, PAIR), lambda p: (0, p)), pl.BlockSpec((t, PAIR), lambda p: (0, N_PAIRS + p)),
                  pl.BlockSpec((t, PAIR), lambda p: (0, V_COL0 + p))],
        out_specs=[out_spec, out_spec], out_shape=[slab, slab],
        scratch_shapes=[pltpu.VMEM((t, PAIR), F32)] * 5,
        compiler_params=_cparams(dimension_semantics=("parallel",)),
    )(qk, qk, z)


def _attn_bwd(qk, z, do, dlse):
    t = z.shape[0]
    n_blocks = t // BAND_BLOCK

    def body(q_ref, k_ref, v_ref, do_ref, dl_ref, dq_ref, dk_ref, dv_ref, qf, kf, vf, dof, dlf, dqf, dkf, dvf):
        def run(dil):
            for src, dst in ((q_ref, qf), (k_ref, kf), (v_ref, vf), (do_ref, dof), (dl_ref, dlf)):
                _fold(src, dst, dil)
            blocks_per_seq = n_blocks // dil

            def block(idx, carry):
                first, cur, prev = _block_rows(idx, blocks_per_seq)
                kmin = jnp.where(first, BAND_BLOCK, 0)
                grads = []
                for q, kp, kc, vp, vc, do_h, dl_h in zip(
                        _heads(qf[cur, :]), _heads(kf[prev, :]), _heads(kf[cur, :]), _heads(vf[prev, :]),
                        _heads(vf[cur, :]), _heads(dof[cur, :]), _heads(dlf[cur, :])):
                    _, vjp = jax.vjp(functools.partial(_attn_block, kmin=kmin), q, kp, kc, vp, vc)
                    grads.append(vjp((do_h, jnp.sum(dl_h, axis=1, keepdims=True))))
                dq, dkp, dkc, dvp, dvc = (jnp.concatenate([a, b], axis=1) for a, b in zip(*grads))
                dqf[cur, :] = dq
                dkf[cur, :] = dkc
                dvf[cur, :] = dvc

                @pl.when(jnp.logical_not(first))
                def _():
                    dkf[prev, :] += dkp
                    dvf[prev, :] += dvp

                return carry

            lax.fori_loop(0, n_blocks, block, 0)
            _unfold(dqf, dq_ref, dil)
            _unfold(dkf, dk_ref, dil)
            _unfold(dvf, dv_ref, dil)

        _per_group(run)

    slab = jax.ShapeDtypeStruct((t, N_PAIRS * PAIR), F32)
    own = pl.BlockSpec((t, PAIR), lambda p: (0, p))
    return pl.pallas_call(
        body, name="attn_bwd", grid=(N_PAIRS,),
        in_specs=[own, pl.BlockSpec((t, PAIR), lambda p: (0, N_PAIRS + p)),
                  pl.BlockSpec((t, PAIR), lambda p: (0, V_COL0 + p)), own, own],
        out_specs=[own] * 3, out_shape=[slab] * 3,
        scratch_shapes=[pltpu.VMEM((t, PAIR), F32)] * 8,
        compiler_params=_cparams(dimension_semantics=("parallel",)),
    )(qk, qk, z, do, dlse)


def _f_comb(o1, o2, o3, l1, l2, l3):
    m = jnp.maximum(jnp.maximum(l1, l2), l3)
    e1, e2, e3 = jnp.exp(l1 - m), jnp.exp(l2 - m), jnp.exp(l3 - m)
    den = e1 + e2 + e3
    return (e1 / den) * o1 + (e2 / den) * o2 + (e3 / den) * o3


def _all_gather_hbm(name, arrs):
    na = len(arrs)

    def body(*refs):
        x_refs, out_refs = refs[:na], refs[na:2 * na]
        send_sems, recv_sems, local_sems = refs[2 * na:]
        mx, my, mc = lax.axis_index("x"), lax.axis_index("y"), lax.axis_index("c")
        me, sibling = (mx, my, mc), (mx, my, 1 - mc)
        chips = [(1 - mx, my), (mx, 1 - my), (1 - mx, 1 - my)]

        def slot(a, px, py, pc):
            return out_refs[a].at[4 * px + 2 * py + pc]

        def copy(a, k, block, to, src=None):
            return pltpu.make_async_remote_copy(
                src_ref=slot(a, *block) if src is None else src, dst_ref=slot(a, *block),
                send_sem=send_sems.at[a, k], recv_sem=recv_sems.at[a, k], device_id=to, device_id_type=MESH)

        mine = [pltpu.make_async_copy(x_refs[a], slot(a, *me), local_sems.at[a]) for a in range(na)]
        for cp in mine:
            cp.start()
        first = []
        for a in range(na):
            first.append(copy(a, 0, me, sibling, src=x_refs[a]))
            first += [copy(a, 1 + j, me, (*chip, mc), src=x_refs[a]) for j, chip in enumerate(chips)]
        for cp in first:
            cp.start()
        passed = []
        for j, chip in enumerate(chips):
            for a in range(na):
                copy(a, 1 + j, (*chip, mc), me).wait_recv()
                passed.append(copy(a, 4 + j, (*chip, mc), sibling))
                passed[-1].start()
        for a in range(na):
            copy(a, 0, sibling, me).wait_recv()
            for j, chip in enumerate(chips):
                copy(a, 4 + j, (*chip, 1 - mc), me).wait_recv()
        for cp in first + passed:
            cp.wait_send()
        for cp in mine:
            cp.wait()

    hbm = pl.BlockSpec(memory_space=pl.ANY)
    return pl.pallas_call(
        body, name=name,
        out_shape=[jax.ShapeDtypeStruct((N_DEV,) + a.shape, a.dtype) for a in arrs],
        in_specs=[hbm] * na, out_specs=[hbm] * na,
        scratch_shapes=[pltpu.SemaphoreType.DMA((na, 7)), pltpu.SemaphoreType.DMA((na, 7)),
                        pltpu.SemaphoreType.DMA((na,))],
    )(*arrs)


def _all_gather_vmem(x):
    rws, cols = x.shape

    def body(x_ref, out_ref, send_sems, recv_sems):
        mx, my, mc = lax.axis_index("x"), lax.axis_index("y"), lax.axis_index("c")
        me, sibling = (mx, my, mc), (mx, my, 1 - mc)
        chips = [(1 - mx, my), (mx, 1 - my), (1 - mx, 1 - my)]

        def slot(px, py, pc):
            return out_ref.at[4 * px + 2 * py + pc]

        def copy(k, block, to, src=None):
            return pltpu.make_async_remote_copy(
                src_ref=slot(*block) if src is None else src, dst_ref=slot(*block),
                send_sem=send_sems.at[k], recv_sem=recv_sems.at[k], device_id=to, device_id_type=MESH)

        first = [copy(0, me, sibling, src=x_ref)]
        first += [copy(1 + j, me, (*chip, mc), src=x_ref) for j, chip in enumerate(chips)]
        for cp in first:
            cp.start()
        out_ref[4 * mx + 2 * my + mc] = x_ref[...]
        passed = [copy(4 + j, (*chip, mc), sibling) for j, chip in enumerate(chips)]
        for j, chip in enumerate(chips):
            copy(1 + j, (*chip, mc), me).wait_recv()
            passed[j].start()
        copy(0, sibling, me).wait_recv()
        for j, chip in enumerate(chips):
            copy(4 + j, (*chip, 1 - mc), me).wait_recv()
        for cp in first + passed:
            cp.wait_send()

    return pl.pallas_call(
        body, name="all_gather_small",
        out_shape=jax.ShapeDtypeStruct((N_DEV, rws, cols), x.dtype),
        in_specs=[pl.BlockSpec(memory_space=pltpu.VMEM)], out_specs=pl.BlockSpec(memory_space=pltpu.VMEM),
        scratch_shapes=[pltpu.SemaphoreType.DMA((7,)), pltpu.SemaphoreType.DMA((7,))],
    )(x)


def _all_to_all_hbm(name, arrs):
    na = len(arrs)

    def body(*refs):
        g_refs, out_refs = refs[:na], refs[na:2 * na]
        send_sems, recv_sems, local_sems = refs[2 * na:]
        mx, my, mc = lax.axis_index("x"), lax.axis_index("y"), lax.axis_index("c")
        me = 4 * mx + 2 * my + mc
        mine = [pltpu.make_async_copy(g_refs[a].at[me], out_refs[a].at[me], local_sems.at[a]) for a in range(na)]
        for cp in mine:
            cp.start()
        copies = []
        for k in range(1, N_DEV):
            px, py, pc = mx ^ (k >> 2), my ^ ((k >> 1) & 1), mc ^ (k & 1)
            peer = 4 * px + 2 * py + pc
            for a in range(na):
                copies.append(pltpu.make_async_remote_copy(
                    src_ref=g_refs[a].at[peer], dst_ref=out_refs[a].at[me], send_sem=send_sems.at[a, k - 1],
                    recv_sem=recv_sems.at[a, k - 1], device_id=(px, py, pc), device_id_type=MESH))
        for cp in copies:
            cp.start()
        for cp in copies:
            cp.wait_recv()
        for cp in copies:
            cp.wait_send()
        for cp in mine:
            cp.wait()

    hbm = pl.BlockSpec(memory_space=pl.ANY)
    return pl.pallas_call(
        body, name=name,
        out_shape=[jax.ShapeDtypeStruct(a.shape, a.dtype) for a in arrs],
        in_specs=[hbm] * na, out_specs=[hbm] * na,
        scratch_shapes=[pltpu.SemaphoreType.DMA((na, 7)), pltpu.SemaphoreType.DMA((na, 7)),
                        pltpu.SemaphoreType.DMA((na,))],
    )(*arrs)


def _scatter_copies(g_refs, land_refs, send_sems, recv_sems):
    mx, my, mc = lax.axis_index("x"), lax.axis_index("y"), lax.axis_index("c")
    me = 4 * mx + 2 * my + mc
    copies = []
    for k in range(1, N_DEV):
        px, py, pc = mx ^ (k >> 2), my ^ ((k >> 1) & 1), mc ^ (k & 1)
        peer = 4 * px + 2 * py + pc
        for a, (g_ref, land_ref) in enumerate(zip(g_refs, land_refs)):
            copies.append(pltpu.make_async_remote_copy(
                src_ref=g_ref.at[peer], dst_ref=land_ref.at[me], send_sem=send_sems.at[a * (N_DEV - 1) + k - 1],
                recv_sem=recv_sems.at[a * (N_DEV - 1) + k - 1], device_id=(px, py, pc), device_id_type=MESH))
    return copies


_HBM = pl.BlockSpec(memory_space=pltpu.HBM)
_SEM = pl.BlockSpec(memory_space=pltpu.SEMAPHORE)
_DATAFLOW = pltpu.SideEffectType.DATAFLOW_SIDE_EFFECTING


def _gather_copies(x_refs, land_refs, send_sems, recv_sems):
    mx, my, mc = lax.axis_index("x"), lax.axis_index("y"), lax.axis_index("c")
    me = 4 * mx + 2 * my + mc
    copies = []
    for k in range(1, N_DEV):
        px, py, pc = mx ^ (k >> 2), my ^ ((k >> 1) & 1), mc ^ (k & 1)
        for a, (x_ref, land_ref) in enumerate(zip(x_refs, land_refs)):
            copies.append(pltpu.make_async_remote_copy(
                src_ref=x_ref, dst_ref=land_ref.at[me], send_sem=send_sems.at[a * (N_DEV - 1) + k - 1],
                recv_sem=recv_sems.at[a * (N_DEV - 1) + k - 1], device_id=(px, py, pc), device_id_type=MESH))
    return copies


def _exchange_start(name, copies, srcs, lands, after):
    na = len(srcs)

    def body(*refs):
        for cp in copies(refs[:na], refs[na:2 * na], refs[2 * na + 1], refs[2 * na + 2]):
            cp.start()
        refs[-1][...] = jnp.zeros_like(refs[-1])

    in_hbm = lambda a: pltpu.with_memory_space_constraint(a, pltpu.HBM)
    outs = pl.pallas_call(
        body, name=name,
        out_shape=(pltpu.SemaphoreType.DMA((na * (N_DEV - 1),)), pltpu.SemaphoreType.DMA((na * (N_DEV - 1),)),
                   *[pltpu.HBM(a.shape, a.dtype) for a in list(srcs) + list(lands)],
                   jax.ShapeDtypeStruct((8, LANES), F32)),
        in_specs=[_HBM] * (2 * na) + [pl.BlockSpec(memory_space=pl.ANY)],
        out_specs=(_SEM, _SEM, *[_HBM] * (2 * na), pl.BlockSpec(memory_space=pltpu.VMEM)),
        input_output_aliases={i: 2 + i for i in range(2 * na)},
        compiler_params=pltpu.CompilerParams(has_side_effects=_DATAFLOW),
    )(*[in_hbm(a) for a in srcs], *[in_hbm(a) for a in lands], after)
    return outs[0], outs[1], outs[2:2 + na], outs[2 + na:2 + 2 * na], outs[-1]


def _exchange_wait(name, copies, send_sems, recv_sems, srcs, lands, after):
    na = len(srcs)

    def body(*refs):
        for cp in copies(refs[:na], refs[na:2 * na], refs[2 * na], refs[2 * na + 1]):
            cp.wait_send()
            cp.wait_recv()

    outs = pl.pallas_call(
        body, name=name,
        out_shape=[pltpu.HBM(a.shape, a.dtype) for a in list(srcs) + list(lands)],
        in_specs=[_HBM] * (2 * na) + [_SEM, _SEM, pl.BlockSpec(memory_space=pl.ANY)], out_specs=[_HBM] * (2 * na),
        input_output_aliases={i: i for i in range(2 * na)},
        compiler_params=pltpu.CompilerParams(has_side_effects=_DATAFLOW),
    )(*srcs, *lands, send_sems, recv_sems, after)
    return outs[:na], outs[na:]


def _sum_slots(name, g, tr):
    _, rws, cols = g.shape

    def body(g_ref, o_ref):
        acc = g_ref[0].astype(F32)
        for j in range(1, N_DEV):
            acc = acc + g_ref[j].astype(F32)
        o_ref[...] = acc

    return pl.pallas_call(
        body, name=name, grid=(rws // tr,),
        in_specs=[pl.BlockSpec((N_DEV, tr, cols), lambda i: (0, i, 0))],
        out_specs=pl.BlockSpec((tr, cols), lambda i: (i, 0)),
        out_shape=jax.ShapeDtypeStruct((rws, cols), F32),
        compiler_params=_cparams(dimension_semantics=("parallel",)),
    )(g)


def _adam_math(wv, gv, mv, vv):
    mn = ADAM_B1 * mv + (1.0 - ADAM_B1) * gv
    vn = ADAM_B2 * vv + (1.0 - ADAM_B2) * jnp.square(gv)
    m_hat = mn / (1.0 - ADAM_B1 ** ADAM_STEP)
    v_hat = vn / (1.0 - ADAM_B2 ** ADAM_STEP)
    delta = -ADAM_LR * (m_hat / (jnp.sqrt(v_hat) + ADAM_EPS) + ADAM_WD * wv)
    return delta, mn, vn


def _adamw(name, w, g, m, v, tr):
    return _rowwise(name, _adam_math, [w, g, m, v], [], [(LANES, F32)] * 3, tr=tr)


def _adamw_slots(name, recv, own, rb, cb, cw, w, m, v, tr):
    nr, nc = w.shape

    def body(g_ref, own_ref, w_ref, m_ref, v_ref, go_ref, d_ref, mo_ref, vo_ref):
        me = 4 * lax.axis_index("x") + 2 * lax.axis_index("y") + lax.axis_index("c")
        acc = None
        for s in range(N_DEV):
            part = jnp.where(me == s, own_ref[s], g_ref[s]).astype(F32)
            acc = part if acc is None else acc + part
        g = acc[:, :nc]
        go_ref[...] = g
        d_ref[...], mo_ref[...], vo_ref[...] = _adam_math(w_ref[...], g, m_ref[...], v_ref[...])

    nat = pl.BlockSpec((tr, nc), lambda i: (i, 0))
    slots = pl.BlockSpec((N_DEV, tr, cw), lambda i: (0, rb + i, cb))
    return pl.pallas_call(
        body, name=name, grid=(nr // tr,),
        in_specs=[slots, slots, nat, nat, nat],
        out_specs=[nat] * 4, out_shape=[jax.ShapeDtypeStruct((nr, nc), F32)] * 4,
        compiler_params=_cparams(dimension_semantics=("parallel",)),
    )(recv, own, w, m, v)


def _local_blocks(w):
    pad_cols = lambda a: jnp.pad(a, ((0, 0), (0, FF_PAD - FF_SHARD)))
    pad_rows = lambda a: jnp.pad(a, ((0, FF_PAD - FF_SHARD), (0, 0)))
    gate_up = lambda tag: jnp.concatenate([pad_cols(w[tag + "_w_gate"]), pad_cols(w[tag + "_w_up"])], axis=1)
    blocks = {
        "ffn1_gu": gate_up("ffn1"), "ffn1_d": pad_rows(w["ffn1_w_down"]), "w_in": w["w_in"],
        "lora": jnp.concatenate([w["rwkv_w2"], w["rwkv_a2"], w["rwkv_g2"]], axis=0),
        "br": jnp.concatenate([w["w_br_rwkv"], w["w_br_attn"], w["ple_w_proj"]], axis=0),
        "w_out": w["w_out"], "ffn2_gu": gate_up("ffn2"), "ffn2_d": pad_rows(w["ffn2_w_down"]),
        "ple_gate": w["ple_w_gate"],
    }
    return {n: a.astype(BF16) for n, a in blocks.items()}


GATHER_GROUPS = {"head": ("ffn1_gu", "ffn1_d"), "mid": ("w_in", "lora"),
                 "rest": ("br", "w_out", "ffn2_gu", "ffn2_d", "ple_gate")}

SCATTER_GROUPS = {"tail": ("ple_gate", "ple_proj", "ffn2_gu", "ffn2_d"), "mixer": ("w_out", "br", "lora", "w_in"),
                  "ffn1_down": ("ffn1_d",), "head": ("ffn1_gu",)}

ADAM_PLAN = (
    ("ffn1_w_gate", "ffn1_gu", 0, 0, FF_PAD, 256), ("ffn1_w_up", "ffn1_gu", 0, 1, FF_PAD, 256),
    ("ffn1_w_down", "ffn1_d", 0, 0, D_MODEL, FF_SHARD // 2), ("w_in", "w_in", 0, 0, IN_SHARD, 256),
    ("rwkv_w2", "lora", 0, 0, HEAD_DIM, 64), ("rwkv_a2", "lora", 1, 0, HEAD_DIM, 64),
    ("rwkv_g2", "lora", 2, 0, HEAD_DIM, 64),
    ("w_br_rwkv", "br", 0, 0, OUT_SHARD, 256), ("w_br_attn", "br", 2, 0, OUT_SHARD, 256),
    ("ple_w_proj", "ple_proj", 0, 0, OUT_SHARD, 256), ("w_out", "w_out", 0, 0, D_MODEL, OUT_SHARD),
    ("ffn2_w_gate", "ffn2_gu", 0, 0, FF_PAD, 256), ("ffn2_w_up", "ffn2_gu", 0, 1, FF_PAD, 256),
    ("ffn2_w_down", "ffn2_d", 0, 0, D_MODEL, FF_SHARD // 2), ("ple_w_gate", "ple_gate", 0, 0, D_MODEL, OUT_SHARD),
)


def _pack_small(arrs, rows):
    flat = jnp.concatenate([a.reshape(-1) for a in arrs])
    return jnp.pad(flat, (0, rows * LANES - flat.shape[0])).reshape(rows, LANES)


def _unpack_small(flat, like):
    flat = flat.reshape(-1)
    out, off = [], 0
    for a in like:
        out.append(flat[off:off + a.size].reshape(a.shape))
        off += a.size
    return out


def _local_step(x, p, pos, target, sm, wg, fetch, on_grads):
    t = x.shape[0]
    hv = t * RWKV_HEADS

    w_d1 = wg["ffn1_d"].reshape(FF_HID, D_MODEL)
    x1, ffn1_saved = _ffn_fwd("ffn1", x, sm["ffn1_norm"], wg["ffn1_gu"], w_d1)
    wg = {**wg, **fetch("mid", x1)}
    full_cols = lambda blk: blk.transpose(1, 0, 2).reshape(blk.shape[1], N_DEV * blk.shape[2])
    lora_w2 = full_cols(wg["lora"][:, :DECAY_LORA])
    lora_a2 = full_cols(wg["lora"][:, DECAY_LORA:DECAY_LORA + ICLR_LORA])
    lora_g2 = full_cols(wg["lora"][:, DECAY_LORA + ICLR_LORA:])
    h2 = _norm_fwd("mix_norm", x1, sm["mix_norm"])
    z = _mmc_nn("w_in", h2, wg["w_in"], 0, 0, IN_SHARD)
    z_r = z[:, :RWKV_COLS]
    z_g = z[:, RWKV_COLS + 3 * ATTN_DIM:]

    r, k, v, lo, gd = _shift_fwd(z_r, sm["rwkv_mu"])
    zero_lo = jnp.zeros((DECAY_LORA, RWKV_DIM), BF16)
    w2p = jnp.concatenate([lora_w2, zero_lo], axis=0).astype(F32)
    a2p = jnp.concatenate([zero_lo, lora_a2], axis=0).astype(F32)
    pre_params = [sm["rwkv_w0"], w2p, sm["rwkv_a0"], a2p, lora_g2.astype(F32), sm["rwkv_k_k"], sm["rwkv_k_a"]]
    wide = [(RWKV_DIM, F32)]
    k2, kk, a, decay, g = _rowwise("rwkv_pre", _f_pre, [k, lo, gd], pre_params, wide * 5)
    as_heads = lambda u: u.reshape(hv, HEAD_DIM)
    kkn, b = _rowwise("rwkv_kk", _f_kk, [as_heads(kk), as_heads(a)], [], [(HEAD_DIM, F32)] * 2, tr=512)
    scan_in = [u.reshape(t, RWKV_HEADS, HEAD_DIM) for u in (r, decay, k2, kkn, b)]
    v_rows = _to_v_rows(v)
    y_rows, states = _wkv_fwd(*scan_in, v_rows)
    y = _from_v_rows(y_rows)
    post_params = [sm["rwkv_gn_w"].reshape(RWKV_HEADS, HEAD_DIM), sm["rwkv_gn_b"].reshape(RWKV_HEADS, HEAD_DIM),
                   sm["rwkv_r_k"].reshape(RWKV_HEADS, HEAD_DIM)]
    post_rows = [as_heads(y), as_heads(r), as_heads(k2), as_heads(v), as_heads(g)]
    y_rwkv = _rowwise("rwkv_post", lambda *av: (_f_post(*av),), post_rows, post_params, [(HEAD_DIM, F32)], tr=512)[0]
    y_rwkv = y_rwkv.reshape(t, RWKV_DIM)

    inv_freq = 1.0 / (ROPE_THETA ** (jnp.arange(0, HEAD_DIM, 2, dtype=F32) / HEAD_DIM))
    freq2 = jnp.tile(inv_freq, 2 * PAIR // HEAD_DIM).reshape(1, PAIR)
    half = jnp.ones((HEAD_DIM // 2,), F32)
    sign2 = jnp.tile(jnp.concatenate([-half, half]), PAIR // HEAD_DIM).reshape(1, PAIR)

    def rope_table(posv, fr, sg):
        ang = posv * fr
        return (jnp.concatenate([jnp.cos(ang), jnp.sin(ang) * sg], axis=1),)
    tab = _rowwise("rope_table", rope_table, [pos.astype(F32).reshape(t, 1)], [freq2, sign2], [(2 * PAIR, F32)])[0]
    qk = _qk_prep(z, tab, sm["q_norm"], sm["k_norm"])
    o_all, lse_all = _attn_fwd(qk, z)
    gw = HEADS_PER_GROUP * HEAD_DIM

    def by_group(ov, lv):
        return [ov[:, i * gw:(i + 1) * gw] for i in range(3)] + [lv[:, i * gw:(i + 1) * gw] for i in range(3)]
    y_attn = _rowwise("attn_comb", lambda ov, lv: (_f_comb(*by_group(ov, lv)),), [o_all, lse_all], [], [(gw, F32)])[0]

    wg = {**wg, **fetch("rest", y_rwkv)}
    w_d2 = wg["ffn2_d"].reshape(FF_HID, D_MODEL)
    w_out = wg["w_out"].reshape(D_MODEL, D_MODEL)
    w_pg = wg["ple_gate"].reshape(D_MODEL, D_MODEL)
    u_r =_mmc_nn("br_rwkv", y_rwkv, wg["br"], 0, 0, OUT_SHARD)
    u_a = _mmc_nn("br_attn", y_attn, wg["br"], 2, 0, OUT_SHARD)

    def f_merge(zgr, zga, ur, ua):
        return _sigmoid(zgr) * ur + _sigmoid(zga) * ua
    merged = _rowwise("merge", lambda zg, ur, ua: (f_merge(zg[:, :D_MODEL], zg[:, D_MODEL:], ur, ua),),
                      [z_g, u_r, u_a], [], [(D_MODEL, BF16)])[0]
    x2 = _mm("w_out", merged, w_out, "nn", res=x1)
    x3, ffn2_saved = _ffn_fwd("ffn2", x2, sm["ffn2_norm"], wg["ffn2_gu"], w_d2)

    hn = _norm_fwd("ple_norm", x3, sm["ple_norm"])
    gz = _mm("ple_gate", hn, w_pg, "nn")
    pp = _mmc_nn("ple_proj", p, wg["br"], 3, 0, OUT_SHARD)

    def f_head(x3v, gzv, ppv, tg):
        sg = _sigmoid(gzv)
        err = x3v + sg * ppv - tg
        part = 0.5 * jnp.sum(jnp.mean(err * err, axis=-1, keepdims=True))
        dx4 = err * (1.0 / D_MODEL)
        return dx4, dx4 * ppv * sg * (1.0 - sg), dx4 * sg, jnp.full((1, LANES), part, F32)
    dx4, dgz, dpp, loss_row = _rowwise("ple_loss", f_head, [x3, gz, pp, target], [],
                                       [(D_MODEL, F32), (D_MODEL, BF16), (D_MODEL, BF16)], [(1, LANES)])
    loss = loss_row[0, 0]

    gs, gm = {}, {}
    row_blocks = lambda g: g.reshape(N_DEV, g.shape[0] // N_DEV, g.shape[1])
    dhn = _mm("ple_dhn", dgz, w_pg, "nt")
    gm["ple_gate"] = row_blocks(_mm("ple_dwgate", hn, dgz, "tn", out_dtype=BF16))
    gm["ple_proj"] = _mmc_tn("ple_dwproj", p, dpp, OUT_SHARD)
    dx3, gs["ple_norm"] = _norm_bwd("ple_dnorm", x3, sm["ple_norm"], dhn, dx4)

    dx2, gs["ffn2_norm"], gm["ffn2_gu"], gm["ffn2_d"] = _ffn_bwd(
        "ffn2", x2, sm["ffn2_norm"], wg["ffn2_gu"], w_d2, ffn2_saved, dx3)
    tail_token = on_grads("tail", {n: gm.pop(n) for n in SCATTER_GROUPS["tail"]})

    dmerged = _mm("w_out_dmerged", dx2, w_out, "nt")
    gm["w_out"] = row_blocks(_mm("w_out_dw", merged, dx2, "tn", out_dtype=BF16))

    def merge_bwd(zg, ur, ua, dm):
        _, vjp = jax.vjp(f_merge, zg[:, :D_MODEL], zg[:, D_MODEL:], ur, ua)
        dzr, dza, dur, dua = vjp(dm)
        return jnp.concatenate([dzr, dza], axis=1), dur, dua
    dz_g, du_r, du_a = _rowwise("merge_bwd", merge_bwd, [z_g, u_r, u_a, dmerged], [],
                                [(2 * D_MODEL, BF16), (D_MODEL, BF16), (D_MODEL, BF16)])
    dy_rwkv = _mmc_nt("br_rwkv_dy", du_r, wg["br"], 0, 0, OUT_SHARD, RWKV_DIM)
    dy_attn = _mmc_nt("br_attn_dy", du_a, wg["br"], 2, 0, OUT_SHARD, HEADS_PER_GROUP * HEAD_DIM)
    gm["br"] = jnp.concatenate([_mmc_tn("br_rwkv_dw", y_rwkv, du_r, OUT_SHARD),
                                _mmc_tn("br_attn_dw", y_attn, du_a, OUT_SHARD)], axis=1)

    def comb_bwd(ov, lv, dyv):
        _, vjp = jax.vjp(_f_comb, *by_group(ov, lv))
        d = vjp(dyv)
        return jnp.concatenate(d[:3], axis=1), jnp.concatenate(d[3:], axis=1)
    do_all, dl_all = _rowwise("attn_comb_bwd", comb_bwd, [o_all, lse_all, dy_attn], [],
                              [(ATTN_DIM, F32), (ATTN_DIM, F32)])
    dq_all, dk_all, dv_all = _attn_bwd(qk, z, do_all, dl_all)
    dqk_raw, gs["q_norm"], gs["k_norm"] = _qk_prep_bwd(z, tab, sm["q_norm"], sm["k_norm"], dq_all, dk_all)

    def post_bwd(yv, rv, k2v, vv, gv, dv_, gnw, gnb, rk):
        _, vjp = jax.vjp(_f_post, yv, rv, k2v, vv, gv, gnw, gnb, rk)
        return vjp(dv_)
    head_acc = (RWKV_HEADS, HEAD_DIM)
    dy, dr1, dk2a, dv1, dg, d_gnw, d_gnb, d_rk = _rowwise(
        "rwkv_post_bwd", post_bwd, post_rows + [as_heads(dy_rwkv)], [post_params[0] + tail_token] + post_params[1:],
        [(HEAD_DIM, F32)] * 5, [head_acc] * 3, tr=512)
    gs["rwkv_gn_w"], gs["rwkv_gn_b"], gs["rwkv_r_k"] = d_gnw, d_gnb, d_rk
    dr2, ddecay, dk2b, dkkn, db, dv_rows = _wkv_bwd(*scan_in, v_rows, states, _to_v_rows(dy.reshape(t, RWKV_DIM)))
    dr2, ddecay, dk2b, dkkn, db = [u.reshape(t, RWKV_DIM) for u in (dr2, ddecay, dk2b, dkkn, db)]
    dv2 = _from_v_rows(dv_rows)

    def kk_bwd(kkv, av, dkknv, dbv):
        _, vjp = jax.vjp(_f_kk, kkv, av)
        return vjp((dkknv, dbv))
    dkk, da = _rowwise("rwkv_kk_bwd", kk_bwd, [as_heads(kk), as_heads(a), as_heads(dkkn), as_heads(db)], [],
                       [(HEAD_DIM, F32)] * 2, tr=512)

    def pre_bwd(kv, lov, gdv, dk2x, dk2y, dkkv, dav, ddec, dgv, w0, w2p_, a0, a2p_, g2, k_k, k_a):
        _, vjp = jax.vjp(_f_pre, kv, lov, gdv, w0, w2p_, a0, a2p_, g2, k_k, k_a)
        return vjp((dk2x + dk2y, dkkv, dav, ddec, dgv))
    lora_acc = (DECAY_LORA + ICLR_LORA, RWKV_DIM)
    dk, dlo, dgd, d_w0, d_w2p, d_a0, d_a2p, d_g2, d_kk, d_ka = _rowwise(
        "rwkv_pre_bwd", pre_bwd,
        [k, lo, gd, dk2a.reshape(t, RWKV_DIM), dk2b, dkk.reshape(t, RWKV_DIM), da.reshape(t, RWKV_DIM), ddecay,
         dg.reshape(t, RWKV_DIM)],
        pre_params, [(RWKV_DIM, F32), (LANES, F32), (LANES, F32)],
        [(1, RWKV_DIM), lora_acc, (1, RWKV_DIM), lora_acc, (GATE_LORA, RWKV_DIM), (1, RWKV_DIM), (1, RWKV_DIM)])
    gs["rwkv_w0"], gs["rwkv_a0"], gs["rwkv_k_k"], gs["rwkv_k_a"] = d_w0, d_a0, d_kk, d_ka
    col_blocks = lambda g: g.reshape(g.shape[0], N_DEV, g.shape[1] // N_DEV).transpose(1, 0, 2)
    gm["lora"] = jnp.concatenate([col_blocks(d_w2p[:DECAY_LORA]), col_blocks(d_a2p[DECAY_LORA:]), col_blocks(d_g2)],
                                 axis=1).astype(BF16)
    add2 = lambda u, w_: _rowwise("rwkv_add", lambda p_, q_: (p_ + q_,), [u, w_], [], [(RWKV_DIM, F32)])[0]
    dz_r, gs["rwkv_mu"] = _shift_bwd(z_r, sm["rwkv_mu"], add2(dr1.reshape(t, RWKV_DIM), dr2), dk,
                                     add2(dv1.reshape(t, RWKV_DIM), dv2), dlo, dgd)

    dz = jnp.concatenate([dz_r, dqk_raw, dv_all.astype(BF16), dz_g], axis=1)
    dh2 = _mmc_nt("w_in_dh", dz, wg["w_in"], 0, 0, IN_SHARD, D_MODEL)
    gm["w_in"] = _mmc_tn("w_in_dw", h2, dz, IN_SHARD)
    mixer_token = on_grads("mixer", {n: gm.pop(n) for n in SCATTER_GROUPS["mixer"]})
    dx1, gs["mix_norm"] = _norm_bwd("mix_dnorm", x1, sm["mix_norm"] + mixer_token, dh2, dx2)

    dx0, gs["ffn1_norm"], gm["ffn1_gu"], _ = _ffn_bwd(
        "ffn1", x, sm["ffn1_norm"], wg["ffn1_gu"], w_d1, ffn1_saved, dx1,
        on_down=lambda blocks: on_grads("ffn1_down", {"ffn1_d": blocks}))
    return loss, dx0, gm, gs


def kernel(x, p, positions, ffn1_norm, ffn1_w_gate, ffn1_w_up, ffn1_w_down, mix_norm, w_in, rwkv_mu, rwkv_w0, rwkv_w2, rwkv_a0, rwkv_a2, rwkv_g2, rwkv_k_k, rwkv_k_a, rwkv_r_k, rwkv_gn_w, rwkv_gn_b, q_norm, k_norm, w_br_rwkv, w_br_attn, w_out, ffn2_norm, ffn2_w_gate, ffn2_w_up, ffn2_w_down, ple_norm, ple_w_gate, ple_w_proj, loss_target, m_ffn1_norm, m_ffn1_w_gate, m_ffn1_w_up, m_ffn1_w_down, m_mix_norm, m_w_in, m_rwkv_mu, m_rwkv_w0, m_rwkv_w2, m_rwkv_a0, m_rwkv_a2, m_rwkv_g2, m_rwkv_k_k, m_rwkv_k_a, m_rwkv_r_k, m_rwkv_gn_w, m_rwkv_gn_b, m_q_norm, m_k_norm, m_w_br_rwkv, m_w_br_attn, m_w_out, m_ffn2_norm, m_ffn2_w_gate, m_ffn2_w_up, m_ffn2_w_down, m_ple_norm, m_ple_w_gate, m_ple_w_proj, v_ffn1_norm, v_ffn1_w_gate, v_ffn1_w_up, v_ffn1_w_down, v_mix_norm, v_w_in, v_rwkv_mu, v_rwkv_w0, v_rwkv_w2, v_rwkv_a0, v_rwkv_a2, v_rwkv_g2, v_rwkv_k_k, v_rwkv_k_a, v_rwkv_r_k, v_rwkv_gn_w, v_rwkv_gn_b, v_q_norm, v_k_norm, v_w_br_rwkv, v_w_br_attn, v_w_out, v_ffn2_norm, v_ffn2_w_gate, v_ffn2_w_up, v_ffn2_w_down, v_ple_norm, v_ple_w_gate, v_ple_w_proj):
    args = locals()
    w = {n: args[n][0] for n in WEIGHTS}
    m = {n: args["m_" + n][0] for n in WEIGHTS}
    v = {n: args["v_" + n][0] for n in WEIGHTS}

    w_loc = _local_blocks(w)
    head = GATHER_GROUPS["head"]
    wg = dict(zip(head, _all_gather_hbm("gather_head", [w_loc[n] for n in head])))
    me = 4 * lax.axis_index("x") + 2 * lax.axis_index("y") + lax.axis_index("c")
    gathering, order_after = {}, wg[head[0]]
    for group in ("mid", "rest"):
        shards = [w_loc[n] for n in GATHER_GROUPS[group]]
        zones = [lax.dynamic_update_slice(lax.empty((N_DEV,) + a.shape, a.dtype), a[None], (me, 0, 0)) for a in shards]
        *gathering[group], order_after = _exchange_start("gather_start_" + group, _gather_copies, shards, zones,
                                                         order_after)

    def fetch(group, after):
        _, got = _exchange_wait("gather_wait_" + group, _gather_copies, *gathering[group], after)
        return dict(zip(GATHER_GROUPS[group], got))

    sm = {n: w[n].reshape(1, -1) for n in SMALL}
    sm["ffn1_norm"] = sm["ffn1_norm"] + order_after[0, 0]
    in_flight = {}

    def scatter_early(group, arrays):
        arrs = [arrays[n] for n in SCATTER_GROUPS[group]]
        *in_flight[group], token = _exchange_start("scatter_start_" + group, _scatter_copies, arrs,
                                                   [lax.empty(a.shape, a.dtype) for a in arrs], arrs[0])
        return token[0, 0]
    loss_part, dx, gm, gs = _local_step(x[0], p[0, 0], positions[0], loss_target[0], sm, wg, fetch, scatter_early)
    loss = lax.psum(loss_part, ("x", "y", "c"))
    recv, own = {}, {}
    for group, handles in in_flight.items():
        sent, lands = _exchange_wait("scatter_wait_" + group, _scatter_copies, *handles, gm[SCATTER_GROUPS["head"][0]])
        own.update(zip(SCATTER_GROUPS[group], sent))
        recv.update(zip(SCATTER_GROUPS[group], lands))
    head = _all_to_all_hbm("scatter_head", [gm[n] for n in SCATTER_GROUPS["head"]])
    recv.update(zip(SCATTER_GROUPS["head"], head))
    own.update(zip(SCATTER_GROUPS["head"], head))
    small_like = [w[n] for n in SMALL]
    small_rows = 80
    gs_all = _all_gather_vmem(_pack_small([gs[n] for n in SMALL], small_rows))
    gs_sum = _sum_slots("sum_small_grads", gs_all, small_rows)

    res = {}
    for n, src, rb, cb, cw, tr in ADAM_PLAN:
        outs4 = _adamw_slots("adamw_" + n, recv[src], own[src], rb, cb, cw, w[n], m[n], v[n], tr)
        for tag, a in zip(("grad", "delta", "new_m", "new_v"), outs4):
            res[tag, n] = a[None]
    d_s, m_s, v_s = _adamw("adamw_small", _pack_small(small_like, small_rows), gs_sum,
                           _pack_small([m[n] for n in SMALL], small_rows),
                           _pack_small([v[n] for n in SMALL], small_rows), small_rows)
    for tag, small in (("grad", gs_sum), ("delta", d_s), ("new_m", m_s), ("new_v", v_s)):
        for n, a in zip(SMALL, _unpack_small(small, small_like)):
            res[tag, n] = a[None]
    outs = [loss, dx[None]]
    for tag in ("grad", "delta", "new_m", "new_v"):
        outs += [res[tag, n] for n in WEIGHTS]
    return tuple(outs)
```

```python
import functools

import jax
import jax.numpy as jnp
from jax import lax
from jax.experimental import pallas as pl
from jax.experimental.pallas import tpu as pltpu

F32, BF16 = jnp.float32, jnp.bfloat16
MESH = pl.DeviceIdType.MESH
N_DEV = 8
LANES = 128
VMEM_LIMIT = 56 * 1024 * 1024

D_MODEL = 1024
PLE_DIM = 256
HEAD_DIM = 64
RWKV_HEADS = 8
RWKV_DIM = RWKV_HEADS * HEAD_DIM
DECAY_LORA = 64
ICLR_LORA = 64
GATE_LORA = 128
GN_EPS = 64e-5
ATTN_GROUPS = ((128, 1), (512, 4), (2048, 16))
HEADS_PER_GROUP = 4
ATTN_HEADS = HEADS_PER_GROUP * len(ATTN_GROUPS)
ATTN_DIM = ATTN_HEADS * HEAD_DIM
BAND_BLOCK = 128
ROPE_THETA = 10000.0
NEG_INF = -1e30
D_FF = 2816
RMS_EPS = 1e-6
RWKV_COLS = 3 * RWKV_DIM + DECAY_LORA + ICLR_LORA + GATE_LORA
ADAM_LR = 0.001
ADAM_B1 = 0.9
ADAM_B2 = 0.999
ADAM_EPS = 1e-08
ADAM_WD = 0.01
ADAM_STEP = 10

V_LO = LANES // RWKV_HEADS
V_HI = HEAD_DIM // V_LO
SCAN_CHUNK = 64

FF_SHARD = D_FF // N_DEV
FF_PAD = -(-FF_SHARD // LANES) * LANES
FF_HID = N_DEV * FF_PAD
IN_SHARD = 6144 // N_DEV
OUT_SHARD = D_MODEL // N_DEV

SMALL = ("ffn1_norm", "mix_norm", "rwkv_mu", "rwkv_w0", "rwkv_a0", "rwkv_k_k", "rwkv_k_a", "rwkv_r_k",
         "rwkv_gn_w", "rwkv_gn_b", "q_norm", "k_norm", "ffn2_norm", "ple_norm")
WEIGHTS = ("ffn1_norm", "ffn1_w_gate", "ffn1_w_up", "ffn1_w_down", "mix_norm", "w_in", "rwkv_mu", "rwkv_w0",
           "rwkv_w2", "rwkv_a0", "rwkv_a2", "rwkv_g2", "rwkv_k_k", "rwkv_k_a", "rwkv_r_k", "rwkv_gn_w",
           "rwkv_gn_b", "q_norm", "k_norm", "w_br_rwkv", "w_br_attn", "w_out", "ffn2_norm", "ffn2_w_gate",
           "ffn2_w_up", "ffn2_w_down", "ple_norm", "ple_w_gate", "ple_w_proj")


def _cparams(**kw):
    return pltpu.CompilerParams(vmem_limit_bytes=VMEM_LIMIT, **kw)


def _tile(n, cap):
    best = None
    for t in range(LANES, min(n, cap) + 1, LANES):
        if n % t == 0:
            best = t
    return best if best is not None else n


@jax.custom_vjp
def _bdot(a, w):
    return jnp.dot(a.astype(BF16), w.astype(BF16), preferred_element_type=F32)


def _bdot_fwd(a, w):
    return _bdot(a, w), (a, w)


def _bdot_bwd(res, g):
    a, w = res
    gb = g.astype(BF16)
    da = lax.dot_general(gb, w.astype(BF16), (((1,), (1,)), ((), ())), preferred_element_type=F32)
    dw = lax.dot_general(a.astype(BF16), gb, (((0,), (0,)), ((), ())), preferred_element_type=F32)
    return da.astype(a.dtype), dw.astype(w.dtype)


_bdot.defvjp(_bdot_fwd, _bdot_bwd)


@jax.custom_vjp
def _bdot_nt(a, b):
    return lax.dot_general(a.astype(BF16), b.astype(BF16), (((1,), (1,)), ((), ())), preferred_element_type=F32)


def _bdot_nt_fwd(a, b):
    return _bdot_nt(a, b), (a, b)


def _bdot_nt_bwd(res, g):
    a, b = res
    gb = g.astype(BF16)
    da = jnp.dot(gb, b.astype(BF16), preferred_element_type=F32)
    db = lax.dot_general(gb, a.astype(BF16), (((0,), (0,)), ((), ())), preferred_element_type=F32)
    return da.astype(a.dtype), db.astype(b.dtype)


_bdot_nt.defvjp(_bdot_nt_fwd, _bdot_nt_bwd)


def _mm(name, a, b, mode, out_dtype=F32, res=None, scale=None):
    if mode == "nn":
        (m, k), n = a.shape, b.shape[1]
    elif mode == "nt":
        (m, k), n = a.shape, b.shape[0]
    else:
        (k, m), n = a.shape, b.shape[1]
    tm, tn = _tile(m, 512), _tile(n, 512)
    a_spec = pl.BlockSpec((k, tm), lambda i, j: (0, i)) if mode == "tn" else pl.BlockSpec((tm, k), lambda i, j: (i, 0))
    b_spec = pl.BlockSpec((tn, k), lambda i, j: (j, 0)) if mode == "nt" else pl.BlockSpec((k, tn), lambda i, j: (0, j))
    dims = {"nn": ((1,), (0,)), "nt": ((1,), (1,)), "tn": ((0,), (0,))}[mode]
    o_spec = pl.BlockSpec((tm, tn), lambda i, j: (i, j))
    ins, in_specs = [a, b], [a_spec, b_spec]
    if res is not None:
        ins.append(res)
        in_specs.append(o_spec)

    def body(*refs):
        acc = lax.dot_general(refs[0][...].astype(BF16), refs[1][...].astype(BF16), (dims, ((), ())),
                              preferred_element_type=F32)
        if scale is not None:
            acc = acc * scale
        if res is not None:
            acc = acc + refs[2][...].astype(F32)
        refs[-1][...] = acc.astype(refs[-1].dtype)

    return pl.pallas_call(
        body, name=name, grid=(m // tm, n // tn), in_specs=in_specs, out_specs=o_spec,
        out_shape=jax.ShapeDtypeStruct((m, n), out_dtype),
        compiler_params=_cparams(dimension_semantics=("parallel", "parallel")),
    )(*ins)


def _mmc_nn(name, a, wb, ki, ci, n, out_dtype=F32):
    m, k = a.shape
    tm = _tile(m, 512)

    def body(a_ref, w_ref, o_ref):
        o_ref[...] = jnp.dot(a_ref[...].astype(BF16), w_ref[...], preferred_element_type=F32).astype(o_ref.dtype)

    return pl.pallas_call(
        body, name=name, grid=(m // tm, N_DEV),
        in_specs=[pl.BlockSpec((tm, k), lambda i, j: (i, 0)), pl.BlockSpec((None, k, n), lambda i, j: (j, ki, ci))],
        out_specs=pl.BlockSpec((tm, n), lambda i, j: (i, j)),
        out_shape=jax.ShapeDtypeStruct((m, N_DEV * n), out_dtype),
        compiler_params=_cparams(dimension_semantics=("parallel", "parallel")),
    )(a, wb)


def _mmc_nt(name, a, wb, ki, ci, n, k, res=None):
    m = a.shape[0]
    tm = _tile(m, 512)
    o_spec = pl.BlockSpec((tm, k), lambda i, j: (i, 0))
    ins = [a, wb] + ([res] if res is not None else [])
    in_specs = [pl.BlockSpec((tm, n), lambda i, j: (i, j)), pl.BlockSpec((None, k, n), lambda i, j: (j, ki, ci))]
    in_specs += [o_spec] if res is not None else []

    def body(*refs):
        a_ref, w_ref, o_ref = refs[0], refs[1], refs[-1]
        acc = lax.dot_general(a_ref[...].astype(BF16), w_ref[...], (((1,), (1,)), ((), ())),
                              preferred_element_type=F32)

        @pl.when(pl.program_id(1) == 0)
        def _():
            o_ref[...] = acc + refs[2][...] if res is not None else acc

        @pl.when(pl.program_id(1) != 0)
        def _():
            o_ref[...] += acc

    return pl.pallas_call(
        body, name=name, grid=(m // tm, N_DEV), in_specs=in_specs, out_specs=o_spec,
        out_shape=jax.ShapeDtypeStruct((m, k), F32),
        compiler_params=_cparams(dimension_semantics=("parallel", "arbitrary")),
    )(*ins)


def _mmc_tn(name, x, dy, n):
    m, k = x.shape
    tk = _tile(k, 512)

    def body(x_ref, dy_ref, o_ref):
        o_ref[...] = lax.dot_general(x_ref[...].astype(BF16), dy_ref[...].astype(BF16), (((0,), (0,)), ((), ())),
                                     preferred_element_type=F32).astype(o_ref.dtype)

    return pl.pallas_call(
        body, name=name, grid=(N_DEV, k // tk),
        in_specs=[pl.BlockSpec((m, tk), lambda j, i: (0, i)), pl.BlockSpec((m, n), lambda j, i: (0, j))],
        out_specs=pl.BlockSpec((None, tk, n), lambda j, i: (j, i, 0)),
        out_shape=jax.ShapeDtypeStruct((N_DEV, k, n), BF16),
        compiler_params=_cparams(dimension_semantics=("parallel", "parallel")),
    )(x, dy)


def _rowwise(name, fn, rows, params, out_rows, out_accs=(), tr=256):
    r = rows[0].shape[0]
    in_specs = [pl.BlockSpec((tr, a.shape[1]), lambda i: (i, 0)) for a in rows]
    in_specs += [pl.BlockSpec(p.shape, lambda i, nd=p.ndim: (0,) * nd) for p in params]
    out_shape = [jax.ShapeDtypeStruct((r, c), dt) for c, dt in out_rows]
    out_shape += [jax.ShapeDtypeStruct(s, F32) for s in out_accs]
    out_specs = [pl.BlockSpec((tr, c), lambda i: (i, 0)) for c, _ in out_rows]
    out_specs += [pl.BlockSpec(s, lambda i, nd=len(s): (0,) * nd) for s in out_accs]
    n_in, n_ro = len(rows) + len(params), len(out_rows)

    def body(*refs):
        res = fn(*[ref[...] for ref in refs[:n_in]])
        outs = refs[n_in:]
        for o, v in zip(outs[:n_ro], res[:n_ro]):
            o[...] = v.astype(o.dtype)
        for o, v in zip(outs[n_ro:], res[n_ro:]):
            _accumulate(o, v)

    return pl.pallas_call(
        body, name=name, grid=(r // tr,), in_specs=in_specs, out_specs=out_specs, out_shape=out_shape,
        compiler_params=_cparams(dimension_semantics=("arbitrary",)),
    )(*rows, *params)


def _accumulate(o_ref, v):
    @pl.when(pl.program_id(0) == 0)
    def _():
        o_ref[...] = v

    @pl.when(pl.program_id(0) != 0)
    def _():
        o_ref[...] += v


def _rms(x, g):
    return x * lax.rsqrt(jnp.mean(x * x, axis=-1, keepdims=True) + RMS_EPS) * g


def _sigmoid(x):
    return jax.nn.sigmoid(x)


def _softplus(x):
    return jnp.maximum(x, 0.0) + jnp.log1p(jnp.exp(-jnp.abs(x)))


def _norm_fwd(name, x, g):
    return _rowwise(name, lambda xv, gv: (_rms(xv, gv),), [x], [g], [(x.shape[1], BF16)])[0]


def _norm_bwd(name, x, g, dh, dres):
    def fn(xv, dhv, drv, gv):
        _, vjp = jax.vjp(_rms, xv, gv)
        dx, dg = vjp(dhv)
        return dx + drv, dg
    return _rowwise(name, fn, [x, dh, dres], [g], [(x.shape[1], F32)], [g.shape])


def _f_act(gate, up):
    return gate * _sigmoid(gate) * up


def _gate_up(guv, j):
    base = j * 2 * FF_PAD
    return guv[:, base:base + FF_PAD], guv[:, base + FF_PAD:base + 2 * FF_PAD]


def _ffn_fwd(tag, x, norm, w_gu, w_down):
    h = _norm_fwd(tag + "_norm", x, norm)
    gu = _mmc_nn(tag + "_gu", h, w_gu, 0, 0, 2 * FF_PAD)

    def act(guv):
        return (jnp.concatenate([_f_act(*_gate_up(guv, j)) for j in range(N_DEV)], axis=1),)
    a = _rowwise(tag + "_act", act, [gu], [], [(FF_HID, BF16)])[0]
    out = _mm(tag + "_down", a, w_down, "nn", res=x, scale=0.5)
    return out, (h, gu, a)


def _ffn_bwd(tag, x, norm, w_gu, w_down, saved, dout, on_down=None):
    h, gu, a = saved
    d_wdown = _mm(tag + "_dwdown", a, dout, "tn", out_dtype=BF16, scale=0.5).reshape(N_DEV, FF_PAD, D_MODEL)
    token = jnp.zeros((1, LANES), F32) + (on_down(d_wdown) if on_down is not None else 0.0)
    da = _mm(tag + "_dact", dout, w_down, "nt", scale=0.5)

    def act_bwd(guv, dav, _):
        outs = []
        for j in range(N_DEV):
            _, vjp = jax.vjp(_f_act, *_gate_up(guv, j))
            outs += list(vjp(dav[:, j * FF_PAD:(j + 1) * FF_PAD]))
        return (jnp.concatenate(outs, axis=1),)
    dgu = _rowwise(tag + "_dgu", act_bwd, [gu, da], [token], [(2 * FF_HID, BF16)])[0]
    dh = _mmc_nt(tag + "_dh", dgu, w_gu, 0, 0, 2 * FF_PAD, D_MODEL)
    d_wgu = _mmc_tn(tag + "_dwgu", h, dgu, 2 * FF_PAD)
    dx, dnorm = _norm_bwd(tag + "_dnorm", x, norm, dh, dout)
    return dx, dnorm, d_wgu, d_wdown


def _shift_fwd(z, mu):
    t, c = z.shape
    tr = 256

    def body(z_ref, zp_ref, mu_ref, r_ref, k_ref, v_ref, lo_ref, gd_ref):
        zv = z_ref[...]
        prev = zp_ref[7:8, :] * jnp.where(pl.program_id(0) == 0, 0.0, 1.0)
        row = lax.broadcasted_iota(jnp.int32, zv.shape, 0)
        zsh = jnp.where(row == 0, prev, pltpu.roll(zv, 1, 0))
        zs = zv + (zsh - zv) * mu_ref[...]
        r_ref[...] = zs[:, 0:512]
        k_ref[...] = zs[:, 512:1024]
        v_ref[...] = zs[:, 1024:1536]
        lo_ref[...] = zs[:, 1536:1664]
        gd_ref[...] = zs[:, 1664:1792]

    widths = (512, 512, 512, 128, 128)
    return pl.pallas_call(
        body, name="rwkv_shift", grid=(t // tr,),
        in_specs=[pl.BlockSpec((tr, c), lambda i: (i, 0)),
                  pl.BlockSpec((8, c), lambda i: (jnp.maximum(i * (tr // 8) - 1, 0), 0)),
                  pl.BlockSpec((1, c), lambda i: (0, 0))],
        out_specs=[pl.BlockSpec((tr, w), lambda i: (i, 0)) for w in widths],
        out_shape=[jax.ShapeDtypeStruct((t, w), F32) for w in widths],
        compiler_params=_cparams(dimension_semantics=("parallel",)),
    )(z, z, mu)


def _shift_bwd(z, mu, dr, dk, dv, dlo, dgd):
    t, c = z.shape
    tr = 256
    nt = t // tr

    def body(z_ref, zp_ref, mu_ref, dr_ref, dk_ref, dv_ref, dlo_ref, dgd_ref,
             drn_ref, dkn_ref, dvn_ref, dlon_ref, dgdn_ref, dz_ref, dmu_ref):
        i = pl.program_id(0)
        zv, muv = z_ref[...], mu_ref[...]
        prev = zp_ref[7:8, :] * jnp.where(i == 0, 0.0, 1.0)
        row = lax.broadcasted_iota(jnp.int32, zv.shape, 0)
        zsh = jnp.where(row == 0, prev, pltpu.roll(zv, 1, 0))
        dzs = jnp.concatenate([dr_ref[...], dk_ref[...], dv_ref[...], dlo_ref[...], dgd_ref[...]], axis=1)
        nxt = jnp.concatenate([drn_ref[0:1, :], dkn_ref[0:1, :], dvn_ref[0:1, :], dlon_ref[0:1, :],
                               dgdn_ref[0:1, :]], axis=1) * jnp.where(i == nt - 1, 0.0, 1.0)
        u = dzs * muv
        un = jnp.where(row == tr - 1, nxt * muv, pltpu.roll(u, tr - 1, 0))
        dz_ref[...] = (dzs - u + un).astype(dz_ref.dtype)
        _accumulate(dmu_ref, jnp.sum(dzs * (zsh - zv), axis=0, keepdims=True))

    widths = (512, 512, 512, 128, 128)
    nxt_map = lambda i: (jnp.minimum((i + 1) * (tr // 8), t // 8 - 1), 0)
    return pl.pallas_call(
        body, name="rwkv_shift_bwd", grid=(nt,),
        in_specs=[pl.BlockSpec((tr, c), lambda i: (i, 0)),
                  pl.BlockSpec((8, c), lambda i: (jnp.maximum(i * (tr // 8) - 1, 0), 0)),
                  pl.BlockSpec((1, c), lambda i: (0, 0))]
        + [pl.BlockSpec((tr, w), lambda i: (i, 0)) for w in widths]
        + [pl.BlockSpec((8, w), nxt_map) for w in widths],
        out_specs=[pl.BlockSpec((tr, c), lambda i: (i, 0)), pl.BlockSpec((1, c), lambda i: (0, 0))],
        out_shape=[jax.ShapeDtypeStruct((t, c), BF16), jax.ShapeDtypeStruct((1, c), F32)],
        compiler_params=_cparams(dimension_semantics=("arbitrary",)),
    )(z, z, mu, dr, dk, dv, dlo, dgd, dr, dk, dv, dlo, dgd)


def _f_pre(k, lo, gd, w0, w2p, a0, a2p, g2, k_k, k_a):
    lane = lax.broadcasted_iota(jnp.int32, lo.shape, 1)
    lo_act = jnp.where(lane < DECAY_LORA, jnp.tanh(lo), lo)
    w = -_softplus(-(w0 + _bdot(lo_act, w2p))) - 0.5
    a = _sigmoid(a0 + _bdot(lo_act, a2p))
    g = _bdot(_sigmoid(gd), g2)
    kk = k * k_k
    k2 = k * (1.0 + (a - 1.0) * k_a)
    decay = jnp.exp(-jnp.exp(w))
    return k2, kk, a, decay, g


def _f_kk(kk, a):
    kkn = kk * lax.rsqrt(jnp.maximum(jnp.sum(kk * kk, axis=-1, keepdims=True), 1e-24))
    return kkn, kkn * a


def _f_post(y, r, k2, v, g, gn_w, gn_b, r_k):
    reps = y.shape[0] // RWKV_HEADS
    mean = jnp.mean(y, axis=-1, keepdims=True)
    var = jnp.mean(jnp.square(y - mean), axis=-1, keepdims=True)
    yn = (y - mean) * lax.rsqrt(var + GN_EPS)
    yn = yn * jnp.tile(gn_w, (reps, 1)) + jnp.tile(gn_b, (reps, 1))
    bonus = jnp.sum(r * k2 * jnp.tile(r_k, (reps, 1)), axis=-1, keepdims=True) * v
    return (yn + bonus) * g


def _to_v_rows(x):
    t = x.shape[0]
    return x.reshape(t, RWKV_HEADS, V_HI, V_LO).transpose(0, 2, 3, 1).reshape(t, V_HI, LANES)


def _from_v_rows(x):
    t = x.shape[0]
    return x.reshape(t, V_HI, V_LO, RWKV_HEADS).transpose(0, 3, 1, 2).reshape(t, RWKV_DIM)


def _k_cols(x):
    return jnp.tile(x, (V_LO, 1)).T


def _k_rows(x):
    xt = x.T
    out = xt[0:RWKV_HEADS]
    for l in range(1, V_LO):
        out = out + xt[l * RWKV_HEADS:(l + 1) * RWKV_HEADS]
    return out


def _wkv_fwd(r, w, k, kk, b, v):
    t = r.shape[0]
    tc = SCAN_CHUNK
    key_spec = pl.BlockSpec((tc, RWKV_HEADS, HEAD_DIM), lambda i: (i, 0, 0))
    row_spec = pl.BlockSpec((tc, V_HI, LANES), lambda i: (i, 0, 0))

    def body(r_ref, w_ref, k_ref, kk_ref, b_ref, v_ref, y_ref, st_ref, s_scr, cols_a, cols_b):
        @pl.when(pl.program_id(0) == 0)
        def _():
            s_scr[...] = jnp.zeros_like(s_scr)

        def prep(ti, buf):
            for n, ref in enumerate((r_ref, w_ref, k_ref, kk_ref, b_ref)):
                buf[n] = _k_cols(ref[ti])

        def step(ti, s, cur, nxt, ti_next):
            rc, wc, kc, kkc, bc = (cur[n] for n in range(5))
            prep(ti_next, nxt)
            vt = v_ref[ti]
            new, ys = [], []
            for j in range(V_HI):
                sa = -jnp.sum(s[j] * kkc, axis=0, keepdims=True)
                nj = s[j] * wc + bc * sa + kc * vt[j:j + 1]
                st_ref[ti, j] = nj
                ys.append(jnp.sum(nj * rc, axis=0, keepdims=True))
                new.append(nj)
            y_ref[ti] = jnp.concatenate(ys, axis=0)
            return tuple(new)

        def pair(i, s):
            s = step(2 * i, s, cols_a, cols_b, 2 * i + 1)
            return step(2 * i + 1, s, cols_b, cols_a, jnp.minimum(2 * i + 2, tc - 1))

        prep(0, cols_a)
        s = lax.fori_loop(0, tc // 2, pair, tuple(s_scr[j] for j in range(V_HI)))
        for j in range(V_HI):
            s_scr[j] = s[j]

    return pl.pallas_call(
        body, name="wkv_fwd", grid=(t // tc,),
        in_specs=[key_spec] * 5 + [row_spec],
        out_specs=[row_spec, pl.BlockSpec((tc, V_HI, HEAD_DIM, LANES), lambda i: (i, 0, 0, 0))],
        out_shape=[jax.ShapeDtypeStruct((t, V_HI, LANES), F32),
                   jax.ShapeDtypeStruct((t, V_HI, HEAD_DIM, LANES), F32)],
        scratch_shapes=[pltpu.VMEM((V_HI, HEAD_DIM, LANES), F32)] + [pltpu.VMEM((5, HEAD_DIM, LANES), F32)] * 2,
        compiler_params=_cparams(dimension_semantics=("arbitrary",)),
    )(r, w, k, kk, b, v)


def _wkv_bwd(r, w, k, kk, b, v, states, dy):
    t = r.shape[0]
    tc = SCAN_CHUNK
    nb = t // tc
    key_spec = pl.BlockSpec((tc, RWKV_HEADS, HEAD_DIM), lambda i: (nb - 1 - i, 0, 0))
    row_spec = pl.BlockSpec((tc, V_HI, LANES), lambda i: (nb - 1 - i, 0, 0))
    st_spec = pl.BlockSpec((tc, V_HI, HEAD_DIM, LANES), lambda i: (nb - 1 - i, 0, 0, 0))
    stp_spec = pl.BlockSpec((1, V_HI, HEAD_DIM, LANES), lambda i: (jnp.maximum((nb - 1 - i) * tc - 1, 0), 0, 0, 0))

    def body(r_ref, w_ref, k_ref, kk_ref, b_ref, v_ref, st_ref, stp_ref, dy_ref,
             dr_ref, dw_ref, dk_ref, dkk_ref, db_ref, dv_ref, ds_scr, cols_a, cols_b, accs_a, accs_b):
        @pl.when(pl.program_id(0) == 0)
        def _():
            ds_scr[...] = jnp.zeros_like(ds_scr)

        def colsum(x):
            return jnp.sum(x, axis=0, keepdims=True)

        def prep(ti, buf):
            for n, ref in enumerate((r_ref, w_ref, k_ref, kk_ref, b_ref)):
                buf[n] = _k_cols(ref[ti])

        def flush(ti, buf):
            for n, ref in enumerate((dr_ref, dk_ref, db_ref, dw_ref, dkk_ref)):
                ref[ti] = _k_rows(buf[n])

        def step(ti, ds, sp, cur, accs):
            rc, wc, kc, kkc, bc = (cur[n] for n in range(5))
            vt, dyt = v_ref[ti], dy_ref[ti]
            acc = None
            new, dvs = [], []
            for j in range(V_HI):
                st = st_ref[ti, j]
                dsj = ds[j] + rc * dyt[j:j + 1]
                sa = -colsum(sp[j] * kkc)
                dsa = colsum(dsj * bc)
                dvs.append(colsum(dsj * kc))
                parts = (st * dyt[j:j + 1], dsj * vt[j:j + 1], dsj * sa, dsj * sp[j], -(sp[j] * dsa))
                acc = parts if acc is None else tuple(a + q for a, q in zip(acc, parts))
                new.append(dsj * wc - kkc * dsa)
            dv_ref[ti] = jnp.concatenate(dvs, axis=0)
            for n in range(5):
                accs[n] = acc[n]
            return tuple(new)

        def states_before(ti):
            return tuple(st_ref[ti - 1, j] for j in range(V_HI))

        def pair(i, ds):
            ta = tc - 1 - 2 * i
            prep(ta - 1, cols_b)
            flush(jnp.minimum(ta + 1, tc - 1), accs_b)
            ds = step(ta, ds, states_before(ta), cols_a, accs_a)
            prep(ta - 2, cols_a)
            flush(ta, accs_a)
            return step(ta - 1, ds, states_before(ta - 1), cols_b, accs_b)

        prep(tc - 1, cols_a)
        accs_b[...] = jnp.zeros_like(accs_b)
        ds = lax.fori_loop(0, tc // 2 - 1, pair, tuple(ds_scr[j] for j in range(V_HI)))
        prep(0, cols_b)
        flush(2, accs_b)
        ds = step(1, ds, states_before(1), cols_a, accs_a)
        flush(1, accs_a)
        keep = jnp.where(pl.program_id(0) == nb - 1, 0.0, 1.0)
        ds = step(0, ds, tuple(stp_ref[0, j] * keep for j in range(V_HI)), cols_b, accs_b)
        flush(0, accs_b)
        for j in range(V_HI):
            ds_scr[j] = ds[j]

    key_out = jax.ShapeDtypeStruct((t, RWKV_HEADS, HEAD_DIM), F32)
    return pl.pallas_call(
        body, name="wkv_bwd", grid=(nb,),
        in_specs=[key_spec] * 5 + [row_spec, st_spec, stp_spec, row_spec],
        out_specs=[key_spec] * 5 + [row_spec],
        out_shape=[key_out] * 5 + [jax.ShapeDtypeStruct((t, V_HI, LANES), F32)],
        scratch_shapes=[pltpu.VMEM((V_HI, HEAD_DIM, LANES), F32)] + [pltpu.VMEM((5, HEAD_DIM, LANES), F32)] * 4,
        compiler_params=_cparams(dimension_semantics=("arbitrary",)),
    )(r, w, k, kk, b, v, states, states, dy)


PAIR = 2 * HEAD_DIM
N_PAIRS = ATTN_HEADS // 2
Q_COL0 = RWKV_COLS // PAIR
K_COL0 = Q_COL0 + N_PAIRS
V_COL0 = K_COL0 + N_PAIRS


def _swap_halves(x):
    lane = lax.broadcasted_iota(jnp.int32, x.shape, 1)
    return jnp.where((lane & (HEAD_DIM - 1)) < HEAD_DIM // 2, pltpu.roll(x, PAIR - HEAD_DIM // 2, 1),
                     pltpu.roll(x, HEAD_DIM // 2, 1))


@jax.custom_vjp
def _rope(x, cosf, sinf):
    return x * cosf + _swap_halves(x) * sinf


def _rope_fwd(x, cosf, sinf):
    return _rope(x, cosf, sinf), (cosf, sinf)


def _rope_bwd(res, d):
    cosf, sinf = res
    return d * cosf + _swap_halves(d * sinf), jnp.zeros_like(cosf), jnp.zeros_like(sinf)


_rope.defvjp(_rope_fwd, _rope_bwd)


def _head_sums(x):
    lane = lax.broadcasted_iota(jnp.int32, x.shape, 1)
    lo = jnp.where(lane < HEAD_DIM, 1.0, 0.0)
    hi = 1.0 - lo
    return lo * jnp.sum(x * lo, axis=1, keepdims=True) + hi * jnp.sum(x * hi, axis=1, keepdims=True)


def _f_qk(x, cosf, sinf, gain2):
    xn = x * lax.rsqrt(_head_sums(x * x) * (1.0 / HEAD_DIM) + RMS_EPS) * gain2
    return _rope(xn, cosf, sinf)


def _qk_prep(z, tab, q_gain, k_gain):
    t = z.shape[0]
    tr = 256

    def body(z_ref, c_ref, s_ref, qg_ref, kg_ref, o_ref):
        g = jnp.where(pl.program_id(0) < N_PAIRS, qg_ref[...], kg_ref[...])
        o_ref[...] = _f_qk(z_ref[...], c_ref[...], s_ref[...], jnp.concatenate([g, g], axis=1))

    gain = pl.BlockSpec((1, HEAD_DIM), lambda c, i: (0, 0))
    return pl.pallas_call(
        body, name="qk_prep", grid=(2 * N_PAIRS, t // tr),
        in_specs=[pl.BlockSpec((tr, PAIR), lambda c, i: (i, Q_COL0 + c)), pl.BlockSpec((tr, PAIR), lambda c, i: (i, 0)),
                  pl.BlockSpec((tr, PAIR), lambda c, i: (i, 1)), gain, gain],
        out_specs=pl.BlockSpec((tr, PAIR), lambda c, i: (i, c)),
        out_shape=jax.ShapeDtypeStruct((t, 2 * N_PAIRS * PAIR), F32),
        compiler_params=_cparams(dimension_semantics=("parallel", "parallel")),
    )(z, tab, tab, q_gain, k_gain)


def _qk_prep_bwd(z, tab, q_gain, k_gain, dq, dk):
    t = z.shape[0]
    tr = 256

    def body(z_ref, c_ref, s_ref, qg_ref, kg_ref, dq_ref, dk_ref, dz_ref, dqg_ref, dkg_ref):
        c, i = pl.program_id(0), pl.program_id(1)
        is_q = c < N_PAIRS
        g = jnp.where(is_q, qg_ref[...], kg_ref[...])
        d = jnp.where(is_q, dq_ref[...], dk_ref[...])
        _, vjp = jax.vjp(lambda xx, gg: _f_qk(xx, c_ref[...], s_ref[...], gg), z_ref[...],
                         jnp.concatenate([g, g], axis=1))
        dx, dg2 = vjp(d)
        dz_ref[...] = dx.astype(dz_ref.dtype)
        dg = dg2[:, :HEAD_DIM] + dg2[:, HEAD_DIM:]
        first_q = jnp.logical_and(c == 0, i == 0)
        first_k = jnp.logical_and(c == N_PAIRS, i == 0)

        @pl.when(first_q)
        def _():
            dqg_ref[...] = dg

        @pl.when(jnp.logical_and(is_q, jnp.logical_not(first_q)))
        def _():
            dqg_ref[...] += dg

        @pl.when(first_k)
        def _():
            dkg_ref[...] = dg

        @pl.when(jnp.logical_and(jnp.logical_not(is_q), jnp.logical_not(first_k)))
        def _():
            dkg_ref[...] += dg

    gain = pl.BlockSpec((1, HEAD_DIM), lambda c, i: (0, 0))
    return pl.pallas_call(
        body, name="qk_prep_bwd", grid=(2 * N_PAIRS, t // tr),
        in_specs=[pl.BlockSpec((tr, PAIR), lambda c, i: (i, Q_COL0 + c)), pl.BlockSpec((tr, PAIR), lambda c, i: (i, 0)),
                  pl.BlockSpec((tr, PAIR), lambda c, i: (i, 1)), gain, gain,
                  pl.BlockSpec((tr, PAIR), lambda c, i: (i, jnp.minimum(c, N_PAIRS - 1))),
                  pl.BlockSpec((tr, PAIR), lambda c, i: (i, jnp.maximum(c - N_PAIRS, 0)))],
        out_specs=[pl.BlockSpec((tr, PAIR), lambda c, i: (i, c)), gain, gain],
        out_shape=[jax.ShapeDtypeStruct((t, 2 * N_PAIRS * PAIR), BF16), jax.ShapeDtypeStruct((1, HEAD_DIM), F32),
                   jax.ShapeDtypeStruct((1, HEAD_DIM), F32)],
        compiler_params=_cparams(dimension_semantics=("arbitrary", "arbitrary")),
    )(z, tab, tab, q_gain, k_gain, dq, dk)


def _attn_block(q, kp, kc, vp, vc, kmin):
    k2 = jnp.concatenate([kp, kc], axis=0)
    v2 = jnp.concatenate([vp, vc], axis=0)
    s = _bdot_nt(q, k2) * (HEAD_DIM ** -0.5)
    qi = lax.broadcasted_iota(jnp.int32, s.shape, 0)
    kj = lax.broadcasted_iota(jnp.int32, s.shape, 1)
    dist = qi + BAND_BLOCK - kj
    valid = (dist >= 0) & (dist <= BAND_BLOCK) & (kj >= kmin)
    s = jnp.where(valid, s, NEG_INF)
    m = lax.stop_gradient(jnp.max(s, axis=-1, keepdims=True))
    e = jnp.exp(s - m)
    l = jnp.sum(e, axis=-1, keepdims=True)
    o = _bdot(e, v2) / l
    return o, m + jnp.log(l)


def _fold(src_ref, dst_ref, dil):
    t = src_ref.shape[0]
    ln = t // dil
    for j in range(dil):
        dst_ref[j * ln:(j + 1) * ln, :] = src_ref[pl.ds(j, ln, stride=dil), :]


def _unfold(src_ref, dst_ref, dil):
    t = src_ref.shape[0]
    ln = t // dil
    for j in range(dil):
        dst_ref[pl.ds(j, ln, stride=dil), :] = src_ref[j * ln:(j + 1) * ln, :]


def _per_group(fn):
    pair = pl.program_id(0)
    for gi, (_, dil) in enumerate(ATTN_GROUPS):
        @pl.when(jnp.logical_or(pair == 2 * gi, pair == 2 * gi + 1))
        def _(dil=dil):
            fn(dil)


def _block_rows(idx, blocks_per_seq):
    first = (idx & (blocks_per_seq - 1)) == 0
    cur = pl.ds(pl.multiple_of(idx * BAND_BLOCK, BAND_BLOCK), BAND_BLOCK)
    prev = pl.ds(pl.multiple_of(jnp.maximum(idx - 1, 0) * BAND_BLOCK, BAND_BLOCK), BAND_BLOCK)
    return first, cur, prev


def _heads(x):
    return x[:, :HEAD_DIM], x[:, HEAD_DIM:]


def _attn_fwd(qk, z):
    t = z.shape[0]
    n_blocks = t // BAND_BLOCK

    def body(q_ref, k_ref, v_ref, o_ref, lse_ref, qf, kf, vf, of, lf):
        def run(dil):
            _fold(q_ref, qf, dil)
            _fold(k_ref, kf, dil)
            _fold(v_ref, vf, dil)
            blocks_per_seq = n_blocks // dil

            def block(idx, carry):
                first, cur, prev = _block_rows(idx, blocks_per_seq)
                kmin = jnp.where(first, BAND_BLOCK, 0)
                outs, lses = [], []
                for q, kp, kc, vp, vc in zip(_heads(qf[cur, :]), _heads(kf[prev, :]), _heads(kf[cur, :]),
                                             _heads(vf[prev, :]), _heads(vf[cur, :])):
                    o, ls = _attn_block(q, kp, kc, vp, vc, kmin)
                    outs.append(o)
                    lses.append(jnp.broadcast_to(ls, o.shape))
                of[cur, :] = jnp.concatenate(outs, axis=1)
                lf[cur, :] = jnp.concatenate(lses, axis=1)
                return carry

            lax.fori_loop(0, n_blocks, block, 0)
            _unfold(of, o_ref, dil)
            _unfold(lf, lse_ref, dil)

        _per_group(run)

    slab = jax.ShapeDtypeStruct((t, N_PAIRS * PAIR), F32)
    out_spec = pl.BlockSpec((t, PAIR), lambda p: (0, p))
    return pl.pallas_call(
        body, name="attn_fwd", grid=(N_PAIRS,),
        in_specs=[pl.BlockSpec((t, PAIR), lambda p: (0, p)), pl.BlockSpec((t, PAIR), lambda p: (0, N_PAIRS + p)),
                  pl.BlockSpec((t, PAIR), lambda p: (0, V_COL0 + p))],
        out_specs=[out_spec, out_spec], out_shape=[slab, slab],
        scratch_shapes=[pltpu.VMEM((t, PAIR), F32)] * 5,
        compiler_params=_cparams(dimension_semantics=("parallel",)),
    )(qk, qk, z)


def _attn_bwd(qk, z, do, dlse):
    t = z.shape[0]
    n_blocks = t // BAND_BLOCK

    def body(q_ref, k_ref, v_ref, do_ref, dl_ref, dq_ref, dk_ref, dv_ref, qf, kf, vf, dof, dlf, dqf, dkf, dvf):
        def run(dil):
            for src, dst in ((q_ref, qf), (k_ref, kf), (v_ref, vf), (do_ref, dof), (dl_ref, dlf)):
                _fold(src, dst, dil)
            blocks_per_seq = n_blocks // dil

            def block(idx, carry):
                first, cur, prev = _block_rows(idx, blocks_per_seq)
                kmin = jnp.where(first, BAND_BLOCK, 0)
                grads = []
                for q, kp, kc, vp, vc, do_h, dl_h in zip(
                        _heads(qf[cur, :]), _heads(kf[prev, :]), _heads(kf[cur, :]), _heads(vf[prev, :]),
                        _heads(vf[cur, :]), _heads(dof[cur, :]), _heads(dlf[cur, :])):
                    _, vjp = jax.vjp(functools.partial(_attn_block, kmin=kmin), q, kp, kc, vp, vc)
                    grads.append(vjp((do_h, jnp.sum(dl_h, axis=1, keepdims=True))))
                dq, dkp, dkc, dvp, dvc = (jnp.concatenate([a, b], axis=1) for a, b in zip(*grads))
                dqf[cur, :] = dq
                dkf[cur, :] = dkc
                dvf[cur, :] = dvc

                @pl.when(jnp.logical_not(first))
                def _():
                    dkf[prev, :] += dkp
                    dvf[prev, :] += dvp

                return carry

            lax.fori_loop(0, n_blocks, block, 0)
            _unfold(dqf, dq_ref, dil)
            _unfold(dkf, dk_ref, dil)
            _unfold(dvf, dv_ref, dil)

        _per_group(run)

    slab = jax.ShapeDtypeStruct((t, N_PAIRS * PAIR), F32)
    own = pl.BlockSpec((t, PAIR), lambda p: (0, p))
    return pl.pallas_call(
        body, name="attn_bwd", grid=(N_PAIRS,),
        in_specs=[own, pl.BlockSpec((t, PAIR), lambda p: (0, N_PAIRS + p)),
                  pl.BlockSpec((t, PAIR), lambda p: (0, V_COL0 + p)), own, own],
        out_specs=[own] * 3, out_shape=[slab] * 3,
        scratch_shapes=[pltpu.VMEM((t, PAIR), F32)] * 8,
        compiler_params=_cparams(dimension_semantics=("parallel",)),
    )(qk, qk, z, do, dlse)


def _f_comb(o1, o2, o3, l1, l2, l3):
    m = jnp.maximum(jnp.maximum(l1, l2), l3)
    e1, e2, e3 = jnp.exp(l1 - m), jnp.exp(l2 - m), jnp.exp(l3 - m)
    den = e1 + e2 + e3
    return (e1 / den) * o1 + (e2 / den) * o2 + (e3 / den) * o3


def _all_gather_hbm(name, arrs):
    na = len(arrs)

    def body(*refs):
        x_refs, out_refs = refs[:na], refs[na:2 * na]
        send_sems, recv_sems, local_sems = refs[2 * na:]
        mx, my, mc = lax.axis_index("x"), lax.axis_index("y"), lax.axis_index("c")
        me, sibling = (mx, my, mc), (mx, my, 1 - mc)
        chips = [(1 - mx, my), (mx, 1 - my), (1 - mx, 1 - my)]

        def slot(a, px, py, pc):
            return out_refs[a].at[4 * px + 2 * py + pc]

        def copy(a, k, block, to, src=None):
            return pltpu.make_async_remote_copy(
                src_ref=slot(a, *block) if src is None else src, dst_ref=slot(a, *block),
                send_sem=send_sems.at[a, k], recv_sem=recv_sems.at[a, k], device_id=to, device_id_type=MESH)

        mine = [pltpu.make_async_copy(x_refs[a], slot(a, *me), local_sems.at[a]) for a in range(na)]
        for cp in mine:
            cp.start()
        first = []
        for a in range(na):
            first.append(copy(a, 0, me, sibling, src=x_refs[a]))
            first += [copy(a, 1 + j, me, (*chip, mc), src=x_refs[a]) for j, chip in enumerate(chips)]
        for cp in first:
            cp.start()
        passed = []
        for j, chip in enumerate(chips):
            for a in range(na):
                copy(a, 1 + j, (*chip, mc), me).wait_recv()
                passed.append(copy(a, 4 + j, (*chip, mc), sibling))
                passed[-1].start()
        for a in range(na):
            copy(a, 0, sibling, me).wait_recv()
            for j, chip in enumerate(chips):
                copy(a, 4 + j, (*chip, 1 - mc), me).wait_recv()
        for cp in first + passed:
            cp.wait_send()
        for cp in mine:
            cp.wait()

    hbm = pl.BlockSpec(memory_space=pl.ANY)
    return pl.pallas_call(
        body, name=name,
        out_shape=[jax.ShapeDtypeStruct((N_DEV,) + a.shape, a.dtype) for a in arrs],
        in_specs=[hbm] * na, out_specs=[hbm] * na,
        scratch_shapes=[pltpu.SemaphoreType.DMA((na, 7)), pltpu.SemaphoreType.DMA((na, 7)),
                        pltpu.SemaphoreType.DMA((na,))],
    )(*arrs)


def _all_gather_vmem(x):
    rws, cols = x.shape

    def body(x_ref, out_ref, send_sems, recv_sems):
        mx, my, mc = lax.axis_index("x"), lax.axis_index("y"), lax.axis_index("c")
        me, sibling = (mx, my, mc), (mx, my, 1 - mc)
        chips = [(1 - mx, my), (mx, 1 - my), (1 - mx, 1 - my)]

        def slot(px, py, pc):
            return out_ref.at[4 * px + 2 * py + pc]

        def copy(k, block, to, src=None):
            return pltpu.make_async_remote_copy(
                src_ref=slot(*block) if src is None else src, dst_ref=slot(*block),
                send_sem=send_sems.at[k], recv_sem=recv_sems.at[k], device_id=to, device_id_type=MESH)

        first = [copy(0, me, sibling, src=x_ref)]
        first += [copy(1 + j, me, (*chip, mc), src=x_ref) for j, chip in enumerate(chips)]
        for cp in first:
            cp.start()
        out_ref[4 * mx + 2 * my + mc] = x_ref[...]
        passed = [copy(4 + j, (*chip, mc), sibling) for j, chip in enumerate(chips)]
        for j, chip in enumerate(chips):
            copy(1 + j, (*chip, mc), me).wait_recv()
            passed[j].start()
        copy(0, sibling, me).wait_recv()
        for j, chip in enumerate(chips):
            copy(4 + j, (*chip, 1 - mc), me).wait_recv()
        for cp in first + passed:
            cp.wait_send()

    return pl.pallas_call(
        body, name="all_gather_small",
        out_shape=jax.ShapeDtypeStruct((N_DEV, rws, cols), x.dtype),
        in_specs=[pl.BlockSpec(memory_space=pltpu.VMEM)], out_specs=pl.BlockSpec(memory_space=pltpu.VMEM),
        scratch_shapes=[pltpu.SemaphoreType.DMA((7,)), pltpu.SemaphoreType.DMA((7,))],
    )(x)


def _all_to_all_hbm(name, arrs):
    na = len(arrs)

    def body(*refs):
        g_refs, out_refs = refs[:na], refs[na:2 * na]
        send_sems, recv_sems, local_sems = refs[2 * na:]
        mx, my, mc = lax.axis_index("x"), lax.axis_index("y"), lax.axis_index("c")
        me = 4 * mx + 2 * my + mc
        mine = [pltpu.make_async_copy(g_refs[a].at[me], out_refs[a].at[me], local_sems.at[a]) for a in range(na)]
        for cp in mine:
            cp.start()
        copies = []
        for k in range(1, N_DEV):
            px, py, pc = mx ^ (k >> 2), my ^ ((k >> 1) & 1), mc ^ (k & 1)
            peer = 4 * px + 2 * py + pc
            for a in range(na):
                copies.append(pltpu.make_async_remote_copy(
                    src_ref=g_refs[a].at[peer], dst_ref=out_refs[a].at[me], send_sem=send_sems.at[a, k - 1],
                    recv_sem=recv_sems.at[a, k - 1], device_id=(px, py, pc), device_id_type=MESH))
        for cp in copies:
            cp.start()
        for cp in copies:
            cp.wait_recv()
        for cp in copies:
            cp.wait_send()
        for cp in mine:
            cp.wait()

    hbm = pl.BlockSpec(memory_space=pl.ANY)
    return pl.pallas_call(
        body, name=name,
        out_shape=[jax.ShapeDtypeStruct(a.shape, a.dtype) for a in arrs],
        in_specs=[hbm] * na, out_specs=[hbm] * na,
        scratch_shapes=[pltpu.SemaphoreType.DMA((na, 7)), pltpu.SemaphoreType.DMA((na, 7)),
                        pltpu.SemaphoreType.DMA((na,))],
    )(*arrs)


def _scatter_copies(g_refs, land_refs, send_sems, recv_sems):
    mx, my, mc = lax.axis_index("x"), lax.axis_index("y"), lax.axis_index("c")
    me = 4 * mx + 2 * my + mc
    copies = []
    for k in range(1, N_DEV):
        px, py, pc = mx ^ (k >> 2), my ^ ((k >> 1) & 1), mc ^ (k & 1)
        peer = 4 * px + 2 * py + pc
        for a, (g_ref, land_ref) in enumerate(zip(g_refs, land_refs)):
            copies.append(pltpu.make_async_remote_copy(
                src_ref=g_ref.at[peer], dst_ref=land_ref.at[me], send_sem=send_sems.at[a * (N_DEV - 1) + k - 1],
                recv_sem=recv_sems.at[a * (N_DEV - 1) + k - 1], device_id=(px, py, pc), device_id_type=MESH))
    return copies


_HBM = pl.BlockSpec(memory_space=pltpu.HBM)
_SEM = pl.BlockSpec(memory_space=pltpu.SEMAPHORE)
_DATAFLOW = pltpu.SideEffectType.DATAFLOW_SIDE_EFFECTING


def _gather_copies(x_refs, land_refs, send_sems, recv_sems):
    mx, my, mc = lax.axis_index("x"), lax.axis_index("y"), lax.axis_index("c")
    me = 4 * mx + 2 * my + mc
    copies = []
    for k in range(1, N_DEV):
        px, py, pc = mx ^ (k >> 2), my ^ ((k >> 1) & 1), mc ^ (k & 1)
        for a, (x_ref, land_ref) in enumerate(zip(x_refs, land_refs)):
            copies.append(pltpu.make_async_remote_copy(
                src_ref=x_ref, dst_ref=land_ref.at[me], send_sem=send_sems.at[a * (N_DEV - 1) + k - 1],
                recv_sem=recv_sems.at[a * (N_DEV - 1) + k - 1], device_id=(px, py, pc), device_id_type=MESH))
    return copies


def _exchange_start(name, copies, srcs, lands, after):
    na = len(srcs)

    def body(*refs):
        for cp in copies(refs[:na], refs[na:2 * na], refs[2 * na + 1], refs[2 * na + 2]):
            cp.start()
        refs[-1][...] = jnp.zeros_like(refs[-1])

    in_hbm = lambda a: pltpu.with_memory_space_constraint(a, pltpu.HBM)
    outs = pl.pallas_call(
        body, name=name,
        out_shape=(pltpu.SemaphoreType.DMA((na * (N_DEV - 1),)), pltpu.SemaphoreType.DMA((na * (N_DEV - 1),)),
                   *[pltpu.HBM(a.shape, a.dtype) for a in list(srcs) + list(lands)],
                   jax.ShapeDtypeStruct((8, LANES), F32)),
        in_specs=[_HBM] * (2 * na) + [pl.BlockSpec(memory_space=pl.ANY)],
        out_specs=(_SEM, _SEM, *[_HBM] * (2 * na), pl.BlockSpec(memory_space=pltpu.VMEM)),
        input_output_aliases={i: 2 + i for i in range(2 * na)},
        compiler_params=pltpu.CompilerParams(has_side_effects=_DATAFLOW),
    )(*[in_hbm(a) for a in srcs], *[in_hbm(a) for a in lands], after)
    return outs[0], outs[1], outs[2:2 + na], outs[2 + na:2 + 2 * na], outs[-1]


def _exchange_wait(name, copies, send_sems, recv_sems, srcs, lands, after):
    na = len(srcs)

    def body(*refs):
        for cp in copies(refs[:na], refs[na:2 * na], refs[2 * na], refs[2 * na + 1]):
            cp.wait_send()
            cp.wait_recv()

    outs = pl.pallas_call(
        body, name=name,
        out_shape=[pltpu.HBM(a.shape, a.dtype) for a in list(srcs) + list(lands)],
        in_specs=[_HBM] * (2 * na) + [_SEM, _SEM, pl.BlockSpec(memory_space=pl.ANY)], out_specs=[_HBM] * (2 * na),
        input_output_aliases={i: i for i in range(2 * na)},
        compiler_params=pltpu.CompilerParams(has_side_effects=_DATAFLOW),
    )(*srcs, *lands, send_sems, recv_sems, after)
    return outs[:na], outs[na:]


def _sum_slots(name, g, tr):
    _, rws, cols = g.shape

    def body(g_ref, o_ref):
        acc = g_ref[0].astype(F32)
        for j in range(1, N_DEV):
            acc = acc + g_ref[j].astype(F32)
        o_ref[...] = acc

    return pl.pallas_call(
        body, name=name, grid=(rws // tr,),
        in_specs=[pl.BlockSpec((N_DEV, tr, cols), lambda i: (0, i, 0))],
        out_specs=pl.BlockSpec((tr, cols), lambda i: (i, 0)),
        out_shape=jax.ShapeDtypeStruct((rws, cols), F32),
        compiler_params=_cparams(dimension_semantics=("parallel",)),
    )(g)


def _adam_math(wv, gv, mv, vv):
    mn = ADAM_B1 * mv + (1.0 - ADAM_B1) * gv
    vn = ADAM_B2 * vv + (1.0 - ADAM_B2) * jnp.square(gv)
    m_hat = mn / (1.0 - ADAM_B1 ** ADAM_STEP)
    v_hat = vn / (1.0 - ADAM_B2 ** ADAM_STEP)
    delta = -ADAM_LR * (m_hat / (jnp.sqrt(v_hat) + ADAM_EPS) + ADAM_WD * wv)
    return delta, mn, vn


def _adamw(name, w, g, m, v, tr):
    return _rowwise(name, _adam_math, [w, g, m, v], [], [(LANES, F32)] * 3, tr=tr)


def _adamw_slots(name, recv, own, rb, cb, cw, w, m, v, tr):
    nr, nc = w.shape

    def body(g_ref, own_ref, w_ref, m_ref, v_ref, go_ref, d_ref, mo_ref, vo_ref):
        me = 4 * lax.axis_index("x") + 2 * lax.axis_index("y") + lax.axis_index("c")
        acc = None
        for s in range(N_DEV):
            part = jnp.where(me == s, own_ref[s], g_ref[s]).astype(F32)
            acc = part if acc is None else acc + part
        g = acc[:, :nc]
        go_ref[...] = g
        d_ref[...], mo_ref[...], vo_ref[...] = _adam_math(w_ref[...], g, m_ref[...], v_ref[...])

    nat = pl.BlockSpec((tr, nc), lambda i: (i, 0))
    slots = pl.BlockSpec((N_DEV, tr, cw), lambda i: (0, rb + i, cb))
    return pl.pallas_call(
        body, name=name, grid=(nr // tr,),
        in_specs=[slots, slots, nat, nat, nat],
        out_specs=[nat] * 4, out_shape=[jax.ShapeDtypeStruct((nr, nc), F32)] * 4,
        compiler_params=_cparams(dimension_semantics=("parallel",)),
    )(recv, own, w, m, v)


def _local_blocks(w):
    pad_cols = lambda a: jnp.pad(a, ((0, 0), (0, FF_PAD - FF_SHARD)))
    pad_rows = lambda a: jnp.pad(a, ((0, FF_PAD - FF_SHARD), (0, 0)))
    gate_up = lambda tag: jnp.concatenate([pad_cols(w[tag + "_w_gate"]), pad_cols(w[tag + "_w_up"])], axis=1)
    blocks = {
        "ffn1_gu": gate_up("ffn1"), "ffn1_d": pad_rows(w["ffn1_w_down"]), "w_in": w["w_in"],
        "lora": jnp.concatenate([w["rwkv_w2"], w["rwkv_a2"], w["rwkv_g2"]], axis=0),
        "br": jnp.concatenate([w["w_br_rwkv"], w["w_br_attn"], w["ple_w_proj"]], axis=0),
        "w_out": w["w_out"], "ffn2_gu": gate_up("ffn2"), "ffn2_d": pad_rows(w["ffn2_w_down"]),
        "ple_gate": w["ple_w_gate"],
    }
    return {n: a.astype(BF16) for n, a in blocks.items()}


GATHER_GROUPS = {"head": ("ffn1_gu", "ffn1_d"), "mid": ("w_in", "lora"),
                 "rest": ("br", "w_out", "ffn2_gu", "ffn2_d", "ple_gate")}

SCATTER_GROUPS = {"tail": ("ple_gate", "ple_proj", "ffn2_gu", "ffn2_d"), "mixer": ("w_out", "br", "lora", "w_in"),
                  "ffn1_down": ("ffn1_d",), "head": ("ffn1_gu",)}

ADAM_PLAN = (
    ("ffn1_w_gate", "ffn1_gu", 0, 0, FF_PAD, 256), ("ffn1_w_up", "ffn1_gu", 0, 1, FF_PAD, 256),
    ("ffn1_w_down", "ffn1_d", 0, 0, D_MODEL, FF_SHARD // 2), ("w_in", "w_in", 0, 0, IN_SHARD, 256),
    ("rwkv_w2", "lora", 0, 0, HEAD_DIM, 64), ("rwkv_a2", "lora", 1, 0, HEAD_DIM, 64),
    ("rwkv_g2", "lora", 2, 0, HEAD_DIM, 64),
    ("w_br_rwkv", "br", 0, 0, OUT_SHARD, 256), ("w_br_attn", "br", 2, 0, OUT_SHARD, 256),
    ("ple_w_proj", "ple_proj", 0, 0, OUT_SHARD, 256), ("w_out", "w_out", 0, 0, D_MODEL, OUT_SHARD),
    ("ffn2_w_gate", "ffn2_gu", 0, 0, FF_PAD, 256), ("ffn2_w_up", "ffn2_gu", 0, 1, FF_PAD, 256),
    ("ffn2_w_down", "ffn2_d", 0, 0, D_MODEL, FF_SHARD // 2), ("ple_w_gate", "ple_gate", 0, 0, D_MODEL, OUT_SHARD),
)


def _pack_small(arrs, rows):
    flat = jnp.concatenate([a.reshape(-1) for a in arrs])
    return jnp.pad(flat, (0, rows * LANES - flat.shape[0])).reshape(rows, LANES)


def _unpack_small(flat, like):
    flat = flat.reshape(-1)
    out, off = [], 0
    for a in like:
        out.append(flat[off:off + a.size].reshape(a.shape))
        off += a.size
    return out


def _local_step(x, p, pos, target, sm, wg, fetch, on_grads):
    t = x.shape[0]
    hv = t * RWKV_HEADS

    w_d1 = wg["ffn1_d"].reshape(FF_HID, D_MODEL)
    x1, ffn1_saved = _ffn_fwd("ffn1", x, sm["ffn1_norm"], wg["ffn1_gu"], w_d1)
    wg = {**wg, **fetch("mid", x1)}
    full_cols = lambda blk: blk.transpose(1, 0, 2).reshape(blk.shape[1], N_DEV * blk.shape[2])
    lora_w2 = full_cols(wg["lora"][:, :DECAY_LORA])
    lora_a2 = full_cols(wg["lora"][:, DECAY_LORA:DECAY_LORA + ICLR_LORA])
    lora_g2 = full_cols(wg["lora"][:, DECAY_LORA + ICLR_LORA:])
    h2 = _norm_fwd("mix_norm", x1, sm["mix_norm"])
    z = _mmc_nn("w_in", h2, wg["w_in"], 0, 0, IN_SHARD)
    z_r = z[:, :RWKV_COLS]
    z_g = z[:, RWKV_COLS + 3 * ATTN_DIM:]

    r, k, v, lo, gd = _shift_fwd(z_r, sm["rwkv_mu"])
    zero_lo = jnp.zeros((DECAY_LORA, RWKV_DIM), BF16)
    w2p = jnp.concatenate([lora_w2, zero_lo], axis=0).astype(F32)
    a2p = jnp.concatenate([zero_lo, lora_a2], axis=0).astype(F32)
    pre_params = [sm["rwkv_w0"], w2p, sm["rwkv_a0"], a2p, lora_g2.astype(F32), sm["rwkv_k_k"], sm["rwkv_k_a"]]
    wide = [(RWKV_DIM, F32)]
    k2, kk, a, decay, g = _rowwise("rwkv_pre", _f_pre, [k, lo, gd], pre_params, wide * 5)
    as_heads = lambda u: u.reshape(hv, HEAD_DIM)
    kkn, b = _rowwise("rwkv_kk", _f_kk, [as_heads(kk), as_heads(a)], [], [(HEAD_DIM, F32)] * 2, tr=512)
    scan_in = [u.reshape(t, RWKV_HEADS, HEAD_DIM) for u in (r, decay, k2, kkn, b)]
    v_rows = _to_v_rows(v)
    y_rows, states = _wkv_fwd(*scan_in, v_rows)
    y = _from_v_rows(y_rows)
    post_params = [sm["rwkv_gn_w"].reshape(RWKV_HEADS, HEAD_DIM), sm["rwkv_gn_b"].reshape(RWKV_HEADS, HEAD_DIM),
                   sm["rwkv_r_k"].reshape(RWKV_HEADS, HEAD_DIM)]
    post_rows = [as_heads(y), as_heads(r), as_heads(k2), as_heads(v), as_heads(g)]
    y_rwkv = _rowwise("rwkv_post", lambda *av: (_f_post(*av),), post_rows, post_params, [(HEAD_DIM, F32)], tr=512)[0]
    y_rwkv = y_rwkv.reshape(t, RWKV_DIM)

    inv_freq = 1.0 / (ROPE_THETA ** (jnp.arange(0, HEAD_DIM, 2, dtype=F32) / HEAD_DIM))
    freq2 = jnp.tile(inv_freq, 2 * PAIR // HEAD_DIM).reshape(1, PAIR)
    half = jnp.ones((HEAD_DIM // 2,), F32)
    sign2 = jnp.tile(jnp.concatenate([-half, half]), PAIR // HEAD_DIM).reshape(1, PAIR)

    def rope_table(posv, fr, sg):
        ang = posv * fr
        return (jnp.concatenate([jnp.cos(ang), jnp.sin(ang) * sg], axis=1),)
    tab = _rowwise("rope_table", rope_table, [pos.astype(F32).reshape(t, 1)], [freq2, sign2], [(2 * PAIR, F32)])[0]
    qk = _qk_prep(z, tab, sm["q_norm"], sm["k_norm"])
    o_all, lse_all = _attn_fwd(qk, z)
    gw = HEADS_PER_GROUP * HEAD_DIM

    def by_group(ov, lv):
        return [ov[:, i * gw:(i + 1) * gw] for i in range(3)] + [lv[:, i * gw:(i + 1) * gw] for i in range(3)]
    y_attn = _rowwise("attn_comb", lambda ov, lv: (_f_comb(*by_group(ov, lv)),), [o_all, lse_all], [], [(gw, F32)])[0]

    wg = {**wg, **fetch("rest", y_rwkv)}
    w_d2 = wg["ffn2_d"].reshape(FF_HID, D_MODEL)
    w_out = wg["w_out"].reshape(D_MODEL, D_MODEL)
    w_pg = wg["ple_gate"].reshape(D_MODEL, D_MODEL)
    u_r =_mmc_nn("br_rwkv", y_rwkv, wg["br"], 0, 0, OUT_SHARD)
    u_a = _mmc_nn("br_attn", y_attn, wg["br"], 2, 0, OUT_SHARD)

    def f_merge(zgr, zga, ur, ua):
        return _sigmoid(zgr) * ur + _sigmoid(zga) * ua
    merged = _rowwise("merge", lambda zg, ur, ua: (f_merge(zg[:, :D_MODEL], zg[:, D_MODEL:], ur, ua),),
                      [z_g, u_r, u_a], [], [(D_MODEL, BF16)])[0]
    x2 = _mm("w_out", merged, w_out, "nn", res=x1)
    x3, ffn2_saved = _ffn_fwd("ffn2", x2, sm["ffn2_norm"], wg["ffn2_gu"], w_d2)

    hn = _norm_fwd("ple_norm", x3, sm["ple_norm"])
    gz = _mm("ple_gate", hn, w_pg, "nn")
    pp = _mmc_nn("ple_proj", p, wg["br"], 3, 0, OUT_SHARD)

    def f_head(x3v, gzv, ppv, tg):
        sg = _sigmoid(gzv)
        err = x3v + sg * ppv - tg
        part = 0.5 * jnp.sum(jnp.mean(err * err, axis=-1, keepdims=True))
        dx4 = err * (1.0 / D_MODEL)
        return dx4, dx4 * ppv * sg * (1.0 - sg), dx4 * sg, jnp.full((1, LANES), part, F32)
    dx4, dgz, dpp, loss_row = _rowwise("ple_loss", f_head, [x3, gz, pp, target], [],
                                       [(D_MODEL, F32), (D_MODEL, BF16), (D_MODEL, BF16)], [(1, LANES)])
    loss = loss_row[0, 0]

    gs, gm = {}, {}
    row_blocks = lambda g: g.reshape(N_DEV, g.shape[0] // N_DEV, g.shape[1])
    dhn = _mm("ple_dhn", dgz, w_pg, "nt")
    gm["ple_gate"] = row_blocks(_mm("ple_dwgate", hn, dgz, "tn", out_dtype=BF16))
    gm["ple_proj"] = _mmc_tn("ple_dwproj", p, dpp, OUT_SHARD)
    dx3, gs["ple_norm"] = _norm_bwd("ple_dnorm", x3, sm["ple_norm"], dhn, dx4)

    dx2, gs["ffn2_norm"], gm["ffn2_gu"], gm["ffn2_d"] = _ffn_bwd(
        "ffn2", x2, sm["ffn2_norm"], wg["ffn2_gu"], w_d2, ffn2_saved, dx3)
    tail_token = on_grads("tail", {n: gm.pop(n) for n in SCATTER_GROUPS["tail"]})

    dmerged = _mm("w_out_dmerged", dx2, w_out, "nt")
    gm["w_out"] = row_blocks(_mm("w_out_dw", merged, dx2, "tn", out_dtype=BF16))

    def merge_bwd(zg, ur, ua, dm):
        _, vjp = jax.vjp(f_merge, zg[:, :D_MODEL], zg[:, D_MODEL:], ur, ua)
        dzr, dza, dur, dua = vjp(dm)
        return jnp.concatenate([dzr, dza], axis=1), dur, dua
    dz_g, du_r, du_a = _rowwise("merge_bwd", merge_bwd, [z_g, u_r, u_a, dmerged], [],
                                [(2 * D_MODEL, BF16), (D_MODEL, BF16), (D_MODEL, BF16)])
    dy_rwkv = _mmc_nt("br_rwkv_dy", du_r, wg["br"], 0, 0, OUT_SHARD, RWKV_DIM)
    dy_attn = _mmc_nt("br_attn_dy", du_a, wg["br"], 2, 0, OUT_SHARD, HEADS_PER_GROUP * HEAD_DIM)
    gm["br"] = jnp.concatenate([_mmc_tn("br_rwkv_dw", y_rwkv, du_r, OUT_SHARD),
                                _mmc_tn("br_attn_dw", y_attn, du_a, OUT_SHARD)], axis=1)

    def comb_bwd(ov, lv, dyv):
        _, vjp = jax.vjp(_f_comb, *by_group(ov, lv))
        d = vjp(dyv)
        return jnp.concatenate(d[:3], axis=1), jnp.concatenate(d[3:], axis=1)
    do_all, dl_all = _rowwise("attn_comb_bwd", comb_bwd, [o_all, lse_all, dy_attn], [],
                              [(ATTN_DIM, F32), (ATTN_DIM, F32)])
    dq_all, dk_all, dv_all = _attn_bwd(qk, z, do_all, dl_all)
    dqk_raw, gs["q_norm"], gs["k_norm"] = _qk_prep_bwd(z, tab, sm["q_norm"], sm["k_norm"], dq_all, dk_all)

    def post_bwd(yv, rv, k2v, vv, gv, dv_, gnw, gnb, rk):
        _, vjp = jax.vjp(_f_post, yv, rv, k2v, vv, gv, gnw, gnb, rk)
        return vjp(dv_)
    head_acc = (RWKV_HEADS, HEAD_DIM)
    dy, dr1, dk2a, dv1, dg, d_gnw, d_gnb, d_rk = _rowwise(
        "rwkv_post_bwd", post_bwd, post_rows + [as_heads(dy_rwkv)], [post_params[0] + tail_token] + post_params[1:],
        [(HEAD_DIM, F32)] * 5, [head_acc] * 3, tr=512)
    gs["rwkv_gn_w"], gs["rwkv_gn_b"], gs["rwkv_r_k"] = d_gnw, d_gnb, d_rk
    dr2, ddecay, dk2b, dkkn, db, dv_rows = _wkv_bwd(*scan_in, v_rows, states, _to_v_rows(dy.reshape(t, RWKV_DIM)))
    dr2, ddecay, dk2b, dkkn, db = [u.reshape(t, RWKV_DIM) for u in (dr2, ddecay, dk2b, dkkn, db)]
    dv2 = _from_v_rows(dv_rows)

    def kk_bwd(kkv, av, dkknv, dbv):
        _, vjp = jax.vjp(_f_kk, kkv, av)
        return vjp((dkknv, dbv))
    dkk, da = _rowwise("rwkv_kk_bwd", kk_bwd, [as_heads(kk), as_heads(a), as_heads(dkkn), as_heads(db)], [],
                       [(HEAD_DIM, F32)] * 2, tr=512)

    def pre_bwd(kv, lov, gdv, dk2x, dk2y, dkkv, dav, ddec, dgv, w0, w2p_, a0, a2p_, g2, k_k, k_a):
        _, vjp = jax.vjp(_f_pre, kv, lov, gdv, w0, w2p_, a0, a2p_, g2, k_k, k_a)
        return vjp((dk2x + dk2y, dkkv, dav, ddec, dgv))
    lora_acc = (DECAY_LORA + ICLR_LORA, RWKV_DIM)
    dk, dlo, dgd, d_w0, d_w2p, d_a0, d_a2p, d_g2, d_kk, d_ka = _rowwise(
        "rwkv_pre_bwd", pre_bwd,
        [k, lo, gd, dk2a.reshape(t, RWKV_DIM), dk2b, dkk.reshape(t, RWKV_DIM), da.reshape(t, RWKV_DIM), ddecay,
         dg.reshape(t, RWKV_DIM)],
        pre_params, [(RWKV_DIM, F32), (LANES, F32), (LANES, F32)],
        [(1, RWKV_DIM), lora_acc, (1, RWKV_DIM), lora_acc, (GATE_LORA, RWKV_DIM), (1, RWKV_DIM), (1, RWKV_DIM)])
    gs["rwkv_w0"], gs["rwkv_a0"], gs["rwkv_k_k"], gs["rwkv_k_a"] = d_w0, d_a0, d_kk, d_ka
    col_blocks = lambda g: g.reshape(g.shape[0], N_DEV, g.shape[1] // N_DEV).transpose(1, 0, 2)
    gm["lora"] = jnp.concatenate([col_blocks(d_w2p[:DECAY_LORA]), col_blocks(d_a2p[DECAY_LORA:]), col_blocks(d_g2)],
                                 axis=1).astype(BF16)
    add2 = lambda u, w_: _rowwise("rwkv_add", lambda p_, q_: (p_ + q_,), [u, w_], [], [(RWKV_DIM, F32)])[0]
    dz_r, gs["rwkv_mu"] = _shift_bwd(z_r, sm["rwkv_mu"], add2(dr1.reshape(t, RWKV_DIM), dr2), dk,
                                     add2(dv1.reshape(t, RWKV_DIM), dv2), dlo, dgd)

    dz = jnp.concatenate([dz_r, dqk_raw, dv_all.astype(BF16), dz_g], axis=1)
    dh2 = _mmc_nt("w_in_dh", dz, wg["w_in"], 0, 0, IN_SHARD, D_MODEL)
    gm["w_in"] = _mmc_tn("w_in_dw", h2, dz, IN_SHARD)
    mixer_token = on_grads("mixer", {n: gm.pop(n) for n in SCATTER_GROUPS["mixer"]})
    dx1, gs["mix_norm"] = _norm_bwd("mix_dnorm", x1, sm["mix_norm"] + mixer_token, dh2, dx2)

    dx0, gs["ffn1_norm"], gm["ffn1_gu"], _ = _ffn_bwd(
        "ffn1", x, sm["ffn1_norm"], wg["ffn1_gu"], w_d1, ffn1_saved, dx1,
        on_down=lambda blocks: on_grads("ffn1_down", {"ffn1_d": blocks}))
    return loss, dx0, gm, gs


def kernel(x, p, positions, ffn1_norm, ffn1_w_gate, ffn1_w_up, ffn1_w_down, mix_norm, w_in, rwkv_mu, rwkv_w0, rwkv_w2, rwkv_a0, rwkv_a2, rwkv_g2, rwkv_k_k, rwkv_k_a, rwkv_r_k, rwkv_gn_w, rwkv_gn_b, q_norm, k_norm, w_br_rwkv, w_br_attn, w_out, ffn2_norm, ffn2_w_gate, ffn2_w_up, ffn2_w_down, ple_norm, ple_w_gate, ple_w_proj, loss_target, m_ffn1_norm, m_ffn1_w_gate, m_ffn1_w_up, m_ffn1_w_down, m_mix_norm, m_w_in, m_rwkv_mu, m_rwkv_w0, m_rwkv_w2, m_rwkv_a0, m_rwkv_a2, m_rwkv_g2, m_rwkv_k_k, m_rwkv_k_a, m_rwkv_r_k, m_rwkv_gn_w, m_rwkv_gn_b, m_q_norm, m_k_norm, m_w_br_rwkv, m_w_br_attn, m_w_out, m_ffn2_norm, m_ffn2_w_gate, m_ffn2_w_up, m_ffn2_w_down, m_ple_norm, m_ple_w_gate, m_ple_w_proj, v_ffn1_norm, v_ffn1_w_gate, v_ffn1_w_up, v_ffn1_w_down, v_mix_norm, v_w_in, v_rwkv_mu, v_rwkv_w0, v_rwkv_w2, v_rwkv_a0, v_rwkv_a2, v_rwkv_g2, v_rwkv_k_k, v_rwkv_k_a, v_rwkv_r_k, v_rwkv_gn_w, v_rwkv_gn_b, v_q_norm, v_k_norm, v_w_br_rwkv, v_w_br_attn, v_w_out, v_ffn2_norm, v_ffn2_w_gate, v_ffn2_w_up, v_ffn2_w_down, v_ple_norm, v_ple_w_gate, v_ple_w_proj):
    args = locals()
    w = {n: args[n][0] for n in WEIGHTS}
    m = {n: args["m_" + n][0] for n in WEIGHTS}
    v = {n: args["v_" + n][0] for n in WEIGHTS}

    w_loc = _local_blocks(w)
    head = GATHER_GROUPS["head"]
    wg = dict(zip(head, _all_gather_hbm("gather_head", [w_loc[n] for n in head])))
    me = 4 * lax.axis_index("x") + 2 * lax.axis_index("y") + lax.axis_index("c")
    gathering, order_after = {}, wg[head[0]]
    for group in ("mid", "rest"):
        shards = [w_loc[n] for n in GATHER_GROUPS[group]]
        zones = [lax.dynamic_update_slice(lax.empty((N_DEV,) + a.shape, a.dtype), a[None], (me, 0, 0)) for a in shards]
        *gathering[group], order_after = _exchange_start("gather_start_" + group, _gather_copies, shards, zones,
                                                         order_after)

    def fetch(group, after):
        _, got = _exchange_wait("gather_wait_" + group, _gather_copies, *gathering[group], after)
        return dict(zip(GATHER_GROUPS[group], got))

    sm = {n: w[n].reshape(1, -1) for n in SMALL}
    sm["ffn1_norm"] = sm["ffn1_norm"] + order_after[0, 0]
    in_flight = {}

    def scatter_early(group, arrays):
        arrs = [arrays[n] for n in SCATTER_GROUPS[group]]
        *in_flight[group], token = _exchange_start("scatter_start_" + group, _scatter_copies, arrs,
                                                   [lax.empty(a.shape, a.dtype) for a in arrs], arrs[0])
        return token[0, 0]
    loss_part, dx, gm, gs = _local_step(x[0], p[0, 0], positions[0], loss_target[0], sm, wg, fetch, scatter_early)
    loss = lax.psum(loss_part, ("x", "y", "c"))
    recv, own = {}, {}
    for group, handles in in_flight.items():
        sent, lands = _exchange_wait("scatter_wait_" + group, _scatter_copies, *handles, gm[SCATTER_GROUPS["head"][0]])
        own.update(zip(SCATTER_GROUPS[group], sent))
        recv.update(zip(SCATTER_GROUPS[group], lands))
    head = _all_to_all_hbm("scatter_head", [gm[n] for n in SCATTER_GROUPS["head"]])
    recv.update(zip(SCATTER_GROUPS["head"], head))
    own.update(zip(SCATTER_GROUPS["head"], head))
    small_like = [w[n] for n in SMALL]
    small_rows = 80
    gs_all = _all_gather_vmem(_pack_small([gs[n] for n in SMALL], small_rows))
    gs_sum = _sum_slots("sum_small_grads", gs_all, small_rows)

    res = {}
    for n, src, rb, cb, cw, tr in ADAM_PLAN:
        outs4 = _adamw_slots("adamw_" + n, recv[src], own[src], rb, cb, cw, w[n], m[n], v[n], tr)
        for tag, a in zip(("grad", "delta", "new_m", "new_v"), outs4):
            res[tag, n] = a[None]
    d_s, m_s, v_s = _adamw("adamw_small", _pack_small(small_like, small_rows), gs_sum,
                           _pack_small([m[n] for n in SMALL], small_rows),
                           _pack_small([v[n] for n in SMALL], small_rows), small_rows)
    for tag, small in (("grad", gs_sum), ("delta", d_s), ("new_m", m_s), ("new_v", v_s)):
        for n, a in zip(SMALL, _unpack_small(small, small_like)):
            res[tag, n] = a[None]
    outs = [loss, dx[None]]
    for tag in ("grad", "delta", "new_m", "new_v"):
        outs += [res[tag, n] for n in WEIGHTS]
    return tuple(outs)
```

```python
import functools

import jax
import jax.numpy as jnp
from jax import lax
from jax.experimental import pallas as pl
from jax.experimental.pallas import tpu as pltpu

F32, BF16 = jnp.float32, jnp.bfloat16
MESH = pl.DeviceIdType.MESH
N_DEV = 8
LANES = 128
VMEM_LIMIT = 56 * 1024 * 1024

D_MODEL = 1024
PLE_DIM = 256
HEAD_DIM = 64
RWKV_HEADS = 8
RWKV_DIM = RWKV_HEADS * HEAD_DIM
DECAY_LORA = 64
ICLR_LORA = 64
GATE_LORA = 128
GN_EPS = 64e-5
ATTN_GROUPS = ((128, 1), (512, 4), (2048, 16))
HEADS_PER_GROUP = 4
ATTN_HEADS = HEADS_PER_GROUP * len(ATTN_GROUPS)
ATTN_DIM = ATTN_HEADS * HEAD_DIM
BAND_BLOCK = 128
ROPE_THETA = 10000.0
NEG_INF = -1e30
D_FF = 2816
RMS_EPS = 1e-6
RWKV_COLS = 3 * RWKV_DIM + DECAY_LORA + ICLR_LORA + GATE_LORA
ADAM_LR = 0.001
ADAM_B1 = 0.9
ADAM_B2 = 0.999
ADAM_EPS = 1e-08
ADAM_WD = 0.01
ADAM_STEP = 10

V_LO = LANES // RWKV_HEADS
V_HI = HEAD_DIM // V_LO
SCAN_CHUNK = 64

FF_SHARD = D_FF // N_DEV
FF_PAD = -(-FF_SHARD // LANES) * LANES
FF_HID = N_DEV * FF_PAD
IN_SHARD = 6144 // N_DEV
OUT_SHARD = D_MODEL // N_DEV

SMALL = ("ffn1_norm", "mix_norm", "rwkv_mu", "rwkv_w0", "rwkv_a0", "rwkv_k_k", "rwkv_k_a", "rwkv_r_k",
         "rwkv_gn_w", "rwkv_gn_b", "q_norm", "k_norm", "ffn2_norm", "ple_norm")
WEIGHTS = ("ffn1_norm", "ffn1_w_gate", "ffn1_w_up", "ffn1_w_down", "mix_norm", "w_in", "rwkv_mu", "rwkv_w0",
           "rwkv_w2", "rwkv_a0", "rwkv_a2", "rwkv_g2", "rwkv_k_k", "rwkv_k_a", "rwkv_r_k", "rwkv_gn_w",
           "rwkv_gn_b", "q_norm", "k_norm", "w_br_rwkv", "w_br_attn", "w_out", "ffn2_norm", "ffn2_w_gate",
           "ffn2_w_up", "ffn2_w_down", "ple_norm", "ple_w_gate", "ple_w_proj")


def _cparams(**kw):
    return pltpu.CompilerParams(vmem_limit_bytes=VMEM_LIMIT, **kw)


def _tile(n, cap):
    best = None
    for t in range(LANES, min(n, cap) + 1, LANES):
        if n % t == 0:
            best = t
    return best if best is not None else n


@jax.custom_vjp
def _bdot(a, w):
    return jnp.dot(a.astype(BF16), w.astype(BF16), preferred_element_type=F32)


def _bdot_fwd(a, w):
    return _bdot(a, w), (a, w)


def _bdot_bwd(res, g):
    a, w = res
    gb = g.astype(BF16)
    da = lax.dot_general(gb, w.astype(BF16), (((1,), (1,)), ((), ())), preferred_element_type=F32)
    dw = lax.dot_general(a.astype(BF16), gb, (((0,), (0,)), ((), ())), preferred_element_type=F32)
    return da.astype(a.dtype), dw.astype(w.dtype)


_bdot.defvjp(_bdot_fwd, _bdot_bwd)


@jax.custom_vjp
def _bdot_nt(a, b):
    return lax.dot_general(a.astype(BF16), b.astype(BF16), (((1,), (1,)), ((), ())), preferred_element_type=F32)


def _bdot_nt_fwd(a, b):
    return _bdot_nt(a, b), (a, b)


def _bdot_nt_bwd(res, g):
    a, b = res
    gb = g.astype(BF16)
    da = jnp.dot(gb, b.astype(BF16), preferred_element_type=F32)
    db = lax.dot_general(gb, a.astype(BF16), (((0,), (0,)), ((), ())), preferred_element_type=F32)
    return da.astype(a.dtype), db.astype(b.dtype)


_bdot_nt.defvjp(_bdot_nt_fwd, _bdot_nt_bwd)


def _mm(name, a, b, mode, out_dtype=F32, res=None, scale=None):
    if mode == "nn":
        (m, k), n = a.shape, b.shape[1]
    elif mode == "nt":
        (m, k), n = a.shape, b.shape[0]
    else:
        (k, m), n = a.shape, b.shape[1]
    tm, tn = _tile(m, 512), _tile(n, 512)
    a_spec = pl.BlockSpec((k, tm), lambda i, j: (0, i)) if mode == "tn" else pl.BlockSpec((tm, k), lambda i, j: (i, 0))
    b_spec = pl.BlockSpec((tn, k), lambda i, j: (j, 0)) if mode == "nt" else pl.BlockSpec((k, tn), lambda i, j: (0, j))
    dims = {"nn": ((1,), (0,)), "nt": ((1,), (1,)), "tn": ((0,), (0,))}[mode]
    o_spec = pl.BlockSpec((tm, tn), lambda i, j: (i, j))
    ins, in_specs = [a, b], [a_spec, b_spec]
    if res is not None:
        ins.append(res)
        in_specs.append(o_spec)

    def body(*refs):
        acc = lax.dot_general(refs[0][...].astype(BF16), refs[1][...].astype(BF16), (dims, ((), ())),
                              preferred_element_type=F32)
        if scale is not None:
            acc = acc * scale
        if res is not None:
            acc = acc + refs[2][...].astype(F32)
        refs[-1][...] = acc.astype(refs[-1].dtype)

    return pl.pallas_call(
        body, name=name, grid=(m // tm, n // tn), in_specs=in_specs, out_specs=o_spec,
        out_shape=jax.ShapeDtypeStruct((m, n), out_dtype),
        compiler_params=_cparams(dimension_semantics=("parallel", "parallel")),
    )(*ins)


def _mmc_nn(name, a, wb, ki, ci, n, out_dtype=F32):
    m, k = a.shape
    tm = _tile(m, 512)

    def body(a_ref, w_ref, o_ref):
        o_ref[...] = jnp.dot(a_ref[...].astype(BF16), w_ref[...], preferred_element_type=F32).astype(o_ref.dtype)

    return pl.pallas_call(
        body, name=name, grid=(m // tm, N_DEV),
        in_specs=[pl.BlockSpec((tm, k), lambda i, j: (i, 0)), pl.BlockSpec((None, k, n), lambda i, j: (j, ki, ci))],
        out_specs=pl.BlockSpec((tm, n), lambda i, j: (i, j)),
        out_shape=jax.ShapeDtypeStruct((m, N_DEV * n), out_dtype),
        compiler_params=_cparams(dimension_semantics=("parallel", "parallel")),
    )(a, wb)


def _mmc_nt(name, a, wb, ki, ci, n, k, res=None):
    m = a.shape[0]
    tm = _tile(m, 512)
    o_spec = pl.BlockSpec((tm, k), lambda i, j: (i, 0))
    ins = [a, wb] + ([res] if res is not None else [])
    in_specs = [pl.BlockSpec((tm, n), lambda i, j: (i, j)), pl.BlockSpec((None, k, n), lambda i, j: (j, ki, ci))]
    in_specs += [o_spec] if res is not None else []

    def body(*refs):
        a_ref, w_ref, o_ref = refs[0], refs[1], refs[-1]
        acc = lax.dot_general(a_ref[...].astype(BF16), w_ref[...], (((1,), (1,)), ((), ())),
                              preferred_element_type=F32)

        @pl.when(pl.program_id(1) == 0)
        def _():
            o_ref[...] = acc + refs[2][...] if res is not None else acc

        @pl.when(pl.program_id(1) != 0)
        def _():
            o_ref[...] += acc

    return pl.pallas_call(
        body, name=name, grid=(m // tm, N_DEV), in_specs=in_specs, out_specs=o_spec,
        out_shape=jax.ShapeDtypeStruct((m, k), F32),
        compiler_params=_cparams(dimension_semantics=("parallel", "arbitrary")),
    )(*ins)


def _mmc_tn(name, x, dy, n):
    m, k = x.shape
    tk = _tile(k, 512)

    def body(x_ref, dy_ref, o_ref):
        o_ref[...] = lax.dot_general(x_ref[...].astype(BF16), dy_ref[...].astype(BF16), (((0,), (0,)), ((), ())),
                                     preferred_element_type=F32).astype(o_ref.dtype)

    return pl.pallas_call(
        body, name=name, grid=(N_DEV, k // tk),
        in_specs=[pl.BlockSpec((m, tk), lambda j, i: (0, i)), pl.BlockSpec((m, n), lambda j, i: (0, j))],
        out_specs=pl.BlockSpec((None, tk, n), lambda j, i: (j, i, 0)),
        out_shape=jax.ShapeDtypeStruct((N_DEV, k, n), BF16),
        compiler_params=_cparams(dimension_semantics=("parallel", "parallel")),
    )(x, dy)


def _rowwise(name, fn, rows, params, out_rows, out_accs=(), tr=256):
    rows = [a if isinstance(a, tuple) else (a, a.shape[1], 0) for a in rows]
    r = rows[0][0].shape[0]
    in_specs = [pl.BlockSpec((tr, wd), lambda i, cb=cb: (i, cb)) for _, wd, cb in rows]
    rows = [a for a, _, _ in rows]
    in_specs += [pl.BlockSpec(p.shape, lambda i, nd=p.ndim: (0,) * nd) for p in params]
    out_shape = [jax.ShapeDtypeStruct((r, c), dt) for c, dt in out_rows]
    out_shape += [jax.ShapeDtypeStruct(s, F32) for s in out_accs]
    out_specs = [pl.BlockSpec((tr, c), lambda i: (i, 0)) for c, _ in out_rows]
    out_specs += [pl.BlockSpec(s, lambda i, nd=len(s): (0,) * nd) for s in out_accs]
    n_in, n_ro = len(rows) + len(params), len(out_rows)

    def body(*refs):
        res = fn(*[ref[...] for ref in refs[:n_in]])
        outs = refs[n_in:]
        for o, v in zip(outs[:n_ro], res[:n_ro]):
            o[...] = v.astype(o.dtype)
        for o, v in zip(outs[n_ro:], res[n_ro:]):
            _accumulate(o, v)

    return pl.pallas_call(
        body, name=name, grid=(r // tr,), in_specs=in_specs, out_specs=out_specs, out_shape=out_shape,
        compiler_params=_cparams(dimension_semantics=("arbitrary",)),
    )(*rows, *params)


def _accumulate(o_ref, v):
    @pl.when(pl.program_id(0) == 0)
    def _():
        o_ref[...] = v

    @pl.when(pl.program_id(0) != 0)
    def _():
        o_ref[...] += v


def _rms(x, g):
    return x * lax.rsqrt(jnp.mean(x * x, axis=-1, keepdims=True) + RMS_EPS) * g


def _sigmoid(x):
    return jax.nn.sigmoid(x)


def _softplus(x):
    return jnp.maximum(x, 0.0) + jnp.log1p(jnp.exp(-jnp.abs(x)))


def _norm_fwd(name, x, g):
    return _rowwise(name, lambda xv, gv: (_rms(xv, gv),), [x], [g], [(x.shape[1], BF16)])[0]


def _norm_bwd(name, x, g, dh, dres):
    def fn(xv, dhv, drv, gv):
        _, vjp = jax.vjp(_rms, xv, gv)
        dx, dg = vjp(dhv)
        return dx + drv, dg
    return _rowwise(name, fn, [x, dh, dres], [g], [(x.shape[1], F32)], [g.shape])


def _f_act(gate, up):
    return gate * _sigmoid(gate) * up


def _gate_up(guv, j):
    base = j * 2 * FF_PAD
    return guv[:, base:base + FF_PAD], guv[:, base + FF_PAD:base + 2 * FF_PAD]


def _ffn_fwd(tag, x, norm, w_gu, w_down):
    h = _norm_fwd(tag + "_norm", x, norm)
    gu = _mmc_nn(tag + "_gu", h, w_gu, 0, 0, 2 * FF_PAD)

    def act(guv):
        return (jnp.concatenate([_f_act(*_gate_up(guv, j)) for j in range(N_DEV)], axis=1),)
    a = _rowwise(tag + "_act", act, [gu], [], [(FF_HID, BF16)])[0]
    wd = w_down(a)
    out = _mm(tag + "_down", a, wd, "nn", res=x, scale=0.5)
    return out, (h, gu, a), wd


def _ffn_bwd(tag, x, norm, w_gu, w_down, saved, dout, on_down=None):
    h, gu, a = saved
    d_wdown = _mm(tag + "_dwdown", a, dout, "tn", out_dtype=BF16, scale=0.5).reshape(N_DEV, FF_PAD, D_MODEL)
    token = on_down(d_wdown)[:1] if on_down is not None else jnp.zeros((1, LANES), F32)
    da = _mm(tag + "_dact", dout, w_down, "nt", scale=0.5)

    def act_bwd(guv, dav, _):
        outs = []
        for j in range(N_DEV):
            _, vjp = jax.vjp(_f_act, *_gate_up(guv, j))
            outs += list(vjp(dav[:, j * FF_PAD:(j + 1) * FF_PAD]))
        return (jnp.concatenate(outs, axis=1),)
    dgu = _rowwise(tag + "_dgu", act_bwd, [gu, da], [token], [(2 * FF_HID, BF16)])[0]
    dh = _mmc_nt(tag + "_dh", dgu, w_gu, 0, 0, 2 * FF_PAD, D_MODEL)
    d_wgu = _mmc_tn(tag + "_dwgu", h, dgu, 2 * FF_PAD)
    dx, dnorm = _norm_bwd(tag + "_dnorm", x, norm, dh, dout)
    return dx, dnorm, d_wgu, d_wdown


def _shift_fwd(z, mu):
    t, c = z.shape[0], RWKV_COLS
    tr = 256

    def body(z_ref, zp_ref, mu_ref, r_ref, k_ref, v_ref, lo_ref, gd_ref):
        zv = z_ref[...]
        prev = zp_ref[7:8, :] * jnp.where(pl.program_id(0) == 0, 0.0, 1.0)
        row = lax.broadcasted_iota(jnp.int32, zv.shape, 0)
        zsh = jnp.where(row == 0, prev, pltpu.roll(zv, 1, 0))
        zs = zv + (zsh - zv) * mu_ref[...]
        r_ref[...] = zs[:, 0:512]
        k_ref[...] = zs[:, 512:1024]
        v_ref[...] = zs[:, 1024:1536]
        lo_ref[...] = zs[:, 1536:1664]
        gd_ref[...] = zs[:, 1664:1792]

    widths = (512, 512, 512, 128, 128)
    return pl.pallas_call(
        body, name="rwkv_shift", grid=(t // tr,),
        in_specs=[pl.BlockSpec((tr, c), lambda i: (i, 0)),
                  pl.BlockSpec((8, c), lambda i: (jnp.maximum(i * (tr // 8) - 1, 0), 0)),
                  pl.BlockSpec((1, c), lambda i: (0, 0))],
        out_specs=[pl.BlockSpec((tr, w), lambda i: (i, 0)) for w in widths],
        out_shape=[jax.ShapeDtypeStruct((t, w), F32) for w in widths],
        compiler_params=_cparams(dimension_semantics=("parallel",)),
    )(z, z, mu)


def _shift_bwd(z, mu, dr, dk, dv, dlo, dgd):
    t, c = z.shape[0], RWKV_COLS
    tr = 256
    nt = t // tr

    def body(z_ref, zp_ref, mu_ref, dr_ref, dk_ref, dv_ref, dlo_ref, dgd_ref,
             drn_ref, dkn_ref, dvn_ref, dlon_ref, dgdn_ref, dz_ref, dmu_ref):
        i = pl.program_id(0)
        zv, muv = z_ref[...], mu_ref[...]
        prev = zp_ref[7:8, :] * jnp.where(i == 0, 0.0, 1.0)
        row = lax.broadcasted_iota(jnp.int32, zv.shape, 0)
        zsh = jnp.where(row == 0, prev, pltpu.roll(zv, 1, 0))
        dzs = jnp.concatenate([dr_ref[...], dk_ref[...], dv_ref[...], dlo_ref[...], dgd_ref[...]], axis=1)
        nxt = jnp.concatenate([drn_ref[0:1, :], dkn_ref[0:1, :], dvn_ref[0:1, :], dlon_ref[0:1, :],
                               dgdn_ref[0:1, :]], axis=1) * jnp.where(i == nt - 1, 0.0, 1.0)
        u = dzs * muv
        un = jnp.where(row == tr - 1, nxt * muv, pltpu.roll(u, tr - 1, 0))
        dz_ref[...] = (dzs - u + un).astype(dz_ref.dtype)
        _accumulate(dmu_ref, jnp.sum(dzs * (zsh - zv), axis=0, keepdims=True))

    widths = (512, 512, 512, 128, 128)
    nxt_map = lambda i: (jnp.minimum((i + 1) * (tr // 8), t // 8 - 1), 0)
    return pl.pallas_call(
        body, name="rwkv_shift_bwd", grid=(nt,),
        in_specs=[pl.BlockSpec((tr, c), lambda i: (i, 0)),
                  pl.BlockSpec((8, c), lambda i: (jnp.maximum(i * (tr // 8) - 1, 0), 0)),
                  pl.BlockSpec((1, c), lambda i: (0, 0))]
        + [pl.BlockSpec((tr, w), lambda i: (i, 0)) for w in widths]
        + [pl.BlockSpec((8, w), nxt_map) for w in widths],
        out_specs=[pl.BlockSpec((tr, c), lambda i: (i, 0)), pl.BlockSpec((1, c), lambda i: (0, 0))],
        out_shape=[jax.ShapeDtypeStruct((t, c), BF16), jax.ShapeDtypeStruct((1, c), F32)],
        compiler_params=_cparams(dimension_semantics=("arbitrary",)),
    )(z, z, mu, dr, dk, dv, dlo, dgd, dr, dk, dv, dlo, dgd)


def _f_pre(k, lo, gd, w0, w2p, a0, a2p, g2, k_k, k_a):
    lane = lax.broadcasted_iota(jnp.int32, lo.shape, 1)
    lo_act = jnp.where(lane < DECAY_LORA, jnp.tanh(lo), lo)
    w = -_softplus(-(w0 + _bdot(lo_act, w2p))) - 0.5
    a = _sigmoid(a0 + _bdot(lo_act, a2p))
    g = _bdot(_sigmoid(gd), g2)
    kk = k * k_k
    k2 = k * (1.0 + (a - 1.0) * k_a)
    decay = jnp.exp(-jnp.exp(w))
    return k2, kk, a, decay, g


def _f_kk(kk, a):
    kkn = kk * lax.rsqrt(jnp.maximum(jnp.sum(kk * kk, axis=-1, keepdims=True), 1e-24))
    return kkn, kkn * a


def _f_post(y, r, k2, v, g, gn_w, gn_b, r_k):
    reps = y.shape[0] // RWKV_HEADS
    mean = jnp.mean(y, axis=-1, keepdims=True)
    var = jnp.mean(jnp.square(y - mean), axis=-1, keepdims=True)
    yn = (y - mean) * lax.rsqrt(var + GN_EPS)
    yn = yn * jnp.tile(gn_w, (reps, 1)) + jnp.tile(gn_b, (reps, 1))
    bonus = jnp.sum(r * k2 * jnp.tile(r_k, (reps, 1)), axis=-1, keepdims=True) * v
    return (yn + bonus) * g


def _to_v_rows(x):
    t = x.shape[0]
    return x.reshape(t, RWKV_HEADS, V_HI, V_LO).transpose(0, 2, 3, 1).reshape(t, V_HI, LANES)


def _from_v_rows(x):
    t = x.shape[0]
    return x.reshape(t, V_HI, V_LO, RWKV_HEADS).transpose(0, 3, 1, 2).reshape(t, RWKV_DIM)


def _k_cols(x):
    return jnp.tile(x, (V_LO, 1)).T


def _k_rows(x):
    xt = x.T
    out = xt[0:RWKV_HEADS]
    for l in range(1, V_LO):
        out = out + xt[l * RWKV_HEADS:(l + 1) * RWKV_HEADS]
    return out


def _wkv_fwd(r, w, k, kk, b, v):
    t = r.shape[0]
    tc = SCAN_CHUNK
    key_spec = pl.BlockSpec((tc, RWKV_HEADS, HEAD_DIM), lambda i: (i, 0, 0))
    row_spec = pl.BlockSpec((tc, V_HI, LANES), lambda i: (i, 0, 0))

    def body(r_ref, w_ref, k_ref, kk_ref, b_ref, v_ref, y_ref, st_ref, s_scr, cols_a, cols_b):
        @pl.when(pl.program_id(0) == 0)
        def _():
            s_scr[...] = jnp.zeros_like(s_scr)

        def prep(ti, buf):
            for n, ref in enumerate((r_ref, w_ref, k_ref, kk_ref, b_ref)):
                buf[n] = _k_cols(ref[ti])

        def step(ti, s, cur, nxt, ti_next):
            rc, wc, kc, kkc, bc = (cur[n] for n in range(5))
            prep(ti_next, nxt)
            vt = v_ref[ti]
            new, ys = [], []
            for j in range(V_HI):
                sa = -jnp.sum(s[j] * kkc, axis=0, keepdims=True)
                nj = s[j] * wc + bc * sa + kc * vt[j:j + 1]
                st_ref[ti, j] = nj
                ys.append(jnp.sum(nj * rc, axis=0, keepdims=True))
                new.append(nj)
            y_ref[ti] = jnp.concatenate(ys, axis=0)
            return tuple(new)

        def pair(i, s):
            s = step(2 * i, s, cols_a, cols_b, 2 * i + 1)
            return step(2 * i + 1, s, cols_b, cols_a, jnp.minimum(2 * i + 2, tc - 1))

        prep(0, cols_a)
        s = lax.fori_loop(0, tc // 2, pair, tuple(s_scr[j] for j in range(V_HI)))
        for j in range(V_HI):
            s_scr[j] = s[j]

    return pl.pallas_call(
        body, name="wkv_fwd", grid=(t // tc,),
        in_specs=[key_spec] * 5 + [row_spec],
        out_specs=[row_spec, pl.BlockSpec((tc, V_HI, HEAD_DIM, LANES), lambda i: (i, 0, 0, 0))],
        out_shape=[jax.ShapeDtypeStruct((t, V_HI, LANES), F32),
                   jax.ShapeDtypeStruct((t, V_HI, HEAD_DIM, LANES), F32)],
        scratch_shapes=[pltpu.VMEM((V_HI, HEAD_DIM, LANES), F32)] + [pltpu.VMEM((5, HEAD_DIM, LANES), F32)] * 2,
        compiler_params=_cparams(dimension_semantics=("arbitrary",)),
    )(r, w, k, kk, b, v)


def _wkv_bwd(r, w, k, kk, b, v, states, dy):
    t = r.shape[0]
    tc = SCAN_CHUNK
    nb = t // tc
    key_spec = pl.BlockSpec((tc, RWKV_HEADS, HEAD_DIM), lambda i: (nb - 1 - i, 0, 0))
    row_spec = pl.BlockSpec((tc, V_HI, LANES), lambda i: (nb - 1 - i, 0, 0))
    st_spec = pl.BlockSpec((tc, V_HI, HEAD_DIM, LANES), lambda i: (nb - 1 - i, 0, 0, 0))
    stp_spec = pl.BlockSpec((1, V_HI, HEAD_DIM, LANES), lambda i: (jnp.maximum((nb - 1 - i) * tc - 1, 0), 0, 0, 0))

    def body(r_ref, w_ref, k_ref, kk_ref, b_ref, v_ref, st_ref, stp_ref, dy_ref,
             dr_ref, dw_ref, dk_ref, dkk_ref, db_ref, dv_ref, ds_scr, cols_a, cols_b, accs_a, accs_b):
        @pl.when(pl.program_id(0) == 0)
        def _():
            ds_scr[...] = jnp.zeros_like(ds_scr)

        def colsum(x):
            return jnp.sum(x, axis=0, keepdims=True)

        def prep(ti, buf):
            for n, ref in enumerate((r_ref, w_ref, k_ref, kk_ref, b_ref)):
                buf[n] = _k_cols(ref[ti])

        def flush(ti, buf):
            for n, ref in enumerate((dr_ref, dk_ref, db_ref, dw_ref, dkk_ref)):
                ref[ti] = _k_rows(buf[n])

        def step(ti, ds, sp, cur, accs):
            rc, wc, kc, kkc, bc = (cur[n] for n in range(5))
            vt, dyt = v_ref[ti], dy_ref[ti]
            acc = None
            new, dvs = [], []
            for j in range(V_HI):
                st = st_ref[ti, j]
                dsj = ds[j] + rc * dyt[j:j + 1]
                sa = -colsum(sp[j] * kkc)
                dsa = colsum(dsj * bc)
                dvs.append(colsum(dsj * kc))
                parts = (st * dyt[j:j + 1], dsj * vt[j:j + 1], dsj * sa, dsj * sp[j], -(sp[j] * dsa))
                acc = parts if acc is None else tuple(a + q for a, q in zip(acc, parts))
                new.append(dsj * wc - kkc * dsa)
            dv_ref[ti] = jnp.concatenate(dvs, axis=0)
            for n in range(5):
                accs[n] = acc[n]
            return tuple(new)

        def states_before(ti):
            return tuple(st_ref[ti - 1, j] for j in range(V_HI))

        def pair(i, ds):
            ta = tc - 1 - 2 * i
            prep(ta - 1, cols_b)
            flush(jnp.minimum(ta + 1, tc - 1), accs_b)
            ds = step(ta, ds, states_before(ta), cols_a, accs_a)
            prep(ta - 2, cols_a)
            flush(ta, accs_a)
            return step(ta - 1, ds, states_before(ta - 1), cols_b, accs_b)

        prep(tc - 1, cols_a)
        accs_b[...] = jnp.zeros_like(accs_b)
        ds = lax.fori_loop(0, tc // 2 - 1, pair, tuple(ds_scr[j] for j in range(V_HI)))
        prep(0, cols_b)
        flush(2, accs_b)
        ds = step(1, ds, states_before(1), cols_a, accs_a)
        flush(1, accs_a)
        keep = jnp.where(pl.program_id(0) == nb - 1, 0.0, 1.0)
        ds = step(0, ds, tuple(stp_ref[0, j] * keep for j in range(V_HI)), cols_b, accs_b)
        flush(0, accs_b)
        for j in range(V_HI):
            ds_scr[j] = ds[j]

    key_out = jax.ShapeDtypeStruct((t, RWKV_HEADS, HEAD_DIM), F32)
    return pl.pallas_call(
        body, name="wkv_bwd", grid=(nb,),
        in_specs=[key_spec] * 5 + [row_spec, st_spec, stp_spec, row_spec],
        out_specs=[key_spec] * 5 + [row_spec],
        out_shape=[key_out] * 5 + [jax.ShapeDtypeStruct((t, V_HI, LANES), F32)],
        scratch_shapes=[pltpu.VMEM((V_HI, HEAD_DIM, LANES), F32)] + [pltpu.VMEM((5, HEAD_DIM, LANES), F32)] * 4,
        compiler_params=_cparams(dimension_semantics=("arbitrary",)),
    )(r, w, k, kk, b, v, states, states, dy)


PAIR = 2 * HEAD_DIM
N_PAIRS = ATTN_HEADS // 2
Q_COL0 = RWKV_COLS // PAIR
K_COL0 = Q_COL0 + N_PAIRS
V_COL0 = K_COL0 + N_PAIRS


def _swap_halves(x):
    lane = lax.broadcasted_iota(jnp.int32, x.shape, 1)
    return jnp.where((lane & (HEAD_DIM - 1)) < HEAD_DIM // 2, pltpu.roll(x, PAIR - HEAD_DIM // 2, 1),
                     pltpu.roll(x, HEAD_DIM // 2, 1))


@jax.custom_vjp
def _rope(x, cosf, sinf):
    return x * cosf + _swap_halves(x) * sinf


def _rope_fwd(x, cosf, sinf):
    return _rope(x, cosf, sinf), (cosf, sinf)


def _rope_bwd(res, d):
    cosf, sinf = res
    return d * cosf + _swap_halves(d * sinf), jnp.zeros_like(cosf), jnp.zeros_like(sinf)


_rope.defvjp(_rope_fwd, _rope_bwd)


def _head_sums(x):
    lane = lax.broadcasted_iota(jnp.int32, x.shape, 1)
    lo = jnp.where(lane < HEAD_DIM, 1.0, 0.0)
    hi = 1.0 - lo
    return lo * jnp.sum(x * lo, axis=1, keepdims=True) + hi * jnp.sum(x * hi, axis=1, keepdims=True)


def _f_qk(x, cosf, sinf, gain2):
    xn = x * lax.rsqrt(_head_sums(x * x) * (1.0 / HEAD_DIM) + RMS_EPS) * gain2
    return _rope(xn, cosf, sinf)


def _qk_prep(z, tab, q_gain, k_gain):
    t = z.shape[0]
    tr = 1024

    def body(z_ref, c_ref, s_ref, qg_ref, kg_ref, o_ref):
        g = jnp.where(pl.program_id(0) < N_PAIRS, qg_ref[...], kg_ref[...])
        o_ref[...] = _f_qk(z_ref[...], c_ref[...], s_ref[...], jnp.concatenate([g, g], axis=1))

    gain = pl.BlockSpec((1, HEAD_DIM), lambda c, i: (0, 0))
    return pl.pallas_call(
        body, name="qk_prep", grid=(2 * N_PAIRS, t // tr),
        in_specs=[pl.BlockSpec((tr, PAIR), lambda c, i: (i, Q_COL0 + c)), pl.BlockSpec((tr, PAIR), lambda c, i: (i, 0)),
                  pl.BlockSpec((tr, PAIR), lambda c, i: (i, 1)), gain, gain],
        out_specs=pl.BlockSpec((tr, PAIR), lambda c, i: (i, c)),
        out_shape=jax.ShapeDtypeStruct((t, 2 * N_PAIRS * PAIR), F32),
        compiler_params=_cparams(dimension_semantics=("parallel", "parallel")),
    )(z, tab, tab, q_gain, k_gain)


def _qk_prep_bwd(z, tab, q_gain, k_gain, dq, dk):
    t = z.shape[0]
    tr = 1024

    def body(z_ref, c_ref, s_ref, qg_ref, kg_ref, dq_ref, dk_ref, dz_ref, dqg_ref, dkg_ref):
        c, i = pl.program_id(0), pl.program_id(1)
        is_q = c < N_PAIRS
        g = jnp.where(is_q, qg_ref[...], kg_ref[...])
        d = jnp.where(is_q, dq_ref[...], dk_ref[...])
        _, vjp = jax.vjp(lambda xx, gg: _f_qk(xx, c_ref[...], s_ref[...], gg), z_ref[...],
                         jnp.concatenate([g, g], axis=1))
        dx, dg2 = vjp(d)
        dz_ref[...] = dx.astype(dz_ref.dtype)
        dg = dg2[:, :HEAD_DIM] + dg2[:, HEAD_DIM:]
        first_q = jnp.logical_and(c == 0, i == 0)
        first_k = jnp.logical_and(c == N_PAIRS, i == 0)

        @pl.when(first_q)
        def _():
            dqg_ref[...] = dg

        @pl.when(jnp.logical_and(is_q, jnp.logical_not(first_q)))
        def _():
            dqg_ref[...] += dg

        @pl.when(first_k)
        def _():
            dkg_ref[...] = dg

        @pl.when(jnp.logical_and(jnp.logical_not(is_q), jnp.logical_not(first_k)))
        def _():
            dkg_ref[...] += dg

    gain = pl.BlockSpec((1, HEAD_DIM), lambda c, i: (0, 0))
    return pl.pallas_call(
        body, name="qk_prep_bwd", grid=(2 * N_PAIRS, t // tr),
        in_specs=[pl.BlockSpec((tr, PAIR), lambda c, i: (i, Q_COL0 + c)), pl.BlockSpec((tr, PAIR), lambda c, i: (i, 0)),
                  pl.BlockSpec((tr, PAIR), lambda c, i: (i, 1)), gain, gain,
                  pl.BlockSpec((tr, PAIR), lambda c, i: (i, jnp.minimum(c, N_PAIRS - 1))),
                  pl.BlockSpec((tr, PAIR), lambda c, i: (i, jnp.maximum(c - N_PAIRS, 0)))],
        out_specs=[pl.BlockSpec((tr, PAIR), lambda c, i: (i, c)), gain, gain],
        out_shape=[jax.ShapeDtypeStruct((t, 2 * N_PAIRS * PAIR), BF16), jax.ShapeDtypeStruct((1, HEAD_DIM), F32),
                   jax.ShapeDtypeStruct((1, HEAD_DIM), F32)],
        compiler_params=_cparams(dimension_semantics=("arbitrary", "arbitrary")),
    )(z, tab, tab, q_gain, k_gain, dq, dk)


def _attn_block(q, kp, kc, vp, vc, kmin):
    k2 = jnp.concatenate([kp, kc], axis=0)
    v2 = jnp.concatenate([vp, vc], axis=0)
    s = _bdot_nt(q, k2) * (HEAD_DIM ** -0.5)
    qi = lax.broadcasted_iota(jnp.int32, s.shape, 0)
    kj = lax.broadcasted_iota(jnp.int32, s.shape, 1)
    dist = qi + BAND_BLOCK - kj
    valid = (dist >= 0) & (dist <= BAND_BLOCK) & (kj >= kmin)
    s = jnp.where(valid, s, NEG_INF)
    m = lax.stop_gradient(jnp.max(s, axis=-1, keepdims=True))
    e = jnp.exp(s - m)
    l = jnp.sum(e, axis=-1, keepdims=True)
    o = _bdot(e, v2) / l
    return o, m + jnp.log(l)


def _fold(src_ref, dst_ref, dil):
    t = src_ref.shape[0]
    ln = t // dil
    for j in range(dil):
        dst_ref[j * ln:(j + 1) * ln, :] = src_ref[pl.ds(j, ln, stride=dil), :]


def _unfold(src_ref, dst_ref, dil):
    t = src_ref.shape[0]
    ln = t // dil
    for j in range(dil):
        dst_ref[pl.ds(j, ln, stride=dil), :] = src_ref[j * ln:(j + 1) * ln, :]


def _per_group(fn):
    pair = pl.program_id(0)
    for gi, (_, dil) in enumerate(ATTN_GROUPS):
        @pl.when(jnp.logical_or(pair == 2 * gi, pair == 2 * gi + 1))
        def _(dil=dil):
            fn(dil)


def _block_rows(idx, blocks_per_seq):
    first = (idx & (blocks_per_seq - 1)) == 0
    cur = pl.ds(pl.multiple_of(idx * BAND_BLOCK, BAND_BLOCK), BAND_BLOCK)
    prev = pl.ds(pl.multiple_of(jnp.maximum(idx - 1, 0) * BAND_BLOCK, BAND_BLOCK), BAND_BLOCK)
    return first, cur, prev


def _heads(x):
    return x[:, :HEAD_DIM], x[:, HEAD_DIM:]


def _attn_fwd(qk, z):
    t = z.shape[0]
    n_blocks = t // BAND_BLOCK

    def body(q_ref, k_ref, v_ref, o_ref, lse_ref, qf, kf, vf, of, lf):
        def run(dil):
            _fold(q_ref, qf, dil)
            _fold(k_ref, kf, dil)
            _fold(v_ref, vf, dil)
            blocks_per_seq = n_blocks // dil

            def block(idx, carry):
                first, cur, prev = _block_rows(idx, blocks_per_seq)
                kmin = jnp.where(first, BAND_BLOCK, 0)
                outs, lses = [], []
                for q, kp, kc, vp, vc in zip(_heads(qf[cur, :]), _heads(kf[prev, :]), _heads(kf[cur, :]),
                                             _heads(vf[prev, :]), _heads(vf[cur, :])):
                    o, ls = _attn_block(q, kp, kc, vp, vc, kmin)
                    outs.append(o)
                    lses.append(jnp.broadcast_to(ls, o.shape))
                of[cur, :] = jnp.concatenate(outs, axis=1)
                lf[cur, :] = jnp.concatenate(lses, axis=1)
                return carry

            lax.fori_loop(0, n_blocks, block, 0)
            _unfold(of, o_ref, dil)
            _unfold(lf, lse_ref, dil)

        _per_group(run)

    slab = jax.ShapeDtypeStruct((t, N_PAIRS * PAIR), F32)
    out_spec = pl.BlockSpec((t, PAIR), lambda p: (0, p))
    return pl.pallas_call(
        body, name="attn_fwd", grid=(N_PAIRS,),
        in_specs=[pl.BlockSpec((t, PAIR), lambda p: (0, p)), pl.BlockSpec((t, PAIR), lambda p: (0, N_PAIRS + p)),
                  pl.BlockSpec((t, PAIR), lambda p: (0, V_COL0 + p))],
        out_specs=[out_spec, out_spec], out_shape=[slab, slab],
        scratch_shapes=[pltpu.VMEM((t, PAIR), F32)] * 5,
        compiler_params=_cparams(dimension_semantics=("parallel",)),
    )(qk, qk, z)


def _attn_bwd(qk, z, do, dlse):
    t = z.shape[0]
    n_blocks = t // BAND_BLOCK

    def body(q_ref, k_ref, v_ref, do_ref, dl_ref, dq_ref, dk_ref, dv_ref, qf, kf, vf, dof, dlf, dqf, dkf, dvf):
        def run(dil):
            for src, dst in ((q_ref, qf), (k_ref, kf), (v_ref, vf), (do_ref, dof), (dl_ref, dlf)):
                _fold(src, dst, dil)
            blocks_per_seq = n_blocks // dil

            def block(idx, carry):
                first, cur, prev = _block_rows(idx, blocks_per_seq)
                kmin = jnp.where(first, BAND_BLOCK, 0)
                grads = []
                for q, kp, kc, vp, vc, do_h, dl_h in zip(
                        _heads(qf[cur, :]), _heads(kf[prev, :]), _heads(kf[cur, :]), _heads(vf[prev, :]),
                        _heads(vf[cur, :]), _heads(dof[cur, :]), _heads(dlf[cur, :])):
                    _, vjp = jax.vjp(functools.partial(_attn_block, kmin=kmin), q, kp, kc, vp, vc)
                    grads.append(vjp((do_h, jnp.sum(dl_h, axis=1, keepdims=True))))
                dq, dkp, dkc, dvp, dvc = (jnp.concatenate([a, b], axis=1) for a, b in zip(*grads))
                dqf[cur, :] = dq
                dkf[cur, :] = dkc
                dvf[cur, :] = dvc

                @pl.when(jnp.logical_not(first))
                def _():
                    dkf[prev, :] += dkp
                    dvf[prev, :] += dvp

                return carry

            lax.fori_loop(0, n_blocks, block, 0)
            _unfold(dqf, dq_ref, dil)
            _unfold(dkf, dk_ref, dil)
            _unfold(dvf, dv_ref, dil)

        _per_group(run)

    slab = jax.ShapeDtypeStruct((t, N_PAIRS * PAIR), F32)
    own = pl.BlockSpec((t, PAIR), lambda p: (0, p))
    return pl.pallas_call(
        body, name="attn_bwd", grid=(N_PAIRS,),
        in_specs=[own, pl.BlockSpec((t, PAIR), lambda p: (0, N_PAIRS + p)),
                  pl.BlockSpec((t, PAIR), lambda p: (0, V_COL0 + p)), own, own],
        out_specs=[own] * 3, out_shape=[slab] * 3,
        scratch_shapes=[pltpu.VMEM((t, PAIR), F32)] * 8,
        compiler_params=_cparams(dimension_semantics=("parallel",)),
    )(qk, qk, z, do, dlse)


def _f_comb(o1, o2, o3, l1, l2, l3):
    m = jnp.maximum(jnp.maximum(l1, l2), l3)
    e1, e2, e3 = jnp.exp(l1 - m), jnp.exp(l2 - m), jnp.exp(l3 - m)
    den = e1 + e2 + e3
    return (e1 / den) * o1 + (e2 / den) * o2 + (e3 / den) * o3


def _all_gather_hbm(name, arrs):
    na = len(arrs)

    def body(*refs):
        x_refs, out_refs = refs[:na], refs[na:2 * na]
        send_sems, recv_sems, local_sems = refs[2 * na:]
        mx, my, mc = lax.axis_index("x"), lax.axis_index("y"), lax.axis_index("c")
        me, sibling = (mx, my, mc), (mx, my, 1 - mc)
        chips = [(1 - mx, my), (mx, 1 - my), (1 - mx, 1 - my)]

        def slot(a, px, py, pc):
            return out_refs[a].at[4 * px + 2 * py + pc]

        def copy(a, k, block, to, src=None):
            return pltpu.make_async_remote_copy(
                src_ref=slot(a, *block) if src is None else src, dst_ref=slot(a, *block),
                send_sem=send_sems.at[a, k], recv_sem=recv_sems.at[a, k], device_id=to, device_id_type=MESH)

        mine = [pltpu.make_async_copy(x_refs[a], slot(a, *me), local_sems.at[a]) for a in range(na)]
        for cp in mine:
            cp.start()
        first = []
        for a in range(na):
            first.append(copy(a, 0, me, sibling, src=x_refs[a]))
            first += [copy(a, 1 + j, me, (*chip, mc), src=x_refs[a]) for j, chip in enumerate(chips)]
        for cp in first:
            cp.start()
        passed = []
        for j, chip in enumerate(chips):
            for a in range(na):
                copy(a, 1 + j, (*chip, mc), me).wait_recv()
                passed.append(copy(a, 4 + j, (*chip, mc), sibling))
                passed[-1].start()
        for a in range(na):
            copy(a, 0, sibling, me).wait_recv()
            for j, chip in enumerate(chips):
                copy(a, 4 + j, (*chip, 1 - mc), me).wait_recv()
        for cp in first + passed:
            cp.wait_send()
        for cp in mine:
            cp.wait()

    hbm = pl.BlockSpec(memory_space=pl.ANY)
    return pl.pallas_call(
        body, name=name,
        out_shape=[jax.ShapeDtypeStruct((N_DEV,) + a.shape, a.dtype) for a in arrs],
        in_specs=[hbm] * na, out_specs=[hbm] * na,
        scratch_shapes=[pltpu.SemaphoreType.DMA((na, 7)), pltpu.SemaphoreType.DMA((na, 7)),
                        pltpu.SemaphoreType.DMA((na,))],
    )(*arrs)


def _all_gather_vmem(x):
    rws, cols = x.shape

    def body(x_ref, out_ref, send_sems, recv_sems):
        mx, my, mc = lax.axis_index("x"), lax.axis_index("y"), lax.axis_index("c")
        me, sibling = (mx, my, mc), (mx, my, 1 - mc)
        chips = [(1 - mx, my), (mx, 1 - my), (1 - mx, 1 - my)]

        def slot(px, py, pc):
            return out_ref.at[4 * px + 2 * py + pc]

        def copy(k, block, to, src=None):
            return pltpu.make_async_remote_copy(
                src_ref=slot(*block) if src is None else src, dst_ref=slot(*block),
                send_sem=send_sems.at[k], recv_sem=recv_sems.at[k], device_id=to, device_id_type=MESH)

        first = [copy(0, me, sibling, src=x_ref)]
        first += [copy(1 + j, me, (*chip, mc), src=x_ref) for j, chip in enumerate(chips)]
        for cp in first:
            cp.start()
        out_ref[4 * mx + 2 * my + mc] = x_ref[...]
        passed = [copy(4 + j, (*chip, mc), sibling) for j, chip in enumerate(chips)]
        for j, chip in enumerate(chips):
            copy(1 + j, (*chip, mc), me).wait_recv()
            passed[j].start()
        copy(0, sibling, me).wait_recv()
        for j, chip in enumerate(chips):
            copy(4 + j, (*chip, 1 - mc), me).wait_recv()
        for cp in first + passed:
            cp.wait_send()

    return pl.pallas_call(
        body, name="all_gather_small",
        out_shape=jax.ShapeDtypeStruct((N_DEV, rws, cols), x.dtype),
        in_specs=[pl.BlockSpec(memory_space=pltpu.VMEM)], out_specs=pl.BlockSpec(memory_space=pltpu.VMEM),
        scratch_shapes=[pltpu.SemaphoreType.DMA((7,)), pltpu.SemaphoreType.DMA((7,))],
    )(x)


def _all_to_all_hbm(name, arrs):
    na = len(arrs)

    def body(*refs):
        g_refs, out_refs = refs[:na], refs[na:2 * na]
        send_sems, recv_sems, local_sems = refs[2 * na:]
        mx, my, mc = lax.axis_index("x"), lax.axis_index("y"), lax.axis_index("c")
        me = 4 * mx + 2 * my + mc
        mine = [pltpu.make_async_copy(g_refs[a].at[me], out_refs[a].at[me], local_sems.at[a]) for a in range(na)]
        for cp in mine:
            cp.start()
        copies = []
        for k in range(1, N_DEV):
            px, py, pc = mx ^ (k >> 2), my ^ ((k >> 1) & 1), mc ^ (k & 1)
            peer = 4 * px + 2 * py + pc
            for a in range(na):
                copies.append(pltpu.make_async_remote_copy(
                    src_ref=g_refs[a].at[peer], dst_ref=out_refs[a].at[me], send_sem=send_sems.at[a, k - 1],
                    recv_sem=recv_sems.at[a, k - 1], device_id=(px, py, pc), device_id_type=MESH))
        for cp in copies:
            cp.start()
        for cp in copies:
            cp.wait_recv()
        for cp in copies:
            cp.wait_send()
        for cp in mine:
            cp.wait()

    hbm = pl.BlockSpec(memory_space=pl.ANY)
    return pl.pallas_call(
        body, name=name,
        out_shape=[jax.ShapeDtypeStruct(a.shape, a.dtype) for a in arrs],
        in_specs=[hbm] * na, out_specs=[hbm] * na,
        scratch_shapes=[pltpu.SemaphoreType.DMA((na, 7)), pltpu.SemaphoreType.DMA((na, 7)),
                        pltpu.SemaphoreType.DMA((na,))],
    )(*arrs)


def _scatter_copies(g_refs, land_refs, send_sems, recv_sems):
    mx, my, mc = lax.axis_index("x"), lax.axis_index("y"), lax.axis_index("c")
    me = 4 * mx + 2 * my + mc
    copies = []
    for k in range(1, N_DEV):
        px, py, pc = mx ^ (k >> 2), my ^ ((k >> 1) & 1), mc ^ (k & 1)
        peer = 4 * px + 2 * py + pc
        for a, (g_ref, land_ref) in enumerate(zip(g_refs, land_refs)):
            copies.append(pltpu.make_async_remote_copy(
                src_ref=g_ref.at[peer], dst_ref=land_ref.at[me], send_sem=send_sems.at[a * (N_DEV - 1) + k - 1],
                recv_sem=recv_sems.at[a * (N_DEV - 1) + k - 1], device_id=(px, py, pc), device_id_type=MESH))
    return copies


_HBM = pl.BlockSpec(memory_space=pltpu.HBM)
_SEM = pl.BlockSpec(memory_space=pltpu.SEMAPHORE)
_DATAFLOW = pltpu.SideEffectType.DATAFLOW_SIDE_EFFECTING


def _gather_copies(x_refs, land_refs, send_sems, recv_sems):
    mx, my, mc = lax.axis_index("x"), lax.axis_index("y"), lax.axis_index("c")
    me = 4 * mx + 2 * my + mc
    copies = []
    for k in range(1, N_DEV):
        px, py, pc = mx ^ (k >> 2), my ^ ((k >> 1) & 1), mc ^ (k & 1)
        for a, (x_ref, land_ref) in enumerate(zip(x_refs, land_refs)):
            copies.append(pltpu.make_async_remote_copy(
                src_ref=x_ref, dst_ref=land_ref.at[me], send_sem=send_sems.at[a * (N_DEV - 1) + k - 1],
                recv_sem=recv_sems.at[a * (N_DEV - 1) + k - 1], device_id=(px, py, pc), device_id_type=MESH))
    return copies


def _exchange_start(name, copies, srcs, lands, after):
    na = len(srcs)

    def body(*refs):
        for cp in copies(refs[:na], refs[na:2 * na], refs[2 * na + 1], refs[2 * na + 2]):
            cp.start()
        refs[-1][...] = jnp.zeros_like(refs[-1])

    in_hbm = lambda a: pltpu.with_memory_space_constraint(a, pltpu.HBM)
    outs = pl.pallas_call(
        body, name=name,
        out_shape=(pltpu.SemaphoreType.DMA((na * (N_DEV - 1),)), pltpu.SemaphoreType.DMA((na * (N_DEV - 1),)),
                   *[pltpu.HBM(a.shape, a.dtype) for a in list(srcs) + list(lands)],
                   jax.ShapeDtypeStruct((8, LANES), F32)),
        in_specs=[_HBM] * (2 * na) + [pl.BlockSpec(memory_space=pl.ANY)],
        out_specs=(_SEM, _SEM, *[_HBM] * (2 * na), pl.BlockSpec(memory_space=pltpu.VMEM)),
        input_output_aliases={i: 2 + i for i in range(2 * na)},
        compiler_params=pltpu.CompilerParams(has_side_effects=_DATAFLOW),
    )(*[in_hbm(a) for a in srcs], *[in_hbm(a) for a in lands], after)
    return outs[0], outs[1], outs[2:2 + na], outs[2 + na:2 + 2 * na], outs[-1]


def _exchange_wait(name, copies, send_sems, recv_sems, srcs, lands, after):
    na = len(srcs)

    def body(*refs):
        for cp in copies(refs[:na], refs[na:2 * na], refs[2 * na], refs[2 * na + 1]):
            cp.wait_send()
            cp.wait_recv()

    outs = pl.pallas_call(
        body, name=name,
        out_shape=[pltpu.HBM(a.shape, a.dtype) for a in list(srcs) + list(lands)],
        in_specs=[_HBM] * (2 * na) + [_SEM, _SEM, pl.BlockSpec(memory_space=pl.ANY)], out_specs=[_HBM] * (2 * na),
        input_output_aliases={i: i for i in range(2 * na)},
        compiler_params=pltpu.CompilerParams(has_side_effects=_DATAFLOW),
    )(*srcs, *lands, send_sems, recv_sems, after)
    return outs[:na], outs[na:]


def _sum_slots(name, g, tr):
    _, rws, cols = g.shape

    def body(g_ref, o_ref):
        acc = g_ref[0].astype(F32)
        for j in range(1, N_DEV):
            acc = acc + g_ref[j].astype(F32)
        o_ref[...] = acc

    return pl.pallas_call(
        body, name=name, grid=(rws // tr,),
        in_specs=[pl.BlockSpec((N_DEV, tr, cols), lambda i: (0, i, 0))],
        out_specs=pl.BlockSpec((tr, cols), lambda i: (i, 0)),
        out_shape=jax.ShapeDtypeStruct((rws, cols), F32),
        compiler_params=_cparams(dimension_semantics=("parallel",)),
    )(g)


def _adam_math(wv, gv, mv, vv):
    mn = ADAM_B1 * mv + (1.0 - ADAM_B1) * gv
    vn = ADAM_B2 * vv + (1.0 - ADAM_B2) * jnp.square(gv)
    m_hat = mn / (1.0 - ADAM_B1 ** ADAM_STEP)
    v_hat = vn / (1.0 - ADAM_B2 ** ADAM_STEP)
    delta = -ADAM_LR * (m_hat / (jnp.sqrt(v_hat) + ADAM_EPS) + ADAM_WD * wv)
    return delta, mn, vn


def _adamw(name, w, g, m, v, tr):
    return _rowwise(name, _adam_math, [w, g, m, v], [], [(LANES, F32)] * 3, tr=tr)


def _adamw_slots(name, recv, own, rb, cb, cw, w, m, v, tr, order):
    nr, nc = w.shape

    def body(g_ref, own_ref, w_ref, m_ref, v_ref, order_ref, go_ref, d_ref, mo_ref, vo_ref):
        me = 4 * lax.axis_index("x") + 2 * lax.axis_index("y") + lax.axis_index("c")
        acc = None
        for s in range(N_DEV):
            part = jnp.where(me == s, own_ref[s], g_ref[s]).astype(F32)
            acc = part if acc is None else acc + part
        g = acc[:, :nc]
        go_ref[...] = g
        d_ref[...], mo_ref[...], vo_ref[...] = _adam_math(w_ref[...], g, m_ref[...], v_ref[...])

    nat = pl.BlockSpec((tr, nc), lambda i: (i, 0))
    slots = pl.BlockSpec((N_DEV, tr, cw), lambda i: (0, rb + i, cb))
    return pl.pallas_call(
        body, name=name, grid=(nr // tr,),
        in_specs=[slots, slots, nat, nat, nat, pl.BlockSpec(memory_space=pl.ANY)],
        out_specs=[nat] * 4, out_shape=[jax.ShapeDtypeStruct((nr, nc), F32)] * 4,
        compiler_params=_cparams(dimension_semantics=("parallel",)),
    )(recv, own, w, m, v, order)


def _local_blocks(w):
    pad_cols = lambda a: jnp.pad(a, ((0, 0), (0, FF_PAD - FF_SHARD)))
    pad_rows = lambda a: jnp.pad(a, ((0, FF_PAD - FF_SHARD), (0, 0)))
    gate_up = lambda tag: jnp.concatenate([pad_cols(w[tag + "_w_gate"]), pad_cols(w[tag + "_w_up"])], axis=1)
    blocks = {
        "ffn1_gu": gate_up("ffn1"), "ffn1_d": pad_rows(w["ffn1_w_down"]), "w_in": w["w_in"],
        "lora": jnp.concatenate([w["rwkv_w2"], w["rwkv_a2"], w["rwkv_g2"]], axis=0),
        "br": jnp.concatenate([w["w_br_rwkv"], w["w_br_attn"], w["ple_w_proj"]], axis=0),
        "w_out": w["w_out"], "ffn2_gu": gate_up("ffn2"), "ffn2_d": pad_rows(w["ffn2_w_down"]),
        "ple_gate": w["ple_w_gate"],
    }
    return {n: a.astype(BF16) for n, a in blocks.items()}


GATHER_GROUPS = {"head": ("ffn1_gu",), "down": ("ffn1_d",), "mid": ("w_in", "lora"),
                 "rest": ("br", "w_out", "ffn2_gu", "ffn2_d", "ple_gate")}

SCATTER_GROUPS = {"tail": ("ple_gate", "ple_proj", "ffn2_gu", "ffn2_d"), "mixer": ("w_out", "br", "lora", "w_in"),
                  "ffn1_down": ("ffn1_d",), "head": ("ffn1_gu",)}

ADAM_PLAN = (
    ("ffn1_w_gate", "ffn1_gu", 0, 0, FF_PAD, 256), ("ffn1_w_up", "ffn1_gu", 0, 1, FF_PAD, 256),
    ("ffn1_w_down", "ffn1_d", 0, 0, D_MODEL, FF_SHARD // 2), ("w_in", "w_in", 0, 0, IN_SHARD, 256),
    ("rwkv_w2", "lora", 0, 0, HEAD_DIM, 64), ("rwkv_a2", "lora", 1, 0, HEAD_DIM, 64),
    ("rwkv_g2", "lora", 2, 0, HEAD_DIM, 64),
    ("w_br_rwkv", "br", 0, 0, OUT_SHARD, 256), ("w_br_attn", "br", 2, 0, OUT_SHARD, 256),
    ("ple_w_proj", "ple_proj", 0, 0, OUT_SHARD, 256), ("w_out", "w_out", 0, 0, D_MODEL, OUT_SHARD),
    ("ffn2_w_gate", "ffn2_gu", 0, 0, FF_PAD, 256), ("ffn2_w_up", "ffn2_gu", 0, 1, FF_PAD, 256),
    ("ffn2_w_down", "ffn2_d", 0, 0, D_MODEL, FF_SHARD // 2), ("ple_w_gate", "ple_gate", 0, 0, D_MODEL, OUT_SHARD),
)


def _pack_small(arrs, rows):
    flat = jnp.concatenate([a.reshape(-1) for a in arrs])
    return jnp.pad(flat, (0, rows * LANES - flat.shape[0])).reshape(rows, LANES)


def _unpack_small(flat, like):
    flat = flat.reshape(-1)
    out, off = [], 0
    for a in like:
        out.append(flat[off:off + a.size].reshape(a.shape))
        off += a.size
    return out


def _local_step(x, p, pos, target, sm, wg, fetch, on_grads):
    t = x.shape[0]
    hv = t * RWKV_HEADS

    x1, ffn1_saved, w_d1 = _ffn_fwd("ffn1", x, sm["ffn1_norm"], wg["ffn1_gu"],
                                    lambda after: fetch("down", after)["ffn1_d"].reshape(FF_HID, D_MODEL))
    wg = {**wg, **fetch("mid", x1)}
    full_cols = lambda blk: blk.transpose(1, 0, 2).reshape(blk.shape[1], N_DEV * blk.shape[2])
    lora_w2 = full_cols(wg["lora"][:, :DECAY_LORA])
    lora_a2 = full_cols(wg["lora"][:, DECAY_LORA:DECAY_LORA + ICLR_LORA])
    lora_g2 = full_cols(wg["lora"][:, DECAY_LORA + ICLR_LORA:])
    h2 = _norm_fwd("mix_norm", x1, sm["mix_norm"])
    z = _mmc_nn("w_in", h2, wg["w_in"], 0, 0, IN_SHARD)
    z_g = (z, 2 * D_MODEL, (RWKV_COLS + 3 * ATTN_DIM) // (2 * D_MODEL))

    r, k, v, lo, gd = _shift_fwd(z, sm["rwkv_mu"])
    zero_lo = jnp.zeros((DECAY_LORA, RWKV_DIM), BF16)
    w2p = jnp.concatenate([lora_w2, zero_lo], axis=0).astype(F32)
    a2p = jnp.concatenate([zero_lo, lora_a2], axis=0).astype(F32)
    pre_params = [sm["rwkv_w0"], w2p, sm["rwkv_a0"], a2p, lora_g2.astype(F32), sm["rwkv_k_k"], sm["rwkv_k_a"]]
    wide = [(RWKV_DIM, F32)]
    k2, kk, a, decay, g = _rowwise("rwkv_pre", _f_pre, [k, lo, gd], pre_params, wide * 5)
    as_heads = lambda u: u.reshape(hv, HEAD_DIM)
    kkn, b = _rowwise("rwkv_kk", _f_kk, [as_heads(kk), as_heads(a)], [], [(HEAD_DIM, F32)] * 2, tr=512)
    scan_in = [u.reshape(t, RWKV_HEADS, HEAD_DIM) for u in (r, decay, k2, kkn, b)]
    v_rows = _to_v_rows(v)
    y_rows, states = _wkv_fwd(*scan_in, v_rows)
    y = _from_v_rows(y_rows)
    post_params = [sm["rwkv_gn_w"].reshape(RWKV_HEADS, HEAD_DIM), sm["rwkv_gn_b"].reshape(RWKV_HEADS, HEAD_DIM),
                   sm["rwkv_r_k"].reshape(RWKV_HEADS, HEAD_DIM)]
    post_rows = [as_heads(y), as_heads(r), as_heads(k2), as_heads(v), as_heads(g)]
    y_rwkv = _rowwise("rwkv_post", lambda *av: (_f_post(*av),), post_rows, post_params, [(HEAD_DIM, F32)], tr=512)[0]
    y_rwkv = y_rwkv.reshape(t, RWKV_DIM)

    inv_freq = 1.0 / (ROPE_THETA ** (jnp.arange(0, HEAD_DIM, 2, dtype=F32) / HEAD_DIM))
    freq2 = jnp.tile(inv_freq, 2 * PAIR // HEAD_DIM).reshape(1, PAIR)
    half = jnp.ones((HEAD_DIM // 2,), F32)
    sign2 = jnp.tile(jnp.concatenate([-half, half]), PAIR // HEAD_DIM).reshape(1, PAIR)

    def rope_table(posv, fr, sg):
        ang = posv * fr
        return (jnp.concatenate([jnp.cos(ang), jnp.sin(ang) * sg], axis=1),)
    tab = _rowwise("rope_table", rope_table, [pos.astype(F32).reshape(t, 1)], [freq2, sign2], [(2 * PAIR, F32)])[0]
    qk = _qk_prep(z, tab, sm["q_norm"], sm["k_norm"])
    o_all, lse_all = _attn_fwd(qk, z)
    gw = HEADS_PER_GROUP * HEAD_DIM

    def by_group(ov, lv):
        return [ov[:, i * gw:(i + 1) * gw] for i in range(3)] + [lv[:, i * gw:(i + 1) * gw] for i in range(3)]
    y_attn = _rowwise("attn_comb", lambda ov, lv: (_f_comb(*by_group(ov, lv)),), [o_all, lse_all], [], [(gw, F32)])[0]

    wg = {**wg, **fetch("rest", y_rwkv)}
    w_d2 = wg["ffn2_d"].reshape(FF_HID, D_MODEL)
    w_out = wg["w_out"].reshape(D_MODEL, D_MODEL)
    w_pg = wg["ple_gate"].reshape(D_MODEL, D_MODEL)
    u_r =_mmc_nn("br_rwkv", y_rwkv, wg["br"], 0, 0, OUT_SHARD)
    u_a = _mmc_nn("br_attn", y_attn, wg["br"], 2, 0, OUT_SHARD)

    def f_merge(zgr, zga, ur, ua):
        return _sigmoid(zgr) * ur + _sigmoid(zga) * ua
    merged = _rowwise("merge", lambda zg, ur, ua: (f_merge(zg[:, :D_MODEL], zg[:, D_MODEL:], ur, ua),),
                      [z_g, u_r, u_a], [], [(D_MODEL, BF16)])[0]
    x2 = _mm("w_out", merged, w_out, "nn", res=x1)
    x3, ffn2_saved, _ = _ffn_fwd("ffn2", x2, sm["ffn2_norm"], wg["ffn2_gu"], lambda after: w_d2)

    hn = _norm_fwd("ple_norm", x3, sm["ple_norm"])
    gz = _mm("ple_gate", hn, w_pg, "nn")
    pp = _mmc_nn("ple_proj", p, wg["br"], 3, 0, OUT_SHARD)

    def f_head(x3v, gzv, ppv, tg):
        sg = _sigmoid(gzv)
        err = x3v + sg * ppv - tg
        part = 0.5 * jnp.sum(jnp.mean(err * err, axis=-1, keepdims=True))
        dx4 = err * (1.0 / D_MODEL)
        return dx4, dx4 * ppv * sg * (1.0 - sg), dx4 * sg, jnp.full((1, LANES), part, F32)
    dx4, dgz, dpp, loss_row = _rowwise("ple_loss", f_head, [x3, gz, pp, target], [],
                                       [(D_MODEL, F32), (D_MODEL, BF16), (D_MODEL, BF16)], [(1, LANES)])
    loss = loss_row[0, 0]

    gs, gm = {}, {}
    row_blocks = lambda g: g.reshape(N_DEV, g.shape[0] // N_DEV, g.shape[1])
    dhn = _mm("ple_dhn", dgz, w_pg, "nt")
    gm["ple_gate"] = row_blocks(_mm("ple_dwgate", hn, dgz, "tn", out_dtype=BF16))
    gm["ple_proj"] = _mmc_tn("ple_dwproj", p, dpp, OUT_SHARD)
    dx3, gs["ple_norm"] = _norm_bwd("ple_dnorm", x3, sm["ple_norm"], dhn, dx4)

    dx2, gs["ffn2_norm"], gm["ffn2_gu"], gm["ffn2_d"] = _ffn_bwd(
        "ffn2", x2, sm["ffn2_norm"], wg["ffn2_gu"], w_d2, ffn2_saved, dx3)
    tail_token = on_grads("tail", {n: gm.pop(n) for n in SCATTER_GROUPS["tail"]})

    dmerged = _mm("w_out_dmerged", dx2, w_out, "nt")
    gm["w_out"] = row_blocks(_mm("w_out_dw", merged, dx2, "tn", out_dtype=BF16))

    def merge_bwd(zg, ur, ua, dm):
        _, vjp = jax.vjp(f_merge, zg[:, :D_MODEL], zg[:, D_MODEL:], ur, ua)
        dzr, dza, dur, dua = vjp(dm)
        return jnp.concatenate([dzr, dza], axis=1), dur, dua
    dz_g, du_r, du_a = _rowwise("merge_bwd", merge_bwd, [z_g, u_r, u_a, dmerged], [],
                                [(2 * D_MODEL, BF16), (D_MODEL, BF16), (D_MODEL, BF16)])
    dy_rwkv = _mmc_nt("br_rwkv_dy", du_r, wg["br"], 0, 0, OUT_SHARD, RWKV_DIM)
    dy_attn = _mmc_nt("br_attn_dy", du_a, wg["br"], 2, 0, OUT_SHARD, HEADS_PER_GROUP * HEAD_DIM)
    gm["br"] = jnp.concatenate([_mmc_tn("br_rwkv_dw", y_rwkv, du_r, OUT_SHARD),
                                _mmc_tn("br_attn_dw", y_attn, du_a, OUT_SHARD)], axis=1)

    def comb_bwd(ov, lv, dyv):
        _, vjp = jax.vjp(_f_comb, *by_group(ov, lv))
        d = vjp(dyv)
        return jnp.concatenate(d[:3], axis=1), jnp.concatenate(d[3:], axis=1)
    do_all, dl_all = _rowwise("attn_comb_bwd", comb_bwd, [o_all, lse_all, dy_attn], [],
                              [(ATTN_DIM, F32), (ATTN_DIM, F32)])
    dq_all, dk_all, dv_all = _attn_bwd(qk, z, do_all, dl_all)
    dqk_raw, gs["q_norm"], gs["k_norm"] = _qk_prep_bwd(z, tab, sm["q_norm"], sm["k_norm"], dq_all, dk_all)

    def post_bwd(yv, rv, k2v, vv, gv, dv_, gnw, gnb, rk):
        _, vjp = jax.vjp(_f_post, yv, rv, k2v, vv, gv, gnw, gnb, rk)
        return vjp(dv_)
    head_acc = (RWKV_HEADS, HEAD_DIM)
    dy, dr1, dk2a, dv1, dg, d_gnw, d_gnb, d_rk = _rowwise(
        "rwkv_post_bwd", post_bwd, post_rows + [as_heads(dy_rwkv)], [post_params[0] + tail_token[0, 0]] + post_params[1:],
        [(HEAD_DIM, F32)] * 5, [head_acc] * 3, tr=512)
    gs["rwkv_gn_w"], gs["rwkv_gn_b"], gs["rwkv_r_k"] = d_gnw, d_gnb, d_rk
    dr2, ddecay, dk2b, dkkn, db, dv_rows = _wkv_bwd(*scan_in, v_rows, states, _to_v_rows(dy.reshape(t, RWKV_DIM)))
    dr2, ddecay, dk2b, dkkn, db = [u.reshape(t, RWKV_DIM) for u in (dr2, ddecay, dk2b, dkkn, db)]
    dv2 = _from_v_rows(dv_rows)

    def kk_bwd(kkv, av, dkknv, dbv):
        _, vjp = jax.vjp(_f_kk, kkv, av)
        return vjp((dkknv, dbv))
    dkk, da = _rowwise("rwkv_kk_bwd", kk_bwd, [as_heads(kk), as_heads(a), as_heads(dkkn), as_heads(db)], [],
                       [(HEAD_DIM, F32)] * 2, tr=512)

    def pre_bwd(kv, lov, gdv, dk2x, dk2y, dkkv, dav, ddec, dgv, w0, w2p_, a0, a2p_, g2, k_k, k_a):
        _, vjp = jax.vjp(_f_pre, kv, lov, gdv, w0, w2p_, a0, a2p_, g2, k_k, k_a)
        return vjp((dk2x + dk2y, dkkv, dav, ddec, dgv))
    lora_acc = (DECAY_LORA + ICLR_LORA, RWKV_DIM)
    dk, dlo, dgd, d_w0, d_w2p, d_a0, d_a2p, d_g2, d_kk, d_ka = _rowwise(
        "rwkv_pre_bwd", pre_bwd,
        [k, lo, gd, dk2a.reshape(t, RWKV_DIM), dk2b, dkk.reshape(t, RWKV_DIM), da.reshape(t, RWKV_DIM), ddecay,
         dg.reshape(t, RWKV_DIM)],
        pre_params, [(RWKV_DIM, F32), (LANES, F32), (LANES, F32)],
        [(1, RWKV_DIM), lora_acc, (1, RWKV_DIM), lora_acc, (GATE_LORA, RWKV_DIM), (1, RWKV_DIM), (1, RWKV_DIM)])
    gs["rwkv_w0"], gs["rwkv_a0"], gs["rwkv_k_k"], gs["rwkv_k_a"] = d_w0, d_a0, d_kk, d_ka
    col_blocks = lambda g: g.reshape(g.shape[0], N_DEV, g.shape[1] // N_DEV).transpose(1, 0, 2)
    gm["lora"] = jnp.concatenate([col_blocks(d_w2p[:DECAY_LORA]), col_blocks(d_a2p[DECAY_LORA:]), col_blocks(d_g2)],
                                 axis=1).astype(BF16)
    add2 = lambda u, w_: _rowwise("rwkv_add", lambda p_, q_: (p_ + q_,), [u, w_], [], [(RWKV_DIM, F32)])[0]
    dz_r, gs["rwkv_mu"] = _shift_bwd(z, sm["rwkv_mu"], add2(dr1.reshape(t, RWKV_DIM), dr2), dk,
                                     add2(dv1.reshape(t, RWKV_DIM), dv2), dlo, dgd)

    dz = jnp.concatenate([dz_r, dqk_raw, dv_all.astype(BF16), dz_g], axis=1)
    dh2 = _mmc_nt("w_in_dh", dz, wg["w_in"], 0, 0, IN_SHARD, D_MODEL)
    gm["w_in"] = _mmc_tn("w_in_dw", h2, dz, IN_SHARD)
    mixer_token = on_grads("mixer", {n: gm.pop(n) for n in SCATTER_GROUPS["mixer"]})
    dx1, gs["mix_norm"] = _norm_bwd("mix_dnorm", x1, sm["mix_norm"] + mixer_token[0, 0], dh2, dx2)

    dx0, gs["ffn1_norm"], gm["ffn1_gu"], _ = _ffn_bwd(
        "ffn1", x, sm["ffn1_norm"], wg["ffn1_gu"], w_d1, ffn1_saved, dx1,
        on_down=lambda blocks: on_grads("ffn1_down", {"ffn1_d": blocks}))
    return loss, dx0, gm, gs


def kernel(x, p, positions, ffn1_norm, ffn1_w_gate, ffn1_w_up, ffn1_w_down, mix_norm, w_in, rwkv_mu, rwkv_w0, rwkv_w2, rwkv_a0, rwkv_a2, rwkv_g2, rwkv_k_k, rwkv_k_a, rwkv_r_k, rwkv_gn_w, rwkv_gn_b, q_norm, k_norm, w_br_rwkv, w_br_attn, w_out, ffn2_norm, ffn2_w_gate, ffn2_w_up, ffn2_w_down, ple_norm, ple_w_gate, ple_w_proj, loss_target, m_ffn1_norm, m_ffn1_w_gate, m_ffn1_w_up, m_ffn1_w_down, m_mix_norm, m_w_in, m_rwkv_mu, m_rwkv_w0, m_rwkv_w2, m_rwkv_a0, m_rwkv_a2, m_rwkv_g2, m_rwkv_k_k, m_rwkv_k_a, m_rwkv_r_k, m_rwkv_gn_w, m_rwkv_gn_b, m_q_norm, m_k_norm, m_w_br_rwkv, m_w_br_attn, m_w_out, m_ffn2_norm, m_ffn2_w_gate, m_ffn2_w_up, m_ffn2_w_down, m_ple_norm, m_ple_w_gate, m_ple_w_proj, v_ffn1_norm, v_ffn1_w_gate, v_ffn1_w_up, v_ffn1_w_down, v_mix_norm, v_w_in, v_rwkv_mu, v_rwkv_w0, v_rwkv_w2, v_rwkv_a0, v_rwkv_a2, v_rwkv_g2, v_rwkv_k_k, v_rwkv_k_a, v_rwkv_r_k, v_rwkv_gn_w, v_rwkv_gn_b, v_q_norm, v_k_norm, v_w_br_rwkv, v_w_br_attn, v_w_out, v_ffn2_norm, v_ffn2_w_gate, v_ffn2_w_up, v_ffn2_w_down, v_ple_norm, v_ple_w_gate, v_ple_w_proj):
    args = locals()
    w = {n: args[n][0] for n in WEIGHTS}
    m = {n: args["m_" + n][0] for n in WEIGHTS}
    v = {n: args["v_" + n][0] for n in WEIGHTS}

    w_loc = _local_blocks(w)
    head = GATHER_GROUPS["head"]
    wg = dict(zip(head, _all_gather_hbm("gather_head", [w_loc[n] for n in head])))
    me = 4 * lax.axis_index("x") + 2 * lax.axis_index("y") + lax.axis_index("c")
    gathering, order_after = {}, wg[head[0]]
    for group in ("down", "mid", "rest"):
        shards = [w_loc[n] for n in GATHER_GROUPS[group]]
        zones = [lax.dynamic_update_slice(lax.empty((N_DEV,) + a.shape, a.dtype), a[None], (me, 0, 0)) for a in shards]
        *gathering[group], order_after = _exchange_start("gather_start_" + group, _gather_copies, shards, zones,
                                                         order_after)

    def fetch(group, after):
        _, got = _exchange_wait("gather_wait_" + group, _gather_copies, *gathering[group], after)
        return dict(zip(GATHER_GROUPS[group], got))

    sm = {n: w[n].reshape(1, -1) for n in SMALL}
    sm["ffn1_norm"] = sm["ffn1_norm"] + order_after[0, 0]
    in_flight = {}

    def scatter_early(group, arrays):
        arrs = [arrays[n] for n in SCATTER_GROUPS[group]]
        *in_flight[group], token = _exchange_start("scatter_start_" + group, _scatter_copies, arrs,
                                                   [lax.empty(a.shape, a.dtype) for a in arrs], arrs[0])
        return token
    loss_part, dx, gm, gs = _local_step(x[0], p[0, 0], positions[0], loss_target[0], sm, wg, fetch, scatter_early)
    loss = lax.psum(loss_part, ("x", "y", "c"))
    head_token = scatter_early("head", gm)
    recv, own = {}, {}

    def arrived(group, after):
        sent, lands = _exchange_wait("scatter_wait_" + group, _scatter_copies, *in_flight[group], after)
        own.update(zip(SCATTER_GROUPS[group], sent))
        recv.update(zip(SCATTER_GROUPS[group], lands))
    for group in ("tail", "mixer", "ffn1_down"):
        arrived(group, head_token)
    small_like = [w[n] for n in SMALL]
    small_rows = 80
    gs_all = _all_gather_vmem(_pack_small([gs[n] for n in SMALL], small_rows))
    gs_sum = _sum_slots("sum_small_grads", gs_all, small_rows)

    res = {}
    early = [e for e in ADAM_PLAN if e[1] not in SCATTER_GROUPS["head"]]
    late = [e for e in ADAM_PLAN if e[1] in SCATTER_GROUPS["head"]]
    for n, src, rb, cb, cw, tr in early + late:
        if (n, src, rb, cb, cw, tr) == late[0]:
            arrived("head", res["delta", early[-1][0]])
        outs4 = _adamw_slots("adamw_" + n, recv[src], own[src], rb, cb, cw, w[n], m[n], v[n], tr, head_token)
        for tag, a in zip(("grad", "delta", "new_m", "new_v"), outs4):
            res[tag, n] = a[None]
    d_s, m_s, v_s = _adamw("adamw_small", _pack_small(small_like, small_rows), gs_sum,
                           _pack_small([m[n] for n in SMALL], small_rows),
                           _pack_small([v[n] for n in SMALL], small_rows), small_rows)
    for tag, small in (("grad", gs_sum), ("delta", d_s), ("new_m", m_s), ("new_v", v_s)):
        for n, a in zip(SMALL, _unpack_small(small, small_like)):
            res[tag, n] = a[None]
    outs = [loss, dx[None]]
    for tag in ("grad", "delta", "new_m", "new_v"):
        outs += [res[tag, n] for n in WEIGHTS]
    return tuple(outs)
```

```python
import functools

import jax
import jax.numpy as jnp
from jax import lax
from jax.experimental import pallas as pl
from jax.experimental.pallas import tpu as pltpu

F32, BF16 = jnp.float32, jnp.bfloat16
MESH = pl.DeviceIdType.MESH
N_DEV = 8
LANES = 128
VMEM_LIMIT = 56 * 1024 * 1024

D_MODEL = 1024
PLE_DIM = 256
HEAD_DIM = 64
RWKV_HEADS = 8
RWKV_DIM = RWKV_HEADS * HEAD_DIM
DECAY_LORA = 64
ICLR_LORA = 64
GATE_LORA = 128
GN_EPS = 64e-5
ATTN_GROUPS = ((128, 1), (512, 4), (2048, 16))
HEADS_PER_GROUP = 4
ATTN_HEADS = HEADS_PER_GROUP * len(ATTN_GROUPS)
ATTN_DIM = ATTN_HEADS * HEAD_DIM
BAND_BLOCK = 128
ROPE_THETA = 10000.0
NEG_INF = -1e30
D_FF = 2816
RMS_EPS = 1e-6
RWKV_COLS = 3 * RWKV_DIM + DECAY_LORA + ICLR_LORA + GATE_LORA
ADAM_LR = 0.001
ADAM_B1 = 0.9
ADAM_B2 = 0.999
ADAM_EPS = 1e-08
ADAM_WD = 0.01
ADAM_STEP = 10

V_LO = LANES // RWKV_HEADS
V_HI = HEAD_DIM // V_LO
SCAN_CHUNK = 64

FF_SHARD = D_FF // N_DEV
FF_PAD = -(-FF_SHARD // LANES) * LANES
FF_HID = N_DEV * FF_PAD
IN_SHARD = 6144 // N_DEV
OUT_SHARD = D_MODEL // N_DEV

SMALL = ("ffn1_norm", "mix_norm", "rwkv_mu", "rwkv_w0", "rwkv_a0", "rwkv_k_k", "rwkv_k_a", "rwkv_r_k",
         "rwkv_gn_w", "rwkv_gn_b", "q_norm", "k_norm", "ffn2_norm", "ple_norm")
WEIGHTS = ("ffn1_norm", "ffn1_w_gate", "ffn1_w_up", "ffn1_w_down", "mix_norm", "w_in", "rwkv_mu", "rwkv_w0",
           "rwkv_w2", "rwkv_a0", "rwkv_a2", "rwkv_g2", "rwkv_k_k", "rwkv_k_a", "rwkv_r_k", "rwkv_gn_w",
           "rwkv_gn_b", "q_norm", "k_norm", "w_br_rwkv", "w_br_attn", "w_out", "ffn2_norm", "ffn2_w_gate",
           "ffn2_w_up", "ffn2_w_down", "ple_norm", "ple_w_gate", "ple_w_proj")


def _cparams(**kw):
    return pltpu.CompilerParams(vmem_limit_bytes=VMEM_LIMIT, **kw)


def _tile(n, cap):
    best = None
    for t in range(LANES, min(n, cap) + 1, LANES):
        if n % t == 0:
            best = t
    return best if best is not None else n


@jax.custom_vjp
def _bdot(a, w):
    return jnp.dot(a.astype(BF16), w.astype(BF16), preferred_element_type=F32)


def _bdot_fwd(a, w):
    return _bdot(a, w), (a, w)


def _bdot_bwd(res, g):
    a, w = res
    gb = g.astype(BF16)
    da = lax.dot_general(gb, w.astype(BF16), (((1,), (1,)), ((), ())), preferred_element_type=F32)
    dw = lax.dot_general(a.astype(BF16), gb, (((0,), (0,)), ((), ())), preferred_element_type=F32)
    return da.astype(a.dtype), dw.astype(w.dtype)


_bdot.defvjp(_bdot_fwd, _bdot_bwd)


@jax.custom_vjp
def _bdot_nt(a, b):
    return lax.dot_general(a.astype(BF16), b.astype(BF16), (((1,), (1,)), ((), ())), preferred_element_type=F32)


def _bdot_nt_fwd(a, b):
    return _bdot_nt(a, b), (a, b)


def _bdot_nt_bwd(res, g):
    a, b = res
    gb = g.astype(BF16)
    da = jnp.dot(gb, b.astype(BF16), preferred_element_type=F32)
    db = lax.dot_general(gb, a.astype(BF16), (((0,), (0,)), ((), ())), preferred_element_type=F32)
    return da.astype(a.dtype), db.astype(b.dtype)


_bdot_nt.defvjp(_bdot_nt_fwd, _bdot_nt_bwd)


def _mm(name, a, b, mode, out_dtype=F32, res=None, scale=None):
    if mode == "nn":
        (m, k), n = a.shape, b.shape[1]
    elif mode == "nt":
        (m, k), n = a.shape, b.shape[0]
    else:
        (k, m), n = a.shape, b.shape[1]
    tm, tn = _tile(m, 512), _tile(n, 512)
    a_spec = pl.BlockSpec((k, tm), lambda i, j: (0, i)) if mode == "tn" else pl.BlockSpec((tm, k), lambda i, j: (i, 0))
    b_spec = pl.BlockSpec((tn, k), lambda i, j: (j, 0)) if mode == "nt" else pl.BlockSpec((k, tn), lambda i, j: (0, j))
    dims = {"nn": ((1,), (0,)), "nt": ((1,), (1,)), "tn": ((0,), (0,))}[mode]
    o_spec = pl.BlockSpec((tm, tn), lambda i, j: (i, j))
    ins, in_specs = [a, b], [a_spec, b_spec]
    if res is not None:
        ins.append(res)
        in_specs.append(o_spec)

    def body(*refs):
        acc = lax.dot_general(refs[0][...].astype(BF16), refs[1][...].astype(BF16), (dims, ((), ())),
                              preferred_element_type=F32)
        if scale is not None:
            acc = acc * scale
        if res is not None:
            acc = acc + refs[2][...].astype(F32)
        refs[-1][...] = acc.astype(refs[-1].dtype)

    return pl.pallas_call(
        body, name=name, grid=(m // tm, n // tn), in_specs=in_specs, out_specs=o_spec,
        out_shape=jax.ShapeDtypeStruct((m, n), out_dtype),
        compiler_params=_cparams(dimension_semantics=("parallel", "parallel")),
    )(*ins)


def _mmc_nn(name, a, wb, ki, ci, n, out_dtype=F32):
    m, k = a.shape
    tm = _tile(m, 512)

    def body(a_ref, w_ref, o_ref):
        o_ref[...] = jnp.dot(a_ref[...].astype(BF16), w_ref[...], preferred_element_type=F32).astype(o_ref.dtype)

    return pl.pallas_call(
        body, name=name, grid=(m // tm, N_DEV),
        in_specs=[pl.BlockSpec((tm, k), lambda i, j: (i, 0)), pl.BlockSpec((None, k, n), lambda i, j: (j, ki, ci))],
        out_specs=pl.BlockSpec((tm, n), lambda i, j: (i, j)),
        out_shape=jax.ShapeDtypeStruct((m, N_DEV * n), out_dtype),
        compiler_params=_cparams(dimension_semantics=("parallel", "parallel")),
    )(a, wb)


def _mmc_nt(name, a, wb, ki, ci, n, k, res=None):
    m = a.shape[0]
    tm = _tile(m, 512)
    o_spec = pl.BlockSpec((tm, k), lambda i, j: (i, 0))
    ins = [a, wb] + ([res] if res is not None else [])
    in_specs = [pl.BlockSpec((tm, n), lambda i, j: (i, j)), pl.BlockSpec((None, k, n), lambda i, j: (j, ki, ci))]
    in_specs += [o_spec] if res is not None else []

    def body(*refs):
        a_ref, w_ref, o_ref = refs[0], refs[1], refs[-1]
        acc = lax.dot_general(a_ref[...].astype(BF16), w_ref[...], (((1,), (1,)), ((), ())),
                              preferred_element_type=F32)

        @pl.when(pl.program_id(1) == 0)
        def _():
            o_ref[...] = acc + refs[2][...] if res is not None else acc

        @pl.when(pl.program_id(1) != 0)
        def _():
            o_ref[...] += acc

    return pl.pallas_call(
        body, name=name, grid=(m // tm, N_DEV), in_specs=in_specs, out_specs=o_spec,
        out_shape=jax.ShapeDtypeStruct((m, k), F32),
        compiler_params=_cparams(dimension_semantics=("parallel", "arbitrary")),
    )(*ins)


def _mmc_tn(name, x, dy, n):
    m, k = x.shape
    tk = _tile(k, 512)

    def body(x_ref, dy_ref, o_ref):
        o_ref[...] = lax.dot_general(x_ref[...].astype(BF16), dy_ref[...].astype(BF16), (((0,), (0,)), ((), ())),
                                     preferred_element_type=F32).astype(o_ref.dtype)

    return pl.pallas_call(
        body, name=name, grid=(N_DEV, k // tk),
        in_specs=[pl.BlockSpec((m, tk), lambda j, i: (0, i)), pl.BlockSpec((m, n), lambda j, i: (0, j))],
        out_specs=pl.BlockSpec((None, tk, n), lambda j, i: (j, i, 0)),
        out_shape=jax.ShapeDtypeStruct((N_DEV, k, n), BF16),
        compiler_params=_cparams(dimension_semantics=("parallel", "parallel")),
    )(x, dy)


def _rowwise(name, fn, rows, params, out_rows, out_accs=(), tr=256):
    rows = [a if isinstance(a, tuple) else (a, a.shape[1], 0) for a in rows]
    r = rows[0][0].shape[0]
    in_specs = [pl.BlockSpec((tr, wd), lambda i, cb=cb: (i, cb)) for _, wd, cb in rows]
    rows = [a for a, _, _ in rows]
    in_specs += [pl.BlockSpec(p.shape, lambda i, nd=p.ndim: (0,) * nd) for p in params]
    out_shape = [jax.ShapeDtypeStruct((r, c), dt) for c, dt in out_rows]
    out_shape += [jax.ShapeDtypeStruct(s, F32) for s in out_accs]
    out_specs = [pl.BlockSpec((tr, c), lambda i: (i, 0)) for c, _ in out_rows]
    out_specs += [pl.BlockSpec(s, lambda i, nd=len(s): (0,) * nd) for s in out_accs]
    n_in, n_ro = len(rows) + len(params), len(out_rows)

    def body(*refs):
        res = fn(*[ref[...] for ref in refs[:n_in]])
        outs = refs[n_in:]
        for o, v in zip(outs[:n_ro], res[:n_ro]):
            o[...] = v.astype(o.dtype)
        for o, v in zip(outs[n_ro:], res[n_ro:]):
            _accumulate(o, v)

    return pl.pallas_call(
        body, name=name, grid=(r // tr,), in_specs=in_specs, out_specs=out_specs, out_shape=out_shape,
        compiler_params=_cparams(dimension_semantics=("arbitrary",)),
    )(*rows, *params)


def _pairwise(name, fn, rows, params, n_out, n_acc=0, tr=512):
    t, c = rows[0].shape
    tile = pl.BlockSpec((tr, 2 * HEAD_DIM), lambda p, i: (i, p))
    vec = pl.BlockSpec((1, 2 * HEAD_DIM), lambda p, i: (0, p))
    n_in = len(rows) + len(params)

    def body(*refs):
        res = fn(*[ref[...] for ref in refs[:n_in]])
        outs = refs[n_in:]
        for o, v in zip(outs[:n_out], res[:n_out]):
            o[...] = v
        first = pl.program_id(1) == 0
        for o, v in zip(outs[n_out:], res[n_out:]):
            @pl.when(first)
            def _(o=o, v=v):
                o[...] = v

            @pl.when(jnp.logical_not(first))
            def _(o=o, v=v):
                o[...] += v

    return pl.pallas_call(
        body, name=name, grid=(c // (2 * HEAD_DIM), t // tr),
        in_specs=[tile] * len(rows) + [vec] * len(params), out_specs=[tile] * n_out + [vec] * n_acc,
        out_shape=[jax.ShapeDtypeStruct((t, c), F32)] * n_out + [jax.ShapeDtypeStruct((1, c), F32)] * n_acc,
        compiler_params=_cparams(dimension_semantics=("parallel", "arbitrary")),
    )(*rows, *params)


def _accumulate(o_ref, v):
    @pl.when(pl.program_id(0) == 0)
    def _():
        o_ref[...] = v

    @pl.when(pl.program_id(0) != 0)
    def _():
        o_ref[...] += v


def _rms(x, g):
    return x * lax.rsqrt(jnp.mean(x * x, axis=-1, keepdims=True) + RMS_EPS) * g


def _sigmoid(x):
    return jax.nn.sigmoid(x)


def _softplus(x):
    return jnp.maximum(x, 0.0) + jnp.log1p(jnp.exp(-jnp.abs(x)))


def _norm_fwd(name, x, g):
    return _rowwise(name, lambda xv, gv: (_rms(xv, gv),), [x], [g], [(x.shape[1], BF16)])[0]


def _norm_bwd(name, x, g, dh, dres):
    def fn(xv, dhv, drv, gv):
        _, vjp = jax.vjp(_rms, xv, gv)
        dx, dg = vjp(dhv)
        return dx + drv, dg
    return _rowwise(name, fn, [x, dh, dres], [g], [(x.shape[1], F32)], [g.shape])


def _f_act(gate, up):
    return gate * _sigmoid(gate) * up


def _gate_up(guv, j):
    base = j * 2 * FF_PAD
    return guv[:, base:base + FF_PAD], guv[:, base + FF_PAD:base + 2 * FF_PAD]


def _ffn_fwd(tag, x, norm, w_gu, w_down):
    h = _norm_fwd(tag + "_norm", x, norm)
    gu = _mmc_nn(tag + "_gu", h, w_gu, 0, 0, 2 * FF_PAD)

    def act(guv):
        return (jnp.concatenate([_f_act(*_gate_up(guv, j)) for j in range(N_DEV)], axis=1),)
    a = _rowwise(tag + "_act", act, [gu], [], [(FF_HID, BF16)])[0]
    wd = w_down(a)
    out = _mm(tag + "_down", a, wd, "nn", res=x, scale=0.5)
    return out, (h, gu, a), wd


def _ffn_bwd(tag, x, norm, w_gu, w_down, saved, dout, on_down=None):
    h, gu, a = saved
    d_wdown = _mm(tag + "_dwdown", a, dout, "tn", out_dtype=BF16, scale=0.5).reshape(N_DEV, FF_PAD, D_MODEL)
    token = on_down(d_wdown)[:1] if on_down is not None else jnp.zeros((1, LANES), F32)
    da = _mm(tag + "_dact", dout, w_down, "nt", scale=0.5)

    def act_bwd(guv, dav, _):
        outs = []
        for j in range(N_DEV):
            _, vjp = jax.vjp(_f_act, *_gate_up(guv, j))
            outs += list(vjp(dav[:, j * FF_PAD:(j + 1) * FF_PAD]))
        return (jnp.concatenate(outs, axis=1),)
    dgu = _rowwise(tag + "_dgu", act_bwd, [gu, da], [token], [(2 * FF_HID, BF16)])[0]
    dh = _mmc_nt(tag + "_dh", dgu, w_gu, 0, 0, 2 * FF_PAD, D_MODEL)
    d_wgu = _mmc_tn(tag + "_dwgu", h, dgu, 2 * FF_PAD)
    dx, dnorm = _norm_bwd(tag + "_dnorm", x, norm, dh, dout)
    return dx, dnorm, d_wgu, d_wdown


def _shift_fwd(z, mu):
    t, c = z.shape[0], RWKV_COLS
    tr = 256

    def body(z_ref, zp_ref, mu_ref, r_ref, k_ref, v_ref, lo_ref, gd_ref):
        zv = z_ref[...]
        prev = zp_ref[7:8, :] * jnp.where(pl.program_id(0) == 0, 0.0, 1.0)
        row = lax.broadcasted_iota(jnp.int32, zv.shape, 0)
        zsh = jnp.where(row == 0, prev, pltpu.roll(zv, 1, 0))
        zs = zv + (zsh - zv) * mu_ref[...]
        r_ref[...] = zs[:, 0:512]
        k_ref[...] = zs[:, 512:1024]
        v_ref[...] = zs[:, 1024:1536]
        lo_ref[...] = zs[:, 1536:1664]
        gd_ref[...] = zs[:, 1664:1792]

    widths = (512, 512, 512, 128, 128)
    return pl.pallas_call(
        body, name="rwkv_shift", grid=(t // tr,),
        in_specs=[pl.BlockSpec((tr, c), lambda i: (i, 0)),
                  pl.BlockSpec((8, c), lambda i: (jnp.maximum(i * (tr // 8) - 1, 0), 0)),
                  pl.BlockSpec((1, c), lambda i: (0, 0))],
        out_specs=[pl.BlockSpec((tr, w), lambda i: (i, 0)) for w in widths],
        out_shape=[jax.ShapeDtypeStruct((t, w), F32) for w in widths],
        compiler_params=_cparams(dimension_semantics=("parallel",)),
    )(z, z, mu)


def _shift_bwd(z, mu, dr, dk, dv, dlo, dgd):
    t, c = z.shape[0], RWKV_COLS
    tr = 256
    nt = t // tr

    def body(z_ref, zp_ref, mu_ref, dr_ref, dk_ref, dv_ref, dlo_ref, dgd_ref,
             drn_ref, dkn_ref, dvn_ref, dlon_ref, dgdn_ref, dz_ref, dmu_ref):
        i = pl.program_id(0)
        zv, muv = z_ref[...], mu_ref[...]
        prev = zp_ref[7:8, :] * jnp.where(i == 0, 0.0, 1.0)
        row = lax.broadcasted_iota(jnp.int32, zv.shape, 0)
        zsh = jnp.where(row == 0, prev, pltpu.roll(zv, 1, 0))
        dzs = jnp.concatenate([dr_ref[...], dk_ref[...], dv_ref[...], dlo_ref[...], dgd_ref[...]], axis=1)
        nxt = jnp.concatenate([drn_ref[0:1, :], dkn_ref[0:1, :], dvn_ref[0:1, :], dlon_ref[0:1, :],
                               dgdn_ref[0:1, :]], axis=1) * jnp.where(i == nt - 1, 0.0, 1.0)
        u = dzs * muv
        un = jnp.where(row == tr - 1, nxt * muv, pltpu.roll(u, tr - 1, 0))
        dz_ref[...] = (dzs - u + un).astype(dz_ref.dtype)
        _accumulate(dmu_ref, jnp.sum(dzs * (zsh - zv), axis=0, keepdims=True))

    widths = (512, 512, 512, 128, 128)
    nxt_map = lambda i: (jnp.minimum((i + 1) * (tr // 8), t // 8 - 1), 0)
    return pl.pallas_call(
        body, name="rwkv_shift_bwd", grid=(nt,),
        in_specs=[pl.BlockSpec((tr, c), lambda i: (i, 0)),
                  pl.BlockSpec((8, c), lambda i: (jnp.maximum(i * (tr // 8) - 1, 0), 0)),
                  pl.BlockSpec((1, c), lambda i: (0, 0))]
        + [pl.BlockSpec((tr, w), lambda i: (i, 0)) for w in widths]
        + [pl.BlockSpec((8, w), nxt_map) for w in widths],
        out_specs=[pl.BlockSpec((tr, c), lambda i: (i, 0)), pl.BlockSpec((1, c), lambda i: (0, 0))],
        out_shape=[jax.ShapeDtypeStruct((t, c), BF16), jax.ShapeDtypeStruct((1, c), F32)],
        compiler_params=_cparams(dimension_semantics=("arbitrary",)),
    )(z, z, mu, dr, dk, dv, dlo, dgd, dr, dk, dv, dlo, dgd)


def _f_pre(k, lo, gd, w0, w2p, a0, a2p, g2, k_k, k_a):
    lane = lax.broadcasted_iota(jnp.int32, lo.shape, 1)
    lo_act = jnp.where(lane < DECAY_LORA, jnp.tanh(lo), lo)
    w = -_softplus(-(w0 + _bdot(lo_act, w2p))) - 0.5
    a = _sigmoid(a0 + _bdot(lo_act, a2p))
    g = _bdot(_sigmoid(gd), g2)
    kk = k * k_k
    k2 = k * (1.0 + (a - 1.0) * k_a)
    decay = jnp.exp(-jnp.exp(w))
    return k2, kk, a, decay, g


def _f_kk(kk, a):
    kkn = kk * lax.rsqrt(jnp.maximum(_head_sums(kk * kk), 1e-24))
    return kkn, kkn * a


def _f_post(y, r, k2, v, g, gn_w, gn_b, r_k):
    mean = _head_sums(y) * (1.0 / HEAD_DIM)
    var = _head_sums(jnp.square(y - mean)) * (1.0 / HEAD_DIM)
    yn = (y - mean) * lax.rsqrt(var + GN_EPS) * gn_w + gn_b
    bonus = _head_sums(r * k2 * r_k) * v
    return (yn + bonus) * g


def _to_v_rows(x):
    t = x.shape[0]
    return x.reshape(t, RWKV_HEADS, V_HI, V_LO).transpose(0, 2, 3, 1).reshape(t, V_HI, LANES)


def _from_v_rows(x):
    t = x.shape[0]
    return x.reshape(t, V_HI, V_LO, RWKV_HEADS).transpose(0, 3, 1, 2).reshape(t, RWKV_DIM)


def _k_cols(x):
    return jnp.tile(x, (V_LO, 1)).T


def _k_rows(x):
    xt = x.T
    out = xt[0:RWKV_HEADS]
    for l in range(1, V_LO):
        out = out + xt[l * RWKV_HEADS:(l + 1) * RWKV_HEADS]
    return out


def _wkv_fwd(r, w, k, kk, b, v):
    t = r.shape[0]
    tc = SCAN_CHUNK
    key_spec = pl.BlockSpec((tc, RWKV_HEADS, HEAD_DIM), lambda i: (i, 0, 0))
    row_spec = pl.BlockSpec((tc, V_HI, LANES), lambda i: (i, 0, 0))

    def body(r_ref, w_ref, k_ref, kk_ref, b_ref, v_ref, y_ref, st_ref, s_scr, cols_a, cols_b):
        @pl.when(pl.program_id(0) == 0)
        def _():
            s_scr[...] = jnp.zeros_like(s_scr)

        def prep(ti, buf):
            for n, ref in enumerate((r_ref, w_ref, k_ref, kk_ref, b_ref)):
                buf[n] = _k_cols(ref[ti])

        def step(ti, s, cur, nxt, ti_next):
            rc, wc, kc, kkc, bc = (cur[n] for n in range(5))
            prep(ti_next, nxt)
            vt = v_ref[ti]
            new, ys = [], []
            for j in range(V_HI):
                sa = -jnp.sum(s[j] * kkc, axis=0, keepdims=True)
                nj = s[j] * wc + bc * sa + kc * vt[j:j + 1]
                st_ref[ti, j] = nj
                ys.append(jnp.sum(nj * rc, axis=0, keepdims=True))
                new.append(nj)
            y_ref[ti] = jnp.concatenate(ys, axis=0)
            return tuple(new)

        def pair(i, s):
            s = step(2 * i, s, cols_a, cols_b, 2 * i + 1)
            return step(2 * i + 1, s, cols_b, cols_a, jnp.minimum(2 * i + 2, tc - 1))

        prep(0, cols_a)
        s = lax.fori_loop(0, tc // 2, pair, tuple(s_scr[j] for j in range(V_HI)))
        for j in range(V_HI):
            s_scr[j] = s[j]

    return pl.pallas_call(
        body, name="wkv_fwd", grid=(t // tc,),
        in_specs=[key_spec] * 5 + [row_spec],
        out_specs=[row_spec, pl.BlockSpec((tc, V_HI, HEAD_DIM, LANES), lambda i: (i, 0, 0, 0))],
        out_shape=[jax.ShapeDtypeStruct((t, V_HI, LANES), F32),
                   jax.ShapeDtypeStruct((t, V_HI, HEAD_DIM, LANES), F32)],
        scratch_shapes=[pltpu.VMEM((V_HI, HEAD_DIM, LANES), F32)] + [pltpu.VMEM((5, HEAD_DIM, LANES), F32)] * 2,
        compiler_params=_cparams(dimension_semantics=("arbitrary",)),
    )(r, w, k, kk, b, v)


def _wkv_bwd(r, w, k, kk, b, v, states, dy):
    t = r.shape[0]
    tc = SCAN_CHUNK
    nb = t // tc
    key_spec = pl.BlockSpec((tc, RWKV_HEADS, HEAD_DIM), lambda i: (nb - 1 - i, 0, 0))
    row_spec = pl.BlockSpec((tc, V_HI, LANES), lambda i: (nb - 1 - i, 0, 0))
    st_spec = pl.BlockSpec((tc, V_HI, HEAD_DIM, LANES), lambda i: (nb - 1 - i, 0, 0, 0))
    stp_spec = pl.BlockSpec((1, V_HI, HEAD_DIM, LANES), lambda i: (jnp.maximum((nb - 1 - i) * tc - 1, 0), 0, 0, 0))

    def body(r_ref, w_ref, k_ref, kk_ref, b_ref, v_ref, st_ref, stp_ref, dy_ref,
             dr_ref, dw_ref, dk_ref, dkk_ref, db_ref, dv_ref, ds_scr, cols_a, cols_b, accs_a, accs_b):
        @pl.when(pl.program_id(0) == 0)
        def _():
            ds_scr[...] = jnp.zeros_like(ds_scr)

        def colsum(x):
            return jnp.sum(x, axis=0, keepdims=True)

        def prep(ti, buf):
            for n, ref in enumerate((r_ref, w_ref, k_ref, kk_ref, b_ref)):
                buf[n] = _k_cols(ref[ti])

        def flush(ti, buf):
            for n, ref in enumerate((dr_ref, dk_ref, db_ref, dw_ref, dkk_ref)):
                ref[ti] = _k_rows(buf[n])

        def step(ti, ds, sp, cur, accs):
            rc, wc, kc, kkc, bc = (cur[n] for n in range(5))
            vt, dyt = v_ref[ti], dy_ref[ti]
            acc = None
            new, dvs = [], []
            for j in range(V_HI):
                st = st_ref[ti, j]
                dsj = ds[j] + rc * dyt[j:j + 1]
                sa = -colsum(sp[j] * kkc)
                dsa = colsum(dsj * bc)
                dvs.append(colsum(dsj * kc))
                parts = (st * dyt[j:j + 1], dsj * vt[j:j + 1], dsj * sa, dsj * sp[j], -(sp[j] * dsa))
                acc = parts if acc is None else tuple(a + q for a, q in zip(acc, parts))
                new.append(dsj * wc - kkc * dsa)
            dv_ref[ti] = jnp.concatenate(dvs, axis=0)
            for n in range(5):
                accs[n] = acc[n]
            return tuple(new)

        def states_before(ti):
            return tuple(st_ref[ti - 1, j] for j in range(V_HI))

        def pair(i, ds):
            ta = tc - 1 - 2 * i
            prep(ta - 1, cols_b)
            flush(jnp.minimum(ta + 1, tc - 1), accs_b)
            ds = step(ta, ds, states_before(ta), cols_a, accs_a)
            prep(ta - 2, cols_a)
            flush(ta, accs_a)
            return step(ta - 1, ds, states_before(ta - 1), cols_b, accs_b)

        prep(tc - 1, cols_a)
        accs_b[...] = jnp.zeros_like(accs_b)
        ds = lax.fori_loop(0, tc // 2 - 1, pair, tuple(ds_scr[j] for j in range(V_HI)))
        prep(0, cols_b)
        flush(2, accs_b)
        ds = step(1, ds, states_before(1), cols_a, accs_a)
        flush(1, accs_a)
        keep = jnp.where(pl.program_id(0) == nb - 1, 0.0, 1.0)
        ds = step(0, ds, tuple(stp_ref[0, j] * keep for j in range(V_HI)), cols_b, accs_b)
        flush(0, accs_b)
        for j in range(V_HI):
            ds_scr[j] = ds[j]

    key_out = jax.ShapeDtypeStruct((t, RWKV_HEADS, HEAD_DIM), F32)
    return pl.pallas_call(
        body, name="wkv_bwd", grid=(nb,),
        in_specs=[key_spec] * 5 + [row_spec, st_spec, stp_spec, row_spec],
        out_specs=[key_spec] * 5 + [row_spec],
        out_shape=[key_out] * 5 + [jax.ShapeDtypeStruct((t, V_HI, LANES), F32)],
        scratch_shapes=[pltpu.VMEM((V_HI, HEAD_DIM, LANES), F32)] + [pltpu.VMEM((5, HEAD_DIM, LANES), F32)] * 4,
        compiler_params=_cparams(dimension_semantics=("arbitrary",)),
    )(r, w, k, kk, b, v, states, states, dy)


PAIR = 2 * HEAD_DIM
N_PAIRS = ATTN_HEADS // 2
Q_COL0 = RWKV_COLS // PAIR
K_COL0 = Q_COL0 + N_PAIRS
V_COL0 = K_COL0 + N_PAIRS


def _swap_halves(x):
    lane = lax.broadcasted_iota(jnp.int32, x.shape, 1)
    return jnp.where((lane & (HEAD_DIM - 1)) < HEAD_DIM // 2, pltpu.roll(x, PAIR - HEAD_DIM // 2, 1),
                     pltpu.roll(x, HEAD_DIM // 2, 1))


@jax.custom_vjp
def _rope(x, cosf, sinf):
    return x * cosf + _swap_halves(x) * sinf


def _rope_fwd(x, cosf, sinf):
    return _rope(x, cosf, sinf), (cosf, sinf)


def _rope_bwd(res, d):
    cosf, sinf = res
    return d * cosf + _swap_halves(d * sinf), jnp.zeros_like(cosf), jnp.zeros_like(sinf)


_rope.defvjp(_rope_fwd, _rope_bwd)


def _head_sums(x):
    lane = lax.broadcasted_iota(jnp.int32, x.shape, 1)
    lo = jnp.where(lane < HEAD_DIM, 1.0, 0.0)
    hi = 1.0 - lo
    return lo * jnp.sum(x * lo, axis=1, keepdims=True) + hi * jnp.sum(x * hi, axis=1, keepdims=True)


def _f_qk(x, cosf, sinf, gain2):
    xn = x * lax.rsqrt(_head_sums(x * x) * (1.0 / HEAD_DIM) + RMS_EPS) * gain2
    return _rope(xn, cosf, sinf)


def _qk_prep(z, tab, q_gain, k_gain):
    t = z.shape[0]
    tr = 1024

    def body(z_ref, c_ref, s_ref, qg_ref, kg_ref, o_ref):
        g = jnp.where(pl.program_id(0) < N_PAIRS, qg_ref[...], kg_ref[...])
        o_ref[...] = _f_qk(z_ref[...], c_ref[...], s_ref[...], jnp.concatenate([g, g], axis=1))

    gain = pl.BlockSpec((1, HEAD_DIM), lambda c, i: (0, 0))
    return pl.pallas_call(
        body, name="qk_prep", grid=(2 * N_PAIRS, t // tr),
        in_specs=[pl.BlockSpec((tr, PAIR), lambda c, i: (i, Q_COL0 + c)), pl.BlockSpec((tr, PAIR), lambda c, i: (i, 0)),
                  pl.BlockSpec((tr, PAIR), lambda c, i: (i, 1)), gain, gain],
        out_specs=pl.BlockSpec((tr, PAIR), lambda c, i: (i, c)),
        out_shape=jax.ShapeDtypeStruct((t, 2 * N_PAIRS * PAIR), F32),
        compiler_params=_cparams(dimension_semantics=("parallel", "parallel")),
    )(z, tab, tab, q_gain, k_gain)


def _qk_prep_bwd(z, tab, q_gain, k_gain, dq, dk):
    t = z.shape[0]
    tr = 1024

    def body(z_ref, c_ref, s_ref, qg_ref, kg_ref, dq_ref, dk_ref, dz_ref, dqg_ref, dkg_ref):
        c, i = pl.program_id(0), pl.program_id(1)
        is_q = c < N_PAIRS
        g = jnp.where(is_q, qg_ref[...], kg_ref[...])
        d = jnp.where(is_q, dq_ref[...], dk_ref[...])
        _, vjp = jax.vjp(lambda xx, gg: _f_qk(xx, c_ref[...], s_ref[...], gg), z_ref[...],
                         jnp.concatenate([g, g], axis=1))
        dx, dg2 = vjp(d)
        dz_ref[...] = dx.astype(dz_ref.dtype)
        dg = dg2[:, :HEAD_DIM] + dg2[:, HEAD_DIM:]
        first_q = jnp.logical_and(c == 0, i == 0)
        first_k = jnp.logical_and(c == N_PAIRS, i == 0)

        @pl.when(first_q)
        def _():
            dqg_ref[...] = dg

        @pl.when(jnp.logical_and(is_q, jnp.logical_not(first_q)))
        def _():
            dqg_ref[...] += dg

        @pl.when(first_k)
        def _():
            dkg_ref[...] = dg

        @pl.when(jnp.logical_and(jnp.logical_not(is_q), jnp.logical_not(first_k)))
        def _():
            dkg_ref[...] += dg

    gain = pl.BlockSpec((1, HEAD_DIM), lambda c, i: (0, 0))
    return pl.pallas_call(
        body, name="qk_prep_bwd", grid=(2 * N_PAIRS, t // tr),
        in_specs=[pl.BlockSpec((tr, PAIR), lambda c, i: (i, Q_COL0 + c)), pl.BlockSpec((tr, PAIR), lambda c, i: (i, 0)),
                  pl.BlockSpec((tr, PAIR), lambda c, i: (i, 1)), gain, gain,
                  pl.BlockSpec((tr, PAIR), lambda c, i: (i, jnp.minimum(c, N_PAIRS - 1))),
                  pl.BlockSpec((tr, PAIR), lambda c, i: (i, jnp.maximum(c - N_PAIRS, 0)))],
        out_specs=[pl.BlockSpec((tr, PAIR), lambda c, i: (i, c)), gain, gain],
        out_shape=[jax.ShapeDtypeStruct((t, 2 * N_PAIRS * PAIR), BF16), jax.ShapeDtypeStruct((1, HEAD_DIM), F32),
                   jax.ShapeDtypeStruct((1, HEAD_DIM), F32)],
        compiler_params=_cparams(dimension_semantics=("arbitrary", "arbitrary")),
    )(z, tab, tab, q_gain, k_gain, dq, dk)


def _attn_block(q, kp, kc, vp, vc, kmin):
    k2 = jnp.concatenate([kp, kc], axis=0)
    v2 = jnp.concatenate([vp, vc], axis=0)
    s = _bdot_nt(q, k2) * (HEAD_DIM ** -0.5)
    qi = lax.broadcasted_iota(jnp.int32, s.shape, 0)
    kj = lax.broadcasted_iota(jnp.int32, s.shape, 1)
    dist = qi + BAND_BLOCK - kj
    valid = (dist >= 0) & (dist <= BAND_BLOCK) & (kj >= kmin)
    s = jnp.where(valid, s, NEG_INF)
    m = lax.stop_gradient(jnp.max(s, axis=-1, keepdims=True))
    e = jnp.exp(s - m)
    l = jnp.sum(e, axis=-1, keepdims=True)
    o = _bdot(e, v2) / l
    return o, m + jnp.log(l)


def _fold(src_ref, dst_ref, dil):
    t = src_ref.shape[0]
    ln = t // dil
    for j in range(dil):
        dst_ref[j * ln:(j + 1) * ln, :] = src_ref[pl.ds(j, ln, stride=dil), :]


def _unfold(src_ref, dst_ref, dil):
    t = src_ref.shape[0]
    ln = t // dil
    for j in range(dil):
        dst_ref[pl.ds(j, ln, stride=dil), :] = src_ref[j * ln:(j + 1) * ln, :]


def _per_group(fn):
    pair = pl.program_id(0)
    for gi, (_, dil) in enumerate(ATTN_GROUPS):
        @pl.when(jnp.logical_or(pair == 2 * gi, pair == 2 * gi + 1))
        def _(dil=dil):
            fn(dil)


def _block_rows(idx, blocks_per_seq):
    first = (idx & (blocks_per_seq - 1)) == 0
    cur = pl.ds(pl.multiple_of(idx * BAND_BLOCK, BAND_BLOCK), BAND_BLOCK)
    prev = pl.ds(pl.multiple_of(jnp.maximum(idx - 1, 0) * BAND_BLOCK, BAND_BLOCK), BAND_BLOCK)
    return first, cur, prev


def _heads(x):
    return x[:, :HEAD_DIM], x[:, HEAD_DIM:]


def _attn_fwd(qk, z):
    t = z.shape[0]
    n_blocks = t // BAND_BLOCK

    def body(q_ref, k_ref, v_ref, o_ref, lse_ref, qf, kf, vf, of, lf):
        def run(dil):
            _fold(q_ref, qf, dil)
            _fold(k_ref, kf, dil)
            _fold(v_ref, vf, dil)
            blocks_per_seq = n_blocks // dil

            def block(idx, carry):
                first, cur, prev = _block_rows(idx, blocks_per_seq)
                kmin = jnp.where(first, BAND_BLOCK, 0)
                outs, lses = [], []
                for q, kp, kc, vp, vc in zip(_heads(qf[cur, :]), _heads(kf[prev, :]), _heads(kf[cur, :]),
                                             _heads(vf[prev, :]), _heads(vf[cur, :])):
                    o, ls = _attn_block(q, kp, kc, vp, vc, kmin)
                    outs.append(o)
                    lses.append(jnp.broadcast_to(ls, o.shape))
                of[cur, :] = jnp.concatenate(outs, axis=1)
                lf[cur, :] = jnp.concatenate(lses, axis=1)
                return carry

            lax.fori_loop(0, n_blocks, block, 0)
            _unfold(of, o_ref, dil)
            _unfold(lf, lse_ref, dil)

        _per_group(run)

    slab = jax.ShapeDtypeStruct((t, N_PAIRS * PAIR), F32)
    out_spec = pl.BlockSpec((t, PAIR), lambda p: (0, p))
    return pl.pallas_call(
        body, name="attn_fwd", grid=(N_PAIRS,),
        in_specs=[pl.BlockSpec((t, PAIR), lambda p: (0, p)), pl.BlockSpec((t, PAIR), lambda p: (0, N_PAIRS + p)),
                  pl.BlockSpec((t, PAIR), lambda p: (0, V_COL0 + p))],
        out_specs=[out_spec, out_spec], out_shape=[slab, slab],
        scratch_shapes=[pltpu.VMEM((t, PAIR), F32)] * 5,
        compiler_params=_cparams(dimension_semantics=("parallel",)),
    )(qk, qk, z)


def _attn_bwd(qk, z, do, dlse):
    t = z.shape[0]
    n_blocks = t // BAND_BLOCK

    def body(q_ref, k_ref, v_ref, do_ref, dl_ref, dq_ref, dk_ref, dv_ref, qf, kf, vf, dof, dlf, dqf, dkf, dvf):
        def run(dil):
            for src, dst in ((q_ref, qf), (k_ref, kf), (v_ref, vf), (do_ref, dof), (dl_ref, dlf)):
                _fold(src, dst, dil)
            blocks_per_seq = n_blocks // dil

            def block(idx, carry):
                first, cur, prev = _block_rows(idx, blocks_per_seq)
                kmin = jnp.where(first, BAND_BLOCK, 0)
                grads = []
                for q, kp, kc, vp, vc, do_h, dl_h in zip(
                        _heads(qf[cur, :]), _heads(kf[prev, :]), _heads(kf[cur, :]), _heads(vf[prev, :]),
                        _heads(vf[cur, :]), _heads(dof[cur, :]), _heads(dlf[cur, :])):
                    _, vjp = jax.vjp(functools.partial(_attn_block, kmin=kmin), q, kp, kc, vp, vc)
                    grads.append(vjp((do_h, jnp.sum(dl_h, axis=1, keepdims=True))))
                dq, dkp, dkc, dvp, dvc = (jnp.concatenate([a, b], axis=1) for a, b in zip(*grads))
                dqf[cur, :] = dq
                dkf[cur, :] = dkc
                dvf[cur, :] = dvc

                @pl.when(jnp.logical_not(first))
                def _():
                    dkf[prev, :] += dkp
                    dvf[prev, :] += dvp

                return carry

            lax.fori_loop(0, n_blocks, block, 0)
            _unfold(dqf, dq_ref, dil)
            _unfold(dkf, dk_ref, dil)
            _unfold(dvf, dv_ref, dil)

        _per_group(run)

    slab = jax.ShapeDtypeStruct((t, N_PAIRS * PAIR), F32)
    own = pl.BlockSpec((t, PAIR), lambda p: (0, p))
    return pl.pallas_call(
        body, name="attn_bwd", grid=(N_PAIRS,),
        in_specs=[own, pl.BlockSpec((t, PAIR), lambda p: (0, N_PAIRS + p)),
                  pl.BlockSpec((t, PAIR), lambda p: (0, V_COL0 + p)), own, own],
        out_specs=[own] * 3, out_shape=[slab] * 3,
        scratch_shapes=[pltpu.VMEM((t, PAIR), F32)] * 8,
        compiler_params=_cparams(dimension_semantics=("parallel",)),
    )(qk, qk, z, do, dlse)


def _f_comb(o1, o2, o3, l1, l2, l3):
    m = jnp.maximum(jnp.maximum(l1, l2), l3)
    e1, e2, e3 = jnp.exp(l1 - m), jnp.exp(l2 - m), jnp.exp(l3 - m)
    den = e1 + e2 + e3
    return (e1 / den) * o1 + (e2 / den) * o2 + (e3 / den) * o3


def _all_gather_hbm(name, arrs):
    na = len(arrs)

    def body(*refs):
        x_refs, out_refs = refs[:na], refs[na:2 * na]
        send_sems, recv_sems, local_sems = refs[2 * na:]
        mx, my, mc = lax.axis_index("x"), lax.axis_index("y"), lax.axis_index("c")
        me, sibling = (mx, my, mc), (mx, my, 1 - mc)
        chips = [(1 - mx, my), (mx, 1 - my), (1 - mx, 1 - my)]

        def slot(a, px, py, pc):
            return out_refs[a].at[4 * px + 2 * py + pc]

        def copy(a, k, block, to, src=None):
            return pltpu.make_async_remote_copy(
                src_ref=slot(a, *block) if src is None else src, dst_ref=slot(a, *block),
                send_sem=send_sems.at[a, k], recv_sem=recv_sems.at[a, k], device_id=to, device_id_type=MESH)

        mine = [pltpu.make_async_copy(x_refs[a], slot(a, *me), local_sems.at[a]) for a in range(na)]
        for cp in mine:
            cp.start()
        first = []
        for a in range(na):
            first.append(copy(a, 0, me, sibling, src=x_refs[a]))
            first += [copy(a, 1 + j, me, (*chip, mc), src=x_refs[a]) for j, chip in enumerate(chips)]
        for cp in first:
            cp.start()
        passed = []
        for j, chip in enumerate(chips):
            for a in range(na):
                copy(a, 1 + j, (*chip, mc), me).wait_recv()
                passed.append(copy(a, 4 + j, (*chip, mc), sibling))
                passed[-1].start()
        for a in range(na):
            copy(a, 0, sibling, me).wait_recv()
            for j, chip in enumerate(chips):
                copy(a, 4 + j, (*chip, 1 - mc), me).wait_recv()
        for cp in first + passed:
            cp.wait_send()
        for cp in mine:
            cp.wait()

    hbm = pl.BlockSpec(memory_space=pl.ANY)
    return pl.pallas_call(
        body, name=name,
        out_shape=[jax.ShapeDtypeStruct((N_DEV,) + a.shape, a.dtype) for a in arrs],
        in_specs=[hbm] * na, out_specs=[hbm] * na,
        scratch_shapes=[pltpu.SemaphoreType.DMA((na, 7)), pltpu.SemaphoreType.DMA((na, 7)),
                        pltpu.SemaphoreType.DMA((na,))],
    )(*arrs)


def _all_gather_vmem(x):
    rws, cols = x.shape

    def body(x_ref, out_ref, send_sems, recv_sems):
        mx, my, mc = lax.axis_index("x"), lax.axis_index("y"), lax.axis_index("c")
        me, sibling = (mx, my, mc), (mx, my, 1 - mc)
        chips = [(1 - mx, my), (mx, 1 - my), (1 - mx, 1 - my)]

        def slot(px, py, pc):
            return out_ref.at[4 * px + 2 * py + pc]

        def copy(k, block, to, src=None):
            return pltpu.make_async_remote_copy(
                src_ref=slot(*block) if src is None else src, dst_ref=slot(*block),
                send_sem=send_sems.at[k], recv_sem=recv_sems.at[k], device_id=to, device_id_type=MESH)

        first = [copy(0, me, sibling, src=x_ref)]
        first += [copy(1 + j, me, (*chip, mc), src=x_ref) for j, chip in enumerate(chips)]
        for cp in first:
            cp.start()
        out_ref[4 * mx + 2 * my + mc] = x_ref[...]
        passed = [copy(4 + j, (*chip, mc), sibling) for j, chip in enumerate(chips)]
        for j, chip in enumerate(chips):
            copy(1 + j, (*chip, mc), me).wait_recv()
            passed[j].start()
        copy(0, sibling, me).wait_recv()
        for j, chip in enumerate(chips):
            copy(4 + j, (*chip, 1 - mc), me).wait_recv()
        for cp in first + passed:
            cp.wait_send()

    return pl.pallas_call(
        body, name="all_gather_small",
        out_shape=jax.ShapeDtypeStruct((N_DEV, rws, cols), x.dtype),
        in_specs=[pl.BlockSpec(memory_space=pltpu.VMEM)], out_specs=pl.BlockSpec(memory_space=pltpu.VMEM),
        scratch_shapes=[pltpu.SemaphoreType.DMA((7,)), pltpu.SemaphoreType.DMA((7,))],
    )(x)


def _all_to_all_hbm(name, arrs):
    na = len(arrs)

    def body(*refs):
        g_refs, out_refs = refs[:na], refs[na:2 * na]
        send_sems, recv_sems, local_sems = refs[2 * na:]
        mx, my, mc = lax.axis_index("x"), lax.axis_index("y"), lax.axis_index("c")
        me = 4 * mx + 2 * my + mc
        mine = [pltpu.make_async_copy(g_refs[a].at[me], out_refs[a].at[me], local_sems.at[a]) for a in range(na)]
        for cp in mine:
            cp.start()
        copies = []
        for k in range(1, N_DEV):
            px, py, pc = mx ^ (k >> 2), my ^ ((k >> 1) & 1), mc ^ (k & 1)
            peer = 4 * px + 2 * py + pc
            for a in range(na):
                copies.append(pltpu.make_async_remote_copy(
                    src_ref=g_refs[a].at[peer], dst_ref=out_refs[a].at[me], send_sem=send_sems.at[a, k - 1],
                    recv_sem=recv_sems.at[a, k - 1], device_id=(px, py, pc), device_id_type=MESH))
        for cp in copies:
            cp.start()
        for cp in copies:
            cp.wait_recv()
        for cp in copies:
            cp.wait_send()
        for cp in mine:
            cp.wait()

    hbm = pl.BlockSpec(memory_space=pl.ANY)
    return pl.pallas_call(
        body, name=name,
        out_shape=[jax.ShapeDtypeStruct(a.shape, a.dtype) for a in arrs],
        in_specs=[hbm] * na, out_specs=[hbm] * na,
        scratch_shapes=[pltpu.SemaphoreType.DMA((na, 7)), pltpu.SemaphoreType.DMA((na, 7)),
                        pltpu.SemaphoreType.DMA((na,))],
    )(*arrs)


def _scatter_copies(g_refs, land_refs, send_sems, recv_sems):
    mx, my, mc = lax.axis_index("x"), lax.axis_index("y"), lax.axis_index("c")
    me = 4 * mx + 2 * my + mc
    copies = []
    for k in range(1, N_DEV):
        px, py, pc = mx ^ (k >> 2), my ^ ((k >> 1) & 1), mc ^ (k & 1)
        peer = 4 * px + 2 * py + pc
        for a, (g_ref, land_ref) in enumerate(zip(g_refs, land_refs)):
            copies.append(pltpu.make_async_remote_copy(
                src_ref=g_ref.at[peer], dst_ref=land_ref.at[me], send_sem=send_sems.at[a * (N_DEV - 1) + k - 1],
                recv_sem=recv_sems.at[a * (N_DEV - 1) + k - 1], device_id=(px, py, pc), device_id_type=MESH))
    return copies


_HBM = pl.BlockSpec(memory_space=pltpu.HBM)
_SEM = pl.BlockSpec(memory_space=pltpu.SEMAPHORE)
_DATAFLOW = pltpu.SideEffectType.DATAFLOW_SIDE_EFFECTING


def _gather_copies(x_refs, land_refs, send_sems, recv_sems):
    mx, my, mc = lax.axis_index("x"), lax.axis_index("y"), lax.axis_index("c")
    me = 4 * mx + 2 * my + mc
    copies = []
    for k in range(1, N_DEV):
        px, py, pc = mx ^ (k >> 2), my ^ ((k >> 1) & 1), mc ^ (k & 1)
        for a, (x_ref, land_ref) in enumerate(zip(x_refs, land_refs)):
            copies.append(pltpu.make_async_remote_copy(
                src_ref=x_ref, dst_ref=land_ref.at[me], send_sem=send_sems.at[a * (N_DEV - 1) + k - 1],
                recv_sem=recv_sems.at[a * (N_DEV - 1) + k - 1], device_id=(px, py, pc), device_id_type=MESH))
    return copies


def _exchange_start(name, copies, srcs, lands, after):
    na = len(srcs)

    def body(*refs):
        for cp in copies(refs[:na], refs[na:2 * na], refs[2 * na + 1], refs[2 * na + 2]):
            cp.start()
        refs[-1][...] = jnp.zeros_like(refs[-1])

    in_hbm = lambda a: pltpu.with_memory_space_constraint(a, pltpu.HBM)
    outs = pl.pallas_call(
        body, name=name,
        out_shape=(pltpu.SemaphoreType.DMA((na * (N_DEV - 1),)), pltpu.SemaphoreType.DMA((na * (N_DEV - 1),)),
                   *[pltpu.HBM(a.shape, a.dtype) for a in list(srcs) + list(lands)],
                   jax.ShapeDtypeStruct((8, LANES), F32)),
        in_specs=[_HBM] * (2 * na) + [pl.BlockSpec(memory_space=pl.ANY)],
        out_specs=(_SEM, _SEM, *[_HBM] * (2 * na), pl.BlockSpec(memory_space=pltpu.VMEM)),
        input_output_aliases={i: 2 + i for i in range(2 * na)},
        compiler_params=pltpu.CompilerParams(has_side_effects=_DATAFLOW),
    )(*[in_hbm(a) for a in srcs], *[in_hbm(a) for a in lands], after)
    return outs[0], outs[1], outs[2:2 + na], outs[2 + na:2 + 2 * na], outs[-1]


def _exchange_wait(name, copies, send_sems, recv_sems, srcs, lands, after):
    na = len(srcs)

    def body(*refs):
        for cp in copies(refs[:na], refs[na:2 * na], refs[2 * na], refs[2 * na + 1]):
            cp.wait_send()
            cp.wait_recv()

    outs = pl.pallas_call(
        body, name=name,
        out_shape=[pltpu.HBM(a.shape, a.dtype) for a in list(srcs) + list(lands)],
        in_specs=[_HBM] * (2 * na) + [_SEM, _SEM, pl.BlockSpec(memory_space=pl.ANY)], out_specs=[_HBM] * (2 * na),
        input_output_aliases={i: i for i in range(2 * na)},
        compiler_params=pltpu.CompilerParams(has_side_effects=_DATAFLOW),
    )(*srcs, *lands, send_sems, recv_sems, after)
    return outs[:na], outs[na:]


def _sum_slots(name, g, tr):
    _, rws, cols = g.shape

    def body(g_ref, o_ref):
        acc = g_ref[0].astype(F32)
        for j in range(1, N_DEV):
            acc = acc + g_ref[j].astype(F32)
        o_ref[...] = acc

    return pl.pallas_call(
        body, name=name, grid=(rws // tr,),
        in_specs=[pl.BlockSpec((N_DEV, tr, cols), lambda i: (0, i, 0))],
        out_specs=pl.BlockSpec((tr, cols), lambda i: (i, 0)),
        out_shape=jax.ShapeDtypeStruct((rws, cols), F32),
        compiler_params=_cparams(dimension_semantics=("parallel",)),
    )(g)


def _adam_math(wv, gv, mv, vv):
    mn = ADAM_B1 * mv + (1.0 - ADAM_B1) * gv
    vn = ADAM_B2 * vv + (1.0 - ADAM_B2) * jnp.square(gv)
    m_hat = mn / (1.0 - ADAM_B1 ** ADAM_STEP)
    v_hat = vn / (1.0 - ADAM_B2 ** ADAM_STEP)
    delta = -ADAM_LR * (m_hat / (jnp.sqrt(v_hat) + ADAM_EPS) + ADAM_WD * wv)
    return delta, mn, vn


def _adamw(name, w, g, m, v, tr):
    return _rowwise(name, _adam_math, [w, g, m, v], [], [(LANES, F32)] * 3, tr=tr)


def _adamw_slots(name, recv, own, rb, cb, cw, w, m, v, tr, order):
    nr, nc = w.shape

    def body(g_ref, own_ref, w_ref, m_ref, v_ref, order_ref, go_ref, d_ref, mo_ref, vo_ref):
        me = 4 * lax.axis_index("x") + 2 * lax.axis_index("y") + lax.axis_index("c")
        acc = None
        for s in range(N_DEV):
            part = jnp.where(me == s, own_ref[s], g_ref[s]).astype(F32)
            acc = part if acc is None else acc + part
        g = acc[:, :nc]
        go_ref[...] = g
        d_ref[...], mo_ref[...], vo_ref[...] = _adam_math(w_ref[...], g, m_ref[...], v_ref[...])

    nat = pl.BlockSpec((tr, nc), lambda i: (i, 0))
    slots = pl.BlockSpec((N_DEV, tr, cw), lambda i: (0, rb + i, cb))
    return pl.pallas_call(
        body, name=name, grid=(nr // tr,),
        in_specs=[slots, slots, nat, nat, nat, pl.BlockSpec(memory_space=pl.ANY)],
        out_specs=[nat] * 4, out_shape=[jax.ShapeDtypeStruct((nr, nc), F32)] * 4,
        compiler_params=_cparams(dimension_semantics=("parallel",)),
    )(recv, own, w, m, v, order)


def _local_blocks(w):
    pad_cols = lambda a: jnp.pad(a, ((0, 0), (0, FF_PAD - FF_SHARD)))
    pad_rows = lambda a: jnp.pad(a, ((0, FF_PAD - FF_SHARD), (0, 0)))
    gate_up = lambda tag: jnp.concatenate([pad_cols(w[tag + "_w_gate"]), pad_cols(w[tag + "_w_up"])], axis=1)
    blocks = {
        "ffn1_gu": gate_up("ffn1"), "ffn1_d": pad_rows(w["ffn1_w_down"]), "w_in": w["w_in"],
        "lora": jnp.concatenate([w["rwkv_w2"], w["rwkv_a2"], w["rwkv_g2"]], axis=0),
        "br": jnp.concatenate([w["w_br_rwkv"], w["w_br_attn"], w["ple_w_proj"]], axis=0),
        "w_out": w["w_out"], "ffn2_gu": gate_up("ffn2"), "ffn2_d": pad_rows(w["ffn2_w_down"]),
        "ple_gate": w["ple_w_gate"],
    }
    return {n: a.astype(BF16) for n, a in blocks.items()}


GATHER_GROUPS = {"head": ("ffn1_gu", "ffn1_d"), "mid": ("w_in", "lora"),
                 "rest": ("br", "w_out", "ffn2_gu", "ffn2_d", "ple_gate")}

SCATTER_GROUPS = {"tail": ("ple_gate", "ple_proj", "ffn2_gu", "ffn2_d"), "mixer": ("w_out", "br", "lora", "w_in"),
                  "ffn1_down": ("ffn1_d",), "head": ("ffn1_gu",)}

ADAM_PLAN = (
    ("ffn1_w_gate", "ffn1_gu", 0, 0, FF_PAD, 256), ("ffn1_w_up", "ffn1_gu", 0, 1, FF_PAD, 256),
    ("ffn1_w_down", "ffn1_d", 0, 0, D_MODEL, FF_SHARD // 2), ("w_in", "w_in", 0, 0, IN_SHARD, 256),
    ("rwkv_w2", "lora", 0, 0, HEAD_DIM, 64), ("rwkv_a2", "lora", 1, 0, HEAD_DIM, 64),
    ("rwkv_g2", "lora", 2, 0, HEAD_DIM, 64),
    ("w_br_rwkv", "br", 0, 0, OUT_SHARD, 256), ("w_br_attn", "br", 2, 0, OUT_SHARD, 256),
    ("ple_w_proj", "ple_proj", 0, 0, OUT_SHARD, 256), ("w_out", "w_out", 0, 0, D_MODEL, OUT_SHARD),
    ("ffn2_w_gate", "ffn2_gu", 0, 0, FF_PAD, 256), ("ffn2_w_up", "ffn2_gu", 0, 1, FF_PAD, 256),
    ("ffn2_w_down", "ffn2_d", 0, 0, D_MODEL, FF_SHARD // 2), ("ple_w_gate", "ple_gate", 0, 0, D_MODEL, OUT_SHARD),
)


def _pack_small(arrs, rows):
    flat = jnp.concatenate([a.reshape(-1) for a in arrs])
    return jnp.pad(flat, (0, rows * LANES - flat.shape[0])).reshape(rows, LANES)


def _unpack_small(flat, like):
    flat = flat.reshape(-1)
    out, off = [], 0
    for a in like:
        out.append(flat[off:off + a.size].reshape(a.shape))
        off += a.size
    return out


def _local_step(x, p, pos, target, sm, wg, fetch, on_grads):
    t = x.shape[0]

    w_d1 = wg["ffn1_d"].reshape(FF_HID, D_MODEL)
    x1, ffn1_saved, _ = _ffn_fwd("ffn1", x, sm["ffn1_norm"], wg["ffn1_gu"], lambda after: w_d1)
    wg = {**wg, **fetch("mid", x1)}
    full_cols = lambda blk: blk.transpose(1, 0, 2).reshape(blk.shape[1], N_DEV * blk.shape[2])
    lora_w2 = full_cols(wg["lora"][:, :DECAY_LORA])
    lora_a2 = full_cols(wg["lora"][:, DECAY_LORA:DECAY_LORA + ICLR_LORA])
    lora_g2 = full_cols(wg["lora"][:, DECAY_LORA + ICLR_LORA:])
    h2 = _norm_fwd("mix_norm", x1, sm["mix_norm"])
    z = _mmc_nn("w_in", h2, wg["w_in"], 0, 0, IN_SHARD)
    z_g = (z, 2 * D_MODEL, (RWKV_COLS + 3 * ATTN_DIM) // (2 * D_MODEL))

    r, k, v, lo, gd = _shift_fwd(z, sm["rwkv_mu"])
    zero_lo = jnp.zeros((DECAY_LORA, RWKV_DIM), BF16)
    w2p = jnp.concatenate([lora_w2, zero_lo], axis=0).astype(F32)
    a2p = jnp.concatenate([zero_lo, lora_a2], axis=0).astype(F32)
    pre_params = [sm["rwkv_w0"], w2p, sm["rwkv_a0"], a2p, lora_g2.astype(F32), sm["rwkv_k_k"], sm["rwkv_k_a"]]
    wide = [(RWKV_DIM, F32)]
    k2, kk, a, decay, g = _rowwise("rwkv_pre", _f_pre, [k, lo, gd], pre_params, wide * 5)
    kkn, b = _pairwise("rwkv_kk", _f_kk, [kk, a], [], 2)
    scan_in = [u.reshape(t, RWKV_HEADS, HEAD_DIM) for u in (r, decay, k2, kkn, b)]
    v_rows = _to_v_rows(v)
    y_rows, states = _wkv_fwd(*scan_in, v_rows)
    y = _from_v_rows(y_rows)
    post_params = [sm["rwkv_gn_w"], sm["rwkv_gn_b"], sm["rwkv_r_k"]]
    post_rows = [y, r, k2, v, g]
    y_rwkv = _pairwise("rwkv_post", lambda *av: (_f_post(*av),), post_rows, post_params, 1)[0]

    inv_freq = 1.0 / (ROPE_THETA ** (jnp.arange(0, HEAD_DIM, 2, dtype=F32) / HEAD_DIM))
    freq2 = jnp.tile(inv_freq, 2 * PAIR // HEAD_DIM).reshape(1, PAIR)
    half = jnp.ones((HEAD_DIM // 2,), F32)
    sign2 = jnp.tile(jnp.concatenate([-half, half]), PAIR // HEAD_DIM).reshape(1, PAIR)

    def rope_table(posv, fr, sg):
        ang = posv * fr
        return (jnp.concatenate([jnp.cos(ang), jnp.sin(ang) * sg], axis=1),)
    tab = _rowwise("rope_table", rope_table, [pos.astype(F32).reshape(t, 1)], [freq2, sign2], [(2 * PAIR, F32)])[0]
    qk = _qk_prep(z, tab, sm["q_norm"], sm["k_norm"])
    o_all, lse_all = _attn_fwd(qk, z)
    gw = HEADS_PER_GROUP * HEAD_DIM

    def by_group(ov, lv):
        return [ov[:, i * gw:(i + 1) * gw] for i in range(3)] + [lv[:, i * gw:(i + 1) * gw] for i in range(3)]
    y_attn = _rowwise("attn_comb", lambda ov, lv: (_f_comb(*by_group(ov, lv)),), [o_all, lse_all], [], [(gw, F32)])[0]

    wg = {**wg, **fetch("rest", y_rwkv)}
    w_d2 = wg["ffn2_d"].reshape(FF_HID, D_MODEL)
    w_out = wg["w_out"].reshape(D_MODEL, D_MODEL)
    w_pg = wg["ple_gate"].reshape(D_MODEL, D_MODEL)
    u_r =_mmc_nn("br_rwkv", y_rwkv, wg["br"], 0, 0, OUT_SHARD)
    u_a = _mmc_nn("br_attn", y_attn, wg["br"], 2, 0, OUT_SHARD)

    def f_merge(zgr, zga, ur, ua):
        return _sigmoid(zgr) * ur + _sigmoid(zga) * ua
    merged = _rowwise("merge", lambda zg, ur, ua: (f_merge(zg[:, :D_MODEL], zg[:, D_MODEL:], ur, ua),),
                      [z_g, u_r, u_a], [], [(D_MODEL, BF16)])[0]
    x2 = _mm("w_out", merged, w_out, "nn", res=x1)
    x3, ffn2_saved, _ = _ffn_fwd("ffn2", x2, sm["ffn2_norm"], wg["ffn2_gu"], lambda after: w_d2)

    hn = _norm_fwd("ple_norm", x3, sm["ple_norm"])
    gz = _mm("ple_gate", hn, w_pg, "nn")
    pp = _mmc_nn("ple_proj", p, wg["br"], 3, 0, OUT_SHARD)

    def f_head(x3v, gzv, ppv, tg):
        sg = _sigmoid(gzv)
        err = x3v + sg * ppv - tg
        part = 0.5 * jnp.sum(jnp.mean(err * err, axis=-1, keepdims=True))
        dx4 = err * (1.0 / D_MODEL)
        return dx4, dx4 * ppv * sg * (1.0 - sg), dx4 * sg, jnp.full((1, LANES), part, F32)
    dx4, dgz, dpp, loss_row = _rowwise("ple_loss", f_head, [x3, gz, pp, target], [],
                                       [(D_MODEL, F32), (D_MODEL, BF16), (D_MODEL, BF16)], [(1, LANES)])
    loss = loss_row[0, 0]

    gs, gm = {}, {}
    row_blocks = lambda g: g.reshape(N_DEV, g.shape[0] // N_DEV, g.shape[1])
    dhn = _mm("ple_dhn", dgz, w_pg, "nt")
    gm["ple_gate"] = row_blocks(_mm("ple_dwgate", hn, dgz, "tn", out_dtype=BF16))
    gm["ple_proj"] = _mmc_tn("ple_dwproj", p, dpp, OUT_SHARD)
    dx3, gs["ple_norm"] = _norm_bwd("ple_dnorm", x3, sm["ple_norm"], dhn, dx4)

    dx2, gs["ffn2_norm"], gm["ffn2_gu"], gm["ffn2_d"] = _ffn_bwd(
        "ffn2", x2, sm["ffn2_norm"], wg["ffn2_gu"], w_d2, ffn2_saved, dx3)
    tail_token = on_grads("tail", {n: gm.pop(n) for n in SCATTER_GROUPS["tail"]})

    dmerged = _mm("w_out_dmerged", dx2, w_out, "nt")
    gm["w_out"] = row_blocks(_mm("w_out_dw", merged, dx2, "tn", out_dtype=BF16))

    def merge_bwd(zg, ur, ua, dm):
        _, vjp = jax.vjp(f_merge, zg[:, :D_MODEL], zg[:, D_MODEL:], ur, ua)
        dzr, dza, dur, dua = vjp(dm)
        return jnp.concatenate([dzr, dza], axis=1), dur, dua
    dz_g, du_r, du_a = _rowwise("merge_bwd", merge_bwd, [z_g, u_r, u_a, dmerged], [],
                                [(2 * D_MODEL, BF16), (D_MODEL, BF16), (D_MODEL, BF16)])
    dy_rwkv = _mmc_nt("br_rwkv_dy", du_r, wg["br"], 0, 0, OUT_SHARD, RWKV_DIM)
    dy_attn = _mmc_nt("br_attn_dy", du_a, wg["br"], 2, 0, OUT_SHARD, HEADS_PER_GROUP * HEAD_DIM)
    gm["br"] = jnp.concatenate([_mmc_tn("br_rwkv_dw", y_rwkv, du_r, OUT_SHARD),
                                _mmc_tn("br_attn_dw", y_attn, du_a, OUT_SHARD)], axis=1)

    def comb_bwd(ov, lv, dyv):
        _, vjp = jax.vjp(_f_comb, *by_group(ov, lv))
        d = vjp(dyv)
        return jnp.concatenate(d[:3], axis=1), jnp.concatenate(d[3:], axis=1)
    do_all, dl_all = _rowwise("attn_comb_bwd", comb_bwd, [o_all, lse_all, dy_attn], [],
                              [(ATTN_DIM, F32), (ATTN_DIM, F32)])
    dq_all, dk_all, dv_all = _attn_bwd(qk, z, do_all, dl_all)
    dqk_raw, gs["q_norm"], gs["k_norm"] = _qk_prep_bwd(z, tab, sm["q_norm"], sm["k_norm"], dq_all, dk_all)

    def post_bwd(yv, rv, k2v, vv, gv, dv_, gnw, gnb, rk):
        _, vjp = jax.vjp(_f_post, yv, rv, k2v, vv, gv, gnw, gnb, rk)
        return vjp(dv_)
    dy, dr1, dk2a, dv1, dg, d_gnw, d_gnb, d_rk = _pairwise(
        "rwkv_post_bwd", post_bwd, post_rows + [dy_rwkv], [post_params[0] + tail_token[0, 0]] + post_params[1:], 5, 3)
    gs["rwkv_gn_w"], gs["rwkv_gn_b"], gs["rwkv_r_k"] = d_gnw, d_gnb, d_rk
    dr2, ddecay, dk2b, dkkn, db, dv_rows = _wkv_bwd(*scan_in, v_rows, states, _to_v_rows(dy))
    dr2, ddecay, dk2b, dkkn, db = [u.reshape(t, RWKV_DIM) for u in (dr2, ddecay, dk2b, dkkn, db)]
    dv2 = _from_v_rows(dv_rows)

    def kk_bwd(kkv, av, dkknv, dbv, dra, drb, dva, dvb):
        _, vjp = jax.vjp(_f_kk, kkv, av)
        return (*vjp((dkknv, dbv)), dra + drb, dva + dvb)
    dkk, da, dr, dv = _pairwise("rwkv_kk_bwd", kk_bwd, [kk, a, dkkn, db, dr1, dr2, dv1, dv2], [], 4)

    def pre_bwd(kv, lov, gdv, dk2x, dk2y, dkkv, dav, ddec, dgv, w0, w2p_, a0, a2p_, g2, k_k, k_a):
        _, vjp = jax.vjp(_f_pre, kv, lov, gdv, w0, w2p_, a0, a2p_, g2, k_k, k_a)
        return vjp((dk2x + dk2y, dkkv, dav, ddec, dgv))
    lora_acc = (DECAY_LORA + ICLR_LORA, RWKV_DIM)
    dk, dlo, dgd, d_w0, d_w2p, d_a0, d_a2p, d_g2, d_kk, d_ka = _rowwise(
        "rwkv_pre_bwd", pre_bwd,
        [k, lo, gd, dk2a, dk2b, dkk, da, ddecay, dg],
        pre_params, [(RWKV_DIM, F32), (LANES, F32), (LANES, F32)],
        [(1, RWKV_DIM), lora_acc, (1, RWKV_DIM), lora_acc, (GATE_LORA, RWKV_DIM), (1, RWKV_DIM), (1, RWKV_DIM)])
    gs["rwkv_w0"], gs["rwkv_a0"], gs["rwkv_k_k"], gs["rwkv_k_a"] = d_w0, d_a0, d_kk, d_ka
    col_blocks = lambda g: g.reshape(g.shape[0], N_DEV, g.shape[1] // N_DEV).transpose(1, 0, 2)
    gm["lora"] = jnp.concatenate([col_blocks(d_w2p[:DECAY_LORA]), col_blocks(d_a2p[DECAY_LORA:]), col_blocks(d_g2)],
                                 axis=1).astype(BF16)
    dz_r, gs["rwkv_mu"] = _shift_bwd(z, sm["rwkv_mu"], dr, dk, dv, dlo, dgd)

    dz = jnp.concatenate([dz_r, dqk_raw, dv_all.astype(BF16), dz_g], axis=1)
    dh2 = _mmc_nt("w_in_dh", dz, wg["w_in"], 0, 0, IN_SHARD, D_MODEL)
    gm["w_in"] = _mmc_tn("w_in_dw", h2, dz, IN_SHARD)
    mixer_token = on_grads("mixer", {n: gm.pop(n) for n in SCATTER_GROUPS["mixer"]})
    dx1, gs["mix_norm"] = _norm_bwd("mix_dnorm", x1, sm["mix_norm"] + mixer_token[0, 0], dh2, dx2)

    dx0, gs["ffn1_norm"], gm["ffn1_gu"], _ = _ffn_bwd(
        "ffn1", x, sm["ffn1_norm"], wg["ffn1_gu"], w_d1, ffn1_saved, dx1,
        on_down=lambda blocks: on_grads("ffn1_down", {"ffn1_d": blocks}))
    return loss, dx0, gm, gs


def kernel(x, p, positions, ffn1_norm, ffn1_w_gate, ffn1_w_up, ffn1_w_down, mix_norm, w_in, rwkv_mu, rwkv_w0, rwkv_w2, rwkv_a0, rwkv_a2, rwkv_g2, rwkv_k_k, rwkv_k_a, rwkv_r_k, rwkv_gn_w, rwkv_gn_b, q_norm, k_norm, w_br_rwkv, w_br_attn, w_out, ffn2_norm, ffn2_w_gate, ffn2_w_up, ffn2_w_down, ple_norm, ple_w_gate, ple_w_proj, loss_target, m_ffn1_norm, m_ffn1_w_gate, m_ffn1_w_up, m_ffn1_w_down, m_mix_norm, m_w_in, m_rwkv_mu, m_rwkv_w0, m_rwkv_w2, m_rwkv_a0, m_rwkv_a2, m_rwkv_g2, m_rwkv_k_k, m_rwkv_k_a, m_rwkv_r_k, m_rwkv_gn_w, m_rwkv_gn_b, m_q_norm, m_k_norm, m_w_br_rwkv, m_w_br_attn, m_w_out, m_ffn2_norm, m_ffn2_w_gate, m_ffn2_w_up, m_ffn2_w_down, m_ple_norm, m_ple_w_gate, m_ple_w_proj, v_ffn1_norm, v_ffn1_w_gate, v_ffn1_w_up, v_ffn1_w_down, v_mix_norm, v_w_in, v_rwkv_mu, v_rwkv_w0, v_rwkv_w2, v_rwkv_a0, v_rwkv_a2, v_rwkv_g2, v_rwkv_k_k, v_rwkv_k_a, v_rwkv_r_k, v_rwkv_gn_w, v_rwkv_gn_b, v_q_norm, v_k_norm, v_w_br_rwkv, v_w_br_attn, v_w_out, v_ffn2_norm, v_ffn2_w_gate, v_ffn2_w_up, v_ffn2_w_down, v_ple_norm, v_ple_w_gate, v_ple_w_proj):
    args = locals()
    w = {n: args[n][0] for n in WEIGHTS}
    m = {n: args["m_" + n][0] for n in WEIGHTS}
    v = {n: args["v_" + n][0] for n in WEIGHTS}

    w_loc = _local_blocks(w)
    head = GATHER_GROUPS["head"]
    wg = dict(zip(head, _all_gather_hbm("gather_head", [w_loc[n] for n in head])))
    me = 4 * lax.axis_index("x") + 2 * lax.axis_index("y") + lax.axis_index("c")
    gathering, order_after = {}, wg[head[0]]
    for group in ("mid", "rest"):
        shards = [w_loc[n] for n in GATHER_GROUPS[group]]
        zones = [lax.dynamic_update_slice(lax.empty((N_DEV,) + a.shape, a.dtype), a[None], (me, 0, 0)) for a in shards]
        *gathering[group], order_after = _exchange_start("gather_start_" + group, _gather_copies, shards, zones,
                                                         order_after)

    def fetch(group, after):
        _, got = _exchange_wait("gather_wait_" + group, _gather_copies, *gathering[group], after)
        return dict(zip(GATHER_GROUPS[group], got))

    sm = {n: w[n].reshape(1, -1) for n in SMALL}
    sm["ffn1_norm"] = sm["ffn1_norm"] + order_after[0, 0]
    in_flight = {}

    def scatter_early(group, arrays):
        arrs = [arrays[n] for n in SCATTER_GROUPS[group]]
        *in_flight[group], token = _exchange_start("scatter_start_" + group, _scatter_copies, arrs,
                                                   [lax.empty(a.shape, a.dtype) for a in arrs], arrs[0])
        return token
    loss_part, dx, gm, gs = _local_step(x[0], p[0, 0], positions[0], loss_target[0], sm, wg, fetch, scatter_early)
    loss = lax.psum(loss_part, ("x", "y", "c"))
    head_token = scatter_early("head", gm)
    recv, own = {}, {}

    def arrived(group, after):
        sent, lands = _exchange_wait("scatter_wait_" + group, _scatter_copies, *in_flight[group], after)
        own.update(zip(SCATTER_GROUPS[group], sent))
        recv.update(zip(SCATTER_GROUPS[group], lands))
    for group in ("tail", "mixer", "ffn1_down"):
        arrived(group, head_token)
    small_like = [w[n] for n in SMALL]
    small_rows = 80
    gs_all = _all_gather_vmem(_pack_small([gs[n] for n in SMALL], small_rows))
    gs_sum = _sum_slots("sum_small_grads", gs_all, small_rows)

    res = {}
    early = [e for e in ADAM_PLAN if e[1] not in SCATTER_GROUPS["head"]]
    late = [e for e in ADAM_PLAN if e[1] in SCATTER_GROUPS["head"]]
    for n, src, rb, cb, cw, tr in early + late:
        if (n, src, rb, cb, cw, tr) == late[0]:
            arrived("head", res["delta", early[-1][0]])
        outs4 = _adamw_slots("adamw_" + n, recv[src], own[src], rb, cb, cw, w[n], m[n], v[n], tr, head_token)
        for tag, a in zip(("grad", "delta", "new_m", "new_v"), outs4):
            res[tag, n] = a[None]
    d_s, m_s, v_s = _adamw("adamw_small", _pack_small(small_like, small_rows), gs_sum,
                           _pack_small([m[n] for n in SMALL], small_rows),
                           _pack_small([v[n] for n in SMALL], small_rows), small_rows)
    for tag, small in (("grad", gs_sum), ("delta", d_s), ("new_m", m_s), ("new_v", v_s)):
        for n, a in zip(SMALL, _unpack_small(small, small_like)):
            res[tag, n] = a[None]
    outs = [loss, dx[None]]
    for tag in ("grad", "delta", "new_m", "new_v"):
        outs += [res[tag, n] for n in WEIGHTS]
    return tuple(outs)
```

```python
import functools

import jax
import jax.numpy as jnp
from jax import lax
from jax.experimental import pallas as pl
from jax.experimental.pallas import tpu as pltpu

F32, BF16 = jnp.float32, jnp.bfloat16
MESH = pl.DeviceIdType.MESH
N_DEV = 8
LANES = 128
VMEM_LIMIT = 56 * 1024 * 1024

D_MODEL = 1024
PLE_DIM = 256
HEAD_DIM = 64
RWKV_HEADS = 8
RWKV_DIM = RWKV_HEADS * HEAD_DIM
DECAY_LORA = 64
ICLR_LORA = 64
GATE_LORA = 128
GN_EPS = 64e-5
ATTN_GROUPS = ((128, 1), (512, 4), (2048, 16))
HEADS_PER_GROUP = 4
ATTN_HEADS = HEADS_PER_GROUP * len(ATTN_GROUPS)
ATTN_DIM = ATTN_HEADS * HEAD_DIM
BAND_BLOCK = 128
ROPE_THETA = 10000.0
NEG_INF = -1e30
D_FF = 2816
RMS_EPS = 1e-6
RWKV_COLS = 3 * RWKV_DIM + DECAY_LORA + ICLR_LORA + GATE_LORA
ADAM_LR = 0.001
ADAM_B1 = 0.9
ADAM_B2 = 0.999
ADAM_EPS = 1e-08
ADAM_WD = 0.01
ADAM_STEP = 10

V_LO = LANES // RWKV_HEADS
V_HI = HEAD_DIM // V_LO
SCAN_CHUNK = 64

FF_SHARD = D_FF // N_DEV
FF_PAD = -(-FF_SHARD // LANES) * LANES
FF_HID = N_DEV * FF_PAD
IN_SHARD = 6144 // N_DEV
OUT_SHARD = D_MODEL // N_DEV

SMALL = ("ffn1_norm", "mix_norm", "rwkv_mu", "rwkv_w0", "rwkv_a0", "rwkv_k_k", "rwkv_k_a", "rwkv_r_k",
         "rwkv_gn_w", "rwkv_gn_b", "q_norm", "k_norm", "ffn2_norm", "ple_norm")
WEIGHTS = ("ffn1_norm", "ffn1_w_gate", "ffn1_w_up", "ffn1_w_down", "mix_norm", "w_in", "rwkv_mu", "rwkv_w0",
           "rwkv_w2", "rwkv_a0", "rwkv_a2", "rwkv_g2", "rwkv_k_k", "rwkv_k_a", "rwkv_r_k", "rwkv_gn_w",
           "rwkv_gn_b", "q_norm", "k_norm", "w_br_rwkv", "w_br_attn", "w_out", "ffn2_norm", "ffn2_w_gate",
           "ffn2_w_up", "ffn2_w_down", "ple_norm", "ple_w_gate", "ple_w_proj")


def _cparams(**kw):
    return pltpu.CompilerParams(vmem_limit_bytes=VMEM_LIMIT, **kw)


def _tile(n, cap):
    best = None
    for t in range(LANES, min(n, cap) + 1, LANES):
        if n % t == 0:
            best = t
    return best if best is not None else n


@jax.custom_vjp
def _bdot(a, w):
    return jnp.dot(a.astype(BF16), w.astype(BF16), preferred_element_type=F32)


def _bdot_fwd(a, w):
    return _bdot(a, w), (a, w)


def _bdot_bwd(res, g):
    a, w = res
    gb = g.astype(BF16)
    da = lax.dot_general(gb, w.astype(BF16), (((1,), (1,)), ((), ())), preferred_element_type=F32)
    dw = lax.dot_general(a.astype(BF16), gb, (((0,), (0,)), ((), ())), preferred_element_type=F32)
    return da.astype(a.dtype), dw.astype(w.dtype)


_bdot.defvjp(_bdot_fwd, _bdot_bwd)


@jax.custom_vjp
def _bdot_nt(a, b):
    return lax.dot_general(a.astype(BF16), b.astype(BF16), (((1,), (1,)), ((), ())), preferred_element_type=F32)


def _bdot_nt_fwd(a, b):
    return _bdot_nt(a, b), (a, b)


def _bdot_nt_bwd(res, g):
    a, b = res
    gb = g.astype(BF16)
    da = jnp.dot(gb, b.astype(BF16), preferred_element_type=F32)
    db = lax.dot_general(gb, a.astype(BF16), (((0,), (0,)), ((), ())), preferred_element_type=F32)
    return da.astype(a.dtype), db.astype(b.dtype)


_bdot_nt.defvjp(_bdot_nt_fwd, _bdot_nt_bwd)


def _mm(name, a, b, mode, out_dtype=F32, res=None, scale=None):
    if mode == "nn":
        (m, k), n = a.shape, b.shape[1]
    elif mode == "nt":
        (m, k), n = a.shape, b.shape[0]
    else:
        (k, m), n = a.shape, b.shape[1]
    tm, tn = _tile(m, 512), _tile(n, 512)
    a_spec = pl.BlockSpec((k, tm), lambda i, j: (0, i)) if mode == "tn" else pl.BlockSpec((tm, k), lambda i, j: (i, 0))
    b_spec = pl.BlockSpec((tn, k), lambda i, j: (j, 0)) if mode == "nt" else pl.BlockSpec((k, tn), lambda i, j: (0, j))
    dims = {"nn": ((1,), (0,)), "nt": ((1,), (1,)), "tn": ((0,), (0,))}[mode]
    o_spec = pl.BlockSpec((tm, tn), lambda i, j: (i, j))
    ins, in_specs = [a, b], [a_spec, b_spec]
    if res is not None:
        ins.append(res)
        in_specs.append(o_spec)

    def body(*refs):
        acc = lax.dot_general(refs[0][...].astype(BF16), refs[1][...].astype(BF16), (dims, ((), ())),
                              preferred_element_type=F32)
        if scale is not None:
            acc = acc * scale
        if res is not None:
            acc = acc + refs[2][...].astype(F32)
        refs[-1][...] = acc.astype(refs[-1].dtype)

    return pl.pallas_call(
        body, name=name, grid=(m // tm, n // tn), in_specs=in_specs, out_specs=o_spec,
        out_shape=jax.ShapeDtypeStruct((m, n), out_dtype),
        compiler_params=_cparams(dimension_semantics=("parallel", "parallel")),
    )(*ins)


def _mmc_nn(name, a, wb, ki, ci, n, out_dtype=F32):
    m, k = a.shape
    tm = _tile(m, 512)

    def body(a_ref, w_ref, o_ref):
        o_ref[...] = jnp.dot(a_ref[...].astype(BF16), w_ref[...], preferred_element_type=F32).astype(o_ref.dtype)

    return pl.pallas_call(
        body, name=name, grid=(m // tm, N_DEV),
        in_specs=[pl.BlockSpec((tm, k), lambda i, j: (i, 0)), pl.BlockSpec((None, k, n), lambda i, j: (j, ki, ci))],
        out_specs=pl.BlockSpec((tm, n), lambda i, j: (i, j)),
        out_shape=jax.ShapeDtypeStruct((m, N_DEV * n), out_dtype),
        compiler_params=_cparams(dimension_semantics=("parallel", "parallel")),
    )(a, wb)


def _mmc_nt(name, a, wb, ki, ci, n, k, res=None):
    m = a.shape[0]
    tm = _tile(m, 512)
    o_spec = pl.BlockSpec((tm, k), lambda i, j: (i, 0))
    ins = [a, wb] + ([res] if res is not None else [])
    in_specs = [pl.BlockSpec((tm, n), lambda i, j: (i, j)), pl.BlockSpec((None, k, n), lambda i, j: (j, ki, ci))]
    in_specs += [o_spec] if res is not None else []

    def body(*refs):
        a_ref, w_ref, o_ref = refs[0], refs[1], refs[-1]
        acc = lax.dot_general(a_ref[...].astype(BF16), w_ref[...], (((1,), (1,)), ((), ())),
                              preferred_element_type=F32)

        @pl.when(pl.program_id(1) == 0)
        def _():
            o_ref[...] = acc + refs[2][...] if res is not None else acc

        @pl.when(pl.program_id(1) != 0)
        def _():
            o_ref[...] += acc

    return pl.pallas_call(
        body, name=name, grid=(m // tm, N_DEV), in_specs=in_specs, out_specs=o_spec,
        out_shape=jax.ShapeDtypeStruct((m, k), F32),
        compiler_params=_cparams(dimension_semantics=("parallel", "arbitrary")),
    )(*ins)


def _mmc_tn(name, x, dy, n):
    m, k = x.shape
    tk = _tile(k, 512)

    def body(x_ref, dy_ref, o_ref):
        o_ref[...] = lax.dot_general(x_ref[...].astype(BF16), dy_ref[...].astype(BF16), (((0,), (0,)), ((), ())),
                                     preferred_element_type=F32).astype(o_ref.dtype)

    return pl.pallas_call(
        body, name=name, grid=(N_DEV, k // tk),
        in_specs=[pl.BlockSpec((m, tk), lambda j, i: (0, i)), pl.BlockSpec((m, n), lambda j, i: (0, j))],
        out_specs=pl.BlockSpec((None, tk, n), lambda j, i: (j, i, 0)),
        out_shape=jax.ShapeDtypeStruct((N_DEV, k, n), BF16),
        compiler_params=_cparams(dimension_semantics=("parallel", "parallel")),
    )(x, dy)


def _rowwise(name, fn, rows, params, out_rows, out_accs=(), tr=256):
    rows = [a if isinstance(a, tuple) else (a, a.shape[1], 0) for a in rows]
    r = rows[0][0].shape[0]
    in_specs = [pl.BlockSpec((tr, wd), lambda i, cb=cb: (i, cb)) for _, wd, cb in rows]
    rows = [a for a, _, _ in rows]
    in_specs += [pl.BlockSpec(p.shape, lambda i, nd=p.ndim: (0,) * nd) for p in params]
    out_shape = [jax.ShapeDtypeStruct((r, c), dt) for c, dt in out_rows]
    out_shape += [jax.ShapeDtypeStruct(s, F32) for s in out_accs]
    out_specs = [pl.BlockSpec((tr, c), lambda i: (i, 0)) for c, _ in out_rows]
    out_specs += [pl.BlockSpec(s, lambda i, nd=len(s): (0,) * nd) for s in out_accs]
    n_in, n_ro = len(rows) + len(params), len(out_rows)

    def body(*refs):
        res = fn(*[ref[...] for ref in refs[:n_in]])
        outs = refs[n_in:]
        for o, v in zip(outs[:n_ro], res[:n_ro]):
            o[...] = v.astype(o.dtype)
        for o, v in zip(outs[n_ro:], res[n_ro:]):
            _accumulate(o, v)

    return pl.pallas_call(
        body, name=name, grid=(r // tr,), in_specs=in_specs, out_specs=out_specs, out_shape=out_shape,
        compiler_params=_cparams(dimension_semantics=("arbitrary",)),
    )(*rows, *params)


def _pairwise(name, fn, rows, params, n_out, n_acc=0, tr=512):
    t, c = rows[0].shape
    tile = pl.BlockSpec((tr, 2 * HEAD_DIM), lambda p, i: (i, p))
    vec = pl.BlockSpec((1, 2 * HEAD_DIM), lambda p, i: (0, p))
    n_in = len(rows) + len(params)

    def body(*refs):
        res = fn(*[ref[...] for ref in refs[:n_in]])
        outs = refs[n_in:]
        for o, v in zip(outs[:n_out], res[:n_out]):
            o[...] = v
        first = pl.program_id(1) == 0
        for o, v in zip(outs[n_out:], res[n_out:]):
            @pl.when(first)
            def _(o=o, v=v):
                o[...] = v

            @pl.when(jnp.logical_not(first))
            def _(o=o, v=v):
                o[...] += v

    return pl.pallas_call(
        body, name=name, grid=(c // (2 * HEAD_DIM), t // tr),
        in_specs=[tile] * len(rows) + [vec] * len(params), out_specs=[tile] * n_out + [vec] * n_acc,
        out_shape=[jax.ShapeDtypeStruct((t, c), F32)] * n_out + [jax.ShapeDtypeStruct((1, c), F32)] * n_acc,
        compiler_params=_cparams(dimension_semantics=("parallel", "arbitrary")),
    )(*rows, *params)


def _accumulate(o_ref, v):
    @pl.when(pl.program_id(0) == 0)
    def _():
        o_ref[...] = v

    @pl.when(pl.program_id(0) != 0)
    def _():
        o_ref[...] += v


def _rms(x, g):
    return x * lax.rsqrt(jnp.mean(x * x, axis=-1, keepdims=True) + RMS_EPS) * g


def _sigmoid(x):
    return jax.nn.sigmoid(x)


def _softplus(x):
    return jnp.maximum(x, 0.0) + jnp.log1p(jnp.exp(-jnp.abs(x)))


def _norm_fwd(name, x, g):
    return _rowwise(name, lambda xv, gv: (_rms(xv, gv),), [x], [g], [(x.shape[1], BF16)])[0]


def _norm_bwd(name, x, g, dh, dres):
    def fn(xv, dhv, drv, gv):
        _, vjp = jax.vjp(_rms, xv, gv)
        dx, dg = vjp(dhv)
        return dx + drv, dg
    return _rowwise(name, fn, [x, dh, dres], [g], [(x.shape[1], F32)], [g.shape])


def _f_act(gate, up):
    return gate * _sigmoid(gate) * up


def _gate_up_act(name, h, w_gu):
    m, k = h.shape
    tm = _tile(m, 512)

    def body(h_ref, w_ref, gu_ref, a_ref):
        gu = jnp.dot(h_ref[...], w_ref[...], preferred_element_type=F32)
        gu_ref[...] = gu
        a_ref[...] = _f_act(gu[:, :FF_PAD], gu[:, FF_PAD:]).astype(a_ref.dtype)

    return pl.pallas_call(
        body, name=name, grid=(m // tm, N_DEV),
        in_specs=[pl.BlockSpec((tm, k), lambda i, j: (i, 0)),
                  pl.BlockSpec((None, k, 2 * FF_PAD), lambda i, j: (j, 0, 0))],
        out_specs=[pl.BlockSpec((tm, 2 * FF_PAD), lambda i, j: (i, j)), pl.BlockSpec((tm, FF_PAD), lambda i, j: (i, j))],
        out_shape=[jax.ShapeDtypeStruct((m, N_DEV * 2 * FF_PAD), F32), jax.ShapeDtypeStruct((m, FF_HID), BF16)],
        compiler_params=_cparams(dimension_semantics=("parallel", "parallel")),
    )(h, w_gu)


def _gate_up_act_bwd(name, dout, w_down, gu, order):
    m, k = dout.shape
    tm = _tile(m, 512)

    def body(d_ref, w_ref, gu_ref, order_ref, o_ref):
        da = 0.5 * lax.dot_general(d_ref[...].astype(BF16), w_ref[...], (((1,), (1,)), ((), ())),
                                   preferred_element_type=F32)
        guv = gu_ref[...]
        _, vjp = jax.vjp(_f_act, guv[:, :FF_PAD], guv[:, FF_PAD:])
        o_ref[...] = jnp.concatenate(vjp(da), axis=1).astype(o_ref.dtype)

    gu_spec = pl.BlockSpec((tm, 2 * FF_PAD), lambda i, j: (i, j))
    return pl.pallas_call(
        body, name=name, grid=(m // tm, N_DEV),
        in_specs=[pl.BlockSpec((tm, k), lambda i, j: (i, 0)), pl.BlockSpec((FF_PAD, k), lambda i, j: (j, 0)), gu_spec,
                  pl.BlockSpec(memory_space=pl.ANY)],
        out_specs=gu_spec, out_shape=jax.ShapeDtypeStruct((m, N_DEV * 2 * FF_PAD), BF16),
        compiler_params=_cparams(dimension_semantics=("parallel", "parallel")),
    )(dout, w_down, gu, order)


def _ffn_fwd(tag, x, norm, w_gu, w_down):
    h = _norm_fwd(tag + "_norm", x, norm)
    gu, a = _gate_up_act(tag + "_gu", h, w_gu)
    wd = w_down(a)
    out = _mm(tag + "_down", a, wd, "nn", res=x, scale=0.5)
    return out, (h, gu, a), wd


def _ffn_bwd(tag, x, norm, w_gu, w_down, saved, dout, on_down=None):
    h, gu, a = saved
    d_wdown = _mm(tag + "_dwdown", a, dout, "tn", out_dtype=BF16, scale=0.5).reshape(N_DEV, FF_PAD, D_MODEL)
    token = on_down(d_wdown) if on_down is not None else jnp.zeros((8, LANES), F32)
    dgu = _gate_up_act_bwd(tag + "_dgu", dout, w_down, gu, token)
    dh =_mmc_nt(tag + "_dh", dgu, w_gu, 0, 0, 2 * FF_PAD, D_MODEL)
    d_wgu = _mmc_tn(tag + "_dwgu", h, dgu, 2 * FF_PAD)
    dx, dnorm = _norm_bwd(tag + "_dnorm", x, norm, dh, dout)
    return dx, dnorm, d_wgu, d_wdown


def _shift_fwd(z, mu):
    t, c = z.shape[0], RWKV_COLS
    tr = 256

    def body(z_ref, zp_ref, mu_ref, r_ref, k_ref, v_ref, lo_ref, gd_ref):
        zv = z_ref[...]
        prev = zp_ref[7:8, :] * jnp.where(pl.program_id(0) == 0, 0.0, 1.0)
        row = lax.broadcasted_iota(jnp.int32, zv.shape, 0)
        zsh = jnp.where(row == 0, prev, pltpu.roll(zv, 1, 0))
        zs = zv + (zsh - zv) * mu_ref[...]
        r_ref[...] = zs[:, 0:512]
        k_ref[...] = zs[:, 512:1024]
        v_ref[...] = zs[:, 1024:1536]
        lo_ref[...] = zs[:, 1536:1664]
        gd_ref[...] = zs[:, 1664:1792]

    widths = (512, 512, 512, 128, 128)
    return pl.pallas_call(
        body, name="rwkv_shift", grid=(t // tr,),
        in_specs=[pl.BlockSpec((tr, c), lambda i: (i, 0)),
                  pl.BlockSpec((8, c), lambda i: (jnp.maximum(i * (tr // 8) - 1, 0), 0)),
                  pl.BlockSpec((1, c), lambda i: (0, 0))],
        out_specs=[pl.BlockSpec((tr, w), lambda i: (i, 0)) for w in widths],
        out_shape=[jax.ShapeDtypeStruct((t, w), F32) for w in widths],
        compiler_params=_cparams(dimension_semantics=("parallel",)),
    )(z, z, mu)


def _shift_bwd(z, mu, dr, dk, dv, dlo, dgd):
    t, c = z.shape[0], RWKV_COLS
    tr = 256
    nt = t // tr

    def body(z_ref, zp_ref, mu_ref, dr_ref, dk_ref, dv_ref, dlo_ref, dgd_ref,
             drn_ref, dkn_ref, dvn_ref, dlon_ref, dgdn_ref, dz_ref, dmu_ref):
        i = pl.program_id(0)
        zv, muv = z_ref[...], mu_ref[...]
        prev = zp_ref[7:8, :] * jnp.where(i == 0, 0.0, 1.0)
        row = lax.broadcasted_iota(jnp.int32, zv.shape, 0)
        zsh = jnp.where(row == 0, prev, pltpu.roll(zv, 1, 0))
        dzs = jnp.concatenate([dr_ref[...], dk_ref[...], dv_ref[...], dlo_ref[...], dgd_ref[...]], axis=1)
        nxt = jnp.concatenate([drn_ref[0:1, :], dkn_ref[0:1, :], dvn_ref[0:1, :], dlon_ref[0:1, :],
                               dgdn_ref[0:1, :]], axis=1) * jnp.where(i == nt - 1, 0.0, 1.0)
        u = dzs * muv
        un = jnp.where(row == tr - 1, nxt * muv, pltpu.roll(u, tr - 1, 0))
        dz_ref[...] = (dzs - u + un).astype(dz_ref.dtype)
        _accumulate(dmu_ref, jnp.sum(dzs * (zsh - zv), axis=0, keepdims=True))

    widths = (512, 512, 512, 128, 128)
    nxt_map = lambda i: (jnp.minimum((i + 1) * (tr // 8), t // 8 - 1), 0)
    return pl.pallas_call(
        body, name="rwkv_shift_bwd", grid=(nt,),
        in_specs=[pl.BlockSpec((tr, c), lambda i: (i, 0)),
                  pl.BlockSpec((8, c), lambda i: (jnp.maximum(i * (tr // 8) - 1, 0), 0)),
                  pl.BlockSpec((1, c), lambda i: (0, 0))]
        + [pl.BlockSpec((tr, w), lambda i: (i, 0)) for w in widths]
        + [pl.BlockSpec((8, w), nxt_map) for w in widths],
        out_specs=[pl.BlockSpec((tr, c), lambda i: (i, 0)), pl.BlockSpec((1, c), lambda i: (0, 0))],
        out_shape=[jax.ShapeDtypeStruct((t, c), BF16), jax.ShapeDtypeStruct((1, c), F32)],
        compiler_params=_cparams(dimension_semantics=("arbitrary",)),
    )(z, z, mu, dr, dk, dv, dlo, dgd, dr, dk, dv, dlo, dgd)


def _f_pre(k, lo, gd, w0, w2p, a0, a2p, g2, k_k, k_a):
    lane = lax.broadcasted_iota(jnp.int32, lo.shape, 1)
    lo_act = jnp.where(lane < DECAY_LORA, jnp.tanh(lo), lo)
    w = -_softplus(-(w0 + _bdot(lo_act, w2p))) - 0.5
    a = _sigmoid(a0 + _bdot(lo_act, a2p))
    g = _bdot(_sigmoid(gd), g2)
    kk = k * k_k
    k2 = k * (1.0 + (a - 1.0) * k_a)
    decay = jnp.exp(-jnp.exp(w))
    return k2, kk, a, decay, g


def _f_kk(kk, a):
    kkn = kk * lax.rsqrt(jnp.maximum(_head_sums(kk * kk), 1e-24))
    return kkn, kkn * a


def _f_post(y, r, k2, v, g, gn_w, gn_b, r_k):
    mean = _head_sums(y) * (1.0 / HEAD_DIM)
    var = _head_sums(jnp.square(y - mean)) * (1.0 / HEAD_DIM)
    yn = (y - mean) * lax.rsqrt(var + GN_EPS) * gn_w + gn_b
    bonus = _head_sums(r * k2 * r_k) * v
    return (yn + bonus) * g


def _to_v_rows(x):
    t = x.shape[0]
    return x.reshape(t, RWKV_HEADS, V_HI, V_LO).transpose(0, 2, 3, 1).reshape(t, V_HI, LANES)


def _from_v_rows(x):
    t = x.shape[0]
    return x.reshape(t, V_HI, V_LO, RWKV_HEADS).transpose(0, 3, 1, 2).reshape(t, RWKV_DIM)


def _k_cols(x):
    return jnp.tile(x, (V_LO, 1)).T


def _k_rows(x):
    xt = x.T
    out = xt[0:RWKV_HEADS]
    for l in range(1, V_LO):
        out = out + xt[l * RWKV_HEADS:(l + 1) * RWKV_HEADS]
    return out


def _wkv_fwd(r, w, k, kk, b, v):
    t = r.shape[0]
    tc = SCAN_CHUNK
    key_spec = pl.BlockSpec((tc, RWKV_HEADS, HEAD_DIM), lambda i: (i, 0, 0))
    row_spec = pl.BlockSpec((tc, V_HI, LANES), lambda i: (i, 0, 0))

    def body(r_ref, w_ref, k_ref, kk_ref, b_ref, v_ref, y_ref, st_ref, s_scr, cols_a, cols_b):
        @pl.when(pl.program_id(0) == 0)
        def _():
            s_scr[...] = jnp.zeros_like(s_scr)

        def prep(ti, buf):
            for n, ref in enumerate((r_ref, w_ref, k_ref, kk_ref, b_ref)):
                buf[n] = _k_cols(ref[ti])

        def step(ti, s, cur, nxt, ti_next):
            rc, wc, kc, kkc, bc = (cur[n] for n in range(5))
            prep(ti_next, nxt)
            vt = v_ref[ti]
            new, ys = [], []
            for j in range(V_HI):
                sa = -jnp.sum(s[j] * kkc, axis=0, keepdims=True)
                nj = s[j] * wc + bc * sa + kc * vt[j:j + 1]
                st_ref[ti, j] = nj
                ys.append(jnp.sum(nj * rc, axis=0, keepdims=True))
                new.append(nj)
            y_ref[ti] = jnp.concatenate(ys, axis=0)
            return tuple(new)

        def pair(i, s):
            s = step(2 * i, s, cols_a, cols_b, 2 * i + 1)
            return step(2 * i + 1, s, cols_b, cols_a, jnp.minimum(2 * i + 2, tc - 1))

        prep(0, cols_a)
        s = lax.fori_loop(0, tc // 2, pair, tuple(s_scr[j] for j in range(V_HI)))
        for j in range(V_HI):
            s_scr[j] = s[j]

    return pl.pallas_call(
        body, name="wkv_fwd", grid=(t // tc,),
        in_specs=[key_spec] * 5 + [row_spec],
        out_specs=[row_spec, pl.BlockSpec((tc, V_HI, HEAD_DIM, LANES), lambda i: (i, 0, 0, 0))],
        out_shape=[jax.ShapeDtypeStruct((t, V_HI, LANES), F32),
                   jax.ShapeDtypeStruct((t, V_HI, HEAD_DIM, LANES), F32)],
        scratch_shapes=[pltpu.VMEM((V_HI, HEAD_DIM, LANES), F32)] + [pltpu.VMEM((5, HEAD_DIM, LANES), F32)] * 2,
        compiler_params=_cparams(dimension_semantics=("arbitrary",)),
    )(r, w, k, kk, b, v)


def _wkv_bwd(r, w, k, kk, b, v, states, dy):
    t = r.shape[0]
    tc = SCAN_CHUNK
    nb = t // tc
    key_spec = pl.BlockSpec((tc, RWKV_HEADS, HEAD_DIM), lambda i: (nb - 1 - i, 0, 0))
    row_spec = pl.BlockSpec((tc, V_HI, LANES), lambda i: (nb - 1 - i, 0, 0))
    st_spec = pl.BlockSpec((tc, V_HI, HEAD_DIM, LANES), lambda i: (nb - 1 - i, 0, 0, 0))
    stp_spec = pl.BlockSpec((1, V_HI, HEAD_DIM, LANES), lambda i: (jnp.maximum((nb - 1 - i) * tc - 1, 0), 0, 0, 0))

    def body(r_ref, w_ref, k_ref, kk_ref, b_ref, v_ref, st_ref, stp_ref, dy_ref,
             dr_ref, dw_ref, dk_ref, dkk_ref, db_ref, dv_ref, ds_scr, cols_a, cols_b, accs_a, accs_b):
        @pl.when(pl.program_id(0) == 0)
        def _():
            ds_scr[...] = jnp.zeros_like(ds_scr)

        def colsum(x):
            return jnp.sum(x, axis=0, keepdims=True)

        def prep(ti, buf):
            for n, ref in enumerate((r_ref, w_ref, k_ref, kk_ref, b_ref)):
                buf[n] = _k_cols(ref[ti])

        def flush(ti, buf):
            for n, ref in enumerate((dr_ref, dk_ref, db_ref, dw_ref, dkk_ref)):
                ref[ti] = _k_rows(buf[n])

        def step(ti, ds, sp, cur, accs):
            rc, wc, kc, kkc, bc = (cur[n] for n in range(5))
            vt, dyt = v_ref[ti], dy_ref[ti]
            acc = None
            new, dvs = [], []
            for j in range(V_HI):
                st = st_ref[ti, j]
                dsj = ds[j] + rc * dyt[j:j + 1]
                sa = -colsum(sp[j] * kkc)
                dsa = colsum(dsj * bc)
                dvs.append(colsum(dsj * kc))
                parts = (st * dyt[j:j + 1], dsj * vt[j:j + 1], dsj * sa, dsj * sp[j], -(sp[j] * dsa))
                acc = parts if acc is None else tuple(a + q for a, q in zip(acc, parts))
                new.append(dsj * wc - kkc * dsa)
            dv_ref[ti] = jnp.concatenate(dvs, axis=0)
            for n in range(5):
                accs[n] = acc[n]
            return tuple(new)

        def states_before(ti):
            return tuple(st_ref[ti - 1, j] for j in range(V_HI))

        def pair(i, ds):
            ta = tc - 1 - 2 * i
            prep(ta - 1, cols_b)
            flush(jnp.minimum(ta + 1, tc - 1), accs_b)
            ds = step(ta, ds, states_before(ta), cols_a, accs_a)
            prep(ta - 2, cols_a)
            flush(ta, accs_a)
            return step(ta - 1, ds, states_before(ta - 1), cols_b, accs_b)

        prep(tc - 1, cols_a)
        accs_b[...] = jnp.zeros_like(accs_b)
        ds = lax.fori_loop(0, tc // 2 - 1, pair, tuple(ds_scr[j] for j in range(V_HI)))
        prep(0, cols_b)
        flush(2, accs_b)
        ds = step(1, ds, states_before(1), cols_a, accs_a)
        flush(1, accs_a)
        keep = jnp.where(pl.program_id(0) == nb - 1, 0.0, 1.0)
        ds = step(0, ds, tuple(stp_ref[0, j] * keep for j in range(V_HI)), cols_b, accs_b)
        flush(0, accs_b)
        for j in range(V_HI):
            ds_scr[j] = ds[j]

    key_out = jax.ShapeDtypeStruct((t, RWKV_HEADS, HEAD_DIM), F32)
    return pl.pallas_call(
        body, name="wkv_bwd", grid=(nb,),
        in_specs=[key_spec] * 5 + [row_spec, st_spec, stp_spec, row_spec],
        out_specs=[key_spec] * 5 + [row_spec],
        out_shape=[key_out] * 5 + [jax.ShapeDtypeStruct((t, V_HI, LANES), F32)],
        scratch_shapes=[pltpu.VMEM((V_HI, HEAD_DIM, LANES), F32)] + [pltpu.VMEM((5, HEAD_DIM, LANES), F32)] * 4,
        compiler_params=_cparams(dimension_semantics=("arbitrary",)),
    )(r, w, k, kk, b, v, states, states, dy)


PAIR = 2 * HEAD_DIM
N_PAIRS = ATTN_HEADS // 2
Q_COL0 = RWKV_COLS // PAIR
K_COL0 = Q_COL0 + N_PAIRS
V_COL0 = K_COL0 + N_PAIRS


def _swap_halves(x):
    lane = lax.broadcasted_iota(jnp.int32, x.shape, 1)
    return jnp.where((lane & (HEAD_DIM - 1)) < HEAD_DIM // 2, pltpu.roll(x, PAIR - HEAD_DIM // 2, 1),
                     pltpu.roll(x, HEAD_DIM // 2, 1))


@jax.custom_vjp
def _rope(x, cosf, sinf):
    return x * cosf + _swap_halves(x) * sinf


def _rope_fwd(x, cosf, sinf):
    return _rope(x, cosf, sinf), (cosf, sinf)


def _rope_bwd(res, d):
    cosf, sinf = res
    return d * cosf + _swap_halves(d * sinf), jnp.zeros_like(cosf), jnp.zeros_like(sinf)


_rope.defvjp(_rope_fwd, _rope_bwd)


def _head_sums(x):
    lane = lax.broadcasted_iota(jnp.int32, x.shape, 1)
    lo = jnp.where(lane < HEAD_DIM, 1.0, 0.0)
    hi = 1.0 - lo
    return lo * jnp.sum(x * lo, axis=1, keepdims=True) + hi * jnp.sum(x * hi, axis=1, keepdims=True)


def _f_qk(x, cosf, sinf, gain2):
    xn = x * lax.rsqrt(_head_sums(x * x) * (1.0 / HEAD_DIM) + RMS_EPS) * gain2
    return _rope(xn, cosf, sinf)


def _qk_prep(z, tab, q_gain, k_gain):
    t = z.shape[0]
    tr = 1024

    def body(z_ref, c_ref, s_ref, qg_ref, kg_ref, o_ref):
        g = jnp.where(pl.program_id(0) < N_PAIRS, qg_ref[...], kg_ref[...])
        o_ref[...] = _f_qk(z_ref[...], c_ref[...], s_ref[...], jnp.concatenate([g, g], axis=1))

    gain = pl.BlockSpec((1, HEAD_DIM), lambda c, i: (0, 0))
    return pl.pallas_call(
        body, name="qk_prep", grid=(2 * N_PAIRS, t // tr),
        in_specs=[pl.BlockSpec((tr, PAIR), lambda c, i: (i, Q_COL0 + c)), pl.BlockSpec((tr, PAIR), lambda c, i: (i, 0)),
                  pl.BlockSpec((tr, PAIR), lambda c, i: (i, 1)), gain, gain],
        out_specs=pl.BlockSpec((tr, PAIR), lambda c, i: (i, c)),
        out_shape=jax.ShapeDtypeStruct((t, 2 * N_PAIRS * PAIR), F32),
        compiler_params=_cparams(dimension_semantics=("parallel", "parallel")),
    )(z, tab, tab, q_gain, k_gain)


def _qk_prep_bwd(z, tab, q_gain, k_gain, dq, dk):
    t = z.shape[0]
    tr = 1024

    def body(z_ref, c_ref, s_ref, qg_ref, kg_ref, dq_ref, dk_ref, dz_ref, dqg_ref, dkg_ref):
        c, i = pl.program_id(0), pl.program_id(1)
        is_q = c < N_PAIRS
        g = jnp.where(is_q, qg_ref[...], kg_ref[...])
        d = jnp.where(is_q, dq_ref[...], dk_ref[...])
        _, vjp = jax.vjp(lambda xx, gg: _f_qk(xx, c_ref[...], s_ref[...], gg), z_ref[...],
                         jnp.concatenate([g, g], axis=1))
        dx, dg2 = vjp(d)
        dz_ref[...] = dx.astype(dz_ref.dtype)
        dg = dg2[:, :HEAD_DIM] + dg2[:, HEAD_DIM:]
        first_q = jnp.logical_and(c == 0, i == 0)
        first_k = jnp.logical_and(c == N_PAIRS, i == 0)

        @pl.when(first_q)
        def _():
            dqg_ref[...] = dg

        @pl.when(jnp.logical_and(is_q, jnp.logical_not(first_q)))
        def _():
            dqg_ref[...] += dg

        @pl.when(first_k)
        def _():
            dkg_ref[...] = dg

        @pl.when(jnp.logical_and(jnp.logical_not(is_q), jnp.logical_not(first_k)))
        def _():
            dkg_ref[...] += dg

    gain = pl.BlockSpec((1, HEAD_DIM), lambda c, i: (0, 0))
    return pl.pallas_call(
        body, name="qk_prep_bwd", grid=(2 * N_PAIRS, t // tr),
        in_specs=[pl.BlockSpec((tr, PAIR), lambda c, i: (i, Q_COL0 + c)), pl.BlockSpec((tr, PAIR), lambda c, i: (i, 0)),
                  pl.BlockSpec((tr, PAIR), lambda c, i: (i, 1)), gain, gain,
                  pl.BlockSpec((tr, PAIR), lambda c, i: (i, jnp.minimum(c, N_PAIRS - 1))),
                  pl.BlockSpec((tr, PAIR), lambda c, i: (i, jnp.maximum(c - N_PAIRS, 0)))],
        out_specs=[pl.BlockSpec((tr, PAIR), lambda c, i: (i, c)), gain, gain],
        out_shape=[jax.ShapeDtypeStruct((t, 2 * N_PAIRS * PAIR), BF16), jax.ShapeDtypeStruct((1, HEAD_DIM), F32),
                   jax.ShapeDtypeStruct((1, HEAD_DIM), F32)],
        compiler_params=_cparams(dimension_semantics=("arbitrary", "arbitrary")),
    )(z, tab, tab, q_gain, k_gain, dq, dk)


def _attn_block(q, kp, kc, vp, vc, kmin):
    k2 = jnp.concatenate([kp, kc], axis=0)
    v2 = jnp.concatenate([vp, vc], axis=0)
    s = _bdot_nt(q, k2) * (HEAD_DIM ** -0.5)
    qi = lax.broadcasted_iota(jnp.int32, s.shape, 0)
    kj = lax.broadcasted_iota(jnp.int32, s.shape, 1)
    dist = qi + BAND_BLOCK - kj
    valid = (dist >= 0) & (dist <= BAND_BLOCK) & (kj >= kmin)
    s = jnp.where(valid, s, NEG_INF)
    m = lax.stop_gradient(jnp.max(s, axis=-1, keepdims=True))
    e = jnp.exp(s - m)
    l = jnp.sum(e, axis=-1, keepdims=True)
    o = _bdot(e, v2) / l
    return o, m + jnp.log(l)


def _fold(src_ref, dst_ref, dil):
    t = src_ref.shape[0]
    ln = t // dil
    for j in range(dil):
        dst_ref[j * ln:(j + 1) * ln, :] = src_ref[pl.ds(j, ln, stride=dil), :]


def _unfold(src_ref, dst_ref, dil):
    t = src_ref.shape[0]
    ln = t // dil
    for j in range(dil):
        dst_ref[pl.ds(j, ln, stride=dil), :] = src_ref[j * ln:(j + 1) * ln, :]


def _per_group(fn):
    pair = pl.program_id(0)
    for gi, (_, dil) in enumerate(ATTN_GROUPS):
        @pl.when(jnp.logical_or(pair == 2 * gi, pair == 2 * gi + 1))
        def _(dil=dil):
            fn(dil)


def _block_rows(idx, blocks_per_seq):
    first = (idx & (blocks_per_seq - 1)) == 0
    cur = pl.ds(pl.multiple_of(idx * BAND_BLOCK, BAND_BLOCK), BAND_BLOCK)
    prev = pl.ds(pl.multiple_of(jnp.maximum(idx - 1, 0) * BAND_BLOCK, BAND_BLOCK), BAND_BLOCK)
    return first, cur, prev


def _heads(x):
    return x[:, :HEAD_DIM], x[:, HEAD_DIM:]


def _attn_fwd(qk, z):
    t = z.shape[0]
    n_blocks = t // BAND_BLOCK

    def body(q_ref, k_ref, v_ref, o_ref, lse_ref, qf, kf, vf, of, lf):
        def run(dil):
            _fold(q_ref, qf, dil)
            _fold(k_ref, kf, dil)
            _fold(v_ref, vf, dil)
            blocks_per_seq = n_blocks // dil

            def block(idx, carry):
                first, cur, prev = _block_rows(idx, blocks_per_seq)
                kmin = jnp.where(first, BAND_BLOCK, 0)
                outs, lses = [], []
                for q, kp, kc, vp, vc in zip(_heads(qf[cur, :]), _heads(kf[prev, :]), _heads(kf[cur, :]),
                                             _heads(vf[prev, :]), _heads(vf[cur, :])):
                    o, ls = _attn_block(q, kp, kc, vp, vc, kmin)
                    outs.append(o)
                    lses.append(jnp.broadcast_to(ls, o.shape))
                of[cur, :] = jnp.concatenate(outs, axis=1)
                lf[cur, :] = jnp.concatenate(lses, axis=1)
                return carry

            lax.fori_loop(0, n_blocks, block, 0)
            _unfold(of, o_ref, dil)
            _unfold(lf, lse_ref, dil)

        _per_group(run)

    slab = jax.ShapeDtypeStruct((t, N_PAIRS * PAIR), F32)
    out_spec = pl.BlockSpec((t, PAIR), lambda p: (0, p))
    return pl.pallas_call(
        body, name="attn_fwd", grid=(N_PAIRS,),
        in_specs=[pl.BlockSpec((t, PAIR), lambda p: (0, p)), pl.BlockSpec((t, PAIR), lambda p: (0, N_PAIRS + p)),
                  pl.BlockSpec((t, PAIR), lambda p: (0, V_COL0 + p))],
        out_specs=[out_spec, out_spec], out_shape=[slab, slab],
        scratch_shapes=[pltpu.VMEM((t, PAIR), F32)] * 5,
        compiler_params=_cparams(dimension_semantics=("parallel",)),
    )(qk, qk, z)


def _attn_bwd(qk, z, do, dlse):
    t = z.shape[0]
    n_blocks = t // BAND_BLOCK

    def body(q_ref, k_ref, v_ref, do_ref, dl_ref, dq_ref, dk_ref, dv_ref, qf, kf, vf, dof, dlf, dqf, dkf, dvf):
        def run(dil):
            for src, dst in ((q_ref, qf), (k_ref, kf), (v_ref, vf), (do_ref, dof), (dl_ref, dlf)):
                _fold(src, dst, dil)
            blocks_per_seq = n_blocks // dil

            def block(idx, carry):
                first, cur, prev = _block_rows(idx, blocks_per_seq)
                kmin = jnp.where(first, BAND_BLOCK, 0)
                grads = []
                for q, kp, kc, vp, vc, do_h, dl_h in zip(
                        _heads(qf[cur, :]), _heads(kf[prev, :]), _heads(kf[cur, :]), _heads(vf[prev, :]),
                        _heads(vf[cur, :]), _heads(dof[cur, :]), _heads(dlf[cur, :])):
                    _, vjp = jax.vjp(functools.partial(_attn_block, kmin=kmin), q, kp, kc, vp, vc)
                    grads.append(vjp((do_h, jnp.sum(dl_h, axis=1, keepdims=True))))
                dq, dkp, dkc, dvp, dvc = (jnp.concatenate([a, b], axis=1) for a, b in zip(*grads))
                dqf[cur, :] = dq
                dkf[cur, :] = dkc
                dvf[cur, :] = dvc

                @pl.when(jnp.logical_not(first))
                def _():
                    dkf[prev, :] += dkp
                    dvf[prev, :] += dvp

                return carry

            lax.fori_loop(0, n_blocks, block, 0)
            _unfold(dqf, dq_ref, dil)
            _unfold(dkf, dk_ref, dil)
            _unfold(dvf, dv_ref, dil)

        _per_group(run)

    slab = jax.ShapeDtypeStruct((t, N_PAIRS * PAIR), F32)
    own = pl.BlockSpec((t, PAIR), lambda p: (0, p))
    return pl.pallas_call(
        body, name="attn_bwd", grid=(N_PAIRS,),
        in_specs=[own, pl.BlockSpec((t, PAIR), lambda p: (0, N_PAIRS + p)),
                  pl.BlockSpec((t, PAIR), lambda p: (0, V_COL0 + p)), own, own],
        out_specs=[own] * 3, out_shape=[slab] * 3,
        scratch_shapes=[pltpu.VMEM((t, PAIR), F32)] * 8,
        compiler_params=_cparams(dimension_semantics=("parallel",)),
    )(qk, qk, z, do, dlse)


def _f_comb(o1, o2, o3, l1, l2, l3):
    m = jnp.maximum(jnp.maximum(l1, l2), l3)
    e1, e2, e3 = jnp.exp(l1 - m), jnp.exp(l2 - m), jnp.exp(l3 - m)
    den = e1 + e2 + e3
    return (e1 / den) * o1 + (e2 / den) * o2 + (e3 / den) * o3


def _all_gather_hbm(name, arrs):
    na = len(arrs)

    def body(*refs):
        x_refs, out_refs = refs[:na], refs[na:2 * na]
        send_sems, recv_sems, local_sems = refs[2 * na:]
        mx, my, mc = lax.axis_index("x"), lax.axis_index("y"), lax.axis_index("c")
        me, sibling = (mx, my, mc), (mx, my, 1 - mc)
        chips = [(1 - mx, my), (mx, 1 - my), (1 - mx, 1 - my)]

        def slot(a, px, py, pc):
            return out_refs[a].at[4 * px + 2 * py + pc]

        def copy(a, k, block, to, src=None):
            return pltpu.make_async_remote_copy(
                src_ref=slot(a, *block) if src is None else src, dst_ref=slot(a, *block),
                send_sem=send_sems.at[a, k], recv_sem=recv_sems.at[a, k], device_id=to, device_id_type=MESH)

        mine = [pltpu.make_async_copy(x_refs[a], slot(a, *me), local_sems.at[a]) for a in range(na)]
        for cp in mine:
            cp.start()
        first = []
        for a in range(na):
            first.append(copy(a, 0, me, sibling, src=x_refs[a]))
            first += [copy(a, 1 + j, me, (*chip, mc), src=x_refs[a]) for j, chip in enumerate(chips)]
        for cp in first:
            cp.start()
        passed = []
        for j, chip in enumerate(chips):
            for a in range(na):
                copy(a, 1 + j, (*chip, mc), me).wait_recv()
                passed.append(copy(a, 4 + j, (*chip, mc), sibling))
                passed[-1].start()
        for a in range(na):
            copy(a, 0, sibling, me).wait_recv()
            for j, chip in enumerate(chips):
                copy(a, 4 + j, (*chip, 1 - mc), me).wait_recv()
        for cp in first + passed:
            cp.wait_send()
        for cp in mine:
            cp.wait()

    hbm = pl.BlockSpec(memory_space=pl.ANY)
    return pl.pallas_call(
        body, name=name,
        out_shape=[jax.ShapeDtypeStruct((N_DEV,) + a.shape, a.dtype) for a in arrs],
        in_specs=[hbm] * na, out_specs=[hbm] * na,
        scratch_shapes=[pltpu.SemaphoreType.DMA((na, 7)), pltpu.SemaphoreType.DMA((na, 7)),
                        pltpu.SemaphoreType.DMA((na,))],
    )(*arrs)


def _all_gather_vmem(x):
    rws, cols = x.shape

    def body(x_ref, out_ref, send_sems, recv_sems):
        mx, my, mc = lax.axis_index("x"), lax.axis_index("y"), lax.axis_index("c")
        me, sibling = (mx, my, mc), (mx, my, 1 - mc)
        chips = [(1 - mx, my), (mx, 1 - my), (1 - mx, 1 - my)]

        def slot(px, py, pc):
            return out_ref.at[4 * px + 2 * py + pc]

        def copy(k, block, to, src=None):
            return pltpu.make_async_remote_copy(
                src_ref=slot(*block) if src is None else src, dst_ref=slot(*block),
                send_sem=send_sems.at[k], recv_sem=recv_sems.at[k], device_id=to, device_id_type=MESH)

        first = [copy(0, me, sibling, src=x_ref)]
        first += [copy(1 + j, me, (*chip, mc), src=x_ref) for j, chip in enumerate(chips)]
        for cp in first:
            cp.start()
        out_ref[4 * mx + 2 * my + mc] = x_ref[...]
        passed = [copy(4 + j, (*chip, mc), sibling) for j, chip in enumerate(chips)]
        for j, chip in enumerate(chips):
            copy(1 + j, (*chip, mc), me).wait_recv()
            passed[j].start()
        copy(0, sibling, me).wait_recv()
        for j, chip in enumerate(chips):
            copy(4 + j, (*chip, 1 - mc), me).wait_recv()
        for cp in first + passed:
            cp.wait_send()

    return pl.pallas_call(
        body, name="all_gather_small",
        out_shape=jax.ShapeDtypeStruct((N_DEV, rws, cols), x.dtype),
        in_specs=[pl.BlockSpec(memory_space=pltpu.VMEM)], out_specs=pl.BlockSpec(memory_space=pltpu.VMEM),
        scratch_shapes=[pltpu.SemaphoreType.DMA((7,)), pltpu.SemaphoreType.DMA((7,))],
    )(x)


def _all_to_all_hbm(name, arrs):
    na = len(arrs)

    def body(*refs):
        g_refs, out_refs = refs[:na], refs[na:2 * na]
        send_sems, recv_sems, local_sems = refs[2 * na:]
        mx, my, mc = lax.axis_index("x"), lax.axis_index("y"), lax.axis_index("c")
        me = 4 * mx + 2 * my + mc
        mine = [pltpu.make_async_copy(g_refs[a].at[me], out_refs[a].at[me], local_sems.at[a]) for a in range(na)]
        for cp in mine:
            cp.start()
        copies = []
        for k in range(1, N_DEV):
            px, py, pc = mx ^ (k >> 2), my ^ ((k >> 1) & 1), mc ^ (k & 1)
            peer = 4 * px + 2 * py + pc
            for a in range(na):
                copies.append(pltpu.make_async_remote_copy(
                    src_ref=g_refs[a].at[peer], dst_ref=out_refs[a].at[me], send_sem=send_sems.at[a, k - 1],
                    recv_sem=recv_sems.at[a, k - 1], device_id=(px, py, pc), device_id_type=MESH))
        for cp in copies:
            cp.start()
        for cp in copies:
            cp.wait_recv()
        for cp in copies:
            cp.wait_send()
        for cp in mine:
            cp.wait()

    hbm = pl.BlockSpec(memory_space=pl.ANY)
    return pl.pallas_call(
        body, name=name,
        out_shape=[jax.ShapeDtypeStruct(a.shape, a.dtype) for a in arrs],
        in_specs=[hbm] * na, out_specs=[hbm] * na,
        scratch_shapes=[pltpu.SemaphoreType.DMA((na, 7)), pltpu.SemaphoreType.DMA((na, 7)),
                        pltpu.SemaphoreType.DMA((na,))],
    )(*arrs)


def _scatter_copies(g_refs, land_refs, send_sems, recv_sems):
    mx, my, mc = lax.axis_index("x"), lax.axis_index("y"), lax.axis_index("c")
    me = 4 * mx + 2 * my + mc
    copies = []
    for k in range(1, N_DEV):
        px, py, pc = mx ^ (k >> 2), my ^ ((k >> 1) & 1), mc ^ (k & 1)
        peer = 4 * px + 2 * py + pc
        for a, (g_ref, land_ref) in enumerate(zip(g_refs, land_refs)):
            copies.append(pltpu.make_async_remote_copy(
                src_ref=g_ref.at[peer], dst_ref=land_ref.at[me], send_sem=send_sems.at[a * (N_DEV - 1) + k - 1],
                recv_sem=recv_sems.at[a * (N_DEV - 1) + k - 1], device_id=(px, py, pc), device_id_type=MESH))
    return copies


_HBM = pl.BlockSpec(memory_space=pltpu.HBM)
_SEM = pl.BlockSpec(memory_space=pltpu.SEMAPHORE)
_DATAFLOW = pltpu.SideEffectType.DATAFLOW_SIDE_EFFECTING


def _gather_copies(x_refs, land_refs, send_sems, recv_sems):
    mx, my, mc = lax.axis_index("x"), lax.axis_index("y"), lax.axis_index("c")
    me = 4 * mx + 2 * my + mc
    copies = []
    for k in range(1, N_DEV):
        px, py, pc = mx ^ (k >> 2), my ^ ((k >> 1) & 1), mc ^ (k & 1)
        for a, (x_ref, land_ref) in enumerate(zip(x_refs, land_refs)):
            copies.append(pltpu.make_async_remote_copy(
                src_ref=x_ref, dst_ref=land_ref.at[me], send_sem=send_sems.at[a * (N_DEV - 1) + k - 1],
                recv_sem=recv_sems.at[a * (N_DEV - 1) + k - 1], device_id=(px, py, pc), device_id_type=MESH))
    return copies


def _exchange_start(name, copies, srcs, lands, after):
    na = len(srcs)

    def body(*refs):
        for cp in copies(refs[:na], refs[na:2 * na], refs[2 * na + 1], refs[2 * na + 2]):
            cp.start()
        refs[-1][...] = jnp.zeros_like(refs[-1])

    in_hbm = lambda a: pltpu.with_memory_space_constraint(a, pltpu.HBM)
    outs = pl.pallas_call(
        body, name=name,
        out_shape=(pltpu.SemaphoreType.DMA((na * (N_DEV - 1),)), pltpu.SemaphoreType.DMA((na * (N_DEV - 1),)),
                   *[pltpu.HBM(a.shape, a.dtype) for a in list(srcs) + list(lands)],
                   jax.ShapeDtypeStruct((8, LANES), F32)),
        in_specs=[_HBM] * (2 * na) + [pl.BlockSpec(memory_space=pl.ANY)],
        out_specs=(_SEM, _SEM, *[_HBM] * (2 * na), pl.BlockSpec(memory_space=pltpu.VMEM)),
        input_output_aliases={i: 2 + i for i in range(2 * na)},
        compiler_params=pltpu.CompilerParams(has_side_effects=_DATAFLOW),
    )(*[in_hbm(a) for a in srcs], *[in_hbm(a) for a in lands], after)
    return outs[0], outs[1], outs[2:2 + na], outs[2 + na:2 + 2 * na], outs[-1]


def _exchange_wait(name, copies, send_sems, recv_sems, srcs, lands, after):
    na = len(srcs)

    def body(*refs):
        for cp in copies(refs[:na], refs[na:2 * na], refs[2 * na], refs[2 * na + 1]):
            cp.wait_send()
            cp.wait_recv()

    outs = pl.pallas_call(
        body, name=name,
        out_shape=[pltpu.HBM(a.shape, a.dtype) for a in list(srcs) + list(lands)],
        in_specs=[_HBM] * (2 * na) + [_SEM, _SEM, pl.BlockSpec(memory_space=pl.ANY)], out_specs=[_HBM] * (2 * na),
        input_output_aliases={i: i for i in range(2 * na)},
        compiler_params=pltpu.CompilerParams(has_side_effects=_DATAFLOW),
    )(*srcs, *lands, send_sems, recv_sems, after)
    return outs[:na], outs[na:]


def _sum_slots(name, g, tr):
    _, rws, cols = g.shape

    def body(g_ref, o_ref):
        acc = g_ref[0].astype(F32)
        for j in range(1, N_DEV):
            acc = acc + g_ref[j].astype(F32)
        o_ref[...] = acc

    return pl.pallas_call(
        body, name=name, grid=(rws // tr,),
        in_specs=[pl.BlockSpec((N_DEV, tr, cols), lambda i: (0, i, 0))],
        out_specs=pl.BlockSpec((tr, cols), lambda i: (i, 0)),
        out_shape=jax.ShapeDtypeStruct((rws, cols), F32),
        compiler_params=_cparams(dimension_semantics=("parallel",)),
    )(g)


def _adam_math(wv, gv, mv, vv):
    mn = ADAM_B1 * mv + (1.0 - ADAM_B1) * gv
    vn = ADAM_B2 * vv + (1.0 - ADAM_B2) * jnp.square(gv)
    m_hat = mn / (1.0 - ADAM_B1 ** ADAM_STEP)
    v_hat = vn / (1.0 - ADAM_B2 ** ADAM_STEP)
    delta = -ADAM_LR * (m_hat / (jnp.sqrt(v_hat) + ADAM_EPS) + ADAM_WD * wv)
    return delta, mn, vn


def _adamw(name, w, g, m, v, tr):
    return _rowwise(name, _adam_math, [w, g, m, v], [], [(LANES, F32)] * 3, tr=tr)


def _adamw_slots(name, recv, own, rb, cb, cw, w, m, v, tr, order):
    nr, nc = w.shape

    def body(g_ref, own_ref, w_ref, m_ref, v_ref, order_ref, go_ref, d_ref, mo_ref, vo_ref):
        me = 4 * lax.axis_index("x") + 2 * lax.axis_index("y") + lax.axis_index("c")
        acc = None
        for s in range(N_DEV):
            part = jnp.where(me == s, own_ref[s], g_ref[s]).astype(F32)
            acc = part if acc is None else acc + part
        g = acc[:, :nc]
        go_ref[...] = g
        d_ref[...], mo_ref[...], vo_ref[...] = _adam_math(w_ref[...], g, m_ref[...], v_ref[...])

    nat = pl.BlockSpec((tr, nc), lambda i: (i, 0))
    slots = pl.BlockSpec((N_DEV, tr, cw), lambda i: (0, rb + i, cb))
    return pl.pallas_call(
        body, name=name, grid=(nr // tr,),
        in_specs=[slots, slots, nat, nat, nat, pl.BlockSpec(memory_space=pl.ANY)],
        out_specs=[nat] * 4, out_shape=[jax.ShapeDtypeStruct((nr, nc), F32)] * 4,
        compiler_params=_cparams(dimension_semantics=("parallel",)),
    )(recv, own, w, m, v, order)


def _local_blocks(w):
    pad_cols = lambda a: jnp.pad(a, ((0, 0), (0, FF_PAD - FF_SHARD)))
    pad_rows = lambda a: jnp.pad(a, ((0, FF_PAD - FF_SHARD), (0, 0)))
    gate_up = lambda tag: jnp.concatenate([pad_cols(w[tag + "_w_gate"]), pad_cols(w[tag + "_w_up"])], axis=1)
    blocks = {
        "ffn1_gu": gate_up("ffn1"), "ffn1_d": pad_rows(w["ffn1_w_down"]), "w_in": w["w_in"],
        "lora": jnp.concatenate([w["rwkv_w2"], w["rwkv_a2"], w["rwkv_g2"]], axis=0),
        "br": jnp.concatenate([w["w_br_rwkv"], w["w_br_attn"], w["ple_w_proj"]], axis=0),
        "w_out": w["w_out"], "ffn2_gu": gate_up("ffn2"), "ffn2_d": pad_rows(w["ffn2_w_down"]),
        "ple_gate": w["ple_w_gate"],
    }
    return {n: a.astype(BF16) for n, a in blocks.items()}


GATHER_GROUPS = {"head": ("ffn1_gu", "ffn1_d"), "mid": ("w_in", "lora"),
                 "rest": ("br", "w_out", "ffn2_gu", "ffn2_d", "ple_gate")}

SCATTER_GROUPS = {"tail": ("ple_gate", "ple_proj", "ffn2_gu", "ffn2_d"), "mixer": ("w_out", "br", "lora", "w_in"),
                  "ffn1_down": ("ffn1_d",), "head": ("ffn1_gu",)}

ADAM_PLAN = (
    ("ffn1_w_gate", "ffn1_gu", 0, 0, FF_PAD, 256), ("ffn1_w_up", "ffn1_gu", 0, 1, FF_PAD, 256),
    ("ffn1_w_down", "ffn1_d", 0, 0, D_MODEL, FF_SHARD // 2), ("w_in", "w_in", 0, 0, IN_SHARD, 256),
    ("rwkv_w2", "lora", 0, 0, HEAD_DIM, 64), ("rwkv_a2", "lora", 1, 0, HEAD_DIM, 64),
    ("rwkv_g2", "lora", 2, 0, HEAD_DIM, 64),
    ("w_br_rwkv", "br", 0, 0, OUT_SHARD, 256), ("w_br_attn", "br", 2, 0, OUT_SHARD, 256),
    ("ple_w_proj", "ple_proj", 0, 0, OUT_SHARD, 256), ("w_out", "w_out", 0, 0, D_MODEL, OUT_SHARD),
    ("ffn2_w_gate", "ffn2_gu", 0, 0, FF_PAD, 256), ("ffn2_w_up", "ffn2_gu", 0, 1, FF_PAD, 256),
    ("ffn2_w_down", "ffn2_d", 0, 0, D_MODEL, FF_SHARD // 2), ("ple_w_gate", "ple_gate", 0, 0, D_MODEL, OUT_SHARD),
)


def _pack_small(arrs, rows):
    flat = jnp.concatenate([a.reshape(-1) for a in arrs])
    return jnp.pad(flat, (0, rows * LANES - flat.shape[0])).reshape(rows, LANES)


def _unpack_small(flat, like):
    flat = flat.reshape(-1)
    out, off = [], 0
    for a in like:
        out.append(flat[off:off + a.size].reshape(a.shape))
        off += a.size
    return out


def _local_step(x, p, pos, target, sm, wg, fetch, on_grads):
    t = x.shape[0]

    w_d1 = wg["ffn1_d"].reshape(FF_HID, D_MODEL)
    x1, ffn1_saved, _ = _ffn_fwd("ffn1", x, sm["ffn1_norm"], wg["ffn1_gu"], lambda after: w_d1)
    wg = {**wg, **fetch("mid", x1)}
    full_cols = lambda blk: blk.transpose(1, 0, 2).reshape(blk.shape[1], N_DEV * blk.shape[2])
    lora_w2 = full_cols(wg["lora"][:, :DECAY_LORA])
    lora_a2 = full_cols(wg["lora"][:, DECAY_LORA:DECAY_LORA + ICLR_LORA])
    lora_g2 = full_cols(wg["lora"][:, DECAY_LORA + ICLR_LORA:])
    h2 = _norm_fwd("mix_norm", x1, sm["mix_norm"])
    z = _mmc_nn("w_in", h2, wg["w_in"], 0, 0, IN_SHARD)
    z_g = (z, 2 * D_MODEL, (RWKV_COLS + 3 * ATTN_DIM) // (2 * D_MODEL))

    r, k, v, lo, gd = _shift_fwd(z, sm["rwkv_mu"])
    zero_lo = jnp.zeros((DECAY_LORA, RWKV_DIM), BF16)
    w2p = jnp.concatenate([lora_w2, zero_lo], axis=0).astype(F32)
    a2p = jnp.concatenate([zero_lo, lora_a2], axis=0).astype(F32)
    pre_params = [sm["rwkv_w0"], w2p, sm["rwkv_a0"], a2p, lora_g2.astype(F32), sm["rwkv_k_k"], sm["rwkv_k_a"]]
    wide = [(RWKV_DIM, F32)]
    k2, kk, a, decay, g = _rowwise("rwkv_pre", _f_pre, [k, lo, gd], pre_params, wide * 5)
    kkn, b = _pairwise("rwkv_kk", _f_kk, [kk, a], [], 2)
    scan_in = [u.reshape(t, RWKV_HEADS, HEAD_DIM) for u in (r, decay, k2, kkn, b)]
    v_rows = _to_v_rows(v)
    y_rows, states = _wkv_fwd(*scan_in, v_rows)
    y = _from_v_rows(y_rows)
    post_params = [sm["rwkv_gn_w"], sm["rwkv_gn_b"], sm["rwkv_r_k"]]
    post_rows = [y, r, k2, v, g]
    y_rwkv = _pairwise("rwkv_post", lambda *av: (_f_post(*av),), post_rows, post_params, 1)[0]

    inv_freq = 1.0 / (ROPE_THETA ** (jnp.arange(0, HEAD_DIM, 2, dtype=F32) / HEAD_DIM))
    freq2 = jnp.tile(inv_freq, 2 * PAIR // HEAD_DIM).reshape(1, PAIR)
    half = jnp.ones((HEAD_DIM // 2,), F32)
    sign2 = jnp.tile(jnp.concatenate([-half, half]), PAIR // HEAD_DIM).reshape(1, PAIR)

    def rope_table(posv, fr, sg):
        ang = posv * fr
        return (jnp.concatenate([jnp.cos(ang), jnp.sin(ang) * sg], axis=1),)
    tab = _rowwise("rope_table", rope_table, [pos.astype(F32).reshape(t, 1)], [freq2, sign2], [(2 * PAIR, F32)])[0]
    qk = _qk_prep(z, tab, sm["q_norm"], sm["k_norm"])
    o_all, lse_all = _attn_fwd(qk, z)
    gw = HEADS_PER_GROUP * HEAD_DIM

    def by_group(ov, lv):
        return [ov[:, i * gw:(i + 1) * gw] for i in range(3)] + [lv[:, i * gw:(i + 1) * gw] for i in range(3)]
    y_attn = _rowwise("attn_comb", lambda ov, lv: (_f_comb(*by_group(ov, lv)),), [o_all, lse_all], [], [(gw, F32)])[0]

    wg = {**wg, **fetch("rest", y_rwkv)}
    w_d2 = wg["ffn2_d"].reshape(FF_HID, D_MODEL)
    w_out = wg["w_out"].reshape(D_MODEL, D_MODEL)
    w_pg = wg["ple_gate"].reshape(D_MODEL, D_MODEL)
    w_brr = full_cols(wg["br"][:, :RWKV_DIM])
    w_bra = full_cols(wg["br"][:, RWKV_DIM:RWKV_DIM + gw])
    w_pp = full_cols(wg["br"][:, RWKV_DIM + gw:])
    u_r = _mm("br_rwkv", y_rwkv, w_brr, "nn")
    u_a = _mm("br_attn", y_attn, w_bra, "nn")

    def f_merge(zgr, zga, ur, ua):
        return _sigmoid(zgr) * ur + _sigmoid(zga) * ua
    merged = _rowwise("merge", lambda zg, ur, ua: (f_merge(zg[:, :D_MODEL], zg[:, D_MODEL:], ur, ua),),
                      [z_g, u_r, u_a], [], [(D_MODEL, BF16)])[0]
    x2 = _mm("w_out", merged, w_out, "nn", res=x1)
    x3, ffn2_saved, _ = _ffn_fwd("ffn2", x2, sm["ffn2_norm"], wg["ffn2_gu"], lambda after: w_d2)

    hn = _norm_fwd("ple_norm", x3, sm["ple_norm"])
    gz = _mm("ple_gate", hn, w_pg, "nn")
    pp = _mm("ple_proj", p, w_pp, "nn")

    def f_head(x3v, gzv, ppv, tg):
        sg = _sigmoid(gzv)
        err = x3v + sg * ppv - tg
        part = 0.5 * jnp.sum(jnp.mean(err * err, axis=-1, keepdims=True))
        dx4 = err * (1.0 / D_MODEL)
        return dx4, dx4 * ppv * sg * (1.0 - sg), dx4 * sg, jnp.full((1, LANES), part, F32)
    dx4, dgz, dpp, loss_row = _rowwise("ple_loss", f_head, [x3, gz, pp, target], [],
                                       [(D_MODEL, F32), (D_MODEL, BF16), (D_MODEL, BF16)], [(1, LANES)])
    loss = loss_row[0, 0]

    gs, gm = {}, {}
    row_blocks = lambda g: g.reshape(N_DEV, g.shape[0] // N_DEV, g.shape[1])
    dhn = _mm("ple_dhn", dgz, w_pg, "nt")
    gm["ple_gate"] = row_blocks(_mm("ple_dwgate", hn, dgz, "tn", out_dtype=BF16))
    col_blocks = lambda g: g.reshape(g.shape[0], N_DEV, g.shape[1] // N_DEV).transpose(1, 0, 2)
    gm["ple_proj"] = col_blocks(_mm("ple_dwproj", p, dpp, "tn", out_dtype=BF16))
    dx3, gs["ple_norm"] = _norm_bwd("ple_dnorm", x3, sm["ple_norm"], dhn, dx4)

    dx2, gs["ffn2_norm"], gm["ffn2_gu"], gm["ffn2_d"] = _ffn_bwd(
        "ffn2", x2, sm["ffn2_norm"], wg["ffn2_gu"], w_d2, ffn2_saved, dx3)
    tail_token = on_grads("tail", {n: gm.pop(n) for n in SCATTER_GROUPS["tail"]})

    dmerged = _mm("w_out_dmerged", dx2, w_out, "nt")
    gm["w_out"] = row_blocks(_mm("w_out_dw", merged, dx2, "tn", out_dtype=BF16))

    def merge_bwd(zg, ur, ua, dm):
        _, vjp = jax.vjp(f_merge, zg[:, :D_MODEL], zg[:, D_MODEL:], ur, ua)
        dzr, dza, dur, dua = vjp(dm)
        return jnp.concatenate([dzr, dza], axis=1), dur, dua
    dz_g, du_r, du_a = _rowwise("merge_bwd", merge_bwd, [z_g, u_r, u_a, dmerged], [],
                                [(2 * D_MODEL, BF16), (D_MODEL, BF16), (D_MODEL, BF16)])
    dy_rwkv = _mm("br_rwkv_dy", du_r, w_brr, "nt")
    dy_attn = _mm("br_attn_dy", du_a, w_bra, "nt")
    gm["br"] = jnp.concatenate([col_blocks(_mm("br_rwkv_dw", y_rwkv, du_r, "tn", out_dtype=BF16)),
                                col_blocks(_mm("br_attn_dw", y_attn, du_a, "tn", out_dtype=BF16))], axis=1)

    def comb_bwd(ov, lv, dyv):
        _, vjp = jax.vjp(_f_comb, *by_group(ov, lv))
        d = vjp(dyv)
        return jnp.concatenate(d[:3], axis=1), jnp.concatenate(d[3:], axis=1)
    do_all, dl_all = _rowwise("attn_comb_bwd", comb_bwd, [o_all, lse_all, dy_attn], [],
                              [(ATTN_DIM, F32), (ATTN_DIM, F32)])
    dq_all, dk_all, dv_all = _attn_bwd(qk, z, do_all, dl_all)
    dqk_raw, gs["q_norm"], gs["k_norm"] = _qk_prep_bwd(z, tab, sm["q_norm"], sm["k_norm"], dq_all, dk_all)

    def post_bwd(yv, rv, k2v, vv, gv, dv_, gnw, gnb, rk):
        _, vjp = jax.vjp(_f_post, yv, rv, k2v, vv, gv, gnw, gnb, rk)
        return vjp(dv_)
    dy, dr1, dk2a, dv1, dg, d_gnw, d_gnb, d_rk = _pairwise(
        "rwkv_post_bwd", post_bwd, post_rows + [dy_rwkv], [post_params[0] + tail_token[0, 0]] + post_params[1:], 5, 3)
    gs["rwkv_gn_w"], gs["rwkv_gn_b"], gs["rwkv_r_k"] = d_gnw, d_gnb, d_rk
    dr2, ddecay, dk2b, dkkn, db, dv_rows = _wkv_bwd(*scan_in, v_rows, states, _to_v_rows(dy))
    dr2, ddecay, dk2b, dkkn, db = [u.reshape(t, RWKV_DIM) for u in (dr2, ddecay, dk2b, dkkn, db)]
    dv2 = _from_v_rows(dv_rows)

    def kk_bwd(kkv, av, dkknv, dbv, dra, drb, dva, dvb):
        _, vjp = jax.vjp(_f_kk, kkv, av)
        return (*vjp((dkknv, dbv)), dra + drb, dva + dvb)
    dkk, da, dr, dv = _pairwise("rwkv_kk_bwd", kk_bwd, [kk, a, dkkn, db, dr1, dr2, dv1, dv2], [], 4)

    def pre_bwd(kv, lov, gdv, dk2x, dk2y, dkkv, dav, ddec, dgv, w0, w2p_, a0, a2p_, g2, k_k, k_a):
        _, vjp = jax.vjp(_f_pre, kv, lov, gdv, w0, w2p_, a0, a2p_, g2, k_k, k_a)
        return vjp((dk2x + dk2y, dkkv, dav, ddec, dgv))
    lora_acc = (DECAY_LORA + ICLR_LORA, RWKV_DIM)
    dk, dlo, dgd, d_w0, d_w2p, d_a0, d_a2p, d_g2, d_kk, d_ka = _rowwise(
        "rwkv_pre_bwd", pre_bwd,
        [k, lo, gd, dk2a, dk2b, dkk, da, ddecay, dg],
        pre_params, [(RWKV_DIM, F32), (LANES, F32), (LANES, F32)],
        [(1, RWKV_DIM), lora_acc, (1, RWKV_DIM), lora_acc, (GATE_LORA, RWKV_DIM), (1, RWKV_DIM), (1, RWKV_DIM)])
    gs["rwkv_w0"], gs["rwkv_a0"], gs["rwkv_k_k"], gs["rwkv_k_a"] = d_w0, d_a0, d_kk, d_ka
    gm["lora"] = jnp.concatenate([col_blocks(d_w2p[:DECAY_LORA]), col_blocks(d_a2p[DECAY_LORA:]), col_blocks(d_g2)],
                                 axis=1).astype(BF16)
    dz_r, gs["rwkv_mu"] = _shift_bwd(z, sm["rwkv_mu"], dr, dk, dv, dlo, dgd)

    dz = jnp.concatenate([dz_r, dqk_raw, dv_all.astype(BF16), dz_g], axis=1)
    dh2 = _mmc_nt("w_in_dh", dz, wg["w_in"], 0, 0, IN_SHARD, D_MODEL)
    gm["w_in"] = _mmc_tn("w_in_dw", h2, dz, IN_SHARD)
    mixer_token = on_grads("mixer", {n: gm.pop(n) for n in SCATTER_GROUPS["mixer"]})
    dx1, gs["mix_norm"] = _norm_bwd("mix_dnorm", x1, sm["mix_norm"] + mixer_token[0, 0], dh2, dx2)

    dx0, gs["ffn1_norm"], gm["ffn1_gu"], _ = _ffn_bwd(
        "ffn1", x, sm["ffn1_norm"], wg["ffn1_gu"], w_d1, ffn1_saved, dx1,
        on_down=lambda blocks: on_grads("ffn1_down", {"ffn1_d": blocks}))
    return loss, dx0, gm, gs


def kernel(x, p, positions, ffn1_norm, ffn1_w_gate, ffn1_w_up, ffn1_w_down, mix_norm, w_in, rwkv_mu, rwkv_w0, rwkv_w2, rwkv_a0, rwkv_a2, rwkv_g2, rwkv_k_k, rwkv_k_a, rwkv_r_k, rwkv_gn_w, rwkv_gn_b, q_norm, k_norm, w_br_rwkv, w_br_attn, w_out, ffn2_norm, ffn2_w_gate, ffn2_w_up, ffn2_w_down, ple_norm, ple_w_gate, ple_w_proj, loss_target, m_ffn1_norm, m_ffn1_w_gate, m_ffn1_w_up, m_ffn1_w_down, m_mix_norm, m_w_in, m_rwkv_mu, m_rwkv_w0, m_rwkv_w2, m_rwkv_a0, m_rwkv_a2, m_rwkv_g2, m_rwkv_k_k, m_rwkv_k_a, m_rwkv_r_k, m_rwkv_gn_w, m_rwkv_gn_b, m_q_norm, m_k_norm, m_w_br_rwkv, m_w_br_attn, m_w_out, m_ffn2_norm, m_ffn2_w_gate, m_ffn2_w_up, m_ffn2_w_down, m_ple_norm, m_ple_w_gate, m_ple_w_proj, v_ffn1_norm, v_ffn1_w_gate, v_ffn1_w_up, v_ffn1_w_down, v_mix_norm, v_w_in, v_rwkv_mu, v_rwkv_w0, v_rwkv_w2, v_rwkv_a0, v_rwkv_a2, v_rwkv_g2, v_rwkv_k_k, v_rwkv_k_a, v_rwkv_r_k, v_rwkv_gn_w, v_rwkv_gn_b, v_q_norm, v_k_norm, v_w_br_rwkv, v_w_br_attn, v_w_out, v_ffn2_norm, v_ffn2_w_gate, v_ffn2_w_up, v_ffn2_w_down, v_ple_norm, v_ple_w_gate, v_ple_w_proj):
    args = locals()
    w = {n: args[n][0] for n in WEIGHTS}
    m = {n: args["m_" + n][0] for n in WEIGHTS}
    v = {n: args["v_" + n][0] for n in WEIGHTS}

    w_loc = _local_blocks(w)
    head = GATHER_GROUPS["head"]
    wg = dict(zip(head, _all_gather_hbm("gather_head", [w_loc[n] for n in head])))
    me = 4 * lax.axis_index("x") + 2 * lax.axis_index("y") + lax.axis_index("c")
    gathering, order_after = {}, wg[head[0]]
    for group in ("mid", "rest"):
        shards = [w_loc[n] for n in GATHER_GROUPS[group]]
        zones = [lax.dynamic_update_slice(lax.empty((N_DEV,) + a.shape, a.dtype), a[None], (me, 0, 0)) for a in shards]
        *gathering[group], order_after = _exchange_start("gather_start_" + group, _gather_copies, shards, zones,
                                                         order_after)

    def fetch(group, after):
        _, got = _exchange_wait("gather_wait_" + group, _gather_copies, *gathering[group], after)
        return dict(zip(GATHER_GROUPS[group], got))

    sm = {n: w[n].reshape(1, -1) for n in SMALL}
    sm["ffn1_norm"] = sm["ffn1_norm"] + order_after[0, 0]
    in_flight = {}

    def scatter_early(group, arrays):
        arrs = [arrays[n] for n in SCATTER_GROUPS[group]]
        *in_flight[group], token = _exchange_start("scatter_start_" + group, _scatter_copies, arrs,
                                                   [lax.empty(a.shape, a.dtype) for a in arrs], arrs[0])
        return token
    loss_part, dx, gm, gs = _local_step(x[0], p[0, 0], positions[0], loss_target[0], sm, wg, fetch, scatter_early)
    loss = lax.psum(loss_part, ("x", "y", "c"))
    head_token = scatter_early("head", gm)
    recv, own = {}, {}

    def arrived(group, after):
        sent, lands = _exchange_wait("scatter_wait_" + group, _scatter_copies, *in_flight[group], after)
        own.update(zip(SCATTER_GROUPS[group], sent))
        recv.update(zip(SCATTER_GROUPS[group], lands))
    for group in ("tail", "mixer", "ffn1_down"):
        arrived(group, head_token)
    small_like = [w[n] for n in SMALL]
    small_rows = 80
    gs_all = _all_gather_vmem(_pack_small([gs[n] for n in SMALL], small_rows))
    gs_sum = _sum_slots("sum_small_grads", gs_all, small_rows)

    res = {}
    early = [e for e in ADAM_PLAN if e[1] not in SCATTER_GROUPS["head"]]
    late = [e for e in ADAM_PLAN if e[1] in SCATTER_GROUPS["head"]]
    for n, src, rb, cb, cw, tr in early + late:
        if (n, src, rb, cb, cw, tr) == late[0]:
            arrived("head", res["delta", early[-1][0]])
        outs4 = _adamw_slots("adamw_" + n, recv[src], own[src], rb, cb, cw, w[n], m[n], v[n], tr, head_token)
        for tag, a in zip(("grad", "delta", "new_m", "new_v"), outs4):
            res[tag, n] = a[None]
    d_s, m_s, v_s = _adamw("adamw_small", _pack_small(small_like, small_rows), gs_sum,
                           _pack_small([m[n] for n in SMALL], small_rows),
                           _pack_small([v[n] for n in SMALL], small_rows), small_rows)
    for tag, small in (("grad", gs_sum), ("delta", d_s), ("new_m", m_s), ("new_v", v_s)):
        for n, a in zip(SMALL, _unpack_small(small, small_like)):
            res[tag, n] = a[None]
    outs = [loss, dx[None]]
    for tag in ("grad", "delta", "new_m", "new_v"):
        outs += [res[tag, n] for n in WEIGHTS]
    return tuple(outs)
```

```python
import functools

import jax
import jax.numpy as jnp
from jax import lax
from jax.experimental import pallas as pl
from jax.experimental.pallas import tpu as pltpu

F32, BF16 = jnp.float32, jnp.bfloat16
MESH = pl.DeviceIdType.MESH
N_DEV = 8
LANES = 128
VMEM_LIMIT = 56 * 1024 * 1024

D_MODEL = 1024
PLE_DIM = 256
HEAD_DIM = 64
RWKV_HEADS = 8
RWKV_DIM = RWKV_HEADS * HEAD_DIM
DECAY_LORA = 64
ICLR_LORA = 64
GATE_LORA = 128
GN_EPS = 64e-5
ATTN_GROUPS = ((128, 1), (512, 4), (2048, 16))
HEADS_PER_GROUP = 4
ATTN_HEADS = HEADS_PER_GROUP * len(ATTN_GROUPS)
ATTN_DIM = ATTN_HEADS * HEAD_DIM
BAND_BLOCK = 128
ROPE_THETA = 10000.0
NEG_INF = -1e30
D_FF = 2816
RMS_EPS = 1e-6
RWKV_COLS = 3 * RWKV_DIM + DECAY_LORA + ICLR_LORA + GATE_LORA
ADAM_LR = 0.001
ADAM_B1 = 0.9
ADAM_B2 = 0.999
ADAM_EPS = 1e-08
ADAM_WD = 0.01
ADAM_STEP = 10

V_LO = LANES // RWKV_HEADS
V_HI = HEAD_DIM // V_LO
SCAN_CHUNK = 64

FF_SHARD = D_FF // N_DEV
FF_PAD = -(-FF_SHARD // LANES) * LANES
FF_HID = N_DEV * FF_PAD
IN_SHARD = 6144 // N_DEV
OUT_SHARD = D_MODEL // N_DEV

SMALL = ("ffn1_norm", "mix_norm", "rwkv_mu", "rwkv_w0", "rwkv_a0", "rwkv_k_k", "rwkv_k_a", "rwkv_r_k",
         "rwkv_gn_w", "rwkv_gn_b", "q_norm", "k_norm", "ffn2_norm", "ple_norm")
WEIGHTS = ("ffn1_norm", "ffn1_w_gate", "ffn1_w_up", "ffn1_w_down", "mix_norm", "w_in", "rwkv_mu", "rwkv_w0",
           "rwkv_w2", "rwkv_a0", "rwkv_a2", "rwkv_g2", "rwkv_k_k", "rwkv_k_a", "rwkv_r_k", "rwkv_gn_w",
           "rwkv_gn_b", "q_norm", "k_norm", "w_br_rwkv", "w_br_attn", "w_out", "ffn2_norm", "ffn2_w_gate",
           "ffn2_w_up", "ffn2_w_down", "ple_norm", "ple_w_gate", "ple_w_proj")


def _cparams(**kw):
    return pltpu.CompilerParams(vmem_limit_bytes=VMEM_LIMIT, **kw)


def _tile(n, cap):
    best = None
    for t in range(LANES, min(n, cap) + 1, LANES):
        if n % t == 0:
            best = t
    return best if best is not None else n


@jax.custom_vjp
def _bdot(a, w):
    return jnp.dot(a.astype(BF16), w.astype(BF16), preferred_element_type=F32)


def _bdot_fwd(a, w):
    return _bdot(a, w), (a, w)


def _bdot_bwd(res, g):
    a, w = res
    gb = g.astype(BF16)
    da = lax.dot_general(gb, w.astype(BF16), (((1,), (1,)), ((), ())), preferred_element_type=F32)
    dw = lax.dot_general(a.astype(BF16), gb, (((0,), (0,)), ((), ())), preferred_element_type=F32)
    return da.astype(a.dtype), dw.astype(w.dtype)


_bdot.defvjp(_bdot_fwd, _bdot_bwd)


@jax.custom_vjp
def _bdot_nt(a, b):
    return lax.dot_general(a.astype(BF16), b.astype(BF16), (((1,), (1,)), ((), ())), preferred_element_type=F32)


def _bdot_nt_fwd(a, b):
    return _bdot_nt(a, b), (a, b)


def _bdot_nt_bwd(res, g):
    a, b = res
    gb = g.astype(BF16)
    da = jnp.dot(gb, b.astype(BF16), preferred_element_type=F32)
    db = lax.dot_general(gb, a.astype(BF16), (((0,), (0,)), ((), ())), preferred_element_type=F32)
    return da.astype(a.dtype), db.astype(b.dtype)


_bdot_nt.defvjp(_bdot_nt_fwd, _bdot_nt_bwd)


def _mm(name, a, b, mode, out_dtype=F32, res=None, scale=None):
    if mode == "nn":
        (m, k), n = a.shape, b.shape[1]
    elif mode == "nt":
        (m, k), n = a.shape, b.shape[0]
    else:
        (k, m), n = a.shape, b.shape[1]
    tm, tn = _tile(m, 512), _tile(n, 512)
    a_spec = pl.BlockSpec((k, tm), lambda i, j: (0, i)) if mode == "tn" else pl.BlockSpec((tm, k), lambda i, j: (i, 0))
    b_spec = pl.BlockSpec((tn, k), lambda i, j: (j, 0)) if mode == "nt" else pl.BlockSpec((k, tn), lambda i, j: (0, j))
    dims = {"nn": ((1,), (0,)), "nt": ((1,), (1,)), "tn": ((0,), (0,))}[mode]
    o_spec = pl.BlockSpec((tm, tn), lambda i, j: (i, j))
    ins, in_specs = [a, b], [a_spec, b_spec]
    if res is not None:
        ins.append(res)
        in_specs.append(o_spec)

    def body(*refs):
        acc = lax.dot_general(refs[0][...].astype(BF16), refs[1][...].astype(BF16), (dims, ((), ())),
                              preferred_element_type=F32)
        if scale is not None:
            acc = acc * scale
        if res is not None:
            acc = acc + refs[2][...].astype(F32)
        refs[-1][...] = acc.astype(refs[-1].dtype)

    return pl.pallas_call(
        body, name=name, grid=(m // tm, n // tn), in_specs=in_specs, out_specs=o_spec,
        out_shape=jax.ShapeDtypeStruct((m, n), out_dtype),
        compiler_params=_cparams(dimension_semantics=("parallel", "parallel")),
    )(*ins)


def _mmc_nn(name, a, wb, ki, ci, n, out_dtype=F32):
    m, k = a.shape
    tm = _tile(m, 512)

    def body(a_ref, w_ref, o_ref):
        o_ref[...] = jnp.dot(a_ref[...].astype(BF16), w_ref[...], preferred_element_type=F32).astype(o_ref.dtype)

    return pl.pallas_call(
        body, name=name, grid=(m // tm, N_DEV),
        in_specs=[pl.BlockSpec((tm, k), lambda i, j: (i, 0)), pl.BlockSpec((None, k, n), lambda i, j: (j, ki, ci))],
        out_specs=pl.BlockSpec((tm, n), lambda i, j: (i, j)),
        out_shape=jax.ShapeDtypeStruct((m, N_DEV * n), out_dtype),
        compiler_params=_cparams(dimension_semantics=("parallel", "parallel")),
    )(a, wb)


def _mmc_nt(name, a, wb, ki, ci, n, k, res=None):
    m = a.shape[0]
    tm = _tile(m, 512)
    o_spec = pl.BlockSpec((tm, k), lambda i, j: (i, 0))
    ins = [a, wb] + ([res] if res is not None else [])
    in_specs = [pl.BlockSpec((tm, n), lambda i, j: (i, j)), pl.BlockSpec((None, k, n), lambda i, j: (j, ki, ci))]
    in_specs += [o_spec] if res is not None else []

    def body(*refs):
        a_ref, w_ref, o_ref = refs[0], refs[1], refs[-1]
        acc = lax.dot_general(a_ref[...].astype(BF16), w_ref[...], (((1,), (1,)), ((), ())),
                              preferred_element_type=F32)

        @pl.when(pl.program_id(1) == 0)
        def _():
            o_ref[...] = acc + refs[2][...] if res is not None else acc

        @pl.when(pl.program_id(1) != 0)
        def _():
            o_ref[...] += acc

    return pl.pallas_call(
        body, name=name, grid=(m // tm, N_DEV), in_specs=in_specs, out_specs=o_spec,
        out_shape=jax.ShapeDtypeStruct((m, k), F32),
        compiler_params=_cparams(dimension_semantics=("parallel", "arbitrary")),
    )(*ins)


def _mmc_tn(name, x, dy, n):
    m, k = x.shape
    tk = _tile(k, 512)

    def body(x_ref, dy_ref, o_ref):
        o_ref[...] = lax.dot_general(x_ref[...].astype(BF16), dy_ref[...].astype(BF16), (((0,), (0,)), ((), ())),
                                     preferred_element_type=F32).astype(o_ref.dtype)

    return pl.pallas_call(
        body, name=name, grid=(N_DEV, k // tk),
        in_specs=[pl.BlockSpec((m, tk), lambda j, i: (0, i)), pl.BlockSpec((m, n), lambda j, i: (0, j))],
        out_specs=pl.BlockSpec((None, tk, n), lambda j, i: (j, i, 0)),
        out_shape=jax.ShapeDtypeStruct((N_DEV, k, n), BF16),
        compiler_params=_cparams(dimension_semantics=("parallel", "parallel")),
    )(x, dy)


def _rowwise(name, fn, rows, params, out_rows, out_accs=(), tr=256):
    rows = [a if isinstance(a, tuple) else (a, a.shape[1], 0) for a in rows]
    r = rows[0][0].shape[0]
    in_specs = [pl.BlockSpec((tr, wd), lambda i, cb=cb: (i, cb)) for _, wd, cb in rows]
    rows = [a for a, _, _ in rows]
    in_specs += [pl.BlockSpec(p.shape, lambda i, nd=p.ndim: (0,) * nd) for p in params]
    out_shape = [jax.ShapeDtypeStruct((r, c), dt) for c, dt in out_rows]
    out_shape += [jax.ShapeDtypeStruct(s, F32) for s in out_accs]
    out_specs = [pl.BlockSpec((tr, c), lambda i: (i, 0)) for c, _ in out_rows]
    out_specs += [pl.BlockSpec(s, lambda i, nd=len(s): (0,) * nd) for s in out_accs]
    n_in, n_ro = len(rows) + len(params), len(out_rows)

    def body(*refs):
        res = fn(*[ref[...] for ref in refs[:n_in]])
        outs = refs[n_in:]
        for o, v in zip(outs[:n_ro], res[:n_ro]):
            o[...] = v.astype(o.dtype)
        for o, v in zip(outs[n_ro:], res[n_ro:]):
            _accumulate(o, v)

    return pl.pallas_call(
        body, name=name, grid=(r // tr,), in_specs=in_specs, out_specs=out_specs, out_shape=out_shape,
        compiler_params=_cparams(dimension_semantics=("arbitrary",)),
    )(*rows, *params)


def _pairwise(name, fn, rows, params, n_out, n_acc=0, tr=512):
    t, c = rows[0].shape
    tile = pl.BlockSpec((tr, 2 * HEAD_DIM), lambda p, i: (i, p))
    vec = pl.BlockSpec((1, 2 * HEAD_DIM), lambda p, i: (0, p))
    n_in = len(rows) + len(params)

    def body(*refs):
        res = fn(*[ref[...] for ref in refs[:n_in]])
        outs = refs[n_in:]
        for o, v in zip(outs[:n_out], res[:n_out]):
            o[...] = v
        first = pl.program_id(1) == 0
        for o, v in zip(outs[n_out:], res[n_out:]):
            @pl.when(first)
            def _(o=o, v=v):
                o[...] = v

            @pl.when(jnp.logical_not(first))
            def _(o=o, v=v):
                o[...] += v

    return pl.pallas_call(
        body, name=name, grid=(c // (2 * HEAD_DIM), t // tr),
        in_specs=[tile] * len(rows) + [vec] * len(params), out_specs=[tile] * n_out + [vec] * n_acc,
        out_shape=[jax.ShapeDtypeStruct((t, c), F32)] * n_out + [jax.ShapeDtypeStruct((1, c), F32)] * n_acc,
        compiler_params=_cparams(dimension_semantics=("parallel", "arbitrary")),
    )(*rows, *params)


def _accumulate(o_ref, v):
    @pl.when(pl.program_id(0) == 0)
    def _():
        o_ref[...] = v

    @pl.when(pl.program_id(0) != 0)
    def _():
        o_ref[...] += v


def _rms(x, g):
    return x * lax.rsqrt(jnp.mean(x * x, axis=-1, keepdims=True) + RMS_EPS) * g


def _sigmoid(x):
    return jax.nn.sigmoid(x)


def _softplus(x):
    return jnp.maximum(x, 0.0) + jnp.log1p(jnp.exp(-jnp.abs(x)))


def _norm_fwd(name, x, g):
    return _rowwise(name, lambda xv, gv: (_rms(xv, gv),), [x], [g], [(x.shape[1], BF16)])[0]


def _norm_bwd(name, x, g, dh, dres):
    def fn(xv, dhv, drv, gv):
        _, vjp = jax.vjp(_rms, xv, gv)
        dx, dg = vjp(dhv)
        return dx + drv, dg
    return _rowwise(name, fn, [x, dh, dres], [g], [(x.shape[1], F32)], [g.shape])


def _f_act(gate, up):
    return gate * _sigmoid(gate) * up


def _gate_up_act(name, h, w_gu):
    m, k = h.shape
    tm = _tile(m, 512)

    def body(h_ref, w_ref, gu_ref, a_ref):
        gu = jnp.dot(h_ref[...], w_ref[...], preferred_element_type=F32)
        gu_ref[...] = gu
        a_ref[...] = _f_act(gu[:, :FF_PAD], gu[:, FF_PAD:]).astype(a_ref.dtype)

    return pl.pallas_call(
        body, name=name, grid=(m // tm, N_DEV),
        in_specs=[pl.BlockSpec((tm, k), lambda i, j: (i, 0)),
                  pl.BlockSpec((None, k, 2 * FF_PAD), lambda i, j: (j, 0, 0))],
        out_specs=[pl.BlockSpec((tm, 2 * FF_PAD), lambda i, j: (i, j)), pl.BlockSpec((tm, FF_PAD), lambda i, j: (i, j))],
        out_shape=[jax.ShapeDtypeStruct((m, N_DEV * 2 * FF_PAD), F32), jax.ShapeDtypeStruct((m, FF_HID), BF16)],
        compiler_params=_cparams(dimension_semantics=("parallel", "parallel")),
    )(h, w_gu)


def _gate_up_act_bwd(name, dout, w_down, gu, order):
    m, k = dout.shape
    tm = _tile(m, 512)

    def body(d_ref, w_ref, gu_ref, order_ref, o_ref):
        da = 0.5 * lax.dot_general(d_ref[...].astype(BF16), w_ref[...], (((1,), (1,)), ((), ())),
                                   preferred_element_type=F32)
        guv = gu_ref[...]
        _, vjp = jax.vjp(_f_act, guv[:, :FF_PAD], guv[:, FF_PAD:])
        o_ref[...] = jnp.concatenate(vjp(da), axis=1).astype(o_ref.dtype)

    gu_spec = pl.BlockSpec((tm, 2 * FF_PAD), lambda i, j: (i, j))
    return pl.pallas_call(
        body, name=name, grid=(m // tm, N_DEV),
        in_specs=[pl.BlockSpec((tm, k), lambda i, j: (i, 0)), pl.BlockSpec((FF_PAD, k), lambda i, j: (j, 0)), gu_spec,
                  pl.BlockSpec(memory_space=pl.ANY)],
        out_specs=gu_spec, out_shape=jax.ShapeDtypeStruct((m, N_DEV * 2 * FF_PAD), BF16),
        compiler_params=_cparams(dimension_semantics=("parallel", "parallel")),
    )(dout, w_down, gu, order)


def _ffn_fwd(tag, x, norm, w_gu, w_down):
    h = _norm_fwd(tag + "_norm", x, norm)
    gu, a = _gate_up_act(tag + "_gu", h, w_gu)
    wd = w_down(a)
    out = _mm(tag + "_down", a, wd, "nn", res=x, scale=0.5)
    return out, (h, gu, a), wd


def _ffn_bwd(tag, x, norm, w_gu, w_down, saved, dout, on_down=None):
    h, gu, a = saved
    d_wdown = _mm(tag + "_dwdown", a, dout, "tn", out_dtype=BF16, scale=0.5).reshape(N_DEV, FF_PAD, D_MODEL)
    token = on_down(d_wdown) if on_down is not None else jnp.zeros((8, LANES), F32)
    dgu = _gate_up_act_bwd(tag + "_dgu", dout, w_down, gu, token)
    dh =_mmc_nt(tag + "_dh", dgu, w_gu, 0, 0, 2 * FF_PAD, D_MODEL)
    d_wgu = _mmc_tn(tag + "_dwgu", h, dgu, 2 * FF_PAD)
    dx, dnorm = _norm_bwd(tag + "_dnorm", x, norm, dh, dout)
    return dx, dnorm, d_wgu, d_wdown


def _shift_fwd(z, mu):
    t, c = z.shape[0], RWKV_COLS
    tr = 256

    def body(z_ref, zp_ref, mu_ref, r_ref, k_ref, v_ref, lo_ref, gd_ref):
        zv = z_ref[...]
        prev = zp_ref[7:8, :] * jnp.where(pl.program_id(0) == 0, 0.0, 1.0)
        row = lax.broadcasted_iota(jnp.int32, zv.shape, 0)
        zsh = jnp.where(row == 0, prev, pltpu.roll(zv, 1, 0))
        zs = zv + (zsh - zv) * mu_ref[...]
        r_ref[...] = zs[:, 0:512]
        k_ref[...] = zs[:, 512:1024]
        v_ref[...] = zs[:, 1024:1536]
        lo_ref[...] = zs[:, 1536:1664]
        gd_ref[...] = zs[:, 1664:1792]

    widths = (512, 512, 512, 128, 128)
    return pl.pallas_call(
        body, name="rwkv_shift", grid=(t // tr,),
        in_specs=[pl.BlockSpec((tr, c), lambda i: (i, 0)),
                  pl.BlockSpec((8, c), lambda i: (jnp.maximum(i * (tr // 8) - 1, 0), 0)),
                  pl.BlockSpec((1, c), lambda i: (0, 0))],
        out_specs=[pl.BlockSpec((tr, w), lambda i: (i, 0)) for w in widths],
        out_shape=[jax.ShapeDtypeStruct((t, w), F32) for w in widths],
        compiler_params=_cparams(dimension_semantics=("parallel",)),
    )(z, z, mu)


def _shift_bwd(z, mu, dr, dk, dv, dlo, dgd):
    t, c = z.shape[0], RWKV_COLS
    tr = 256
    nt = t // tr

    def body(z_ref, zp_ref, mu_ref, dr_ref, dk_ref, dv_ref, dlo_ref, dgd_ref,
             drn_ref, dkn_ref, dvn_ref, dlon_ref, dgdn_ref, dz_ref, dmu_ref):
        i = pl.program_id(0)
        zv, muv = z_ref[...], mu_ref[...]
        prev = zp_ref[7:8, :] * jnp.where(i == 0, 0.0, 1.0)
        row = lax.broadcasted_iota(jnp.int32, zv.shape, 0)
        zsh = jnp.where(row == 0, prev, pltpu.roll(zv, 1, 0))
        dzs = jnp.concatenate([dr_ref[...], dk_ref[...], dv_ref[...], dlo_ref[...], dgd_ref[...]], axis=1)
        nxt = jnp.concatenate([drn_ref[0:1, :], dkn_ref[0:1, :], dvn_ref[0:1, :], dlon_ref[0:1, :],
                               dgdn_ref[0:1, :]], axis=1) * jnp.where(i == nt - 1, 0.0, 1.0)
        u = dzs * muv
        un = jnp.where(row == tr - 1, nxt * muv, pltpu.roll(u, tr - 1, 0))
        dz_ref[...] = (dzs - u + un).astype(dz_ref.dtype)
        _accumulate(dmu_ref, jnp.sum(dzs * (zsh - zv), axis=0, keepdims=True))

    widths = (512, 512, 512, 128, 128)
    nxt_map = lambda i: (jnp.minimum((i + 1) * (tr // 8), t // 8 - 1), 0)
    return pl.pallas_call(
        body, name="rwkv_shift_bwd", grid=(nt,),
        in_specs=[pl.BlockSpec((tr, c), lambda i: (i, 0)),
                  pl.BlockSpec((8, c), lambda i: (jnp.maximum(i * (tr // 8) - 1, 0), 0)),
                  pl.BlockSpec((1, c), lambda i: (0, 0))]
        + [pl.BlockSpec((tr, w), lambda i: (i, 0)) for w in widths]
        + [pl.BlockSpec((8, w), nxt_map) for w in widths],
        out_specs=[pl.BlockSpec((tr, c), lambda i: (i, 0)), pl.BlockSpec((1, c), lambda i: (0, 0))],
        out_shape=[jax.ShapeDtypeStruct((t, c), BF16), jax.ShapeDtypeStruct((1, c), F32)],
        compiler_params=_cparams(dimension_semantics=("arbitrary",)),
    )(z, z, mu, dr, dk, dv, dlo, dgd, dr, dk, dv, dlo, dgd)


def _f_pre(k, lo, gd, w0, w2p, a0, a2p, g2, k_k, k_a):
    lane = lax.broadcasted_iota(jnp.int32, lo.shape, 1)
    lo_act = jnp.where(lane < DECAY_LORA, jnp.tanh(lo), lo)
    w = -_softplus(-(w0 + _bdot(lo_act, w2p))) - 0.5
    a = _sigmoid(a0 + _bdot(lo_act, a2p))
    g = _bdot(_sigmoid(gd), g2)
    kk = k * k_k
    k2 = k * (1.0 + (a - 1.0) * k_a)
    decay = jnp.exp(-jnp.exp(w))
    return k2, kk, a, decay, g


def _f_kk(kk, a):
    kkn = kk * lax.rsqrt(jnp.maximum(_head_sums(kk * kk), 1e-24))
    return kkn, kkn * a


def _f_post(y, r, k2, v, g, gn_w, gn_b, r_k):
    mean = _head_sums(y) * (1.0 / HEAD_DIM)
    var = _head_sums(jnp.square(y - mean)) * (1.0 / HEAD_DIM)
    yn = (y - mean) * lax.rsqrt(var + GN_EPS) * gn_w + gn_b
    bonus = _head_sums(r * k2 * r_k) * v
    return (yn + bonus) * g


def _to_v_rows(x):
    t = x.shape[0]
    return x.reshape(t, RWKV_HEADS, V_HI, V_LO).transpose(0, 2, 3, 1).reshape(t, V_HI, LANES)


def _from_v_rows(x):
    t = x.shape[0]
    return x.reshape(t, V_HI, V_LO, RWKV_HEADS).transpose(0, 3, 1, 2).reshape(t, RWKV_DIM)


def _k_cols(x):
    return jnp.tile(x, (V_LO, 1)).T


def _k_rows(x):
    xt = x.T
    out = xt[0:RWKV_HEADS]
    for l in range(1, V_LO):
        out = out + xt[l * RWKV_HEADS:(l + 1) * RWKV_HEADS]
    return out


def _wkv_fwd(r, w, k, kk, b, v):
    t = r.shape[0]
    tc = SCAN_CHUNK
    key_spec = pl.BlockSpec((tc, RWKV_HEADS, HEAD_DIM), lambda i: (i, 0, 0))
    row_spec = pl.BlockSpec((tc, V_HI, LANES), lambda i: (i, 0, 0))

    def body(r_ref, w_ref, k_ref, kk_ref, b_ref, v_ref, y_ref, st_ref, s_scr, cols_a, cols_b):
        @pl.when(pl.program_id(0) == 0)
        def _():
            s_scr[...] = jnp.zeros_like(s_scr)

        def prep(ti, buf):
            for n, ref in enumerate((r_ref, w_ref, k_ref, kk_ref, b_ref)):
                buf[n] = _k_cols(ref[ti])

        def step(ti, s, cur, nxt, ti_next):
            rc, wc, kc, kkc, bc = (cur[n] for n in range(5))
            prep(ti_next, nxt)
            vt = v_ref[ti]
            new, ys = [], []
            for j in range(V_HI):
                sa = -jnp.sum(s[j] * kkc, axis=0, keepdims=True)
                nj = s[j] * wc + bc * sa + kc * vt[j:j + 1]
                st_ref[ti, j] = nj
                ys.append(jnp.sum(nj * rc, axis=0, keepdims=True))
                new.append(nj)
            y_ref[ti] = jnp.concatenate(ys, axis=0)
            return tuple(new)

        def pair(i, s):
            s = step(2 * i, s, cols_a, cols_b, 2 * i + 1)
            return step(2 * i + 1, s, cols_b, cols_a, jnp.minimum(2 * i + 2, tc - 1))

        prep(0, cols_a)
        s = lax.fori_loop(0, tc // 2, pair, tuple(s_scr[j] for j in range(V_HI)))
        for j in range(V_HI):
            s_scr[j] = s[j]

    return pl.pallas_call(
        body, name="wkv_fwd", grid=(t // tc,),
        in_specs=[key_spec] * 5 + [row_spec],
        out_specs=[row_spec, pl.BlockSpec((tc, V_HI, HEAD_DIM, LANES), lambda i: (i, 0, 0, 0))],
        out_shape=[jax.ShapeDtypeStruct((t, V_HI, LANES), F32),
                   jax.ShapeDtypeStruct((t, V_HI, HEAD_DIM, LANES), F32)],
        scratch_shapes=[pltpu.VMEM((V_HI, HEAD_DIM, LANES), F32)] + [pltpu.VMEM((5, HEAD_DIM, LANES), F32)] * 2,
        compiler_params=_cparams(dimension_semantics=("arbitrary",)),
    )(r, w, k, kk, b, v)


def _wkv_bwd(r, w, k, kk, b, v, states, dy):
    t = r.shape[0]
    tc = SCAN_CHUNK
    nb = t // tc
    key_spec = pl.BlockSpec((tc, RWKV_HEADS, HEAD_DIM), lambda i: (nb - 1 - i, 0, 0))
    row_spec = pl.BlockSpec((tc, V_HI, LANES), lambda i: (nb - 1 - i, 0, 0))
    st_spec = pl.BlockSpec((tc, V_HI, HEAD_DIM, LANES), lambda i: (nb - 1 - i, 0, 0, 0))
    stp_spec = pl.BlockSpec((1, V_HI, HEAD_DIM, LANES), lambda i: (jnp.maximum((nb - 1 - i) * tc - 1, 0), 0, 0, 0))

    def body(r_ref, w_ref, k_ref, kk_ref, b_ref, v_ref, st_ref, stp_ref, dy_ref,
             dr_ref, dw_ref, dk_ref, dkk_ref, db_ref, dv_ref, ds_scr, cols_a, cols_b, accs_a, accs_b):
        @pl.when(pl.program_id(0) == 0)
        def _():
            ds_scr[...] = jnp.zeros_like(ds_scr)

        def colsum(x):
            return jnp.sum(x, axis=0, keepdims=True)

        def prep(ti, buf):
            for n, ref in enumerate((r_ref, w_ref, k_ref, kk_ref, b_ref)):
                buf[n] = _k_cols(ref[ti])

        def flush(ti, buf):
            for n, ref in enumerate((dr_ref, dk_ref, db_ref, dw_ref, dkk_ref)):
                ref[ti] = _k_rows(buf[n])

        def step(ti, ds, sp, cur, accs):
            rc, wc, kc, kkc, bc = (cur[n] for n in range(5))
            vt, dyt = v_ref[ti], dy_ref[ti]
            acc = None
            new, dvs = [], []
            for j in range(V_HI):
                st = st_ref[ti, j]
                dsj = ds[j] + rc * dyt[j:j + 1]
                sa = -colsum(sp[j] * kkc)
                dsa = colsum(dsj * bc)
                dvs.append(colsum(dsj * kc))
                parts = (st * dyt[j:j + 1], dsj * vt[j:j + 1], dsj * sa, dsj * sp[j], -(sp[j] * dsa))
                acc = parts if acc is None else tuple(a + q for a, q in zip(acc, parts))
                new.append(dsj * wc - kkc * dsa)
            dv_ref[ti] = jnp.concatenate(dvs, axis=0)
            for n in range(5):
                accs[n] = acc[n]
            return tuple(new)

        def states_before(ti):
            return tuple(st_ref[ti - 1, j] for j in range(V_HI))

        def pair(i, ds):
            ta = tc - 1 - 2 * i
            prep(ta - 1, cols_b)
            flush(jnp.minimum(ta + 1, tc - 1), accs_b)
            ds = step(ta, ds, states_before(ta), cols_a, accs_a)
            prep(ta - 2, cols_a)
            flush(ta, accs_a)
            return step(ta - 1, ds, states_before(ta - 1), cols_b, accs_b)

        prep(tc - 1, cols_a)
        accs_b[...] = jnp.zeros_like(accs_b)
        ds = lax.fori_loop(0, tc // 2 - 1, pair, tuple(ds_scr[j] for j in range(V_HI)))
        prep(0, cols_b)
        flush(2, accs_b)
        ds = step(1, ds, states_before(1), cols_a, accs_a)
        flush(1, accs_a)
        keep = jnp.where(pl.program_id(0) == nb - 1, 0.0, 1.0)
        ds = step(0, ds, tuple(stp_ref[0, j] * keep for j in range(V_HI)), cols_b, accs_b)
        flush(0, accs_b)
        for j in range(V_HI):
            ds_scr[j] = ds[j]

    key_out = jax.ShapeDtypeStruct((t, RWKV_HEADS, HEAD_DIM), F32)
    return pl.pallas_call(
        body, name="wkv_bwd", grid=(nb,),
        in_specs=[key_spec] * 5 + [row_spec, st_spec, stp_spec, row_spec],
        out_specs=[key_spec] * 5 + [row_spec],
        out_shape=[key_out] * 5 + [jax.ShapeDtypeStruct((t, V_HI, LANES), F32)],
        scratch_shapes=[pltpu.VMEM((V_HI, HEAD_DIM, LANES), F32)] + [pltpu.VMEM((5, HEAD_DIM, LANES), F32)] * 4,
        compiler_params=_cparams(dimension_semantics=("arbitrary",)),
    )(r, w, k, kk, b, v, states, states, dy)


PAIR = 2 * HEAD_DIM
N_PAIRS = ATTN_HEADS // 2
Q_COL0 = RWKV_COLS // PAIR
K_COL0 = Q_COL0 + N_PAIRS
V_COL0 = K_COL0 + N_PAIRS


def _swap_halves(x):
    lane = lax.broadcasted_iota(jnp.int32, x.shape, 1)
    return jnp.where((lane & (HEAD_DIM - 1)) < HEAD_DIM // 2, pltpu.roll(x, PAIR - HEAD_DIM // 2, 1),
                     pltpu.roll(x, HEAD_DIM // 2, 1))


@jax.custom_vjp
def _rope(x, cosf, sinf):
    return x * cosf + _swap_halves(x) * sinf


def _rope_fwd(x, cosf, sinf):
    return _rope(x, cosf, sinf), (cosf, sinf)


def _rope_bwd(res, d):
    cosf, sinf = res
    return d * cosf + _swap_halves(d * sinf), jnp.zeros_like(cosf), jnp.zeros_like(sinf)


_rope.defvjp(_rope_fwd, _rope_bwd)


def _head_sums(x):
    lane = lax.broadcasted_iota(jnp.int32, x.shape, 1)
    lo = jnp.where(lane < HEAD_DIM, 1.0, 0.0)
    hi = 1.0 - lo
    return lo * jnp.sum(x * lo, axis=1, keepdims=True) + hi * jnp.sum(x * hi, axis=1, keepdims=True)


def _f_qk(x, cosf, sinf, gain2):
    xn = x * lax.rsqrt(_head_sums(x * x) * (1.0 / HEAD_DIM) + RMS_EPS) * gain2
    return _rope(xn, cosf, sinf)


def _qk_prep(z, tab, q_gain, k_gain):
    t = z.shape[0]
    tr = 1024

    def body(z_ref, c_ref, s_ref, qg_ref, kg_ref, o_ref):
        g = jnp.where(pl.program_id(0) < N_PAIRS, qg_ref[...], kg_ref[...])
        o_ref[...] = _f_qk(z_ref[...], c_ref[...], s_ref[...], jnp.concatenate([g, g], axis=1))

    gain = pl.BlockSpec((1, HEAD_DIM), lambda c, i: (0, 0))
    return pl.pallas_call(
        body, name="qk_prep", grid=(2 * N_PAIRS, t // tr),
        in_specs=[pl.BlockSpec((tr, PAIR), lambda c, i: (i, Q_COL0 + c)), pl.BlockSpec((tr, PAIR), lambda c, i: (i, 0)),
                  pl.BlockSpec((tr, PAIR), lambda c, i: (i, 1)), gain, gain],
        out_specs=pl.BlockSpec((tr, PAIR), lambda c, i: (i, c)),
        out_shape=jax.ShapeDtypeStruct((t, 2 * N_PAIRS * PAIR), F32),
        compiler_params=_cparams(dimension_semantics=("parallel", "parallel")),
    )(z, tab, tab, q_gain, k_gain)


def _qk_prep_bwd(z, tab, q_gain, k_gain, dq, dk):
    t = z.shape[0]
    tr = 1024

    def body(z_ref, c_ref, s_ref, qg_ref, kg_ref, dq_ref, dk_ref, dz_ref, dqg_ref, dkg_ref):
        c, i = pl.program_id(0), pl.program_id(1)
        is_q = c < N_PAIRS
        g = jnp.where(is_q, qg_ref[...], kg_ref[...])
        d = jnp.where(is_q, dq_ref[...], dk_ref[...])
        _, vjp = jax.vjp(lambda xx, gg: _f_qk(xx, c_ref[...], s_ref[...], gg), z_ref[...],
                         jnp.concatenate([g, g], axis=1))
        dx, dg2 = vjp(d)
        dz_ref[...] = dx.astype(dz_ref.dtype)
        dg = dg2[:, :HEAD_DIM] + dg2[:, HEAD_DIM:]
        first_q = jnp.logical_and(c == 0, i == 0)
        first_k = jnp.logical_and(c == N_PAIRS, i == 0)

        @pl.when(first_q)
        def _():
            dqg_ref[...] = dg

        @pl.when(jnp.logical_and(is_q, jnp.logical_not(first_q)))
        def _():
            dqg_ref[...] += dg

        @pl.when(first_k)
        def _():
            dkg_ref[...] = dg

        @pl.when(jnp.logical_and(jnp.logical_not(is_q), jnp.logical_not(first_k)))
        def _():
            dkg_ref[...] += dg

    gain = pl.BlockSpec((1, HEAD_DIM), lambda c, i: (0, 0))
    return pl.pallas_call(
        body, name="qk_prep_bwd", grid=(2 * N_PAIRS, t // tr),
        in_specs=[pl.BlockSpec((tr, PAIR), lambda c, i: (i, Q_COL0 + c)), pl.BlockSpec((tr, PAIR), lambda c, i: (i, 0)),
                  pl.BlockSpec((tr, PAIR), lambda c, i: (i, 1)), gain, gain,
                  pl.BlockSpec((tr, PAIR), lambda c, i: (i, jnp.minimum(c, N_PAIRS - 1))),
                  pl.BlockSpec((tr, PAIR), lambda c, i: (i, jnp.maximum(c - N_PAIRS, 0)))],
        out_specs=[pl.BlockSpec((tr, PAIR), lambda c, i: (i, c)), gain, gain],
        out_shape=[jax.ShapeDtypeStruct((t, 2 * N_PAIRS * PAIR), BF16), jax.ShapeDtypeStruct((1, HEAD_DIM), F32),
                   jax.ShapeDtypeStruct((1, HEAD_DIM), F32)],
        compiler_params=_cparams(dimension_semantics=("arbitrary", "arbitrary")),
    )(z, tab, tab, q_gain, k_gain, dq, dk)


def _attn_block(q, kp, kc, vp, vc, kmin):
    k2 = jnp.concatenate([kp, kc], axis=0)
    v2 = jnp.concatenate([vp, vc], axis=0)
    s = _bdot_nt(q, k2) * (HEAD_DIM ** -0.5)
    qi = lax.broadcasted_iota(jnp.int32, s.shape, 0)
    kj = lax.broadcasted_iota(jnp.int32, s.shape, 1)
    dist = qi + BAND_BLOCK - kj
    valid = (dist >= 0) & (dist <= BAND_BLOCK) & (kj >= kmin)
    s = jnp.where(valid, s, NEG_INF)
    m = lax.stop_gradient(jnp.max(s, axis=-1, keepdims=True))
    e = jnp.exp(s - m)
    l = jnp.sum(e, axis=-1, keepdims=True)
    o = _bdot(e, v2) / l
    return o, m + jnp.log(l)


def _fold(src_ref, dst_ref, dil):
    t = src_ref.shape[0]
    ln = t // dil
    for j in range(dil):
        dst_ref[j * ln:(j + 1) * ln, :] = src_ref[pl.ds(j, ln, stride=dil), :]


def _unfold(src_ref, dst_ref, dil):
    t = src_ref.shape[0]
    ln = t // dil
    for j in range(dil):
        dst_ref[pl.ds(j, ln, stride=dil), :] = src_ref[j * ln:(j + 1) * ln, :]


def _per_group(fn):
    pair = pl.program_id(0)
    for gi, (_, dil) in enumerate(ATTN_GROUPS):
        @pl.when(jnp.logical_or(pair == 2 * gi, pair == 2 * gi + 1))
        def _(dil=dil):
            fn(dil)


def _block_rows(idx, blocks_per_seq):
    first = (idx & (blocks_per_seq - 1)) == 0
    cur = pl.ds(pl.multiple_of(idx * BAND_BLOCK, BAND_BLOCK), BAND_BLOCK)
    prev = pl.ds(pl.multiple_of(jnp.maximum(idx - 1, 0) * BAND_BLOCK, BAND_BLOCK), BAND_BLOCK)
    return first, cur, prev


def _heads(x):
    return x[:, :HEAD_DIM], x[:, HEAD_DIM:]


def _attn_fwd(qk, z):
    t = z.shape[0]
    n_blocks = t // BAND_BLOCK

    def body(q_ref, k_ref, v_ref, o_ref, lse_ref, qf, kf, vf, of, lf):
        def run(dil):
            _fold(q_ref, qf, dil)
            _fold(k_ref, kf, dil)
            _fold(v_ref, vf, dil)
            blocks_per_seq = n_blocks // dil

            def block(idx, carry):
                first, cur, prev = _block_rows(idx, blocks_per_seq)
                kmin = jnp.where(first, BAND_BLOCK, 0)
                outs, lses = [], []
                for q, kp, kc, vp, vc in zip(_heads(qf[cur, :]), _heads(kf[prev, :]), _heads(kf[cur, :]),
                                             _heads(vf[prev, :]), _heads(vf[cur, :])):
                    o, ls = _attn_block(q, kp, kc, vp, vc, kmin)
                    outs.append(o)
                    lses.append(jnp.broadcast_to(ls, o.shape))
                of[cur, :] = jnp.concatenate(outs, axis=1)
                lf[cur, :] = jnp.concatenate(lses, axis=1)
                return carry

            lax.fori_loop(0, n_blocks, block, 0)
            _unfold(of, o_ref, dil)
            _unfold(lf, lse_ref, dil)

        _per_group(run)

    slab = jax.ShapeDtypeStruct((t, N_PAIRS * PAIR), F32)
    out_spec = pl.BlockSpec((t, PAIR), lambda p: (0, p))
    return pl.pallas_call(
        body, name="attn_fwd", grid=(N_PAIRS,),
        in_specs=[pl.BlockSpec((t, PAIR), lambda p: (0, p)), pl.BlockSpec((t, PAIR), lambda p: (0, N_PAIRS + p)),
                  pl.BlockSpec((t, PAIR), lambda p: (0, V_COL0 + p))],
        out_specs=[out_spec, out_spec], out_shape=[slab, slab],
        scratch_shapes=[pltpu.VMEM((t, PAIR), F32)] * 5,
        compiler_params=_cparams(dimension_semantics=("parallel",)),
    )(qk, qk, z)


def _attn_bwd(qk, z, do, dlse):
    t = z.shape[0]
    n_blocks = t // BAND_BLOCK

    def body(q_ref, k_ref, v_ref, do_ref, dl_ref, dq_ref, dk_ref, dv_ref, qf, kf, vf, dof, dlf, dqf, dkf, dvf):
        def run(dil):
            for src, dst in ((q_ref, qf), (k_ref, kf), (v_ref, vf), (do_ref, dof), (dl_ref, dlf)):
                _fold(src, dst, dil)
            blocks_per_seq = n_blocks // dil

            def block(idx, carry):
                first, cur, prev = _block_rows(idx, blocks_per_seq)
                kmin = jnp.where(first, BAND_BLOCK, 0)
                grads = []
                for q, kp, kc, vp, vc, do_h, dl_h in zip(
                        _heads(qf[cur, :]), _heads(kf[prev, :]), _heads(kf[cur, :]), _heads(vf[prev, :]),
                        _heads(vf[cur, :]), _heads(dof[cur, :]), _heads(dlf[cur, :])):
                    _, vjp = jax.vjp(functools.partial(_attn_block, kmin=kmin), q, kp, kc, vp, vc)
                    grads.append(vjp((do_h, jnp.sum(dl_h, axis=1, keepdims=True))))
                dq, dkp, dkc, dvp, dvc = (jnp.concatenate([a, b], axis=1) for a, b in zip(*grads))
                dqf[cur, :] = dq
                dkf[cur, :] = dkc
                dvf[cur, :] = dvc

                @pl.when(jnp.logical_not(first))
                def _():
                    dkf[prev, :] += dkp
                    dvf[prev, :] += dvp

                return carry

            lax.fori_loop(0, n_blocks, block, 0)
            _unfold(dqf, dq_ref, dil)
            _unfold(dkf, dk_ref, dil)
            _unfold(dvf, dv_ref, dil)

        _per_group(run)

    slab = jax.ShapeDtypeStruct((t, N_PAIRS * PAIR), F32)
    own = pl.BlockSpec((t, PAIR), lambda p: (0, p))
    return pl.pallas_call(
        body, name="attn_bwd", grid=(N_PAIRS,),
        in_specs=[own, pl.BlockSpec((t, PAIR), lambda p: (0, N_PAIRS + p)),
                  pl.BlockSpec((t, PAIR), lambda p: (0, V_COL0 + p)), own, own],
        out_specs=[own] * 3, out_shape=[slab] * 3,
        scratch_shapes=[pltpu.VMEM((t, PAIR), F32)] * 8,
        compiler_params=_cparams(dimension_semantics=("parallel",)),
    )(qk, qk, z, do, dlse)


def _f_comb(o1, o2, o3, l1, l2, l3):
    m = jnp.maximum(jnp.maximum(l1, l2), l3)
    e1, e2, e3 = jnp.exp(l1 - m), jnp.exp(l2 - m), jnp.exp(l3 - m)
    den = e1 + e2 + e3
    return (e1 / den) * o1 + (e2 / den) * o2 + (e3 / den) * o3


def _all_gather_hbm(name, arrs):
    na = len(arrs)

    def body(*refs):
        x_refs, out_refs = refs[:na], refs[na:2 * na]
        send_sems, recv_sems, local_sems = refs[2 * na:]
        mx, my, mc = lax.axis_index("x"), lax.axis_index("y"), lax.axis_index("c")
        me, sibling = (mx, my, mc), (mx, my, 1 - mc)
        chips = [(1 - mx, my), (mx, 1 - my), (1 - mx, 1 - my)]

        def slot(a, px, py, pc):
            return out_refs[a].at[4 * px + 2 * py + pc]

        def copy(a, k, block, to, src=None):
            return pltpu.make_async_remote_copy(
                src_ref=slot(a, *block) if src is None else src, dst_ref=slot(a, *block),
                send_sem=send_sems.at[a, k], recv_sem=recv_sems.at[a, k], device_id=to, device_id_type=MESH)

        mine = [pltpu.make_async_copy(x_refs[a], slot(a, *me), local_sems.at[a]) for a in range(na)]
        for cp in mine:
            cp.start()
        first = []
        for a in range(na):
            first.append(copy(a, 0, me, sibling, src=x_refs[a]))
            first += [copy(a, 1 + j, me, (*chip, mc), src=x_refs[a]) for j, chip in enumerate(chips)]
        for cp in first:
            cp.start()
        passed = []
        for j, chip in enumerate(chips):
            for a in range(na):
                copy(a, 1 + j, (*chip, mc), me).wait_recv()
                passed.append(copy(a, 4 + j, (*chip, mc), sibling))
                passed[-1].start()
        for a in range(na):
            copy(a, 0, sibling, me).wait_recv()
            for j, chip in enumerate(chips):
                copy(a, 4 + j, (*chip, 1 - mc), me).wait_recv()
        for cp in first + passed:
            cp.wait_send()
        for cp in mine:
            cp.wait()

    hbm = pl.BlockSpec(memory_space=pl.ANY)
    return pl.pallas_call(
        body, name=name,
        out_shape=[jax.ShapeDtypeStruct((N_DEV,) + a.shape, a.dtype) for a in arrs],
        in_specs=[hbm] * na, out_specs=[hbm] * na,
        scratch_shapes=[pltpu.SemaphoreType.DMA((na, 7)), pltpu.SemaphoreType.DMA((na, 7)),
                        pltpu.SemaphoreType.DMA((na,))],
    )(*arrs)


def _all_gather_vmem(x):
    rws, cols = x.shape

    def body(x_ref, out_ref, send_sems, recv_sems):
        mx, my, mc = lax.axis_index("x"), lax.axis_index("y"), lax.axis_index("c")
        me, sibling = (mx, my, mc), (mx, my, 1 - mc)
        chips = [(1 - mx, my), (mx, 1 - my), (1 - mx, 1 - my)]

        def slot(px, py, pc):
            return out_ref.at[4 * px + 2 * py + pc]

        def copy(k, block, to, src=None):
            return pltpu.make_async_remote_copy(
                src_ref=slot(*block) if src is None else src, dst_ref=slot(*block),
                send_sem=send_sems.at[k], recv_sem=recv_sems.at[k], device_id=to, device_id_type=MESH)

        first = [copy(0, me, sibling, src=x_ref)]
        first += [copy(1 + j, me, (*chip, mc), src=x_ref) for j, chip in enumerate(chips)]
        for cp in first:
            cp.start()
        out_ref[4 * mx + 2 * my + mc] = x_ref[...]
        passed = [copy(4 + j, (*chip, mc), sibling) for j, chip in enumerate(chips)]
        for j, chip in enumerate(chips):
            copy(1 + j, (*chip, mc), me).wait_recv()
            passed[j].start()
        copy(0, sibling, me).wait_recv()
        for j, chip in enumerate(chips):
            copy(4 + j, (*chip, 1 - mc), me).wait_recv()
        for cp in first + passed:
            cp.wait_send()

    return pl.pallas_call(
        body, name="all_gather_small",
        out_shape=jax.ShapeDtypeStruct((N_DEV, rws, cols), x.dtype),
        in_specs=[pl.BlockSpec(memory_space=pltpu.VMEM)], out_specs=pl.BlockSpec(memory_space=pltpu.VMEM),
        scratch_shapes=[pltpu.SemaphoreType.DMA((7,)), pltpu.SemaphoreType.DMA((7,))],
    )(x)


def _all_to_all_hbm(name, arrs):
    na = len(arrs)

    def body(*refs):
        g_refs, out_refs = refs[:na], refs[na:2 * na]
        send_sems, recv_sems, local_sems = refs[2 * na:]
        mx, my, mc = lax.axis_index("x"), lax.axis_index("y"), lax.axis_index("c")
        me = 4 * mx + 2 * my + mc
        mine = [pltpu.make_async_copy(g_refs[a].at[me], out_refs[a].at[me], local_sems.at[a]) for a in range(na)]
        for cp in mine:
            cp.start()
        copies = []
        for k in range(1, N_DEV):
            px, py, pc = mx ^ (k >> 2), my ^ ((k >> 1) & 1), mc ^ (k & 1)
            peer = 4 * px + 2 * py + pc
            for a in range(na):
                copies.append(pltpu.make_async_remote_copy(
                    src_ref=g_refs[a].at[peer], dst_ref=out_refs[a].at[me], send_sem=send_sems.at[a, k - 1],
                    recv_sem=recv_sems.at[a, k - 1], device_id=(px, py, pc), device_id_type=MESH))
        for cp in copies:
            cp.start()
        for cp in copies:
            cp.wait_recv()
        for cp in copies:
            cp.wait_send()
        for cp in mine:
            cp.wait()

    hbm = pl.BlockSpec(memory_space=pl.ANY)
    return pl.pallas_call(
        body, name=name,
        out_shape=[jax.ShapeDtypeStruct(a.shape, a.dtype) for a in arrs],
        in_specs=[hbm] * na, out_specs=[hbm] * na,
        scratch_shapes=[pltpu.SemaphoreType.DMA((na, 7)), pltpu.SemaphoreType.DMA((na, 7)),
                        pltpu.SemaphoreType.DMA((na,))],
    )(*arrs)


def _scatter_copies(g_refs, land_refs, send_sems, recv_sems):
    mx, my, mc = lax.axis_index("x"), lax.axis_index("y"), lax.axis_index("c")
    me = 4 * mx + 2 * my + mc
    copies = []
    for k in range(1, N_DEV):
        px, py, pc = mx ^ (k >> 2), my ^ ((k >> 1) & 1), mc ^ (k & 1)
        peer = 4 * px + 2 * py + pc
        for a, (g_ref, land_ref) in enumerate(zip(g_refs, land_refs)):
            copies.append(pltpu.make_async_remote_copy(
                src_ref=g_ref.at[peer], dst_ref=land_ref.at[me], send_sem=send_sems.at[a * (N_DEV - 1) + k - 1],
                recv_sem=recv_sems.at[a * (N_DEV - 1) + k - 1], device_id=(px, py, pc), device_id_type=MESH))
    return copies


_HBM = pl.BlockSpec(memory_space=pltpu.HBM)
_SEM = pl.BlockSpec(memory_space=pltpu.SEMAPHORE)
_DATAFLOW = pltpu.SideEffectType.DATAFLOW_SIDE_EFFECTING


def _gather_copies(x_refs, land_refs, send_sems, recv_sems):
    mx, my, mc = lax.axis_index("x"), lax.axis_index("y"), lax.axis_index("c")
    me = 4 * mx + 2 * my + mc
    copies = []
    for k in range(1, N_DEV):
        px, py, pc = mx ^ (k >> 2), my ^ ((k >> 1) & 1), mc ^ (k & 1)
        for a, (x_ref, land_ref) in enumerate(zip(x_refs, land_refs)):
            copies.append(pltpu.make_async_remote_copy(
                src_ref=x_ref, dst_ref=land_ref.at[me], send_sem=send_sems.at[a * (N_DEV - 1) + k - 1],
                recv_sem=recv_sems.at[a * (N_DEV - 1) + k - 1], device_id=(px, py, pc), device_id_type=MESH))
    return copies


N_CHIPS = N_DEV // 2


def _pair_swap(name, arrs):
    na = len(arrs)

    def body(*refs):
        g_refs, out_refs = refs[:na], refs[na:2 * na]
        send_sems, recv_sems = refs[2 * na:]
        mx, my, mc = lax.axis_index("x"), lax.axis_index("y"), lax.axis_index("c")
        copies = [pltpu.make_async_remote_copy(
            src_ref=g_refs[a].at[2 * q + 1 - mc], dst_ref=out_refs[a].at[q], send_sem=send_sems.at[a, q],
            recv_sem=recv_sems.at[a, q], device_id=(mx, my, 1 - mc), device_id_type=MESH)
            for a in range(na) for q in range(N_CHIPS)]
        for cp in copies:
            cp.start()
        for cp in copies:
            cp.wait_recv()
        for cp in copies:
            cp.wait_send()

    hbm = pl.BlockSpec(memory_space=pl.ANY)
    return pl.pallas_call(
        body, name=name,
        out_shape=[jax.ShapeDtypeStruct((N_CHIPS,) + a.shape[1:], a.dtype) for a in arrs],
        in_specs=[hbm] * na, out_specs=[hbm] * na,
        scratch_shapes=[pltpu.SemaphoreType.DMA((na, N_CHIPS)), pltpu.SemaphoreType.DMA((na, N_CHIPS))],
    )(*arrs)


def _pair_add(name, g, half, core):
    _, r, c = g.shape
    tr = _tile(r, 512) if r % LANES == 0 else r

    def body(core_ref, g_ref, h_ref, o_ref):
        o_ref[...] = (g_ref[...].astype(F32) + h_ref[...].astype(F32)).astype(o_ref.dtype)

    return pl.pallas_call(
        body, name=name,
        grid_spec=pltpu.PrefetchScalarGridSpec(
            num_scalar_prefetch=1, grid=(N_CHIPS, r // tr),
            in_specs=[pl.BlockSpec((None, tr, c), lambda q, i, core_ref: (2 * q + core_ref[0], i, 0)),
                      pl.BlockSpec((None, tr, c), lambda q, i, core_ref: (q, i, 0))],
            out_specs=pl.BlockSpec((None, tr, c), lambda q, i, core_ref: (q, i, 0))),
        out_shape=jax.ShapeDtypeStruct((N_CHIPS, r, c), g.dtype),
        compiler_params=_cparams(dimension_semantics=("parallel", "parallel")),
    )(core, g, half)


def _chip_copies(h_refs, land_refs, send_sems, recv_sems):
    mx, my, mc = lax.axis_index("x"), lax.axis_index("y"), lax.axis_index("c")
    my_chip = 2 * mx + my
    copies = []
    for k in range(1, N_CHIPS):
        px, py = mx ^ (k >> 1), my ^ (k & 1)
        for a, (h_ref, land_ref) in enumerate(zip(h_refs, land_refs)):
            copies.append(pltpu.make_async_remote_copy(
                src_ref=h_ref.at[2 * px + py], dst_ref=land_ref.at[my_chip], send_sem=send_sems.at[a * (N_DEV - 1) + k - 1],
                recv_sem=recv_sems.at[a * (N_DEV - 1) + k - 1], device_id=(px, py, mc), device_id_type=MESH))
    return copies


def _exchange_start(name, copies, srcs, lands, after):
    na = len(srcs)

    def body(*refs):
        for cp in copies(refs[:na], refs[na:2 * na], refs[2 * na + 1], refs[2 * na + 2]):
            cp.start()
        refs[-1][...] = jnp.zeros_like(refs[-1])

    in_hbm = lambda a: pltpu.with_memory_space_constraint(a, pltpu.HBM)
    outs = pl.pallas_call(
        body, name=name,
        out_shape=(pltpu.SemaphoreType.DMA((na * (N_DEV - 1),)), pltpu.SemaphoreType.DMA((na * (N_DEV - 1),)),
                   *[pltpu.HBM(a.shape, a.dtype) for a in list(srcs) + list(lands)],
                   jax.ShapeDtypeStruct((8, LANES), F32)),
        in_specs=[_HBM] * (2 * na) + [pl.BlockSpec(memory_space=pl.ANY)],
        out_specs=(_SEM, _SEM, *[_HBM] * (2 * na), pl.BlockSpec(memory_space=pltpu.VMEM)),
        input_output_aliases={i: 2 + i for i in range(2 * na)},
        compiler_params=pltpu.CompilerParams(has_side_effects=_DATAFLOW),
    )(*[in_hbm(a) for a in srcs], *[in_hbm(a) for a in lands], after)
    return outs[0], outs[1], outs[2:2 + na], outs[2 + na:2 + 2 * na], outs[-1]


def _exchange_wait(name, copies, send_sems, recv_sems, srcs, lands, after):
    na = len(srcs)

    def body(*refs):
        for cp in copies(refs[:na], refs[na:2 * na], refs[2 * na], refs[2 * na + 1]):
            cp.wait_send()
            cp.wait_recv()

    outs = pl.pallas_call(
        body, name=name,
        out_shape=[pltpu.HBM(a.shape, a.dtype) for a in list(srcs) + list(lands)],
        in_specs=[_HBM] * (2 * na) + [_SEM, _SEM, pl.BlockSpec(memory_space=pl.ANY)], out_specs=[_HBM] * (2 * na),
        input_output_aliases={i: i for i in range(2 * na)},
        compiler_params=pltpu.CompilerParams(has_side_effects=_DATAFLOW),
    )(*srcs, *lands, send_sems, recv_sems, after)
    return outs[:na], outs[na:]


def _sum_slots(name, g, tr):
    _, rws, cols = g.shape

    def body(g_ref, o_ref):
        acc = g_ref[0].astype(F32)
        for j in range(1, N_DEV):
            acc = acc + g_ref[j].astype(F32)
        o_ref[...] = acc

    return pl.pallas_call(
        body, name=name, grid=(rws // tr,),
        in_specs=[pl.BlockSpec((N_DEV, tr, cols), lambda i: (0, i, 0))],
        out_specs=pl.BlockSpec((tr, cols), lambda i: (i, 0)),
        out_shape=jax.ShapeDtypeStruct((rws, cols), F32),
        compiler_params=_cparams(dimension_semantics=("parallel",)),
    )(g)


def _adam_math(wv, gv, mv, vv):
    mn = ADAM_B1 * mv + (1.0 - ADAM_B1) * gv
    vn = ADAM_B2 * vv + (1.0 - ADAM_B2) * jnp.square(gv)
    m_hat = mn / (1.0 - ADAM_B1 ** ADAM_STEP)
    v_hat = vn / (1.0 - ADAM_B2 ** ADAM_STEP)
    delta = -ADAM_LR * (m_hat / (jnp.sqrt(v_hat) + ADAM_EPS) + ADAM_WD * wv)
    return delta, mn, vn


def _adamw(name, w, g, m, v, tr):
    return _rowwise(name, _adam_math, [w, g, m, v], [], [(LANES, F32)] * 3, tr=tr)


def _adamw_slots(name, recv, own, rb, cb, cw, w, m, v, tr, order):
    nr, nc = w.shape
    n = recv.shape[0]

    def body(g_ref, own_ref, w_ref, m_ref, v_ref, order_ref, go_ref, d_ref, mo_ref, vo_ref):
        me = 2 * lax.axis_index("x") + lax.axis_index("y")
        if n == N_DEV:
            me = 2 * me + lax.axis_index("c")
        acc = None
        for s in range(n):
            part = jnp.where(me == s, own_ref[s], g_ref[s]).astype(F32)
            acc = part if acc is None else acc + part
        g = acc[:, :nc]
        go_ref[...] = g
        d_ref[...], mo_ref[...], vo_ref[...] = _adam_math(w_ref[...], g, m_ref[...], v_ref[...])

    nat = pl.BlockSpec((tr, nc), lambda i: (i, 0))
    slots = pl.BlockSpec((n, tr, cw), lambda i: (0, rb + i, cb))
    return pl.pallas_call(
        body, name=name, grid=(nr // tr,),
        in_specs=[slots, slots, nat, nat, nat, pl.BlockSpec(memory_space=pl.ANY)],
        out_specs=[nat] * 4, out_shape=[jax.ShapeDtypeStruct((nr, nc), F32)] * 4,
        compiler_params=_cparams(dimension_semantics=("parallel",)),
    )(recv, own, w, m, v, order)


def _local_blocks(w):
    pad_cols = lambda a: jnp.pad(a, ((0, 0), (0, FF_PAD - FF_SHARD)))
    pad_rows = lambda a: jnp.pad(a, ((0, FF_PAD - FF_SHARD), (0, 0)))
    gate_up = lambda tag: jnp.concatenate([pad_cols(w[tag + "_w_gate"]), pad_cols(w[tag + "_w_up"])], axis=1)
    blocks = {
        "ffn1_gu": gate_up("ffn1"), "ffn1_d": pad_rows(w["ffn1_w_down"]), "w_in": w["w_in"],
        "lora": jnp.concatenate([w["rwkv_w2"], w["rwkv_a2"], w["rwkv_g2"]], axis=0),
        "br": jnp.concatenate([w["w_br_rwkv"], w["w_br_attn"], w["ple_w_proj"]], axis=0),
        "w_out": w["w_out"], "ffn2_gu": gate_up("ffn2"), "ffn2_d": pad_rows(w["ffn2_w_down"]),
        "ple_gate": w["ple_w_gate"],
    }
    return {n: a.astype(BF16) for n, a in blocks.items()}


GATHER_GROUPS = {"head": ("ffn1_gu", "ffn1_d"), "mid": ("w_in", "lora"),
                 "rest": ("br", "w_out", "ffn2_gu", "ffn2_d", "ple_gate")}

SCATTER_GROUPS = {"tail": ("ple_gate", "ple_proj", "ffn2_gu", "ffn2_d"), "mixer": ("w_out", "br", "lora", "w_in"),
                  "ffn1_down": ("ffn1_d",), "head": ("ffn1_gu",)}

ADAM_PLAN = (
    ("ffn1_w_gate", "ffn1_gu", 0, 0, FF_PAD, 256), ("ffn1_w_up", "ffn1_gu", 0, 1, FF_PAD, 256),
    ("ffn1_w_down", "ffn1_d", 0, 0, D_MODEL, FF_SHARD // 2), ("w_in", "w_in", 0, 0, IN_SHARD, 256),
    ("rwkv_w2", "lora", 0, 0, HEAD_DIM, 64), ("rwkv_a2", "lora", 1, 0, HEAD_DIM, 64),
    ("rwkv_g2", "lora", 2, 0, HEAD_DIM, 64),
    ("w_br_rwkv", "br", 0, 0, OUT_SHARD, 256), ("w_br_attn", "br", 2, 0, OUT_SHARD, 256),
    ("ple_w_proj", "ple_proj", 0, 0, OUT_SHARD, 256), ("w_out", "w_out", 0, 0, D_MODEL, OUT_SHARD),
    ("ffn2_w_gate", "ffn2_gu", 0, 0, FF_PAD, 256), ("ffn2_w_up", "ffn2_gu", 0, 1, FF_PAD, 256),
    ("ffn2_w_down", "ffn2_d", 0, 0, D_MODEL, FF_SHARD // 2), ("ple_w_gate", "ple_gate", 0, 0, D_MODEL, OUT_SHARD),
)


def _pack_small(arrs, rows):
    flat = jnp.concatenate([a.reshape(-1) for a in arrs])
    return jnp.pad(flat, (0, rows * LANES - flat.shape[0])).reshape(rows, LANES)


def _unpack_small(flat, like):
    flat = flat.reshape(-1)
    out, off = [], 0
    for a in like:
        out.append(flat[off:off + a.size].reshape(a.shape))
        off += a.size
    return out


def _local_step(x, p, pos, target, sm, wg, fetch, on_grads):
    t = x.shape[0]

    w_d1 = wg["ffn1_d"].reshape(FF_HID, D_MODEL)
    x1, ffn1_saved, _ = _ffn_fwd("ffn1", x, sm["ffn1_norm"], wg["ffn1_gu"], lambda after: w_d1)
    wg = {**wg, **fetch("mid", x1)}
    full_cols = lambda blk: blk.transpose(1, 0, 2).reshape(blk.shape[1], N_DEV * blk.shape[2])
    lora_w2 = full_cols(wg["lora"][:, :DECAY_LORA])
    lora_a2 = full_cols(wg["lora"][:, DECAY_LORA:DECAY_LORA + ICLR_LORA])
    lora_g2 = full_cols(wg["lora"][:, DECAY_LORA + ICLR_LORA:])
    h2 = _norm_fwd("mix_norm", x1, sm["mix_norm"])
    z = _mmc_nn("w_in", h2, wg["w_in"], 0, 0, IN_SHARD)
    z_g = (z, 2 * D_MODEL, (RWKV_COLS + 3 * ATTN_DIM) // (2 * D_MODEL))

    r, k, v, lo, gd = _shift_fwd(z, sm["rwkv_mu"])
    zero_lo = jnp.zeros((DECAY_LORA, RWKV_DIM), BF16)
    w2p = jnp.concatenate([lora_w2, zero_lo], axis=0).astype(F32)
    a2p = jnp.concatenate([zero_lo, lora_a2], axis=0).astype(F32)
    pre_params = [sm["rwkv_w0"], w2p, sm["rwkv_a0"], a2p, lora_g2.astype(F32), sm["rwkv_k_k"], sm["rwkv_k_a"]]
    wide = [(RWKV_DIM, F32)]
    k2, kk, a, decay, g = _rowwise("rwkv_pre", _f_pre, [k, lo, gd], pre_params, wide * 5)
    kkn, b = _pairwise("rwkv_kk", _f_kk, [kk, a], [], 2)
    scan_in = [u.reshape(t, RWKV_HEADS, HEAD_DIM) for u in (r, decay, k2, kkn, b)]
    v_rows = _to_v_rows(v)
    y_rows, states = _wkv_fwd(*scan_in, v_rows)
    y = _from_v_rows(y_rows)
    post_params = [sm["rwkv_gn_w"], sm["rwkv_gn_b"], sm["rwkv_r_k"]]
    post_rows = [y, r, k2, v, g]
    y_rwkv = _pairwise("rwkv_post", lambda *av: (_f_post(*av),), post_rows, post_params, 1)[0]

    inv_freq = 1.0 / (ROPE_THETA ** (jnp.arange(0, HEAD_DIM, 2, dtype=F32) / HEAD_DIM))
    freq2 = jnp.tile(inv_freq, 2 * PAIR // HEAD_DIM).reshape(1, PAIR)
    half = jnp.ones((HEAD_DIM // 2,), F32)
    sign2 = jnp.tile(jnp.concatenate([-half, half]), PAIR // HEAD_DIM).reshape(1, PAIR)

    def rope_table(posv, fr, sg):
        ang = posv * fr
        return (jnp.concatenate([jnp.cos(ang), jnp.sin(ang) * sg], axis=1),)
    tab = _rowwise("rope_table", rope_table, [pos.astype(F32).reshape(t, 1)], [freq2, sign2], [(2 * PAIR, F32)])[0]
    qk = _qk_prep(z, tab, sm["q_norm"], sm["k_norm"])
    o_all, lse_all = _attn_fwd(qk, z)
    gw = HEADS_PER_GROUP * HEAD_DIM

    def by_group(ov, lv):
        return [ov[:, i * gw:(i + 1) * gw] for i in range(3)] + [lv[:, i * gw:(i + 1) * gw] for i in range(3)]
    y_attn = _rowwise("attn_comb", lambda ov, lv: (_f_comb(*by_group(ov, lv)),), [o_all, lse_all], [], [(gw, F32)])[0]

    wg = {**wg, **fetch("rest", y_rwkv)}
    w_d2 = wg["ffn2_d"].reshape(FF_HID, D_MODEL)
    w_out = wg["w_out"].reshape(D_MODEL, D_MODEL)
    w_pg = wg["ple_gate"].reshape(D_MODEL, D_MODEL)
    w_brr = full_cols(wg["br"][:, :RWKV_DIM])
    w_bra = full_cols(wg["br"][:, RWKV_DIM:RWKV_DIM + gw])
    w_pp = full_cols(wg["br"][:, RWKV_DIM + gw:])
    u_r = _mm("br_rwkv", y_rwkv, w_brr, "nn")
    u_a = _mm("br_attn", y_attn, w_bra, "nn")

    def f_merge(zgr, zga, ur, ua):
        return _sigmoid(zgr) * ur + _sigmoid(zga) * ua
    merged = _rowwise("merge", lambda zg, ur, ua: (f_merge(zg[:, :D_MODEL], zg[:, D_MODEL:], ur, ua),),
                      [z_g, u_r, u_a], [], [(D_MODEL, BF16)])[0]
    x2 = _mm("w_out", merged, w_out, "nn", res=x1)
    x3, ffn2_saved, _ = _ffn_fwd("ffn2", x2, sm["ffn2_norm"], wg["ffn2_gu"], lambda after: w_d2)

    hn = _norm_fwd("ple_norm", x3, sm["ple_norm"])
    gz = _mm("ple_gate", hn, w_pg, "nn")
    pp = _mm("ple_proj", p, w_pp, "nn")

    def f_head(x3v, gzv, ppv, tg):
        sg = _sigmoid(gzv)
        err = x3v + sg * ppv - tg
        part = 0.5 * jnp.sum(jnp.mean(err * err, axis=-1, keepdims=True))
        dx4 = err * (1.0 / D_MODEL)
        return dx4, dx4 * ppv * sg * (1.0 - sg), dx4 * sg, jnp.full((1, LANES), part, F32)
    dx4, dgz, dpp, loss_row = _rowwise("ple_loss", f_head, [x3, gz, pp, target], [],
                                       [(D_MODEL, F32), (D_MODEL, BF16), (D_MODEL, BF16)], [(1, LANES)])
    loss = loss_row[0, 0]

    gs, gm = {}, {}
    row_blocks = lambda g: g.reshape(N_DEV, g.shape[0] // N_DEV, g.shape[1])
    dhn = _mm("ple_dhn", dgz, w_pg, "nt")
    gm["ple_gate"] = row_blocks(_mm("ple_dwgate", hn, dgz, "tn", out_dtype=BF16))
    col_blocks = lambda g: g.reshape(g.shape[0], N_DEV, g.shape[1] // N_DEV).transpose(1, 0, 2)
    gm["ple_proj"] = col_blocks(_mm("ple_dwproj", p, dpp, "tn", out_dtype=BF16))
    dx3, gs["ple_norm"] = _norm_bwd("ple_dnorm", x3, sm["ple_norm"], dhn, dx4)

    dx2, gs["ffn2_norm"], gm["ffn2_gu"], gm["ffn2_d"] = _ffn_bwd(
        "ffn2", x2, sm["ffn2_norm"], wg["ffn2_gu"], w_d2, ffn2_saved, dx3)
    tail_token = on_grads("tail", {n: gm.pop(n) for n in SCATTER_GROUPS["tail"]})

    dmerged = _mm("w_out_dmerged", dx2, w_out, "nt")
    gm["w_out"] = row_blocks(_mm("w_out_dw", merged, dx2, "tn", out_dtype=BF16))

    def merge_bwd(zg, ur, ua, dm):
        _, vjp = jax.vjp(f_merge, zg[:, :D_MODEL], zg[:, D_MODEL:], ur, ua)
        dzr, dza, dur, dua = vjp(dm)
        return jnp.concatenate([dzr, dza], axis=1), dur, dua
    dz_g, du_r, du_a = _rowwise("merge_bwd", merge_bwd, [z_g, u_r, u_a, dmerged], [],
                                [(2 * D_MODEL, BF16), (D_MODEL, BF16), (D_MODEL, BF16)])
    dy_rwkv = _mm("br_rwkv_dy", du_r, w_brr, "nt")
    dy_attn = _mm("br_attn_dy", du_a, w_bra, "nt")
    gm["br"] = jnp.concatenate([col_blocks(_mm("br_rwkv_dw", y_rwkv, du_r, "tn", out_dtype=BF16)),
                                col_blocks(_mm("br_attn_dw", y_attn, du_a, "tn", out_dtype=BF16))], axis=1)

    def comb_bwd(ov, lv, dyv):
        _, vjp = jax.vjp(_f_comb, *by_group(ov, lv))
        d = vjp(dyv)
        return jnp.concatenate(d[:3], axis=1), jnp.concatenate(d[3:], axis=1)
    do_all, dl_all = _rowwise("attn_comb_bwd", comb_bwd, [o_all, lse_all, dy_attn], [],
                              [(ATTN_DIM, F32), (ATTN_DIM, F32)])
    dq_all, dk_all, dv_all = _attn_bwd(qk, z, do_all, dl_all)
    dqk_raw, gs["q_norm"], gs["k_norm"] = _qk_prep_bwd(z, tab, sm["q_norm"], sm["k_norm"], dq_all, dk_all)

    def post_bwd(yv, rv, k2v, vv, gv, dv_, gnw, gnb, rk):
        _, vjp = jax.vjp(_f_post, yv, rv, k2v, vv, gv, gnw, gnb, rk)
        return vjp(dv_)
    dy, dr1, dk2a, dv1, dg, d_gnw, d_gnb, d_rk = _pairwise(
        "rwkv_post_bwd", post_bwd, post_rows + [dy_rwkv], [post_params[0] + tail_token[0, 0]] + post_params[1:], 5, 3)
    gs["rwkv_gn_w"], gs["rwkv_gn_b"], gs["rwkv_r_k"] = d_gnw, d_gnb, d_rk
    dr2, ddecay, dk2b, dkkn, db, dv_rows = _wkv_bwd(*scan_in, v_rows, states, _to_v_rows(dy))
    dr2, ddecay, dk2b, dkkn, db = [u.reshape(t, RWKV_DIM) for u in (dr2, ddecay, dk2b, dkkn, db)]
    dv2 = _from_v_rows(dv_rows)

    def kk_bwd(kkv, av, dkknv, dbv, dra, drb, dva, dvb):
        _, vjp = jax.vjp(_f_kk, kkv, av)
        return (*vjp((dkknv, dbv)), dra + drb, dva + dvb)
    dkk, da, dr, dv = _pairwise("rwkv_kk_bwd", kk_bwd, [kk, a, dkkn, db, dr1, dr2, dv1, dv2], [], 4)

    def pre_bwd(kv, lov, gdv, dk2x, dk2y, dkkv, dav, ddec, dgv, w0, w2p_, a0, a2p_, g2, k_k, k_a):
        _, vjp = jax.vjp(_f_pre, kv, lov, gdv, w0, w2p_, a0, a2p_, g2, k_k, k_a)
        return vjp((dk2x + dk2y, dkkv, dav, ddec, dgv))
    lora_acc = (DECAY_LORA + ICLR_LORA, RWKV_DIM)
    dk, dlo, dgd, d_w0, d_w2p, d_a0, d_a2p, d_g2, d_kk, d_ka = _rowwise(
        "rwkv_pre_bwd", pre_bwd,
        [k, lo, gd, dk2a, dk2b, dkk, da, ddecay, dg],
        pre_params, [(RWKV_DIM, F32), (LANES, F32), (LANES, F32)],
        [(1, RWKV_DIM), lora_acc, (1, RWKV_DIM), lora_acc, (GATE_LORA, RWKV_DIM), (1, RWKV_DIM), (1, RWKV_DIM)])
    gs["rwkv_w0"], gs["rwkv_a0"], gs["rwkv_k_k"], gs["rwkv_k_a"] = d_w0, d_a0, d_kk, d_ka
    gm["lora"] = jnp.concatenate([col_blocks(d_w2p[:DECAY_LORA]), col_blocks(d_a2p[DECAY_LORA:]), col_blocks(d_g2)],
                                 axis=1).astype(BF16)
    dz_r, gs["rwkv_mu"] = _shift_bwd(z, sm["rwkv_mu"], dr, dk, dv, dlo, dgd)

    dz = jnp.concatenate([dz_r, dqk_raw, dv_all.astype(BF16), dz_g], axis=1)
    dh2 = _mmc_nt("w_in_dh", dz, wg["w_in"], 0, 0, IN_SHARD, D_MODEL)
    gm["w_in"] = _mmc_tn("w_in_dw", h2, dz, IN_SHARD)
    mixer_token = on_grads("mixer", {n: gm.pop(n) for n in SCATTER_GROUPS["mixer"]})
    dx1, gs["mix_norm"] = _norm_bwd("mix_dnorm", x1, sm["mix_norm"] + mixer_token[0, 0], dh2, dx2)

    dx0, gs["ffn1_norm"], gm["ffn1_gu"], _ = _ffn_bwd(
        "ffn1", x, sm["ffn1_norm"], wg["ffn1_gu"], w_d1, ffn1_saved, dx1,
        on_down=lambda blocks: on_grads("ffn1_down", {"ffn1_d": blocks}))
    return loss, dx0, gm, gs


def kernel(x, p, positions, ffn1_norm, ffn1_w_gate, ffn1_w_up, ffn1_w_down, mix_norm, w_in, rwkv_mu, rwkv_w0, rwkv_w2, rwkv_a0, rwkv_a2, rwkv_g2, rwkv_k_k, rwkv_k_a, rwkv_r_k, rwkv_gn_w, rwkv_gn_b, q_norm, k_norm, w_br_rwkv, w_br_attn, w_out, ffn2_norm, ffn2_w_gate, ffn2_w_up, ffn2_w_down, ple_norm, ple_w_gate, ple_w_proj, loss_target, m_ffn1_norm, m_ffn1_w_gate, m_ffn1_w_up, m_ffn1_w_down, m_mix_norm, m_w_in, m_rwkv_mu, m_rwkv_w0, m_rwkv_w2, m_rwkv_a0, m_rwkv_a2, m_rwkv_g2, m_rwkv_k_k, m_rwkv_k_a, m_rwkv_r_k, m_rwkv_gn_w, m_rwkv_gn_b, m_q_norm, m_k_norm, m_w_br_rwkv, m_w_br_attn, m_w_out, m_ffn2_norm, m_ffn2_w_gate, m_ffn2_w_up, m_ffn2_w_down, m_ple_norm, m_ple_w_gate, m_ple_w_proj, v_ffn1_norm, v_ffn1_w_gate, v_ffn1_w_up, v_ffn1_w_down, v_mix_norm, v_w_in, v_rwkv_mu, v_rwkv_w0, v_rwkv_w2, v_rwkv_a0, v_rwkv_a2, v_rwkv_g2, v_rwkv_k_k, v_rwkv_k_a, v_rwkv_r_k, v_rwkv_gn_w, v_rwkv_gn_b, v_q_norm, v_k_norm, v_w_br_rwkv, v_w_br_attn, v_w_out, v_ffn2_norm, v_ffn2_w_gate, v_ffn2_w_up, v_ffn2_w_down, v_ple_norm, v_ple_w_gate, v_ple_w_proj):
    args = locals()
    w = {n: args[n][0] for n in WEIGHTS}
    m = {n: args["m_" + n][0] for n in WEIGHTS}
    v = {n: args["v_" + n][0] for n in WEIGHTS}

    w_loc = _local_blocks(w)
    head = GATHER_GROUPS["head"]
    wg = dict(zip(head, _all_gather_hbm("gather_head", [w_loc[n] for n in head])))
    me = 4 * lax.axis_index("x") + 2 * lax.axis_index("y") + lax.axis_index("c")
    gathering, order_after = {}, wg[head[0]]
    for group in ("mid", "rest"):
        shards = [w_loc[n] for n in GATHER_GROUPS[group]]
        zones = [lax.dynamic_update_slice(lax.empty((N_DEV,) + a.shape, a.dtype), a[None], (me, 0, 0)) for a in shards]
        *gathering[group], order_after = _exchange_start("gather_start_" + group, _gather_copies, shards, zones,
                                                         order_after)

    def fetch(group, after):
        _, got = _exchange_wait("gather_wait_" + group, _gather_copies, *gathering[group], after)
        return dict(zip(GATHER_GROUPS[group], got))

    sm = {n: w[n].reshape(1, -1) for n in SMALL}
    sm["ffn1_norm"] = sm["ffn1_norm"] + order_after[0, 0]
    in_flight = {}

    def scatter_early(group, arrays):
        arrs = [arrays[n] for n in SCATTER_GROUPS[group]]
        *in_flight[group], token = _exchange_start("scatter_start_" + group, _scatter_copies, arrs,
                                                   [lax.empty(a.shape, a.dtype) for a in arrs], arrs[0])
        return token
    loss_part, dx, gm, gs = _local_step(x[0], p[0, 0], positions[0], loss_target[0], sm, wg, fetch, scatter_early)
    loss = lax.psum(loss_part, ("x", "y", "c"))
    head_names = SCATTER_GROUPS["head"]
    core = lax.axis_index("c").astype(jnp.int32).reshape(1)
    halves = _pair_swap("scatter_head_pair", [gm[n] for n in head_names])
    chip_sums = [_pair_add("scatter_head_add_" + n, gm[n], hf, core) for n, hf in zip(head_names, halves)]
    *in_flight["head"], head_token = _exchange_start(
        "scatter_start_head", _chip_copies, chip_sums, [lax.empty(a.shape, a.dtype) for a in chip_sums], halves[0])
    recv, own = {}, {}

    def arrived(group, after):
        copies = _chip_copies if group == "head" else _scatter_copies
        sent, lands = _exchange_wait("scatter_wait_" + group, copies, *in_flight[group], after)
        own.update(zip(SCATTER_GROUPS[group], sent))
        recv.update(zip(SCATTER_GROUPS[group], lands))
    for group in ("tail", "mixer", "ffn1_down"):
        arrived(group, head_token)
    small_like = [w[n] for n in SMALL]
    small_rows = 80
    gs_all = _all_gather_vmem(_pack_small([gs[n] for n in SMALL], small_rows))
    gs_sum = _sum_slots("sum_small_grads", gs_all, small_rows)

    res = {}
    early = [e for e in ADAM_PLAN if e[1] not in SCATTER_GROUPS["head"]]
    late = [e for e in ADAM_PLAN if e[1] in SCATTER_GROUPS["head"]]
    for n, src, rb, cb, cw, tr in early + late:
        if (n, src, rb, cb, cw, tr) == late[0]:
            arrived("head", res["delta", early[-1][0]])
        outs4 = _adamw_slots("adamw_" + n, recv[src], own[src], rb, cb, cw, w[n], m[n], v[n], tr, head_token)
        for tag, a in zip(("grad", "delta", "new_m", "new_v"), outs4):
            res[tag, n] = a[None]
    d_s, m_s, v_s = _adamw("adamw_small", _pack_small(small_like, small_rows), gs_sum,
                           _pack_small([m[n] for n in SMALL], small_rows),
                           _pack_small([v[n] for n in SMALL], small_rows), small_rows)
    for tag, small in (("grad", gs_sum), ("delta", d_s), ("new_m", m_s), ("new_v", v_s)):
        for n, a in zip(SMALL, _unpack_small(small, small_like)):
            res[tag, n] = a[None]
    outs = [loss, dx[None]]
    for tag in ("grad", "delta", "new_m", "new_v"):
        outs += [res[tag, n] for n in WEIGHTS]
    return tuple(outs)
```

```python
import functools

import jax
import jax.numpy as jnp
from jax import lax
from jax.experimental import pallas as pl
from jax.experimental.pallas import tpu as pltpu

F32, BF16 = jnp.float32, jnp.bfloat16
MESH = pl.DeviceIdType.MESH
N_DEV = 8
LANES = 128
VMEM_LIMIT = 56 * 1024 * 1024

D_MODEL = 1024
PLE_DIM = 256
HEAD_DIM = 64
RWKV_HEADS = 8
RWKV_DIM = RWKV_HEADS * HEAD_DIM
DECAY_LORA = 64
ICLR_LORA = 64
GATE_LORA = 128
GN_EPS = 64e-5
ATTN_GROUPS = ((128, 1), (512, 4), (2048, 16))
HEADS_PER_GROUP = 4
ATTN_HEADS = HEADS_PER_GROUP * len(ATTN_GROUPS)
ATTN_DIM = ATTN_HEADS * HEAD_DIM
BAND_BLOCK = 128
ROPE_THETA = 10000.0
NEG_INF = -1e30
D_FF = 2816
RMS_EPS = 1e-6
RWKV_COLS = 3 * RWKV_DIM + DECAY_LORA + ICLR_LORA + GATE_LORA
ADAM_LR = 0.001
ADAM_B1 = 0.9
ADAM_B2 = 0.999
ADAM_EPS = 1e-08
ADAM_WD = 0.01
ADAM_STEP = 10

V_LO = LANES // RWKV_HEADS
V_HI = HEAD_DIM // V_LO
SCAN_CHUNK = 64

FF_SHARD = D_FF // N_DEV
FF_PAD = -(-FF_SHARD // LANES) * LANES
FF_HID = N_DEV * FF_PAD
IN_SHARD = 6144 // N_DEV
OUT_SHARD = D_MODEL // N_DEV

SMALL = ("ffn1_norm", "mix_norm", "rwkv_mu", "rwkv_w0", "rwkv_a0", "rwkv_k_k", "rwkv_k_a", "rwkv_r_k",
         "rwkv_gn_w", "rwkv_gn_b", "q_norm", "k_norm", "ffn2_norm", "ple_norm")
WEIGHTS = ("ffn1_norm", "ffn1_w_gate", "ffn1_w_up", "ffn1_w_down", "mix_norm", "w_in", "rwkv_mu", "rwkv_w0",
           "rwkv_w2", "rwkv_a0", "rwkv_a2", "rwkv_g2", "rwkv_k_k", "rwkv_k_a", "rwkv_r_k", "rwkv_gn_w",
           "rwkv_gn_b", "q_norm", "k_norm", "w_br_rwkv", "w_br_attn", "w_out", "ffn2_norm", "ffn2_w_gate",
           "ffn2_w_up", "ffn2_w_down", "ple_norm", "ple_w_gate", "ple_w_proj")


def _cparams(**kw):
    return pltpu.CompilerParams(vmem_limit_bytes=VMEM_LIMIT, **kw)


def _tile(n, cap):
    best = None
    for t in range(LANES, min(n, cap) + 1, LANES):
        if n % t == 0:
            best = t
    return best if best is not None else n


@jax.custom_vjp
def _bdot(a, w):
    return jnp.dot(a.astype(BF16), w.astype(BF16), preferred_element_type=F32)


def _bdot_fwd(a, w):
    return _bdot(a, w), (a, w)


def _bdot_bwd(res, g):
    a, w = res
    gb = g.astype(BF16)
    da = lax.dot_general(gb, w.astype(BF16), (((1,), (1,)), ((), ())), preferred_element_type=F32)
    dw = lax.dot_general(a.astype(BF16), gb, (((0,), (0,)), ((), ())), preferred_element_type=F32)
    return da.astype(a.dtype), dw.astype(w.dtype)


_bdot.defvjp(_bdot_fwd, _bdot_bwd)


@jax.custom_vjp
def _bdot_nt(a, b):
    return lax.dot_general(a.astype(BF16), b.astype(BF16), (((1,), (1,)), ((), ())), preferred_element_type=F32)


def _bdot_nt_fwd(a, b):
    return _bdot_nt(a, b), (a, b)


def _bdot_nt_bwd(res, g):
    a, b = res
    gb = g.astype(BF16)
    da = jnp.dot(gb, b.astype(BF16), preferred_element_type=F32)
    db = lax.dot_general(gb, a.astype(BF16), (((0,), (0,)), ((), ())), preferred_element_type=F32)
    return da.astype(a.dtype), db.astype(b.dtype)


_bdot_nt.defvjp(_bdot_nt_fwd, _bdot_nt_bwd)


def _mm(name, a, b, mode, out_dtype=F32, res=None, scale=None):
    if mode == "nn":
        (m, k), n = a.shape, b.shape[1]
    elif mode == "nt":
        (m, k), n = a.shape, b.shape[0]
    else:
        (k, m), n = a.shape, b.shape[1]
    tm, tn = _tile(m, 512), _tile(n, 512)
    a_spec = pl.BlockSpec((k, tm), lambda i, j: (0, i)) if mode == "tn" else pl.BlockSpec((tm, k), lambda i, j: (i, 0))
    b_spec = pl.BlockSpec((tn, k), lambda i, j: (j, 0)) if mode == "nt" else pl.BlockSpec((k, tn), lambda i, j: (0, j))
    dims = {"nn": ((1,), (0,)), "nt": ((1,), (1,)), "tn": ((0,), (0,))}[mode]
    o_spec = pl.BlockSpec((tm, tn), lambda i, j: (i, j))
    ins, in_specs = [a, b], [a_spec, b_spec]
    if res is not None:
        ins.append(res)
        in_specs.append(o_spec)

    def body(*refs):
        acc = lax.dot_general(refs[0][...].astype(BF16), refs[1][...].astype(BF16), (dims, ((), ())),
                              preferred_element_type=F32)
        if scale is not None:
            acc = acc * scale
        if res is not None:
            acc = acc + refs[2][...].astype(F32)
        refs[-1][...] = acc.astype(refs[-1].dtype)

    return pl.pallas_call(
        body, name=name, grid=(m // tm, n // tn), in_specs=in_specs, out_specs=o_spec,
        out_shape=jax.ShapeDtypeStruct((m, n), out_dtype),
        compiler_params=_cparams(dimension_semantics=("parallel", "parallel")),
    )(*ins)


def _mmc_nn(name, a, wb, ki, ci, n, out_dtype=F32):
    m, k = a.shape
    tm = _tile(m, 512)

    def body(a_ref, w_ref, o_ref):
        o_ref[...] = jnp.dot(a_ref[...].astype(BF16), w_ref[...], preferred_element_type=F32).astype(o_ref.dtype)

    return pl.pallas_call(
        body, name=name, grid=(m // tm, N_DEV),
        in_specs=[pl.BlockSpec((tm, k), lambda i, j: (i, 0)), pl.BlockSpec((None, k, n), lambda i, j: (j, ki, ci))],
        out_specs=pl.BlockSpec((tm, n), lambda i, j: (i, j)),
        out_shape=jax.ShapeDtypeStruct((m, N_DEV * n), out_dtype),
        compiler_params=_cparams(dimension_semantics=("parallel", "parallel")),
    )(a, wb)


def _mmc_nt(name, a, wb, ki, ci, n, k, res=None):
    m = a.shape[0]
    tm = _tile(m, 512)
    o_spec = pl.BlockSpec((tm, k), lambda i, j: (i, 0))
    ins = [a, wb] + ([res] if res is not None else [])
    in_specs = [pl.BlockSpec((tm, n), lambda i, j: (i, j)), pl.BlockSpec((None, k, n), lambda i, j: (j, ki, ci))]
    in_specs += [o_spec] if res is not None else []

    def body(*refs):
        a_ref, w_ref, o_ref = refs[0], refs[1], refs[-1]
        acc = lax.dot_general(a_ref[...].astype(BF16), w_ref[...], (((1,), (1,)), ((), ())),
                              preferred_element_type=F32)

        @pl.when(pl.program_id(1) == 0)
        def _():
            o_ref[...] = acc + refs[2][...] if res is not None else acc

        @pl.when(pl.program_id(1) != 0)
        def _():
            o_ref[...] += acc

    return pl.pallas_call(
        body, name=name, grid=(m // tm, N_DEV), in_specs=in_specs, out_specs=o_spec,
        out_shape=jax.ShapeDtypeStruct((m, k), F32),
        compiler_params=_cparams(dimension_semantics=("parallel", "arbitrary")),
    )(*ins)


def _mmc_tn(name, x, dy, n):
    m, k = x.shape
    tk = _tile(k, 512)

    def body(x_ref, dy_ref, o_ref):
        o_ref[...] = lax.dot_general(x_ref[...].astype(BF16), dy_ref[...].astype(BF16), (((0,), (0,)), ((), ())),
                                     preferred_element_type=F32).astype(o_ref.dtype)

    return pl.pallas_call(
        body, name=name, grid=(N_DEV, k // tk),
        in_specs=[pl.BlockSpec((m, tk), lambda j, i: (0, i)), pl.BlockSpec((m, n), lambda j, i: (0, j))],
        out_specs=pl.BlockSpec((None, tk, n), lambda j, i: (j, i, 0)),
        out_shape=jax.ShapeDtypeStruct((N_DEV, k, n), BF16),
        compiler_params=_cparams(dimension_semantics=("parallel", "parallel")),
    )(x, dy)


def _rowwise(name, fn, rows, params, out_rows, out_accs=(), tr=256):
    rows = [a if isinstance(a, tuple) else (a, a.shape[1], 0) for a in rows]
    r = rows[0][0].shape[0]
    in_specs = [pl.BlockSpec((tr, wd), lambda i, cb=cb: (i, cb)) for _, wd, cb in rows]
    rows = [a for a, _, _ in rows]
    in_specs += [pl.BlockSpec(p.shape, lambda i, nd=p.ndim: (0,) * nd) for p in params]
    out_shape = [jax.ShapeDtypeStruct((r, c), dt) for c, dt in out_rows]
    out_shape += [jax.ShapeDtypeStruct(s, F32) for s in out_accs]
    out_specs = [pl.BlockSpec((tr, c), lambda i: (i, 0)) for c, _ in out_rows]
    out_specs += [pl.BlockSpec(s, lambda i, nd=len(s): (0,) * nd) for s in out_accs]
    n_in, n_ro = len(rows) + len(params), len(out_rows)

    def body(*refs):
        res = fn(*[ref[...] for ref in refs[:n_in]])
        outs = refs[n_in:]
        for o, v in zip(outs[:n_ro], res[:n_ro]):
            o[...] = v.astype(o.dtype)
        for o, v in zip(outs[n_ro:], res[n_ro:]):
            _accumulate(o, v)

    return pl.pallas_call(
        body, name=name, grid=(r // tr,), in_specs=in_specs, out_specs=out_specs, out_shape=out_shape,
        compiler_params=_cparams(dimension_semantics=("arbitrary",)),
    )(*rows, *params)


def _pairwise(name, fn, rows, params, n_out, n_acc=0, tr=512):
    t, c = rows[0].shape
    tile = pl.BlockSpec((tr, 2 * HEAD_DIM), lambda p, i: (i, p))
    vec = pl.BlockSpec((1, 2 * HEAD_DIM), lambda p, i: (0, p))
    n_in = len(rows) + len(params)

    def body(*refs):
        res = fn(*[ref[...] for ref in refs[:n_in]])
        outs = refs[n_in:]
        for o, v in zip(outs[:n_out], res[:n_out]):
            o[...] = v
        first = pl.program_id(1) == 0
        for o, v in zip(outs[n_out:], res[n_out:]):
            @pl.when(first)
            def _(o=o, v=v):
                o[...] = v

            @pl.when(jnp.logical_not(first))
            def _(o=o, v=v):
                o[...] += v

    return pl.pallas_call(
        body, name=name, grid=(c // (2 * HEAD_DIM), t // tr),
        in_specs=[tile] * len(rows) + [vec] * len(params), out_specs=[tile] * n_out + [vec] * n_acc,
        out_shape=[jax.ShapeDtypeStruct((t, c), F32)] * n_out + [jax.ShapeDtypeStruct((1, c), F32)] * n_acc,
        compiler_params=_cparams(dimension_semantics=("parallel", "arbitrary")),
    )(*rows, *params)


def _accumulate(o_ref, v):
    @pl.when(pl.program_id(0) == 0)
    def _():
        o_ref[...] = v

    @pl.when(pl.program_id(0) != 0)
    def _():
        o_ref[...] += v


def _rms(x, g):
    return x * lax.rsqrt(jnp.mean(x * x, axis=-1, keepdims=True) + RMS_EPS) * g


def _sigmoid(x):
    return jax.nn.sigmoid(x)


def _softplus(x):
    return jnp.maximum(x, 0.0) + jnp.log1p(jnp.exp(-jnp.abs(x)))


def _norm_fwd(name, x, g):
    return _rowwise(name, lambda xv, gv: (_rms(xv, gv),), [x], [g], [(x.shape[1], BF16)])[0]


def _norm_bwd(name, x, g, dh, dres):
    def fn(xv, dhv, drv, gv):
        _, vjp = jax.vjp(_rms, xv, gv)
        dx, dg = vjp(dhv)
        return dx + drv, dg
    return _rowwise(name, fn, [x, dh, dres], [g], [(x.shape[1], F32)], [g.shape])


def _f_act(gate, up):
    return gate * _sigmoid(gate) * up


def _gate_up_act(name, h, w_gu):
    m, k = h.shape
    tm = _tile(m, 512)

    def body(h_ref, w_ref, gu_ref, a_ref):
        gu = jnp.dot(h_ref[...], w_ref[...], preferred_element_type=F32)
        gu_ref[...] = gu
        a_ref[...] = _f_act(gu[:, :FF_PAD], gu[:, FF_PAD:]).astype(a_ref.dtype)

    return pl.pallas_call(
        body, name=name, grid=(m // tm, N_DEV),
        in_specs=[pl.BlockSpec((tm, k), lambda i, j: (i, 0)),
                  pl.BlockSpec((None, k, 2 * FF_PAD), lambda i, j: (j, 0, 0))],
        out_specs=[pl.BlockSpec((tm, 2 * FF_PAD), lambda i, j: (i, j)), pl.BlockSpec((tm, FF_PAD), lambda i, j: (i, j))],
        out_shape=[jax.ShapeDtypeStruct((m, N_DEV * 2 * FF_PAD), F32), jax.ShapeDtypeStruct((m, FF_HID), BF16)],
        compiler_params=_cparams(dimension_semantics=("parallel", "parallel")),
    )(h, w_gu)


def _gate_up_act_bwd(name, dout, w_down, gu, order):
    m, k = dout.shape
    tm = _tile(m, 512)

    def body(d_ref, w_ref, gu_ref, order_ref, o_ref):
        da = 0.5 * lax.dot_general(d_ref[...].astype(BF16), w_ref[...], (((1,), (1,)), ((), ())),
                                   preferred_element_type=F32)
        guv = gu_ref[...]
        _, vjp = jax.vjp(_f_act, guv[:, :FF_PAD], guv[:, FF_PAD:])
        o_ref[...] = jnp.concatenate(vjp(da), axis=1).astype(o_ref.dtype)

    gu_spec = pl.BlockSpec((tm, 2 * FF_PAD), lambda i, j: (i, j))
    return pl.pallas_call(
        body, name=name, grid=(m // tm, N_DEV),
        in_specs=[pl.BlockSpec((tm, k), lambda i, j: (i, 0)), pl.BlockSpec((FF_PAD, k), lambda i, j: (j, 0)), gu_spec,
                  pl.BlockSpec(memory_space=pl.ANY)],
        out_specs=gu_spec, out_shape=jax.ShapeDtypeStruct((m, N_DEV * 2 * FF_PAD), BF16),
        compiler_params=_cparams(dimension_semantics=("parallel", "parallel")),
    )(dout, w_down, gu, order)


def _ffn_fwd(tag, x, norm, w_gu, w_down):
    h = _norm_fwd(tag + "_norm", x, norm)
    gu, a = _gate_up_act(tag + "_gu", h, w_gu)
    wd = w_down(a)
    out = _mm(tag + "_down", a, wd, "nn", res=x, scale=0.5)
    return out, (h, gu, a), wd


def _ffn_bwd(tag, x, norm, w_gu, w_down, saved, dout, on_down=None):
    h, gu, a = saved
    d_wdown = _mm(tag + "_dwdown", a, dout, "tn", out_dtype=BF16, scale=0.5).reshape(N_DEV, FF_PAD, D_MODEL)
    token = on_down(d_wdown) if on_down is not None else jnp.zeros((8, LANES), F32)
    dgu = _gate_up_act_bwd(tag + "_dgu", dout, w_down, gu, token)
    dh =_mmc_nt(tag + "_dh", dgu, w_gu, 0, 0, 2 * FF_PAD, D_MODEL)
    d_wgu = _mmc_tn(tag + "_dwgu", h, dgu, 2 * FF_PAD)
    dx, dnorm = _norm_bwd(tag + "_dnorm", x, norm, dh, dout)
    return dx, dnorm, d_wgu, d_wdown


def _shift_fwd(z, mu):
    t, c = z.shape[0], RWKV_COLS
    tr = 256

    def body(z_ref, zp_ref, mu_ref, r_ref, k_ref, v_ref, lo_ref, gd_ref):
        zv = z_ref[...]
        prev = zp_ref[7:8, :] * jnp.where(pl.program_id(0) == 0, 0.0, 1.0)
        row = lax.broadcasted_iota(jnp.int32, zv.shape, 0)
        zsh = jnp.where(row == 0, prev, pltpu.roll(zv, 1, 0))
        zs = zv + (zsh - zv) * mu_ref[...]
        r_ref[...] = zs[:, 0:512]
        k_ref[...] = zs[:, 512:1024]
        v_ref[...] = zs[:, 1024:1536]
        lo_ref[...] = zs[:, 1536:1664]
        gd_ref[...] = zs[:, 1664:1792]

    widths = (512, 512, 512, 128, 128)
    return pl.pallas_call(
        body, name="rwkv_shift", grid=(t // tr,),
        in_specs=[pl.BlockSpec((tr, c), lambda i: (i, 0)),
                  pl.BlockSpec((8, c), lambda i: (jnp.maximum(i * (tr // 8) - 1, 0), 0)),
                  pl.BlockSpec((1, c), lambda i: (0, 0))],
        out_specs=[pl.BlockSpec((tr, w), lambda i: (i, 0)) for w in widths],
        out_shape=[jax.ShapeDtypeStruct((t, w), F32) for w in widths],
        compiler_params=_cparams(dimension_semantics=("parallel",)),
    )(z, z, mu)


def _shift_bwd(z, mu, dr, dk, dv, dlo, dgd):
    t, c = z.shape[0], RWKV_COLS
    tr = 256
    nt = t // tr

    def body(z_ref, zp_ref, mu_ref, dr_ref, dk_ref, dv_ref, dlo_ref, dgd_ref,
             drn_ref, dkn_ref, dvn_ref, dlon_ref, dgdn_ref, dz_ref, dmu_ref):
        i = pl.program_id(0)
        zv, muv = z_ref[...], mu_ref[...]
        prev = zp_ref[7:8, :] * jnp.where(i == 0, 0.0, 1.0)
        row = lax.broadcasted_iota(jnp.int32, zv.shape, 0)
        zsh = jnp.where(row == 0, prev, pltpu.roll(zv, 1, 0))
        dzs = jnp.concatenate([dr_ref[...], dk_ref[...], dv_ref[...], dlo_ref[...], dgd_ref[...]], axis=1)
        nxt = jnp.concatenate([drn_ref[0:1, :], dkn_ref[0:1, :], dvn_ref[0:1, :], dlon_ref[0:1, :],
                               dgdn_ref[0:1, :]], axis=1) * jnp.where(i == nt - 1, 0.0, 1.0)
        u = dzs * muv
        un = jnp.where(row == tr - 1, nxt * muv, pltpu.roll(u, tr - 1, 0))
        dz_ref[...] = (dzs - u + un).astype(dz_ref.dtype)
        _accumulate(dmu_ref, jnp.sum(dzs * (zsh - zv), axis=0, keepdims=True))

    widths = (512, 512, 512, 128, 128)
    nxt_map = lambda i: (jnp.minimum((i + 1) * (tr // 8), t // 8 - 1), 0)
    return pl.pallas_call(
        body, name="rwkv_shift_bwd", grid=(nt,),
        in_specs=[pl.BlockSpec((tr, c), lambda i: (i, 0)),
                  pl.BlockSpec((8, c), lambda i: (jnp.maximum(i * (tr // 8) - 1, 0), 0)),
                  pl.BlockSpec((1, c), lambda i: (0, 0))]
        + [pl.BlockSpec((tr, w), lambda i: (i, 0)) for w in widths]
        + [pl.BlockSpec((8, w), nxt_map) for w in widths],
        out_specs=[pl.BlockSpec((tr, c), lambda i: (i, 0)), pl.BlockSpec((1, c), lambda i: (0, 0))],
        out_shape=[jax.ShapeDtypeStruct((t, c), BF16), jax.ShapeDtypeStruct((1, c), F32)],
        compiler_params=_cparams(dimension_semantics=("arbitrary",)),
    )(z, z, mu, dr, dk, dv, dlo, dgd, dr, dk, dv, dlo, dgd)


def _f_pre(k, lo, gd, w0, w2p, a0, a2p, g2, k_k, k_a):
    lane = lax.broadcasted_iota(jnp.int32, lo.shape, 1)
    lo_act = jnp.where(lane < DECAY_LORA, jnp.tanh(lo), lo)
    w = -_softplus(-(w0 + _bdot(lo_act, w2p))) - 0.5
    a = _sigmoid(a0 + _bdot(lo_act, a2p))
    g = _bdot(_sigmoid(gd), g2)
    kk = k * k_k
    k2 = k * (1.0 + (a - 1.0) * k_a)
    decay = jnp.exp(-jnp.exp(w))
    return k2, kk, a, decay, g


def _f_kk(kk, a):
    kkn = kk * lax.rsqrt(jnp.maximum(_head_sums(kk * kk), 1e-24))
    return kkn, kkn * a


def _f_post(y, r, k2, v, g, gn_w, gn_b, r_k):
    mean = _head_sums(y) * (1.0 / HEAD_DIM)
    var = _head_sums(jnp.square(y - mean)) * (1.0 / HEAD_DIM)
    yn = (y - mean) * lax.rsqrt(var + GN_EPS) * gn_w + gn_b
    bonus = _head_sums(r * k2 * r_k) * v
    return (yn + bonus) * g


def _to_v_rows(x):
    t = x.shape[0]
    return x.reshape(t, RWKV_HEADS, V_HI, V_LO).transpose(0, 2, 3, 1).reshape(t, V_HI, LANES)


def _from_v_rows(x):
    t = x.shape[0]
    return x.reshape(t, V_HI, V_LO, RWKV_HEADS).transpose(0, 3, 1, 2).reshape(t, RWKV_DIM)


def _k_cols(x):
    return jnp.tile(x, (V_LO, 1)).T


def _k_rows(x):
    xt = x.T
    out = xt[0:RWKV_HEADS]
    for l in range(1, V_LO):
        out = out + xt[l * RWKV_HEADS:(l + 1) * RWKV_HEADS]
    return out


def _wkv_fwd(r, w, k, kk, b, v):
    t = r.shape[0]
    tc = SCAN_CHUNK
    key_spec = pl.BlockSpec((tc, RWKV_HEADS, HEAD_DIM), lambda i: (i, 0, 0))
    row_spec = pl.BlockSpec((tc, V_HI, LANES), lambda i: (i, 0, 0))

    def body(r_ref, w_ref, k_ref, kk_ref, b_ref, v_ref, y_ref, st_ref, s_scr, cols_a, cols_b):
        @pl.when(pl.program_id(0) == 0)
        def _():
            s_scr[...] = jnp.zeros_like(s_scr)

        def prep(ti, buf):
            for n, ref in enumerate((r_ref, w_ref, k_ref, kk_ref, b_ref)):
                buf[n] = _k_cols(ref[ti])

        def step(ti, s, cur, nxt, ti_next):
            rc, wc, kc, kkc, bc = (cur[n] for n in range(5))
            prep(ti_next, nxt)
            vt = v_ref[ti]
            new, ys = [], []
            for j in range(V_HI):
                sa = -jnp.sum(s[j] * kkc, axis=0, keepdims=True)
                nj = s[j] * wc + bc * sa + kc * vt[j:j + 1]
                st_ref[ti, j] = nj
                ys.append(jnp.sum(nj * rc, axis=0, keepdims=True))
                new.append(nj)
            y_ref[ti] = jnp.concatenate(ys, axis=0)
            return tuple(new)

        def pair(i, s):
            s = step(2 * i, s, cols_a, cols_b, 2 * i + 1)
            return step(2 * i + 1, s, cols_b, cols_a, jnp.minimum(2 * i + 2, tc - 1))

        prep(0, cols_a)
        s = lax.fori_loop(0, tc // 2, pair, tuple(s_scr[j] for j in range(V_HI)))
        for j in range(V_HI):
            s_scr[j] = s[j]

    return pl.pallas_call(
        body, name="wkv_fwd", grid=(t // tc,),
        in_specs=[key_spec] * 5 + [row_spec],
        out_specs=[row_spec, pl.BlockSpec((tc, V_HI, HEAD_DIM, LANES), lambda i: (i, 0, 0, 0))],
        out_shape=[jax.ShapeDtypeStruct((t, V_HI, LANES), F32),
                   jax.ShapeDtypeStruct((t, V_HI, HEAD_DIM, LANES), F32)],
        scratch_shapes=[pltpu.VMEM((V_HI, HEAD_DIM, LANES), F32)] + [pltpu.VMEM((5, HEAD_DIM, LANES), F32)] * 2,
        compiler_params=_cparams(dimension_semantics=("arbitrary",)),
    )(r, w, k, kk, b, v)


def _wkv_bwd(r, w, k, kk, b, v, states, dy):
    t = r.shape[0]
    tc = SCAN_CHUNK
    nb = t // tc
    key_spec = pl.BlockSpec((tc, RWKV_HEADS, HEAD_DIM), lambda i: (nb - 1 - i, 0, 0))
    row_spec = pl.BlockSpec((tc, V_HI, LANES), lambda i: (nb - 1 - i, 0, 0))
    st_spec = pl.BlockSpec((tc, V_HI, HEAD_DIM, LANES), lambda i: (nb - 1 - i, 0, 0, 0))
    stp_spec = pl.BlockSpec((1, V_HI, HEAD_DIM, LANES), lambda i: (jnp.maximum((nb - 1 - i) * tc - 1, 0), 0, 0, 0))

    def body(r_ref, w_ref, k_ref, kk_ref, b_ref, v_ref, st_ref, stp_ref, dy_ref,
             dr_ref, dw_ref, dk_ref, dkk_ref, db_ref, dv_ref, ds_scr, cols_a, cols_b, accs_a, accs_b):
        @pl.when(pl.program_id(0) == 0)
        def _():
            ds_scr[...] = jnp.zeros_like(ds_scr)

        def colsum(x):
            return jnp.sum(x, axis=0, keepdims=True)

        def prep(ti, buf):
            for n, ref in enumerate((r_ref, w_ref, k_ref, kk_ref, b_ref)):
                buf[n] = _k_cols(ref[ti])

        def flush(ti, buf):
            for n, ref in enumerate((dr_ref, dk_ref, db_ref, dw_ref, dkk_ref)):
                ref[ti] = _k_rows(buf[n])

        def step(ti, ds, sp, cur, accs):
            rc, wc, kc, kkc, bc = (cur[n] for n in range(5))
            vt, dyt = v_ref[ti], dy_ref[ti]
            acc = None
            new, dvs = [], []
            for j in range(V_HI):
                st = st_ref[ti, j]
                dsj = ds[j] + rc * dyt[j:j + 1]
                sa = -colsum(sp[j] * kkc)
                dsa = colsum(dsj * bc)
                dvs.append(colsum(dsj * kc))
                parts = (st * dyt[j:j + 1], dsj * vt[j:j + 1], dsj * sa, dsj * sp[j], -(sp[j] * dsa))
                acc = parts if acc is None else tuple(a + q for a, q in zip(acc, parts))
                new.append(dsj * wc - kkc * dsa)
            dv_ref[ti] = jnp.concatenate(dvs, axis=0)
            for n in range(5):
                accs[n] = acc[n]
            return tuple(new)

        def states_before(ti):
            return tuple(st_ref[ti - 1, j] for j in range(V_HI))

        def pair(i, ds):
            ta = tc - 1 - 2 * i
            prep(ta - 1, cols_b)
            flush(jnp.minimum(ta + 1, tc - 1), accs_b)
            ds = step(ta, ds, states_before(ta), cols_a, accs_a)
            prep(ta - 2, cols_a)
            flush(ta, accs_a)
            return step(ta - 1, ds, states_before(ta - 1), cols_b, accs_b)

        prep(tc - 1, cols_a)
        accs_b[...] = jnp.zeros_like(accs_b)
        ds = lax.fori_loop(0, tc // 2 - 1, pair, tuple(ds_scr[j] for j in range(V_HI)))
        prep(0, cols_b)
        flush(2, accs_b)
        ds = step(1, ds, states_before(1), cols_a, accs_a)
        flush(1, accs_a)
        keep = jnp.where(pl.program_id(0) == nb - 1, 0.0, 1.0)
        ds = step(0, ds, tuple(stp_ref[0, j] * keep for j in range(V_HI)), cols_b, accs_b)
        flush(0, accs_b)
        for j in range(V_HI):
            ds_scr[j] = ds[j]

    key_out = jax.ShapeDtypeStruct((t, RWKV_HEADS, HEAD_DIM), F32)
    return pl.pallas_call(
        body, name="wkv_bwd", grid=(nb,),
        in_specs=[key_spec] * 5 + [row_spec, st_spec, stp_spec, row_spec],
        out_specs=[key_spec] * 5 + [row_spec],
        out_shape=[key_out] * 5 + [jax.ShapeDtypeStruct((t, V_HI, LANES), F32)],
        scratch_shapes=[pltpu.VMEM((V_HI, HEAD_DIM, LANES), F32)] + [pltpu.VMEM((5, HEAD_DIM, LANES), F32)] * 4,
        compiler_params=_cparams(dimension_semantics=("arbitrary",)),
    )(r, w, k, kk, b, v, states, states, dy)


PAIR = 2 * HEAD_DIM
N_PAIRS = ATTN_HEADS // 2
Q_COL0 = RWKV_COLS // PAIR
K_COL0 = Q_COL0 + N_PAIRS
V_COL0 = K_COL0 + N_PAIRS


def _swap_halves(x):
    lane = lax.broadcasted_iota(jnp.int32, x.shape, 1)
    return jnp.where((lane & (HEAD_DIM - 1)) < HEAD_DIM // 2, pltpu.roll(x, PAIR - HEAD_DIM // 2, 1),
                     pltpu.roll(x, HEAD_DIM // 2, 1))


@jax.custom_vjp
def _rope(x, cosf, sinf):
    return x * cosf + _swap_halves(x) * sinf


def _rope_fwd(x, cosf, sinf):
    return _rope(x, cosf, sinf), (cosf, sinf)


def _rope_bwd(res, d):
    cosf, sinf = res
    return d * cosf + _swap_halves(d * sinf), jnp.zeros_like(cosf), jnp.zeros_like(sinf)


_rope.defvjp(_rope_fwd, _rope_bwd)


def _head_sums(x):
    lane = lax.broadcasted_iota(jnp.int32, x.shape, 1)
    lo = jnp.where(lane < HEAD_DIM, 1.0, 0.0)
    hi = 1.0 - lo
    return lo * jnp.sum(x * lo, axis=1, keepdims=True) + hi * jnp.sum(x * hi, axis=1, keepdims=True)


def _f_qk(x, cosf, sinf, gain2):
    xn = x * lax.rsqrt(_head_sums(x * x) * (1.0 / HEAD_DIM) + RMS_EPS) * gain2
    return _rope(xn, cosf, sinf)


def _qk_prep(z, tab, q_gain, k_gain):
    t = z.shape[0]
    tr = 1024

    def body(z_ref, c_ref, s_ref, qg_ref, kg_ref, o_ref):
        g = jnp.where(pl.program_id(0) < N_PAIRS, qg_ref[...], kg_ref[...])
        o_ref[...] = _f_qk(z_ref[...], c_ref[...], s_ref[...], jnp.concatenate([g, g], axis=1))

    gain = pl.BlockSpec((1, HEAD_DIM), lambda c, i: (0, 0))
    return pl.pallas_call(
        body, name="qk_prep", grid=(2 * N_PAIRS, t // tr),
        in_specs=[pl.BlockSpec((tr, PAIR), lambda c, i: (i, Q_COL0 + c)), pl.BlockSpec((tr, PAIR), lambda c, i: (i, 0)),
                  pl.BlockSpec((tr, PAIR), lambda c, i: (i, 1)), gain, gain],
        out_specs=pl.BlockSpec((tr, PAIR), lambda c, i: (i, c)),
        out_shape=jax.ShapeDtypeStruct((t, 2 * N_PAIRS * PAIR), F32),
        compiler_params=_cparams(dimension_semantics=("parallel", "parallel")),
    )(z, tab, tab, q_gain, k_gain)


def _qk_prep_bwd(z, tab, q_gain, k_gain, dq, dk):
    t = z.shape[0]
    tr = 1024

    def body(z_ref, c_ref, s_ref, qg_ref, kg_ref, dq_ref, dk_ref, dz_ref, dqg_ref, dkg_ref):
        c, i = pl.program_id(0), pl.program_id(1)
        is_q = c < N_PAIRS
        g = jnp.where(is_q, qg_ref[...], kg_ref[...])
        d = jnp.where(is_q, dq_ref[...], dk_ref[...])
        _, vjp = jax.vjp(lambda xx, gg: _f_qk(xx, c_ref[...], s_ref[...], gg), z_ref[...],
                         jnp.concatenate([g, g], axis=1))
        dx, dg2 = vjp(d)
        dz_ref[...] = dx.astype(dz_ref.dtype)
        dg = dg2[:, :HEAD_DIM] + dg2[:, HEAD_DIM:]
        first_q = jnp.logical_and(c == 0, i == 0)
        first_k = jnp.logical_and(c == N_PAIRS, i == 0)

        @pl.when(first_q)
        def _():
            dqg_ref[...] = dg

        @pl.when(jnp.logical_and(is_q, jnp.logical_not(first_q)))
        def _():
            dqg_ref[...] += dg

        @pl.when(first_k)
        def _():
            dkg_ref[...] = dg

        @pl.when(jnp.logical_and(jnp.logical_not(is_q), jnp.logical_not(first_k)))
        def _():
            dkg_ref[...] += dg

    gain = pl.BlockSpec((1, HEAD_DIM), lambda c, i: (0, 0))
    return pl.pallas_call(
        body, name="qk_prep_bwd", grid=(2 * N_PAIRS, t // tr),
        in_specs=[pl.BlockSpec((tr, PAIR), lambda c, i: (i, Q_COL0 + c)), pl.BlockSpec((tr, PAIR), lambda c, i: (i, 0)),
                  pl.BlockSpec((tr, PAIR), lambda c, i: (i, 1)), gain, gain,
                  pl.BlockSpec((tr, PAIR), lambda c, i: (i, jnp.minimum(c, N_PAIRS - 1))),
                  pl.BlockSpec((tr, PAIR), lambda c, i: (i, jnp.maximum(c - N_PAIRS, 0)))],
        out_specs=[pl.BlockSpec((tr, PAIR), lambda c, i: (i, c)), gain, gain],
        out_shape=[jax.ShapeDtypeStruct((t, 2 * N_PAIRS * PAIR), BF16), jax.ShapeDtypeStruct((1, HEAD_DIM), F32),
                   jax.ShapeDtypeStruct((1, HEAD_DIM), F32)],
        compiler_params=_cparams(dimension_semantics=("arbitrary", "arbitrary")),
    )(z, tab, tab, q_gain, k_gain, dq, dk)


def _attn_block(q, kp, kc, vp, vc, kmin):
    k2 = jnp.concatenate([kp, kc], axis=0)
    v2 = jnp.concatenate([vp, vc], axis=0)
    s = _bdot_nt(q, k2) * (HEAD_DIM ** -0.5)
    qi = lax.broadcasted_iota(jnp.int32, s.shape, 0)
    kj = lax.broadcasted_iota(jnp.int32, s.shape, 1)
    dist = qi + BAND_BLOCK - kj
    valid = (dist >= 0) & (dist <= BAND_BLOCK) & (kj >= kmin)
    s = jnp.where(valid, s, NEG_INF)
    m = lax.stop_gradient(jnp.max(s, axis=-1, keepdims=True))
    e = jnp.exp(s - m)
    l = jnp.sum(e, axis=-1, keepdims=True)
    o = _bdot(e, v2) / l
    return o, m + jnp.log(l)


def _fold(src_ref, dst_ref, dil):
    t = src_ref.shape[0]
    ln = t // dil
    for j in range(dil):
        dst_ref[j * ln:(j + 1) * ln, :] = src_ref[pl.ds(j, ln, stride=dil), :]


def _unfold(src_ref, dst_ref, dil):
    t = src_ref.shape[0]
    ln = t // dil
    for j in range(dil):
        dst_ref[pl.ds(j, ln, stride=dil), :] = src_ref[j * ln:(j + 1) * ln, :]


def _per_group(fn):
    pair = pl.program_id(0)
    for gi, (_, dil) in enumerate(ATTN_GROUPS):
        @pl.when(jnp.logical_or(pair == 2 * gi, pair == 2 * gi + 1))
        def _(dil=dil):
            fn(dil)


def _block_rows(idx, blocks_per_seq):
    first = (idx & (blocks_per_seq - 1)) == 0
    cur = pl.ds(pl.multiple_of(idx * BAND_BLOCK, BAND_BLOCK), BAND_BLOCK)
    prev = pl.ds(pl.multiple_of(jnp.maximum(idx - 1, 0) * BAND_BLOCK, BAND_BLOCK), BAND_BLOCK)
    return first, cur, prev


def _heads(x):
    return x[:, :HEAD_DIM], x[:, HEAD_DIM:]


def _attn_fwd(qk, z):
    t = z.shape[0]
    n_blocks = t // BAND_BLOCK

    def body(q_ref, k_ref, v_ref, o_ref, lse_ref, qf, kf, vf, of, lf):
        def run(dil):
            _fold(q_ref, qf, dil)
            _fold(k_ref, kf, dil)
            _fold(v_ref, vf, dil)
            blocks_per_seq = n_blocks // dil

            def block(idx, carry):
                first, cur, prev = _block_rows(idx, blocks_per_seq)
                kmin = jnp.where(first, BAND_BLOCK, 0)
                outs, lses = [], []
                for q, kp, kc, vp, vc in zip(_heads(qf[cur, :]), _heads(kf[prev, :]), _heads(kf[cur, :]),
                                             _heads(vf[prev, :]), _heads(vf[cur, :])):
                    o, ls = _attn_block(q, kp, kc, vp, vc, kmin)
                    outs.append(o)
                    lses.append(jnp.broadcast_to(ls, o.shape))
                of[cur, :] = jnp.concatenate(outs, axis=1)
                lf[cur, :] = jnp.concatenate(lses, axis=1)
                return carry

            lax.fori_loop(0, n_blocks, block, 0)
            _unfold(of, o_ref, dil)
            _unfold(lf, lse_ref, dil)

        _per_group(run)

    slab = jax.ShapeDtypeStruct((t, N_PAIRS * PAIR), F32)
    out_spec = pl.BlockSpec((t, PAIR), lambda p: (0, p))
    return pl.pallas_call(
        body, name="attn_fwd", grid=(N_PAIRS,),
        in_specs=[pl.BlockSpec((t, PAIR), lambda p: (0, p)), pl.BlockSpec((t, PAIR), lambda p: (0, N_PAIRS + p)),
                  pl.BlockSpec((t, PAIR), lambda p: (0, V_COL0 + p))],
        out_specs=[out_spec, out_spec], out_shape=[slab, slab],
        scratch_shapes=[pltpu.VMEM((t, PAIR), F32)] * 5,
        compiler_params=_cparams(dimension_semantics=("parallel",)),
    )(qk, qk, z)


def _attn_bwd(qk, z, do, dlse):
    t = z.shape[0]
    n_blocks = t // BAND_BLOCK

    def body(q_ref, k_ref, v_ref, do_ref, dl_ref, dq_ref, dk_ref, dv_ref, qf, kf, vf, dof, dlf, dqf, dkf, dvf):
        def run(dil):
            for src, dst in ((q_ref, qf), (k_ref, kf), (v_ref, vf), (do_ref, dof), (dl_ref, dlf)):
                _fold(src, dst, dil)
            blocks_per_seq = n_blocks // dil

            def block(idx, carry):
                first, cur, prev = _block_rows(idx, blocks_per_seq)
                kmin = jnp.where(first, BAND_BLOCK, 0)
                grads = []
                for q, kp, kc, vp, vc, do_h, dl_h in zip(
                        _heads(qf[cur, :]), _heads(kf[prev, :]), _heads(kf[cur, :]), _heads(vf[prev, :]),
                        _heads(vf[cur, :]), _heads(dof[cur, :]), _heads(dlf[cur, :])):
                    _, vjp = jax.vjp(functools.partial(_attn_block, kmin=kmin), q, kp, kc, vp, vc)
                    grads.append(vjp((do_h, jnp.sum(dl_h, axis=1, keepdims=True))))
                dq, dkp, dkc, dvp, dvc = (jnp.concatenate([a, b], axis=1) for a, b in zip(*grads))
                dqf[cur, :] = dq
                dkf[cur, :] = dkc
                dvf[cur, :] = dvc

                @pl.when(jnp.logical_not(first))
                def _():
                    dkf[prev, :] += dkp
                    dvf[prev, :] += dvp

                return carry

            lax.fori_loop(0, n_blocks, block, 0)
            _unfold(dqf, dq_ref, dil)
            _unfold(dkf, dk_ref, dil)
            _unfold(dvf, dv_ref, dil)

        _per_group(run)

    slab = jax.ShapeDtypeStruct((t, N_PAIRS * PAIR), F32)
    own = pl.BlockSpec((t, PAIR), lambda p: (0, p))
    return pl.pallas_call(
        body, name="attn_bwd", grid=(N_PAIRS,),
        in_specs=[own, pl.BlockSpec((t, PAIR), lambda p: (0, N_PAIRS + p)),
                  pl.BlockSpec((t, PAIR), lambda p: (0, V_COL0 + p)), own, own],
        out_specs=[own] * 3, out_shape=[slab] * 3,
        scratch_shapes=[pltpu.VMEM((t, PAIR), F32)] * 8,
        compiler_params=_cparams(dimension_semantics=("parallel",)),
    )(qk, qk, z, do, dlse)


def _f_comb(o1, o2, o3, l1, l2, l3):
    m = jnp.maximum(jnp.maximum(l1, l2), l3)
    e1, e2, e3 = jnp.exp(l1 - m), jnp.exp(l2 - m), jnp.exp(l3 - m)
    den = e1 + e2 + e3
    return (e1 / den) * o1 + (e2 / den) * o2 + (e3 / den) * o3


def _all_gather_hbm(name, arrs):
    na = len(arrs)

    def body(*refs):
        x_refs, out_refs = refs[:na], refs[na:2 * na]
        send_sems, recv_sems, local_sems = refs[2 * na:]
        mx, my, mc = lax.axis_index("x"), lax.axis_index("y"), lax.axis_index("c")
        me, sibling = (mx, my, mc), (mx, my, 1 - mc)
        chips = [(1 - mx, my), (mx, 1 - my), (1 - mx, 1 - my)]

        def slot(a, px, py, pc):
            return out_refs[a].at[4 * px + 2 * py + pc]

        def copy(a, k, block, to, src=None):
            return pltpu.make_async_remote_copy(
                src_ref=slot(a, *block) if src is None else src, dst_ref=slot(a, *block),
                send_sem=send_sems.at[a, k], recv_sem=recv_sems.at[a, k], device_id=to, device_id_type=MESH)

        mine = [pltpu.make_async_copy(x_refs[a], slot(a, *me), local_sems.at[a]) for a in range(na)]
        for cp in mine:
            cp.start()
        first = []
        for a in range(na):
            first.append(copy(a, 0, me, sibling, src=x_refs[a]))
            first += [copy(a, 1 + j, me, (*chip, mc), src=x_refs[a]) for j, chip in enumerate(chips)]
        for cp in first:
            cp.start()
        passed = []
        for j, chip in enumerate(chips):
            for a in range(na):
                copy(a, 1 + j, (*chip, mc), me).wait_recv()
                passed.append(copy(a, 4 + j, (*chip, mc), sibling))
                passed[-1].start()
        for a in range(na):
            copy(a, 0, sibling, me).wait_recv()
            for j, chip in enumerate(chips):
                copy(a, 4 + j, (*chip, 1 - mc), me).wait_recv()
        for cp in first + passed:
            cp.wait_send()
        for cp in mine:
            cp.wait()

    hbm = pl.BlockSpec(memory_space=pl.ANY)
    return pl.pallas_call(
        body, name=name,
        out_shape=[jax.ShapeDtypeStruct((N_DEV,) + a.shape, a.dtype) for a in arrs],
        in_specs=[hbm] * na, out_specs=[hbm] * na,
        scratch_shapes=[pltpu.SemaphoreType.DMA((na, 7)), pltpu.SemaphoreType.DMA((na, 7)),
                        pltpu.SemaphoreType.DMA((na,))],
    )(*arrs)


def _all_gather_vmem(x):
    rws, cols = x.shape

    def body(x_ref, out_ref, send_sems, recv_sems):
        mx, my, mc = lax.axis_index("x"), lax.axis_index("y"), lax.axis_index("c")
        me, sibling = (mx, my, mc), (mx, my, 1 - mc)
        chips = [(1 - mx, my), (mx, 1 - my), (1 - mx, 1 - my)]

        def slot(px, py, pc):
            return out_ref.at[4 * px + 2 * py + pc]

        def copy(k, block, to, src=None):
            return pltpu.make_async_remote_copy(
                src_ref=slot(*block) if src is None else src, dst_ref=slot(*block),
                send_sem=send_sems.at[k], recv_sem=recv_sems.at[k], device_id=to, device_id_type=MESH)

        first = [copy(0, me, sibling, src=x_ref)]
        first += [copy(1 + j, me, (*chip, mc), src=x_ref) for j, chip in enumerate(chips)]
        for cp in first:
            cp.start()
        out_ref[4 * mx + 2 * my + mc] = x_ref[...]
        passed = [copy(4 + j, (*chip, mc), sibling) for j, chip in enumerate(chips)]
        for j, chip in enumerate(chips):
            copy(1 + j, (*chip, mc), me).wait_recv()
            passed[j].start()
        copy(0, sibling, me).wait_recv()
        for j, chip in enumerate(chips):
            copy(4 + j, (*chip, 1 - mc), me).wait_recv()
        for cp in first + passed:
            cp.wait_send()

    return pl.pallas_call(
        body, name="all_gather_small",
        out_shape=jax.ShapeDtypeStruct((N_DEV, rws, cols), x.dtype),
        in_specs=[pl.BlockSpec(memory_space=pltpu.VMEM)], out_specs=pl.BlockSpec(memory_space=pltpu.VMEM),
        scratch_shapes=[pltpu.SemaphoreType.DMA((7,)), pltpu.SemaphoreType.DMA((7,))],
    )(x)


def _all_to_all_hbm(name, arrs):
    na = len(arrs)

    def body(*refs):
        g_refs, out_refs = refs[:na], refs[na:2 * na]
        send_sems, recv_sems, local_sems = refs[2 * na:]
        mx, my, mc = lax.axis_index("x"), lax.axis_index("y"), lax.axis_index("c")
        me = 4 * mx + 2 * my + mc
        mine = [pltpu.make_async_copy(g_refs[a].at[me], out_refs[a].at[me], local_sems.at[a]) for a in range(na)]
        for cp in mine:
            cp.start()
        copies = []
        for k in range(1, N_DEV):
            px, py, pc = mx ^ (k >> 2), my ^ ((k >> 1) & 1), mc ^ (k & 1)
            peer = 4 * px + 2 * py + pc
            for a in range(na):
                copies.append(pltpu.make_async_remote_copy(
                    src_ref=g_refs[a].at[peer], dst_ref=out_refs[a].at[me], send_sem=send_sems.at[a, k - 1],
                    recv_sem=recv_sems.at[a, k - 1], device_id=(px, py, pc), device_id_type=MESH))
        for cp in copies:
            cp.start()
        for cp in copies:
            cp.wait_recv()
        for cp in copies:
            cp.wait_send()
        for cp in mine:
            cp.wait()

    hbm = pl.BlockSpec(memory_space=pl.ANY)
    return pl.pallas_call(
        body, name=name,
        out_shape=[jax.ShapeDtypeStruct(a.shape, a.dtype) for a in arrs],
        in_specs=[hbm] * na, out_specs=[hbm] * na,
        scratch_shapes=[pltpu.SemaphoreType.DMA((na, 7)), pltpu.SemaphoreType.DMA((na, 7)),
                        pltpu.SemaphoreType.DMA((na,))],
    )(*arrs)


def _scatter_copies(g_refs, land_refs, send_sems, recv_sems):
    mx, my, mc = lax.axis_index("x"), lax.axis_index("y"), lax.axis_index("c")
    me = 4 * mx + 2 * my + mc
    copies = []
    for k in range(1, N_DEV):
        px, py, pc = mx ^ (k >> 2), my ^ ((k >> 1) & 1), mc ^ (k & 1)
        peer = 4 * px + 2 * py + pc
        for a, (g_ref, land_ref) in enumerate(zip(g_refs, land_refs)):
            copies.append(pltpu.make_async_remote_copy(
                src_ref=g_ref.at[peer], dst_ref=land_ref.at[me], send_sem=send_sems.at[a * (N_DEV - 1) + k - 1],
                recv_sem=recv_sems.at[a * (N_DEV - 1) + k - 1], device_id=(px, py, pc), device_id_type=MESH))
    return copies


_HBM = pl.BlockSpec(memory_space=pltpu.HBM)
_SEM = pl.BlockSpec(memory_space=pltpu.SEMAPHORE)
_DATAFLOW = pltpu.SideEffectType.DATAFLOW_SIDE_EFFECTING


def _gather_copies(x_refs, land_refs, send_sems, recv_sems):
    mx, my, mc = lax.axis_index("x"), lax.axis_index("y"), lax.axis_index("c")
    me = 4 * mx + 2 * my + mc
    copies = []
    for k in range(1, N_DEV):
        px, py, pc = mx ^ (k >> 2), my ^ ((k >> 1) & 1), mc ^ (k & 1)
        for a, (x_ref, land_ref) in enumerate(zip(x_refs, land_refs)):
            copies.append(pltpu.make_async_remote_copy(
                src_ref=x_ref, dst_ref=land_ref.at[me], send_sem=send_sems.at[a * (N_DEV - 1) + k - 1],
                recv_sem=recv_sems.at[a * (N_DEV - 1) + k - 1], device_id=(px, py, pc), device_id_type=MESH))
    return copies


N_CHIPS = N_DEV // 2


def _pair_swap(name, arrs):
    na = len(arrs)

    def body(*refs):
        g_refs, out_refs = refs[:na], refs[na:2 * na]
        send_sems, recv_sems = refs[2 * na:]
        mx, my, mc = lax.axis_index("x"), lax.axis_index("y"), lax.axis_index("c")
        copies = [pltpu.make_async_remote_copy(
            src_ref=g_refs[a].at[2 * q + 1 - mc], dst_ref=out_refs[a].at[q], send_sem=send_sems.at[a, q],
            recv_sem=recv_sems.at[a, q], device_id=(mx, my, 1 - mc), device_id_type=MESH)
            for a in range(na) for q in range(N_CHIPS)]
        for cp in copies:
            cp.start()
        for cp in copies:
            cp.wait_recv()
        for cp in copies:
            cp.wait_send()

    hbm = pl.BlockSpec(memory_space=pl.ANY)
    return pl.pallas_call(
        body, name=name,
        out_shape=[jax.ShapeDtypeStruct((N_CHIPS,) + a.shape[1:], a.dtype) for a in arrs],
        in_specs=[hbm] * na, out_specs=[hbm] * na,
        scratch_shapes=[pltpu.SemaphoreType.DMA((na, N_CHIPS)), pltpu.SemaphoreType.DMA((na, N_CHIPS))],
    )(*arrs)


def _pair_add(name, g, half, core):
    _, r, c = g.shape
    tr = _tile(r, 512) if r % LANES == 0 else r

    def body(core_ref, g_ref, h_ref, o_ref):
        o_ref[...] = (g_ref[...].astype(F32) + h_ref[...].astype(F32)).astype(o_ref.dtype)

    return pl.pallas_call(
        body, name=name,
        grid_spec=pltpu.PrefetchScalarGridSpec(
            num_scalar_prefetch=1, grid=(N_CHIPS, r // tr),
            in_specs=[pl.BlockSpec((None, tr, c), lambda q, i, core_ref: (2 * q + core_ref[0], i, 0)),
                      pl.BlockSpec((None, tr, c), lambda q, i, core_ref: (q, i, 0))],
            out_specs=pl.BlockSpec((None, tr, c), lambda q, i, core_ref: (q, i, 0))),
        out_shape=jax.ShapeDtypeStruct((N_CHIPS, r, c), g.dtype),
        compiler_params=_cparams(dimension_semantics=("parallel", "parallel")),
    )(core, g, half)


def _chip_copies(h_refs, land_refs, send_sems, recv_sems):
    mx, my, mc = lax.axis_index("x"), lax.axis_index("y"), lax.axis_index("c")
    my_chip = 2 * mx + my
    copies = []
    for k in range(1, N_CHIPS):
        px, py = mx ^ (k >> 1), my ^ (k & 1)
        for a, (h_ref, land_ref) in enumerate(zip(h_refs, land_refs)):
            copies.append(pltpu.make_async_remote_copy(
                src_ref=h_ref.at[2 * px + py], dst_ref=land_ref.at[my_chip], send_sem=send_sems.at[a * (N_DEV - 1) + k - 1],
                recv_sem=recv_sems.at[a * (N_DEV - 1) + k - 1], device_id=(px, py, mc), device_id_type=MESH))
    return copies


def _exchange_start(name, copies, srcs, lands, after):
    na = len(srcs)

    def body(*refs):
        for cp in copies(refs[:na], refs[na:2 * na], refs[2 * na + 1], refs[2 * na + 2]):
            cp.start()
        refs[-1][...] = jnp.zeros_like(refs[-1])

    in_hbm = lambda a: pltpu.with_memory_space_constraint(a, pltpu.HBM)
    outs = pl.pallas_call(
        body, name=name,
        out_shape=(pltpu.SemaphoreType.DMA((na * (N_DEV - 1),)), pltpu.SemaphoreType.DMA((na * (N_DEV - 1),)),
                   *[pltpu.HBM(a.shape, a.dtype) for a in list(srcs) + list(lands)],
                   jax.ShapeDtypeStruct((8, LANES), F32)),
        in_specs=[_HBM] * (2 * na) + [pl.BlockSpec(memory_space=pl.ANY)],
        out_specs=(_SEM, _SEM, *[_HBM] * (2 * na), pl.BlockSpec(memory_space=pltpu.VMEM)),
        input_output_aliases={i: 2 + i for i in range(2 * na)},
        compiler_params=pltpu.CompilerParams(has_side_effects=_DATAFLOW),
    )(*[in_hbm(a) for a in srcs], *[in_hbm(a) for a in lands], after)
    return outs[0], outs[1], outs[2:2 + na], outs[2 + na:2 + 2 * na], outs[-1]


def _exchange_wait(name, copies, send_sems, recv_sems, srcs, lands, after):
    na = len(srcs)

    def body(*refs):
        for cp in copies(refs[:na], refs[na:2 * na], refs[2 * na], refs[2 * na + 1]):
            cp.wait_send()
            cp.wait_recv()

    outs = pl.pallas_call(
        body, name=name,
        out_shape=[pltpu.HBM(a.shape, a.dtype) for a in list(srcs) + list(lands)],
        in_specs=[_HBM] * (2 * na) + [_SEM, _SEM, pl.BlockSpec(memory_space=pl.ANY)], out_specs=[_HBM] * (2 * na),
        input_output_aliases={i: i for i in range(2 * na)},
        compiler_params=pltpu.CompilerParams(has_side_effects=_DATAFLOW),
    )(*srcs, *lands, send_sems, recv_sems, after)
    return outs[:na], outs[na:]


def _sum_slots(name, g, tr):
    _, rws, cols = g.shape

    def body(g_ref, o_ref):
        acc = g_ref[0].astype(F32)
        for j in range(1, N_DEV):
            acc = acc + g_ref[j].astype(F32)
        o_ref[...] = acc

    return pl.pallas_call(
        body, name=name, grid=(rws // tr,),
        in_specs=[pl.BlockSpec((N_DEV, tr, cols), lambda i: (0, i, 0))],
        out_specs=pl.BlockSpec((tr, cols), lambda i: (i, 0)),
        out_shape=jax.ShapeDtypeStruct((rws, cols), F32),
        compiler_params=_cparams(dimension_semantics=("parallel",)),
    )(g)


def _adam_math(wv, gv, mv, vv):
    mn = ADAM_B1 * mv + (1.0 - ADAM_B1) * gv
    vn = ADAM_B2 * vv + (1.0 - ADAM_B2) * jnp.square(gv)
    m_hat = mn / (1.0 - ADAM_B1 ** ADAM_STEP)
    v_hat = vn / (1.0 - ADAM_B2 ** ADAM_STEP)
    delta = -ADAM_LR * (m_hat / (jnp.sqrt(v_hat) + ADAM_EPS) + ADAM_WD * wv)
    return delta, mn, vn


def _adamw(name, w, g, m, v, tr):
    return _rowwise(name, _adam_math, [w, g, m, v], [], [(LANES, F32)] * 3, tr=tr)


def _adamw_slots(name, recv, own, rb, cb, cw, w, m, v, tr, order):
    nr, nc = w.shape
    n = recv.shape[0]

    def body(g_ref, own_ref, w_ref, m_ref, v_ref, order_ref, go_ref, d_ref, mo_ref, vo_ref):
        me = 2 * lax.axis_index("x") + lax.axis_index("y")
        if n == N_DEV:
            me = 2 * me + lax.axis_index("c")
        acc = None
        for s in range(n):
            part = jnp.where(me == s, own_ref[s], g_ref[s]).astype(F32)
            acc = part if acc is None else acc + part
        g = acc[:, :nc]
        go_ref[...] = g
        d_ref[...], mo_ref[...], vo_ref[...] = _adam_math(w_ref[...], g, m_ref[...], v_ref[...])

    nat = pl.BlockSpec((tr, nc), lambda i: (i, 0))
    slots = pl.BlockSpec((n, tr, cw), lambda i: (0, rb + i, cb))
    return pl.pallas_call(
        body, name=name, grid=(nr // tr,),
        in_specs=[slots, slots, nat, nat, nat, pl.BlockSpec(memory_space=pl.ANY)],
        out_specs=[nat] * 4, out_shape=[jax.ShapeDtypeStruct((nr, nc), F32)] * 4,
        compiler_params=_cparams(dimension_semantics=("parallel",)),
    )(recv, own, w, m, v, order)


def _local_blocks(w):
    pad_cols = lambda a: jnp.pad(a, ((0, 0), (0, FF_PAD - FF_SHARD)))
    pad_rows = lambda a: jnp.pad(a, ((0, FF_PAD - FF_SHARD), (0, 0)))
    gate_up = lambda tag: jnp.concatenate([pad_cols(w[tag + "_w_gate"]), pad_cols(w[tag + "_w_up"])], axis=1)
    blocks = {
        "ffn1_gu": gate_up("ffn1"), "ffn1_d": pad_rows(w["ffn1_w_down"]), "w_in": w["w_in"],
        "lora": jnp.concatenate([w["rwkv_w2"], w["rwkv_a2"], w["rwkv_g2"]], axis=0),
        "br": jnp.concatenate([w["w_br_rwkv"], w["w_br_attn"], w["ple_w_proj"]], axis=0),
        "w_out": w["w_out"], "ffn2_gu": gate_up("ffn2"), "ffn2_d": pad_rows(w["ffn2_w_down"]),
        "ple_gate": w["ple_w_gate"],
    }
    return {n: a.astype(BF16) for n, a in blocks.items()}


GATHER_GROUPS = {"head": ("ffn1_gu", "ffn1_d"), "mid": ("w_in", "lora"),
                 "rest": ("br", "w_out", "ffn2_gu", "ffn2_d", "ple_gate")}

SCATTER_GROUPS = {"tail": ("ple_gate", "ple_proj", "ffn2_gu", "ffn2_d"), "branch": ("w_out", "br"),
                  "mixer": ("lora", "w_in"),
                  "ffn1_down": ("ffn1_d",), "head": ("ffn1_gu",)}

ADAM_PLAN = (
    ("ffn1_w_gate", "ffn1_gu", 0, 0, FF_PAD, 256), ("ffn1_w_up", "ffn1_gu", 0, 1, FF_PAD, 256),
    ("ffn1_w_down", "ffn1_d", 0, 0, D_MODEL, FF_SHARD // 2), ("w_in", "w_in", 0, 0, IN_SHARD, 256),
    ("rwkv_w2", "lora", 0, 0, HEAD_DIM, 64), ("rwkv_a2", "lora", 1, 0, HEAD_DIM, 64),
    ("rwkv_g2", "lora", 2, 0, HEAD_DIM, 64),
    ("w_br_rwkv", "br", 0, 0, OUT_SHARD, 256), ("w_br_attn", "br", 2, 0, OUT_SHARD, 256),
    ("ple_w_proj", "ple_proj", 0, 0, OUT_SHARD, 256), ("w_out", "w_out", 0, 0, D_MODEL, OUT_SHARD),
    ("ffn2_w_gate", "ffn2_gu", 0, 0, FF_PAD, 256), ("ffn2_w_up", "ffn2_gu", 0, 1, FF_PAD, 256),
    ("ffn2_w_down", "ffn2_d", 0, 0, D_MODEL, FF_SHARD // 2), ("ple_w_gate", "ple_gate", 0, 0, D_MODEL, OUT_SHARD),
)


def _pack_small(arrs, rows):
    flat = jnp.concatenate([a.reshape(-1) for a in arrs])
    return jnp.pad(flat, (0, rows * LANES - flat.shape[0])).reshape(rows, LANES)


def _unpack_small(flat, like):
    flat = flat.reshape(-1)
    out, off = [], 0
    for a in like:
        out.append(flat[off:off + a.size].reshape(a.shape))
        off += a.size
    return out


def _local_step(x, p, pos, target, sm, wg, fetch, on_grads):
    t = x.shape[0]

    w_d1 = wg["ffn1_d"].reshape(FF_HID, D_MODEL)
    x1, ffn1_saved, _ = _ffn_fwd("ffn1", x, sm["ffn1_norm"], wg["ffn1_gu"], lambda after: w_d1)
    wg = {**wg, **fetch("mid", x1)}
    full_cols = lambda blk: blk.transpose(1, 0, 2).reshape(blk.shape[1], N_DEV * blk.shape[2])
    lora_w2 = full_cols(wg["lora"][:, :DECAY_LORA])
    lora_a2 = full_cols(wg["lora"][:, DECAY_LORA:DECAY_LORA + ICLR_LORA])
    lora_g2 = full_cols(wg["lora"][:, DECAY_LORA + ICLR_LORA:])
    h2 = _norm_fwd("mix_norm", x1, sm["mix_norm"])
    z = _mmc_nn("w_in", h2, wg["w_in"], 0, 0, IN_SHARD)
    z_g = (z, 2 * D_MODEL, (RWKV_COLS + 3 * ATTN_DIM) // (2 * D_MODEL))

    r, k, v, lo, gd = _shift_fwd(z, sm["rwkv_mu"])
    zero_lo = jnp.zeros((DECAY_LORA, RWKV_DIM), BF16)
    w2p = jnp.concatenate([lora_w2, zero_lo], axis=0).astype(F32)
    a2p = jnp.concatenate([zero_lo, lora_a2], axis=0).astype(F32)
    pre_params = [sm["rwkv_w0"], w2p, sm["rwkv_a0"], a2p, lora_g2.astype(F32), sm["rwkv_k_k"], sm["rwkv_k_a"]]
    wide = [(RWKV_DIM, F32)]
    k2, kk, a, decay, g = _rowwise("rwkv_pre", _f_pre, [k, lo, gd], pre_params, wide * 5)
    kkn, b = _pairwise("rwkv_kk", _f_kk, [kk, a], [], 2)
    scan_in = [u.reshape(t, RWKV_HEADS, HEAD_DIM) for u in (r, decay, k2, kkn, b)]
    v_rows = _to_v_rows(v)
    y_rows, states = _wkv_fwd(*scan_in, v_rows)
    y = _from_v_rows(y_rows)
    post_params = [sm["rwkv_gn_w"], sm["rwkv_gn_b"], sm["rwkv_r_k"]]
    post_rows = [y, r, k2, v, g]
    y_rwkv = _pairwise("rwkv_post", lambda *av: (_f_post(*av),), post_rows, post_params, 1)[0]

    inv_freq = 1.0 / (ROPE_THETA ** (jnp.arange(0, HEAD_DIM, 2, dtype=F32) / HEAD_DIM))
    freq2 = jnp.tile(inv_freq, 2 * PAIR // HEAD_DIM).reshape(1, PAIR)
    half = jnp.ones((HEAD_DIM // 2,), F32)
    sign2 = jnp.tile(jnp.concatenate([-half, half]), PAIR // HEAD_DIM).reshape(1, PAIR)

    def rope_table(posv, fr, sg):
        ang = posv * fr
        return (jnp.concatenate([jnp.cos(ang), jnp.sin(ang) * sg], axis=1),)
    tab = _rowwise("rope_table", rope_table, [pos.astype(F32).reshape(t, 1)], [freq2, sign2], [(2 * PAIR, F32)])[0]
    qk = _qk_prep(z, tab, sm["q_norm"], sm["k_norm"])
    o_all, lse_all = _attn_fwd(qk, z)
    gw = HEADS_PER_GROUP * HEAD_DIM

    def by_group(ov, lv):
        return [ov[:, i * gw:(i + 1) * gw] for i in range(3)] + [lv[:, i * gw:(i + 1) * gw] for i in range(3)]
    y_attn = _rowwise("attn_comb", lambda ov, lv: (_f_comb(*by_group(ov, lv)),), [o_all, lse_all], [], [(gw, F32)])[0]

    wg = {**wg, **fetch("rest", y_rwkv)}
    w_d2 = wg["ffn2_d"].reshape(FF_HID, D_MODEL)
    w_out = wg["w_out"].reshape(D_MODEL, D_MODEL)
    w_pg = wg["ple_gate"].reshape(D_MODEL, D_MODEL)
    w_brr = full_cols(wg["br"][:, :RWKV_DIM])
    w_bra = full_cols(wg["br"][:, RWKV_DIM:RWKV_DIM + gw])
    w_pp = full_cols(wg["br"][:, RWKV_DIM + gw:])
    u_r = _mm("br_rwkv", y_rwkv, w_brr, "nn")
    u_a = _mm("br_attn", y_attn, w_bra, "nn")

    def f_merge(zgr, zga, ur, ua):
        return _sigmoid(zgr) * ur + _sigmoid(zga) * ua
    merged = _rowwise("merge", lambda zg, ur, ua: (f_merge(zg[:, :D_MODEL], zg[:, D_MODEL:], ur, ua),),
                      [z_g, u_r, u_a], [], [(D_MODEL, BF16)])[0]
    x2 = _mm("w_out", merged, w_out, "nn", res=x1)
    x3, ffn2_saved, _ = _ffn_fwd("ffn2", x2, sm["ffn2_norm"], wg["ffn2_gu"], lambda after: w_d2)

    hn = _norm_fwd("ple_norm", x3, sm["ple_norm"])
    gz = _mm("ple_gate", hn, w_pg, "nn")
    pp = _mm("ple_proj", p, w_pp, "nn")

    def f_head(x3v, gzv, ppv, tg):
        sg = _sigmoid(gzv)
        err = x3v + sg * ppv - tg
        part = 0.5 * jnp.sum(jnp.mean(err * err, axis=-1, keepdims=True))
        dx4 = err * (1.0 / D_MODEL)
        return dx4, dx4 * ppv * sg * (1.0 - sg), dx4 * sg, jnp.full((1, LANES), part, F32)
    dx4, dgz, dpp, loss_row = _rowwise("ple_loss", f_head, [x3, gz, pp, target], [],
                                       [(D_MODEL, F32), (D_MODEL, BF16), (D_MODEL, BF16)], [(1, LANES)])
    loss = loss_row[0, 0]

    gs, gm = {}, {}
    row_blocks = lambda g: g.reshape(N_DEV, g.shape[0] // N_DEV, g.shape[1])
    dhn = _mm("ple_dhn", dgz, w_pg, "nt")
    gm["ple_gate"] = row_blocks(_mm("ple_dwgate", hn, dgz, "tn", out_dtype=BF16))
    col_blocks = lambda g: g.reshape(g.shape[0], N_DEV, g.shape[1] // N_DEV).transpose(1, 0, 2)
    gm["ple_proj"] = col_blocks(_mm("ple_dwproj", p, dpp, "tn", out_dtype=BF16))
    dx3, gs["ple_norm"] = _norm_bwd("ple_dnorm", x3, sm["ple_norm"], dhn, dx4)

    dx2, gs["ffn2_norm"], gm["ffn2_gu"], gm["ffn2_d"] = _ffn_bwd(
        "ffn2", x2, sm["ffn2_norm"], wg["ffn2_gu"], w_d2, ffn2_saved, dx3)
    tail_token = on_grads("tail", {n: gm.pop(n) for n in SCATTER_GROUPS["tail"]})

    dmerged = _mm("w_out_dmerged", dx2, w_out, "nt")
    gm["w_out"] = row_blocks(_mm("w_out_dw", merged, dx2, "tn", out_dtype=BF16))

    def merge_bwd(zg, ur, ua, dm):
        _, vjp = jax.vjp(f_merge, zg[:, :D_MODEL], zg[:, D_MODEL:], ur, ua)
        dzr, dza, dur, dua = vjp(dm)
        return jnp.concatenate([dzr, dza], axis=1), dur, dua
    dz_g, du_r, du_a = _rowwise("merge_bwd", merge_bwd, [z_g, u_r, u_a, dmerged], [],
                                [(2 * D_MODEL, BF16), (D_MODEL, BF16), (D_MODEL, BF16)])
    dy_rwkv = _mm("br_rwkv_dy", du_r, w_brr, "nt")
    dy_attn = _mm("br_attn_dy", du_a, w_bra, "nt")
    gm["br"] = jnp.concatenate([col_blocks(_mm("br_rwkv_dw", y_rwkv, du_r, "tn", out_dtype=BF16)),
                                col_blocks(_mm("br_attn_dw", y_attn, du_a, "tn", out_dtype=BF16))], axis=1)
    branch_token = on_grads("branch", {n: gm.pop(n) for n in SCATTER_GROUPS["branch"]})

    def comb_bwd(ov, lv, dyv):
        _, vjp = jax.vjp(_f_comb, *by_group(ov, lv))
        d = vjp(dyv)
        return jnp.concatenate(d[:3], axis=1), jnp.concatenate(d[3:], axis=1)
    do_all, dl_all = _rowwise("attn_comb_bwd", comb_bwd, [o_all, lse_all, dy_attn], [],
                              [(ATTN_DIM, F32), (ATTN_DIM, F32)])
    dq_all, dk_all, dv_all = _attn_bwd(qk, z, do_all, dl_all)
    dqk_raw, gs["q_norm"], gs["k_norm"] = _qk_prep_bwd(z, tab, sm["q_norm"], sm["k_norm"], dq_all, dk_all)

    def post_bwd(yv, rv, k2v, vv, gv, dv_, gnw, gnb, rk):
        _, vjp = jax.vjp(_f_post, yv, rv, k2v, vv, gv, gnw, gnb, rk)
        return vjp(dv_)
    dy, dr1, dk2a, dv1, dg, d_gnw, d_gnb, d_rk = _pairwise(
        "rwkv_post_bwd", post_bwd, post_rows + [dy_rwkv], [post_params[0] + tail_token[0, 0] + branch_token[0, 0]] + post_params[1:], 5, 3)
    gs["rwkv_gn_w"], gs["rwkv_gn_b"], gs["rwkv_r_k"] = d_gnw, d_gnb, d_rk
    dr2, ddecay, dk2b, dkkn, db, dv_rows = _wkv_bwd(*scan_in, v_rows, states, _to_v_rows(dy))
    dr2, ddecay, dk2b, dkkn, db = [u.reshape(t, RWKV_DIM) for u in (dr2, ddecay, dk2b, dkkn, db)]
    dv2 = _from_v_rows(dv_rows)

    def kk_bwd(kkv, av, dkknv, dbv, dra, drb, dva, dvb):
        _, vjp = jax.vjp(_f_kk, kkv, av)
        return (*vjp((dkknv, dbv)), dra + drb, dva + dvb)
    dkk, da, dr, dv = _pairwise("rwkv_kk_bwd", kk_bwd, [kk, a, dkkn, db, dr1, dr2, dv1, dv2], [], 4)

    def pre_bwd(kv, lov, gdv, dk2x, dk2y, dkkv, dav, ddec, dgv, w0, w2p_, a0, a2p_, g2, k_k, k_a):
        _, vjp = jax.vjp(_f_pre, kv, lov, gdv, w0, w2p_, a0, a2p_, g2, k_k, k_a)
        return vjp((dk2x + dk2y, dkkv, dav, ddec, dgv))
    lora_acc = (DECAY_LORA + ICLR_LORA, RWKV_DIM)
    dk, dlo, dgd, d_w0, d_w2p, d_a0, d_a2p, d_g2, d_kk, d_ka = _rowwise(
        "rwkv_pre_bwd", pre_bwd,
        [k, lo, gd, dk2a, dk2b, dkk, da, ddecay, dg],
        pre_params, [(RWKV_DIM, F32), (LANES, F32), (LANES, F32)],
        [(1, RWKV_DIM), lora_acc, (1, RWKV_DIM), lora_acc, (GATE_LORA, RWKV_DIM), (1, RWKV_DIM), (1, RWKV_DIM)])
    gs["rwkv_w0"], gs["rwkv_a0"], gs["rwkv_k_k"], gs["rwkv_k_a"] = d_w0, d_a0, d_kk, d_ka
    gm["lora"] = jnp.concatenate([col_blocks(d_w2p[:DECAY_LORA]), col_blocks(d_a2p[DECAY_LORA:]), col_blocks(d_g2)],
                                 axis=1).astype(BF16)
    dz_r, gs["rwkv_mu"] = _shift_bwd(z, sm["rwkv_mu"], dr, dk, dv, dlo, dgd)

    dz = jnp.concatenate([dz_r, dqk_raw, dv_all.astype(BF16), dz_g], axis=1)
    dh2 = _mmc_nt("w_in_dh", dz, wg["w_in"], 0, 0, IN_SHARD, D_MODEL)
    gm["w_in"] = _mmc_tn("w_in_dw", h2, dz, IN_SHARD)
    mixer_token = on_grads("mixer", {n: gm.pop(n) for n in SCATTER_GROUPS["mixer"]})
    dx1, gs["mix_norm"] = _norm_bwd("mix_dnorm", x1, sm["mix_norm"] + mixer_token[0, 0], dh2, dx2)

    dx0, gs["ffn1_norm"], gm["ffn1_gu"], _ = _ffn_bwd(
        "ffn1", x, sm["ffn1_norm"], wg["ffn1_gu"], w_d1, ffn1_saved, dx1,
        on_down=lambda blocks: on_grads("ffn1_down", {"ffn1_d": blocks}))
    return loss, dx0, gm, gs


def kernel(x, p, positions, ffn1_norm, ffn1_w_gate, ffn1_w_up, ffn1_w_down, mix_norm, w_in, rwkv_mu, rwkv_w0, rwkv_w2, rwkv_a0, rwkv_a2, rwkv_g2, rwkv_k_k, rwkv_k_a, rwkv_r_k, rwkv_gn_w, rwkv_gn_b, q_norm, k_norm, w_br_rwkv, w_br_attn, w_out, ffn2_norm, ffn2_w_gate, ffn2_w_up, ffn2_w_down, ple_norm, ple_w_gate, ple_w_proj, loss_target, m_ffn1_norm, m_ffn1_w_gate, m_ffn1_w_up, m_ffn1_w_down, m_mix_norm, m_w_in, m_rwkv_mu, m_rwkv_w0, m_rwkv_w2, m_rwkv_a0, m_rwkv_a2, m_rwkv_g2, m_rwkv_k_k, m_rwkv_k_a, m_rwkv_r_k, m_rwkv_gn_w, m_rwkv_gn_b, m_q_norm, m_k_norm, m_w_br_rwkv, m_w_br_attn, m_w_out, m_ffn2_norm, m_ffn2_w_gate, m_ffn2_w_up, m_ffn2_w_down, m_ple_norm, m_ple_w_gate, m_ple_w_proj, v_ffn1_norm, v_ffn1_w_gate, v_ffn1_w_up, v_ffn1_w_down, v_mix_norm, v_w_in, v_rwkv_mu, v_rwkv_w0, v_rwkv_w2, v_rwkv_a0, v_rwkv_a2, v_rwkv_g2, v_rwkv_k_k, v_rwkv_k_a, v_rwkv_r_k, v_rwkv_gn_w, v_rwkv_gn_b, v_q_norm, v_k_norm, v_w_br_rwkv, v_w_br_attn, v_w_out, v_ffn2_norm, v_ffn2_w_gate, v_ffn2_w_up, v_ffn2_w_down, v_ple_norm, v_ple_w_gate, v_ple_w_proj):
    args = locals()
    w = {n: args[n][0] for n in WEIGHTS}
    m = {n: args["m_" + n][0] for n in WEIGHTS}
    v = {n: args["v_" + n][0] for n in WEIGHTS}

    w_loc = _local_blocks(w)
    head = GATHER_GROUPS["head"]
    wg = dict(zip(head, _all_gather_hbm("gather_head", [w_loc[n] for n in head])))
    me = 4 * lax.axis_index("x") + 2 * lax.axis_index("y") + lax.axis_index("c")
    gathering, order_after = {}, wg[head[0]]
    for group in ("mid", "rest"):
        shards = [w_loc[n] for n in GATHER_GROUPS[group]]
        zones = [lax.dynamic_update_slice(lax.empty((N_DEV,) + a.shape, a.dtype), a[None], (me, 0, 0)) for a in shards]
        *gathering[group], order_after = _exchange_start("gather_start_" + group, _gather_copies, shards, zones,
                                                         order_after)

    def fetch(group, after):
        _, got = _exchange_wait("gather_wait_" + group, _gather_copies, *gathering[group], after)
        return dict(zip(GATHER_GROUPS[group], got))

    sm = {n: w[n].reshape(1, -1) for n in SMALL}
    sm["ffn1_norm"] = sm["ffn1_norm"] + order_after[0, 0]
    in_flight = {}

    def scatter_early(group, arrays):
        arrs = [arrays[n] for n in SCATTER_GROUPS[group]]
        *in_flight[group], token = _exchange_start("scatter_start_" + group, _scatter_copies, arrs,
                                                   [lax.empty(a.shape, a.dtype) for a in arrs], arrs[0])
        return token
    loss_part, dx, gm, gs = _local_step(x[0], p[0, 0], positions[0], loss_target[0], sm, wg, fetch, scatter_early)
    loss = lax.psum(loss_part, ("x", "y", "c"))
    head_names = SCATTER_GROUPS["head"]
    core = lax.axis_index("c").astype(jnp.int32).reshape(1)
    halves = _pair_swap("scatter_head_pair", [gm[n] for n in head_names])
    chip_sums = [_pair_add("scatter_head_add_" + n, gm[n], hf, core) for n, hf in zip(head_names, halves)]
    *in_flight["head"], head_token = _exchange_start(
        "scatter_start_head", _chip_copies, chip_sums, [lax.empty(a.shape, a.dtype) for a in chip_sums], halves[0])
    recv, own = {}, {}

    def arrived(group, after):
        copies = _chip_copies if group == "head" else _scatter_copies
        sent, lands = _exchange_wait("scatter_wait_" + group, copies, *in_flight[group], after)
        own.update(zip(SCATTER_GROUPS[group], sent))
        recv.update(zip(SCATTER_GROUPS[group], lands))
    for group in ("tail", "branch", "mixer", "ffn1_down"):
        arrived(group, head_token)
    small_like = [w[n] for n in SMALL]
    small_rows = 80
    gs_all = _all_gather_vmem(_pack_small([gs[n] for n in SMALL], small_rows))
    gs_sum = _sum_slots("sum_small_grads", gs_all, small_rows)

    res = {}
    early = [e for e in ADAM_PLAN if e[1] not in SCATTER_GROUPS["head"]]
    late = [e for e in ADAM_PLAN if e[1] in SCATTER_GROUPS["head"]]
    for n, src, rb, cb, cw, tr in early + late:
        if (n, src, rb, cb, cw, tr) == late[0]:
            arrived("head", res["delta", early[-1][0]])
        outs4 = _adamw_slots("adamw_" + n, recv[src], own[src], rb, cb, cw, w[n], m[n], v[n], tr, head_token)
        for tag, a in zip(("grad", "delta", "new_m", "new_v"), outs4):
            res[tag, n] = a[None]
    d_s, m_s, v_s = _adamw("adamw_small", _pack_small(small_like, small_rows), gs_sum,
                           _pack_small([m[n] for n in SMALL], small_rows),
                           _pack_small([v[n] for n in SMALL], small_rows), small_rows)
    for tag, small in (("grad", gs_sum), ("delta", d_s), ("new_m", m_s), ("new_v", v_s)):
        for n, a in zip(SMALL, _unpack_small(small, small_like)):
            res[tag, n] = a[None]
    outs = [loss, dx[None]]
    for tag in ("grad", "delta", "new_m", "new_v"):
        outs += [res[tag, n] for n in WEIGHTS]
    return tuple(outs)
```

```python
import functools

import jax
import jax.numpy as jnp
from jax import lax
from jax.experimental import pallas as pl
from jax.experimental.pallas import tpu as pltpu

F32, BF16 = jnp.float32, jnp.bfloat16
MESH = pl.DeviceIdType.MESH
N_DEV = 8
LANES = 128
VMEM_LIMIT = 56 * 1024 * 1024

D_MODEL = 1024
PLE_DIM = 256
HEAD_DIM = 64
RWKV_HEADS = 8
RWKV_DIM = RWKV_HEADS * HEAD_DIM
DECAY_LORA = 64
ICLR_LORA = 64
GATE_LORA = 128
GN_EPS = 64e-5
ATTN_GROUPS = ((128, 1), (512, 4), (2048, 16))
HEADS_PER_GROUP = 4
ATTN_HEADS = HEADS_PER_GROUP * len(ATTN_GROUPS)
ATTN_DIM = ATTN_HEADS * HEAD_DIM
BAND_BLOCK = 128
ROPE_THETA = 10000.0
NEG_INF = -1e30
D_FF = 2816
RMS_EPS = 1e-6
RWKV_COLS = 3 * RWKV_DIM + DECAY_LORA + ICLR_LORA + GATE_LORA
ADAM_LR = 0.001
ADAM_B1 = 0.9
ADAM_B2 = 0.999
ADAM_EPS = 1e-08
ADAM_WD = 0.01
ADAM_STEP = 10

V_LO = LANES // RWKV_HEADS
V_HI = HEAD_DIM // V_LO
SCAN_CHUNK = 64
MM_ROWS = 2048

FF_SHARD = D_FF // N_DEV
FF_PAD = -(-FF_SHARD // LANES) * LANES
FF_HID = N_DEV * FF_PAD
IN_SHARD = 6144 // N_DEV
OUT_SHARD = D_MODEL // N_DEV

SMALL = ("ffn1_norm", "mix_norm", "rwkv_mu", "rwkv_w0", "rwkv_a0", "rwkv_k_k", "rwkv_k_a", "rwkv_r_k",
         "rwkv_gn_w", "rwkv_gn_b", "q_norm", "k_norm", "ffn2_norm", "ple_norm")
WEIGHTS = ("ffn1_norm", "ffn1_w_gate", "ffn1_w_up", "ffn1_w_down", "mix_norm", "w_in", "rwkv_mu", "rwkv_w0",
           "rwkv_w2", "rwkv_a0", "rwkv_a2", "rwkv_g2", "rwkv_k_k", "rwkv_k_a", "rwkv_r_k", "rwkv_gn_w",
           "rwkv_gn_b", "q_norm", "k_norm", "w_br_rwkv", "w_br_attn", "w_out", "ffn2_norm", "ffn2_w_gate",
           "ffn2_w_up", "ffn2_w_down", "ple_norm", "ple_w_gate", "ple_w_proj")


def _cparams(**kw):
    return pltpu.CompilerParams(vmem_limit_bytes=VMEM_LIMIT, **kw)


def _tile(n, cap):
    best = None
    for t in range(LANES, min(n, cap) + 1, LANES):
        if n % t == 0:
            best = t
    return best if best is not None else n


@jax.custom_vjp
def _bdot(a, w):
    return jnp.dot(a.astype(BF16), w.astype(BF16), preferred_element_type=F32)


def _bdot_fwd(a, w):
    return _bdot(a, w), (a, w)


def _bdot_bwd(res, g):
    a, w = res
    gb = g.astype(BF16)
    da = lax.dot_general(gb, w.astype(BF16), (((1,), (1,)), ((), ())), preferred_element_type=F32)
    dw = lax.dot_general(a.astype(BF16), gb, (((0,), (0,)), ((), ())), preferred_element_type=F32)
    return da.astype(a.dtype), dw.astype(w.dtype)


_bdot.defvjp(_bdot_fwd, _bdot_bwd)


@jax.custom_vjp
def _bdot_nt(a, b):
    return lax.dot_general(a.astype(BF16), b.astype(BF16), (((1,), (1,)), ((), ())), preferred_element_type=F32)


def _bdot_nt_fwd(a, b):
    return _bdot_nt(a, b), (a, b)


def _bdot_nt_bwd(res, g):
    a, b = res
    gb = g.astype(BF16)
    da = jnp.dot(gb, b.astype(BF16), preferred_element_type=F32)
    db = lax.dot_general(gb, a.astype(BF16), (((0,), (0,)), ((), ())), preferred_element_type=F32)
    return da.astype(a.dtype), db.astype(b.dtype)


_bdot_nt.defvjp(_bdot_nt_fwd, _bdot_nt_bwd)


def _mm(name, a, b, mode, out_dtype=F32, res=None, scale=None):
    if mode == "nn":
        (m, k), n = a.shape, b.shape[1]
    elif mode == "nt":
        (m, k), n = a.shape, b.shape[0]
    else:
        (k, m), n = a.shape, b.shape[1]
    tm, tn = _tile(m, 512 if mode == "tn" else MM_ROWS), _tile(n, 512)
    a_spec = pl.BlockSpec((k, tm), lambda i, j: (0, i)) if mode == "tn" else pl.BlockSpec((tm, k), lambda i, j: (i, 0))
    b_spec = pl.BlockSpec((tn, k), lambda i, j: (j, 0)) if mode == "nt" else pl.BlockSpec((k, tn), lambda i, j: (0, j))
    dims = {"nn": ((1,), (0,)), "nt": ((1,), (1,)), "tn": ((0,), (0,))}[mode]
    o_spec = pl.BlockSpec((tm, tn), lambda i, j: (i, j))
    ins, in_specs = [a, b], [a_spec, b_spec]
    if res is not None:
        ins.append(res)
        in_specs.append(o_spec)

    def body(*refs):
        acc = lax.dot_general(refs[0][...].astype(BF16), refs[1][...].astype(BF16), (dims, ((), ())),
                              preferred_element_type=F32)
        if scale is not None:
            acc = acc * scale
        if res is not None:
            acc = acc + refs[2][...].astype(F32)
        refs[-1][...] = acc.astype(refs[-1].dtype)

    return pl.pallas_call(
        body, name=name, grid=(m // tm, n // tn), in_specs=in_specs, out_specs=o_spec,
        out_shape=jax.ShapeDtypeStruct((m, n), out_dtype),
        compiler_params=_cparams(dimension_semantics=("parallel", "parallel")),
    )(*ins)


def _mmc_nn(name, a, wb, ki, ci, n, out_dtype=F32):
    m, k = a.shape
    tm = _tile(m, MM_ROWS)

    def body(a_ref, w_ref, o_ref):
        o_ref[...] = jnp.dot(a_ref[...].astype(BF16), w_ref[...], preferred_element_type=F32).astype(o_ref.dtype)

    return pl.pallas_call(
        body, name=name, grid=(m // tm, N_DEV),
        in_specs=[pl.BlockSpec((tm, k), lambda i, j: (i, 0)), pl.BlockSpec((None, k, n), lambda i, j: (j, ki, ci))],
        out_specs=pl.BlockSpec((tm, n), lambda i, j: (i, j)),
        out_shape=jax.ShapeDtypeStruct((m, N_DEV * n), out_dtype),
        compiler_params=_cparams(dimension_semantics=("parallel", "parallel")),
    )(a, wb)


def _mmc_nt(name, a, wb, ki, ci, n, k, res=None):
    m = a.shape[0]
    tm = _tile(m, MM_ROWS)
    o_spec = pl.BlockSpec((tm, k), lambda i, j: (i, 0))
    ins = [a, wb] + ([res] if res is not None else [])
    in_specs = [pl.BlockSpec((tm, n), lambda i, j: (i, j)), pl.BlockSpec((None, k, n), lambda i, j: (j, ki, ci))]
    in_specs += [o_spec] if res is not None else []

    def body(*refs):
        a_ref, w_ref, o_ref = refs[0], refs[1], refs[-1]
        acc = lax.dot_general(a_ref[...].astype(BF16), w_ref[...], (((1,), (1,)), ((), ())),
                              preferred_element_type=F32)

        @pl.when(pl.program_id(1) == 0)
        def _():
            o_ref[...] = acc + refs[2][...] if res is not None else acc

        @pl.when(pl.program_id(1) != 0)
        def _():
            o_ref[...] += acc

    return pl.pallas_call(
        body, name=name, grid=(m // tm, N_DEV), in_specs=in_specs, out_specs=o_spec,
        out_shape=jax.ShapeDtypeStruct((m, k), F32),
        compiler_params=_cparams(dimension_semantics=("parallel", "arbitrary")),
    )(*ins)


def _mmc_tn(name, x, dy, n):
    m, k = x.shape
    tk = _tile(k, 1024)

    def body(x_ref, dy_ref, o_ref):
        o_ref[...] = lax.dot_general(x_ref[...].astype(BF16), dy_ref[...].astype(BF16), (((0,), (0,)), ((), ())),
                                     preferred_element_type=F32).astype(o_ref.dtype)

    return pl.pallas_call(
        body, name=name, grid=(N_DEV, k // tk),
        in_specs=[pl.BlockSpec((m, tk), lambda j, i: (0, i)), pl.BlockSpec((m, n), lambda j, i: (0, j))],
        out_specs=pl.BlockSpec((None, tk, n), lambda j, i: (j, i, 0)),
        out_shape=jax.ShapeDtypeStruct((N_DEV, k, n), BF16),
        compiler_params=_cparams(dimension_semantics=("parallel", "parallel")),
    )(x, dy)


def _rowwise(name, fn, rows, params, out_rows, out_accs=(), tr=256):
    rows = [a if isinstance(a, tuple) else (a, a.shape[1], 0) for a in rows]
    r = rows[0][0].shape[0]
    in_specs = [pl.BlockSpec((tr, wd), lambda i, cb=cb: (i, cb)) for _, wd, cb in rows]
    rows = [a for a, _, _ in rows]
    in_specs += [pl.BlockSpec(p.shape, lambda i, nd=p.ndim: (0,) * nd) for p in params]
    out_shape = [jax.ShapeDtypeStruct((r, c), dt) for c, dt in out_rows]
    out_shape += [jax.ShapeDtypeStruct(s, F32) for s in out_accs]
    out_specs = [pl.BlockSpec((tr, c), lambda i: (i, 0)) for c, _ in out_rows]
    out_specs += [pl.BlockSpec(s, lambda i, nd=len(s): (0,) * nd) for s in out_accs]
    n_in, n_ro = len(rows) + len(params), len(out_rows)

    def body(*refs):
        res = fn(*[ref[...] for ref in refs[:n_in]])
        outs = refs[n_in:]
        for o, v in zip(outs[:n_ro], res[:n_ro]):
            o[...] = v.astype(o.dtype)
        for o, v in zip(outs[n_ro:], res[n_ro:]):
            _accumulate(o, v)

    return pl.pallas_call(
        body, name=name, grid=(r // tr,), in_specs=in_specs, out_specs=out_specs, out_shape=out_shape,
        compiler_params=_cparams(dimension_semantics=("arbitrary",)),
    )(*rows, *params)


def _pairwise(name, fn, rows, params, n_out, n_acc=0, tr=512):
    t, c = rows[0].shape
    tile = pl.BlockSpec((tr, 2 * HEAD_DIM), lambda p, i: (i, p))
    vec = pl.BlockSpec((1, 2 * HEAD_DIM), lambda p, i: (0, p))
    n_in = len(rows) + len(params)

    def body(*refs):
        res = fn(*[ref[...] for ref in refs[:n_in]])
        outs = refs[n_in:]
        for o, v in zip(outs[:n_out], res[:n_out]):
            o[...] = v
        first = pl.program_id(1) == 0
        for o, v in zip(outs[n_out:], res[n_out:]):
            @pl.when(first)
            def _(o=o, v=v):
                o[...] = v

            @pl.when(jnp.logical_not(first))
            def _(o=o, v=v):
                o[...] += v

    return pl.pallas_call(
        body, name=name, grid=(c // (2 * HEAD_DIM), t // tr),
        in_specs=[tile] * len(rows) + [vec] * len(params), out_specs=[tile] * n_out + [vec] * n_acc,
        out_shape=[jax.ShapeDtypeStruct((t, c), F32)] * n_out + [jax.ShapeDtypeStruct((1, c), F32)] * n_acc,
        compiler_params=_cparams(dimension_semantics=("parallel", "arbitrary")),
    )(*rows, *params)


def _accumulate(o_ref, v):
    @pl.when(pl.program_id(0) == 0)
    def _():
        o_ref[...] = v

    @pl.when(pl.program_id(0) != 0)
    def _():
        o_ref[...] += v


def _rms(x, g):
    return x * lax.rsqrt(jnp.mean(x * x, axis=-1, keepdims=True) + RMS_EPS) * g


def _sigmoid(x):
    return jax.nn.sigmoid(x)


def _softplus(x):
    return jnp.maximum(x, 0.0) + jnp.log1p(jnp.exp(-jnp.abs(x)))


def _norm_fwd(name, x, g):
    return _rowwise(name, lambda xv, gv: (_rms(xv, gv),), [x], [g], [(x.shape[1], BF16)])[0]


def _norm_bwd(name, x, g, dh, dres):
    def fn(xv, dhv, drv, gv):
        _, vjp = jax.vjp(_rms, xv, gv)
        dx, dg = vjp(dhv)
        return dx + drv, dg
    return _rowwise(name, fn, [x, dh, dres], [g], [(x.shape[1], F32)], [g.shape])


def _f_act(gate, up):
    return gate * _sigmoid(gate) * up


def _gate_up_act(name, h, w_gu):
    m, k = h.shape
    tm = _tile(m, MM_ROWS)

    def body(h_ref, w_ref, gu_ref, a_ref):
        gu = jnp.dot(h_ref[...], w_ref[...], preferred_element_type=F32)
        gu_ref[...] = gu
        a_ref[...] = _f_act(gu[:, :FF_PAD], gu[:, FF_PAD:]).astype(a_ref.dtype)

    return pl.pallas_call(
        body, name=name, grid=(m // tm, N_DEV),
        in_specs=[pl.BlockSpec((tm, k), lambda i, j: (i, 0)),
                  pl.BlockSpec((None, k, 2 * FF_PAD), lambda i, j: (j, 0, 0))],
        out_specs=[pl.BlockSpec((tm, 2 * FF_PAD), lambda i, j: (i, j)), pl.BlockSpec((tm, FF_PAD), lambda i, j: (i, j))],
        out_shape=[jax.ShapeDtypeStruct((m, N_DEV * 2 * FF_PAD), F32), jax.ShapeDtypeStruct((m, FF_HID), BF16)],
        compiler_params=_cparams(dimension_semantics=("parallel", "parallel")),
    )(h, w_gu)


def _gate_up_act_bwd(name, dout, w_down, gu, order):
    m, k = dout.shape
    tm = _tile(m, MM_ROWS)

    def body(d_ref, w_ref, gu_ref, order_ref, o_ref):
        da = 0.5 * lax.dot_general(d_ref[...].astype(BF16), w_ref[...], (((1,), (1,)), ((), ())),
                                   preferred_element_type=F32)
        guv = gu_ref[...]
        _, vjp = jax.vjp(_f_act, guv[:, :FF_PAD], guv[:, FF_PAD:])
        o_ref[...] = jnp.concatenate(vjp(da), axis=1).astype(o_ref.dtype)

    gu_spec = pl.BlockSpec((tm, 2 * FF_PAD), lambda i, j: (i, j))
    return pl.pallas_call(
        body, name=name, grid=(m // tm, N_DEV),
        in_specs=[pl.BlockSpec((tm, k), lambda i, j: (i, 0)), pl.BlockSpec((FF_PAD, k), lambda i, j: (j, 0)), gu_spec,
                  pl.BlockSpec(memory_space=pl.ANY)],
        out_specs=gu_spec, out_shape=jax.ShapeDtypeStruct((m, N_DEV * 2 * FF_PAD), BF16),
        compiler_params=_cparams(dimension_semantics=("parallel", "parallel")),
    )(dout, w_down, gu, order)


def _ffn_fwd(tag, x, norm, w_gu, w_down):
    h = _norm_fwd(tag + "_norm", x, norm)
    gu, a = _gate_up_act(tag + "_gu", h, w_gu)
    wd = w_down(a)
    out = _mm(tag + "_down", a, wd, "nn", res=x, scale=0.5)
    return out, (h, gu, a), wd


def _ffn_bwd(tag, x, norm, w_gu, w_down, saved, dout, on_down=None):
    h, gu, a = saved
    d_wdown = _mm(tag + "_dwdown", a, dout, "tn", out_dtype=BF16, scale=0.5).reshape(N_DEV, FF_PAD, D_MODEL)
    token = on_down(d_wdown) if on_down is not None else jnp.zeros((8, LANES), F32)
    dgu = _gate_up_act_bwd(tag + "_dgu", dout, w_down, gu, token)
    dh =_mmc_nt(tag + "_dh", dgu, w_gu, 0, 0, 2 * FF_PAD, D_MODEL)
    d_wgu = _mmc_tn(tag + "_dwgu", h, dgu, 2 * FF_PAD)
    dx, dnorm = _norm_bwd(tag + "_dnorm", x, norm, dh, dout)
    return dx, dnorm, d_wgu, d_wdown


def _shift_fwd(z, mu):
    t, c = z.shape[0], RWKV_COLS
    tr = 256

    def body(z_ref, zp_ref, mu_ref, r_ref, k_ref, v_ref, lo_ref, gd_ref):
        zv = z_ref[...]
        prev = zp_ref[7:8, :] * jnp.where(pl.program_id(0) == 0, 0.0, 1.0)
        row = lax.broadcasted_iota(jnp.int32, zv.shape, 0)
        zsh = jnp.where(row == 0, prev, pltpu.roll(zv, 1, 0))
        zs = zv + (zsh - zv) * mu_ref[...]
        r_ref[...] = zs[:, 0:512]
        k_ref[...] = zs[:, 512:1024]
        v_ref[...] = zs[:, 1024:1536]
        lo_ref[...] = zs[:, 1536:1664]
        gd_ref[...] = zs[:, 1664:1792]

    widths = (512, 512, 512, 128, 128)
    return pl.pallas_call(
        body, name="rwkv_shift", grid=(t // tr,),
        in_specs=[pl.BlockSpec((tr, c), lambda i: (i, 0)),
                  pl.BlockSpec((8, c), lambda i: (jnp.maximum(i * (tr // 8) - 1, 0), 0)),
                  pl.BlockSpec((1, c), lambda i: (0, 0))],
        out_specs=[pl.BlockSpec((tr, w), lambda i: (i, 0)) for w in widths],
        out_shape=[jax.ShapeDtypeStruct((t, w), F32) for w in widths],
        compiler_params=_cparams(dimension_semantics=("parallel",)),
    )(z, z, mu)


def _shift_bwd(z, mu, dr, dk, dv, dlo, dgd):
    t, c = z.shape[0], RWKV_COLS
    tr = 256
    nt = t // tr

    def body(z_ref, zp_ref, mu_ref, dr_ref, dk_ref, dv_ref, dlo_ref, dgd_ref,
             drn_ref, dkn_ref, dvn_ref, dlon_ref, dgdn_ref, dz_ref, dmu_ref):
        i = pl.program_id(0)
        zv, muv = z_ref[...], mu_ref[...]
        prev = zp_ref[7:8, :] * jnp.where(i == 0, 0.0, 1.0)
        row = lax.broadcasted_iota(jnp.int32, zv.shape, 0)
        zsh = jnp.where(row == 0, prev, pltpu.roll(zv, 1, 0))
        dzs = jnp.concatenate([dr_ref[...], dk_ref[...], dv_ref[...], dlo_ref[...], dgd_ref[...]], axis=1)
        nxt = jnp.concatenate([drn_ref[0:1, :], dkn_ref[0:1, :], dvn_ref[0:1, :], dlon_ref[0:1, :],
                               dgdn_ref[0:1, :]], axis=1) * jnp.where(i == nt - 1, 0.0, 1.0)
        u = dzs * muv
        un = jnp.where(row == tr - 1, nxt * muv, pltpu.roll(u, tr - 1, 0))
        dz_ref[...] = (dzs - u + un).astype(dz_ref.dtype)
        _accumulate(dmu_ref, jnp.sum(dzs * (zsh - zv), axis=0, keepdims=True))

    widths = (512, 512, 512, 128, 128)
    nxt_map = lambda i: (jnp.minimum((i + 1) * (tr // 8), t // 8 - 1), 0)
    return pl.pallas_call(
        body, name="rwkv_shift_bwd", grid=(nt,),
        in_specs=[pl.BlockSpec((tr, c), lambda i: (i, 0)),
                  pl.BlockSpec((8, c), lambda i: (jnp.maximum(i * (tr // 8) - 1, 0), 0)),
                  pl.BlockSpec((1, c), lambda i: (0, 0))]
        + [pl.BlockSpec((tr, w), lambda i: (i, 0)) for w in widths]
        + [pl.BlockSpec((8, w), nxt_map) for w in widths],
        out_specs=[pl.BlockSpec((tr, c), lambda i: (i, 0)), pl.BlockSpec((1, c), lambda i: (0, 0))],
        out_shape=[jax.ShapeDtypeStruct((t, c), BF16), jax.ShapeDtypeStruct((1, c), F32)],
        compiler_params=_cparams(dimension_semantics=("arbitrary",)),
    )(z, z, mu, dr, dk, dv, dlo, dgd, dr, dk, dv, dlo, dgd)


def _f_pre(k, lo, gd, w0, w2p, a0, a2p, g2, k_k, k_a):
    lane = lax.broadcasted_iota(jnp.int32, lo.shape, 1)
    lo_act = jnp.where(lane < DECAY_LORA, jnp.tanh(lo), lo)
    w = -_softplus(-(w0 + _bdot(lo_act, w2p))) - 0.5
    a = _sigmoid(a0 + _bdot(lo_act, a2p))
    g = _bdot(_sigmoid(gd), g2)
    kk = k * k_k
    k2 = k * (1.0 + (a - 1.0) * k_a)
    decay = jnp.exp(-jnp.exp(w))
    return k2, kk, a, decay, g


def _f_kk(kk, a):
    kkn = kk * lax.rsqrt(jnp.maximum(_head_sums(kk * kk), 1e-24))
    return kkn, kkn * a


def _f_post(y, r, k2, v, g, gn_w, gn_b, r_k):
    mean = _head_sums(y) * (1.0 / HEAD_DIM)
    var = _head_sums(jnp.square(y - mean)) * (1.0 / HEAD_DIM)
    yn = (y - mean) * lax.rsqrt(var + GN_EPS) * gn_w + gn_b
    bonus = _head_sums(r * k2 * r_k) * v
    return (yn + bonus) * g


def _to_v_rows(x):
    t = x.shape[0]
    return x.reshape(t, RWKV_HEADS, V_HI, V_LO).transpose(0, 2, 3, 1).reshape(t, V_HI, LANES)


def _from_v_rows(x):
    t = x.shape[0]
    return x.reshape(t, V_HI, V_LO, RWKV_HEADS).transpose(0, 3, 1, 2).reshape(t, RWKV_DIM)


def _k_cols(x):
    return jnp.tile(x, (V_LO, 1)).T


def _k_rows(x):
    xt = x.T
    out = xt[0:RWKV_HEADS]
    for l in range(1, V_LO):
        out = out + xt[l * RWKV_HEADS:(l + 1) * RWKV_HEADS]
    return out


def _wkv_fwd(r, w, k, kk, b, v):
    t = r.shape[0]
    tc = SCAN_CHUNK
    key_spec = pl.BlockSpec((tc, RWKV_HEADS, HEAD_DIM), lambda i: (i, 0, 0))
    row_spec = pl.BlockSpec((tc, V_HI, LANES), lambda i: (i, 0, 0))

    def body(r_ref, w_ref, k_ref, kk_ref, b_ref, v_ref, y_ref, st_ref, s_scr, cols_a, cols_b):
        @pl.when(pl.program_id(0) == 0)
        def _():
            s_scr[...] = jnp.zeros_like(s_scr)

        def prep(ti, buf):
            for n, ref in enumerate((r_ref, w_ref, k_ref, kk_ref, b_ref)):
                buf[n] = _k_cols(ref[ti])

        def step(ti, s, cur, nxt, ti_next):
            rc, wc, kc, kkc, bc = (cur[n] for n in range(5))
            prep(ti_next, nxt)
            vt = v_ref[ti]
            new, ys = [], []
            for j in range(V_HI):
                sa = -jnp.sum(s[j] * kkc, axis=0, keepdims=True)
                nj = s[j] * wc + bc * sa + kc * vt[j:j + 1]
                st_ref[ti, j] = nj
                ys.append(jnp.sum(nj * rc, axis=0, keepdims=True))
                new.append(nj)
            y_ref[ti] = jnp.concatenate(ys, axis=0)
            return tuple(new)

        def pair(i, s):
            s = step(2 * i, s, cols_a, cols_b, 2 * i + 1)
            return step(2 * i + 1, s, cols_b, cols_a, jnp.minimum(2 * i + 2, tc - 1))

        prep(0, cols_a)
        s = lax.fori_loop(0, tc // 2, pair, tuple(s_scr[j] for j in range(V_HI)))
        for j in range(V_HI):
            s_scr[j] = s[j]

    return pl.pallas_call(
        body, name="wkv_fwd", grid=(t // tc,),
        in_specs=[key_spec] * 5 + [row_spec],
        out_specs=[row_spec, pl.BlockSpec((tc, V_HI, HEAD_DIM, LANES), lambda i: (i, 0, 0, 0))],
        out_shape=[jax.ShapeDtypeStruct((t, V_HI, LANES), F32),
                   jax.ShapeDtypeStruct((t, V_HI, HEAD_DIM, LANES), F32)],
        scratch_shapes=[pltpu.VMEM((V_HI, HEAD_DIM, LANES), F32)] + [pltpu.VMEM((5, HEAD_DIM, LANES), F32)] * 2,
        compiler_params=_cparams(dimension_semantics=("arbitrary",)),
    )(r, w, k, kk, b, v)


def _wkv_bwd(r, w, k, kk, b, v, states, dy):
    t = r.shape[0]
    tc = SCAN_CHUNK
    nb = t // tc
    key_spec = pl.BlockSpec((tc, RWKV_HEADS, HEAD_DIM), lambda i: (nb - 1 - i, 0, 0))
    row_spec = pl.BlockSpec((tc, V_HI, LANES), lambda i: (nb - 1 - i, 0, 0))
    st_spec = pl.BlockSpec((tc, V_HI, HEAD_DIM, LANES), lambda i: (nb - 1 - i, 0, 0, 0))
    stp_spec = pl.BlockSpec((1, V_HI, HEAD_DIM, LANES), lambda i: (jnp.maximum((nb - 1 - i) * tc - 1, 0), 0, 0, 0))

    def body(r_ref, w_ref, k_ref, kk_ref, b_ref, v_ref, st_ref, stp_ref, dy_ref,
             dr_ref, dw_ref, dk_ref, dkk_ref, db_ref, dv_ref, ds_scr, cols_a, cols_b, accs_a, accs_b):
        @pl.when(pl.program_id(0) == 0)
        def _():
            ds_scr[...] = jnp.zeros_like(ds_scr)

        def colsum(x):
            return jnp.sum(x, axis=0, keepdims=True)

        def prep(ti, buf):
            for n, ref in enumerate((r_ref, w_ref, k_ref, kk_ref, b_ref)):
                buf[n] = _k_cols(ref[ti])

        def flush(ti, buf):
            for n, ref in enumerate((dr_ref, dk_ref, db_ref, dw_ref, dkk_ref)):
                ref[ti] = _k_rows(buf[n])

        def step(ti, ds, sp, cur, accs):
            rc, wc, kc, kkc, bc = (cur[n] for n in range(5))
            vt, dyt = v_ref[ti], dy_ref[ti]
            acc = None
            new, dvs = [], []
            for j in range(V_HI):
                st = st_ref[ti, j]
                dsj = ds[j] + rc * dyt[j:j + 1]
                sa = -colsum(sp[j] * kkc)
                dsa = colsum(dsj * bc)
                dvs.append(colsum(dsj * kc))
                parts = (st * dyt[j:j + 1], dsj * vt[j:j + 1], dsj * sa, dsj * sp[j], -(sp[j] * dsa))
                acc = parts if acc is None else tuple(a + q for a, q in zip(acc, parts))
                new.append(dsj * wc - kkc * dsa)
            dv_ref[ti] = jnp.concatenate(dvs, axis=0)
            for n in range(5):
                accs[n] = acc[n]
            return tuple(new)

        def states_before(ti):
            return tuple(st_ref[ti - 1, j] for j in range(V_HI))

        def pair(i, ds):
            ta = tc - 1 - 2 * i
            prep(ta - 1, cols_b)
            flush(jnp.minimum(ta + 1, tc - 1), accs_b)
            ds = step(ta, ds, states_before(ta), cols_a, accs_a)
            prep(ta - 2, cols_a)
            flush(ta, accs_a)
            return step(ta - 1, ds, states_before(ta - 1), cols_b, accs_b)

        prep(tc - 1, cols_a)
        accs_b[...] = jnp.zeros_like(accs_b)
        ds = lax.fori_loop(0, tc // 2 - 1, pair, tuple(ds_scr[j] for j in range(V_HI)))
        prep(0, cols_b)
        flush(2, accs_b)
        ds = step(1, ds, states_before(1), cols_a, accs_a)
        flush(1, accs_a)
        keep = jnp.where(pl.program_id(0) == nb - 1, 0.0, 1.0)
        ds = step(0, ds, tuple(stp_ref[0, j] * keep for j in range(V_HI)), cols_b, accs_b)
        flush(0, accs_b)
        for j in range(V_HI):
            ds_scr[j] = ds[j]

    key_out = jax.ShapeDtypeStruct((t, RWKV_HEADS, HEAD_DIM), F32)
    return pl.pallas_call(
        body, name="wkv_bwd", grid=(nb,),
        in_specs=[key_spec] * 5 + [row_spec, st_spec, stp_spec, row_spec],
        out_specs=[key_spec] * 5 + [row_spec],
        out_shape=[key_out] * 5 + [jax.ShapeDtypeStruct((t, V_HI, LANES), F32)],
        scratch_shapes=[pltpu.VMEM((V_HI, HEAD_DIM, LANES), F32)] + [pltpu.VMEM((5, HEAD_DIM, LANES), F32)] * 4,
        compiler_params=_cparams(dimension_semantics=("arbitrary",)),
    )(r, w, k, kk, b, v, states, states, dy)


PAIR = 2 * HEAD_DIM
N_PAIRS = ATTN_HEADS // 2
Q_COL0 = RWKV_COLS // PAIR
K_COL0 = Q_COL0 + N_PAIRS
V_COL0 = K_COL0 + N_PAIRS


def _swap_halves(x):
    lane = lax.broadcasted_iota(jnp.int32, x.shape, 1)
    return jnp.where((lane & (HEAD_DIM - 1)) < HEAD_DIM // 2, pltpu.roll(x, PAIR - HEAD_DIM // 2, 1),
                     pltpu.roll(x, HEAD_DIM // 2, 1))


@jax.custom_vjp
def _rope(x, cosf, sinf):
    return x * cosf + _swap_halves(x) * sinf


def _rope_fwd(x, cosf, sinf):
    return _rope(x, cosf, sinf), (cosf, sinf)


def _rope_bwd(res, d):
    cosf, sinf = res
    return d * cosf + _swap_halves(d * sinf), jnp.zeros_like(cosf), jnp.zeros_like(sinf)


_rope.defvjp(_rope_fwd, _rope_bwd)


def _head_sums(x):
    lane = lax.broadcasted_iota(jnp.int32, x.shape, 1)
    lo = jnp.where(lane < HEAD_DIM, 1.0, 0.0)
    hi = 1.0 - lo
    return lo * jnp.sum(x * lo, axis=1, keepdims=True) + hi * jnp.sum(x * hi, axis=1, keepdims=True)


def _f_qk(x, cosf, sinf, gain2):
    xn = x * lax.rsqrt(_head_sums(x * x) * (1.0 / HEAD_DIM) + RMS_EPS) * gain2
    return _rope(xn, cosf, sinf)


def _qk_prep(z, tab, q_gain, k_gain):
    t = z.shape[0]
    tr = 1024

    def body(z_ref, c_ref, s_ref, qg_ref, kg_ref, o_ref):
        g = jnp.where(pl.program_id(0) < N_PAIRS, qg_ref[...], kg_ref[...])
        o_ref[...] = _f_qk(z_ref[...], c_ref[...], s_ref[...], jnp.concatenate([g, g], axis=1))

    gain = pl.BlockSpec((1, HEAD_DIM), lambda c, i: (0, 0))
    return pl.pallas_call(
        body, name="qk_prep", grid=(2 * N_PAIRS, t // tr),
        in_specs=[pl.BlockSpec((tr, PAIR), lambda c, i: (i, Q_COL0 + c)), pl.BlockSpec((tr, PAIR), lambda c, i: (i, 0)),
                  pl.BlockSpec((tr, PAIR), lambda c, i: (i, 1)), gain, gain],
        out_specs=pl.BlockSpec((tr, PAIR), lambda c, i: (i, c)),
        out_shape=jax.ShapeDtypeStruct((t, 2 * N_PAIRS * PAIR), F32),
        compiler_params=_cparams(dimension_semantics=("parallel", "parallel")),
    )(z, tab, tab, q_gain, k_gain)


def _qk_prep_bwd(z, tab, q_gain, k_gain, dq, dk):
    t = z.shape[0]
    tr = 1024

    def body(z_ref, c_ref, s_ref, qg_ref, kg_ref, dq_ref, dk_ref, dz_ref, dqg_ref, dkg_ref):
        c, i = pl.program_id(0), pl.program_id(1)
        is_q = c < N_PAIRS
        g = jnp.where(is_q, qg_ref[...], kg_ref[...])
        d = jnp.where(is_q, dq_ref[...], dk_ref[...])
        _, vjp = jax.vjp(lambda xx, gg: _f_qk(xx, c_ref[...], s_ref[...], gg), z_ref[...],
                         jnp.concatenate([g, g], axis=1))
        dx, dg2 = vjp(d)
        dz_ref[...] = dx.astype(dz_ref.dtype)
        dg = dg2[:, :HEAD_DIM] + dg2[:, HEAD_DIM:]
        first_q = jnp.logical_and(c == 0, i == 0)
        first_k = jnp.logical_and(c == N_PAIRS, i == 0)

        @pl.when(first_q)
        def _():
            dqg_ref[...] = dg

        @pl.when(jnp.logical_and(is_q, jnp.logical_not(first_q)))
        def _():
            dqg_ref[...] += dg

        @pl.when(first_k)
        def _():
            dkg_ref[...] = dg

        @pl.when(jnp.logical_and(jnp.logical_not(is_q), jnp.logical_not(first_k)))
        def _():
            dkg_ref[...] += dg

    gain = pl.BlockSpec((1, HEAD_DIM), lambda c, i: (0, 0))
    return pl.pallas_call(
        body, name="qk_prep_bwd", grid=(2 * N_PAIRS, t // tr),
        in_specs=[pl.BlockSpec((tr, PAIR), lambda c, i: (i, Q_COL0 + c)), pl.BlockSpec((tr, PAIR), lambda c, i: (i, 0)),
                  pl.BlockSpec((tr, PAIR), lambda c, i: (i, 1)), gain, gain,
                  pl.BlockSpec((tr, PAIR), lambda c, i: (i, jnp.minimum(c, N_PAIRS - 1))),
                  pl.BlockSpec((tr, PAIR), lambda c, i: (i, jnp.maximum(c - N_PAIRS, 0)))],
        out_specs=[pl.BlockSpec((tr, PAIR), lambda c, i: (i, c)), gain, gain],
        out_shape=[jax.ShapeDtypeStruct((t, 2 * N_PAIRS * PAIR), BF16), jax.ShapeDtypeStruct((1, HEAD_DIM), F32),
                   jax.ShapeDtypeStruct((1, HEAD_DIM), F32)],
        compiler_params=_cparams(dimension_semantics=("arbitrary", "arbitrary")),
    )(z, tab, tab, q_gain, k_gain, dq, dk)


def _attn_block(q, kp, kc, vp, vc, kmin):
    k2 = jnp.concatenate([kp, kc], axis=0)
    v2 = jnp.concatenate([vp, vc], axis=0)
    s = _bdot_nt(q, k2) * (HEAD_DIM ** -0.5)
    qi = lax.broadcasted_iota(jnp.int32, s.shape, 0)
    kj = lax.broadcasted_iota(jnp.int32, s.shape, 1)
    dist = qi + BAND_BLOCK - kj
    valid = (dist >= 0) & (dist <= BAND_BLOCK) & (kj >= kmin)
    s = jnp.where(valid, s, NEG_INF)
    m = lax.stop_gradient(jnp.max(s, axis=-1, keepdims=True))
    e = jnp.exp(s - m)
    l = jnp.sum(e, axis=-1, keepdims=True)
    o = _bdot(e, v2) / l
    return o, m + jnp.log(l)


def _fold(src_ref, dst_ref, dil):
    t = src_ref.shape[0]
    ln = t // dil
    for j in range(dil):
        dst_ref[j * ln:(j + 1) * ln, :] = src_ref[pl.ds(j, ln, stride=dil), :]


def _unfold(src_ref, dst_ref, dil):
    t = src_ref.shape[0]
    ln = t // dil
    for j in range(dil):
        dst_ref[pl.ds(j, ln, stride=dil), :] = src_ref[j * ln:(j + 1) * ln, :]


def _per_group(fn):
    pair = pl.program_id(0)
    for gi, (_, dil) in enumerate(ATTN_GROUPS):
        @pl.when(jnp.logical_or(pair == 2 * gi, pair == 2 * gi + 1))
        def _(dil=dil):
            fn(dil)


def _block_rows(idx, blocks_per_seq):
    first = (idx & (blocks_per_seq - 1)) == 0
    cur = pl.ds(pl.multiple_of(idx * BAND_BLOCK, BAND_BLOCK), BAND_BLOCK)
    prev = pl.ds(pl.multiple_of(jnp.maximum(idx - 1, 0) * BAND_BLOCK, BAND_BLOCK), BAND_BLOCK)
    return first, cur, prev


def _heads(x):
    return x[:, :HEAD_DIM], x[:, HEAD_DIM:]


def _attn_fwd(qk, z):
    t = z.shape[0]
    n_blocks = t // BAND_BLOCK

    def body(q_ref, k_ref, v_ref, o_ref, lse_ref, qf, kf, vf, of, lf):
        def run(dil):
            _fold(q_ref, qf, dil)
            _fold(k_ref, kf, dil)
            _fold(v_ref, vf, dil)
            blocks_per_seq = n_blocks // dil

            def block(idx, carry):
                first, cur, prev = _block_rows(idx, blocks_per_seq)
                kmin = jnp.where(first, BAND_BLOCK, 0)
                outs, lses = [], []
                for q, kp, kc, vp, vc in zip(_heads(qf[cur, :]), _heads(kf[prev, :]), _heads(kf[cur, :]),
                                             _heads(vf[prev, :]), _heads(vf[cur, :])):
                    o, ls = _attn_block(q, kp, kc, vp, vc, kmin)
                    outs.append(o)
                    lses.append(jnp.broadcast_to(ls, o.shape))
                of[cur, :] = jnp.concatenate(outs, axis=1)
                lf[cur, :] = jnp.concatenate(lses, axis=1)
                return carry

            lax.fori_loop(0, n_blocks, block, 0)
            _unfold(of, o_ref, dil)
            _unfold(lf, lse_ref, dil)

        _per_group(run)

    slab = jax.ShapeDtypeStruct((t, N_PAIRS * PAIR), F32)
    out_spec = pl.BlockSpec((t, PAIR), lambda p: (0, p))
    return pl.pallas_call(
        body, name="attn_fwd", grid=(N_PAIRS,),
        in_specs=[pl.BlockSpec((t, PAIR), lambda p: (0, p)), pl.BlockSpec((t, PAIR), lambda p: (0, N_PAIRS + p)),
                  pl.BlockSpec((t, PAIR), lambda p: (0, V_COL0 + p))],
        out_specs=[out_spec, out_spec], out_shape=[slab, slab],
        scratch_shapes=[pltpu.VMEM((t, PAIR), F32)] * 5,
        compiler_params=_cparams(dimension_semantics=("parallel",)),
    )(qk, qk, z)


def _attn_bwd(qk, z, do, dlse):
    t = z.shape[0]
    n_blocks = t // BAND_BLOCK

    def body(q_ref, k_ref, v_ref, do_ref, dl_ref, dq_ref, dk_ref, dv_ref, qf, kf, vf, dof, dlf, dqf, dkf, dvf):
        def run(dil):
            for src, dst in ((q_ref, qf), (k_ref, kf), (v_ref, vf), (do_ref, dof), (dl_ref, dlf)):
                _fold(src, dst, dil)
            blocks_per_seq = n_blocks // dil

            def block(idx, carry):
                first, cur, prev = _block_rows(idx, blocks_per_seq)
                kmin = jnp.where(first, BAND_BLOCK, 0)
                grads = []
                for q, kp, kc, vp, vc, do_h, dl_h in zip(
                        _heads(qf[cur, :]), _heads(kf[prev, :]), _heads(kf[cur, :]), _heads(vf[prev, :]),
                        _heads(vf[cur, :]), _heads(dof[cur, :]), _heads(dlf[cur, :])):
                    _, vjp = jax.vjp(functools.partial(_attn_block, kmin=kmin), q, kp, kc, vp, vc)
                    grads.append(vjp((do_h, jnp.sum(dl_h, axis=1, keepdims=True))))
                dq, dkp, dkc, dvp, dvc = (jnp.concatenate([a, b], axis=1) for a, b in zip(*grads))
                dqf[cur, :] = dq
                dkf[cur, :] = dkc
                dvf[cur, :] = dvc

                @pl.when(jnp.logical_not(first))
                def _():
                    dkf[prev, :] += dkp
                    dvf[prev, :] += dvp

                return carry

            lax.fori_loop(0, n_blocks, block, 0)
            _unfold(dqf, dq_ref, dil)
            _unfold(dkf, dk_ref, dil)
            _unfold(dvf, dv_ref, dil)

        _per_group(run)

    slab = jax.ShapeDtypeStruct((t, N_PAIRS * PAIR), F32)
    own = pl.BlockSpec((t, PAIR), lambda p: (0, p))
    return pl.pallas_call(
        body, name="attn_bwd", grid=(N_PAIRS,),
        in_specs=[own, pl.BlockSpec((t, PAIR), lambda p: (0, N_PAIRS + p)),
                  pl.BlockSpec((t, PAIR), lambda p: (0, V_COL0 + p)), own, own],
        out_specs=[own] * 3, out_shape=[slab] * 3,
        scratch_shapes=[pltpu.VMEM((t, PAIR), F32)] * 8,
        compiler_params=_cparams(dimension_semantics=("parallel",)),
    )(qk, qk, z, do, dlse)


def _f_comb(o1, o2, o3, l1, l2, l3):
    m = jnp.maximum(jnp.maximum(l1, l2), l3)
    e1, e2, e3 = jnp.exp(l1 - m), jnp.exp(l2 - m), jnp.exp(l3 - m)
    den = e1 + e2 + e3
    return (e1 / den) * o1 + (e2 / den) * o2 + (e3 / den) * o3


def _all_gather_hbm(name, arrs):
    na = len(arrs)

    def body(*refs):
        x_refs, out_refs = refs[:na], refs[na:2 * na]
        send_sems, recv_sems, local_sems = refs[2 * na:]
        mx, my, mc = lax.axis_index("x"), lax.axis_index("y"), lax.axis_index("c")
        me, sibling = (mx, my, mc), (mx, my, 1 - mc)
        chips = [(1 - mx, my), (mx, 1 - my), (1 - mx, 1 - my)]

        def slot(a, px, py, pc):
            return out_refs[a].at[4 * px + 2 * py + pc]

        def copy(a, k, block, to, src=None):
            return pltpu.make_async_remote_copy(
                src_ref=slot(a, *block) if src is None else src, dst_ref=slot(a, *block),
                send_sem=send_sems.at[a, k], recv_sem=recv_sems.at[a, k], device_id=to, device_id_type=MESH)

        mine = [pltpu.make_async_copy(x_refs[a], slot(a, *me), local_sems.at[a]) for a in range(na)]
        for cp in mine:
            cp.start()
        first = []
        for a in range(na):
            first.append(copy(a, 0, me, sibling, src=x_refs[a]))
            first += [copy(a, 1 + j, me, (*chip, mc), src=x_refs[a]) for j, chip in enumerate(chips)]
        for cp in first:
            cp.start()
        passed = []
        for j, chip in enumerate(chips):
            for a in range(na):
                copy(a, 1 + j, (*chip, mc), me).wait_recv()
                passed.append(copy(a, 4 + j, (*chip, mc), sibling))
                passed[-1].start()
        for a in range(na):
            copy(a, 0, sibling, me).wait_recv()
            for j, chip in enumerate(chips):
                copy(a, 4 + j, (*chip, 1 - mc), me).wait_recv()
        for cp in first + passed:
            cp.wait_send()
        for cp in mine:
            cp.wait()

    hbm = pl.BlockSpec(memory_space=pl.ANY)
    return pl.pallas_call(
        body, name=name,
        out_shape=[jax.ShapeDtypeStruct((N_DEV,) + a.shape, a.dtype) for a in arrs],
        in_specs=[hbm] * na, out_specs=[hbm] * na,
        scratch_shapes=[pltpu.SemaphoreType.DMA((na, 7)), pltpu.SemaphoreType.DMA((na, 7)),
                        pltpu.SemaphoreType.DMA((na,))],
    )(*arrs)


def _all_gather_vmem(x):
    rws, cols = x.shape

    def body(x_ref, out_ref, send_sems, recv_sems):
        mx, my, mc = lax.axis_index("x"), lax.axis_index("y"), lax.axis_index("c")
        me, sibling = (mx, my, mc), (mx, my, 1 - mc)
        chips = [(1 - mx, my), (mx, 1 - my), (1 - mx, 1 - my)]

        def slot(px, py, pc):
            return out_ref.at[4 * px + 2 * py + pc]

        def copy(k, block, to, src=None):
            return pltpu.make_async_remote_copy(
                src_ref=slot(*block) if src is None else src, dst_ref=slot(*block),
                send_sem=send_sems.at[k], recv_sem=recv_sems.at[k], device_id=to, device_id_type=MESH)

        first = [copy(0, me, sibling, src=x_ref)]
        first += [copy(1 + j, me, (*chip, mc), src=x_ref) for j, chip in enumerate(chips)]
        for cp in first:
            cp.start()
        out_ref[4 * mx + 2 * my + mc] = x_ref[...]
        passed = [copy(4 + j, (*chip, mc), sibling) for j, chip in enumerate(chips)]
        for j, chip in enumerate(chips):
            copy(1 + j, (*chip, mc), me).wait_recv()
            passed[j].start()
        copy(0, sibling, me).wait_recv()
        for j, chip in enumerate(chips):
            copy(4 + j, (*chip, 1 - mc), me).wait_recv()
        for cp in first + passed:
            cp.wait_send()

    return pl.pallas_call(
        body, name="all_gather_small",
        out_shape=jax.ShapeDtypeStruct((N_DEV, rws, cols), x.dtype),
        in_specs=[pl.BlockSpec(memory_space=pltpu.VMEM)], out_specs=pl.BlockSpec(memory_space=pltpu.VMEM),
        scratch_shapes=[pltpu.SemaphoreType.DMA((7,)), pltpu.SemaphoreType.DMA((7,))],
    )(x)


def _all_to_all_hbm(name, arrs):
    na = len(arrs)

    def body(*refs):
        g_refs, out_refs = refs[:na], refs[na:2 * na]
        send_sems, recv_sems, local_sems = refs[2 * na:]
        mx, my, mc = lax.axis_index("x"), lax.axis_index("y"), lax.axis_index("c")
        me = 4 * mx + 2 * my + mc
        mine = [pltpu.make_async_copy(g_refs[a].at[me], out_refs[a].at[me], local_sems.at[a]) for a in range(na)]
        for cp in mine:
            cp.start()
        copies = []
        for k in range(1, N_DEV):
            px, py, pc = mx ^ (k >> 2), my ^ ((k >> 1) & 1), mc ^ (k & 1)
            peer = 4 * px + 2 * py + pc
            for a in range(na):
                copies.append(pltpu.make_async_remote_copy(
                    src_ref=g_refs[a].at[peer], dst_ref=out_refs[a].at[me], send_sem=send_sems.at[a, k - 1],
                    recv_sem=recv_sems.at[a, k - 1], device_id=(px, py, pc), device_id_type=MESH))
        for cp in copies:
            cp.start()
        for cp in copies:
            cp.wait_recv()
        for cp in copies:
            cp.wait_send()
        for cp in mine:
            cp.wait()

    hbm = pl.BlockSpec(memory_space=pl.ANY)
    return pl.pallas_call(
        body, name=name,
        out_shape=[jax.ShapeDtypeStruct(a.shape, a.dtype) for a in arrs],
        in_specs=[hbm] * na, out_specs=[hbm] * na,
        scratch_shapes=[pltpu.SemaphoreType.DMA((na, 7)), pltpu.SemaphoreType.DMA((na, 7)),
                        pltpu.SemaphoreType.DMA((na,))],
    )(*arrs)


def _scatter_copies(g_refs, land_refs, send_sems, recv_sems):
    mx, my, mc = lax.axis_index("x"), lax.axis_index("y"), lax.axis_index("c")
    me = 4 * mx + 2 * my + mc
    copies = []
    for k in range(1, N_DEV):
        px, py, pc = mx ^ (k >> 2), my ^ ((k >> 1) & 1), mc ^ (k & 1)
        peer = 4 * px + 2 * py + pc
        for a, (g_ref, land_ref) in enumerate(zip(g_refs, land_refs)):
            copies.append(pltpu.make_async_remote_copy(
                src_ref=g_ref.at[peer], dst_ref=land_ref.at[me], send_sem=send_sems.at[a * (N_DEV - 1) + k - 1],
                recv_sem=recv_sems.at[a * (N_DEV - 1) + k - 1], device_id=(px, py, pc), device_id_type=MESH))
    return copies


_HBM = pl.BlockSpec(memory_space=pltpu.HBM)
_SEM = pl.BlockSpec(memory_space=pltpu.SEMAPHORE)
_DATAFLOW = pltpu.SideEffectType.DATAFLOW_SIDE_EFFECTING


def _gather_copies(x_refs, land_refs, send_sems, recv_sems):
    mx, my, mc = lax.axis_index("x"), lax.axis_index("y"), lax.axis_index("c")
    me = 4 * mx + 2 * my + mc
    copies = []
    for k in range(1, N_DEV):
        px, py, pc = mx ^ (k >> 2), my ^ ((k >> 1) & 1), mc ^ (k & 1)
        for a, (x_ref, land_ref) in enumerate(zip(x_refs, land_refs)):
            copies.append(pltpu.make_async_remote_copy(
                src_ref=x_ref, dst_ref=land_ref.at[me], send_sem=send_sems.at[a * (N_DEV - 1) + k - 1],
                recv_sem=recv_sems.at[a * (N_DEV - 1) + k - 1], device_id=(px, py, pc), device_id_type=MESH))
    return copies


N_CHIPS = N_DEV // 2


def _pair_swap(name, arrs):
    na = len(arrs)

    def body(*refs):
        g_refs, out_refs = refs[:na], refs[na:2 * na]
        send_sems, recv_sems = refs[2 * na:]
        mx, my, mc = lax.axis_index("x"), lax.axis_index("y"), lax.axis_index("c")
        copies = [pltpu.make_async_remote_copy(
            src_ref=g_refs[a].at[2 * q + 1 - mc], dst_ref=out_refs[a].at[q], send_sem=send_sems.at[a, q],
            recv_sem=recv_sems.at[a, q], device_id=(mx, my, 1 - mc), device_id_type=MESH)
            for a in range(na) for q in range(N_CHIPS)]
        for cp in copies:
            cp.start()
        for cp in copies:
            cp.wait_recv()
        for cp in copies:
            cp.wait_send()

    hbm = pl.BlockSpec(memory_space=pl.ANY)
    return pl.pallas_call(
        body, name=name,
        out_shape=[jax.ShapeDtypeStruct((N_CHIPS,) + a.shape[1:], a.dtype) for a in arrs],
        in_specs=[hbm] * na, out_specs=[hbm] * na,
        scratch_shapes=[pltpu.SemaphoreType.DMA((na, N_CHIPS)), pltpu.SemaphoreType.DMA((na, N_CHIPS))],
    )(*arrs)


def _pair_add(name, g, half, core):
    _, r, c = g.shape
    tr = _tile(r, 512) if r % LANES == 0 else r

    def body(core_ref, g_ref, h_ref, o_ref):
        o_ref[...] = (g_ref[...].astype(F32) + h_ref[...].astype(F32)).astype(o_ref.dtype)

    return pl.pallas_call(
        body, name=name,
        grid_spec=pltpu.PrefetchScalarGridSpec(
            num_scalar_prefetch=1, grid=(N_CHIPS, r // tr),
            in_specs=[pl.BlockSpec((None, tr, c), lambda q, i, core_ref: (2 * q + core_ref[0], i, 0)),
                      pl.BlockSpec((None, tr, c), lambda q, i, core_ref: (q, i, 0))],
            out_specs=pl.BlockSpec((None, tr, c), lambda q, i, core_ref: (q, i, 0))),
        out_shape=jax.ShapeDtypeStruct((N_CHIPS, r, c), g.dtype),
        compiler_params=_cparams(dimension_semantics=("parallel", "parallel")),
    )(core, g, half)


def _chip_copies(h_refs, land_refs, send_sems, recv_sems):
    mx, my, mc = lax.axis_index("x"), lax.axis_index("y"), lax.axis_index("c")
    my_chip = 2 * mx + my
    copies = []
    for k in range(1, N_CHIPS):
        px, py = mx ^ (k >> 1), my ^ (k & 1)
        for a, (h_ref, land_ref) in enumerate(zip(h_refs, land_refs)):
            copies.append(pltpu.make_async_remote_copy(
                src_ref=h_ref.at[2 * px + py], dst_ref=land_ref.at[my_chip], send_sem=send_sems.at[a * (N_DEV - 1) + k - 1],
                recv_sem=recv_sems.at[a * (N_DEV - 1) + k - 1], device_id=(px, py, mc), device_id_type=MESH))
    return copies


def _exchange_start(name, copies, srcs, lands, after):
    na = len(srcs)

    def body(*refs):
        for cp in copies(refs[:na], refs[na:2 * na], refs[2 * na + 1], refs[2 * na + 2]):
            cp.start()
        refs[-1][...] = jnp.zeros_like(refs[-1])

    in_hbm = lambda a: pltpu.with_memory_space_constraint(a, pltpu.HBM)
    outs = pl.pallas_call(
        body, name=name,
        out_shape=(pltpu.SemaphoreType.DMA((na * (N_DEV - 1),)), pltpu.SemaphoreType.DMA((na * (N_DEV - 1),)),
                   *[pltpu.HBM(a.shape, a.dtype) for a in list(srcs) + list(lands)],
                   jax.ShapeDtypeStruct((8, LANES), F32)),
        in_specs=[_HBM] * (2 * na) + [pl.BlockSpec(memory_space=pl.ANY)],
        out_specs=(_SEM, _SEM, *[_HBM] * (2 * na), pl.BlockSpec(memory_space=pltpu.VMEM)),
        input_output_aliases={i: 2 + i for i in range(2 * na)},
        compiler_params=pltpu.CompilerParams(has_side_effects=_DATAFLOW),
    )(*[in_hbm(a) for a in srcs], *[in_hbm(a) for a in lands], after)
    return outs[0], outs[1], outs[2:2 + na], outs[2 + na:2 + 2 * na], outs[-1]


def _exchange_wait(name, copies, send_sems, recv_sems, srcs, lands, after):
    na = len(srcs)

    def body(*refs):
        for cp in copies(refs[:na], refs[na:2 * na], refs[2 * na], refs[2 * na + 1]):
            cp.wait_send()
            cp.wait_recv()

    outs = pl.pallas_call(
        body, name=name,
        out_shape=[pltpu.HBM(a.shape, a.dtype) for a in list(srcs) + list(lands)],
        in_specs=[_HBM] * (2 * na) + [_SEM, _SEM, pl.BlockSpec(memory_space=pl.ANY)], out_specs=[_HBM] * (2 * na),
        input_output_aliases={i: i for i in range(2 * na)},
        compiler_params=pltpu.CompilerParams(has_side_effects=_DATAFLOW),
    )(*srcs, *lands, send_sems, recv_sems, after)
    return outs[:na], outs[na:]


def _sum_slots(name, g, tr):
    _, rws, cols = g.shape

    def body(g_ref, o_ref):
        acc = g_ref[0].astype(F32)
        for j in range(1, N_DEV):
            acc = acc + g_ref[j].astype(F32)
        o_ref[...] = acc

    return pl.pallas_call(
        body, name=name, grid=(rws // tr,),
        in_specs=[pl.BlockSpec((N_DEV, tr, cols), lambda i: (0, i, 0))],
        out_specs=pl.BlockSpec((tr, cols), lambda i: (i, 0)),
        out_shape=jax.ShapeDtypeStruct((rws, cols), F32),
        compiler_params=_cparams(dimension_semantics=("parallel",)),
    )(g)


def _adam_math(wv, gv, mv, vv):
    mn = ADAM_B1 * mv + (1.0 - ADAM_B1) * gv
    vn = ADAM_B2 * vv + (1.0 - ADAM_B2) * jnp.square(gv)
    m_hat = mn / (1.0 - ADAM_B1 ** ADAM_STEP)
    v_hat = vn / (1.0 - ADAM_B2 ** ADAM_STEP)
    delta = -ADAM_LR * (m_hat / (jnp.sqrt(v_hat) + ADAM_EPS) + ADAM_WD * wv)
    return delta, mn, vn


def _adamw(name, w, g, m, v, tr):
    return _rowwise(name, _adam_math, [w, g, m, v], [], [(LANES, F32)] * 3, tr=tr)


def _adamw_slots(name, recv, own, rb, cb, cw, w, m, v, tr, order):
    nr, nc = w.shape
    n = recv.shape[0]

    def body(g_ref, own_ref, w_ref, m_ref, v_ref, order_ref, go_ref, d_ref, mo_ref, vo_ref):
        me = 2 * lax.axis_index("x") + lax.axis_index("y")
        if n == N_DEV:
            me = 2 * me + lax.axis_index("c")
        acc = None
        for s in range(n):
            part = jnp.where(me == s, own_ref[s], g_ref[s]).astype(F32)
            acc = part if acc is None else acc + part
        g = acc[:, :nc]
        go_ref[...] = g
        d_ref[...], mo_ref[...], vo_ref[...] = _adam_math(w_ref[...], g, m_ref[...], v_ref[...])

    nat = pl.BlockSpec((tr, nc), lambda i: (i, 0))
    slots = pl.BlockSpec((n, tr, cw), lambda i: (0, rb + i, cb))
    return pl.pallas_call(
        body, name=name, grid=(nr // tr,),
        in_specs=[slots, slots, nat, nat, nat, pl.BlockSpec(memory_space=pl.ANY)],
        out_specs=[nat] * 4, out_shape=[jax.ShapeDtypeStruct((nr, nc), F32)] * 4,
        compiler_params=_cparams(dimension_semantics=("parallel",)),
    )(recv, own, w, m, v, order)


def _local_blocks(w):
    pad_cols = lambda a: jnp.pad(a, ((0, 0), (0, FF_PAD - FF_SHARD)))
    pad_rows = lambda a: jnp.pad(a, ((0, FF_PAD - FF_SHARD), (0, 0)))
    gate_up = lambda tag: jnp.concatenate([pad_cols(w[tag + "_w_gate"]), pad_cols(w[tag + "_w_up"])], axis=1)
    blocks = {
        "ffn1_gu": gate_up("ffn1"), "ffn1_d": pad_rows(w["ffn1_w_down"]), "w_in": w["w_in"],
        "lora": jnp.concatenate([w["rwkv_w2"], w["rwkv_a2"], w["rwkv_g2"]], axis=0),
        "br": jnp.concatenate([w["w_br_rwkv"], w["w_br_attn"], w["ple_w_proj"]], axis=0),
        "w_out": w["w_out"], "ffn2_gu": gate_up("ffn2"), "ffn2_d": pad_rows(w["ffn2_w_down"]),
        "ple_gate": w["ple_w_gate"],
    }
    return {n: a.astype(BF16) for n, a in blocks.items()}


GATHER_GROUPS = {"head": ("ffn1_gu", "ffn1_d"), "mid": ("w_in", "lora"),
                 "rest": ("br", "w_out", "ffn2_gu", "ffn2_d", "ple_gate")}

SCATTER_GROUPS = {"tail": ("ple_gate", "ple_proj", "ffn2_gu", "ffn2_d"), "branch": ("w_out", "br"),
                  "mixer": ("lora", "w_in"),
                  "ffn1_down": ("ffn1_d",), "head": ("ffn1_gu",)}

ADAM_PLAN = (
    ("ffn1_w_gate", "ffn1_gu", 0, 0, FF_PAD, 256), ("ffn1_w_up", "ffn1_gu", 0, 1, FF_PAD, 256),
    ("ffn1_w_down", "ffn1_d", 0, 0, D_MODEL, FF_SHARD // 2), ("w_in", "w_in", 0, 0, IN_SHARD, 256),
    ("rwkv_w2", "lora", 0, 0, HEAD_DIM, 64), ("rwkv_a2", "lora", 1, 0, HEAD_DIM, 64),
    ("rwkv_g2", "lora", 2, 0, HEAD_DIM, 64),
    ("w_br_rwkv", "br", 0, 0, OUT_SHARD, 256), ("w_br_attn", "br", 2, 0, OUT_SHARD, 256),
    ("ple_w_proj", "ple_proj", 0, 0, OUT_SHARD, 256), ("w_out", "w_out", 0, 0, D_MODEL, OUT_SHARD),
    ("ffn2_w_gate", "ffn2_gu", 0, 0, FF_PAD, 256), ("ffn2_w_up", "ffn2_gu", 0, 1, FF_PAD, 256),
    ("ffn2_w_down", "ffn2_d", 0, 0, D_MODEL, FF_SHARD // 2), ("ple_w_gate", "ple_gate", 0, 0, D_MODEL, OUT_SHARD),
)


def _pack_small(arrs, rows):
    flat = jnp.concatenate([a.reshape(-1) for a in arrs])
    return jnp.pad(flat, (0, rows * LANES - flat.shape[0])).reshape(rows, LANES)


def _unpack_small(flat, like):
    flat = flat.reshape(-1)
    out, off = [], 0
    for a in like:
        out.append(flat[off:off + a.size].reshape(a.shape))
        off += a.size
    return out


def _local_step(x, p, pos, target, sm, wg, fetch, on_grads):
    t = x.shape[0]

    w_d1 = wg["ffn1_d"].reshape(FF_HID, D_MODEL)
    x1, ffn1_saved, _ = _ffn_fwd("ffn1", x, sm["ffn1_norm"], wg["ffn1_gu"], lambda after: w_d1)
    wg = {**wg, **fetch("mid", x1)}
    full_cols = lambda blk: blk.transpose(1, 0, 2).reshape(blk.shape[1], N_DEV * blk.shape[2])
    lora_w2 = full_cols(wg["lora"][:, :DECAY_LORA])
    lora_a2 = full_cols(wg["lora"][:, DECAY_LORA:DECAY_LORA + ICLR_LORA])
    lora_g2 = full_cols(wg["lora"][:, DECAY_LORA + ICLR_LORA:])
    h2 = _norm_fwd("mix_norm", x1, sm["mix_norm"])
    z = _mmc_nn("w_in", h2, wg["w_in"], 0, 0, IN_SHARD)
    z_g = (z, 2 * D_MODEL, (RWKV_COLS + 3 * ATTN_DIM) // (2 * D_MODEL))

    r, k, v, lo, gd = _shift_fwd(z, sm["rwkv_mu"])
    zero_lo = jnp.zeros((DECAY_LORA, RWKV_DIM), BF16)
    w2p = jnp.concatenate([lora_w2, zero_lo], axis=0).astype(F32)
    a2p = jnp.concatenate([zero_lo, lora_a2], axis=0).astype(F32)
    pre_params = [sm["rwkv_w0"], w2p, sm["rwkv_a0"], a2p, lora_g2.astype(F32), sm["rwkv_k_k"], sm["rwkv_k_a"]]
    wide = [(RWKV_DIM, F32)]
    k2, kk, a, decay, g = _rowwise("rwkv_pre", _f_pre, [k, lo, gd], pre_params, wide * 5)
    kkn, b = _pairwise("rwkv_kk", _f_kk, [kk, a], [], 2)
    scan_in = [u.reshape(t, RWKV_HEADS, HEAD_DIM) for u in (r, decay, k2, kkn, b)]
    v_rows = _to_v_rows(v)
    y_rows, states = _wkv_fwd(*scan_in, v_rows)
    y = _from_v_rows(y_rows)
    post_params = [sm["rwkv_gn_w"], sm["rwkv_gn_b"], sm["rwkv_r_k"]]
    post_rows = [y, r, k2, v, g]
    y_rwkv = _pairwise("rwkv_post", lambda *av: (_f_post(*av),), post_rows, post_params, 1)[0]

    inv_freq = 1.0 / (ROPE_THETA ** (jnp.arange(0, HEAD_DIM, 2, dtype=F32) / HEAD_DIM))
    freq2 = jnp.tile(inv_freq, 2 * PAIR // HEAD_DIM).reshape(1, PAIR)
    half = jnp.ones((HEAD_DIM // 2,), F32)
    sign2 = jnp.tile(jnp.concatenate([-half, half]), PAIR // HEAD_DIM).reshape(1, PAIR)

    def rope_table(posv, fr, sg):
        ang = posv * fr
        return (jnp.concatenate([jnp.cos(ang), jnp.sin(ang) * sg], axis=1),)
    tab = _rowwise("rope_table", rope_table, [pos.astype(F32).reshape(t, 1)], [freq2, sign2], [(2 * PAIR, F32)])[0]
    qk = _qk_prep(z, tab, sm["q_norm"], sm["k_norm"])
    o_all, lse_all = _attn_fwd(qk, z)
    gw = HEADS_PER_GROUP * HEAD_DIM

    def by_group(ov, lv):
        return [ov[:, i * gw:(i + 1) * gw] for i in range(3)] + [lv[:, i * gw:(i + 1) * gw] for i in range(3)]
    y_attn = _rowwise("attn_comb", lambda ov, lv: (_f_comb(*by_group(ov, lv)),), [o_all, lse_all], [], [(gw, F32)])[0]

    wg = {**wg, **fetch("rest", y_rwkv)}
    w_d2 = wg["ffn2_d"].reshape(FF_HID, D_MODEL)
    w_out = wg["w_out"].reshape(D_MODEL, D_MODEL)
    w_pg = wg["ple_gate"].reshape(D_MODEL, D_MODEL)
    w_brr = full_cols(wg["br"][:, :RWKV_DIM])
    w_bra = full_cols(wg["br"][:, RWKV_DIM:RWKV_DIM + gw])
    w_pp = full_cols(wg["br"][:, RWKV_DIM + gw:])
    u_r = _mm("br_rwkv", y_rwkv, w_brr, "nn")
    u_a = _mm("br_attn", y_attn, w_bra, "nn")

    def f_merge(zgr, zga, ur, ua):
        return _sigmoid(zgr) * ur + _sigmoid(zga) * ua
    merged = _rowwise("merge", lambda zg, ur, ua: (f_merge(zg[:, :D_MODEL], zg[:, D_MODEL:], ur, ua),),
                      [z_g, u_r, u_a], [], [(D_MODEL, BF16)])[0]
    x2 = _mm("w_out", merged, w_out, "nn", res=x1)
    x3, ffn2_saved, _ = _ffn_fwd("ffn2", x2, sm["ffn2_norm"], wg["ffn2_gu"], lambda after: w_d2)

    hn = _norm_fwd("ple_norm", x3, sm["ple_norm"])
    gz = _mm("ple_gate", hn, w_pg, "nn")
    pp = _mm("ple_proj", p, w_pp, "nn")

    def f_head(x3v, gzv, ppv, tg):
        sg = _sigmoid(gzv)
        err = x3v + sg * ppv - tg
        part = 0.5 * jnp.sum(jnp.mean(err * err, axis=-1, keepdims=True))
        dx4 = err * (1.0 / D_MODEL)
        return dx4, dx4 * ppv * sg * (1.0 - sg), dx4 * sg, jnp.full((1, LANES), part, F32)
    dx4, dgz, dpp, loss_row = _rowwise("ple_loss", f_head, [x3, gz, pp, target], [],
                                       [(D_MODEL, F32), (D_MODEL, BF16), (D_MODEL, BF16)], [(1, LANES)])
    loss = loss_row[0, 0]

    gs, gm = {}, {}
    row_blocks = lambda g: g.reshape(N_DEV, g.shape[0] // N_DEV, g.shape[1])
    dhn = _mm("ple_dhn", dgz, w_pg, "nt")
    gm["ple_gate"] = row_blocks(_mm("ple_dwgate", hn, dgz, "tn", out_dtype=BF16))
    col_blocks = lambda g: g.reshape(g.shape[0], N_DEV, g.shape[1] // N_DEV).transpose(1, 0, 2)
    gm["ple_proj"] = col_blocks(_mm("ple_dwproj", p, dpp, "tn", out_dtype=BF16))
    dx3, gs["ple_norm"] = _norm_bwd("ple_dnorm", x3, sm["ple_norm"], dhn, dx4)

    dx2, gs["ffn2_norm"], gm["ffn2_gu"], gm["ffn2_d"] = _ffn_bwd(
        "ffn2", x2, sm["ffn2_norm"], wg["ffn2_gu"], w_d2, ffn2_saved, dx3)
    tail_token = on_grads("tail", {n: gm.pop(n) for n in SCATTER_GROUPS["tail"]})

    dmerged = _mm("w_out_dmerged", dx2, w_out, "nt")
    gm["w_out"] = row_blocks(_mm("w_out_dw", merged, dx2, "tn", out_dtype=BF16))

    def merge_bwd(zg, ur, ua, dm):
        _, vjp = jax.vjp(f_merge, zg[:, :D_MODEL], zg[:, D_MODEL:], ur, ua)
        dzr, dza, dur, dua = vjp(dm)
        return jnp.concatenate([dzr, dza], axis=1), dur, dua
    dz_g, du_r, du_a = _rowwise("merge_bwd", merge_bwd, [z_g, u_r, u_a, dmerged], [],
                                [(2 * D_MODEL, BF16), (D_MODEL, BF16), (D_MODEL, BF16)])
    dy_rwkv = _mm("br_rwkv_dy", du_r, w_brr, "nt")
    dy_attn = _mm("br_attn_dy", du_a, w_bra, "nt")
    gm["br"] = jnp.concatenate([col_blocks(_mm("br_rwkv_dw", y_rwkv, du_r, "tn", out_dtype=BF16)),
                                col_blocks(_mm("br_attn_dw", y_attn, du_a, "tn", out_dtype=BF16))], axis=1)
    branch_token = on_grads("branch", {n: gm.pop(n) for n in SCATTER_GROUPS["branch"]})

    def comb_bwd(ov, lv, dyv):
        _, vjp = jax.vjp(_f_comb, *by_group(ov, lv))
        d = vjp(dyv)
        return jnp.concatenate(d[:3], axis=1), jnp.concatenate(d[3:], axis=1)
    do_all, dl_all = _rowwise("attn_comb_bwd", comb_bwd, [o_all, lse_all, dy_attn], [],
                              [(ATTN_DIM, F32), (ATTN_DIM, F32)])
    dq_all, dk_all, dv_all = _attn_bwd(qk, z, do_all, dl_all)
    dqk_raw, gs["q_norm"], gs["k_norm"] = _qk_prep_bwd(z, tab, sm["q_norm"], sm["k_norm"], dq_all, dk_all)

    def post_bwd(yv, rv, k2v, vv, gv, dv_, gnw, gnb, rk):
        _, vjp = jax.vjp(_f_post, yv, rv, k2v, vv, gv, gnw, gnb, rk)
        return vjp(dv_)
    dy, dr1, dk2a, dv1, dg, d_gnw, d_gnb, d_rk = _pairwise(
        "rwkv_post_bwd", post_bwd, post_rows + [dy_rwkv], [post_params[0] + tail_token[0, 0] + branch_token[0, 0]] + post_params[1:], 5, 3)
    gs["rwkv_gn_w"], gs["rwkv_gn_b"], gs["rwkv_r_k"] = d_gnw, d_gnb, d_rk
    dr2, ddecay, dk2b, dkkn, db, dv_rows = _wkv_bwd(*scan_in, v_rows, states, _to_v_rows(dy))
    dr2, ddecay, dk2b, dkkn, db = [u.reshape(t, RWKV_DIM) for u in (dr2, ddecay, dk2b, dkkn, db)]
    dv2 = _from_v_rows(dv_rows)

    def kk_bwd(kkv, av, dkknv, dbv, dra, drb, dva, dvb):
        _, vjp = jax.vjp(_f_kk, kkv, av)
        return (*vjp((dkknv, dbv)), dra + drb, dva + dvb)
    dkk, da, dr, dv = _pairwise("rwkv_kk_bwd", kk_bwd, [kk, a, dkkn, db, dr1, dr2, dv1, dv2], [], 4)

    def pre_bwd(kv, lov, gdv, dk2x, dk2y, dkkv, dav, ddec, dgv, w0, w2p_, a0, a2p_, g2, k_k, k_a):
        _, vjp = jax.vjp(_f_pre, kv, lov, gdv, w0, w2p_, a0, a2p_, g2, k_k, k_a)
        return vjp((dk2x + dk2y, dkkv, dav, ddec, dgv))
    lora_acc = (DECAY_LORA + ICLR_LORA, RWKV_DIM)
    dk, dlo, dgd, d_w0, d_w2p, d_a0, d_a2p, d_g2, d_kk, d_ka = _rowwise(
        "rwkv_pre_bwd", pre_bwd,
        [k, lo, gd, dk2a, dk2b, dkk, da, ddecay, dg],
        pre_params, [(RWKV_DIM, F32), (LANES, F32), (LANES, F32)],
        [(1, RWKV_DIM), lora_acc, (1, RWKV_DIM), lora_acc, (GATE_LORA, RWKV_DIM), (1, RWKV_DIM), (1, RWKV_DIM)])
    gs["rwkv_w0"], gs["rwkv_a0"], gs["rwkv_k_k"], gs["rwkv_k_a"] = d_w0, d_a0, d_kk, d_ka
    gm["lora"] = jnp.concatenate([col_blocks(d_w2p[:DECAY_LORA]), col_blocks(d_a2p[DECAY_LORA:]), col_blocks(d_g2)],
                                 axis=1).astype(BF16)
    dz_r, gs["rwkv_mu"] = _shift_bwd(z, sm["rwkv_mu"], dr, dk, dv, dlo, dgd)

    dz = jnp.concatenate([dz_r, dqk_raw, dv_all.astype(BF16), dz_g], axis=1)
    dh2 = _mmc_nt("w_in_dh", dz, wg["w_in"], 0, 0, IN_SHARD, D_MODEL)
    gm["w_in"] = _mmc_tn("w_in_dw", h2, dz, IN_SHARD)
    mixer_token = on_grads("mixer", {n: gm.pop(n) for n in SCATTER_GROUPS["mixer"]})
    dx1, gs["mix_norm"] = _norm_bwd("mix_dnorm", x1, sm["mix_norm"] + mixer_token[0, 0], dh2, dx2)

    dx0, gs["ffn1_norm"], gm["ffn1_gu"], _ = _ffn_bwd(
        "ffn1", x, sm["ffn1_norm"], wg["ffn1_gu"], w_d1, ffn1_saved, dx1,
        on_down=lambda blocks: on_grads("ffn1_down", {"ffn1_d": blocks}))
    return loss, dx0, gm, gs


def kernel(x, p, positions, ffn1_norm, ffn1_w_gate, ffn1_w_up, ffn1_w_down, mix_norm, w_in, rwkv_mu, rwkv_w0, rwkv_w2, rwkv_a0, rwkv_a2, rwkv_g2, rwkv_k_k, rwkv_k_a, rwkv_r_k, rwkv_gn_w, rwkv_gn_b, q_norm, k_norm, w_br_rwkv, w_br_attn, w_out, ffn2_norm, ffn2_w_gate, ffn2_w_up, ffn2_w_down, ple_norm, ple_w_gate, ple_w_proj, loss_target, m_ffn1_norm, m_ffn1_w_gate, m_ffn1_w_up, m_ffn1_w_down, m_mix_norm, m_w_in, m_rwkv_mu, m_rwkv_w0, m_rwkv_w2, m_rwkv_a0, m_rwkv_a2, m_rwkv_g2, m_rwkv_k_k, m_rwkv_k_a, m_rwkv_r_k, m_rwkv_gn_w, m_rwkv_gn_b, m_q_norm, m_k_norm, m_w_br_rwkv, m_w_br_attn, m_w_out, m_ffn2_norm, m_ffn2_w_gate, m_ffn2_w_up, m_ffn2_w_down, m_ple_norm, m_ple_w_gate, m_ple_w_proj, v_ffn1_norm, v_ffn1_w_gate, v_ffn1_w_up, v_ffn1_w_down, v_mix_norm, v_w_in, v_rwkv_mu, v_rwkv_w0, v_rwkv_w2, v_rwkv_a0, v_rwkv_a2, v_rwkv_g2, v_rwkv_k_k, v_rwkv_k_a, v_rwkv_r_k, v_rwkv_gn_w, v_rwkv_gn_b, v_q_norm, v_k_norm, v_w_br_rwkv, v_w_br_attn, v_w_out, v_ffn2_norm, v_ffn2_w_gate, v_ffn2_w_up, v_ffn2_w_down, v_ple_norm, v_ple_w_gate, v_ple_w_proj):
    args = locals()
    w = {n: args[n][0] for n in WEIGHTS}
    m = {n: args["m_" + n][0] for n in WEIGHTS}
    v = {n: args["v_" + n][0] for n in WEIGHTS}

    w_loc = _local_blocks(w)
    head = GATHER_GROUPS["head"]
    wg = dict(zip(head, _all_gather_hbm("gather_head", [w_loc[n] for n in head])))
    me = 4 * lax.axis_index("x") + 2 * lax.axis_index("y") + lax.axis_index("c")
    gathering, order_after = {}, wg[head[0]]
    for group in ("mid", "rest"):
        shards = [w_loc[n] for n in GATHER_GROUPS[group]]
        zones = [lax.dynamic_update_slice(lax.empty((N_DEV,) + a.shape, a.dtype), a[None], (me, 0, 0)) for a in shards]
        *gathering[group], order_after = _exchange_start("gather_start_" + group, _gather_copies, shards, zones,
                                                         order_after)

    def fetch(group, after):
        _, got = _exchange_wait("gather_wait_" + group, _gather_copies, *gathering[group], after)
        return dict(zip(GATHER_GROUPS[group], got))

    sm = {n: w[n].reshape(1, -1) for n in SMALL}
    sm["ffn1_norm"] = sm["ffn1_norm"] + order_after[0, 0]
    in_flight = {}

    def scatter_early(group, arrays):
        arrs = [arrays[n] for n in SCATTER_GROUPS[group]]
        *in_flight[group], token = _exchange_start("scatter_start_" + group, _scatter_copies, arrs,
                                                   [lax.empty(a.shape, a.dtype) for a in arrs], arrs[0])
        return token
    loss_part, dx, gm, gs = _local_step(x[0], p[0, 0], positions[0], loss_target[0], sm, wg, fetch, scatter_early)
    loss = lax.psum(loss_part, ("x", "y", "c"))
    head_names = SCATTER_GROUPS["head"]
    core = lax.axis_index("c").astype(jnp.int32).reshape(1)
    halves = _pair_swap("scatter_head_pair", [gm[n] for n in head_names])
    chip_sums = [_pair_add("scatter_head_add_" + n, gm[n], hf, core) for n, hf in zip(head_names, halves)]
    *in_flight["head"], head_token = _exchange_start(
        "scatter_start_head", _chip_copies, chip_sums, [lax.empty(a.shape, a.dtype) for a in chip_sums], halves[0])
    recv, own = {}, {}

    def arrived(group, after):
        copies = _chip_copies if group == "head" else _scatter_copies
        sent, lands = _exchange_wait("scatter_wait_" + group, copies, *in_flight[group], after)
        own.update(zip(SCATTER_GROUPS[group], sent))
        recv.update(zip(SCATTER_GROUPS[group], lands))
    for group in ("tail", "branch", "mixer", "ffn1_down"):
        arrived(group, head_token)
    small_like = [w[n] for n in SMALL]
    small_rows = 80
    gs_all = _all_gather_vmem(_pack_small([gs[n] for n in SMALL], small_rows))
    gs_sum = _sum_slots("sum_small_grads", gs_all, small_rows)

    res = {}
    early = [e for e in ADAM_PLAN if e[1] not in SCATTER_GROUPS["head"]]
    late = [e for e in ADAM_PLAN if e[1] in SCATTER_GROUPS["head"]]
    for n, src, rb, cb, cw, tr in early + late:
        if (n, src, rb, cb, cw, tr) == late[0]:
            arrived("head", res["delta", early[-1][0]])
        outs4 = _adamw_slots("adamw_" + n, recv[src], own[src], rb, cb, cw, w[n], m[n], v[n], tr, head_token)
        for tag, a in zip(("grad", "delta", "new_m", "new_v"), outs4):
            res[tag, n] = a[None]
    d_s, m_s, v_s = _adamw("adamw_small", _pack_small(small_like, small_rows), gs_sum,
                           _pack_small([m[n] for n in SMALL], small_rows),
                           _pack_small([v[n] for n in SMALL], small_rows), small_rows)
    for tag, small in (("grad", gs_sum), ("delta", d_s), ("new_m", m_s), ("new_v", v_s)):
        for n, a in zip(SMALL, _unpack_small(small, small_like)):
            res[tag, n] = a[None]
    outs = [loss, dx[None]]
    for tag in ("grad", "delta", "new_m", "new_v"):
        outs += [res[tag, n] for n in WEIGHTS]
    return tuple(outs)
```

```python
import functools

import jax
import jax.numpy as jnp
from jax import lax
from jax.experimental import pallas as pl
from jax.experimental.pallas import tpu as pltpu

F32, BF16 = jnp.float32, jnp.bfloat16
MESH = pl.DeviceIdType.MESH
N_DEV = 8
LANES = 128
VMEM_LIMIT = 56 * 1024 * 1024

D_MODEL = 1024
PLE_DIM = 256
HEAD_DIM = 64
RWKV_HEADS = 8
RWKV_DIM = RWKV_HEADS * HEAD_DIM
DECAY_LORA = 64
ICLR_LORA = 64
GATE_LORA = 128
GN_EPS = 64e-5
ATTN_GROUPS = ((128, 1), (512, 4), (2048, 16))
HEADS_PER_GROUP = 4
ATTN_HEADS = HEADS_PER_GROUP * len(ATTN_GROUPS)
ATTN_DIM = ATTN_HEADS * HEAD_DIM
BAND_BLOCK = 128
ROPE_THETA = 10000.0
NEG_INF = -1e30
D_FF = 2816
RMS_EPS = 1e-6
RWKV_COLS = 3 * RWKV_DIM + DECAY_LORA + ICLR_LORA + GATE_LORA
ADAM_LR = 0.001
ADAM_B1 = 0.9
ADAM_B2 = 0.999
ADAM_EPS = 1e-08
ADAM_WD = 0.01
ADAM_STEP = 10

V_LO = LANES // RWKV_HEADS
V_HI = HEAD_DIM // V_LO
SCAN_CHUNK = 64
MM_ROWS = 2048

FF_SHARD = D_FF // N_DEV
FF_PAD = -(-FF_SHARD // LANES) * LANES
FF_HID = N_DEV * FF_PAD
IN_SHARD = 6144 // N_DEV
OUT_SHARD = D_MODEL // N_DEV

SMALL = ("ffn1_norm", "mix_norm", "rwkv_mu", "rwkv_w0", "rwkv_a0", "rwkv_k_k", "rwkv_k_a", "rwkv_r_k",
         "rwkv_gn_w", "rwkv_gn_b", "q_norm", "k_norm", "ffn2_norm", "ple_norm")
WEIGHTS = ("ffn1_norm", "ffn1_w_gate", "ffn1_w_up", "ffn1_w_down", "mix_norm", "w_in", "rwkv_mu", "rwkv_w0",
           "rwkv_w2", "rwkv_a0", "rwkv_a2", "rwkv_g2", "rwkv_k_k", "rwkv_k_a", "rwkv_r_k", "rwkv_gn_w",
           "rwkv_gn_b", "q_norm", "k_norm", "w_br_rwkv", "w_br_attn", "w_out", "ffn2_norm", "ffn2_w_gate",
           "ffn2_w_up", "ffn2_w_down", "ple_norm", "ple_w_gate", "ple_w_proj")


def _cparams(**kw):
    return pltpu.CompilerParams(vmem_limit_bytes=VMEM_LIMIT, **kw)


def _tile(n, cap):
    best = None
    for t in range(LANES, min(n, cap) + 1, LANES):
        if n % t == 0:
            best = t
    return best if best is not None else n


@jax.custom_vjp
def _bdot(a, w):
    return jnp.dot(a.astype(BF16), w.astype(BF16), preferred_element_type=F32)


def _bdot_fwd(a, w):
    return _bdot(a, w), (a, w)


def _bdot_bwd(res, g):
    a, w = res
    gb = g.astype(BF16)
    da = lax.dot_general(gb, w.astype(BF16), (((1,), (1,)), ((), ())), preferred_element_type=F32)
    dw = lax.dot_general(a.astype(BF16), gb, (((0,), (0,)), ((), ())), preferred_element_type=F32)
    return da.astype(a.dtype), dw.astype(w.dtype)


_bdot.defvjp(_bdot_fwd, _bdot_bwd)


@jax.custom_vjp
def _bdot_nt(a, b):
    return lax.dot_general(a.astype(BF16), b.astype(BF16), (((1,), (1,)), ((), ())), preferred_element_type=F32)


def _bdot_nt_fwd(a, b):
    return _bdot_nt(a, b), (a, b)


def _bdot_nt_bwd(res, g):
    a, b = res
    gb = g.astype(BF16)
    da = jnp.dot(gb, b.astype(BF16), preferred_element_type=F32)
    db = lax.dot_general(gb, a.astype(BF16), (((0,), (0,)), ((), ())), preferred_element_type=F32)
    return da.astype(a.dtype), db.astype(b.dtype)


_bdot_nt.defvjp(_bdot_nt_fwd, _bdot_nt_bwd)


def _mm(name, a, b, mode, out_dtype=F32, res=None, scale=None):
    if mode == "nn":
        (m, k), n = a.shape, b.shape[1]
    elif mode == "nt":
        (m, k), n = a.shape, b.shape[0]
    else:
        (k, m), n = a.shape, b.shape[1]
    tm, tn = _tile(m, 512 if mode == "tn" else MM_ROWS), _tile(n, 512)
    a_spec = pl.BlockSpec((k, tm), lambda i, j: (0, i)) if mode == "tn" else pl.BlockSpec((tm, k), lambda i, j: (i, 0))
    b_spec = pl.BlockSpec((tn, k), lambda i, j: (j, 0)) if mode == "nt" else pl.BlockSpec((k, tn), lambda i, j: (0, j))
    dims = {"nn": ((1,), (0,)), "nt": ((1,), (1,)), "tn": ((0,), (0,))}[mode]
    o_spec = pl.BlockSpec((tm, tn), lambda i, j: (i, j))
    ins, in_specs = [a, b], [a_spec, b_spec]
    if res is not None:
        ins.append(res)
        in_specs.append(o_spec)

    def body(*refs):
        acc = lax.dot_general(refs[0][...].astype(BF16), refs[1][...].astype(BF16), (dims, ((), ())),
                              preferred_element_type=F32)
        if scale is not None:
            acc = acc * scale
        if res is not None:
            acc = acc + refs[2][...].astype(F32)
        refs[-1][...] = acc.astype(refs[-1].dtype)

    return pl.pallas_call(
        body, name=name, grid=(m // tm, n // tn), in_specs=in_specs, out_specs=o_spec,
        out_shape=jax.ShapeDtypeStruct((m, n), out_dtype),
        compiler_params=_cparams(dimension_semantics=("parallel", "parallel")),
    )(*ins)


def _mmc_nn(name, a, wb, ki, ci, n, out_dtype=F32):
    m, k = a.shape
    tm = _tile(m, MM_ROWS)

    def body(a_ref, w_ref, o_ref):
        o_ref[...] = jnp.dot(a_ref[...].astype(BF16), w_ref[...], preferred_element_type=F32).astype(o_ref.dtype)

    return pl.pallas_call(
        body, name=name, grid=(m // tm, N_DEV),
        in_specs=[pl.BlockSpec((tm, k), lambda i, j: (i, 0)), pl.BlockSpec((None, k, n), lambda i, j: (j, ki, ci))],
        out_specs=pl.BlockSpec((tm, n), lambda i, j: (i, j)),
        out_shape=jax.ShapeDtypeStruct((m, N_DEV * n), out_dtype),
        compiler_params=_cparams(dimension_semantics=("parallel", "parallel")),
    )(a, wb)


def _mmc_nt(name, a, wb, ki, ci, n, k, res=None):
    m = a.shape[0]
    tm = _tile(m, MM_ROWS)
    o_spec = pl.BlockSpec((tm, k), lambda i, j: (i, 0))
    ins = [a, wb] + ([res] if res is not None else [])
    in_specs = [pl.BlockSpec((tm, n), lambda i, j: (i, j)), pl.BlockSpec((None, k, n), lambda i, j: (j, ki, ci))]
    in_specs += [o_spec] if res is not None else []

    def body(*refs):
        a_ref, w_ref, o_ref = refs[0], refs[1], refs[-1]
        acc = lax.dot_general(a_ref[...].astype(BF16), w_ref[...], (((1,), (1,)), ((), ())),
                              preferred_element_type=F32)

        @pl.when(pl.program_id(1) == 0)
        def _():
            o_ref[...] = acc + refs[2][...] if res is not None else acc

        @pl.when(pl.program_id(1) != 0)
        def _():
            o_ref[...] += acc

    return pl.pallas_call(
        body, name=name, grid=(m // tm, N_DEV), in_specs=in_specs, out_specs=o_spec,
        out_shape=jax.ShapeDtypeStruct((m, k), F32),
        compiler_params=_cparams(dimension_semantics=("parallel", "arbitrary")),
    )(*ins)


def _mmc_tn(name, x, dy, n):
    m, k = x.shape
    tk = _tile(k, 1024)

    def body(x_ref, dy_ref, o_ref):
        o_ref[...] = lax.dot_general(x_ref[...].astype(BF16), dy_ref[...].astype(BF16), (((0,), (0,)), ((), ())),
                                     preferred_element_type=F32).astype(o_ref.dtype)

    return pl.pallas_call(
        body, name=name, grid=(N_DEV, k // tk),
        in_specs=[pl.BlockSpec((m, tk), lambda j, i: (0, i)), pl.BlockSpec((m, n), lambda j, i: (0, j))],
        out_specs=pl.BlockSpec((None, tk, n), lambda j, i: (j, i, 0)),
        out_shape=jax.ShapeDtypeStruct((N_DEV, k, n), BF16),
        compiler_params=_cparams(dimension_semantics=("parallel", "parallel")),
    )(x, dy)


def _rowwise(name, fn, rows, params, out_rows, out_accs=(), tr=512):
    rows = [a if isinstance(a, tuple) else (a, a.shape[1], 0) for a in rows]
    r = rows[0][0].shape[0]
    in_specs = [pl.BlockSpec((tr, wd), lambda i, cb=cb: (i, cb)) for _, wd, cb in rows]
    rows = [a for a, _, _ in rows]
    in_specs += [pl.BlockSpec(p.shape, lambda i, nd=p.ndim: (0,) * nd) for p in params]
    out_shape = [jax.ShapeDtypeStruct((r, c), dt) for c, dt in out_rows]
    out_shape += [jax.ShapeDtypeStruct(s, F32) for s in out_accs]
    out_specs = [pl.BlockSpec((tr, c), lambda i: (i, 0)) for c, _ in out_rows]
    out_specs += [pl.BlockSpec(s, lambda i, nd=len(s): (0,) * nd) for s in out_accs]
    n_in, n_ro = len(rows) + len(params), len(out_rows)

    def body(*refs):
        res = fn(*[ref[...] for ref in refs[:n_in]])
        outs = refs[n_in:]
        for o, v in zip(outs[:n_ro], res[:n_ro]):
            o[...] = v.astype(o.dtype)
        for o, v in zip(outs[n_ro:], res[n_ro:]):
            _accumulate(o, v)

    return pl.pallas_call(
        body, name=name, grid=(r // tr,), in_specs=in_specs, out_specs=out_specs, out_shape=out_shape,
        compiler_params=_cparams(dimension_semantics=("arbitrary",)),
    )(*rows, *params)


def _pairwise(name, fn, rows, params, n_out, n_acc=0, tr=512):
    t, c = rows[0].shape
    tile = pl.BlockSpec((tr, 2 * HEAD_DIM), lambda p, i: (i, p))
    vec = pl.BlockSpec((1, 2 * HEAD_DIM), lambda p, i: (0, p))
    n_in = len(rows) + len(params)

    def body(*refs):
        res = fn(*[ref[...] for ref in refs[:n_in]])
        outs = refs[n_in:]
        for o, v in zip(outs[:n_out], res[:n_out]):
            o[...] = v
        first = pl.program_id(1) == 0
        for o, v in zip(outs[n_out:], res[n_out:]):
            @pl.when(first)
            def _(o=o, v=v):
                o[...] = v

            @pl.when(jnp.logical_not(first))
            def _(o=o, v=v):
                o[...] += v

    return pl.pallas_call(
        body, name=name, grid=(c // (2 * HEAD_DIM), t // tr),
        in_specs=[tile] * len(rows) + [vec] * len(params), out_specs=[tile] * n_out + [vec] * n_acc,
        out_shape=[jax.ShapeDtypeStruct((t, c), F32)] * n_out + [jax.ShapeDtypeStruct((1, c), F32)] * n_acc,
        compiler_params=_cparams(dimension_semantics=("parallel", "arbitrary")),
    )(*rows, *params)


def _accumulate(o_ref, v):
    @pl.when(pl.program_id(0) == 0)
    def _():
        o_ref[...] = v

    @pl.when(pl.program_id(0) != 0)
    def _():
        o_ref[...] += v


def _rms(x, g):
    return x * lax.rsqrt(jnp.mean(x * x, axis=-1, keepdims=True) + RMS_EPS) * g


def _sigmoid(x):
    return jax.nn.sigmoid(x)


def _softplus(x):
    return jnp.maximum(x, 0.0) + jnp.log1p(jnp.exp(-jnp.abs(x)))


def _norm_fwd(name, x, g):
    return _rowwise(name, lambda xv, gv: (_rms(xv, gv),), [x], [g], [(x.shape[1], BF16)])[0]


def _norm_bwd(name, x, g, dh, dres):
    def fn(xv, dhv, drv, gv):
        _, vjp = jax.vjp(_rms, xv, gv)
        dx, dg = vjp(dhv)
        return dx + drv, dg
    return _rowwise(name, fn, [x, dh, dres], [g], [(x.shape[1], F32)], [g.shape])


def _f_act(gate, up):
    return gate * _sigmoid(gate) * up


def _gate_up_act(name, h, w_gu):
    m, k = h.shape
    tm = _tile(m, MM_ROWS)

    def body(h_ref, w_ref, gu_ref, a_ref):
        gu = jnp.dot(h_ref[...], w_ref[...], preferred_element_type=F32)
        gu_ref[...] = gu
        a_ref[...] = _f_act(gu[:, :FF_PAD], gu[:, FF_PAD:]).astype(a_ref.dtype)

    return pl.pallas_call(
        body, name=name, grid=(m // tm, N_DEV),
        in_specs=[pl.BlockSpec((tm, k), lambda i, j: (i, 0)),
                  pl.BlockSpec((None, k, 2 * FF_PAD), lambda i, j: (j, 0, 0))],
        out_specs=[pl.BlockSpec((tm, 2 * FF_PAD), lambda i, j: (i, j)), pl.BlockSpec((tm, FF_PAD), lambda i, j: (i, j))],
        out_shape=[jax.ShapeDtypeStruct((m, N_DEV * 2 * FF_PAD), F32), jax.ShapeDtypeStruct((m, FF_HID), BF16)],
        compiler_params=_cparams(dimension_semantics=("parallel", "parallel")),
    )(h, w_gu)


def _gate_up_act_bwd(name, dout, w_down, gu, order):
    m, k = dout.shape
    tm = _tile(m, MM_ROWS)

    def body(d_ref, w_ref, gu_ref, order_ref, o_ref):
        da = 0.5 * lax.dot_general(d_ref[...].astype(BF16), w_ref[...], (((1,), (1,)), ((), ())),
                                   preferred_element_type=F32)
        guv = gu_ref[...]
        _, vjp = jax.vjp(_f_act, guv[:, :FF_PAD], guv[:, FF_PAD:])
        o_ref[...] = jnp.concatenate(vjp(da), axis=1).astype(o_ref.dtype)

    gu_spec = pl.BlockSpec((tm, 2 * FF_PAD), lambda i, j: (i, j))
    return pl.pallas_call(
        body, name=name, grid=(m // tm, N_DEV),
        in_specs=[pl.BlockSpec((tm, k), lambda i, j: (i, 0)), pl.BlockSpec((FF_PAD, k), lambda i, j: (j, 0)), gu_spec,
                  pl.BlockSpec(memory_space=pl.ANY)],
        out_specs=gu_spec, out_shape=jax.ShapeDtypeStruct((m, N_DEV * 2 * FF_PAD), BF16),
        compiler_params=_cparams(dimension_semantics=("parallel", "parallel")),
    )(dout, w_down, gu, order)


def _ffn_fwd(tag, x, norm, w_gu, w_down):
    h = _norm_fwd(tag + "_norm", x, norm)
    gu, a = _gate_up_act(tag + "_gu", h, w_gu)
    wd = w_down(a)
    out = _mm(tag + "_down", a, wd, "nn", res=x, scale=0.5)
    return out, (h, gu, a), wd


def _ffn_bwd(tag, x, norm, w_gu, w_down, saved, dout, on_down=None):
    h, gu, a = saved
    d_wdown = _mm(tag + "_dwdown", a, dout, "tn", out_dtype=BF16, scale=0.5).reshape(N_DEV, FF_PAD, D_MODEL)
    token = on_down(d_wdown) if on_down is not None else jnp.zeros((8, LANES), F32)
    dgu = _gate_up_act_bwd(tag + "_dgu", dout, w_down, gu, token)
    dh =_mmc_nt(tag + "_dh", dgu, w_gu, 0, 0, 2 * FF_PAD, D_MODEL)
    d_wgu = _mmc_tn(tag + "_dwgu", h, dgu, 2 * FF_PAD)
    dx, dnorm = _norm_bwd(tag + "_dnorm", x, norm, dh, dout)
    return dx, dnorm, d_wgu, d_wdown


def _shift_fwd(z, mu):
    t, c = z.shape[0], RWKV_COLS
    tr = 256

    def body(z_ref, zp_ref, mu_ref, r_ref, k_ref, v_ref, lo_ref, gd_ref):
        zv = z_ref[...]
        prev = zp_ref[7:8, :] * jnp.where(pl.program_id(0) == 0, 0.0, 1.0)
        row = lax.broadcasted_iota(jnp.int32, zv.shape, 0)
        zsh = jnp.where(row == 0, prev, pltpu.roll(zv, 1, 0))
        zs = zv + (zsh - zv) * mu_ref[...]
        r_ref[...] = zs[:, 0:512]
        k_ref[...] = zs[:, 512:1024]
        v_ref[...] = zs[:, 1024:1536]
        lo_ref[...] = zs[:, 1536:1664]
        gd_ref[...] = zs[:, 1664:1792]

    widths = (512, 512, 512, 128, 128)
    return pl.pallas_call(
        body, name="rwkv_shift", grid=(t // tr,),
        in_specs=[pl.BlockSpec((tr, c), lambda i: (i, 0)),
                  pl.BlockSpec((8, c), lambda i: (jnp.maximum(i * (tr // 8) - 1, 0), 0)),
                  pl.BlockSpec((1, c), lambda i: (0, 0))],
        out_specs=[pl.BlockSpec((tr, w), lambda i: (i, 0)) for w in widths],
        out_shape=[jax.ShapeDtypeStruct((t, w), F32) for w in widths],
        compiler_params=_cparams(dimension_semantics=("parallel",)),
    )(z, z, mu)


def _shift_bwd(z, mu, dr, dk, dv, dlo, dgd):
    t, c = z.shape[0], RWKV_COLS
    tr = 256
    nt = t // tr

    def body(z_ref, zp_ref, mu_ref, dr_ref, dk_ref, dv_ref, dlo_ref, dgd_ref,
             drn_ref, dkn_ref, dvn_ref, dlon_ref, dgdn_ref, dz_ref, dmu_ref):
        i = pl.program_id(0)
        zv, muv = z_ref[...], mu_ref[...]
        prev = zp_ref[7:8, :] * jnp.where(i == 0, 0.0, 1.0)
        row = lax.broadcasted_iota(jnp.int32, zv.shape, 0)
        zsh = jnp.where(row == 0, prev, pltpu.roll(zv, 1, 0))
        dzs = jnp.concatenate([dr_ref[...], dk_ref[...], dv_ref[...], dlo_ref[...], dgd_ref[...]], axis=1)
        nxt = jnp.concatenate([drn_ref[0:1, :], dkn_ref[0:1, :], dvn_ref[0:1, :], dlon_ref[0:1, :],
                               dgdn_ref[0:1, :]], axis=1) * jnp.where(i == nt - 1, 0.0, 1.0)
        u = dzs * muv
        un = jnp.where(row == tr - 1, nxt * muv, pltpu.roll(u, tr - 1, 0))
        dz_ref[...] = (dzs - u + un).astype(dz_ref.dtype)
        _accumulate(dmu_ref, jnp.sum(dzs * (zsh - zv), axis=0, keepdims=True))

    widths = (512, 512, 512, 128, 128)
    nxt_map = lambda i: (jnp.minimum((i + 1) * (tr // 8), t // 8 - 1), 0)
    return pl.pallas_call(
        body, name="rwkv_shift_bwd", grid=(nt,),
        in_specs=[pl.BlockSpec((tr, c), lambda i: (i, 0)),
                  pl.BlockSpec((8, c), lambda i: (jnp.maximum(i * (tr // 8) - 1, 0), 0)),
                  pl.BlockSpec((1, c), lambda i: (0, 0))]
        + [pl.BlockSpec((tr, w), lambda i: (i, 0)) for w in widths]
        + [pl.BlockSpec((8, w), nxt_map) for w in widths],
        out_specs=[pl.BlockSpec((tr, c), lambda i: (i, 0)), pl.BlockSpec((1, c), lambda i: (0, 0))],
        out_shape=[jax.ShapeDtypeStruct((t, c), BF16), jax.ShapeDtypeStruct((1, c), F32)],
        compiler_params=_cparams(dimension_semantics=("arbitrary",)),
    )(z, z, mu, dr, dk, dv, dlo, dgd, dr, dk, dv, dlo, dgd)


def _f_pre(k, lo, gd, w0, w2p, a0, a2p, g2, k_k, k_a):
    lane = lax.broadcasted_iota(jnp.int32, lo.shape, 1)
    lo_act = jnp.where(lane < DECAY_LORA, jnp.tanh(lo), lo)
    w = -_softplus(-(w0 + _bdot(lo_act, w2p))) - 0.5
    a = _sigmoid(a0 + _bdot(lo_act, a2p))
    g = _bdot(_sigmoid(gd), g2)
    kk = k * k_k
    k2 = k * (1.0 + (a - 1.0) * k_a)
    decay = jnp.exp(-jnp.exp(w))
    return k2, kk, a, decay, g


def _f_kk(kk, a):
    kkn = kk * lax.rsqrt(jnp.maximum(_head_sums(kk * kk), 1e-24))
    return kkn, kkn * a


def _f_post(y, r, k2, v, g, gn_w, gn_b, r_k):
    mean = _head_sums(y) * (1.0 / HEAD_DIM)
    var = _head_sums(jnp.square(y - mean)) * (1.0 / HEAD_DIM)
    yn = (y - mean) * lax.rsqrt(var + GN_EPS) * gn_w + gn_b
    bonus = _head_sums(r * k2 * r_k) * v
    return (yn + bonus) * g


def _to_v_rows(x):
    t = x.shape[0]
    return x.reshape(t, RWKV_HEADS, V_HI, V_LO).transpose(0, 2, 3, 1).reshape(t, V_HI, LANES)


def _from_v_rows(x):
    t = x.shape[0]
    return x.reshape(t, V_HI, V_LO, RWKV_HEADS).transpose(0, 3, 1, 2).reshape(t, RWKV_DIM)


def _k_cols(x):
    return jnp.tile(x, (V_LO, 1)).T


def _k_rows(x):
    xt = x.T
    out = xt[0:RWKV_HEADS]
    for l in range(1, V_LO):
        out = out + xt[l * RWKV_HEADS:(l + 1) * RWKV_HEADS]
    return out


def _wkv_fwd(r, w, k, kk, b, v):
    t = r.shape[0]
    tc = SCAN_CHUNK
    key_spec = pl.BlockSpec((tc, RWKV_HEADS, HEAD_DIM), lambda i: (i, 0, 0))
    row_spec = pl.BlockSpec((tc, V_HI, LANES), lambda i: (i, 0, 0))

    def body(r_ref, w_ref, k_ref, kk_ref, b_ref, v_ref, y_ref, st_ref, s_scr, cols_a, cols_b):
        @pl.when(pl.program_id(0) == 0)
        def _():
            s_scr[...] = jnp.zeros_like(s_scr)

        def prep(ti, buf):
            for n, ref in enumerate((r_ref, w_ref, k_ref, kk_ref, b_ref)):
                buf[n] = _k_cols(ref[ti])

        def step(ti, s, cur, nxt, ti_next):
            rc, wc, kc, kkc, bc = (cur[n] for n in range(5))
            prep(ti_next, nxt)
            vt = v_ref[ti]
            new, ys = [], []
            for j in range(V_HI):
                sa = -jnp.sum(s[j] * kkc, axis=0, keepdims=True)
                nj = s[j] * wc + bc * sa + kc * vt[j:j + 1]
                st_ref[ti, j] = nj
                ys.append(jnp.sum(nj * rc, axis=0, keepdims=True))
                new.append(nj)
            y_ref[ti] = jnp.concatenate(ys, axis=0)
            return tuple(new)

        def pair(i, s):
            s = step(2 * i, s, cols_a, cols_b, 2 * i + 1)
            return step(2 * i + 1, s, cols_b, cols_a, jnp.minimum(2 * i + 2, tc - 1))

        prep(0, cols_a)
        s = lax.fori_loop(0, tc // 2, pair, tuple(s_scr[j] for j in range(V_HI)))
        for j in range(V_HI):
            s_scr[j] = s[j]

    return pl.pallas_call(
        body, name="wkv_fwd", grid=(t // tc,),
        in_specs=[key_spec] * 5 + [row_spec],
        out_specs=[row_spec, pl.BlockSpec((tc, V_HI, HEAD_DIM, LANES), lambda i: (i, 0, 0, 0))],
        out_shape=[jax.ShapeDtypeStruct((t, V_HI, LANES), F32),
                   jax.ShapeDtypeStruct((t, V_HI, HEAD_DIM, LANES), F32)],
        scratch_shapes=[pltpu.VMEM((V_HI, HEAD_DIM, LANES), F32)] + [pltpu.VMEM((5, HEAD_DIM, LANES), F32)] * 2,
        compiler_params=_cparams(dimension_semantics=("arbitrary",)),
    )(r, w, k, kk, b, v)


def _wkv_bwd(r, w, k, kk, b, v, states, dy):
    t = r.shape[0]
    tc = SCAN_CHUNK
    nb = t // tc
    key_spec = pl.BlockSpec((tc, RWKV_HEADS, HEAD_DIM), lambda i: (nb - 1 - i, 0, 0))
    row_spec = pl.BlockSpec((tc, V_HI, LANES), lambda i: (nb - 1 - i, 0, 0))
    st_spec = pl.BlockSpec((tc, V_HI, HEAD_DIM, LANES), lambda i: (nb - 1 - i, 0, 0, 0))
    stp_spec = pl.BlockSpec((1, V_HI, HEAD_DIM, LANES), lambda i: (jnp.maximum((nb - 1 - i) * tc - 1, 0), 0, 0, 0))

    def body(r_ref, w_ref, k_ref, kk_ref, b_ref, v_ref, st_ref, stp_ref, dy_ref,
             dr_ref, dw_ref, dk_ref, dkk_ref, db_ref, dv_ref, ds_scr, cols_a, cols_b, accs_a, accs_b):
        @pl.when(pl.program_id(0) == 0)
        def _():
            ds_scr[...] = jnp.zeros_like(ds_scr)

        def colsum(x):
            return jnp.sum(x, axis=0, keepdims=True)

        def prep(ti, buf):
            for n, ref in enumerate((r_ref, w_ref, k_ref, kk_ref, b_ref)):
                buf[n] = _k_cols(ref[ti])

        def flush(ti, buf):
            for n, ref in enumerate((dr_ref, dk_ref, db_ref, dw_ref, dkk_ref)):
                ref[ti] = _k_rows(buf[n])

        def step(ti, ds, sp, cur, accs):
            rc, wc, kc, kkc, bc = (cur[n] for n in range(5))
            vt, dyt = v_ref[ti], dy_ref[ti]
            acc = None
            new, dvs = [], []
            for j in range(V_HI):
                st = st_ref[ti, j]
                dsj = ds[j] + rc * dyt[j:j + 1]
                sa = -colsum(sp[j] * kkc)
                dsa = colsum(dsj * bc)
                dvs.append(colsum(dsj * kc))
                parts = (st * dyt[j:j + 1], dsj * vt[j:j + 1], dsj * sa, dsj * sp[j], -(sp[j] * dsa))
                acc = parts if acc is None else tuple(a + q for a, q in zip(acc, parts))
                new.append(dsj * wc - kkc * dsa)
            dv_ref[ti] = jnp.concatenate(dvs, axis=0)
            for n in range(5):
                accs[n] = acc[n]
            return tuple(new)

        def states_before(ti):
            return tuple(st_ref[ti - 1, j] for j in range(V_HI))

        def pair(i, ds):
            ta = tc - 1 - 2 * i
            prep(ta - 1, cols_b)
            flush(jnp.minimum(ta + 1, tc - 1), accs_b)
            ds = step(ta, ds, states_before(ta), cols_a, accs_a)
            prep(ta - 2, cols_a)
            flush(ta, accs_a)
            return step(ta - 1, ds, states_before(ta - 1), cols_b, accs_b)

        prep(tc - 1, cols_a)
        accs_b[...] = jnp.zeros_like(accs_b)
        ds = lax.fori_loop(0, tc // 2 - 1, pair, tuple(ds_scr[j] for j in range(V_HI)))
        prep(0, cols_b)
        flush(2, accs_b)
        ds = step(1, ds, states_before(1), cols_a, accs_a)
        flush(1, accs_a)
        keep = jnp.where(pl.program_id(0) == nb - 1, 0.0, 1.0)
        ds = step(0, ds, tuple(stp_ref[0, j] * keep for j in range(V_HI)), cols_b, accs_b)
        flush(0, accs_b)
        for j in range(V_HI):
            ds_scr[j] = ds[j]

    key_out = jax.ShapeDtypeStruct((t, RWKV_HEADS, HEAD_DIM), F32)
    return pl.pallas_call(
        body, name="wkv_bwd", grid=(nb,),
        in_specs=[key_spec] * 5 + [row_spec, st_spec, stp_spec, row_spec],
        out_specs=[key_spec] * 5 + [row_spec],
        out_shape=[key_out] * 5 + [jax.ShapeDtypeStruct((t, V_HI, LANES), F32)],
        scratch_shapes=[pltpu.VMEM((V_HI, HEAD_DIM, LANES), F32)] + [pltpu.VMEM((5, HEAD_DIM, LANES), F32)] * 4,
        compiler_params=_cparams(dimension_semantics=("arbitrary",)),
    )(r, w, k, kk, b, v, states, states, dy)


PAIR = 2 * HEAD_DIM
N_PAIRS = ATTN_HEADS // 2
Q_COL0 = RWKV_COLS // PAIR
K_COL0 = Q_COL0 + N_PAIRS
V_COL0 = K_COL0 + N_PAIRS


def _swap_halves(x):
    lane = lax.broadcasted_iota(jnp.int32, x.shape, 1)
    return jnp.where((lane & (HEAD_DIM - 1)) < HEAD_DIM // 2, pltpu.roll(x, PAIR - HEAD_DIM // 2, 1),
                     pltpu.roll(x, HEAD_DIM // 2, 1))


@jax.custom_vjp
def _rope(x, cosf, sinf):
    return x * cosf + _swap_halves(x) * sinf


def _rope_fwd(x, cosf, sinf):
    return _rope(x, cosf, sinf), (cosf, sinf)


def _rope_bwd(res, d):
    cosf, sinf = res
    return d * cosf + _swap_halves(d * sinf), jnp.zeros_like(cosf), jnp.zeros_like(sinf)


_rope.defvjp(_rope_fwd, _rope_bwd)


def _head_sums(x):
    lane = lax.broadcasted_iota(jnp.int32, x.shape, 1)
    lo = jnp.where(lane < HEAD_DIM, 1.0, 0.0)
    hi = 1.0 - lo
    return lo * jnp.sum(x * lo, axis=1, keepdims=True) + hi * jnp.sum(x * hi, axis=1, keepdims=True)


def _f_qk(x, cosf, sinf, gain2):
    xn = x * lax.rsqrt(_head_sums(x * x) * (1.0 / HEAD_DIM) + RMS_EPS) * gain2
    return _rope(xn, cosf, sinf)


def _qk_prep(z, tab, q_gain, k_gain):
    t = z.shape[0]
    tr = 1024

    def body(z_ref, c_ref, s_ref, qg_ref, kg_ref, o_ref):
        g = jnp.where(pl.program_id(0) < N_PAIRS, qg_ref[...], kg_ref[...])
        o_ref[...] = _f_qk(z_ref[...], c_ref[...], s_ref[...], jnp.concatenate([g, g], axis=1))

    gain = pl.BlockSpec((1, HEAD_DIM), lambda c, i: (0, 0))
    return pl.pallas_call(
        body, name="qk_prep", grid=(2 * N_PAIRS, t // tr),
        in_specs=[pl.BlockSpec((tr, PAIR), lambda c, i: (i, Q_COL0 + c)), pl.BlockSpec((tr, PAIR), lambda c, i: (i, 0)),
                  pl.BlockSpec((tr, PAIR), lambda c, i: (i, 1)), gain, gain],
        out_specs=pl.BlockSpec((tr, PAIR), lambda c, i: (i, c)),
        out_shape=jax.ShapeDtypeStruct((t, 2 * N_PAIRS * PAIR), F32),
        compiler_params=_cparams(dimension_semantics=("parallel", "parallel")),
    )(z, tab, tab, q_gain, k_gain)


def _qk_prep_bwd(z, tab, q_gain, k_gain, dq, dk):
    t = z.shape[0]
    tr = 1024

    def body(z_ref, c_ref, s_ref, qg_ref, kg_ref, dq_ref, dk_ref, dz_ref, dqg_ref, dkg_ref):
        c, i = pl.program_id(0), pl.program_id(1)
        is_q = c < N_PAIRS
        g = jnp.where(is_q, qg_ref[...], kg_ref[...])
        d = jnp.where(is_q, dq_ref[...], dk_ref[...])
        _, vjp = jax.vjp(lambda xx, gg: _f_qk(xx, c_ref[...], s_ref[...], gg), z_ref[...],
                         jnp.concatenate([g, g], axis=1))
        dx, dg2 = vjp(d)
        dz_ref[...] = dx.astype(dz_ref.dtype)
        dg = dg2[:, :HEAD_DIM] + dg2[:, HEAD_DIM:]
        first_q = jnp.logical_and(c == 0, i == 0)
        first_k = jnp.logical_and(c == N_PAIRS, i == 0)

        @pl.when(first_q)
        def _():
            dqg_ref[...] = dg

        @pl.when(jnp.logical_and(is_q, jnp.logical_not(first_q)))
        def _():
            dqg_ref[...] += dg

        @pl.when(first_k)
        def _():
            dkg_ref[...] = dg

        @pl.when(jnp.logical_and(jnp.logical_not(is_q), jnp.logical_not(first_k)))
        def _():
            dkg_ref[...] += dg

    gain = pl.BlockSpec((1, HEAD_DIM), lambda c, i: (0, 0))
    return pl.pallas_call(
        body, name="qk_prep_bwd", grid=(2 * N_PAIRS, t // tr),
        in_specs=[pl.BlockSpec((tr, PAIR), lambda c, i: (i, Q_COL0 + c)), pl.BlockSpec((tr, PAIR), lambda c, i: (i, 0)),
                  pl.BlockSpec((tr, PAIR), lambda c, i: (i, 1)), gain, gain,
                  pl.BlockSpec((tr, PAIR), lambda c, i: (i, jnp.minimum(c, N_PAIRS - 1))),
                  pl.BlockSpec((tr, PAIR), lambda c, i: (i, jnp.maximum(c - N_PAIRS, 0)))],
        out_specs=[pl.BlockSpec((tr, PAIR), lambda c, i: (i, c)), gain, gain],
        out_shape=[jax.ShapeDtypeStruct((t, 2 * N_PAIRS * PAIR), BF16), jax.ShapeDtypeStruct((1, HEAD_DIM), F32),
                   jax.ShapeDtypeStruct((1, HEAD_DIM), F32)],
        compiler_params=_cparams(dimension_semantics=("arbitrary", "arbitrary")),
    )(z, tab, tab, q_gain, k_gain, dq, dk)


def _attn_block(q, kp, kc, vp, vc, kmin):
    k2 = jnp.concatenate([kp, kc], axis=0)
    v2 = jnp.concatenate([vp, vc], axis=0)
    s = _bdot_nt(q, k2) * (HEAD_DIM ** -0.5)
    qi = lax.broadcasted_iota(jnp.int32, s.shape, 0)
    kj = lax.broadcasted_iota(jnp.int32, s.shape, 1)
    dist = qi + BAND_BLOCK - kj
    valid = (dist >= 0) & (dist <= BAND_BLOCK) & (kj >= kmin)
    s = jnp.where(valid, s, NEG_INF)
    m = lax.stop_gradient(jnp.max(s, axis=-1, keepdims=True))
    e = jnp.exp(s - m)
    l = jnp.sum(e, axis=-1, keepdims=True)
    o = _bdot(e, v2) / l
    return o, m + jnp.log(l)


def _fold(src_ref, dst_ref, dil):
    t = src_ref.shape[0]
    ln = t // dil
    for j in range(dil):
        dst_ref[j * ln:(j + 1) * ln, :] = src_ref[pl.ds(j, ln, stride=dil), :]


def _unfold(src_ref, dst_ref, dil):
    t = src_ref.shape[0]
    ln = t // dil
    for j in range(dil):
        dst_ref[pl.ds(j, ln, stride=dil), :] = src_ref[j * ln:(j + 1) * ln, :]


def _per_group(fn):
    pair = pl.program_id(0)
    for gi, (_, dil) in enumerate(ATTN_GROUPS):
        @pl.when(jnp.logical_or(pair == 2 * gi, pair == 2 * gi + 1))
        def _(dil=dil):
            fn(dil)


def _block_rows(idx, blocks_per_seq):
    first = (idx & (blocks_per_seq - 1)) == 0
    cur = pl.ds(pl.multiple_of(idx * BAND_BLOCK, BAND_BLOCK), BAND_BLOCK)
    prev = pl.ds(pl.multiple_of(jnp.maximum(idx - 1, 0) * BAND_BLOCK, BAND_BLOCK), BAND_BLOCK)
    return first, cur, prev


def _heads(x):
    return x[:, :HEAD_DIM], x[:, HEAD_DIM:]


def _attn_fwd(qk, z):
    t = z.shape[0]
    n_blocks = t // BAND_BLOCK

    def body(q_ref, k_ref, v_ref, o_ref, lse_ref, qf, kf, vf, of, lf):
        def run(dil):
            _fold(q_ref, qf, dil)
            _fold(k_ref, kf, dil)
            _fold(v_ref, vf, dil)
            blocks_per_seq = n_blocks // dil

            def block(idx, carry):
                first, cur, prev = _block_rows(idx, blocks_per_seq)
                kmin = jnp.where(first, BAND_BLOCK, 0)
                outs, lses = [], []
                for q, kp, kc, vp, vc in zip(_heads(qf[cur, :]), _heads(kf[prev, :]), _heads(kf[cur, :]),
                                             _heads(vf[prev, :]), _heads(vf[cur, :])):
                    o, ls = _attn_block(q, kp, kc, vp, vc, kmin)
                    outs.append(o)
                    lses.append(jnp.broadcast_to(ls, o.shape))
                of[cur, :] = jnp.concatenate(outs, axis=1)
                lf[cur, :] = jnp.concatenate(lses, axis=1)
                return carry

            lax.fori_loop(0, n_blocks, block, 0, unroll=4)
            _unfold(of, o_ref, dil)
            _unfold(lf, lse_ref, dil)

        _per_group(run)

    slab = jax.ShapeDtypeStruct((t, N_PAIRS * PAIR), F32)
    out_spec = pl.BlockSpec((t, PAIR), lambda p: (0, p))
    return pl.pallas_call(
        body, name="attn_fwd", grid=(N_PAIRS,),
        in_specs=[pl.BlockSpec((t, PAIR), lambda p: (0, p)), pl.BlockSpec((t, PAIR), lambda p: (0, N_PAIRS + p)),
                  pl.BlockSpec((t, PAIR), lambda p: (0, V_COL0 + p))],
        out_specs=[out_spec, out_spec], out_shape=[slab, slab],
        scratch_shapes=[pltpu.VMEM((t, PAIR), F32)] * 5,
        compiler_params=_cparams(dimension_semantics=("parallel",)),
    )(qk, qk, z)


def _attn_bwd(qk, z, do, dlse):
    t = z.shape[0]
    n_blocks = t // BAND_BLOCK

    def body(q_ref, k_ref, v_ref, do_ref, dl_ref, dq_ref, dk_ref, dv_ref, qf, kf, vf, dof, dlf, dqf, dkf, dvf,
             dkpf, dvpf):
        def run(dil):
            for src, dst in ((q_ref, qf), (k_ref, kf), (v_ref, vf), (do_ref, dof), (dl_ref, dlf)):
                _fold(src, dst, dil)
            blocks_per_seq = n_blocks // dil

            def block(idx, carry):
                first, cur, prev = _block_rows(idx, blocks_per_seq)
                kmin = jnp.where(first, BAND_BLOCK, 0)
                grads = []
                for q, kp, kc, vp, vc, do_h, dl_h in zip(
                        _heads(qf[cur, :]), _heads(kf[prev, :]), _heads(kf[cur, :]), _heads(vf[prev, :]),
                        _heads(vf[cur, :]), _heads(dof[cur, :]), _heads(dlf[cur, :])):
                    _, vjp = jax.vjp(functools.partial(_attn_block, kmin=kmin), q, kp, kc, vp, vc)
                    grads.append(vjp((do_h, jnp.sum(dl_h, axis=1, keepdims=True))))
                dq, dkp, dkc, dvp, dvc = (jnp.concatenate([a, b], axis=1) for a, b in zip(*grads))
                dqf[cur, :], dkf[cur, :], dvf[cur, :], dkpf[cur, :], dvpf[cur, :] = dq, dkc, dvc, dkp, dvp
                return carry

            lax.fori_loop(0, n_blocks, block, 0, unroll=2)

            def join(idx, carry):
                first, cur, prev = _block_rows(idx, blocks_per_seq)

                @pl.when(jnp.logical_not(first))
                def _():
                    dkf[prev, :] += dkpf[cur, :]
                    dvf[prev, :] += dvpf[cur, :]

                return carry

            lax.fori_loop(0, n_blocks, join, 0)
            _unfold(dqf, dq_ref, dil)
            _unfold(dkf, dk_ref, dil)
            _unfold(dvf, dv_ref, dil)

        _per_group(run)

    slab = jax.ShapeDtypeStruct((t, N_PAIRS * PAIR), F32)
    own = pl.BlockSpec((t, PAIR), lambda p: (0, p))
    return pl.pallas_call(
        body, name="attn_bwd", grid=(N_PAIRS,),
        in_specs=[own, pl.BlockSpec((t, PAIR), lambda p: (0, N_PAIRS + p)),
                  pl.BlockSpec((t, PAIR), lambda p: (0, V_COL0 + p)), own, own],
        out_specs=[own] * 3, out_shape=[slab] * 3,
        scratch_shapes=[pltpu.VMEM((t, PAIR), F32)] * 10,
        compiler_params=_cparams(dimension_semantics=("parallel",)),
    )(qk, qk, z, do, dlse)


def _f_comb(o1, o2, o3, l1, l2, l3):
    m = jnp.maximum(jnp.maximum(l1, l2), l3)
    e1, e2, e3 = jnp.exp(l1 - m), jnp.exp(l2 - m), jnp.exp(l3 - m)
    den = e1 + e2 + e3
    return (e1 / den) * o1 + (e2 / den) * o2 + (e3 / den) * o3


def _all_gather_hbm(name, arrs):
    na = len(arrs)

    def body(*refs):
        x_refs, out_refs = refs[:na], refs[na:2 * na]
        send_sems, recv_sems, local_sems = refs[2 * na:]
        mx, my, mc = lax.axis_index("x"), lax.axis_index("y"), lax.axis_index("c")
        me, sibling = (mx, my, mc), (mx, my, 1 - mc)
        chips = [(1 - mx, my), (mx, 1 - my), (1 - mx, 1 - my)]

        def slot(a, px, py, pc):
            return out_refs[a].at[4 * px + 2 * py + pc]

        def copy(a, k, block, to, src=None):
            return pltpu.make_async_remote_copy(
                src_ref=slot(a, *block) if src is None else src, dst_ref=slot(a, *block),
                send_sem=send_sems.at[a, k], recv_sem=recv_sems.at[a, k], device_id=to, device_id_type=MESH)

        mine = [pltpu.make_async_copy(x_refs[a], slot(a, *me), local_sems.at[a]) for a in range(na)]
        for cp in mine:
            cp.start()
        first = []
        for a in range(na):
            first.append(copy(a, 0, me, sibling, src=x_refs[a]))
            first += [copy(a, 1 + j, me, (*chip, mc), src=x_refs[a]) for j, chip in enumerate(chips)]
        for cp in first:
            cp.start()
        passed = []
        for j, chip in enumerate(chips):
            for a in range(na):
                copy(a, 1 + j, (*chip, mc), me).wait_recv()
                passed.append(copy(a, 4 + j, (*chip, mc), sibling))
                passed[-1].start()
        for a in range(na):
            copy(a, 0, sibling, me).wait_recv()
            for j, chip in enumerate(chips):
                copy(a, 4 + j, (*chip, 1 - mc), me).wait_recv()
        for cp in first + passed:
            cp.wait_send()
        for cp in mine:
            cp.wait()

    hbm = pl.BlockSpec(memory_space=pl.ANY)
    return pl.pallas_call(
        body, name=name,
        out_shape=[jax.ShapeDtypeStruct((N_DEV,) + a.shape, a.dtype) for a in arrs],
        in_specs=[hbm] * na, out_specs=[hbm] * na,
        scratch_shapes=[pltpu.SemaphoreType.DMA((na, 7)), pltpu.SemaphoreType.DMA((na, 7)),
                        pltpu.SemaphoreType.DMA((na,))],
    )(*arrs)


def _all_gather_vmem(x):
    rws, cols = x.shape

    def body(x_ref, out_ref, send_sems, recv_sems):
        mx, my, mc = lax.axis_index("x"), lax.axis_index("y"), lax.axis_index("c")
        me, sibling = (mx, my, mc), (mx, my, 1 - mc)
        chips = [(1 - mx, my), (mx, 1 - my), (1 - mx, 1 - my)]

        def slot(px, py, pc):
            return out_ref.at[4 * px + 2 * py + pc]

        def copy(k, block, to, src=None):
            return pltpu.make_async_remote_copy(
                src_ref=slot(*block) if src is None else src, dst_ref=slot(*block),
                send_sem=send_sems.at[k], recv_sem=recv_sems.at[k], device_id=to, device_id_type=MESH)

        first = [copy(0, me, sibling, src=x_ref)]
        first += [copy(1 + j, me, (*chip, mc), src=x_ref) for j, chip in enumerate(chips)]
        for cp in first:
            cp.start()
        out_ref[4 * mx + 2 * my + mc] = x_ref[...]
        passed = [copy(4 + j, (*chip, mc), sibling) for j, chip in enumerate(chips)]
        for j, chip in enumerate(chips):
            copy(1 + j, (*chip, mc), me).wait_recv()
            passed[j].start()
        copy(0, sibling, me).wait_recv()
        for j, chip in enumerate(chips):
            copy(4 + j, (*chip, 1 - mc), me).wait_recv()
        for cp in first + passed:
            cp.wait_send()

    return pl.pallas_call(
        body, name="all_gather_small",
        out_shape=jax.ShapeDtypeStruct((N_DEV, rws, cols), x.dtype),
        in_specs=[pl.BlockSpec(memory_space=pltpu.VMEM)], out_specs=pl.BlockSpec(memory_space=pltpu.VMEM),
        scratch_shapes=[pltpu.SemaphoreType.DMA((7,)), pltpu.SemaphoreType.DMA((7,))],
    )(x)


def _all_to_all_hbm(name, arrs):
    na = len(arrs)

    def body(*refs):
        g_refs, out_refs = refs[:na], refs[na:2 * na]
        send_sems, recv_sems, local_sems = refs[2 * na:]
        mx, my, mc = lax.axis_index("x"), lax.axis_index("y"), lax.axis_index("c")
        me = 4 * mx + 2 * my + mc
        mine = [pltpu.make_async_copy(g_refs[a].at[me], out_refs[a].at[me], local_sems.at[a]) for a in range(na)]
        for cp in mine:
            cp.start()
        copies = []
        for k in range(1, N_DEV):
            px, py, pc = mx ^ (k >> 2), my ^ ((k >> 1) & 1), mc ^ (k & 1)
            peer = 4 * px + 2 * py + pc
            for a in range(na):
                copies.append(pltpu.make_async_remote_copy(
                    src_ref=g_refs[a].at[peer], dst_ref=out_refs[a].at[me], send_sem=send_sems.at[a, k - 1],
                    recv_sem=recv_sems.at[a, k - 1], device_id=(px, py, pc), device_id_type=MESH))
        for cp in copies:
            cp.start()
        for cp in copies:
            cp.wait_recv()
        for cp in copies:
            cp.wait_send()
        for cp in mine:
            cp.wait()

    hbm = pl.BlockSpec(memory_space=pl.ANY)
    return pl.pallas_call(
        body, name=name,
        out_shape=[jax.ShapeDtypeStruct(a.shape, a.dtype) for a in arrs],
        in_specs=[hbm] * na, out_specs=[hbm] * na,
        scratch_shapes=[pltpu.SemaphoreType.DMA((na, 7)), pltpu.SemaphoreType.DMA((na, 7)),
                        pltpu.SemaphoreType.DMA((na,))],
    )(*arrs)


def _scatter_copies(g_refs, land_refs, send_sems, recv_sems):
    mx, my, mc = lax.axis_index("x"), lax.axis_index("y"), lax.axis_index("c")
    me = 4 * mx + 2 * my + mc
    copies = []
    for k in range(1, N_DEV):
        px, py, pc = mx ^ (k >> 2), my ^ ((k >> 1) & 1), mc ^ (k & 1)
        peer = 4 * px + 2 * py + pc
        for a, (g_ref, land_ref) in enumerate(zip(g_refs, land_refs)):
            copies.append(pltpu.make_async_remote_copy(
                src_ref=g_ref.at[peer], dst_ref=land_ref.at[me], send_sem=send_sems.at[a * (N_DEV - 1) + k - 1],
                recv_sem=recv_sems.at[a * (N_DEV - 1) + k - 1], device_id=(px, py, pc), device_id_type=MESH))
    return copies


_HBM = pl.BlockSpec(memory_space=pltpu.HBM)
_SEM = pl.BlockSpec(memory_space=pltpu.SEMAPHORE)
_DATAFLOW = pltpu.SideEffectType.DATAFLOW_SIDE_EFFECTING


def _gather_copies(x_refs, land_refs, send_sems, recv_sems):
    mx, my, mc = lax.axis_index("x"), lax.axis_index("y"), lax.axis_index("c")
    me = 4 * mx + 2 * my + mc
    copies = []
    for k in range(1, N_DEV):
        px, py, pc = mx ^ (k >> 2), my ^ ((k >> 1) & 1), mc ^ (k & 1)
        for a, (x_ref, land_ref) in enumerate(zip(x_refs, land_refs)):
            copies.append(pltpu.make_async_remote_copy(
                src_ref=x_ref, dst_ref=land_ref.at[me], send_sem=send_sems.at[a * (N_DEV - 1) + k - 1],
                recv_sem=recv_sems.at[a * (N_DEV - 1) + k - 1], device_id=(px, py, pc), device_id_type=MESH))
    return copies


N_CHIPS = N_DEV // 2


def _pair_swap(name, arrs):
    na = len(arrs)

    def body(*refs):
        g_refs, out_refs = refs[:na], refs[na:2 * na]
        send_sems, recv_sems = refs[2 * na:]
        mx, my, mc = lax.axis_index("x"), lax.axis_index("y"), lax.axis_index("c")
        copies = [pltpu.make_async_remote_copy(
            src_ref=g_refs[a].at[2 * q + 1 - mc], dst_ref=out_refs[a].at[q], send_sem=send_sems.at[a, q],
            recv_sem=recv_sems.at[a, q], device_id=(mx, my, 1 - mc), device_id_type=MESH)
            for a in range(na) for q in range(N_CHIPS)]
        for cp in copies:
            cp.start()
        for cp in copies:
            cp.wait_recv()
        for cp in copies:
            cp.wait_send()

    hbm = pl.BlockSpec(memory_space=pl.ANY)
    return pl.pallas_call(
        body, name=name,
        out_shape=[jax.ShapeDtypeStruct((N_CHIPS,) + a.shape[1:], a.dtype) for a in arrs],
        in_specs=[hbm] * na, out_specs=[hbm] * na,
        scratch_shapes=[pltpu.SemaphoreType.DMA((na, N_CHIPS)), pltpu.SemaphoreType.DMA((na, N_CHIPS))],
    )(*arrs)


def _pair_add(name, g, half, core):
    _, r, c = g.shape
    tr = _tile(r, 512) if r % LANES == 0 else r

    def body(core_ref, g_ref, h_ref, o_ref):
        o_ref[...] = (g_ref[...].astype(F32) + h_ref[...].astype(F32)).astype(o_ref.dtype)

    return pl.pallas_call(
        body, name=name,
        grid_spec=pltpu.PrefetchScalarGridSpec(
            num_scalar_prefetch=1, grid=(N_CHIPS, r // tr),
            in_specs=[pl.BlockSpec((None, tr, c), lambda q, i, core_ref: (2 * q + core_ref[0], i, 0)),
                      pl.BlockSpec((None, tr, c), lambda q, i, core_ref: (q, i, 0))],
            out_specs=pl.BlockSpec((None, tr, c), lambda q, i, core_ref: (q, i, 0))),
        out_shape=jax.ShapeDtypeStruct((N_CHIPS, r, c), g.dtype),
        compiler_params=_cparams(dimension_semantics=("parallel", "parallel")),
    )(core, g, half)


def _chip_copies(h_refs, land_refs, send_sems, recv_sems):
    mx, my, mc = lax.axis_index("x"), lax.axis_index("y"), lax.axis_index("c")
    my_chip = 2 * mx + my
    copies = []
    for k in range(1, N_CHIPS):
        px, py = mx ^ (k >> 1), my ^ (k & 1)
        for a, (h_ref, land_ref) in enumerate(zip(h_refs, land_refs)):
            copies.append(pltpu.make_async_remote_copy(
                src_ref=h_ref.at[2 * px + py], dst_ref=land_ref.at[my_chip], send_sem=send_sems.at[a * (N_DEV - 1) + k - 1],
                recv_sem=recv_sems.at[a * (N_DEV - 1) + k - 1], device_id=(px, py, mc), device_id_type=MESH))
    return copies


def _exchange_start(name, copies, srcs, lands, after):
    na = len(srcs)

    def body(*refs):
        for cp in copies(refs[:na], refs[na:2 * na], refs[2 * na + 1], refs[2 * na + 2]):
            cp.start()
        refs[-1][...] = jnp.zeros_like(refs[-1])

    in_hbm = lambda a: pltpu.with_memory_space_constraint(a, pltpu.HBM)
    outs = pl.pallas_call(
        body, name=name,
        out_shape=(pltpu.SemaphoreType.DMA((na * (N_DEV - 1),)), pltpu.SemaphoreType.DMA((na * (N_DEV - 1),)),
                   *[pltpu.HBM(a.shape, a.dtype) for a in list(srcs) + list(lands)],
                   jax.ShapeDtypeStruct((8, LANES), F32)),
        in_specs=[_HBM] * (2 * na) + [pl.BlockSpec(memory_space=pl.ANY)],
        out_specs=(_SEM, _SEM, *[_HBM] * (2 * na), pl.BlockSpec(memory_space=pltpu.VMEM)),
        input_output_aliases={i: 2 + i for i in range(2 * na)},
        compiler_params=pltpu.CompilerParams(has_side_effects=_DATAFLOW),
    )(*[in_hbm(a) for a in srcs], *[in_hbm(a) for a in lands], after)
    return outs[0], outs[1], outs[2:2 + na], outs[2 + na:2 + 2 * na], outs[-1]


def _exchange_wait(name, copies, send_sems, recv_sems, srcs, lands, after):
    na = len(srcs)

    def body(*refs):
        for cp in copies(refs[:na], refs[na:2 * na], refs[2 * na], refs[2 * na + 1]):
            cp.wait_send()
            cp.wait_recv()

    outs = pl.pallas_call(
        body, name=name,
        out_shape=[pltpu.HBM(a.shape, a.dtype) for a in list(srcs) + list(lands)],
        in_specs=[_HBM] * (2 * na) + [_SEM, _SEM, pl.BlockSpec(memory_space=pl.ANY)], out_specs=[_HBM] * (2 * na),
        input_output_aliases={i: i for i in range(2 * na)},
        compiler_params=pltpu.CompilerParams(has_side_effects=_DATAFLOW),
    )(*srcs, *lands, send_sems, recv_sems, after)
    return outs[:na], outs[na:]


def _sum_slots(name, g, tr):
    _, rws, cols = g.shape

    def body(g_ref, o_ref):
        acc = g_ref[0].astype(F32)
        for j in range(1, N_DEV):
            acc = acc + g_ref[j].astype(F32)
        o_ref[...] = acc

    return pl.pallas_call(
        body, name=name, grid=(rws // tr,),
        in_specs=[pl.BlockSpec((N_DEV, tr, cols), lambda i: (0, i, 0))],
        out_specs=pl.BlockSpec((tr, cols), lambda i: (i, 0)),
        out_shape=jax.ShapeDtypeStruct((rws, cols), F32),
        compiler_params=_cparams(dimension_semantics=("parallel",)),
    )(g)


def _adam_math(wv, gv, mv, vv):
    mn = ADAM_B1 * mv + (1.0 - ADAM_B1) * gv
    vn = ADAM_B2 * vv + (1.0 - ADAM_B2) * jnp.square(gv)
    m_hat = mn / (1.0 - ADAM_B1 ** ADAM_STEP)
    v_hat = vn / (1.0 - ADAM_B2 ** ADAM_STEP)
    delta = -ADAM_LR * (m_hat / (jnp.sqrt(v_hat) + ADAM_EPS) + ADAM_WD * wv)
    return delta, mn, vn


def _adamw(name, w, g, m, v, tr):
    return _rowwise(name, _adam_math, [w, g, m, v], [], [(LANES, F32)] * 3, tr=tr)


def _adamw_slots(name, recv, own, rb, cb, cw, w, m, v, tr, order):
    nr, nc = w.shape
    n = recv.shape[0]

    def body(g_ref, own_ref, w_ref, m_ref, v_ref, order_ref, go_ref, d_ref, mo_ref, vo_ref):
        me = 2 * lax.axis_index("x") + lax.axis_index("y")
        if n == N_DEV:
            me = 2 * me + lax.axis_index("c")
        acc = None
        for s in range(n):
            part = jnp.where(me == s, own_ref[s], g_ref[s]).astype(F32)
            acc = part if acc is None else acc + part
        g = acc[:, :nc]
        go_ref[...] = g
        d_ref[...], mo_ref[...], vo_ref[...] = _adam_math(w_ref[...], g, m_ref[...], v_ref[...])

    nat = pl.BlockSpec((tr, nc), lambda i: (i, 0))
    slots = pl.BlockSpec((n, tr, cw), lambda i: (0, rb + i, cb))
    return pl.pallas_call(
        body, name=name, grid=(nr // tr,),
        in_specs=[slots, slots, nat, nat, nat, pl.BlockSpec(memory_space=pl.ANY)],
        out_specs=[nat] * 4, out_shape=[jax.ShapeDtypeStruct((nr, nc), F32)] * 4,
        compiler_params=_cparams(dimension_semantics=("parallel",)),
    )(recv, own, w, m, v, order)


def _local_blocks(w):
    pad_cols = lambda a: jnp.pad(a, ((0, 0), (0, FF_PAD - FF_SHARD)))
    pad_rows = lambda a: jnp.pad(a, ((0, FF_PAD - FF_SHARD), (0, 0)))
    gate_up = lambda tag: jnp.concatenate([pad_cols(w[tag + "_w_gate"]), pad_cols(w[tag + "_w_up"])], axis=1)
    blocks = {
        "ffn1_gu": gate_up("ffn1"), "ffn1_d": pad_rows(w["ffn1_w_down"]), "w_in": w["w_in"],
        "lora": jnp.concatenate([w["rwkv_w2"], w["rwkv_a2"], w["rwkv_g2"]], axis=0),
        "br": jnp.concatenate([w["w_br_rwkv"], w["w_br_attn"], w["ple_w_proj"]], axis=0),
        "w_out": w["w_out"], "ffn2_gu": gate_up("ffn2"), "ffn2_d": pad_rows(w["ffn2_w_down"]),
        "ple_gate": w["ple_w_gate"],
    }
    return {n: a.astype(BF16) for n, a in blocks.items()}


GATHER_GROUPS = {"head": ("ffn1_gu", "ffn1_d"), "mid": ("w_in", "lora"),
                 "rest": ("br", "w_out", "ffn2_gu", "ffn2_d", "ple_gate")}

SCATTER_GROUPS = {"tail": ("ple_gate", "ple_proj", "ffn2_gu", "ffn2_d"), "branch": ("w_out", "br"),
                  "mixer": ("lora", "w_in"),
                  "ffn1_down": ("ffn1_d",), "head": ("ffn1_gu",)}

ADAM_PLAN = (
    ("ffn1_w_gate", "ffn1_gu", 0, 0, FF_PAD, 256), ("ffn1_w_up", "ffn1_gu", 0, 1, FF_PAD, 256),
    ("ffn1_w_down", "ffn1_d", 0, 0, D_MODEL, FF_SHARD // 2), ("w_in", "w_in", 0, 0, IN_SHARD, 256),
    ("rwkv_w2", "lora", 0, 0, HEAD_DIM, 64), ("rwkv_a2", "lora", 1, 0, HEAD_DIM, 64),
    ("rwkv_g2", "lora", 2, 0, HEAD_DIM, 64),
    ("w_br_rwkv", "br", 0, 0, OUT_SHARD, 256), ("w_br_attn", "br", 2, 0, OUT_SHARD, 256),
    ("ple_w_proj", "ple_proj", 0, 0, OUT_SHARD, 256), ("w_out", "w_out", 0, 0, D_MODEL, OUT_SHARD),
    ("ffn2_w_gate", "ffn2_gu", 0, 0, FF_PAD, 256), ("ffn2_w_up", "ffn2_gu", 0, 1, FF_PAD, 256),
    ("ffn2_w_down", "ffn2_d", 0, 0, D_MODEL, FF_SHARD // 2), ("ple_w_gate", "ple_gate", 0, 0, D_MODEL, OUT_SHARD),
)


def _pack_small(arrs, rows):
    flat = jnp.concatenate([a.reshape(-1) for a in arrs])
    return jnp.pad(flat, (0, rows * LANES - flat.shape[0])).reshape(rows, LANES)


def _unpack_small(flat, like):
    flat = flat.reshape(-1)
    out, off = [], 0
    for a in like:
        out.append(flat[off:off + a.size].reshape(a.shape))
        off += a.size
    return out


def _local_step(x, p, pos, target, sm, wg, fetch, on_grads):
    t = x.shape[0]

    w_d1 = wg["ffn1_d"].reshape(FF_HID, D_MODEL)
    x1, ffn1_saved, _ = _ffn_fwd("ffn1", x, sm["ffn1_norm"], wg["ffn1_gu"], lambda after: w_d1)
    wg = {**wg, **fetch("mid", x1)}
    full_cols = lambda blk: blk.transpose(1, 0, 2).reshape(blk.shape[1], N_DEV * blk.shape[2])
    lora_w2 = full_cols(wg["lora"][:, :DECAY_LORA])
    lora_a2 = full_cols(wg["lora"][:, DECAY_LORA:DECAY_LORA + ICLR_LORA])
    lora_g2 = full_cols(wg["lora"][:, DECAY_LORA + ICLR_LORA:])
    h2 = _norm_fwd("mix_norm", x1, sm["mix_norm"])
    z = _mmc_nn("w_in", h2, wg["w_in"], 0, 0, IN_SHARD)
    z_g = (z, 2 * D_MODEL, (RWKV_COLS + 3 * ATTN_DIM) // (2 * D_MODEL))

    r, k, v, lo, gd = _shift_fwd(z, sm["rwkv_mu"])
    zero_lo = jnp.zeros((DECAY_LORA, RWKV_DIM), BF16)
    w2p = jnp.concatenate([lora_w2, zero_lo], axis=0).astype(F32)
    a2p = jnp.concatenate([zero_lo, lora_a2], axis=0).astype(F32)
    pre_params = [sm["rwkv_w0"], w2p, sm["rwkv_a0"], a2p, lora_g2.astype(F32), sm["rwkv_k_k"], sm["rwkv_k_a"]]
    wide = [(RWKV_DIM, F32)]
    k2, kk, a, decay, g = _rowwise("rwkv_pre", _f_pre, [k, lo, gd], pre_params, wide * 5)
    kkn, b = _pairwise("rwkv_kk", _f_kk, [kk, a], [], 2)
    scan_in = [u.reshape(t, RWKV_HEADS, HEAD_DIM) for u in (r, decay, k2, kkn, b)]
    v_rows = _to_v_rows(v)
    y_rows, states = _wkv_fwd(*scan_in, v_rows)
    y = _from_v_rows(y_rows)
    post_params = [sm["rwkv_gn_w"], sm["rwkv_gn_b"], sm["rwkv_r_k"]]
    post_rows = [y, r, k2, v, g]
    y_rwkv = _pairwise("rwkv_post", lambda *av: (_f_post(*av),), post_rows, post_params, 1)[0]

    inv_freq = 1.0 / (ROPE_THETA ** (jnp.arange(0, HEAD_DIM, 2, dtype=F32) / HEAD_DIM))
    freq2 = jnp.tile(inv_freq, 2 * PAIR // HEAD_DIM).reshape(1, PAIR)
    half = jnp.ones((HEAD_DIM // 2,), F32)
    sign2 = jnp.tile(jnp.concatenate([-half, half]), PAIR // HEAD_DIM).reshape(1, PAIR)

    def rope_table(posv, fr, sg):
        ang = posv * fr
        return (jnp.concatenate([jnp.cos(ang), jnp.sin(ang) * sg], axis=1),)
    tab = _rowwise("rope_table", rope_table, [pos.astype(F32).reshape(t, 1)], [freq2, sign2], [(2 * PAIR, F32)])[0]
    qk = _qk_prep(z, tab, sm["q_norm"], sm["k_norm"])
    o_all, lse_all = _attn_fwd(qk, z)
    gw = HEADS_PER_GROUP * HEAD_DIM

    def by_group(ov, lv):
        return [ov[:, i * gw:(i + 1) * gw] for i in range(3)] + [lv[:, i * gw:(i + 1) * gw] for i in range(3)]
    y_attn = _rowwise("attn_comb", lambda ov, lv: (_f_comb(*by_group(ov, lv)),), [o_all, lse_all], [], [(gw, F32)])[0]

    wg = {**wg, **fetch("rest", y_rwkv)}
    w_d2 = wg["ffn2_d"].reshape(FF_HID, D_MODEL)
    w_out = wg["w_out"].reshape(D_MODEL, D_MODEL)
    w_pg = wg["ple_gate"].reshape(D_MODEL, D_MODEL)
    w_brr = full_cols(wg["br"][:, :RWKV_DIM])
    w_bra = full_cols(wg["br"][:, RWKV_DIM:RWKV_DIM + gw])
    w_pp = full_cols(wg["br"][:, RWKV_DIM + gw:])
    u_r = _mm("br_rwkv", y_rwkv, w_brr, "nn")
    u_a = _mm("br_attn", y_attn, w_bra, "nn")

    def f_merge(zgr, zga, ur, ua):
        return _sigmoid(zgr) * ur + _sigmoid(zga) * ua
    merged = _rowwise("merge", lambda zg, ur, ua: (f_merge(zg[:, :D_MODEL], zg[:, D_MODEL:], ur, ua),),
                      [z_g, u_r, u_a], [], [(D_MODEL, BF16)])[0]
    x2 = _mm("w_out", merged, w_out, "nn", res=x1)
    x3, ffn2_saved, _ = _ffn_fwd("ffn2", x2, sm["ffn2_norm"], wg["ffn2_gu"], lambda after: w_d2)

    hn = _norm_fwd("ple_norm", x3, sm["ple_norm"])
    gz = _mm("ple_gate", hn, w_pg, "nn")
    pp = _mm("ple_proj", p, w_pp, "nn")

    def f_head(x3v, gzv, ppv, tg):
        sg = _sigmoid(gzv)
        err = x3v + sg * ppv - tg
        part = 0.5 * jnp.sum(jnp.mean(err * err, axis=-1, keepdims=True))
        dx4 = err * (1.0 / D_MODEL)
        return dx4, dx4 * ppv * sg * (1.0 - sg), dx4 * sg, jnp.full((1, LANES), part, F32)
    dx4, dgz, dpp, loss_row = _rowwise("ple_loss", f_head, [x3, gz, pp, target], [],
                                       [(D_MODEL, F32), (D_MODEL, BF16), (D_MODEL, BF16)], [(1, LANES)])
    loss = loss_row[0, 0]

    gs, gm = {}, {}
    row_blocks = lambda g: g.reshape(N_DEV, g.shape[0] // N_DEV, g.shape[1])
    dhn = _mm("ple_dhn", dgz, w_pg, "nt")
    gm["ple_gate"] = row_blocks(_mm("ple_dwgate", hn, dgz, "tn", out_dtype=BF16))
    col_blocks = lambda g: g.reshape(g.shape[0], N_DEV, g.shape[1] // N_DEV).transpose(1, 0, 2)
    gm["ple_proj"] = col_blocks(_mm("ple_dwproj", p, dpp, "tn", out_dtype=BF16))
    dx3, gs["ple_norm"] = _norm_bwd("ple_dnorm", x3, sm["ple_norm"], dhn, dx4)

    dx2, gs["ffn2_norm"], gm["ffn2_gu"], gm["ffn2_d"] = _ffn_bwd(
        "ffn2", x2, sm["ffn2_norm"], wg["ffn2_gu"], w_d2, ffn2_saved, dx3)
    tail_token = on_grads("tail", {n: gm.pop(n) for n in SCATTER_GROUPS["tail"]})

    dmerged = _mm("w_out_dmerged", dx2, w_out, "nt")
    gm["w_out"] = row_blocks(_mm("w_out_dw", merged, dx2, "tn", out_dtype=BF16))

    def merge_bwd(zg, ur, ua, dm):
        _, vjp = jax.vjp(f_merge, zg[:, :D_MODEL], zg[:, D_MODEL:], ur, ua)
        dzr, dza, dur, dua = vjp(dm)
        return jnp.concatenate([dzr, dza], axis=1), dur, dua
    dz_g, du_r, du_a = _rowwise("merge_bwd", merge_bwd, [z_g, u_r, u_a, dmerged], [],
                                [(2 * D_MODEL, BF16), (D_MODEL, BF16), (D_MODEL, BF16)])
    dy_rwkv = _mm("br_rwkv_dy", du_r, w_brr, "nt")
    dy_attn = _mm("br_attn_dy", du_a, w_bra, "nt")
    gm["br"] = jnp.concatenate([col_blocks(_mm("br_rwkv_dw", y_rwkv, du_r, "tn", out_dtype=BF16)),
                                col_blocks(_mm("br_attn_dw", y_attn, du_a, "tn", out_dtype=BF16))], axis=1)
    branch_token = on_grads("branch", {n: gm.pop(n) for n in SCATTER_GROUPS["branch"]})

    def comb_bwd(ov, lv, dyv):
        _, vjp = jax.vjp(_f_comb, *by_group(ov, lv))
        d = vjp(dyv)
        return jnp.concatenate(d[:3], axis=1), jnp.concatenate(d[3:], axis=1)
    do_all, dl_all = _rowwise("attn_comb_bwd", comb_bwd, [o_all, lse_all, dy_attn], [],
                              [(ATTN_DIM, F32), (ATTN_DIM, F32)])
    dq_all, dk_all, dv_all = _attn_bwd(qk, z, do_all, dl_all)
    dqk_raw, gs["q_norm"], gs["k_norm"] = _qk_prep_bwd(z, tab, sm["q_norm"], sm["k_norm"], dq_all, dk_all)

    def post_bwd(yv, rv, k2v, vv, gv, dv_, gnw, gnb, rk):
        _, vjp = jax.vjp(_f_post, yv, rv, k2v, vv, gv, gnw, gnb, rk)
        return vjp(dv_)
    dy, dr1, dk2a, dv1, dg, d_gnw, d_gnb, d_rk = _pairwise(
        "rwkv_post_bwd", post_bwd, post_rows + [dy_rwkv], [post_params[0] + tail_token[0, 0] + branch_token[0, 0]] + post_params[1:], 5, 3)
    gs["rwkv_gn_w"], gs["rwkv_gn_b"], gs["rwkv_r_k"] = d_gnw, d_gnb, d_rk
    dr2, ddecay, dk2b, dkkn, db, dv_rows = _wkv_bwd(*scan_in, v_rows, states, _to_v_rows(dy))
    dr2, ddecay, dk2b, dkkn, db = [u.reshape(t, RWKV_DIM) for u in (dr2, ddecay, dk2b, dkkn, db)]
    dv2 = _from_v_rows(dv_rows)

    def kk_bwd(kkv, av, dkknv, dbv, dra, drb, dva, dvb):
        _, vjp = jax.vjp(_f_kk, kkv, av)
        return (*vjp((dkknv, dbv)), dra + drb, dva + dvb)
    dkk, da, dr, dv = _pairwise("rwkv_kk_bwd", kk_bwd, [kk, a, dkkn, db, dr1, dr2, dv1, dv2], [], 4)

    def pre_bwd(kv, lov, gdv, dk2x, dk2y, dkkv, dav, ddec, dgv, w0, w2p_, a0, a2p_, g2, k_k, k_a):
        _, vjp = jax.vjp(_f_pre, kv, lov, gdv, w0, w2p_, a0, a2p_, g2, k_k, k_a)
        return vjp((dk2x + dk2y, dkkv, dav, ddec, dgv))
    lora_acc = (DECAY_LORA + ICLR_LORA, RWKV_DIM)
    dk, dlo, dgd, d_w0, d_w2p, d_a0, d_a2p, d_g2, d_kk, d_ka = _rowwise(
        "rwkv_pre_bwd", pre_bwd,
        [k, lo, gd, dk2a, dk2b, dkk, da, ddecay, dg],
        pre_params, [(RWKV_DIM, F32), (LANES, F32), (LANES, F32)],
        [(1, RWKV_DIM), lora_acc, (1, RWKV_DIM), lora_acc, (GATE_LORA, RWKV_DIM), (1, RWKV_DIM), (1, RWKV_DIM)])
    gs["rwkv_w0"], gs["rwkv_a0"], gs["rwkv_k_k"], gs["rwkv_k_a"] = d_w0, d_a0, d_kk, d_ka
    gm["lora"] = jnp.concatenate([col_blocks(d_w2p[:DECAY_LORA]), col_blocks(d_a2p[DECAY_LORA:]), col_blocks(d_g2)],
                                 axis=1).astype(BF16)
    dz_r, gs["rwkv_mu"] = _shift_bwd(z, sm["rwkv_mu"], dr, dk, dv, dlo, dgd)

    dz = jnp.concatenate([dz_r, dqk_raw, dv_all.astype(BF16), dz_g], axis=1)
    dh2 = _mmc_nt("w_in_dh", dz, wg["w_in"], 0, 0, IN_SHARD, D_MODEL)
    gm["w_in"] = _mmc_tn("w_in_dw", h2, dz, IN_SHARD)
    mixer_token = on_grads("mixer", {n: gm.pop(n) for n in SCATTER_GROUPS["mixer"]})
    dx1, gs["mix_norm"] = _norm_bwd("mix_dnorm", x1, sm["mix_norm"] + mixer_token[0, 0], dh2, dx2)

    dx0, gs["ffn1_norm"], gm["ffn1_gu"], _ = _ffn_bwd(
        "ffn1", x, sm["ffn1_norm"], wg["ffn1_gu"], w_d1, ffn1_saved, dx1,
        on_down=lambda blocks: on_grads("ffn1_down", {"ffn1_d": blocks}))
    return loss, dx0, gm, gs


def kernel(x, p, positions, ffn1_norm, ffn1_w_gate, ffn1_w_up, ffn1_w_down, mix_norm, w_in, rwkv_mu, rwkv_w0, rwkv_w2, rwkv_a0, rwkv_a2, rwkv_g2, rwkv_k_k, rwkv_k_a, rwkv_r_k, rwkv_gn_w, rwkv_gn_b, q_norm, k_norm, w_br_rwkv, w_br_attn, w_out, ffn2_norm, ffn2_w_gate, ffn2_w_up, ffn2_w_down, ple_norm, ple_w_gate, ple_w_proj, loss_target, m_ffn1_norm, m_ffn1_w_gate, m_ffn1_w_up, m_ffn1_w_down, m_mix_norm, m_w_in, m_rwkv_mu, m_rwkv_w0, m_rwkv_w2, m_rwkv_a0, m_rwkv_a2, m_rwkv_g2, m_rwkv_k_k, m_rwkv_k_a, m_rwkv_r_k, m_rwkv_gn_w, m_rwkv_gn_b, m_q_norm, m_k_norm, m_w_br_rwkv, m_w_br_attn, m_w_out, m_ffn2_norm, m_ffn2_w_gate, m_ffn2_w_up, m_ffn2_w_down, m_ple_norm, m_ple_w_gate, m_ple_w_proj, v_ffn1_norm, v_ffn1_w_gate, v_ffn1_w_up, v_ffn1_w_down, v_mix_norm, v_w_in, v_rwkv_mu, v_rwkv_w0, v_rwkv_w2, v_rwkv_a0, v_rwkv_a2, v_rwkv_g2, v_rwkv_k_k, v_rwkv_k_a, v_rwkv_r_k, v_rwkv_gn_w, v_rwkv_gn_b, v_q_norm, v_k_norm, v_w_br_rwkv, v_w_br_attn, v_w_out, v_ffn2_norm, v_ffn2_w_gate, v_ffn2_w_up, v_ffn2_w_down, v_ple_norm, v_ple_w_gate, v_ple_w_proj):
    args = locals()
    w = {n: args[n][0] for n in WEIGHTS}
    m = {n: args["m_" + n][0] for n in WEIGHTS}
    v = {n: args["v_" + n][0] for n in WEIGHTS}

    w_loc = _local_blocks(w)
    head = GATHER_GROUPS["head"]
    wg = dict(zip(head, _all_gather_hbm("gather_head", [w_loc[n] for n in head])))
    me = 4 * lax.axis_index("x") + 2 * lax.axis_index("y") + lax.axis_index("c")
    gathering, order_after = {}, wg[head[0]]
    for group in ("mid", "rest"):
        shards = [w_loc[n] for n in GATHER_GROUPS[group]]
        zones = [lax.dynamic_update_slice(lax.empty((N_DEV,) + a.shape, a.dtype), a[None], (me, 0, 0)) for a in shards]
        *gathering[group], order_after = _exchange_start("gather_start_" + group, _gather_copies, shards, zones,
                                                         order_after)

    def fetch(group, after):
        _, got = _exchange_wait("gather_wait_" + group, _gather_copies, *gathering[group], after)
        return dict(zip(GATHER_GROUPS[group], got))

    sm = {n: w[n].reshape(1, -1) for n in SMALL}
    sm["ffn1_norm"] = sm["ffn1_norm"] + order_after[0, 0]
    in_flight = {}

    def scatter_early(group, arrays):
        arrs = [arrays[n] for n in SCATTER_GROUPS[group]]
        *in_flight[group], token = _exchange_start("scatter_start_" + group, _scatter_copies, arrs,
                                                   [lax.empty(a.shape, a.dtype) for a in arrs], arrs[0])
        return token
    loss_part, dx, gm, gs = _local_step(x[0], p[0, 0], positions[0], loss_target[0], sm, wg, fetch, scatter_early)
    loss = lax.psum(loss_part, ("x", "y", "c"))
    head_names = SCATTER_GROUPS["head"]
    core = lax.axis_index("c").astype(jnp.int32).reshape(1)
    halves = _pair_swap("scatter_head_pair", [gm[n] for n in head_names])
    chip_sums = [_pair_add("scatter_head_add_" + n, gm[n], hf, core) for n, hf in zip(head_names, halves)]
    *in_flight["head"], head_token = _exchange_start(
        "scatter_start_head", _chip_copies, chip_sums, [lax.empty(a.shape, a.dtype) for a in chip_sums], halves[0])
    recv, own = {}, {}

    def arrived(group, after):
        copies = _chip_copies if group == "head" else _scatter_copies
        sent, lands = _exchange_wait("scatter_wait_" + group, copies, *in_flight[group], after)
        own.update(zip(SCATTER_GROUPS[group], sent))
        recv.update(zip(SCATTER_GROUPS[group], lands))
    for group in ("tail", "branch", "mixer", "ffn1_down"):
        arrived(group, head_token)
    small_like = [w[n] for n in SMALL]
    small_rows = 80
    gs_all = _all_gather_vmem(_pack_small([gs[n] for n in SMALL], small_rows))
    gs_sum = _sum_slots("sum_small_grads", gs_all, small_rows)

    res = {}
    early = [e for e in ADAM_PLAN if e[1] not in SCATTER_GROUPS["head"]]
    late = [e for e in ADAM_PLAN if e[1] in SCATTER_GROUPS["head"]]
    for n, src, rb, cb, cw, tr in early + late:
        if (n, src, rb, cb, cw, tr) == late[0]:
            arrived("head", res["delta", early[-1][0]])
        outs4 = _adamw_slots("adamw_" + n, recv[src], own[src], rb, cb, cw, w[n], m[n], v[n], tr, head_token)
        for tag, a in zip(("grad", "delta", "new_m", "new_v"), outs4):
            res[tag, n] = a[None]
    d_s, m_s, v_s = _adamw("adamw_small", _pack_small(small_like, small_rows), gs_sum,
                           _pack_small([m[n] for n in SMALL], small_rows),
                           _pack_small([v[n] for n in SMALL], small_rows), small_rows)
    for tag, small in (("grad", gs_sum), ("delta", d_s), ("new_m", m_s), ("new_v", v_s)):
        for n, a in zip(SMALL, _unpack_small(small, small_like)):
            res[tag, n] = a[None]
    outs = [loss, dx[None]]
    for tag in ("grad", "delta", "new_m", "new_v"):
        outs += [res[tag, n] for n in WEIGHTS]
    return tuple(outs)
```

```python
import functools

import jax
import jax.numpy as jnp
from jax import lax
from jax.experimental import pallas as pl
from jax.experimental.pallas import tpu as pltpu

F32, BF16 = jnp.float32, jnp.bfloat16
MESH = pl.DeviceIdType.MESH
N_DEV = 8
LANES = 128
VMEM_LIMIT = 56 * 1024 * 1024

D_MODEL = 1024
PLE_DIM = 256
HEAD_DIM = 64
RWKV_HEADS = 8
RWKV_DIM = RWKV_HEADS * HEAD_DIM
DECAY_LORA = 64
ICLR_LORA = 64
GATE_LORA = 128
GN_EPS = 64e-5
ATTN_GROUPS = ((128, 1), (512, 4), (2048, 16))
HEADS_PER_GROUP = 4
ATTN_HEADS = HEADS_PER_GROUP * len(ATTN_GROUPS)
ATTN_DIM = ATTN_HEADS * HEAD_DIM
BAND_BLOCK = 128
ROPE_THETA = 10000.0
NEG_INF = -1e30
D_FF = 2816
RMS_EPS = 1e-6
RWKV_COLS = 3 * RWKV_DIM + DECAY_LORA + ICLR_LORA + GATE_LORA
ADAM_LR = 0.001
ADAM_B1 = 0.9
ADAM_B2 = 0.999
ADAM_EPS = 1e-08
ADAM_WD = 0.01
ADAM_STEP = 10

V_LO = LANES // RWKV_HEADS
V_HI = HEAD_DIM // V_LO
SCAN_CHUNK = 64
MM_ROWS = 2048

FF_SHARD = D_FF // N_DEV
FF_PAD = -(-FF_SHARD // LANES) * LANES
FF_HID = N_DEV * FF_PAD
IN_SHARD = 6144 // N_DEV
OUT_SHARD = D_MODEL // N_DEV

SMALL = ("ffn1_norm", "mix_norm", "rwkv_mu", "rwkv_w0", "rwkv_a0", "rwkv_k_k", "rwkv_k_a", "rwkv_r_k",
         "rwkv_gn_w", "rwkv_gn_b", "q_norm", "k_norm", "ffn2_norm", "ple_norm")
WEIGHTS = ("ffn1_norm", "ffn1_w_gate", "ffn1_w_up", "ffn1_w_down", "mix_norm", "w_in", "rwkv_mu", "rwkv_w0",
           "rwkv_w2", "rwkv_a0", "rwkv_a2", "rwkv_g2", "rwkv_k_k", "rwkv_k_a", "rwkv_r_k", "rwkv_gn_w",
           "rwkv_gn_b", "q_norm", "k_norm", "w_br_rwkv", "w_br_attn", "w_out", "ffn2_norm", "ffn2_w_gate",
           "ffn2_w_up", "ffn2_w_down", "ple_norm", "ple_w_gate", "ple_w_proj")


def _cparams(**kw):
    return pltpu.CompilerParams(vmem_limit_bytes=VMEM_LIMIT, **kw)


def _tile(n, cap):
    best = None
    for t in range(LANES, min(n, cap) + 1, LANES):
        if n % t == 0:
            best = t
    return best if best is not None else n


@jax.custom_vjp
def _bdot(a, w):
    return jnp.dot(a.astype(BF16), w.astype(BF16), preferred_element_type=F32)


def _bdot_fwd(a, w):
    return _bdot(a, w), (a, w)


def _bdot_bwd(res, g):
    a, w = res
    gb = g.astype(BF16)
    da = lax.dot_general(gb, w.astype(BF16), (((1,), (1,)), ((), ())), preferred_element_type=F32)
    dw = lax.dot_general(a.astype(BF16), gb, (((0,), (0,)), ((), ())), preferred_element_type=F32)
    return da.astype(a.dtype), dw.astype(w.dtype)


_bdot.defvjp(_bdot_fwd, _bdot_bwd)


@jax.custom_vjp
def _bdot_nt(a, b):
    return lax.dot_general(a.astype(BF16), b.astype(BF16), (((1,), (1,)), ((), ())), preferred_element_type=F32)


def _bdot_nt_fwd(a, b):
    return _bdot_nt(a, b), (a, b)


def _bdot_nt_bwd(res, g):
    a, b = res
    gb = g.astype(BF16)
    da = jnp.dot(gb, b.astype(BF16), preferred_element_type=F32)
    db = lax.dot_general(gb, a.astype(BF16), (((0,), (0,)), ((), ())), preferred_element_type=F32)
    return da.astype(a.dtype), db.astype(b.dtype)


_bdot_nt.defvjp(_bdot_nt_fwd, _bdot_nt_bwd)


def _mm(name, a, b, mode, out_dtype=F32, res=None, scale=None):
    if mode == "nn":
        (m, k), n = a.shape, b.shape[1]
    elif mode == "nt":
        (m, k), n = a.shape, b.shape[0]
    else:
        (k, m), n = a.shape, b.shape[1]
    tm, tn = _tile(m, 512 if mode == "tn" else MM_ROWS), _tile(n, 512)
    a_spec = pl.BlockSpec((k, tm), lambda i, j: (0, i)) if mode == "tn" else pl.BlockSpec((tm, k), lambda i, j: (i, 0))
    b_spec = pl.BlockSpec((tn, k), lambda i, j: (j, 0)) if mode == "nt" else pl.BlockSpec((k, tn), lambda i, j: (0, j))
    dims = {"nn": ((1,), (0,)), "nt": ((1,), (1,)), "tn": ((0,), (0,))}[mode]
    o_spec = pl.BlockSpec((tm, tn), lambda i, j: (i, j))
    ins, in_specs = [a, b], [a_spec, b_spec]
    if res is not None:
        ins.append(res)
        in_specs.append(o_spec)

    def body(*refs):
        acc = lax.dot_general(refs[0][...].astype(BF16), refs[1][...].astype(BF16), (dims, ((), ())),
                              preferred_element_type=F32)
        if scale is not None:
            acc = acc * scale
        if res is not None:
            acc = acc + refs[2][...].astype(F32)
        refs[-1][...] = acc.astype(refs[-1].dtype)

    return pl.pallas_call(
        body, name=name, grid=(m // tm, n // tn), in_specs=in_specs, out_specs=o_spec,
        out_shape=jax.ShapeDtypeStruct((m, n), out_dtype),
        compiler_params=_cparams(dimension_semantics=("parallel", "parallel")),
    )(*ins)


def _mmc_nn(name, a, wb, ki, ci, n, out_dtype=F32):
    m, k = a.shape
    tm = _tile(m, MM_ROWS)

    def body(a_ref, w_ref, o_ref):
        o_ref[...] = jnp.dot(a_ref[...].astype(BF16), w_ref[...], preferred_element_type=F32).astype(o_ref.dtype)

    return pl.pallas_call(
        body, name=name, grid=(m // tm, N_DEV),
        in_specs=[pl.BlockSpec((tm, k), lambda i, j: (i, 0)), pl.BlockSpec((None, k, n), lambda i, j: (j, ki, ci))],
        out_specs=pl.BlockSpec((tm, n), lambda i, j: (i, j)),
        out_shape=jax.ShapeDtypeStruct((m, N_DEV * n), out_dtype),
        compiler_params=_cparams(dimension_semantics=("parallel", "parallel")),
    )(a, wb)


def _mmc_nt(name, a, wb, ki, ci, n, k, res=None):
    m = a.shape[0]
    tm = _tile(m, MM_ROWS)
    o_spec = pl.BlockSpec((tm, k), lambda i, j: (i, 0))
    ins = [a, wb] + ([res] if res is not None else [])
    in_specs = [pl.BlockSpec((tm, n), lambda i, j: (i, j)), pl.BlockSpec((None, k, n), lambda i, j: (j, ki, ci))]
    in_specs += [o_spec] if res is not None else []

    def body(*refs):
        a_ref, w_ref, o_ref = refs[0], refs[1], refs[-1]
        acc = lax.dot_general(a_ref[...].astype(BF16), w_ref[...], (((1,), (1,)), ((), ())),
                              preferred_element_type=F32)

        @pl.when(pl.program_id(1) == 0)
        def _():
            o_ref[...] = acc + refs[2][...] if res is not None else acc

        @pl.when(pl.program_id(1) != 0)
        def _():
            o_ref[...] += acc

    return pl.pallas_call(
        body, name=name, grid=(m // tm, N_DEV), in_specs=in_specs, out_specs=o_spec,
        out_shape=jax.ShapeDtypeStruct((m, k), F32),
        compiler_params=_cparams(dimension_semantics=("parallel", "arbitrary")),
    )(*ins)


def _mmc_tn(name, x, dy, n):
    m, k = x.shape
    tk = _tile(k, 1024)

    def body(x_ref, dy_ref, o_ref):
        o_ref[...] = lax.dot_general(x_ref[...].astype(BF16), dy_ref[...].astype(BF16), (((0,), (0,)), ((), ())),
                                     preferred_element_type=F32).astype(o_ref.dtype)

    return pl.pallas_call(
        body, name=name, grid=(N_DEV, k // tk),
        in_specs=[pl.BlockSpec((m, tk), lambda j, i: (0, i)), pl.BlockSpec((m, n), lambda j, i: (0, j))],
        out_specs=pl.BlockSpec((None, tk, n), lambda j, i: (j, i, 0)),
        out_shape=jax.ShapeDtypeStruct((N_DEV, k, n), BF16),
        compiler_params=_cparams(dimension_semantics=("parallel", "parallel")),
    )(x, dy)


def _rowwise(name, fn, rows, params, out_rows, out_accs=(), tr=512):
    rows = [a if isinstance(a, tuple) else (a, a.shape[1], 0) for a in rows]
    r = rows[0][0].shape[0]
    in_specs = [pl.BlockSpec((tr, wd), lambda i, cb=cb: (i, cb)) for _, wd, cb in rows]
    rows = [a for a, _, _ in rows]
    in_specs += [pl.BlockSpec(p.shape, lambda i, nd=p.ndim: (0,) * nd) for p in params]
    out_shape = [jax.ShapeDtypeStruct((r, c), dt) for c, dt in out_rows]
    out_shape += [jax.ShapeDtypeStruct(s, F32) for s in out_accs]
    out_specs = [pl.BlockSpec((tr, c), lambda i: (i, 0)) for c, _ in out_rows]
    out_specs += [pl.BlockSpec(s, lambda i, nd=len(s): (0,) * nd) for s in out_accs]
    n_in, n_ro = len(rows) + len(params), len(out_rows)

    def body(*refs):
        res = fn(*[ref[...] for ref in refs[:n_in]])
        outs = refs[n_in:]
        for o, v in zip(outs[:n_ro], res[:n_ro]):
            o[...] = v.astype(o.dtype)
        for o, v in zip(outs[n_ro:], res[n_ro:]):
            _accumulate(o, v)

    return pl.pallas_call(
        body, name=name, grid=(r // tr,), in_specs=in_specs, out_specs=out_specs, out_shape=out_shape,
        compiler_params=_cparams(dimension_semantics=("arbitrary",)),
    )(*rows, *params)


def _pairwise(name, fn, rows, params, n_out, n_acc=0, tr=512):
    t, c = rows[0].shape
    tile = pl.BlockSpec((tr, 2 * HEAD_DIM), lambda p, i: (i, p))
    vec = pl.BlockSpec((1, 2 * HEAD_DIM), lambda p, i: (0, p))
    n_in = len(rows) + len(params)

    def body(*refs):
        res = fn(*[ref[...] for ref in refs[:n_in]])
        outs = refs[n_in:]
        for o, v in zip(outs[:n_out], res[:n_out]):
            o[...] = v
        first = pl.program_id(1) == 0
        for o, v in zip(outs[n_out:], res[n_out:]):
            @pl.when(first)
            def _(o=o, v=v):
                o[...] = v

            @pl.when(jnp.logical_not(first))
            def _(o=o, v=v):
                o[...] += v

    return pl.pallas_call(
        body, name=name, grid=(c // (2 * HEAD_DIM), t // tr),
        in_specs=[tile] * len(rows) + [vec] * len(params), out_specs=[tile] * n_out + [vec] * n_acc,
        out_shape=[jax.ShapeDtypeStruct((t, c), F32)] * n_out + [jax.ShapeDtypeStruct((1, c), F32)] * n_acc,
        compiler_params=_cparams(dimension_semantics=("parallel", "arbitrary")),
    )(*rows, *params)


def _accumulate(o_ref, v):
    @pl.when(pl.program_id(0) == 0)
    def _():
        o_ref[...] = v

    @pl.when(pl.program_id(0) != 0)
    def _():
        o_ref[...] += v


def _rms(x, g):
    return x * lax.rsqrt(jnp.mean(x * x, axis=-1, keepdims=True) + RMS_EPS) * g


def _sigmoid(x):
    return jax.nn.sigmoid(x)


def _softplus(x):
    return jnp.maximum(x, 0.0) + jnp.log1p(jnp.exp(-jnp.abs(x)))


def _norm_fwd(name, x, g):
    return _rowwise(name, lambda xv, gv: (_rms(xv, gv),), [x], [g], [(x.shape[1], BF16)])[0]


def _norm_bwd(name, x, g, dh, dres):
    def fn(xv, dhv, drv, gv):
        _, vjp = jax.vjp(_rms, xv, gv)
        dx, dg = vjp(dhv)
        return dx + drv, dg
    return _rowwise(name, fn, [x, dh, dres], [g], [(x.shape[1], F32)], [g.shape])


def _f_act(gate, up):
    return gate * _sigmoid(gate) * up


def _gate_up_act(name, h, w_gu):
    m, k = h.shape
    tm = _tile(m, MM_ROWS)

    def body(h_ref, w_ref, gu_ref, a_ref):
        gu = jnp.dot(h_ref[...], w_ref[...], preferred_element_type=F32)
        gu_ref[...] = gu
        a_ref[...] = _f_act(gu[:, :FF_PAD], gu[:, FF_PAD:]).astype(a_ref.dtype)

    return pl.pallas_call(
        body, name=name, grid=(m // tm, N_DEV),
        in_specs=[pl.BlockSpec((tm, k), lambda i, j: (i, 0)),
                  pl.BlockSpec((None, k, 2 * FF_PAD), lambda i, j: (j, 0, 0))],
        out_specs=[pl.BlockSpec((tm, 2 * FF_PAD), lambda i, j: (i, j)), pl.BlockSpec((tm, FF_PAD), lambda i, j: (i, j))],
        out_shape=[jax.ShapeDtypeStruct((m, N_DEV * 2 * FF_PAD), F32), jax.ShapeDtypeStruct((m, FF_HID), BF16)],
        compiler_params=_cparams(dimension_semantics=("parallel", "parallel")),
    )(h, w_gu)


def _gate_up_act_bwd(name, dout, w_down, gu, order):
    m, k = dout.shape
    tm = _tile(m, MM_ROWS)

    def body(d_ref, w_ref, gu_ref, order_ref, o_ref):
        da = 0.5 * lax.dot_general(d_ref[...].astype(BF16), w_ref[...], (((1,), (1,)), ((), ())),
                                   preferred_element_type=F32)
        guv = gu_ref[...]
        _, vjp = jax.vjp(_f_act, guv[:, :FF_PAD], guv[:, FF_PAD:])
        o_ref[...] = jnp.concatenate(vjp(da), axis=1).astype(o_ref.dtype)

    gu_spec = pl.BlockSpec((tm, 2 * FF_PAD), lambda i, j: (i, j))
    return pl.pallas_call(
        body, name=name, grid=(m // tm, N_DEV),
        in_specs=[pl.BlockSpec((tm, k), lambda i, j: (i, 0)), pl.BlockSpec((FF_PAD, k), lambda i, j: (j, 0)), gu_spec,
                  pl.BlockSpec(memory_space=pl.ANY)],
        out_specs=gu_spec, out_shape=jax.ShapeDtypeStruct((m, N_DEV * 2 * FF_PAD), BF16),
        compiler_params=_cparams(dimension_semantics=("parallel", "parallel")),
    )(dout, w_down, gu, order)


def _ffn_fwd(tag, x, norm, w_gu, w_down):
    h = _norm_fwd(tag + "_norm", x, norm)
    gu, a = _gate_up_act(tag + "_gu", h, w_gu)
    wd = w_down(a)
    out = _mm(tag + "_down", a, wd, "nn", res=x, scale=0.5)
    return out, (h, gu, a), wd


def _ffn_bwd(tag, x, norm, w_gu, w_down, saved, dout, on_down=None):
    h, gu, a = saved
    d_wdown = _mm(tag + "_dwdown", a, dout, "tn", out_dtype=BF16, scale=0.5).reshape(N_DEV, FF_PAD, D_MODEL)
    token = on_down(d_wdown) if on_down is not None else jnp.zeros((8, LANES), F32)
    dgu = _gate_up_act_bwd(tag + "_dgu", dout, w_down, gu, token)
    dh =_mmc_nt(tag + "_dh", dgu, w_gu, 0, 0, 2 * FF_PAD, D_MODEL)
    d_wgu = _mmc_tn(tag + "_dwgu", h, dgu, 2 * FF_PAD)
    dx, dnorm = _norm_bwd(tag + "_dnorm", x, norm, dh, dout)
    return dx, dnorm, d_wgu, d_wdown


def _shift_fwd(z, mu):
    t, c = z.shape[0], RWKV_COLS
    tr = 256

    def body(z_ref, zp_ref, mu_ref, r_ref, k_ref, v_ref, lo_ref, gd_ref):
        zv = z_ref[...]
        prev = zp_ref[7:8, :] * jnp.where(pl.program_id(0) == 0, 0.0, 1.0)
        row = lax.broadcasted_iota(jnp.int32, zv.shape, 0)
        zsh = jnp.where(row == 0, prev, pltpu.roll(zv, 1, 0))
        zs = zv + (zsh - zv) * mu_ref[...]
        r_ref[...] = zs[:, 0:512]
        k_ref[...] = zs[:, 512:1024]
        v_ref[...] = zs[:, 1024:1536]
        lo_ref[...] = zs[:, 1536:1664]
        gd_ref[...] = zs[:, 1664:1792]

    widths = (512, 512, 512, 128, 128)
    return pl.pallas_call(
        body, name="rwkv_shift", grid=(t // tr,),
        in_specs=[pl.BlockSpec((tr, c), lambda i: (i, 0)),
                  pl.BlockSpec((8, c), lambda i: (jnp.maximum(i * (tr // 8) - 1, 0), 0)),
                  pl.BlockSpec((1, c), lambda i: (0, 0))],
        out_specs=[pl.BlockSpec((tr, w), lambda i: (i, 0)) for w in widths],
        out_shape=[jax.ShapeDtypeStruct((t, w), F32) for w in widths],
        compiler_params=_cparams(dimension_semantics=("parallel",)),
    )(z, z, mu)


def _shift_bwd(z, mu, dr, dk, dv, dlo, dgd):
    t, c = z.shape[0], RWKV_COLS
    tr = 256
    nt = t // tr

    def body(z_ref, zp_ref, mu_ref, dr_ref, dk_ref, dv_ref, dlo_ref, dgd_ref,
             drn_ref, dkn_ref, dvn_ref, dlon_ref, dgdn_ref, dz_ref, dmu_ref):
        i = pl.program_id(0)
        zv, muv = z_ref[...], mu_ref[...]
        prev = zp_ref[7:8, :] * jnp.where(i == 0, 0.0, 1.0)
        row = lax.broadcasted_iota(jnp.int32, zv.shape, 0)
        zsh = jnp.where(row == 0, prev, pltpu.roll(zv, 1, 0))
        dzs = jnp.concatenate([dr_ref[...], dk_ref[...], dv_ref[...], dlo_ref[...], dgd_ref[...]], axis=1)
        nxt = jnp.concatenate([drn_ref[0:1, :], dkn_ref[0:1, :], dvn_ref[0:1, :], dlon_ref[0:1, :],
                               dgdn_ref[0:1, :]], axis=1) * jnp.where(i == nt - 1, 0.0, 1.0)
        u = dzs * muv
        un = jnp.where(row == tr - 1, nxt * muv, pltpu.roll(u, tr - 1, 0))
        dz_ref[...] = (dzs - u + un).astype(dz_ref.dtype)
        _accumulate(dmu_ref, jnp.sum(dzs * (zsh - zv), axis=0, keepdims=True))

    widths = (512, 512, 512, 128, 128)
    nxt_map = lambda i: (jnp.minimum((i + 1) * (tr // 8), t // 8 - 1), 0)
    return pl.pallas_call(
        body, name="rwkv_shift_bwd", grid=(nt,),
        in_specs=[pl.BlockSpec((tr, c), lambda i: (i, 0)),
                  pl.BlockSpec((8, c), lambda i: (jnp.maximum(i * (tr // 8) - 1, 0), 0)),
                  pl.BlockSpec((1, c), lambda i: (0, 0))]
        + [pl.BlockSpec((tr, w), lambda i: (i, 0)) for w in widths]
        + [pl.BlockSpec((8, w), nxt_map) for w in widths],
        out_specs=[pl.BlockSpec((tr, c), lambda i: (i, 0)), pl.BlockSpec((1, c), lambda i: (0, 0))],
        out_shape=[jax.ShapeDtypeStruct((t, c), BF16), jax.ShapeDtypeStruct((1, c), F32)],
        compiler_params=_cparams(dimension_semantics=("arbitrary",)),
    )(z, z, mu, dr, dk, dv, dlo, dgd, dr, dk, dv, dlo, dgd)


def _f_pre(k, lo, gd, w0, w2p, a0, a2p, g2, k_k, k_a):
    lane = lax.broadcasted_iota(jnp.int32, lo.shape, 1)
    lo_act = jnp.where(lane < DECAY_LORA, jnp.tanh(lo), lo)
    w = -_softplus(-(w0 + _bdot(lo_act, w2p))) - 0.5
    a = _sigmoid(a0 + _bdot(lo_act, a2p))
    g = _bdot(_sigmoid(gd), g2)
    kk = k * k_k
    k2 = k * (1.0 + (a - 1.0) * k_a)
    decay = jnp.exp(-jnp.exp(w))
    return k2, kk, a, decay, g


def _f_kk(kk, a):
    kkn = kk * lax.rsqrt(jnp.maximum(_head_sums(kk * kk), 1e-24))
    return kkn, kkn * a


def _f_post(y, r, k2, v, g, gn_w, gn_b, r_k):
    mean = _head_sums(y) * (1.0 / HEAD_DIM)
    var = _head_sums(jnp.square(y - mean)) * (1.0 / HEAD_DIM)
    yn = (y - mean) * lax.rsqrt(var + GN_EPS) * gn_w + gn_b
    bonus = _head_sums(r * k2 * r_k) * v
    return (yn + bonus) * g


def _to_v_rows(x):
    t = x.shape[0]
    return x.reshape(t, RWKV_HEADS, V_HI, V_LO).transpose(0, 2, 3, 1).reshape(t, V_HI, LANES)


def _from_v_rows(x):
    t = x.shape[0]
    return x.reshape(t, V_HI, V_LO, RWKV_HEADS).transpose(0, 3, 1, 2).reshape(t, RWKV_DIM)


def _k_cols(x):
    return jnp.tile(x, (V_LO, 1)).T


def _k_rows(x):
    xt = x.T
    out = xt[0:RWKV_HEADS]
    for l in range(1, V_LO):
        out = out + xt[l * RWKV_HEADS:(l + 1) * RWKV_HEADS]
    return out


def _wkv_fwd(r, w, k, kk, b, v):
    t = r.shape[0]
    tc = SCAN_CHUNK
    key_spec = pl.BlockSpec((tc, RWKV_HEADS, HEAD_DIM), lambda i: (i, 0, 0))
    row_spec = pl.BlockSpec((tc, V_HI, LANES), lambda i: (i, 0, 0))

    def body(r_ref, w_ref, k_ref, kk_ref, b_ref, v_ref, y_ref, st_ref, s_scr, cols_a, cols_b):
        @pl.when(pl.program_id(0) == 0)
        def _():
            s_scr[...] = jnp.zeros_like(s_scr)

        def prep(ti, buf):
            for n, ref in enumerate((r_ref, w_ref, k_ref, kk_ref, b_ref)):
                buf[n] = _k_cols(ref[ti])

        def step(ti, s, cur, nxt, ti_next):
            rc, wc, kc, kkc, bc = (cur[n] for n in range(5))
            prep(ti_next, nxt)
            vt = v_ref[ti]
            new, ys = [], []
            for j in range(V_HI):
                sa = -jnp.sum(s[j] * kkc, axis=0, keepdims=True)
                nj = s[j] * wc + bc * sa + kc * vt[j:j + 1]
                st_ref[ti, j] = nj
                ys.append(jnp.sum(nj * rc, axis=0, keepdims=True))
                new.append(nj)
            y_ref[ti] = jnp.concatenate(ys, axis=0)
            return tuple(new)

        def pair(i, s):
            s = step(2 * i, s, cols_a, cols_b, 2 * i + 1)
            return step(2 * i + 1, s, cols_b, cols_a, jnp.minimum(2 * i + 2, tc - 1))

        prep(0, cols_a)
        s = lax.fori_loop(0, tc // 2, pair, tuple(s_scr[j] for j in range(V_HI)))
        for j in range(V_HI):
            s_scr[j] = s[j]

    return pl.pallas_call(
        body, name="wkv_fwd", grid=(t // tc,),
        in_specs=[key_spec] * 5 + [row_spec],
        out_specs=[row_spec, pl.BlockSpec((tc, V_HI, HEAD_DIM, LANES), lambda i: (i, 0, 0, 0))],
        out_shape=[jax.ShapeDtypeStruct((t, V_HI, LANES), F32),
                   jax.ShapeDtypeStruct((t, V_HI, HEAD_DIM, LANES), F32)],
        scratch_shapes=[pltpu.VMEM((V_HI, HEAD_DIM, LANES), F32)] + [pltpu.VMEM((5, HEAD_DIM, LANES), F32)] * 2,
        compiler_params=_cparams(dimension_semantics=("arbitrary",)),
    )(r, w, k, kk, b, v)


def _wkv_bwd(r, w, k, kk, b, v, states, dy):
    t = r.shape[0]
    tc = SCAN_CHUNK
    nb = t // tc
    key_spec = pl.BlockSpec((tc, RWKV_HEADS, HEAD_DIM), lambda i: (nb - 1 - i, 0, 0))
    row_spec = pl.BlockSpec((tc, V_HI, LANES), lambda i: (nb - 1 - i, 0, 0))
    st_spec = pl.BlockSpec((tc, V_HI, HEAD_DIM, LANES), lambda i: (nb - 1 - i, 0, 0, 0))
    stp_spec = pl.BlockSpec((1, V_HI, HEAD_DIM, LANES), lambda i: (jnp.maximum((nb - 1 - i) * tc - 1, 0), 0, 0, 0))

    def body(r_ref, w_ref, k_ref, kk_ref, b_ref, v_ref, st_ref, stp_ref, dy_ref,
             dr_ref, dw_ref, dk_ref, dkk_ref, db_ref, dv_ref, ds_scr, cols_a, cols_b, accs_a, accs_b):
        @pl.when(pl.program_id(0) == 0)
        def _():
            ds_scr[...] = jnp.zeros_like(ds_scr)

        def colsum(x):
            return jnp.sum(x, axis=0, keepdims=True)

        def prep(ti, buf):
            for n, ref in enumerate((r_ref, w_ref, k_ref, kk_ref, b_ref)):
                buf[n] = _k_cols(ref[ti])

        def flush(ti, buf):
            for n, ref in enumerate((dr_ref, dk_ref, db_ref, dw_ref, dkk_ref)):
                ref[ti] = _k_rows(buf[n])

        def step(ti, ds, sp, cur, accs):
            rc, wc, kc, kkc, bc = (cur[n] for n in range(5))
            vt, dyt = v_ref[ti], dy_ref[ti]
            acc = None
            new, dvs = [], []
            for j in range(V_HI):
                st = st_ref[ti, j]
                dsj = ds[j] + rc * dyt[j:j + 1]
                sa = -colsum(sp[j] * kkc)
                dsa = colsum(dsj * bc)
                dvs.append(colsum(dsj * kc))
                parts = (st * dyt[j:j + 1], dsj * vt[j:j + 1], dsj * sa, dsj * sp[j], -(sp[j] * dsa))
                acc = parts if acc is None else tuple(a + q for a, q in zip(acc, parts))
                new.append(dsj * wc - kkc * dsa)
            dv_ref[ti] = jnp.concatenate(dvs, axis=0)
            for n in range(5):
                accs[n] = acc[n]
            return tuple(new)

        def states_before(ti):
            return tuple(st_ref[ti - 1, j] for j in range(V_HI))

        def pair(i, ds):
            ta = tc - 1 - 2 * i
            prep(ta - 1, cols_b)
            flush(jnp.minimum(ta + 1, tc - 1), accs_b)
            ds = step(ta, ds, states_before(ta), cols_a, accs_a)
            prep(ta - 2, cols_a)
            flush(ta, accs_a)
            return step(ta - 1, ds, states_before(ta - 1), cols_b, accs_b)

        prep(tc - 1, cols_a)
        accs_b[...] = jnp.zeros_like(accs_b)
        ds = lax.fori_loop(0, tc // 2 - 1, pair, tuple(ds_scr[j] for j in range(V_HI)))
        prep(0, cols_b)
        flush(2, accs_b)
        ds = step(1, ds, states_before(1), cols_a, accs_a)
        flush(1, accs_a)
        keep = jnp.where(pl.program_id(0) == nb - 1, 0.0, 1.0)
        ds = step(0, ds, tuple(stp_ref[0, j] * keep for j in range(V_HI)), cols_b, accs_b)
        flush(0, accs_b)
        for j in range(V_HI):
            ds_scr[j] = ds[j]

    key_out = jax.ShapeDtypeStruct((t, RWKV_HEADS, HEAD_DIM), F32)
    return pl.pallas_call(
        body, name="wkv_bwd", grid=(nb,),
        in_specs=[key_spec] * 5 + [row_spec, st_spec, stp_spec, row_spec],
        out_specs=[key_spec] * 5 + [row_spec],
        out_shape=[key_out] * 5 + [jax.ShapeDtypeStruct((t, V_HI, LANES), F32)],
        scratch_shapes=[pltpu.VMEM((V_HI, HEAD_DIM, LANES), F32)] + [pltpu.VMEM((5, HEAD_DIM, LANES), F32)] * 4,
        compiler_params=_cparams(dimension_semantics=("arbitrary",)),
    )(r, w, k, kk, b, v, states, states, dy)


PAIR = 2 * HEAD_DIM
N_PAIRS = ATTN_HEADS // 2
Q_COL0 = RWKV_COLS // PAIR
K_COL0 = Q_COL0 + N_PAIRS
V_COL0 = K_COL0 + N_PAIRS


def _swap_halves(x):
    lane = lax.broadcasted_iota(jnp.int32, x.shape, 1)
    return jnp.where((lane & (HEAD_DIM - 1)) < HEAD_DIM // 2, pltpu.roll(x, PAIR - HEAD_DIM // 2, 1),
                     pltpu.roll(x, HEAD_DIM // 2, 1))


@jax.custom_vjp
def _rope(x, cosf, sinf):
    return x * cosf + _swap_halves(x) * sinf


def _rope_fwd(x, cosf, sinf):
    return _rope(x, cosf, sinf), (cosf, sinf)


def _rope_bwd(res, d):
    cosf, sinf = res
    return d * cosf + _swap_halves(d * sinf), jnp.zeros_like(cosf), jnp.zeros_like(sinf)


_rope.defvjp(_rope_fwd, _rope_bwd)


def _head_sums(x):
    lane = lax.broadcasted_iota(jnp.int32, x.shape, 1)
    lo = jnp.where(lane < HEAD_DIM, 1.0, 0.0)
    hi = 1.0 - lo
    return lo * jnp.sum(x * lo, axis=1, keepdims=True) + hi * jnp.sum(x * hi, axis=1, keepdims=True)


def _f_qk(x, cosf, sinf, gain2):
    xn = x * lax.rsqrt(_head_sums(x * x) * (1.0 / HEAD_DIM) + RMS_EPS) * gain2
    return _rope(xn, cosf, sinf)


def _qk_prep(z, tab, q_gain, k_gain):
    t = z.shape[0]
    tr = 1024

    def body(z_ref, c_ref, s_ref, qg_ref, kg_ref, o_ref):
        g = jnp.where(pl.program_id(0) < N_PAIRS, qg_ref[...], kg_ref[...])
        o_ref[...] = _f_qk(z_ref[...], c_ref[...], s_ref[...], jnp.concatenate([g, g], axis=1))

    gain = pl.BlockSpec((1, HEAD_DIM), lambda c, i: (0, 0))
    return pl.pallas_call(
        body, name="qk_prep", grid=(2 * N_PAIRS, t // tr),
        in_specs=[pl.BlockSpec((tr, PAIR), lambda c, i: (i, Q_COL0 + c)), pl.BlockSpec((tr, PAIR), lambda c, i: (i, 0)),
                  pl.BlockSpec((tr, PAIR), lambda c, i: (i, 1)), gain, gain],
        out_specs=pl.BlockSpec((tr, PAIR), lambda c, i: (i, c)),
        out_shape=jax.ShapeDtypeStruct((t, 2 * N_PAIRS * PAIR), F32),
        compiler_params=_cparams(dimension_semantics=("parallel", "parallel")),
    )(z, tab, tab, q_gain, k_gain)


def _qk_prep_bwd(z, tab, q_gain, k_gain, dq, dk):
    t = z.shape[0]
    tr = 1024

    def body(z_ref, c_ref, s_ref, qg_ref, kg_ref, dq_ref, dk_ref, dz_ref, dqg_ref, dkg_ref):
        c, i = pl.program_id(0), pl.program_id(1)
        is_q = c < N_PAIRS
        g = jnp.where(is_q, qg_ref[...], kg_ref[...])
        d = jnp.where(is_q, dq_ref[...], dk_ref[...])
        _, vjp = jax.vjp(lambda xx, gg: _f_qk(xx, c_ref[...], s_ref[...], gg), z_ref[...],
                         jnp.concatenate([g, g], axis=1))
        dx, dg2 = vjp(d)
        dz_ref[...] = dx.astype(dz_ref.dtype)
        dg = dg2[:, :HEAD_DIM] + dg2[:, HEAD_DIM:]
        first_q = jnp.logical_and(c == 0, i == 0)
        first_k = jnp.logical_and(c == N_PAIRS, i == 0)

        @pl.when(first_q)
        def _():
            dqg_ref[...] = dg

        @pl.when(jnp.logical_and(is_q, jnp.logical_not(first_q)))
        def _():
            dqg_ref[...] += dg

        @pl.when(first_k)
        def _():
            dkg_ref[...] = dg

        @pl.when(jnp.logical_and(jnp.logical_not(is_q), jnp.logical_not(first_k)))
        def _():
            dkg_ref[...] += dg

    gain = pl.BlockSpec((1, HEAD_DIM), lambda c, i: (0, 0))
    return pl.pallas_call(
        body, name="qk_prep_bwd", grid=(2 * N_PAIRS, t // tr),
        in_specs=[pl.BlockSpec((tr, PAIR), lambda c, i: (i, Q_COL0 + c)), pl.BlockSpec((tr, PAIR), lambda c, i: (i, 0)),
                  pl.BlockSpec((tr, PAIR), lambda c, i: (i, 1)), gain, gain,
                  pl.BlockSpec((tr, PAIR), lambda c, i: (i, jnp.minimum(c, N_PAIRS - 1))),
                  pl.BlockSpec((tr, PAIR), lambda c, i: (i, jnp.maximum(c - N_PAIRS, 0)))],
        out_specs=[pl.BlockSpec((tr, PAIR), lambda c, i: (i, c)), gain, gain],
        out_shape=[jax.ShapeDtypeStruct((t, 2 * N_PAIRS * PAIR), BF16), jax.ShapeDtypeStruct((1, HEAD_DIM), F32),
                   jax.ShapeDtypeStruct((1, HEAD_DIM), F32)],
        compiler_params=_cparams(dimension_semantics=("arbitrary", "arbitrary")),
    )(z, tab, tab, q_gain, k_gain, dq, dk)


def _attn_block(q, kp, kc, vp, vc, kmin):
    k2 = jnp.concatenate([kp, kc], axis=0)
    v2 = jnp.concatenate([vp, vc], axis=0)
    s = _bdot_nt(q, k2) * (HEAD_DIM ** -0.5)
    qi = lax.broadcasted_iota(jnp.int32, s.shape, 0)
    kj = lax.broadcasted_iota(jnp.int32, s.shape, 1)
    dist = qi + BAND_BLOCK - kj
    valid = (dist >= 0) & (dist <= BAND_BLOCK) & (kj >= kmin)
    s = jnp.where(valid, s, NEG_INF)
    m = lax.stop_gradient(jnp.max(s, axis=-1, keepdims=True))
    e = jnp.exp(s - m)
    l = jnp.sum(e, axis=-1, keepdims=True)
    o = _bdot(e, v2) / l
    return o, m + jnp.log(l)


def _fold(src_ref, dst_ref, dil):
    t = src_ref.shape[0]
    ln = t // dil
    for j in range(dil):
        dst_ref[j * ln:(j + 1) * ln, :] = src_ref[pl.ds(j, ln, stride=dil), :]


def _unfold(src_ref, dst_ref, dil):
    t = src_ref.shape[0]
    ln = t // dil
    for j in range(dil):
        dst_ref[pl.ds(j, ln, stride=dil), :] = src_ref[j * ln:(j + 1) * ln, :]


def _per_group(fn):
    pair = pl.program_id(0)
    for gi, (_, dil) in enumerate(ATTN_GROUPS):
        @pl.when(jnp.logical_or(pair == 2 * gi, pair == 2 * gi + 1))
        def _(dil=dil):
            fn(dil)


def _block_rows(idx, blocks_per_seq):
    first = (idx & (blocks_per_seq - 1)) == 0
    cur = pl.ds(pl.multiple_of(idx * BAND_BLOCK, BAND_BLOCK), BAND_BLOCK)
    prev = pl.ds(pl.multiple_of(jnp.maximum(idx - 1, 0) * BAND_BLOCK, BAND_BLOCK), BAND_BLOCK)
    return first, cur, prev


def _heads(x):
    return x[:, :HEAD_DIM], x[:, HEAD_DIM:]


def _attn_fwd(qk, z):
    t = z.shape[0]
    n_blocks = t // BAND_BLOCK

    def body(q_ref, k_ref, v_ref, o_ref, lse_ref, qf, kf, vf, of, lf):
        def run(dil):
            _fold(q_ref, qf, dil)
            _fold(k_ref, kf, dil)
            _fold(v_ref, vf, dil)
            blocks_per_seq = n_blocks // dil

            def block(idx, carry):
                first, cur, prev = _block_rows(idx, blocks_per_seq)
                kmin = jnp.where(first, BAND_BLOCK, 0)
                outs, lses = [], []
                for q, kp, kc, vp, vc in zip(_heads(qf[cur, :]), _heads(kf[prev, :]), _heads(kf[cur, :]),
                                             _heads(vf[prev, :]), _heads(vf[cur, :])):
                    o, ls = _attn_block(q, kp, kc, vp, vc, kmin)
                    outs.append(o)
                    lses.append(jnp.broadcast_to(ls, o.shape))
                of[cur, :] = jnp.concatenate(outs, axis=1)
                lf[cur, :] = jnp.concatenate(lses, axis=1)
                return carry

            lax.fori_loop(0, n_blocks, block, 0, unroll=4)
            _unfold(of, o_ref, dil)
            _unfold(lf, lse_ref, dil)

        _per_group(run)

    slab = jax.ShapeDtypeStruct((t, N_PAIRS * PAIR), F32)
    out_spec = pl.BlockSpec((t, PAIR), lambda p: (0, p))
    return pl.pallas_call(
        body, name="attn_fwd", grid=(N_PAIRS,),
        in_specs=[pl.BlockSpec((t, PAIR), lambda p: (0, p)), pl.BlockSpec((t, PAIR), lambda p: (0, N_PAIRS + p)),
                  pl.BlockSpec((t, PAIR), lambda p: (0, V_COL0 + p))],
        out_specs=[out_spec, out_spec], out_shape=[slab, slab],
        scratch_shapes=[pltpu.VMEM((t, PAIR), F32)] * 5,
        compiler_params=_cparams(dimension_semantics=("parallel",)),
    )(qk, qk, z)


def _attn_bwd(qk, z, do, dlse):
    t = z.shape[0]
    n_blocks = t // BAND_BLOCK

    def body(q_ref, k_ref, v_ref, do_ref, dl_ref, dq_ref, dk_ref, dv_ref, qf, kf, vf, dof, dlf, dqf, dkf, dvf,
             dkpf, dvpf):
        def run(dil):
            for src, dst in ((q_ref, qf), (k_ref, kf), (v_ref, vf), (do_ref, dof), (dl_ref, dlf)):
                _fold(src, dst, dil)
            blocks_per_seq = n_blocks // dil

            def block(idx, carry):
                first, cur, prev = _block_rows(idx, blocks_per_seq)
                kmin = jnp.where(first, BAND_BLOCK, 0)
                grads = []
                for q, kp, kc, vp, vc, do_h, dl_h in zip(
                        _heads(qf[cur, :]), _heads(kf[prev, :]), _heads(kf[cur, :]), _heads(vf[prev, :]),
                        _heads(vf[cur, :]), _heads(dof[cur, :]), _heads(dlf[cur, :])):
                    _, vjp = jax.vjp(functools.partial(_attn_block, kmin=kmin), q, kp, kc, vp, vc)
                    grads.append(vjp((do_h, jnp.sum(dl_h, axis=1, keepdims=True))))
                dq, dkp, dkc, dvp, dvc = (jnp.concatenate([a, b], axis=1) for a, b in zip(*grads))
                dqf[cur, :], dkf[cur, :], dvf[cur, :], dkpf[cur, :], dvpf[cur, :] = dq, dkc, dvc, dkp, dvp
                return carry

            lax.fori_loop(0, n_blocks, block, 0, unroll=2)

            def join(idx, carry):
                first, cur, prev = _block_rows(idx, blocks_per_seq)

                @pl.when(jnp.logical_not(first))
                def _():
                    dkf[prev, :] += dkpf[cur, :]
                    dvf[prev, :] += dvpf[cur, :]

                return carry

            lax.fori_loop(0, n_blocks, join, 0)
            _unfold(dqf, dq_ref, dil)
            _unfold(dkf, dk_ref, dil)
            _unfold(dvf, dv_ref, dil)

        _per_group(run)

    slab = jax.ShapeDtypeStruct((t, N_PAIRS * PAIR), F32)
    own = pl.BlockSpec((t, PAIR), lambda p: (0, p))
    return pl.pallas_call(
        body, name="attn_bwd", grid=(N_PAIRS,),
        in_specs=[own, pl.BlockSpec((t, PAIR), lambda p: (0, N_PAIRS + p)),
                  pl.BlockSpec((t, PAIR), lambda p: (0, V_COL0 + p)), own, own],
        out_specs=[own] * 3, out_shape=[slab] * 3,
        scratch_shapes=[pltpu.VMEM((t, PAIR), F32)] * 10,
        compiler_params=_cparams(dimension_semantics=("parallel",)),
    )(qk, qk, z, do, dlse)


def _f_comb(o1, o2, o3, l1, l2, l3):
    m = jnp.maximum(jnp.maximum(l1, l2), l3)
    e1, e2, e3 = jnp.exp(l1 - m), jnp.exp(l2 - m), jnp.exp(l3 - m)
    den = e1 + e2 + e3
    return (e1 / den) * o1 + (e2 / den) * o2 + (e3 / den) * o3


def _all_gather_hbm(name, arrs):
    na = len(arrs)

    def body(*refs):
        x_refs, out_refs = refs[:na], refs[na:2 * na]
        send_sems, recv_sems, local_sems = refs[2 * na:]
        mx, my, mc = lax.axis_index("x"), lax.axis_index("y"), lax.axis_index("c")
        me, sibling = (mx, my, mc), (mx, my, 1 - mc)
        chips = [(1 - mx, my), (mx, 1 - my), (1 - mx, 1 - my)]

        def slot(a, px, py, pc):
            return out_refs[a].at[4 * px + 2 * py + pc]

        def copy(a, k, block, to, src=None):
            return pltpu.make_async_remote_copy(
                src_ref=slot(a, *block) if src is None else src, dst_ref=slot(a, *block),
                send_sem=send_sems.at[a, k], recv_sem=recv_sems.at[a, k], device_id=to, device_id_type=MESH)

        mine = [pltpu.make_async_copy(x_refs[a], slot(a, *me), local_sems.at[a]) for a in range(na)]
        for cp in mine:
            cp.start()
        first = []
        for a in range(na):
            first.append(copy(a, 0, me, sibling, src=x_refs[a]))
            first += [copy(a, 1 + j, me, (*chip, mc), src=x_refs[a]) for j, chip in enumerate(chips)]
        for cp in first:
            cp.start()
        passed = []
        for j, chip in enumerate(chips):
            for a in range(na):
                copy(a, 1 + j, (*chip, mc), me).wait_recv()
                passed.append(copy(a, 4 + j, (*chip, mc), sibling))
                passed[-1].start()
        for a in range(na):
            copy(a, 0, sibling, me).wait_recv()
            for j, chip in enumerate(chips):
                copy(a, 4 + j, (*chip, 1 - mc), me).wait_recv()
        for cp in first + passed:
            cp.wait_send()
        for cp in mine:
            cp.wait()

    hbm = pl.BlockSpec(memory_space=pl.ANY)
    return pl.pallas_call(
        body, name=name,
        out_shape=[jax.ShapeDtypeStruct((N_DEV,) + a.shape, a.dtype) for a in arrs],
        in_specs=[hbm] * na, out_specs=[hbm] * na,
        scratch_shapes=[pltpu.SemaphoreType.DMA((na, 7)), pltpu.SemaphoreType.DMA((na, 7)),
                        pltpu.SemaphoreType.DMA((na,))],
    )(*arrs)


def _all_gather_vmem(x):
    rws, cols = x.shape

    def body(x_ref, out_ref, send_sems, recv_sems):
        mx, my, mc = lax.axis_index("x"), lax.axis_index("y"), lax.axis_index("c")
        me, sibling = (mx, my, mc), (mx, my, 1 - mc)
        chips = [(1 - mx, my), (mx, 1 - my), (1 - mx, 1 - my)]

        def slot(px, py, pc):
            return out_ref.at[4 * px + 2 * py + pc]

        def copy(k, block, to, src=None):
            return pltpu.make_async_remote_copy(
                src_ref=slot(*block) if src is None else src, dst_ref=slot(*block),
                send_sem=send_sems.at[k], recv_sem=recv_sems.at[k], device_id=to, device_id_type=MESH)

        first = [copy(0, me, sibling, src=x_ref)]
        first += [copy(1 + j, me, (*chip, mc), src=x_ref) for j, chip in enumerate(chips)]
        for cp in first:
            cp.start()
        out_ref[4 * mx + 2 * my + mc] = x_ref[...]
        passed = [copy(4 + j, (*chip, mc), sibling) for j, chip in enumerate(chips)]
        for j, chip in enumerate(chips):
            copy(1 + j, (*chip, mc), me).wait_recv()
            passed[j].start()
        copy(0, sibling, me).wait_recv()
        for j, chip in enumerate(chips):
            copy(4 + j, (*chip, 1 - mc), me).wait_recv()
        for cp in first + passed:
            cp.wait_send()

    return pl.pallas_call(
        body, name="all_gather_small",
        out_shape=jax.ShapeDtypeStruct((N_DEV, rws, cols), x.dtype),
        in_specs=[pl.BlockSpec(memory_space=pltpu.VMEM)], out_specs=pl.BlockSpec(memory_space=pltpu.VMEM),
        scratch_shapes=[pltpu.SemaphoreType.DMA((7,)), pltpu.SemaphoreType.DMA((7,))],
    )(x)


def _all_to_all_hbm(name, arrs):
    na = len(arrs)

    def body(*refs):
        g_refs, out_refs = refs[:na], refs[na:2 * na]
        send_sems, recv_sems, local_sems = refs[2 * na:]
        mx, my, mc = lax.axis_index("x"), lax.axis_index("y"), lax.axis_index("c")
        me = 4 * mx + 2 * my + mc
        mine = [pltpu.make_async_copy(g_refs[a].at[me], out_refs[a].at[me], local_sems.at[a]) for a in range(na)]
        for cp in mine:
            cp.start()
        copies = []
        for k in range(1, N_DEV):
            px, py, pc = mx ^ (k >> 2), my ^ ((k >> 1) & 1), mc ^ (k & 1)
            peer = 4 * px + 2 * py + pc
            for a in range(na):
                copies.append(pltpu.make_async_remote_copy(
                    src_ref=g_refs[a].at[peer], dst_ref=out_refs[a].at[me], send_sem=send_sems.at[a, k - 1],
                    recv_sem=recv_sems.at[a, k - 1], device_id=(px, py, pc), device_id_type=MESH))
        for cp in copies:
            cp.start()
        for cp in copies:
            cp.wait_recv()
        for cp in copies:
            cp.wait_send()
        for cp in mine:
            cp.wait()

    hbm = pl.BlockSpec(memory_space=pl.ANY)
    return pl.pallas_call(
        body, name=name,
        out_shape=[jax.ShapeDtypeStruct(a.shape, a.dtype) for a in arrs],
        in_specs=[hbm] * na, out_specs=[hbm] * na,
        scratch_shapes=[pltpu.SemaphoreType.DMA((na, 7)), pltpu.SemaphoreType.DMA((na, 7)),
                        pltpu.SemaphoreType.DMA((na,))],
    )(*arrs)


def _scatter_copies(g_refs, land_refs, send_sems, recv_sems):
    mx, my, mc = lax.axis_index("x"), lax.axis_index("y"), lax.axis_index("c")
    me = 4 * mx + 2 * my + mc
    copies = []
    for k in range(1, N_DEV):
        px, py, pc = mx ^ (k >> 2), my ^ ((k >> 1) & 1), mc ^ (k & 1)
        peer = 4 * px + 2 * py + pc
        for a, (g_ref, land_ref) in enumerate(zip(g_refs, land_refs)):
            copies.append(pltpu.make_async_remote_copy(
                src_ref=g_ref.at[peer], dst_ref=land_ref.at[me], send_sem=send_sems.at[a * (N_DEV - 1) + k - 1],
                recv_sem=recv_sems.at[a * (N_DEV - 1) + k - 1], device_id=(px, py, pc), device_id_type=MESH))
    return copies


_HBM = pl.BlockSpec(memory_space=pltpu.HBM)
_SEM = pl.BlockSpec(memory_space=pltpu.SEMAPHORE)
_DATAFLOW = pltpu.SideEffectType.DATAFLOW_SIDE_EFFECTING


def _gather_copies(x_refs, land_refs, send_sems, recv_sems):
    mx, my, mc = lax.axis_index("x"), lax.axis_index("y"), lax.axis_index("c")
    me = 4 * mx + 2 * my + mc
    copies = []
    for k in range(1, N_DEV):
        px, py, pc = mx ^ (k >> 2), my ^ ((k >> 1) & 1), mc ^ (k & 1)
        for a, (x_ref, land_ref) in enumerate(zip(x_refs, land_refs)):
            copies.append(pltpu.make_async_remote_copy(
                src_ref=x_ref, dst_ref=land_ref.at[me], send_sem=send_sems.at[a * (N_DEV - 1) + k - 1],
                recv_sem=recv_sems.at[a * (N_DEV - 1) + k - 1], device_id=(px, py, pc), device_id_type=MESH))
    return copies


N_CHIPS = N_DEV // 2


def _pair_swap(name, arrs):
    na = len(arrs)

    def body(*refs):
        g_refs, out_refs = refs[:na], refs[na:2 * na]
        send_sems, recv_sems = refs[2 * na:]
        mx, my, mc = lax.axis_index("x"), lax.axis_index("y"), lax.axis_index("c")
        copies = [pltpu.make_async_remote_copy(
            src_ref=g_refs[a].at[2 * q + 1 - mc], dst_ref=out_refs[a].at[q], send_sem=send_sems.at[a, q],
            recv_sem=recv_sems.at[a, q], device_id=(mx, my, 1 - mc), device_id_type=MESH)
            for a in range(na) for q in range(N_CHIPS)]
        for cp in copies:
            cp.start()
        for cp in copies:
            cp.wait_recv()
        for cp in copies:
            cp.wait_send()

    hbm = pl.BlockSpec(memory_space=pl.ANY)
    return pl.pallas_call(
        body, name=name,
        out_shape=[jax.ShapeDtypeStruct((N_CHIPS,) + a.shape[1:], a.dtype) for a in arrs],
        in_specs=[hbm] * na, out_specs=[hbm] * na,
        scratch_shapes=[pltpu.SemaphoreType.DMA((na, N_CHIPS)), pltpu.SemaphoreType.DMA((na, N_CHIPS))],
    )(*arrs)


def _pair_add(name, g, half, core):
    _, r, c = g.shape
    tr = _tile(r, 512) if r % LANES == 0 else r

    def body(core_ref, g_ref, h_ref, o_ref):
        o_ref[...] = (g_ref[...].astype(F32) + h_ref[...].astype(F32)).astype(o_ref.dtype)

    return pl.pallas_call(
        body, name=name,
        grid_spec=pltpu.PrefetchScalarGridSpec(
            num_scalar_prefetch=1, grid=(N_CHIPS, r // tr),
            in_specs=[pl.BlockSpec((None, tr, c), lambda q, i, core_ref: (2 * q + core_ref[0], i, 0)),
                      pl.BlockSpec((None, tr, c), lambda q, i, core_ref: (q, i, 0))],
            out_specs=pl.BlockSpec((None, tr, c), lambda q, i, core_ref: (q, i, 0))),
        out_shape=jax.ShapeDtypeStruct((N_CHIPS, r, c), g.dtype),
        compiler_params=_cparams(dimension_semantics=("parallel", "parallel")),
    )(core, g, half)


def _chip_copies(h_refs, land_refs, send_sems, recv_sems):
    mx, my, mc = lax.axis_index("x"), lax.axis_index("y"), lax.axis_index("c")
    my_chip = 2 * mx + my
    copies = []
    for k in range(1, N_CHIPS):
        px, py = mx ^ (k >> 1), my ^ (k & 1)
        for a, (h_ref, land_ref) in enumerate(zip(h_refs, land_refs)):
            copies.append(pltpu.make_async_remote_copy(
                src_ref=h_ref.at[2 * px + py], dst_ref=land_ref.at[my_chip], send_sem=send_sems.at[a * (N_DEV - 1) + k - 1],
                recv_sem=recv_sems.at[a * (N_DEV - 1) + k - 1], device_id=(px, py, mc), device_id_type=MESH))
    return copies


def _exchange_start(name, copies, srcs, lands, after):
    na = len(srcs)

    def body(*refs):
        for cp in copies(refs[:na], refs[na:2 * na], refs[2 * na + 1], refs[2 * na + 2]):
            cp.start()
        refs[-1][...] = jnp.zeros_like(refs[-1])

    in_hbm = lambda a: pltpu.with_memory_space_constraint(a, pltpu.HBM)
    outs = pl.pallas_call(
        body, name=name,
        out_shape=(pltpu.SemaphoreType.DMA((na * (N_DEV - 1),)), pltpu.SemaphoreType.DMA((na * (N_DEV - 1),)),
                   *[pltpu.HBM(a.shape, a.dtype) for a in list(srcs) + list(lands)],
                   jax.ShapeDtypeStruct((8, LANES), F32)),
        in_specs=[_HBM] * (2 * na) + [pl.BlockSpec(memory_space=pl.ANY)],
        out_specs=(_SEM, _SEM, *[_HBM] * (2 * na), pl.BlockSpec(memory_space=pltpu.VMEM)),
        input_output_aliases={i: 2 + i for i in range(2 * na)},
        compiler_params=pltpu.CompilerParams(has_side_effects=_DATAFLOW),
    )(*[in_hbm(a) for a in srcs], *[in_hbm(a) for a in lands], after)
    return outs[0], outs[1], outs[2:2 + na], outs[2 + na:2 + 2 * na], outs[-1]


def _exchange_wait(name, copies, send_sems, recv_sems, srcs, lands, after):
    na = len(srcs)

    def body(*refs):
        for cp in copies(refs[:na], refs[na:2 * na], refs[2 * na], refs[2 * na + 1]):
            cp.wait_send()
            cp.wait_recv()

    outs = pl.pallas_call(
        body, name=name,
        out_shape=[pltpu.HBM(a.shape, a.dtype) for a in list(srcs) + list(lands)],
        in_specs=[_HBM] * (2 * na) + [_SEM, _SEM, pl.BlockSpec(memory_space=pl.ANY)], out_specs=[_HBM] * (2 * na),
        input_output_aliases={i: i for i in range(2 * na)},
        compiler_params=pltpu.CompilerParams(has_side_effects=_DATAFLOW),
    )(*srcs, *lands, send_sems, recv_sems, after)
    return outs[:na], outs[na:]


def _sum_slots(name, g, tr):
    _, rws, cols = g.shape

    def body(g_ref, o_ref):
        acc = g_ref[0].astype(F32)
        for j in range(1, N_DEV):
            acc = acc + g_ref[j].astype(F32)
        o_ref[...] = acc

    return pl.pallas_call(
        body, name=name, grid=(rws // tr,),
        in_specs=[pl.BlockSpec((N_DEV, tr, cols), lambda i: (0, i, 0))],
        out_specs=pl.BlockSpec((tr, cols), lambda i: (i, 0)),
        out_shape=jax.ShapeDtypeStruct((rws, cols), F32),
        compiler_params=_cparams(dimension_semantics=("parallel",)),
    )(g)


def _adam_math(wv, gv, mv, vv):
    mn = ADAM_B1 * mv + (1.0 - ADAM_B1) * gv
    vn = ADAM_B2 * vv + (1.0 - ADAM_B2) * jnp.square(gv)
    m_hat = mn / (1.0 - ADAM_B1 ** ADAM_STEP)
    v_hat = vn / (1.0 - ADAM_B2 ** ADAM_STEP)
    delta = -ADAM_LR * (m_hat / (jnp.sqrt(v_hat) + ADAM_EPS) + ADAM_WD * wv)
    return delta, mn, vn


def _adamw(name, w, g, m, v, tr):
    return _rowwise(name, _adam_math, [w, g, m, v], [], [(LANES, F32)] * 3, tr=tr)


def _adamw_slots(name, recv, own, rb, cb, cw, w, m, v, tr, order):
    nr, nc = w.shape
    n = recv.shape[0]

    def body(g_ref, own_ref, w_ref, m_ref, v_ref, order_ref, go_ref, d_ref, mo_ref, vo_ref):
        me = 2 * lax.axis_index("x") + lax.axis_index("y")
        if n == N_DEV:
            me = 2 * me + lax.axis_index("c")
        acc = None
        for s in range(n):
            part = jnp.where(me == s, own_ref[s], g_ref[s]).astype(F32)
            acc = part if acc is None else acc + part
        g = acc[:, :nc]
        go_ref[...] = g
        d_ref[...], mo_ref[...], vo_ref[...] = _adam_math(w_ref[...], g, m_ref[...], v_ref[...])

    nat = pl.BlockSpec((tr, nc), lambda i: (i, 0))
    slots = pl.BlockSpec((n, tr, cw), lambda i: (0, rb + i, cb))
    return pl.pallas_call(
        body, name=name, grid=(nr // tr,),
        in_specs=[slots, slots, nat, nat, nat, pl.BlockSpec(memory_space=pl.ANY)],
        out_specs=[nat] * 4, out_shape=[jax.ShapeDtypeStruct((nr, nc), F32)] * 4,
        compiler_params=_cparams(dimension_semantics=("parallel",)),
    )(recv, own, w, m, v, order)


def _local_blocks(w):
    pad_cols = lambda a: jnp.pad(a, ((0, 0), (0, FF_PAD - FF_SHARD)))
    pad_rows = lambda a: jnp.pad(a, ((0, FF_PAD - FF_SHARD), (0, 0)))
    gate_up = lambda tag: jnp.concatenate([pad_cols(w[tag + "_w_gate"]), pad_cols(w[tag + "_w_up"])], axis=1)
    blocks = {
        "ffn1_gu": gate_up("ffn1"), "ffn1_d": pad_rows(w["ffn1_w_down"]), "w_in": w["w_in"],
        "lora": jnp.concatenate([w["rwkv_w2"], w["rwkv_a2"], w["rwkv_g2"]], axis=0),
        "br": jnp.concatenate([w["w_br_rwkv"], w["w_br_attn"], w["ple_w_proj"]], axis=0),
        "w_out": w["w_out"], "ffn2_gu": gate_up("ffn2"), "ffn2_d": pad_rows(w["ffn2_w_down"]),
        "ple_gate": w["ple_w_gate"],
    }
    return {n: a.astype(BF16) for n, a in blocks.items()}


GATHER_GROUPS = {"head": ("ffn1_gu", "ffn1_d"), "mid": ("w_in", "lora"),
                 "rest": ("br", "w_out", "ffn2_gu", "ffn2_d", "ple_gate")}

SCATTER_GROUPS = {"tail": ("ple_gate", "ple_proj", "ffn2_gu", "ffn2_d"), "branch": ("w_out", "br"),
                  "mixer": ("lora", "w_in"),
                  "ffn1_down": ("ffn1_d",), "head": ("ffn1_gu",)}

TWO_LEVEL = ("mixer", "head")

ADAM_PLAN = (
    ("ffn1_w_gate", "ffn1_gu", 0, 0, FF_PAD, 256), ("ffn1_w_up", "ffn1_gu", 0, 1, FF_PAD, 256),
    ("ffn1_w_down", "ffn1_d", 0, 0, D_MODEL, FF_SHARD // 2), ("w_in", "w_in", 0, 0, IN_SHARD, 256),
    ("rwkv_w2", "lora", 0, 0, HEAD_DIM, 64), ("rwkv_a2", "lora", 1, 0, HEAD_DIM, 64),
    ("rwkv_g2", "lora", 2, 0, HEAD_DIM, 64),
    ("w_br_rwkv", "br", 0, 0, OUT_SHARD, 256), ("w_br_attn", "br", 2, 0, OUT_SHARD, 256),
    ("ple_w_proj", "ple_proj", 0, 0, OUT_SHARD, 256), ("w_out", "w_out", 0, 0, D_MODEL, OUT_SHARD),
    ("ffn2_w_gate", "ffn2_gu", 0, 0, FF_PAD, 256), ("ffn2_w_up", "ffn2_gu", 0, 1, FF_PAD, 256),
    ("ffn2_w_down", "ffn2_d", 0, 0, D_MODEL, FF_SHARD // 2), ("ple_w_gate", "ple_gate", 0, 0, D_MODEL, OUT_SHARD),
)


def _pack_small(arrs, rows):
    flat = jnp.concatenate([a.reshape(-1) for a in arrs])
    return jnp.pad(flat, (0, rows * LANES - flat.shape[0])).reshape(rows, LANES)


def _unpack_small(flat, like):
    flat = flat.reshape(-1)
    out, off = [], 0
    for a in like:
        out.append(flat[off:off + a.size].reshape(a.shape))
        off += a.size
    return out


def _local_step(x, p, pos, target, sm, wg, fetch, on_grads):
    t = x.shape[0]

    w_d1 = wg["ffn1_d"].reshape(FF_HID, D_MODEL)
    x1, ffn1_saved, _ = _ffn_fwd("ffn1", x, sm["ffn1_norm"], wg["ffn1_gu"], lambda after: w_d1)
    wg = {**wg, **fetch("mid", x1)}
    full_cols = lambda blk: blk.transpose(1, 0, 2).reshape(blk.shape[1], N_DEV * blk.shape[2])
    lora_w2 = full_cols(wg["lora"][:, :DECAY_LORA])
    lora_a2 = full_cols(wg["lora"][:, DECAY_LORA:DECAY_LORA + ICLR_LORA])
    lora_g2 = full_cols(wg["lora"][:, DECAY_LORA + ICLR_LORA:])
    h2 = _norm_fwd("mix_norm", x1, sm["mix_norm"])
    z = _mmc_nn("w_in", h2, wg["w_in"], 0, 0, IN_SHARD)
    z_g = (z, 2 * D_MODEL, (RWKV_COLS + 3 * ATTN_DIM) // (2 * D_MODEL))

    r, k, v, lo, gd = _shift_fwd(z, sm["rwkv_mu"])
    zero_lo = jnp.zeros((DECAY_LORA, RWKV_DIM), BF16)
    w2p = jnp.concatenate([lora_w2, zero_lo], axis=0).astype(F32)
    a2p = jnp.concatenate([zero_lo, lora_a2], axis=0).astype(F32)
    pre_params = [sm["rwkv_w0"], w2p, sm["rwkv_a0"], a2p, lora_g2.astype(F32), sm["rwkv_k_k"], sm["rwkv_k_a"]]
    wide = [(RWKV_DIM, F32)]
    k2, kk, a, decay, g = _rowwise("rwkv_pre", _f_pre, [k, lo, gd], pre_params, wide * 5)
    kkn, b = _pairwise("rwkv_kk", _f_kk, [kk, a], [], 2)
    scan_in = [u.reshape(t, RWKV_HEADS, HEAD_DIM) for u in (r, decay, k2, kkn, b)]
    v_rows = _to_v_rows(v)
    y_rows, states = _wkv_fwd(*scan_in, v_rows)
    y = _from_v_rows(y_rows)
    post_params = [sm["rwkv_gn_w"], sm["rwkv_gn_b"], sm["rwkv_r_k"]]
    post_rows = [y, r, k2, v, g]
    y_rwkv = _pairwise("rwkv_post", lambda *av: (_f_post(*av),), post_rows, post_params, 1)[0]

    inv_freq = 1.0 / (ROPE_THETA ** (jnp.arange(0, HEAD_DIM, 2, dtype=F32) / HEAD_DIM))
    freq2 = jnp.tile(inv_freq, 2 * PAIR // HEAD_DIM).reshape(1, PAIR)
    half = jnp.ones((HEAD_DIM // 2,), F32)
    sign2 = jnp.tile(jnp.concatenate([-half, half]), PAIR // HEAD_DIM).reshape(1, PAIR)

    def rope_table(posv, fr, sg):
        ang = posv * fr
        return (jnp.concatenate([jnp.cos(ang), jnp.sin(ang) * sg], axis=1),)
    tab = _rowwise("rope_table", rope_table, [pos.astype(F32).reshape(t, 1)], [freq2, sign2], [(2 * PAIR, F32)])[0]
    qk = _qk_prep(z, tab, sm["q_norm"], sm["k_norm"])
    o_all, lse_all = _attn_fwd(qk, z)
    gw = HEADS_PER_GROUP * HEAD_DIM

    def by_group(ov, lv):
        return [ov[:, i * gw:(i + 1) * gw] for i in range(3)] + [lv[:, i * gw:(i + 1) * gw] for i in range(3)]
    y_attn = _rowwise("attn_comb", lambda ov, lv: (_f_comb(*by_group(ov, lv)),), [o_all, lse_all], [], [(gw, F32)])[0]

    wg = {**wg, **fetch("rest", y_rwkv)}
    w_d2 = wg["ffn2_d"].reshape(FF_HID, D_MODEL)
    w_out = wg["w_out"].reshape(D_MODEL, D_MODEL)
    w_pg = wg["ple_gate"].reshape(D_MODEL, D_MODEL)
    w_brr = full_cols(wg["br"][:, :RWKV_DIM])
    w_bra = full_cols(wg["br"][:, RWKV_DIM:RWKV_DIM + gw])
    w_pp = full_cols(wg["br"][:, RWKV_DIM + gw:])
    u_r = _mm("br_rwkv", y_rwkv, w_brr, "nn")
    u_a = _mm("br_attn", y_attn, w_bra, "nn")

    def f_merge(zgr, zga, ur, ua):
        return _sigmoid(zgr) * ur + _sigmoid(zga) * ua
    merged = _rowwise("merge", lambda zg, ur, ua: (f_merge(zg[:, :D_MODEL], zg[:, D_MODEL:], ur, ua),),
                      [z_g, u_r, u_a], [], [(D_MODEL, BF16)])[0]
    x2 = _mm("w_out", merged, w_out, "nn", res=x1)
    x3, ffn2_saved, _ = _ffn_fwd("ffn2", x2, sm["ffn2_norm"], wg["ffn2_gu"], lambda after: w_d2)

    hn = _norm_fwd("ple_norm", x3, sm["ple_norm"])
    gz = _mm("ple_gate", hn, w_pg, "nn")
    pp = _mm("ple_proj", p, w_pp, "nn")

    def f_head(x3v, gzv, ppv, tg):
        sg = _sigmoid(gzv)
        err = x3v + sg * ppv - tg
        part = 0.5 * jnp.sum(jnp.mean(err * err, axis=-1, keepdims=True))
        dx4 = err * (1.0 / D_MODEL)
        return dx4, dx4 * ppv * sg * (1.0 - sg), dx4 * sg, jnp.full((1, LANES), part, F32)
    dx4, dgz, dpp, loss_row = _rowwise("ple_loss", f_head, [x3, gz, pp, target], [],
                                       [(D_MODEL, F32), (D_MODEL, BF16), (D_MODEL, BF16)], [(1, LANES)])
    loss = loss_row[0, 0]

    gs, gm = {}, {}
    row_blocks = lambda g: g.reshape(N_DEV, g.shape[0] // N_DEV, g.shape[1])
    dhn = _mm("ple_dhn", dgz, w_pg, "nt")
    gm["ple_gate"] = row_blocks(_mm("ple_dwgate", hn, dgz, "tn", out_dtype=BF16))
    col_blocks = lambda g: g.reshape(g.shape[0], N_DEV, g.shape[1] // N_DEV).transpose(1, 0, 2)
    gm["ple_proj"] = col_blocks(_mm("ple_dwproj", p, dpp, "tn", out_dtype=BF16))
    dx3, gs["ple_norm"] = _norm_bwd("ple_dnorm", x3, sm["ple_norm"], dhn, dx4)

    dx2, gs["ffn2_norm"], gm["ffn2_gu"], gm["ffn2_d"] = _ffn_bwd(
        "ffn2", x2, sm["ffn2_norm"], wg["ffn2_gu"], w_d2, ffn2_saved, dx3)
    tail_token = on_grads("tail", {n: gm.pop(n) for n in SCATTER_GROUPS["tail"]})

    dmerged = _mm("w_out_dmerged", dx2, w_out, "nt")
    gm["w_out"] = row_blocks(_mm("w_out_dw", merged, dx2, "tn", out_dtype=BF16))

    def merge_bwd(zg, ur, ua, dm):
        _, vjp = jax.vjp(f_merge, zg[:, :D_MODEL], zg[:, D_MODEL:], ur, ua)
        dzr, dza, dur, dua = vjp(dm)
        return jnp.concatenate([dzr, dza], axis=1), dur, dua
    dz_g, du_r, du_a = _rowwise("merge_bwd", merge_bwd, [z_g, u_r, u_a, dmerged], [],
                                [(2 * D_MODEL, BF16), (D_MODEL, BF16), (D_MODEL, BF16)])
    dy_rwkv = _mm("br_rwkv_dy", du_r, w_brr, "nt")
    dy_attn = _mm("br_attn_dy", du_a, w_bra, "nt")
    gm["br"] = jnp.concatenate([col_blocks(_mm("br_rwkv_dw", y_rwkv, du_r, "tn", out_dtype=BF16)),
                                col_blocks(_mm("br_attn_dw", y_attn, du_a, "tn", out_dtype=BF16))], axis=1)
    branch_token = on_grads("branch", {n: gm.pop(n) for n in SCATTER_GROUPS["branch"]})

    def comb_bwd(ov, lv, dyv):
        _, vjp = jax.vjp(_f_comb, *by_group(ov, lv))
        d = vjp(dyv)
        return jnp.concatenate(d[:3], axis=1), jnp.concatenate(d[3:], axis=1)
    do_all, dl_all = _rowwise("attn_comb_bwd", comb_bwd, [o_all, lse_all, dy_attn], [],
                              [(ATTN_DIM, F32), (ATTN_DIM, F32)])
    dq_all, dk_all, dv_all = _attn_bwd(qk, z, do_all, dl_all)
    dqk_raw, gs["q_norm"], gs["k_norm"] = _qk_prep_bwd(z, tab, sm["q_norm"], sm["k_norm"], dq_all, dk_all)

    def post_bwd(yv, rv, k2v, vv, gv, dv_, gnw, gnb, rk):
        _, vjp = jax.vjp(_f_post, yv, rv, k2v, vv, gv, gnw, gnb, rk)
        return vjp(dv_)
    dy, dr1, dk2a, dv1, dg, d_gnw, d_gnb, d_rk = _pairwise(
        "rwkv_post_bwd", post_bwd, post_rows + [dy_rwkv], [post_params[0] + tail_token[0, 0] + branch_token[0, 0]] + post_params[1:], 5, 3)
    gs["rwkv_gn_w"], gs["rwkv_gn_b"], gs["rwkv_r_k"] = d_gnw, d_gnb, d_rk
    dr2, ddecay, dk2b, dkkn, db, dv_rows = _wkv_bwd(*scan_in, v_rows, states, _to_v_rows(dy))
    dr2, ddecay, dk2b, dkkn, db = [u.reshape(t, RWKV_DIM) for u in (dr2, ddecay, dk2b, dkkn, db)]
    dv2 = _from_v_rows(dv_rows)

    def kk_bwd(kkv, av, dkknv, dbv, dra, drb, dva, dvb):
        _, vjp = jax.vjp(_f_kk, kkv, av)
        return (*vjp((dkknv, dbv)), dra + drb, dva + dvb)
    dkk, da, dr, dv = _pairwise("rwkv_kk_bwd", kk_bwd, [kk, a, dkkn, db, dr1, dr2, dv1, dv2], [], 4)

    def pre_bwd(kv, lov, gdv, dk2x, dk2y, dkkv, dav, ddec, dgv, w0, w2p_, a0, a2p_, g2, k_k, k_a):
        _, vjp = jax.vjp(_f_pre, kv, lov, gdv, w0, w2p_, a0, a2p_, g2, k_k, k_a)
        return vjp((dk2x + dk2y, dkkv, dav, ddec, dgv))
    lora_acc = (DECAY_LORA + ICLR_LORA, RWKV_DIM)
    dk, dlo, dgd, d_w0, d_w2p, d_a0, d_a2p, d_g2, d_kk, d_ka = _rowwise(
        "rwkv_pre_bwd", pre_bwd,
        [k, lo, gd, dk2a, dk2b, dkk, da, ddecay, dg],
        pre_params, [(RWKV_DIM, F32), (LANES, F32), (LANES, F32)],
        [(1, RWKV_DIM), lora_acc, (1, RWKV_DIM), lora_acc, (GATE_LORA, RWKV_DIM), (1, RWKV_DIM), (1, RWKV_DIM)])
    gs["rwkv_w0"], gs["rwkv_a0"], gs["rwkv_k_k"], gs["rwkv_k_a"] = d_w0, d_a0, d_kk, d_ka
    gm["lora"] = jnp.concatenate([col_blocks(d_w2p[:DECAY_LORA]), col_blocks(d_a2p[DECAY_LORA:]), col_blocks(d_g2)],
                                 axis=1).astype(BF16)
    dz_r, gs["rwkv_mu"] = _shift_bwd(z, sm["rwkv_mu"], dr, dk, dv, dlo, dgd)

    dz = jnp.concatenate([dz_r, dqk_raw, dv_all.astype(BF16), dz_g], axis=1)
    dh2 = _mmc_nt("w_in_dh", dz, wg["w_in"], 0, 0, IN_SHARD, D_MODEL)
    gm["w_in"] = _mmc_tn("w_in_dw", h2, dz, IN_SHARD)
    mixer_token = on_grads("mixer", {n: gm.pop(n) for n in SCATTER_GROUPS["mixer"]})
    dx1, gs["mix_norm"] = _norm_bwd("mix_dnorm", x1, sm["mix_norm"] + mixer_token[0, 0], dh2, dx2)

    dx0, gs["ffn1_norm"], gm["ffn1_gu"], _ = _ffn_bwd(
        "ffn1", x, sm["ffn1_norm"], wg["ffn1_gu"], w_d1, ffn1_saved, dx1,
        on_down=lambda blocks: on_grads("ffn1_down", {"ffn1_d": blocks}))
    return loss, dx0, gm, gs


def kernel(x, p, positions, ffn1_norm, ffn1_w_gate, ffn1_w_up, ffn1_w_down, mix_norm, w_in, rwkv_mu, rwkv_w0, rwkv_w2, rwkv_a0, rwkv_a2, rwkv_g2, rwkv_k_k, rwkv_k_a, rwkv_r_k, rwkv_gn_w, rwkv_gn_b, q_norm, k_norm, w_br_rwkv, w_br_attn, w_out, ffn2_norm, ffn2_w_gate, ffn2_w_up, ffn2_w_down, ple_norm, ple_w_gate, ple_w_proj, loss_target, m_ffn1_norm, m_ffn1_w_gate, m_ffn1_w_up, m_ffn1_w_down, m_mix_norm, m_w_in, m_rwkv_mu, m_rwkv_w0, m_rwkv_w2, m_rwkv_a0, m_rwkv_a2, m_rwkv_g2, m_rwkv_k_k, m_rwkv_k_a, m_rwkv_r_k, m_rwkv_gn_w, m_rwkv_gn_b, m_q_norm, m_k_norm, m_w_br_rwkv, m_w_br_attn, m_w_out, m_ffn2_norm, m_ffn2_w_gate, m_ffn2_w_up, m_ffn2_w_down, m_ple_norm, m_ple_w_gate, m_ple_w_proj, v_ffn1_norm, v_ffn1_w_gate, v_ffn1_w_up, v_ffn1_w_down, v_mix_norm, v_w_in, v_rwkv_mu, v_rwkv_w0, v_rwkv_w2, v_rwkv_a0, v_rwkv_a2, v_rwkv_g2, v_rwkv_k_k, v_rwkv_k_a, v_rwkv_r_k, v_rwkv_gn_w, v_rwkv_gn_b, v_q_norm, v_k_norm, v_w_br_rwkv, v_w_br_attn, v_w_out, v_ffn2_norm, v_ffn2_w_gate, v_ffn2_w_up, v_ffn2_w_down, v_ple_norm, v_ple_w_gate, v_ple_w_proj):
    args = locals()
    w = {n: args[n][0] for n in WEIGHTS}
    m = {n: args["m_" + n][0] for n in WEIGHTS}
    v = {n: args["v_" + n][0] for n in WEIGHTS}

    w_loc = _local_blocks(w)
    head = GATHER_GROUPS["head"]
    wg = dict(zip(head, _all_gather_hbm("gather_head", [w_loc[n] for n in head])))
    me = 4 * lax.axis_index("x") + 2 * lax.axis_index("y") + lax.axis_index("c")
    gathering, order_after = {}, wg[head[0]]
    for group in ("mid", "rest"):
        shards = [w_loc[n] for n in GATHER_GROUPS[group]]
        zones = [lax.dynamic_update_slice(lax.empty((N_DEV,) + a.shape, a.dtype), a[None], (me, 0, 0)) for a in shards]
        *gathering[group], order_after = _exchange_start("gather_start_" + group, _gather_copies, shards, zones,
                                                         order_after)

    def fetch(group, after):
        _, got = _exchange_wait("gather_wait_" + group, _gather_copies, *gathering[group], after)
        return dict(zip(GATHER_GROUPS[group], got))

    sm = {n: w[n].reshape(1, -1) for n in SMALL}
    sm["ffn1_norm"] = sm["ffn1_norm"] + order_after[0, 0]
    in_flight = {}

    core = lax.axis_index("c").astype(jnp.int32).reshape(1)

    def scatter_early(group, arrays):
        arrs = [arrays[n] for n in SCATTER_GROUPS[group]]
        copies, after = _scatter_copies, arrs[0]
        if group in TWO_LEVEL:
            halves = _pair_swap("scatter_pair_" + group, arrs)
            arrs = [_pair_add("scatter_add_%s_%s" % (group, n), a, hf, core)
                    for n, a, hf in zip(SCATTER_GROUPS[group], arrs, halves)]
            copies, after = _chip_copies, halves[0]
        *in_flight[group], token = _exchange_start("scatter_start_" + group, copies, arrs,
                                                   [lax.empty(a.shape, a.dtype) for a in arrs], after)
        return token
    loss_part, dx, gm, gs = _local_step(x[0], p[0, 0], positions[0], loss_target[0], sm, wg, fetch, scatter_early)
    loss = lax.psum(loss_part, ("x", "y", "c"))
    head_token = scatter_early("head", gm)
    recv, own = {}, {}

    def arrived(group, after):
        copies = _chip_copies if group in TWO_LEVEL else _scatter_copies
        sent, lands = _exchange_wait("scatter_wait_" + group, copies, *in_flight[group], after)
        own.update(zip(SCATTER_GROUPS[group], sent))
        recv.update(zip(SCATTER_GROUPS[group], lands))
    for group in ("tail", "branch", "mixer", "ffn1_down"):
        arrived(group, head_token)
    small_like = [w[n] for n in SMALL]
    small_rows = 80
    gs_all = _all_gather_vmem(_pack_small([gs[n] for n in SMALL], small_rows))
    gs_sum = _sum_slots("sum_small_grads", gs_all, small_rows)

    res = {}
    early = [e for group in ("tail", "branch", "mixer", "ffn1_down") for e in ADAM_PLAN
             if e[1] in SCATTER_GROUPS[group]]
    late = [e for e in ADAM_PLAN if e[1] in SCATTER_GROUPS["head"]]
    order = head_token
    for n, src, rb, cb, cw, tr in early + late:
        if (n, src, rb, cb, cw, tr) == late[0]:
            arrived("head", order)
        outs4 = _adamw_slots("adamw_" + n, recv[src], own[src], rb, cb, cw, w[n], m[n], v[n], tr, order)
        order = outs4[1]
        for tag, a in zip(("grad", "delta", "new_m", "new_v"), outs4):
            res[tag, n] = a[None]
    d_s, m_s, v_s = _adamw("adamw_small", _pack_small(small_like, small_rows), gs_sum,
                           _pack_small([m[n] for n in SMALL], small_rows),
                           _pack_small([v[n] for n in SMALL], small_rows), small_rows)
    for tag, small in (("grad", gs_sum), ("delta", d_s), ("new_m", m_s), ("new_v", v_s)):
        for n, a in zip(SMALL, _unpack_small(small, small_like)):
            res[tag, n] = a[None]
    outs = [loss, dx[None]]
    for tag in ("grad", "delta", "new_m", "new_v"):
        outs += [res[tag, n] for n in WEIGHTS]
    return tuple(outs)
```

```python
import functools

import jax
import jax.numpy as jnp
from jax import lax
from jax.experimental import pallas as pl
from jax.experimental.pallas import tpu as pltpu

F32, BF16 = jnp.float32, jnp.bfloat16
MESH = pl.DeviceIdType.MESH
N_DEV = 8
LANES = 128
VMEM_LIMIT = 56 * 1024 * 1024

D_MODEL = 1024
PLE_DIM = 256
HEAD_DIM = 64
RWKV_HEADS = 8
RWKV_DIM = RWKV_HEADS * HEAD_DIM
DECAY_LORA = 64
ICLR_LORA = 64
GATE_LORA = 128
GN_EPS = 64e-5
ATTN_GROUPS = ((128, 1), (512, 4), (2048, 16))
HEADS_PER_GROUP = 4
ATTN_HEADS = HEADS_PER_GROUP * len(ATTN_GROUPS)
ATTN_DIM = ATTN_HEADS * HEAD_DIM
BAND_BLOCK = 128
ROPE_THETA = 10000.0
NEG_INF = -1e30
D_FF = 2816
RMS_EPS = 1e-6
RWKV_COLS = 3 * RWKV_DIM + DECAY_LORA + ICLR_LORA + GATE_LORA
ADAM_LR = 0.001
ADAM_B1 = 0.9
ADAM_B2 = 0.999
ADAM_EPS = 1e-08
ADAM_WD = 0.01
ADAM_STEP = 10

V_LO = LANES // RWKV_HEADS
V_HI = HEAD_DIM // V_LO
SCAN_CHUNK = 64
MM_ROWS = 2048

FF_SHARD = D_FF // N_DEV
FF_PAD = -(-FF_SHARD // LANES) * LANES
FF_HID = N_DEV * FF_PAD
IN_SHARD = 6144 // N_DEV
OUT_SHARD = D_MODEL // N_DEV

SMALL = ("ffn1_norm", "mix_norm", "rwkv_mu", "rwkv_w0", "rwkv_a0", "rwkv_k_k", "rwkv_k_a", "rwkv_r_k",
         "rwkv_gn_w", "rwkv_gn_b", "q_norm", "k_norm", "ffn2_norm", "ple_norm")
WEIGHTS = ("ffn1_norm", "ffn1_w_gate", "ffn1_w_up", "ffn1_w_down", "mix_norm", "w_in", "rwkv_mu", "rwkv_w0",
           "rwkv_w2", "rwkv_a0", "rwkv_a2", "rwkv_g2", "rwkv_k_k", "rwkv_k_a", "rwkv_r_k", "rwkv_gn_w",
           "rwkv_gn_b", "q_norm", "k_norm", "w_br_rwkv", "w_br_attn", "w_out", "ffn2_norm", "ffn2_w_gate",
           "ffn2_w_up", "ffn2_w_down", "ple_norm", "ple_w_gate", "ple_w_proj")


def _cparams(**kw):
    return pltpu.CompilerParams(vmem_limit_bytes=VMEM_LIMIT, **kw)


def _tile(n, cap):
    best = None
    for t in range(LANES, min(n, cap) + 1, LANES):
        if n % t == 0:
            best = t
    return best if best is not None else n


@jax.custom_vjp
def _bdot(a, w):
    return jnp.dot(a.astype(BF16), w.astype(BF16), preferred_element_type=F32)


def _bdot_fwd(a, w):
    return _bdot(a, w), (a, w)


def _bdot_bwd(res, g):
    a, w = res
    gb = g.astype(BF16)
    da = lax.dot_general(gb, w.astype(BF16), (((1,), (1,)), ((), ())), preferred_element_type=F32)
    dw = lax.dot_general(a.astype(BF16), gb, (((0,), (0,)), ((), ())), preferred_element_type=F32)
    return da.astype(a.dtype), dw.astype(w.dtype)


_bdot.defvjp(_bdot_fwd, _bdot_bwd)


@jax.custom_vjp
def _bdot_nt(a, b):
    return lax.dot_general(a.astype(BF16), b.astype(BF16), (((1,), (1,)), ((), ())), preferred_element_type=F32)


def _bdot_nt_fwd(a, b):
    return _bdot_nt(a, b), (a, b)


def _bdot_nt_bwd(res, g):
    a, b = res
    gb = g.astype(BF16)
    da = jnp.dot(gb, b.astype(BF16), preferred_element_type=F32)
    db = lax.dot_general(gb, a.astype(BF16), (((0,), (0,)), ((), ())), preferred_element_type=F32)
    return da.astype(a.dtype), db.astype(b.dtype)


_bdot_nt.defvjp(_bdot_nt_fwd, _bdot_nt_bwd)


def _mm(name, a, b, mode, out_dtype=F32, res=None, scale=None):
    if mode == "nn":
        (m, k), n = a.shape, b.shape[1]
    elif mode == "nt":
        (m, k), n = a.shape, b.shape[0]
    else:
        (k, m), n = a.shape, b.shape[1]
    tm, tn = _tile(m, 512 if mode == "tn" else MM_ROWS), _tile(n, 512)
    a_spec = pl.BlockSpec((k, tm), lambda i, j: (0, i)) if mode == "tn" else pl.BlockSpec((tm, k), lambda i, j: (i, 0))
    b_spec = pl.BlockSpec((tn, k), lambda i, j: (j, 0)) if mode == "nt" else pl.BlockSpec((k, tn), lambda i, j: (0, j))
    dims = {"nn": ((1,), (0,)), "nt": ((1,), (1,)), "tn": ((0,), (0,))}[mode]
    o_spec = pl.BlockSpec((tm, tn), lambda i, j: (i, j))
    ins, in_specs = [a, b], [a_spec, b_spec]
    if res is not None:
        ins.append(res)
        in_specs.append(o_spec)

    def body(*refs):
        acc = lax.dot_general(refs[0][...].astype(BF16), refs[1][...].astype(BF16), (dims, ((), ())),
                              preferred_element_type=F32)
        if scale is not None:
            acc = acc * scale
        if res is not None:
            acc = acc + refs[2][...].astype(F32)
        refs[-1][...] = acc.astype(refs[-1].dtype)

    return pl.pallas_call(
        body, name=name, grid=(m // tm, n // tn), in_specs=in_specs, out_specs=o_spec,
        out_shape=jax.ShapeDtypeStruct((m, n), out_dtype),
        compiler_params=_cparams(dimension_semantics=("parallel", "parallel")),
    )(*ins)


def _mmc_nn(name, a, wb, ki, ci, n, out_dtype=F32):
    m, k = a.shape
    tm = _tile(m, MM_ROWS)

    def body(a_ref, w_ref, o_ref):
        o_ref[...] = jnp.dot(a_ref[...].astype(BF16), w_ref[...], preferred_element_type=F32).astype(o_ref.dtype)

    return pl.pallas_call(
        body, name=name, grid=(m // tm, N_DEV),
        in_specs=[pl.BlockSpec((tm, k), lambda i, j: (i, 0)), pl.BlockSpec((None, k, n), lambda i, j: (j, ki, ci))],
        out_specs=pl.BlockSpec((tm, n), lambda i, j: (i, j)),
        out_shape=jax.ShapeDtypeStruct((m, N_DEV * n), out_dtype),
        compiler_params=_cparams(dimension_semantics=("parallel", "parallel")),
    )(a, wb)


def _mmc_nt(name, a, wb, ki, ci, n, k, res=None):
    m = a.shape[0]
    tm = _tile(m, MM_ROWS)
    o_spec = pl.BlockSpec((tm, k), lambda i, j: (i, 0))
    ins = [a, wb] + ([res] if res is not None else [])
    in_specs = [pl.BlockSpec((tm, n), lambda i, j: (i, j)), pl.BlockSpec((None, k, n), lambda i, j: (j, ki, ci))]
    in_specs += [o_spec] if res is not None else []

    def body(*refs):
        a_ref, w_ref, o_ref = refs[0], refs[1], refs[-1]
        acc = lax.dot_general(a_ref[...].astype(BF16), w_ref[...], (((1,), (1,)), ((), ())),
                              preferred_element_type=F32)

        @pl.when(pl.program_id(1) == 0)
        def _():
            o_ref[...] = acc + refs[2][...] if res is not None else acc

        @pl.when(pl.program_id(1) != 0)
        def _():
            o_ref[...] += acc

    return pl.pallas_call(
        body, name=name, grid=(m // tm, N_DEV), in_specs=in_specs, out_specs=o_spec,
        out_shape=jax.ShapeDtypeStruct((m, k), F32),
        compiler_params=_cparams(dimension_semantics=("parallel", "arbitrary")),
    )(*ins)


def _mmc_tn(name, x, dy, n):
    m, k = x.shape
    tk = _tile(k, 1024)

    def body(x_ref, dy_ref, o_ref):
        o_ref[...] = lax.dot_general(x_ref[...].astype(BF16), dy_ref[...].astype(BF16), (((0,), (0,)), ((), ())),
                                     preferred_element_type=F32).astype(o_ref.dtype)

    return pl.pallas_call(
        body, name=name, grid=(N_DEV, k // tk),
        in_specs=[pl.BlockSpec((m, tk), lambda j, i: (0, i)), pl.BlockSpec((m, n), lambda j, i: (0, j))],
        out_specs=pl.BlockSpec((None, tk, n), lambda j, i: (j, i, 0)),
        out_shape=jax.ShapeDtypeStruct((N_DEV, k, n), BF16),
        compiler_params=_cparams(dimension_semantics=("parallel", "parallel")),
    )(x, dy)


def _rowwise(name, fn, rows, params, out_rows, out_accs=(), tr=512):
    rows = [a if isinstance(a, tuple) else (a, a.shape[1], 0) for a in rows]
    r = rows[0][0].shape[0]
    in_specs = [pl.BlockSpec((tr, wd), lambda i, cb=cb: (i, cb)) for _, wd, cb in rows]
    rows = [a for a, _, _ in rows]
    in_specs += [pl.BlockSpec(p.shape, lambda i, nd=p.ndim: (0,) * nd) for p in params]
    out_shape = [jax.ShapeDtypeStruct((r, c), dt) for c, dt in out_rows]
    out_shape += [jax.ShapeDtypeStruct(s, F32) for s in out_accs]
    out_specs = [pl.BlockSpec((tr, c), lambda i: (i, 0)) for c, _ in out_rows]
    out_specs += [pl.BlockSpec(s, lambda i, nd=len(s): (0,) * nd) for s in out_accs]
    n_in, n_ro = len(rows) + len(params), len(out_rows)

    def body(*refs):
        res = fn(*[ref[...] for ref in refs[:n_in]])
        outs = refs[n_in:]
        for o, v in zip(outs[:n_ro], res[:n_ro]):
            o[...] = v.astype(o.dtype)
        for o, v in zip(outs[n_ro:], res[n_ro:]):
            _accumulate(o, v)

    return pl.pallas_call(
        body, name=name, grid=(r // tr,), in_specs=in_specs, out_specs=out_specs, out_shape=out_shape,
        compiler_params=_cparams(dimension_semantics=("arbitrary",)),
    )(*rows, *params)


def _pairwise(name, fn, rows, params, n_out, n_acc=0, tr=512):
    t, c = rows[0].shape
    tile = pl.BlockSpec((tr, 2 * HEAD_DIM), lambda p, i: (i, p))
    vec = pl.BlockSpec((1, 2 * HEAD_DIM), lambda p, i: (0, p))
    n_in = len(rows) + len(params)

    def body(*refs):
        res = fn(*[ref[...] for ref in refs[:n_in]])
        outs = refs[n_in:]
        for o, v in zip(outs[:n_out], res[:n_out]):
            o[...] = v
        first = pl.program_id(1) == 0
        for o, v in zip(outs[n_out:], res[n_out:]):
            @pl.when(first)
            def _(o=o, v=v):
                o[...] = v

            @pl.when(jnp.logical_not(first))
            def _(o=o, v=v):
                o[...] += v

    return pl.pallas_call(
        body, name=name, grid=(c // (2 * HEAD_DIM), t // tr),
        in_specs=[tile] * len(rows) + [vec] * len(params), out_specs=[tile] * n_out + [vec] * n_acc,
        out_shape=[jax.ShapeDtypeStruct((t, c), F32)] * n_out + [jax.ShapeDtypeStruct((1, c), F32)] * n_acc,
        compiler_params=_cparams(dimension_semantics=("parallel", "arbitrary")),
    )(*rows, *params)


def _accumulate(o_ref, v):
    @pl.when(pl.program_id(0) == 0)
    def _():
        o_ref[...] = v

    @pl.when(pl.program_id(0) != 0)
    def _():
        o_ref[...] += v


def _rms(x, g):
    return x * lax.rsqrt(jnp.mean(x * x, axis=-1, keepdims=True) + RMS_EPS) * g


def _sigmoid(x):
    return jax.nn.sigmoid(x)


def _softplus(x):
    return jnp.maximum(x, 0.0) + jnp.log1p(jnp.exp(-jnp.abs(x)))


def _norm_fwd(name, x, g):
    return _rowwise(name, lambda xv, gv: (_rms(xv, gv),), [x], [g], [(x.shape[1], BF16)])[0]


def _norm_bwd(name, x, g, dh, dres):
    def fn(xv, dhv, drv, gv):
        _, vjp = jax.vjp(_rms, xv, gv)
        dx, dg = vjp(dhv)
        return dx + drv, dg
    return _rowwise(name, fn, [x, dh, dres], [g], [(x.shape[1], F32)], [g.shape])


def _f_act(gate, up):
    return gate * _sigmoid(gate) * up


def _gate_up_act(name, h, w_gu):
    m, k = h.shape
    tm = _tile(m, MM_ROWS)

    def body(h_ref, w_ref, gu_ref, a_ref):
        gu = jnp.dot(h_ref[...], w_ref[...], preferred_element_type=F32)
        gu_ref[...] = gu
        a_ref[...] = _f_act(gu[:, :FF_PAD], gu[:, FF_PAD:]).astype(a_ref.dtype)

    return pl.pallas_call(
        body, name=name, grid=(m // tm, N_DEV),
        in_specs=[pl.BlockSpec((tm, k), lambda i, j: (i, 0)),
                  pl.BlockSpec((None, k, 2 * FF_PAD), lambda i, j: (j, 0, 0))],
        out_specs=[pl.BlockSpec((tm, 2 * FF_PAD), lambda i, j: (i, j)), pl.BlockSpec((tm, FF_PAD), lambda i, j: (i, j))],
        out_shape=[jax.ShapeDtypeStruct((m, N_DEV * 2 * FF_PAD), F32), jax.ShapeDtypeStruct((m, FF_HID), BF16)],
        compiler_params=_cparams(dimension_semantics=("parallel", "parallel")),
    )(h, w_gu)


def _gate_up_act_bwd(name, dout, w_down, gu, order):
    m, k = dout.shape
    tm = _tile(m, MM_ROWS)

    def body(d_ref, w_ref, gu_ref, order_ref, o_ref):
        da = 0.5 * lax.dot_general(d_ref[...].astype(BF16), w_ref[...], (((1,), (1,)), ((), ())),
                                   preferred_element_type=F32)
        guv = gu_ref[...]
        _, vjp = jax.vjp(_f_act, guv[:, :FF_PAD], guv[:, FF_PAD:])
        o_ref[...] = jnp.concatenate(vjp(da), axis=1).astype(o_ref.dtype)

    gu_spec = pl.BlockSpec((tm, 2 * FF_PAD), lambda i, j: (i, j))
    return pl.pallas_call(
        body, name=name, grid=(m // tm, N_DEV),
        in_specs=[pl.BlockSpec((tm, k), lambda i, j: (i, 0)), pl.BlockSpec((FF_PAD, k), lambda i, j: (j, 0)), gu_spec,
                  pl.BlockSpec(memory_space=pl.ANY)],
        out_specs=gu_spec, out_shape=jax.ShapeDtypeStruct((m, N_DEV * 2 * FF_PAD), BF16),
        compiler_params=_cparams(dimension_semantics=("parallel", "parallel")),
    )(dout, w_down, gu, order)


def _ffn_fwd(tag, x, norm, w_gu, w_down):
    h = _norm_fwd(tag + "_norm", x, norm)
    gu, a = _gate_up_act(tag + "_gu", h, w_gu)
    wd = w_down(a)
    out = _mm(tag + "_down", a, wd, "nn", res=x, scale=0.5)
    return out, (h, gu, a), wd


def _ffn_bwd(tag, x, norm, w_gu, w_down, saved, dout, on_down=None):
    h, gu, a = saved
    d_wdown = _mm(tag + "_dwdown", a, dout, "tn", out_dtype=BF16, scale=0.5).reshape(N_DEV, FF_PAD, D_MODEL)
    token = on_down(d_wdown) if on_down is not None else jnp.zeros((8, LANES), F32)
    dgu = _gate_up_act_bwd(tag + "_dgu", dout, w_down, gu, token)
    dh =_mmc_nt(tag + "_dh", dgu, w_gu, 0, 0, 2 * FF_PAD, D_MODEL)
    d_wgu = _mmc_tn(tag + "_dwgu", h, dgu, 2 * FF_PAD)
    dx, dnorm = _norm_bwd(tag + "_dnorm", x, norm, dh, dout)
    return dx, dnorm, d_wgu, d_wdown


def _shift_fwd(z, mu):
    t, c = z.shape[0], RWKV_COLS
    tr = 256

    def body(z_ref, zp_ref, mu_ref, r_ref, k_ref, v_ref, lo_ref, gd_ref):
        zv = z_ref[...]
        prev = zp_ref[7:8, :] * jnp.where(pl.program_id(0) == 0, 0.0, 1.0)
        row = lax.broadcasted_iota(jnp.int32, zv.shape, 0)
        zsh = jnp.where(row == 0, prev, pltpu.roll(zv, 1, 0))
        zs = zv + (zsh - zv) * mu_ref[...]
        r_ref[...] = zs[:, 0:512]
        k_ref[...] = zs[:, 512:1024]
        v_ref[...] = zs[:, 1024:1536]
        lo_ref[...] = zs[:, 1536:1664]
        gd_ref[...] = zs[:, 1664:1792]

    widths = (512, 512, 512, 128, 128)
    return pl.pallas_call(
        body, name="rwkv_shift", grid=(t // tr,),
        in_specs=[pl.BlockSpec((tr, c), lambda i: (i, 0)),
                  pl.BlockSpec((8, c), lambda i: (jnp.maximum(i * (tr // 8) - 1, 0), 0)),
                  pl.BlockSpec((1, c), lambda i: (0, 0))],
        out_specs=[pl.BlockSpec((tr, w), lambda i: (i, 0)) for w in widths],
        out_shape=[jax.ShapeDtypeStruct((t, w), F32) for w in widths],
        compiler_params=_cparams(dimension_semantics=("parallel",)),
    )(z, z, mu)


def _shift_bwd(z, mu, dr, dk, dv, dlo, dgd):
    t, c = z.shape[0], RWKV_COLS
    tr = 256
    nt = t // tr

    def body(z_ref, zp_ref, mu_ref, dr_ref, dk_ref, dv_ref, dlo_ref, dgd_ref,
             drn_ref, dkn_ref, dvn_ref, dlon_ref, dgdn_ref, dz_ref, dmu_ref):
        i = pl.program_id(0)
        zv, muv = z_ref[...], mu_ref[...]
        prev = zp_ref[7:8, :] * jnp.where(i == 0, 0.0, 1.0)
        row = lax.broadcasted_iota(jnp.int32, zv.shape, 0)
        zsh = jnp.where(row == 0, prev, pltpu.roll(zv, 1, 0))
        dzs = jnp.concatenate([dr_ref[...], dk_ref[...], dv_ref[...], dlo_ref[...], dgd_ref[...]], axis=1)
        nxt = jnp.concatenate([drn_ref[0:1, :], dkn_ref[0:1, :], dvn_ref[0:1, :], dlon_ref[0:1, :],
                               dgdn_ref[0:1, :]], axis=1) * jnp.where(i == nt - 1, 0.0, 1.0)
        u = dzs * muv
        un = jnp.where(row == tr - 1, nxt * muv, pltpu.roll(u, tr - 1, 0))
        dz_ref[...] = (dzs - u + un).astype(dz_ref.dtype)
        _accumulate(dmu_ref, jnp.sum(dzs * (zsh - zv), axis=0, keepdims=True))

    widths = (512, 512, 512, 128, 128)
    nxt_map = lambda i: (jnp.minimum((i + 1) * (tr // 8), t // 8 - 1), 0)
    return pl.pallas_call(
        body, name="rwkv_shift_bwd", grid=(nt,),
        in_specs=[pl.BlockSpec((tr, c), lambda i: (i, 0)),
                  pl.BlockSpec((8, c), lambda i: (jnp.maximum(i * (tr // 8) - 1, 0), 0)),
                  pl.BlockSpec((1, c), lambda i: (0, 0))]
        + [pl.BlockSpec((tr, w), lambda i: (i, 0)) for w in widths]
        + [pl.BlockSpec((8, w), nxt_map) for w in widths],
        out_specs=[pl.BlockSpec((tr, c), lambda i: (i, 0)), pl.BlockSpec((1, c), lambda i: (0, 0))],
        out_shape=[jax.ShapeDtypeStruct((t, c), BF16), jax.ShapeDtypeStruct((1, c), F32)],
        compiler_params=_cparams(dimension_semantics=("arbitrary",)),
    )(z, z, mu, dr, dk, dv, dlo, dgd, dr, dk, dv, dlo, dgd)


def _f_pre(k, lo, gd, w0, w2p, a0, a2p, g2, k_k, k_a):
    lane = lax.broadcasted_iota(jnp.int32, lo.shape, 1)
    lo_act = jnp.where(lane < DECAY_LORA, jnp.tanh(lo), lo)
    w = -_softplus(-(w0 + _bdot(lo_act, w2p))) - 0.5
    a = _sigmoid(a0 + _bdot(lo_act, a2p))
    g = _bdot(_sigmoid(gd), g2)
    kk = k * k_k
    k2 = k * (1.0 + (a - 1.0) * k_a)
    decay = jnp.exp(-jnp.exp(w))
    return k2, kk, a, decay, g


def _f_kk(kk, a):
    kkn = kk * lax.rsqrt(jnp.maximum(_head_sums(kk * kk), 1e-24))
    return kkn, kkn * a


def _f_post(y, r, k2, v, g, gn_w, gn_b, r_k):
    mean = _head_sums(y) * (1.0 / HEAD_DIM)
    var = _head_sums(jnp.square(y - mean)) * (1.0 / HEAD_DIM)
    yn = (y - mean) * lax.rsqrt(var + GN_EPS) * gn_w + gn_b
    bonus = _head_sums(r * k2 * r_k) * v
    return (yn + bonus) * g


def _to_v_rows(x):
    t = x.shape[0]
    return x.reshape(t, RWKV_HEADS, V_HI, V_LO).transpose(0, 2, 3, 1).reshape(t, V_HI, LANES)


def _from_v_rows(x):
    t = x.shape[0]
    return x.reshape(t, V_HI, V_LO, RWKV_HEADS).transpose(0, 3, 1, 2).reshape(t, RWKV_DIM)


def _k_cols(x):
    return jnp.tile(x, (V_LO, 1)).T


def _k_rows(x):
    xt = x.T
    out = xt[0:RWKV_HEADS]
    for l in range(1, V_LO):
        out = out + xt[l * RWKV_HEADS:(l + 1) * RWKV_HEADS]
    return out


def _wkv_fwd(r, w, k, kk, b, v):
    t = r.shape[0]
    tc = SCAN_CHUNK
    key_spec = pl.BlockSpec((tc, RWKV_HEADS, HEAD_DIM), lambda i: (i, 0, 0))
    row_spec = pl.BlockSpec((tc, V_HI, LANES), lambda i: (i, 0, 0))

    def body(r_ref, w_ref, k_ref, kk_ref, b_ref, v_ref, y_ref, st_ref, s_scr, cols_a, cols_b):
        @pl.when(pl.program_id(0) == 0)
        def _():
            s_scr[...] = jnp.zeros_like(s_scr)

        def prep(ti, buf):
            for n, ref in enumerate((r_ref, w_ref, k_ref, kk_ref, b_ref)):
                buf[n] = _k_cols(ref[ti])

        def step(ti, s, cur, nxt, ti_next):
            rc, wc, kc, kkc, bc = (cur[n] for n in range(5))
            prep(ti_next, nxt)
            vt = v_ref[ti]
            new, ys = [], []
            for j in range(V_HI):
                sa = -jnp.sum(s[j] * kkc, axis=0, keepdims=True)
                nj = s[j] * wc + bc * sa + kc * vt[j:j + 1]
                st_ref[ti, j] = nj
                ys.append(jnp.sum(nj * rc, axis=0, keepdims=True))
                new.append(nj)
            y_ref[ti] = jnp.concatenate(ys, axis=0)
            return tuple(new)

        def pair(i, s):
            s = step(2 * i, s, cols_a, cols_b, 2 * i + 1)
            return step(2 * i + 1, s, cols_b, cols_a, jnp.minimum(2 * i + 2, tc - 1))

        prep(0, cols_a)
        s = lax.fori_loop(0, tc // 2, pair, tuple(s_scr[j] for j in range(V_HI)))
        for j in range(V_HI):
            s_scr[j] = s[j]

    return pl.pallas_call(
        body, name="wkv_fwd", grid=(t // tc,),
        in_specs=[key_spec] * 5 + [row_spec],
        out_specs=[row_spec, pl.BlockSpec((tc, V_HI, HEAD_DIM, LANES), lambda i: (i, 0, 0, 0))],
        out_shape=[jax.ShapeDtypeStruct((t, V_HI, LANES), F32),
                   jax.ShapeDtypeStruct((t, V_HI, HEAD_DIM, LANES), F32)],
        scratch_shapes=[pltpu.VMEM((V_HI, HEAD_DIM, LANES), F32)] + [pltpu.VMEM((5, HEAD_DIM, LANES), F32)] * 2,
        compiler_params=_cparams(dimension_semantics=("arbitrary",)),
    )(r, w, k, kk, b, v)


def _wkv_bwd(r, w, k, kk, b, v, states, dy):
    t = r.shape[0]
    tc = SCAN_CHUNK
    nb = t // tc
    key_spec = pl.BlockSpec((tc, RWKV_HEADS, HEAD_DIM), lambda i: (nb - 1 - i, 0, 0))
    row_spec = pl.BlockSpec((tc, V_HI, LANES), lambda i: (nb - 1 - i, 0, 0))
    st_spec = pl.BlockSpec((tc, V_HI, HEAD_DIM, LANES), lambda i: (nb - 1 - i, 0, 0, 0))
    stp_spec = pl.BlockSpec((1, V_HI, HEAD_DIM, LANES), lambda i: (jnp.maximum((nb - 1 - i) * tc - 1, 0), 0, 0, 0))

    def body(r_ref, w_ref, k_ref, kk_ref, b_ref, v_ref, st_ref, stp_ref, dy_ref,
             dr_ref, dw_ref, dk_ref, dkk_ref, db_ref, dv_ref, ds_scr, cols_a, cols_b, accs_a, accs_b):
        @pl.when(pl.program_id(0) == 0)
        def _():
            ds_scr[...] = jnp.zeros_like(ds_scr)

        def colsum(x):
            return jnp.sum(x, axis=0, keepdims=True)

        def prep(ti, buf):
            for n, ref in enumerate((r_ref, w_ref, k_ref, kk_ref, b_ref)):
                buf[n] = _k_cols(ref[ti])

        def flush(ti, buf):
            for n, ref in enumerate((dr_ref, dk_ref, db_ref, dw_ref, dkk_ref)):
                ref[ti] = _k_rows(buf[n])

        def step(ti, ds, sp, cur, accs):
            rc, wc, kc, kkc, bc = (cur[n] for n in range(5))
            vt, dyt = v_ref[ti], dy_ref[ti]
            acc = None
            new, dvs = [], []
            for j in range(V_HI):
                st = st_ref[ti, j]
                dsj = ds[j] + rc * dyt[j:j + 1]
                sa = -colsum(sp[j] * kkc)
                dsa = colsum(dsj * bc)
                dvs.append(colsum(dsj * kc))
                parts = (st * dyt[j:j + 1], dsj * vt[j:j + 1], dsj * sa, dsj * sp[j], -(sp[j] * dsa))
                acc = parts if acc is None else tuple(a + q for a, q in zip(acc, parts))
                new.append(dsj * wc - kkc * dsa)
            dv_ref[ti] = jnp.concatenate(dvs, axis=0)
            for n in range(5):
                accs[n] = acc[n]
            return tuple(new)

        def states_before(ti):
            return tuple(st_ref[ti - 1, j] for j in range(V_HI))

        def pair(i, ds):
            ta = tc - 1 - 2 * i
            prep(ta - 1, cols_b)
            flush(jnp.minimum(ta + 1, tc - 1), accs_b)
            ds = step(ta, ds, states_before(ta), cols_a, accs_a)
            prep(ta - 2, cols_a)
            flush(ta, accs_a)
            return step(ta - 1, ds, states_before(ta - 1), cols_b, accs_b)

        prep(tc - 1, cols_a)
        accs_b[...] = jnp.zeros_like(accs_b)
        ds = lax.fori_loop(0, tc // 2 - 1, pair, tuple(ds_scr[j] for j in range(V_HI)))
        prep(0, cols_b)
        flush(2, accs_b)
        ds = step(1, ds, states_before(1), cols_a, accs_a)
        flush(1, accs_a)
        keep = jnp.where(pl.program_id(0) == nb - 1, 0.0, 1.0)
        ds = step(0, ds, tuple(stp_ref[0, j] * keep for j in range(V_HI)), cols_b, accs_b)
        flush(0, accs_b)
        for j in range(V_HI):
            ds_scr[j] = ds[j]

    key_out = jax.ShapeDtypeStruct((t, RWKV_HEADS, HEAD_DIM), F32)
    return pl.pallas_call(
        body, name="wkv_bwd", grid=(nb,),
        in_specs=[key_spec] * 5 + [row_spec, st_spec, stp_spec, row_spec],
        out_specs=[key_spec] * 5 + [row_spec],
        out_shape=[key_out] * 5 + [jax.ShapeDtypeStruct((t, V_HI, LANES), F32)],
        scratch_shapes=[pltpu.VMEM((V_HI, HEAD_DIM, LANES), F32)] + [pltpu.VMEM((5, HEAD_DIM, LANES), F32)] * 4,
        compiler_params=_cparams(dimension_semantics=("arbitrary",)),
    )(r, w, k, kk, b, v, states, states, dy)


PAIR = 2 * HEAD_DIM
N_PAIRS = ATTN_HEADS // 2
Q_COL0 = RWKV_COLS // PAIR
K_COL0 = Q_COL0 + N_PAIRS
V_COL0 = K_COL0 + N_PAIRS


def _swap_halves(x):
    lane = lax.broadcasted_iota(jnp.int32, x.shape, 1)
    return jnp.where((lane & (HEAD_DIM - 1)) < HEAD_DIM // 2, pltpu.roll(x, PAIR - HEAD_DIM // 2, 1),
                     pltpu.roll(x, HEAD_DIM // 2, 1))


@jax.custom_vjp
def _rope(x, cosf, sinf):
    return x * cosf + _swap_halves(x) * sinf


def _rope_fwd(x, cosf, sinf):
    return _rope(x, cosf, sinf), (cosf, sinf)


def _rope_bwd(res, d):
    cosf, sinf = res
    return d * cosf + _swap_halves(d * sinf), jnp.zeros_like(cosf), jnp.zeros_like(sinf)


_rope.defvjp(_rope_fwd, _rope_bwd)


def _head_sums(x):
    lane = lax.broadcasted_iota(jnp.int32, x.shape, 1)
    lo = jnp.where(lane < HEAD_DIM, 1.0, 0.0)
    hi = 1.0 - lo
    return lo * jnp.sum(x * lo, axis=1, keepdims=True) + hi * jnp.sum(x * hi, axis=1, keepdims=True)


def _f_qk(x, cosf, sinf, gain2):
    xn = x * lax.rsqrt(_head_sums(x * x) * (1.0 / HEAD_DIM) + RMS_EPS) * gain2
    return _rope(xn, cosf, sinf)


def _qk_prep(z, tab, q_gain, k_gain):
    t = z.shape[0]
    tr = 1024

    def body(z_ref, c_ref, s_ref, qg_ref, kg_ref, o_ref):
        g = jnp.where(pl.program_id(0) < N_PAIRS, qg_ref[...], kg_ref[...])
        o_ref[...] = _f_qk(z_ref[...], c_ref[...], s_ref[...], jnp.concatenate([g, g], axis=1))

    gain = pl.BlockSpec((1, HEAD_DIM), lambda c, i: (0, 0))
    return pl.pallas_call(
        body, name="qk_prep", grid=(2 * N_PAIRS, t // tr),
        in_specs=[pl.BlockSpec((tr, PAIR), lambda c, i: (i, Q_COL0 + c)), pl.BlockSpec((tr, PAIR), lambda c, i: (i, 0)),
                  pl.BlockSpec((tr, PAIR), lambda c, i: (i, 1)), gain, gain],
        out_specs=pl.BlockSpec((tr, PAIR), lambda c, i: (i, c)),
        out_shape=jax.ShapeDtypeStruct((t, 2 * N_PAIRS * PAIR), F32),
        compiler_params=_cparams(dimension_semantics=("parallel", "parallel")),
    )(z, tab, tab, q_gain, k_gain)


def _qk_prep_bwd(z, tab, q_gain, k_gain, dq, dk):
    t = z.shape[0]
    tr = 1024

    def body(z_ref, c_ref, s_ref, qg_ref, kg_ref, dq_ref, dk_ref, dz_ref, dqg_ref, dkg_ref):
        c, i = pl.program_id(0), pl.program_id(1)
        is_q = c < N_PAIRS
        g = jnp.where(is_q, qg_ref[...], kg_ref[...])
        d = jnp.where(is_q, dq_ref[...], dk_ref[...])
        _, vjp = jax.vjp(lambda xx, gg: _f_qk(xx, c_ref[...], s_ref[...], gg), z_ref[...],
                         jnp.concatenate([g, g], axis=1))
        dx, dg2 = vjp(d)
        dz_ref[...] = dx.astype(dz_ref.dtype)
        dg = dg2[:, :HEAD_DIM] + dg2[:, HEAD_DIM:]
        first_q = jnp.logical_and(c == 0, i == 0)
        first_k = jnp.logical_and(c == N_PAIRS, i == 0)

        @pl.when(first_q)
        def _():
            dqg_ref[...] = dg

        @pl.when(jnp.logical_and(is_q, jnp.logical_not(first_q)))
        def _():
            dqg_ref[...] += dg

        @pl.when(first_k)
        def _():
            dkg_ref[...] = dg

        @pl.when(jnp.logical_and(jnp.logical_not(is_q), jnp.logical_not(first_k)))
        def _():
            dkg_ref[...] += dg

    gain = pl.BlockSpec((1, HEAD_DIM), lambda c, i: (0, 0))
    return pl.pallas_call(
        body, name="qk_prep_bwd", grid=(2 * N_PAIRS, t // tr),
        in_specs=[pl.BlockSpec((tr, PAIR), lambda c, i: (i, Q_COL0 + c)), pl.BlockSpec((tr, PAIR), lambda c, i: (i, 0)),
                  pl.BlockSpec((tr, PAIR), lambda c, i: (i, 1)), gain, gain,
                  pl.BlockSpec((tr, PAIR), lambda c, i: (i, jnp.minimum(c, N_PAIRS - 1))),
                  pl.BlockSpec((tr, PAIR), lambda c, i: (i, jnp.maximum(c - N_PAIRS, 0)))],
        out_specs=[pl.BlockSpec((tr, PAIR), lambda c, i: (i, c)), gain, gain],
        out_shape=[jax.ShapeDtypeStruct((t, 2 * N_PAIRS * PAIR), BF16), jax.ShapeDtypeStruct((1, HEAD_DIM), F32),
                   jax.ShapeDtypeStruct((1, HEAD_DIM), F32)],
        compiler_params=_cparams(dimension_semantics=("arbitrary", "arbitrary")),
    )(z, tab, tab, q_gain, k_gain, dq, dk)


def _attn_block(q, kp, kc, vp, vc, kmin):
    k2 = jnp.concatenate([kp, kc], axis=0)
    v2 = jnp.concatenate([vp, vc], axis=0)
    s = _bdot_nt(q, k2) * (HEAD_DIM ** -0.5)
    qi = lax.broadcasted_iota(jnp.int32, s.shape, 0)
    kj = lax.broadcasted_iota(jnp.int32, s.shape, 1)
    dist = qi + BAND_BLOCK - kj
    valid = (dist >= 0) & (dist <= BAND_BLOCK) & (kj >= kmin)
    s = jnp.where(valid, s, NEG_INF)
    m = lax.stop_gradient(jnp.max(s, axis=-1, keepdims=True))
    e = jnp.exp(s - m)
    l = jnp.sum(e, axis=-1, keepdims=True)
    o = _bdot(e, v2) / l
    return o, m + jnp.log(l)


def _fold(src_ref, dst_ref, dil):
    t = src_ref.shape[0]
    ln = t // dil
    for j in range(dil):
        dst_ref[j * ln:(j + 1) * ln, :] = src_ref[pl.ds(j, ln, stride=dil), :]


def _unfold(src_ref, dst_ref, dil):
    t = src_ref.shape[0]
    ln = t // dil
    for j in range(dil):
        dst_ref[pl.ds(j, ln, stride=dil), :] = src_ref[j * ln:(j + 1) * ln, :]


def _per_group(fn):
    pair = pl.program_id(0)
    for gi, (_, dil) in enumerate(ATTN_GROUPS):
        @pl.when(jnp.logical_or(pair == 2 * gi, pair == 2 * gi + 1))
        def _(dil=dil):
            fn(dil)


def _block_rows(idx, blocks_per_seq):
    first = (idx & (blocks_per_seq - 1)) == 0
    cur = pl.ds(pl.multiple_of(idx * BAND_BLOCK, BAND_BLOCK), BAND_BLOCK)
    prev = pl.ds(pl.multiple_of(jnp.maximum(idx - 1, 0) * BAND_BLOCK, BAND_BLOCK), BAND_BLOCK)
    return first, cur, prev


def _heads(x):
    return x[:, :HEAD_DIM], x[:, HEAD_DIM:]


def _attn_fwd(qk, z):
    t = z.shape[0]
    n_blocks = t // BAND_BLOCK

    def body(q_ref, k_ref, v_ref, o_ref, lse_ref, qf, kf, vf, of, lf):
        def run(dil):
            _fold(q_ref, qf, dil)
            _fold(k_ref, kf, dil)
            _fold(v_ref, vf, dil)
            blocks_per_seq = n_blocks // dil

            def block(idx, carry):
                first, cur, prev = _block_rows(idx, blocks_per_seq)
                kmin = jnp.where(first, BAND_BLOCK, 0)
                outs, lses = [], []
                for q, kp, kc, vp, vc in zip(_heads(qf[cur, :]), _heads(kf[prev, :]), _heads(kf[cur, :]),
                                             _heads(vf[prev, :]), _heads(vf[cur, :])):
                    o, ls = _attn_block(q, kp, kc, vp, vc, kmin)
                    outs.append(o)
                    lses.append(jnp.broadcast_to(ls, o.shape))
                of[cur, :] = jnp.concatenate(outs, axis=1)
                lf[cur, :] = jnp.concatenate(lses, axis=1)
                return carry

            lax.fori_loop(0, n_blocks, block, 0, unroll=4)
            _unfold(of, o_ref, dil)
            _unfold(lf, lse_ref, dil)

        _per_group(run)

    slab = jax.ShapeDtypeStruct((t, N_PAIRS * PAIR), F32)
    out_spec = pl.BlockSpec((t, PAIR), lambda p: (0, p))
    return pl.pallas_call(
        body, name="attn_fwd", grid=(N_PAIRS,),
        in_specs=[pl.BlockSpec((t, PAIR), lambda p: (0, p)), pl.BlockSpec((t, PAIR), lambda p: (0, N_PAIRS + p)),
                  pl.BlockSpec((t, PAIR), lambda p: (0, V_COL0 + p))],
        out_specs=[out_spec, out_spec], out_shape=[slab, slab],
        scratch_shapes=[pltpu.VMEM((t, PAIR), F32)] * 5,
        compiler_params=_cparams(dimension_semantics=("parallel",)),
    )(qk, qk, z)


def _attn_bwd(qk, z, do, dlse):
    t = z.shape[0]
    n_blocks = t // BAND_BLOCK

    def body(q_ref, k_ref, v_ref, do_ref, dl_ref, dq_ref, dk_ref, dv_ref, qf, kf, vf, dof, dlf, dqf, dkf, dvf,
             dkpf, dvpf):
        def run(dil):
            for src, dst in ((q_ref, qf), (k_ref, kf), (v_ref, vf), (do_ref, dof), (dl_ref, dlf)):
                _fold(src, dst, dil)
            blocks_per_seq = n_blocks // dil

            def block(idx, carry):
                first, cur, prev = _block_rows(idx, blocks_per_seq)
                kmin = jnp.where(first, BAND_BLOCK, 0)
                grads = []
                for q, kp, kc, vp, vc, do_h, dl_h in zip(
                        _heads(qf[cur, :]), _heads(kf[prev, :]), _heads(kf[cur, :]), _heads(vf[prev, :]),
                        _heads(vf[cur, :]), _heads(dof[cur, :]), _heads(dlf[cur, :])):
                    _, vjp = jax.vjp(functools.partial(_attn_block, kmin=kmin), q, kp, kc, vp, vc)
                    grads.append(vjp((do_h, jnp.sum(dl_h, axis=1, keepdims=True))))
                dq, dkp, dkc, dvp, dvc = (jnp.concatenate([a, b], axis=1) for a, b in zip(*grads))
                dqf[cur, :], dkf[cur, :], dvf[cur, :], dkpf[cur, :], dvpf[cur, :] = dq, dkc, dvc, dkp, dvp
                return carry

            lax.fori_loop(0, n_blocks, block, 0, unroll=2)

            def join(idx, carry):
                first, cur, prev = _block_rows(idx, blocks_per_seq)

                @pl.when(jnp.logical_not(first))
                def _():
                    dkf[prev, :] += dkpf[cur, :]
                    dvf[prev, :] += dvpf[cur, :]

                return carry

            lax.fori_loop(0, n_blocks, join, 0)
            _unfold(dqf, dq_ref, dil)
            _unfold(dkf, dk_ref, dil)
            _unfold(dvf, dv_ref, dil)

        _per_group(run)

    slab = jax.ShapeDtypeStruct((t, N_PAIRS * PAIR), F32)
    own = pl.BlockSpec((t, PAIR), lambda p: (0, p))
    return pl.pallas_call(
        body, name="attn_bwd", grid=(N_PAIRS,),
        in_specs=[own, pl.BlockSpec((t, PAIR), lambda p: (0, N_PAIRS + p)),
                  pl.BlockSpec((t, PAIR), lambda p: (0, V_COL0 + p)), own, own],
        out_specs=[own] * 3, out_shape=[slab] * 3,
        scratch_shapes=[pltpu.VMEM((t, PAIR), F32)] * 10,
        compiler_params=_cparams(dimension_semantics=("parallel",)),
    )(qk, qk, z, do, dlse)


def _f_comb(o1, o2, o3, l1, l2, l3):
    m = jnp.maximum(jnp.maximum(l1, l2), l3)
    e1, e2, e3 = jnp.exp(l1 - m), jnp.exp(l2 - m), jnp.exp(l3 - m)
    den = e1 + e2 + e3
    return (e1 / den) * o1 + (e2 / den) * o2 + (e3 / den) * o3


def _all_gather_hbm(name, arrs):
    na = len(arrs)

    def body(*refs):
        x_refs, out_refs = refs[:na], refs[na:2 * na]
        send_sems, recv_sems, local_sems = refs[2 * na:]
        mx, my, mc = lax.axis_index("x"), lax.axis_index("y"), lax.axis_index("c")
        me, sibling = (mx, my, mc), (mx, my, 1 - mc)
        chips = [(1 - mx, my), (mx, 1 - my), (1 - mx, 1 - my)]

        def slot(a, px, py, pc):
            return out_refs[a].at[4 * px + 2 * py + pc]

        def copy(a, k, block, to, src=None):
            return pltpu.make_async_remote_copy(
                src_ref=slot(a, *block) if src is None else src, dst_ref=slot(a, *block),
                send_sem=send_sems.at[a, k], recv_sem=recv_sems.at[a, k], device_id=to, device_id_type=MESH)

        mine = [pltpu.make_async_copy(x_refs[a], slot(a, *me), local_sems.at[a]) for a in range(na)]
        for cp in mine:
            cp.start()
        first = []
        for a in range(na):
            first.append(copy(a, 0, me, sibling, src=x_refs[a]))
            first += [copy(a, 1 + j, me, (*chip, mc), src=x_refs[a]) for j, chip in enumerate(chips)]
        for cp in first:
            cp.start()
        passed = []
        for j, chip in enumerate(chips):
            for a in range(na):
                copy(a, 1 + j, (*chip, mc), me).wait_recv()
                passed.append(copy(a, 4 + j, (*chip, mc), sibling))
                passed[-1].start()
        for a in range(na):
            copy(a, 0, sibling, me).wait_recv()
            for j, chip in enumerate(chips):
                copy(a, 4 + j, (*chip, 1 - mc), me).wait_recv()
        for cp in first + passed:
            cp.wait_send()
        for cp in mine:
            cp.wait()

    hbm = pl.BlockSpec(memory_space=pl.ANY)
    return pl.pallas_call(
        body, name=name,
        out_shape=[jax.ShapeDtypeStruct((N_DEV,) + a.shape, a.dtype) for a in arrs],
        in_specs=[hbm] * na, out_specs=[hbm] * na,
        scratch_shapes=[pltpu.SemaphoreType.DMA((na, 7)), pltpu.SemaphoreType.DMA((na, 7)),
                        pltpu.SemaphoreType.DMA((na,))],
    )(*arrs)


def _all_gather_vmem(x):
    rws, cols = x.shape

    def body(x_ref, out_ref, send_sems, recv_sems):
        mx, my, mc = lax.axis_index("x"), lax.axis_index("y"), lax.axis_index("c")
        me, sibling = (mx, my, mc), (mx, my, 1 - mc)
        chips = [(1 - mx, my), (mx, 1 - my), (1 - mx, 1 - my)]

        def slot(px, py, pc):
            return out_ref.at[4 * px + 2 * py + pc]

        def copy(k, block, to, src=None):
            return pltpu.make_async_remote_copy(
                src_ref=slot(*block) if src is None else src, dst_ref=slot(*block),
                send_sem=send_sems.at[k], recv_sem=recv_sems.at[k], device_id=to, device_id_type=MESH)

        first = [copy(0, me, sibling, src=x_ref)]
        first += [copy(1 + j, me, (*chip, mc), src=x_ref) for j, chip in enumerate(chips)]
        for cp in first:
            cp.start()
        out_ref[4 * mx + 2 * my + mc] = x_ref[...]
        passed = [copy(4 + j, (*chip, mc), sibling) for j, chip in enumerate(chips)]
        for j, chip in enumerate(chips):
            copy(1 + j, (*chip, mc), me).wait_recv()
            passed[j].start()
        copy(0, sibling, me).wait_recv()
        for j, chip in enumerate(chips):
            copy(4 + j, (*chip, 1 - mc), me).wait_recv()
        for cp in first + passed:
            cp.wait_send()

    return pl.pallas_call(
        body, name="all_gather_small",
        out_shape=jax.ShapeDtypeStruct((N_DEV, rws, cols), x.dtype),
        in_specs=[pl.BlockSpec(memory_space=pltpu.VMEM)], out_specs=pl.BlockSpec(memory_space=pltpu.VMEM),
        scratch_shapes=[pltpu.SemaphoreType.DMA((7,)), pltpu.SemaphoreType.DMA((7,))],
    )(x)


def _all_to_all_hbm(name, arrs):
    na = len(arrs)

    def body(*refs):
        g_refs, out_refs = refs[:na], refs[na:2 * na]
        send_sems, recv_sems, local_sems = refs[2 * na:]
        mx, my, mc = lax.axis_index("x"), lax.axis_index("y"), lax.axis_index("c")
        me = 4 * mx + 2 * my + mc
        mine = [pltpu.make_async_copy(g_refs[a].at[me], out_refs[a].at[me], local_sems.at[a]) for a in range(na)]
        for cp in mine:
            cp.start()
        copies = []
        for k in range(1, N_DEV):
            px, py, pc = mx ^ (k >> 2), my ^ ((k >> 1) & 1), mc ^ (k & 1)
            peer = 4 * px + 2 * py + pc
            for a in range(na):
                copies.append(pltpu.make_async_remote_copy(
                    src_ref=g_refs[a].at[peer], dst_ref=out_refs[a].at[me], send_sem=send_sems.at[a, k - 1],
                    recv_sem=recv_sems.at[a, k - 1], device_id=(px, py, pc), device_id_type=MESH))
        for cp in copies:
            cp.start()
        for cp in copies:
            cp.wait_recv()
        for cp in copies:
            cp.wait_send()
        for cp in mine:
            cp.wait()

    hbm = pl.BlockSpec(memory_space=pl.ANY)
    return pl.pallas_call(
        body, name=name,
        out_shape=[jax.ShapeDtypeStruct(a.shape, a.dtype) for a in arrs],
        in_specs=[hbm] * na, out_specs=[hbm] * na,
        scratch_shapes=[pltpu.SemaphoreType.DMA((na, 7)), pltpu.SemaphoreType.DMA((na, 7)),
                        pltpu.SemaphoreType.DMA((na,))],
    )(*arrs)


def _scatter_copies(g_refs, land_refs, send_sems, recv_sems):
    mx, my, mc = lax.axis_index("x"), lax.axis_index("y"), lax.axis_index("c")
    me = 4 * mx + 2 * my + mc
    copies = []
    for k in range(1, N_DEV):
        px, py, pc = mx ^ (k >> 2), my ^ ((k >> 1) & 1), mc ^ (k & 1)
        peer = 4 * px + 2 * py + pc
        for a, (g_ref, land_ref) in enumerate(zip(g_refs, land_refs)):
            copies.append(pltpu.make_async_remote_copy(
                src_ref=g_ref.at[peer], dst_ref=land_ref.at[me], send_sem=send_sems.at[a * (N_DEV - 1) + k - 1],
                recv_sem=recv_sems.at[a * (N_DEV - 1) + k - 1], device_id=(px, py, pc), device_id_type=MESH))
    return copies


_HBM = pl.BlockSpec(memory_space=pltpu.HBM)
_SEM = pl.BlockSpec(memory_space=pltpu.SEMAPHORE)
_DATAFLOW = pltpu.SideEffectType.DATAFLOW_SIDE_EFFECTING


def _gather_copies(x_refs, land_refs, send_sems, recv_sems):
    mx, my, mc = lax.axis_index("x"), lax.axis_index("y"), lax.axis_index("c")
    me = 4 * mx + 2 * my + mc
    copies = []
    for k in range(1, N_DEV):
        px, py, pc = mx ^ (k >> 2), my ^ ((k >> 1) & 1), mc ^ (k & 1)
        for a, (x_ref, land_ref) in enumerate(zip(x_refs, land_refs)):
            copies.append(pltpu.make_async_remote_copy(
                src_ref=x_ref, dst_ref=land_ref.at[me], send_sem=send_sems.at[a * (N_DEV - 1) + k - 1],
                recv_sem=recv_sems.at[a * (N_DEV - 1) + k - 1], device_id=(px, py, pc), device_id_type=MESH))
    return copies


N_CHIPS = N_DEV // 2


def _gather2_first(x_refs, land_refs, send_sems, recv_sems, ks=(0, 1, 2, 3)):
    mx, my, mc = lax.axis_index("x"), lax.axis_index("y"), lax.axis_index("c")
    me = 4 * mx + 2 * my + mc
    peers = [(mx, my, 1 - mc), (1 - mx, my, mc), (mx, 1 - my, mc), (1 - mx, 1 - my, mc)]
    return [pltpu.make_async_remote_copy(
        src_ref=x_ref, dst_ref=land_ref.at[me], send_sem=send_sems.at[a * (N_DEV - 1) + k],
        recv_sem=recv_sems.at[a * (N_DEV - 1) + k], device_id=peers[k], device_id_type=MESH)
        for a, (x_ref, land_ref) in enumerate(zip(x_refs, land_refs)) for k in ks]


def _gather2_second(x_refs, land_refs, send_sems, recv_sems):
    mx, my, mc = lax.axis_index("x"), lax.axis_index("y"), lax.axis_index("c")
    copies = []
    for a, land_ref in enumerate(land_refs):
        for j, (px, py) in enumerate([(1 - mx, my), (mx, 1 - my), (1 - mx, 1 - my)]):
            slot = 4 * px + 2 * py + mc
            copies.append(pltpu.make_async_remote_copy(
                src_ref=land_ref.at[slot], dst_ref=land_ref.at[slot], send_sem=send_sems.at[a * (N_DEV - 1) + j],
                recv_sem=recv_sems.at[a * (N_DEV - 1) + j], device_id=(mx, my, 1 - mc), device_id_type=MESH))
    return copies


def _pair_swap(name, arrs):
    na = len(arrs)

    def body(*refs):
        g_refs, out_refs = refs[:na], refs[na:2 * na]
        send_sems, recv_sems = refs[2 * na:]
        mx, my, mc = lax.axis_index("x"), lax.axis_index("y"), lax.axis_index("c")
        copies = [pltpu.make_async_remote_copy(
            src_ref=g_refs[a].at[2 * q + 1 - mc], dst_ref=out_refs[a].at[q], send_sem=send_sems.at[a, q],
            recv_sem=recv_sems.at[a, q], device_id=(mx, my, 1 - mc), device_id_type=MESH)
            for a in range(na) for q in range(N_CHIPS)]
        for cp in copies:
            cp.start()
        for cp in copies:
            cp.wait_recv()
        for cp in copies:
            cp.wait_send()

    hbm = pl.BlockSpec(memory_space=pl.ANY)
    return pl.pallas_call(
        body, name=name,
        out_shape=[jax.ShapeDtypeStruct((N_CHIPS,) + a.shape[1:], a.dtype) for a in arrs],
        in_specs=[hbm] * na, out_specs=[hbm] * na,
        scratch_shapes=[pltpu.SemaphoreType.DMA((na, N_CHIPS)), pltpu.SemaphoreType.DMA((na, N_CHIPS))],
    )(*arrs)


def _pair_add(name, g, half, core):
    _, r, c = g.shape
    tr = _tile(r, 512) if r % LANES == 0 else r

    def body(core_ref, g_ref, h_ref, o_ref):
        o_ref[...] = (g_ref[...].astype(F32) + h_ref[...].astype(F32)).astype(o_ref.dtype)

    return pl.pallas_call(
        body, name=name,
        grid_spec=pltpu.PrefetchScalarGridSpec(
            num_scalar_prefetch=1, grid=(N_CHIPS, r // tr),
            in_specs=[pl.BlockSpec((None, tr, c), lambda q, i, core_ref: (2 * q + core_ref[0], i, 0)),
                      pl.BlockSpec((None, tr, c), lambda q, i, core_ref: (q, i, 0))],
            out_specs=pl.BlockSpec((None, tr, c), lambda q, i, core_ref: (q, i, 0))),
        out_shape=jax.ShapeDtypeStruct((N_CHIPS, r, c), g.dtype),
        compiler_params=_cparams(dimension_semantics=("parallel", "parallel")),
    )(core, g, half)


def _chip_copies(h_refs, land_refs, send_sems, recv_sems):
    mx, my, mc = lax.axis_index("x"), lax.axis_index("y"), lax.axis_index("c")
    my_chip = 2 * mx + my
    copies = []
    for k in range(1, N_CHIPS):
        px, py = mx ^ (k >> 1), my ^ (k & 1)
        for a, (h_ref, land_ref) in enumerate(zip(h_refs, land_refs)):
            copies.append(pltpu.make_async_remote_copy(
                src_ref=h_ref.at[2 * px + py], dst_ref=land_ref.at[my_chip], send_sem=send_sems.at[a * (N_DEV - 1) + k - 1],
                recv_sem=recv_sems.at[a * (N_DEV - 1) + k - 1], device_id=(px, py, mc), device_id_type=MESH))
    return copies


def _exchange_start(name, copies, srcs, lands, after):
    na = len(srcs)

    def body(*refs):
        for cp in copies(refs[:na], refs[na:2 * na], refs[2 * na + 1], refs[2 * na + 2]):
            cp.start()
        refs[-1][...] = jnp.zeros_like(refs[-1])

    in_hbm = lambda a: pltpu.with_memory_space_constraint(a, pltpu.HBM)
    outs = pl.pallas_call(
        body, name=name,
        out_shape=(pltpu.SemaphoreType.DMA((na * (N_DEV - 1),)), pltpu.SemaphoreType.DMA((na * (N_DEV - 1),)),
                   *[pltpu.HBM(a.shape, a.dtype) for a in list(srcs) + list(lands)],
                   jax.ShapeDtypeStruct((8, LANES), F32)),
        in_specs=[_HBM] * (2 * na) + [pl.BlockSpec(memory_space=pl.ANY)],
        out_specs=(_SEM, _SEM, *[_HBM] * (2 * na), pl.BlockSpec(memory_space=pltpu.VMEM)),
        input_output_aliases={i: 2 + i for i in range(2 * na)},
        compiler_params=pltpu.CompilerParams(has_side_effects=_DATAFLOW),
    )(*[in_hbm(a) for a in srcs], *[in_hbm(a) for a in lands], after)
    return outs[0], outs[1], outs[2:2 + na], outs[2 + na:2 + 2 * na], outs[-1]


def _exchange_wait(name, copies, send_sems, recv_sems, srcs, lands, after, parts=("send", "recv")):
    na = len(srcs)

    def body(*refs):
        for cp in copies(refs[:na], refs[na:2 * na], refs[2 * na], refs[2 * na + 1]):
            if "send" in parts:
                cp.wait_send()
            if "recv" in parts:
                cp.wait_recv()

    outs = pl.pallas_call(
        body, name=name,
        out_shape=[pltpu.HBM(a.shape, a.dtype) for a in list(srcs) + list(lands)],
        in_specs=[_HBM] * (2 * na) + [_SEM, _SEM, pl.BlockSpec(memory_space=pl.ANY)], out_specs=[_HBM] * (2 * na),
        input_output_aliases={i: i for i in range(2 * na)},
        compiler_params=pltpu.CompilerParams(has_side_effects=_DATAFLOW),
    )(*srcs, *lands, send_sems, recv_sems, after)
    return outs[:na], outs[na:]


def _sum_slots(name, g, tr):
    _, rws, cols = g.shape

    def body(g_ref, o_ref):
        acc = g_ref[0].astype(F32)
        for j in range(1, N_DEV):
            acc = acc + g_ref[j].astype(F32)
        o_ref[...] = acc

    return pl.pallas_call(
        body, name=name, grid=(rws // tr,),
        in_specs=[pl.BlockSpec((N_DEV, tr, cols), lambda i: (0, i, 0))],
        out_specs=pl.BlockSpec((tr, cols), lambda i: (i, 0)),
        out_shape=jax.ShapeDtypeStruct((rws, cols), F32),
        compiler_params=_cparams(dimension_semantics=("parallel",)),
    )(g)


def _adam_math(wv, gv, mv, vv):
    mn = ADAM_B1 * mv + (1.0 - ADAM_B1) * gv
    vn = ADAM_B2 * vv + (1.0 - ADAM_B2) * jnp.square(gv)
    m_hat = mn / (1.0 - ADAM_B1 ** ADAM_STEP)
    v_hat = vn / (1.0 - ADAM_B2 ** ADAM_STEP)
    delta = -ADAM_LR * (m_hat / (jnp.sqrt(v_hat) + ADAM_EPS) + ADAM_WD * wv)
    return delta, mn, vn


def _adamw(name, w, g, m, v, tr):
    return _rowwise(name, _adam_math, [w, g, m, v], [], [(LANES, F32)] * 3, tr=tr)


def _adamw_slots(name, recv, own, rb, cb, cw, w, m, v, tr, order):
    nr, nc = w.shape
    n = recv.shape[0]

    def body(g_ref, own_ref, w_ref, m_ref, v_ref, order_ref, go_ref, d_ref, mo_ref, vo_ref):
        me = 2 * lax.axis_index("x") + lax.axis_index("y")
        if n == N_DEV:
            me = 2 * me + lax.axis_index("c")
        acc = None
        for s in range(n):
            part = jnp.where(me == s, own_ref[s], g_ref[s]).astype(F32)
            acc = part if acc is None else acc + part
        g = acc[:, :nc]
        go_ref[...] = g
        d_ref[...], mo_ref[...], vo_ref[...] = _adam_math(w_ref[...], g, m_ref[...], v_ref[...])

    nat = pl.BlockSpec((tr, nc), lambda i: (i, 0))
    slots = pl.BlockSpec((n, tr, cw), lambda i: (0, rb + i, cb))
    return pl.pallas_call(
        body, name=name, grid=(nr // tr,),
        in_specs=[slots, slots, nat, nat, nat, pl.BlockSpec(memory_space=pl.ANY)],
        out_specs=[nat] * 4, out_shape=[jax.ShapeDtypeStruct((nr, nc), F32)] * 4,
        compiler_params=_cparams(dimension_semantics=("parallel",)),
    )(recv, own, w, m, v, order)


def _local_blocks(w):
    pad_cols = lambda a: jnp.pad(a, ((0, 0), (0, FF_PAD - FF_SHARD)))
    pad_rows = lambda a: jnp.pad(a, ((0, FF_PAD - FF_SHARD), (0, 0)))
    gate_up = lambda tag: jnp.concatenate([pad_cols(w[tag + "_w_gate"]), pad_cols(w[tag + "_w_up"])], axis=1)
    blocks = {
        "ffn1_gu": gate_up("ffn1"), "ffn1_d": pad_rows(w["ffn1_w_down"]), "w_in": w["w_in"],
        "lora": jnp.concatenate([w["rwkv_w2"], w["rwkv_a2"], w["rwkv_g2"]], axis=0),
        "br": jnp.concatenate([w["w_br_rwkv"], w["w_br_attn"], w["ple_w_proj"]], axis=0),
        "w_out": w["w_out"], "ffn2_gu": gate_up("ffn2"), "ffn2_d": pad_rows(w["ffn2_w_down"]),
        "ple_gate": w["ple_w_gate"],
    }
    return {n: a.astype(BF16) for n, a in blocks.items()}


GATHER_GROUPS = {"head": ("ffn1_gu", "ffn1_d"), "mid": ("w_in", "lora"),
                 "rest": ("br", "w_out", "ffn2_gu", "ffn2_d", "ple_gate")}

SCATTER_GROUPS = {"tail": ("ple_gate", "ple_proj", "ffn2_gu", "ffn2_d"), "branch": ("w_out", "br"),
                  "mixer": ("lora", "w_in"),
                  "ffn1_down": ("ffn1_d",), "head": ("ffn1_gu",)}

TWO_LEVEL = ("mixer", "head")

ADAM_PLAN = (
    ("ffn1_w_gate", "ffn1_gu", 0, 0, FF_PAD, 256), ("ffn1_w_up", "ffn1_gu", 0, 1, FF_PAD, 256),
    ("ffn1_w_down", "ffn1_d", 0, 0, D_MODEL, FF_SHARD // 2), ("w_in", "w_in", 0, 0, IN_SHARD, 256),
    ("rwkv_w2", "lora", 0, 0, HEAD_DIM, 64), ("rwkv_a2", "lora", 1, 0, HEAD_DIM, 64),
    ("rwkv_g2", "lora", 2, 0, HEAD_DIM, 64),
    ("w_br_rwkv", "br", 0, 0, OUT_SHARD, 256), ("w_br_attn", "br", 2, 0, OUT_SHARD, 256),
    ("ple_w_proj", "ple_proj", 0, 0, OUT_SHARD, 256), ("w_out", "w_out", 0, 0, D_MODEL, OUT_SHARD),
    ("ffn2_w_gate", "ffn2_gu", 0, 0, FF_PAD, 256), ("ffn2_w_up", "ffn2_gu", 0, 1, FF_PAD, 256),
    ("ffn2_w_down", "ffn2_d", 0, 0, D_MODEL, FF_SHARD // 2), ("ple_w_gate", "ple_gate", 0, 0, D_MODEL, OUT_SHARD),
)


def _pack_small(arrs, rows):
    flat = jnp.concatenate([a.reshape(-1) for a in arrs])
    return jnp.pad(flat, (0, rows * LANES - flat.shape[0])).reshape(rows, LANES)


def _unpack_small(flat, like):
    flat = flat.reshape(-1)
    out, off = [], 0
    for a in like:
        out.append(flat[off:off + a.size].reshape(a.shape))
        off += a.size
    return out


def _local_step(x, p, pos, target, sm, wg, fetch, on_grads):
    t = x.shape[0]

    w_d1 = wg["ffn1_d"].reshape(FF_HID, D_MODEL)
    x1, ffn1_saved, _ = _ffn_fwd("ffn1", x, sm["ffn1_norm"], wg["ffn1_gu"], lambda after: w_d1)
    fetch("forward", x1)
    h2 = _norm_fwd("mix_norm", x1, sm["mix_norm"])
    wg = {**wg, **fetch("mid", h2)}
    full_cols = lambda blk: blk.transpose(1, 0, 2).reshape(blk.shape[1], N_DEV * blk.shape[2])
    lora_w2 = full_cols(wg["lora"][:, :DECAY_LORA])
    lora_a2 = full_cols(wg["lora"][:, DECAY_LORA:DECAY_LORA + ICLR_LORA])
    lora_g2 = full_cols(wg["lora"][:, DECAY_LORA + ICLR_LORA:])
    z =_mmc_nn("w_in", h2, wg["w_in"], 0, 0, IN_SHARD)
    z_g = (z, 2 * D_MODEL, (RWKV_COLS + 3 * ATTN_DIM) // (2 * D_MODEL))

    r, k, v, lo, gd = _shift_fwd(z, sm["rwkv_mu"])
    zero_lo = jnp.zeros((DECAY_LORA, RWKV_DIM), BF16)
    w2p = jnp.concatenate([lora_w2, zero_lo], axis=0).astype(F32)
    a2p = jnp.concatenate([zero_lo, lora_a2], axis=0).astype(F32)
    pre_params = [sm["rwkv_w0"], w2p, sm["rwkv_a0"], a2p, lora_g2.astype(F32), sm["rwkv_k_k"], sm["rwkv_k_a"]]
    wide = [(RWKV_DIM, F32)]
    k2, kk, a, decay, g = _rowwise("rwkv_pre", _f_pre, [k, lo, gd], pre_params, wide * 5)
    kkn, b = _pairwise("rwkv_kk", _f_kk, [kk, a], [], 2)
    scan_in = [u.reshape(t, RWKV_HEADS, HEAD_DIM) for u in (r, decay, k2, kkn, b)]
    v_rows = _to_v_rows(v)
    y_rows, states = _wkv_fwd(*scan_in, v_rows)
    y = _from_v_rows(y_rows)
    post_params = [sm["rwkv_gn_w"], sm["rwkv_gn_b"], sm["rwkv_r_k"]]
    post_rows = [y, r, k2, v, g]
    y_rwkv = _pairwise("rwkv_post", lambda *av: (_f_post(*av),), post_rows, post_params, 1)[0]

    inv_freq = 1.0 / (ROPE_THETA ** (jnp.arange(0, HEAD_DIM, 2, dtype=F32) / HEAD_DIM))
    freq2 = jnp.tile(inv_freq, 2 * PAIR // HEAD_DIM).reshape(1, PAIR)
    half = jnp.ones((HEAD_DIM // 2,), F32)
    sign2 = jnp.tile(jnp.concatenate([-half, half]), PAIR // HEAD_DIM).reshape(1, PAIR)

    def rope_table(posv, fr, sg):
        ang = posv * fr
        return (jnp.concatenate([jnp.cos(ang), jnp.sin(ang) * sg], axis=1),)
    tab = _rowwise("rope_table", rope_table, [pos.astype(F32).reshape(t, 1)], [freq2, sign2], [(2 * PAIR, F32)])[0]
    qk = _qk_prep(z, tab, sm["q_norm"], sm["k_norm"])
    o_all, lse_all = _attn_fwd(qk, z)
    gw = HEADS_PER_GROUP * HEAD_DIM

    def by_group(ov, lv):
        return [ov[:, i * gw:(i + 1) * gw] for i in range(3)] + [lv[:, i * gw:(i + 1) * gw] for i in range(3)]
    y_attn = _rowwise("attn_comb", lambda ov, lv: (_f_comb(*by_group(ov, lv)),), [o_all, lse_all], [], [(gw, F32)])[0]

    wg = {**wg, **fetch("rest", y_rwkv)}
    w_d2 = wg["ffn2_d"].reshape(FF_HID, D_MODEL)
    w_out = wg["w_out"].reshape(D_MODEL, D_MODEL)
    w_pg = wg["ple_gate"].reshape(D_MODEL, D_MODEL)
    w_brr = full_cols(wg["br"][:, :RWKV_DIM])
    w_bra = full_cols(wg["br"][:, RWKV_DIM:RWKV_DIM + gw])
    w_pp = full_cols(wg["br"][:, RWKV_DIM + gw:])
    u_r = _mm("br_rwkv", y_rwkv, w_brr, "nn")
    u_a = _mm("br_attn", y_attn, w_bra, "nn")

    def f_merge(zgr, zga, ur, ua):
        return _sigmoid(zgr) * ur + _sigmoid(zga) * ua
    merged = _rowwise("merge", lambda zg, ur, ua: (f_merge(zg[:, :D_MODEL], zg[:, D_MODEL:], ur, ua),),
                      [z_g, u_r, u_a], [], [(D_MODEL, BF16)])[0]
    x2 = _mm("w_out", merged, w_out, "nn", res=x1)
    x3, ffn2_saved, _ = _ffn_fwd("ffn2", x2, sm["ffn2_norm"], wg["ffn2_gu"], lambda after: w_d2)

    hn = _norm_fwd("ple_norm", x3, sm["ple_norm"])
    gz = _mm("ple_gate", hn, w_pg, "nn")
    pp = _mm("ple_proj", p, w_pp, "nn")

    def f_head(x3v, gzv, ppv, tg):
        sg = _sigmoid(gzv)
        err = x3v + sg * ppv - tg
        part = 0.5 * jnp.sum(jnp.mean(err * err, axis=-1, keepdims=True))
        dx4 = err * (1.0 / D_MODEL)
        return dx4, dx4 * ppv * sg * (1.0 - sg), dx4 * sg, jnp.full((1, LANES), part, F32)
    dx4, dgz, dpp, loss_row = _rowwise("ple_loss", f_head, [x3, gz, pp, target], [],
                                       [(D_MODEL, F32), (D_MODEL, BF16), (D_MODEL, BF16)], [(1, LANES)])
    loss = loss_row[0, 0]

    gs, gm = {}, {}
    row_blocks = lambda g: g.reshape(N_DEV, g.shape[0] // N_DEV, g.shape[1])
    dhn = _mm("ple_dhn", dgz, w_pg, "nt")
    gm["ple_gate"] = row_blocks(_mm("ple_dwgate", hn, dgz, "tn", out_dtype=BF16))
    col_blocks = lambda g: g.reshape(g.shape[0], N_DEV, g.shape[1] // N_DEV).transpose(1, 0, 2)
    gm["ple_proj"] = col_blocks(_mm("ple_dwproj", p, dpp, "tn", out_dtype=BF16))
    dx3, gs["ple_norm"] = _norm_bwd("ple_dnorm", x3, sm["ple_norm"], dhn, dx4)

    dx2, gs["ffn2_norm"], gm["ffn2_gu"], gm["ffn2_d"] = _ffn_bwd(
        "ffn2", x2, sm["ffn2_norm"], wg["ffn2_gu"], w_d2, ffn2_saved, dx3)
    tail_token = on_grads("tail", {n: gm.pop(n) for n in SCATTER_GROUPS["tail"]})

    dmerged = _mm("w_out_dmerged", dx2, w_out, "nt")
    gm["w_out"] = row_blocks(_mm("w_out_dw", merged, dx2, "tn", out_dtype=BF16))

    def merge_bwd(zg, ur, ua, dm):
        _, vjp = jax.vjp(f_merge, zg[:, :D_MODEL], zg[:, D_MODEL:], ur, ua)
        dzr, dza, dur, dua = vjp(dm)
        return jnp.concatenate([dzr, dza], axis=1), dur, dua
    dz_g, du_r, du_a = _rowwise("merge_bwd", merge_bwd, [z_g, u_r, u_a, dmerged], [],
                                [(2 * D_MODEL, BF16), (D_MODEL, BF16), (D_MODEL, BF16)])
    dy_rwkv = _mm("br_rwkv_dy", du_r, w_brr, "nt")
    dy_attn = _mm("br_attn_dy", du_a, w_bra, "nt")
    gm["br"] = jnp.concatenate([col_blocks(_mm("br_rwkv_dw", y_rwkv, du_r, "tn", out_dtype=BF16)),
                                col_blocks(_mm("br_attn_dw", y_attn, du_a, "tn", out_dtype=BF16))], axis=1)
    branch_token = on_grads("branch", {n: gm.pop(n) for n in SCATTER_GROUPS["branch"]})

    def comb_bwd(ov, lv, dyv):
        _, vjp = jax.vjp(_f_comb, *by_group(ov, lv))
        d = vjp(dyv)
        return jnp.concatenate(d[:3], axis=1), jnp.concatenate(d[3:], axis=1)
    do_all, dl_all = _rowwise("attn_comb_bwd", comb_bwd, [o_all, lse_all, dy_attn], [],
                              [(ATTN_DIM, F32), (ATTN_DIM, F32)])
    dq_all, dk_all, dv_all = _attn_bwd(qk, z, do_all, dl_all)
    dqk_raw, gs["q_norm"], gs["k_norm"] = _qk_prep_bwd(z, tab, sm["q_norm"], sm["k_norm"], dq_all, dk_all)

    def post_bwd(yv, rv, k2v, vv, gv, dv_, gnw, gnb, rk):
        _, vjp = jax.vjp(_f_post, yv, rv, k2v, vv, gv, gnw, gnb, rk)
        return vjp(dv_)
    dy, dr1, dk2a, dv1, dg, d_gnw, d_gnb, d_rk = _pairwise(
        "rwkv_post_bwd", post_bwd, post_rows + [dy_rwkv], [post_params[0] + tail_token[0, 0] + branch_token[0, 0]] + post_params[1:], 5, 3)
    gs["rwkv_gn_w"], gs["rwkv_gn_b"], gs["rwkv_r_k"] = d_gnw, d_gnb, d_rk
    dr2, ddecay, dk2b, dkkn, db, dv_rows = _wkv_bwd(*scan_in, v_rows, states, _to_v_rows(dy))
    dr2, ddecay, dk2b, dkkn, db = [u.reshape(t, RWKV_DIM) for u in (dr2, ddecay, dk2b, dkkn, db)]
    dv2 = _from_v_rows(dv_rows)

    def kk_bwd(kkv, av, dkknv, dbv, dra, drb, dva, dvb):
        _, vjp = jax.vjp(_f_kk, kkv, av)
        return (*vjp((dkknv, dbv)), dra + drb, dva + dvb)
    dkk, da, dr, dv = _pairwise("rwkv_kk_bwd", kk_bwd, [kk, a, dkkn, db, dr1, dr2, dv1, dv2], [], 4)

    def pre_bwd(kv, lov, gdv, dk2x, dk2y, dkkv, dav, ddec, dgv, w0, w2p_, a0, a2p_, g2, k_k, k_a):
        _, vjp = jax.vjp(_f_pre, kv, lov, gdv, w0, w2p_, a0, a2p_, g2, k_k, k_a)
        return vjp((dk2x + dk2y, dkkv, dav, ddec, dgv))
    lora_acc = (DECAY_LORA + ICLR_LORA, RWKV_DIM)
    dk, dlo, dgd, d_w0, d_w2p, d_a0, d_a2p, d_g2, d_kk, d_ka = _rowwise(
        "rwkv_pre_bwd", pre_bwd,
        [k, lo, gd, dk2a, dk2b, dkk, da, ddecay, dg],
        pre_params, [(RWKV_DIM, F32), (LANES, F32), (LANES, F32)],
        [(1, RWKV_DIM), lora_acc, (1, RWKV_DIM), lora_acc, (GATE_LORA, RWKV_DIM), (1, RWKV_DIM), (1, RWKV_DIM)])
    gs["rwkv_w0"], gs["rwkv_a0"], gs["rwkv_k_k"], gs["rwkv_k_a"] = d_w0, d_a0, d_kk, d_ka
    gm["lora"] = jnp.concatenate([col_blocks(d_w2p[:DECAY_LORA]), col_blocks(d_a2p[DECAY_LORA:]), col_blocks(d_g2)],
                                 axis=1).astype(BF16)
    dz_r, gs["rwkv_mu"] = _shift_bwd(z, sm["rwkv_mu"], dr, dk, dv, dlo, dgd)

    dz = jnp.concatenate([dz_r, dqk_raw, dv_all.astype(BF16), dz_g], axis=1)
    dh2 = _mmc_nt("w_in_dh", dz, wg["w_in"], 0, 0, IN_SHARD, D_MODEL)
    gm["w_in"] = _mmc_tn("w_in_dw", h2, dz, IN_SHARD)
    mixer_token = on_grads("mixer", {n: gm.pop(n) for n in SCATTER_GROUPS["mixer"]})
    dx1, gs["mix_norm"] = _norm_bwd("mix_dnorm", x1, sm["mix_norm"] + mixer_token[0, 0], dh2, dx2)

    dx0, gs["ffn1_norm"], gm["ffn1_gu"], _ = _ffn_bwd(
        "ffn1", x, sm["ffn1_norm"], wg["ffn1_gu"], w_d1, ffn1_saved, dx1,
        on_down=lambda blocks: on_grads("ffn1_down", {"ffn1_d": blocks}))
    return loss, dx0, gm, gs


def kernel(x, p, positions, ffn1_norm, ffn1_w_gate, ffn1_w_up, ffn1_w_down, mix_norm, w_in, rwkv_mu, rwkv_w0, rwkv_w2, rwkv_a0, rwkv_a2, rwkv_g2, rwkv_k_k, rwkv_k_a, rwkv_r_k, rwkv_gn_w, rwkv_gn_b, q_norm, k_norm, w_br_rwkv, w_br_attn, w_out, ffn2_norm, ffn2_w_gate, ffn2_w_up, ffn2_w_down, ple_norm, ple_w_gate, ple_w_proj, loss_target, m_ffn1_norm, m_ffn1_w_gate, m_ffn1_w_up, m_ffn1_w_down, m_mix_norm, m_w_in, m_rwkv_mu, m_rwkv_w0, m_rwkv_w2, m_rwkv_a0, m_rwkv_a2, m_rwkv_g2, m_rwkv_k_k, m_rwkv_k_a, m_rwkv_r_k, m_rwkv_gn_w, m_rwkv_gn_b, m_q_norm, m_k_norm, m_w_br_rwkv, m_w_br_attn, m_w_out, m_ffn2_norm, m_ffn2_w_gate, m_ffn2_w_up, m_ffn2_w_down, m_ple_norm, m_ple_w_gate, m_ple_w_proj, v_ffn1_norm, v_ffn1_w_gate, v_ffn1_w_up, v_ffn1_w_down, v_mix_norm, v_w_in, v_rwkv_mu, v_rwkv_w0, v_rwkv_w2, v_rwkv_a0, v_rwkv_a2, v_rwkv_g2, v_rwkv_k_k, v_rwkv_k_a, v_rwkv_r_k, v_rwkv_gn_w, v_rwkv_gn_b, v_q_norm, v_k_norm, v_w_br_rwkv, v_w_br_attn, v_w_out, v_ffn2_norm, v_ffn2_w_gate, v_ffn2_w_up, v_ffn2_w_down, v_ple_norm, v_ple_w_gate, v_ple_w_proj):
    args = locals()
    w = {n: args[n][0] for n in WEIGHTS}
    m = {n: args["m_" + n][0] for n in WEIGHTS}
    v = {n: args["v_" + n][0] for n in WEIGHTS}

    w_loc = _local_blocks(w)
    head = GATHER_GROUPS["head"]
    wg = dict(zip(head, _all_gather_hbm("gather_head", [w_loc[n] for n in head])))
    me = 4 * lax.axis_index("x") + 2 * lax.axis_index("y") + lax.axis_index("c")
    gathering, order_after = {}, wg[head[0]]
    for group in ("mid", "rest"):
        shards = [w_loc[n] for n in GATHER_GROUPS[group]]
        zones = [lax.dynamic_update_slice(lax.empty((N_DEV,) + a.shape, a.dtype), a[None], (me, 0, 0)) for a in shards]
        copies = _gather2_first if group == "mid" else _gather_copies
        *gathering[group], order_after = _exchange_start("gather_start_" + group, copies, shards, zones, order_after)

    def fetch(group, after):
        if group == "forward":
            s1, r1, srcs, lands = gathering["mid"]
            ici = functools.partial(_gather2_first, ks=(1, 2, 3))
            srcs, lands = _exchange_wait("gather_arrived_mid", ici, s1, r1, srcs, lands, after, parts=("recv",))
            s2, r2, srcs, lands, _ = _exchange_start("gather_forward_mid", _gather2_second, srcs, lands, after)
            gathering["mid"] = (s1, r1, s2, r2, srcs, lands)
            return {}
        if group == "mid":
            s1, r1, s2, r2, srcs, lands = gathering["mid"]
            sib = functools.partial(_gather2_first, ks=(0,))
            srcs, lands = _exchange_wait("gather_passed_mid", _gather2_second, s2, r2, srcs, lands, after)
            srcs, lands = _exchange_wait("gather_sent_mid", _gather2_first, s1, r1, srcs, lands, after, parts=("send",))
            _, got = _exchange_wait("gather_wait_mid", sib, s1, r1, srcs, lands, after, parts=("recv",))
        else:
            _, got = _exchange_wait("gather_wait_" + group, _gather_copies, *gathering[group], after)
        return dict(zip(GATHER_GROUPS[group], got))

    sm = {n: w[n].reshape(1, -1) for n in SMALL}
    sm["ffn1_norm"] = sm["ffn1_norm"] + order_after[0, 0]
    in_flight = {}

    core = lax.axis_index("c").astype(jnp.int32).reshape(1)

    def scatter_early(group, arrays):
        arrs = [arrays[n] for n in SCATTER_GROUPS[group]]
        copies, after = _scatter_copies, arrs[0]
        if group in TWO_LEVEL:
            halves = _pair_swap("scatter_pair_" + group, arrs)
            arrs = [_pair_add("scatter_add_%s_%s" % (group, n), a, hf, core)
                    for n, a, hf in zip(SCATTER_GROUPS[group], arrs, halves)]
            copies, after = _chip_copies, halves[0]
        *in_flight[group], token = _exchange_start("scatter_start_" + group, copies, arrs,
                                                   [lax.empty(a.shape, a.dtype) for a in arrs], after)
        return token
    loss_part, dx, gm, gs = _local_step(x[0], p[0, 0], positions[0], loss_target[0], sm, wg, fetch, scatter_early)
    loss = lax.psum(loss_part, ("x", "y", "c"))
    head_token = scatter_early("head", gm)
    recv, own = {}, {}

    def arrived(group, after):
        copies = _chip_copies if group in TWO_LEVEL else _scatter_copies
        sent, lands = _exchange_wait("scatter_wait_" + group, copies, *in_flight[group], after)
        own.update(zip(SCATTER_GROUPS[group], sent))
        recv.update(zip(SCATTER_GROUPS[group], lands))
    for group in ("tail", "branch", "mixer", "ffn1_down"):
        arrived(group, head_token)
    small_like = [w[n] for n in SMALL]
    small_rows = 80
    gs_all = _all_gather_vmem(_pack_small([gs[n] for n in SMALL], small_rows))
    gs_sum = _sum_slots("sum_small_grads", gs_all, small_rows)

    res = {}
    early = [e for group in ("tail", "branch", "mixer", "ffn1_down") for e in ADAM_PLAN
             if e[1] in SCATTER_GROUPS[group]]
    late = [e for e in ADAM_PLAN if e[1] in SCATTER_GROUPS["head"]]
    order = head_token
    for n, src, rb, cb, cw, tr in early + late:
        if (n, src, rb, cb, cw, tr) == late[0]:
            arrived("head", order)
        outs4 = _adamw_slots("adamw_" + n, recv[src], own[src], rb, cb, cw, w[n], m[n], v[n], tr, order)
        order = outs4[1]
        for tag, a in zip(("grad", "delta", "new_m", "new_v"), outs4):
            res[tag, n] = a[None]
    d_s, m_s, v_s = _adamw("adamw_small", _pack_small(small_like, small_rows), gs_sum,
                           _pack_small([m[n] for n in SMALL], small_rows),
                           _pack_small([v[n] for n in SMALL], small_rows), small_rows)
    for tag, small in (("grad", gs_sum), ("delta", d_s), ("new_m", m_s), ("new_v", v_s)):
        for n, a in zip(SMALL, _unpack_small(small, small_like)):
            res[tag, n] = a[None]
    outs = [loss, dx[None]]
    for tag in ("grad", "delta", "new_m", "new_v"):
        outs += [res[tag, n] for n in WEIGHTS]
    return tuple(outs)
```

```python
import functools

import jax
import jax.numpy as jnp
from jax import lax
from jax.experimental import pallas as pl
from jax.experimental.pallas import tpu as pltpu

F32, BF16 = jnp.float32, jnp.bfloat16
MESH = pl.DeviceIdType.MESH
N_DEV = 8
LANES = 128
VMEM_LIMIT = 56 * 1024 * 1024

D_MODEL = 1024
PLE_DIM = 256
HEAD_DIM = 64
RWKV_HEADS = 8
RWKV_DIM = RWKV_HEADS * HEAD_DIM
DECAY_LORA = 64
ICLR_LORA = 64
GATE_LORA = 128
GN_EPS = 64e-5
ATTN_GROUPS = ((128, 1), (512, 4), (2048, 16))
HEADS_PER_GROUP = 4
ATTN_HEADS = HEADS_PER_GROUP * len(ATTN_GROUPS)
ATTN_DIM = ATTN_HEADS * HEAD_DIM
BAND_BLOCK = 128
ROPE_THETA = 10000.0
NEG_INF = -1e30
D_FF = 2816
RMS_EPS = 1e-6
RWKV_COLS = 3 * RWKV_DIM + DECAY_LORA + ICLR_LORA + GATE_LORA
ADAM_LR = 0.001
ADAM_B1 = 0.9
ADAM_B2 = 0.999
ADAM_EPS = 1e-08
ADAM_WD = 0.01
ADAM_STEP = 10

V_LO = LANES // RWKV_HEADS
V_HI = HEAD_DIM // V_LO
SCAN_CHUNK = 64
MM_ROWS = 2048

FF_SHARD = D_FF // N_DEV
FF_PAD = -(-FF_SHARD // LANES) * LANES
FF_HID = N_DEV * FF_PAD
IN_SHARD = 6144 // N_DEV
OUT_SHARD = D_MODEL // N_DEV

SMALL = ("ffn1_norm", "mix_norm", "rwkv_mu", "rwkv_w0", "rwkv_a0", "rwkv_k_k", "rwkv_k_a", "rwkv_r_k",
         "rwkv_gn_w", "rwkv_gn_b", "q_norm", "k_norm", "ffn2_norm", "ple_norm")
WEIGHTS = ("ffn1_norm", "ffn1_w_gate", "ffn1_w_up", "ffn1_w_down", "mix_norm", "w_in", "rwkv_mu", "rwkv_w0",
           "rwkv_w2", "rwkv_a0", "rwkv_a2", "rwkv_g2", "rwkv_k_k", "rwkv_k_a", "rwkv_r_k", "rwkv_gn_w",
           "rwkv_gn_b", "q_norm", "k_norm", "w_br_rwkv", "w_br_attn", "w_out", "ffn2_norm", "ffn2_w_gate",
           "ffn2_w_up", "ffn2_w_down", "ple_norm", "ple_w_gate", "ple_w_proj")


def _cparams(**kw):
    return pltpu.CompilerParams(vmem_limit_bytes=VMEM_LIMIT, **kw)


def _tile(n, cap):
    best = None
    for t in range(LANES, min(n, cap) + 1, LANES):
        if n % t == 0:
            best = t
    return best if best is not None else n


@jax.custom_vjp
def _bdot(a, w):
    return jnp.dot(a.astype(BF16), w.astype(BF16), preferred_element_type=F32)


def _bdot_fwd(a, w):
    return _bdot(a, w), (a, w)


def _bdot_bwd(res, g):
    a, w = res
    gb = g.astype(BF16)
    da = lax.dot_general(gb, w.astype(BF16), (((1,), (1,)), ((), ())), preferred_element_type=F32)
    dw = lax.dot_general(a.astype(BF16), gb, (((0,), (0,)), ((), ())), preferred_element_type=F32)
    return da.astype(a.dtype), dw.astype(w.dtype)


_bdot.defvjp(_bdot_fwd, _bdot_bwd)


@jax.custom_vjp
def _bdot_nt(a, b):
    return lax.dot_general(a.astype(BF16), b.astype(BF16), (((1,), (1,)), ((), ())), preferred_element_type=F32)


def _bdot_nt_fwd(a, b):
    return _bdot_nt(a, b), (a, b)


def _bdot_nt_bwd(res, g):
    a, b = res
    gb = g.astype(BF16)
    da = jnp.dot(gb, b.astype(BF16), preferred_element_type=F32)
    db = lax.dot_general(gb, a.astype(BF16), (((0,), (0,)), ((), ())), preferred_element_type=F32)
    return da.astype(a.dtype), db.astype(b.dtype)


_bdot_nt.defvjp(_bdot_nt_fwd, _bdot_nt_bwd)


def _mm(name, a, b, mode, out_dtype=F32, res=None, scale=None):
    if mode == "nn":
        (m, k), n = a.shape, b.shape[1]
    elif mode == "nt":
        (m, k), n = a.shape, b.shape[0]
    else:
        (k, m), n = a.shape, b.shape[1]
    tm, tn = _tile(m, 1024), _tile(n, 512)
    a_spec = pl.BlockSpec((k, tm), lambda i, j: (0, i)) if mode == "tn" else pl.BlockSpec((tm, k), lambda i, j: (i, 0))
    b_spec = pl.BlockSpec((tn, k), lambda i, j: (j, 0)) if mode == "nt" else pl.BlockSpec((k, tn), lambda i, j: (0, j))
    dims = {"nn": ((1,), (0,)), "nt": ((1,), (1,)), "tn": ((0,), (0,))}[mode]
    o_spec = pl.BlockSpec((tm, tn), lambda i, j: (i, j))
    ins, in_specs = [a, b], [a_spec, b_spec]
    if res is not None:
        ins.append(res)
        in_specs.append(o_spec)

    def body(*refs):
        acc = lax.dot_general(refs[0][...].astype(BF16), refs[1][...].astype(BF16), (dims, ((), ())),
                              preferred_element_type=F32)
        if scale is not None:
            acc = acc * scale
        if res is not None:
            acc = acc + refs[2][...].astype(F32)
        refs[-1][...] = acc.astype(refs[-1].dtype)

    return pl.pallas_call(
        body, name=name, grid=(m // tm, n // tn), in_specs=in_specs, out_specs=o_spec,
        out_shape=jax.ShapeDtypeStruct((m, n), out_dtype),
        compiler_params=_cparams(dimension_semantics=("parallel", "parallel")),
    )(*ins)


def _mmc_nn(name, a, wb, ki, ci, n, out_dtype=F32):
    m, k = a.shape
    tm = _tile(m, MM_ROWS)

    def body(a_ref, w_ref, o_ref):
        o_ref[...] = jnp.dot(a_ref[...].astype(BF16), w_ref[...], preferred_element_type=F32).astype(o_ref.dtype)

    return pl.pallas_call(
        body, name=name, grid=(m // tm, N_DEV),
        in_specs=[pl.BlockSpec((tm, k), lambda i, j: (i, 0)), pl.BlockSpec((None, k, n), lambda i, j: (j, ki, ci))],
        out_specs=pl.BlockSpec((tm, n), lambda i, j: (i, j)),
        out_shape=jax.ShapeDtypeStruct((m, N_DEV * n), out_dtype),
        compiler_params=_cparams(dimension_semantics=("parallel", "parallel")),
    )(a, wb)


def _mmc_nt(name, a, wb, ki, ci, n, k, res=None):
    m = a.shape[0]
    tm = _tile(m, MM_ROWS)
    o_spec = pl.BlockSpec((tm, k), lambda i, j: (i, 0))
    ins = [a, wb] + ([res] if res is not None else [])
    in_specs = [pl.BlockSpec((tm, n), lambda i, j: (i, j)), pl.BlockSpec((None, k, n), lambda i, j: (j, ki, ci))]
    in_specs += [o_spec] if res is not None else []

    def body(*refs):
        a_ref, w_ref, o_ref = refs[0], refs[1], refs[-1]
        acc = lax.dot_general(a_ref[...].astype(BF16), w_ref[...], (((1,), (1,)), ((), ())),
                              preferred_element_type=F32)

        @pl.when(pl.program_id(1) == 0)
        def _():
            o_ref[...] = acc + refs[2][...] if res is not None else acc

        @pl.when(pl.program_id(1) != 0)
        def _():
            o_ref[...] += acc

    return pl.pallas_call(
        body, name=name, grid=(m // tm, N_DEV), in_specs=in_specs, out_specs=o_spec,
        out_shape=jax.ShapeDtypeStruct((m, k), F32),
        compiler_params=_cparams(dimension_semantics=("parallel", "arbitrary")),
    )(*ins)


def _mmc_tn(name, x, dy, n):
    m, k = x.shape
    tk = _tile(k, 1024)

    def body(x_ref, dy_ref, o_ref):
        o_ref[...] = lax.dot_general(x_ref[...].astype(BF16), dy_ref[...].astype(BF16), (((0,), (0,)), ((), ())),
                                     preferred_element_type=F32).astype(o_ref.dtype)

    return pl.pallas_call(
        body, name=name, grid=(N_DEV, k // tk),
        in_specs=[pl.BlockSpec((m, tk), lambda j, i: (0, i)), pl.BlockSpec((m, n), lambda j, i: (0, j))],
        out_specs=pl.BlockSpec((None, tk, n), lambda j, i: (j, i, 0)),
        out_shape=jax.ShapeDtypeStruct((N_DEV, k, n), BF16),
        compiler_params=_cparams(dimension_semantics=("parallel", "parallel")),
    )(x, dy)


def _rowwise(name, fn, rows, params, out_rows, out_accs=(), tr=512):
    rows = [a if isinstance(a, tuple) else (a, a.shape[1], 0) for a in rows]
    r = rows[0][0].shape[0]
    in_specs = [pl.BlockSpec((tr, wd), lambda i, cb=cb: (i, cb)) for _, wd, cb in rows]
    rows = [a for a, _, _ in rows]
    in_specs += [pl.BlockSpec(p.shape, lambda i, nd=p.ndim: (0,) * nd) for p in params]
    out_shape = [jax.ShapeDtypeStruct((r, c), dt) for c, dt in out_rows]
    out_shape += [jax.ShapeDtypeStruct(s, F32) for s in out_accs]
    out_specs = [pl.BlockSpec((tr, c), lambda i: (i, 0)) for c, _ in out_rows]
    out_specs += [pl.BlockSpec(s, lambda i, nd=len(s): (0,) * nd) for s in out_accs]
    n_in, n_ro = len(rows) + len(params), len(out_rows)

    def body(*refs):
        res = fn(*[ref[...] for ref in refs[:n_in]])
        outs = refs[n_in:]
        for o, v in zip(outs[:n_ro], res[:n_ro]):
            o[...] = v.astype(o.dtype)
        for o, v in zip(outs[n_ro:], res[n_ro:]):
            _accumulate(o, v)

    return pl.pallas_call(
        body, name=name, grid=(r // tr,), in_specs=in_specs, out_specs=out_specs, out_shape=out_shape,
        compiler_params=_cparams(dimension_semantics=("arbitrary",)),
    )(*rows, *params)


def _pairwise(name, fn, rows, params, n_out, n_acc=0, tr=512):
    t, c = rows[0].shape
    tile = pl.BlockSpec((tr, 2 * HEAD_DIM), lambda p, i: (i, p))
    vec = pl.BlockSpec((1, 2 * HEAD_DIM), lambda p, i: (0, p))
    n_in = len(rows) + len(params)

    def body(*refs):
        res = fn(*[ref[...] for ref in refs[:n_in]])
        outs = refs[n_in:]
        for o, v in zip(outs[:n_out], res[:n_out]):
            o[...] = v
        first = pl.program_id(1) == 0
        for o, v in zip(outs[n_out:], res[n_out:]):
            @pl.when(first)
            def _(o=o, v=v):
                o[...] = v

            @pl.when(jnp.logical_not(first))
            def _(o=o, v=v):
                o[...] += v

    return pl.pallas_call(
        body, name=name, grid=(c // (2 * HEAD_DIM), t // tr),
        in_specs=[tile] * len(rows) + [vec] * len(params), out_specs=[tile] * n_out + [vec] * n_acc,
        out_shape=[jax.ShapeDtypeStruct((t, c), F32)] * n_out + [jax.ShapeDtypeStruct((1, c), F32)] * n_acc,
        compiler_params=_cparams(dimension_semantics=("parallel", "arbitrary")),
    )(*rows, *params)


def _accumulate(o_ref, v):
    @pl.when(pl.program_id(0) == 0)
    def _():
        o_ref[...] = v

    @pl.when(pl.program_id(0) != 0)
    def _():
        o_ref[...] += v


def _rms(x, g):
    return x * lax.rsqrt(jnp.mean(x * x, axis=-1, keepdims=True) + RMS_EPS) * g


def _sigmoid(x):
    return jax.nn.sigmoid(x)


def _softplus(x):
    return jnp.maximum(x, 0.0) + jnp.log1p(jnp.exp(-jnp.abs(x)))


def _norm_fwd(name, x, g):
    return _rowwise(name, lambda xv, gv: (_rms(xv, gv),), [x], [g], [(x.shape[1], BF16)])[0]


def _norm_bwd(name, x, g, dh, dres):
    def fn(xv, dhv, drv, gv):
        _, vjp = jax.vjp(_rms, xv, gv)
        dx, dg = vjp(dhv)
        return dx + drv, dg
    return _rowwise(name, fn, [x, dh, dres], [g], [(x.shape[1], F32)], [g.shape])


def _f_act(gate, up):
    return gate * _sigmoid(gate) * up


def _gate_up_act(name, h, w_gu):
    m, k = h.shape
    tm = _tile(m, MM_ROWS)

    def body(h_ref, w_ref, gu_ref, a_ref):
        gu = jnp.dot(h_ref[...], w_ref[...], preferred_element_type=F32)
        gu_ref[...] = gu
        a_ref[...] = _f_act(gu[:, :FF_PAD], gu[:, FF_PAD:]).astype(a_ref.dtype)

    return pl.pallas_call(
        body, name=name, grid=(m // tm, N_DEV),
        in_specs=[pl.BlockSpec((tm, k), lambda i, j: (i, 0)),
                  pl.BlockSpec((None, k, 2 * FF_PAD), lambda i, j: (j, 0, 0))],
        out_specs=[pl.BlockSpec((tm, 2 * FF_PAD), lambda i, j: (i, j)), pl.BlockSpec((tm, FF_PAD), lambda i, j: (i, j))],
        out_shape=[jax.ShapeDtypeStruct((m, N_DEV * 2 * FF_PAD), F32), jax.ShapeDtypeStruct((m, FF_HID), BF16)],
        compiler_params=_cparams(dimension_semantics=("parallel", "parallel")),
    )(h, w_gu)


def _gate_up_act_bwd(name, dout, w_down, gu, order):
    m, k = dout.shape
    tm = _tile(m, MM_ROWS)

    def body(d_ref, w_ref, gu_ref, order_ref, o_ref):
        da = 0.5 * lax.dot_general(d_ref[...].astype(BF16), w_ref[...], (((1,), (1,)), ((), ())),
                                   preferred_element_type=F32)
        guv = gu_ref[...]
        _, vjp = jax.vjp(_f_act, guv[:, :FF_PAD], guv[:, FF_PAD:])
        o_ref[...] = jnp.concatenate(vjp(da), axis=1).astype(o_ref.dtype)

    gu_spec = pl.BlockSpec((tm, 2 * FF_PAD), lambda i, j: (i, j))
    return pl.pallas_call(
        body, name=name, grid=(m // tm, N_DEV),
        in_specs=[pl.BlockSpec((tm, k), lambda i, j: (i, 0)), pl.BlockSpec((FF_PAD, k), lambda i, j: (j, 0)), gu_spec,
                  pl.BlockSpec(memory_space=pl.ANY)],
        out_specs=gu_spec, out_shape=jax.ShapeDtypeStruct((m, N_DEV * 2 * FF_PAD), BF16),
        compiler_params=_cparams(dimension_semantics=("parallel", "parallel")),
    )(dout, w_down, gu, order)


def _ffn_fwd(tag, x, norm, w_gu, w_down):
    h = _norm_fwd(tag + "_norm", x, norm)
    gu, a = _gate_up_act(tag + "_gu", h, w_gu)
    wd = w_down(a)
    out = _mm(tag + "_down", a, wd, "nn", res=x, scale=0.5)
    return out, (h, gu, a), wd


def _ffn_bwd(tag, x, norm, w_gu, w_down, saved, dout, on_down=None):
    h, gu, a = saved
    d_wdown = _mm(tag + "_dwdown", a, dout, "tn", out_dtype=BF16, scale=0.5).reshape(N_DEV, FF_PAD, D_MODEL)
    token = on_down(d_wdown) if on_down is not None else jnp.zeros((8, LANES), F32)
    dgu = _gate_up_act_bwd(tag + "_dgu", dout, w_down, gu, token)
    dh =_mmc_nt(tag + "_dh", dgu, w_gu, 0, 0, 2 * FF_PAD, D_MODEL)
    d_wgu = _mmc_tn(tag + "_dwgu", h, dgu, 2 * FF_PAD)
    dx, dnorm = _norm_bwd(tag + "_dnorm", x, norm, dh, dout)
    return dx, dnorm, d_wgu, d_wdown


def _shift_fwd(z, mu):
    t, c = z.shape[0], RWKV_COLS
    tr = 256

    def body(z_ref, zp_ref, mu_ref, r_ref, k_ref, v_ref, lo_ref, gd_ref):
        zv = z_ref[...]
        prev = zp_ref[7:8, :] * jnp.where(pl.program_id(0) == 0, 0.0, 1.0)
        row = lax.broadcasted_iota(jnp.int32, zv.shape, 0)
        zsh = jnp.where(row == 0, prev, pltpu.roll(zv, 1, 0))
        zs = zv + (zsh - zv) * mu_ref[...]
        r_ref[...] = zs[:, 0:512]
        k_ref[...] = zs[:, 512:1024]
        v_ref[...] = zs[:, 1024:1536]
        lo_ref[...] = zs[:, 1536:1664]
        gd_ref[...] = zs[:, 1664:1792]

    widths = (512, 512, 512, 128, 128)
    return pl.pallas_call(
        body, name="rwkv_shift", grid=(t // tr,),
        in_specs=[pl.BlockSpec((tr, c), lambda i: (i, 0)),
                  pl.BlockSpec((8, c), lambda i: (jnp.maximum(i * (tr // 8) - 1, 0), 0)),
                  pl.BlockSpec((1, c), lambda i: (0, 0))],
        out_specs=[pl.BlockSpec((tr, w), lambda i: (i, 0)) for w in widths],
        out_shape=[jax.ShapeDtypeStruct((t, w), F32) for w in widths],
        compiler_params=_cparams(dimension_semantics=("parallel",)),
    )(z, z, mu)


def _shift_bwd(z, mu, dr, dk, dv, dlo, dgd):
    t, c = z.shape[0], RWKV_COLS
    tr = 256
    nt = t // tr

    def body(z_ref, zp_ref, mu_ref, dr_ref, dk_ref, dv_ref, dlo_ref, dgd_ref,
             drn_ref, dkn_ref, dvn_ref, dlon_ref, dgdn_ref, dz_ref, dmu_ref):
        i = pl.program_id(0)
        zv, muv = z_ref[...], mu_ref[...]
        prev = zp_ref[7:8, :] * jnp.where(i == 0, 0.0, 1.0)
        row = lax.broadcasted_iota(jnp.int32, zv.shape, 0)
        zsh = jnp.where(row == 0, prev, pltpu.roll(zv, 1, 0))
        dzs = jnp.concatenate([dr_ref[...], dk_ref[...], dv_ref[...], dlo_ref[...], dgd_ref[...]], axis=1)
        nxt = jnp.concatenate([drn_ref[0:1, :], dkn_ref[0:1, :], dvn_ref[0:1, :], dlon_ref[0:1, :],
                               dgdn_ref[0:1, :]], axis=1) * jnp.where(i == nt - 1, 0.0, 1.0)
        u = dzs * muv
        un = jnp.where(row == tr - 1, nxt * muv, pltpu.roll(u, tr - 1, 0))
        dz_ref[...] = (dzs - u + un).astype(dz_ref.dtype)
        _accumulate(dmu_ref, jnp.sum(dzs * (zsh - zv), axis=0, keepdims=True))

    widths = (512, 512, 512, 128, 128)
    nxt_map = lambda i: (jnp.minimum((i + 1) * (tr // 8), t // 8 - 1), 0)
    return pl.pallas_call(
        body, name="rwkv_shift_bwd", grid=(nt,),
        in_specs=[pl.BlockSpec((tr, c), lambda i: (i, 0)),
                  pl.BlockSpec((8, c), lambda i: (jnp.maximum(i * (tr // 8) - 1, 0), 0)),
                  pl.BlockSpec((1, c), lambda i: (0, 0))]
        + [pl.BlockSpec((tr, w), lambda i: (i, 0)) for w in widths]
        + [pl.BlockSpec((8, w), nxt_map) for w in widths],
        out_specs=[pl.BlockSpec((tr, c), lambda i: (i, 0)), pl.BlockSpec((1, c), lambda i: (0, 0))],
        out_shape=[jax.ShapeDtypeStruct((t, c), BF16), jax.ShapeDtypeStruct((1, c), F32)],
        compiler_params=_cparams(dimension_semantics=("arbitrary",)),
    )(z, z, mu, dr, dk, dv, dlo, dgd, dr, dk, dv, dlo, dgd)


def _f_pre(k, lo, gd, w0, w2p, a0, a2p, g2, k_k, k_a):
    lane = lax.broadcasted_iota(jnp.int32, lo.shape, 1)
    lo_act = jnp.where(lane < DECAY_LORA, jnp.tanh(lo), lo)
    w = -_softplus(-(w0 + _bdot(lo_act, w2p))) - 0.5
    a = _sigmoid(a0 + _bdot(lo_act, a2p))
    g = _bdot(_sigmoid(gd), g2)
    kk = k * k_k
    k2 = k * (1.0 + (a - 1.0) * k_a)
    decay = jnp.exp(-jnp.exp(w))
    return k2, kk, a, decay, g


def _f_kk(kk, a):
    kkn = kk * lax.rsqrt(jnp.maximum(_head_sums(kk * kk), 1e-24))
    return kkn, kkn * a


def _f_post(y, r, k2, v, g, gn_w, gn_b, r_k):
    mean = _head_sums(y) * (1.0 / HEAD_DIM)
    var = _head_sums(jnp.square(y - mean)) * (1.0 / HEAD_DIM)
    yn = (y - mean) * lax.rsqrt(var + GN_EPS) * gn_w + gn_b
    bonus = _head_sums(r * k2 * r_k) * v
    return (yn + bonus) * g


def _to_v_rows(x):
    t = x.shape[0]
    return x.reshape(t, RWKV_HEADS, V_HI, V_LO).transpose(0, 2, 3, 1).reshape(t, V_HI, LANES)


def _from_v_rows(x):
    t = x.shape[0]
    return x.reshape(t, V_HI, V_LO, RWKV_HEADS).transpose(0, 3, 1, 2).reshape(t, RWKV_DIM)


def _k_cols(x):
    return jnp.tile(x, (V_LO, 1)).T


def _k_rows(x):
    xt = x.T
    out = xt[0:RWKV_HEADS]
    for l in range(1, V_LO):
        out = out + xt[l * RWKV_HEADS:(l + 1) * RWKV_HEADS]
    return out


def _wkv_fwd(r, w, k, kk, b, v):
    t = r.shape[0]
    tc = SCAN_CHUNK
    key_spec = pl.BlockSpec((tc, RWKV_HEADS, HEAD_DIM), lambda i: (i, 0, 0))
    row_spec = pl.BlockSpec((tc, V_HI, LANES), lambda i: (i, 0, 0))

    def body(r_ref, w_ref, k_ref, kk_ref, b_ref, v_ref, y_ref, st_ref, s_scr, cols_a, cols_b):
        @pl.when(pl.program_id(0) == 0)
        def _():
            s_scr[...] = jnp.zeros_like(s_scr)

        def prep(ti, buf):
            for n, ref in enumerate((r_ref, w_ref, k_ref, kk_ref, b_ref)):
                buf[n] = _k_cols(ref[ti])

        def step(ti, s, cur, nxt, ti_next):
            rc, wc, kc, kkc, bc = (cur[n] for n in range(5))
            prep(ti_next, nxt)
            vt = v_ref[ti]
            new, ys = [], []
            for j in range(V_HI):
                sa = -jnp.sum(s[j] * kkc, axis=0, keepdims=True)
                nj = s[j] * wc + bc * sa + kc * vt[j:j + 1]
                st_ref[ti, j] = nj
                ys.append(jnp.sum(nj * rc, axis=0, keepdims=True))
                new.append(nj)
            y_ref[ti] = jnp.concatenate(ys, axis=0)
            return tuple(new)

        def pair(i, s):
            s = step(2 * i, s, cols_a, cols_b, 2 * i + 1)
            return step(2 * i + 1, s, cols_b, cols_a, jnp.minimum(2 * i + 2, tc - 1))

        prep(0, cols_a)
        s = lax.fori_loop(0, tc // 2, pair, tuple(s_scr[j] for j in range(V_HI)))
        for j in range(V_HI):
            s_scr[j] = s[j]

    return pl.pallas_call(
        body, name="wkv_fwd", grid=(t // tc,),
        in_specs=[key_spec] * 5 + [row_spec],
        out_specs=[row_spec, pl.BlockSpec((tc, V_HI, HEAD_DIM, LANES), lambda i: (i, 0, 0, 0))],
        out_shape=[jax.ShapeDtypeStruct((t, V_HI, LANES), F32),
                   jax.ShapeDtypeStruct((t, V_HI, HEAD_DIM, LANES), F32)],
        scratch_shapes=[pltpu.VMEM((V_HI, HEAD_DIM, LANES), F32)] + [pltpu.VMEM((5, HEAD_DIM, LANES), F32)] * 2,
        compiler_params=_cparams(dimension_semantics=("arbitrary",)),
    )(r, w, k, kk, b, v)


def _wkv_bwd(r, w, k, kk, b, v, states, dy):
    t = r.shape[0]
    tc = SCAN_CHUNK
    nb = t // tc
    key_spec = pl.BlockSpec((tc, RWKV_HEADS, HEAD_DIM), lambda i: (nb - 1 - i, 0, 0))
    row_spec = pl.BlockSpec((tc, V_HI, LANES), lambda i: (nb - 1 - i, 0, 0))
    st_spec = pl.BlockSpec((tc, V_HI, HEAD_DIM, LANES), lambda i: (nb - 1 - i, 0, 0, 0))
    stp_spec = pl.BlockSpec((1, V_HI, HEAD_DIM, LANES), lambda i: (jnp.maximum((nb - 1 - i) * tc - 1, 0), 0, 0, 0))

    def body(r_ref, w_ref, k_ref, kk_ref, b_ref, v_ref, st_ref, stp_ref, dy_ref,
             dr_ref, dw_ref, dk_ref, dkk_ref, db_ref, dv_ref, ds_scr, cols_a, cols_b, accs_a, accs_b):
        @pl.when(pl.program_id(0) == 0)
        def _():
            ds_scr[...] = jnp.zeros_like(ds_scr)

        def colsum(x):
            return jnp.sum(x, axis=0, keepdims=True)

        def prep(ti, buf):
            for n, ref in enumerate((r_ref, w_ref, k_ref, kk_ref, b_ref)):
                buf[n] = _k_cols(ref[ti])

        def flush(ti, buf):
            for n, ref in enumerate((dr_ref, dk_ref, db_ref, dw_ref, dkk_ref)):
                ref[ti] = _k_rows(buf[n])

        def step(ti, ds, sp, cur, accs):
            rc, wc, kc, kkc, bc = (cur[n] for n in range(5))
            vt, dyt = v_ref[ti], dy_ref[ti]
            acc = None
            new, dvs = [], []
            for j in range(V_HI):
                st = st_ref[ti, j]
                dsj = ds[j] + rc * dyt[j:j + 1]
                sa = -colsum(sp[j] * kkc)
                dsa = colsum(dsj * bc)
                dvs.append(colsum(dsj * kc))
                parts = (st * dyt[j:j + 1], dsj * vt[j:j + 1], dsj * sa, dsj * sp[j], -(sp[j] * dsa))
                acc = parts if acc is None else tuple(a + q for a, q in zip(acc, parts))
                new.append(dsj * wc - kkc * dsa)
            dv_ref[ti] = jnp.concatenate(dvs, axis=0)
            for n in range(5):
                accs[n] = acc[n]
            return tuple(new)

        def states_before(ti):
            return tuple(st_ref[ti - 1, j] for j in range(V_HI))

        def pair(i, ds):
            ta = tc - 1 - 2 * i
            prep(ta - 1, cols_b)
            flush(jnp.minimum(ta + 1, tc - 1), accs_b)
            ds = step(ta, ds, states_before(ta), cols_a, accs_a)
            prep(ta - 2, cols_a)
            flush(ta, accs_a)
            return step(ta - 1, ds, states_before(ta - 1), cols_b, accs_b)

        prep(tc - 1, cols_a)
        accs_b[...] = jnp.zeros_like(accs_b)
        ds = lax.fori_loop(0, tc // 2 - 1, pair, tuple(ds_scr[j] for j in range(V_HI)))
        prep(0, cols_b)
        flush(2, accs_b)
        ds = step(1, ds, states_before(1), cols_a, accs_a)
        flush(1, accs_a)
        keep = jnp.where(pl.program_id(0) == nb - 1, 0.0, 1.0)
        ds = step(0, ds, tuple(stp_ref[0, j] * keep for j in range(V_HI)), cols_b, accs_b)
        flush(0, accs_b)
        for j in range(V_HI):
            ds_scr[j] = ds[j]

    key_out = jax.ShapeDtypeStruct((t, RWKV_HEADS, HEAD_DIM), F32)
    return pl.pallas_call(
        body, name="wkv_bwd", grid=(nb,),
        in_specs=[key_spec] * 5 + [row_spec, st_spec, stp_spec, row_spec],
        out_specs=[key_spec] * 5 + [row_spec],
        out_shape=[key_out] * 5 + [jax.ShapeDtypeStruct((t, V_HI, LANES), F32)],
        scratch_shapes=[pltpu.VMEM((V_HI, HEAD_DIM, LANES), F32)] + [pltpu.VMEM((5, HEAD_DIM, LANES), F32)] * 4,
        compiler_params=_cparams(dimension_semantics=("arbitrary",)),
    )(r, w, k, kk, b, v, states, states, dy)


PAIR = 2 * HEAD_DIM
N_PAIRS = ATTN_HEADS // 2
Q_COL0 = RWKV_COLS // PAIR
K_COL0 = Q_COL0 + N_PAIRS
V_COL0 = K_COL0 + N_PAIRS


def _swap_halves(x):
    lane = lax.broadcasted_iota(jnp.int32, x.shape, 1)
    return jnp.where((lane & (HEAD_DIM - 1)) < HEAD_DIM // 2, pltpu.roll(x, PAIR - HEAD_DIM // 2, 1),
                     pltpu.roll(x, HEAD_DIM // 2, 1))


@jax.custom_vjp
def _rope(x, cosf, sinf):
    return x * cosf + _swap_halves(x) * sinf


def _rope_fwd(x, cosf, sinf):
    return _rope(x, cosf, sinf), (cosf, sinf)


def _rope_bwd(res, d):
    cosf, sinf = res
    return d * cosf + _swap_halves(d * sinf), jnp.zeros_like(cosf), jnp.zeros_like(sinf)


_rope.defvjp(_rope_fwd, _rope_bwd)


def _head_sums(x):
    lane = lax.broadcasted_iota(jnp.int32, x.shape, 1)
    lo = jnp.where(lane < HEAD_DIM, 1.0, 0.0)
    hi = 1.0 - lo
    return lo * jnp.sum(x * lo, axis=1, keepdims=True) + hi * jnp.sum(x * hi, axis=1, keepdims=True)


def _f_qk(x, cosf, sinf, gain2):
    xn = x * lax.rsqrt(_head_sums(x * x) * (1.0 / HEAD_DIM) + RMS_EPS) * gain2
    return _rope(xn, cosf, sinf)


def _qk_prep(z, tab, q_gain, k_gain):
    t = z.shape[0]
    tr = 1024

    def body(z_ref, c_ref, s_ref, qg_ref, kg_ref, o_ref):
        g = jnp.where(pl.program_id(0) < N_PAIRS, qg_ref[...], kg_ref[...])
        o_ref[...] = _f_qk(z_ref[...], c_ref[...], s_ref[...], jnp.concatenate([g, g], axis=1))

    gain = pl.BlockSpec((1, HEAD_DIM), lambda c, i: (0, 0))
    return pl.pallas_call(
        body, name="qk_prep", grid=(2 * N_PAIRS, t // tr),
        in_specs=[pl.BlockSpec((tr, PAIR), lambda c, i: (i, Q_COL0 + c)), pl.BlockSpec((tr, PAIR), lambda c, i: (i, 0)),
                  pl.BlockSpec((tr, PAIR), lambda c, i: (i, 1)), gain, gain],
        out_specs=pl.BlockSpec((tr, PAIR), lambda c, i: (i, c)),
        out_shape=jax.ShapeDtypeStruct((t, 2 * N_PAIRS * PAIR), F32),
        compiler_params=_cparams(dimension_semantics=("parallel", "parallel")),
    )(z, tab, tab, q_gain, k_gain)


def _qk_prep_bwd(z, tab, q_gain, k_gain, dq, dk):
    t = z.shape[0]
    tr = 1024

    def body(z_ref, c_ref, s_ref, qg_ref, kg_ref, dq_ref, dk_ref, dz_ref, dqg_ref, dkg_ref):
        c, i = pl.program_id(0), pl.program_id(1)
        is_q = c < N_PAIRS
        g = jnp.where(is_q, qg_ref[...], kg_ref[...])
        d = jnp.where(is_q, dq_ref[...], dk_ref[...])
        _, vjp = jax.vjp(lambda xx, gg: _f_qk(xx, c_ref[...], s_ref[...], gg), z_ref[...],
                         jnp.concatenate([g, g], axis=1))
        dx, dg2 = vjp(d)
        dz_ref[...] = dx.astype(dz_ref.dtype)
        dg = dg2[:, :HEAD_DIM] + dg2[:, HEAD_DIM:]
        first_q = jnp.logical_and(c == 0, i == 0)
        first_k = jnp.logical_and(c == N_PAIRS, i == 0)

        @pl.when(first_q)
        def _():
            dqg_ref[...] = dg

        @pl.when(jnp.logical_and(is_q, jnp.logical_not(first_q)))
        def _():
            dqg_ref[...] += dg

        @pl.when(first_k)
        def _():
            dkg_ref[...] = dg

        @pl.when(jnp.logical_and(jnp.logical_not(is_q), jnp.logical_not(first_k)))
        def _():
            dkg_ref[...] += dg

    gain = pl.BlockSpec((1, HEAD_DIM), lambda c, i: (0, 0))
    return pl.pallas_call(
        body, name="qk_prep_bwd", grid=(2 * N_PAIRS, t // tr),
        in_specs=[pl.BlockSpec((tr, PAIR), lambda c, i: (i, Q_COL0 + c)), pl.BlockSpec((tr, PAIR), lambda c, i: (i, 0)),
                  pl.BlockSpec((tr, PAIR), lambda c, i: (i, 1)), gain, gain,
                  pl.BlockSpec((tr, PAIR), lambda c, i: (i, jnp.minimum(c, N_PAIRS - 1))),
                  pl.BlockSpec((tr, PAIR), lambda c, i: (i, jnp.maximum(c - N_PAIRS, 0)))],
        out_specs=[pl.BlockSpec((tr, PAIR), lambda c, i: (i, c)), gain, gain],
        out_shape=[jax.ShapeDtypeStruct((t, 2 * N_PAIRS * PAIR), BF16), jax.ShapeDtypeStruct((1, HEAD_DIM), F32),
                   jax.ShapeDtypeStruct((1, HEAD_DIM), F32)],
        compiler_params=_cparams(dimension_semantics=("arbitrary", "arbitrary")),
    )(z, tab, tab, q_gain, k_gain, dq, dk)


def _attn_block(q, kp, kc, vp, vc, kmin):
    k2 = jnp.concatenate([kp, kc], axis=0)
    v2 = jnp.concatenate([vp, vc], axis=0)
    s = _bdot_nt(q, k2) * (HEAD_DIM ** -0.5)
    qi = lax.broadcasted_iota(jnp.int32, s.shape, 0)
    kj = lax.broadcasted_iota(jnp.int32, s.shape, 1)
    dist = qi + BAND_BLOCK - kj
    valid = (dist >= 0) & (dist <= BAND_BLOCK) & (kj >= kmin)
    s = jnp.where(valid, s, NEG_INF)
    m = lax.stop_gradient(jnp.max(s, axis=-1, keepdims=True))
    e = jnp.exp(s - m)
    l = jnp.sum(e, axis=-1, keepdims=True)
    o = _bdot(e, v2) / l
    return o, m + jnp.log(l)


def _fold(src_ref, dst_ref, dil):
    t = src_ref.shape[0]
    ln = t // dil
    for j in range(dil):
        dst_ref[j * ln:(j + 1) * ln, :] = src_ref[pl.ds(j, ln, stride=dil), :]


def _unfold(src_ref, dst_ref, dil):
    t = src_ref.shape[0]
    ln = t // dil
    for j in range(dil):
        dst_ref[pl.ds(j, ln, stride=dil), :] = src_ref[j * ln:(j + 1) * ln, :]


def _per_group(fn):
    pair = pl.program_id(0)
    for gi, (_, dil) in enumerate(ATTN_GROUPS):
        @pl.when(jnp.logical_or(pair == 2 * gi, pair == 2 * gi + 1))
        def _(dil=dil):
            fn(dil)


def _block_rows(idx, blocks_per_seq):
    first = (idx & (blocks_per_seq - 1)) == 0
    cur = pl.ds(pl.multiple_of(idx * BAND_BLOCK, BAND_BLOCK), BAND_BLOCK)
    prev = pl.ds(pl.multiple_of(jnp.maximum(idx - 1, 0) * BAND_BLOCK, BAND_BLOCK), BAND_BLOCK)
    return first, cur, prev


def _heads(x):
    return x[:, :HEAD_DIM], x[:, HEAD_DIM:]


def _attn_fwd(qk, z):
    t = z.shape[0]
    n_blocks = t // BAND_BLOCK

    def body(q_ref, k_ref, v_ref, o_ref, lse_ref, qf, kf, vf, of, lf):
        def run(dil):
            _fold(q_ref, qf, dil)
            _fold(k_ref, kf, dil)
            _fold(v_ref, vf, dil)
            blocks_per_seq = n_blocks // dil

            def block(idx, carry):
                first, cur, prev = _block_rows(idx, blocks_per_seq)
                kmin = jnp.where(first, BAND_BLOCK, 0)
                outs, lses = [], []
                for q, kp, kc, vp, vc in zip(_heads(qf[cur, :]), _heads(kf[prev, :]), _heads(kf[cur, :]),
                                             _heads(vf[prev, :]), _heads(vf[cur, :])):
                    o, ls = _attn_block(q, kp, kc, vp, vc, kmin)
                    outs.append(o)
                    lses.append(jnp.broadcast_to(ls, o.shape))
                of[cur, :] = jnp.concatenate(outs, axis=1)
                lf[cur, :] = jnp.concatenate(lses, axis=1)
                return carry

            lax.fori_loop(0, n_blocks, block, 0, unroll=4)
            _unfold(of, o_ref, dil)
            _unfold(lf, lse_ref, dil)

        _per_group(run)

    slab = jax.ShapeDtypeStruct((t, N_PAIRS * PAIR), F32)
    out_spec = pl.BlockSpec((t, PAIR), lambda p: (0, p))
    return pl.pallas_call(
        body, name="attn_fwd", grid=(N_PAIRS,),
        in_specs=[pl.BlockSpec((t, PAIR), lambda p: (0, p)), pl.BlockSpec((t, PAIR), lambda p: (0, N_PAIRS + p)),
                  pl.BlockSpec((t, PAIR), lambda p: (0, V_COL0 + p))],
        out_specs=[out_spec, out_spec], out_shape=[slab, slab],
        scratch_shapes=[pltpu.VMEM((t, PAIR), F32)] * 5,
        compiler_params=_cparams(dimension_semantics=("parallel",)),
    )(qk, qk, z)


def _attn_bwd(qk, z, do, dlse):
    t = z.shape[0]
    n_blocks = t // BAND_BLOCK

    def body(q_ref, k_ref, v_ref, do_ref, dl_ref, dq_ref, dk_ref, dv_ref, qf, kf, vf, dof, dlf, dqf, dkf, dvf,
             dkpf, dvpf):
        def run(dil):
            for src, dst in ((q_ref, qf), (k_ref, kf), (v_ref, vf), (do_ref, dof), (dl_ref, dlf)):
                _fold(src, dst, dil)
            blocks_per_seq = n_blocks // dil

            def block(idx, carry):
                first, cur, prev = _block_rows(idx, blocks_per_seq)
                kmin = jnp.where(first, BAND_BLOCK, 0)
                grads = []
                for q, kp, kc, vp, vc, do_h, dl_h in zip(
                        _heads(qf[cur, :]), _heads(kf[prev, :]), _heads(kf[cur, :]), _heads(vf[prev, :]),
                        _heads(vf[cur, :]), _heads(dof[cur, :]), _heads(dlf[cur, :])):
                    _, vjp = jax.vjp(functools.partial(_attn_block, kmin=kmin), q, kp, kc, vp, vc)
                    grads.append(vjp((do_h, jnp.sum(dl_h, axis=1, keepdims=True))))
                dq, dkp, dkc, dvp, dvc = (jnp.concatenate([a, b], axis=1) for a, b in zip(*grads))
                dqf[cur, :], dkf[cur, :], dvf[cur, :], dkpf[cur, :], dvpf[cur, :] = dq, dkc, dvc, dkp, dvp
                return carry

            lax.fori_loop(0, n_blocks, block, 0, unroll=2)

            def join(idx, carry):
                first, cur, prev = _block_rows(idx, blocks_per_seq)

                @pl.when(jnp.logical_not(first))
                def _():
                    dkf[prev, :] += dkpf[cur, :]
                    dvf[prev, :] += dvpf[cur, :]

                return carry

            lax.fori_loop(0, n_blocks, join, 0)
            _unfold(dqf, dq_ref, dil)
            _unfold(dkf, dk_ref, dil)
            _unfold(dvf, dv_ref, dil)

        _per_group(run)

    slab = jax.ShapeDtypeStruct((t, N_PAIRS * PAIR), F32)
    own = pl.BlockSpec((t, PAIR), lambda p: (0, p))
    return pl.pallas_call(
        body, name="attn_bwd", grid=(N_PAIRS,),
        in_specs=[own, pl.BlockSpec((t, PAIR), lambda p: (0, N_PAIRS + p)),
                  pl.BlockSpec((t, PAIR), lambda p: (0, V_COL0 + p)), own, own],
        out_specs=[own] * 3, out_shape=[slab] * 3,
        scratch_shapes=[pltpu.VMEM((t, PAIR), F32)] * 10,
        compiler_params=_cparams(dimension_semantics=("parallel",)),
    )(qk, qk, z, do, dlse)


def _f_comb(o1, o2, o3, l1, l2, l3):
    m = jnp.maximum(jnp.maximum(l1, l2), l3)
    e1, e2, e3 = jnp.exp(l1 - m), jnp.exp(l2 - m), jnp.exp(l3 - m)
    den = e1 + e2 + e3
    return (e1 / den) * o1 + (e2 / den) * o2 + (e3 / den) * o3


def _all_gather_hbm(name, arrs):
    na = len(arrs)

    def body(*refs):
        x_refs, out_refs = refs[:na], refs[na:2 * na]
        send_sems, recv_sems, local_sems = refs[2 * na:]
        mx, my, mc = lax.axis_index("x"), lax.axis_index("y"), lax.axis_index("c")
        me, sibling = (mx, my, mc), (mx, my, 1 - mc)
        chips = [(1 - mx, my), (mx, 1 - my), (1 - mx, 1 - my)]

        def slot(a, px, py, pc):
            return out_refs[a].at[4 * px + 2 * py + pc]

        def copy(a, k, block, to, src=None):
            return pltpu.make_async_remote_copy(
                src_ref=slot(a, *block) if src is None else src, dst_ref=slot(a, *block),
                send_sem=send_sems.at[a, k], recv_sem=recv_sems.at[a, k], device_id=to, device_id_type=MESH)

        mine = [pltpu.make_async_copy(x_refs[a], slot(a, *me), local_sems.at[a]) for a in range(na)]
        for cp in mine:
            cp.start()
        first = []
        for a in range(na):
            first.append(copy(a, 0, me, sibling, src=x_refs[a]))
            first += [copy(a, 1 + j, me, (*chip, mc), src=x_refs[a]) for j, chip in enumerate(chips)]
        for cp in first:
            cp.start()
        passed = []
        for j, chip in enumerate(chips):
            for a in range(na):
                copy(a, 1 + j, (*chip, mc), me).wait_recv()
                passed.append(copy(a, 4 + j, (*chip, mc), sibling))
                passed[-1].start()
        for a in range(na):
            copy(a, 0, sibling, me).wait_recv()
            for j, chip in enumerate(chips):
                copy(a, 4 + j, (*chip, 1 - mc), me).wait_recv()
        for cp in first + passed:
            cp.wait_send()
        for cp in mine:
            cp.wait()

    hbm = pl.BlockSpec(memory_space=pl.ANY)
    return pl.pallas_call(
        body, name=name,
        out_shape=[jax.ShapeDtypeStruct((N_DEV,) + a.shape, a.dtype) for a in arrs],
        in_specs=[hbm] * na, out_specs=[hbm] * na,
        scratch_shapes=[pltpu.SemaphoreType.DMA((na, 7)), pltpu.SemaphoreType.DMA((na, 7)),
                        pltpu.SemaphoreType.DMA((na,))],
    )(*arrs)


def _all_gather_vmem(x):
    rws, cols = x.shape

    def body(x_ref, out_ref, send_sems, recv_sems):
        mx, my, mc = lax.axis_index("x"), lax.axis_index("y"), lax.axis_index("c")
        me, sibling = (mx, my, mc), (mx, my, 1 - mc)
        chips = [(1 - mx, my), (mx, 1 - my), (1 - mx, 1 - my)]

        def slot(px, py, pc):
            return out_ref.at[4 * px + 2 * py + pc]

        def copy(k, block, to, src=None):
            return pltpu.make_async_remote_copy(
                src_ref=slot(*block) if src is None else src, dst_ref=slot(*block),
                send_sem=send_sems.at[k], recv_sem=recv_sems.at[k], device_id=to, device_id_type=MESH)

        first = [copy(0, me, sibling, src=x_ref)]
        first += [copy(1 + j, me, (*chip, mc), src=x_ref) for j, chip in enumerate(chips)]
        for cp in first:
            cp.start()
        out_ref[4 * mx + 2 * my + mc] = x_ref[...]
        passed = [copy(4 + j, (*chip, mc), sibling) for j, chip in enumerate(chips)]
        for j, chip in enumerate(chips):
            copy(1 + j, (*chip, mc), me).wait_recv()
            passed[j].start()
        copy(0, sibling, me).wait_recv()
        for j, chip in enumerate(chips):
            copy(4 + j, (*chip, 1 - mc), me).wait_recv()
        for cp in first + passed:
            cp.wait_send()

    return pl.pallas_call(
        body, name="all_gather_small",
        out_shape=jax.ShapeDtypeStruct((N_DEV, rws, cols), x.dtype),
        in_specs=[pl.BlockSpec(memory_space=pltpu.VMEM)], out_specs=pl.BlockSpec(memory_space=pltpu.VMEM),
        scratch_shapes=[pltpu.SemaphoreType.DMA((7,)), pltpu.SemaphoreType.DMA((7,))],
    )(x)


def _scatter_copies(g_refs, land_refs, send_sems, recv_sems):
    mx, my, mc = lax.axis_index("x"), lax.axis_index("y"), lax.axis_index("c")
    me = 4 * mx + 2 * my + mc
    copies = []
    for k in range(1, N_DEV):
        px, py, pc = mx ^ (k >> 2), my ^ ((k >> 1) & 1), mc ^ (k & 1)
        peer = 4 * px + 2 * py + pc
        for a, (g_ref, land_ref) in enumerate(zip(g_refs, land_refs)):
            copies.append(pltpu.make_async_remote_copy(
                src_ref=g_ref.at[peer], dst_ref=land_ref.at[me], send_sem=send_sems.at[a * (N_DEV - 1) + k - 1],
                recv_sem=recv_sems.at[a * (N_DEV - 1) + k - 1], device_id=(px, py, pc), device_id_type=MESH))
    return copies


_HBM = pl.BlockSpec(memory_space=pltpu.HBM)
_SEM = pl.BlockSpec(memory_space=pltpu.SEMAPHORE)
_DATAFLOW = pltpu.SideEffectType.DATAFLOW_SIDE_EFFECTING


def _gather_copies(x_refs, land_refs, send_sems, recv_sems):
    mx, my, mc = lax.axis_index("x"), lax.axis_index("y"), lax.axis_index("c")
    me = 4 * mx + 2 * my + mc
    copies = []
    for k in range(1, N_DEV):
        px, py, pc = mx ^ (k >> 2), my ^ ((k >> 1) & 1), mc ^ (k & 1)
        for a, (x_ref, land_ref) in enumerate(zip(x_refs, land_refs)):
            copies.append(pltpu.make_async_remote_copy(
                src_ref=x_ref, dst_ref=land_ref.at[me], send_sem=send_sems.at[a * (N_DEV - 1) + k - 1],
                recv_sem=recv_sems.at[a * (N_DEV - 1) + k - 1], device_id=(px, py, pc), device_id_type=MESH))
    return copies


N_CHIPS = N_DEV // 2


def _gather2_first(x_refs, land_refs, send_sems, recv_sems, ks=(0, 1, 2, 3)):
    mx, my, mc = lax.axis_index("x"), lax.axis_index("y"), lax.axis_index("c")
    me = 4 * mx + 2 * my + mc
    peers = [(mx, my, 1 - mc), (1 - mx, my, mc), (mx, 1 - my, mc), (1 - mx, 1 - my, mc)]
    return [pltpu.make_async_remote_copy(
        src_ref=x_ref, dst_ref=land_ref.at[me], send_sem=send_sems.at[a * (N_DEV - 1) + k],
        recv_sem=recv_sems.at[a * (N_DEV - 1) + k], device_id=peers[k], device_id_type=MESH)
        for a, (x_ref, land_ref) in enumerate(zip(x_refs, land_refs)) for k in ks]


def _gather2_second(x_refs, land_refs, send_sems, recv_sems):
    mx, my, mc = lax.axis_index("x"), lax.axis_index("y"), lax.axis_index("c")
    copies = []
    for a, land_ref in enumerate(land_refs):
        for j, (px, py) in enumerate([(1 - mx, my), (mx, 1 - my), (1 - mx, 1 - my)]):
            slot = 4 * px + 2 * py + mc
            copies.append(pltpu.make_async_remote_copy(
                src_ref=land_ref.at[slot], dst_ref=land_ref.at[slot], send_sem=send_sems.at[a * (N_DEV - 1) + j],
                recv_sem=recv_sems.at[a * (N_DEV - 1) + j], device_id=(mx, my, 1 - mc), device_id_type=MESH))
    return copies


def _pair_swap(name, arrs):
    na = len(arrs)

    def body(*refs):
        g_refs, out_refs = refs[:na], refs[na:2 * na]
        send_sems, recv_sems = refs[2 * na:]
        mx, my, mc = lax.axis_index("x"), lax.axis_index("y"), lax.axis_index("c")
        copies = [pltpu.make_async_remote_copy(
            src_ref=g_refs[a].at[2 * q + 1 - mc], dst_ref=out_refs[a].at[q], send_sem=send_sems.at[a, q],
            recv_sem=recv_sems.at[a, q], device_id=(mx, my, 1 - mc), device_id_type=MESH)
            for a in range(na) for q in range(N_CHIPS)]
        for cp in copies:
            cp.start()
        for cp in copies:
            cp.wait_recv()
        for cp in copies:
            cp.wait_send()

    hbm = pl.BlockSpec(memory_space=pl.ANY)
    return pl.pallas_call(
        body, name=name,
        out_shape=[jax.ShapeDtypeStruct((N_CHIPS,) + a.shape[1:], a.dtype) for a in arrs],
        in_specs=[hbm] * na, out_specs=[hbm] * na,
        scratch_shapes=[pltpu.SemaphoreType.DMA((na, N_CHIPS)), pltpu.SemaphoreType.DMA((na, N_CHIPS))],
    )(*arrs)


def _pair_add(name, g, half, core):
    _, r, c = g.shape
    tr = _tile(r, 512) if r % LANES == 0 else r

    def body(core_ref, g_ref, h_ref, o_ref):
        o_ref[...] = (g_ref[...].astype(F32) + h_ref[...].astype(F32)).astype(o_ref.dtype)

    return pl.pallas_call(
        body, name=name,
        grid_spec=pltpu.PrefetchScalarGridSpec(
            num_scalar_prefetch=1, grid=(N_CHIPS, r // tr),
            in_specs=[pl.BlockSpec((None, tr, c), lambda q, i, core_ref: (2 * q + core_ref[0], i, 0)),
                      pl.BlockSpec((None, tr, c), lambda q, i, core_ref: (q, i, 0))],
            out_specs=pl.BlockSpec((None, tr, c), lambda q, i, core_ref: (q, i, 0))),
        out_shape=jax.ShapeDtypeStruct((N_CHIPS, r, c), g.dtype),
        compiler_params=_cparams(dimension_semantics=("parallel", "parallel")),
    )(core, g, half)


def _chip_copies(h_refs, land_refs, send_sems, recv_sems):
    mx, my, mc = lax.axis_index("x"), lax.axis_index("y"), lax.axis_index("c")
    my_chip = 2 * mx + my
    copies = []
    for k in range(1, N_CHIPS):
        px, py = mx ^ (k >> 1), my ^ (k & 1)
        for a, (h_ref, land_ref) in enumerate(zip(h_refs, land_refs)):
            copies.append(pltpu.make_async_remote_copy(
                src_ref=h_ref.at[2 * px + py], dst_ref=land_ref.at[my_chip], send_sem=send_sems.at[a * (N_DEV - 1) + k - 1],
                recv_sem=recv_sems.at[a * (N_DEV - 1) + k - 1], device_id=(px, py, mc), device_id_type=MESH))
    return copies


def _exchange_start(name, copies, srcs, lands, after):
    na = len(srcs)

    def body(*refs):
        for cp in copies(refs[:na], refs[na:2 * na], refs[2 * na + 1], refs[2 * na + 2]):
            cp.start()
        refs[-1][...] = jnp.zeros_like(refs[-1])

    in_hbm = lambda a: pltpu.with_memory_space_constraint(a, pltpu.HBM)
    outs = pl.pallas_call(
        body, name=name,
        out_shape=(pltpu.SemaphoreType.DMA((na * (N_DEV - 1),)), pltpu.SemaphoreType.DMA((na * (N_DEV - 1),)),
                   *[pltpu.HBM(a.shape, a.dtype) for a in list(srcs) + list(lands)],
                   jax.ShapeDtypeStruct((8, LANES), F32)),
        in_specs=[_HBM] * (2 * na) + [pl.BlockSpec(memory_space=pl.ANY)],
        out_specs=(_SEM, _SEM, *[_HBM] * (2 * na), pl.BlockSpec(memory_space=pltpu.VMEM)),
        input_output_aliases={i: 2 + i for i in range(2 * na)},
        compiler_params=pltpu.CompilerParams(has_side_effects=_DATAFLOW),
    )(*[in_hbm(a) for a in srcs], *[in_hbm(a) for a in lands], after)
    return outs[0], outs[1], outs[2:2 + na], outs[2 + na:2 + 2 * na], outs[-1]


def _exchange_wait(name, copies, send_sems, recv_sems, srcs, lands, after, parts=("send", "recv")):
    na = len(srcs)

    def body(*refs):
        for cp in copies(refs[:na], refs[na:2 * na], refs[2 * na], refs[2 * na + 1]):
            if "send" in parts:
                cp.wait_send()
            if "recv" in parts:
                cp.wait_recv()

    outs = pl.pallas_call(
        body, name=name,
        out_shape=[pltpu.HBM(a.shape, a.dtype) for a in list(srcs) + list(lands)],
        in_specs=[_HBM] * (2 * na) + [_SEM, _SEM, pl.BlockSpec(memory_space=pl.ANY)], out_specs=[_HBM] * (2 * na),
        input_output_aliases={i: i for i in range(2 * na)},
        compiler_params=pltpu.CompilerParams(has_side_effects=_DATAFLOW),
    )(*srcs, *lands, send_sems, recv_sems, after)
    return outs[:na], outs[na:]


def _sum_slots(name, g, tr):
    _, rws, cols = g.shape

    def body(g_ref, o_ref):
        acc = g_ref[0].astype(F32)
        for j in range(1, N_DEV):
            acc = acc + g_ref[j].astype(F32)
        o_ref[...] = acc

    return pl.pallas_call(
        body, name=name, grid=(rws // tr,),
        in_specs=[pl.BlockSpec((N_DEV, tr, cols), lambda i: (0, i, 0))],
        out_specs=pl.BlockSpec((tr, cols), lambda i: (i, 0)),
        out_shape=jax.ShapeDtypeStruct((rws, cols), F32),
        compiler_params=_cparams(dimension_semantics=("parallel",)),
    )(g)


def _adam_math(wv, gv, mv, vv):
    mn = ADAM_B1 * mv + (1.0 - ADAM_B1) * gv
    vn = ADAM_B2 * vv + (1.0 - ADAM_B2) * jnp.square(gv)
    m_hat = mn / (1.0 - ADAM_B1 ** ADAM_STEP)
    v_hat = vn / (1.0 - ADAM_B2 ** ADAM_STEP)
    delta = -ADAM_LR * (m_hat / (jnp.sqrt(v_hat) + ADAM_EPS) + ADAM_WD * wv)
    return delta, mn, vn


def _adamw(name, w, g, m, v, tr):
    return _rowwise(name, _adam_math, [w, g, m, v], [], [(LANES, F32)] * 3, tr=tr)


def _adamw_slots(name, recv, own, rb, cb, cw, w, m, v, tr, order):
    nr, nc = w.shape
    n = recv.shape[0]

    def body(g_ref, own_ref, w_ref, m_ref, v_ref, order_ref, go_ref, d_ref, mo_ref, vo_ref):
        me = 2 * lax.axis_index("x") + lax.axis_index("y")
        if n == N_DEV:
            me = 2 * me + lax.axis_index("c")
        acc = None
        for s in range(n):
            part = jnp.where(me == s, own_ref[s], g_ref[s]).astype(F32)
            acc = part if acc is None else acc + part
        g = acc[:, :nc]
        go_ref[...] = g
        d_ref[...], mo_ref[...], vo_ref[...] = _adam_math(w_ref[...], g, m_ref[...], v_ref[...])

    nat = pl.BlockSpec((tr, nc), lambda i: (i, 0))
    slots = pl.BlockSpec((n, tr, cw), lambda i: (0, rb + i, cb))
    return pl.pallas_call(
        body, name=name, grid=(nr // tr,),
        in_specs=[slots, slots, nat, nat, nat, pl.BlockSpec(memory_space=pl.ANY)],
        out_specs=[nat] * 4, out_shape=[jax.ShapeDtypeStruct((nr, nc), F32)] * 4,
        compiler_params=_cparams(dimension_semantics=("parallel",)),
    )(recv, own, w, m, v, order)


def _local_blocks(w):
    pad_cols = lambda a: jnp.pad(a, ((0, 0), (0, FF_PAD - FF_SHARD)))
    pad_rows = lambda a: jnp.pad(a, ((0, FF_PAD - FF_SHARD), (0, 0)))
    gate_up = lambda tag: jnp.concatenate([pad_cols(w[tag + "_w_gate"]), pad_cols(w[tag + "_w_up"])], axis=1)
    blocks = {
        "ffn1_gu": gate_up("ffn1"), "ffn1_d": pad_rows(w["ffn1_w_down"]), "w_in": w["w_in"],
        "lora": jnp.concatenate([w["rwkv_w2"], w["rwkv_a2"], w["rwkv_g2"]], axis=0),
        "br": jnp.concatenate([w["w_br_rwkv"], w["w_br_attn"], w["ple_w_proj"]], axis=0),
        "w_out": w["w_out"], "ffn2_gu": gate_up("ffn2"), "ffn2_d": pad_rows(w["ffn2_w_down"]),
        "ple_gate": w["ple_w_gate"],
    }
    return {n: a.astype(BF16) for n, a in blocks.items()}


GATHER_GROUPS = {"head": ("ffn1_gu", "ffn1_d"), "mid": ("w_in", "lora"),
                 "rest": ("br", "w_out", "ffn2_gu", "ffn2_d", "ple_gate")}

SCATTER_GROUPS = {"tail": ("ple_gate", "ple_proj", "ffn2_gu", "ffn2_d"), "branch": ("w_out", "br"),
                  "mixer": ("lora", "w_in"),
                  "ffn1_down": ("ffn1_d",), "head": ("ffn1_gu",)}

TWO_LEVEL = ("mixer", "head")

ADAM_PLAN = (
    ("ffn1_w_gate", "ffn1_gu", 0, 0, FF_PAD, 256), ("ffn1_w_up", "ffn1_gu", 0, 1, FF_PAD, 256),
    ("ffn1_w_down", "ffn1_d", 0, 0, D_MODEL, FF_SHARD // 2), ("w_in", "w_in", 0, 0, IN_SHARD, 256),
    ("rwkv_w2", "lora", 0, 0, HEAD_DIM, 64), ("rwkv_a2", "lora", 1, 0, HEAD_DIM, 64),
    ("rwkv_g2", "lora", 2, 0, HEAD_DIM, 64),
    ("w_br_rwkv", "br", 0, 0, OUT_SHARD, 256), ("w_br_attn", "br", 2, 0, OUT_SHARD, 256),
    ("ple_w_proj", "ple_proj", 0, 0, OUT_SHARD, 256), ("w_out", "w_out", 0, 0, D_MODEL, OUT_SHARD),
    ("ffn2_w_gate", "ffn2_gu", 0, 0, FF_PAD, 256), ("ffn2_w_up", "ffn2_gu", 0, 1, FF_PAD, 256),
    ("ffn2_w_down", "ffn2_d", 0, 0, D_MODEL, FF_SHARD // 2), ("ple_w_gate", "ple_gate", 0, 0, D_MODEL, OUT_SHARD),
)


def _pack_small(arrs, rows):
    flat = jnp.concatenate([a.reshape(-1) for a in arrs])
    return jnp.pad(flat, (0, rows * LANES - flat.shape[0])).reshape(rows, LANES)


def _unpack_small(flat, like):
    flat = flat.reshape(-1)
    out, off = [], 0
    for a in like:
        out.append(flat[off:off + a.size].reshape(a.shape))
        off += a.size
    return out


def _local_step(x, p, pos, target, sm, wg, fetch, on_grads):
    t = x.shape[0]

    w_d1 = wg["ffn1_d"].reshape(FF_HID, D_MODEL)
    x1, ffn1_saved, _ = _ffn_fwd("ffn1", x, sm["ffn1_norm"], wg["ffn1_gu"], lambda after: w_d1)
    fetch("forward", x1)
    h2 = _norm_fwd("mix_norm", x1, sm["mix_norm"])
    wg = {**wg, **fetch("mid", h2)}
    full_cols = lambda blk: blk.transpose(1, 0, 2).reshape(blk.shape[1], N_DEV * blk.shape[2])
    lora_w2 = full_cols(wg["lora"][:, :DECAY_LORA])
    lora_a2 = full_cols(wg["lora"][:, DECAY_LORA:DECAY_LORA + ICLR_LORA])
    lora_g2 = full_cols(wg["lora"][:, DECAY_LORA + ICLR_LORA:])
    z =_mmc_nn("w_in", h2, wg["w_in"], 0, 0, IN_SHARD)
    z_g = (z, 2 * D_MODEL, (RWKV_COLS + 3 * ATTN_DIM) // (2 * D_MODEL))

    r, k, v, lo, gd = _shift_fwd(z, sm["rwkv_mu"])
    zero_lo = jnp.zeros((DECAY_LORA, RWKV_DIM), BF16)
    w2p = jnp.concatenate([lora_w2, zero_lo], axis=0).astype(F32)
    a2p = jnp.concatenate([zero_lo, lora_a2], axis=0).astype(F32)
    pre_params = [sm["rwkv_w0"], w2p, sm["rwkv_a0"], a2p, lora_g2.astype(F32), sm["rwkv_k_k"], sm["rwkv_k_a"]]
    wide = [(RWKV_DIM, F32)]
    k2, kk, a, decay, g = _rowwise("rwkv_pre", _f_pre, [k, lo, gd], pre_params, wide * 5)
    kkn, b = _pairwise("rwkv_kk", _f_kk, [kk, a], [], 2)
    scan_in = [u.reshape(t, RWKV_HEADS, HEAD_DIM) for u in (r, decay, k2, kkn, b)]
    v_rows = _to_v_rows(v)
    y_rows, states = _wkv_fwd(*scan_in, v_rows)
    y = _from_v_rows(y_rows)
    post_params = [sm["rwkv_gn_w"], sm["rwkv_gn_b"], sm["rwkv_r_k"]]
    post_rows = [y, r, k2, v, g]
    y_rwkv = _pairwise("rwkv_post", lambda *av: (_f_post(*av),), post_rows, post_params, 1)[0]

    inv_freq = 1.0 / (ROPE_THETA ** (jnp.arange(0, HEAD_DIM, 2, dtype=F32) / HEAD_DIM))
    freq2 = jnp.tile(inv_freq, 2 * PAIR // HEAD_DIM).reshape(1, PAIR)
    half = jnp.ones((HEAD_DIM // 2,), F32)
    sign2 = jnp.tile(jnp.concatenate([-half, half]), PAIR // HEAD_DIM).reshape(1, PAIR)

    def rope_table(posv, fr, sg):
        ang = posv * fr
        return (jnp.concatenate([jnp.cos(ang), jnp.sin(ang) * sg], axis=1),)
    tab = _rowwise("rope_table", rope_table, [pos.astype(F32).reshape(t, 1)], [freq2, sign2], [(2 * PAIR, F32)])[0]
    qk = _qk_prep(z, tab, sm["q_norm"], sm["k_norm"])
    o_all, lse_all = _attn_fwd(qk, z)
    gw = HEADS_PER_GROUP * HEAD_DIM

    def by_group(ov, lv):
        return [ov[:, i * gw:(i + 1) * gw] for i in range(3)] + [lv[:, i * gw:(i + 1) * gw] for i in range(3)]
    y_attn = _rowwise("attn_comb", lambda ov, lv: (_f_comb(*by_group(ov, lv)),), [o_all, lse_all], [], [(gw, F32)])[0]

    wg = {**wg, **fetch("rest", y_rwkv)}
    w_d2 = wg["ffn2_d"].reshape(FF_HID, D_MODEL)
    w_out = wg["w_out"].reshape(D_MODEL, D_MODEL)
    w_pg = wg["ple_gate"].reshape(D_MODEL, D_MODEL)
    w_brr = full_cols(wg["br"][:, :RWKV_DIM])
    w_bra = full_cols(wg["br"][:, RWKV_DIM:RWKV_DIM + gw])
    w_pp = full_cols(wg["br"][:, RWKV_DIM + gw:])
    u_r = _mm("br_rwkv", y_rwkv, w_brr, "nn")
    u_a = _mm("br_attn", y_attn, w_bra, "nn")

    def f_merge(zgr, zga, ur, ua):
        return _sigmoid(zgr) * ur + _sigmoid(zga) * ua
    merged = _rowwise("merge", lambda zg, ur, ua: (f_merge(zg[:, :D_MODEL], zg[:, D_MODEL:], ur, ua),),
                      [z_g, u_r, u_a], [], [(D_MODEL, BF16)])[0]
    x2 = _mm("w_out", merged, w_out, "nn", res=x1)
    x3, ffn2_saved, _ = _ffn_fwd("ffn2", x2, sm["ffn2_norm"], wg["ffn2_gu"], lambda after: w_d2)

    hn = _norm_fwd("ple_norm", x3, sm["ple_norm"])
    gz = _mm("ple_gate", hn, w_pg, "nn")
    pp = _mm("ple_proj", p, w_pp, "nn")

    def f_head(x3v, gzv, ppv, tg):
        sg = _sigmoid(gzv)
        err = x3v + sg * ppv - tg
        part = 0.5 * jnp.sum(jnp.mean(err * err, axis=-1, keepdims=True))
        dx4 = err * (1.0 / D_MODEL)
        return dx4, dx4 * ppv * sg * (1.0 - sg), dx4 * sg, jnp.full((1, LANES), part, F32)
    dx4, dgz, dpp, loss_row = _rowwise("ple_loss", f_head, [x3, gz, pp, target], [],
                                       [(D_MODEL, F32), (D_MODEL, BF16), (D_MODEL, BF16)], [(1, LANES)])
    loss = loss_row[0, 0]

    gs, gm = {}, {}
    row_blocks = lambda g: g.reshape(N_DEV, g.shape[0] // N_DEV, g.shape[1])
    dhn = _mm("ple_dhn", dgz, w_pg, "nt")
    gm["ple_gate"] = row_blocks(_mm("ple_dwgate", hn, dgz, "tn", out_dtype=BF16))
    col_blocks = lambda g: g.reshape(g.shape[0], N_DEV, g.shape[1] // N_DEV).transpose(1, 0, 2)
    gm["ple_proj"] = col_blocks(_mm("ple_dwproj", p, dpp, "tn", out_dtype=BF16))
    dx3, gs["ple_norm"] = _norm_bwd("ple_dnorm", x3, sm["ple_norm"], dhn, dx4)

    dx2, gs["ffn2_norm"], gm["ffn2_gu"], gm["ffn2_d"] = _ffn_bwd(
        "ffn2", x2, sm["ffn2_norm"], wg["ffn2_gu"], w_d2, ffn2_saved, dx3)
    tail_token = on_grads("tail", {n: gm.pop(n) for n in SCATTER_GROUPS["tail"]})

    dmerged = _mm("w_out_dmerged", dx2, w_out, "nt")
    gm["w_out"] = row_blocks(_mm("w_out_dw", merged, dx2, "tn", out_dtype=BF16))

    def merge_bwd(zg, ur, ua, dm):
        _, vjp = jax.vjp(f_merge, zg[:, :D_MODEL], zg[:, D_MODEL:], ur, ua)
        dzr, dza, dur, dua = vjp(dm)
        return jnp.concatenate([dzr, dza], axis=1), dur, dua
    dz_g, du_r, du_a = _rowwise("merge_bwd", merge_bwd, [z_g, u_r, u_a, dmerged], [],
                                [(2 * D_MODEL, BF16), (D_MODEL, BF16), (D_MODEL, BF16)])
    dy_rwkv = _mm("br_rwkv_dy", du_r, w_brr, "nt")
    dy_attn = _mm("br_attn_dy", du_a, w_bra, "nt")
    gm["br"] = jnp.concatenate([col_blocks(_mm("br_rwkv_dw", y_rwkv, du_r, "tn", out_dtype=BF16)),
                                col_blocks(_mm("br_attn_dw", y_attn, du_a, "tn", out_dtype=BF16))], axis=1)
    branch_token = on_grads("branch", {n: gm.pop(n) for n in SCATTER_GROUPS["branch"]})

    def comb_bwd(ov, lv, dyv):
        _, vjp = jax.vjp(_f_comb, *by_group(ov, lv))
        d = vjp(dyv)
        return jnp.concatenate(d[:3], axis=1), jnp.concatenate(d[3:], axis=1)
    do_all, dl_all = _rowwise("attn_comb_bwd", comb_bwd, [o_all, lse_all, dy_attn], [],
                              [(ATTN_DIM, F32), (ATTN_DIM, F32)])
    dq_all, dk_all, dv_all = _attn_bwd(qk, z, do_all, dl_all)
    dqk_raw, gs["q_norm"], gs["k_norm"] = _qk_prep_bwd(z, tab, sm["q_norm"], sm["k_norm"], dq_all, dk_all)

    def post_bwd(yv, rv, k2v, vv, gv, dv_, gnw, gnb, rk):
        _, vjp = jax.vjp(_f_post, yv, rv, k2v, vv, gv, gnw, gnb, rk)
        return vjp(dv_)
    dy, dr1, dk2a, dv1, dg, d_gnw, d_gnb, d_rk = _pairwise(
        "rwkv_post_bwd", post_bwd, post_rows + [dy_rwkv], [post_params[0] + tail_token[0, 0] + branch_token[0, 0]] + post_params[1:], 5, 3)
    gs["rwkv_gn_w"], gs["rwkv_gn_b"], gs["rwkv_r_k"] = d_gnw, d_gnb, d_rk
    dr2, ddecay, dk2b, dkkn, db, dv_rows = _wkv_bwd(*scan_in, v_rows, states, _to_v_rows(dy))
    dr2, ddecay, dk2b, dkkn, db = [u.reshape(t, RWKV_DIM) for u in (dr2, ddecay, dk2b, dkkn, db)]
    dv2 = _from_v_rows(dv_rows)

    def kk_bwd(kkv, av, dkknv, dbv, dra, drb, dva, dvb):
        _, vjp = jax.vjp(_f_kk, kkv, av)
        return (*vjp((dkknv, dbv)), dra + drb, dva + dvb)
    dkk, da, dr, dv = _pairwise("rwkv_kk_bwd", kk_bwd, [kk, a, dkkn, db, dr1, dr2, dv1, dv2], [], 4)

    def pre_bwd(kv, lov, gdv, dk2x, dk2y, dkkv, dav, ddec, dgv, w0, w2p_, a0, a2p_, g2, k_k, k_a):
        _, vjp = jax.vjp(_f_pre, kv, lov, gdv, w0, w2p_, a0, a2p_, g2, k_k, k_a)
        return vjp((dk2x + dk2y, dkkv, dav, ddec, dgv))
    lora_acc = (DECAY_LORA + ICLR_LORA, RWKV_DIM)
    dk, dlo, dgd, d_w0, d_w2p, d_a0, d_a2p, d_g2, d_kk, d_ka = _rowwise(
        "rwkv_pre_bwd", pre_bwd,
        [k, lo, gd, dk2a, dk2b, dkk, da, ddecay, dg],
        pre_params, [(RWKV_DIM, F32), (LANES, F32), (LANES, F32)],
        [(1, RWKV_DIM), lora_acc, (1, RWKV_DIM), lora_acc, (GATE_LORA, RWKV_DIM), (1, RWKV_DIM), (1, RWKV_DIM)])
    gs["rwkv_w0"], gs["rwkv_a0"], gs["rwkv_k_k"], gs["rwkv_k_a"] = d_w0, d_a0, d_kk, d_ka
    gm["lora"] = jnp.concatenate([col_blocks(d_w2p[:DECAY_LORA]), col_blocks(d_a2p[DECAY_LORA:]), col_blocks(d_g2)],
                                 axis=1).astype(BF16)
    dz_r, gs["rwkv_mu"] = _shift_bwd(z, sm["rwkv_mu"], dr, dk, dv, dlo, dgd)

    dz = jnp.concatenate([dz_r, dqk_raw, dv_all.astype(BF16), dz_g], axis=1)
    dh2 = _mmc_nt("w_in_dh", dz, wg["w_in"], 0, 0, IN_SHARD, D_MODEL)
    gm["w_in"] = _mmc_tn("w_in_dw", h2, dz, IN_SHARD)
    mixer_token = on_grads("mixer", {n: gm.pop(n) for n in SCATTER_GROUPS["mixer"]})
    dx1, gs["mix_norm"] = _norm_bwd("mix_dnorm", x1, sm["mix_norm"] + mixer_token[0, 0], dh2, dx2)

    dx0, gs["ffn1_norm"], gm["ffn1_gu"], _ = _ffn_bwd(
        "ffn1", x, sm["ffn1_norm"], wg["ffn1_gu"], w_d1, ffn1_saved, dx1,
        on_down=lambda blocks: on_grads("ffn1_down", {"ffn1_d": blocks}))
    return loss, dx0, gm, gs


def kernel(x, p, positions, ffn1_norm, ffn1_w_gate, ffn1_w_up, ffn1_w_down, mix_norm, w_in, rwkv_mu, rwkv_w0, rwkv_w2, rwkv_a0, rwkv_a2, rwkv_g2, rwkv_k_k, rwkv_k_a, rwkv_r_k, rwkv_gn_w, rwkv_gn_b, q_norm, k_norm, w_br_rwkv, w_br_attn, w_out, ffn2_norm, ffn2_w_gate, ffn2_w_up, ffn2_w_down, ple_norm, ple_w_gate, ple_w_proj, loss_target, m_ffn1_norm, m_ffn1_w_gate, m_ffn1_w_up, m_ffn1_w_down, m_mix_norm, m_w_in, m_rwkv_mu, m_rwkv_w0, m_rwkv_w2, m_rwkv_a0, m_rwkv_a2, m_rwkv_g2, m_rwkv_k_k, m_rwkv_k_a, m_rwkv_r_k, m_rwkv_gn_w, m_rwkv_gn_b, m_q_norm, m_k_norm, m_w_br_rwkv, m_w_br_attn, m_w_out, m_ffn2_norm, m_ffn2_w_gate, m_ffn2_w_up, m_ffn2_w_down, m_ple_norm, m_ple_w_gate, m_ple_w_proj, v_ffn1_norm, v_ffn1_w_gate, v_ffn1_w_up, v_ffn1_w_down, v_mix_norm, v_w_in, v_rwkv_mu, v_rwkv_w0, v_rwkv_w2, v_rwkv_a0, v_rwkv_a2, v_rwkv_g2, v_rwkv_k_k, v_rwkv_k_a, v_rwkv_r_k, v_rwkv_gn_w, v_rwkv_gn_b, v_q_norm, v_k_norm, v_w_br_rwkv, v_w_br_attn, v_w_out, v_ffn2_norm, v_ffn2_w_gate, v_ffn2_w_up, v_ffn2_w_down, v_ple_norm, v_ple_w_gate, v_ple_w_proj):
    args = locals()
    w = {n: args[n][0] for n in WEIGHTS}
    m = {n: args["m_" + n][0] for n in WEIGHTS}
    v = {n: args["v_" + n][0] for n in WEIGHTS}

    w_loc = _local_blocks(w)
    head = GATHER_GROUPS["head"]
    wg = dict(zip(head, _all_gather_hbm("gather_head", [w_loc[n] for n in head])))
    me = 4 * lax.axis_index("x") + 2 * lax.axis_index("y") + lax.axis_index("c")
    gathering, order_after = {}, wg[head[0]]
    for group in ("mid", "rest"):
        shards = [w_loc[n] for n in GATHER_GROUPS[group]]
        zones = [lax.dynamic_update_slice(lax.empty((N_DEV,) + a.shape, a.dtype), a[None], (me, 0, 0)) for a in shards]
        copies = _gather2_first if group == "mid" else _gather_copies
        *gathering[group], order_after = _exchange_start("gather_start_" + group, copies, shards, zones, order_after)

    def fetch(group, after):
        if group == "forward":
            s1, r1, srcs, lands = gathering["mid"]
            ici = functools.partial(_gather2_first, ks=(1, 2, 3))
            srcs, lands = _exchange_wait("gather_arrived_mid", ici, s1, r1, srcs, lands, after, parts=("recv",))
            s2, r2, srcs, lands, _ = _exchange_start("gather_forward_mid", _gather2_second, srcs, lands, after)
            gathering["mid"] = (s1, r1, s2, r2, srcs, lands)
            return {}
        if group == "mid":
            s1, r1, s2, r2, srcs, lands = gathering["mid"]
            sib = functools.partial(_gather2_first, ks=(0,))
            srcs, lands = _exchange_wait("gather_passed_mid", _gather2_second, s2, r2, srcs, lands, after)
            srcs, lands = _exchange_wait("gather_sent_mid", _gather2_first, s1, r1, srcs, lands, after, parts=("send",))
            _, got = _exchange_wait("gather_wait_mid", sib, s1, r1, srcs, lands, after, parts=("recv",))
        else:
            _, got = _exchange_wait("gather_wait_" + group, _gather_copies, *gathering[group], after)
        return dict(zip(GATHER_GROUPS[group], got))

    sm = {n: w[n].reshape(1, -1) for n in SMALL}
    sm["ffn1_norm"] = sm["ffn1_norm"] + order_after[0, 0]
    in_flight = {}

    core = lax.axis_index("c").astype(jnp.int32).reshape(1)

    def scatter_early(group, arrays):
        arrs = [arrays[n] for n in SCATTER_GROUPS[group]]
        copies, after = _scatter_copies, arrs[0]
        if group in TWO_LEVEL:
            halves = _pair_swap("scatter_pair_" + group, arrs)
            arrs = [_pair_add("scatter_add_%s_%s" % (group, n), a, hf, core)
                    for n, a, hf in zip(SCATTER_GROUPS[group], arrs, halves)]
            copies, after = _chip_copies, halves[0]
        *in_flight[group], token = _exchange_start("scatter_start_" + group, copies, arrs,
                                                   [lax.empty(a.shape, a.dtype) for a in arrs], after)
        return token
    loss_part, dx, gm, gs = _local_step(x[0], p[0, 0], positions[0], loss_target[0], sm, wg, fetch, scatter_early)
    loss = lax.psum(loss_part, ("x", "y", "c"))
    head_token = scatter_early("head", gm)
    recv, own = {}, {}

    def arrived(group, after):
        copies = _chip_copies if group in TWO_LEVEL else _scatter_copies
        sent, lands = _exchange_wait("scatter_wait_" + group, copies, *in_flight[group], after)
        own.update(zip(SCATTER_GROUPS[group], sent))
        recv.update(zip(SCATTER_GROUPS[group], lands))
    for group in ("tail", "branch", "mixer", "ffn1_down"):
        arrived(group, head_token)
    small_like = [w[n] for n in SMALL]
    small_rows = 80
    gs_all = _all_gather_vmem(_pack_small([gs[n] for n in SMALL], small_rows))
    gs_sum = _sum_slots("sum_small_grads", gs_all, small_rows)

    res = {}
    early = [e for group in ("tail", "branch", "mixer", "ffn1_down") for e in ADAM_PLAN
             if e[1] in SCATTER_GROUPS[group]]
    late = [e for e in ADAM_PLAN if e[1] in SCATTER_GROUPS["head"]]
    order = head_token
    for n, src, rb, cb, cw, tr in early + late:
        if (n, src, rb, cb, cw, tr) == late[0]:
            arrived("head", order)
        outs4 = _adamw_slots("adamw_" + n, recv[src], own[src], rb, cb, cw, w[n], m[n], v[n], tr, order)
        order = outs4[1]
        for tag, a in zip(("grad", "delta", "new_m", "new_v"), outs4):
            res[tag, n] = a[None]
    d_s, m_s, v_s = _adamw("adamw_small", _pack_small(small_like, small_rows), gs_sum,
                           _pack_small([m[n] for n in SMALL], small_rows),
                           _pack_small([v[n] for n in SMALL], small_rows), small_rows)
    for tag, small in (("grad", gs_sum), ("delta", d_s), ("new_m", m_s), ("new_v", v_s)):
        for n, a in zip(SMALL, _unpack_small(small, small_like)):
            res[tag, n] = a[None]
    outs = [loss, dx[None]]
    for tag in ("grad", "delta", "new_m", "new_v"):
        outs += [res[tag, n] for n in WEIGHTS]
    return tuple(outs)
```

```python
import functools

import jax
import jax.numpy as jnp
from jax import lax
from jax.experimental import pallas as pl
from jax.experimental.pallas import tpu as pltpu

F32, BF16 = jnp.float32, jnp.bfloat16
MESH = pl.DeviceIdType.MESH
N_DEV = 8
LANES = 128
VMEM_LIMIT = 56 * 1024 * 1024

D_MODEL = 1024
PLE_DIM = 256
HEAD_DIM = 64
RWKV_HEADS = 8
RWKV_DIM = RWKV_HEADS * HEAD_DIM
DECAY_LORA = 64
ICLR_LORA = 64
GATE_LORA = 128
GN_EPS = 64e-5
ATTN_GROUPS = ((128, 1), (512, 4), (2048, 16))
HEADS_PER_GROUP = 4
ATTN_HEADS = HEADS_PER_GROUP * len(ATTN_GROUPS)
ATTN_DIM = ATTN_HEADS * HEAD_DIM
BAND_BLOCK = 128
ROPE_THETA = 10000.0
NEG_INF = -1e30
D_FF = 2816
RMS_EPS = 1e-6
RWKV_COLS = 3 * RWKV_DIM + DECAY_LORA + ICLR_LORA + GATE_LORA
ADAM_LR = 0.001
ADAM_B1 = 0.9
ADAM_B2 = 0.999
ADAM_EPS = 1e-08
ADAM_WD = 0.01
ADAM_STEP = 10

V_LO = LANES // RWKV_HEADS
V_HI = HEAD_DIM // V_LO
SCAN_CHUNK = 64
MM_ROWS = 2048

FF_SHARD = D_FF // N_DEV
FF_PAD = -(-FF_SHARD // LANES) * LANES
FF_HID = N_DEV * FF_PAD
IN_SHARD = 6144 // N_DEV
OUT_SHARD = D_MODEL // N_DEV

SMALL = ("ffn1_norm", "mix_norm", "rwkv_mu", "rwkv_w0", "rwkv_a0", "rwkv_k_k", "rwkv_k_a", "rwkv_r_k",
         "rwkv_gn_w", "rwkv_gn_b", "q_norm", "k_norm", "ffn2_norm", "ple_norm")
WEIGHTS = ("ffn1_norm", "ffn1_w_gate", "ffn1_w_up", "ffn1_w_down", "mix_norm", "w_in", "rwkv_mu", "rwkv_w0",
           "rwkv_w2", "rwkv_a0", "rwkv_a2", "rwkv_g2", "rwkv_k_k", "rwkv_k_a", "rwkv_r_k", "rwkv_gn_w",
           "rwkv_gn_b", "q_norm", "k_norm", "w_br_rwkv", "w_br_attn", "w_out", "ffn2_norm", "ffn2_w_gate",
           "ffn2_w_up", "ffn2_w_down", "ple_norm", "ple_w_gate", "ple_w_proj")


def _cparams(**kw):
    return pltpu.CompilerParams(vmem_limit_bytes=VMEM_LIMIT, **kw)


def _tile(n, cap):
    best = None
    for t in range(LANES, min(n, cap) + 1, LANES):
        if n % t == 0:
            best = t
    return best if best is not None else n


@jax.custom_vjp
def _bdot(a, w):
    return jnp.dot(a.astype(BF16), w.astype(BF16), preferred_element_type=F32)


def _bdot_fwd(a, w):
    return _bdot(a, w), (a, w)


def _bdot_bwd(res, g):
    a, w = res
    gb = g.astype(BF16)
    da = lax.dot_general(gb, w.astype(BF16), (((1,), (1,)), ((), ())), preferred_element_type=F32)
    dw = lax.dot_general(a.astype(BF16), gb, (((0,), (0,)), ((), ())), preferred_element_type=F32)
    return da.astype(a.dtype), dw.astype(w.dtype)


_bdot.defvjp(_bdot_fwd, _bdot_bwd)


@jax.custom_vjp
def _bdot_nt(a, b):
    return lax.dot_general(a.astype(BF16), b.astype(BF16), (((1,), (1,)), ((), ())), preferred_element_type=F32)


def _bdot_nt_fwd(a, b):
    return _bdot_nt(a, b), (a, b)


def _bdot_nt_bwd(res, g):
    a, b = res
    gb = g.astype(BF16)
    da = jnp.dot(gb, b.astype(BF16), preferred_element_type=F32)
    db = lax.dot_general(gb, a.astype(BF16), (((0,), (0,)), ((), ())), preferred_element_type=F32)
    return da.astype(a.dtype), db.astype(b.dtype)


_bdot_nt.defvjp(_bdot_nt_fwd, _bdot_nt_bwd)


def _mm(name, a, b, mode, out_dtype=F32, res=None, scale=None):
    if mode == "nn":
        (m, k), n = a.shape, b.shape[1]
    elif mode == "nt":
        (m, k), n = a.shape, b.shape[0]
    else:
        (k, m), n = a.shape, b.shape[1]
    tm, tn = _tile(m, 1024), _tile(n, 512)
    a_spec = pl.BlockSpec((k, tm), lambda i, j: (0, i)) if mode == "tn" else pl.BlockSpec((tm, k), lambda i, j: (i, 0))
    b_spec = pl.BlockSpec((tn, k), lambda i, j: (j, 0)) if mode == "nt" else pl.BlockSpec((k, tn), lambda i, j: (0, j))
    dims = {"nn": ((1,), (0,)), "nt": ((1,), (1,)), "tn": ((0,), (0,))}[mode]
    o_spec = pl.BlockSpec((tm, tn), lambda i, j: (i, j))
    ins, in_specs = [a, b], [a_spec, b_spec]
    if res is not None:
        ins.append(res)
        in_specs.append(o_spec)

    def body(*refs):
        acc = lax.dot_general(refs[0][...].astype(BF16), refs[1][...].astype(BF16), (dims, ((), ())),
                              preferred_element_type=F32)
        if scale is not None:
            acc = acc * scale
        if res is not None:
            acc = acc + refs[2][...].astype(F32)
        refs[-1][...] = acc.astype(refs[-1].dtype)

    return pl.pallas_call(
        body, name=name, grid=(m // tm, n // tn), in_specs=in_specs, out_specs=o_spec,
        out_shape=jax.ShapeDtypeStruct((m, n), out_dtype),
        compiler_params=_cparams(dimension_semantics=("parallel", "parallel")),
    )(*ins)


def _mmc_nn(name, a, wb, ki, ci, n, out_dtype=F32):
    m, k = a.shape
    tm = _tile(m, MM_ROWS)

    def body(a_ref, w_ref, o_ref):
        o_ref[...] = jnp.dot(a_ref[...].astype(BF16), w_ref[...], preferred_element_type=F32).astype(o_ref.dtype)

    return pl.pallas_call(
        body, name=name, grid=(m // tm, N_DEV),
        in_specs=[pl.BlockSpec((tm, k), lambda i, j: (i, 0)), pl.BlockSpec((None, k, n), lambda i, j: (j, ki, ci))],
        out_specs=pl.BlockSpec((tm, n), lambda i, j: (i, j)),
        out_shape=jax.ShapeDtypeStruct((m, N_DEV * n), out_dtype),
        compiler_params=_cparams(dimension_semantics=("parallel", "parallel")),
    )(a, wb)


def _mmc_nt(name, a, wb, ki, ci, n, k, res=None):
    m = a.shape[0]
    tm = _tile(m, MM_ROWS)
    o_spec = pl.BlockSpec((tm, k), lambda i, j: (i, 0))
    ins = [a, wb] + ([res] if res is not None else [])
    in_specs = [pl.BlockSpec((tm, n), lambda i, j: (i, j)), pl.BlockSpec((None, k, n), lambda i, j: (j, ki, ci))]
    in_specs += [o_spec] if res is not None else []

    def body(*refs):
        a_ref, w_ref, o_ref = refs[0], refs[1], refs[-1]
        acc = lax.dot_general(a_ref[...].astype(BF16), w_ref[...], (((1,), (1,)), ((), ())),
                              preferred_element_type=F32)

        @pl.when(pl.program_id(1) == 0)
        def _():
            o_ref[...] = acc + refs[2][...] if res is not None else acc

        @pl.when(pl.program_id(1) != 0)
        def _():
            o_ref[...] += acc

    return pl.pallas_call(
        body, name=name, grid=(m // tm, N_DEV), in_specs=in_specs, out_specs=o_spec,
        out_shape=jax.ShapeDtypeStruct((m, k), F32),
        compiler_params=_cparams(dimension_semantics=("parallel", "arbitrary")),
    )(*ins)


def _mmc_tn(name, x, dy, n):
    m, k = x.shape
    tk = _tile(k, 1024)

    def body(x_ref, dy_ref, o_ref):
        o_ref[...] = lax.dot_general(x_ref[...].astype(BF16), dy_ref[...].astype(BF16), (((0,), (0,)), ((), ())),
                                     preferred_element_type=F32).astype(o_ref.dtype)

    return pl.pallas_call(
        body, name=name, grid=(N_DEV, k // tk),
        in_specs=[pl.BlockSpec((m, tk), lambda j, i: (0, i)), pl.BlockSpec((m, n), lambda j, i: (0, j))],
        out_specs=pl.BlockSpec((None, tk, n), lambda j, i: (j, i, 0)),
        out_shape=jax.ShapeDtypeStruct((N_DEV, k, n), BF16),
        compiler_params=_cparams(dimension_semantics=("parallel", "parallel")),
    )(x, dy)


def _rowwise(name, fn, rows, params, out_rows, out_accs=(), tr=512):
    rows = [a if isinstance(a, tuple) else (a, a.shape[1], 0) for a in rows]
    r = rows[0][0].shape[0]
    in_specs = [pl.BlockSpec((tr, wd), lambda i, cb=cb: (i, cb)) for _, wd, cb in rows]
    rows = [a for a, _, _ in rows]
    in_specs += [pl.BlockSpec(p.shape, lambda i, nd=p.ndim: (0,) * nd) for p in params]
    out_shape = [jax.ShapeDtypeStruct((r, c), dt) for c, dt in out_rows]
    out_shape += [jax.ShapeDtypeStruct(s, F32) for s in out_accs]
    out_specs = [pl.BlockSpec((tr, c), lambda i: (i, 0)) for c, _ in out_rows]
    out_specs += [pl.BlockSpec(s, lambda i, nd=len(s): (0,) * nd) for s in out_accs]
    n_in, n_ro = len(rows) + len(params), len(out_rows)

    def body(*refs):
        res = fn(*[ref[...] for ref in refs[:n_in]])
        outs = refs[n_in:]
        for o, v in zip(outs[:n_ro], res[:n_ro]):
            o[...] = v.astype(o.dtype)
        for o, v in zip(outs[n_ro:], res[n_ro:]):
            _accumulate(o, v)

    return pl.pallas_call(
        body, name=name, grid=(r // tr,), in_specs=in_specs, out_specs=out_specs, out_shape=out_shape,
        compiler_params=_cparams(dimension_semantics=("arbitrary",)),
    )(*rows, *params)


def _pairwise(name, fn, rows, params, n_out, n_acc=0, tr=512):
    t, c = rows[0].shape
    tile = pl.BlockSpec((tr, 2 * HEAD_DIM), lambda p, i: (i, p))
    vec = pl.BlockSpec((1, 2 * HEAD_DIM), lambda p, i: (0, p))
    n_in = len(rows) + len(params)

    def body(*refs):
        res = fn(*[ref[...] for ref in refs[:n_in]])
        outs = refs[n_in:]
        for o, v in zip(outs[:n_out], res[:n_out]):
            o[...] = v
        first = pl.program_id(1) == 0
        for o, v in zip(outs[n_out:], res[n_out:]):
            @pl.when(first)
            def _(o=o, v=v):
                o[...] = v

            @pl.when(jnp.logical_not(first))
            def _(o=o, v=v):
                o[...] += v

    return pl.pallas_call(
        body, name=name, grid=(c // (2 * HEAD_DIM), t // tr),
        in_specs=[tile] * len(rows) + [vec] * len(params), out_specs=[tile] * n_out + [vec] * n_acc,
        out_shape=[jax.ShapeDtypeStruct((t, c), F32)] * n_out + [jax.ShapeDtypeStruct((1, c), F32)] * n_acc,
        compiler_params=_cparams(dimension_semantics=("parallel", "arbitrary")),
    )(*rows, *params)


def _accumulate(o_ref, v):
    @pl.when(pl.program_id(0) == 0)
    def _():
        o_ref[...] = v

    @pl.when(pl.program_id(0) != 0)
    def _():
        o_ref[...] += v


def _rms(x, g):
    return x * lax.rsqrt(jnp.mean(x * x, axis=-1, keepdims=True) + RMS_EPS) * g


def _sigmoid(x):
    return jax.nn.sigmoid(x)


def _softplus(x):
    return jnp.maximum(x, 0.0) + jnp.log1p(jnp.exp(-jnp.abs(x)))


def _norm_fwd(name, x, g):
    return _rowwise(name, lambda xv, gv: (_rms(xv, gv),), [x], [g], [(x.shape[1], BF16)])[0]


def _norm_bwd(name, x, g, dh, dres):
    def fn(xv, dhv, drv, gv):
        _, vjp = jax.vjp(_rms, xv, gv)
        dx, dg = vjp(dhv)
        return dx + drv, dg
    return _rowwise(name, fn, [x, dh, dres], [g], [(x.shape[1], F32)], [g.shape])


def _f_act(gate, up):
    return gate * _sigmoid(gate) * up


def _gate_up_act(name, h, w_gu):
    m, k = h.shape
    tm = _tile(m, MM_ROWS)

    def body(h_ref, w_ref, gu_ref, a_ref):
        gu = jnp.dot(h_ref[...], w_ref[...], preferred_element_type=F32)
        gu_ref[...] = gu
        a_ref[...] = _f_act(gu[:, :FF_PAD], gu[:, FF_PAD:]).astype(a_ref.dtype)

    return pl.pallas_call(
        body, name=name, grid=(m // tm, N_DEV),
        in_specs=[pl.BlockSpec((tm, k), lambda i, j: (i, 0)),
                  pl.BlockSpec((None, k, 2 * FF_PAD), lambda i, j: (j, 0, 0))],
        out_specs=[pl.BlockSpec((tm, 2 * FF_PAD), lambda i, j: (i, j)), pl.BlockSpec((tm, FF_PAD), lambda i, j: (i, j))],
        out_shape=[jax.ShapeDtypeStruct((m, N_DEV * 2 * FF_PAD), F32), jax.ShapeDtypeStruct((m, FF_HID), BF16)],
        compiler_params=_cparams(dimension_semantics=("parallel", "parallel")),
    )(h, w_gu)


def _gate_up_act_bwd(name, dout, w_down, gu, order):
    m, k = dout.shape
    tm = _tile(m, MM_ROWS)

    def body(d_ref, w_ref, gu_ref, order_ref, o_ref):
        da = 0.5 * lax.dot_general(d_ref[...].astype(BF16), w_ref[...], (((1,), (1,)), ((), ())),
                                   preferred_element_type=F32)
        guv = gu_ref[...]
        _, vjp = jax.vjp(_f_act, guv[:, :FF_PAD], guv[:, FF_PAD:])
        o_ref[...] = jnp.concatenate(vjp(da), axis=1).astype(o_ref.dtype)

    gu_spec = pl.BlockSpec((tm, 2 * FF_PAD), lambda i, j: (i, j))
    return pl.pallas_call(
        body, name=name, grid=(m // tm, N_DEV),
        in_specs=[pl.BlockSpec((tm, k), lambda i, j: (i, 0)), pl.BlockSpec((FF_PAD, k), lambda i, j: (j, 0)), gu_spec,
                  pl.BlockSpec(memory_space=pl.ANY)],
        out_specs=gu_spec, out_shape=jax.ShapeDtypeStruct((m, N_DEV * 2 * FF_PAD), BF16),
        compiler_params=_cparams(dimension_semantics=("parallel", "parallel")),
    )(dout, w_down, gu, order)


def _ffn_fwd(tag, x, norm, w_gu, w_down):
    h = _norm_fwd(tag + "_norm", x, norm)
    gu, a = _gate_up_act(tag + "_gu", h, w_gu)
    wd = w_down(a)
    out = _mm(tag + "_down", a, wd, "nn", res=x, scale=0.5)
    return out, (h, gu, a), wd


def _ffn_bwd(tag, x, norm, w_gu, w_down, saved, dout, on_down=None):
    h, gu, a = saved
    d_wdown = _mm(tag + "_dwdown", a, dout, "tn", out_dtype=BF16, scale=0.5).reshape(N_DEV, FF_PAD, D_MODEL)
    token = on_down(d_wdown) if on_down is not None else jnp.zeros((8, LANES), F32)
    dgu = _gate_up_act_bwd(tag + "_dgu", dout, w_down, gu, token)
    dh =_mmc_nt(tag + "_dh", dgu, w_gu, 0, 0, 2 * FF_PAD, D_MODEL)
    d_wgu = _mmc_tn(tag + "_dwgu", h, dgu, 2 * FF_PAD)
    dx, dnorm = _norm_bwd(tag + "_dnorm", x, norm, dh, dout)
    return dx, dnorm, d_wgu, d_wdown


def _shift_fwd(z, mu):
    t, c = z.shape[0], RWKV_COLS
    tr = 256

    def body(z_ref, zp_ref, mu_ref, r_ref, k_ref, v_ref, lo_ref, gd_ref):
        zv = z_ref[...]
        prev = zp_ref[7:8, :] * jnp.where(pl.program_id(0) == 0, 0.0, 1.0)
        row = lax.broadcasted_iota(jnp.int32, zv.shape, 0)
        zsh = jnp.where(row == 0, prev, pltpu.roll(zv, 1, 0))
        zs = zv + (zsh - zv) * mu_ref[...]
        r_ref[...] = zs[:, 0:512]
        k_ref[...] = zs[:, 512:1024]
        v_ref[...] = zs[:, 1024:1536]
        lo_ref[...] = zs[:, 1536:1664]
        gd_ref[...] = zs[:, 1664:1792]

    widths = (512, 512, 512, 128, 128)
    return pl.pallas_call(
        body, name="rwkv_shift", grid=(t // tr,),
        in_specs=[pl.BlockSpec((tr, c), lambda i: (i, 0)),
                  pl.BlockSpec((8, c), lambda i: (jnp.maximum(i * (tr // 8) - 1, 0), 0)),
                  pl.BlockSpec((1, c), lambda i: (0, 0))],
        out_specs=[pl.BlockSpec((tr, w), lambda i: (i, 0)) for w in widths],
        out_shape=[jax.ShapeDtypeStruct((t, w), F32) for w in widths],
        compiler_params=_cparams(dimension_semantics=("parallel",)),
    )(z, z, mu)


def _shift_bwd(z, mu, dr, dk, dv, dlo, dgd):
    t, c = z.shape[0], RWKV_COLS
    tr = 256
    nt = t // tr

    def body(z_ref, zp_ref, mu_ref, dr_ref, dk_ref, dv_ref, dlo_ref, dgd_ref,
             drn_ref, dkn_ref, dvn_ref, dlon_ref, dgdn_ref, dz_ref, dmu_ref):
        i = pl.program_id(0)
        zv, muv = z_ref[...], mu_ref[...]
        prev = zp_ref[7:8, :] * jnp.where(i == 0, 0.0, 1.0)
        row = lax.broadcasted_iota(jnp.int32, zv.shape, 0)
        zsh = jnp.where(row == 0, prev, pltpu.roll(zv, 1, 0))
        dzs = jnp.concatenate([dr_ref[...], dk_ref[...], dv_ref[...], dlo_ref[...], dgd_ref[...]], axis=1)
        nxt = jnp.concatenate([drn_ref[0:1, :], dkn_ref[0:1, :], dvn_ref[0:1, :], dlon_ref[0:1, :],
                               dgdn_ref[0:1, :]], axis=1) * jnp.where(i == nt - 1, 0.0, 1.0)
        u = dzs * muv
        un = jnp.where(row == tr - 1, nxt * muv, pltpu.roll(u, tr - 1, 0))
        dz_ref[...] = (dzs - u + un).astype(dz_ref.dtype)
        _accumulate(dmu_ref, jnp.sum(dzs * (zsh - zv), axis=0, keepdims=True))

    widths = (512, 512, 512, 128, 128)
    nxt_map = lambda i: (jnp.minimum((i + 1) * (tr // 8), t // 8 - 1), 0)
    return pl.pallas_call(
        body, name="rwkv_shift_bwd", grid=(nt,),
        in_specs=[pl.BlockSpec((tr, c), lambda i: (i, 0)),
                  pl.BlockSpec((8, c), lambda i: (jnp.maximum(i * (tr // 8) - 1, 0), 0)),
                  pl.BlockSpec((1, c), lambda i: (0, 0))]
        + [pl.BlockSpec((tr, w), lambda i: (i, 0)) for w in widths]
        + [pl.BlockSpec((8, w), nxt_map) for w in widths],
        out_specs=[pl.BlockSpec((tr, c), lambda i: (i, 0)), pl.BlockSpec((1, c), lambda i: (0, 0))],
        out_shape=[jax.ShapeDtypeStruct((t, c), BF16), jax.ShapeDtypeStruct((1, c), F32)],
        compiler_params=_cparams(dimension_semantics=("arbitrary",)),
    )(z, z, mu, dr, dk, dv, dlo, dgd, dr, dk, dv, dlo, dgd)


def _f_pre(k, lo, gd, w0, w2p, a0, a2p, g2, k_k, k_a):
    lane = lax.broadcasted_iota(jnp.int32, lo.shape, 1)
    lo_act = jnp.where(lane < DECAY_LORA, jnp.tanh(lo), lo)
    w = -_softplus(-(w0 + _bdot(lo_act, w2p))) - 0.5
    a = _sigmoid(a0 + _bdot(lo_act, a2p))
    g = _bdot(_sigmoid(gd), g2)
    kk = k * k_k
    k2 = k * (1.0 + (a - 1.0) * k_a)
    decay = jnp.exp(-jnp.exp(w))
    return k2, kk, a, decay, g


def _f_kk(kk, a):
    kkn = kk * lax.rsqrt(jnp.maximum(_head_sums(kk * kk), 1e-24))
    return kkn, kkn * a


def _f_post(y, r, k2, v, g, gn_w, gn_b, r_k):
    mean = _head_sums(y) * (1.0 / HEAD_DIM)
    var = _head_sums(jnp.square(y - mean)) * (1.0 / HEAD_DIM)
    yn = (y - mean) * lax.rsqrt(var + GN_EPS) * gn_w + gn_b
    bonus = _head_sums(r * k2 * r_k) * v
    return (yn + bonus) * g


def _to_v_rows(x):
    t = x.shape[0]
    return x.reshape(t, RWKV_HEADS, V_HI, V_LO).transpose(0, 2, 3, 1).reshape(t, V_HI, LANES)


def _from_v_rows(x):
    t = x.shape[0]
    return x.reshape(t, V_HI, V_LO, RWKV_HEADS).transpose(0, 3, 1, 2).reshape(t, RWKV_DIM)


def _k_cols(x):
    return jnp.tile(x, (V_LO, 1)).T


def _k_rows(x):
    xt = x.T
    out = xt[0:RWKV_HEADS]
    for l in range(1, V_LO):
        out = out + xt[l * RWKV_HEADS:(l + 1) * RWKV_HEADS]
    return out


def _wkv_fwd(r, w, k, kk, b, v):
    t = r.shape[0]
    tc = SCAN_CHUNK
    key_spec = pl.BlockSpec((tc, RWKV_HEADS, HEAD_DIM), lambda i: (i, 0, 0))
    row_spec = pl.BlockSpec((tc, V_HI, LANES), lambda i: (i, 0, 0))

    def body(r_ref, w_ref, k_ref, kk_ref, b_ref, v_ref, y_ref, st_ref, s_scr, cols_a, cols_b):
        @pl.when(pl.program_id(0) == 0)
        def _():
            s_scr[...] = jnp.zeros_like(s_scr)

        def prep(ti, buf):
            for n, ref in enumerate((r_ref, w_ref, k_ref, kk_ref, b_ref)):
                buf[n] = _k_cols(ref[ti])

        def step(ti, s, cur, nxt, ti_next):
            rc, wc, kc, kkc, bc = (cur[n] for n in range(5))
            prep(ti_next, nxt)
            vt = v_ref[ti]
            new, ys = [], []
            for j in range(V_HI):
                sa = -jnp.sum(s[j] * kkc, axis=0, keepdims=True)
                nj = s[j] * wc + bc * sa + kc * vt[j:j + 1]
                st_ref[ti, j] = nj
                ys.append(jnp.sum(nj * rc, axis=0, keepdims=True))
                new.append(nj)
            y_ref[ti] = jnp.concatenate(ys, axis=0)
            return tuple(new)

        def pair(i, s):
            s = step(2 * i, s, cols_a, cols_b, 2 * i + 1)
            return step(2 * i + 1, s, cols_b, cols_a, jnp.minimum(2 * i + 2, tc - 1))

        prep(0, cols_a)
        s = lax.fori_loop(0, tc // 2, pair, tuple(s_scr[j] for j in range(V_HI)))
        for j in range(V_HI):
            s_scr[j] = s[j]

    return pl.pallas_call(
        body, name="wkv_fwd", grid=(t // tc,),
        in_specs=[key_spec] * 5 + [row_spec],
        out_specs=[row_spec, pl.BlockSpec((tc, V_HI, HEAD_DIM, LANES), lambda i: (i, 0, 0, 0))],
        out_shape=[jax.ShapeDtypeStruct((t, V_HI, LANES), F32),
                   jax.ShapeDtypeStruct((t, V_HI, HEAD_DIM, LANES), F32)],
        scratch_shapes=[pltpu.VMEM((V_HI, HEAD_DIM, LANES), F32)] + [pltpu.VMEM((5, HEAD_DIM, LANES), F32)] * 2,
        compiler_params=_cparams(dimension_semantics=("arbitrary",)),
    )(r, w, k, kk, b, v)


def _wkv_bwd(r, w, k, kk, b, v, states, dy):
    t = r.shape[0]
    tc = SCAN_CHUNK
    nb = t // tc
    key_spec = pl.BlockSpec((tc, RWKV_HEADS, HEAD_DIM), lambda i: (nb - 1 - i, 0, 0))
    row_spec = pl.BlockSpec((tc, V_HI, LANES), lambda i: (nb - 1 - i, 0, 0))
    st_spec = pl.BlockSpec((tc, V_HI, HEAD_DIM, LANES), lambda i: (nb - 1 - i, 0, 0, 0))
    stp_spec = pl.BlockSpec((1, V_HI, HEAD_DIM, LANES), lambda i: (jnp.maximum((nb - 1 - i) * tc - 1, 0), 0, 0, 0))

    def body(r_ref, w_ref, k_ref, kk_ref, b_ref, v_ref, st_ref, stp_ref, dy_ref,
             dr_ref, dw_ref, dk_ref, dkk_ref, db_ref, dv_ref, ds_scr, cols_a, cols_b, accs_a, accs_b):
        @pl.when(pl.program_id(0) == 0)
        def _():
            ds_scr[...] = jnp.zeros_like(ds_scr)

        def colsum(x):
            return jnp.sum(x, axis=0, keepdims=True)

        def prep(ti, buf):
            for n, ref in enumerate((r_ref, w_ref, k_ref, kk_ref, b_ref)):
                buf[n] = _k_cols(ref[ti])

        def flush(ti, buf):
            for n, ref in enumerate((dr_ref, dk_ref, db_ref, dw_ref, dkk_ref)):
                ref[ti] = _k_rows(buf[n])

        def step(ti, ds, sp, cur, accs):
            rc, wc, kc, kkc, bc = (cur[n] for n in range(5))
            vt, dyt = v_ref[ti], dy_ref[ti]
            acc = None
            new, dvs = [], []
            for j in range(V_HI):
                st = st_ref[ti, j]
                dsj = ds[j] + rc * dyt[j:j + 1]
                sa = -colsum(sp[j] * kkc)
                dsa = colsum(dsj * bc)
                dvs.append(colsum(dsj * kc))
                parts = (st * dyt[j:j + 1], dsj * vt[j:j + 1], dsj * sa, dsj * sp[j], -(sp[j] * dsa))
                acc = parts if acc is None else tuple(a + q for a, q in zip(acc, parts))
                new.append(dsj * wc - kkc * dsa)
            dv_ref[ti] = jnp.concatenate(dvs, axis=0)
            for n in range(5):
                accs[n] = acc[n]
            return tuple(new)

        def states_before(ti):
            return tuple(st_ref[ti - 1, j] for j in range(V_HI))

        def pair(i, ds):
            ta = tc - 1 - 2 * i
            prep(ta - 1, cols_b)
            flush(jnp.minimum(ta + 1, tc - 1), accs_b)
            ds = step(ta, ds, states_before(ta), cols_a, accs_a)
            prep(ta - 2, cols_a)
            flush(ta, accs_a)
            return step(ta - 1, ds, states_before(ta - 1), cols_b, accs_b)

        prep(tc - 1, cols_a)
        accs_b[...] = jnp.zeros_like(accs_b)
        ds = lax.fori_loop(0, tc // 2 - 1, pair, tuple(ds_scr[j] for j in range(V_HI)))
        prep(0, cols_b)
        flush(2, accs_b)
        ds = step(1, ds, states_before(1), cols_a, accs_a)
        flush(1, accs_a)
        keep = jnp.where(pl.program_id(0) == nb - 1, 0.0, 1.0)
        ds = step(0, ds, tuple(stp_ref[0, j] * keep for j in range(V_HI)), cols_b, accs_b)
        flush(0, accs_b)
        for j in range(V_HI):
            ds_scr[j] = ds[j]

    key_out = jax.ShapeDtypeStruct((t, RWKV_HEADS, HEAD_DIM), F32)
    return pl.pallas_call(
        body, name="wkv_bwd", grid=(nb,),
        in_specs=[key_spec] * 5 + [row_spec, st_spec, stp_spec, row_spec],
        out_specs=[key_spec] * 5 + [row_spec],
        out_shape=[key_out] * 5 + [jax.ShapeDtypeStruct((t, V_HI, LANES), F32)],
        scratch_shapes=[pltpu.VMEM((V_HI, HEAD_DIM, LANES), F32)] + [pltpu.VMEM((5, HEAD_DIM, LANES), F32)] * 4,
        compiler_params=_cparams(dimension_semantics=("arbitrary",)),
    )(r, w, k, kk, b, v, states, states, dy)


PAIR = 2 * HEAD_DIM
N_PAIRS = ATTN_HEADS // 2
Q_COL0 = RWKV_COLS // PAIR
K_COL0 = Q_COL0 + N_PAIRS
V_COL0 = K_COL0 + N_PAIRS


def _swap_halves(x):
    lane = lax.broadcasted_iota(jnp.int32, x.shape, 1)
    return jnp.where((lane & (HEAD_DIM - 1)) < HEAD_DIM // 2, pltpu.roll(x, PAIR - HEAD_DIM // 2, 1),
                     pltpu.roll(x, HEAD_DIM // 2, 1))


@jax.custom_vjp
def _rope(x, cosf, sinf):
    return x * cosf + _swap_halves(x) * sinf


def _rope_fwd(x, cosf, sinf):
    return _rope(x, cosf, sinf), (cosf, sinf)


def _rope_bwd(res, d):
    cosf, sinf = res
    return d * cosf + _swap_halves(d * sinf), jnp.zeros_like(cosf), jnp.zeros_like(sinf)


_rope.defvjp(_rope_fwd, _rope_bwd)


def _head_sums(x):
    lane = lax.broadcasted_iota(jnp.int32, x.shape, 1)
    lo = jnp.where(lane < HEAD_DIM, 1.0, 0.0)
    hi = 1.0 - lo
    return lo * jnp.sum(x * lo, axis=1, keepdims=True) + hi * jnp.sum(x * hi, axis=1, keepdims=True)


def _f_qk(x, cosf, sinf, gain2):
    xn = x * lax.rsqrt(_head_sums(x * x) * (1.0 / HEAD_DIM) + RMS_EPS) * gain2
    return _rope(xn, cosf, sinf)


def _qk_prep(z, tab, q_gain, k_gain):
    t = z.shape[0]
    tr = 1024

    def body(z_ref, c_ref, s_ref, qg_ref, kg_ref, o_ref):
        g = jnp.where(pl.program_id(0) < N_PAIRS, qg_ref[...], kg_ref[...])
        o_ref[...] = _f_qk(z_ref[...], c_ref[...], s_ref[...], jnp.concatenate([g, g], axis=1))

    gain = pl.BlockSpec((1, HEAD_DIM), lambda c, i: (0, 0))
    return pl.pallas_call(
        body, name="qk_prep", grid=(2 * N_PAIRS, t // tr),
        in_specs=[pl.BlockSpec((tr, PAIR), lambda c, i: (i, Q_COL0 + c)), pl.BlockSpec((tr, PAIR), lambda c, i: (i, 0)),
                  pl.BlockSpec((tr, PAIR), lambda c, i: (i, 1)), gain, gain],
        out_specs=pl.BlockSpec((tr, PAIR), lambda c, i: (i, c)),
        out_shape=jax.ShapeDtypeStruct((t, 2 * N_PAIRS * PAIR), F32),
        compiler_params=_cparams(dimension_semantics=("parallel", "parallel")),
    )(z, tab, tab, q_gain, k_gain)


def _qk_prep_bwd(z, tab, q_gain, k_gain, dq, dk):
    t = z.shape[0]
    tr = 1024

    def body(z_ref, c_ref, s_ref, qg_ref, kg_ref, dq_ref, dk_ref, dz_ref, dqg_ref, dkg_ref):
        c, i = pl.program_id(0), pl.program_id(1)
        is_q = c < N_PAIRS
        g = jnp.where(is_q, qg_ref[...], kg_ref[...])
        d = jnp.where(is_q, dq_ref[...], dk_ref[...])
        _, vjp = jax.vjp(lambda xx, gg: _f_qk(xx, c_ref[...], s_ref[...], gg), z_ref[...],
                         jnp.concatenate([g, g], axis=1))
        dx, dg2 = vjp(d)
        dz_ref[...] = dx.astype(dz_ref.dtype)
        dg = dg2[:, :HEAD_DIM] + dg2[:, HEAD_DIM:]
        first_q = jnp.logical_and(c == 0, i == 0)
        first_k = jnp.logical_and(c == N_PAIRS, i == 0)

        @pl.when(first_q)
        def _():
            dqg_ref[...] = dg

        @pl.when(jnp.logical_and(is_q, jnp.logical_not(first_q)))
        def _():
            dqg_ref[...] += dg

        @pl.when(first_k)
        def _():
            dkg_ref[...] = dg

        @pl.when(jnp.logical_and(jnp.logical_not(is_q), jnp.logical_not(first_k)))
        def _():
            dkg_ref[...] += dg

    gain = pl.BlockSpec((1, HEAD_DIM), lambda c, i: (0, 0))
    return pl.pallas_call(
        body, name="qk_prep_bwd", grid=(2 * N_PAIRS, t // tr),
        in_specs=[pl.BlockSpec((tr, PAIR), lambda c, i: (i, Q_COL0 + c)), pl.BlockSpec((tr, PAIR), lambda c, i: (i, 0)),
                  pl.BlockSpec((tr, PAIR), lambda c, i: (i, 1)), gain, gain,
                  pl.BlockSpec((tr, PAIR), lambda c, i: (i, jnp.minimum(c, N_PAIRS - 1))),
                  pl.BlockSpec((tr, PAIR), lambda c, i: (i, jnp.maximum(c - N_PAIRS, 0)))],
        out_specs=[pl.BlockSpec((tr, PAIR), lambda c, i: (i, c)), gain, gain],
        out_shape=[jax.ShapeDtypeStruct((t, 2 * N_PAIRS * PAIR), BF16), jax.ShapeDtypeStruct((1, HEAD_DIM), F32),
                   jax.ShapeDtypeStruct((1, HEAD_DIM), F32)],
        compiler_params=_cparams(dimension_semantics=("arbitrary", "arbitrary")),
    )(z, tab, tab, q_gain, k_gain, dq, dk)


def _attn_block(q, kp, kc, vp, vc, kmin):
    k2 = jnp.concatenate([kp, kc], axis=0)
    v2 = jnp.concatenate([vp, vc], axis=0)
    s = _bdot_nt(q, k2) * (HEAD_DIM ** -0.5)
    qi = lax.broadcasted_iota(jnp.int32, s.shape, 0)
    kj = lax.broadcasted_iota(jnp.int32, s.shape, 1)
    dist = qi + BAND_BLOCK - kj
    valid = (dist >= 0) & (dist <= BAND_BLOCK) & (kj >= kmin)
    s = jnp.where(valid, s, NEG_INF)
    m = lax.stop_gradient(jnp.max(s, axis=-1, keepdims=True))
    e = jnp.exp(s - m)
    l = jnp.sum(e, axis=-1, keepdims=True)
    o = _bdot(e, v2) / l
    return o, m + jnp.log(l)


def _fold(src_ref, dst_ref, dil):
    t = src_ref.shape[0]
    ln = t // dil
    for j in range(dil):
        dst_ref[j * ln:(j + 1) * ln, :] = src_ref[pl.ds(j, ln, stride=dil), :]


def _unfold(src_ref, dst_ref, dil):
    t = src_ref.shape[0]
    ln = t // dil
    for j in range(dil):
        dst_ref[pl.ds(j, ln, stride=dil), :] = src_ref[j * ln:(j + 1) * ln, :]


def _per_group(fn):
    pair = pl.program_id(0)
    for gi, (_, dil) in enumerate(ATTN_GROUPS):
        @pl.when(jnp.logical_or(pair == 2 * gi, pair == 2 * gi + 1))
        def _(dil=dil):
            fn(dil)


def _block_rows(idx, blocks_per_seq):
    first = (idx & (blocks_per_seq - 1)) == 0
    cur = pl.ds(pl.multiple_of(idx * BAND_BLOCK, BAND_BLOCK), BAND_BLOCK)
    prev = pl.ds(pl.multiple_of(jnp.maximum(idx - 1, 0) * BAND_BLOCK, BAND_BLOCK), BAND_BLOCK)
    return first, cur, prev


def _heads(x):
    return x[:, :HEAD_DIM], x[:, HEAD_DIM:]


def _attn_fwd(qk, z):
    t = z.shape[0]
    n_blocks = t // BAND_BLOCK

    def body(q_ref, k_ref, v_ref, o_ref, lse_ref, qf, kf, vf, of, lf):
        def run(dil):
            _fold(q_ref, qf, dil)
            _fold(k_ref, kf, dil)
            _fold(v_ref, vf, dil)
            blocks_per_seq = n_blocks // dil

            def block(idx, carry):
                first, cur, prev = _block_rows(idx, blocks_per_seq)
                kmin = jnp.where(first, BAND_BLOCK, 0)
                outs, lses = [], []
                for q, kp, kc, vp, vc in zip(_heads(qf[cur, :]), _heads(kf[prev, :]), _heads(kf[cur, :]),
                                             _heads(vf[prev, :]), _heads(vf[cur, :])):
                    o, ls = _attn_block(q, kp, kc, vp, vc, kmin)
                    outs.append(o)
                    lses.append(jnp.broadcast_to(ls, o.shape))
                of[cur, :] = jnp.concatenate(outs, axis=1)
                lf[cur, :] = jnp.concatenate(lses, axis=1)
                return carry

            lax.fori_loop(0, n_blocks, block, 0, unroll=4)
            _unfold(of, o_ref, dil)
            _unfold(lf, lse_ref, dil)

        _per_group(run)

    slab = jax.ShapeDtypeStruct((t, N_PAIRS * PAIR), F32)
    out_spec = pl.BlockSpec((t, PAIR), lambda p: (0, p))
    return pl.pallas_call(
        body, name="attn_fwd", grid=(N_PAIRS,),
        in_specs=[pl.BlockSpec((t, PAIR), lambda p: (0, p)), pl.BlockSpec((t, PAIR), lambda p: (0, N_PAIRS + p)),
                  pl.BlockSpec((t, PAIR), lambda p: (0, V_COL0 + p))],
        out_specs=[out_spec, out_spec], out_shape=[slab, slab],
        scratch_shapes=[pltpu.VMEM((t, PAIR), F32)] * 5,
        compiler_params=_cparams(dimension_semantics=("parallel",)),
    )(qk, qk, z)


def _attn_bwd(qk, z, do, dlse):
    t = z.shape[0]
    n_blocks = t // BAND_BLOCK

    def body(q_ref, k_ref, v_ref, do_ref, dl_ref, dq_ref, dk_ref, dv_ref, qf, kf, vf, dof, dlf, dqf, dkf, dvf,
             dkpf, dvpf):
        def run(dil):
            for src, dst in ((q_ref, qf), (k_ref, kf), (v_ref, vf), (do_ref, dof), (dl_ref, dlf)):
                _fold(src, dst, dil)
            blocks_per_seq = n_blocks // dil

            def block(idx, carry):
                first, cur, prev = _block_rows(idx, blocks_per_seq)
                kmin = jnp.where(first, BAND_BLOCK, 0)
                grads = []
                for q, kp, kc, vp, vc, do_h, dl_h in zip(
                        _heads(qf[cur, :]), _heads(kf[prev, :]), _heads(kf[cur, :]), _heads(vf[prev, :]),
                        _heads(vf[cur, :]), _heads(dof[cur, :]), _heads(dlf[cur, :])):
                    _, vjp = jax.vjp(functools.partial(_attn_block, kmin=kmin), q, kp, kc, vp, vc)
                    grads.append(vjp((do_h, jnp.sum(dl_h, axis=1, keepdims=True))))
                dq, dkp, dkc, dvp, dvc = (jnp.concatenate([a, b], axis=1) for a, b in zip(*grads))
                dqf[cur, :], dkf[cur, :], dvf[cur, :], dkpf[cur, :], dvpf[cur, :] = dq, dkc, dvc, dkp, dvp
                return carry

            lax.fori_loop(0, n_blocks, block, 0, unroll=2)

            def join(idx, carry):
                first, cur, prev = _block_rows(idx, blocks_per_seq)

                @pl.when(jnp.logical_not(first))
                def _():
                    dkf[prev, :] += dkpf[cur, :]
                    dvf[prev, :] += dvpf[cur, :]

                return carry

            lax.fori_loop(0, n_blocks, join, 0)
            _unfold(dqf, dq_ref, dil)
            _unfold(dkf, dk_ref, dil)
            _unfold(dvf, dv_ref, dil)

        _per_group(run)

    slab = jax.ShapeDtypeStruct((t, N_PAIRS * PAIR), F32)
    own = pl.BlockSpec((t, PAIR), lambda p: (0, p))
    return pl.pallas_call(
        body, name="attn_bwd", grid=(N_PAIRS,),
        in_specs=[own, pl.BlockSpec((t, PAIR), lambda p: (0, N_PAIRS + p)),
                  pl.BlockSpec((t, PAIR), lambda p: (0, V_COL0 + p)), own, own],
        out_specs=[own] * 3, out_shape=[slab] * 3,
        scratch_shapes=[pltpu.VMEM((t, PAIR), F32)] * 10,
        compiler_params=_cparams(dimension_semantics=("parallel",)),
    )(qk, qk, z, do, dlse)


def _f_comb(o1, o2, o3, l1, l2, l3):
    m = jnp.maximum(jnp.maximum(l1, l2), l3)
    e1, e2, e3 = jnp.exp(l1 - m), jnp.exp(l2 - m), jnp.exp(l3 - m)
    den = e1 + e2 + e3
    return (e1 / den) * o1 + (e2 / den) * o2 + (e3 / den) * o3


def _all_gather_hbm(name, arrs):
    na = len(arrs)

    def body(*refs):
        x_refs, out_refs = refs[:na], refs[na:2 * na]
        send_sems, recv_sems, local_sems = refs[2 * na:]
        mx, my, mc = lax.axis_index("x"), lax.axis_index("y"), lax.axis_index("c")
        me, sibling = (mx, my, mc), (mx, my, 1 - mc)
        chips = [(1 - mx, my), (mx, 1 - my), (1 - mx, 1 - my)]

        def slot(a, px, py, pc):
            return out_refs[a].at[4 * px + 2 * py + pc]

        def copy(a, k, block, to, src=None):
            return pltpu.make_async_remote_copy(
                src_ref=slot(a, *block) if src is None else src, dst_ref=slot(a, *block),
                send_sem=send_sems.at[a, k], recv_sem=recv_sems.at[a, k], device_id=to, device_id_type=MESH)

        mine = [pltpu.make_async_copy(x_refs[a], slot(a, *me), local_sems.at[a]) for a in range(na)]
        for cp in mine:
            cp.start()
        first = []
        for a in range(na):
            first.append(copy(a, 0, me, sibling, src=x_refs[a]))
            first += [copy(a, 1 + j, me, (*chip, mc), src=x_refs[a]) for j, chip in enumerate(chips)]
        for cp in first:
            cp.start()
        passed = []
        for j, chip in enumerate(chips):
            for a in range(na):
                copy(a, 1 + j, (*chip, mc), me).wait_recv()
                passed.append(copy(a, 4 + j, (*chip, mc), sibling))
                passed[-1].start()
        for a in range(na):
            copy(a, 0, sibling, me).wait_recv()
            for j, chip in enumerate(chips):
                copy(a, 4 + j, (*chip, 1 - mc), me).wait_recv()
        for cp in first + passed:
            cp.wait_send()
        for cp in mine:
            cp.wait()

    hbm = pl.BlockSpec(memory_space=pl.ANY)
    return pl.pallas_call(
        body, name=name,
        out_shape=[jax.ShapeDtypeStruct((N_DEV,) + a.shape, a.dtype) for a in arrs],
        in_specs=[hbm] * na, out_specs=[hbm] * na,
        scratch_shapes=[pltpu.SemaphoreType.DMA((na, 7)), pltpu.SemaphoreType.DMA((na, 7)),
                        pltpu.SemaphoreType.DMA((na,))],
    )(*arrs)


def _all_gather_vmem(x):
    rws, cols = x.shape

    def body(x_ref, out_ref, send_sems, recv_sems):
        mx, my, mc = lax.axis_index("x"), lax.axis_index("y"), lax.axis_index("c")
        me, sibling = (mx, my, mc), (mx, my, 1 - mc)
        chips = [(1 - mx, my), (mx, 1 - my), (1 - mx, 1 - my)]

        def slot(px, py, pc):
            return out_ref.at[4 * px + 2 * py + pc]

        def copy(k, block, to, src=None):
            return pltpu.make_async_remote_copy(
                src_ref=slot(*block) if src is None else src, dst_ref=slot(*block),
                send_sem=send_sems.at[k], recv_sem=recv_sems.at[k], device_id=to, device_id_type=MESH)

        first = [copy(0, me, sibling, src=x_ref)]
        first += [copy(1 + j, me, (*chip, mc), src=x_ref) for j, chip in enumerate(chips)]
        for cp in first:
            cp.start()
        out_ref[4 * mx + 2 * my + mc] = x_ref[...]
        passed = [copy(4 + j, (*chip, mc), sibling) for j, chip in enumerate(chips)]
        for j, chip in enumerate(chips):
            copy(1 + j, (*chip, mc), me).wait_recv()
            passed[j].start()
        copy(0, sibling, me).wait_recv()
        for j, chip in enumerate(chips):
            copy(4 + j, (*chip, 1 - mc), me).wait_recv()
        for cp in first + passed:
            cp.wait_send()

    return pl.pallas_call(
        body, name="all_gather_small",
        out_shape=jax.ShapeDtypeStruct((N_DEV, rws, cols), x.dtype),
        in_specs=[pl.BlockSpec(memory_space=pltpu.VMEM)], out_specs=pl.BlockSpec(memory_space=pltpu.VMEM),
        scratch_shapes=[pltpu.SemaphoreType.DMA((7,)), pltpu.SemaphoreType.DMA((7,))],
    )(x)


def _scatter_copies(g_refs, land_refs, send_sems, recv_sems):
    mx, my, mc = lax.axis_index("x"), lax.axis_index("y"), lax.axis_index("c")
    me = 4 * mx + 2 * my + mc
    copies = []
    for k in range(1, N_DEV):
        px, py, pc = mx ^ (k >> 2), my ^ ((k >> 1) & 1), mc ^ (k & 1)
        peer = 4 * px + 2 * py + pc
        for a, (g_ref, land_ref) in enumerate(zip(g_refs, land_refs)):
            copies.append(pltpu.make_async_remote_copy(
                src_ref=g_ref.at[peer], dst_ref=land_ref.at[me], send_sem=send_sems.at[a * (N_DEV - 1) + k - 1],
                recv_sem=recv_sems.at[a * (N_DEV - 1) + k - 1], device_id=(px, py, pc), device_id_type=MESH))
    return copies


_HBM = pl.BlockSpec(memory_space=pltpu.HBM)
_SEM = pl.BlockSpec(memory_space=pltpu.SEMAPHORE)
_DATAFLOW = pltpu.SideEffectType.DATAFLOW_SIDE_EFFECTING


def _gather_copies(x_refs, land_refs, send_sems, recv_sems):
    mx, my, mc = lax.axis_index("x"), lax.axis_index("y"), lax.axis_index("c")
    me = 4 * mx + 2 * my + mc
    copies = []
    for k in range(1, N_DEV):
        px, py, pc = mx ^ (k >> 2), my ^ ((k >> 1) & 1), mc ^ (k & 1)
        for a, (x_ref, land_ref) in enumerate(zip(x_refs, land_refs)):
            copies.append(pltpu.make_async_remote_copy(
                src_ref=x_ref, dst_ref=land_ref.at[me], send_sem=send_sems.at[a * (N_DEV - 1) + k - 1],
                recv_sem=recv_sems.at[a * (N_DEV - 1) + k - 1], device_id=(px, py, pc), device_id_type=MESH))
    return copies


N_CHIPS = N_DEV // 2


def _gather2_first(x_refs, land_refs, send_sems, recv_sems, ks=(0, 1, 2, 3)):
    mx, my, mc = lax.axis_index("x"), lax.axis_index("y"), lax.axis_index("c")
    me = 4 * mx + 2 * my + mc
    peers = [(mx, my, 1 - mc), (1 - mx, my, mc), (mx, 1 - my, mc), (1 - mx, 1 - my, mc)]
    return [pltpu.make_async_remote_copy(
        src_ref=x_ref, dst_ref=land_ref.at[me], send_sem=send_sems.at[a * (N_DEV - 1) + k],
        recv_sem=recv_sems.at[a * (N_DEV - 1) + k], device_id=peers[k], device_id_type=MESH)
        for a, (x_ref, land_ref) in enumerate(zip(x_refs, land_refs)) for k in ks]


def _gather2_second(x_refs, land_refs, send_sems, recv_sems):
    mx, my, mc = lax.axis_index("x"), lax.axis_index("y"), lax.axis_index("c")
    copies = []
    for a, land_ref in enumerate(land_refs):
        for j, (px, py) in enumerate([(1 - mx, my), (mx, 1 - my), (1 - mx, 1 - my)]):
            slot = 4 * px + 2 * py + mc
            copies.append(pltpu.make_async_remote_copy(
                src_ref=land_ref.at[slot], dst_ref=land_ref.at[slot], send_sem=send_sems.at[a * (N_DEV - 1) + j],
                recv_sem=recv_sems.at[a * (N_DEV - 1) + j], device_id=(mx, my, 1 - mc), device_id_type=MESH))
    return copies


def _pair_swap(name, arrs):
    na = len(arrs)

    def body(*refs):
        g_refs, out_refs = refs[:na], refs[na:2 * na]
        send_sems, recv_sems = refs[2 * na:]
        mx, my, mc = lax.axis_index("x"), lax.axis_index("y"), lax.axis_index("c")
        copies = [pltpu.make_async_remote_copy(
            src_ref=g_refs[a].at[2 * q + 1 - mc], dst_ref=out_refs[a].at[q], send_sem=send_sems.at[a, q],
            recv_sem=recv_sems.at[a, q], device_id=(mx, my, 1 - mc), device_id_type=MESH)
            for a in range(na) for q in range(N_CHIPS)]
        for cp in copies:
            cp.start()
        for cp in copies:
            cp.wait_recv()
        for cp in copies:
            cp.wait_send()

    hbm = pl.BlockSpec(memory_space=pl.ANY)
    return pl.pallas_call(
        body, name=name,
        out_shape=[jax.ShapeDtypeStruct((N_CHIPS,) + a.shape[1:], a.dtype) for a in arrs],
        in_specs=[hbm] * na, out_specs=[hbm] * na,
        scratch_shapes=[pltpu.SemaphoreType.DMA((na, N_CHIPS)), pltpu.SemaphoreType.DMA((na, N_CHIPS))],
    )(*arrs)


def _pair_add(name, g, half, core):
    _, r, c = g.shape
    tr = _tile(r, 512) if r % LANES == 0 else r

    def body(core_ref, g_ref, h_ref, o_ref):
        o_ref[...] = (g_ref[...].astype(F32) + h_ref[...].astype(F32)).astype(o_ref.dtype)

    return pl.pallas_call(
        body, name=name,
        grid_spec=pltpu.PrefetchScalarGridSpec(
            num_scalar_prefetch=1, grid=(N_CHIPS, r // tr),
            in_specs=[pl.BlockSpec((None, tr, c), lambda q, i, core_ref: (2 * q + core_ref[0], i, 0)),
                      pl.BlockSpec((None, tr, c), lambda q, i, core_ref: (q, i, 0))],
            out_specs=pl.BlockSpec((None, tr, c), lambda q, i, core_ref: (q, i, 0))),
        out_shape=jax.ShapeDtypeStruct((N_CHIPS, r, c), g.dtype),
        compiler_params=_cparams(dimension_semantics=("parallel", "parallel")),
    )(core, g, half)


def _chip_copies(h_refs, land_refs, send_sems, recv_sems):
    mx, my, mc = lax.axis_index("x"), lax.axis_index("y"), lax.axis_index("c")
    my_chip = 2 * mx + my
    copies = []
    for k in range(1, N_CHIPS):
        px, py = mx ^ (k >> 1), my ^ (k & 1)
        for a, (h_ref, land_ref) in enumerate(zip(h_refs, land_refs)):
            copies.append(pltpu.make_async_remote_copy(
                src_ref=h_ref.at[2 * px + py], dst_ref=land_ref.at[my_chip], send_sem=send_sems.at[a * (N_DEV - 1) + k - 1],
                recv_sem=recv_sems.at[a * (N_DEV - 1) + k - 1], device_id=(px, py, mc), device_id_type=MESH))
    return copies


def _exchange_start(name, copies, srcs, lands, after):
    na = len(srcs)

    def body(*refs):
        for cp in copies(refs[:na], refs[na:2 * na], refs[2 * na + 1], refs[2 * na + 2]):
            cp.start()
        refs[-1][...] = jnp.zeros_like(refs[-1])

    in_hbm = lambda a: pltpu.with_memory_space_constraint(a, pltpu.HBM)
    outs = pl.pallas_call(
        body, name=name,
        out_shape=(pltpu.SemaphoreType.DMA((na * (N_DEV - 1),)), pltpu.SemaphoreType.DMA((na * (N_DEV - 1),)),
                   *[pltpu.HBM(a.shape, a.dtype) for a in list(srcs) + list(lands)],
                   jax.ShapeDtypeStruct((8, LANES), F32)),
        in_specs=[_HBM] * (2 * na) + [pl.BlockSpec(memory_space=pl.ANY)],
        out_specs=(_SEM, _SEM, *[_HBM] * (2 * na), pl.BlockSpec(memory_space=pltpu.VMEM)),
        input_output_aliases={i: 2 + i for i in range(2 * na)},
        compiler_params=pltpu.CompilerParams(has_side_effects=_DATAFLOW),
    )(*[in_hbm(a) for a in srcs], *[in_hbm(a) for a in lands], after)
    return outs[0], outs[1], outs[2:2 + na], outs[2 + na:2 + 2 * na], outs[-1]


def _exchange_wait(name, copies, send_sems, recv_sems, srcs, lands, after, parts=("send", "recv")):
    na = len(srcs)

    def body(*refs):
        for cp in copies(refs[:na], refs[na:2 * na], refs[2 * na], refs[2 * na + 1]):
            if "send" in parts:
                cp.wait_send()
            if "recv" in parts:
                cp.wait_recv()

    outs = pl.pallas_call(
        body, name=name,
        out_shape=[pltpu.HBM(a.shape, a.dtype) for a in list(srcs) + list(lands)],
        in_specs=[_HBM] * (2 * na) + [_SEM, _SEM, pl.BlockSpec(memory_space=pl.ANY)], out_specs=[_HBM] * (2 * na),
        input_output_aliases={i: i for i in range(2 * na)},
        compiler_params=pltpu.CompilerParams(has_side_effects=_DATAFLOW),
    )(*srcs, *lands, send_sems, recv_sems, after)
    return outs[:na], outs[na:]


def _sum_slots(name, g, tr):
    _, rws, cols = g.shape

    def body(g_ref, o_ref):
        acc = g_ref[0].astype(F32)
        for j in range(1, N_DEV):
            acc = acc + g_ref[j].astype(F32)
        o_ref[...] = acc

    return pl.pallas_call(
        body, name=name, grid=(rws // tr,),
        in_specs=[pl.BlockSpec((N_DEV, tr, cols), lambda i: (0, i, 0))],
        out_specs=pl.BlockSpec((tr, cols), lambda i: (i, 0)),
        out_shape=jax.ShapeDtypeStruct((rws, cols), F32),
        compiler_params=_cparams(dimension_semantics=("parallel",)),
    )(g)


def _adam_math(wv, gv, mv, vv):
    mn = ADAM_B1 * mv + (1.0 - ADAM_B1) * gv
    vn = ADAM_B2 * vv + (1.0 - ADAM_B2) * jnp.square(gv)
    m_hat = mn / (1.0 - ADAM_B1 ** ADAM_STEP)
    v_hat = vn / (1.0 - ADAM_B2 ** ADAM_STEP)
    delta = -ADAM_LR * (m_hat / (jnp.sqrt(v_hat) + ADAM_EPS) + ADAM_WD * wv)
    return delta, mn, vn


def _adamw(name, w, g, m, v, tr):
    return _rowwise(name, _adam_math, [w, g, m, v], [], [(LANES, F32)] * 3, tr=tr)


def _adamw_slots(name, recv, own, rb, cb, cw, w, m, v, tr, order):
    nr, nc = w.shape
    n = recv.shape[0]

    def body(g_ref, own_ref, w_ref, m_ref, v_ref, order_ref, go_ref, d_ref, mo_ref, vo_ref):
        me = 2 * lax.axis_index("x") + lax.axis_index("y")
        if n == N_DEV:
            me = 2 * me + lax.axis_index("c")
        acc = None
        for s in range(n):
            part = jnp.where(me == s, own_ref[s], g_ref[s]).astype(F32)
            acc = part if acc is None else acc + part
        g = acc[:, :nc]
        go_ref[...] = g
        d_ref[...], mo_ref[...], vo_ref[...] = _adam_math(w_ref[...], g, m_ref[...], v_ref[...])

    nat = pl.BlockSpec((tr, nc), lambda i: (i, 0))
    slots = pl.BlockSpec((n, tr, cw), lambda i: (0, rb + i, cb))
    return pl.pallas_call(
        body, name=name, grid=(nr // tr,),
        in_specs=[slots, slots, nat, nat, nat, pl.BlockSpec(memory_space=pl.ANY)],
        out_specs=[nat] * 4, out_shape=[jax.ShapeDtypeStruct((nr, nc), F32)] * 4,
        compiler_params=_cparams(dimension_semantics=("parallel",)),
    )(recv, own, w, m, v, order)


def _local_blocks(w):
    pad_cols = lambda a: jnp.pad(a, ((0, 0), (0, FF_PAD - FF_SHARD)))
    pad_rows = lambda a: jnp.pad(a, ((0, FF_PAD - FF_SHARD), (0, 0)))
    gate_up = lambda tag: jnp.concatenate([pad_cols(w[tag + "_w_gate"]), pad_cols(w[tag + "_w_up"])], axis=1)
    blocks = {
        "ffn1_gu": gate_up("ffn1"), "ffn1_d": pad_rows(w["ffn1_w_down"]), "w_in": w["w_in"],
        "lora": jnp.concatenate([w["rwkv_w2"], w["rwkv_a2"], w["rwkv_g2"]], axis=0),
        "br": jnp.concatenate([w["w_br_rwkv"], w["w_br_attn"], w["ple_w_proj"]], axis=0),
        "w_out": w["w_out"], "ffn2_gu": gate_up("ffn2"), "ffn2_d": pad_rows(w["ffn2_w_down"]),
        "ple_gate": w["ple_w_gate"],
    }
    return {n: a.astype(BF16) for n, a in blocks.items()}


GATHER_GROUPS = {"head": ("ffn1_gu", "ffn1_d"), "mid": ("w_in", "lora"),
                 "rest": ("br", "w_out", "ffn2_gu", "ffn2_d", "ple_gate")}

SCATTER_GROUPS = {"tail": ("ple_gate", "ple_proj", "ffn2_gu", "ffn2_d"), "branch": ("w_out", "br"),
                  "mixer": ("lora", "w_in"),
                  "ffn1_down": ("ffn1_d",), "head": ("ffn1_gu",)}

TWO_LEVEL = ("mixer", "head")

ADAM_PLAN = (
    ("ffn1_w_gate", "ffn1_gu", 0, 0, FF_PAD, 256), ("ffn1_w_up", "ffn1_gu", 0, 1, FF_PAD, 256),
    ("ffn1_w_down", "ffn1_d", 0, 0, D_MODEL, FF_SHARD // 2), ("w_in", "w_in", 0, 0, IN_SHARD, 256),
    ("rwkv_w2", "lora", 0, 0, HEAD_DIM, 64), ("rwkv_a2", "lora", 1, 0, HEAD_DIM, 64),
    ("rwkv_g2", "lora", 2, 0, HEAD_DIM, 64),
    ("w_br_rwkv", "br", 0, 0, OUT_SHARD, 256), ("w_br_attn", "br", 2, 0, OUT_SHARD, 256),
    ("ple_w_proj", "ple_proj", 0, 0, OUT_SHARD, 256), ("w_out", "w_out", 0, 0, D_MODEL, OUT_SHARD),
    ("ffn2_w_gate", "ffn2_gu", 0, 0, FF_PAD, 256), ("ffn2_w_up", "ffn2_gu", 0, 1, FF_PAD, 256),
    ("ffn2_w_down", "ffn2_d", 0, 0, D_MODEL, FF_SHARD // 2), ("ple_w_gate", "ple_gate", 0, 0, D_MODEL, OUT_SHARD),
)


def _pack_small(arrs, rows):
    flat = jnp.concatenate([a.reshape(-1) for a in arrs])
    return jnp.pad(flat, (0, rows * LANES - flat.shape[0])).reshape(rows, LANES)


def _unpack_small(flat, like):
    flat = flat.reshape(-1)
    out, off = [], 0
    for a in like:
        out.append(flat[off:off + a.size].reshape(a.shape))
        off += a.size
    return out


def _local_step(x, p, pos, target, sm, wg, fetch, on_grads):
    t = x.shape[0]

    w_d1 = wg["ffn1_d"].reshape(FF_HID, D_MODEL)
    x1, ffn1_saved, _ = _ffn_fwd("ffn1", x, sm["ffn1_norm"], wg["ffn1_gu"], lambda after: w_d1)
    fetch("forward", x1)
    h2 = _norm_fwd("mix_norm", x1, sm["mix_norm"])
    wg = {**wg, **fetch("mid", h2)}
    full_cols = lambda blk: blk.transpose(1, 0, 2).reshape(blk.shape[1], N_DEV * blk.shape[2])
    lora_w2 = full_cols(wg["lora"][:, :DECAY_LORA])
    lora_a2 = full_cols(wg["lora"][:, DECAY_LORA:DECAY_LORA + ICLR_LORA])
    lora_g2 = full_cols(wg["lora"][:, DECAY_LORA + ICLR_LORA:])
    z =_mmc_nn("w_in", h2, wg["w_in"], 0, 0, IN_SHARD)
    z_g = (z, 2 * D_MODEL, (RWKV_COLS + 3 * ATTN_DIM) // (2 * D_MODEL))

    r, k, v, lo, gd = _shift_fwd(z, sm["rwkv_mu"])
    zero_lo = jnp.zeros((DECAY_LORA, RWKV_DIM), BF16)
    w2p = jnp.concatenate([lora_w2, zero_lo], axis=0).astype(F32)
    a2p = jnp.concatenate([zero_lo, lora_a2], axis=0).astype(F32)
    pre_params = [sm["rwkv_w0"], w2p, sm["rwkv_a0"], a2p, lora_g2.astype(F32), sm["rwkv_k_k"], sm["rwkv_k_a"]]
    wide = [(RWKV_DIM, F32)]
    k2, kk, a, decay, g = _rowwise("rwkv_pre", _f_pre, [k, lo, gd], pre_params, wide * 5)
    kkn, b = _pairwise("rwkv_kk", _f_kk, [kk, a], [], 2)
    scan_in = [u.reshape(t, RWKV_HEADS, HEAD_DIM) for u in (r, decay, k2, kkn, b)]
    v_rows = _to_v_rows(v)
    y_rows, states = _wkv_fwd(*scan_in, v_rows)
    y = _from_v_rows(y_rows)
    post_params = [sm["rwkv_gn_w"], sm["rwkv_gn_b"], sm["rwkv_r_k"]]
    post_rows = [y, r, k2, v, g]
    y_rwkv = _pairwise("rwkv_post", lambda *av: (_f_post(*av),), post_rows, post_params, 1)[0]

    inv_freq = 1.0 / (ROPE_THETA ** (jnp.arange(0, HEAD_DIM, 2, dtype=F32) / HEAD_DIM))
    freq2 = jnp.tile(inv_freq, 2 * PAIR // HEAD_DIM).reshape(1, PAIR)
    half = jnp.ones((HEAD_DIM // 2,), F32)
    sign2 = jnp.tile(jnp.concatenate([-half, half]), PAIR // HEAD_DIM).reshape(1, PAIR)

    def rope_table(posv, fr, sg):
        ang = posv * fr
        return (jnp.concatenate([jnp.cos(ang), jnp.sin(ang) * sg], axis=1),)
    tab = _rowwise("rope_table", rope_table, [pos.astype(F32).reshape(t, 1)], [freq2, sign2], [(2 * PAIR, F32)])[0]
    qk = _qk_prep(z, tab, sm["q_norm"], sm["k_norm"])
    o_all, lse_all = _attn_fwd(qk, z)
    gw = HEADS_PER_GROUP * HEAD_DIM

    def by_group(ov, lv):
        return [ov[:, i * gw:(i + 1) * gw] for i in range(3)] + [lv[:, i * gw:(i + 1) * gw] for i in range(3)]
    y_attn = _rowwise("attn_comb", lambda ov, lv: (_f_comb(*by_group(ov, lv)),), [o_all, lse_all], [], [(gw, F32)])[0]

    wg = {**wg, **fetch("rest", y_rwkv)}
    w_d2 = wg["ffn2_d"].reshape(FF_HID, D_MODEL)
    w_out = wg["w_out"].reshape(D_MODEL, D_MODEL)
    w_pg = wg["ple_gate"].reshape(D_MODEL, D_MODEL)
    w_brr = full_cols(wg["br"][:, :RWKV_DIM])
    w_bra = full_cols(wg["br"][:, RWKV_DIM:RWKV_DIM + gw])
    w_pp = full_cols(wg["br"][:, RWKV_DIM + gw:])
    u_r = _mm("br_rwkv", y_rwkv, w_brr, "nn")
    u_a = _mm("br_attn", y_attn, w_bra, "nn")

    def f_merge(zgr, zga, ur, ua):
        return _sigmoid(zgr) * ur + _sigmoid(zga) * ua
    merged = _rowwise("merge", lambda zg, ur, ua: (f_merge(zg[:, :D_MODEL], zg[:, D_MODEL:], ur, ua),),
                      [z_g, u_r, u_a], [], [(D_MODEL, BF16)])[0]
    x2 = _mm("w_out", merged, w_out, "nn", res=x1)
    x3, ffn2_saved, _ = _ffn_fwd("ffn2", x2, sm["ffn2_norm"], wg["ffn2_gu"], lambda after: w_d2)

    hn = _norm_fwd("ple_norm", x3, sm["ple_norm"])
    gz = _mm("ple_gate", hn, w_pg, "nn")
    pp = _mm("ple_proj", p, w_pp, "nn")

    def f_head(x3v, gzv, ppv, tg):
        sg = _sigmoid(gzv)
        err = x3v + sg * ppv - tg
        part = 0.5 * jnp.sum(jnp.mean(err * err, axis=-1, keepdims=True))
        dx4 = err * (1.0 / D_MODEL)
        return dx4, dx4 * ppv * sg * (1.0 - sg), dx4 * sg, jnp.full((1, LANES), part, F32)
    dx4, dgz, dpp, loss_row = _rowwise("ple_loss", f_head, [x3, gz, pp, target], [],
                                       [(D_MODEL, F32), (D_MODEL, BF16), (D_MODEL, BF16)], [(1, LANES)])
    loss = loss_row[0, 0]

    gs, gm = {}, {}
    row_blocks = lambda g: g.reshape(N_DEV, g.shape[0] // N_DEV, g.shape[1])
    dhn = _mm("ple_dhn", dgz, w_pg, "nt")
    gm["ple_gate"] = row_blocks(_mm("ple_dwgate", hn, dgz, "tn", out_dtype=BF16))
    col_blocks = lambda g: g.reshape(g.shape[0], N_DEV, g.shape[1] // N_DEV).transpose(1, 0, 2)
    gm["ple_proj"] = col_blocks(_mm("ple_dwproj", p, dpp, "tn", out_dtype=BF16))
    dx3, gs["ple_norm"] = _norm_bwd("ple_dnorm", x3, sm["ple_norm"], dhn, dx4)

    dx2, gs["ffn2_norm"], gm["ffn2_gu"], gm["ffn2_d"] = _ffn_bwd(
        "ffn2", x2, sm["ffn2_norm"], wg["ffn2_gu"], w_d2, ffn2_saved, dx3)
    tail_token = on_grads("tail", {n: gm.pop(n) for n in SCATTER_GROUPS["tail"]})

    dmerged = _mm("w_out_dmerged", dx2, w_out, "nt")
    gm["w_out"] = row_blocks(_mm("w_out_dw", merged, dx2, "tn", out_dtype=BF16))

    def merge_bwd(zg, ur, ua, dm):
        _, vjp = jax.vjp(f_merge, zg[:, :D_MODEL], zg[:, D_MODEL:], ur, ua)
        dzr, dza, dur, dua = vjp(dm)
        return jnp.concatenate([dzr, dza], axis=1), dur, dua
    dz_g, du_r, du_a = _rowwise("merge_bwd", merge_bwd, [z_g, u_r, u_a, dmerged], [],
                                [(2 * D_MODEL, BF16), (D_MODEL, BF16), (D_MODEL, BF16)])
    dy_rwkv = _mm("br_rwkv_dy", du_r, w_brr, "nt")
    dy_attn = _mm("br_attn_dy", du_a, w_bra, "nt")
    gm["br"] = jnp.concatenate([col_blocks(_mm("br_rwkv_dw", y_rwkv, du_r, "tn", out_dtype=BF16)),
                                col_blocks(_mm("br_attn_dw", y_attn, du_a, "tn", out_dtype=BF16))], axis=1)
    branch_token = on_grads("branch", {n: gm.pop(n) for n in SCATTER_GROUPS["branch"]})

    def comb_bwd(ov, lv, dyv):
        _, vjp = jax.vjp(_f_comb, *by_group(ov, lv))
        d = vjp(dyv)
        return jnp.concatenate(d[:3], axis=1), jnp.concatenate(d[3:], axis=1)
    do_all, dl_all = _rowwise("attn_comb_bwd", comb_bwd, [o_all, lse_all, dy_attn], [],
                              [(ATTN_DIM, F32), (ATTN_DIM, F32)])
    dq_all, dk_all, dv_all = _attn_bwd(qk, z, do_all, dl_all)
    dqk_raw, gs["q_norm"], gs["k_norm"] = _qk_prep_bwd(z, tab, sm["q_norm"], sm["k_norm"], dq_all, dk_all)

    def post_bwd(yv, rv, k2v, vv, gv, dv_, gnw, gnb, rk):
        _, vjp = jax.vjp(_f_post, yv, rv, k2v, vv, gv, gnw, gnb, rk)
        return vjp(dv_)
    dy, dr1, dk2a, dv1, dg, d_gnw, d_gnb, d_rk = _pairwise(
        "rwkv_post_bwd", post_bwd, post_rows + [dy_rwkv], [post_params[0] + tail_token[0, 0] + branch_token[0, 0]] + post_params[1:], 5, 3)
    gs["rwkv_gn_w"], gs["rwkv_gn_b"], gs["rwkv_r_k"] = d_gnw, d_gnb, d_rk
    dr2, ddecay, dk2b, dkkn, db, dv_rows = _wkv_bwd(*scan_in, v_rows, states, _to_v_rows(dy))
    dr2, ddecay, dk2b, dkkn, db = [u.reshape(t, RWKV_DIM) for u in (dr2, ddecay, dk2b, dkkn, db)]
    dv2 = _from_v_rows(dv_rows)

    def kk_bwd(kkv, av, dkknv, dbv, dra, drb, dva, dvb):
        _, vjp = jax.vjp(_f_kk, kkv, av)
        return (*vjp((dkknv, dbv)), dra + drb, dva + dvb)
    dkk, da, dr, dv = _pairwise("rwkv_kk_bwd", kk_bwd, [kk, a, dkkn, db, dr1, dr2, dv1, dv2], [], 4)

    def pre_bwd(kv, lov, gdv, dk2x, dk2y, dkkv, dav, ddec, dgv, w0, w2p_, a0, a2p_, g2, k_k, k_a):
        _, vjp = jax.vjp(_f_pre, kv, lov, gdv, w0, w2p_, a0, a2p_, g2, k_k, k_a)
        return vjp((dk2x + dk2y, dkkv, dav, ddec, dgv))
    lora_acc = (DECAY_LORA + ICLR_LORA, RWKV_DIM)
    dk, dlo, dgd, d_w0, d_w2p, d_a0, d_a2p, d_g2, d_kk, d_ka = _rowwise(
        "rwkv_pre_bwd", pre_bwd,
        [k, lo, gd, dk2a, dk2b, dkk, da, ddecay, dg],
        pre_params, [(RWKV_DIM, F32), (LANES, F32), (LANES, F32)],
        [(1, RWKV_DIM), lora_acc, (1, RWKV_DIM), lora_acc, (GATE_LORA, RWKV_DIM), (1, RWKV_DIM), (1, RWKV_DIM)])
    gs["rwkv_w0"], gs["rwkv_a0"], gs["rwkv_k_k"], gs["rwkv_k_a"] = d_w0, d_a0, d_kk, d_ka
    gm["lora"] = jnp.concatenate([col_blocks(d_w2p[:DECAY_LORA]), col_blocks(d_a2p[DECAY_LORA:]), col_blocks(d_g2)],
                                 axis=1).astype(BF16)
    dz_r, gs["rwkv_mu"] = _shift_bwd(z, sm["rwkv_mu"], dr, dk, dv, dlo, dgd)

    dz = jnp.concatenate([dz_r, dqk_raw, dv_all.astype(BF16), dz_g], axis=1)
    dh2 = _mmc_nt("w_in_dh", dz, wg["w_in"], 0, 0, IN_SHARD, D_MODEL)
    gm["w_in"] = _mmc_tn("w_in_dw", h2, dz, IN_SHARD)
    mixer_token = on_grads("mixer", {n: gm.pop(n) for n in SCATTER_GROUPS["mixer"]})
    dx1, gs["mix_norm"] = _norm_bwd("mix_dnorm", x1, sm["mix_norm"] + mixer_token[0, 0], dh2, dx2)

    dx0, gs["ffn1_norm"], gm["ffn1_gu"], _ = _ffn_bwd(
        "ffn1", x, sm["ffn1_norm"], wg["ffn1_gu"], w_d1, ffn1_saved, dx1,
        on_down=lambda blocks: on_grads("ffn1_down", {"ffn1_d": blocks}))
    return loss, dx0, gm, gs


def kernel(x, p, positions, ffn1_norm, ffn1_w_gate, ffn1_w_up, ffn1_w_down, mix_norm, w_in, rwkv_mu, rwkv_w0, rwkv_w2, rwkv_a0, rwkv_a2, rwkv_g2, rwkv_k_k, rwkv_k_a, rwkv_r_k, rwkv_gn_w, rwkv_gn_b, q_norm, k_norm, w_br_rwkv, w_br_attn, w_out, ffn2_norm, ffn2_w_gate, ffn2_w_up, ffn2_w_down, ple_norm, ple_w_gate, ple_w_proj, loss_target, m_ffn1_norm, m_ffn1_w_gate, m_ffn1_w_up, m_ffn1_w_down, m_mix_norm, m_w_in, m_rwkv_mu, m_rwkv_w0, m_rwkv_w2, m_rwkv_a0, m_rwkv_a2, m_rwkv_g2, m_rwkv_k_k, m_rwkv_k_a, m_rwkv_r_k, m_rwkv_gn_w, m_rwkv_gn_b, m_q_norm, m_k_norm, m_w_br_rwkv, m_w_br_attn, m_w_out, m_ffn2_norm, m_ffn2_w_gate, m_ffn2_w_up, m_ffn2_w_down, m_ple_norm, m_ple_w_gate, m_ple_w_proj, v_ffn1_norm, v_ffn1_w_gate, v_ffn1_w_up, v_ffn1_w_down, v_mix_norm, v_w_in, v_rwkv_mu, v_rwkv_w0, v_rwkv_w2, v_rwkv_a0, v_rwkv_a2, v_rwkv_g2, v_rwkv_k_k, v_rwkv_k_a, v_rwkv_r_k, v_rwkv_gn_w, v_rwkv_gn_b, v_q_norm, v_k_norm, v_w_br_rwkv, v_w_br_attn, v_w_out, v_ffn2_norm, v_ffn2_w_gate, v_ffn2_w_up, v_ffn2_w_down, v_ple_norm, v_ple_w_gate, v_ple_w_proj):
    args = locals()
    w = {n: args[n][0] for n in WEIGHTS}
    m = {n: args["m_" + n][0] for n in WEIGHTS}
    v = {n: args["v_" + n][0] for n in WEIGHTS}

    w_loc = _local_blocks(w)
    head = GATHER_GROUPS["head"]
    wg = dict(zip(head, _all_gather_hbm("gather_head", [w_loc[n] for n in head])))
    me = 4 * lax.axis_index("x") + 2 * lax.axis_index("y") + lax.axis_index("c")
    gathering, order_after = {}, wg[head[0]]
    for group in ("mid", "rest"):
        shards = [w_loc[n] for n in GATHER_GROUPS[group]]
        zones = [lax.dynamic_update_slice(lax.empty((N_DEV,) + a.shape, a.dtype), a[None], (me, 0, 0)) for a in shards]
        copies = _gather2_first if group == "mid" else _gather_copies
        *gathering[group], order_after = _exchange_start("gather_start_" + group, copies, shards, zones, order_after)

    def fetch(group, after):
        if group == "forward":
            s1, r1, srcs, lands = gathering["mid"]
            ici = functools.partial(_gather2_first, ks=(1, 2, 3))
            srcs, lands = _exchange_wait("gather_arrived_mid", ici, s1, r1, srcs, lands, after, parts=("recv",))
            s2, r2, srcs, lands, _ = _exchange_start("gather_forward_mid", _gather2_second, srcs, lands, after)
            gathering["mid"] = (s1, r1, s2, r2, srcs, lands)
            return {}
        if group == "mid":
            s1, r1, s2, r2, srcs, lands = gathering["mid"]
            sib = functools.partial(_gather2_first, ks=(0,))
            srcs, lands = _exchange_wait("gather_passed_mid", _gather2_second, s2, r2, srcs, lands, after)
            srcs, lands = _exchange_wait("gather_sent_mid", _gather2_first, s1, r1, srcs, lands, after, parts=("send",))
            _, got = _exchange_wait("gather_wait_mid", sib, s1, r1, srcs, lands, after, parts=("recv",))
        else:
            _, got = _exchange_wait("gather_wait_" + group, _gather_copies, *gathering[group], after)
        return dict(zip(GATHER_GROUPS[group], got))

    sm = {n: w[n].reshape(1, -1) for n in SMALL}
    sm["ffn1_norm"] = sm["ffn1_norm"] + order_after[0, 0]
    in_flight = {}

    core = lax.axis_index("c").astype(jnp.int32).reshape(1)

    def scatter_early(group, arrays):
        arrs = [arrays[n] for n in SCATTER_GROUPS[group]]
        copies, after = _scatter_copies, arrs[0]
        if group in TWO_LEVEL:
            halves = _pair_swap("scatter_pair_" + group, arrs)
            arrs = [_pair_add("scatter_add_%s_%s" % (group, n), a, hf, core)
                    for n, a, hf in zip(SCATTER_GROUPS[group], arrs, halves)]
            copies, after = _chip_copies, halves[0]
        *in_flight[group], token = _exchange_start("scatter_start_" + group, copies, arrs,
                                                   [lax.empty(a.shape, a.dtype) for a in arrs], after)
        return token
    loss_part, dx, gm, gs = _local_step(x[0], p[0, 0], positions[0], loss_target[0], sm, wg, fetch, scatter_early)
    head_token = scatter_early("head", gm)
    recv, own = {}, {}

    def arrived(group, after):
        copies = _chip_copies if group in TWO_LEVEL else _scatter_copies
        sent, lands = _exchange_wait("scatter_wait_" + group, copies, *in_flight[group], after)
        own.update(zip(SCATTER_GROUPS[group], sent))
        recv.update(zip(SCATTER_GROUPS[group], lands))
    for group in ("tail", "branch", "mixer", "ffn1_down"):
        arrived(group, head_token)
    small_like = [w[n] for n in SMALL]
    small_rows = 80
    gs_all = _all_gather_vmem(_pack_small([gs[n] for n in SMALL] + [loss_part], small_rows))
    gs_sum = _sum_slots("sum_small_grads", gs_all, small_rows)
    loss = gs_sum.reshape(-1)[sum(a.size for a in small_like)]

    res = {}
    early = [e for group in ("tail", "branch", "mixer", "ffn1_down") for e in ADAM_PLAN
             if e[1] in SCATTER_GROUPS[group]]
    late = [e for e in ADAM_PLAN if e[1] in SCATTER_GROUPS["head"]]
    order = head_token
    for n, src, rb, cb, cw, tr in early + late:
        if (n, src, rb, cb, cw, tr) == late[0]:
            arrived("head", order)
        outs4 = _adamw_slots("adamw_" + n, recv[src], own[src], rb, cb, cw, w[n], m[n], v[n], tr, order)
        order = outs4[1]
        for tag, a in zip(("grad", "delta", "new_m", "new_v"), outs4):
            res[tag, n] = a[None]
    d_s, m_s, v_s = _adamw("adamw_small", _pack_small(small_like, small_rows), gs_sum,
                           _pack_small([m[n] for n in SMALL], small_rows),
                           _pack_small([v[n] for n in SMALL], small_rows), small_rows)
    for tag, small in (("grad", gs_sum), ("delta", d_s), ("new_m", m_s), ("new_v", v_s)):
        for n, a in zip(SMALL, _unpack_small(small, small_like)):
            res[tag, n] = a[None]
    outs = [loss, dx[None]]
    for tag in ("grad", "delta", "new_m", "new_v"):
        outs += [res[tag, n] for n in WEIGHTS]
    return tuple(outs)
```

```python
import functools

import jax
import jax.numpy as jnp
from jax import lax
from jax.experimental import pallas as pl
from jax.experimental.pallas import tpu as pltpu

F32, BF16 = jnp.float32, jnp.bfloat16
MESH = pl.DeviceIdType.MESH
N_DEV = 8
LANES = 128
VMEM_LIMIT = 56 * 1024 * 1024

D_MODEL = 1024
PLE_DIM = 256
HEAD_DIM = 64
RWKV_HEADS = 8
RWKV_DIM = RWKV_HEADS * HEAD_DIM
DECAY_LORA = 64
ICLR_LORA = 64
GATE_LORA = 128
GN_EPS = 64e-5
ATTN_GROUPS = ((128, 1), (512, 4), (2048, 16))
HEADS_PER_GROUP = 4
ATTN_HEADS = HEADS_PER_GROUP * len(ATTN_GROUPS)
ATTN_DIM = ATTN_HEADS * HEAD_DIM
BAND_BLOCK = 128
ROPE_THETA = 10000.0
NEG_INF = -1e30
D_FF = 2816
RMS_EPS = 1e-6
RWKV_COLS = 3 * RWKV_DIM + DECAY_LORA + ICLR_LORA + GATE_LORA
ADAM_LR = 0.001
ADAM_B1 = 0.9
ADAM_B2 = 0.999
ADAM_EPS = 1e-08
ADAM_WD = 0.01
ADAM_STEP = 10

V_LO = LANES // RWKV_HEADS
V_HI = HEAD_DIM // V_LO
SCAN_CHUNK = 64
MM_ROWS = 2048

FF_SHARD = D_FF // N_DEV
FF_PAD = -(-FF_SHARD // LANES) * LANES
FF_HID = N_DEV * FF_PAD
IN_SHARD = 6144 // N_DEV
OUT_SHARD = D_MODEL // N_DEV

SMALL = ("ffn1_norm", "mix_norm", "rwkv_mu", "rwkv_w0", "rwkv_a0", "rwkv_k_k", "rwkv_k_a", "rwkv_r_k",
         "rwkv_gn_w", "rwkv_gn_b", "q_norm", "k_norm", "ffn2_norm", "ple_norm")
WEIGHTS = ("ffn1_norm", "ffn1_w_gate", "ffn1_w_up", "ffn1_w_down", "mix_norm", "w_in", "rwkv_mu", "rwkv_w0",
           "rwkv_w2", "rwkv_a0", "rwkv_a2", "rwkv_g2", "rwkv_k_k", "rwkv_k_a", "rwkv_r_k", "rwkv_gn_w",
           "rwkv_gn_b", "q_norm", "k_norm", "w_br_rwkv", "w_br_attn", "w_out", "ffn2_norm", "ffn2_w_gate",
           "ffn2_w_up", "ffn2_w_down", "ple_norm", "ple_w_gate", "ple_w_proj")


def _cparams(**kw):
    return pltpu.CompilerParams(vmem_limit_bytes=VMEM_LIMIT, **kw)


def _tile(n, cap):
    best = None
    for t in range(LANES, min(n, cap) + 1, LANES):
        if n % t == 0:
            best = t
    return best if best is not None else n


@jax.custom_vjp
def _bdot(a, w):
    return jnp.dot(a.astype(BF16), w.astype(BF16), preferred_element_type=F32)


def _bdot_fwd(a, w):
    return _bdot(a, w), (a, w)


def _bdot_bwd(res, g):
    a, w = res
    gb = g.astype(BF16)
    da = lax.dot_general(gb, w.astype(BF16), (((1,), (1,)), ((), ())), preferred_element_type=F32)
    dw = lax.dot_general(a.astype(BF16), gb, (((0,), (0,)), ((), ())), preferred_element_type=F32)
    return da.astype(a.dtype), dw.astype(w.dtype)


_bdot.defvjp(_bdot_fwd, _bdot_bwd)


@jax.custom_vjp
def _bdot_nt(a, b):
    return lax.dot_general(a.astype(BF16), b.astype(BF16), (((1,), (1,)), ((), ())), preferred_element_type=F32)


def _bdot_nt_fwd(a, b):
    return _bdot_nt(a, b), (a, b)


def _bdot_nt_bwd(res, g):
    a, b = res
    gb = g.astype(BF16)
    da = jnp.dot(gb, b.astype(BF16), preferred_element_type=F32)
    db = lax.dot_general(gb, a.astype(BF16), (((0,), (0,)), ((), ())), preferred_element_type=F32)
    return da.astype(a.dtype), db.astype(b.dtype)


_bdot_nt.defvjp(_bdot_nt_fwd, _bdot_nt_bwd)


def _mm(name, a, b, mode, out_dtype=F32, res=None, scale=None):
    if mode == "nn":
        (m, k), n = a.shape, b.shape[1]
    elif mode == "nt":
        (m, k), n = a.shape, b.shape[0]
    else:
        (k, m), n = a.shape, b.shape[1]
    tm, tn = _tile(m, 1024), _tile(n, 512)
    a_spec = pl.BlockSpec((k, tm), lambda i, j: (0, i)) if mode == "tn" else pl.BlockSpec((tm, k), lambda i, j: (i, 0))
    b_spec = pl.BlockSpec((tn, k), lambda i, j: (j, 0)) if mode == "nt" else pl.BlockSpec((k, tn), lambda i, j: (0, j))
    dims = {"nn": ((1,), (0,)), "nt": ((1,), (1,)), "tn": ((0,), (0,))}[mode]
    o_spec = pl.BlockSpec((tm, tn), lambda i, j: (i, j))
    ins, in_specs = [a, b], [a_spec, b_spec]
    if res is not None:
        ins.append(res)
        in_specs.append(o_spec)

    def body(*refs):
        acc = lax.dot_general(refs[0][...].astype(BF16), refs[1][...].astype(BF16), (dims, ((), ())),
                              preferred_element_type=F32)
        if scale is not None:
            acc = acc * scale
        if res is not None:
            acc = acc + refs[2][...].astype(F32)
        refs[-1][...] = acc.astype(refs[-1].dtype)

    return pl.pallas_call(
        body, name=name, grid=(m // tm, n // tn), in_specs=in_specs, out_specs=o_spec,
        out_shape=jax.ShapeDtypeStruct((m, n), out_dtype),
        compiler_params=_cparams(dimension_semantics=("parallel", "parallel")),
    )(*ins)


def _mmc_nn(name, a, wb, ki, ci, n, out_dtype=F32):
    m, k = a.shape
    tm = _tile(m, MM_ROWS)

    def body(a_ref, w_ref, o_ref):
        o_ref[...] = jnp.dot(a_ref[...].astype(BF16), w_ref[...], preferred_element_type=F32).astype(o_ref.dtype)

    return pl.pallas_call(
        body, name=name, grid=(m // tm, N_DEV),
        in_specs=[pl.BlockSpec((tm, k), lambda i, j: (i, 0)), pl.BlockSpec((None, k, n), lambda i, j: (j, ki, ci))],
        out_specs=pl.BlockSpec((tm, n), lambda i, j: (i, j)),
        out_shape=jax.ShapeDtypeStruct((m, N_DEV * n), out_dtype),
        compiler_params=_cparams(dimension_semantics=("parallel", "parallel")),
    )(a, wb)


def _mmc_nt(name, a, wb, ki, ci, n, k, res=None):
    m = a.shape[0]
    tm = _tile(m, MM_ROWS)
    o_spec = pl.BlockSpec((tm, k), lambda i, j: (i, 0))
    ins = [a, wb] + ([res] if res is not None else [])
    in_specs = [pl.BlockSpec((tm, n), lambda i, j: (i, j)), pl.BlockSpec((None, k, n), lambda i, j: (j, ki, ci))]
    in_specs += [o_spec] if res is not None else []

    def body(*refs):
        a_ref, w_ref, o_ref = refs[0], refs[1], refs[-1]
        acc = lax.dot_general(a_ref[...].astype(BF16), w_ref[...], (((1,), (1,)), ((), ())),
                              preferred_element_type=F32)

        @pl.when(pl.program_id(1) == 0)
        def _():
            o_ref[...] = acc + refs[2][...] if res is not None else acc

        @pl.when(pl.program_id(1) != 0)
        def _():
            o_ref[...] += acc

    return pl.pallas_call(
        body, name=name, grid=(m // tm, N_DEV), in_specs=in_specs, out_specs=o_spec,
        out_shape=jax.ShapeDtypeStruct((m, k), F32),
        compiler_params=_cparams(dimension_semantics=("parallel", "arbitrary")),
    )(*ins)


def _mmc_tn(name, x, dy, n):
    m, k = x.shape
    tk = _tile(k, 1024)

    def body(x_ref, dy_ref, o_ref):
        o_ref[...] = lax.dot_general(x_ref[...].astype(BF16), dy_ref[...].astype(BF16), (((0,), (0,)), ((), ())),
                                     preferred_element_type=F32).astype(o_ref.dtype)

    return pl.pallas_call(
        body, name=name, grid=(N_DEV, k // tk),
        in_specs=[pl.BlockSpec((m, tk), lambda j, i: (0, i)), pl.BlockSpec((m, n), lambda j, i: (0, j))],
        out_specs=pl.BlockSpec((None, tk, n), lambda j, i: (j, i, 0)),
        out_shape=jax.ShapeDtypeStruct((N_DEV, k, n), BF16),
        compiler_params=_cparams(dimension_semantics=("parallel", "parallel")),
    )(x, dy)


def _rowwise(name, fn, rows, params, out_rows, out_accs=(), tr=512):
    rows = [a if isinstance(a, tuple) else (a, a.shape[1], 0) for a in rows]
    r = rows[0][0].shape[0]
    in_specs = [pl.BlockSpec((tr, wd), lambda i, cb=cb: (i, cb)) for _, wd, cb in rows]
    rows = [a for a, _, _ in rows]
    in_specs += [pl.BlockSpec(p.shape, lambda i, nd=p.ndim: (0,) * nd) for p in params]
    out_shape = [jax.ShapeDtypeStruct((r, c), dt) for c, dt in out_rows]
    out_shape += [jax.ShapeDtypeStruct(s, F32) for s in out_accs]
    out_specs = [pl.BlockSpec((tr, c), lambda i: (i, 0)) for c, _ in out_rows]
    out_specs += [pl.BlockSpec(s, lambda i, nd=len(s): (0,) * nd) for s in out_accs]
    n_in, n_ro = len(rows) + len(params), len(out_rows)

    def body(*refs):
        res = fn(*[ref[...] for ref in refs[:n_in]])
        outs = refs[n_in:]
        for o, v in zip(outs[:n_ro], res[:n_ro]):
            o[...] = v.astype(o.dtype)
        for o, v in zip(outs[n_ro:], res[n_ro:]):
            _accumulate(o, v)

    return pl.pallas_call(
        body, name=name, grid=(r // tr,), in_specs=in_specs, out_specs=out_specs, out_shape=out_shape,
        compiler_params=_cparams(dimension_semantics=("arbitrary",)),
    )(*rows, *params)


def _pairwise(name, fn, rows, params, n_out, n_acc=0, tr=512):
    t, c = rows[0].shape
    tile = pl.BlockSpec((tr, 2 * HEAD_DIM), lambda p, i: (i, p))
    vec = pl.BlockSpec((1, 2 * HEAD_DIM), lambda p, i: (0, p))
    n_in = len(rows) + len(params)

    def body(*refs):
        res = fn(*[ref[...] for ref in refs[:n_in]])
        outs = refs[n_in:]
        for o, v in zip(outs[:n_out], res[:n_out]):
            o[...] = v
        first = pl.program_id(1) == 0
        for o, v in zip(outs[n_out:], res[n_out:]):
            @pl.when(first)
            def _(o=o, v=v):
                o[...] = v

            @pl.when(jnp.logical_not(first))
            def _(o=o, v=v):
                o[...] += v

    return pl.pallas_call(
        body, name=name, grid=(c // (2 * HEAD_DIM), t // tr),
        in_specs=[tile] * len(rows) + [vec] * len(params), out_specs=[tile] * n_out + [vec] * n_acc,
        out_shape=[jax.ShapeDtypeStruct((t, c), F32)] * n_out + [jax.ShapeDtypeStruct((1, c), F32)] * n_acc,
        compiler_params=_cparams(dimension_semantics=("parallel", "arbitrary")),
    )(*rows, *params)


def _accumulate(o_ref, v):
    @pl.when(pl.program_id(0) == 0)
    def _():
        o_ref[...] = v

    @pl.when(pl.program_id(0) != 0)
    def _():
        o_ref[...] += v


def _rms(x, g):
    return x * lax.rsqrt(jnp.mean(x * x, axis=-1, keepdims=True) + RMS_EPS) * g


def _sigmoid(x):
    return jax.nn.sigmoid(x)


def _softplus(x):
    return jnp.maximum(x, 0.0) + jnp.log1p(jnp.exp(-jnp.abs(x)))


def _norm_fwd(name, x, g):
    return _rowwise(name, lambda xv, gv: (_rms(xv, gv),), [x], [g], [(x.shape[1], BF16)])[0]


def _norm_bwd(name, x, g, dh, dres):
    def fn(xv, dhv, drv, gv):
        _, vjp = jax.vjp(_rms, xv, gv)
        dx, dg = vjp(dhv)
        return dx + drv, dg
    return _rowwise(name, fn, [x, dh, dres], [g], [(x.shape[1], F32)], [g.shape])


def _f_act(gate, up):
    return gate * _sigmoid(gate) * up


def _gate_up_act(name, h, w_gu):
    m, k = h.shape
    tm = _tile(m, MM_ROWS)

    def body(h_ref, w_ref, gu_ref, a_ref):
        gu = jnp.dot(h_ref[...], w_ref[...], preferred_element_type=F32)
        gu_ref[...] = gu
        a_ref[...] = _f_act(gu[:, :FF_PAD], gu[:, FF_PAD:]).astype(a_ref.dtype)

    return pl.pallas_call(
        body, name=name, grid=(m // tm, N_DEV),
        in_specs=[pl.BlockSpec((tm, k), lambda i, j: (i, 0)),
                  pl.BlockSpec((None, k, 2 * FF_PAD), lambda i, j: (j, 0, 0))],
        out_specs=[pl.BlockSpec((tm, 2 * FF_PAD), lambda i, j: (i, j)), pl.BlockSpec((tm, FF_PAD), lambda i, j: (i, j))],
        out_shape=[jax.ShapeDtypeStruct((m, N_DEV * 2 * FF_PAD), F32), jax.ShapeDtypeStruct((m, FF_HID), BF16)],
        compiler_params=_cparams(dimension_semantics=("parallel", "parallel")),
    )(h, w_gu)


def _gate_up_act_bwd(name, dout, w_down, gu, order):
    m, k = dout.shape
    tm = _tile(m, MM_ROWS)

    def body(d_ref, w_ref, gu_ref, order_ref, o_ref):
        da = 0.5 * lax.dot_general(d_ref[...].astype(BF16), w_ref[...], (((1,), (1,)), ((), ())),
                                   preferred_element_type=F32)
        guv = gu_ref[...]
        _, vjp = jax.vjp(_f_act, guv[:, :FF_PAD], guv[:, FF_PAD:])
        o_ref[...] = jnp.concatenate(vjp(da), axis=1).astype(o_ref.dtype)

    gu_spec = pl.BlockSpec((tm, 2 * FF_PAD), lambda i, j: (i, j))
    return pl.pallas_call(
        body, name=name, grid=(m // tm, N_DEV),
        in_specs=[pl.BlockSpec((tm, k), lambda i, j: (i, 0)), pl.BlockSpec((FF_PAD, k), lambda i, j: (j, 0)), gu_spec,
                  pl.BlockSpec(memory_space=pl.ANY)],
        out_specs=gu_spec, out_shape=jax.ShapeDtypeStruct((m, N_DEV * 2 * FF_PAD), BF16),
        compiler_params=_cparams(dimension_semantics=("parallel", "parallel")),
    )(dout, w_down, gu, order)


def _ffn_fwd(tag, x, norm, w_gu, w_down):
    h = _norm_fwd(tag + "_norm", x, norm)
    gu, a = _gate_up_act(tag + "_gu", h, w_gu)
    wd = w_down(a)
    out = _mm(tag + "_down", a, wd, "nn", res=x, scale=0.5)
    return out, (h, gu, a), wd


def _ffn_bwd(tag, x, norm, w_gu, w_down, saved, dout, on_down=None):
    h, gu, a = saved
    d_wdown = _mm(tag + "_dwdown", a, dout, "tn", out_dtype=BF16, scale=0.5).reshape(N_DEV, FF_PAD, D_MODEL)
    token = on_down(d_wdown) if on_down is not None else jnp.zeros((8, LANES), F32)
    dgu = _gate_up_act_bwd(tag + "_dgu", dout, w_down, gu, token)
    dh =_mmc_nt(tag + "_dh", dgu, w_gu, 0, 0, 2 * FF_PAD, D_MODEL)
    d_wgu = _mmc_tn(tag + "_dwgu", h, dgu, 2 * FF_PAD)
    dx, dnorm = _norm_bwd(tag + "_dnorm", x, norm, dh, dout)
    return dx, dnorm, d_wgu, d_wdown


def _shift_fwd(z, mu):
    t, c = z.shape[0], RWKV_COLS
    tr = 256

    def body(z_ref, zp_ref, mu_ref, r_ref, k_ref, v_ref, lo_ref, gd_ref):
        zv = z_ref[...]
        prev = zp_ref[7:8, :] * jnp.where(pl.program_id(0) == 0, 0.0, 1.0)
        row = lax.broadcasted_iota(jnp.int32, zv.shape, 0)
        zsh = jnp.where(row == 0, prev, pltpu.roll(zv, 1, 0))
        zs = zv + (zsh - zv) * mu_ref[...]
        r_ref[...] = zs[:, 0:512]
        k_ref[...] = zs[:, 512:1024]
        v_ref[...] = zs[:, 1024:1536]
        lo_ref[...] = zs[:, 1536:1664]
        gd_ref[...] = zs[:, 1664:1792]

    widths = (512, 512, 512, 128, 128)
    return pl.pallas_call(
        body, name="rwkv_shift", grid=(t // tr,),
        in_specs=[pl.BlockSpec((tr, c), lambda i: (i, 0)),
                  pl.BlockSpec((8, c), lambda i: (jnp.maximum(i * (tr // 8) - 1, 0), 0)),
                  pl.BlockSpec((1, c), lambda i: (0, 0))],
        out_specs=[pl.BlockSpec((tr, w), lambda i: (i, 0)) for w in widths],
        out_shape=[jax.ShapeDtypeStruct((t, w), F32) for w in widths],
        compiler_params=_cparams(dimension_semantics=("parallel",)),
    )(z, z, mu)


def _shift_bwd(z, mu, dr, dk, dv, dlo, dgd):
    t, c = z.shape[0], RWKV_COLS
    tr = 256
    nt = t // tr

    def body(z_ref, zp_ref, mu_ref, dr_ref, dk_ref, dv_ref, dlo_ref, dgd_ref,
             drn_ref, dkn_ref, dvn_ref, dlon_ref, dgdn_ref, dz_ref, dmu_ref):
        i = pl.program_id(0)
        zv, muv = z_ref[...], mu_ref[...]
        prev = zp_ref[7:8, :] * jnp.where(i == 0, 0.0, 1.0)
        row = lax.broadcasted_iota(jnp.int32, zv.shape, 0)
        zsh = jnp.where(row == 0, prev, pltpu.roll(zv, 1, 0))
        dzs = jnp.concatenate([dr_ref[...], dk_ref[...], dv_ref[...], dlo_ref[...], dgd_ref[...]], axis=1)
        nxt = jnp.concatenate([drn_ref[0:1, :], dkn_ref[0:1, :], dvn_ref[0:1, :], dlon_ref[0:1, :],
                               dgdn_ref[0:1, :]], axis=1) * jnp.where(i == nt - 1, 0.0, 1.0)
        u = dzs * muv
        un = jnp.where(row == tr - 1, nxt * muv, pltpu.roll(u, tr - 1, 0))
        dz_ref[...] = (dzs - u + un).astype(dz_ref.dtype)
        _accumulate(dmu_ref, jnp.sum(dzs * (zsh - zv), axis=0, keepdims=True))

    widths = (512, 512, 512, 128, 128)
    nxt_map = lambda i: (jnp.minimum((i + 1) * (tr // 8), t // 8 - 1), 0)
    return pl.pallas_call(
        body, name="rwkv_shift_bwd", grid=(nt,),
        in_specs=[pl.BlockSpec((tr, c), lambda i: (i, 0)),
                  pl.BlockSpec((8, c), lambda i: (jnp.maximum(i * (tr // 8) - 1, 0), 0)),
                  pl.BlockSpec((1, c), lambda i: (0, 0))]
        + [pl.BlockSpec((tr, w), lambda i: (i, 0)) for w in widths]
        + [pl.BlockSpec((8, w), nxt_map) for w in widths],
        out_specs=[pl.BlockSpec((tr, c), lambda i: (i, 0)), pl.BlockSpec((1, c), lambda i: (0, 0))],
        out_shape=[jax.ShapeDtypeStruct((t, c), BF16), jax.ShapeDtypeStruct((1, c), F32)],
        compiler_params=_cparams(dimension_semantics=("arbitrary",)),
    )(z, z, mu, dr, dk, dv, dlo, dgd, dr, dk, dv, dlo, dgd)


def _f_pre(k, lo, gd, w0, w2p, a0, a2p, g2, k_k, k_a):
    lane = lax.broadcasted_iota(jnp.int32, lo.shape, 1)
    lo_act = jnp.where(lane < DECAY_LORA, jnp.tanh(lo), lo)
    w = -_softplus(-(w0 + _bdot(lo_act, w2p))) - 0.5
    a = _sigmoid(a0 + _bdot(lo_act, a2p))
    g = _bdot(_sigmoid(gd), g2)
    kk = k * k_k
    k2 = k * (1.0 + (a - 1.0) * k_a)
    decay = jnp.exp(-jnp.exp(w))
    return k2, kk, a, decay, g


def _f_kk(kk, a):
    kkn = kk * lax.rsqrt(jnp.maximum(_head_sums(kk * kk), 1e-24))
    return kkn, kkn * a


def _f_post(y, r, k2, v, g, gn_w, gn_b, r_k):
    mean = _head_sums(y) * (1.0 / HEAD_DIM)
    var = _head_sums(jnp.square(y - mean)) * (1.0 / HEAD_DIM)
    yn = (y - mean) * lax.rsqrt(var + GN_EPS) * gn_w + gn_b
    bonus = _head_sums(r * k2 * r_k) * v
    return (yn + bonus) * g


def _to_v_rows(x):
    t = x.shape[0]
    return x.reshape(t, RWKV_HEADS, V_HI, V_LO).transpose(0, 2, 3, 1).reshape(t, V_HI, LANES)


def _from_v_rows(x):
    t = x.shape[0]
    return x.reshape(t, V_HI, V_LO, RWKV_HEADS).transpose(0, 3, 1, 2).reshape(t, RWKV_DIM)


def _k_cols(x):
    return jnp.tile(x, (V_LO, 1)).T


def _k_rows(x):
    xt = x.T
    out = xt[0:RWKV_HEADS]
    for l in range(1, V_LO):
        out = out + xt[l * RWKV_HEADS:(l + 1) * RWKV_HEADS]
    return out


def _wkv_fwd(r, w, k, kk, b, v):
    t = r.shape[0]
    tc = SCAN_CHUNK
    key_spec = pl.BlockSpec((tc, RWKV_HEADS, HEAD_DIM), lambda i: (i, 0, 0))
    row_spec = pl.BlockSpec((tc, V_HI, LANES), lambda i: (i, 0, 0))

    def body(r_ref, w_ref, k_ref, kk_ref, b_ref, v_ref, y_ref, st_ref, s_scr, cols_a, cols_b):
        @pl.when(pl.program_id(0) == 0)
        def _():
            s_scr[...] = jnp.zeros_like(s_scr)

        def prep(ti, buf):
            for n, ref in enumerate((r_ref, w_ref, k_ref, kk_ref, b_ref)):
                buf[n] = _k_cols(ref[ti])

        def step(ti, s, cur, nxt, ti_next):
            rc, wc, kc, kkc, bc = (cur[n] for n in range(5))
            prep(ti_next, nxt)
            vt = v_ref[ti]
            new, ys = [], []
            for j in range(V_HI):
                sa = -jnp.sum(s[j] * kkc, axis=0, keepdims=True)
                nj = s[j] * wc + bc * sa + kc * vt[j:j + 1]
                st_ref[ti, j] = nj
                ys.append(jnp.sum(nj * rc, axis=0, keepdims=True))
                new.append(nj)
            y_ref[ti] = jnp.concatenate(ys, axis=0)
            return tuple(new)

        def pair(i, s):
            s = step(2 * i, s, cols_a, cols_b, 2 * i + 1)
            return step(2 * i + 1, s, cols_b, cols_a, jnp.minimum(2 * i + 2, tc - 1))

        prep(0, cols_a)
        s = lax.fori_loop(0, tc // 2, pair, tuple(s_scr[j] for j in range(V_HI)))
        for j in range(V_HI):
            s_scr[j] = s[j]

    return pl.pallas_call(
        body, name="wkv_fwd", grid=(t // tc,),
        in_specs=[key_spec] * 5 + [row_spec],
        out_specs=[row_spec, pl.BlockSpec((tc, V_HI, HEAD_DIM, LANES), lambda i: (i, 0, 0, 0))],
        out_shape=[jax.ShapeDtypeStruct((t, V_HI, LANES), F32),
                   jax.ShapeDtypeStruct((t, V_HI, HEAD_DIM, LANES), F32)],
        scratch_shapes=[pltpu.VMEM((V_HI, HEAD_DIM, LANES), F32)] + [pltpu.VMEM((5, HEAD_DIM, LANES), F32)] * 2,
        compiler_params=_cparams(dimension_semantics=("arbitrary",)),
    )(r, w, k, kk, b, v)


def _wkv_bwd(r, w, k, kk, b, v, states, dy):
    t = r.shape[0]
    tc = SCAN_CHUNK
    nb = t // tc
    key_spec = pl.BlockSpec((tc, RWKV_HEADS, HEAD_DIM), lambda i: (nb - 1 - i, 0, 0))
    row_spec = pl.BlockSpec((tc, V_HI, LANES), lambda i: (nb - 1 - i, 0, 0))
    st_spec = pl.BlockSpec((tc, V_HI, HEAD_DIM, LANES), lambda i: (nb - 1 - i, 0, 0, 0))
    stp_spec = pl.BlockSpec((1, V_HI, HEAD_DIM, LANES), lambda i: (jnp.maximum((nb - 1 - i) * tc - 1, 0), 0, 0, 0))

    def body(r_ref, w_ref, k_ref, kk_ref, b_ref, v_ref, st_ref, stp_ref, dy_ref,
             dr_ref, dw_ref, dk_ref, dkk_ref, db_ref, dv_ref, ds_scr, cols_a, cols_b, accs_a, accs_b):
        @pl.when(pl.program_id(0) == 0)
        def _():
            ds_scr[...] = jnp.zeros_like(ds_scr)

        def colsum(x):
            return jnp.sum(x, axis=0, keepdims=True)

        def prep(ti, buf):
            for n, ref in enumerate((r_ref, w_ref, k_ref, kk_ref, b_ref)):
                buf[n] = _k_cols(ref[ti])

        def flush(ti, buf):
            for n, ref in enumerate((dr_ref, dk_ref, db_ref, dw_ref, dkk_ref)):
                ref[ti] = _k_rows(buf[n])

        def step(ti, ds, sp, cur, accs):
            rc, wc, kc, kkc, bc = (cur[n] for n in range(5))
            vt, dyt = v_ref[ti], dy_ref[ti]
            acc = None
            new, dvs = [], []
            for j in range(V_HI):
                st = st_ref[ti, j]
                dsj = ds[j] + rc * dyt[j:j + 1]
                sa = -colsum(sp[j] * kkc)
                dsa = colsum(dsj * bc)
                dvs.append(colsum(dsj * kc))
                parts = (st * dyt[j:j + 1], dsj * vt[j:j + 1], dsj * sa, dsj * sp[j], -(sp[j] * dsa))
                acc = parts if acc is None else tuple(a + q for a, q in zip(acc, parts))
                new.append(dsj * wc - kkc * dsa)
            dv_ref[ti] = jnp.concatenate(dvs, axis=0)
            for n in range(5):
                accs[n] = acc[n]
            return tuple(new)

        def states_before(ti):
            return tuple(st_ref[ti - 1, j] for j in range(V_HI))

        def pair(i, ds):
            ta = tc - 1 - 2 * i
            prep(ta - 1, cols_b)
            flush(jnp.minimum(ta + 1, tc - 1), accs_b)
            ds = step(ta, ds, states_before(ta), cols_a, accs_a)
            prep(ta - 2, cols_a)
            flush(ta, accs_a)
            return step(ta - 1, ds, states_before(ta - 1), cols_b, accs_b)

        prep(tc - 1, cols_a)
        accs_b[...] = jnp.zeros_like(accs_b)
        ds = lax.fori_loop(0, tc // 2 - 1, pair, tuple(ds_scr[j] for j in range(V_HI)))
        prep(0, cols_b)
        flush(2, accs_b)
        ds = step(1, ds, states_before(1), cols_a, accs_a)
        flush(1, accs_a)
        keep = jnp.where(pl.program_id(0) == nb - 1, 0.0, 1.0)
        ds = step(0, ds, tuple(stp_ref[0, j] * keep for j in range(V_HI)), cols_b, accs_b)
        flush(0, accs_b)
        for j in range(V_HI):
            ds_scr[j] = ds[j]

    key_out = jax.ShapeDtypeStruct((t, RWKV_HEADS, HEAD_DIM), F32)
    return pl.pallas_call(
        body, name="wkv_bwd", grid=(nb,),
        in_specs=[key_spec] * 5 + [row_spec, st_spec, stp_spec, row_spec],
        out_specs=[key_spec] * 5 + [row_spec],
        out_shape=[key_out] * 5 + [jax.ShapeDtypeStruct((t, V_HI, LANES), F32)],
        scratch_shapes=[pltpu.VMEM((V_HI, HEAD_DIM, LANES), F32)] + [pltpu.VMEM((5, HEAD_DIM, LANES), F32)] * 4,
        compiler_params=_cparams(dimension_semantics=("arbitrary",)),
    )(r, w, k, kk, b, v, states, states, dy)


PAIR = 2 * HEAD_DIM
N_PAIRS = ATTN_HEADS // 2
Q_COL0 = RWKV_COLS // PAIR
K_COL0 = Q_COL0 + N_PAIRS
V_COL0 = K_COL0 + N_PAIRS


def _swap_halves(x):
    lane = lax.broadcasted_iota(jnp.int32, x.shape, 1)
    return jnp.where((lane & (HEAD_DIM - 1)) < HEAD_DIM // 2, pltpu.roll(x, PAIR - HEAD_DIM // 2, 1),
                     pltpu.roll(x, HEAD_DIM // 2, 1))


@jax.custom_vjp
def _rope(x, cosf, sinf):
    return x * cosf + _swap_halves(x) * sinf


def _rope_fwd(x, cosf, sinf):
    return _rope(x, cosf, sinf), (cosf, sinf)


def _rope_bwd(res, d):
    cosf, sinf = res
    return d * cosf + _swap_halves(d * sinf), jnp.zeros_like(cosf), jnp.zeros_like(sinf)


_rope.defvjp(_rope_fwd, _rope_bwd)


def _head_sums(x):
    lane = lax.broadcasted_iota(jnp.int32, x.shape, 1)
    lo = jnp.where(lane < HEAD_DIM, 1.0, 0.0)
    hi = 1.0 - lo
    return lo * jnp.sum(x * lo, axis=1, keepdims=True) + hi * jnp.sum(x * hi, axis=1, keepdims=True)


def _f_qk(x, cosf, sinf, gain2):
    xn = x * lax.rsqrt(_head_sums(x * x) * (1.0 / HEAD_DIM) + RMS_EPS) * gain2
    return _rope(xn, cosf, sinf)


def _qk_prep(z, tab, q_gain, k_gain):
    t = z.shape[0]
    tr = 1024

    def body(z_ref, c_ref, s_ref, qg_ref, kg_ref, o_ref):
        g = jnp.where(pl.program_id(0) < N_PAIRS, qg_ref[...], kg_ref[...])
        o_ref[...] = _f_qk(z_ref[...], c_ref[...], s_ref[...], jnp.concatenate([g, g], axis=1))

    gain = pl.BlockSpec((1, HEAD_DIM), lambda c, i: (0, 0))
    return pl.pallas_call(
        body, name="qk_prep", grid=(2 * N_PAIRS, t // tr),
        in_specs=[pl.BlockSpec((tr, PAIR), lambda c, i: (i, Q_COL0 + c)), pl.BlockSpec((tr, PAIR), lambda c, i: (i, 0)),
                  pl.BlockSpec((tr, PAIR), lambda c, i: (i, 1)), gain, gain],
        out_specs=pl.BlockSpec((tr, PAIR), lambda c, i: (i, c)),
        out_shape=jax.ShapeDtypeStruct((t, 2 * N_PAIRS * PAIR), F32),
        compiler_params=_cparams(dimension_semantics=("parallel", "parallel")),
    )(z, tab, tab, q_gain, k_gain)


def _qk_prep_bwd(z, tab, q_gain, k_gain, dq, dk):
    t = z.shape[0]
    tr = 1024

    def body(z_ref, c_ref, s_ref, qg_ref, kg_ref, dq_ref, dk_ref, dz_ref, dqg_ref, dkg_ref):
        c, i = pl.program_id(0), pl.program_id(1)
        is_q = c < N_PAIRS
        g = jnp.where(is_q, qg_ref[...], kg_ref[...])
        d = jnp.where(is_q, dq_ref[...], dk_ref[...])
        _, vjp = jax.vjp(lambda xx, gg: _f_qk(xx, c_ref[...], s_ref[...], gg), z_ref[...],
                         jnp.concatenate([g, g], axis=1))
        dx, dg2 = vjp(d)
        dz_ref[...] = dx.astype(dz_ref.dtype)
        dg = dg2[:, :HEAD_DIM] + dg2[:, HEAD_DIM:]
        first_q = jnp.logical_and(c == 0, i == 0)
        first_k = jnp.logical_and(c == N_PAIRS, i == 0)

        @pl.when(first_q)
        def _():
            dqg_ref[...] = dg

        @pl.when(jnp.logical_and(is_q, jnp.logical_not(first_q)))
        def _():
            dqg_ref[...] += dg

        @pl.when(first_k)
        def _():
            dkg_ref[...] = dg

        @pl.when(jnp.logical_and(jnp.logical_not(is_q), jnp.logical_not(first_k)))
        def _():
            dkg_ref[...] += dg

    gain = pl.BlockSpec((1, HEAD_DIM), lambda c, i: (0, 0))
    return pl.pallas_call(
        body, name="qk_prep_bwd", grid=(2 * N_PAIRS, t // tr),
        in_specs=[pl.BlockSpec((tr, PAIR), lambda c, i: (i, Q_COL0 + c)), pl.BlockSpec((tr, PAIR), lambda c, i: (i, 0)),
                  pl.BlockSpec((tr, PAIR), lambda c, i: (i, 1)), gain, gain,
                  pl.BlockSpec((tr, PAIR), lambda c, i: (i, jnp.minimum(c, N_PAIRS - 1))),
                  pl.BlockSpec((tr, PAIR), lambda c, i: (i, jnp.maximum(c - N_PAIRS, 0)))],
        out_specs=[pl.BlockSpec((tr, PAIR), lambda c, i: (i, c)), gain, gain],
        out_shape=[jax.ShapeDtypeStruct((t, 2 * N_PAIRS * PAIR), BF16), jax.ShapeDtypeStruct((1, HEAD_DIM), F32),
                   jax.ShapeDtypeStruct((1, HEAD_DIM), F32)],
        compiler_params=_cparams(dimension_semantics=("arbitrary", "arbitrary")),
    )(z, tab, tab, q_gain, k_gain, dq, dk)


def _attn_block(q, kp, kc, vp, vc, kmin):
    k2 = jnp.concatenate([kp, kc], axis=0)
    v2 = jnp.concatenate([vp, vc], axis=0)
    s = _bdot_nt(q, k2) * (HEAD_DIM ** -0.5)
    qi = lax.broadcasted_iota(jnp.int32, s.shape, 0)
    kj = lax.broadcasted_iota(jnp.int32, s.shape, 1)
    dist = qi + BAND_BLOCK - kj
    valid = (dist >= 0) & (dist <= BAND_BLOCK) & (kj >= kmin)
    s = jnp.where(valid, s, NEG_INF)
    m = lax.stop_gradient(jnp.max(s, axis=-1, keepdims=True))
    e = jnp.exp(s - m)
    l = jnp.sum(e, axis=-1, keepdims=True)
    o = _bdot(e, v2) / l
    return o, m + jnp.log(l)


def _fold(src_ref, dst_ref, dil):
    t = src_ref.shape[0]
    ln = t // dil
    for j in range(dil):
        dst_ref[j * ln:(j + 1) * ln, :] = src_ref[pl.ds(j, ln, stride=dil), :]


def _unfold(src_ref, dst_ref, dil):
    t = src_ref.shape[0]
    ln = t // dil
    for j in range(dil):
        dst_ref[pl.ds(j, ln, stride=dil), :] = src_ref[j * ln:(j + 1) * ln, :]


def _per_group(fn):
    pair = pl.program_id(0)
    for gi, (_, dil) in enumerate(ATTN_GROUPS):
        @pl.when(jnp.logical_or(pair == 2 * gi, pair == 2 * gi + 1))
        def _(dil=dil):
            fn(dil)


def _block_rows(idx, blocks_per_seq):
    first = (idx & (blocks_per_seq - 1)) == 0
    cur = pl.ds(pl.multiple_of(idx * BAND_BLOCK, BAND_BLOCK), BAND_BLOCK)
    prev = pl.ds(pl.multiple_of(jnp.maximum(idx - 1, 0) * BAND_BLOCK, BAND_BLOCK), BAND_BLOCK)
    return first, cur, prev


def _heads(x):
    return x[:, :HEAD_DIM], x[:, HEAD_DIM:]


def _attn_fwd(qk, z):
    t = z.shape[0]
    n_blocks = t // BAND_BLOCK

    def body(q_ref, k_ref, v_ref, o_ref, lse_ref, qf, kf, vf, of, lf):
        def run(dil):
            _fold(q_ref, qf, dil)
            _fold(k_ref, kf, dil)
            _fold(v_ref, vf, dil)
            blocks_per_seq = n_blocks // dil

            def block(idx, carry):
                first, cur, prev = _block_rows(idx, blocks_per_seq)
                kmin = jnp.where(first, BAND_BLOCK, 0)
                outs, lses = [], []
                for q, kp, kc, vp, vc in zip(_heads(qf[cur, :]), _heads(kf[prev, :]), _heads(kf[cur, :]),
                                             _heads(vf[prev, :]), _heads(vf[cur, :])):
                    o, ls = _attn_block(q, kp, kc, vp, vc, kmin)
                    outs.append(o)
                    lses.append(jnp.broadcast_to(ls, o.shape))
                of[cur, :] = jnp.concatenate(outs, axis=1)
                lf[cur, :] = jnp.concatenate(lses, axis=1)
                return carry

            lax.fori_loop(0, n_blocks, block, 0, unroll=4)
            _unfold(of, o_ref, dil)
            _unfold(lf, lse_ref, dil)

        _per_group(run)

    slab = jax.ShapeDtypeStruct((t, N_PAIRS * PAIR), F32)
    out_spec = pl.BlockSpec((t, PAIR), lambda p: (0, p))
    return pl.pallas_call(
        body, name="attn_fwd", grid=(N_PAIRS,),
        in_specs=[pl.BlockSpec((t, PAIR), lambda p: (0, p)), pl.BlockSpec((t, PAIR), lambda p: (0, N_PAIRS + p)),
                  pl.BlockSpec((t, PAIR), lambda p: (0, V_COL0 + p))],
        out_specs=[out_spec, out_spec], out_shape=[slab, slab],
        scratch_shapes=[pltpu.VMEM((t, PAIR), F32)] * 5,
        compiler_params=_cparams(dimension_semantics=("parallel",)),
    )(qk, qk, z)


def _attn_bwd(qk, z, do, dlse):
    t = z.shape[0]
    n_blocks = t // BAND_BLOCK

    def body(q_ref, k_ref, v_ref, do_ref, dl_ref, dq_ref, dk_ref, dv_ref, qf, kf, vf, dof, dlf, dqf, dkf, dvf,
             dkpf, dvpf):
        def run(dil):
            for src, dst in ((q_ref, qf), (k_ref, kf), (v_ref, vf), (do_ref, dof), (dl_ref, dlf)):
                _fold(src, dst, dil)
            blocks_per_seq = n_blocks // dil

            def block(idx, carry):
                first, cur, prev = _block_rows(idx, blocks_per_seq)
                kmin = jnp.where(first, BAND_BLOCK, 0)
                grads = []
                for q, kp, kc, vp, vc, do_h, dl_h in zip(
                        _heads(qf[cur, :]), _heads(kf[prev, :]), _heads(kf[cur, :]), _heads(vf[prev, :]),
                        _heads(vf[cur, :]), _heads(dof[cur, :]), _heads(dlf[cur, :])):
                    _, vjp = jax.vjp(functools.partial(_attn_block, kmin=kmin), q, kp, kc, vp, vc)
                    grads.append(vjp((do_h, jnp.sum(dl_h, axis=1, keepdims=True))))
                dq, dkp, dkc, dvp, dvc = (jnp.concatenate([a, b], axis=1) for a, b in zip(*grads))
                dqf[cur, :], dkf[cur, :], dvf[cur, :], dkpf[cur, :], dvpf[cur, :] = dq, dkc, dvc, dkp, dvp
                return carry

            lax.fori_loop(0, n_blocks, block, 0, unroll=2)

            def join(idx, carry):
                first, cur, prev = _block_rows(idx, blocks_per_seq)

                @pl.when(jnp.logical_not(first))
                def _():
                    dkf[prev, :] += dkpf[cur, :]
                    dvf[prev, :] += dvpf[cur, :]

                return carry

            lax.fori_loop(0, n_blocks, join, 0)
            _unfold(dqf, dq_ref, dil)
            _unfold(dkf, dk_ref, dil)
            _unfold(dvf, dv_ref, dil)

        _per_group(run)

    slab = jax.ShapeDtypeStruct((t, N_PAIRS * PAIR), F32)
    own = pl.BlockSpec((t, PAIR), lambda p: (0, p))
    return pl.pallas_call(
        body, name="attn_bwd", grid=(N_PAIRS,),
        in_specs=[own, pl.BlockSpec((t, PAIR), lambda p: (0, N_PAIRS + p)),
                  pl.BlockSpec((t, PAIR), lambda p: (0, V_COL0 + p)), own, own],
        out_specs=[own] * 3, out_shape=[slab] * 3,
        scratch_shapes=[pltpu.VMEM((t, PAIR), F32)] * 10,
        compiler_params=_cparams(dimension_semantics=("parallel",)),
    )(qk, qk, z, do, dlse)


def _f_comb(o1, o2, o3, l1, l2, l3):
    m = jnp.maximum(jnp.maximum(l1, l2), l3)
    e1, e2, e3 = jnp.exp(l1 - m), jnp.exp(l2 - m), jnp.exp(l3 - m)
    den = e1 + e2 + e3
    return (e1 / den) * o1 + (e2 / den) * o2 + (e3 / den) * o3


def _all_gather_hbm(name, arrs):
    na = len(arrs)

    def body(*refs):
        x_refs, out_refs = refs[:na], refs[na:2 * na]
        send_sems, recv_sems, local_sems = refs[2 * na:]
        mx, my, mc = lax.axis_index("x"), lax.axis_index("y"), lax.axis_index("c")
        me, sibling = (mx, my, mc), (mx, my, 1 - mc)
        chips = [(1 - mx, my), (mx, 1 - my), (1 - mx, 1 - my)]

        def slot(a, px, py, pc):
            return out_refs[a].at[4 * px + 2 * py + pc]

        def copy(a, k, block, to, src=None):
            return pltpu.make_async_remote_copy(
                src_ref=slot(a, *block) if src is None else src, dst_ref=slot(a, *block),
                send_sem=send_sems.at[a, k], recv_sem=recv_sems.at[a, k], device_id=to, device_id_type=MESH)

        mine = [pltpu.make_async_copy(x_refs[a], slot(a, *me), local_sems.at[a]) for a in range(na)]
        for cp in mine:
            cp.start()
        first = []
        for a in range(na):
            first.append(copy(a, 0, me, sibling, src=x_refs[a]))
            first += [copy(a, 1 + j, me, (*chip, mc), src=x_refs[a]) for j, chip in enumerate(chips)]
        for cp in first:
            cp.start()
        passed = []
        for j, chip in enumerate(chips):
            for a in range(na):
                copy(a, 1 + j, (*chip, mc), me).wait_recv()
                passed.append(copy(a, 4 + j, (*chip, mc), sibling))
                passed[-1].start()
        for a in range(na):
            copy(a, 0, sibling, me).wait_recv()
            for j, chip in enumerate(chips):
                copy(a, 4 + j, (*chip, 1 - mc), me).wait_recv()
        for cp in first + passed:
            cp.wait_send()
        for cp in mine:
            cp.wait()

    hbm = pl.BlockSpec(memory_space=pl.ANY)
    return pl.pallas_call(
        body, name=name,
        out_shape=[jax.ShapeDtypeStruct((N_DEV,) + a.shape, a.dtype) for a in arrs],
        in_specs=[hbm] * na, out_specs=[hbm] * na,
        scratch_shapes=[pltpu.SemaphoreType.DMA((na, 7)), pltpu.SemaphoreType.DMA((na, 7)),
                        pltpu.SemaphoreType.DMA((na,))],
    )(*arrs)


def _all_gather_vmem(x):
    rws, cols = x.shape

    def body(x_ref, out_ref, send_sems, recv_sems):
        mx, my, mc = lax.axis_index("x"), lax.axis_index("y"), lax.axis_index("c")
        me, sibling = (mx, my, mc), (mx, my, 1 - mc)
        chips = [(1 - mx, my), (mx, 1 - my), (1 - mx, 1 - my)]

        def slot(px, py, pc):
            return out_ref.at[4 * px + 2 * py + pc]

        def copy(k, block, to, src=None):
            return pltpu.make_async_remote_copy(
                src_ref=slot(*block) if src is None else src, dst_ref=slot(*block),
                send_sem=send_sems.at[k], recv_sem=recv_sems.at[k], device_id=to, device_id_type=MESH)

        first = [copy(0, me, sibling, src=x_ref)]
        first += [copy(1 + j, me, (*chip, mc), src=x_ref) for j, chip in enumerate(chips)]
        for cp in first:
            cp.start()
        out_ref[4 * mx + 2 * my + mc] = x_ref[...]
        passed = [copy(4 + j, (*chip, mc), sibling) for j, chip in enumerate(chips)]
        for j, chip in enumerate(chips):
            copy(1 + j, (*chip, mc), me).wait_recv()
            passed[j].start()
        copy(0, sibling, me).wait_recv()
        for j, chip in enumerate(chips):
            copy(4 + j, (*chip, 1 - mc), me).wait_recv()
        for cp in first + passed:
            cp.wait_send()

    return pl.pallas_call(
        body, name="all_gather_small",
        out_shape=jax.ShapeDtypeStruct((N_DEV, rws, cols), x.dtype),
        in_specs=[pl.BlockSpec(memory_space=pltpu.VMEM)], out_specs=pl.BlockSpec(memory_space=pltpu.VMEM),
        scratch_shapes=[pltpu.SemaphoreType.DMA((7,)), pltpu.SemaphoreType.DMA((7,))],
    )(x)


def _scatter_copies(g_refs, land_refs, send_sems, recv_sems):
    mx, my, mc = lax.axis_index("x"), lax.axis_index("y"), lax.axis_index("c")
    me = 4 * mx + 2 * my + mc
    copies = []
    for k in range(1, N_DEV):
        px, py, pc = mx ^ (k >> 2), my ^ ((k >> 1) & 1), mc ^ (k & 1)
        peer = 4 * px + 2 * py + pc
        for a, (g_ref, land_ref) in enumerate(zip(g_refs, land_refs)):
            copies.append(pltpu.make_async_remote_copy(
                src_ref=g_ref.at[peer], dst_ref=land_ref.at[me], send_sem=send_sems.at[a * (N_DEV - 1) + k - 1],
                recv_sem=recv_sems.at[a * (N_DEV - 1) + k - 1], device_id=(px, py, pc), device_id_type=MESH))
    return copies


_HBM = pl.BlockSpec(memory_space=pltpu.HBM)
_SEM = pl.BlockSpec(memory_space=pltpu.SEMAPHORE)
_DATAFLOW = pltpu.SideEffectType.DATAFLOW_SIDE_EFFECTING


def _gather_copies(x_refs, land_refs, send_sems, recv_sems):
    mx, my, mc = lax.axis_index("x"), lax.axis_index("y"), lax.axis_index("c")
    me = 4 * mx + 2 * my + mc
    copies = []
    for k in range(1, N_DEV):
        px, py, pc = mx ^ (k >> 2), my ^ ((k >> 1) & 1), mc ^ (k & 1)
        for a, (x_ref, land_ref) in enumerate(zip(x_refs, land_refs)):
            copies.append(pltpu.make_async_remote_copy(
                src_ref=x_ref, dst_ref=land_ref.at[me], send_sem=send_sems.at[a * (N_DEV - 1) + k - 1],
                recv_sem=recv_sems.at[a * (N_DEV - 1) + k - 1], device_id=(px, py, pc), device_id_type=MESH))
    return copies


N_CHIPS = N_DEV // 2


def _gather2_first(x_refs, land_refs, send_sems, recv_sems, ks=(0, 1, 2, 3)):
    mx, my, mc = lax.axis_index("x"), lax.axis_index("y"), lax.axis_index("c")
    me = 4 * mx + 2 * my + mc
    peers = [(mx, my, 1 - mc), (1 - mx, my, mc), (mx, 1 - my, mc), (1 - mx, 1 - my, mc)]
    return [pltpu.make_async_remote_copy(
        src_ref=x_ref, dst_ref=land_ref.at[me], send_sem=send_sems.at[a * (N_DEV - 1) + k],
        recv_sem=recv_sems.at[a * (N_DEV - 1) + k], device_id=peers[k], device_id_type=MESH)
        for a, (x_ref, land_ref) in enumerate(zip(x_refs, land_refs)) for k in ks]


def _gather2_second(x_refs, land_refs, send_sems, recv_sems):
    mx, my, mc = lax.axis_index("x"), lax.axis_index("y"), lax.axis_index("c")
    copies = []
    for a, land_ref in enumerate(land_refs):
        for j, (px, py) in enumerate([(1 - mx, my), (mx, 1 - my), (1 - mx, 1 - my)]):
            slot = 4 * px + 2 * py + mc
            copies.append(pltpu.make_async_remote_copy(
                src_ref=land_ref.at[slot], dst_ref=land_ref.at[slot], send_sem=send_sems.at[a * (N_DEV - 1) + j],
                recv_sem=recv_sems.at[a * (N_DEV - 1) + j], device_id=(mx, my, 1 - mc), device_id_type=MESH))
    return copies


def _pair_swap(name, arrs):
    na = len(arrs)

    def body(*refs):
        g_refs, out_refs = refs[:na], refs[na:2 * na]
        send_sems, recv_sems = refs[2 * na:]
        mx, my, mc = lax.axis_index("x"), lax.axis_index("y"), lax.axis_index("c")
        copies = [pltpu.make_async_remote_copy(
            src_ref=g_refs[a].at[2 * q + 1 - mc], dst_ref=out_refs[a].at[q], send_sem=send_sems.at[a, q],
            recv_sem=recv_sems.at[a, q], device_id=(mx, my, 1 - mc), device_id_type=MESH)
            for a in range(na) for q in range(N_CHIPS)]
        for cp in copies:
            cp.start()
        for cp in copies:
            cp.wait_recv()
        for cp in copies:
            cp.wait_send()

    hbm = pl.BlockSpec(memory_space=pl.ANY)
    return pl.pallas_call(
        body, name=name,
        out_shape=[jax.ShapeDtypeStruct((N_CHIPS,) + a.shape[1:], a.dtype) for a in arrs],
        in_specs=[hbm] * na, out_specs=[hbm] * na,
        scratch_shapes=[pltpu.SemaphoreType.DMA((na, N_CHIPS)), pltpu.SemaphoreType.DMA((na, N_CHIPS))],
    )(*arrs)


def _pair_add(name, g, half, core):
    _, r, c = g.shape
    tr = _tile(r, 512) if r % LANES == 0 else r

    def body(core_ref, g_ref, h_ref, o_ref):
        o_ref[...] = (g_ref[...].astype(F32) + h_ref[...].astype(F32)).astype(o_ref.dtype)

    return pl.pallas_call(
        body, name=name,
        grid_spec=pltpu.PrefetchScalarGridSpec(
            num_scalar_prefetch=1, grid=(N_CHIPS, r // tr),
            in_specs=[pl.BlockSpec((None, tr, c), lambda q, i, core_ref: (2 * q + core_ref[0], i, 0)),
                      pl.BlockSpec((None, tr, c), lambda q, i, core_ref: (q, i, 0))],
            out_specs=pl.BlockSpec((None, tr, c), lambda q, i, core_ref: (q, i, 0))),
        out_shape=jax.ShapeDtypeStruct((N_CHIPS, r, c), g.dtype),
        compiler_params=_cparams(dimension_semantics=("parallel", "parallel")),
    )(core, g, half)


def _chip_copies(h_refs, land_refs, send_sems, recv_sems):
    mx, my, mc = lax.axis_index("x"), lax.axis_index("y"), lax.axis_index("c")
    my_chip = 2 * mx + my
    copies = []
    for k in range(1, N_CHIPS):
        px, py = mx ^ (k >> 1), my ^ (k & 1)
        for a, (h_ref, land_ref) in enumerate(zip(h_refs, land_refs)):
            copies.append(pltpu.make_async_remote_copy(
                src_ref=h_ref.at[2 * px + py], dst_ref=land_ref.at[my_chip], send_sem=send_sems.at[a * (N_DEV - 1) + k - 1],
                recv_sem=recv_sems.at[a * (N_DEV - 1) + k - 1], device_id=(px, py, mc), device_id_type=MESH))
    return copies


def _exchange_start(name, copies, srcs, lands, after):
    na = len(srcs)

    def body(*refs):
        for cp in copies(refs[:na], refs[na:2 * na], refs[2 * na + 1], refs[2 * na + 2]):
            cp.start()
        refs[-1][...] = jnp.zeros_like(refs[-1])

    in_hbm = lambda a: pltpu.with_memory_space_constraint(a, pltpu.HBM)
    outs = pl.pallas_call(
        body, name=name,
        out_shape=(pltpu.SemaphoreType.DMA((na * (N_DEV - 1),)), pltpu.SemaphoreType.DMA((na * (N_DEV - 1),)),
                   *[pltpu.HBM(a.shape, a.dtype) for a in list(srcs) + list(lands)],
                   jax.ShapeDtypeStruct((8, LANES), F32)),
        in_specs=[_HBM] * (2 * na) + [pl.BlockSpec(memory_space=pl.ANY)],
        out_specs=(_SEM, _SEM, *[_HBM] * (2 * na), pl.BlockSpec(memory_space=pltpu.VMEM)),
        input_output_aliases={i: 2 + i for i in range(2 * na)},
        compiler_params=pltpu.CompilerParams(has_side_effects=_DATAFLOW),
    )(*[in_hbm(a) for a in srcs], *[in_hbm(a) for a in lands], after)
    return outs[0], outs[1], outs[2:2 + na], outs[2 + na:2 + 2 * na], outs[-1]


def _exchange_wait(name, copies, send_sems, recv_sems, srcs, lands, after, parts=("send", "recv")):
    na = len(srcs)

    def body(*refs):
        for cp in copies(refs[:na], refs[na:2 * na], refs[2 * na], refs[2 * na + 1]):
            if "send" in parts:
                cp.wait_send()
            if "recv" in parts:
                cp.wait_recv()

    outs = pl.pallas_call(
        body, name=name,
        out_shape=[pltpu.HBM(a.shape, a.dtype) for a in list(srcs) + list(lands)],
        in_specs=[_HBM] * (2 * na) + [_SEM, _SEM, pl.BlockSpec(memory_space=pl.ANY)], out_specs=[_HBM] * (2 * na),
        input_output_aliases={i: i for i in range(2 * na)},
        compiler_params=pltpu.CompilerParams(has_side_effects=_DATAFLOW),
    )(*srcs, *lands, send_sems, recv_sems, after)
    return outs[:na], outs[na:]


def _sum_slots(name, g, tr):
    _, rws, cols = g.shape

    def body(g_ref, o_ref):
        acc = g_ref[0].astype(F32)
        for j in range(1, N_DEV):
            acc = acc + g_ref[j].astype(F32)
        o_ref[...] = acc

    return pl.pallas_call(
        body, name=name, grid=(rws // tr,),
        in_specs=[pl.BlockSpec((N_DEV, tr, cols), lambda i: (0, i, 0))],
        out_specs=pl.BlockSpec((tr, cols), lambda i: (i, 0)),
        out_shape=jax.ShapeDtypeStruct((rws, cols), F32),
        compiler_params=_cparams(dimension_semantics=("parallel",)),
    )(g)


def _adam_math(wv, gv, mv, vv):
    mn = ADAM_B1 * mv + (1.0 - ADAM_B1) * gv
    vn = ADAM_B2 * vv + (1.0 - ADAM_B2) * jnp.square(gv)
    m_hat = mn / (1.0 - ADAM_B1 ** ADAM_STEP)
    v_hat = vn / (1.0 - ADAM_B2 ** ADAM_STEP)
    delta = -ADAM_LR * (m_hat / (jnp.sqrt(v_hat) + ADAM_EPS) + ADAM_WD * wv)
    return delta, mn, vn


def _adamw(name, w, g, m, v, tr):
    return _rowwise(name, _adam_math, [w, g, m, v], [], [(LANES, F32)] * 3, tr=tr)


def _adamw_slots(name, recv, own, slot, rb, cb, cw, w, m, v, tr, order):
    nr, nc = w.shape
    n = recv.shape[0]

    def body(slot_ref, g_ref, own_ref, w_ref, m_ref, v_ref, order_ref, go_ref, d_ref, mo_ref, vo_ref):
        acc = None
        for s in range(n):
            part = jnp.where(slot_ref[0] == s, own_ref[...], g_ref[s]).astype(F32)
            acc = part if acc is None else acc + part
        g = acc[:, :nc]
        go_ref[...] = g
        d_ref[...], mo_ref[...], vo_ref[...] = _adam_math(w_ref[...], g, m_ref[...], v_ref[...])

    nat = pl.BlockSpec((tr, nc), lambda i, slot_ref: (i, 0))
    return pl.pallas_call(
        body, name=name,
        grid_spec=pltpu.PrefetchScalarGridSpec(
            num_scalar_prefetch=1, grid=(nr // tr,),
            in_specs=[pl.BlockSpec((n, tr, cw), lambda i, slot_ref: (0, rb + i, cb)),
                      pl.BlockSpec((None, tr, cw), lambda i, slot_ref: (slot_ref[0], rb + i, cb)),
                      nat, nat, nat, pl.BlockSpec(memory_space=pl.ANY)],
            out_specs=[nat] * 4),
        out_shape=[jax.ShapeDtypeStruct((nr, nc), F32)] * 4,
        compiler_params=_cparams(dimension_semantics=("parallel",)),
    )(slot, recv, own, w, m, v, order)


def _local_blocks(w):
    pad_cols = lambda a: jnp.pad(a, ((0, 0), (0, FF_PAD - FF_SHARD)))
    pad_rows = lambda a: jnp.pad(a, ((0, FF_PAD - FF_SHARD), (0, 0)))
    gate_up = lambda tag: jnp.concatenate([pad_cols(w[tag + "_w_gate"]), pad_cols(w[tag + "_w_up"])], axis=1)
    blocks = {
        "ffn1_gu": gate_up("ffn1"), "ffn1_d": pad_rows(w["ffn1_w_down"]), "w_in": w["w_in"],
        "lora": jnp.concatenate([w["rwkv_w2"], w["rwkv_a2"], w["rwkv_g2"]], axis=0),
        "br": jnp.concatenate([w["w_br_rwkv"], w["w_br_attn"], w["ple_w_proj"]], axis=0),
        "w_out": w["w_out"], "ffn2_gu": gate_up("ffn2"), "ffn2_d": pad_rows(w["ffn2_w_down"]),
        "ple_gate": w["ple_w_gate"],
    }
    return {n: a.astype(BF16) for n, a in blocks.items()}


GATHER_GROUPS = {"head": ("ffn1_gu", "ffn1_d"), "mid": ("w_in", "lora"),
                 "rest": ("br", "w_out", "ffn2_gu", "ffn2_d", "ple_gate")}

SCATTER_GROUPS = {"tail": ("ple_gate", "ple_proj", "ffn2_gu", "ffn2_d"), "branch": ("w_out", "br"),
                  "mixer": ("lora", "w_in"),
                  "ffn1_down": ("ffn1_d",), "head": ("ffn1_gu",)}

TWO_LEVEL = ("mixer", "head")

ADAM_PLAN = (
    ("ffn1_w_gate", "ffn1_gu", 0, 0, FF_PAD, 256), ("ffn1_w_up", "ffn1_gu", 0, 1, FF_PAD, 256),
    ("ffn1_w_down", "ffn1_d", 0, 0, D_MODEL, FF_SHARD // 2), ("w_in", "w_in", 0, 0, IN_SHARD, 256),
    ("rwkv_w2", "lora", 0, 0, HEAD_DIM, 64), ("rwkv_a2", "lora", 1, 0, HEAD_DIM, 64),
    ("rwkv_g2", "lora", 2, 0, HEAD_DIM, 64),
    ("w_br_rwkv", "br", 0, 0, OUT_SHARD, 256), ("w_br_attn", "br", 2, 0, OUT_SHARD, 256),
    ("ple_w_proj", "ple_proj", 0, 0, OUT_SHARD, 256), ("w_out", "w_out", 0, 0, D_MODEL, OUT_SHARD),
    ("ffn2_w_gate", "ffn2_gu", 0, 0, FF_PAD, 256), ("ffn2_w_up", "ffn2_gu", 0, 1, FF_PAD, 256),
    ("ffn2_w_down", "ffn2_d", 0, 0, D_MODEL, FF_SHARD // 2), ("ple_w_gate", "ple_gate", 0, 0, D_MODEL, OUT_SHARD),
)


def _pack_small(arrs, rows):
    flat = jnp.concatenate([a.reshape(-1) for a in arrs])
    return jnp.pad(flat, (0, rows * LANES - flat.shape[0])).reshape(rows, LANES)


def _unpack_small(flat, like):
    flat = flat.reshape(-1)
    out, off = [], 0
    for a in like:
        out.append(flat[off:off + a.size].reshape(a.shape))
        off += a.size
    return out


def _local_step(x, p, pos, target, sm, wg, fetch, on_grads):
    t = x.shape[0]

    w_d1 = wg["ffn1_d"].reshape(FF_HID, D_MODEL)
    x1, ffn1_saved, _ = _ffn_fwd("ffn1", x, sm["ffn1_norm"], wg["ffn1_gu"], lambda after: w_d1)
    fetch("forward", x1)
    h2 = _norm_fwd("mix_norm", x1, sm["mix_norm"])
    wg = {**wg, **fetch("mid", h2)}
    full_cols = lambda blk: blk.transpose(1, 0, 2).reshape(blk.shape[1], N_DEV * blk.shape[2])
    lora_w2 = full_cols(wg["lora"][:, :DECAY_LORA])
    lora_a2 = full_cols(wg["lora"][:, DECAY_LORA:DECAY_LORA + ICLR_LORA])
    lora_g2 = full_cols(wg["lora"][:, DECAY_LORA + ICLR_LORA:])
    z =_mmc_nn("w_in", h2, wg["w_in"], 0, 0, IN_SHARD)
    z_g = (z, 2 * D_MODEL, (RWKV_COLS + 3 * ATTN_DIM) // (2 * D_MODEL))

    r, k, v, lo, gd = _shift_fwd(z, sm["rwkv_mu"])
    zero_lo = jnp.zeros((DECAY_LORA, RWKV_DIM), BF16)
    w2p = jnp.concatenate([lora_w2, zero_lo], axis=0).astype(F32)
    a2p = jnp.concatenate([zero_lo, lora_a2], axis=0).astype(F32)
    pre_params = [sm["rwkv_w0"], w2p, sm["rwkv_a0"], a2p, lora_g2.astype(F32), sm["rwkv_k_k"], sm["rwkv_k_a"]]
    wide = [(RWKV_DIM, F32)]
    k2, kk, a, decay, g = _rowwise("rwkv_pre", _f_pre, [k, lo, gd], pre_params, wide * 5)
    kkn, b = _pairwise("rwkv_kk", _f_kk, [kk, a], [], 2)
    scan_in = [u.reshape(t, RWKV_HEADS, HEAD_DIM) for u in (r, decay, k2, kkn, b)]
    v_rows = _to_v_rows(v)
    y_rows, states = _wkv_fwd(*scan_in, v_rows)
    y = _from_v_rows(y_rows)
    post_params = [sm["rwkv_gn_w"], sm["rwkv_gn_b"], sm["rwkv_r_k"]]
    post_rows = [y, r, k2, v, g]
    y_rwkv = _pairwise("rwkv_post", lambda *av: (_f_post(*av),), post_rows, post_params, 1)[0]

    inv_freq = 1.0 / (ROPE_THETA ** (jnp.arange(0, HEAD_DIM, 2, dtype=F32) / HEAD_DIM))
    freq2 = jnp.tile(inv_freq, 2 * PAIR // HEAD_DIM).reshape(1, PAIR)
    half = jnp.ones((HEAD_DIM // 2,), F32)
    sign2 = jnp.tile(jnp.concatenate([-half, half]), PAIR // HEAD_DIM).reshape(1, PAIR)

    def rope_table(posv, fr, sg):
        ang = posv * fr
        return (jnp.concatenate([jnp.cos(ang), jnp.sin(ang) * sg], axis=1),)
    tab = _rowwise("rope_table", rope_table, [pos.astype(F32).reshape(t, 1)], [freq2, sign2], [(2 * PAIR, F32)])[0]
    qk = _qk_prep(z, tab, sm["q_norm"], sm["k_norm"])
    o_all, lse_all = _attn_fwd(qk, z)
    gw = HEADS_PER_GROUP * HEAD_DIM

    def by_group(ov, lv):
        return [ov[:, i * gw:(i + 1) * gw] for i in range(3)] + [lv[:, i * gw:(i + 1) * gw] for i in range(3)]
    y_attn = _rowwise("attn_comb", lambda ov, lv: (_f_comb(*by_group(ov, lv)),), [o_all, lse_all], [], [(gw, F32)])[0]

    wg = {**wg, **fetch("rest", y_rwkv)}
    w_d2 = wg["ffn2_d"].reshape(FF_HID, D_MODEL)
    w_out = wg["w_out"].reshape(D_MODEL, D_MODEL)
    w_pg = wg["ple_gate"].reshape(D_MODEL, D_MODEL)
    w_brr = full_cols(wg["br"][:, :RWKV_DIM])
    w_bra = full_cols(wg["br"][:, RWKV_DIM:RWKV_DIM + gw])
    w_pp = full_cols(wg["br"][:, RWKV_DIM + gw:])
    u_r = _mm("br_rwkv", y_rwkv, w_brr, "nn")
    u_a = _mm("br_attn", y_attn, w_bra, "nn")

    def f_merge(zgr, zga, ur, ua):
        return _sigmoid(zgr) * ur + _sigmoid(zga) * ua
    merged = _rowwise("merge", lambda zg, ur, ua: (f_merge(zg[:, :D_MODEL], zg[:, D_MODEL:], ur, ua),),
                      [z_g, u_r, u_a], [], [(D_MODEL, BF16)])[0]
    x2 = _mm("w_out", merged, w_out, "nn", res=x1)
    x3, ffn2_saved, _ = _ffn_fwd("ffn2", x2, sm["ffn2_norm"], wg["ffn2_gu"], lambda after: w_d2)

    hn = _norm_fwd("ple_norm", x3, sm["ple_norm"])
    gz = _mm("ple_gate", hn, w_pg, "nn")
    pp = _mm("ple_proj", p, w_pp, "nn")

    def f_head(x3v, gzv, ppv, tg):
        sg = _sigmoid(gzv)
        err = x3v + sg * ppv - tg
        part = 0.5 * jnp.sum(jnp.mean(err * err, axis=-1, keepdims=True))
        dx4 = err * (1.0 / D_MODEL)
        return dx4, dx4 * ppv * sg * (1.0 - sg), dx4 * sg, jnp.full((1, LANES), part, F32)
    dx4, dgz, dpp, loss_row = _rowwise("ple_loss", f_head, [x3, gz, pp, target], [],
                                       [(D_MODEL, F32), (D_MODEL, BF16), (D_MODEL, BF16)], [(1, LANES)])
    loss = loss_row[0, 0]

    gs, gm = {}, {}
    row_blocks = lambda g: g.reshape(N_DEV, g.shape[0] // N_DEV, g.shape[1])
    dhn = _mm("ple_dhn", dgz, w_pg, "nt")
    gm["ple_gate"] = row_blocks(_mm("ple_dwgate", hn, dgz, "tn", out_dtype=BF16))
    col_blocks = lambda g: g.reshape(g.shape[0], N_DEV, g.shape[1] // N_DEV).transpose(1, 0, 2)
    gm["ple_proj"] = col_blocks(_mm("ple_dwproj", p, dpp, "tn", out_dtype=BF16))
    dx3, gs["ple_norm"] = _norm_bwd("ple_dnorm", x3, sm["ple_norm"], dhn, dx4)

    dx2, gs["ffn2_norm"], gm["ffn2_gu"], gm["ffn2_d"] = _ffn_bwd(
        "ffn2", x2, sm["ffn2_norm"], wg["ffn2_gu"], w_d2, ffn2_saved, dx3)
    tail_token = on_grads("tail", {n: gm.pop(n) for n in SCATTER_GROUPS["tail"]})

    dmerged = _mm("w_out_dmerged", dx2, w_out, "nt")
    gm["w_out"] = row_blocks(_mm("w_out_dw", merged, dx2, "tn", out_dtype=BF16))

    def merge_bwd(zg, ur, ua, dm):
        _, vjp = jax.vjp(f_merge, zg[:, :D_MODEL], zg[:, D_MODEL:], ur, ua)
        dzr, dza, dur, dua = vjp(dm)
        return jnp.concatenate([dzr, dza], axis=1), dur, dua
    dz_g, du_r, du_a = _rowwise("merge_bwd", merge_bwd, [z_g, u_r, u_a, dmerged], [],
                                [(2 * D_MODEL, BF16), (D_MODEL, BF16), (D_MODEL, BF16)])
    dy_rwkv = _mm("br_rwkv_dy", du_r, w_brr, "nt")
    dy_attn = _mm("br_attn_dy", du_a, w_bra, "nt")
    gm["br"] = jnp.concatenate([col_blocks(_mm("br_rwkv_dw", y_rwkv, du_r, "tn", out_dtype=BF16)),
                                col_blocks(_mm("br_attn_dw", y_attn, du_a, "tn", out_dtype=BF16))], axis=1)
    branch_token = on_grads("branch", {n: gm.pop(n) for n in SCATTER_GROUPS["branch"]})

    def comb_bwd(ov, lv, dyv):
        _, vjp = jax.vjp(_f_comb, *by_group(ov, lv))
        d = vjp(dyv)
        return jnp.concatenate(d[:3], axis=1), jnp.concatenate(d[3:], axis=1)
    do_all, dl_all = _rowwise("attn_comb_bwd", comb_bwd, [o_all, lse_all, dy_attn], [],
                              [(ATTN_DIM, F32), (ATTN_DIM, F32)])
    dq_all, dk_all, dv_all = _attn_bwd(qk, z, do_all, dl_all)
    dqk_raw, gs["q_norm"], gs["k_norm"] = _qk_prep_bwd(z, tab, sm["q_norm"], sm["k_norm"], dq_all, dk_all)

    def post_bwd(yv, rv, k2v, vv, gv, dv_, gnw, gnb, rk):
        _, vjp = jax.vjp(_f_post, yv, rv, k2v, vv, gv, gnw, gnb, rk)
        return vjp(dv_)
    dy, dr1, dk2a, dv1, dg, d_gnw, d_gnb, d_rk = _pairwise(
        "rwkv_post_bwd", post_bwd, post_rows + [dy_rwkv], [post_params[0] + tail_token[0, 0] + branch_token[0, 0]] + post_params[1:], 5, 3)
    gs["rwkv_gn_w"], gs["rwkv_gn_b"], gs["rwkv_r_k"] = d_gnw, d_gnb, d_rk
    dr2, ddecay, dk2b, dkkn, db, dv_rows = _wkv_bwd(*scan_in, v_rows, states, _to_v_rows(dy))
    dr2, ddecay, dk2b, dkkn, db = [u.reshape(t, RWKV_DIM) for u in (dr2, ddecay, dk2b, dkkn, db)]
    dv2 = _from_v_rows(dv_rows)

    def kk_bwd(kkv, av, dkknv, dbv, dra, drb, dva, dvb):
        _, vjp = jax.vjp(_f_kk, kkv, av)
        return (*vjp((dkknv, dbv)), dra + drb, dva + dvb)
    dkk, da, dr, dv = _pairwise("rwkv_kk_bwd", kk_bwd, [kk, a, dkkn, db, dr1, dr2, dv1, dv2], [], 4)

    def pre_bwd(kv, lov, gdv, dk2x, dk2y, dkkv, dav, ddec, dgv, w0, w2p_, a0, a2p_, g2, k_k, k_a):
        _, vjp = jax.vjp(_f_pre, kv, lov, gdv, w0, w2p_, a0, a2p_, g2, k_k, k_a)
        return vjp((dk2x + dk2y, dkkv, dav, ddec, dgv))
    lora_acc = (DECAY_LORA + ICLR_LORA, RWKV_DIM)
    dk, dlo, dgd, d_w0, d_w2p, d_a0, d_a2p, d_g2, d_kk, d_ka = _rowwise(
        "rwkv_pre_bwd", pre_bwd,
        [k, lo, gd, dk2a, dk2b, dkk, da, ddecay, dg],
        pre_params, [(RWKV_DIM, F32), (LANES, F32), (LANES, F32)],
        [(1, RWKV_DIM), lora_acc, (1, RWKV_DIM), lora_acc, (GATE_LORA, RWKV_DIM), (1, RWKV_DIM), (1, RWKV_DIM)])
    gs["rwkv_w0"], gs["rwkv_a0"], gs["rwkv_k_k"], gs["rwkv_k_a"] = d_w0, d_a0, d_kk, d_ka
    gm["lora"] = jnp.concatenate([col_blocks(d_w2p[:DECAY_LORA]), col_blocks(d_a2p[DECAY_LORA:]), col_blocks(d_g2)],
                                 axis=1).astype(BF16)
    dz_r, gs["rwkv_mu"] = _shift_bwd(z, sm["rwkv_mu"], dr, dk, dv, dlo, dgd)

    dz = jnp.concatenate([dz_r, dqk_raw, dv_all.astype(BF16), dz_g], axis=1)
    dh2 = _mmc_nt("w_in_dh", dz, wg["w_in"], 0, 0, IN_SHARD, D_MODEL)
    gm["w_in"] = _mmc_tn("w_in_dw", h2, dz, IN_SHARD)
    mixer_token = on_grads("mixer", {n: gm.pop(n) for n in SCATTER_GROUPS["mixer"]})
    dx1, gs["mix_norm"] = _norm_bwd("mix_dnorm", x1, sm["mix_norm"] + mixer_token[0, 0], dh2, dx2)

    dx0, gs["ffn1_norm"], gm["ffn1_gu"], _ = _ffn_bwd(
        "ffn1", x, sm["ffn1_norm"], wg["ffn1_gu"], w_d1, ffn1_saved, dx1,
        on_down=lambda blocks: on_grads("ffn1_down", {"ffn1_d": blocks}))
    return loss, dx0, gm, gs


def kernel(x, p, positions, ffn1_norm, ffn1_w_gate, ffn1_w_up, ffn1_w_down, mix_norm, w_in, rwkv_mu, rwkv_w0, rwkv_w2, rwkv_a0, rwkv_a2, rwkv_g2, rwkv_k_k, rwkv_k_a, rwkv_r_k, rwkv_gn_w, rwkv_gn_b, q_norm, k_norm, w_br_rwkv, w_br_attn, w_out, ffn2_norm, ffn2_w_gate, ffn2_w_up, ffn2_w_down, ple_norm, ple_w_gate, ple_w_proj, loss_target, m_ffn1_norm, m_ffn1_w_gate, m_ffn1_w_up, m_ffn1_w_down, m_mix_norm, m_w_in, m_rwkv_mu, m_rwkv_w0, m_rwkv_w2, m_rwkv_a0, m_rwkv_a2, m_rwkv_g2, m_rwkv_k_k, m_rwkv_k_a, m_rwkv_r_k, m_rwkv_gn_w, m_rwkv_gn_b, m_q_norm, m_k_norm, m_w_br_rwkv, m_w_br_attn, m_w_out, m_ffn2_norm, m_ffn2_w_gate, m_ffn2_w_up, m_ffn2_w_down, m_ple_norm, m_ple_w_gate, m_ple_w_proj, v_ffn1_norm, v_ffn1_w_gate, v_ffn1_w_up, v_ffn1_w_down, v_mix_norm, v_w_in, v_rwkv_mu, v_rwkv_w0, v_rwkv_w2, v_rwkv_a0, v_rwkv_a2, v_rwkv_g2, v_rwkv_k_k, v_rwkv_k_a, v_rwkv_r_k, v_rwkv_gn_w, v_rwkv_gn_b, v_q_norm, v_k_norm, v_w_br_rwkv, v_w_br_attn, v_w_out, v_ffn2_norm, v_ffn2_w_gate, v_ffn2_w_up, v_ffn2_w_down, v_ple_norm, v_ple_w_gate, v_ple_w_proj):
    args = locals()
    w = {n: args[n][0] for n in WEIGHTS}
    m = {n: args["m_" + n][0] for n in WEIGHTS}
    v = {n: args["v_" + n][0] for n in WEIGHTS}

    w_loc = _local_blocks(w)
    head = GATHER_GROUPS["head"]
    wg = dict(zip(head, _all_gather_hbm("gather_head", [w_loc[n] for n in head])))
    me = 4 * lax.axis_index("x") + 2 * lax.axis_index("y") + lax.axis_index("c")
    gathering, order_after = {}, wg[head[0]]
    for group in ("mid", "rest"):
        shards = [w_loc[n] for n in GATHER_GROUPS[group]]
        zones = [lax.dynamic_update_slice(lax.empty((N_DEV,) + a.shape, a.dtype), a[None], (me, 0, 0)) for a in shards]
        copies = _gather2_first if group == "mid" else _gather_copies
        *gathering[group], order_after = _exchange_start("gather_start_" + group, copies, shards, zones, order_after)

    def fetch(group, after):
        if group == "forward":
            s1, r1, srcs, lands = gathering["mid"]
            ici = functools.partial(_gather2_first, ks=(1, 2, 3))
            srcs, lands = _exchange_wait("gather_arrived_mid", ici, s1, r1, srcs, lands, after, parts=("recv",))
            s2, r2, srcs, lands, _ = _exchange_start("gather_forward_mid", _gather2_second, srcs, lands, after)
            gathering["mid"] = (s1, r1, s2, r2, srcs, lands)
            return {}
        if group == "mid":
            s1, r1, s2, r2, srcs, lands = gathering["mid"]
            sib = functools.partial(_gather2_first, ks=(0,))
            srcs, lands = _exchange_wait("gather_passed_mid", _gather2_second, s2, r2, srcs, lands, after)
            srcs, lands = _exchange_wait("gather_sent_mid", _gather2_first, s1, r1, srcs, lands, after, parts=("send",))
            _, got = _exchange_wait("gather_wait_mid", sib, s1, r1, srcs, lands, after, parts=("recv",))
        else:
            _, got = _exchange_wait("gather_wait_" + group, _gather_copies, *gathering[group], after)
        return dict(zip(GATHER_GROUPS[group], got))

    sm = {n: w[n].reshape(1, -1) for n in SMALL}
    sm["ffn1_norm"] = sm["ffn1_norm"] + order_after[0, 0]
    in_flight = {}

    core = lax.axis_index("c").astype(jnp.int32).reshape(1)
    chip_slot = (2 * lax.axis_index("x") + lax.axis_index("y")).astype(jnp.int32).reshape(1)
    device_slot = 2 * chip_slot + core

    def scatter_early(group, arrays):
        arrs = [arrays[n] for n in SCATTER_GROUPS[group]]
        copies, after = _scatter_copies, arrs[0]
        if group in TWO_LEVEL:
            halves = _pair_swap("scatter_pair_" + group, arrs)
            arrs = [_pair_add("scatter_add_%s_%s" % (group, n), a, hf, core)
                    for n, a, hf in zip(SCATTER_GROUPS[group], arrs, halves)]
            copies, after = _chip_copies, halves[0]
        *in_flight[group], token = _exchange_start("scatter_start_" + group, copies, arrs,
                                                   [lax.empty(a.shape, a.dtype) for a in arrs], after)
        return token
    loss_part, dx, gm, gs = _local_step(x[0], p[0, 0], positions[0], loss_target[0], sm, wg, fetch, scatter_early)
    head_token = scatter_early("head", gm)
    recv, own = {}, {}

    def arrived(group, after):
        copies = _chip_copies if group in TWO_LEVEL else _scatter_copies
        sent, lands = _exchange_wait("scatter_wait_" + group, copies, *in_flight[group], after)
        own.update(zip(SCATTER_GROUPS[group], sent))
        recv.update(zip(SCATTER_GROUPS[group], lands))
    for group in ("tail", "branch", "mixer", "ffn1_down"):
        arrived(group, head_token)
    small_like = [w[n] for n in SMALL]
    small_rows = 80
    gs_all = _all_gather_vmem(_pack_small([gs[n] for n in SMALL] + [loss_part], small_rows))
    gs_sum = _sum_slots("sum_small_grads", gs_all, small_rows)
    loss = gs_sum.reshape(-1)[sum(a.size for a in small_like)]

    res = {}
    early = [e for group in ("tail", "branch", "mixer", "ffn1_down") for e in ADAM_PLAN
             if e[1] in SCATTER_GROUPS[group]]
    late = [e for e in ADAM_PLAN if e[1] in SCATTER_GROUPS["head"]]
    order = head_token
    for n, src, rb, cb, cw, tr in early + late:
        if (n, src, rb, cb, cw, tr) == late[0]:
            arrived("head", order)
        slot = chip_slot if recv[src].shape[0] == N_CHIPS else device_slot
        outs4 = _adamw_slots("adamw_" + n, recv[src], own[src], slot, rb, cb, cw, w[n], m[n], v[n], tr, order)
        order = outs4[1]
        for tag, a in zip(("grad", "delta", "new_m", "new_v"), outs4):
            res[tag, n] = a[None]
    d_s, m_s, v_s = _adamw("adamw_small", _pack_small(small_like, small_rows), gs_sum,
                           _pack_small([m[n] for n in SMALL], small_rows),
                           _pack_small([v[n] for n in SMALL], small_rows), small_rows)
    for tag, small in (("grad", gs_sum), ("delta", d_s), ("new_m", m_s), ("new_v", v_s)):
        for n, a in zip(SMALL, _unpack_small(small, small_like)):
            res[tag, n] = a[None]
    outs = [loss, dx[None]]
    for tag in ("grad", "delta", "new_m", "new_v"):
        outs += [res[tag, n] for n in WEIGHTS]
    return tuple(outs)
```

```python
import functools

import jax
import jax.numpy as jnp
from jax import lax
from jax.experimental import pallas as pl
from jax.experimental.pallas import tpu as pltpu

F32, BF16 = jnp.float32, jnp.bfloat16
MESH = pl.DeviceIdType.MESH
N_DEV = 8
LANES = 128
VMEM_LIMIT = 56 * 1024 * 1024

D_MODEL = 1024
PLE_DIM = 256
HEAD_DIM = 64
RWKV_HEADS = 8
RWKV_DIM = RWKV_HEADS * HEAD_DIM
DECAY_LORA = 64
ICLR_LORA = 64
GATE_LORA = 128
GN_EPS = 64e-5
ATTN_GROUPS = ((128, 1), (512, 4), (2048, 16))
HEADS_PER_GROUP = 4
ATTN_HEADS = HEADS_PER_GROUP * len(ATTN_GROUPS)
ATTN_DIM = ATTN_HEADS * HEAD_DIM
BAND_BLOCK = 128
ROPE_THETA = 10000.0
NEG_INF = -1e30
D_FF = 2816
RMS_EPS = 1e-6
RWKV_COLS = 3 * RWKV_DIM + DECAY_LORA + ICLR_LORA + GATE_LORA
ADAM_LR = 0.001
ADAM_B1 = 0.9
ADAM_B2 = 0.999
ADAM_EPS = 1e-08
ADAM_WD = 0.01
ADAM_STEP = 10

V_LO = LANES // RWKV_HEADS
V_HI = HEAD_DIM // V_LO
SCAN_CHUNK = 64
MM_ROWS = 2048
BIG_ROWS = 1024

FF_SHARD = D_FF // N_DEV
FF_PAD = -(-FF_SHARD // LANES) * LANES
FF_HID = N_DEV * FF_PAD
IN_SHARD = 6144 // N_DEV
OUT_SHARD = D_MODEL // N_DEV

SMALL = ("ffn1_norm", "mix_norm", "rwkv_mu", "rwkv_w0", "rwkv_a0", "rwkv_k_k", "rwkv_k_a", "rwkv_r_k",
         "rwkv_gn_w", "rwkv_gn_b", "q_norm", "k_norm", "ffn2_norm", "ple_norm")
WEIGHTS = ("ffn1_norm", "ffn1_w_gate", "ffn1_w_up", "ffn1_w_down", "mix_norm", "w_in", "rwkv_mu", "rwkv_w0",
           "rwkv_w2", "rwkv_a0", "rwkv_a2", "rwkv_g2", "rwkv_k_k", "rwkv_k_a", "rwkv_r_k", "rwkv_gn_w",
           "rwkv_gn_b", "q_norm", "k_norm", "w_br_rwkv", "w_br_attn", "w_out", "ffn2_norm", "ffn2_w_gate",
           "ffn2_w_up", "ffn2_w_down", "ple_norm", "ple_w_gate", "ple_w_proj")


def _cparams(**kw):
    return pltpu.CompilerParams(vmem_limit_bytes=VMEM_LIMIT, **kw)


def _tile(n, cap):
    best = None
    for t in range(LANES, min(n, cap) + 1, LANES):
        if n % t == 0:
            best = t
    return best if best is not None else n


@jax.custom_vjp
def _bdot(a, w):
    return jnp.dot(a.astype(BF16), w.astype(BF16), preferred_element_type=F32)


def _bdot_fwd(a, w):
    return _bdot(a, w), (a, w)


def _bdot_bwd(res, g):
    a, w = res
    gb = g.astype(BF16)
    da = lax.dot_general(gb, w.astype(BF16), (((1,), (1,)), ((), ())), preferred_element_type=F32)
    dw = lax.dot_general(a.astype(BF16), gb, (((0,), (0,)), ((), ())), preferred_element_type=F32)
    return da.astype(a.dtype), dw.astype(w.dtype)


_bdot.defvjp(_bdot_fwd, _bdot_bwd)


@jax.custom_vjp
def _bdot_nt(a, b):
    return lax.dot_general(a.astype(BF16), b.astype(BF16), (((1,), (1,)), ((), ())), preferred_element_type=F32)


def _bdot_nt_fwd(a, b):
    return _bdot_nt(a, b), (a, b)


def _bdot_nt_bwd(res, g):
    a, b = res
    gb = g.astype(BF16)
    da = jnp.dot(gb, b.astype(BF16), preferred_element_type=F32)
    db = lax.dot_general(gb, a.astype(BF16), (((0,), (0,)), ((), ())), preferred_element_type=F32)
    return da.astype(a.dtype), db.astype(b.dtype)


_bdot_nt.defvjp(_bdot_nt_fwd, _bdot_nt_bwd)


def _mm(name, a, b, mode, out_dtype=F32, res=None, scale=None):
    if mode == "nn":
        (m, k), n = a.shape, b.shape[1]
    elif mode == "nt":
        (m, k), n = a.shape, b.shape[0]
    else:
        (k, m), n = a.shape, b.shape[1]
    tm, tn = _tile(m, 1024), _tile(n, 512)
    a_spec = pl.BlockSpec((k, tm), lambda i, j: (0, i)) if mode == "tn" else pl.BlockSpec((tm, k), lambda i, j: (i, 0))
    b_spec = pl.BlockSpec((tn, k), lambda i, j: (j, 0)) if mode == "nt" else pl.BlockSpec((k, tn), lambda i, j: (0, j))
    dims = {"nn": ((1,), (0,)), "nt": ((1,), (1,)), "tn": ((0,), (0,))}[mode]
    o_spec = pl.BlockSpec((tm, tn), lambda i, j: (i, j))
    ins, in_specs = [a, b], [a_spec, b_spec]
    if res is not None:
        ins.append(res)
        in_specs.append(o_spec)

    def body(*refs):
        acc = lax.dot_general(refs[0][...].astype(BF16), refs[1][...].astype(BF16), (dims, ((), ())),
                              preferred_element_type=F32)
        if scale is not None:
            acc = acc * scale
        if res is not None:
            acc = acc + refs[2][...].astype(F32)
        refs[-1][...] = acc.astype(refs[-1].dtype)

    return pl.pallas_call(
        body, name=name, grid=(m // tm, n // tn), in_specs=in_specs, out_specs=o_spec,
        out_shape=jax.ShapeDtypeStruct((m, n), out_dtype),
        compiler_params=_cparams(dimension_semantics=("parallel", "parallel")),
    )(*ins)


def _mmc_nn(name, a, wb, ki, ci, n, out_dtype=F32):
    m, k = a.shape
    tm = _tile(m, MM_ROWS)

    def body(a_ref, w_ref, o_ref):
        o_ref[...] = jnp.dot(a_ref[...].astype(BF16), w_ref[...], preferred_element_type=F32).astype(o_ref.dtype)

    return pl.pallas_call(
        body, name=name, grid=(m // tm, N_DEV),
        in_specs=[pl.BlockSpec((tm, k), lambda i, j: (i, 0)), pl.BlockSpec((None, k, n), lambda i, j: (j, ki, ci))],
        out_specs=pl.BlockSpec((tm, n), lambda i, j: (i, j)),
        out_shape=jax.ShapeDtypeStruct((m, N_DEV * n), out_dtype),
        compiler_params=_cparams(dimension_semantics=("parallel", "parallel")),
    )(a, wb)


def _mmc_nt(name, a, wb, ki, ci, n, k, res=None):
    m = a.shape[0]
    tm = _tile(m, MM_ROWS)
    o_spec = pl.BlockSpec((tm, k), lambda i, j: (i, 0))
    ins = [a, wb] + ([res] if res is not None else [])
    in_specs = [pl.BlockSpec((tm, n), lambda i, j: (i, j)), pl.BlockSpec((None, k, n), lambda i, j: (j, ki, ci))]
    in_specs += [o_spec] if res is not None else []

    def body(*refs):
        a_ref, w_ref, o_ref = refs[0], refs[1], refs[-1]
        acc = lax.dot_general(a_ref[...].astype(BF16), w_ref[...], (((1,), (1,)), ((), ())),
                              preferred_element_type=F32)

        @pl.when(pl.program_id(1) == 0)
        def _():
            o_ref[...] = acc + refs[2][...] if res is not None else acc

        @pl.when(pl.program_id(1) != 0)
        def _():
            o_ref[...] += acc

    return pl.pallas_call(
        body, name=name, grid=(m // tm, N_DEV), in_specs=in_specs, out_specs=o_spec,
        out_shape=jax.ShapeDtypeStruct((m, k), F32),
        compiler_params=_cparams(dimension_semantics=("parallel", "arbitrary")),
    )(*ins)


def _mmc_tn(name, x, dy, n):
    m, k = x.shape
    tk = _tile(k, 1024)

    def body(x_ref, dy_ref, o_ref):
        o_ref[...] = lax.dot_general(x_ref[...].astype(BF16), dy_ref[...].astype(BF16), (((0,), (0,)), ((), ())),
                                     preferred_element_type=F32).astype(o_ref.dtype)

    return pl.pallas_call(
        body, name=name, grid=(N_DEV, k // tk),
        in_specs=[pl.BlockSpec((m, tk), lambda j, i: (0, i)), pl.BlockSpec((m, n), lambda j, i: (0, j))],
        out_specs=pl.BlockSpec((None, tk, n), lambda j, i: (j, i, 0)),
        out_shape=jax.ShapeDtypeStruct((N_DEV, k, n), BF16),
        compiler_params=_cparams(dimension_semantics=("parallel", "parallel")),
    )(x, dy)


def _rowwise(name, fn, rows, params, out_rows, out_accs=(), tr=512):
    rows = [a if isinstance(a, tuple) else (a, a.shape[1], 0) for a in rows]
    r = rows[0][0].shape[0]
    in_specs = [pl.BlockSpec((tr, wd), lambda i, cb=cb: (i, cb)) for _, wd, cb in rows]
    rows = [a for a, _, _ in rows]
    in_specs += [pl.BlockSpec(p.shape, lambda i, nd=p.ndim: (0,) * nd) for p in params]
    out_shape = [jax.ShapeDtypeStruct((r, c), dt) for c, dt in out_rows]
    out_shape += [jax.ShapeDtypeStruct(s, F32) for s in out_accs]
    out_specs = [pl.BlockSpec((tr, c), lambda i: (i, 0)) for c, _ in out_rows]
    out_specs += [pl.BlockSpec(s, lambda i, nd=len(s): (0,) * nd) for s in out_accs]
    n_in, n_ro = len(rows) + len(params), len(out_rows)

    def body(*refs):
        res = fn(*[ref[...] for ref in refs[:n_in]])
        outs = refs[n_in:]
        for o, v in zip(outs[:n_ro], res[:n_ro]):
            o[...] = v.astype(o.dtype)
        for o, v in zip(outs[n_ro:], res[n_ro:]):
            _accumulate(o, v)

    return pl.pallas_call(
        body, name=name, grid=(r // tr,), in_specs=in_specs, out_specs=out_specs, out_shape=out_shape,
        compiler_params=_cparams(dimension_semantics=("arbitrary",)),
    )(*rows, *params)


def _pairwise(name, fn, rows, params, n_out, n_acc=0, tr=2048):
    t, c = rows[0].shape
    tile = pl.BlockSpec((tr, 2 * HEAD_DIM), lambda p, i: (i, p))
    vec = pl.BlockSpec((1, 2 * HEAD_DIM), lambda p, i: (0, p))
    n_in = len(rows) + len(params)

    def body(*refs):
        res = fn(*[ref[...] for ref in refs[:n_in]])
        outs = refs[n_in:]
        for o, v in zip(outs[:n_out], res[:n_out]):
            o[...] = v
        first = pl.program_id(1) == 0
        for o, v in zip(outs[n_out:], res[n_out:]):
            @pl.when(first)
            def _(o=o, v=v):
                o[...] = v

            @pl.when(jnp.logical_not(first))
            def _(o=o, v=v):
                o[...] += v

    return pl.pallas_call(
        body, name=name, grid=(c // (2 * HEAD_DIM), t // tr),
        in_specs=[tile] * len(rows) + [vec] * len(params), out_specs=[tile] * n_out + [vec] * n_acc,
        out_shape=[jax.ShapeDtypeStruct((t, c), F32)] * n_out + [jax.ShapeDtypeStruct((1, c), F32)] * n_acc,
        compiler_params=_cparams(dimension_semantics=("parallel", "arbitrary")),
    )(*rows, *params)


def _accumulate(o_ref, v):
    @pl.when(pl.program_id(0) == 0)
    def _():
        o_ref[...] = v

    @pl.when(pl.program_id(0) != 0)
    def _():
        o_ref[...] += v


def _rms(x, g):
    return x * lax.rsqrt(jnp.mean(x * x, axis=-1, keepdims=True) + RMS_EPS) * g


def _sigmoid(x):
    return jax.nn.sigmoid(x)


def _softplus(x):
    return jnp.maximum(x, 0.0) + jnp.log1p(jnp.exp(-jnp.abs(x)))


def _norm_fwd(name, x, g):
    return _rowwise(name, lambda xv, gv: (_rms(xv, gv),), [x], [g], [(x.shape[1], BF16)], tr=BIG_ROWS)[0]


def _norm_bwd(name, x, g, dh, dres):
    def fn(xv, dhv, drv, gv):
        _, vjp = jax.vjp(_rms, xv, gv)
        dx, dg = vjp(dhv)
        return dx + drv, dg
    return _rowwise(name, fn, [x, dh, dres], [g], [(x.shape[1], F32)], [g.shape], tr=BIG_ROWS)


def _f_act(gate, up):
    return gate * _sigmoid(gate) * up


def _gate_up_act(name, h, w_gu):
    m, k = h.shape
    tm = _tile(m, MM_ROWS)

    def body(h_ref, w_ref, gu_ref, a_ref):
        gu = jnp.dot(h_ref[...], w_ref[...], preferred_element_type=F32)
        gu_ref[...] = gu
        a_ref[...] = _f_act(gu[:, :FF_PAD], gu[:, FF_PAD:]).astype(a_ref.dtype)

    return pl.pallas_call(
        body, name=name, grid=(m // tm, N_DEV),
        in_specs=[pl.BlockSpec((tm, k), lambda i, j: (i, 0)),
                  pl.BlockSpec((None, k, 2 * FF_PAD), lambda i, j: (j, 0, 0))],
        out_specs=[pl.BlockSpec((tm, 2 * FF_PAD), lambda i, j: (i, j)), pl.BlockSpec((tm, FF_PAD), lambda i, j: (i, j))],
        out_shape=[jax.ShapeDtypeStruct((m, N_DEV * 2 * FF_PAD), F32), jax.ShapeDtypeStruct((m, FF_HID), BF16)],
        compiler_params=_cparams(dimension_semantics=("parallel", "parallel")),
    )(h, w_gu)


def _gate_up_act_bwd(name, dout, w_down, gu, order):
    m, k = dout.shape
    tm = _tile(m, MM_ROWS)

    def body(d_ref, w_ref, gu_ref, order_ref, o_ref):
        da = 0.5 * lax.dot_general(d_ref[...].astype(BF16), w_ref[...], (((1,), (1,)), ((), ())),
                                   preferred_element_type=F32)
        guv = gu_ref[...]
        _, vjp = jax.vjp(_f_act, guv[:, :FF_PAD], guv[:, FF_PAD:])
        o_ref[...] = jnp.concatenate(vjp(da), axis=1).astype(o_ref.dtype)

    gu_spec = pl.BlockSpec((tm, 2 * FF_PAD), lambda i, j: (i, j))
    return pl.pallas_call(
        body, name=name, grid=(m // tm, N_DEV),
        in_specs=[pl.BlockSpec((tm, k), lambda i, j: (i, 0)), pl.BlockSpec((FF_PAD, k), lambda i, j: (j, 0)), gu_spec,
                  pl.BlockSpec(memory_space=pl.ANY)],
        out_specs=gu_spec, out_shape=jax.ShapeDtypeStruct((m, N_DEV * 2 * FF_PAD), BF16),
        compiler_params=_cparams(dimension_semantics=("parallel", "parallel")),
    )(dout, w_down, gu, order)


def _ffn_fwd(tag, x, norm, w_gu, w_down):
    h = _norm_fwd(tag + "_norm", x, norm)
    gu, a = _gate_up_act(tag + "_gu", h, w_gu)
    wd = w_down(a)
    out = _mm(tag + "_down", a, wd, "nn", res=x, scale=0.5)
    return out, (h, gu, a), wd


def _ffn_bwd(tag, x, norm, w_gu, w_down, saved, dout, on_down=None):
    h, gu, a = saved
    d_wdown = _mm(tag + "_dwdown", a, dout, "tn", out_dtype=BF16, scale=0.5).reshape(N_DEV, FF_PAD, D_MODEL)
    token = on_down(d_wdown) if on_down is not None else jnp.zeros((8, LANES), F32)
    dgu = _gate_up_act_bwd(tag + "_dgu", dout, w_down, gu, token)
    dh =_mmc_nt(tag + "_dh", dgu, w_gu, 0, 0, 2 * FF_PAD, D_MODEL)
    d_wgu = _mmc_tn(tag + "_dwgu", h, dgu, 2 * FF_PAD)
    dx, dnorm = _norm_bwd(tag + "_dnorm", x, norm, dh, dout)
    return dx, dnorm, d_wgu, d_wdown


def _shift_fwd(z, mu):
    t, c = z.shape[0], RWKV_COLS
    tr = 512

    def body(z_ref, zp_ref, mu_ref, r_ref, k_ref, v_ref, lo_ref, gd_ref):
        zv = z_ref[...]
        prev = zp_ref[7:8, :] * jnp.where(pl.program_id(0) == 0, 0.0, 1.0)
        row = lax.broadcasted_iota(jnp.int32, zv.shape, 0)
        zsh = jnp.where(row == 0, prev, pltpu.roll(zv, 1, 0))
        zs = zv + (zsh - zv) * mu_ref[...]
        r_ref[...] = zs[:, 0:512]
        k_ref[...] = zs[:, 512:1024]
        v_ref[...] = zs[:, 1024:1536]
        lo_ref[...] = zs[:, 1536:1664]
        gd_ref[...] = zs[:, 1664:1792]

    widths = (512, 512, 512, 128, 128)
    return pl.pallas_call(
        body, name="rwkv_shift", grid=(t // tr,),
        in_specs=[pl.BlockSpec((tr, c), lambda i: (i, 0)),
                  pl.BlockSpec((8, c), lambda i: (jnp.maximum(i * (tr // 8) - 1, 0), 0)),
                  pl.BlockSpec((1, c), lambda i: (0, 0))],
        out_specs=[pl.BlockSpec((tr, w), lambda i: (i, 0)) for w in widths],
        out_shape=[jax.ShapeDtypeStruct((t, w), F32) for w in widths],
        compiler_params=_cparams(dimension_semantics=("parallel",)),
    )(z, z, mu)


def _shift_bwd(z, mu, dr, dk, dv, dlo, dgd):
    t, c = z.shape[0], RWKV_COLS
    tr = 512
    nt = t // tr

    def body(z_ref, zp_ref, mu_ref, dr_ref, dk_ref, dv_ref, dlo_ref, dgd_ref,
             drn_ref, dkn_ref, dvn_ref, dlon_ref, dgdn_ref, dz_ref, dmu_ref):
        i = pl.program_id(0)
        zv, muv = z_ref[...], mu_ref[...]
        prev = zp_ref[7:8, :] * jnp.where(i == 0, 0.0, 1.0)
        row = lax.broadcasted_iota(jnp.int32, zv.shape, 0)
        zsh = jnp.where(row == 0, prev, pltpu.roll(zv, 1, 0))
        dzs = jnp.concatenate([dr_ref[...], dk_ref[...], dv_ref[...], dlo_ref[...], dgd_ref[...]], axis=1)
        nxt = jnp.concatenate([drn_ref[0:1, :], dkn_ref[0:1, :], dvn_ref[0:1, :], dlon_ref[0:1, :],
                               dgdn_ref[0:1, :]], axis=1) * jnp.where(i == nt - 1, 0.0, 1.0)
        u = dzs * muv
        un = jnp.where(row == tr - 1, nxt * muv, pltpu.roll(u, tr - 1, 0))
        dz_ref[...] = (dzs - u + un).astype(dz_ref.dtype)
        _accumulate(dmu_ref, jnp.sum(dzs * (zsh - zv), axis=0, keepdims=True))

    widths = (512, 512, 512, 128, 128)
    nxt_map = lambda i: (jnp.minimum((i + 1) * (tr // 8), t // 8 - 1), 0)
    return pl.pallas_call(
        body, name="rwkv_shift_bwd", grid=(nt,),
        in_specs=[pl.BlockSpec((tr, c), lambda i: (i, 0)),
                  pl.BlockSpec((8, c), lambda i: (jnp.maximum(i * (tr // 8) - 1, 0), 0)),
                  pl.BlockSpec((1, c), lambda i: (0, 0))]
        + [pl.BlockSpec((tr, w), lambda i: (i, 0)) for w in widths]
        + [pl.BlockSpec((8, w), nxt_map) for w in widths],
        out_specs=[pl.BlockSpec((tr, c), lambda i: (i, 0)), pl.BlockSpec((1, c), lambda i: (0, 0))],
        out_shape=[jax.ShapeDtypeStruct((t, c), BF16), jax.ShapeDtypeStruct((1, c), F32)],
        compiler_params=_cparams(dimension_semantics=("arbitrary",)),
    )(z, z, mu, dr, dk, dv, dlo, dgd, dr, dk, dv, dlo, dgd)


def _f_pre(k, lo, gd, w0, w2p, a0, a2p, g2, k_k, k_a):
    lane = lax.broadcasted_iota(jnp.int32, lo.shape, 1)
    lo_act = jnp.where(lane < DECAY_LORA, jnp.tanh(lo), lo)
    w = -_softplus(-(w0 + _bdot(lo_act, w2p))) - 0.5
    a = _sigmoid(a0 + _bdot(lo_act, a2p))
    g = _bdot(_sigmoid(gd), g2)
    kk = k * k_k
    k2 = k * (1.0 + (a - 1.0) * k_a)
    decay = jnp.exp(-jnp.exp(w))
    return k2, kk, a, decay, g


def _f_kk(kk, a):
    kkn = kk * lax.rsqrt(jnp.maximum(_head_sums(kk * kk), 1e-24))
    return kkn, kkn * a


def _f_post(y, r, k2, v, g, gn_w, gn_b, r_k):
    mean = _head_sums(y) * (1.0 / HEAD_DIM)
    var = _head_sums(jnp.square(y - mean)) * (1.0 / HEAD_DIM)
    yn = (y - mean) * lax.rsqrt(var + GN_EPS) * gn_w + gn_b
    bonus = _head_sums(r * k2 * r_k) * v
    return (yn + bonus) * g


def _to_v_rows(x):
    t = x.shape[0]
    return x.reshape(t, RWKV_HEADS, V_HI, V_LO).transpose(0, 2, 3, 1).reshape(t, V_HI, LANES)


def _from_v_rows(x):
    t = x.shape[0]
    return x.reshape(t, V_HI, V_LO, RWKV_HEADS).transpose(0, 3, 1, 2).reshape(t, RWKV_DIM)


def _k_cols(x):
    return jnp.tile(x, (V_LO, 1)).T


def _k_rows(x):
    xt = x.T
    out = xt[0:RWKV_HEADS]
    for l in range(1, V_LO):
        out = out + xt[l * RWKV_HEADS:(l + 1) * RWKV_HEADS]
    return out


def _wkv_fwd(r, w, k, kk, b, v):
    t = r.shape[0]
    tc = SCAN_CHUNK
    key_spec = pl.BlockSpec((tc, RWKV_HEADS, HEAD_DIM), lambda i: (i, 0, 0))
    row_spec = pl.BlockSpec((tc, V_HI, LANES), lambda i: (i, 0, 0))

    def body(r_ref, w_ref, k_ref, kk_ref, b_ref, v_ref, y_ref, st_ref, s_scr, cols_a, cols_b):
        @pl.when(pl.program_id(0) == 0)
        def _():
            s_scr[...] = jnp.zeros_like(s_scr)

        def prep(ti, buf):
            for n, ref in enumerate((r_ref, w_ref, k_ref, kk_ref, b_ref)):
                buf[n] = _k_cols(ref[ti])

        def step(ti, s, cur, nxt, ti_next):
            rc, wc, kc, kkc, bc = (cur[n] for n in range(5))
            prep(ti_next, nxt)
            vt = v_ref[ti]
            new, ys = [], []
            for j in range(V_HI):
                sa = -jnp.sum(s[j] * kkc, axis=0, keepdims=True)
                nj = s[j] * wc + bc * sa + kc * vt[j:j + 1]
                st_ref[ti, j] = nj
                ys.append(jnp.sum(nj * rc, axis=0, keepdims=True))
                new.append(nj)
            y_ref[ti] = jnp.concatenate(ys, axis=0)
            return tuple(new)

        def pair(i, s):
            s = step(2 * i, s, cols_a, cols_b, 2 * i + 1)
            return step(2 * i + 1, s, cols_b, cols_a, jnp.minimum(2 * i + 2, tc - 1))

        prep(0, cols_a)
        s = lax.fori_loop(0, tc // 2, pair, tuple(s_scr[j] for j in range(V_HI)))
        for j in range(V_HI):
            s_scr[j] = s[j]

    return pl.pallas_call(
        body, name="wkv_fwd", grid=(t // tc,),
        in_specs=[key_spec] * 5 + [row_spec],
        out_specs=[row_spec, pl.BlockSpec((tc, V_HI, HEAD_DIM, LANES), lambda i: (i, 0, 0, 0))],
        out_shape=[jax.ShapeDtypeStruct((t, V_HI, LANES), F32),
                   jax.ShapeDtypeStruct((t, V_HI, HEAD_DIM, LANES), F32)],
        scratch_shapes=[pltpu.VMEM((V_HI, HEAD_DIM, LANES), F32)] + [pltpu.VMEM((5, HEAD_DIM, LANES), F32)] * 2,
        compiler_params=_cparams(dimension_semantics=("arbitrary",)),
    )(r, w, k, kk, b, v)


def _wkv_bwd(r, w, k, kk, b, v, states, dy):
    t = r.shape[0]
    tc = SCAN_CHUNK
    nb = t // tc
    key_spec = pl.BlockSpec((tc, RWKV_HEADS, HEAD_DIM), lambda i: (nb - 1 - i, 0, 0))
    row_spec = pl.BlockSpec((tc, V_HI, LANES), lambda i: (nb - 1 - i, 0, 0))
    st_spec = pl.BlockSpec((tc, V_HI, HEAD_DIM, LANES), lambda i: (nb - 1 - i, 0, 0, 0))
    stp_spec = pl.BlockSpec((1, V_HI, HEAD_DIM, LANES), lambda i: (jnp.maximum((nb - 1 - i) * tc - 1, 0), 0, 0, 0))

    def body(r_ref, w_ref, k_ref, kk_ref, b_ref, v_ref, st_ref, stp_ref, dy_ref,
             dr_ref, dw_ref, dk_ref, dkk_ref, db_ref, dv_ref, ds_scr, cols_a, cols_b, accs_a, accs_b):
        @pl.when(pl.program_id(0) == 0)
        def _():
            ds_scr[...] = jnp.zeros_like(ds_scr)

        def colsum(x):
            return jnp.sum(x, axis=0, keepdims=True)

        def prep(ti, buf):
            for n, ref in enumerate((r_ref, w_ref, k_ref, kk_ref, b_ref)):
                buf[n] = _k_cols(ref[ti])

        def flush(ti, buf):
            for n, ref in enumerate((dr_ref, dk_ref, db_ref, dw_ref, dkk_ref)):
                ref[ti] = _k_rows(buf[n])

        def step(ti, ds, sp, cur, accs):
            rc, wc, kc, kkc, bc = (cur[n] for n in range(5))
            vt, dyt = v_ref[ti], dy_ref[ti]
            acc = None
            new, dvs = [], []
            for j in range(V_HI):
                st = st_ref[ti, j]
                dsj = ds[j] + rc * dyt[j:j + 1]
                sa = -colsum(sp[j] * kkc)
                dsa = colsum(dsj * bc)
                dvs.append(colsum(dsj * kc))
                parts = (st * dyt[j:j + 1], dsj * vt[j:j + 1], dsj * sa, dsj * sp[j], -(sp[j] * dsa))
                acc = parts if acc is None else tuple(a + q for a, q in zip(acc, parts))
                new.append(dsj * wc - kkc * dsa)
            dv_ref[ti] = jnp.concatenate(dvs, axis=0)
            for n in range(5):
                accs[n] = acc[n]
            return tuple(new)

        def states_before(ti):
            return tuple(st_ref[ti - 1, j] for j in range(V_HI))

        def pair(i, ds):
            ta = tc - 1 - 2 * i
            prep(ta - 1, cols_b)
            flush(jnp.minimum(ta + 1, tc - 1), accs_b)
            ds = step(ta, ds, states_before(ta), cols_a, accs_a)
            prep(ta - 2, cols_a)
            flush(ta, accs_a)
            return step(ta - 1, ds, states_before(ta - 1), cols_b, accs_b)

        prep(tc - 1, cols_a)
        accs_b[...] = jnp.zeros_like(accs_b)
        ds = lax.fori_loop(0, tc // 2 - 1, pair, tuple(ds_scr[j] for j in range(V_HI)))
        prep(0, cols_b)
        flush(2, accs_b)
        ds = step(1, ds, states_before(1), cols_a, accs_a)
        flush(1, accs_a)
        keep = jnp.where(pl.program_id(0) == nb - 1, 0.0, 1.0)
        ds = step(0, ds, tuple(stp_ref[0, j] * keep for j in range(V_HI)), cols_b, accs_b)
        flush(0, accs_b)
        for j in range(V_HI):
            ds_scr[j] = ds[j]

    key_out = jax.ShapeDtypeStruct((t, RWKV_HEADS, HEAD_DIM), F32)
    return pl.pallas_call(
        body, name="wkv_bwd", grid=(nb,),
        in_specs=[key_spec] * 5 + [row_spec, st_spec, stp_spec, row_spec],
        out_specs=[key_spec] * 5 + [row_spec],
        out_shape=[key_out] * 5 + [jax.ShapeDtypeStruct((t, V_HI, LANES), F32)],
        scratch_shapes=[pltpu.VMEM((V_HI, HEAD_DIM, LANES), F32)] + [pltpu.VMEM((5, HEAD_DIM, LANES), F32)] * 4,
        compiler_params=_cparams(dimension_semantics=("arbitrary",)),
    )(r, w, k, kk, b, v, states, states, dy)


PAIR = 2 * HEAD_DIM
N_PAIRS = ATTN_HEADS // 2
Q_COL0 = RWKV_COLS // PAIR
K_COL0 = Q_COL0 + N_PAIRS
V_COL0 = K_COL0 + N_PAIRS


def _swap_halves(x):
    lane = lax.broadcasted_iota(jnp.int32, x.shape, 1)
    return jnp.where((lane & (HEAD_DIM - 1)) < HEAD_DIM // 2, pltpu.roll(x, PAIR - HEAD_DIM // 2, 1),
                     pltpu.roll(x, HEAD_DIM // 2, 1))


@jax.custom_vjp
def _rope(x, cosf, sinf):
    return x * cosf + _swap_halves(x) * sinf


def _rope_fwd(x, cosf, sinf):
    return _rope(x, cosf, sinf), (cosf, sinf)


def _rope_bwd(res, d):
    cosf, sinf = res
    return d * cosf + _swap_halves(d * sinf), jnp.zeros_like(cosf), jnp.zeros_like(sinf)


_rope.defvjp(_rope_fwd, _rope_bwd)


def _head_sums(x):
    lane = lax.broadcasted_iota(jnp.int32, x.shape, 1)
    lo = jnp.where(lane < HEAD_DIM, 1.0, 0.0)
    hi = 1.0 - lo
    return lo * jnp.sum(x * lo, axis=1, keepdims=True) + hi * jnp.sum(x * hi, axis=1, keepdims=True)


def _f_qk(x, cosf, sinf, gain2):
    xn = x * lax.rsqrt(_head_sums(x * x) * (1.0 / HEAD_DIM) + RMS_EPS) * gain2
    return _rope(xn, cosf, sinf)


def _qk_prep(z, tab, q_gain, k_gain):
    t = z.shape[0]
    tr = 2048

    def body(z_ref, c_ref, s_ref, qg_ref, kg_ref, o_ref):
        g = jnp.where(pl.program_id(0) < N_PAIRS, qg_ref[...], kg_ref[...])
        o_ref[...] = _f_qk(z_ref[...], c_ref[...], s_ref[...], jnp.concatenate([g, g], axis=1))

    gain = pl.BlockSpec((1, HEAD_DIM), lambda c, i: (0, 0))
    return pl.pallas_call(
        body, name="qk_prep", grid=(2 * N_PAIRS, t // tr),
        in_specs=[pl.BlockSpec((tr, PAIR), lambda c, i: (i, Q_COL0 + c)), pl.BlockSpec((tr, PAIR), lambda c, i: (i, 0)),
                  pl.BlockSpec((tr, PAIR), lambda c, i: (i, 1)), gain, gain],
        out_specs=pl.BlockSpec((tr, PAIR), lambda c, i: (i, c)),
        out_shape=jax.ShapeDtypeStruct((t, 2 * N_PAIRS * PAIR), F32),
        compiler_params=_cparams(dimension_semantics=("parallel", "parallel")),
    )(z, tab, tab, q_gain, k_gain)


def _qk_prep_bwd(z, tab, q_gain, k_gain, dq, dk):
    t = z.shape[0]
    tr = 2048

    def body(z_ref, c_ref, s_ref, qg_ref, kg_ref, dq_ref, dk_ref, dz_ref, dqg_ref, dkg_ref):
        c, i = pl.program_id(0), pl.program_id(1)
        is_q = c < N_PAIRS
        g = jnp.where(is_q, qg_ref[...], kg_ref[...])
        d = jnp.where(is_q, dq_ref[...], dk_ref[...])
        _, vjp = jax.vjp(lambda xx, gg: _f_qk(xx, c_ref[...], s_ref[...], gg), z_ref[...],
                         jnp.concatenate([g, g], axis=1))
        dx, dg2 = vjp(d)
        dz_ref[...] = dx.astype(dz_ref.dtype)
        dg = dg2[:, :HEAD_DIM] + dg2[:, HEAD_DIM:]
        first_q = jnp.logical_and(c == 0, i == 0)
        first_k = jnp.logical_and(c == N_PAIRS, i == 0)

        @pl.when(first_q)
        def _():
            dqg_ref[...] = dg

        @pl.when(jnp.logical_and(is_q, jnp.logical_not(first_q)))
        def _():
            dqg_ref[...] += dg

        @pl.when(first_k)
        def _():
            dkg_ref[...] = dg

        @pl.when(jnp.logical_and(jnp.logical_not(is_q), jnp.logical_not(first_k)))
        def _():
            dkg_ref[...] += dg

    gain = pl.BlockSpec((1, HEAD_DIM), lambda c, i: (0, 0))
    return pl.pallas_call(
        body, name="qk_prep_bwd", grid=(2 * N_PAIRS, t // tr),
        in_specs=[pl.BlockSpec((tr, PAIR), lambda c, i: (i, Q_COL0 + c)), pl.BlockSpec((tr, PAIR), lambda c, i: (i, 0)),
                  pl.BlockSpec((tr, PAIR), lambda c, i: (i, 1)), gain, gain,
                  pl.BlockSpec((tr, PAIR), lambda c, i: (i, jnp.minimum(c, N_PAIRS - 1))),
                  pl.BlockSpec((tr, PAIR), lambda c, i: (i, jnp.maximum(c - N_PAIRS, 0)))],
        out_specs=[pl.BlockSpec((tr, PAIR), lambda c, i: (i, c)), gain, gain],
        out_shape=[jax.ShapeDtypeStruct((t, 2 * N_PAIRS * PAIR), BF16), jax.ShapeDtypeStruct((1, HEAD_DIM), F32),
                   jax.ShapeDtypeStruct((1, HEAD_DIM), F32)],
        compiler_params=_cparams(dimension_semantics=("arbitrary", "arbitrary")),
    )(z, tab, tab, q_gain, k_gain, dq, dk)


def _attn_block(q, kp, kc, vp, vc, kmin):
    k2 = jnp.concatenate([kp, kc], axis=0)
    v2 = jnp.concatenate([vp, vc], axis=0)
    s = _bdot_nt(q, k2) * (HEAD_DIM ** -0.5)
    qi = lax.broadcasted_iota(jnp.int32, s.shape, 0)
    kj = lax.broadcasted_iota(jnp.int32, s.shape, 1)
    dist = qi + BAND_BLOCK - kj
    valid = (dist >= 0) & (dist <= BAND_BLOCK) & (kj >= kmin)
    s = jnp.where(valid, s, NEG_INF)
    m = lax.stop_gradient(jnp.max(s, axis=-1, keepdims=True))
    e = jnp.exp(s - m)
    l = jnp.sum(e, axis=-1, keepdims=True)
    o = _bdot(e, v2) / l
    return o, m + jnp.log(l)


def _fold(src_ref, dst_ref, dil):
    t = src_ref.shape[0]
    ln = t // dil
    for j in range(dil):
        dst_ref[j * ln:(j + 1) * ln, :] = src_ref[pl.ds(j, ln, stride=dil), :]


def _unfold(src_ref, dst_ref, dil):
    t = src_ref.shape[0]
    ln = t // dil
    for j in range(dil):
        dst_ref[pl.ds(j, ln, stride=dil), :] = src_ref[j * ln:(j + 1) * ln, :]


def _per_group(fn):
    pair = pl.program_id(0)
    for gi, (_, dil) in enumerate(ATTN_GROUPS):
        @pl.when(jnp.logical_or(pair == 2 * gi, pair == 2 * gi + 1))
        def _(dil=dil):
            fn(dil)


def _block_rows(idx, blocks_per_seq):
    first = (idx & (blocks_per_seq - 1)) == 0
    cur = pl.ds(pl.multiple_of(idx * BAND_BLOCK, BAND_BLOCK), BAND_BLOCK)
    prev = pl.ds(pl.multiple_of(jnp.maximum(idx - 1, 0) * BAND_BLOCK, BAND_BLOCK), BAND_BLOCK)
    return first, cur, prev


def _heads(x):
    return x[:, :HEAD_DIM], x[:, HEAD_DIM:]


def _attn_fwd(qk, z):
    t = z.shape[0]
    n_blocks = t // BAND_BLOCK

    def body(q_ref, k_ref, v_ref, o_ref, lse_ref, qf, kf, vf, of, lf):
        def run(dil):
            _fold(q_ref, qf, dil)
            _fold(k_ref, kf, dil)
            _fold(v_ref, vf, dil)
            blocks_per_seq = n_blocks // dil

            def block(idx, carry):
                first, cur, prev = _block_rows(idx, blocks_per_seq)
                kmin = jnp.where(first, BAND_BLOCK, 0)
                outs, lses = [], []
                for q, kp, kc, vp, vc in zip(_heads(qf[cur, :]), _heads(kf[prev, :]), _heads(kf[cur, :]),
                                             _heads(vf[prev, :]), _heads(vf[cur, :])):
                    o, ls = _attn_block(q, kp, kc, vp, vc, kmin)
                    outs.append(o)
                    lses.append(jnp.broadcast_to(ls, o.shape))
                of[cur, :] = jnp.concatenate(outs, axis=1)
                lf[cur, :] = jnp.concatenate(lses, axis=1)
                return carry

            lax.fori_loop(0, n_blocks, block, 0, unroll=4)
            _unfold(of, o_ref, dil)
            _unfold(lf, lse_ref, dil)

        _per_group(run)

    slab = jax.ShapeDtypeStruct((t, N_PAIRS * PAIR), F32)
    out_spec = pl.BlockSpec((t, PAIR), lambda p: (0, p))
    return pl.pallas_call(
        body, name="attn_fwd", grid=(N_PAIRS,),
        in_specs=[pl.BlockSpec((t, PAIR), lambda p: (0, p)), pl.BlockSpec((t, PAIR), lambda p: (0, N_PAIRS + p)),
                  pl.BlockSpec((t, PAIR), lambda p: (0, V_COL0 + p))],
        out_specs=[out_spec, out_spec], out_shape=[slab, slab],
        scratch_shapes=[pltpu.VMEM((t, PAIR), F32)] * 5,
        compiler_params=_cparams(dimension_semantics=("parallel",)),
    )(qk, qk, z)


def _attn_bwd(qk, z, do, dlse):
    t = z.shape[0]
    n_blocks = t // BAND_BLOCK

    def body(q_ref, k_ref, v_ref, do_ref, dl_ref, dq_ref, dk_ref, dv_ref, qf, kf, vf, dof, dlf, dqf, dkf, dvf,
             dkpf, dvpf):
        def run(dil):
            for src, dst in ((q_ref, qf), (k_ref, kf), (v_ref, vf), (do_ref, dof), (dl_ref, dlf)):
                _fold(src, dst, dil)
            blocks_per_seq = n_blocks // dil

            def block(idx, carry):
                first, cur, prev = _block_rows(idx, blocks_per_seq)
                kmin = jnp.where(first, BAND_BLOCK, 0)
                grads = []
                for q, kp, kc, vp, vc, do_h, dl_h in zip(
                        _heads(qf[cur, :]), _heads(kf[prev, :]), _heads(kf[cur, :]), _heads(vf[prev, :]),
                        _heads(vf[cur, :]), _heads(dof[cur, :]), _heads(dlf[cur, :])):
                    _, vjp = jax.vjp(functools.partial(_attn_block, kmin=kmin), q, kp, kc, vp, vc)
                    grads.append(vjp((do_h, jnp.sum(dl_h, axis=1, keepdims=True))))
                dq, dkp, dkc, dvp, dvc = (jnp.concatenate([a, b], axis=1) for a, b in zip(*grads))
                dqf[cur, :], dkf[cur, :], dvf[cur, :], dkpf[cur, :], dvpf[cur, :] = dq, dkc, dvc, dkp, dvp
                return carry

            lax.fori_loop(0, n_blocks, block, 0, unroll=2)

            def join(idx, carry):
                first, cur, prev = _block_rows(idx, blocks_per_seq)

                @pl.when(jnp.logical_not(first))
                def _():
                    dkf[prev, :] += dkpf[cur, :]
                    dvf[prev, :] += dvpf[cur, :]

                return carry

            lax.fori_loop(0, n_blocks, join, 0)
            _unfold(dqf, dq_ref, dil)
            _unfold(dkf, dk_ref, dil)
            _unfold(dvf, dv_ref, dil)

        _per_group(run)

    slab = jax.ShapeDtypeStruct((t, N_PAIRS * PAIR), F32)
    own = pl.BlockSpec((t, PAIR), lambda p: (0, p))
    return pl.pallas_call(
        body, name="attn_bwd", grid=(N_PAIRS,),
        in_specs=[own, pl.BlockSpec((t, PAIR), lambda p: (0, N_PAIRS + p)),
                  pl.BlockSpec((t, PAIR), lambda p: (0, V_COL0 + p)), own, own],
        out_specs=[own] * 3, out_shape=[slab] * 3,
        scratch_shapes=[pltpu.VMEM((t, PAIR), F32)] * 10,
        compiler_params=_cparams(dimension_semantics=("parallel",)),
    )(qk, qk, z, do, dlse)


def _f_comb(o1, o2, o3, l1, l2, l3):
    m = jnp.maximum(jnp.maximum(l1, l2), l3)
    e1, e2, e3 = jnp.exp(l1 - m), jnp.exp(l2 - m), jnp.exp(l3 - m)
    den = e1 + e2 + e3
    return (e1 / den) * o1 + (e2 / den) * o2 + (e3 / den) * o3


def _all_gather_hbm(name, arrs):
    na = len(arrs)

    def body(*refs):
        x_refs, out_refs = refs[:na], refs[na:2 * na]
        send_sems, recv_sems, local_sems = refs[2 * na:]
        mx, my, mc = lax.axis_index("x"), lax.axis_index("y"), lax.axis_index("c")
        me, sibling = (mx, my, mc), (mx, my, 1 - mc)
        chips = [(1 - mx, my), (mx, 1 - my), (1 - mx, 1 - my)]

        def slot(a, px, py, pc):
            return out_refs[a].at[4 * px + 2 * py + pc]

        def copy(a, k, block, to, src=None):
            return pltpu.make_async_remote_copy(
                src_ref=slot(a, *block) if src is None else src, dst_ref=slot(a, *block),
                send_sem=send_sems.at[a, k], recv_sem=recv_sems.at[a, k], device_id=to, device_id_type=MESH)

        mine = [pltpu.make_async_copy(x_refs[a], slot(a, *me), local_sems.at[a]) for a in range(na)]
        for cp in mine:
            cp.start()
        first = []
        for a in range(na):
            first.append(copy(a, 0, me, sibling, src=x_refs[a]))
            first += [copy(a, 1 + j, me, (*chip, mc), src=x_refs[a]) for j, chip in enumerate(chips)]
        for cp in first:
            cp.start()
        passed = []
        for j, chip in enumerate(chips):
            for a in range(na):
                copy(a, 1 + j, (*chip, mc), me).wait_recv()
                passed.append(copy(a, 4 + j, (*chip, mc), sibling))
                passed[-1].start()
        for a in range(na):
            copy(a, 0, sibling, me).wait_recv()
            for j, chip in enumerate(chips):
                copy(a, 4 + j, (*chip, 1 - mc), me).wait_recv()
        for cp in first + passed:
            cp.wait_send()
        for cp in mine:
            cp.wait()

    hbm = pl.BlockSpec(memory_space=pl.ANY)
    return pl.pallas_call(
        body, name=name,
        out_shape=[jax.ShapeDtypeStruct((N_DEV,) + a.shape, a.dtype) for a in arrs],
        in_specs=[hbm] * na, out_specs=[hbm] * na,
        scratch_shapes=[pltpu.SemaphoreType.DMA((na, 7)), pltpu.SemaphoreType.DMA((na, 7)),
                        pltpu.SemaphoreType.DMA((na,))],
    )(*arrs)


def _all_gather_vmem(x):
    rws, cols = x.shape

    def body(x_ref, out_ref, send_sems, recv_sems):
        mx, my, mc = lax.axis_index("x"), lax.axis_index("y"), lax.axis_index("c")
        me, sibling = (mx, my, mc), (mx, my, 1 - mc)
        chips = [(1 - mx, my), (mx, 1 - my), (1 - mx, 1 - my)]

        def slot(px, py, pc):
            return out_ref.at[4 * px + 2 * py + pc]

        def copy(k, block, to, src=None):
            return pltpu.make_async_remote_copy(
                src_ref=slot(*block) if src is None else src, dst_ref=slot(*block),
                send_sem=send_sems.at[k], recv_sem=recv_sems.at[k], device_id=to, device_id_type=MESH)

        first = [copy(0, me, sibling, src=x_ref)]
        first += [copy(1 + j, me, (*chip, mc), src=x_ref) for j, chip in enumerate(chips)]
        for cp in first:
            cp.start()
        out_ref[4 * mx + 2 * my + mc] = x_ref[...]
        passed = [copy(4 + j, (*chip, mc), sibling) for j, chip in enumerate(chips)]
        for j, chip in enumerate(chips):
            copy(1 + j, (*chip, mc), me).wait_recv()
            passed[j].start()
        copy(0, sibling, me).wait_recv()
        for j, chip in enumerate(chips):
            copy(4 + j, (*chip, 1 - mc), me).wait_recv()
        for cp in first + passed:
            cp.wait_send()

    return pl.pallas_call(
        body, name="all_gather_small",
        out_shape=jax.ShapeDtypeStruct((N_DEV, rws, cols), x.dtype),
        in_specs=[pl.BlockSpec(memory_space=pltpu.VMEM)], out_specs=pl.BlockSpec(memory_space=pltpu.VMEM),
        scratch_shapes=[pltpu.SemaphoreType.DMA((7,)), pltpu.SemaphoreType.DMA((7,))],
    )(x)


def _scatter_copies(g_refs, land_refs, send_sems, recv_sems):
    mx, my, mc = lax.axis_index("x"), lax.axis_index("y"), lax.axis_index("c")
    me = 4 * mx + 2 * my + mc
    copies = []
    for k in range(1, N_DEV):
        px, py, pc = mx ^ (k >> 2), my ^ ((k >> 1) & 1), mc ^ (k & 1)
        peer = 4 * px + 2 * py + pc
        for a, (g_ref, land_ref) in enumerate(zip(g_refs, land_refs)):
            copies.append(pltpu.make_async_remote_copy(
                src_ref=g_ref.at[peer], dst_ref=land_ref.at[me], send_sem=send_sems.at[a * (N_DEV - 1) + k - 1],
                recv_sem=recv_sems.at[a * (N_DEV - 1) + k - 1], device_id=(px, py, pc), device_id_type=MESH))
    return copies


_HBM = pl.BlockSpec(memory_space=pltpu.HBM)
_SEM = pl.BlockSpec(memory_space=pltpu.SEMAPHORE)
_DATAFLOW = pltpu.SideEffectType.DATAFLOW_SIDE_EFFECTING


def _gather_copies(x_refs, land_refs, send_sems, recv_sems):
    mx, my, mc = lax.axis_index("x"), lax.axis_index("y"), lax.axis_index("c")
    me = 4 * mx + 2 * my + mc
    copies = []
    for k in range(1, N_DEV):
        px, py, pc = mx ^ (k >> 2), my ^ ((k >> 1) & 1), mc ^ (k & 1)
        for a, (x_ref, land_ref) in enumerate(zip(x_refs, land_refs)):
            copies.append(pltpu.make_async_remote_copy(
                src_ref=x_ref, dst_ref=land_ref.at[me], send_sem=send_sems.at[a * (N_DEV - 1) + k - 1],
                recv_sem=recv_sems.at[a * (N_DEV - 1) + k - 1], device_id=(px, py, pc), device_id_type=MESH))
    return copies


N_CHIPS = N_DEV // 2


def _gather2_first(x_refs, land_refs, send_sems, recv_sems, ks=(0, 1, 2, 3)):
    mx, my, mc = lax.axis_index("x"), lax.axis_index("y"), lax.axis_index("c")
    me = 4 * mx + 2 * my + mc
    peers = [(mx, my, 1 - mc), (1 - mx, my, mc), (mx, 1 - my, mc), (1 - mx, 1 - my, mc)]
    return [pltpu.make_async_remote_copy(
        src_ref=x_ref, dst_ref=land_ref.at[me], send_sem=send_sems.at[a * (N_DEV - 1) + k],
        recv_sem=recv_sems.at[a * (N_DEV - 1) + k], device_id=peers[k], device_id_type=MESH)
        for a, (x_ref, land_ref) in enumerate(zip(x_refs, land_refs)) for k in ks]


def _gather2_second(x_refs, land_refs, send_sems, recv_sems):
    mx, my, mc = lax.axis_index("x"), lax.axis_index("y"), lax.axis_index("c")
    copies = []
    for a, land_ref in enumerate(land_refs):
        for j, (px, py) in enumerate([(1 - mx, my), (mx, 1 - my), (1 - mx, 1 - my)]):
            slot = 4 * px + 2 * py + mc
            copies.append(pltpu.make_async_remote_copy(
                src_ref=land_ref.at[slot], dst_ref=land_ref.at[slot], send_sem=send_sems.at[a * (N_DEV - 1) + j],
                recv_sem=recv_sems.at[a * (N_DEV - 1) + j], device_id=(mx, my, 1 - mc), device_id_type=MESH))
    return copies


def _pair_swap(name, arrs):
    na = len(arrs)

    def body(*refs):
        g_refs, out_refs = refs[:na], refs[na:2 * na]
        send_sems, recv_sems = refs[2 * na:]
        mx, my, mc = lax.axis_index("x"), lax.axis_index("y"), lax.axis_index("c")
        copies = [pltpu.make_async_remote_copy(
            src_ref=g_refs[a].at[2 * q + 1 - mc], dst_ref=out_refs[a].at[q], send_sem=send_sems.at[a, q],
            recv_sem=recv_sems.at[a, q], device_id=(mx, my, 1 - mc), device_id_type=MESH)
            for a in range(na) for q in range(N_CHIPS)]
        for cp in copies:
            cp.start()
        for cp in copies:
            cp.wait_recv()
        for cp in copies:
            cp.wait_send()

    hbm = pl.BlockSpec(memory_space=pl.ANY)
    return pl.pallas_call(
        body, name=name,
        out_shape=[jax.ShapeDtypeStruct((N_CHIPS,) + a.shape[1:], a.dtype) for a in arrs],
        in_specs=[hbm] * na, out_specs=[hbm] * na,
        scratch_shapes=[pltpu.SemaphoreType.DMA((na, N_CHIPS)), pltpu.SemaphoreType.DMA((na, N_CHIPS))],
    )(*arrs)


def _pair_add(name, g, half, core):
    _, r, c = g.shape
    tr = _tile(r, 512) if r % LANES == 0 else r

    def body(core_ref, g_ref, h_ref, o_ref):
        o_ref[...] = (g_ref[...].astype(F32) + h_ref[...].astype(F32)).astype(o_ref.dtype)

    return pl.pallas_call(
        body, name=name,
        grid_spec=pltpu.PrefetchScalarGridSpec(
            num_scalar_prefetch=1, grid=(N_CHIPS, r // tr),
            in_specs=[pl.BlockSpec((None, tr, c), lambda q, i, core_ref: (2 * q + core_ref[0], i, 0)),
                      pl.BlockSpec((None, tr, c), lambda q, i, core_ref: (q, i, 0))],
            out_specs=pl.BlockSpec((None, tr, c), lambda q, i, core_ref: (q, i, 0))),
        out_shape=jax.ShapeDtypeStruct((N_CHIPS, r, c), g.dtype),
        compiler_params=_cparams(dimension_semantics=("parallel", "parallel")),
    )(core, g, half)


def _chip_copies(h_refs, land_refs, send_sems, recv_sems):
    mx, my, mc = lax.axis_index("x"), lax.axis_index("y"), lax.axis_index("c")
    my_chip = 2 * mx + my
    copies = []
    for k in range(1, N_CHIPS):
        px, py = mx ^ (k >> 1), my ^ (k & 1)
        for a, (h_ref, land_ref) in enumerate(zip(h_refs, land_refs)):
            copies.append(pltpu.make_async_remote_copy(
                src_ref=h_ref.at[2 * px + py], dst_ref=land_ref.at[my_chip], send_sem=send_sems.at[a * (N_DEV - 1) + k - 1],
                recv_sem=recv_sems.at[a * (N_DEV - 1) + k - 1], device_id=(px, py, mc), device_id_type=MESH))
    return copies


def _exchange_start(name, copies, srcs, lands, after):
    na = len(srcs)

    def body(*refs):
        for cp in copies(refs[:na], refs[na:2 * na], refs[2 * na + 1], refs[2 * na + 2]):
            cp.start()
        refs[-1][...] = jnp.zeros_like(refs[-1])

    in_hbm = lambda a: pltpu.with_memory_space_constraint(a, pltpu.HBM)
    outs = pl.pallas_call(
        body, name=name,
        out_shape=(pltpu.SemaphoreType.DMA((na * (N_DEV - 1),)), pltpu.SemaphoreType.DMA((na * (N_DEV - 1),)),
                   *[pltpu.HBM(a.shape, a.dtype) for a in list(srcs) + list(lands)],
                   jax.ShapeDtypeStruct((8, LANES), F32)),
        in_specs=[_HBM] * (2 * na) + [pl.BlockSpec(memory_space=pl.ANY)],
        out_specs=(_SEM, _SEM, *[_HBM] * (2 * na), pl.BlockSpec(memory_space=pltpu.VMEM)),
        input_output_aliases={i: 2 + i for i in range(2 * na)},
        compiler_params=pltpu.CompilerParams(has_side_effects=_DATAFLOW),
    )(*[in_hbm(a) for a in srcs], *[in_hbm(a) for a in lands], after)
    return outs[0], outs[1], outs[2:2 + na], outs[2 + na:2 + 2 * na], outs[-1]


def _exchange_wait(name, copies, send_sems, recv_sems, srcs, lands, after, parts=("send", "recv")):
    na = len(srcs)

    def body(*refs):
        for cp in copies(refs[:na], refs[na:2 * na], refs[2 * na], refs[2 * na + 1]):
            if "send" in parts:
                cp.wait_send()
            if "recv" in parts:
                cp.wait_recv()

    outs = pl.pallas_call(
        body, name=name,
        out_shape=[pltpu.HBM(a.shape, a.dtype) for a in list(srcs) + list(lands)],
        in_specs=[_HBM] * (2 * na) + [_SEM, _SEM, pl.BlockSpec(memory_space=pl.ANY)], out_specs=[_HBM] * (2 * na),
        input_output_aliases={i: i for i in range(2 * na)},
        compiler_params=pltpu.CompilerParams(has_side_effects=_DATAFLOW),
    )(*srcs, *lands, send_sems, recv_sems, after)
    return outs[:na], outs[na:]


def _sum_slots(name, g, tr):
    _, rws, cols = g.shape

    def body(g_ref, o_ref):
        acc = g_ref[0].astype(F32)
        for j in range(1, N_DEV):
            acc = acc + g_ref[j].astype(F32)
        o_ref[...] = acc

    return pl.pallas_call(
        body, name=name, grid=(rws // tr,),
        in_specs=[pl.BlockSpec((N_DEV, tr, cols), lambda i: (0, i, 0))],
        out_specs=pl.BlockSpec((tr, cols), lambda i: (i, 0)),
        out_shape=jax.ShapeDtypeStruct((rws, cols), F32),
        compiler_params=_cparams(dimension_semantics=("parallel",)),
    )(g)


def _adam_math(wv, gv, mv, vv):
    mn = ADAM_B1 * mv + (1.0 - ADAM_B1) * gv
    vn = ADAM_B2 * vv + (1.0 - ADAM_B2) * jnp.square(gv)
    m_hat = mn / (1.0 - ADAM_B1 ** ADAM_STEP)
    v_hat = vn / (1.0 - ADAM_B2 ** ADAM_STEP)
    delta = -ADAM_LR * (m_hat / (jnp.sqrt(v_hat) + ADAM_EPS) + ADAM_WD * wv)
    return delta, mn, vn


def _adamw(name, w, g, m, v, tr):
    return _rowwise(name, _adam_math, [w, g, m, v], [], [(LANES, F32)] * 3, tr=tr)


def _adamw_slots(name, recv, own, slot, rb, cb, cw, w, m, v, tr, order):
    nr, nc = w.shape
    n = recv.shape[0]

    def body(slot_ref, g_ref, own_ref, w_ref, m_ref, v_ref, order_ref, go_ref, d_ref, mo_ref, vo_ref):
        acc = None
        for s in range(n):
            part = jnp.where(slot_ref[0] == s, own_ref[...], g_ref[s]).astype(F32)
            acc = part if acc is None else acc + part
        g = acc[:, :nc]
        go_ref[...] = g
        d_ref[...], mo_ref[...], vo_ref[...] = _adam_math(w_ref[...], g, m_ref[...], v_ref[...])

    nat = pl.BlockSpec((tr, nc), lambda i, slot_ref: (i, 0))
    return pl.pallas_call(
        body, name=name,
        grid_spec=pltpu.PrefetchScalarGridSpec(
            num_scalar_prefetch=1, grid=(nr // tr,),
            in_specs=[pl.BlockSpec((n, tr, cw), lambda i, slot_ref: (0, rb + i, cb)),
                      pl.BlockSpec((None, tr, cw), lambda i, slot_ref: (slot_ref[0], rb + i, cb)),
                      nat, nat, nat, pl.BlockSpec(memory_space=pl.ANY)],
            out_specs=[nat] * 4),
        out_shape=[jax.ShapeDtypeStruct((nr, nc), F32)] * 4,
        compiler_params=_cparams(dimension_semantics=("parallel",)),
    )(slot, recv, own, w, m, v, order)


def _local_blocks(w):
    pad_cols = lambda a: jnp.pad(a, ((0, 0), (0, FF_PAD - FF_SHARD)))
    pad_rows = lambda a: jnp.pad(a, ((0, FF_PAD - FF_SHARD), (0, 0)))
    gate_up = lambda tag: jnp.concatenate([pad_cols(w[tag + "_w_gate"]), pad_cols(w[tag + "_w_up"])], axis=1)
    blocks = {
        "ffn1_gu": gate_up("ffn1"), "ffn1_d": pad_rows(w["ffn1_w_down"]), "w_in": w["w_in"],
        "lora": jnp.concatenate([w["rwkv_w2"], w["rwkv_a2"], w["rwkv_g2"]], axis=0),
        "br": jnp.concatenate([w["w_br_rwkv"], w["w_br_attn"], w["ple_w_proj"]], axis=0),
        "w_out": w["w_out"], "ffn2_gu": gate_up("ffn2"), "ffn2_d": pad_rows(w["ffn2_w_down"]),
        "ple_gate": w["ple_w_gate"],
    }
    return {n: a.astype(BF16) for n, a in blocks.items()}


GATHER_GROUPS = {"head": ("ffn1_gu", "ffn1_d"), "mid": ("w_in", "lora"),
                 "rest": ("br", "w_out", "ffn2_gu", "ffn2_d", "ple_gate")}

SCATTER_GROUPS = {"tail": ("ple_gate", "ple_proj", "ffn2_gu", "ffn2_d"), "branch": ("w_out", "br"),
                  "mixer": ("lora", "w_in"),
                  "ffn1_down": ("ffn1_d",), "head": ("ffn1_gu",)}

TWO_LEVEL = ("mixer", "head")

ADAM_PLAN = (
    ("ffn1_w_gate", "ffn1_gu", 0, 0, FF_PAD, 256), ("ffn1_w_up", "ffn1_gu", 0, 1, FF_PAD, 256),
    ("ffn1_w_down", "ffn1_d", 0, 0, D_MODEL, FF_SHARD // 2), ("w_in", "w_in", 0, 0, IN_SHARD, 256),
    ("rwkv_w2", "lora", 0, 0, HEAD_DIM, 64), ("rwkv_a2", "lora", 1, 0, HEAD_DIM, 64),
    ("rwkv_g2", "lora", 2, 0, HEAD_DIM, 64),
    ("w_br_rwkv", "br", 0, 0, OUT_SHARD, 256), ("w_br_attn", "br", 2, 0, OUT_SHARD, 256),
    ("ple_w_proj", "ple_proj", 0, 0, OUT_SHARD, 256), ("w_out", "w_out", 0, 0, D_MODEL, OUT_SHARD),
    ("ffn2_w_gate", "ffn2_gu", 0, 0, FF_PAD, 256), ("ffn2_w_up", "ffn2_gu", 0, 1, FF_PAD, 256),
    ("ffn2_w_down", "ffn2_d", 0, 0, D_MODEL, FF_SHARD // 2), ("ple_w_gate", "ple_gate", 0, 0, D_MODEL, OUT_SHARD),
)


def _pack_small(arrs, rows):
    flat = jnp.concatenate([a.reshape(-1) for a in arrs])
    return jnp.pad(flat, (0, rows * LANES - flat.shape[0])).reshape(rows, LANES)


def _unpack_small(flat, like):
    flat = flat.reshape(-1)
    out, off = [], 0
    for a in like:
        out.append(flat[off:off + a.size].reshape(a.shape))
        off += a.size
    return out


def _local_step(x, p, pos, target, sm, wg, fetch, on_grads):
    t = x.shape[0]

    w_d1 = wg["ffn1_d"].reshape(FF_HID, D_MODEL)
    x1, ffn1_saved, _ = _ffn_fwd("ffn1", x, sm["ffn1_norm"], wg["ffn1_gu"], lambda after: w_d1)
    fetch("forward", x1)
    h2 = _norm_fwd("mix_norm", x1, sm["mix_norm"])
    wg = {**wg, **fetch("mid", h2)}
    full_cols = lambda blk: blk.transpose(1, 0, 2).reshape(blk.shape[1], N_DEV * blk.shape[2])
    lora_w2 = full_cols(wg["lora"][:, :DECAY_LORA])
    lora_a2 = full_cols(wg["lora"][:, DECAY_LORA:DECAY_LORA + ICLR_LORA])
    lora_g2 = full_cols(wg["lora"][:, DECAY_LORA + ICLR_LORA:])
    z =_mmc_nn("w_in", h2, wg["w_in"], 0, 0, IN_SHARD)
    z_g = (z, 2 * D_MODEL, (RWKV_COLS + 3 * ATTN_DIM) // (2 * D_MODEL))

    r, k, v, lo, gd = _shift_fwd(z, sm["rwkv_mu"])
    zero_lo = jnp.zeros((DECAY_LORA, RWKV_DIM), BF16)
    w2p = jnp.concatenate([lora_w2, zero_lo], axis=0).astype(F32)
    a2p = jnp.concatenate([zero_lo, lora_a2], axis=0).astype(F32)
    pre_params = [sm["rwkv_w0"], w2p, sm["rwkv_a0"], a2p, lora_g2.astype(F32), sm["rwkv_k_k"], sm["rwkv_k_a"]]
    wide = [(RWKV_DIM, F32)]
    k2, kk, a, decay, g = _rowwise("rwkv_pre", _f_pre, [k, lo, gd], pre_params, wide * 5, tr=BIG_ROWS)
    kkn, b = _pairwise("rwkv_kk", _f_kk, [kk, a], [], 2)
    scan_in = [u.reshape(t, RWKV_HEADS, HEAD_DIM) for u in (r, decay, k2, kkn, b)]
    v_rows = _to_v_rows(v)
    y_rows, states = _wkv_fwd(*scan_in, v_rows)
    y = _from_v_rows(y_rows)
    post_params = [sm["rwkv_gn_w"], sm["rwkv_gn_b"], sm["rwkv_r_k"]]
    post_rows = [y, r, k2, v, g]
    y_rwkv = _pairwise("rwkv_post", lambda *av: (_f_post(*av),), post_rows, post_params, 1)[0]

    inv_freq = 1.0 / (ROPE_THETA ** (jnp.arange(0, HEAD_DIM, 2, dtype=F32) / HEAD_DIM))
    freq2 = jnp.tile(inv_freq, 2 * PAIR // HEAD_DIM).reshape(1, PAIR)
    half = jnp.ones((HEAD_DIM // 2,), F32)
    sign2 = jnp.tile(jnp.concatenate([-half, half]), PAIR // HEAD_DIM).reshape(1, PAIR)

    def rope_table(posv, fr, sg):
        ang = posv * fr
        return (jnp.concatenate([jnp.cos(ang), jnp.sin(ang) * sg], axis=1),)
    tab = _rowwise("rope_table", rope_table, [pos.astype(F32).reshape(t, 1)], [freq2, sign2], [(2 * PAIR, F32)])[0]
    qk = _qk_prep(z, tab, sm["q_norm"], sm["k_norm"])
    o_all, lse_all = _attn_fwd(qk, z)
    gw = HEADS_PER_GROUP * HEAD_DIM

    def by_group(ov, lv):
        return [ov[:, i * gw:(i + 1) * gw] for i in range(3)] + [lv[:, i * gw:(i + 1) * gw] for i in range(3)]
    y_attn = _rowwise("attn_comb", lambda ov, lv: (_f_comb(*by_group(ov, lv)),), [o_all, lse_all], [], [(gw, F32)],
                      tr=BIG_ROWS)[0]

    wg = {**wg, **fetch("rest", y_rwkv)}
    w_d2 = wg["ffn2_d"].reshape(FF_HID, D_MODEL)
    w_out = wg["w_out"].reshape(D_MODEL, D_MODEL)
    w_pg = wg["ple_gate"].reshape(D_MODEL, D_MODEL)
    w_brr = full_cols(wg["br"][:, :RWKV_DIM])
    w_bra = full_cols(wg["br"][:, RWKV_DIM:RWKV_DIM + gw])
    w_pp = full_cols(wg["br"][:, RWKV_DIM + gw:])
    u_r = _mm("br_rwkv", y_rwkv, w_brr, "nn")
    u_a = _mm("br_attn", y_attn, w_bra, "nn")

    def f_merge(zgr, zga, ur, ua):
        return _sigmoid(zgr) * ur + _sigmoid(zga) * ua
    merged = _rowwise("merge", lambda zg, ur, ua: (f_merge(zg[:, :D_MODEL], zg[:, D_MODEL:], ur, ua),),
                      [z_g, u_r, u_a], [], [(D_MODEL, BF16)], tr=BIG_ROWS)[0]
    x2 = _mm("w_out", merged, w_out, "nn", res=x1)
    x3, ffn2_saved, _ = _ffn_fwd("ffn2", x2, sm["ffn2_norm"], wg["ffn2_gu"], lambda after: w_d2)

    hn = _norm_fwd("ple_norm", x3, sm["ple_norm"])
    gz = _mm("ple_gate", hn, w_pg, "nn")
    pp = _mm("ple_proj", p, w_pp, "nn")

    def f_head(x3v, gzv, ppv, tg):
        sg = _sigmoid(gzv)
        err = x3v + sg * ppv - tg
        part = 0.5 * jnp.sum(jnp.mean(err * err, axis=-1, keepdims=True))
        dx4 = err * (1.0 / D_MODEL)
        return dx4, dx4 * ppv * sg * (1.0 - sg), dx4 * sg, jnp.full((1, LANES), part, F32)
    dx4, dgz, dpp, loss_row = _rowwise("ple_loss", f_head, [x3, gz, pp, target], [],
                                       [(D_MODEL, F32), (D_MODEL, BF16), (D_MODEL, BF16)], [(1, LANES)])
    loss = loss_row[0, 0]

    gs, gm = {}, {}
    row_blocks = lambda g: g.reshape(N_DEV, g.shape[0] // N_DEV, g.shape[1])
    dhn = _mm("ple_dhn", dgz, w_pg, "nt")
    gm["ple_gate"] = row_blocks(_mm("ple_dwgate", hn, dgz, "tn", out_dtype=BF16))
    col_blocks = lambda g: g.reshape(g.shape[0], N_DEV, g.shape[1] // N_DEV).transpose(1, 0, 2)
    gm["ple_proj"] = col_blocks(_mm("ple_dwproj", p, dpp, "tn", out_dtype=BF16))
    dx3, gs["ple_norm"] = _norm_bwd("ple_dnorm", x3, sm["ple_norm"], dhn, dx4)

    dx2, gs["ffn2_norm"], gm["ffn2_gu"], gm["ffn2_d"] = _ffn_bwd(
        "ffn2", x2, sm["ffn2_norm"], wg["ffn2_gu"], w_d2, ffn2_saved, dx3)
    tail_token = on_grads("tail", {n: gm.pop(n) for n in SCATTER_GROUPS["tail"]})

    dmerged = _mm("w_out_dmerged", dx2, w_out, "nt")
    gm["w_out"] = row_blocks(_mm("w_out_dw", merged, dx2, "tn", out_dtype=BF16))

    def merge_bwd(zg, ur, ua, dm):
        _, vjp = jax.vjp(f_merge, zg[:, :D_MODEL], zg[:, D_MODEL:], ur, ua)
        dzr, dza, dur, dua = vjp(dm)
        return jnp.concatenate([dzr, dza], axis=1), dur, dua
    dz_g, du_r, du_a = _rowwise("merge_bwd", merge_bwd, [z_g, u_r, u_a, dmerged], [],
                                [(2 * D_MODEL, BF16), (D_MODEL, BF16), (D_MODEL, BF16)])
    dy_rwkv = _mm("br_rwkv_dy", du_r, w_brr, "nt")
    dy_attn = _mm("br_attn_dy", du_a, w_bra, "nt")
    gm["br"] = jnp.concatenate([col_blocks(_mm("br_rwkv_dw", y_rwkv, du_r, "tn", out_dtype=BF16)),
                                col_blocks(_mm("br_attn_dw", y_attn, du_a, "tn", out_dtype=BF16))], axis=1)
    branch_token = on_grads("branch", {n: gm.pop(n) for n in SCATTER_GROUPS["branch"]})

    def comb_bwd(ov, lv, dyv):
        _, vjp = jax.vjp(_f_comb, *by_group(ov, lv))
        d = vjp(dyv)
        return jnp.concatenate(d[:3], axis=1), jnp.concatenate(d[3:], axis=1)
    do_all, dl_all = _rowwise("attn_comb_bwd", comb_bwd, [o_all, lse_all, dy_attn], [],
                              [(ATTN_DIM, F32), (ATTN_DIM, F32)], tr=BIG_ROWS)
    dq_all, dk_all, dv_all = _attn_bwd(qk, z, do_all, dl_all)
    dqk_raw, gs["q_norm"], gs["k_norm"] = _qk_prep_bwd(z, tab, sm["q_norm"], sm["k_norm"], dq_all, dk_all)

    def post_bwd(yv, rv, k2v, vv, gv, dv_, gnw, gnb, rk):
        _, vjp = jax.vjp(_f_post, yv, rv, k2v, vv, gv, gnw, gnb, rk)
        return vjp(dv_)
    dy, dr1, dk2a, dv1, dg, d_gnw, d_gnb, d_rk = _pairwise(
        "rwkv_post_bwd", post_bwd, post_rows + [dy_rwkv], [post_params[0] + tail_token[0, 0] + branch_token[0, 0]] + post_params[1:], 5, 3)
    gs["rwkv_gn_w"], gs["rwkv_gn_b"], gs["rwkv_r_k"] = d_gnw, d_gnb, d_rk
    dr2, ddecay, dk2b, dkkn, db, dv_rows = _wkv_bwd(*scan_in, v_rows, states, _to_v_rows(dy))
    dr2, ddecay, dk2b, dkkn, db = [u.reshape(t, RWKV_DIM) for u in (dr2, ddecay, dk2b, dkkn, db)]
    dv2 = _from_v_rows(dv_rows)

    def kk_bwd(kkv, av, dkknv, dbv, dra, drb, dva, dvb):
        _, vjp = jax.vjp(_f_kk, kkv, av)
        return (*vjp((dkknv, dbv)), dra + drb, dva + dvb)
    dkk, da, dr, dv = _pairwise("rwkv_kk_bwd", kk_bwd, [kk, a, dkkn, db, dr1, dr2, dv1, dv2], [], 4)

    def pre_bwd(kv, lov, gdv, dk2x, dk2y, dkkv, dav, ddec, dgv, w0, w2p_, a0, a2p_, g2, k_k, k_a):
        _, vjp = jax.vjp(_f_pre, kv, lov, gdv, w0, w2p_, a0, a2p_, g2, k_k, k_a)
        return vjp((dk2x + dk2y, dkkv, dav, ddec, dgv))
    lora_acc = (DECAY_LORA + ICLR_LORA, RWKV_DIM)
    dk, dlo, dgd, d_w0, d_w2p, d_a0, d_a2p, d_g2, d_kk, d_ka = _rowwise(
        "rwkv_pre_bwd", pre_bwd,
        [k, lo, gd, dk2a, dk2b, dkk, da, ddecay, dg],
        pre_params, [(RWKV_DIM, F32), (LANES, F32), (LANES, F32)],
        [(1, RWKV_DIM), lora_acc, (1, RWKV_DIM), lora_acc, (GATE_LORA, RWKV_DIM), (1, RWKV_DIM), (1, RWKV_DIM)])
    gs["rwkv_w0"], gs["rwkv_a0"], gs["rwkv_k_k"], gs["rwkv_k_a"] = d_w0, d_a0, d_kk, d_ka
    gm["lora"] = jnp.concatenate([col_blocks(d_w2p[:DECAY_LORA]), col_blocks(d_a2p[DECAY_LORA:]), col_blocks(d_g2)],
                                 axis=1).astype(BF16)
    dz_r, gs["rwkv_mu"] = _shift_bwd(z, sm["rwkv_mu"], dr, dk, dv, dlo, dgd)

    dz = jnp.concatenate([dz_r, dqk_raw, dv_all.astype(BF16), dz_g], axis=1)
    dh2 = _mmc_nt("w_in_dh", dz, wg["w_in"], 0, 0, IN_SHARD, D_MODEL)
    gm["w_in"] = _mmc_tn("w_in_dw", h2, dz, IN_SHARD)
    mixer_token = on_grads("mixer", {n: gm.pop(n) for n in SCATTER_GROUPS["mixer"]})
    dx1, gs["mix_norm"] = _norm_bwd("mix_dnorm", x1, sm["mix_norm"] + mixer_token[0, 0], dh2, dx2)

    dx0, gs["ffn1_norm"], gm["ffn1_gu"], _ = _ffn_bwd(
        "ffn1", x, sm["ffn1_norm"], wg["ffn1_gu"], w_d1, ffn1_saved, dx1,
        on_down=lambda blocks: on_grads("ffn1_down", {"ffn1_d": blocks}))
    return loss, dx0, gm, gs


def kernel(x, p, positions, ffn1_norm, ffn1_w_gate, ffn1_w_up, ffn1_w_down, mix_norm, w_in, rwkv_mu, rwkv_w0, rwkv_w2, rwkv_a0, rwkv_a2, rwkv_g2, rwkv_k_k, rwkv_k_a, rwkv_r_k, rwkv_gn_w, rwkv_gn_b, q_norm, k_norm, w_br_rwkv, w_br_attn, w_out, ffn2_norm, ffn2_w_gate, ffn2_w_up, ffn2_w_down, ple_norm, ple_w_gate, ple_w_proj, loss_target, m_ffn1_norm, m_ffn1_w_gate, m_ffn1_w_up, m_ffn1_w_down, m_mix_norm, m_w_in, m_rwkv_mu, m_rwkv_w0, m_rwkv_w2, m_rwkv_a0, m_rwkv_a2, m_rwkv_g2, m_rwkv_k_k, m_rwkv_k_a, m_rwkv_r_k, m_rwkv_gn_w, m_rwkv_gn_b, m_q_norm, m_k_norm, m_w_br_rwkv, m_w_br_attn, m_w_out, m_ffn2_norm, m_ffn2_w_gate, m_ffn2_w_up, m_ffn2_w_down, m_ple_norm, m_ple_w_gate, m_ple_w_proj, v_ffn1_norm, v_ffn1_w_gate, v_ffn1_w_up, v_ffn1_w_down, v_mix_norm, v_w_in, v_rwkv_mu, v_rwkv_w0, v_rwkv_w2, v_rwkv_a0, v_rwkv_a2, v_rwkv_g2, v_rwkv_k_k, v_rwkv_k_a, v_rwkv_r_k, v_rwkv_gn_w, v_rwkv_gn_b, v_q_norm, v_k_norm, v_w_br_rwkv, v_w_br_attn, v_w_out, v_ffn2_norm, v_ffn2_w_gate, v_ffn2_w_up, v_ffn2_w_down, v_ple_norm, v_ple_w_gate, v_ple_w_proj):
    args = locals()
    w = {n: args[n][0] for n in WEIGHTS}
    m = {n: args["m_" + n][0] for n in WEIGHTS}
    v = {n: args["v_" + n][0] for n in WEIGHTS}

    w_loc = _local_blocks(w)
    head = GATHER_GROUPS["head"]
    wg = dict(zip(head, _all_gather_hbm("gather_head", [w_loc[n] for n in head])))
    me = 4 * lax.axis_index("x") + 2 * lax.axis_index("y") + lax.axis_index("c")
    gathering, order_after = {}, wg[head[0]]
    for group in ("mid", "rest"):
        shards = [w_loc[n] for n in GATHER_GROUPS[group]]
        zones = [lax.dynamic_update_slice(lax.empty((N_DEV,) + a.shape, a.dtype), a[None], (me, 0, 0)) for a in shards]
        copies = _gather2_first if group == "mid" else _gather_copies
        *gathering[group], order_after = _exchange_start("gather_start_" + group, copies, shards, zones, order_after)

    def fetch(group, after):
        if group == "forward":
            s1, r1, srcs, lands = gathering["mid"]
            ici = functools.partial(_gather2_first, ks=(1, 2, 3))
            srcs, lands = _exchange_wait("gather_arrived_mid", ici, s1, r1, srcs, lands, after, parts=("recv",))
            s2, r2, srcs, lands, _ = _exchange_start("gather_forward_mid", _gather2_second, srcs, lands, after)
            gathering["mid"] = (s1, r1, s2, r2, srcs, lands)
            return {}
        if group == "mid":
            s1, r1, s2, r2, srcs, lands = gathering["mid"]
            sib = functools.partial(_gather2_first, ks=(0,))
            srcs, lands = _exchange_wait("gather_passed_mid", _gather2_second, s2, r2, srcs, lands, after)
            srcs, lands = _exchange_wait("gather_sent_mid", _gather2_first, s1, r1, srcs, lands, after, parts=("send",))
            _, got = _exchange_wait("gather_wait_mid", sib, s1, r1, srcs, lands, after, parts=("recv",))
        else:
            _, got = _exchange_wait("gather_wait_" + group, _gather_copies, *gathering[group], after)
        return dict(zip(GATHER_GROUPS[group], got))

    sm = {n: w[n].reshape(1, -1) for n in SMALL}
    sm["ffn1_norm"] = sm["ffn1_norm"] + order_after[0, 0]
    in_flight = {}

    core = lax.axis_index("c").astype(jnp.int32).reshape(1)
    chip_slot = (2 * lax.axis_index("x") + lax.axis_index("y")).astype(jnp.int32).reshape(1)
    device_slot = 2 * chip_slot + core

    def scatter_early(group, arrays):
        arrs = [arrays[n] for n in SCATTER_GROUPS[group]]
        copies, after = _scatter_copies, arrs[0]
        if group in TWO_LEVEL:
            halves = _pair_swap("scatter_pair_" + group, arrs)
            arrs = [_pair_add("scatter_add_%s_%s" % (group, n), a, hf, core)
                    for n, a, hf in zip(SCATTER_GROUPS[group], arrs, halves)]
            copies, after = _chip_copies, halves[0]
        *in_flight[group], token = _exchange_start("scatter_start_" + group, copies, arrs,
                                                   [lax.empty(a.shape, a.dtype) for a in arrs], after)
        return token
    loss_part, dx, gm, gs = _local_step(x[0], p[0, 0], positions[0], loss_target[0], sm, wg, fetch, scatter_early)
    head_token = scatter_early("head", gm)
    recv, own = {}, {}

    def arrived(group, after):
        copies = _chip_copies if group in TWO_LEVEL else _scatter_copies
        sent, lands = _exchange_wait("scatter_wait_" + group, copies, *in_flight[group], after)
        own.update(zip(SCATTER_GROUPS[group], sent))
        recv.update(zip(SCATTER_GROUPS[group], lands))
    for group in ("tail", "branch", "mixer", "ffn1_down"):
        arrived(group, head_token)
    small_like = [w[n] for n in SMALL]
    small_rows = 80
    gs_all = _all_gather_vmem(_pack_small([gs[n] for n in SMALL] + [loss_part], small_rows))
    gs_sum = _sum_slots("sum_small_grads", gs_all, small_rows)
    loss = gs_sum.reshape(-1)[sum(a.size for a in small_like)]

    res = {}
    early = [e for group in ("tail", "branch", "mixer", "ffn1_down") for e in ADAM_PLAN
             if e[1] in SCATTER_GROUPS[group]]
    late = [e for e in ADAM_PLAN if e[1] in SCATTER_GROUPS["head"]]
    order = head_token
    for n, src, rb, cb, cw, tr in early + late:
        if (n, src, rb, cb, cw, tr) == late[0]:
            arrived("head", order)
        slot = chip_slot if recv[src].shape[0] == N_CHIPS else device_slot
        outs4 = _adamw_slots("adamw_" + n, recv[src], own[src], slot, rb, cb, cw, w[n], m[n], v[n], tr, order)
        order = outs4[1]
        for tag, a in zip(("grad", "delta", "new_m", "new_v"), outs4):
            res[tag, n] = a[None]
    d_s, m_s, v_s = _adamw("adamw_small", _pack_small(small_like, small_rows), gs_sum,
                           _pack_small([m[n] for n in SMALL], small_rows),
                           _pack_small([v[n] for n in SMALL], small_rows), small_rows)
    for tag, small in (("grad", gs_sum), ("delta", d_s), ("new_m", m_s), ("new_v", v_s)):
        for n, a in zip(SMALL, _unpack_small(small, small_like)):
            res[tag, n] = a[None]
    outs = [loss, dx[None]]
    for tag in ("grad", "delta", "new_m", "new_v"):
        outs += [res[tag, n] for n in WEIGHTS]
    return tuple(outs)
```

```python
import functools

import jax
import jax.numpy as jnp
from jax import lax
from jax.experimental import pallas as pl
from jax.experimental.pallas import tpu as pltpu

F32, BF16 = jnp.float32, jnp.bfloat16
MESH = pl.DeviceIdType.MESH
N_DEV = 8
LANES = 128
VMEM_LIMIT = 56 * 1024 * 1024

D_MODEL = 1024
PLE_DIM = 256
HEAD_DIM = 64
RWKV_HEADS = 8
RWKV_DIM = RWKV_HEADS * HEAD_DIM
DECAY_LORA = 64
ICLR_LORA = 64
GATE_LORA = 128
GN_EPS = 64e-5
ATTN_GROUPS = ((128, 1), (512, 4), (2048, 16))
HEADS_PER_GROUP = 4
ATTN_HEADS = HEADS_PER_GROUP * len(ATTN_GROUPS)
ATTN_DIM = ATTN_HEADS * HEAD_DIM
BAND_BLOCK = 128
ROPE_THETA = 10000.0
NEG_INF = -1e30
D_FF = 2816
RMS_EPS = 1e-6
RWKV_COLS = 3 * RWKV_DIM + DECAY_LORA + ICLR_LORA + GATE_LORA
ADAM_LR = 0.001
ADAM_B1 = 0.9
ADAM_B2 = 0.999
ADAM_EPS = 1e-08
ADAM_WD = 0.01
ADAM_STEP = 10

V_LO = LANES // RWKV_HEADS
V_HI = HEAD_DIM // V_LO
SCAN_CHUNK = 64
MM_ROWS = 2048
BIG_ROWS = 1024

FF_SHARD = D_FF // N_DEV
FF_PAD = -(-FF_SHARD // LANES) * LANES
FF_HID = N_DEV * FF_PAD
IN_SHARD = 6144 // N_DEV
OUT_SHARD = D_MODEL // N_DEV

SMALL = ("ffn1_norm", "mix_norm", "rwkv_mu", "rwkv_w0", "rwkv_a0", "rwkv_k_k", "rwkv_k_a", "rwkv_r_k",
         "rwkv_gn_w", "rwkv_gn_b", "q_norm", "k_norm", "ffn2_norm", "ple_norm")
WEIGHTS = ("ffn1_norm", "ffn1_w_gate", "ffn1_w_up", "ffn1_w_down", "mix_norm", "w_in", "rwkv_mu", "rwkv_w0",
           "rwkv_w2", "rwkv_a0", "rwkv_a2", "rwkv_g2", "rwkv_k_k", "rwkv_k_a", "rwkv_r_k", "rwkv_gn_w",
           "rwkv_gn_b", "q_norm", "k_norm", "w_br_rwkv", "w_br_attn", "w_out", "ffn2_norm", "ffn2_w_gate",
           "ffn2_w_up", "ffn2_w_down", "ple_norm", "ple_w_gate", "ple_w_proj")


def _cparams(**kw):
    return pltpu.CompilerParams(vmem_limit_bytes=VMEM_LIMIT, **kw)


def _tile(n, cap):
    best = None
    for t in range(LANES, min(n, cap) + 1, LANES):
        if n % t == 0:
            best = t
    return best if best is not None else n


@jax.custom_vjp
def _bdot(a, w):
    return jnp.dot(a.astype(BF16), w.astype(BF16), preferred_element_type=F32)


def _bdot_fwd(a, w):
    return _bdot(a, w), (a, w)


def _bdot_bwd(res, g):
    a, w = res
    gb = g.astype(BF16)
    da = lax.dot_general(gb, w.astype(BF16), (((1,), (1,)), ((), ())), preferred_element_type=F32)
    dw = lax.dot_general(a.astype(BF16), gb, (((0,), (0,)), ((), ())), preferred_element_type=F32)
    return da.astype(a.dtype), dw.astype(w.dtype)


_bdot.defvjp(_bdot_fwd, _bdot_bwd)


@jax.custom_vjp
def _bdot_nt(a, b):
    return lax.dot_general(a.astype(BF16), b.astype(BF16), (((1,), (1,)), ((), ())), preferred_element_type=F32)


def _bdot_nt_fwd(a, b):
    return _bdot_nt(a, b), (a, b)


def _bdot_nt_bwd(res, g):
    a, b = res
    gb = g.astype(BF16)
    da = jnp.dot(gb, b.astype(BF16), preferred_element_type=F32)
    db = lax.dot_general(gb, a.astype(BF16), (((0,), (0,)), ((), ())), preferred_element_type=F32)
    return da.astype(a.dtype), db.astype(b.dtype)


_bdot_nt.defvjp(_bdot_nt_fwd, _bdot_nt_bwd)


def _mm(name, a, b, mode, out_dtype=F32, res=None, scale=None):
    if mode == "nn":
        (m, k), n = a.shape, b.shape[1]
    elif mode == "nt":
        (m, k), n = a.shape, b.shape[0]
    else:
        (k, m), n = a.shape, b.shape[1]
    tm, tn = _tile(m, 1024), _tile(n, 1024)
    a_spec = pl.BlockSpec((k, tm), lambda i, j: (0, i)) if mode == "tn" else pl.BlockSpec((tm, k), lambda i, j: (i, 0))
    b_spec = pl.BlockSpec((tn, k), lambda i, j: (j, 0)) if mode == "nt" else pl.BlockSpec((k, tn), lambda i, j: (0, j))
    dims = {"nn": ((1,), (0,)), "nt": ((1,), (1,)), "tn": ((0,), (0,))}[mode]
    o_spec = pl.BlockSpec((tm, tn), lambda i, j: (i, j))
    ins, in_specs = [a, b], [a_spec, b_spec]
    if res is not None:
        ins.append(res)
        in_specs.append(o_spec)

    def body(*refs):
        acc = lax.dot_general(refs[0][...].astype(BF16), refs[1][...].astype(BF16), (dims, ((), ())),
                              preferred_element_type=F32)
        if scale is not None:
            acc = acc * scale
        if res is not None:
            acc = acc + refs[2][...].astype(F32)
        refs[-1][...] = acc.astype(refs[-1].dtype)

    return pl.pallas_call(
        body, name=name, grid=(m // tm, n // tn), in_specs=in_specs, out_specs=o_spec,
        out_shape=jax.ShapeDtypeStruct((m, n), out_dtype),
        compiler_params=_cparams(dimension_semantics=("parallel", "parallel")),
    )(*ins)


def _mmc_nn(name, a, wb, ki, ci, n, out_dtype=F32):
    m, k = a.shape
    tm = _tile(m, MM_ROWS)

    def body(a_ref, w_ref, o_ref):
        o_ref[...] = jnp.dot(a_ref[...].astype(BF16), w_ref[...], preferred_element_type=F32).astype(o_ref.dtype)

    return pl.pallas_call(
        body, name=name, grid=(m // tm, N_DEV),
        in_specs=[pl.BlockSpec((tm, k), lambda i, j: (i, 0)), pl.BlockSpec((None, k, n), lambda i, j: (j, ki, ci))],
        out_specs=pl.BlockSpec((tm, n), lambda i, j: (i, j)),
        out_shape=jax.ShapeDtypeStruct((m, N_DEV * n), out_dtype),
        compiler_params=_cparams(dimension_semantics=("parallel", "parallel")),
    )(a, wb)


def _mmc_nt(name, a, wb, ki, ci, n, k, res=None):
    m = a.shape[0]
    tm = _tile(m, MM_ROWS)
    o_spec = pl.BlockSpec((tm, k), lambda i, j: (i, 0))
    ins = [a, wb] + ([res] if res is not None else [])
    in_specs = [pl.BlockSpec((tm, n), lambda i, j: (i, j)), pl.BlockSpec((None, k, n), lambda i, j: (j, ki, ci))]
    in_specs += [o_spec] if res is not None else []

    def body(*refs):
        a_ref, w_ref, o_ref = refs[0], refs[1], refs[-1]
        acc = lax.dot_general(a_ref[...].astype(BF16), w_ref[...], (((1,), (1,)), ((), ())),
                              preferred_element_type=F32)

        @pl.when(pl.program_id(1) == 0)
        def _():
            o_ref[...] = acc + refs[2][...] if res is not None else acc

        @pl.when(pl.program_id(1) != 0)
        def _():
            o_ref[...] += acc

    return pl.pallas_call(
        body, name=name, grid=(m // tm, N_DEV), in_specs=in_specs, out_specs=o_spec,
        out_shape=jax.ShapeDtypeStruct((m, k), F32),
        compiler_params=_cparams(dimension_semantics=("parallel", "arbitrary")),
    )(*ins)


def _mmc_tn(name, x, dy, n):
    m, k = x.shape
    tk = _tile(k, 1024)

    def body(x_ref, dy_ref, o_ref):
        o_ref[...] = lax.dot_general(x_ref[...].astype(BF16), dy_ref[...].astype(BF16), (((0,), (0,)), ((), ())),
                                     preferred_element_type=F32).astype(o_ref.dtype)

    return pl.pallas_call(
        body, name=name, grid=(N_DEV, k // tk),
        in_specs=[pl.BlockSpec((m, tk), lambda j, i: (0, i)), pl.BlockSpec((m, n), lambda j, i: (0, j))],
        out_specs=pl.BlockSpec((None, tk, n), lambda j, i: (j, i, 0)),
        out_shape=jax.ShapeDtypeStruct((N_DEV, k, n), BF16),
        compiler_params=_cparams(dimension_semantics=("parallel", "parallel")),
    )(x, dy)


def _rowwise(name, fn, rows, params, out_rows, out_accs=(), tr=512):
    rows = [a if isinstance(a, tuple) else (a, a.shape[1], 0) for a in rows]
    r = rows[0][0].shape[0]
    in_specs = [pl.BlockSpec((tr, wd), lambda i, cb=cb: (i, cb)) for _, wd, cb in rows]
    rows = [a for a, _, _ in rows]
    in_specs += [pl.BlockSpec(p.shape, lambda i, nd=p.ndim: (0,) * nd) for p in params]
    out_shape = [jax.ShapeDtypeStruct((r, c), dt) for c, dt in out_rows]
    out_shape += [jax.ShapeDtypeStruct(s, F32) for s in out_accs]
    out_specs = [pl.BlockSpec((tr, c), lambda i: (i, 0)) for c, _ in out_rows]
    out_specs += [pl.BlockSpec(s, lambda i, nd=len(s): (0,) * nd) for s in out_accs]
    n_in, n_ro = len(rows) + len(params), len(out_rows)

    def body(*refs):
        res = fn(*[ref[...] for ref in refs[:n_in]])
        outs = refs[n_in:]
        for o, v in zip(outs[:n_ro], res[:n_ro]):
            o[...] = v.astype(o.dtype)
        for o, v in zip(outs[n_ro:], res[n_ro:]):
            _accumulate(o, v)

    return pl.pallas_call(
        body, name=name, grid=(r // tr,), in_specs=in_specs, out_specs=out_specs, out_shape=out_shape,
        compiler_params=_cparams(dimension_semantics=("arbitrary",)),
    )(*rows, *params)


def _pairwise(name, fn, rows, params, n_out, n_acc=0, tr=2048):
    t, c = rows[0].shape
    tile = pl.BlockSpec((tr, 2 * HEAD_DIM), lambda p, i: (i, p))
    vec = pl.BlockSpec((1, 2 * HEAD_DIM), lambda p, i: (0, p))
    n_in = len(rows) + len(params)

    def body(*refs):
        res = fn(*[ref[...] for ref in refs[:n_in]])
        outs = refs[n_in:]
        for o, v in zip(outs[:n_out], res[:n_out]):
            o[...] = v
        first = pl.program_id(1) == 0
        for o, v in zip(outs[n_out:], res[n_out:]):
            @pl.when(first)
            def _(o=o, v=v):
                o[...] = v

            @pl.when(jnp.logical_not(first))
            def _(o=o, v=v):
                o[...] += v

    return pl.pallas_call(
        body, name=name, grid=(c // (2 * HEAD_DIM), t // tr),
        in_specs=[tile] * len(rows) + [vec] * len(params), out_specs=[tile] * n_out + [vec] * n_acc,
        out_shape=[jax.ShapeDtypeStruct((t, c), F32)] * n_out + [jax.ShapeDtypeStruct((1, c), F32)] * n_acc,
        compiler_params=_cparams(dimension_semantics=("parallel", "arbitrary")),
    )(*rows, *params)


def _accumulate(o_ref, v):
    @pl.when(pl.program_id(0) == 0)
    def _():
        o_ref[...] = v

    @pl.when(pl.program_id(0) != 0)
    def _():
        o_ref[...] += v


def _rms(x, g):
    return x * lax.rsqrt(jnp.mean(x * x, axis=-1, keepdims=True) + RMS_EPS) * g


def _sigmoid(x):
    return jax.nn.sigmoid(x)


def _softplus(x):
    return jnp.maximum(x, 0.0) + jnp.log1p(jnp.exp(-jnp.abs(x)))


def _norm_fwd(name, x, g):
    return _rowwise(name, lambda xv, gv: (_rms(xv, gv),), [x], [g], [(x.shape[1], BF16)], tr=BIG_ROWS)[0]


def _norm_bwd(name, x, g, dh, dres):
    def fn(xv, dhv, drv, gv):
        _, vjp = jax.vjp(_rms, xv, gv)
        dx, dg = vjp(dhv)
        return dx + drv, dg
    return _rowwise(name, fn, [x, dh, dres], [g], [(x.shape[1], F32)], [g.shape], tr=BIG_ROWS)


def _f_act(gate, up):
    return gate * _sigmoid(gate) * up


def _gate_up_act(name, h, w_gu):
    m, k = h.shape
    tm = _tile(m, MM_ROWS)

    def body(h_ref, w_ref, gu_ref, a_ref):
        gu = jnp.dot(h_ref[...], w_ref[...], preferred_element_type=F32)
        gu_ref[...] = gu
        a_ref[...] = _f_act(gu[:, :FF_PAD], gu[:, FF_PAD:]).astype(a_ref.dtype)

    return pl.pallas_call(
        body, name=name, grid=(m // tm, N_DEV),
        in_specs=[pl.BlockSpec((tm, k), lambda i, j: (i, 0)),
                  pl.BlockSpec((None, k, 2 * FF_PAD), lambda i, j: (j, 0, 0))],
        out_specs=[pl.BlockSpec((tm, 2 * FF_PAD), lambda i, j: (i, j)), pl.BlockSpec((tm, FF_PAD), lambda i, j: (i, j))],
        out_shape=[jax.ShapeDtypeStruct((m, N_DEV * 2 * FF_PAD), F32), jax.ShapeDtypeStruct((m, FF_HID), BF16)],
        compiler_params=_cparams(dimension_semantics=("parallel", "parallel")),
    )(h, w_gu)


def _gate_up_act_bwd(name, dout, w_down, gu, order):
    m, k = dout.shape
    tm = _tile(m, MM_ROWS)

    def body(d_ref, w_ref, gu_ref, order_ref, o_ref):
        da = 0.5 * lax.dot_general(d_ref[...].astype(BF16), w_ref[...], (((1,), (1,)), ((), ())),
                                   preferred_element_type=F32)
        guv = gu_ref[...]
        _, vjp = jax.vjp(_f_act, guv[:, :FF_PAD], guv[:, FF_PAD:])
        o_ref[...] = jnp.concatenate(vjp(da), axis=1).astype(o_ref.dtype)

    gu_spec = pl.BlockSpec((tm, 2 * FF_PAD), lambda i, j: (i, j))
    return pl.pallas_call(
        body, name=name, grid=(m // tm, N_DEV),
        in_specs=[pl.BlockSpec((tm, k), lambda i, j: (i, 0)), pl.BlockSpec((FF_PAD, k), lambda i, j: (j, 0)), gu_spec,
                  pl.BlockSpec(memory_space=pl.ANY)],
        out_specs=gu_spec, out_shape=jax.ShapeDtypeStruct((m, N_DEV * 2 * FF_PAD), BF16),
        compiler_params=_cparams(dimension_semantics=("parallel", "parallel")),
    )(dout, w_down, gu, order)


def _ffn_fwd(tag, x, norm, w_gu, w_down):
    h = _norm_fwd(tag + "_norm", x, norm)
    gu, a = _gate_up_act(tag + "_gu", h, w_gu)
    wd = w_down(a)
    out = _mm(tag + "_down", a, wd, "nn", res=x, scale=0.5)
    return out, (h, gu, a), wd


def _ffn_bwd(tag, x, norm, w_gu, w_down, saved, dout, on_down=None):
    h, gu, a = saved
    d_wdown = _mm(tag + "_dwdown", a, dout, "tn", out_dtype=BF16, scale=0.5).reshape(N_DEV, FF_PAD, D_MODEL)
    token = on_down(d_wdown) if on_down is not None else jnp.zeros((8, LANES), F32)
    dgu = _gate_up_act_bwd(tag + "_dgu", dout, w_down, gu, token)
    dh =_mmc_nt(tag + "_dh", dgu, w_gu, 0, 0, 2 * FF_PAD, D_MODEL)
    d_wgu = _mmc_tn(tag + "_dwgu", h, dgu, 2 * FF_PAD)
    dx, dnorm = _norm_bwd(tag + "_dnorm", x, norm, dh, dout)
    return dx, dnorm, d_wgu, d_wdown


def _shift_fwd(z, mu):
    t, c = z.shape[0], RWKV_COLS
    tr = 512

    def body(z_ref, zp_ref, mu_ref, r_ref, k_ref, v_ref, lo_ref, gd_ref):
        zv = z_ref[...]
        prev = zp_ref[7:8, :] * jnp.where(pl.program_id(0) == 0, 0.0, 1.0)
        row = lax.broadcasted_iota(jnp.int32, zv.shape, 0)
        zsh = jnp.where(row == 0, prev, pltpu.roll(zv, 1, 0))
        zs = zv + (zsh - zv) * mu_ref[...]
        r_ref[...] = zs[:, 0:512]
        k_ref[...] = zs[:, 512:1024]
        v_ref[...] = zs[:, 1024:1536]
        lo_ref[...] = zs[:, 1536:1664]
        gd_ref[...] = zs[:, 1664:1792]

    widths = (512, 512, 512, 128, 128)
    return pl.pallas_call(
        body, name="rwkv_shift", grid=(t // tr,),
        in_specs=[pl.BlockSpec((tr, c), lambda i: (i, 0)),
                  pl.BlockSpec((8, c), lambda i: (jnp.maximum(i * (tr // 8) - 1, 0), 0)),
                  pl.BlockSpec((1, c), lambda i: (0, 0))],
        out_specs=[pl.BlockSpec((tr, w), lambda i: (i, 0)) for w in widths],
        out_shape=[jax.ShapeDtypeStruct((t, w), F32) for w in widths],
        compiler_params=_cparams(dimension_semantics=("parallel",)),
    )(z, z, mu)


def _shift_bwd(z, mu, dr, dk, dv, dlo, dgd):
    t, c = z.shape[0], RWKV_COLS
    tr = 512
    nt = t // tr

    def body(z_ref, zp_ref, mu_ref, dr_ref, dk_ref, dv_ref, dlo_ref, dgd_ref,
             drn_ref, dkn_ref, dvn_ref, dlon_ref, dgdn_ref, dz_ref, dmu_ref):
        i = pl.program_id(0)
        zv, muv = z_ref[...], mu_ref[...]
        prev = zp_ref[7:8, :] * jnp.where(i == 0, 0.0, 1.0)
        row = lax.broadcasted_iota(jnp.int32, zv.shape, 0)
        zsh = jnp.where(row == 0, prev, pltpu.roll(zv, 1, 0))
        dzs = jnp.concatenate([dr_ref[...], dk_ref[...], dv_ref[...], dlo_ref[...], dgd_ref[...]], axis=1)
        nxt = jnp.concatenate([drn_ref[0:1, :], dkn_ref[0:1, :], dvn_ref[0:1, :], dlon_ref[0:1, :],
                               dgdn_ref[0:1, :]], axis=1) * jnp.where(i == nt - 1, 0.0, 1.0)
        u = dzs * muv
        un = jnp.where(row == tr - 1, nxt * muv, pltpu.roll(u, tr - 1, 0))
        dz_ref[...] = (dzs - u + un).astype(dz_ref.dtype)
        _accumulate(dmu_ref, jnp.sum(dzs * (zsh - zv), axis=0, keepdims=True))

    widths = (512, 512, 512, 128, 128)
    nxt_map = lambda i: (jnp.minimum((i + 1) * (tr // 8), t // 8 - 1), 0)
    return pl.pallas_call(
        body, name="rwkv_shift_bwd", grid=(nt,),
        in_specs=[pl.BlockSpec((tr, c), lambda i: (i, 0)),
                  pl.BlockSpec((8, c), lambda i: (jnp.maximum(i * (tr // 8) - 1, 0), 0)),
                  pl.BlockSpec((1, c), lambda i: (0, 0))]
        + [pl.BlockSpec((tr, w), lambda i: (i, 0)) for w in widths]
        + [pl.BlockSpec((8, w), nxt_map) for w in widths],
        out_specs=[pl.BlockSpec((tr, c), lambda i: (i, 0)), pl.BlockSpec((1, c), lambda i: (0, 0))],
        out_shape=[jax.ShapeDtypeStruct((t, c), BF16), jax.ShapeDtypeStruct((1, c), F32)],
        compiler_params=_cparams(dimension_semantics=("arbitrary",)),
    )(z, z, mu, dr, dk, dv, dlo, dgd, dr, dk, dv, dlo, dgd)


def _f_pre(k, lo, gd, w0, w2p, a0, a2p, g2, k_k, k_a):
    lane = lax.broadcasted_iota(jnp.int32, lo.shape, 1)
    lo_act = jnp.where(lane < DECAY_LORA, jnp.tanh(lo), lo)
    w = -_softplus(-(w0 + _bdot(lo_act, w2p))) - 0.5
    a = _sigmoid(a0 + _bdot(lo_act, a2p))
    g = _bdot(_sigmoid(gd), g2)
    kk = k * k_k
    k2 = k * (1.0 + (a - 1.0) * k_a)
    decay = jnp.exp(-jnp.exp(w))
    return k2, kk, a, decay, g


def _f_kk(kk, a):
    kkn = kk * lax.rsqrt(jnp.maximum(_head_sums(kk * kk), 1e-24))
    return kkn, kkn * a


def _f_post(y, r, k2, v, g, gn_w, gn_b, r_k):
    mean = _head_sums(y) * (1.0 / HEAD_DIM)
    var = _head_sums(jnp.square(y - mean)) * (1.0 / HEAD_DIM)
    yn = (y - mean) * lax.rsqrt(var + GN_EPS) * gn_w + gn_b
    bonus = _head_sums(r * k2 * r_k) * v
    return (yn + bonus) * g


def _to_v_rows(x):
    t = x.shape[0]
    return x.reshape(t, RWKV_HEADS, V_HI, V_LO).transpose(0, 2, 3, 1).reshape(t, V_HI, LANES)


def _from_v_rows(x):
    t = x.shape[0]
    return x.reshape(t, V_HI, V_LO, RWKV_HEADS).transpose(0, 3, 1, 2).reshape(t, RWKV_DIM)


def _k_cols(x):
    return jnp.tile(x, (V_LO, 1)).T


def _k_rows(x):
    xt = x.T
    out = xt[0:RWKV_HEADS]
    for l in range(1, V_LO):
        out = out + xt[l * RWKV_HEADS:(l + 1) * RWKV_HEADS]
    return out


def _wkv_fwd(r, w, k, kk, b, v):
    t = r.shape[0]
    tc = SCAN_CHUNK
    key_spec = pl.BlockSpec((tc, RWKV_HEADS, HEAD_DIM), lambda i: (i, 0, 0))
    row_spec = pl.BlockSpec((tc, V_HI, LANES), lambda i: (i, 0, 0))

    def body(r_ref, w_ref, k_ref, kk_ref, b_ref, v_ref, y_ref, st_ref, s_scr, cols_a, cols_b):
        @pl.when(pl.program_id(0) == 0)
        def _():
            s_scr[...] = jnp.zeros_like(s_scr)

        def prep(ti, buf):
            for n, ref in enumerate((r_ref, w_ref, k_ref, kk_ref, b_ref)):
                buf[n] = _k_cols(ref[ti])

        def step(ti, s, cur, nxt, ti_next):
            rc, wc, kc, kkc, bc = (cur[n] for n in range(5))
            prep(ti_next, nxt)
            vt = v_ref[ti]
            new, ys = [], []
            for j in range(V_HI):
                sa = -jnp.sum(s[j] * kkc, axis=0, keepdims=True)
                nj = s[j] * wc + bc * sa + kc * vt[j:j + 1]
                st_ref[ti, j] = nj
                ys.append(jnp.sum(nj * rc, axis=0, keepdims=True))
                new.append(nj)
            y_ref[ti] = jnp.concatenate(ys, axis=0)
            return tuple(new)

        def pair(i, s):
            s = step(2 * i, s, cols_a, cols_b, 2 * i + 1)
            return step(2 * i + 1, s, cols_b, cols_a, jnp.minimum(2 * i + 2, tc - 1))

        prep(0, cols_a)
        s = lax.fori_loop(0, tc // 2, pair, tuple(s_scr[j] for j in range(V_HI)))
        for j in range(V_HI):
            s_scr[j] = s[j]

    return pl.pallas_call(
        body, name="wkv_fwd", grid=(t // tc,),
        in_specs=[key_spec] * 5 + [row_spec],
        out_specs=[row_spec, pl.BlockSpec((tc, V_HI, HEAD_DIM, LANES), lambda i: (i, 0, 0, 0))],
        out_shape=[jax.ShapeDtypeStruct((t, V_HI, LANES), F32),
                   jax.ShapeDtypeStruct((t, V_HI, HEAD_DIM, LANES), F32)],
        scratch_shapes=[pltpu.VMEM((V_HI, HEAD_DIM, LANES), F32)] + [pltpu.VMEM((5, HEAD_DIM, LANES), F32)] * 2,
        compiler_params=_cparams(dimension_semantics=("arbitrary",)),
    )(r, w, k, kk, b, v)


def _wkv_bwd(r, w, k, kk, b, v, states, dy):
    t = r.shape[0]
    tc = SCAN_CHUNK
    nb = t // tc
    key_spec = pl.BlockSpec((tc, RWKV_HEADS, HEAD_DIM), lambda i: (nb - 1 - i, 0, 0))
    row_spec = pl.BlockSpec((tc, V_HI, LANES), lambda i: (nb - 1 - i, 0, 0))
    st_spec = pl.BlockSpec((tc, V_HI, HEAD_DIM, LANES), lambda i: (nb - 1 - i, 0, 0, 0))
    stp_spec = pl.BlockSpec((1, V_HI, HEAD_DIM, LANES), lambda i: (jnp.maximum((nb - 1 - i) * tc - 1, 0), 0, 0, 0))

    def body(r_ref, w_ref, k_ref, kk_ref, b_ref, v_ref, st_ref, stp_ref, dy_ref,
             dr_ref, dw_ref, dk_ref, dkk_ref, db_ref, dv_ref, ds_scr, cols_a, cols_b, accs_a, accs_b):
        @pl.when(pl.program_id(0) == 0)
        def _():
            ds_scr[...] = jnp.zeros_like(ds_scr)

        def colsum(x):
            return jnp.sum(x, axis=0, keepdims=True)

        def prep(ti, buf):
            for n, ref in enumerate((r_ref, w_ref, k_ref, kk_ref, b_ref)):
                buf[n] = _k_cols(ref[ti])

        def flush(ti, buf):
            for n, ref in enumerate((dr_ref, dk_ref, db_ref, dw_ref, dkk_ref)):
                ref[ti] = _k_rows(buf[n])

        def step(ti, ds, sp, cur, accs):
            rc, wc, kc, kkc, bc = (cur[n] for n in range(5))
            vt, dyt = v_ref[ti], dy_ref[ti]
            acc = None
            new, dvs = [], []
            for j in range(V_HI):
                st = st_ref[ti, j]
                dsj = ds[j] + rc * dyt[j:j + 1]
                sa = -colsum(sp[j] * kkc)
                dsa = colsum(dsj * bc)
                dvs.append(colsum(dsj * kc))
                parts = (st * dyt[j:j + 1], dsj * vt[j:j + 1], dsj * sa, dsj * sp[j], -(sp[j] * dsa))
                acc = parts if acc is None else tuple(a + q for a, q in zip(acc, parts))
                new.append(dsj * wc - kkc * dsa)
            dv_ref[ti] = jnp.concatenate(dvs, axis=0)
            for n in range(5):
                accs[n] = acc[n]
            return tuple(new)

        def states_before(ti):
            return tuple(st_ref[ti - 1, j] for j in range(V_HI))

        def pair(i, ds):
            ta = tc - 1 - 2 * i
            prep(ta - 1, cols_b)
            flush(jnp.minimum(ta + 1, tc - 1), accs_b)
            ds = step(ta, ds, states_before(ta), cols_a, accs_a)
            prep(ta - 2, cols_a)
            flush(ta, accs_a)
            return step(ta - 1, ds, states_before(ta - 1), cols_b, accs_b)

        prep(tc - 1, cols_a)
        accs_b[...] = jnp.zeros_like(accs_b)
        ds = lax.fori_loop(0, tc // 2 - 1, pair, tuple(ds_scr[j] for j in range(V_HI)))
        prep(0, cols_b)
        flush(2, accs_b)
        ds = step(1, ds, states_before(1), cols_a, accs_a)
        flush(1, accs_a)
        keep = jnp.where(pl.program_id(0) == nb - 1, 0.0, 1.0)
        ds = step(0, ds, tuple(stp_ref[0, j] * keep for j in range(V_HI)), cols_b, accs_b)
        flush(0, accs_b)
        for j in range(V_HI):
            ds_scr[j] = ds[j]

    key_out = jax.ShapeDtypeStruct((t, RWKV_HEADS, HEAD_DIM), F32)
    return pl.pallas_call(
        body, name="wkv_bwd", grid=(nb,),
        in_specs=[key_spec] * 5 + [row_spec, st_spec, stp_spec, row_spec],
        out_specs=[key_spec] * 5 + [row_spec],
        out_shape=[key_out] * 5 + [jax.ShapeDtypeStruct((t, V_HI, LANES), F32)],
        scratch_shapes=[pltpu.VMEM((V_HI, HEAD_DIM, LANES), F32)] + [pltpu.VMEM((5, HEAD_DIM, LANES), F32)] * 4,
        compiler_params=_cparams(dimension_semantics=("arbitrary",)),
    )(r, w, k, kk, b, v, states, states, dy)


PAIR = 2 * HEAD_DIM
N_PAIRS = ATTN_HEADS // 2
Q_COL0 = RWKV_COLS // PAIR
K_COL0 = Q_COL0 + N_PAIRS
V_COL0 = K_COL0 + N_PAIRS


def _swap_halves(x):
    lane = lax.broadcasted_iota(jnp.int32, x.shape, 1)
    return jnp.where((lane & (HEAD_DIM - 1)) < HEAD_DIM // 2, pltpu.roll(x, PAIR - HEAD_DIM // 2, 1),
                     pltpu.roll(x, HEAD_DIM // 2, 1))


@jax.custom_vjp
def _rope(x, cosf, sinf):
    return x * cosf + _swap_halves(x) * sinf


def _rope_fwd(x, cosf, sinf):
    return _rope(x, cosf, sinf), (cosf, sinf)


def _rope_bwd(res, d):
    cosf, sinf = res
    return d * cosf + _swap_halves(d * sinf), jnp.zeros_like(cosf), jnp.zeros_like(sinf)


_rope.defvjp(_rope_fwd, _rope_bwd)


def _head_sums(x):
    lane = lax.broadcasted_iota(jnp.int32, x.shape, 1)
    lo = jnp.where(lane < HEAD_DIM, 1.0, 0.0)
    hi = 1.0 - lo
    return lo * jnp.sum(x * lo, axis=1, keepdims=True) + hi * jnp.sum(x * hi, axis=1, keepdims=True)


def _f_qk(x, cosf, sinf, gain2):
    xn = x * lax.rsqrt(_head_sums(x * x) * (1.0 / HEAD_DIM) + RMS_EPS) * gain2
    return _rope(xn, cosf, sinf)


def _qk_prep(z, tab, q_gain, k_gain):
    t = z.shape[0]
    tr = 2048

    def body(z_ref, c_ref, s_ref, qg_ref, kg_ref, o_ref):
        g = jnp.where(pl.program_id(0) < N_PAIRS, qg_ref[...], kg_ref[...])
        o_ref[...] = _f_qk(z_ref[...], c_ref[...], s_ref[...], jnp.concatenate([g, g], axis=1))

    gain = pl.BlockSpec((1, HEAD_DIM), lambda c, i: (0, 0))
    return pl.pallas_call(
        body, name="qk_prep", grid=(2 * N_PAIRS, t // tr),
        in_specs=[pl.BlockSpec((tr, PAIR), lambda c, i: (i, Q_COL0 + c)), pl.BlockSpec((tr, PAIR), lambda c, i: (i, 0)),
                  pl.BlockSpec((tr, PAIR), lambda c, i: (i, 1)), gain, gain],
        out_specs=pl.BlockSpec((tr, PAIR), lambda c, i: (i, c)),
        out_shape=jax.ShapeDtypeStruct((t, 2 * N_PAIRS * PAIR), F32),
        compiler_params=_cparams(dimension_semantics=("parallel", "parallel")),
    )(z, tab, tab, q_gain, k_gain)


def _qk_prep_bwd(z, tab, q_gain, k_gain, dq, dk):
    t = z.shape[0]
    tr = 2048

    def body(z_ref, c_ref, s_ref, qg_ref, kg_ref, dq_ref, dk_ref, dz_ref, dqg_ref, dkg_ref):
        c, i = pl.program_id(0), pl.program_id(1)
        is_q = c < N_PAIRS
        g = jnp.where(is_q, qg_ref[...], kg_ref[...])
        d = jnp.where(is_q, dq_ref[...], dk_ref[...])
        _, vjp = jax.vjp(lambda xx, gg: _f_qk(xx, c_ref[...], s_ref[...], gg), z_ref[...],
                         jnp.concatenate([g, g], axis=1))
        dx, dg2 = vjp(d)
        dz_ref[...] = dx.astype(dz_ref.dtype)
        dg = dg2[:, :HEAD_DIM] + dg2[:, HEAD_DIM:]
        first_q = jnp.logical_and(c == 0, i == 0)
        first_k = jnp.logical_and(c == N_PAIRS, i == 0)

        @pl.when(first_q)
        def _():
            dqg_ref[...] = dg

        @pl.when(jnp.logical_and(is_q, jnp.logical_not(first_q)))
        def _():
            dqg_ref[...] += dg

        @pl.when(first_k)
        def _():
            dkg_ref[...] = dg

        @pl.when(jnp.logical_and(jnp.logical_not(is_q), jnp.logical_not(first_k)))
        def _():
            dkg_ref[...] += dg

    gain = pl.BlockSpec((1, HEAD_DIM), lambda c, i: (0, 0))
    return pl.pallas_call(
        body, name="qk_prep_bwd", grid=(2 * N_PAIRS, t // tr),
        in_specs=[pl.BlockSpec((tr, PAIR), lambda c, i: (i, Q_COL0 + c)), pl.BlockSpec((tr, PAIR), lambda c, i: (i, 0)),
                  pl.BlockSpec((tr, PAIR), lambda c, i: (i, 1)), gain, gain,
                  pl.BlockSpec((tr, PAIR), lambda c, i: (i, jnp.minimum(c, N_PAIRS - 1))),
                  pl.BlockSpec((tr, PAIR), lambda c, i: (i, jnp.maximum(c - N_PAIRS, 0)))],
        out_specs=[pl.BlockSpec((tr, PAIR), lambda c, i: (i, c)), gain, gain],
        out_shape=[jax.ShapeDtypeStruct((t, 2 * N_PAIRS * PAIR), BF16), jax.ShapeDtypeStruct((1, HEAD_DIM), F32),
                   jax.ShapeDtypeStruct((1, HEAD_DIM), F32)],
        compiler_params=_cparams(dimension_semantics=("arbitrary", "arbitrary")),
    )(z, tab, tab, q_gain, k_gain, dq, dk)


def _attn_block(q, kp, kc, vp, vc, kmin):
    k2 = jnp.concatenate([kp, kc], axis=0)
    v2 = jnp.concatenate([vp, vc], axis=0)
    s = _bdot_nt(q, k2) * (HEAD_DIM ** -0.5)
    qi = lax.broadcasted_iota(jnp.int32, s.shape, 0)
    kj = lax.broadcasted_iota(jnp.int32, s.shape, 1)
    dist = qi + BAND_BLOCK - kj
    valid = (dist >= 0) & (dist <= BAND_BLOCK) & (kj >= kmin)
    s = jnp.where(valid, s, NEG_INF)
    m = lax.stop_gradient(jnp.max(s, axis=-1, keepdims=True))
    e = jnp.exp(s - m)
    l = jnp.sum(e, axis=-1, keepdims=True)
    o = _bdot(e, v2) / l
    return o, m + jnp.log(l)


def _fold(src_ref, dst_ref, dil):
    t = src_ref.shape[0]
    ln = t // dil
    for j in range(dil):
        dst_ref[j * ln:(j + 1) * ln, :] = src_ref[pl.ds(j, ln, stride=dil), :]


def _unfold(src_ref, dst_ref, dil):
    t = src_ref.shape[0]
    ln = t // dil
    for j in range(dil):
        dst_ref[pl.ds(j, ln, stride=dil), :] = src_ref[j * ln:(j + 1) * ln, :]


def _per_group(fn):
    pair = pl.program_id(0)
    for gi, (_, dil) in enumerate(ATTN_GROUPS):
        @pl.when(jnp.logical_or(pair == 2 * gi, pair == 2 * gi + 1))
        def _(dil=dil):
            fn(dil)


def _block_rows(idx, blocks_per_seq):
    first = (idx & (blocks_per_seq - 1)) == 0
    cur = pl.ds(pl.multiple_of(idx * BAND_BLOCK, BAND_BLOCK), BAND_BLOCK)
    prev = pl.ds(pl.multiple_of(jnp.maximum(idx - 1, 0) * BAND_BLOCK, BAND_BLOCK), BAND_BLOCK)
    return first, cur, prev


def _heads(x):
    return x[:, :HEAD_DIM], x[:, HEAD_DIM:]


def _attn_fwd(qk, z):
    t = z.shape[0]
    n_blocks = t // BAND_BLOCK

    def body(q_ref, k_ref, v_ref, o_ref, lse_ref, qf, kf, vf, of, lf):
        def run(dil):
            _fold(q_ref, qf, dil)
            _fold(k_ref, kf, dil)
            _fold(v_ref, vf, dil)
            blocks_per_seq = n_blocks // dil

            def block(idx, carry):
                first, cur, prev = _block_rows(idx, blocks_per_seq)
                kmin = jnp.where(first, BAND_BLOCK, 0)
                outs, lses = [], []
                for q, kp, kc, vp, vc in zip(_heads(qf[cur, :]), _heads(kf[prev, :]), _heads(kf[cur, :]),
                                             _heads(vf[prev, :]), _heads(vf[cur, :])):
                    o, ls = _attn_block(q, kp, kc, vp, vc, kmin)
                    outs.append(o)
                    lses.append(jnp.broadcast_to(ls, o.shape))
                of[cur, :] = jnp.concatenate(outs, axis=1)
                lf[cur, :] = jnp.concatenate(lses, axis=1)
                return carry

            lax.fori_loop(0, n_blocks, block, 0, unroll=4)
            _unfold(of, o_ref, dil)
            _unfold(lf, lse_ref, dil)

        _per_group(run)

    slab = jax.ShapeDtypeStruct((t, N_PAIRS * PAIR), F32)
    out_spec = pl.BlockSpec((t, PAIR), lambda p: (0, p))
    return pl.pallas_call(
        body, name="attn_fwd", grid=(N_PAIRS,),
        in_specs=[pl.BlockSpec((t, PAIR), lambda p: (0, p)), pl.BlockSpec((t, PAIR), lambda p: (0, N_PAIRS + p)),
                  pl.BlockSpec((t, PAIR), lambda p: (0, V_COL0 + p))],
        out_specs=[out_spec, out_spec], out_shape=[slab, slab],
        scratch_shapes=[pltpu.VMEM((t, PAIR), F32)] * 5,
        compiler_params=_cparams(dimension_semantics=("parallel",)),
    )(qk, qk, z)


def _attn_bwd(qk, z, do, dlse):
    t = z.shape[0]
    n_blocks = t // BAND_BLOCK

    def body(q_ref, k_ref, v_ref, do_ref, dl_ref, dq_ref, dk_ref, dv_ref, qf, kf, vf, dof, dlf, dqf, dkf, dvf,
             dkpf, dvpf):
        def run(dil):
            for src, dst in ((q_ref, qf), (k_ref, kf), (v_ref, vf), (do_ref, dof), (dl_ref, dlf)):
                _fold(src, dst, dil)
            blocks_per_seq = n_blocks // dil

            def block(idx, carry):
                first, cur, prev = _block_rows(idx, blocks_per_seq)
                kmin = jnp.where(first, BAND_BLOCK, 0)
                grads = []
                for q, kp, kc, vp, vc, do_h, dl_h in zip(
                        _heads(qf[cur, :]), _heads(kf[prev, :]), _heads(kf[cur, :]), _heads(vf[prev, :]),
                        _heads(vf[cur, :]), _heads(dof[cur, :]), _heads(dlf[cur, :])):
                    _, vjp = jax.vjp(functools.partial(_attn_block, kmin=kmin), q, kp, kc, vp, vc)
                    grads.append(vjp((do_h, jnp.sum(dl_h, axis=1, keepdims=True))))
                dq, dkp, dkc, dvp, dvc = (jnp.concatenate([a, b], axis=1) for a, b in zip(*grads))
                dqf[cur, :], dkf[cur, :], dvf[cur, :], dkpf[cur, :], dvpf[cur, :] = dq, dkc, dvc, dkp, dvp
                return carry

            lax.fori_loop(0, n_blocks, block, 0, unroll=4)

            def join(idx, carry):
                first, cur, prev = _block_rows(idx, blocks_per_seq)

                @pl.when(jnp.logical_not(first))
                def _():
                    dkf[prev, :] += dkpf[cur, :]
                    dvf[prev, :] += dvpf[cur, :]

                return carry

            lax.fori_loop(0, n_blocks, join, 0)
            _unfold(dqf, dq_ref, dil)
            _unfold(dkf, dk_ref, dil)
            _unfold(dvf, dv_ref, dil)

        _per_group(run)

    slab = jax.ShapeDtypeStruct((t, N_PAIRS * PAIR), F32)
    own = pl.BlockSpec((t, PAIR), lambda p: (0, p))
    return pl.pallas_call(
        body, name="attn_bwd", grid=(N_PAIRS,),
        in_specs=[own, pl.BlockSpec((t, PAIR), lambda p: (0, N_PAIRS + p)),
                  pl.BlockSpec((t, PAIR), lambda p: (0, V_COL0 + p)), own, own],
        out_specs=[own] * 3, out_shape=[slab] * 3,
        scratch_shapes=[pltpu.VMEM((t, PAIR), F32)] * 10,
        compiler_params=_cparams(dimension_semantics=("parallel",)),
    )(qk, qk, z, do, dlse)


def _f_comb(o1, o2, o3, l1, l2, l3):
    m = jnp.maximum(jnp.maximum(l1, l2), l3)
    e1, e2, e3 = jnp.exp(l1 - m), jnp.exp(l2 - m), jnp.exp(l3 - m)
    den = e1 + e2 + e3
    return (e1 / den) * o1 + (e2 / den) * o2 + (e3 / den) * o3


def _all_gather_hbm(name, arrs):
    na = len(arrs)

    def body(*refs):
        x_refs, out_refs = refs[:na], refs[na:2 * na]
        send_sems, recv_sems, local_sems = refs[2 * na:]
        mx, my, mc = lax.axis_index("x"), lax.axis_index("y"), lax.axis_index("c")
        me, sibling = (mx, my, mc), (mx, my, 1 - mc)
        chips = [(1 - mx, my), (mx, 1 - my), (1 - mx, 1 - my)]

        def slot(a, px, py, pc):
            return out_refs[a].at[4 * px + 2 * py + pc]

        def copy(a, k, block, to, src=None):
            return pltpu.make_async_remote_copy(
                src_ref=slot(a, *block) if src is None else src, dst_ref=slot(a, *block),
                send_sem=send_sems.at[a, k], recv_sem=recv_sems.at[a, k], device_id=to, device_id_type=MESH)

        mine = [pltpu.make_async_copy(x_refs[a], slot(a, *me), local_sems.at[a]) for a in range(na)]
        for cp in mine:
            cp.start()
        first = []
        for a in range(na):
            first.append(copy(a, 0, me, sibling, src=x_refs[a]))
            first += [copy(a, 1 + j, me, (*chip, mc), src=x_refs[a]) for j, chip in enumerate(chips)]
        for cp in first:
            cp.start()
        passed = []
        for j, chip in enumerate(chips):
            for a in range(na):
                copy(a, 1 + j, (*chip, mc), me).wait_recv()
                passed.append(copy(a, 4 + j, (*chip, mc), sibling))
                passed[-1].start()
        for a in range(na):
            copy(a, 0, sibling, me).wait_recv()
            for j, chip in enumerate(chips):
                copy(a, 4 + j, (*chip, 1 - mc), me).wait_recv()
        for cp in first + passed:
            cp.wait_send()
        for cp in mine:
            cp.wait()

    hbm = pl.BlockSpec(memory_space=pl.ANY)
    return pl.pallas_call(
        body, name=name,
        out_shape=[jax.ShapeDtypeStruct((N_DEV,) + a.shape, a.dtype) for a in arrs],
        in_specs=[hbm] * na, out_specs=[hbm] * na,
        scratch_shapes=[pltpu.SemaphoreType.DMA((na, 7)), pltpu.SemaphoreType.DMA((na, 7)),
                        pltpu.SemaphoreType.DMA((na,))],
    )(*arrs)


def _all_gather_vmem(x):
    rws, cols = x.shape

    def body(x_ref, out_ref, send_sems, recv_sems):
        mx, my, mc = lax.axis_index("x"), lax.axis_index("y"), lax.axis_index("c")
        me, sibling = (mx, my, mc), (mx, my, 1 - mc)
        chips = [(1 - mx, my), (mx, 1 - my), (1 - mx, 1 - my)]

        def slot(px, py, pc):
            return out_ref.at[4 * px + 2 * py + pc]

        def copy(k, block, to, src=None):
            return pltpu.make_async_remote_copy(
                src_ref=slot(*block) if src is None else src, dst_ref=slot(*block),
                send_sem=send_sems.at[k], recv_sem=recv_sems.at[k], device_id=to, device_id_type=MESH)

        first = [copy(0, me, sibling, src=x_ref)]
        first += [copy(1 + j, me, (*chip, mc), src=x_ref) for j, chip in enumerate(chips)]
        for cp in first:
            cp.start()
        out_ref[4 * mx + 2 * my + mc] = x_ref[...]
        passed = [copy(4 + j, (*chip, mc), sibling) for j, chip in enumerate(chips)]
        for j, chip in enumerate(chips):
            copy(1 + j, (*chip, mc), me).wait_recv()
            passed[j].start()
        copy(0, sibling, me).wait_recv()
        for j, chip in enumerate(chips):
            copy(4 + j, (*chip, 1 - mc), me).wait_recv()
        for cp in first + passed:
            cp.wait_send()

    return pl.pallas_call(
        body, name="all_gather_small",
        out_shape=jax.ShapeDtypeStruct((N_DEV, rws, cols), x.dtype),
        in_specs=[pl.BlockSpec(memory_space=pltpu.VMEM)], out_specs=pl.BlockSpec(memory_space=pltpu.VMEM),
        scratch_shapes=[pltpu.SemaphoreType.DMA((7,)), pltpu.SemaphoreType.DMA((7,))],
    )(x)


def _scatter_copies(g_refs, land_refs, send_sems, recv_sems):
    mx, my, mc = lax.axis_index("x"), lax.axis_index("y"), lax.axis_index("c")
    me = 4 * mx + 2 * my + mc
    copies = []
    for k in range(1, N_DEV):
        px, py, pc = mx ^ (k >> 2), my ^ ((k >> 1) & 1), mc ^ (k & 1)
        peer = 4 * px + 2 * py + pc
        for a, (g_ref, land_ref) in enumerate(zip(g_refs, land_refs)):
            copies.append(pltpu.make_async_remote_copy(
                src_ref=g_ref.at[peer], dst_ref=land_ref.at[me], send_sem=send_sems.at[a * (N_DEV - 1) + k - 1],
                recv_sem=recv_sems.at[a * (N_DEV - 1) + k - 1], device_id=(px, py, pc), device_id_type=MESH))
    return copies


_HBM = pl.BlockSpec(memory_space=pltpu.HBM)
_SEM = pl.BlockSpec(memory_space=pltpu.SEMAPHORE)
_DATAFLOW = pltpu.SideEffectType.DATAFLOW_SIDE_EFFECTING


def _gather_copies(x_refs, land_refs, send_sems, recv_sems):
    mx, my, mc = lax.axis_index("x"), lax.axis_index("y"), lax.axis_index("c")
    me = 4 * mx + 2 * my + mc
    copies = []
    for k in range(1, N_DEV):
        px, py, pc = mx ^ (k >> 2), my ^ ((k >> 1) & 1), mc ^ (k & 1)
        for a, (x_ref, land_ref) in enumerate(zip(x_refs, land_refs)):
            copies.append(pltpu.make_async_remote_copy(
                src_ref=x_ref, dst_ref=land_ref.at[me], send_sem=send_sems.at[a * (N_DEV - 1) + k - 1],
                recv_sem=recv_sems.at[a * (N_DEV - 1) + k - 1], device_id=(px, py, pc), device_id_type=MESH))
    return copies


N_CHIPS = N_DEV // 2


def _gather2_first(x_refs, land_refs, send_sems, recv_sems, ks=(0, 1, 2, 3)):
    mx, my, mc = lax.axis_index("x"), lax.axis_index("y"), lax.axis_index("c")
    me = 4 * mx + 2 * my + mc
    peers = [(mx, my, 1 - mc), (1 - mx, my, mc), (mx, 1 - my, mc), (1 - mx, 1 - my, mc)]
    return [pltpu.make_async_remote_copy(
        src_ref=x_ref, dst_ref=land_ref.at[me], send_sem=send_sems.at[a * (N_DEV - 1) + k],
        recv_sem=recv_sems.at[a * (N_DEV - 1) + k], device_id=peers[k], device_id_type=MESH)
        for a, (x_ref, land_ref) in enumerate(zip(x_refs, land_refs)) for k in ks]


def _gather2_second(x_refs, land_refs, send_sems, recv_sems):
    mx, my, mc = lax.axis_index("x"), lax.axis_index("y"), lax.axis_index("c")
    copies = []
    for a, land_ref in enumerate(land_refs):
        for j, (px, py) in enumerate([(1 - mx, my), (mx, 1 - my), (1 - mx, 1 - my)]):
            slot = 4 * px + 2 * py + mc
            copies.append(pltpu.make_async_remote_copy(
                src_ref=land_ref.at[slot], dst_ref=land_ref.at[slot], send_sem=send_sems.at[a * (N_DEV - 1) + j],
                recv_sem=recv_sems.at[a * (N_DEV - 1) + j], device_id=(mx, my, 1 - mc), device_id_type=MESH))
    return copies


def _pair_swap(name, arrs):
    na = len(arrs)

    def body(*refs):
        g_refs, out_refs = refs[:na], refs[na:2 * na]
        send_sems, recv_sems = refs[2 * na:]
        mx, my, mc = lax.axis_index("x"), lax.axis_index("y"), lax.axis_index("c")
        copies = [pltpu.make_async_remote_copy(
            src_ref=g_refs[a].at[2 * q + 1 - mc], dst_ref=out_refs[a].at[q], send_sem=send_sems.at[a, q],
            recv_sem=recv_sems.at[a, q], device_id=(mx, my, 1 - mc), device_id_type=MESH)
            for a in range(na) for q in range(N_CHIPS)]
        for cp in copies:
            cp.start()
        for cp in copies:
            cp.wait_recv()
        for cp in copies:
            cp.wait_send()

    hbm = pl.BlockSpec(memory_space=pl.ANY)
    return pl.pallas_call(
        body, name=name,
        out_shape=[jax.ShapeDtypeStruct((N_CHIPS,) + a.shape[1:], a.dtype) for a in arrs],
        in_specs=[hbm] * na, out_specs=[hbm] * na,
        scratch_shapes=[pltpu.SemaphoreType.DMA((na, N_CHIPS)), pltpu.SemaphoreType.DMA((na, N_CHIPS))],
    )(*arrs)


def _pair_add(name, g, half, core):
    _, r, c = g.shape
    tr = _tile(r, 512) if r % LANES == 0 else r

    def body(core_ref, g_ref, h_ref, o_ref):
        o_ref[...] = (g_ref[...].astype(F32) + h_ref[...].astype(F32)).astype(o_ref.dtype)

    return pl.pallas_call(
        body, name=name,
        grid_spec=pltpu.PrefetchScalarGridSpec(
            num_scalar_prefetch=1, grid=(N_CHIPS, r // tr),
            in_specs=[pl.BlockSpec((None, tr, c), lambda q, i, core_ref: (2 * q + core_ref[0], i, 0)),
                      pl.BlockSpec((None, tr, c), lambda q, i, core_ref: (q, i, 0))],
            out_specs=pl.BlockSpec((None, tr, c), lambda q, i, core_ref: (q, i, 0))),
        out_shape=jax.ShapeDtypeStruct((N_CHIPS, r, c), g.dtype),
        compiler_params=_cparams(dimension_semantics=("parallel", "parallel")),
    )(core, g, half)


def _chip_copies(h_refs, land_refs, send_sems, recv_sems):
    mx, my, mc = lax.axis_index("x"), lax.axis_index("y"), lax.axis_index("c")
    my_chip = 2 * mx + my
    copies = []
    for k in range(1, N_CHIPS):
        px, py = mx ^ (k >> 1), my ^ (k & 1)
        for a, (h_ref, land_ref) in enumerate(zip(h_refs, land_refs)):
            copies.append(pltpu.make_async_remote_copy(
                src_ref=h_ref.at[2 * px + py], dst_ref=land_ref.at[my_chip], send_sem=send_sems.at[a * (N_DEV - 1) + k - 1],
                recv_sem=recv_sems.at[a * (N_DEV - 1) + k - 1], device_id=(px, py, mc), device_id_type=MESH))
    return copies


def _exchange_start(name, copies, srcs, lands, after):
    na = len(srcs)

    def body(*refs):
        for cp in copies(refs[:na], refs[na:2 * na], refs[2 * na + 1], refs[2 * na + 2]):
            cp.start()
        refs[-1][...] = jnp.zeros_like(refs[-1])

    in_hbm = lambda a: pltpu.with_memory_space_constraint(a, pltpu.HBM)
    outs = pl.pallas_call(
        body, name=name,
        out_shape=(pltpu.SemaphoreType.DMA((na * (N_DEV - 1),)), pltpu.SemaphoreType.DMA((na * (N_DEV - 1),)),
                   *[pltpu.HBM(a.shape, a.dtype) for a in list(srcs) + list(lands)],
                   jax.ShapeDtypeStruct((8, LANES), F32)),
        in_specs=[_HBM] * (2 * na) + [pl.BlockSpec(memory_space=pl.ANY)],
        out_specs=(_SEM, _SEM, *[_HBM] * (2 * na), pl.BlockSpec(memory_space=pltpu.VMEM)),
        input_output_aliases={i: 2 + i for i in range(2 * na)},
        compiler_params=pltpu.CompilerParams(has_side_effects=_DATAFLOW),
    )(*[in_hbm(a) for a in srcs], *[in_hbm(a) for a in lands], after)
    return outs[0], outs[1], outs[2:2 + na], outs[2 + na:2 + 2 * na], outs[-1]


def _exchange_wait(name, copies, send_sems, recv_sems, srcs, lands, after, parts=("send", "recv")):
    na = len(srcs)

    def body(*refs):
        for cp in copies(refs[:na], refs[na:2 * na], refs[2 * na], refs[2 * na + 1]):
            if "send" in parts:
                cp.wait_send()
            if "recv" in parts:
                cp.wait_recv()

    outs = pl.pallas_call(
        body, name=name,
        out_shape=[pltpu.HBM(a.shape, a.dtype) for a in list(srcs) + list(lands)],
        in_specs=[_HBM] * (2 * na) + [_SEM, _SEM, pl.BlockSpec(memory_space=pl.ANY)], out_specs=[_HBM] * (2 * na),
        input_output_aliases={i: i for i in range(2 * na)},
        compiler_params=pltpu.CompilerParams(has_side_effects=_DATAFLOW),
    )(*srcs, *lands, send_sems, recv_sems, after)
    return outs[:na], outs[na:]


def _sum_slots(name, g, tr):
    _, rws, cols = g.shape

    def body(g_ref, o_ref):
        acc = g_ref[0].astype(F32)
        for j in range(1, N_DEV):
            acc = acc + g_ref[j].astype(F32)
        o_ref[...] = acc

    return pl.pallas_call(
        body, name=name, grid=(rws // tr,),
        in_specs=[pl.BlockSpec((N_DEV, tr, cols), lambda i: (0, i, 0))],
        out_specs=pl.BlockSpec((tr, cols), lambda i: (i, 0)),
        out_shape=jax.ShapeDtypeStruct((rws, cols), F32),
        compiler_params=_cparams(dimension_semantics=("parallel",)),
    )(g)


def _adam_math(wv, gv, mv, vv):
    mn = ADAM_B1 * mv + (1.0 - ADAM_B1) * gv
    vn = ADAM_B2 * vv + (1.0 - ADAM_B2) * jnp.square(gv)
    m_hat = mn / (1.0 - ADAM_B1 ** ADAM_STEP)
    v_hat = vn / (1.0 - ADAM_B2 ** ADAM_STEP)
    delta = -ADAM_LR * (m_hat / (jnp.sqrt(v_hat) + ADAM_EPS) + ADAM_WD * wv)
    return delta, mn, vn


def _adamw(name, w, g, m, v, tr):
    return _rowwise(name, _adam_math, [w, g, m, v], [], [(LANES, F32)] * 3, tr=tr)


def _adamw_slots(name, recv, own, slot, rb, cb, cw, w, m, v, tr, order):
    nr, nc = w.shape
    n = recv.shape[0]

    def body(slot_ref, g_ref, own_ref, w_ref, m_ref, v_ref, order_ref, go_ref, d_ref, mo_ref, vo_ref):
        acc = None
        for s in range(n):
            part = jnp.where(slot_ref[0] == s, own_ref[...], g_ref[s]).astype(F32)
            acc = part if acc is None else acc + part
        g = acc[:, :nc]
        go_ref[...] = g
        d_ref[...], mo_ref[...], vo_ref[...] = _adam_math(w_ref[...], g, m_ref[...], v_ref[...])

    nat = pl.BlockSpec((tr, nc), lambda i, slot_ref: (i, 0))
    return pl.pallas_call(
        body, name=name,
        grid_spec=pltpu.PrefetchScalarGridSpec(
            num_scalar_prefetch=1, grid=(nr // tr,),
            in_specs=[pl.BlockSpec((n, tr, cw), lambda i, slot_ref: (0, rb + i, cb)),
                      pl.BlockSpec((None, tr, cw), lambda i, slot_ref: (slot_ref[0], rb + i, cb)),
                      nat, nat, nat, pl.BlockSpec(memory_space=pl.ANY)],
            out_specs=[nat] * 4),
        out_shape=[jax.ShapeDtypeStruct((nr, nc), F32)] * 4,
        compiler_params=_cparams(dimension_semantics=("parallel",)),
    )(slot, recv, own, w, m, v, order)


def _local_blocks(w):
    pad_cols = lambda a: jnp.pad(a, ((0, 0), (0, FF_PAD - FF_SHARD)))
    pad_rows = lambda a: jnp.pad(a, ((0, FF_PAD - FF_SHARD), (0, 0)))
    gate_up = lambda tag: jnp.concatenate([pad_cols(w[tag + "_w_gate"]), pad_cols(w[tag + "_w_up"])], axis=1)
    blocks = {
        "ffn1_gu": gate_up("ffn1"), "ffn1_d": pad_rows(w["ffn1_w_down"]), "w_in": w["w_in"],
        "lora": jnp.concatenate([w["rwkv_w2"], w["rwkv_a2"], w["rwkv_g2"]], axis=0),
        "br": jnp.concatenate([w["w_br_rwkv"], w["w_br_attn"], w["ple_w_proj"]], axis=0),
        "w_out": w["w_out"], "ffn2_gu": gate_up("ffn2"), "ffn2_d": pad_rows(w["ffn2_w_down"]),
        "ple_gate": w["ple_w_gate"],
    }
    return {n: a.astype(BF16) for n, a in blocks.items()}


GATHER_GROUPS = {"head": ("ffn1_gu", "ffn1_d"), "mid": ("w_in", "lora"),
                 "rest": ("br", "w_out", "ffn2_gu", "ffn2_d", "ple_gate")}

SCATTER_GROUPS = {"tail": ("ple_gate", "ple_proj", "ffn2_gu", "ffn2_d"), "branch": ("w_out", "br"),
                  "mixer": ("lora", "w_in"),
                  "ffn1_down": ("ffn1_d",), "head": ("ffn1_gu",)}

TWO_LEVEL = ("mixer", "head")

ADAM_PLAN = (
    ("ffn1_w_gate", "ffn1_gu", 0, 0, FF_PAD, 256), ("ffn1_w_up", "ffn1_gu", 0, 1, FF_PAD, 256),
    ("ffn1_w_down", "ffn1_d", 0, 0, D_MODEL, FF_SHARD // 2), ("w_in", "w_in", 0, 0, IN_SHARD, 256),
    ("rwkv_w2", "lora", 0, 0, HEAD_DIM, 64), ("rwkv_a2", "lora", 1, 0, HEAD_DIM, 64),
    ("rwkv_g2", "lora", 2, 0, HEAD_DIM, 64),
    ("w_br_rwkv", "br", 0, 0, OUT_SHARD, 256), ("w_br_attn", "br", 2, 0, OUT_SHARD, 256),
    ("ple_w_proj", "ple_proj", 0, 0, OUT_SHARD, 256), ("w_out", "w_out", 0, 0, D_MODEL, OUT_SHARD),
    ("ffn2_w_gate", "ffn2_gu", 0, 0, FF_PAD, 256), ("ffn2_w_up", "ffn2_gu", 0, 1, FF_PAD, 256),
    ("ffn2_w_down", "ffn2_d", 0, 0, D_MODEL, FF_SHARD // 2), ("ple_w_gate", "ple_gate", 0, 0, D_MODEL, OUT_SHARD),
)


def _pack_small(arrs, rows):
    flat = jnp.concatenate([a.reshape(-1) for a in arrs])
    return jnp.pad(flat, (0, rows * LANES - flat.shape[0])).reshape(rows, LANES)


def _unpack_small(flat, like):
    flat = flat.reshape(-1)
    out, off = [], 0
    for a in like:
        out.append(flat[off:off + a.size].reshape(a.shape))
        off += a.size
    return out


def _local_step(x, p, pos, target, sm, wg, fetch, on_grads):
    t = x.shape[0]

    w_d1 = wg["ffn1_d"].reshape(FF_HID, D_MODEL)
    x1, ffn1_saved, _ = _ffn_fwd("ffn1", x, sm["ffn1_norm"], wg["ffn1_gu"], lambda after: w_d1)
    fetch("forward", x1)
    h2 = _norm_fwd("mix_norm", x1, sm["mix_norm"])
    wg = {**wg, **fetch("mid", h2)}
    full_cols = lambda blk: blk.transpose(1, 0, 2).reshape(blk.shape[1], N_DEV * blk.shape[2])
    lora_w2 = full_cols(wg["lora"][:, :DECAY_LORA])
    lora_a2 = full_cols(wg["lora"][:, DECAY_LORA:DECAY_LORA + ICLR_LORA])
    lora_g2 = full_cols(wg["lora"][:, DECAY_LORA + ICLR_LORA:])
    z =_mmc_nn("w_in", h2, wg["w_in"], 0, 0, IN_SHARD)
    z_g = (z, 2 * D_MODEL, (RWKV_COLS + 3 * ATTN_DIM) // (2 * D_MODEL))

    r, k, v, lo, gd = _shift_fwd(z, sm["rwkv_mu"])
    zero_lo = jnp.zeros((DECAY_LORA, RWKV_DIM), BF16)
    w2p = jnp.concatenate([lora_w2, zero_lo], axis=0).astype(F32)
    a2p = jnp.concatenate([zero_lo, lora_a2], axis=0).astype(F32)
    pre_params = [sm["rwkv_w0"], w2p, sm["rwkv_a0"], a2p, lora_g2.astype(F32), sm["rwkv_k_k"], sm["rwkv_k_a"]]
    wide = [(RWKV_DIM, F32)]
    k2, kk, a, decay, g = _rowwise("rwkv_pre", _f_pre, [k, lo, gd], pre_params, wide * 5, tr=BIG_ROWS)
    kkn, b = _pairwise("rwkv_kk", _f_kk, [kk, a], [], 2)
    scan_in = [u.reshape(t, RWKV_HEADS, HEAD_DIM) for u in (r, decay, k2, kkn, b)]
    v_rows = _to_v_rows(v)
    y_rows, states = _wkv_fwd(*scan_in, v_rows)
    y = _from_v_rows(y_rows)
    post_params = [sm["rwkv_gn_w"], sm["rwkv_gn_b"], sm["rwkv_r_k"]]
    post_rows = [y, r, k2, v, g]
    y_rwkv = _pairwise("rwkv_post", lambda *av: (_f_post(*av),), post_rows, post_params, 1)[0]

    inv_freq = 1.0 / (ROPE_THETA ** (jnp.arange(0, HEAD_DIM, 2, dtype=F32) / HEAD_DIM))
    freq2 = jnp.tile(inv_freq, 2 * PAIR // HEAD_DIM).reshape(1, PAIR)
    half = jnp.ones((HEAD_DIM // 2,), F32)
    sign2 = jnp.tile(jnp.concatenate([-half, half]), PAIR // HEAD_DIM).reshape(1, PAIR)

    def rope_table(posv, fr, sg):
        ang = posv * fr
        return (jnp.concatenate([jnp.cos(ang), jnp.sin(ang) * sg], axis=1),)
    tab = _rowwise("rope_table", rope_table, [pos.astype(F32).reshape(t, 1)], [freq2, sign2], [(2 * PAIR, F32)])[0]
    qk = _qk_prep(z, tab, sm["q_norm"], sm["k_norm"])
    o_all, lse_all = _attn_fwd(qk, z)
    gw = HEADS_PER_GROUP * HEAD_DIM

    def by_group(ov, lv):
        return [ov[:, i * gw:(i + 1) * gw] for i in range(3)] + [lv[:, i * gw:(i + 1) * gw] for i in range(3)]
    y_attn = _rowwise("attn_comb", lambda ov, lv: (_f_comb(*by_group(ov, lv)),), [o_all, lse_all], [], [(gw, F32)],
                      tr=BIG_ROWS)[0]

    wg = {**wg, **fetch("rest", y_rwkv)}
    w_d2 = wg["ffn2_d"].reshape(FF_HID, D_MODEL)
    w_out = wg["w_out"].reshape(D_MODEL, D_MODEL)
    w_pg = wg["ple_gate"].reshape(D_MODEL, D_MODEL)
    w_brr = full_cols(wg["br"][:, :RWKV_DIM])
    w_bra = full_cols(wg["br"][:, RWKV_DIM:RWKV_DIM + gw])
    w_pp = full_cols(wg["br"][:, RWKV_DIM + gw:])
    u_r = _mm("br_rwkv", y_rwkv, w_brr, "nn")
    u_a = _mm("br_attn", y_attn, w_bra, "nn")

    def f_merge(zgr, zga, ur, ua):
        return _sigmoid(zgr) * ur + _sigmoid(zga) * ua
    merged = _rowwise("merge", lambda zg, ur, ua: (f_merge(zg[:, :D_MODEL], zg[:, D_MODEL:], ur, ua),),
                      [z_g, u_r, u_a], [], [(D_MODEL, BF16)], tr=BIG_ROWS)[0]
    x2 = _mm("w_out", merged, w_out, "nn", res=x1)
    x3, ffn2_saved, _ = _ffn_fwd("ffn2", x2, sm["ffn2_norm"], wg["ffn2_gu"], lambda after: w_d2)

    hn = _norm_fwd("ple_norm", x3, sm["ple_norm"])
    gz = _mm("ple_gate", hn, w_pg, "nn")
    pp = _mm("ple_proj", p, w_pp, "nn")

    def f_head(x3v, gzv, ppv, tg):
        sg = _sigmoid(gzv)
        err = x3v + sg * ppv - tg
        part = 0.5 * jnp.sum(jnp.mean(err * err, axis=-1, keepdims=True))
        dx4 = err * (1.0 / D_MODEL)
        return dx4, dx4 * ppv * sg * (1.0 - sg), dx4 * sg, jnp.full((1, LANES), part, F32)
    dx4, dgz, dpp, loss_row = _rowwise("ple_loss", f_head, [x3, gz, pp, target], [],
                                       [(D_MODEL, F32), (D_MODEL, BF16), (D_MODEL, BF16)], [(1, LANES)])
    loss = loss_row[0, 0]

    gs, gm = {}, {}
    row_blocks = lambda g: g.reshape(N_DEV, g.shape[0] // N_DEV, g.shape[1])
    dhn = _mm("ple_dhn", dgz, w_pg, "nt")
    gm["ple_gate"] = row_blocks(_mm("ple_dwgate", hn, dgz, "tn", out_dtype=BF16))
    col_blocks = lambda g: g.reshape(g.shape[0], N_DEV, g.shape[1] // N_DEV).transpose(1, 0, 2)
    gm["ple_proj"] = col_blocks(_mm("ple_dwproj", p, dpp, "tn", out_dtype=BF16))
    dx3, gs["ple_norm"] = _norm_bwd("ple_dnorm", x3, sm["ple_norm"], dhn, dx4)

    dx2, gs["ffn2_norm"], gm["ffn2_gu"], gm["ffn2_d"] = _ffn_bwd(
        "ffn2", x2, sm["ffn2_norm"], wg["ffn2_gu"], w_d2, ffn2_saved, dx3)
    tail_token = on_grads("tail", {n: gm.pop(n) for n in SCATTER_GROUPS["tail"]})

    dmerged = _mm("w_out_dmerged", dx2, w_out, "nt")
    gm["w_out"] = row_blocks(_mm("w_out_dw", merged, dx2, "tn", out_dtype=BF16))

    def merge_bwd(zg, ur, ua, dm):
        _, vjp = jax.vjp(f_merge, zg[:, :D_MODEL], zg[:, D_MODEL:], ur, ua)
        dzr, dza, dur, dua = vjp(dm)
        return jnp.concatenate([dzr, dza], axis=1), dur, dua
    dz_g, du_r, du_a = _rowwise("merge_bwd", merge_bwd, [z_g, u_r, u_a, dmerged], [],
                                [(2 * D_MODEL, BF16), (D_MODEL, BF16), (D_MODEL, BF16)])
    dy_rwkv = _mm("br_rwkv_dy", du_r, w_brr, "nt")
    dy_attn = _mm("br_attn_dy", du_a, w_bra, "nt")
    gm["br"] = jnp.concatenate([col_blocks(_mm("br_rwkv_dw", y_rwkv, du_r, "tn", out_dtype=BF16)),
                                col_blocks(_mm("br_attn_dw", y_attn, du_a, "tn", out_dtype=BF16))], axis=1)
    branch_token = on_grads("branch", {n: gm.pop(n) for n in SCATTER_GROUPS["branch"]})

    def comb_bwd(ov, lv, dyv):
        _, vjp = jax.vjp(_f_comb, *by_group(ov, lv))
        d = vjp(dyv)
        return jnp.concatenate(d[:3], axis=1), jnp.concatenate(d[3:], axis=1)
    do_all, dl_all = _rowwise("attn_comb_bwd", comb_bwd, [o_all, lse_all, dy_attn], [],
                              [(ATTN_DIM, F32), (ATTN_DIM, F32)], tr=BIG_ROWS)
    dq_all, dk_all, dv_all = _attn_bwd(qk, z, do_all, dl_all)
    dqk_raw, gs["q_norm"], gs["k_norm"] = _qk_prep_bwd(z, tab, sm["q_norm"], sm["k_norm"], dq_all, dk_all)

    def post_bwd(yv, rv, k2v, vv, gv, dv_, gnw, gnb, rk):
        _, vjp = jax.vjp(_f_post, yv, rv, k2v, vv, gv, gnw, gnb, rk)
        return vjp(dv_)
    dy, dr1, dk2a, dv1, dg, d_gnw, d_gnb, d_rk = _pairwise(
        "rwkv_post_bwd", post_bwd, post_rows + [dy_rwkv], [post_params[0] + tail_token[0, 0] + branch_token[0, 0]] + post_params[1:], 5, 3)
    gs["rwkv_gn_w"], gs["rwkv_gn_b"], gs["rwkv_r_k"] = d_gnw, d_gnb, d_rk
    dr2, ddecay, dk2b, dkkn, db, dv_rows = _wkv_bwd(*scan_in, v_rows, states, _to_v_rows(dy))
    dr2, ddecay, dk2b, dkkn, db = [u.reshape(t, RWKV_DIM) for u in (dr2, ddecay, dk2b, dkkn, db)]
    dv2 = _from_v_rows(dv_rows)

    def kk_bwd(kkv, av, dkknv, dbv, dra, drb, dva, dvb):
        _, vjp = jax.vjp(_f_kk, kkv, av)
        return (*vjp((dkknv, dbv)), dra + drb, dva + dvb)
    dkk, da, dr, dv = _pairwise("rwkv_kk_bwd", kk_bwd, [kk, a, dkkn, db, dr1, dr2, dv1, dv2], [], 4)

    def pre_bwd(kv, lov, gdv, dk2x, dk2y, dkkv, dav, ddec, dgv, w0, w2p_, a0, a2p_, g2, k_k, k_a):
        _, vjp = jax.vjp(_f_pre, kv, lov, gdv, w0, w2p_, a0, a2p_, g2, k_k, k_a)
        return vjp((dk2x + dk2y, dkkv, dav, ddec, dgv))
    lora_acc = (DECAY_LORA + ICLR_LORA, RWKV_DIM)
    dk, dlo, dgd, d_w0, d_w2p, d_a0, d_a2p, d_g2, d_kk, d_ka = _rowwise(
        "rwkv_pre_bwd", pre_bwd,
        [k, lo, gd, dk2a, dk2b, dkk, da, ddecay, dg],
        pre_params, [(RWKV_DIM, F32), (LANES, F32), (LANES, F32)],
        [(1, RWKV_DIM), lora_acc, (1, RWKV_DIM), lora_acc, (GATE_LORA, RWKV_DIM), (1, RWKV_DIM), (1, RWKV_DIM)])
    gs["rwkv_w0"], gs["rwkv_a0"], gs["rwkv_k_k"], gs["rwkv_k_a"] = d_w0, d_a0, d_kk, d_ka
    gm["lora"] = jnp.concatenate([col_blocks(d_w2p[:DECAY_LORA]), col_blocks(d_a2p[DECAY_LORA:]), col_blocks(d_g2)],
                                 axis=1).astype(BF16)
    dz_r, gs["rwkv_mu"] = _shift_bwd(z, sm["rwkv_mu"], dr, dk, dv, dlo, dgd)

    dz = jnp.concatenate([dz_r, dqk_raw, dv_all.astype(BF16), dz_g], axis=1)
    dh2 = _mmc_nt("w_in_dh", dz, wg["w_in"], 0, 0, IN_SHARD, D_MODEL)
    gm["w_in"] = _mmc_tn("w_in_dw", h2, dz, IN_SHARD)
    mixer_token = on_grads("mixer", {n: gm.pop(n) for n in SCATTER_GROUPS["mixer"]})
    dx1, gs["mix_norm"] = _norm_bwd("mix_dnorm", x1, sm["mix_norm"] + mixer_token[0, 0], dh2, dx2)

    dx0, gs["ffn1_norm"], gm["ffn1_gu"], _ = _ffn_bwd(
        "ffn1", x, sm["ffn1_norm"], wg["ffn1_gu"], w_d1, ffn1_saved, dx1,
        on_down=lambda blocks: on_grads("ffn1_down", {"ffn1_d": blocks}))
    return loss, dx0, gm, gs


def kernel(x, p, positions, ffn1_norm, ffn1_w_gate, ffn1_w_up, ffn1_w_down, mix_norm, w_in, rwkv_mu, rwkv_w0, rwkv_w2, rwkv_a0, rwkv_a2, rwkv_g2, rwkv_k_k, rwkv_k_a, rwkv_r_k, rwkv_gn_w, rwkv_gn_b, q_norm, k_norm, w_br_rwkv, w_br_attn, w_out, ffn2_norm, ffn2_w_gate, ffn2_w_up, ffn2_w_down, ple_norm, ple_w_gate, ple_w_proj, loss_target, m_ffn1_norm, m_ffn1_w_gate, m_ffn1_w_up, m_ffn1_w_down, m_mix_norm, m_w_in, m_rwkv_mu, m_rwkv_w0, m_rwkv_w2, m_rwkv_a0, m_rwkv_a2, m_rwkv_g2, m_rwkv_k_k, m_rwkv_k_a, m_rwkv_r_k, m_rwkv_gn_w, m_rwkv_gn_b, m_q_norm, m_k_norm, m_w_br_rwkv, m_w_br_attn, m_w_out, m_ffn2_norm, m_ffn2_w_gate, m_ffn2_w_up, m_ffn2_w_down, m_ple_norm, m_ple_w_gate, m_ple_w_proj, v_ffn1_norm, v_ffn1_w_gate, v_ffn1_w_up, v_ffn1_w_down, v_mix_norm, v_w_in, v_rwkv_mu, v_rwkv_w0, v_rwkv_w2, v_rwkv_a0, v_rwkv_a2, v_rwkv_g2, v_rwkv_k_k, v_rwkv_k_a, v_rwkv_r_k, v_rwkv_gn_w, v_rwkv_gn_b, v_q_norm, v_k_norm, v_w_br_rwkv, v_w_br_attn, v_w_out, v_ffn2_norm, v_ffn2_w_gate, v_ffn2_w_up, v_ffn2_w_down, v_ple_norm, v_ple_w_gate, v_ple_w_proj):
    args = locals()
    w = {n: args[n][0] for n in WEIGHTS}
    m = {n: args["m_" + n][0] for n in WEIGHTS}
    v = {n: args["v_" + n][0] for n in WEIGHTS}

    w_loc = _local_blocks(w)
    head = GATHER_GROUPS["head"]
    wg = dict(zip(head, _all_gather_hbm("gather_head", [w_loc[n] for n in head])))
    me = 4 * lax.axis_index("x") + 2 * lax.axis_index("y") + lax.axis_index("c")
    gathering, order_after = {}, wg[head[0]]
    for group in ("mid", "rest"):
        shards = [w_loc[n] for n in GATHER_GROUPS[group]]
        zones = [lax.dynamic_update_slice(lax.empty((N_DEV,) + a.shape, a.dtype), a[None], (me, 0, 0)) for a in shards]
        copies = _gather2_first if group == "mid" else _gather_copies
        *gathering[group], order_after = _exchange_start("gather_start_" + group, copies, shards, zones, order_after)

    def fetch(group, after):
        if group == "forward":
            s1, r1, srcs, lands = gathering["mid"]
            ici = functools.partial(_gather2_first, ks=(1, 2, 3))
            srcs, lands = _exchange_wait("gather_arrived_mid", ici, s1, r1, srcs, lands, after, parts=("recv",))
            s2, r2, srcs, lands, _ = _exchange_start("gather_forward_mid", _gather2_second, srcs, lands, after)
            gathering["mid"] = (s1, r1, s2, r2, srcs, lands)
            return {}
        if group == "mid":
            s1, r1, s2, r2, srcs, lands = gathering["mid"]
            sib = functools.partial(_gather2_first, ks=(0,))
            srcs, lands = _exchange_wait("gather_passed_mid", _gather2_second, s2, r2, srcs, lands, after)
            srcs, lands = _exchange_wait("gather_sent_mid", _gather2_first, s1, r1, srcs, lands, after, parts=("send",))
            _, got = _exchange_wait("gather_wait_mid", sib, s1, r1, srcs, lands, after, parts=("recv",))
        else:
            _, got = _exchange_wait("gather_wait_" + group, _gather_copies, *gathering[group], after)
        return dict(zip(GATHER_GROUPS[group], got))

    sm = {n: w[n].reshape(1, -1) for n in SMALL}
    sm["ffn1_norm"] = sm["ffn1_norm"] + order_after[0, 0]
    in_flight = {}

    core = lax.axis_index("c").astype(jnp.int32).reshape(1)
    chip_slot = (2 * lax.axis_index("x") + lax.axis_index("y")).astype(jnp.int32).reshape(1)
    device_slot = 2 * chip_slot + core

    def scatter_early(group, arrays):
        arrs = [arrays[n] for n in SCATTER_GROUPS[group]]
        copies, after = _scatter_copies, arrs[0]
        if group in TWO_LEVEL:
            halves = _pair_swap("scatter_pair_" + group, arrs)
            arrs = [_pair_add("scatter_add_%s_%s" % (group, n), a, hf, core)
                    for n, a, hf in zip(SCATTER_GROUPS[group], arrs, halves)]
            copies, after = _chip_copies, halves[0]
        *in_flight[group], token = _exchange_start("scatter_start_" + group, copies, arrs,
                                                   [lax.empty(a.shape, a.dtype) for a in arrs], after)
        return token
    loss_part, dx, gm, gs = _local_step(x[0], p[0, 0], positions[0], loss_target[0], sm, wg, fetch, scatter_early)
    head_token = scatter_early("head", gm)
    recv, own = {}, {}

    def arrived(group, after):
        copies = _chip_copies if group in TWO_LEVEL else _scatter_copies
        sent, lands = _exchange_wait("scatter_wait_" + group, copies, *in_flight[group], after)
        own.update(zip(SCATTER_GROUPS[group], sent))
        recv.update(zip(SCATTER_GROUPS[group], lands))
    for group in ("tail", "branch", "mixer", "ffn1_down"):
        arrived(group, head_token)
    small_like = [w[n] for n in SMALL]
    small_rows = 80
    gs_all = _all_gather_vmem(_pack_small([gs[n] for n in SMALL] + [loss_part], small_rows))
    gs_sum = _sum_slots("sum_small_grads", gs_all, small_rows)
    loss = gs_sum.reshape(-1)[sum(a.size for a in small_like)]

    res = {}
    early = [e for group in ("tail", "branch", "mixer", "ffn1_down") for e in ADAM_PLAN
             if e[1] in SCATTER_GROUPS[group]]
    late = [e for e in ADAM_PLAN if e[1] in SCATTER_GROUPS["head"]]
    order = head_token
    for n, src, rb, cb, cw, tr in early + late:
        if (n, src, rb, cb, cw, tr) == late[0]:
            arrived("head", order)
        slot = chip_slot if recv[src].shape[0] == N_CHIPS else device_slot
        outs4 = _adamw_slots("adamw_" + n, recv[src], own[src], slot, rb, cb, cw, w[n], m[n], v[n], tr, order)
        order = outs4[1]
        for tag, a in zip(("grad", "delta", "new_m", "new_v"), outs4):
            res[tag, n] = a[None]
    d_s, m_s, v_s = _adamw("adamw_small", _pack_small(small_like, small_rows), gs_sum,
                           _pack_small([m[n] for n in SMALL], small_rows),
                           _pack_small([v[n] for n in SMALL], small_rows), small_rows)
    for tag, small in (("grad", gs_sum), ("delta", d_s), ("new_m", m_s), ("new_v", v_s)):
        for n, a in zip(SMALL, _unpack_small(small, small_like)):
            res[tag, n] = a[None]
    outs = [loss, dx[None]]
    for tag in ("grad", "delta", "new_m", "new_v"):
        outs += [res[tag, n] for n in WEIGHTS]
    return tuple(outs)
```

```python
import functools

import jax
import jax.numpy as jnp
from jax import lax
from jax.experimental import pallas as pl
from jax.experimental.pallas import tpu as pltpu

F32, BF16 = jnp.float32, jnp.bfloat16
MESH = pl.DeviceIdType.MESH
N_DEV = 8
LANES = 128
VMEM_LIMIT = 56 * 1024 * 1024

D_MODEL = 1024
PLE_DIM = 256
HEAD_DIM = 64
RWKV_HEADS = 8
RWKV_DIM = RWKV_HEADS * HEAD_DIM
DECAY_LORA = 64
ICLR_LORA = 64
GATE_LORA = 128
GN_EPS = 64e-5
ATTN_GROUPS = ((128, 1), (512, 4), (2048, 16))
HEADS_PER_GROUP = 4
ATTN_HEADS = HEADS_PER_GROUP * len(ATTN_GROUPS)
ATTN_DIM = ATTN_HEADS * HEAD_DIM
BAND_BLOCK = 128
ROPE_THETA = 10000.0
NEG_INF = -1e30
D_FF = 2816
RMS_EPS = 1e-6
RWKV_COLS = 3 * RWKV_DIM + DECAY_LORA + ICLR_LORA + GATE_LORA
ADAM_LR = 0.001
ADAM_B1 = 0.9
ADAM_B2 = 0.999
ADAM_EPS = 1e-08
ADAM_WD = 0.01
ADAM_STEP = 10

V_LO = LANES // RWKV_HEADS
V_HI = HEAD_DIM // V_LO
SCAN_CHUNK = 64
MM_ROWS = 2048
BIG_ROWS = 1024

FF_SHARD = D_FF // N_DEV
FF_PAD = -(-FF_SHARD // LANES) * LANES
FF_HID = N_DEV * FF_PAD
IN_SHARD = 6144 // N_DEV
OUT_SHARD = D_MODEL // N_DEV

SMALL = ("ffn1_norm", "mix_norm", "rwkv_mu", "rwkv_w0", "rwkv_a0", "rwkv_k_k", "rwkv_k_a", "rwkv_r_k",
         "rwkv_gn_w", "rwkv_gn_b", "q_norm", "k_norm", "ffn2_norm", "ple_norm")
WEIGHTS = ("ffn1_norm", "ffn1_w_gate", "ffn1_w_up", "ffn1_w_down", "mix_norm", "w_in", "rwkv_mu", "rwkv_w0",
           "rwkv_w2", "rwkv_a0", "rwkv_a2", "rwkv_g2", "rwkv_k_k", "rwkv_k_a", "rwkv_r_k", "rwkv_gn_w",
           "rwkv_gn_b", "q_norm", "k_norm", "w_br_rwkv", "w_br_attn", "w_out", "ffn2_norm", "ffn2_w_gate",
           "ffn2_w_up", "ffn2_w_down", "ple_norm", "ple_w_gate", "ple_w_proj")


def _cparams(**kw):
    return pltpu.CompilerParams(vmem_limit_bytes=VMEM_LIMIT, **kw)


def _tile(n, cap):
    best = None
    for t in range(LANES, min(n, cap) + 1, LANES):
        if n % t == 0:
            best = t
    return best if best is not None else n


@jax.custom_vjp
def _bdot(a, w):
    return jnp.dot(a.astype(BF16), w.astype(BF16), preferred_element_type=F32)


def _bdot_fwd(a, w):
    return _bdot(a, w), (a, w)


def _bdot_bwd(res, g):
    a, w = res
    gb = g.astype(BF16)
    da = lax.dot_general(gb, w.astype(BF16), (((1,), (1,)), ((), ())), preferred_element_type=F32)
    dw = lax.dot_general(a.astype(BF16), gb, (((0,), (0,)), ((), ())), preferred_element_type=F32)
    return da.astype(a.dtype), dw.astype(w.dtype)


_bdot.defvjp(_bdot_fwd, _bdot_bwd)


@jax.custom_vjp
def _bdot_nt(a, b):
    return lax.dot_general(a.astype(BF16), b.astype(BF16), (((1,), (1,)), ((), ())), preferred_element_type=F32)


def _bdot_nt_fwd(a, b):
    return _bdot_nt(a, b), (a, b)


def _bdot_nt_bwd(res, g):
    a, b = res
    gb = g.astype(BF16)
    da = jnp.dot(gb, b.astype(BF16), preferred_element_type=F32)
    db = lax.dot_general(gb, a.astype(BF16), (((0,), (0,)), ((), ())), preferred_element_type=F32)
    return da.astype(a.dtype), db.astype(b.dtype)


_bdot_nt.defvjp(_bdot_nt_fwd, _bdot_nt_bwd)


def _mm(name, a, b, mode, out_dtype=F32, res=None, scale=None):
    if mode == "nn":
        (m, k), n = a.shape, b.shape[1]
    elif mode == "nt":
        (m, k), n = a.shape, b.shape[0]
    else:
        (k, m), n = a.shape, b.shape[1]
    tm, tn = _tile(m, 1024), _tile(n, 1024)
    a_spec = pl.BlockSpec((k, tm), lambda i, j: (0, i)) if mode == "tn" else pl.BlockSpec((tm, k), lambda i, j: (i, 0))
    b_spec = pl.BlockSpec((tn, k), lambda i, j: (j, 0)) if mode == "nt" else pl.BlockSpec((k, tn), lambda i, j: (0, j))
    dims = {"nn": ((1,), (0,)), "nt": ((1,), (1,)), "tn": ((0,), (0,))}[mode]
    o_spec = pl.BlockSpec((tm, tn), lambda i, j: (i, j))
    ins, in_specs = [a, b], [a_spec, b_spec]
    if res is not None:
        ins.append(res)
        in_specs.append(o_spec)

    def body(*refs):
        acc = lax.dot_general(refs[0][...].astype(BF16), refs[1][...].astype(BF16), (dims, ((), ())),
                              preferred_element_type=F32)
        if scale is not None:
            acc = acc * scale
        if res is not None:
            acc = acc + refs[2][...].astype(F32)
        refs[-1][...] = acc.astype(refs[-1].dtype)

    return pl.pallas_call(
        body, name=name, grid=(m // tm, n // tn), in_specs=in_specs, out_specs=o_spec,
        out_shape=jax.ShapeDtypeStruct((m, n), out_dtype),
        compiler_params=_cparams(dimension_semantics=("parallel", "parallel")),
    )(*ins)


def _mmc_nn(name, a, wb, ki, ci, n, out_dtype=F32):
    m, k = a.shape
    tm = _tile(m, MM_ROWS)

    def body(a_ref, w_ref, o_ref):
        o_ref[...] = jnp.dot(a_ref[...].astype(BF16), w_ref[...], preferred_element_type=F32).astype(o_ref.dtype)

    return pl.pallas_call(
        body, name=name, grid=(m // tm, N_DEV),
        in_specs=[pl.BlockSpec((tm, k), lambda i, j: (i, 0)), pl.BlockSpec((None, k, n), lambda i, j: (j, ki, ci))],
        out_specs=pl.BlockSpec((tm, n), lambda i, j: (i, j)),
        out_shape=jax.ShapeDtypeStruct((m, N_DEV * n), out_dtype),
        compiler_params=_cparams(dimension_semantics=("parallel", "parallel")),
    )(a, wb)


def _mmc_nt(name, a, wb, ki, ci, n, k, res=None):
    m = a.shape[0]
    tm = _tile(m, MM_ROWS)
    o_spec = pl.BlockSpec((tm, k), lambda i, j: (i, 0))
    ins = [a, wb] + ([res] if res is not None else [])
    in_specs = [pl.BlockSpec((tm, n), lambda i, j: (i, j)), pl.BlockSpec((None, k, n), lambda i, j: (j, ki, ci))]
    in_specs += [o_spec] if res is not None else []

    def body(*refs):
        a_ref, w_ref, o_ref = refs[0], refs[1], refs[-1]
        acc = lax.dot_general(a_ref[...].astype(BF16), w_ref[...], (((1,), (1,)), ((), ())),
                              preferred_element_type=F32)

        @pl.when(pl.program_id(1) == 0)
        def _():
            o_ref[...] = acc + refs[2][...] if res is not None else acc

        @pl.when(pl.program_id(1) != 0)
        def _():
            o_ref[...] += acc

    return pl.pallas_call(
        body, name=name, grid=(m // tm, N_DEV), in_specs=in_specs, out_specs=o_spec,
        out_shape=jax.ShapeDtypeStruct((m, k), F32),
        compiler_params=_cparams(dimension_semantics=("parallel", "arbitrary")),
    )(*ins)


def _mmc_tn(name, x, dy, n):
    m, k = x.shape
    tk = _tile(k, 1024)

    def body(x_ref, dy_ref, o_ref):
        o_ref[...] = lax.dot_general(x_ref[...].astype(BF16), dy_ref[...].astype(BF16), (((0,), (0,)), ((), ())),
                                     preferred_element_type=F32).astype(o_ref.dtype)

    return pl.pallas_call(
        body, name=name, grid=(N_DEV, k // tk),
        in_specs=[pl.BlockSpec((m, tk), lambda j, i: (0, i)), pl.BlockSpec((m, n), lambda j, i: (0, j))],
        out_specs=pl.BlockSpec((None, tk, n), lambda j, i: (j, i, 0)),
        out_shape=jax.ShapeDtypeStruct((N_DEV, k, n), BF16),
        compiler_params=_cparams(dimension_semantics=("parallel", "parallel")),
    )(x, dy)


def _rowwise(name, fn, rows, params, out_rows, out_accs=(), tr=512):
    rows = [a if isinstance(a, tuple) else (a, a.shape[1], 0) for a in rows]
    r = rows[0][0].shape[0]
    in_specs = [pl.BlockSpec((tr, wd), lambda i, cb=cb: (i, cb)) for _, wd, cb in rows]
    rows = [a for a, _, _ in rows]
    in_specs += [pl.BlockSpec(p.shape, lambda i, nd=p.ndim: (0,) * nd) for p in params]
    out_shape = [jax.ShapeDtypeStruct((r, c), dt) for c, dt in out_rows]
    out_shape += [jax.ShapeDtypeStruct(s, F32) for s in out_accs]
    out_specs = [pl.BlockSpec((tr, c), lambda i: (i, 0)) for c, _ in out_rows]
    out_specs += [pl.BlockSpec(s, lambda i, nd=len(s): (0,) * nd) for s in out_accs]
    n_in, n_ro = len(rows) + len(params), len(out_rows)

    def body(*refs):
        res = fn(*[ref[...] for ref in refs[:n_in]])
        outs = refs[n_in:]
        for o, v in zip(outs[:n_ro], res[:n_ro]):
            o[...] = v.astype(o.dtype)
        for o, v in zip(outs[n_ro:], res[n_ro:]):
            _accumulate(o, v)

    return pl.pallas_call(
        body, name=name, grid=(r // tr,), in_specs=in_specs, out_specs=out_specs, out_shape=out_shape,
        compiler_params=_cparams(dimension_semantics=("arbitrary",)),
    )(*rows, *params)


def _pairwise(name, fn, rows, params, n_out, n_acc=0, tr=2048):
    t, c = rows[0].shape
    tile = pl.BlockSpec((tr, 2 * HEAD_DIM), lambda p, i: (i, p))
    vec = pl.BlockSpec((1, 2 * HEAD_DIM), lambda p, i: (0, p))
    n_in = len(rows) + len(params)

    def body(*refs):
        res = fn(*[ref[...] for ref in refs[:n_in]])
        outs = refs[n_in:]
        for o, v in zip(outs[:n_out], res[:n_out]):
            o[...] = v
        first = pl.program_id(1) == 0
        for o, v in zip(outs[n_out:], res[n_out:]):
            @pl.when(first)
            def _(o=o, v=v):
                o[...] = v

            @pl.when(jnp.logical_not(first))
            def _(o=o, v=v):
                o[...] += v

    return pl.pallas_call(
        body, name=name, grid=(c // (2 * HEAD_DIM), t // tr),
        in_specs=[tile] * len(rows) + [vec] * len(params), out_specs=[tile] * n_out + [vec] * n_acc,
        out_shape=[jax.ShapeDtypeStruct((t, c), F32)] * n_out + [jax.ShapeDtypeStruct((1, c), F32)] * n_acc,
        compiler_params=_cparams(dimension_semantics=("parallel", "arbitrary")),
    )(*rows, *params)


def _accumulate(o_ref, v):
    @pl.when(pl.program_id(0) == 0)
    def _():
        o_ref[...] = v

    @pl.when(pl.program_id(0) != 0)
    def _():
        o_ref[...] += v


def _rms(x, g):
    return x * lax.rsqrt(jnp.mean(x * x, axis=-1, keepdims=True) + RMS_EPS) * g


def _sigmoid(x):
    return jax.nn.sigmoid(x)


def _softplus(x):
    return jnp.maximum(x, 0.0) + jnp.log1p(jnp.exp(-jnp.abs(x)))


def _norm_fwd(name, x, g):
    return _rowwise(name, lambda xv, gv: (_rms(xv, gv),), [x], [g], [(x.shape[1], BF16)], tr=BIG_ROWS)[0]


def _norm_bwd(name, x, g, dh, dres):
    def fn(xv, dhv, drv, gv):
        _, vjp = jax.vjp(_rms, xv, gv)
        dx, dg = vjp(dhv)
        return dx + drv, dg
    return _rowwise(name, fn, [x, dh, dres], [g], [(x.shape[1], F32)], [g.shape], tr=BIG_ROWS)


def _f_act(gate, up):
    return gate * _sigmoid(gate) * up


def _gate_up_act(name, h, w_gu):
    m, k = h.shape
    tm = _tile(m, MM_ROWS)

    def body(h_ref, w_ref, gu_ref, a_ref):
        gu = jnp.dot(h_ref[...], w_ref[...], preferred_element_type=F32)
        gu_ref[...] = gu
        a_ref[...] = _f_act(gu[:, :FF_PAD], gu[:, FF_PAD:]).astype(a_ref.dtype)

    return pl.pallas_call(
        body, name=name, grid=(m // tm, N_DEV),
        in_specs=[pl.BlockSpec((tm, k), lambda i, j: (i, 0)),
                  pl.BlockSpec((None, k, 2 * FF_PAD), lambda i, j: (j, 0, 0))],
        out_specs=[pl.BlockSpec((tm, 2 * FF_PAD), lambda i, j: (i, j)), pl.BlockSpec((tm, FF_PAD), lambda i, j: (i, j))],
        out_shape=[jax.ShapeDtypeStruct((m, N_DEV * 2 * FF_PAD), F32), jax.ShapeDtypeStruct((m, FF_HID), BF16)],
        compiler_params=_cparams(dimension_semantics=("parallel", "parallel")),
    )(h, w_gu)


def _gate_up_act_bwd(name, dout, w_down, gu, order):
    m, k = dout.shape
    tm = _tile(m, MM_ROWS)

    def body(d_ref, w_ref, gu_ref, order_ref, o_ref):
        da = 0.5 * lax.dot_general(d_ref[...].astype(BF16), w_ref[...], (((1,), (1,)), ((), ())),
                                   preferred_element_type=F32)
        guv = gu_ref[...]
        _, vjp = jax.vjp(_f_act, guv[:, :FF_PAD], guv[:, FF_PAD:])
        o_ref[...] = jnp.concatenate(vjp(da), axis=1).astype(o_ref.dtype)

    gu_spec = pl.BlockSpec((tm, 2 * FF_PAD), lambda i, j: (i, j))
    return pl.pallas_call(
        body, name=name, grid=(m // tm, N_DEV),
        in_specs=[pl.BlockSpec((tm, k), lambda i, j: (i, 0)), pl.BlockSpec((FF_PAD, k), lambda i, j: (j, 0)), gu_spec,
                  pl.BlockSpec(memory_space=pl.ANY)],
        out_specs=gu_spec, out_shape=jax.ShapeDtypeStruct((m, N_DEV * 2 * FF_PAD), BF16),
        compiler_params=_cparams(dimension_semantics=("parallel", "parallel")),
    )(dout, w_down, gu, order)


def _ffn_fwd(tag, x, norm, w_gu, w_down):
    h = _norm_fwd(tag + "_norm", x, norm)
    gu, a = _gate_up_act(tag + "_gu", h, w_gu)
    wd = w_down(a)
    out = _mm(tag + "_down", a, wd, "nn", res=x, scale=0.5)
    return out, (h, gu, a), wd


def _ffn_bwd(tag, x, norm, w_gu, w_down, saved, dout, on_down=None):
    h, gu, a = saved
    d_wdown = _mm(tag + "_dwdown", a, dout, "tn", out_dtype=BF16, scale=0.5).reshape(N_DEV, FF_PAD, D_MODEL)
    token = on_down(d_wdown) if on_down is not None else jnp.zeros((8, LANES), F32)
    dgu = _gate_up_act_bwd(tag + "_dgu", dout, w_down, gu, token)
    dh =_mmc_nt(tag + "_dh", dgu, w_gu, 0, 0, 2 * FF_PAD, D_MODEL)
    d_wgu = _mmc_tn(tag + "_dwgu", h, dgu, 2 * FF_PAD)
    dx, dnorm = _norm_bwd(tag + "_dnorm", x, norm, dh, dout)
    return dx, dnorm, d_wgu, d_wdown


def _shift_fwd(z, mu):
    t, c = z.shape[0], RWKV_COLS
    tr = 512

    def body(z_ref, zp_ref, mu_ref, r_ref, k_ref, v_ref, lo_ref, gd_ref):
        zv = z_ref[...]
        prev = zp_ref[7:8, :] * jnp.where(pl.program_id(0) == 0, 0.0, 1.0)
        row = lax.broadcasted_iota(jnp.int32, zv.shape, 0)
        zsh = jnp.where(row == 0, prev, pltpu.roll(zv, 1, 0))
        zs = zv + (zsh - zv) * mu_ref[...]
        r_ref[...] = zs[:, 0:512]
        k_ref[...] = zs[:, 512:1024]
        v_ref[...] = zs[:, 1024:1536]
        lo_ref[...] = zs[:, 1536:1664]
        gd_ref[...] = zs[:, 1664:1792]

    widths = (512, 512, 512, 128, 128)
    return pl.pallas_call(
        body, name="rwkv_shift", grid=(t // tr,),
        in_specs=[pl.BlockSpec((tr, c), lambda i: (i, 0)),
                  pl.BlockSpec((8, c), lambda i: (jnp.maximum(i * (tr // 8) - 1, 0), 0)),
                  pl.BlockSpec((1, c), lambda i: (0, 0))],
        out_specs=[pl.BlockSpec((tr, w), lambda i: (i, 0)) for w in widths],
        out_shape=[jax.ShapeDtypeStruct((t, w), F32) for w in widths],
        compiler_params=_cparams(dimension_semantics=("parallel",)),
    )(z, z, mu)


def _shift_bwd(z, mu, dr, dk, dv, dlo, dgd):
    t, c = z.shape[0], RWKV_COLS
    tr = 512
    nt = t // tr

    def body(z_ref, zp_ref, mu_ref, dr_ref, dk_ref, dv_ref, dlo_ref, dgd_ref,
             drn_ref, dkn_ref, dvn_ref, dlon_ref, dgdn_ref, dz_ref, dmu_ref):
        i = pl.program_id(0)
        zv, muv = z_ref[...], mu_ref[...]
        prev = zp_ref[7:8, :] * jnp.where(i == 0, 0.0, 1.0)
        row = lax.broadcasted_iota(jnp.int32, zv.shape, 0)
        zsh = jnp.where(row == 0, prev, pltpu.roll(zv, 1, 0))
        dzs = jnp.concatenate([dr_ref[...], dk_ref[...], dv_ref[...], dlo_ref[...], dgd_ref[...]], axis=1)
        nxt = jnp.concatenate([drn_ref[0:1, :], dkn_ref[0:1, :], dvn_ref[0:1, :], dlon_ref[0:1, :],
                               dgdn_ref[0:1, :]], axis=1) * jnp.where(i == nt - 1, 0.0, 1.0)
        u = dzs * muv
        un = jnp.where(row == tr - 1, nxt * muv, pltpu.roll(u, tr - 1, 0))
        dz_ref[...] = (dzs - u + un).astype(dz_ref.dtype)
        _accumulate(dmu_ref, jnp.sum(dzs * (zsh - zv), axis=0, keepdims=True))

    widths = (512, 512, 512, 128, 128)
    nxt_map = lambda i: (jnp.minimum((i + 1) * (tr // 8), t // 8 - 1), 0)
    return pl.pallas_call(
        body, name="rwkv_shift_bwd", grid=(nt,),
        in_specs=[pl.BlockSpec((tr, c), lambda i: (i, 0)),
                  pl.BlockSpec((8, c), lambda i: (jnp.maximum(i * (tr // 8) - 1, 0), 0)),
                  pl.BlockSpec((1, c), lambda i: (0, 0))]
        + [pl.BlockSpec((tr, w), lambda i: (i, 0)) for w in widths]
        + [pl.BlockSpec((8, w), nxt_map) for w in widths],
        out_specs=[pl.BlockSpec((tr, c), lambda i: (i, 0)), pl.BlockSpec((1, c), lambda i: (0, 0))],
        out_shape=[jax.ShapeDtypeStruct((t, c), BF16), jax.ShapeDtypeStruct((1, c), F32)],
        compiler_params=_cparams(dimension_semantics=("arbitrary",)),
    )(z, z, mu, dr, dk, dv, dlo, dgd, dr, dk, dv, dlo, dgd)


def _f_pre(k, lo, gd, w0, w2p, a0, a2p, g2, k_k, k_a):
    lane = lax.broadcasted_iota(jnp.int32, lo.shape, 1)
    lo_act = jnp.where(lane < DECAY_LORA, jnp.tanh(lo), lo)
    w = -_softplus(-(w0 + _bdot(lo_act, w2p))) - 0.5
    a = _sigmoid(a0 + _bdot(lo_act, a2p))
    g = _bdot(_sigmoid(gd), g2)
    kk = k * k_k
    k2 = k * (1.0 + (a - 1.0) * k_a)
    decay = jnp.exp(-jnp.exp(w))
    return k2, kk, a, decay, g


def _f_kk(kk, a):
    kkn = kk * lax.rsqrt(jnp.maximum(_head_sums(kk * kk), 1e-24))
    return kkn, kkn * a


def _f_post(y, r, k2, v, g, gn_w, gn_b, r_k):
    mean = _head_sums(y) * (1.0 / HEAD_DIM)
    var = _head_sums(jnp.square(y - mean)) * (1.0 / HEAD_DIM)
    yn = (y - mean) * lax.rsqrt(var + GN_EPS) * gn_w + gn_b
    bonus = _head_sums(r * k2 * r_k) * v
    return (yn + bonus) * g


def _to_v_rows(x):
    t = x.shape[0]
    return x.reshape(t, RWKV_HEADS, V_HI, V_LO).transpose(0, 2, 3, 1).reshape(t, V_HI, LANES)


def _from_v_rows(x):
    t = x.shape[0]
    return x.reshape(t, V_HI, V_LO, RWKV_HEADS).transpose(0, 3, 1, 2).reshape(t, RWKV_DIM)


def _k_cols(x):
    return jnp.tile(x, (V_LO, 1)).T


def _k_rows(x):
    xt = x.T
    out = xt[0:RWKV_HEADS]
    for l in range(1, V_LO):
        out = out + xt[l * RWKV_HEADS:(l + 1) * RWKV_HEADS]
    return out


def _wkv_fwd(r, w, k, kk, b, v):
    t = r.shape[0]
    tc = SCAN_CHUNK
    key_spec = pl.BlockSpec((tc, RWKV_HEADS, HEAD_DIM), lambda i: (i, 0, 0))
    row_spec = pl.BlockSpec((tc, V_HI, LANES), lambda i: (i, 0, 0))

    def body(r_ref, w_ref, k_ref, kk_ref, b_ref, v_ref, y_ref, st_ref, s_scr, cols_a, cols_b):
        @pl.when(pl.program_id(0) == 0)
        def _():
            s_scr[...] = jnp.zeros_like(s_scr)

        def prep(ti, buf):
            for n, ref in enumerate((r_ref, w_ref, k_ref, kk_ref, b_ref)):
                buf[n] = _k_cols(ref[ti])

        def step(ti, s, cur, nxt, ti_next):
            rc, wc, kc, kkc, bc = (cur[n] for n in range(5))
            prep(ti_next, nxt)
            vt = v_ref[ti]
            new, ys = [], []
            for j in range(V_HI):
                sa = -jnp.sum(s[j] * kkc, axis=0, keepdims=True)
                nj = s[j] * wc + bc * sa + kc * vt[j:j + 1]
                st_ref[ti, j] = nj
                ys.append(jnp.sum(nj * rc, axis=0, keepdims=True))
                new.append(nj)
            y_ref[ti] = jnp.concatenate(ys, axis=0)
            return tuple(new)

        def pair(i, s):
            s = step(2 * i, s, cols_a, cols_b, 2 * i + 1)
            return step(2 * i + 1, s, cols_b, cols_a, jnp.minimum(2 * i + 2, tc - 1))

        prep(0, cols_a)
        s = lax.fori_loop(0, tc // 2, pair, tuple(s_scr[j] for j in range(V_HI)))
        for j in range(V_HI):
            s_scr[j] = s[j]

    return pl.pallas_call(
        body, name="wkv_fwd", grid=(t // tc,),
        in_specs=[key_spec] * 5 + [row_spec],
        out_specs=[row_spec, pl.BlockSpec((tc, V_HI, HEAD_DIM, LANES), lambda i: (i, 0, 0, 0))],
        out_shape=[jax.ShapeDtypeStruct((t, V_HI, LANES), F32),
                   jax.ShapeDtypeStruct((t, V_HI, HEAD_DIM, LANES), F32)],
        scratch_shapes=[pltpu.VMEM((V_HI, HEAD_DIM, LANES), F32)] + [pltpu.VMEM((5, HEAD_DIM, LANES), F32)] * 2,
        compiler_params=_cparams(dimension_semantics=("arbitrary",)),
    )(r, w, k, kk, b, v)


def _wkv_bwd(r, w, k, kk, b, v, states, dy):
    t = r.shape[0]
    tc = SCAN_CHUNK
    nb = t // tc
    key_spec = pl.BlockSpec((tc, RWKV_HEADS, HEAD_DIM), lambda i: (nb - 1 - i, 0, 0))
    row_spec = pl.BlockSpec((tc, V_HI, LANES), lambda i: (nb - 1 - i, 0, 0))
    st_spec = pl.BlockSpec((tc, V_HI, HEAD_DIM, LANES), lambda i: (nb - 1 - i, 0, 0, 0))
    stp_spec = pl.BlockSpec((1, V_HI, HEAD_DIM, LANES), lambda i: (jnp.maximum((nb - 1 - i) * tc - 1, 0), 0, 0, 0))

    def body(r_ref, w_ref, k_ref, kk_ref, b_ref, v_ref, st_ref, stp_ref, dy_ref,
             dr_ref, dw_ref, dk_ref, dkk_ref, db_ref, dv_ref, ds_scr, cols_a, cols_b, accs_a, accs_b):
        @pl.when(pl.program_id(0) == 0)
        def _():
            ds_scr[...] = jnp.zeros_like(ds_scr)

        def colsum(x):
            return jnp.sum(x, axis=0, keepdims=True)

        def prep(ti, buf):
            for n, ref in enumerate((r_ref, w_ref, k_ref, kk_ref, b_ref)):
                buf[n] = _k_cols(ref[ti])

        def flush(ti, buf):
            for n, ref in enumerate((dr_ref, dk_ref, db_ref, dw_ref, dkk_ref)):
                ref[ti] = _k_rows(buf[n])

        def step(ti, ds, sp, cur, accs):
            rc, wc, kc, kkc, bc = (cur[n] for n in range(5))
            vt, dyt = v_ref[ti], dy_ref[ti]
            acc = None
            new, dvs = [], []
            for j in range(V_HI):
                st = st_ref[ti, j]
                dsj = ds[j] + rc * dyt[j:j + 1]
                sa = -colsum(sp[j] * kkc)
                dsa = colsum(dsj * bc)
                dvs.append(colsum(dsj * kc))
                parts = (st * dyt[j:j + 1], dsj * vt[j:j + 1], dsj * sa, dsj * sp[j], -(sp[j] * dsa))
                acc = parts if acc is None else tuple(a + q for a, q in zip(acc, parts))
                new.append(dsj * wc - kkc * dsa)
            dv_ref[ti] = jnp.concatenate(dvs, axis=0)
            for n in range(5):
                accs[n] = acc[n]
            return tuple(new)

        def states_before(ti):
            return tuple(st_ref[ti - 1, j] for j in range(V_HI))

        def pair(i, ds):
            ta = tc - 1 - 2 * i
            prep(ta - 1, cols_b)
            flush(jnp.minimum(ta + 1, tc - 1), accs_b)
            ds = step(ta, ds, states_before(ta), cols_a, accs_a)
            prep(ta - 2, cols_a)
            flush(ta, accs_a)
            return step(ta - 1, ds, states_before(ta - 1), cols_b, accs_b)

        prep(tc - 1, cols_a)
        accs_b[...] = jnp.zeros_like(accs_b)
        ds = lax.fori_loop(0, tc // 2 - 1, pair, tuple(ds_scr[j] for j in range(V_HI)))
        prep(0, cols_b)
        flush(2, accs_b)
        ds = step(1, ds, states_before(1), cols_a, accs_a)
        flush(1, accs_a)
        keep = jnp.where(pl.program_id(0) == nb - 1, 0.0, 1.0)
        ds = step(0, ds, tuple(stp_ref[0, j] * keep for j in range(V_HI)), cols_b, accs_b)
        flush(0, accs_b)
        for j in range(V_HI):
            ds_scr[j] = ds[j]

    key_out = jax.ShapeDtypeStruct((t, RWKV_HEADS, HEAD_DIM), F32)
    return pl.pallas_call(
        body, name="wkv_bwd", grid=(nb,),
        in_specs=[key_spec] * 5 + [row_spec, st_spec, stp_spec, row_spec],
        out_specs=[key_spec] * 5 + [row_spec],
        out_shape=[key_out] * 5 + [jax.ShapeDtypeStruct((t, V_HI, LANES), F32)],
        scratch_shapes=[pltpu.VMEM((V_HI, HEAD_DIM, LANES), F32)] + [pltpu.VMEM((5, HEAD_DIM, LANES), F32)] * 4,
        compiler_params=_cparams(dimension_semantics=("arbitrary",)),
    )(r, w, k, kk, b, v, states, states, dy)


PAIR = 2 * HEAD_DIM
N_PAIRS = ATTN_HEADS // 2
Q_COL0 = RWKV_COLS // PAIR
K_COL0 = Q_COL0 + N_PAIRS
V_COL0 = K_COL0 + N_PAIRS


def _swap_halves(x):
    lane = lax.broadcasted_iota(jnp.int32, x.shape, 1)
    return jnp.where((lane & (HEAD_DIM - 1)) < HEAD_DIM // 2, pltpu.roll(x, PAIR - HEAD_DIM // 2, 1),
                     pltpu.roll(x, HEAD_DIM // 2, 1))


@jax.custom_vjp
def _rope(x, cosf, sinf):
    return x * cosf + _swap_halves(x) * sinf


def _rope_fwd(x, cosf, sinf):
    return _rope(x, cosf, sinf), (cosf, sinf)


def _rope_bwd(res, d):
    cosf, sinf = res
    return d * cosf + _swap_halves(d * sinf), jnp.zeros_like(cosf), jnp.zeros_like(sinf)


_rope.defvjp(_rope_fwd, _rope_bwd)


def _head_sums(x):
    lane = lax.broadcasted_iota(jnp.int32, x.shape, 1)
    lo = jnp.where(lane < HEAD_DIM, 1.0, 0.0)
    hi = 1.0 - lo
    return lo * jnp.sum(x * lo, axis=1, keepdims=True) + hi * jnp.sum(x * hi, axis=1, keepdims=True)


def _f_qk(x, cosf, sinf, gain2):
    xn = x * lax.rsqrt(_head_sums(x * x) * (1.0 / HEAD_DIM) + RMS_EPS) * gain2
    return _rope(xn, cosf, sinf)


def _qk_prep(z, tab, q_gain, k_gain):
    t = z.shape[0]
    tr = 2048

    def body(z_ref, c_ref, s_ref, qg_ref, kg_ref, o_ref):
        g = jnp.where(pl.program_id(0) < N_PAIRS, qg_ref[...], kg_ref[...])
        o_ref[...] = _f_qk(z_ref[...], c_ref[...], s_ref[...], jnp.concatenate([g, g], axis=1))

    gain = pl.BlockSpec((1, HEAD_DIM), lambda c, i: (0, 0))
    return pl.pallas_call(
        body, name="qk_prep", grid=(2 * N_PAIRS, t // tr),
        in_specs=[pl.BlockSpec((tr, PAIR), lambda c, i: (i, Q_COL0 + c)), pl.BlockSpec((tr, PAIR), lambda c, i: (i, 0)),
                  pl.BlockSpec((tr, PAIR), lambda c, i: (i, 1)), gain, gain],
        out_specs=pl.BlockSpec((tr, PAIR), lambda c, i: (i, c)),
        out_shape=jax.ShapeDtypeStruct((t, 2 * N_PAIRS * PAIR), F32),
        compiler_params=_cparams(dimension_semantics=("parallel", "parallel")),
    )(z, tab, tab, q_gain, k_gain)


def _qk_prep_bwd(z, tab, q_gain, k_gain, dq, dk):
    t = z.shape[0]
    tr = 2048

    def body(z_ref, c_ref, s_ref, qg_ref, kg_ref, dq_ref, dk_ref, dz_ref, dqg_ref, dkg_ref):
        c, i = pl.program_id(0), pl.program_id(1)
        is_q = c < N_PAIRS
        g = jnp.where(is_q, qg_ref[...], kg_ref[...])
        d = jnp.where(is_q, dq_ref[...], dk_ref[...])
        _, vjp = jax.vjp(lambda xx, gg: _f_qk(xx, c_ref[...], s_ref[...], gg), z_ref[...],
                         jnp.concatenate([g, g], axis=1))
        dx, dg2 = vjp(d)
        dz_ref[...] = dx.astype(dz_ref.dtype)
        dg = dg2[:, :HEAD_DIM] + dg2[:, HEAD_DIM:]
        first_q = jnp.logical_and(c == 0, i == 0)
        first_k = jnp.logical_and(c == N_PAIRS, i == 0)

        @pl.when(first_q)
        def _():
            dqg_ref[...] = dg

        @pl.when(jnp.logical_and(is_q, jnp.logical_not(first_q)))
        def _():
            dqg_ref[...] += dg

        @pl.when(first_k)
        def _():
            dkg_ref[...] = dg

        @pl.when(jnp.logical_and(jnp.logical_not(is_q), jnp.logical_not(first_k)))
        def _():
            dkg_ref[...] += dg

    gain = pl.BlockSpec((1, HEAD_DIM), lambda c, i: (0, 0))
    return pl.pallas_call(
        body, name="qk_prep_bwd", grid=(2 * N_PAIRS, t // tr),
        in_specs=[pl.BlockSpec((tr, PAIR), lambda c, i: (i, Q_COL0 + c)), pl.BlockSpec((tr, PAIR), lambda c, i: (i, 0)),
                  pl.BlockSpec((tr, PAIR), lambda c, i: (i, 1)), gain, gain,
                  pl.BlockSpec((tr, PAIR), lambda c, i: (i, jnp.minimum(c, N_PAIRS - 1))),
                  pl.BlockSpec((tr, PAIR), lambda c, i: (i, jnp.maximum(c - N_PAIRS, 0)))],
        out_specs=[pl.BlockSpec((tr, PAIR), lambda c, i: (i, c)), gain, gain],
        out_shape=[jax.ShapeDtypeStruct((t, 2 * N_PAIRS * PAIR), BF16), jax.ShapeDtypeStruct((1, HEAD_DIM), F32),
                   jax.ShapeDtypeStruct((1, HEAD_DIM), F32)],
        compiler_params=_cparams(dimension_semantics=("arbitrary", "arbitrary")),
    )(z, tab, tab, q_gain, k_gain, dq, dk)


def _attn_block(q, kp, kc, vp, vc, kmin):
    k2 = jnp.concatenate([kp, kc], axis=0)
    v2 = jnp.concatenate([vp, vc], axis=0)
    s = _bdot_nt(q, k2) * (HEAD_DIM ** -0.5)
    qi = lax.broadcasted_iota(jnp.int32, s.shape, 0)
    kj = lax.broadcasted_iota(jnp.int32, s.shape, 1)
    dist = qi + BAND_BLOCK - kj
    valid = (dist >= 0) & (dist <= BAND_BLOCK) & (kj >= kmin)
    s = jnp.where(valid, s, NEG_INF)
    m = lax.stop_gradient(jnp.max(s, axis=-1, keepdims=True))
    e = jnp.exp(s - m)
    l = jnp.sum(e, axis=-1, keepdims=True)
    o = _bdot(e, v2) / l
    return o, m + jnp.log(l)


def _fold(src_ref, dst_ref, dil):
    t = src_ref.shape[0]
    ln = t // dil
    for j in range(dil):
        dst_ref[j * ln:(j + 1) * ln, :] = src_ref[pl.ds(j, ln, stride=dil), :]


def _unfold(src_ref, dst_ref, dil):
    t = src_ref.shape[0]
    ln = t // dil
    for j in range(dil):
        dst_ref[pl.ds(j, ln, stride=dil), :] = src_ref[j * ln:(j + 1) * ln, :]


def _per_group(fn):
    pair = pl.program_id(0)
    for gi, (_, dil) in enumerate(ATTN_GROUPS):
        @pl.when(jnp.logical_or(pair == 2 * gi, pair == 2 * gi + 1))
        def _(dil=dil):
            fn(dil)


def _block_rows(idx, blocks_per_seq):
    first = (idx & (blocks_per_seq - 1)) == 0
    cur = pl.ds(pl.multiple_of(idx * BAND_BLOCK, BAND_BLOCK), BAND_BLOCK)
    prev = pl.ds(pl.multiple_of(jnp.maximum(idx - 1, 0) * BAND_BLOCK, BAND_BLOCK), BAND_BLOCK)
    return first, cur, prev


def _heads(x):
    return x[:, :HEAD_DIM], x[:, HEAD_DIM:]


def _attn_fwd(qk, z):
    t = z.shape[0]
    n_blocks = t // BAND_BLOCK

    def body(q_ref, k_ref, v_ref, o_ref, lse_ref, qf, kf, vf, of, lf):
        def run(dil):
            _fold(q_ref, qf, dil)
            _fold(k_ref, kf, dil)
            _fold(v_ref, vf, dil)
            blocks_per_seq = n_blocks // dil

            def block(idx, carry):
                first, cur, prev = _block_rows(idx, blocks_per_seq)
                kmin = jnp.where(first, BAND_BLOCK, 0)
                outs, lses = [], []
                for q, kp, kc, vp, vc in zip(_heads(qf[cur, :]), _heads(kf[prev, :]), _heads(kf[cur, :]),
                                             _heads(vf[prev, :]), _heads(vf[cur, :])):
                    o, ls = _attn_block(q, kp, kc, vp, vc, kmin)
                    outs.append(o)
                    lses.append(jnp.broadcast_to(ls, o.shape))
                of[cur, :] = jnp.concatenate(outs, axis=1)
                lf[cur, :] = jnp.concatenate(lses, axis=1)
                return carry

            lax.fori_loop(0, n_blocks, block, 0, unroll=4)
            _unfold(of, o_ref, dil)
            _unfold(lf, lse_ref, dil)

        _per_group(run)

    slab = jax.ShapeDtypeStruct((t, N_PAIRS * PAIR), F32)
    out_spec = pl.BlockSpec((t, PAIR), lambda p: (0, p))
    return pl.pallas_call(
        body, name="attn_fwd", grid=(N_PAIRS,),
        in_specs=[pl.BlockSpec((t, PAIR), lambda p: (0, p)), pl.BlockSpec((t, PAIR), lambda p: (0, N_PAIRS + p)),
                  pl.BlockSpec((t, PAIR), lambda p: (0, V_COL0 + p))],
        out_specs=[out_spec, out_spec], out_shape=[slab, slab],
        scratch_shapes=[pltpu.VMEM((t, PAIR), F32)] * 5,
        compiler_params=_cparams(dimension_semantics=("parallel",)),
    )(qk, qk, z)


def _attn_bwd(qk, z, do, dlse):
    t = z.shape[0]
    n_blocks = t // BAND_BLOCK

    def body(q_ref, k_ref, v_ref, do_ref, dl_ref, dq_ref, dk_ref, dv_ref, qf, kf, vf, dof, dlf, dqf, dkf, dvf,
             dkpf, dvpf):
        def run(dil):
            for src, dst in ((q_ref, qf), (k_ref, kf), (v_ref, vf), (do_ref, dof), (dl_ref, dlf)):
                _fold(src, dst, dil)
            blocks_per_seq = n_blocks // dil

            def block(idx, carry):
                first, cur, prev = _block_rows(idx, blocks_per_seq)
                kmin = jnp.where(first, BAND_BLOCK, 0)
                grads = []
                for q, kp, kc, vp, vc, do_h, dl_h in zip(
                        _heads(qf[cur, :]), _heads(kf[prev, :]), _heads(kf[cur, :]), _heads(vf[prev, :]),
                        _heads(vf[cur, :]), _heads(dof[cur, :]), _heads(dlf[cur, :])):
                    _, vjp = jax.vjp(functools.partial(_attn_block, kmin=kmin), q, kp, kc, vp, vc)
                    grads.append(vjp((do_h, jnp.sum(dl_h, axis=1, keepdims=True))))
                dq, dkp, dkc, dvp, dvc = (jnp.concatenate([a, b], axis=1) for a, b in zip(*grads))
                dqf[cur, :], dkf[cur, :], dvf[cur, :], dkpf[cur, :], dvpf[cur, :] = dq, dkc, dvc, dkp, dvp
                return carry

            lax.fori_loop(0, n_blocks, block, 0, unroll=4)

            def join(idx, carry):
                first, cur, prev = _block_rows(idx, blocks_per_seq)

                @pl.when(jnp.logical_not(first))
                def _():
                    dkf[prev, :] += dkpf[cur, :]
                    dvf[prev, :] += dvpf[cur, :]

                return carry

            lax.fori_loop(0, n_blocks, join, 0)
            _unfold(dqf, dq_ref, dil)
            _unfold(dkf, dk_ref, dil)
            _unfold(dvf, dv_ref, dil)

        _per_group(run)

    slab = jax.ShapeDtypeStruct((t, N_PAIRS * PAIR), F32)
    own = pl.BlockSpec((t, PAIR), lambda p: (0, p))
    return pl.pallas_call(
        body, name="attn_bwd", grid=(N_PAIRS,),
        in_specs=[own, pl.BlockSpec((t, PAIR), lambda p: (0, N_PAIRS + p)),
                  pl.BlockSpec((t, PAIR), lambda p: (0, V_COL0 + p)), own, own],
        out_specs=[own] * 3, out_shape=[slab] * 3,
        scratch_shapes=[pltpu.VMEM((t, PAIR), F32)] * 10,
        compiler_params=_cparams(dimension_semantics=("parallel",)),
    )(qk, qk, z, do, dlse)


def _f_comb(o1, o2, o3, l1, l2, l3):
    m = jnp.maximum(jnp.maximum(l1, l2), l3)
    e1, e2, e3 = jnp.exp(l1 - m), jnp.exp(l2 - m), jnp.exp(l3 - m)
    den = e1 + e2 + e3
    return (e1 / den) * o1 + (e2 / den) * o2 + (e3 / den) * o3


def _all_gather_hbm(name, arrs):
    na = len(arrs)

    def body(*refs):
        x_refs, out_refs = refs[:na], refs[na:2 * na]
        send_sems, recv_sems, local_sems = refs[2 * na:]
        mx, my, mc = lax.axis_index("x"), lax.axis_index("y"), lax.axis_index("c")
        me, sibling = (mx, my, mc), (mx, my, 1 - mc)
        chips = [(1 - mx, my), (mx, 1 - my), (1 - mx, 1 - my)]

        def slot(a, px, py, pc):
            return out_refs[a].at[4 * px + 2 * py + pc]

        def copy(a, k, block, to, src=None):
            return pltpu.make_async_remote_copy(
                src_ref=slot(a, *block) if src is None else src, dst_ref=slot(a, *block),
                send_sem=send_sems.at[a, k], recv_sem=recv_sems.at[a, k], device_id=to, device_id_type=MESH)

        mine = [pltpu.make_async_copy(x_refs[a], slot(a, *me), local_sems.at[a]) for a in range(na)]
        for cp in mine:
            cp.start()
        first = []
        for a in range(na):
            first.append(copy(a, 0, me, sibling, src=x_refs[a]))
            first += [copy(a, 1 + j, me, (*chip, mc), src=x_refs[a]) for j, chip in enumerate(chips)]
        for cp in first:
            cp.start()
        passed = []
        for j, chip in enumerate(chips):
            for a in range(na):
                copy(a, 1 + j, (*chip, mc), me).wait_recv()
                passed.append(copy(a, 4 + j, (*chip, mc), sibling))
                passed[-1].start()
        for a in range(na):
            copy(a, 0, sibling, me).wait_recv()
            for j, chip in enumerate(chips):
                copy(a, 4 + j, (*chip, 1 - mc), me).wait_recv()
        for cp in first + passed:
            cp.wait_send()
        for cp in mine:
            cp.wait()

    hbm = pl.BlockSpec(memory_space=pl.ANY)
    return pl.pallas_call(
        body, name=name,
        out_shape=[jax.ShapeDtypeStruct((N_DEV,) + a.shape, a.dtype) for a in arrs],
        in_specs=[hbm] * na, out_specs=[hbm] * na,
        scratch_shapes=[pltpu.SemaphoreType.DMA((na, 7)), pltpu.SemaphoreType.DMA((na, 7)),
                        pltpu.SemaphoreType.DMA((na,))],
    )(*arrs)


def _all_gather_vmem(x):
    rws, cols = x.shape

    def body(x_ref, out_ref, send_sems, recv_sems):
        mx, my, mc = lax.axis_index("x"), lax.axis_index("y"), lax.axis_index("c")
        me, sibling = (mx, my, mc), (mx, my, 1 - mc)
        chips = [(1 - mx, my), (mx, 1 - my), (1 - mx, 1 - my)]

        def slot(px, py, pc):
            return out_ref.at[4 * px + 2 * py + pc]

        def copy(k, block, to, src=None):
            return pltpu.make_async_remote_copy(
                src_ref=slot(*block) if src is None else src, dst_ref=slot(*block),
                send_sem=send_sems.at[k], recv_sem=recv_sems.at[k], device_id=to, device_id_type=MESH)

        first = [copy(0, me, sibling, src=x_ref)]
        first += [copy(1 + j, me, (*chip, mc), src=x_ref) for j, chip in enumerate(chips)]
        for cp in first:
            cp.start()
        out_ref[4 * mx + 2 * my + mc] = x_ref[...]
        passed = [copy(4 + j, (*chip, mc), sibling) for j, chip in enumerate(chips)]
        for j, chip in enumerate(chips):
            copy(1 + j, (*chip, mc), me).wait_recv()
            passed[j].start()
        copy(0, sibling, me).wait_recv()
        for j, chip in enumerate(chips):
            copy(4 + j, (*chip, 1 - mc), me).wait_recv()
        for cp in first + passed:
            cp.wait_send()

    return pl.pallas_call(
        body, name="all_gather_small",
        out_shape=jax.ShapeDtypeStruct((N_DEV, rws, cols), x.dtype),
        in_specs=[pl.BlockSpec(memory_space=pltpu.VMEM)], out_specs=pl.BlockSpec(memory_space=pltpu.VMEM),
        scratch_shapes=[pltpu.SemaphoreType.DMA((7,)), pltpu.SemaphoreType.DMA((7,))],
    )(x)


def _scatter_copies(g_refs, land_refs, send_sems, recv_sems):
    mx, my, mc = lax.axis_index("x"), lax.axis_index("y"), lax.axis_index("c")
    me = 4 * mx + 2 * my + mc
    copies = []
    for k in range(1, N_DEV):
        px, py, pc = mx ^ (k >> 2), my ^ ((k >> 1) & 1), mc ^ (k & 1)
        peer = 4 * px + 2 * py + pc
        for a, (g_ref, land_ref) in enumerate(zip(g_refs, land_refs)):
            copies.append(pltpu.make_async_remote_copy(
                src_ref=g_ref.at[peer], dst_ref=land_ref.at[me], send_sem=send_sems.at[a * (N_DEV - 1) + k - 1],
                recv_sem=recv_sems.at[a * (N_DEV - 1) + k - 1], device_id=(px, py, pc), device_id_type=MESH))
    return copies


_HBM = pl.BlockSpec(memory_space=pltpu.HBM)
_SEM = pl.BlockSpec(memory_space=pltpu.SEMAPHORE)
_DATAFLOW = pltpu.SideEffectType.DATAFLOW_SIDE_EFFECTING


def _gather_copies(x_refs, land_refs, send_sems, recv_sems):
    mx, my, mc = lax.axis_index("x"), lax.axis_index("y"), lax.axis_index("c")
    me = 4 * mx + 2 * my + mc
    copies = []
    for k in range(1, N_DEV):
        px, py, pc = mx ^ (k >> 2), my ^ ((k >> 1) & 1), mc ^ (k & 1)
        for a, (x_ref, land_ref) in enumerate(zip(x_refs, land_refs)):
            copies.append(pltpu.make_async_remote_copy(
                src_ref=x_ref, dst_ref=land_ref.at[me], send_sem=send_sems.at[a * (N_DEV - 1) + k - 1],
                recv_sem=recv_sems.at[a * (N_DEV - 1) + k - 1], device_id=(px, py, pc), device_id_type=MESH))
    return copies


N_CHIPS = N_DEV // 2


def _gather2_first(x_refs, land_refs, send_sems, recv_sems, ks=(0, 1, 2, 3)):
    mx, my, mc = lax.axis_index("x"), lax.axis_index("y"), lax.axis_index("c")
    me = 4 * mx + 2 * my + mc
    peers = [(mx, my, 1 - mc), (1 - mx, my, mc), (mx, 1 - my, mc), (1 - mx, 1 - my, mc)]
    return [pltpu.make_async_remote_copy(
        src_ref=x_ref, dst_ref=land_ref.at[me], send_sem=send_sems.at[a * (N_DEV - 1) + k],
        recv_sem=recv_sems.at[a * (N_DEV - 1) + k], device_id=peers[k], device_id_type=MESH)
        for a, (x_ref, land_ref) in enumerate(zip(x_refs, land_refs)) for k in ks]


def _gather2_second(x_refs, land_refs, send_sems, recv_sems):
    mx, my, mc = lax.axis_index("x"), lax.axis_index("y"), lax.axis_index("c")
    copies = []
    for a, land_ref in enumerate(land_refs):
        for j, (px, py) in enumerate([(1 - mx, my), (mx, 1 - my), (1 - mx, 1 - my)]):
            slot = 4 * px + 2 * py + mc
            copies.append(pltpu.make_async_remote_copy(
                src_ref=land_ref.at[slot], dst_ref=land_ref.at[slot], send_sem=send_sems.at[a * (N_DEV - 1) + j],
                recv_sem=recv_sems.at[a * (N_DEV - 1) + j], device_id=(mx, my, 1 - mc), device_id_type=MESH))
    return copies


def _pair_swap(name, arrs):
    na = len(arrs)

    def body(*refs):
        g_refs, out_refs = refs[:na], refs[na:2 * na]
        send_sems, recv_sems = refs[2 * na:]
        mx, my, mc = lax.axis_index("x"), lax.axis_index("y"), lax.axis_index("c")
        copies = [pltpu.make_async_remote_copy(
            src_ref=g_refs[a].at[2 * q + 1 - mc], dst_ref=out_refs[a].at[q], send_sem=send_sems.at[a, q],
            recv_sem=recv_sems.at[a, q], device_id=(mx, my, 1 - mc), device_id_type=MESH)
            for a in range(na) for q in range(N_CHIPS)]
        for cp in copies:
            cp.start()
        for cp in copies:
            cp.wait_recv()
        for cp in copies:
            cp.wait_send()

    hbm = pl.BlockSpec(memory_space=pl.ANY)
    return pl.pallas_call(
        body, name=name,
        out_shape=[jax.ShapeDtypeStruct((N_CHIPS,) + a.shape[1:], a.dtype) for a in arrs],
        in_specs=[hbm] * na, out_specs=[hbm] * na,
        scratch_shapes=[pltpu.SemaphoreType.DMA((na, N_CHIPS)), pltpu.SemaphoreType.DMA((na, N_CHIPS))],
    )(*arrs)


def _pair_add(name, g, half, core):
    _, r, c = g.shape
    tr = _tile(r, 512) if r % LANES == 0 else r

    def body(core_ref, g_ref, h_ref, o_ref):
        o_ref[...] = (g_ref[...].astype(F32) + h_ref[...].astype(F32)).astype(o_ref.dtype)

    return pl.pallas_call(
        body, name=name,
        grid_spec=pltpu.PrefetchScalarGridSpec(
            num_scalar_prefetch=1, grid=(N_CHIPS, r // tr),
            in_specs=[pl.BlockSpec((None, tr, c), lambda q, i, core_ref: (2 * q + core_ref[0], i, 0)),
                      pl.BlockSpec((None, tr, c), lambda q, i, core_ref: (q, i, 0))],
            out_specs=pl.BlockSpec((None, tr, c), lambda q, i, core_ref: (q, i, 0))),
        out_shape=jax.ShapeDtypeStruct((N_CHIPS, r, c), g.dtype),
        compiler_params=_cparams(dimension_semantics=("parallel", "parallel")),
    )(core, g, half)


def _chip_copies(h_refs, land_refs, send_sems, recv_sems):
    mx, my, mc = lax.axis_index("x"), lax.axis_index("y"), lax.axis_index("c")
    my_chip = 2 * mx + my
    copies = []
    for k in range(1, N_CHIPS):
        px, py = mx ^ (k >> 1), my ^ (k & 1)
        for a, (h_ref, land_ref) in enumerate(zip(h_refs, land_refs)):
            copies.append(pltpu.make_async_remote_copy(
                src_ref=h_ref.at[2 * px + py], dst_ref=land_ref.at[my_chip], send_sem=send_sems.at[a * (N_DEV - 1) + k - 1],
                recv_sem=recv_sems.at[a * (N_DEV - 1) + k - 1], device_id=(px, py, mc), device_id_type=MESH))
    return copies


def _exchange_start(name, copies, srcs, lands, after):
    na = len(srcs)

    def body(*refs):
        for cp in copies(refs[:na], refs[na:2 * na], refs[2 * na + 1], refs[2 * na + 2]):
            cp.start()
        refs[-1][...] = jnp.zeros_like(refs[-1])

    in_hbm = lambda a: pltpu.with_memory_space_constraint(a, pltpu.HBM)
    outs = pl.pallas_call(
        body, name=name,
        out_shape=(pltpu.SemaphoreType.DMA((na * (N_DEV - 1),)), pltpu.SemaphoreType.DMA((na * (N_DEV - 1),)),
                   *[pltpu.HBM(a.shape, a.dtype) for a in list(srcs) + list(lands)],
                   jax.ShapeDtypeStruct((8, LANES), F32)),
        in_specs=[_HBM] * (2 * na) + [pl.BlockSpec(memory_space=pl.ANY)],
        out_specs=(_SEM, _SEM, *[_HBM] * (2 * na), pl.BlockSpec(memory_space=pltpu.VMEM)),
        input_output_aliases={i: 2 + i for i in range(2 * na)},
        compiler_params=pltpu.CompilerParams(has_side_effects=_DATAFLOW),
    )(*[in_hbm(a) for a in srcs], *[in_hbm(a) for a in lands], after)
    return outs[0], outs[1], outs[2:2 + na], outs[2 + na:2 + 2 * na], outs[-1]


def _exchange_wait(name, copies, send_sems, recv_sems, srcs, lands, after, parts=("send", "recv")):
    na = len(srcs)

    def body(*refs):
        for cp in copies(refs[:na], refs[na:2 * na], refs[2 * na], refs[2 * na + 1]):
            if "send" in parts:
                cp.wait_send()
            if "recv" in parts:
                cp.wait_recv()

    outs = pl.pallas_call(
        body, name=name,
        out_shape=[pltpu.HBM(a.shape, a.dtype) for a in list(srcs) + list(lands)],
        in_specs=[_HBM] * (2 * na) + [_SEM, _SEM, pl.BlockSpec(memory_space=pl.ANY)], out_specs=[_HBM] * (2 * na),
        input_output_aliases={i: i for i in range(2 * na)},
        compiler_params=pltpu.CompilerParams(has_side_effects=_DATAFLOW),
    )(*srcs, *lands, send_sems, recv_sems, after)
    return outs[:na], outs[na:]


def _exchange_wait_all(name, waits, sems, srcs, lands, after):
    na = len(srcs)

    def body(*refs):
        for copies, parts, k, (lo, hi) in waits:
            for cp in copies(refs[lo:hi], refs[na + lo:na + hi], refs[2 * na + 2 * k], refs[2 * na + 2 * k + 1]):
                if "send" in parts:
                    cp.wait_send()
                if "recv" in parts:
                    cp.wait_recv()

    outs = pl.pallas_call(
        body, name=name,
        out_shape=[pltpu.HBM(a.shape, a.dtype) for a in list(srcs) + list(lands)],
        in_specs=[_HBM] * (2 * na) + [_SEM] * (2 * len(sems)) + [pl.BlockSpec(memory_space=pl.ANY)],
        out_specs=[_HBM] * (2 * na),
        input_output_aliases={i: i for i in range(2 * na)},
        compiler_params=pltpu.CompilerParams(has_side_effects=_DATAFLOW),
    )(*srcs, *lands, *[s for pair in sems for s in pair], after)
    return outs[:na], outs[na:]


def _sum_slots(name, g, tr):
    _, rws, cols = g.shape

    def body(g_ref, o_ref):
        acc = g_ref[0].astype(F32)
        for j in range(1, N_DEV):
            acc = acc + g_ref[j].astype(F32)
        o_ref[...] = acc

    return pl.pallas_call(
        body, name=name, grid=(rws // tr,),
        in_specs=[pl.BlockSpec((N_DEV, tr, cols), lambda i: (0, i, 0))],
        out_specs=pl.BlockSpec((tr, cols), lambda i: (i, 0)),
        out_shape=jax.ShapeDtypeStruct((rws, cols), F32),
        compiler_params=_cparams(dimension_semantics=("parallel",)),
    )(g)


def _adam_math(wv, gv, mv, vv):
    mn = ADAM_B1 * mv + (1.0 - ADAM_B1) * gv
    vn = ADAM_B2 * vv + (1.0 - ADAM_B2) * jnp.square(gv)
    m_hat = mn / (1.0 - ADAM_B1 ** ADAM_STEP)
    v_hat = vn / (1.0 - ADAM_B2 ** ADAM_STEP)
    delta = -ADAM_LR * (m_hat / (jnp.sqrt(v_hat) + ADAM_EPS) + ADAM_WD * wv)
    return delta, mn, vn


def _adamw(name, w, g, m, v, tr):
    return _rowwise(name, _adam_math, [w, g, m, v], [], [(LANES, F32)] * 3, tr=tr)


def _adamw_slots(name, recv, own, slot, rb, cb, cw, w, m, v, tr, order):
    nr, nc = w.shape
    n = recv.shape[0]

    def body(slot_ref, g_ref, own_ref, w_ref, m_ref, v_ref, order_ref, go_ref, d_ref, mo_ref, vo_ref):
        acc = None
        for s in range(n):
            part = jnp.where(slot_ref[0] == s, own_ref[...], g_ref[s]).astype(F32)
            acc = part if acc is None else acc + part
        g = acc[:, :nc]
        go_ref[...] = g
        d_ref[...], mo_ref[...], vo_ref[...] = _adam_math(w_ref[...], g, m_ref[...], v_ref[...])

    nat = pl.BlockSpec((tr, nc), lambda i, slot_ref: (i, 0))
    return pl.pallas_call(
        body, name=name,
        grid_spec=pltpu.PrefetchScalarGridSpec(
            num_scalar_prefetch=1, grid=(nr // tr,),
            in_specs=[pl.BlockSpec((n, tr, cw), lambda i, slot_ref: (0, rb + i, cb)),
                      pl.BlockSpec((None, tr, cw), lambda i, slot_ref: (slot_ref[0], rb + i, cb)),
                      nat, nat, nat, pl.BlockSpec(memory_space=pl.ANY)],
            out_specs=[nat] * 4),
        out_shape=[jax.ShapeDtypeStruct((nr, nc), F32)] * 4,
        compiler_params=_cparams(dimension_semantics=("parallel",)),
    )(slot, recv, own, w, m, v, order)


def _local_blocks(w):
    pad_cols = lambda a: jnp.pad(a, ((0, 0), (0, FF_PAD - FF_SHARD)))
    pad_rows = lambda a: jnp.pad(a, ((0, FF_PAD - FF_SHARD), (0, 0)))
    gate_up = lambda tag: jnp.concatenate([pad_cols(w[tag + "_w_gate"]), pad_cols(w[tag + "_w_up"])], axis=1)
    blocks = {
        "ffn1_gu": gate_up("ffn1"), "ffn1_d": pad_rows(w["ffn1_w_down"]), "w_in": w["w_in"],
        "lora": jnp.concatenate([w["rwkv_w2"], w["rwkv_a2"], w["rwkv_g2"]], axis=0),
        "br": jnp.concatenate([w["w_br_rwkv"], w["w_br_attn"], w["ple_w_proj"]], axis=0),
        "w_out": w["w_out"], "ffn2_gu": gate_up("ffn2"), "ffn2_d": pad_rows(w["ffn2_w_down"]),
        "ple_gate": w["ple_w_gate"],
    }
    return {n: a.astype(BF16) for n, a in blocks.items()}


GATHER_GROUPS = {"head": ("ffn1_gu", "ffn1_d"), "mid": ("w_in", "lora"),
                 "rest": ("br", "w_out", "ffn2_gu", "ffn2_d", "ple_gate")}

SCATTER_GROUPS = {"tail": ("ple_gate", "ple_proj", "ffn2_gu", "ffn2_d"), "branch": ("w_out", "br"),
                  "mixer": ("lora", "w_in"),
                  "ffn1_down": ("ffn1_d",), "head": ("ffn1_gu",)}

TWO_LEVEL = ("mixer", "head")

ADAM_PLAN = (
    ("ffn1_w_gate", "ffn1_gu", 0, 0, FF_PAD, 256), ("ffn1_w_up", "ffn1_gu", 0, 1, FF_PAD, 256),
    ("ffn1_w_down", "ffn1_d", 0, 0, D_MODEL, FF_SHARD // 2), ("w_in", "w_in", 0, 0, IN_SHARD, 256),
    ("rwkv_w2", "lora", 0, 0, HEAD_DIM, 64), ("rwkv_a2", "lora", 1, 0, HEAD_DIM, 64),
    ("rwkv_g2", "lora", 2, 0, HEAD_DIM, 64),
    ("w_br_rwkv", "br", 0, 0, OUT_SHARD, 256), ("w_br_attn", "br", 2, 0, OUT_SHARD, 256),
    ("ple_w_proj", "ple_proj", 0, 0, OUT_SHARD, 256), ("w_out", "w_out", 0, 0, D_MODEL, OUT_SHARD),
    ("ffn2_w_gate", "ffn2_gu", 0, 0, FF_PAD, 256), ("ffn2_w_up", "ffn2_gu", 0, 1, FF_PAD, 256),
    ("ffn2_w_down", "ffn2_d", 0, 0, D_MODEL, FF_SHARD // 2), ("ple_w_gate", "ple_gate", 0, 0, D_MODEL, OUT_SHARD),
)


def _pack_small(arrs, rows):
    flat = jnp.concatenate([a.reshape(-1) for a in arrs])
    return jnp.pad(flat, (0, rows * LANES - flat.shape[0])).reshape(rows, LANES)


def _unpack_small(flat, like):
    flat = flat.reshape(-1)
    out, off = [], 0
    for a in like:
        out.append(flat[off:off + a.size].reshape(a.shape))
        off += a.size
    return out


def _local_step(x, p, pos, target, sm, wg, fetch, on_grads):
    t = x.shape[0]

    w_d1 = wg["ffn1_d"].reshape(FF_HID, D_MODEL)
    x1, ffn1_saved, _ = _ffn_fwd("ffn1", x, sm["ffn1_norm"], wg["ffn1_gu"], lambda after: w_d1)
    fetch("forward", x1)
    h2 = _norm_fwd("mix_norm", x1, sm["mix_norm"])
    wg = {**wg, **fetch("mid", h2)}
    full_cols = lambda blk: blk.transpose(1, 0, 2).reshape(blk.shape[1], N_DEV * blk.shape[2])
    lora_w2 = full_cols(wg["lora"][:, :DECAY_LORA])
    lora_a2 = full_cols(wg["lora"][:, DECAY_LORA:DECAY_LORA + ICLR_LORA])
    lora_g2 = full_cols(wg["lora"][:, DECAY_LORA + ICLR_LORA:])
    z =_mmc_nn("w_in", h2, wg["w_in"], 0, 0, IN_SHARD)
    z_g = (z, 2 * D_MODEL, (RWKV_COLS + 3 * ATTN_DIM) // (2 * D_MODEL))

    r, k, v, lo, gd = _shift_fwd(z, sm["rwkv_mu"])
    zero_lo = jnp.zeros((DECAY_LORA, RWKV_DIM), BF16)
    w2p = jnp.concatenate([lora_w2, zero_lo], axis=0).astype(F32)
    a2p = jnp.concatenate([zero_lo, lora_a2], axis=0).astype(F32)
    pre_params = [sm["rwkv_w0"], w2p, sm["rwkv_a0"], a2p, lora_g2.astype(F32), sm["rwkv_k_k"], sm["rwkv_k_a"]]
    wide = [(RWKV_DIM, F32)]
    k2, kk, a, decay, g = _rowwise("rwkv_pre", _f_pre, [k, lo, gd], pre_params, wide * 5, tr=BIG_ROWS)
    kkn, b = _pairwise("rwkv_kk", _f_kk, [kk, a], [], 2)
    scan_in = [u.reshape(t, RWKV_HEADS, HEAD_DIM) for u in (r, decay, k2, kkn, b)]
    v_rows = _to_v_rows(v)
    y_rows, states = _wkv_fwd(*scan_in, v_rows)
    y = _from_v_rows(y_rows)
    post_params = [sm["rwkv_gn_w"], sm["rwkv_gn_b"], sm["rwkv_r_k"]]
    post_rows = [y, r, k2, v, g]
    y_rwkv = _pairwise("rwkv_post", lambda *av: (_f_post(*av),), post_rows, post_params, 1)[0]

    inv_freq = 1.0 / (ROPE_THETA ** (jnp.arange(0, HEAD_DIM, 2, dtype=F32) / HEAD_DIM))
    freq2 = jnp.tile(inv_freq, 2 * PAIR // HEAD_DIM).reshape(1, PAIR)
    half = jnp.ones((HEAD_DIM // 2,), F32)
    sign2 = jnp.tile(jnp.concatenate([-half, half]), PAIR // HEAD_DIM).reshape(1, PAIR)

    def rope_table(posv, fr, sg):
        ang = posv * fr
        return (jnp.concatenate([jnp.cos(ang), jnp.sin(ang) * sg], axis=1),)
    tab = _rowwise("rope_table", rope_table, [pos.astype(F32).reshape(t, 1)], [freq2, sign2], [(2 * PAIR, F32)])[0]
    qk = _qk_prep(z, tab, sm["q_norm"], sm["k_norm"])
    o_all, lse_all = _attn_fwd(qk, z)
    gw = HEADS_PER_GROUP * HEAD_DIM

    def by_group(ov, lv):
        return [ov[:, i * gw:(i + 1) * gw] for i in range(3)] + [lv[:, i * gw:(i + 1) * gw] for i in range(3)]
    y_attn = _rowwise("attn_comb", lambda ov, lv: (_f_comb(*by_group(ov, lv)),), [o_all, lse_all], [], [(gw, F32)],
                      tr=BIG_ROWS)[0]

    wg = {**wg, **fetch("rest", y_rwkv)}
    w_d2 = wg["ffn2_d"].reshape(FF_HID, D_MODEL)
    w_out = wg["w_out"].reshape(D_MODEL, D_MODEL)
    w_pg = wg["ple_gate"].reshape(D_MODEL, D_MODEL)
    w_brr = full_cols(wg["br"][:, :RWKV_DIM])
    w_bra = full_cols(wg["br"][:, RWKV_DIM:RWKV_DIM + gw])
    w_pp = full_cols(wg["br"][:, RWKV_DIM + gw:])
    u_r = _mm("br_rwkv", y_rwkv, w_brr, "nn")
    u_a = _mm("br_attn", y_attn, w_bra, "nn")

    def f_merge(zgr, zga, ur, ua):
        return _sigmoid(zgr) * ur + _sigmoid(zga) * ua
    merged = _rowwise("merge", lambda zg, ur, ua: (f_merge(zg[:, :D_MODEL], zg[:, D_MODEL:], ur, ua),),
                      [z_g, u_r, u_a], [], [(D_MODEL, BF16)], tr=BIG_ROWS)[0]
    x2 = _mm("w_out", merged, w_out, "nn", res=x1)
    x3, ffn2_saved, _ = _ffn_fwd("ffn2", x2, sm["ffn2_norm"], wg["ffn2_gu"], lambda after: w_d2)

    hn = _norm_fwd("ple_norm", x3, sm["ple_norm"])
    gz = _mm("ple_gate", hn, w_pg, "nn")
    pp = _mm("ple_proj", p, w_pp, "nn")

    def f_head(x3v, gzv, ppv, tg):
        sg = _sigmoid(gzv)
        err = x3v + sg * ppv - tg
        part = 0.5 * jnp.sum(jnp.mean(err * err, axis=-1, keepdims=True))
        dx4 = err * (1.0 / D_MODEL)
        return dx4, dx4 * ppv * sg * (1.0 - sg), dx4 * sg, jnp.full((1, LANES), part, F32)
    dx4, dgz, dpp, loss_row = _rowwise("ple_loss", f_head, [x3, gz, pp, target], [],
                                       [(D_MODEL, F32), (D_MODEL, BF16), (D_MODEL, BF16)], [(1, LANES)])
    loss = loss_row[0, 0]

    gs, gm = {}, {}
    row_blocks = lambda g: g.reshape(N_DEV, g.shape[0] // N_DEV, g.shape[1])
    dhn = _mm("ple_dhn", dgz, w_pg, "nt")
    gm["ple_gate"] = row_blocks(_mm("ple_dwgate", hn, dgz, "tn", out_dtype=BF16))
    col_blocks = lambda g: g.reshape(g.shape[0], N_DEV, g.shape[1] // N_DEV).transpose(1, 0, 2)
    gm["ple_proj"] = col_blocks(_mm("ple_dwproj", p, dpp, "tn", out_dtype=BF16))
    dx3, gs["ple_norm"] = _norm_bwd("ple_dnorm", x3, sm["ple_norm"], dhn, dx4)

    dx2, gs["ffn2_norm"], gm["ffn2_gu"], gm["ffn2_d"] = _ffn_bwd(
        "ffn2", x2, sm["ffn2_norm"], wg["ffn2_gu"], w_d2, ffn2_saved, dx3)
    tail_token = on_grads("tail", {n: gm.pop(n) for n in SCATTER_GROUPS["tail"]})

    dmerged = _mm("w_out_dmerged", dx2, w_out, "nt")
    gm["w_out"] = row_blocks(_mm("w_out_dw", merged, dx2, "tn", out_dtype=BF16))

    def merge_bwd(zg, ur, ua, dm):
        _, vjp = jax.vjp(f_merge, zg[:, :D_MODEL], zg[:, D_MODEL:], ur, ua)
        dzr, dza, dur, dua = vjp(dm)
        return jnp.concatenate([dzr, dza], axis=1), dur, dua
    dz_g, du_r, du_a = _rowwise("merge_bwd", merge_bwd, [z_g, u_r, u_a, dmerged], [],
                                [(2 * D_MODEL, BF16), (D_MODEL, BF16), (D_MODEL, BF16)])
    dy_rwkv = _mm("br_rwkv_dy", du_r, w_brr, "nt")
    dy_attn = _mm("br_attn_dy", du_a, w_bra, "nt")
    gm["br"] = jnp.concatenate([col_blocks(_mm("br_rwkv_dw", y_rwkv, du_r, "tn", out_dtype=BF16)),
                                col_blocks(_mm("br_attn_dw", y_attn, du_a, "tn", out_dtype=BF16))], axis=1)
    branch_token = on_grads("branch", {n: gm.pop(n) for n in SCATTER_GROUPS["branch"]})

    def comb_bwd(ov, lv, dyv):
        _, vjp = jax.vjp(_f_comb, *by_group(ov, lv))
        d = vjp(dyv)
        return jnp.concatenate(d[:3], axis=1), jnp.concatenate(d[3:], axis=1)
    do_all, dl_all = _rowwise("attn_comb_bwd", comb_bwd, [o_all, lse_all, dy_attn], [],
                              [(ATTN_DIM, F32), (ATTN_DIM, F32)], tr=BIG_ROWS)
    dq_all, dk_all, dv_all = _attn_bwd(qk, z, do_all, dl_all)
    dqk_raw, gs["q_norm"], gs["k_norm"] = _qk_prep_bwd(z, tab, sm["q_norm"], sm["k_norm"], dq_all, dk_all)

    def post_bwd(yv, rv, k2v, vv, gv, dv_, gnw, gnb, rk):
        _, vjp = jax.vjp(_f_post, yv, rv, k2v, vv, gv, gnw, gnb, rk)
        return vjp(dv_)
    dy, dr1, dk2a, dv1, dg, d_gnw, d_gnb, d_rk = _pairwise(
        "rwkv_post_bwd", post_bwd, post_rows + [dy_rwkv], [post_params[0] + tail_token[0, 0] + branch_token[0, 0]] + post_params[1:], 5, 3)
    gs["rwkv_gn_w"], gs["rwkv_gn_b"], gs["rwkv_r_k"] = d_gnw, d_gnb, d_rk
    dr2, ddecay, dk2b, dkkn, db, dv_rows = _wkv_bwd(*scan_in, v_rows, states, _to_v_rows(dy))
    dr2, ddecay, dk2b, dkkn, db = [u.reshape(t, RWKV_DIM) for u in (dr2, ddecay, dk2b, dkkn, db)]
    dv2 = _from_v_rows(dv_rows)

    def kk_bwd(kkv, av, dkknv, dbv, dra, drb, dva, dvb):
        _, vjp = jax.vjp(_f_kk, kkv, av)
        return (*vjp((dkknv, dbv)), dra + drb, dva + dvb)
    dkk, da, dr, dv = _pairwise("rwkv_kk_bwd", kk_bwd, [kk, a, dkkn, db, dr1, dr2, dv1, dv2], [], 4)

    def pre_bwd(kv, lov, gdv, dk2x, dk2y, dkkv, dav, ddec, dgv, w0, w2p_, a0, a2p_, g2, k_k, k_a):
        _, vjp = jax.vjp(_f_pre, kv, lov, gdv, w0, w2p_, a0, a2p_, g2, k_k, k_a)
        return vjp((dk2x + dk2y, dkkv, dav, ddec, dgv))
    lora_acc = (DECAY_LORA + ICLR_LORA, RWKV_DIM)
    dk, dlo, dgd, d_w0, d_w2p, d_a0, d_a2p, d_g2, d_kk, d_ka = _rowwise(
        "rwkv_pre_bwd", pre_bwd,
        [k, lo, gd, dk2a, dk2b, dkk, da, ddecay, dg],
        pre_params, [(RWKV_DIM, F32), (LANES, F32), (LANES, F32)],
        [(1, RWKV_DIM), lora_acc, (1, RWKV_DIM), lora_acc, (GATE_LORA, RWKV_DIM), (1, RWKV_DIM), (1, RWKV_DIM)])
    gs["rwkv_w0"], gs["rwkv_a0"], gs["rwkv_k_k"], gs["rwkv_k_a"] = d_w0, d_a0, d_kk, d_ka
    gm["lora"] = jnp.concatenate([col_blocks(d_w2p[:DECAY_LORA]), col_blocks(d_a2p[DECAY_LORA:]), col_blocks(d_g2)],
                                 axis=1).astype(BF16)
    dz_r, gs["rwkv_mu"] = _shift_bwd(z, sm["rwkv_mu"], dr, dk, dv, dlo, dgd)

    dz = jnp.concatenate([dz_r, dqk_raw, dv_all.astype(BF16), dz_g], axis=1)
    dh2 = _mmc_nt("w_in_dh", dz, wg["w_in"], 0, 0, IN_SHARD, D_MODEL)
    gm["w_in"] = _mmc_tn("w_in_dw", h2, dz, IN_SHARD)
    mixer_token = on_grads("mixer", {n: gm.pop(n) for n in SCATTER_GROUPS["mixer"]})
    dx1, gs["mix_norm"] = _norm_bwd("mix_dnorm", x1, sm["mix_norm"] + mixer_token[0, 0], dh2, dx2)

    dx0, gs["ffn1_norm"], gm["ffn1_gu"], _ = _ffn_bwd(
        "ffn1", x, sm["ffn1_norm"], wg["ffn1_gu"], w_d1, ffn1_saved, dx1,
        on_down=lambda blocks: on_grads("ffn1_down", {"ffn1_d": blocks}))
    return loss, dx0, gm, gs


def kernel(x, p, positions, ffn1_norm, ffn1_w_gate, ffn1_w_up, ffn1_w_down, mix_norm, w_in, rwkv_mu, rwkv_w0, rwkv_w2, rwkv_a0, rwkv_a2, rwkv_g2, rwkv_k_k, rwkv_k_a, rwkv_r_k, rwkv_gn_w, rwkv_gn_b, q_norm, k_norm, w_br_rwkv, w_br_attn, w_out, ffn2_norm, ffn2_w_gate, ffn2_w_up, ffn2_w_down, ple_norm, ple_w_gate, ple_w_proj, loss_target, m_ffn1_norm, m_ffn1_w_gate, m_ffn1_w_up, m_ffn1_w_down, m_mix_norm, m_w_in, m_rwkv_mu, m_rwkv_w0, m_rwkv_w2, m_rwkv_a0, m_rwkv_a2, m_rwkv_g2, m_rwkv_k_k, m_rwkv_k_a, m_rwkv_r_k, m_rwkv_gn_w, m_rwkv_gn_b, m_q_norm, m_k_norm, m_w_br_rwkv, m_w_br_attn, m_w_out, m_ffn2_norm, m_ffn2_w_gate, m_ffn2_w_up, m_ffn2_w_down, m_ple_norm, m_ple_w_gate, m_ple_w_proj, v_ffn1_norm, v_ffn1_w_gate, v_ffn1_w_up, v_ffn1_w_down, v_mix_norm, v_w_in, v_rwkv_mu, v_rwkv_w0, v_rwkv_w2, v_rwkv_a0, v_rwkv_a2, v_rwkv_g2, v_rwkv_k_k, v_rwkv_k_a, v_rwkv_r_k, v_rwkv_gn_w, v_rwkv_gn_b, v_q_norm, v_k_norm, v_w_br_rwkv, v_w_br_attn, v_w_out, v_ffn2_norm, v_ffn2_w_gate, v_ffn2_w_up, v_ffn2_w_down, v_ple_norm, v_ple_w_gate, v_ple_w_proj):
    args = locals()
    w = {n: args[n][0] for n in WEIGHTS}
    m = {n: args["m_" + n][0] for n in WEIGHTS}
    v = {n: args["v_" + n][0] for n in WEIGHTS}

    w_loc = _local_blocks(w)
    head = GATHER_GROUPS["head"]
    wg = dict(zip(head, _all_gather_hbm("gather_head", [w_loc[n] for n in head])))
    me = 4 * lax.axis_index("x") + 2 * lax.axis_index("y") + lax.axis_index("c")
    gathering, order_after = {}, wg[head[0]]
    for group in ("mid", "rest"):
        shards = [w_loc[n] for n in GATHER_GROUPS[group]]
        zones = [lax.dynamic_update_slice(lax.empty((N_DEV,) + a.shape, a.dtype), a[None], (me, 0, 0)) for a in shards]
        copies = _gather2_first if group == "mid" else _gather_copies
        *gathering[group], order_after = _exchange_start("gather_start_" + group, copies, shards, zones, order_after)

    def fetch(group, after):
        if group == "forward":
            s1, r1, srcs, lands = gathering["mid"]
            ici = functools.partial(_gather2_first, ks=(1, 2, 3))
            srcs, lands = _exchange_wait("gather_arrived_mid", ici, s1, r1, srcs, lands, after, parts=("recv",))
            s2, r2, srcs, lands, _ = _exchange_start("gather_forward_mid", _gather2_second, srcs, lands, after)
            gathering["mid"] = (s1, r1, s2, r2, srcs, lands)
            return {}
        if group == "mid":
            s1, r1, s2, r2, srcs, lands = gathering["mid"]
            sib = functools.partial(_gather2_first, ks=(0,))
            every = (0, len(srcs))
            waits = [(_gather2_second, ("send", "recv"), 0, every), (_gather2_first, ("send",), 1, every),
                     (sib, ("recv",), 1, every)]
            _, got = _exchange_wait_all("gather_wait_mid", waits, [(s2, r2), (s1, r1)], srcs, lands, after)
        else:
            _, got = _exchange_wait("gather_wait_" + group, _gather_copies, *gathering[group], after)
        return dict(zip(GATHER_GROUPS[group], got))

    sm = {n: w[n].reshape(1, -1) for n in SMALL}
    sm["ffn1_norm"] = sm["ffn1_norm"] + order_after[0, 0]
    in_flight = {}

    core = lax.axis_index("c").astype(jnp.int32).reshape(1)
    chip_slot = (2 * lax.axis_index("x") + lax.axis_index("y")).astype(jnp.int32).reshape(1)
    device_slot = 2 * chip_slot + core

    def scatter_early(group, arrays):
        arrs = [arrays[n] for n in SCATTER_GROUPS[group]]
        copies, after = _scatter_copies, arrs[0]
        if group in TWO_LEVEL:
            halves = _pair_swap("scatter_pair_" + group, arrs)
            arrs = [_pair_add("scatter_add_%s_%s" % (group, n), a, hf, core)
                    for n, a, hf in zip(SCATTER_GROUPS[group], arrs, halves)]
            copies, after = _chip_copies, halves[0]
        *in_flight[group], token = _exchange_start("scatter_start_" + group, copies, arrs,
                                                   [lax.empty(a.shape, a.dtype) for a in arrs], after)
        return token
    loss_part, dx, gm, gs = _local_step(x[0], p[0, 0], positions[0], loss_target[0], sm, wg, fetch, scatter_early)
    head_token = scatter_early("head", gm)
    recv, own = {}, {}

    def arrived(group, after):
        copies = _chip_copies if group in TWO_LEVEL else _scatter_copies
        sent, lands = _exchange_wait("scatter_wait_" + group, copies, *in_flight[group], after)
        own.update(zip(SCATTER_GROUPS[group], sent))
        recv.update(zip(SCATTER_GROUPS[group], lands))
    early_groups = ("tail", "branch", "mixer", "ffn1_down")
    early_srcs, early_lands, early_sems, early_waits = [], [], [], []
    for k, group in enumerate(early_groups):
        s, r, srcs, lands = in_flight[group]
        first = len(early_srcs)
        early_srcs += list(srcs)
        early_lands += list(lands)
        early_sems.append((s, r))
        early_waits.append((_chip_copies if group in TWO_LEVEL else _scatter_copies, ("send", "recv"), k,
                            (first, len(early_srcs))))
    sent, landed = _exchange_wait_all("scatter_wait_early", early_waits, early_sems, early_srcs, early_lands,
                                      head_token)
    early_names = [n for group in early_groups for n in SCATTER_GROUPS[group]]
    own.update(zip(early_names, sent))
    recv.update(zip(early_names, landed))
    small_like = [w[n] for n in SMALL]
    small_rows = 80
    gs_all = _all_gather_vmem(_pack_small([gs[n] for n in SMALL] + [loss_part], small_rows))
    gs_sum = _sum_slots("sum_small_grads", gs_all, small_rows)
    loss = gs_sum.reshape(-1)[sum(a.size for a in small_like)]

    res = {}
    early = [e for group in ("tail", "branch", "mixer", "ffn1_down") for e in ADAM_PLAN
             if e[1] in SCATTER_GROUPS[group]]
    late = [e for e in ADAM_PLAN if e[1] in SCATTER_GROUPS["head"]]
    order = head_token
    for n, src, rb, cb, cw, tr in early + late:
        if (n, src, rb, cb, cw, tr) == late[0]:
            arrived("head", order)
        slot = chip_slot if recv[src].shape[0] == N_CHIPS else device_slot
        outs4 = _adamw_slots("adamw_" + n, recv[src], own[src], slot, rb, cb, cw, w[n], m[n], v[n], tr, order)
        order = outs4[1]
        for tag, a in zip(("grad", "delta", "new_m", "new_v"), outs4):
            res[tag, n] = a[None]
    d_s, m_s, v_s = _adamw("adamw_small", _pack_small(small_like, small_rows), gs_sum,
                           _pack_small([m[n] for n in SMALL], small_rows),
                           _pack_small([v[n] for n in SMALL], small_rows), small_rows)
    for tag, small in (("grad", gs_sum), ("delta", d_s), ("new_m", m_s), ("new_v", v_s)):
        for n, a in zip(SMALL, _unpack_small(small, small_like)):
            res[tag, n] = a[None]
    outs = [loss, dx[None]]
    for tag in ("grad", "delta", "new_m", "new_v"):
        outs += [res[tag, n] for n in WEIGHTS]
    return tuple(outs)
```
